```python
import math
import jax, jax.numpy as jnp
from jax import lax
import numpy as np

D_MODEL = 1024
BATCH = 8
SEQ = 2048
DEPTH = 1

EPS = 1e-6
D_FF = 2816
FFN_RES_WEIGHT = 0.5
N_MOD = 9

SWA_HEADS = 8
SWA_KV_HEADS = 2
SWA_HEAD_DIM = 64
SWA_GROUP = SWA_HEADS // SWA_KV_HEADS
WINDOW = 128

MLA_HEADS = 4
MLA_Q_RANK = 256
MLA_KV_RANK = 128
MLA_NOPE = 128
MLA_ROPE = 64
MLA_V = 128
ROPE_THETA = 10000.0
Q_BLOCK = 128

NUM_BUCKETS = 32
MAX_DISTANCE = 128

IN_SPLITS = (SWA_HEADS * SWA_HEAD_DIM, SWA_KV_HEADS * SWA_HEAD_DIM, SWA_KV_HEADS * SWA_HEAD_DIM,
             MLA_Q_RANK, MLA_KV_RANK, MLA_ROPE)
D_IN = sum(IN_SPLITS)
MIX_OUT = SWA_HEADS * SWA_HEAD_DIM + MLA_HEADS * MLA_V

kernel_name = "hybrid_swa_sink_mla_macaron_adaln"


def rms_norm(x, g):
    xf = x.astype(jnp.float32)
    y = xf * lax.rsqrt(jnp.mean(xf * xf, axis=-1, keepdims=True) + EPS)
    return (y * g.astype(jnp.float32)).astype(x.dtype)


def modulate(h, shift, scale):
    return h * (1 + scale[:, None, :]) + shift[:, None, :]


def swiglu(h, w_gate, w_up, w_down):
    return (jax.nn.silu(h @ w_gate) * (h @ w_up)) @ w_down


def t5_bucket(dist):
    max_exact = NUM_BUCKETS // 2
    n = jnp.maximum(dist, 0)
    nf = jnp.maximum(n, 1).astype(jnp.float32)
    large = max_exact + (jnp.log(nf / max_exact) / math.log(MAX_DISTANCE / max_exact)
                         * (NUM_BUCKETS - max_exact)).astype(jnp.int32)
    large = jnp.minimum(large, NUM_BUCKETS - 1)
    return jnp.where(n < max_exact, n, large)


def rope_tables(seq, dim):
    inv = ROPE_THETA ** (-jnp.arange(0, dim, 2, dtype=jnp.float32) / dim)
    ang = jnp.arange(seq, dtype=jnp.float32)[:, None] * inv[None, :]
    return jnp.cos(ang), jnp.sin(ang)


def apply_rope(x, cos, sin):
    c = cos[None, :, None, :].astype(x.dtype)
    s = sin[None, :, None, :].astype(x.dtype)
    x1, x2 = jnp.split(x, 2, axis=-1)
    return jnp.concatenate([x1 * c - x2 * s, x2 * c + x1 * s], axis=-1)


def sliding_window_gqa(q, k, v, sinks, rel_bias):
    B, S = q.shape[0], q.shape[1]
    nb = S // WINDOW
    qb = q.reshape(B, nb, WINDOW, SWA_KV_HEADS, SWA_GROUP, SWA_HEAD_DIM)
    kb = k.reshape(B, nb, WINDOW, SWA_KV_HEADS, SWA_HEAD_DIM)
    vb = v.reshape(B, nb, WINDOW, SWA_KV_HEADS, SWA_HEAD_DIM)
    pad_k = jnp.zeros_like(kb[:, :1])
    kk = jnp.concatenate([jnp.concatenate([pad_k, kb[:, :-1]], axis=1), kb], axis=2)
    vv = jnp.concatenate([jnp.concatenate([jnp.zeros_like(vb[:, :1]), vb[:, :-1]], axis=1), vb], axis=2)

    s = jnp.einsum('bnqhgd,bnkhd->bnhgqk', qb, kk).astype(jnp.float32) * (SWA_HEAD_DIM ** -0.5)

    qi = jnp.arange(WINDOW)[:, None]
    kj = jnp.arange(2 * WINDOW)[None, :]
    dist = qi + WINDOW - kj
    band = (dist >= 0) & (dist < WINDOW)
    blk = jnp.arange(nb)[:, None]
    key_ok = (blk > 0) | (jnp.arange(2 * WINDOW)[None, :] >= WINDOW)
    valid = band[None, :, :] & key_ok[:, None, :]

    bias = rel_bias.astype(jnp.float32)[t5_bucket(dist)]
    bias = bias.transpose(2, 0, 1).reshape(SWA_KV_HEADS, SWA_GROUP, WINDOW, 2 * WINDOW)
    s = jnp.where(valid[None, :, None, None], s + bias[None, None], -jnp.inf)

    sink = jnp.broadcast_to(sinks.astype(jnp.float32).reshape(1, 1, SWA_KV_HEADS, SWA_GROUP, 1, 1),
                            s.shape[:-1] + (1,))
    p = jax.nn.softmax(jnp.concatenate([s, sink], axis=-1), axis=-1)[..., :-1]
    o = jnp.einsum('bnhgqk,bnkhd->bnqhgd', p.astype(vv.dtype), vv)
    return o.reshape(B, S, SWA_HEADS * SWA_HEAD_DIM)


def mla_attention(q_lat, kv_lat, k_rope, q_norm, kv_norm, w_uq, w_ukv):
    B, S = q_lat.shape[0], q_lat.shape[1]
    cos, sin = rope_tables(S, MLA_ROPE)
    q = (rms_norm(q_lat, q_norm) @ w_uq).reshape(B, S, MLA_HEADS, MLA_NOPE + MLA_ROPE)
    q_nope, q_rope = q[..., :MLA_NOPE], apply_rope(q[..., MLA_NOPE:], cos, sin)
    kv = (rms_norm(kv_lat, kv_norm) @ w_ukv).reshape(B, S, MLA_HEADS, MLA_NOPE + MLA_V)
    k_nope, v = kv[..., :MLA_NOPE], kv[..., MLA_NOPE:]
    k_r = apply_rope(k_rope[:, :, None, :], cos, sin)[:, :, 0, :]
    scale = (MLA_NOPE + MLA_ROPE) ** -0.5

    nb = S // Q_BLOCK
    qn = q_nope.reshape(B, nb, Q_BLOCK, MLA_HEADS, MLA_NOPE).transpose(1, 0, 2, 3, 4)
    qr = q_rope.reshape(B, nb, Q_BLOCK, MLA_HEADS, MLA_ROPE).transpose(1, 0, 2, 3, 4)
    kpos = jnp.arange(S)

    def one_block(args):
        qn_b, qr_b, i = args
        s = (jnp.einsum('bqhd,bkhd->bhqk', qn_b, k_nope)
             + jnp.einsum('bqhd,bkd->bhqk', qr_b, k_r)).astype(jnp.float32) * scale
        qpos = i * Q_BLOCK + jnp.arange(Q_BLOCK)
        s = jnp.where(kpos[None, :] <= qpos[:, None], s, -jnp.inf)
        p = jax.nn.softmax(s, axis=-1).astype(v.dtype)
        return jnp.einsum('bhqk,bkhd->bqhd', p, v)

    o = lax.map(one_block, (qn, qr, jnp.arange(nb)))
    return o.transpose(1, 0, 2, 3, 4).reshape(B, S, MLA_HEADS * MLA_V)


def token_mixing(h, w_in, q_norm, kv_norm, w_uq, w_ukv, sinks, w_o, rel_bias):
    B, S = h.shape[0], h.shape[1]
    proj = h @ w_in
    idx = np.cumsum(IN_SPLITS)[:-1].tolist()
    q_a, k_a, v_a, q_lat, kv_lat, k_rope = jnp.split(proj, idx, axis=-1)
    out_a = sliding_window_gqa(q_a.reshape(B, S, SWA_HEADS, SWA_HEAD_DIM),
                               k_a.reshape(B, S, SWA_KV_HEADS, SWA_HEAD_DIM),
                               v_a.reshape(B, S, SWA_KV_HEADS, SWA_HEAD_DIM),
                               sinks, rel_bias)
    out_b = mla_attention(q_lat, kv_lat, k_rope, q_norm, kv_norm, w_uq, w_ukv)
    return jnp.concatenate([out_a, out_b], axis=-1) @ w_o


def _fwd_setup_inputs(seed: int = 0) -> dict:
    key = jax.random.key(seed)
    ks = jax.random.split(key, 24)
    L, D = DEPTH, D_MODEL

    def w(k, shape, fan_in):
        return jax.random.normal(k, shape, jnp.float32) * fan_in ** -0.5

    def gain(k, shape):
        return 1.0 + 0.05 * jax.random.normal(k, shape, jnp.float32)

    return {
        "x": jax.random.normal(ks[0], (BATCH, SEQ, D), jnp.float32),
        "c": jax.random.normal(ks[1], (BATCH, D), jnp.float32),
        "w_mod": w(ks[2], (L, D, N_MOD * D), D) * 0.5,
        "b_mod": 0.02 * jax.random.normal(ks[3], (L, N_MOD * D), jnp.float32),
        "norm_ffn1": gain(ks[4], (L, D)),
        "ffn1_gate": w(ks[5], (L, D, D_FF), D),
        "ffn1_up": w(ks[6], (L, D, D_FF), D),
        "ffn1_down": w(ks[7], (L, D_FF, D), D_FF),
        "norm_mix": gain(ks[8], (L, D)),
        "w_in": w(ks[9], (L, D, D_IN), D),
        "q_norm": gain(ks[10], (L, MLA_Q_RANK)),
        "kv_norm": gain(ks[11], (L, MLA_KV_RANK)),
        "w_uq": w(ks[12], (L, MLA_Q_RANK, MLA_HEADS * (MLA_NOPE + MLA_ROPE)), MLA_Q_RANK),
        "w_ukv": w(ks[13], (L, MLA_KV_RANK, MLA_HEADS * (MLA_NOPE + MLA_V)), MLA_KV_RANK),
        "sinks": 0.5 * jax.random.normal(ks[14], (L, SWA_HEADS), jnp.float32),
        "w_o": w(ks[15], (L, MIX_OUT, D), MIX_OUT),
        "norm_ffn2": gain(ks[16], (L, D)),
        "ffn2_gate": w(ks[17], (L, D, D_FF), D),
        "ffn2_up": w(ks[18], (L, D, D_FF), D),
        "ffn2_down": w(ks[19], (L, D_FF, D), D_FF),
        "rel_bias": 0.5 * jax.random.normal(ks[20], (NUM_BUCKETS, SWA_HEADS), jnp.float32),
        "norm_final": gain(ks[21], (D,)),
    }


def _fwd_reference(x, c, w_mod, b_mod, norm_ffn1, ffn1_gate, ffn1_up, ffn1_down, norm_mix, w_in,
              q_norm, kv_norm, w_uq, w_ukv, sinks, w_o, norm_ffn2, ffn2_gate, ffn2_up, ffn2_down,
              rel_bias, norm_final):
    c_act = jax.nn.silu(c)
    for l in range(DEPTH):
        mod = c_act @ w_mod[l] + b_mod[l]
        sh1, sc1, g1, sh2, sc2, g2, sh3, sc3, g3 = jnp.split(mod, N_MOD, axis=-1)

        h = modulate(rms_norm(x, norm_ffn1[l]), sh1, sc1)
        x = x + FFN_RES_WEIGHT * g1[:, None, :] * swiglu(h, ffn1_gate[l], ffn1_up[l], ffn1_down[l])

        h = modulate(rms_norm(x, norm_mix[l]), sh2, sc2)
        x = x + g2[:, None, :] * token_mixing(h, w_in[l], q_norm[l], kv_norm[l], w_uq[l], w_ukv[l],
                                              sinks[l], w_o[l], rel_bias)

        h = modulate(rms_norm(x, norm_ffn2[l]), sh3, sc3)
        x = x + FFN_RES_WEIGHT * g3[:, None, :] * swiglu(h, ffn2_gate[l], ffn2_up[l], ffn2_down[l])
    return rms_norm(x, norm_final)


import jax as _jax
import jax.numpy as _jnp

TWIN_FORMAT = 'train_step'
FWD_PARAMS = ['x', 'c', 'w_mod', 'b_mod', 'norm_ffn1', 'ffn1_gate', 'ffn1_up', 'ffn1_down', 'norm_mix', 'w_in', 'q_norm', 'kv_norm', 'w_uq', 'w_ukv', 'sinks', 'w_o', 'norm_ffn2', 'ffn2_gate', 'ffn2_up', 'ffn2_down', 'rel_bias', 'norm_final']
TWIN_WEIGHTS = ['w_mod', 'b_mod', 'norm_ffn1', 'ffn1_gate', 'ffn1_up', 'ffn1_down', 'norm_mix', 'w_in', 'q_norm', 'kv_norm', 'w_uq', 'w_ukv', 'sinks', 'w_o', 'norm_ffn2', 'ffn2_gate', 'ffn2_up', 'ffn2_down', 'rel_bias', 'norm_final']
TWIN_DIFF_INPUT = 'x'
TWIN_INPUTS = ['x', 'c', 'w_mod', 'b_mod', 'norm_ffn1', 'ffn1_gate', 'ffn1_up', 'ffn1_down', 'norm_mix', 'w_in', 'q_norm', 'kv_norm', 'w_uq', 'w_ukv', 'sinks', 'w_o', 'norm_ffn2', 'ffn2_gate', 'ffn2_up', 'ffn2_down', 'rel_bias', 'norm_final', 'loss_target', 'm_w_mod', 'm_b_mod', 'm_norm_ffn1', 'm_ffn1_gate', 'm_ffn1_up', 'm_ffn1_down', 'm_norm_mix', 'm_w_in', 'm_q_norm', 'm_kv_norm', 'm_w_uq', 'm_w_ukv', 'm_sinks', 'm_w_o', 'm_norm_ffn2', 'm_ffn2_gate', 'm_ffn2_up', 'm_ffn2_down', 'm_rel_bias', 'm_norm_final', 'v_w_mod', 'v_b_mod', 'v_norm_ffn1', 'v_ffn1_gate', 'v_ffn1_up', 'v_ffn1_down', 'v_norm_mix', 'v_w_in', 'v_q_norm', 'v_kv_norm', 'v_w_uq', 'v_w_ukv', 'v_sinks', 'v_w_o', 'v_norm_ffn2', 'v_ffn2_gate', 'v_ffn2_up', 'v_ffn2_down', 'v_rel_bias', 'v_norm_final']
TWIN_OUTPUTS = ['loss', 'grad_x', 'grad_w_mod', 'grad_b_mod', 'grad_norm_ffn1', 'grad_ffn1_gate', 'grad_ffn1_up', 'grad_ffn1_down', 'grad_norm_mix', 'grad_w_in', 'grad_q_norm', 'grad_kv_norm', 'grad_w_uq', 'grad_w_ukv', 'grad_sinks', 'grad_w_o', 'grad_norm_ffn2', 'grad_ffn2_gate', 'grad_ffn2_up', 'grad_ffn2_down', 'grad_rel_bias', 'grad_norm_final', 'delta_w_mod', 'delta_b_mod', 'delta_norm_ffn1', 'delta_ffn1_gate', 'delta_ffn1_up', 'delta_ffn1_down', 'delta_norm_mix', 'delta_w_in', 'delta_q_norm', 'delta_kv_norm', 'delta_w_uq', 'delta_w_ukv', 'delta_sinks', 'delta_w_o', 'delta_norm_ffn2', 'delta_ffn2_gate', 'delta_ffn2_up', 'delta_ffn2_down', 'delta_rel_bias', 'delta_norm_final', 'new_m_w_mod', 'new_m_b_mod', 'new_m_norm_ffn1', 'new_m_ffn1_gate', 'new_m_ffn1_up', 'new_m_ffn1_down', 'new_m_norm_mix', 'new_m_w_in', 'new_m_q_norm', 'new_m_kv_norm', 'new_m_w_uq', 'new_m_w_ukv', 'new_m_sinks', 'new_m_w_o', 'new_m_norm_ffn2', 'new_m_ffn2_gate', 'new_m_ffn2_up', 'new_m_ffn2_down', 'new_m_rel_bias', 'new_m_norm_final', 'new_v_w_mod', 'new_v_b_mod', 'new_v_norm_ffn1', 'new_v_ffn1_gate', 'new_v_ffn1_up', 'new_v_ffn1_down', 'new_v_norm_mix', 'new_v_w_in', 'new_v_q_norm', 'new_v_kv_norm', 'new_v_w_uq', 'new_v_w_ukv', 'new_v_sinks', 'new_v_w_o', 'new_v_norm_ffn2', 'new_v_ffn2_gate', 'new_v_ffn2_up', 'new_v_ffn2_down', 'new_v_rel_bias', 'new_v_norm_final']
TWIN_LEAF_KINDS = {'loss': 'loss', 'grad_x': 'grad_x', 'grad_w_mod': 'grad_w', 'grad_b_mod': 'grad_w', 'grad_norm_ffn1': 'grad_w', 'grad_ffn1_gate': 'grad_w', 'grad_ffn1_up': 'grad_w', 'grad_ffn1_down': 'grad_w', 'grad_norm_mix': 'grad_w', 'grad_w_in': 'grad_w', 'grad_q_norm': 'grad_w', 'grad_kv_norm': 'grad_w', 'grad_w_uq': 'grad_w', 'grad_w_ukv': 'grad_w', 'grad_sinks': 'grad_w', 'grad_w_o': 'grad_w', 'grad_norm_ffn2': 'grad_w', 'grad_ffn2_gate': 'grad_w', 'grad_ffn2_up': 'grad_w', 'grad_ffn2_down': 'grad_w', 'grad_rel_bias': 'grad_w', 'grad_norm_final': 'grad_w', 'delta_w_mod': 'delta_w', 'delta_b_mod': 'delta_w', 'delta_norm_ffn1': 'delta_w', 'delta_ffn1_gate': 'delta_w', 'delta_ffn1_up': 'delta_w', 'delta_ffn1_down': 'delta_w', 'delta_norm_mix': 'delta_w', 'delta_w_in': 'delta_w', 'delta_q_norm': 'delta_w', 'delta_kv_norm': 'delta_w', 'delta_w_uq': 'delta_w', 'delta_w_ukv': 'delta_w', 'delta_sinks': 'delta_w', 'delta_w_o': 'delta_w', 'delta_norm_ffn2': 'delta_w', 'delta_ffn2_gate': 'delta_w', 'delta_ffn2_up': 'delta_w', 'delta_ffn2_down': 'delta_w', 'delta_rel_bias': 'delta_w', 'delta_norm_final': 'delta_w', 'new_m_w_mod': 'new_m', 'new_m_b_mod': 'new_m', 'new_m_norm_ffn1': 'new_m', 'new_m_ffn1_gate': 'new_m', 'new_m_ffn1_up': 'new_m', 'new_m_ffn1_down': 'new_m', 'new_m_norm_mix': 'new_m', 'new_m_w_in': 'new_m', 'new_m_q_norm': 'new_m', 'new_m_kv_norm': 'new_m', 'new_m_w_uq': 'new_m', 'new_m_w_ukv': 'new_m', 'new_m_sinks': 'new_m', 'new_m_w_o': 'new_m', 'new_m_norm_ffn2': 'new_m', 'new_m_ffn2_gate': 'new_m', 'new_m_ffn2_up': 'new_m', 'new_m_ffn2_down': 'new_m', 'new_m_rel_bias': 'new_m', 'new_m_norm_final': 'new_m', 'new_v_w_mod': 'new_v', 'new_v_b_mod': 'new_v', 'new_v_norm_ffn1': 'new_v', 'new_v_ffn1_gate': 'new_v', 'new_v_ffn1_up': 'new_v', 'new_v_ffn1_down': 'new_v', 'new_v_norm_mix': 'new_v', 'new_v_w_in': 'new_v', 'new_v_q_norm': 'new_v', 'new_v_kv_norm': 'new_v', 'new_v_w_uq': 'new_v', 'new_v_w_ukv': 'new_v', 'new_v_sinks': 'new_v', 'new_v_w_o': 'new_v', 'new_v_norm_ffn2': 'new_v', 'new_v_ffn2_gate': 'new_v', 'new_v_ffn2_up': 'new_v', 'new_v_ffn2_down': 'new_v', 'new_v_rel_bias': 'new_v', 'new_v_norm_final': 'new_v'}


def _forward(args):
    return _fwd_reference(*[args[k] for k in FWD_PARAMS])


def _output_shape():
    out = _jax.eval_shape(lambda: _forward(_fwd_setup_inputs(0)))
    return out.shape, out.dtype

N_MICROBATCH = 1
ADAM_LR = 0.001
ADAM_B1 = 0.9
ADAM_B2 = 0.999
ADAM_EPS = 1e-08
ADAM_WD = 0.01
ADAM_STEP = 10
PER_EXAMPLE_BATCH_AXIS = {'x': 0, 'c': 0, 'loss_target': 0}
SHARED_INPUTS = []
_WEIGHT_DTYPES = {'w_mod': _jnp.float32, 'b_mod': _jnp.float32, 'norm_ffn1': _jnp.float32, 'ffn1_gate': _jnp.float32, 'ffn1_up': _jnp.float32, 'ffn1_down': _jnp.float32, 'norm_mix': _jnp.float32, 'w_in': _jnp.float32, 'q_norm': _jnp.float32, 'kv_norm': _jnp.float32, 'w_uq': _jnp.float32, 'w_ukv': _jnp.float32, 'sinks': _jnp.float32, 'w_o': _jnp.float32, 'norm_ffn2': _jnp.float32, 'ffn2_gate': _jnp.float32, 'ffn2_up': _jnp.float32, 'ffn2_down': _jnp.float32, 'rel_bias': _jnp.float32, 'norm_final': _jnp.float32}
MOMENT_SCALE = {'w_mod': 2.118913e-02, 'b_mod': 3.598027e-02, 'norm_ffn1': 1.973400e-02, 'ffn1_gate': 8.583983e-03, 'ffn1_up': 8.302756e-03, 'ffn1_down': 1.377495e-02, 'norm_mix': 1.496399e-02, 'w_in': 1.731922e-02, 'q_norm': 1.040248e-02, 'kv_norm': 2.942687e-02, 'w_uq': 5.587434e-03, 'w_ukv': 1.101776e-02, 'sinks': 3.708554e-03, 'w_o': 1.526293e-02, 'norm_ffn2': 1.901642e-02, 'ffn2_gate': 8.675068e-03, 'ffn2_up': 8.413444e-03, 'ffn2_down': 1.395574e-02, 'rel_bias': 1.131749e-02, 'norm_final': 1.604794e+01}


def _to_microbatches(a, axis):
    t = _jnp.moveaxis(a, axis, 0)
    t = t.reshape((N_MICROBATCH, t.shape[0] // N_MICROBATCH) + t.shape[1:])
    return _jnp.moveaxis(t, 1, axis + 1)


def setup_inputs(seed: int = 0) -> dict:
    inp = _fwd_setup_inputs(seed)
    key = _jax.random.fold_in(_jax.random.key(seed), 7919)
    shape, _ = _output_shape()
    out = dict(inp)
    out["loss_target"] = _jax.random.normal(_jax.random.fold_in(key, 0), shape, _jnp.float32)
    for i, name in enumerate(TWIN_WEIGHTS):
        w = inp[name].astype(_jnp.float32)
        if MOMENT_SCALE is None:
            s = _jnp.sqrt(_jnp.mean(_jnp.square(w)) + 1e-30)
        else:
            s = MOMENT_SCALE[name]
        km, kv = _jax.random.split(_jax.random.fold_in(key, i + 1))
        out[name] = w
        out["m_" + name] = s * _jax.random.normal(km, w.shape, _jnp.float32)
        out["v_" + name] = (s * s) * _jax.random.uniform(kv, w.shape, _jnp.float32, 0.5, 1.5)
    if N_MICROBATCH > 1:
        for name, axis in PER_EXAMPLE_BATCH_AXIS.items():
            out[name] = _to_microbatches(out[name], axis)
    return {'x': out['x'], 'c': out['c'], 'w_mod': out['w_mod'], 'b_mod': out['b_mod'], 'norm_ffn1': out['norm_ffn1'], 'ffn1_gate': out['ffn1_gate'], 'ffn1_up': out['ffn1_up'], 'ffn1_down': out['ffn1_down'], 'norm_mix': out['norm_mix'], 'w_in': out['w_in'], 'q_norm': out['q_norm'], 'kv_norm': out['kv_norm'], 'w_uq': out['w_uq'], 'w_ukv': out['w_ukv'], 'sinks': out['sinks'], 'w_o': out['w_o'], 'norm_ffn2': out['norm_ffn2'], 'ffn2_gate': out['ffn2_gate'], 'ffn2_up': out['ffn2_up'], 'ffn2_down': out['ffn2_down'], 'rel_bias': out['rel_bias'], 'norm_final': out['norm_final'], 'loss_target': out['loss_target'], 'm_w_mod': out['m_w_mod'], 'm_b_mod': out['m_b_mod'], 'm_norm_ffn1': out['m_norm_ffn1'], 'm_ffn1_gate': out['m_ffn1_gate'], 'm_ffn1_up': out['m_ffn1_up'], 'm_ffn1_down': out['m_ffn1_down'], 'm_norm_mix': out['m_norm_mix'], 'm_w_in': out['m_w_in'], 'm_q_norm': out['m_q_norm'], 'm_kv_norm': out['m_kv_norm'], 'm_w_uq': out['m_w_uq'], 'm_w_ukv': out['m_w_ukv'], 'm_sinks': out['m_sinks'], 'm_w_o': out['m_w_o'], 'm_norm_ffn2': out['m_norm_ffn2'], 'm_ffn2_gate': out['m_ffn2_gate'], 'm_ffn2_up': out['m_ffn2_up'], 'm_ffn2_down': out['m_ffn2_down'], 'm_rel_bias': out['m_rel_bias'], 'm_norm_final': out['m_norm_final'], 'v_w_mod': out['v_w_mod'], 'v_b_mod': out['v_b_mod'], 'v_norm_ffn1': out['v_norm_ffn1'], 'v_ffn1_gate': out['v_ffn1_gate'], 'v_ffn1_up': out['v_ffn1_up'], 'v_ffn1_down': out['v_ffn1_down'], 'v_norm_mix': out['v_norm_mix'], 'v_w_in': out['v_w_in'], 'v_q_norm': out['v_q_norm'], 'v_kv_norm': out['v_kv_norm'], 'v_w_uq': out['v_w_uq'], 'v_w_ukv': out['v_w_ukv'], 'v_sinks': out['v_sinks'], 'v_w_o': out['v_w_o'], 'v_norm_ffn2': out['v_norm_ffn2'], 'v_ffn2_gate': out['v_ffn2_gate'], 'v_ffn2_up': out['v_ffn2_up'], 'v_ffn2_down': out['v_ffn2_down'], 'v_rel_bias': out['v_rel_bias'], 'v_norm_final': out['v_norm_final']}


def _loss(weights, diff, rest, loss_target):
    with _jax.named_scope("forward"):
        args = {**rest, TWIN_DIFF_INPUT: diff, **{k: w.astype(_WEIGHT_DTYPES[k]) for k, w in weights.items()}}
        y = _forward(args)
    with _jax.named_scope("loss_head"):
        err = _jnp.square(y.astype(_jnp.float32) - loss_target)
        return 0.5 * _jnp.sum(_jnp.mean(err, axis=-1)) if err.ndim else 0.5 * err


def _adamw(w, g, m, v):
    m = ADAM_B1 * m + (1.0 - ADAM_B1) * g
    v = ADAM_B2 * v + (1.0 - ADAM_B2) * _jnp.square(g)
    m_hat = m / (1.0 - ADAM_B1 ** ADAM_STEP)
    v_hat = v / (1.0 - ADAM_B2 ** ADAM_STEP)
    delta = -ADAM_LR * (m_hat / (_jnp.sqrt(v_hat) + ADAM_EPS) + ADAM_WD * w)
    return delta, m, v


def reference(x, c, w_mod, b_mod, norm_ffn1, ffn1_gate, ffn1_up, ffn1_down, norm_mix, w_in, q_norm, kv_norm, w_uq, w_ukv, sinks, w_o, norm_ffn2, ffn2_gate, ffn2_up, ffn2_down, rel_bias, norm_final, loss_target, m_w_mod, m_b_mod, m_norm_ffn1, m_ffn1_gate, m_ffn1_up, m_ffn1_down, m_norm_mix, m_w_in, m_q_norm, m_kv_norm, m_w_uq, m_w_ukv, m_sinks, m_w_o, m_norm_ffn2, m_ffn2_gate, m_ffn2_up, m_ffn2_down, m_rel_bias, m_norm_final, v_w_mod, v_b_mod, v_norm_ffn1, v_ffn1_gate, v_ffn1_up, v_ffn1_down, v_norm_mix, v_w_in, v_q_norm, v_kv_norm, v_w_uq, v_w_ukv, v_sinks, v_w_o, v_norm_ffn2, v_ffn2_gate, v_ffn2_up, v_ffn2_down, v_rel_bias, v_norm_final):
    given = dict(x=x, c=c, w_mod=w_mod, b_mod=b_mod, norm_ffn1=norm_ffn1, ffn1_gate=ffn1_gate, ffn1_up=ffn1_up, ffn1_down=ffn1_down, norm_mix=norm_mix, w_in=w_in, q_norm=q_norm, kv_norm=kv_norm, w_uq=w_uq, w_ukv=w_ukv, sinks=sinks, w_o=w_o, norm_ffn2=norm_ffn2, ffn2_gate=ffn2_gate, ffn2_up=ffn2_up, ffn2_down=ffn2_down, rel_bias=rel_bias, norm_final=norm_final, loss_target=loss_target, m_w_mod=m_w_mod, m_b_mod=m_b_mod, m_norm_ffn1=m_norm_ffn1, m_ffn1_gate=m_ffn1_gate, m_ffn1_up=m_ffn1_up, m_ffn1_down=m_ffn1_down, m_norm_mix=m_norm_mix, m_w_in=m_w_in, m_q_norm=m_q_norm, m_kv_norm=m_kv_norm, m_w_uq=m_w_uq, m_w_ukv=m_w_ukv, m_sinks=m_sinks, m_w_o=m_w_o, m_norm_ffn2=m_norm_ffn2, m_ffn2_gate=m_ffn2_gate, m_ffn2_up=m_ffn2_up, m_ffn2_down=m_ffn2_down, m_rel_bias=m_rel_bias, m_norm_final=m_norm_final, v_w_mod=v_w_mod, v_b_mod=v_b_mod, v_norm_ffn1=v_norm_ffn1, v_ffn1_gate=v_ffn1_gate, v_ffn1_up=v_ffn1_up, v_ffn1_down=v_ffn1_down, v_norm_mix=v_norm_mix, v_w_in=v_w_in, v_q_norm=v_q_norm, v_kv_norm=v_kv_norm, v_w_uq=v_w_uq, v_w_ukv=v_w_ukv, v_sinks=v_sinks, v_w_o=v_w_o, v_norm_ffn2=v_norm_ffn2, v_ffn2_gate=v_ffn2_gate, v_ffn2_up=v_ffn2_up, v_ffn2_down=v_ffn2_down, v_rel_bias=v_rel_bias, v_norm_final=v_norm_final)
    weights = {n: given[n] for n in TWIN_WEIGHTS}
    shared = {n: given[n] for n in SHARED_INPUTS}
    per_example = {n: given[n] for n in ['x', 'c']}
    grad_fn = _jax.value_and_grad(_loss, argnums=(0, 1))

    def one_microbatch(ex, loss_target):
        ex = dict(ex)
        diff = ex.pop(TWIN_DIFF_INPUT)
        return grad_fn(weights, diff, {**shared, **ex}, loss_target)

    if N_MICROBATCH == 1:
        loss, (grad_w, grad_x) = one_microbatch(per_example, given["loss_target"])
    else:
        def body(carry, xs):
            loss_sum, grad_sum = carry
            l_k, (gw_k, gx_k) = one_microbatch(xs[0], xs[1])
            with _jax.named_scope("update"):
                return (loss_sum + l_k, _jax.tree.map(_jnp.add, grad_sum, gw_k)), gx_k

        init = (_jnp.zeros((), _jnp.float32), _jax.tree.map(_jnp.zeros_like, weights))
        (loss, grad_w), grad_x = _jax.lax.scan(body, init, (per_example, given["loss_target"]))
    with _jax.named_scope("update"):
        delta_w, new_m, new_v = {}, {}, {}
        for n in TWIN_WEIGHTS:
            delta_w[n], new_m[n], new_v[n] = _adamw(weights[n], grad_w[n], given["m_" + n], given["v_" + n])
    return (loss, grad_x, *[grad_w[n] for n in TWIN_WEIGHTS], *[delta_w[n] for n in TWIN_WEIGHTS],
            *[new_m[n] for n in TWIN_WEIGHTS], *[new_v[n] for n in TWIN_WEIGHTS])
```

```python
import functools
import math

import numpy as np
import jax
import jax.numpy as jnp
from jax import lax
from jax.experimental import pallas as pl
from jax.experimental.pallas import tpu as pltpu

F32 = jnp.float32
BF16 = jnp.bfloat16
MESH = pl.DeviceIdType.MESH

N_DEV = 8
D = 1024
D_FF = 2816
EPS = 1e-6
N_MOD = 9
SWA_HEADS = 8
SWA_DH = 64
WINDOW = 128
MLA_HEADS = 4
MLA_NOPE = 128
MLA_ROPE = 64
MLA_V = 128
MLA_QR = 256
MLA_KVR = 128
ROPE_THETA = 10000.0
NUM_BUCKETS = 32
D_IN = 1216
D_IN_PAD = 1280
SWA_SCALE = SWA_DH ** -0.5
MLA_SCALE = (MLA_NOPE + MLA_ROPE) ** -0.5

ADAM_LR = 0.001
ADAM_B1 = 0.9
ADAM_B2 = 0.999
ADAM_EPS = 1e-08
ADAM_WD = 0.01
ADAM_STEP = 10

V7X_VMEM_LIMIT = 56 * 1024 * 1024

NT_DIMS = (((1,), (1,)), ((), ()))
TN_DIMS = (((0,), (0,)), ((), ()))


def _dot(a, b):
    return jnp.dot(a, b, preferred_element_type=F32)


def _dot_nt(a, b):
    return lax.dot_general(a, b, NT_DIMS, preferred_element_type=F32)


def _dot_tn(a, b):
    return lax.dot_general(a, b, TN_DIMS, preferred_element_type=F32)


def _params(sem=None):
    return pltpu.CompilerParams(dimension_semantics=sem, vmem_limit_bytes=V7X_VMEM_LIMIT)


def _rstd(x):
    return lax.rsqrt(jnp.mean(x * x, axis=-1, keepdims=True) + EPS)


def _rms_bwd(dy, xhat, r):
    return r * (dy - xhat * jnp.mean(dy * xhat, axis=-1, keepdims=True))


def _sigmoid(a):
    return 1.0 / (1.0 + jnp.exp(-a))


def _ffn_fwd(x, vecs, wgT, wuT, wd, *, name, tm=512, tf=256):
    S = x.shape[0]
    tm = min(tm, S)
    ni, nj = S // tm, D_FF // tf

    def body(x_ref, vec_ref, wg_ref, wu_ref, wd_ref, xo_ref, h_ref, a_ref, b_ref, f_ref, acc_ref):
        j = pl.program_id(1)

        @pl.when(j == 0)
        def _():
            xv = x_ref[...]
            hn = xv * _rstd(xv) * vec_ref[0:1, :]
            h_ref[...] = (hn * (1.0 + vec_ref[2:3, :]) + vec_ref[1:2, :]).astype(BF16)
            acc_ref[...] = jnp.zeros_like(acc_ref)

        h = h_ref[...]
        a = _dot_nt(h, wg_ref[...])
        b = _dot_nt(h, wu_ref[...])
        a_ref[...] = a.astype(BF16)
        b_ref[...] = b.astype(BF16)
        hsw = (a * _sigmoid(a) * b).astype(BF16)
        acc_ref[...] += _dot(hsw, wd_ref[...])

        @pl.when(j == nj - 1)
        def _():
            f = acc_ref[...]
            f_ref[...] = f
            xo_ref[...] = x_ref[...] + (0.5 * vec_ref[3:4, :]) * f

    row = pl.BlockSpec((tm, D), lambda i, j: (i, 0))
    wspec = pl.BlockSpec((tf, D), lambda i, j: (j, 0))
    act = pl.BlockSpec((tm, tf), lambda i, j: (i, j))
    return pl.pallas_call(
        body, name=name, grid=(ni, nj),
        in_specs=[row, pl.BlockSpec((8, D), lambda i, j: (0, 0)), wspec, wspec, wspec],
        out_specs=[row, row, act, act, row],
        out_shape=[jax.ShapeDtypeStruct((S, D), F32), jax.ShapeDtypeStruct((S, D), BF16),
                   jax.ShapeDtypeStruct((S, D_FF), BF16), jax.ShapeDtypeStruct((S, D_FF), BF16),
                   jax.ShapeDtypeStruct((S, D), F32)],
        scratch_shapes=[pltpu.VMEM((tm, D), F32)],
        compiler_params=_params(("parallel", "arbitrary")),
    )(x, vecs, wgT, wuT, wd)


def _ffn_bwd_pre(dxo, f, vecs, *, name, tm=256):
    S = dxo.shape[0]

    def body(dx_ref, f_ref, vec_ref, df_ref, part_ref):
        @pl.when(pl.program_id(0) == 0)
        def _():
            part_ref[...] = jnp.zeros_like(part_ref)

        dx = dx_ref[...]
        df_ref[...] = ((0.5 * vec_ref[3:4, :]) * dx).astype(BF16)
        part_ref[0:1, :] += 0.5 * jnp.sum(dx * f_ref[...], axis=0, keepdims=True)

    row = pl.BlockSpec((tm, D), lambda i: (i, 0))
    vec = pl.BlockSpec((8, D), lambda i: (0, 0))
    return pl.pallas_call(
        body, name=name, grid=(S // tm,), in_specs=[row, row, vec], out_specs=[row, vec],
        out_shape=[jax.ShapeDtypeStruct((S, D), BF16), jax.ShapeDtypeStruct((8, D), F32)],
        compiler_params=_params(("arbitrary",)),
    )(dxo, f, vecs)


def _ffn_bwd_main(h, df, a, b, wgT, wuT, wd, *, name, tm=512, tf=256):
    S = h.shape[0]
    tm = min(tm, S)
    ni, nj = S // tm, D_FF // tf

    def body(h_hbm, df_hbm, a_ref, b_ref, wg_ref, wu_ref, wd_ref,
             gg_ref, gu_ref, gd_ref, dh_hbm,
             h_v, df_v, dh_v, gg_acc, gu_acc, gd_acc, sem):
        j = pl.program_id(0)
        i = pl.program_id(1)

        @pl.when((j == 0) & (i == 0))
        def _():
            c1 = pltpu.make_async_copy(h_hbm, h_v, sem.at[0])
            c2 = pltpu.make_async_copy(df_hbm, df_v, sem.at[1])
            c1.start()
            c2.start()
            c1.wait()
            c2.wait()

        @pl.when(i == 0)
        def _():
            gg_acc[...] = jnp.zeros_like(gg_acc)
            gu_acc[...] = jnp.zeros_like(gu_acc)
            gd_acc[...] = jnp.zeros_like(gd_acc)

        rows = pl.ds(pl.multiple_of(i * tm, tm), tm)
        hi = h_v[rows, :]
        dfi = df_v[rows, :]
        av = a_ref[...].astype(F32)
        bv = b_ref[...].astype(F32)
        sg = _sigmoid(av)
        sa = av * sg
        hsw = (sa * bv).astype(BF16)
        dhsw = _dot_nt(dfi, wd_ref[...])
        da = (dhsw * bv * (sg * (1.0 + av * (1.0 - sg)))).astype(BF16)
        db = (dhsw * sa).astype(BF16)
        gd_acc[...] += _dot_tn(hsw, dfi)
        gg_acc[...] += _dot_tn(da, hi)
        gu_acc[...] += _dot_tn(db, hi)
        dh = _dot(da, wg_ref[...]) + _dot(db, wu_ref[...])

        @pl.when(j == 0)
        def _():
            dh_v[rows, :] = dh

        @pl.when(j > 0)
        def _():
            dh_v[rows, :] += dh

        @pl.when(i == ni - 1)
        def _():
            gg_ref[...] = gg_acc[...].astype(BF16)
            gu_ref[...] = gu_acc[...].astype(BF16)
            gd_ref[...] = gd_acc[...].astype(BF16)

        @pl.when((j == nj - 1) & (i == ni - 1))
        def _():
            c3 = pltpu.make_async_copy(dh_v, dh_hbm, sem.at[2])
            c3.start()
            c3.wait()

    anyspec = pl.BlockSpec(memory_space=pl.ANY)
    wspec = pl.BlockSpec((tf, D), lambda j, i: (j, 0))
    act = pl.BlockSpec((tm, tf), lambda j, i: (i, j))
    return pl.pallas_call(
        body, name=name, grid=(nj, ni),
        in_specs=[anyspec, anyspec, act, act, wspec, wspec, wspec],
        out_specs=[wspec, wspec, wspec, anyspec],
        out_shape=[jax.ShapeDtypeStruct((D_FF, D), BF16)] * 3 + [jax.ShapeDtypeStruct((S, D), F32)],
        scratch_shapes=[pltpu.VMEM((S, D), BF16), pltpu.VMEM((S, D), BF16), pltpu.VMEM((S, D), F32),
                        pltpu.VMEM((tf, D), F32), pltpu.VMEM((tf, D), F32), pltpu.VMEM((tf, D), F32),
                        pltpu.SemaphoreType.DMA((3,))],
        compiler_params=_params(("arbitrary", "arbitrary")),
    )(h, df, a, b, wgT, wuT, wd)


def _norm_bwd(dh, x, dxo, vecs, *, name, tm=256):
    S = x.shape[0]

    def body(dh_ref, x_ref, dxo_ref, vec_ref, dx_ref, part_ref):
        @pl.when(pl.program_id(0) == 0)
        def _():
            part_ref[...] = jnp.zeros_like(part_ref)

        dh = dh_ref[...]
        xv = x_ref[...]
        r = _rstd(xv)
        xhat = xv * r
        w = vec_ref[0:1, :]
        xn = xhat * w
        dxn = dh * (1.0 + vec_ref[2:3, :])
        part_ref[0:1, :] += jnp.sum(dxn * xhat, axis=0, keepdims=True)
        part_ref[1:2, :] += jnp.sum(dh, axis=0, keepdims=True)
        part_ref[2:3, :] += jnp.sum(dh * xn, axis=0, keepdims=True)
        dx_ref[...] = dxo_ref[...] + _rms_bwd(dxn * w, xhat, r)

    row = pl.BlockSpec((tm, D), lambda i: (i, 0))
    vec = pl.BlockSpec((8, D), lambda i: (0, 0))
    return pl.pallas_call(
        body, name=name, grid=(S // tm,), in_specs=[row, row, row, vec], out_specs=[row, vec],
        out_shape=[jax.ShapeDtypeStruct((S, D), F32), jax.ShapeDtypeStruct((8, D), F32)],
        compiler_params=_params(("arbitrary",)),
    )(dh, x, dxo, vecs)


def _head(x, tgt, nf, *, tm=256):
    S = x.shape[0]

    def body(x_ref, t_ref, nf_ref, dx_ref, part_ref):
        @pl.when(pl.program_id(0) == 0)
        def _():
            part_ref[...] = jnp.zeros_like(part_ref)

        xv = x_ref[...]
        r = _rstd(xv)
        xhat = xv * r
        w = nf_ref[...]
        e = xhat * w - t_ref[...]
        dy = e * (1.0 / D)
        part_ref[0:1, :] += jnp.sum(dy * xhat, axis=0, keepdims=True)
        part_ref[1:2, :] += jnp.sum(e * e) * (0.5 / D)
        dx_ref[...] = _rms_bwd(dy * w, xhat, r)

    row = pl.BlockSpec((tm, D), lambda i: (i, 0))
    return pl.pallas_call(
        body, name="head", grid=(S // tm,),
        in_specs=[row, row, pl.BlockSpec((1, D), lambda i: (0, 0))],
        out_specs=[row, pl.BlockSpec((8, D), lambda i: (0, 0))],
        out_shape=[jax.ShapeDtypeStruct((S, D), F32), jax.ShapeDtypeStruct((8, D), F32)],
        compiler_params=_params(("arbitrary",)),
    )(x, tgt, nf)


def _mix_in_fwd(x, vecs, w_inT, *, tm=256):
    S = x.shape[0]

    def body(x_ref, vec_ref, w_ref, h_ref, p_ref):
        xv = x_ref[...]
        hn = xv * _rstd(xv) * vec_ref[0:1, :]
        h = (hn * (1.0 + vec_ref[2:3, :]) + vec_ref[1:2, :]).astype(BF16)
        h_ref[...] = h
        p_ref[...] = _dot_nt(h, w_ref[...])

    row = pl.BlockSpec((tm, D), lambda i: (i, 0))
    return pl.pallas_call(
        body, name="mix_in_fwd", grid=(S // tm,),
        in_specs=[row, pl.BlockSpec((8, D), lambda i: (0, 0)), pl.BlockSpec((D_IN_PAD, D), lambda i: (0, 0))],
        out_specs=[row, pl.BlockSpec((tm, D_IN_PAD), lambda i: (i, 0))],
        out_shape=[jax.ShapeDtypeStruct((S, D), BF16), jax.ShapeDtypeStruct((S, D_IN_PAD), F32)],
        compiler_params=_params(("parallel",)),
    )(x, vecs, w_inT)


def _bucket_table():
    qi = np.arange(WINDOW)[:, None]
    kj = np.arange(2 * WINDOW)[None, :]
    dist = qi + WINDOW - kj
    max_exact = NUM_BUCKETS // 2
    n = np.maximum(dist, 0)
    nf = np.maximum(n, 1).astype(np.float32)
    large = max_exact + (np.log(nf / np.float32(max_exact)) / np.float32(math.log(WINDOW / max_exact))
                         * np.float32(NUM_BUCKETS - max_exact)).astype(np.int32)
    large = np.minimum(large, NUM_BUCKETS - 1)
    return np.where(n < max_exact, n, large).astype(np.int32)


def _bias_build(rel_bias, bucket):
    def body(rb_ref, bk_ref, out_ref):
        bk = bk_ref[...]
        for h in range(SWA_HEADS):
            acc = jnp.zeros((WINDOW, 2 * WINDOW), F32)
            for b in range(NUM_BUCKETS):
                acc = jnp.where(bk == b, rb_ref[b, h], acc)
            out_ref[h] = acc

    return pl.pallas_call(
        body, name="bias_build",
        in_specs=[pl.BlockSpec(memory_space=pltpu.SMEM), pl.BlockSpec(memory_space=pltpu.VMEM)],
        out_specs=pl.BlockSpec(memory_space=pltpu.VMEM),
        out_shape=jax.ShapeDtypeStruct((SWA_HEADS, WINDOW, 2 * WINDOW), F32),
    )(rel_bias, bucket)


def _swa_valid(n):
    row = lax.broadcasted_iota(jnp.int32, (WINDOW, 2 * WINDOW), 0)
    col = lax.broadcasted_iota(jnp.int32, (WINDOW, 2 * WINDOW), 1)
    dist = row + WINDOW - col
    return (dist >= 0) & (dist < WINDOW) & ((col >= WINDOW) | (n > 0))


def _swa_probs(qh, kk, bias_h, sink, valid):
    s = _dot_nt(qh, kk) * SWA_SCALE + bias_h
    s = jnp.where(valid, s, -jnp.inf)
    m = jnp.maximum(jnp.max(s, axis=-1, keepdims=True), sink)
    p = jnp.exp(s - m)
    ps = jnp.exp(sink - m)
    inv = 1.0 / (jnp.sum(p, axis=-1, keepdims=True) + ps)
    return p * inv, ps * inv


def _swa_specs():
    prev = lambda n: jnp.maximum(n - 1, 0)
    return [pl.BlockSpec((WINDOW, 512), lambda n: (n, 0)),
            pl.BlockSpec((WINDOW, 128), lambda n: (n, 4)),
            pl.BlockSpec((WINDOW, 128), lambda n: (prev(n), 4)),
            pl.BlockSpec((WINDOW, 128), lambda n: (n, 5)),
            pl.BlockSpec((WINDOW, 128), lambda n: (prev(n), 5)),
            pl.BlockSpec((SWA_HEADS, WINDOW, 2 * WINDOW), lambda n: (0, 0, 0)),
            pl.BlockSpec(memory_space=pltpu.SMEM)]


def _swa_fwd(proj, bias, sinks):
    S = proj.shape[0]

    def body(q_ref, kc_ref, kp_ref, vc_ref, vp_ref, bias_ref, sink_ref, o_ref):
        valid = _swa_valid(pl.program_id(0))
        q = q_ref[...].astype(BF16)
        kfull = jnp.concatenate([kp_ref[...], kc_ref[...]], axis=0).astype(BF16)
        vfull = jnp.concatenate([vp_ref[...], vc_ref[...]], axis=0).astype(BF16)
        outs = []
        for h in range(SWA_HEADS):
            g = h // 4
            kk = kfull[:, 64 * g:64 * g + 64]
            vv = vfull[:, 64 * g:64 * g + 64]
            pk, _ = _swa_probs(q[:, 64 * h:64 * h + 64], kk, bias_ref[h], sink_ref[0, h], valid)
            outs.append(_dot(pk.astype(BF16), vv))
        o_ref[...] = jnp.concatenate(outs, axis=1)

    return pl.pallas_call(
        body, name="swa_fwd", grid=(S // WINDOW,),
        in_specs=_swa_specs(),
        out_specs=pl.BlockSpec((WINDOW, 512), lambda n: (n, 0)),
        out_shape=jax.ShapeDtypeStruct((S, 512), F32),
        compiler_params=_params(("parallel",)),
    )(proj, proj, proj, proj, proj, bias, sinks)


def _swa_bwd(proj, bias, sinks, do, bucket):
    S = proj.shape[0]
    nb = S // WINDOW

    def body(q_ref, kc_ref, kp_ref, vc_ref, vp_ref, bias_ref, sink_ref, do_ref, bk_ref,
             dq_ref, dk_ref, dv_ref, drb_ref, dsk_ref, dbias_acc):
        n = pl.program_id(0)

        @pl.when(n == 0)
        def _():
            dk_ref[...] = jnp.zeros_like(dk_ref)
            dv_ref[...] = jnp.zeros_like(dv_ref)
            dsk_ref[...] = jnp.zeros_like(dsk_ref)
            dbias_acc[...] = jnp.zeros_like(dbias_acc)
            drb_ref[...] = jnp.zeros_like(drb_ref)

        valid = _swa_valid(n)
        q = q_ref[...].astype(BF16)
        dov = do_ref[...]
        kfull = jnp.concatenate([kp_ref[...], kc_ref[...]], axis=0).astype(BF16)
        vfull = jnp.concatenate([vp_ref[...], vc_ref[...]], axis=0).astype(BF16)
        prow = pl.ds(pl.multiple_of(jnp.maximum(n - 1, 0) * WINDOW, WINDOW), WINDOW)
        crow = pl.ds(pl.multiple_of(n * WINDOW, WINDOW), WINDOW)
        dqs = []
        for g in range(2):
            kk = kfull[:, 64 * g:64 * g + 64]
            vv = vfull[:, 64 * g:64 * g + 64]
            dkk = jnp.zeros((2 * WINDOW, SWA_DH), F32)
            dvv = jnp.zeros((2 * WINDOW, SWA_DH), F32)
            for h in range(4 * g, 4 * g + 4):
                qh = q[:, 64 * h:64 * h + 64]
                pk, psink = _swa_probs(qh, kk, bias_ref[h], sink_ref[0, h], valid)
                pkb = pk.astype(BF16)
                o = _dot(pkb, vv)
                doh = dov[:, 64 * h:64 * h + 64]
                dob = doh.astype(BF16)
                dp = _dot_nt(dob, vv)
                delta = jnp.sum(doh * o, axis=-1, keepdims=True)
                ds = pk * (dp - delta)
                dsk_ref[h:h + 1, :] += jnp.broadcast_to(jnp.sum(-psink * delta, keepdims=True), (1, 128))
                dbias_acc[h] += ds
                dsb = (ds * SWA_SCALE).astype(BF16)
                dqs.append(_dot(dsb, kk))
                dkk += _dot_tn(dsb, qh)
                dvv += _dot_tn(pkb, dob)
            dk_ref[prow, 64 * g:64 * g + 64] += dkk[:WINDOW]
            dk_ref[crow, 64 * g:64 * g + 64] += dkk[WINDOW:]
            dv_ref[prow, 64 * g:64 * g + 64] += dvv[:WINDOW]
            dv_ref[crow, 64 * g:64 * g + 64] += dvv[WINDOW:]
        dq_ref[...] = jnp.concatenate(dqs, axis=1)

        @pl.when(n == nb - 1)
        def _():
            bk = bk_ref[...]
            for h in range(SWA_HEADS):
                dbh = dbias_acc[h]
                for b in range(NUM_BUCKETS):
                    val = jnp.sum(jnp.where(bk == b, dbh, 0.0), keepdims=True)
                    drb_ref[b * 8 + h:b * 8 + h + 1, :] = jnp.broadcast_to(val, (1, 128))

    full = lambda shape: pl.BlockSpec(shape, lambda n: tuple(0 for _ in shape))
    return pl.pallas_call(
        body, name="swa_bwd", grid=(nb,),
        in_specs=_swa_specs() + [pl.BlockSpec((WINDOW, 512), lambda n: (n, 0)), full((WINDOW, 2 * WINDOW))],
        out_specs=[pl.BlockSpec((WINDOW, 512), lambda n: (n, 0)), full((S, 128)), full((S, 128)),
                   full((NUM_BUCKETS * 8, 128)), full((8, 128))],
        out_shape=[jax.ShapeDtypeStruct((S, 512), F32), jax.ShapeDtypeStruct((S, 128), F32),
                   jax.ShapeDtypeStruct((S, 128), F32), jax.ShapeDtypeStruct((NUM_BUCKETS * 8, 128), F32),
                   jax.ShapeDtypeStruct((8, 128), F32)],
        scratch_shapes=[pltpu.VMEM((SWA_HEADS, WINDOW, 2 * WINDOW), F32)],
        compiler_params=_params(("arbitrary",)),
    )(proj, proj, proj, proj, proj, bias, sinks, do, bucket)


def _rope_tables(S):
    inv = ROPE_THETA ** (-jnp.arange(0, MLA_ROPE, 2, dtype=F32) / MLA_ROPE)
    ang = jnp.arange(S, dtype=F32)[:, None] * inv[None, :]
    cos, sin = jnp.cos(ang), jnp.sin(ang)
    return jnp.tile(jnp.concatenate([cos, cos], axis=1), (1, 4)), jnp.tile(jnp.concatenate([-sin, sin], axis=1), (1, 4))


def _swap_halves(x):
    w = x.shape[-1]
    lane = lax.broadcasted_iota(jnp.int32, x.shape, x.ndim - 1)
    return jnp.where((lane % 64) < 32, pltpu.roll(x, w - 32, x.ndim - 1), pltpu.roll(x, 32, x.ndim - 1))


def _mla_pre_fwd(proj, qn_w, kvn_w, wuqT, wukv, cos, sin, *, tm=256):
    S = proj.shape[0]

    def body(ql_ref, kl_ref, kr_ref, qw_ref, kw_ref, wuq_ref, wukv_ref, cos_ref, sin_ref,
             qc_ref, kc_ref, vv_ref):
        ql = ql_ref[...]
        qn = (ql * _rstd(ql) * qw_ref[...]).astype(BF16)
        q = _dot_nt(qn, wuq_ref[...])
        cs, sn = cos_ref[...], sin_ref[...]
        qr = q[:, 512:768]
        qr = qr * cs + _swap_halves(qr) * sn
        half = lax.broadcasted_iota(jnp.int32, (tm, 128), 1) // 64
        kl = kl_ref[...]
        kvn = (kl * _rstd(kl) * kw_ref[...]).astype(BF16)
        kr = kr_ref[...]
        kr = kr * cs[:, :128] + _swap_halves(kr) * sn[:, :128]
        kr2 = (kr + pltpu.roll(kr, 64, 1)).astype(BF16)
        for h in range(MLA_HEADS):
            qc_ref[h, :, 0:128] = q[:, 128 * h:128 * h + 128].astype(BF16)
            chunk = qr[:, 128 * (h // 2):128 * (h // 2) + 128]
            qc_ref[h, :, 128:256] = jnp.where(half == (h % 2), chunk, 0.0).astype(BF16)
            kc_ref[h, :, 0:128] = _dot(kvn, wukv_ref[2 * h]).astype(BF16)
            kc_ref[h, :, 128:256] = kr2
            vv_ref[h] = _dot(kvn, wukv_ref[2 * h + 1]).astype(BF16)

    const = lambda shape: pl.BlockSpec(shape, lambda i: tuple(0 for _ in shape))
    return pl.pallas_call(
        body, name="mla_pre_fwd", grid=(S // tm,),
        in_specs=[pl.BlockSpec((tm, 256), lambda i: (i, 3)), pl.BlockSpec((tm, 128), lambda i: (i, 8)),
                  pl.BlockSpec((tm, 128), lambda i: (i, 9)), const((1, 256)), const((1, 128)),
                  const((768, 256)), const((8, 128, 128)),
                  pl.BlockSpec((tm, 256), lambda i: (i, 0)), pl.BlockSpec((tm, 256), lambda i: (i, 0))],
        out_specs=[pl.BlockSpec((MLA_HEADS, tm, 256), lambda i: (0, i, 0)),
                   pl.BlockSpec((MLA_HEADS, tm, 256), lambda i: (0, i, 0)),
                   pl.BlockSpec((MLA_HEADS, tm, 128), lambda i: (0, i, 0))],
        out_shape=[jax.ShapeDtypeStruct((MLA_HEADS, S, 256), BF16), jax.ShapeDtypeStruct((MLA_HEADS, S, 256), BF16),
                   jax.ShapeDtypeStruct((MLA_HEADS, S, 128), BF16)],
        compiler_params=_params(("parallel",)),
    )(proj, proj, proj, qn_w, kvn_w, wuqT, wukv, cos, sin)


def _causal(i, j, t):
    row = i * t + lax.broadcasted_iota(jnp.int32, (t, t), 0)
    col = j * t + lax.broadcasted_iota(jnp.int32, (t, t), 1)
    return col <= row


def _mla_attn_fwd(qc, kc, vv, *, t=256):
    S = qc.shape[1]

    def body(q_ref, k_ref, v_ref, o_ref, l_ref):
        i = pl.program_id(1)
        q = q_ref[0]

        def step(j, carry):
            m, l, acc = carry
            rows = pl.ds(pl.multiple_of(j * t, t), t)
            s = _dot_nt(q, k_ref[0, rows, :]) * MLA_SCALE
            s = jnp.where(_causal(i, j, t), s, -jnp.inf)
            m_new = jnp.maximum(m, jnp.max(s, axis=-1, keepdims=True))
            alpha = jnp.exp(m - m_new)
            p = jnp.exp(s - m_new)
            l = alpha * l + jnp.sum(p, axis=-1, keepdims=True)
            acc = alpha * acc + _dot(p.astype(BF16), v_ref[0, rows, :])
            return m_new, l, acc

        init = (jnp.full((t, 1), -jnp.inf, F32), jnp.zeros((t, 1), F32), jnp.zeros((t, MLA_V), F32))
        m, l, acc = lax.fori_loop(0, i + 1, step, init)
        o_ref[...] = acc / l
        l_ref[0] = jnp.broadcast_to(m + jnp.log(l), (t, 128))

    return pl.pallas_call(
        body, name="mla_attn_fwd", grid=(MLA_HEADS, S // t),
        in_specs=[pl.BlockSpec((1, t, 256), lambda h, i: (h, i, 0)),
                  pl.BlockSpec((1, S, 256), lambda h, i: (h, 0, 0)),
                  pl.BlockSpec((1, S, 128), lambda h, i: (h, 0, 0))],
        out_specs=[pl.BlockSpec((t, 128), lambda h, i: (i, h)),
                   pl.BlockSpec((1, t, 128), lambda h, i: (h, i, 0))],
        out_shape=[jax.ShapeDtypeStruct((S, 512), F32), jax.ShapeDtypeStruct((MLA_HEADS, S, 128), F32)],
        compiler_params=_params(("parallel", "parallel")),
    )(qc, kc, vv)


def _mla_attn_bwd(qc, kc, vv, o, lse, do, *, t=256):
    S = qc.shape[1]
    nblk = S // t

    def body(q_ref, k_ref, v_ref, o_ref, l_ref, do_ref, dq_ref, dk_ref, dv_ref):
        j = pl.program_id(1)

        @pl.when(j == 0)
        def _():
            dq_ref[...] = jnp.zeros_like(dq_ref)

        k = k_ref[0]
        v = v_ref[0]

        def step(i, carry):
            dk, dv = carry
            rows = pl.ds(pl.multiple_of(i * t, t), t)
            q = q_ref[0, rows, :]
            dov = do_ref[rows, :]
            lrow = l_ref[0, rows, :][:, 0:1]
            s = _dot_nt(q, k) * MLA_SCALE
            p = jnp.where(_causal(i, j, t), jnp.exp(s - lrow), 0.0)
            dob = dov.astype(BF16)
            dv = dv + _dot_tn(p.astype(BF16), dob)
            dp = _dot_nt(dob, v)
            delta = jnp.sum(dov * o_ref[rows, :], axis=-1, keepdims=True)
            ds = (p * (dp - delta) * MLA_SCALE).astype(BF16)
            dk = dk + _dot_tn(ds, q)
            dq_ref[0, rows, :] += _dot(ds, k)
            return dk, dv

        dk, dv = lax.fori_loop(j, nblk, step, (jnp.zeros((t, 256), F32), jnp.zeros((t, MLA_V), F32)))
        dk_ref[0] = dk
        dv_ref[0] = dv

    return pl.pallas_call(
        body, name="mla_attn_bwd", grid=(MLA_HEADS, nblk),
        in_specs=[pl.BlockSpec((1, S, 256), lambda h, j: (h, 0, 0)),
                  pl.BlockSpec((1, t, 256), lambda h, j: (h, j, 0)),
                  pl.BlockSpec((1, t, 128), lambda h, j: (h, j, 0)),
                  pl.BlockSpec((S, 128), lambda h, j: (0, h)),
                  pl.BlockSpec((1, S, 128), lambda h, j: (h, 0, 0)),
                  pl.BlockSpec((S, 128), lambda h, j: (0, h))],
        out_specs=[pl.BlockSpec((1, S, 256), lambda h, j: (h, 0, 0)),
                   pl.BlockSpec((1, t, 256), lambda h, j: (h, j, 0)),
                   pl.BlockSpec((1, t, 128), lambda h, j: (h, j, 0))],
        out_shape=[jax.ShapeDtypeStruct((MLA_HEADS, S, 256), F32), jax.ShapeDtypeStruct((MLA_HEADS, S, 256), F32),
                   jax.ShapeDtypeStruct((MLA_HEADS, S, 128), F32)],
        compiler_params=_params(("parallel", "arbitrary")),
    )(qc, kc, vv, o, lse, do)


def _mla_pre_bwd(proj, qn_w, kvn_w, wuqT, wukv, cos, sin, dqc, dkc, dvv, *, tm=256):
    S = proj.shape[0]

    def body(ql_ref, kl_ref, qw_ref, kw_ref, wuq_ref, wukv_ref, cos_ref, sin_ref, dqc_ref, dkc_ref, dvv_ref,
             dql_ref, dkl_ref, dkr_ref, gq_ref, gkv_ref, part_ref):
        @pl.when(pl.program_id(0) == 0)
        def _():
            gq_ref[...] = jnp.zeros_like(gq_ref)
            gkv_ref[...] = jnp.zeros_like(gkv_ref)
            part_ref[...] = jnp.zeros_like(part_ref)

        cs, sn = cos_ref[...], sin_ref[...]
        half = lax.broadcasted_iota(jnp.int32, (tm, 128), 1) // 64
        ql = ql_ref[...]
        rq = _rstd(ql)
        qhat = ql * rq
        qw = qw_ref[...]
        qn = (qhat * qw).astype(BF16)
        chunks = []
        for pair in range(2):
            chunks.append(jnp.where(half == 0, dqc_ref[2 * pair, :, 128:256], dqc_ref[2 * pair + 1, :, 128:256]))
        dqr = jnp.concatenate(chunks, axis=1)
        dqr = dqr * cs + _swap_halves(dqr * sn)
        dq = jnp.concatenate([dqc_ref[h, :, 0:128] for h in range(MLA_HEADS)] + [dqr], axis=1).astype(BF16)
        gq_ref[...] += _dot_tn(dq, qn)
        dqn = _dot(dq, wuq_ref[...])
        part_ref[0:1, :] += jnp.sum(dqn * qhat, axis=0, keepdims=True)
        dql_ref[...] = _rms_bwd(dqn * qw, qhat, rq)
        kl = kl_ref[...]
        rk = _rstd(kl)
        khat = kl * rk
        kw = kw_ref[...]
        kvn = (khat * kw).astype(BF16)
        dkvn = jnp.zeros((tm, MLA_KVR), F32)
        dkr2 = jnp.zeros((tm, 128), F32)
        for h in range(MLA_HEADS):
            dkn = dkc_ref[h, :, 0:128].astype(BF16)
            dvh = dvv_ref[h].astype(BF16)
            gkv_ref[2 * h] += _dot_tn(kvn, dkn)
            gkv_ref[2 * h + 1] += _dot_tn(kvn, dvh)
            dkvn += _dot_nt(dkn, wukv_ref[2 * h]) + _dot_nt(dvh, wukv_ref[2 * h + 1])
            dkr2 += dkc_ref[h, :, 128:256]
        part_ref[1:2, 0:128] += jnp.sum(dkvn * khat, axis=0, keepdims=True)
        dkl_ref[...] = _rms_bwd(dkvn * kw, khat, rk)
        dkr = jnp.where(half == 0, dkr2 + pltpu.roll(dkr2, 64, 1), 0.0)
        dkr_ref[...] = dkr * cs[:, :128] + _swap_halves(dkr * sn[:, :128])

    const = lambda shape: pl.BlockSpec(shape, lambda i: tuple(0 for _ in shape))
    heads = lambda w: pl.BlockSpec((MLA_HEADS, tm, w), lambda i: (0, i, 0))
    return pl.pallas_call(
        body, name="mla_pre_bwd", grid=(S // tm,),
        in_specs=[pl.BlockSpec((tm, 256), lambda i: (i, 3)), pl.BlockSpec((tm, 128), lambda i: (i, 8)),
                  const((1, 256)), const((1, 128)), const((768, 256)), const((8, 128, 128)),
                  pl.BlockSpec((tm, 256), lambda i: (i, 0)), pl.BlockSpec((tm, 256), lambda i: (i, 0)),
                  heads(256), heads(256), heads(128)],
        out_specs=[pl.BlockSpec((tm, 256), lambda i: (i, 0)), pl.BlockSpec((tm, 128), lambda i: (i, 0)),
                   pl.BlockSpec((tm, 128), lambda i: (i, 0)), const((768, 256)), const((8, 128, 128)), const((8, 256))],
        out_shape=[jax.ShapeDtypeStruct((S, 256), F32), jax.ShapeDtypeStruct((S, 128), F32),
                   jax.ShapeDtypeStruct((S, 128), F32), jax.ShapeDtypeStruct((768, 256), F32),
                   jax.ShapeDtypeStruct((8, 128, 128), F32), jax.ShapeDtypeStruct((8, 256), F32)],
        compiler_params=_params(("arbitrary",)),
    )(proj, proj, qn_w, kvn_w, wuqT, wukv, cos, sin, dqc, dkc, dvv)


def _mix_out_fwd(x, oa, ob, w_o, vecs, *, tm=256):
    S = x.shape[0]

    def body(x_ref, oa_ref, ob_ref, w_ref, vec_ref, xo_ref, mo_ref):
        mo = _dot(oa_ref[...].astype(BF16), w_ref[0:512, :]) + _dot(ob_ref[...].astype(BF16), w_ref[512:1024, :])
        mo_ref[...] = mo
        xo_ref[...] = x_ref[...] + vec_ref[3:4, :] * mo

    row = pl.BlockSpec((tm, D), lambda i: (i, 0))
    half = pl.BlockSpec((tm, 512), lambda i: (i, 0))
    return pl.pallas_call(
        body, name="mix_out_fwd", grid=(S // tm,),
        in_specs=[row, half, half, pl.BlockSpec((D, D), lambda i: (0, 0)), pl.BlockSpec((8, D), lambda i: (0, 0))],
        out_specs=[row, row],
        out_shape=[jax.ShapeDtypeStruct((S, D), F32), jax.ShapeDtypeStruct((S, D), F32)],
        compiler_params=_params(("parallel",)),
    )(x, oa, ob, w_o, vecs)


def _mix_out_bwd(dxo, mo, oa, ob, w_o, vecs, *, tm=256):
    S = dxo.shape[0]

    def body(dx_ref, mo_ref, oa_ref, ob_ref, w_ref, vec_ref, doa_ref, dob_ref, gw_ref, part_ref):
        @pl.when(pl.program_id(0) == 0)
        def _():
            gw_ref[...] = jnp.zeros_like(gw_ref)
            part_ref[...] = jnp.zeros_like(part_ref)

        dx = dx_ref[...]
        part_ref[0:1, :] += jnp.sum(dx * mo_ref[...], axis=0, keepdims=True)
        dmo = (vec_ref[3:4, :] * dx).astype(BF16)
        doa_ref[...] = _dot_nt(dmo, w_ref[0:512, :])
        dob_ref[...] = _dot_nt(dmo, w_ref[512:1024, :])
        gw_ref[0:512, :] += _dot_tn(oa_ref[...].astype(BF16), dmo)
        gw_ref[512:1024, :] += _dot_tn(ob_ref[...].astype(BF16), dmo)

    row = pl.BlockSpec((tm, D), lambda i: (i, 0))
    half = pl.BlockSpec((tm, 512), lambda i: (i, 0))
    return pl.pallas_call(
        body, name="mix_out_bwd", grid=(S // tm,),
        in_specs=[row, row, half, half, pl.BlockSpec((D, D), lambda i: (0, 0)), pl.BlockSpec((8, D), lambda i: (0, 0))],
        out_specs=[half, half, pl.BlockSpec((D, D), lambda i: (0, 0)), pl.BlockSpec((8, D), lambda i: (0, 0))],
        out_shape=[jax.ShapeDtypeStruct((S, 512), F32), jax.ShapeDtypeStruct((S, 512), F32),
                   jax.ShapeDtypeStruct((D, D), F32), jax.ShapeDtypeStruct((8, D), F32)],
        compiler_params=_params(("arbitrary",)),
    )(dxo, mo, oa, ob, w_o, vecs)


def _mix_in_bwd(h, w_inT, dq, dk, dv, dql, dkl, dkr, *, tm=256):
    S = h.shape[0]
    offs = (0, 512, 640, 768, 1024, 1152)
    wid = (512, 128, 128, 256, 128, 128)

    def body(h_ref, w_ref, dq_ref, dk_ref, dv_ref, dql_ref, dkl_ref, dkr_ref, dh_ref, gw_ref):
        @pl.when(pl.program_id(0) == 0)
        def _():
            gw_ref[...] = jnp.zeros_like(gw_ref)

        hv = h_ref[...]
        dh = jnp.zeros((tm, D), F32)
        for ref, o, w in zip((dq_ref, dk_ref, dv_ref, dql_ref, dkl_ref, dkr_ref), offs, wid):
            dpart = ref[...].astype(BF16)
            dh += _dot(dpart, w_ref[o:o + w, :])
            gw_ref[o:o + w, :] += _dot_tn(dpart, hv)
        dh_ref[...] = dh

    row = pl.BlockSpec((tm, D), lambda i: (i, 0))
    part = lambda w: pl.BlockSpec((tm, w), lambda i: (i, 0))
    return pl.pallas_call(
        body, name="mix_in_bwd", grid=(S // tm,),
        in_specs=[row, pl.BlockSpec((D_IN_PAD, D), lambda i: (0, 0))] + [part(w) for w in wid],
        out_specs=[row, pl.BlockSpec((D_IN_PAD, D), lambda i: (0, 0))],
        out_shape=[jax.ShapeDtypeStruct((S, D), F32), jax.ShapeDtypeStruct((D_IN_PAD, D), F32)],
        compiler_params=_params(("arbitrary",)),
    )(h, w_inT, dq, dk, dv, dql, dkl, dkr)


def _vecs(norm_w, mod9, k):
    return jnp.concatenate([norm_w.reshape(1, D), mod9[3 * k:3 * k + 3], jnp.zeros((4, D), F32)], axis=0)


def _uq_group_rows(wuqT):
    per = MLA_NOPE + MLA_ROPE
    nope = [wuqT[per * h:per * h + MLA_NOPE] for h in range(MLA_HEADS)]
    rope = [wuqT[per * h + MLA_NOPE:per * (h + 1)] for h in range(MLA_HEADS)]
    return jnp.concatenate(nope + rope, axis=0)


def _uq_ungroup_rows(g):
    parts = []
    for h in range(MLA_HEADS):
        parts += [g[MLA_NOPE * h:MLA_NOPE * (h + 1)], g[512 + MLA_ROPE * h:512 + MLA_ROPE * (h + 1)]]
    return jnp.concatenate(parts, axis=0)


def _local_step(x, tgt, mod9, norms, sinks, rel_bias, q_norm, kv_norm, W):
    S = x.shape[0]
    v1 = _vecs(norms["ffn1"], mod9, 0)
    v2 = _vecs(norms["mix"], mod9, 1)
    v3 = _vecs(norms["ffn2"], mod9, 2)
    bucket = jnp.asarray(_bucket_table())
    cos, sin = _rope_tables(S)
    w_inT = jnp.pad(W["w_inT"], ((0, D_IN_PAD - D_IN), (0, 0))).astype(BF16)
    wuqT = _uq_group_rows(W["w_uqT"])

    x1, h1, a1, b1, f1 = _ffn_fwd(x, v1, W["g1T"], W["u1T"], W["d1"], name="ffn1_fwd")
    h2, proj = _mix_in_fwd(x1, v2, w_inT)
    bias = _bias_build(rel_bias, bucket)
    oa = _swa_fwd(proj, bias, sinks)
    qc, kc, vv = _mla_pre_fwd(proj, q_norm, kv_norm, wuqT, W["w_ukv"], cos, sin)
    ob, lse = _mla_attn_fwd(qc, kc, vv)
    x2, mo = _mix_out_fwd(x1, oa, ob, W["w_o"], v2)
    x3, h3, a3, b3, f3 = _ffn_fwd(x2, v3, W["g3T"], W["u3T"], W["d3"], name="ffn2_fwd")
    dx3, head_part = _head(x3, tgt, norms["final"])

    df3, g3_part = _ffn_bwd_pre(dx3, f3, v3, name="ffn2_bwd_pre")
    gg3, gu3, gd3, dh3 = _ffn_bwd_main(h3, df3, a3, b3, W["g3T"], W["u3T"], W["d3"], name="ffn2_bwd")
    dx2, n3_part = _norm_bwd(dh3, x2, dx3, v3, name="ffn2_norm_bwd")
    doa, dob, g_wo, g2_part = _mix_out_bwd(dx2, mo, oa, ob, W["w_o"], v2)
    dq, dk, dv, drb, dsk = _swa_bwd(proj, bias, sinks, doa, bucket)
    dqc, dkc, dvv = _mla_attn_bwd(qc, kc, vv, ob, lse, dob)
    dql, dkl, dkr, g_uq, g_ukv, mla_part = _mla_pre_bwd(proj, q_norm, kv_norm, wuqT, W["w_ukv"], cos, sin, dqc, dkc, dvv)
    dh2, g_win = _mix_in_bwd(h2, w_inT, dq, dk, dv, dql, dkl, dkr)
    dx1, n2_part = _norm_bwd(dh2, x1, dx2, v2, name="mix_norm_bwd")
    df1, g1_part = _ffn_bwd_pre(dx1, f1, v1, name="ffn1_bwd_pre")
    gg1, gu1, gd1, dh1 = _ffn_bwd_main(h1, df1, a1, b1, W["g1T"], W["u1T"], W["d1"], name="ffn1_bwd")
    dx0, n1_part = _norm_bwd(dh1, x, dx1, v1, name="ffn1_norm_bwd")

    grads = {"g1T": gg1, "u1T": gu1, "d1": gd1, "g3T": gg3, "u3T": gu3, "d3": gd3,
             "w_inT": g_win[:D_IN], "w_uqT": _uq_ungroup_rows(g_uq).astype(BF16),
             "w_ukv": g_ukv.astype(BF16), "w_o": g_wo.astype(BF16)}
    dmod9 = jnp.concatenate([n1_part[1:3], g1_part[0:1], n2_part[1:3], g2_part[0:1],
                             n3_part[1:3], g3_part[0:1]], axis=0)
    small = jnp.concatenate([n1_part[0], n2_part[0], n3_part[0], head_part[0], mla_part[0],
                             mla_part[1, :128], jnp.pad(dsk[:, 0], (0, 120)), drb[:, 0]])
    return head_part[1, 0], dx0, grads, small, dmod9


SMALL_LAYOUT = (("norm_ffn1", 1024), ("norm_mix", 1024), ("norm_ffn2", 1024), ("norm_final", 1024),
                ("q_norm", 256), ("kv_norm", 128), ("sinks", 128), ("rel_bias", 256))
N_SMALL = sum(n for _, n in SMALL_LAYOUT)


def _coords():
    return lax.axis_index("x"), lax.axis_index("y"), lax.axis_index("c")


def _flip(v, bit):
    return 1 - v if bit else v


def _peer(r):
    x, y, c = _coords()
    return (_flip(x, r & 4), _flip(y, r & 2), _flip(c, r & 1))


def _mod_fwd(c_tile, w_mod, b_mod3):
    W = w_mod.shape[1]

    def body(c_ref, w_ref, b_ref, mod_ref, ca_ref, call_ref, part_ref, send_sems, recv_sems):
        x, y, c = _coords()
        me = 4 * x + 2 * y + c
        call_ref[me] = c_ref[...]
        sends = []
        for r in range(1, N_DEV):
            cp = pltpu.make_async_remote_copy(c_ref, call_ref.at[me], send_sems.at[0, r], recv_sems.at[0, r],
                                              device_id=_peer(r), device_id_type=MESH)
            cp.start()
            sends.append(cp)
        for r in range(1, N_DEV):
            pltpu.make_async_remote_copy(c_ref, call_ref.at[me], send_sems.at[0, r], recv_sems.at[0, r],
                                         device_id=_peer(r), device_id_type=MESH).wait_recv()
        cv = call_ref[...].reshape(8 * N_DEV, D)
        ca = (cv * _sigmoid(cv)).astype(BF16)
        ca_ref[...] = ca
        part_ref[...] = _dot(ca, w_ref[...].astype(BF16)).reshape(N_DEV, 8, W)
        mod_ref[me] = part_ref[me] + b_ref[me]
        for r in range(1, N_DEV):
            cp = pltpu.make_async_remote_copy(part_ref.at[me ^ r], mod_ref.at[me], send_sems.at[1, r],
                                              recv_sems.at[1, r], device_id=_peer(r), device_id_type=MESH)
            cp.start()
            sends.append(cp)
        for r in range(1, N_DEV):
            pltpu.make_async_remote_copy(part_ref.at[me ^ r], mod_ref.at[me], send_sems.at[1, r],
                                         recv_sems.at[1, r], device_id=_peer(r), device_id_type=MESH).wait_recv()
            mod_ref[me ^ r] = mod_ref[me ^ r] + b_ref[me ^ r]
        for cp in sends:
            cp.wait_send()

    vm = pl.BlockSpec(memory_space=pltpu.VMEM)
    return pl.pallas_call(
        body, name="mod_fwd", in_specs=[vm, vm, vm], out_specs=[vm, vm],
        out_shape=[jax.ShapeDtypeStruct((N_DEV, 8, W), F32), jax.ShapeDtypeStruct((8 * N_DEV, D), BF16)],
        scratch_shapes=[pltpu.VMEM((N_DEV, 8, D), F32), pltpu.VMEM((N_DEV, 8, W), F32),
                        pltpu.SemaphoreType.DMA((2, N_DEV)), pltpu.SemaphoreType.DMA((2, N_DEV))],
        compiler_params=_params(),
    )(c_tile, w_mod, b_mod3)


def _mod_bwd(dmod3, small_tile, ca):
    W = dmod3.shape[2]

    def body(dm_ref, sm_ref, ca_ref, gw_ref, gb_ref, ssum_ref, dmcols, sm_all, gb_mine, send_sems, recv_sems):
        x, y, c = _coords()
        me = 4 * x + 2 * y + c
        dmcols[me] = dm_ref[me]
        sm_all[me] = sm_ref[...]
        sends = []
        for r in range(1, N_DEV):
            cp = pltpu.make_async_remote_copy(dm_ref.at[me ^ r], dmcols.at[me], send_sems.at[0, r], recv_sems.at[0, r],
                                              device_id=_peer(r), device_id_type=MESH)
            cp.start()
            sends.append(cp)
            cp = pltpu.make_async_remote_copy(sm_ref, sm_all.at[me], send_sems.at[1, r], recv_sems.at[1, r],
                                              device_id=_peer(r), device_id_type=MESH)
            cp.start()
            sends.append(cp)
        for r in range(1, N_DEV):
            pltpu.make_async_remote_copy(dm_ref.at[me ^ r], dmcols.at[me], send_sems.at[0, r], recv_sems.at[0, r],
                                         device_id=_peer(r), device_id_type=MESH).wait_recv()
            pltpu.make_async_remote_copy(sm_ref, sm_all.at[me], send_sems.at[1, r], recv_sems.at[1, r],
                                         device_id=_peer(r), device_id_type=MESH).wait_recv()
        dm = dmcols[...].reshape(8 * N_DEV, W)
        gw_ref[...] = _dot_tn(ca_ref[...], dm.astype(BF16))
        first_row = lax.broadcasted_iota(jnp.int32, (8, W), 0) == 0
        gb_mine[...] = jnp.where(first_row, jnp.sum(dm, axis=0, keepdims=True), 0.0)
        gb_ref[me] = gb_mine[...]
        for r in range(1, N_DEV):
            cp = pltpu.make_async_remote_copy(gb_mine, gb_ref.at[me], send_sems.at[2, r], recv_sems.at[2, r],
                                              device_id=_peer(r), device_id_type=MESH)
            cp.start()
            sends.append(cp)
        total = sm_all[0]
        for k in range(1, N_DEV):
            total = total + sm_all[k]
        ssum_ref[...] = total
        for r in range(1, N_DEV):
            pltpu.make_async_remote_copy(gb_mine, gb_ref.at[me], send_sems.at[2, r], recv_sems.at[2, r],
                                         device_id=_peer(r), device_id_type=MESH).wait_recv()
        for cp in sends:
            cp.wait_send()

    vm = pl.BlockSpec(memory_space=pltpu.VMEM)
    return pl.pallas_call(
        body, name="mod_bwd", in_specs=[vm, vm, vm], out_specs=[vm, vm, vm],
        out_shape=[jax.ShapeDtypeStruct((D, W), F32), jax.ShapeDtypeStruct((N_DEV, 8, W), F32),
                   jax.ShapeDtypeStruct((8, N_SMALL), F32)],
        scratch_shapes=[pltpu.VMEM((N_DEV, 8, W), F32), pltpu.VMEM((N_DEV, 8, N_SMALL), F32), pltpu.VMEM((8, W), F32),
                        pltpu.SemaphoreType.DMA((3, N_DEV)), pltpu.SemaphoreType.DMA((3, N_DEV))],
        compiler_params=_params(),
    )(dmod3, small_tile, ca)


def _wgather(shards):
    n = len(shards)

    def body(*refs):
        ins, outs = refs[:n], refs[n:2 * n]
        send_sems, recv_sems, local_sems = refs[2 * n:]
        x, y, c = _coords()
        me = 4 * x + 2 * y + c
        sib = (x, y, 1 - c)
        chips = [(1 - x, y), (x, 1 - y), (1 - x, 1 - y)]

        def copy(k, slot, block, to, src=None):
            return pltpu.make_async_remote_copy(
                src_ref=outs[k].at[block] if src is None else src, dst_ref=outs[k].at[block],
                send_sem=send_sems.at[k, slot], recv_sem=recv_sems.at[k, slot], device_id=to, device_id_type=MESH)

        local = [pltpu.make_async_copy(ins[k], outs[k].at[me], local_sems.at[k]) for k in range(n)]
        for cp in local:
            cp.start()
        first = []
        for k in range(n):
            first.append(copy(k, 0, me, sib, src=ins[k]))
            for j, chip in enumerate(chips):
                first.append(copy(k, 1 + j, me, (*chip, c), src=ins[k]))
        for cp in first:
            cp.start()
        passed = []
        for j, (cx, cy) in enumerate(chips):
            for k in range(n):
                blk = 4 * cx + 2 * cy + c
                copy(k, 1 + j, blk, sib).wait_recv()
                cp = copy(k, 4 + j, blk, sib)
                cp.start()
                passed.append(cp)
        for k in range(n):
            copy(k, 0, 4 * x + 2 * y + (1 - c), sib).wait_recv()
            for j, (cx, cy) in enumerate(chips):
                copy(k, 4 + j, 4 * cx + 2 * cy + (1 - c), sib).wait_recv()
        for cp in first + passed:
            cp.wait_send()
        for cp in local:
            cp.wait()

    anyspec = pl.BlockSpec(memory_space=pl.ANY)
    return pl.pallas_call(
        body, name="wgather", in_specs=[anyspec] * n, out_specs=[anyspec] * n,
        out_shape=[jax.ShapeDtypeStruct((N_DEV,) + s.shape, s.dtype) for s in shards],
        scratch_shapes=[pltpu.SemaphoreType.DMA((n, 7)), pltpu.SemaphoreType.DMA((n, 7)),
                        pltpu.SemaphoreType.DMA((n,))],
    )(*shards)


def _rs_d2d(grads):
    n = len(grads)

    def body(*refs):
        ins, outs = refs[:n], refs[n:2 * n]
        send_sems, recv_sems = refs[2 * n:]
        x, y, c = _coords()
        sib = (x, y, 1 - c)
        cps = []
        for k in range(n):
            for q in range(4):
                cps.append(pltpu.make_async_remote_copy(
                    src_ref=ins[k].at[2 * q + (1 - c)], dst_ref=outs[k].at[q],
                    send_sem=send_sems.at[k, q], recv_sem=recv_sems.at[k, q], device_id=sib, device_id_type=MESH))
        for cp in cps:
            cp.start()
        for cp in cps:
            cp.wait_recv()
        for cp in cps:
            cp.wait_send()

    anyspec = pl.BlockSpec(memory_space=pl.ANY)
    return pl.pallas_call(
        body, name="rs_d2d", in_specs=[anyspec] * n, out_specs=[anyspec] * n,
        out_shape=[jax.ShapeDtypeStruct((4,) + g.shape[1:], g.dtype) for g in grads],
        scratch_shapes=[pltpu.SemaphoreType.DMA((n, 4)), pltpu.SemaphoreType.DMA((n, 4))],
    )(*grads)


def _chipsum(g, sib, cidx, *, name):
    _, r, cc = g.shape

    def body(c_ref, g_ref, s_ref, o_ref):
        o_ref[...] = (g_ref[...].astype(F32) + s_ref[...].astype(F32)).astype(o_ref.dtype)

    return pl.pallas_call(
        body, name=name,
        grid_spec=pltpu.PrefetchScalarGridSpec(
            num_scalar_prefetch=1, grid=(4,),
            in_specs=[pl.BlockSpec((1, r, cc), lambda q, c_ref: (2 * q + c_ref[0], 0, 0)),
                      pl.BlockSpec((1, r, cc), lambda q, c_ref: (q, 0, 0))],
            out_specs=pl.BlockSpec((1, r, cc), lambda q, c_ref: (q, 0, 0))),
        out_shape=jax.ShapeDtypeStruct((4, r, cc), g.dtype),
        compiler_params=_params(("arbitrary",)),
    )(cidx, g, sib)


def _rs_ici(sums):
    n = len(sums)

    def body(*refs):
        ins, outs = refs[:n], refs[n:2 * n]
        send_sems, recv_sems = refs[2 * n:]
        x, y, c = _coords()
        chips = [(1 - x, y), (x, 1 - y), (1 - x, 1 - y)]
        cps = []
        for k in range(n):
            for j, (cx, cy) in enumerate(chips):
                cps.append(pltpu.make_async_remote_copy(
                    src_ref=ins[k].at[2 * cx + cy], dst_ref=outs[k].at[j],
                    send_sem=send_sems.at[k, j], recv_sem=recv_sems.at[k, j],
                    device_id=(cx, cy, c), device_id_type=MESH))
        for cp in cps:
            cp.start()
        for cp in cps:
            cp.wait_recv()
        for cp in cps:
            cp.wait_send()

    anyspec = pl.BlockSpec(memory_space=pl.ANY)
    return pl.pallas_call(
        body, name="rs_ici", in_specs=[anyspec] * n, out_specs=[anyspec] * n,
        out_shape=[jax.ShapeDtypeStruct((3,) + s.shape[1:], s.dtype) for s in sums],
        scratch_shapes=[pltpu.SemaphoreType.DMA((n, 3)), pltpu.SemaphoreType.DMA((n, 3))],
    )(*sums)


def _gsum(cs, rcv, qidx, *, name):
    _, r, cc = cs.shape

    def body(q_ref, c_ref, r_ref, o_ref):
        o_ref[...] = ((c_ref[0].astype(F32) + r_ref[0].astype(F32)) + r_ref[1].astype(F32)) + r_ref[2].astype(F32)

    return pl.pallas_call(
        body, name=name,
        grid_spec=pltpu.PrefetchScalarGridSpec(
            num_scalar_prefetch=1, grid=(1,),
            in_specs=[pl.BlockSpec((1, r, cc), lambda i, q_ref: (q_ref[0], 0, 0)),
                      pl.BlockSpec((3, r, cc), lambda i, q_ref: (0, 0, 0))],
            out_specs=pl.BlockSpec((r, cc), lambda i, q_ref: (0, 0))),
        out_shape=jax.ShapeDtypeStruct((r, cc), F32),
        compiler_params=_params(("arbitrary",)),
    )(qidx, cs, rcv)


def _adamw(w, g, m, v, *, name):
    R, C = w.shape
    tr = R if R <= 512 else 256
    c1 = 1.0 / (1.0 - ADAM_B1 ** ADAM_STEP)
    c2 = 1.0 / (1.0 - ADAM_B2 ** ADAM_STEP)

    def body(w_ref, g_ref, m_ref, v_ref, d_ref, nm_ref, nv_ref):
        gv = g_ref[...]
        m2 = ADAM_B1 * m_ref[...] + (1.0 - ADAM_B1) * gv
        v2 = ADAM_B2 * v_ref[...] + (1.0 - ADAM_B2) * (gv * gv)
        nm_ref[...] = m2
        nv_ref[...] = v2
        d_ref[...] = -ADAM_LR * ((m2 * c1) / (jnp.sqrt(v2 * c2) + ADAM_EPS) + ADAM_WD * w_ref[...])

    blk = pl.BlockSpec((tr, C), lambda i: (i, 0))
    return pl.pallas_call(
        body, name=name, grid=(R // tr,), in_specs=[blk] * 4, out_specs=[blk] * 3,
        out_shape=[jax.ShapeDtypeStruct((R, C), F32)] * 3,
        compiler_params=_params(("parallel",)),
    )(w, g, m, v)


BIG = ("g1T", "u1T", "d1", "g3T", "u3T", "d3", "w_inT", "w_uqT", "w_ukv", "w_o")
TRANSPOSED = ("g1T", "u1T", "g3T", "u3T", "w_inT", "w_uqT")


def kernel(x, c, w_mod, b_mod, norm_ffn1, ffn1_gate, ffn1_up, ffn1_down, norm_mix, w_in, q_norm, kv_norm, w_uq, w_ukv, sinks, w_o, norm_ffn2, ffn2_gate, ffn2_up, ffn2_down, rel_bias, norm_final, loss_target, m_w_mod, m_b_mod, m_norm_ffn1, m_ffn1_gate, m_ffn1_up, m_ffn1_down, m_norm_mix, m_w_in, m_q_norm, m_kv_norm, m_w_uq, m_w_ukv, m_sinks, m_w_o, m_norm_ffn2, m_ffn2_gate, m_ffn2_up, m_ffn2_down, m_rel_bias, m_norm_final, v_w_mod, v_b_mod, v_norm_ffn1, v_ffn1_gate, v_ffn1_up, v_ffn1_down, v_norm_mix, v_w_in, v_q_norm, v_kv_norm, v_w_uq, v_w_ukv, v_sinks, v_w_o, v_norm_ffn2, v_ffn2_gate, v_ffn2_up, v_ffn2_down, v_rel_bias, v_norm_final):
    mx, my, mc = _coords()
    cidx = jnp.reshape(mc, (1,)).astype(jnp.int32)
    qidx = jnp.reshape(2 * mx + my, (1,)).astype(jnp.int32)
    WM = w_mod.shape[2]

    c_tile = jnp.pad(c, ((0, 7), (0, 0)))
    b_mod3 = jnp.pad(b_mod.reshape(N_DEV, 1, WM), ((0, 0), (0, 7), (0, 0)))
    mod3, ca = _mod_fwd(c_tile, w_mod[0], b_mod3)
    mod9 = mod3[:, 0, :].reshape(N_MOD, D)

    shards = {"g1T": ffn1_gate[0].T.astype(BF16), "u1T": ffn1_up[0].T.astype(BF16), "d1": ffn1_down[0].astype(BF16),
              "g3T": ffn2_gate[0].T.astype(BF16), "u3T": ffn2_up[0].T.astype(BF16), "d3": ffn2_down[0].astype(BF16),
              "w_inT": w_in[0].T, "w_uqT": w_uq[0].T.astype(BF16), "w_ukv": w_ukv[0].astype(BF16),
              "w_o": w_o[0].astype(BF16)}
    gathered = _wgather([shards[k] for k in BIG])
    W = {}
    for k, gth in zip(BIG, gathered):
        W[k] = gth if k == "w_ukv" else gth.reshape(N_DEV * gth.shape[1], gth.shape[2])

    norms = {"ffn1": norm_ffn1, "mix": norm_mix, "ffn2": norm_ffn2, "final": norm_final.reshape(1, D)}
    loss_local, grad_x, grads, small, dmod9 = _local_step(
        x[0], loss_target[0], mod9, norms, sinks, rel_bias, q_norm, kv_norm, W)
    loss = lax.psum(loss_local, ("x", "y", "c"))

    by_dest = [grads[k] if k == "w_ukv" else grads[k].reshape((N_DEV, grads[k].shape[0] // N_DEV) + grads[k].shape[1:])
               for k in BIG]
    from_sib = _rs_d2d(by_dest)
    chip_sums = [_chipsum(g, s, cidx, name="chipsum_" + k) for k, g, s in zip(BIG, by_dest, from_sib)]
    received = _rs_ici(chip_sums)
    gw = {}
    for k, cs, rc in zip(BIG, chip_sums, received):
        g = _gsum(cs, rc, qidx, name="gsum_" + k)
        gw[k] = g.T if k in TRANSPOSED else g

    dmod3 = jnp.pad(dmod9.reshape(N_DEV, 1, WM), ((0, 0), (0, 7), (0, 0)))
    small_tile = jnp.pad(small.reshape(1, N_SMALL), ((0, 7), (0, 0)))
    g_wmod, gb3, ssum = _mod_bwd(dmod3, small_tile, ca)
    g_small = jnp.concatenate([ssum[0], gb3[:, 0, :].reshape(N_DEV * WM)])

    big = {"w_mod": (w_mod, g_wmod, m_w_mod, v_w_mod),
           "ffn1_gate": (ffn1_gate, gw["g1T"], m_ffn1_gate, v_ffn1_gate),
           "ffn1_up": (ffn1_up, gw["u1T"], m_ffn1_up, v_ffn1_up),
           "ffn1_down": (ffn1_down, gw["d1"], m_ffn1_down, v_ffn1_down),
           "w_in": (w_in, gw["w_inT"], m_w_in, v_w_in),
           "w_uq": (w_uq, gw["w_uqT"], m_w_uq, v_w_uq),
           "w_ukv": (w_ukv, gw["w_ukv"], m_w_ukv, v_w_ukv),
           "w_o": (w_o, gw["w_o"], m_w_o, v_w_o),
           "ffn2_gate": (ffn2_gate, gw["g3T"], m_ffn2_gate, v_ffn2_gate),
           "ffn2_up": (ffn2_up, gw["u3T"], m_ffn2_up, v_ffn2_up),
           "ffn2_down": (ffn2_down, gw["d3"], m_ffn2_down, v_ffn2_down)}
    res = {}
    for k, (wk, gk, mk, vk) in big.items():
        d, nm, nv = _adamw(wk[0], gk, mk[0], vk[0], name="adamw_" + k)
        res[k] = tuple(a[None] for a in (gk, d, nm, nv))

    def pack(parts):
        flat = [parts["norm_ffn1"][0], parts["norm_mix"][0], parts["norm_ffn2"][0], parts["norm_final"],
                parts["q_norm"][0], parts["kv_norm"][0], jnp.pad(parts["sinks"][0], (0, 120)),
                parts["rel_bias"].reshape(NUM_BUCKETS * SWA_HEADS), parts["b_mod"][0]]
        return jnp.concatenate(flat).reshape(-1, 128)

    w_small = pack(dict(norm_ffn1=norm_ffn1, norm_mix=norm_mix, norm_ffn2=norm_ffn2, norm_final=norm_final,
                        q_norm=q_norm, kv_norm=kv_norm, sinks=sinks, rel_bias=rel_bias, b_mod=b_mod))
    m_small = pack(dict(norm_ffn1=m_norm_ffn1, norm_mix=m_norm_mix, norm_ffn2=m_norm_ffn2, norm_final=m_norm_final,
                        q_norm=m_q_norm, kv_norm=m_kv_norm, sinks=m_sinks, rel_bias=m_rel_bias, b_mod=m_b_mod))
    v_small = pack(dict(norm_ffn1=v_norm_ffn1, norm_mix=v_norm_mix, norm_ffn2=v_norm_ffn2, norm_final=v_norm_final,
                        q_norm=v_q_norm, kv_norm=v_kv_norm, sinks=v_sinks, rel_bias=v_rel_bias, b_mod=v_b_mod))
    packed = (g_small.reshape(-1, 128),) + tuple(_adamw(w_small, g_small.reshape(-1, 128), m_small, v_small,
                                                        name="adamw_small"))

    def unpack(flat):
        flat = flat.reshape(-1)
        out, off = {}, 0
        for nm, width in SMALL_LAYOUT + (("b_mod", N_DEV * WM),):
            out[nm] = flat[off:off + width]
            off += width
        return {"norm_ffn1": out["norm_ffn1"][None], "norm_mix": out["norm_mix"][None],
                "norm_ffn2": out["norm_ffn2"][None], "norm_final": out["norm_final"],
                "q_norm": out["q_norm"][None], "kv_norm": out["kv_norm"][None], "sinks": out["sinks"][None, :SWA_HEADS],
                "rel_bias": out["rel_bias"].reshape(NUM_BUCKETS, SWA_HEADS), "b_mod": out["b_mod"][None]}

    small_res = [unpack(p) for p in packed]
    order = ("w_mod", "b_mod", "norm_ffn1", "ffn1_gate", "ffn1_up", "ffn1_down", "norm_mix", "w_in", "q_norm",
             "kv_norm", "w_uq", "w_ukv", "sinks", "w_o", "norm_ffn2", "ffn2_gate", "ffn2_up", "ffn2_down",
             "rel_bias", "norm_final")
    outs = [loss, grad_x[None]]
    for kind in range(4):
        for nm in order:
            outs.append(res[nm][kind] if nm in res else small_res[kind][nm])
    return tuple(outs)
```

```python
import functools
import math

import numpy as np
import jax
import jax.numpy as jnp
from jax import lax
from jax.experimental import pallas as pl
from jax.experimental.pallas import tpu as pltpu

F32 = jnp.float32
BF16 = jnp.bfloat16
MESH = pl.DeviceIdType.MESH

N_DEV = 8
D = 1024
D_FF = 2816
EPS = 1e-6
N_MOD = 9
SWA_HEADS = 8
SWA_DH = 64
WINDOW = 128
MLA_HEADS = 4
MLA_NOPE = 128
MLA_ROPE = 64
MLA_V = 128
MLA_QR = 256
MLA_KVR = 128
ROPE_THETA = 10000.0
NUM_BUCKETS = 32
D_IN = 1216
D_IN_PAD = 1280
SWA_SCALE = SWA_DH ** -0.5
MLA_SCALE = (MLA_NOPE + MLA_ROPE) ** -0.5

ADAM_LR = 0.001
ADAM_B1 = 0.9
ADAM_B2 = 0.999
ADAM_EPS = 1e-08
ADAM_WD = 0.01
ADAM_STEP = 10

V7X_VMEM_LIMIT = 56 * 1024 * 1024

NT_DIMS = (((1,), (1,)), ((), ()))
TN_DIMS = (((0,), (0,)), ((), ()))


def _dot(a, b):
    return jnp.dot(a, b, preferred_element_type=F32)


def _dot_nt(a, b):
    return lax.dot_general(a, b, NT_DIMS, preferred_element_type=F32)


def _dot_tn(a, b):
    return lax.dot_general(a, b, TN_DIMS, preferred_element_type=F32)


def _params(sem=None):
    return pltpu.CompilerParams(dimension_semantics=sem, vmem_limit_bytes=V7X_VMEM_LIMIT)


def _rstd(x):
    return lax.rsqrt(jnp.mean(x * x, axis=-1, keepdims=True) + EPS)


def _rms_bwd(dy, xhat, r):
    return r * (dy - xhat * jnp.mean(dy * xhat, axis=-1, keepdims=True))


def _sigmoid(a):
    return 1.0 / (1.0 + jnp.exp(-a))


def _ffn_fwd(x, vecs, wgT, wuT, wd, *, name, tm=512, tf=256):
    S = x.shape[0]
    tm = min(tm, S)
    ni, nj = S // tm, D_FF // tf

    def body(x_ref, vec_ref, wg_ref, wu_ref, wd_ref, xo_ref, h_ref, a_ref, b_ref, f_ref, acc_ref):
        j = pl.program_id(1)

        @pl.when(j == 0)
        def _():
            xv = x_ref[...]
            hn = xv * _rstd(xv) * vec_ref[0:1, :]
            h_ref[...] = (hn * (1.0 + vec_ref[2:3, :]) + vec_ref[1:2, :]).astype(BF16)
            acc_ref[...] = jnp.zeros_like(acc_ref)

        h = h_ref[...]
        a = _dot_nt(h, wg_ref[...])
        b = _dot_nt(h, wu_ref[...])
        a_ref[...] = a.astype(BF16)
        b_ref[...] = b.astype(BF16)
        hsw = (a * _sigmoid(a) * b).astype(BF16)
        acc_ref[...] += _dot(hsw, wd_ref[...])

        @pl.when(j == nj - 1)
        def _():
            f = acc_ref[...]
            f_ref[...] = f
            xo_ref[...] = x_ref[...] + (0.5 * vec_ref[3:4, :]) * f

    row = pl.BlockSpec((tm, D), lambda i, j: (i, 0))
    wspec = pl.BlockSpec((tf, D), lambda i, j: (j, 0))
    act = pl.BlockSpec((tm, tf), lambda i, j: (i, j))
    return pl.pallas_call(
        body, name=name, grid=(ni, nj),
        in_specs=[row, pl.BlockSpec((8, D), lambda i, j: (0, 0)), wspec, wspec, wspec],
        out_specs=[row, row, act, act, row],
        out_shape=[jax.ShapeDtypeStruct((S, D), F32), jax.ShapeDtypeStruct((S, D), BF16),
                   jax.ShapeDtypeStruct((S, D_FF), BF16), jax.ShapeDtypeStruct((S, D_FF), BF16),
                   jax.ShapeDtypeStruct((S, D), F32)],
        scratch_shapes=[pltpu.VMEM((tm, D), F32)],
        compiler_params=_params(("parallel", "arbitrary")),
    )(x, vecs, wgT, wuT, wd)


def _ffn_bwd_pre(dxo, f, vecs, *, name, tm=256):
    S = dxo.shape[0]

    def body(dx_ref, f_ref, vec_ref, df_ref, part_ref):
        @pl.when(pl.program_id(0) == 0)
        def _():
            part_ref[...] = jnp.zeros_like(part_ref)

        dx = dx_ref[...]
        df_ref[...] = ((0.5 * vec_ref[3:4, :]) * dx).astype(BF16)
        part_ref[0:1, :] += 0.5 * jnp.sum(dx * f_ref[...], axis=0, keepdims=True)

    row = pl.BlockSpec((tm, D), lambda i: (i, 0))
    vec = pl.BlockSpec((8, D), lambda i: (0, 0))
    return pl.pallas_call(
        body, name=name, grid=(S // tm,), in_specs=[row, row, vec], out_specs=[row, vec],
        out_shape=[jax.ShapeDtypeStruct((S, D), BF16), jax.ShapeDtypeStruct((8, D), F32)],
        compiler_params=_params(("arbitrary",)),
    )(dxo, f, vecs)


def _ffn_bwd_main(h, df, a, b, wgT, wuT, wd, *, name, tm=512, tf=256):
    S = h.shape[0]
    tm = min(tm, S)
    ni, nj = S // tm, D_FF // tf

    def body(h_hbm, df_hbm, a_ref, b_ref, wg_ref, wu_ref, wd_ref,
             gg_ref, gu_ref, gd_ref, dh_hbm,
             h_v, df_v, dh_v, gg_acc, gu_acc, gd_acc, sem):
        j = pl.program_id(0)
        i = pl.program_id(1)

        @pl.when((j == 0) & (i == 0))
        def _():
            c1 = pltpu.make_async_copy(h_hbm, h_v, sem.at[0])
            c2 = pltpu.make_async_copy(df_hbm, df_v, sem.at[1])
            c1.start()
            c2.start()
            c1.wait()
            c2.wait()

        @pl.when(i == 0)
        def _():
            gg_acc[...] = jnp.zeros_like(gg_acc)
            gu_acc[...] = jnp.zeros_like(gu_acc)
            gd_acc[...] = jnp.zeros_like(gd_acc)

        rows = pl.ds(pl.multiple_of(i * tm, tm), tm)
        hi = h_v[rows, :]
        dfi = df_v[rows, :]
        av = a_ref[...].astype(F32)
        bv = b_ref[...].astype(F32)
        sg = _sigmoid(av)
        sa = av * sg
        hsw = (sa * bv).astype(BF16)
        dhsw = _dot_nt(dfi, wd_ref[...])
        da = (dhsw * bv * (sg * (1.0 + av * (1.0 - sg)))).astype(BF16)
        db = (dhsw * sa).astype(BF16)
        gd_acc[...] += _dot_tn(hsw, dfi)
        gg_acc[...] += _dot_tn(da, hi)
        gu_acc[...] += _dot_tn(db, hi)
        dh = _dot(da, wg_ref[...]) + _dot(db, wu_ref[...])

        @pl.when(j == 0)
        def _():
            dh_v[rows, :] = dh

        @pl.when(j > 0)
        def _():
            dh_v[rows, :] += dh

        @pl.when(i == ni - 1)
        def _():
            gg_ref[...] = gg_acc[...].astype(BF16)
            gu_ref[...] = gu_acc[...].astype(BF16)
            gd_ref[...] = gd_acc[...].astype(BF16)

        @pl.when((j == nj - 1) & (i == ni - 1))
        def _():
            c3 = pltpu.make_async_copy(dh_v, dh_hbm, sem.at[2])
            c3.start()
            c3.wait()

    anyspec = pl.BlockSpec(memory_space=pl.ANY)
    wspec = pl.BlockSpec((tf, D), lambda j, i: (j, 0))
    act = pl.BlockSpec((tm, tf), lambda j, i: (i, j))
    return pl.pallas_call(
        body, name=name, grid=(nj, ni),
        in_specs=[anyspec, anyspec, act, act, wspec, wspec, wspec],
        out_specs=[wspec, wspec, wspec, anyspec],
        out_shape=[jax.ShapeDtypeStruct((D_FF, D), BF16)] * 3 + [jax.ShapeDtypeStruct((S, D), F32)],
        scratch_shapes=[pltpu.VMEM((S, D), BF16), pltpu.VMEM((S, D), BF16), pltpu.VMEM((S, D), F32),
                        pltpu.VMEM((tf, D), F32), pltpu.VMEM((tf, D), F32), pltpu.VMEM((tf, D), F32),
                        pltpu.SemaphoreType.DMA((3,))],
        compiler_params=_params(("arbitrary", "arbitrary")),
    )(h, df, a, b, wgT, wuT, wd)


def _norm_bwd(dh, x, dxo, vecs, *, name, tm=256):
    S = x.shape[0]

    def body(dh_ref, x_ref, dxo_ref, vec_ref, dx_ref, part_ref):
        @pl.when(pl.program_id(0) == 0)
        def _():
            part_ref[...] = jnp.zeros_like(part_ref)

        dh = dh_ref[...]
        xv = x_ref[...]
        r = _rstd(xv)
        xhat = xv * r
        w = vec_ref[0:1, :]
        xn = xhat * w
        dxn = dh * (1.0 + vec_ref[2:3, :])
        part_ref[0:1, :] += jnp.sum(dxn * xhat, axis=0, keepdims=True)
        part_ref[1:2, :] += jnp.sum(dh, axis=0, keepdims=True)
        part_ref[2:3, :] += jnp.sum(dh * xn, axis=0, keepdims=True)
        dx_ref[...] = dxo_ref[...] + _rms_bwd(dxn * w, xhat, r)

    row = pl.BlockSpec((tm, D), lambda i: (i, 0))
    vec = pl.BlockSpec((8, D), lambda i: (0, 0))
    return pl.pallas_call(
        body, name=name, grid=(S // tm,), in_specs=[row, row, row, vec], out_specs=[row, vec],
        out_shape=[jax.ShapeDtypeStruct((S, D), F32), jax.ShapeDtypeStruct((8, D), F32)],
        compiler_params=_params(("arbitrary",)),
    )(dh, x, dxo, vecs)


def _head(x, tgt, nf, *, tm=256):
    S = x.shape[0]

    def body(x_ref, t_ref, nf_ref, dx_ref, part_ref):
        @pl.when(pl.program_id(0) == 0)
        def _():
            part_ref[...] = jnp.zeros_like(part_ref)

        xv = x_ref[...]
        r = _rstd(xv)
        xhat = xv * r
        w = nf_ref[...]
        e = xhat * w - t_ref[...]
        dy = e * (1.0 / D)
        part_ref[0:1, :] += jnp.sum(dy * xhat, axis=0, keepdims=True)
        part_ref[1:2, :] += jnp.sum(e * e) * (0.5 / D)
        dx_ref[...] = _rms_bwd(dy * w, xhat, r)

    row = pl.BlockSpec((tm, D), lambda i: (i, 0))
    return pl.pallas_call(
        body, name="head", grid=(S // tm,),
        in_specs=[row, row, pl.BlockSpec((1, D), lambda i: (0, 0))],
        out_specs=[row, pl.BlockSpec((8, D), lambda i: (0, 0))],
        out_shape=[jax.ShapeDtypeStruct((S, D), F32), jax.ShapeDtypeStruct((8, D), F32)],
        compiler_params=_params(("arbitrary",)),
    )(x, tgt, nf)


def _mix_in_fwd(x, vecs, w_inT, *, tm=256):
    S = x.shape[0]

    def body(x_ref, vec_ref, w_ref, h_ref, p_ref):
        xv = x_ref[...]
        hn = xv * _rstd(xv) * vec_ref[0:1, :]
        h = (hn * (1.0 + vec_ref[2:3, :]) + vec_ref[1:2, :]).astype(BF16)
        h_ref[...] = h
        p_ref[...] = _dot_nt(h, w_ref[...])

    row = pl.BlockSpec((tm, D), lambda i: (i, 0))
    return pl.pallas_call(
        body, name="mix_in_fwd", grid=(S // tm,),
        in_specs=[row, pl.BlockSpec((8, D), lambda i: (0, 0)), pl.BlockSpec((D_IN_PAD, D), lambda i: (0, 0))],
        out_specs=[row, pl.BlockSpec((tm, D_IN_PAD), lambda i: (i, 0))],
        out_shape=[jax.ShapeDtypeStruct((S, D), BF16), jax.ShapeDtypeStruct((S, D_IN_PAD), F32)],
        compiler_params=_params(("parallel",)),
    )(x, vecs, w_inT)


def _bucket_table():
    qi = np.arange(WINDOW)[:, None]
    kj = np.arange(2 * WINDOW)[None, :]
    dist = qi + WINDOW - kj
    max_exact = NUM_BUCKETS // 2
    n = np.maximum(dist, 0)
    nf = np.maximum(n, 1).astype(np.float32)
    large = max_exact + (np.log(nf / np.float32(max_exact)) / np.float32(math.log(WINDOW / max_exact))
                         * np.float32(NUM_BUCKETS - max_exact)).astype(np.int32)
    large = np.minimum(large, NUM_BUCKETS - 1)
    return np.where(n < max_exact, n, large).astype(np.int32)


def _bias_build(rel_bias, bucket):
    def body(rb_ref, bk_ref, out_ref):
        bk = bk_ref[...]
        for h in range(SWA_HEADS):
            acc = jnp.zeros((WINDOW, 2 * WINDOW), F32)
            for b in range(NUM_BUCKETS):
                acc = jnp.where(bk == b, rb_ref[b, h], acc)
            out_ref[h] = acc

    return pl.pallas_call(
        body, name="bias_build",
        in_specs=[pl.BlockSpec(memory_space=pltpu.SMEM), pl.BlockSpec(memory_space=pltpu.VMEM)],
        out_specs=pl.BlockSpec(memory_space=pltpu.VMEM),
        out_shape=jax.ShapeDtypeStruct((SWA_HEADS, WINDOW, 2 * WINDOW), F32),
    )(rel_bias, bucket)


def _swa_valid(n):
    row = lax.broadcasted_iota(jnp.int32, (WINDOW, 2 * WINDOW), 0)
    col = lax.broadcasted_iota(jnp.int32, (WINDOW, 2 * WINDOW), 1)
    dist = row + WINDOW - col
    return (dist >= 0) & (dist < WINDOW) & ((col >= WINDOW) | (n > 0))


def _swa_probs(qh, kk, bias_h, sink, valid):
    s = _dot_nt(qh, kk) * SWA_SCALE + bias_h
    s = jnp.where(valid, s, -jnp.inf)
    m = jnp.maximum(jnp.max(s, axis=-1, keepdims=True), sink)
    p = jnp.exp(s - m)
    ps = jnp.exp(sink - m)
    inv = 1.0 / (jnp.sum(p, axis=-1, keepdims=True) + ps)
    return p * inv, ps * inv


def _swa_specs():
    prev = lambda n: jnp.maximum(n - 1, 0)
    return [pl.BlockSpec((WINDOW, 512), lambda n: (n, 0)),
            pl.BlockSpec((WINDOW, 128), lambda n: (n, 4)),
            pl.BlockSpec((WINDOW, 128), lambda n: (prev(n), 4)),
            pl.BlockSpec((WINDOW, 128), lambda n: (n, 5)),
            pl.BlockSpec((WINDOW, 128), lambda n: (prev(n), 5)),
            pl.BlockSpec((SWA_HEADS, WINDOW, 2 * WINDOW), lambda n: (0, 0, 0)),
            pl.BlockSpec(memory_space=pltpu.SMEM)]


def _swa_fwd(proj, bias, sinks):
    S = proj.shape[0]

    def body(q_ref, kc_ref, kp_ref, vc_ref, vp_ref, bias_ref, sink_ref, o_ref):
        valid = _swa_valid(pl.program_id(0))
        q = q_ref[...].astype(BF16)
        kfull = jnp.concatenate([kp_ref[...], kc_ref[...]], axis=0).astype(BF16)
        vfull = jnp.concatenate([vp_ref[...], vc_ref[...]], axis=0).astype(BF16)
        outs = []
        for h in range(SWA_HEADS):
            g = h // 4
            kk = kfull[:, 64 * g:64 * g + 64]
            vv = vfull[:, 64 * g:64 * g + 64]
            pk, _ = _swa_probs(q[:, 64 * h:64 * h + 64], kk, bias_ref[h], sink_ref[0, h], valid)
            outs.append(_dot(pk.astype(BF16), vv))
        o_ref[...] = jnp.concatenate(outs, axis=1)

    return pl.pallas_call(
        body, name="swa_fwd", grid=(S // WINDOW,),
        in_specs=_swa_specs(),
        out_specs=pl.BlockSpec((WINDOW, 512), lambda n: (n, 0)),
        out_shape=jax.ShapeDtypeStruct((S, 512), F32),
        compiler_params=_params(("parallel",)),
    )(proj, proj, proj, proj, proj, bias, sinks)


def _swa_bwd(proj, bias, sinks, do, bucket):
    S = proj.shape[0]
    nb = S // WINDOW

    def body(q_ref, kc_ref, kp_ref, vc_ref, vp_ref, bias_ref, sink_ref, do_ref, bk_ref,
             dq_ref, dk_ref, dv_ref, drb_ref, dsk_ref, dbias_acc):
        n = pl.program_id(0)

        @pl.when(n == 0)
        def _():
            dk_ref[...] = jnp.zeros_like(dk_ref)
            dv_ref[...] = jnp.zeros_like(dv_ref)
            dsk_ref[...] = jnp.zeros_like(dsk_ref)
            dbias_acc[...] = jnp.zeros_like(dbias_acc)
            drb_ref[...] = jnp.zeros_like(drb_ref)

        valid = _swa_valid(n)
        q = q_ref[...].astype(BF16)
        dov = do_ref[...]
        kfull = jnp.concatenate([kp_ref[...], kc_ref[...]], axis=0).astype(BF16)
        vfull = jnp.concatenate([vp_ref[...], vc_ref[...]], axis=0).astype(BF16)
        prow = pl.ds(pl.multiple_of(jnp.maximum(n - 1, 0) * WINDOW, WINDOW), WINDOW)
        crow = pl.ds(pl.multiple_of(n * WINDOW, WINDOW), WINDOW)
        dqs = []
        for g in range(2):
            kk = kfull[:, 64 * g:64 * g + 64]
            vv = vfull[:, 64 * g:64 * g + 64]
            dkk = jnp.zeros((2 * WINDOW, SWA_DH), F32)
            dvv = jnp.zeros((2 * WINDOW, SWA_DH), F32)
            for h in range(4 * g, 4 * g + 4):
                qh = q[:, 64 * h:64 * h + 64]
                pk, psink = _swa_probs(qh, kk, bias_ref[h], sink_ref[0, h], valid)
                pkb = pk.astype(BF16)
                o = _dot(pkb, vv)
                doh = dov[:, 64 * h:64 * h + 64]
                dob = doh.astype(BF16)
                dp = _dot_nt(dob, vv)
                delta = jnp.sum(doh * o, axis=-1, keepdims=True)
                ds = pk * (dp - delta)
                dsk_ref[h:h + 1, :] += jnp.broadcast_to(jnp.sum(-psink * delta, keepdims=True), (1, 128))
                dbias_acc[h] += ds
                dsb = (ds * SWA_SCALE).astype(BF16)
                dqs.append(_dot(dsb, kk))
                dkk += _dot_tn(dsb, qh)
                dvv += _dot_tn(pkb, dob)
            dk_ref[prow, 64 * g:64 * g + 64] += dkk[:WINDOW]
            dk_ref[crow, 64 * g:64 * g + 64] += dkk[WINDOW:]
            dv_ref[prow, 64 * g:64 * g + 64] += dvv[:WINDOW]
            dv_ref[crow, 64 * g:64 * g + 64] += dvv[WINDOW:]
        dq_ref[...] = jnp.concatenate(dqs, axis=1)

        @pl.when(n == nb - 1)
        def _():
            bk = bk_ref[...]
            for h in range(SWA_HEADS):
                dbh = dbias_acc[h]
                for b in range(NUM_BUCKETS):
                    val = jnp.sum(jnp.where(bk == b, dbh, 0.0), keepdims=True)
                    drb_ref[b * 8 + h:b * 8 + h + 1, :] = jnp.broadcast_to(val, (1, 128))

    full = lambda shape: pl.BlockSpec(shape, lambda n: tuple(0 for _ in shape))
    return pl.pallas_call(
        body, name="swa_bwd", grid=(nb,),
        in_specs=_swa_specs() + [pl.BlockSpec((WINDOW, 512), lambda n: (n, 0)), full((WINDOW, 2 * WINDOW))],
        out_specs=[pl.BlockSpec((WINDOW, 512), lambda n: (n, 0)), full((S, 128)), full((S, 128)),
                   full((NUM_BUCKETS * 8, 128)), full((8, 128))],
        out_shape=[jax.ShapeDtypeStruct((S, 512), F32), jax.ShapeDtypeStruct((S, 128), F32),
                   jax.ShapeDtypeStruct((S, 128), F32), jax.ShapeDtypeStruct((NUM_BUCKETS * 8, 128), F32),
                   jax.ShapeDtypeStruct((8, 128), F32)],
        scratch_shapes=[pltpu.VMEM((SWA_HEADS, WINDOW, 2 * WINDOW), F32)],
        compiler_params=_params(("arbitrary",)),
    )(proj, proj, proj, proj, proj, bias, sinks, do, bucket)


def _rope_tables(S):
    inv = ROPE_THETA ** (-jnp.arange(0, MLA_ROPE, 2, dtype=F32) / MLA_ROPE)
    ang = jnp.arange(S, dtype=F32)[:, None] * inv[None, :]
    cos, sin = jnp.cos(ang), jnp.sin(ang)
    return jnp.tile(jnp.concatenate([cos, cos], axis=1), (1, 4)), jnp.tile(jnp.concatenate([-sin, sin], axis=1), (1, 4))


def _swap_halves(x):
    w = x.shape[-1]
    lane = lax.broadcasted_iota(jnp.int32, x.shape, x.ndim - 1)
    return jnp.where((lane % 64) < 32, pltpu.roll(x, w - 32, x.ndim - 1), pltpu.roll(x, 32, x.ndim - 1))


def _mla_pre_fwd(proj, qn_w, kvn_w, wuqT, wukv, cos, sin, *, tm=256):
    S = proj.shape[0]

    def body(ql_ref, kl_ref, kr_ref, qw_ref, kw_ref, wuq_ref, wukv_ref, cos_ref, sin_ref,
             qc_ref, kc_ref, vv_ref):
        ql = ql_ref[...]
        qn = (ql * _rstd(ql) * qw_ref[...]).astype(BF16)
        q = _dot_nt(qn, wuq_ref[...])
        cs, sn = cos_ref[...], sin_ref[...]
        qr = q[:, 512:768]
        qr = qr * cs + _swap_halves(qr) * sn
        half = lax.broadcasted_iota(jnp.int32, (tm, 128), 1) // 64
        kl = kl_ref[...]
        kvn = (kl * _rstd(kl) * kw_ref[...]).astype(BF16)
        kr = kr_ref[...]
        kr = kr * cs[:, :128] + _swap_halves(kr) * sn[:, :128]
        kr2 = (kr + pltpu.roll(kr, 64, 1)).astype(BF16)
        for h in range(MLA_HEADS):
            qc_ref[h, :, 0:128] = q[:, 128 * h:128 * h + 128].astype(BF16)
            chunk = qr[:, 128 * (h // 2):128 * (h // 2) + 128]
            qc_ref[h, :, 128:256] = jnp.where(half == (h % 2), chunk, 0.0).astype(BF16)
            kc_ref[h, :, 0:128] = _dot(kvn, wukv_ref[2 * h]).astype(BF16)
            kc_ref[h, :, 128:256] = kr2
            vv_ref[h] = _dot(kvn, wukv_ref[2 * h + 1]).astype(BF16)

    const = lambda shape: pl.BlockSpec(shape, lambda i: tuple(0 for _ in shape))
    return pl.pallas_call(
        body, name="mla_pre_fwd", grid=(S // tm,),
        in_specs=[pl.BlockSpec((tm, 256), lambda i: (i, 3)), pl.BlockSpec((tm, 128), lambda i: (i, 8)),
                  pl.BlockSpec((tm, 128), lambda i: (i, 9)), const((1, 256)), const((1, 128)),
                  const((768, 256)), const((8, 128, 128)),
                  pl.BlockSpec((tm, 256), lambda i: (i, 0)), pl.BlockSpec((tm, 256), lambda i: (i, 0))],
        out_specs=[pl.BlockSpec((MLA_HEADS, tm, 256), lambda i: (0, i, 0)),
                   pl.BlockSpec((MLA_HEADS, tm, 256), lambda i: (0, i, 0)),
                   pl.BlockSpec((MLA_HEADS, tm, 128), lambda i: (0, i, 0))],
        out_shape=[jax.ShapeDtypeStruct((MLA_HEADS, S, 256), BF16), jax.ShapeDtypeStruct((MLA_HEADS, S, 256), BF16),
                   jax.ShapeDtypeStruct((MLA_HEADS, S, 128), BF16)],
        compiler_params=_params(("parallel",)),
    )(proj, proj, proj, qn_w, kvn_w, wuqT, wukv, cos, sin)


def _causal(i, j, t):
    row = i * t + lax.broadcasted_iota(jnp.int32, (t, t), 0)
    col = j * t + lax.broadcasted_iota(jnp.int32, (t, t), 1)
    return col <= row


def _mla_attn_fwd(qc, kc, vv, *, t=256):
    S = qc.shape[1]

    def body(q_ref, k_ref, v_ref, o_ref, l_ref):
        i = pl.program_id(1)
        q = q_ref[0]

        def step(j, carry):
            m, l, acc = carry
            rows = pl.ds(pl.multiple_of(j * t, t), t)
            s = _dot_nt(q, k_ref[0, rows, :]) * MLA_SCALE
            s = jnp.where(_causal(i, j, t), s, -jnp.inf)
            m_new = jnp.maximum(m, jnp.max(s, axis=-1, keepdims=True))
            alpha = jnp.exp(m - m_new)
            p = jnp.exp(s - m_new)
            l = alpha * l + jnp.sum(p, axis=-1, keepdims=True)
            acc = alpha * acc + _dot(p.astype(BF16), v_ref[0, rows, :])
            return m_new, l, acc

        init = (jnp.full((t, 1), -jnp.inf, F32), jnp.zeros((t, 1), F32), jnp.zeros((t, MLA_V), F32))
        m, l, acc = lax.fori_loop(0, i + 1, step, init)
        o_ref[...] = acc / l
        l_ref[0] = jnp.broadcast_to(m + jnp.log(l), (t, 128))

    return pl.pallas_call(
        body, name="mla_attn_fwd", grid=(MLA_HEADS, S // t),
        in_specs=[pl.BlockSpec((1, t, 256), lambda h, i: (h, i, 0)),
                  pl.BlockSpec((1, S, 256), lambda h, i: (h, 0, 0)),
                  pl.BlockSpec((1, S, 128), lambda h, i: (h, 0, 0))],
        out_specs=[pl.BlockSpec((t, 128), lambda h, i: (i, h)),
                   pl.BlockSpec((1, t, 128), lambda h, i: (h, i, 0))],
        out_shape=[jax.ShapeDtypeStruct((S, 512), F32), jax.ShapeDtypeStruct((MLA_HEADS, S, 128), F32)],
        compiler_params=_params(("parallel", "parallel")),
    )(qc, kc, vv)


def _mla_attn_bwd(qc, kc, vv, o, lse, do, *, t=256):
    S = qc.shape[1]
    nblk = S // t

    def body(q_ref, k_ref, v_ref, o_ref, l_ref, do_ref, dq_ref, dk_ref, dv_ref):
        j = pl.program_id(1)

        @pl.when(j == 0)
        def _():
            dq_ref[...] = jnp.zeros_like(dq_ref)

        k = k_ref[0]
        v = v_ref[0]

        def step(i, carry):
            dk, dv = carry
            rows = pl.ds(pl.multiple_of(i * t, t), t)
            q = q_ref[0, rows, :]
            dov = do_ref[rows, :]
            lrow = l_ref[0, rows, :][:, 0:1]
            s = _dot_nt(q, k) * MLA_SCALE
            p = jnp.where(_causal(i, j, t), jnp.exp(s - lrow), 0.0)
            dob = dov.astype(BF16)
            dv = dv + _dot_tn(p.astype(BF16), dob)
            dp = _dot_nt(dob, v)
            delta = jnp.sum(dov * o_ref[rows, :], axis=-1, keepdims=True)
            ds = (p * (dp - delta) * MLA_SCALE).astype(BF16)
            dk = dk + _dot_tn(ds, q)
            dq_ref[0, rows, :] += _dot(ds, k)
            return dk, dv

        dk, dv = lax.fori_loop(j, nblk, step, (jnp.zeros((t, 256), F32), jnp.zeros((t, MLA_V), F32)))
        dk_ref[0] = dk
        dv_ref[0] = dv

    return pl.pallas_call(
        body, name="mla_attn_bwd", grid=(MLA_HEADS, nblk),
        in_specs=[pl.BlockSpec((1, S, 256), lambda h, j: (h, 0, 0)),
                  pl.BlockSpec((1, t, 256), lambda h, j: (h, j, 0)),
                  pl.BlockSpec((1, t, 128), lambda h, j: (h, j, 0)),
                  pl.BlockSpec((S, 128), lambda h, j: (0, h)),
                  pl.BlockSpec((1, S, 128), lambda h, j: (h, 0, 0)),
                  pl.BlockSpec((S, 128), lambda h, j: (0, h))],
        out_specs=[pl.BlockSpec((1, S, 256), lambda h, j: (h, 0, 0)),
                   pl.BlockSpec((1, t, 256), lambda h, j: (h, j, 0)),
                   pl.BlockSpec((1, t, 128), lambda h, j: (h, j, 0))],
        out_shape=[jax.ShapeDtypeStruct((MLA_HEADS, S, 256), F32), jax.ShapeDtypeStruct((MLA_HEADS, S, 256), F32),
                   jax.ShapeDtypeStruct((MLA_HEADS, S, 128), F32)],
        compiler_params=_params(("parallel", "arbitrary")),
    )(qc, kc, vv, o, lse, do)


def _mla_pre_bwd(proj, qn_w, kvn_w, wuqT, wukv, cos, sin, dqc, dkc, dvv, *, tm=256):
    S = proj.shape[0]

    def body(ql_ref, kl_ref, qw_ref, kw_ref, wuq_ref, wukv_ref, cos_ref, sin_ref, dqc_ref, dkc_ref, dvv_ref,
             dql_ref, dkl_ref, dkr_ref, gq_ref, gkv_ref, part_ref):
        @pl.when(pl.program_id(0) == 0)
        def _():
            gq_ref[...] = jnp.zeros_like(gq_ref)
            gkv_ref[...] = jnp.zeros_like(gkv_ref)
            part_ref[...] = jnp.zeros_like(part_ref)

        cs, sn = cos_ref[...], sin_ref[...]
        half = lax.broadcasted_iota(jnp.int32, (tm, 128), 1) // 64
        ql = ql_ref[...]
        rq = _rstd(ql)
        qhat = ql * rq
        qw = qw_ref[...]
        qn = (qhat * qw).astype(BF16)
        chunks = []
        for pair in range(2):
            chunks.append(jnp.where(half == 0, dqc_ref[2 * pair, :, 128:256], dqc_ref[2 * pair + 1, :, 128:256]))
        dqr = jnp.concatenate(chunks, axis=1)
        dqr = dqr * cs + _swap_halves(dqr * sn)
        dq = jnp.concatenate([dqc_ref[h, :, 0:128] for h in range(MLA_HEADS)] + [dqr], axis=1).astype(BF16)
        gq_ref[...] += _dot_tn(dq, qn)
        dqn = _dot(dq, wuq_ref[...])
        part_ref[0:1, :] += jnp.sum(dqn * qhat, axis=0, keepdims=True)
        dql_ref[...] = _rms_bwd(dqn * qw, qhat, rq)
        kl = kl_ref[...]
        rk = _rstd(kl)
        khat = kl * rk
        kw = kw_ref[...]
        kvn = (khat * kw).astype(BF16)
        dkvn = jnp.zeros((tm, MLA_KVR), F32)
        dkr2 = jnp.zeros((tm, 128), F32)
        for h in range(MLA_HEADS):
            dkn = dkc_ref[h, :, 0:128].astype(BF16)
            dvh = dvv_ref[h].astype(BF16)
            gkv_ref[2 * h] += _dot_tn(kvn, dkn)
            gkv_ref[2 * h + 1] += _dot_tn(kvn, dvh)
            dkvn += _dot_nt(dkn, wukv_ref[2 * h]) + _dot_nt(dvh, wukv_ref[2 * h + 1])
            dkr2 += dkc_ref[h, :, 128:256]
        part_ref[1:2, 0:128] += jnp.sum(dkvn * khat, axis=0, keepdims=True)
        dkl_ref[...] = _rms_bwd(dkvn * kw, khat, rk)
        dkr = jnp.where(half == 0, dkr2 + pltpu.roll(dkr2, 64, 1), 0.0)
        dkr_ref[...] = dkr * cs[:, :128] + _swap_halves(dkr * sn[:, :128])

    const = lambda shape: pl.BlockSpec(shape, lambda i: tuple(0 for _ in shape))
    heads = lambda w: pl.BlockSpec((MLA_HEADS, tm, w), lambda i: (0, i, 0))
    return pl.pallas_call(
        body, name="mla_pre_bwd", grid=(S // tm,),
        in_specs=[pl.BlockSpec((tm, 256), lambda i: (i, 3)), pl.BlockSpec((tm, 128), lambda i: (i, 8)),
                  const((1, 256)), const((1, 128)), const((768, 256)), const((8, 128, 128)),
                  pl.BlockSpec((tm, 256), lambda i: (i, 0)), pl.BlockSpec((tm, 256), lambda i: (i, 0)),
                  heads(256), heads(256), heads(128)],
        out_specs=[pl.BlockSpec((tm, 256), lambda i: (i, 0)), pl.BlockSpec((tm, 128), lambda i: (i, 0)),
                   pl.BlockSpec((tm, 128), lambda i: (i, 0)), const((768, 256)), const((8, 128, 128)), const((8, 256))],
        out_shape=[jax.ShapeDtypeStruct((S, 256), F32), jax.ShapeDtypeStruct((S, 128), F32),
                   jax.ShapeDtypeStruct((S, 128), F32), jax.ShapeDtypeStruct((768, 256), F32),
                   jax.ShapeDtypeStruct((8, 128, 128), F32), jax.ShapeDtypeStruct((8, 256), F32)],
        compiler_params=_params(("arbitrary",)),
    )(proj, proj, qn_w, kvn_w, wuqT, wukv, cos, sin, dqc, dkc, dvv)


def _mix_out_fwd(x, oa, ob, w_o, vecs, *, tm=256):
    S = x.shape[0]

    def body(x_ref, oa_ref, ob_ref, w_ref, vec_ref, xo_ref, mo_ref):
        mo = _dot(oa_ref[...].astype(BF16), w_ref[0:512, :]) + _dot(ob_ref[...].astype(BF16), w_ref[512:1024, :])
        mo_ref[...] = mo
        xo_ref[...] = x_ref[...] + vec_ref[3:4, :] * mo

    row = pl.BlockSpec((tm, D), lambda i: (i, 0))
    half = pl.BlockSpec((tm, 512), lambda i: (i, 0))
    return pl.pallas_call(
        body, name="mix_out_fwd", grid=(S // tm,),
        in_specs=[row, half, half, pl.BlockSpec((D, D), lambda i: (0, 0)), pl.BlockSpec((8, D), lambda i: (0, 0))],
        out_specs=[row, row],
        out_shape=[jax.ShapeDtypeStruct((S, D), F32), jax.ShapeDtypeStruct((S, D), F32)],
        compiler_params=_params(("parallel",)),
    )(x, oa, ob, w_o, vecs)


def _mix_out_bwd(dxo, mo, oa, ob, w_o, vecs, *, tm=256):
    S = dxo.shape[0]

    def body(dx_ref, mo_ref, oa_ref, ob_ref, w_ref, vec_ref, doa_ref, dob_ref, gw_ref, part_ref):
        @pl.when(pl.program_id(0) == 0)
        def _():
            gw_ref[...] = jnp.zeros_like(gw_ref)
            part_ref[...] = jnp.zeros_like(part_ref)

        dx = dx_ref[...]
        part_ref[0:1, :] += jnp.sum(dx * mo_ref[...], axis=0, keepdims=True)
        dmo = (vec_ref[3:4, :] * dx).astype(BF16)
        doa_ref[...] = _dot_nt(dmo, w_ref[0:512, :])
        dob_ref[...] = _dot_nt(dmo, w_ref[512:1024, :])
        gw_ref[0:512, :] += _dot_tn(oa_ref[...].astype(BF16), dmo)
        gw_ref[512:1024, :] += _dot_tn(ob_ref[...].astype(BF16), dmo)

    row = pl.BlockSpec((tm, D), lambda i: (i, 0))
    half = pl.BlockSpec((tm, 512), lambda i: (i, 0))
    return pl.pallas_call(
        body, name="mix_out_bwd", grid=(S // tm,),
        in_specs=[row, row, half, half, pl.BlockSpec((D, D), lambda i: (0, 0)), pl.BlockSpec((8, D), lambda i: (0, 0))],
        out_specs=[half, half, pl.BlockSpec((D, D), lambda i: (0, 0)), pl.BlockSpec((8, D), lambda i: (0, 0))],
        out_shape=[jax.ShapeDtypeStruct((S, 512), F32), jax.ShapeDtypeStruct((S, 512), F32),
                   jax.ShapeDtypeStruct((D, D), F32), jax.ShapeDtypeStruct((8, D), F32)],
        compiler_params=_params(("arbitrary",)),
    )(dxo, mo, oa, ob, w_o, vecs)


def _mix_in_bwd(h, w_inT, dq, dk, dv, dql, dkl, dkr, *, tm=256):
    S = h.shape[0]
    offs = (0, 512, 640, 768, 1024, 1152)
    wid = (512, 128, 128, 256, 128, 128)

    def body(h_ref, w_ref, dq_ref, dk_ref, dv_ref, dql_ref, dkl_ref, dkr_ref, dh_ref, gw_ref):
        @pl.when(pl.program_id(0) == 0)
        def _():
            gw_ref[...] = jnp.zeros_like(gw_ref)

        hv = h_ref[...]
        dh = jnp.zeros((tm, D), F32)
        for ref, o, w in zip((dq_ref, dk_ref, dv_ref, dql_ref, dkl_ref, dkr_ref), offs, wid):
            dpart = ref[...].astype(BF16)
            dh += _dot(dpart, w_ref[o:o + w, :])
            gw_ref[o:o + w, :] += _dot_tn(dpart, hv)
        dh_ref[...] = dh

    row = pl.BlockSpec((tm, D), lambda i: (i, 0))
    part = lambda w: pl.BlockSpec((tm, w), lambda i: (i, 0))
    return pl.pallas_call(
        body, name="mix_in_bwd", grid=(S // tm,),
        in_specs=[row, pl.BlockSpec((D_IN_PAD, D), lambda i: (0, 0))] + [part(w) for w in wid],
        out_specs=[row, pl.BlockSpec((D_IN_PAD, D), lambda i: (0, 0))],
        out_shape=[jax.ShapeDtypeStruct((S, D), F32), jax.ShapeDtypeStruct((D_IN_PAD, D), F32)],
        compiler_params=_params(("arbitrary",)),
    )(h, w_inT, dq, dk, dv, dql, dkl, dkr)


def _vecs(norm_w, mod9, k):
    return jnp.concatenate([norm_w.reshape(1, D), mod9[3 * k:3 * k + 3], jnp.zeros((4, D), F32)], axis=0)


def _uq_group_rows(wuqT):
    per = MLA_NOPE + MLA_ROPE
    nope = [wuqT[per * h:per * h + MLA_NOPE] for h in range(MLA_HEADS)]
    rope = [wuqT[per * h + MLA_NOPE:per * (h + 1)] for h in range(MLA_HEADS)]
    return jnp.concatenate(nope + rope, axis=0)


def _uq_ungroup_rows(g):
    parts = []
    for h in range(MLA_HEADS):
        parts += [g[MLA_NOPE * h:MLA_NOPE * (h + 1)], g[512 + MLA_ROPE * h:512 + MLA_ROPE * (h + 1)]]
    return jnp.concatenate(parts, axis=0)


def _local_step(x, tgt, mod9, norms, sinks, rel_bias, q_norm, kv_norm, W, on_grads=None):
    if on_grads is None:
        on_grads = lambda group, grads, vecs: vecs
    S = x.shape[0]
    v1 = _vecs(norms["ffn1"], mod9, 0)
    v2 = _vecs(norms["mix"], mod9, 1)
    v3 = _vecs(norms["ffn2"], mod9, 2)
    bucket = jnp.asarray(_bucket_table())
    cos, sin = _rope_tables(S)
    w_inT = jnp.pad(W["w_inT"], ((0, D_IN_PAD - D_IN), (0, 0))).astype(BF16)
    wuqT = _uq_group_rows(W["w_uqT"])

    x1, h1, a1, b1, f1 = _ffn_fwd(x, v1, W["g1T"], W["u1T"], W["d1"], name="ffn1_fwd")
    h2, proj = _mix_in_fwd(x1, v2, w_inT)
    bias = _bias_build(rel_bias, bucket)
    oa = _swa_fwd(proj, bias, sinks)
    qc, kc, vv = _mla_pre_fwd(proj, q_norm, kv_norm, wuqT, W["w_ukv"], cos, sin)
    ob, lse = _mla_attn_fwd(qc, kc, vv)
    x2, mo = _mix_out_fwd(x1, oa, ob, W["w_o"], v2)
    x3, h3, a3, b3, f3 = _ffn_fwd(x2, v3, W["g3T"], W["u3T"], W["d3"], name="ffn2_fwd")
    dx3, head_part = _head(x3, tgt, norms["final"])

    df3, g3_part = _ffn_bwd_pre(dx3, f3, v3, name="ffn2_bwd_pre")
    gg3, gu3, gd3, dh3 = _ffn_bwd_main(h3, df3, a3, b3, W["g3T"], W["u3T"], W["d3"], name="ffn2_bwd")
    ffn2 = {"g3T": gg3, "u3T": gu3, "d3": gd3}
    v2 = on_grads("ffn2", ffn2, v2)
    dx2, n3_part = _norm_bwd(dh3, x2, dx3, v3, name="ffn2_norm_bwd")
    doa, dob, g_wo, g2_part = _mix_out_bwd(dx2, mo, oa, ob, W["w_o"], v2)
    dq, dk, dv, drb, dsk = _swa_bwd(proj, bias, sinks, doa, bucket)
    dqc, dkc, dvv = _mla_attn_bwd(qc, kc, vv, ob, lse, dob)
    dql, dkl, dkr, g_uq, g_ukv, mla_part = _mla_pre_bwd(proj, q_norm, kv_norm, wuqT, W["w_ukv"], cos, sin, dqc, dkc, dvv)
    dh2, g_win = _mix_in_bwd(h2, w_inT, dq, dk, dv, dql, dkl, dkr)
    mixer = {"w_inT": g_win[:D_IN], "w_uqT": _uq_ungroup_rows(g_uq).astype(BF16),
             "w_ukv": g_ukv.astype(BF16), "w_o": g_wo.astype(BF16)}
    v1 = on_grads("mixer", mixer, v1)
    dx1, n2_part = _norm_bwd(dh2, x1, dx2, v2, name="mix_norm_bwd")
    df1, g1_part = _ffn_bwd_pre(dx1, f1, v1, name="ffn1_bwd_pre")
    gg1, gu1, gd1, dh1 = _ffn_bwd_main(h1, df1, a1, b1, W["g1T"], W["u1T"], W["d1"], name="ffn1_bwd")
    ffn1 = {"g1T": gg1, "u1T": gu1, "d1": gd1}
    v1 = on_grads("ffn1", ffn1, v1)
    dx0, n1_part = _norm_bwd(dh1, x, dx1, v1, name="ffn1_norm_bwd")

    grads = {**ffn1, **ffn2, **mixer}
    dmod9 = jnp.concatenate([n1_part[1:3], g1_part[0:1], n2_part[1:3], g2_part[0:1],
                             n3_part[1:3], g3_part[0:1]], axis=0)
    small = jnp.concatenate([n1_part[0], n2_part[0], n3_part[0], head_part[0], mla_part[0],
                             mla_part[1, :128], jnp.pad(dsk[:, 0], (0, 120)), drb[:, 0]])
    return head_part[1, 0], dx0, grads, small, dmod9


SMALL_LAYOUT = (("norm_ffn1", 1024), ("norm_mix", 1024), ("norm_ffn2", 1024), ("norm_final", 1024),
                ("q_norm", 256), ("kv_norm", 128), ("sinks", 128), ("rel_bias", 256))
N_SMALL = sum(n for _, n in SMALL_LAYOUT)


def _coords():
    return lax.axis_index("x"), lax.axis_index("y"), lax.axis_index("c")


def _flip(v, bit):
    return 1 - v if bit else v


def _peer(r):
    x, y, c = _coords()
    return (_flip(x, r & 4), _flip(y, r & 2), _flip(c, r & 1))


def _mod_fwd(c_tile, w_mod, b_mod3):
    W = w_mod.shape[1]

    def body(c_ref, w_ref, b_ref, mod_ref, ca_ref, call_ref, part_ref, send_sems, recv_sems):
        x, y, c = _coords()
        me = 4 * x + 2 * y + c
        call_ref[me] = c_ref[...]
        sends = []
        for r in range(1, N_DEV):
            cp = pltpu.make_async_remote_copy(c_ref, call_ref.at[me], send_sems.at[0, r], recv_sems.at[0, r],
                                              device_id=_peer(r), device_id_type=MESH)
            cp.start()
            sends.append(cp)
        for r in range(1, N_DEV):
            pltpu.make_async_remote_copy(c_ref, call_ref.at[me], send_sems.at[0, r], recv_sems.at[0, r],
                                         device_id=_peer(r), device_id_type=MESH).wait_recv()
        cv = call_ref[...].reshape(8 * N_DEV, D)
        ca = (cv * _sigmoid(cv)).astype(BF16)
        ca_ref[...] = ca
        part_ref[...] = _dot(ca, w_ref[...].astype(BF16)).reshape(N_DEV, 8, W)
        mod_ref[me] = part_ref[me] + b_ref[me]
        for r in range(1, N_DEV):
            cp = pltpu.make_async_remote_copy(part_ref.at[me ^ r], mod_ref.at[me], send_sems.at[1, r],
                                              recv_sems.at[1, r], device_id=_peer(r), device_id_type=MESH)
            cp.start()
            sends.append(cp)
        for r in range(1, N_DEV):
            pltpu.make_async_remote_copy(part_ref.at[me ^ r], mod_ref.at[me], send_sems.at[1, r],
                                         recv_sems.at[1, r], device_id=_peer(r), device_id_type=MESH).wait_recv()
            mod_ref[me ^ r] = mod_ref[me ^ r] + b_ref[me ^ r]
        for cp in sends:
            cp.wait_send()

    vm = pl.BlockSpec(memory_space=pltpu.VMEM)
    return pl.pallas_call(
        body, name="mod_fwd", in_specs=[vm, vm, vm], out_specs=[vm, vm],
        out_shape=[jax.ShapeDtypeStruct((N_DEV, 8, W), F32), jax.ShapeDtypeStruct((8 * N_DEV, D), BF16)],
        scratch_shapes=[pltpu.VMEM((N_DEV, 8, D), F32), pltpu.VMEM((N_DEV, 8, W), F32),
                        pltpu.SemaphoreType.DMA((2, N_DEV)), pltpu.SemaphoreType.DMA((2, N_DEV))],
        compiler_params=_params(),
    )(c_tile, w_mod, b_mod3)


def _mod_bwd(dmod3, small_tile, ca):
    W = dmod3.shape[2]

    def body(dm_ref, sm_ref, ca_ref, gw_ref, gb_ref, ssum_ref, dmcols, sm_all, gb_mine, send_sems, recv_sems):
        x, y, c = _coords()
        me = 4 * x + 2 * y + c
        dmcols[me] = dm_ref[me]
        sm_all[me] = sm_ref[...]
        sends = []
        for r in range(1, N_DEV):
            cp = pltpu.make_async_remote_copy(dm_ref.at[me ^ r], dmcols.at[me], send_sems.at[0, r], recv_sems.at[0, r],
                                              device_id=_peer(r), device_id_type=MESH)
            cp.start()
            sends.append(cp)
            cp = pltpu.make_async_remote_copy(sm_ref, sm_all.at[me], send_sems.at[1, r], recv_sems.at[1, r],
                                              device_id=_peer(r), device_id_type=MESH)
            cp.start()
            sends.append(cp)
        for r in range(1, N_DEV):
            pltpu.make_async_remote_copy(dm_ref.at[me ^ r], dmcols.at[me], send_sems.at[0, r], recv_sems.at[0, r],
                                         device_id=_peer(r), device_id_type=MESH).wait_recv()
            pltpu.make_async_remote_copy(sm_ref, sm_all.at[me], send_sems.at[1, r], recv_sems.at[1, r],
                                         device_id=_peer(r), device_id_type=MESH).wait_recv()
        dm = dmcols[...].reshape(8 * N_DEV, W)
        gw_ref[...] = _dot_tn(ca_ref[...], dm.astype(BF16))
        first_row = lax.broadcasted_iota(jnp.int32, (8, W), 0) == 0
        gb_mine[...] = jnp.where(first_row, jnp.sum(dm, axis=0, keepdims=True), 0.0)
        gb_ref[me] = gb_mine[...]
        for r in range(1, N_DEV):
            cp = pltpu.make_async_remote_copy(gb_mine, gb_ref.at[me], send_sems.at[2, r], recv_sems.at[2, r],
                                              device_id=_peer(r), device_id_type=MESH)
            cp.start()
            sends.append(cp)
        total = sm_all[0]
        for k in range(1, N_DEV):
            total = total + sm_all[k]
        ssum_ref[...] = total
        for r in range(1, N_DEV):
            pltpu.make_async_remote_copy(gb_mine, gb_ref.at[me], send_sems.at[2, r], recv_sems.at[2, r],
                                         device_id=_peer(r), device_id_type=MESH).wait_recv()
        for cp in sends:
            cp.wait_send()

    vm = pl.BlockSpec(memory_space=pltpu.VMEM)
    return pl.pallas_call(
        body, name="mod_bwd", in_specs=[vm, vm, vm], out_specs=[vm, vm, vm],
        out_shape=[jax.ShapeDtypeStruct((D, W), F32), jax.ShapeDtypeStruct((N_DEV, 8, W), F32),
                   jax.ShapeDtypeStruct((8, N_SMALL), F32)],
        scratch_shapes=[pltpu.VMEM((N_DEV, 8, W), F32), pltpu.VMEM((N_DEV, 8, N_SMALL), F32), pltpu.VMEM((8, W), F32),
                        pltpu.SemaphoreType.DMA((3, N_DEV)), pltpu.SemaphoreType.DMA((3, N_DEV))],
        compiler_params=_params(),
    )(dmod3, small_tile, ca)


def _wgather(shards):
    n = len(shards)

    def body(*refs):
        ins, outs = refs[:n], refs[n:2 * n]
        send_sems, recv_sems, local_sems = refs[2 * n:]
        x, y, c = _coords()
        me = 4 * x + 2 * y + c
        sib = (x, y, 1 - c)
        chips = [(1 - x, y), (x, 1 - y), (1 - x, 1 - y)]

        def copy(k, slot, block, to, src=None):
            return pltpu.make_async_remote_copy(
                src_ref=outs[k].at[block] if src is None else src, dst_ref=outs[k].at[block],
                send_sem=send_sems.at[k, slot], recv_sem=recv_sems.at[k, slot], device_id=to, device_id_type=MESH)

        local = [pltpu.make_async_copy(ins[k], outs[k].at[me], local_sems.at[k]) for k in range(n)]
        for cp in local:
            cp.start()
        first = []
        for k in range(n):
            first.append(copy(k, 0, me, sib, src=ins[k]))
            for j, chip in enumerate(chips):
                first.append(copy(k, 1 + j, me, (*chip, c), src=ins[k]))
        for cp in first:
            cp.start()
        passed = []
        for j, (cx, cy) in enumerate(chips):
            for k in range(n):
                blk = 4 * cx + 2 * cy + c
                copy(k, 1 + j, blk, sib).wait_recv()
                cp = copy(k, 4 + j, blk, sib)
                cp.start()
                passed.append(cp)
        for k in range(n):
            copy(k, 0, 4 * x + 2 * y + (1 - c), sib).wait_recv()
            for j, (cx, cy) in enumerate(chips):
                copy(k, 4 + j, 4 * cx + 2 * cy + (1 - c), sib).wait_recv()
        for cp in first + passed:
            cp.wait_send()
        for cp in local:
            cp.wait()

    anyspec = pl.BlockSpec(memory_space=pl.ANY)
    return pl.pallas_call(
        body, name="wgather", in_specs=[anyspec] * n, out_specs=[anyspec] * n,
        out_shape=[jax.ShapeDtypeStruct((N_DEV,) + s.shape, s.dtype) for s in shards],
        scratch_shapes=[pltpu.SemaphoreType.DMA((n, 7)), pltpu.SemaphoreType.DMA((n, 7)),
                        pltpu.SemaphoreType.DMA((n,))],
    )(*shards)


def _rs_d2d(grads, *, name):
    n = len(grads)

    def body(*refs):
        ins, outs = refs[:n], refs[n:2 * n]
        send_sems, recv_sems = refs[2 * n:]
        x, y, c = _coords()
        sib = (x, y, 1 - c)
        cps = []
        for k in range(n):
            for q in range(4):
                cps.append(pltpu.make_async_remote_copy(
                    src_ref=ins[k].at[2 * q + (1 - c)], dst_ref=outs[k].at[q],
                    send_sem=send_sems.at[k, q], recv_sem=recv_sems.at[k, q], device_id=sib, device_id_type=MESH))
        for cp in cps:
            cp.start()
        for cp in cps:
            cp.wait_recv()
        for cp in cps:
            cp.wait_send()

    anyspec = pl.BlockSpec(memory_space=pl.ANY)
    return pl.pallas_call(
        body, name=name, in_specs=[anyspec] * n, out_specs=[anyspec] * n,
        out_shape=[jax.ShapeDtypeStruct((4,) + g.shape[1:], g.dtype) for g in grads],
        scratch_shapes=[pltpu.SemaphoreType.DMA((n, 4)), pltpu.SemaphoreType.DMA((n, 4))],
    )(*grads)


def _chipsum(g, sib, cidx, *, name):
    _, r, cc = g.shape

    def body(c_ref, g_ref, s_ref, o_ref):
        o_ref[...] = (g_ref[...].astype(F32) + s_ref[...].astype(F32)).astype(o_ref.dtype)

    return pl.pallas_call(
        body, name=name,
        grid_spec=pltpu.PrefetchScalarGridSpec(
            num_scalar_prefetch=1, grid=(4,),
            in_specs=[pl.BlockSpec((1, r, cc), lambda q, c_ref: (2 * q + c_ref[0], 0, 0)),
                      pl.BlockSpec((1, r, cc), lambda q, c_ref: (q, 0, 0))],
            out_specs=pl.BlockSpec((1, r, cc), lambda q, c_ref: (q, 0, 0))),
        out_shape=jax.ShapeDtypeStruct((4, r, cc), g.dtype),
        compiler_params=_params(("arbitrary",)),
    )(cidx, g, sib)


HBM_SPEC = pl.BlockSpec(memory_space=pltpu.HBM)
SEM_SPEC = pl.BlockSpec(memory_space=pltpu.SEMAPHORE)
DATAFLOW = pltpu.SideEffectType.DATAFLOW_SIDE_EFFECTING


def _in_hbm(a):
    return pltpu.with_memory_space_constraint(a, pltpu.HBM)


def _ici_copies(sums, lands, send_sems, recv_sems):
    x, y, c = _coords()
    chips = [(1 - x, y), (x, 1 - y), (1 - x, 1 - y)]
    cps = []
    for k in range(len(sums)):
        for j, (cx, cy) in enumerate(chips):
            cps.append(pltpu.make_async_remote_copy(
                src_ref=sums[k].at[2 * cx + cy], dst_ref=lands[k].at[j],
                send_sem=send_sems.at[3 * k + j], recv_sem=recv_sems.at[3 * k + j],
                device_id=(cx, cy, c), device_id_type=MESH))
    return cps


def _rs_ici_start(sums, *, name):
    n = len(sums)

    def body(*refs):
        token = refs[-1]
        for cp in _ici_copies(refs[:n], refs[n:2 * n], refs[2 * n], refs[2 * n + 1]):
            cp.start()
        token[...] = jnp.zeros_like(token)

    lands = [_in_hbm(lax.empty((3,) + s.shape[1:], s.dtype)) for s in sums]
    out = pl.pallas_call(
        body, name=name,
        out_shape=(pltpu.SemaphoreType.DMA((3 * n,)), pltpu.SemaphoreType.DMA((3 * n,)),
                   *[pltpu.HBM(s.shape, s.dtype) for s in sums], *[pltpu.HBM(l.shape, l.dtype) for l in lands],
                   jax.ShapeDtypeStruct((8, 128), F32)),
        in_specs=[HBM_SPEC] * (2 * n),
        out_specs=(SEM_SPEC, SEM_SPEC, *[HBM_SPEC] * (2 * n), pl.BlockSpec(memory_space=pltpu.VMEM)),
        input_output_aliases={i: 2 + i for i in range(2 * n)},
        compiler_params=pltpu.CompilerParams(has_side_effects=DATAFLOW),
    )(*[_in_hbm(s) for s in sums], *lands)
    return out[0], out[1], list(out[2:2 + n]), list(out[2 + n:2 + 2 * n]), out[-1]


def _rs_ici_wait(send_sems, recv_sems, sums, lands, after, *, name):
    n = len(sums)

    def body(*refs):
        for cp in _ici_copies(refs[:n], refs[n:2 * n], refs[2 * n], refs[2 * n + 1]):
            cp.wait_send()
            cp.wait_recv()

    out = pl.pallas_call(
        body, name=name,
        out_shape=[pltpu.HBM(a.shape, a.dtype) for a in list(sums) + list(lands)],
        in_specs=[HBM_SPEC] * (2 * n) + [SEM_SPEC, SEM_SPEC] + [pl.BlockSpec(memory_space=pl.ANY)] * len(after),
        out_specs=[HBM_SPEC] * (2 * n),
        input_output_aliases={i: i for i in range(2 * n)},
        compiler_params=pltpu.CompilerParams(has_side_effects=DATAFLOW),
    )(*sums, *lands, send_sems, recv_sems, *after)
    return list(out[:n]), list(out[n:])


def _gsum(cs, rcv, qidx, *, name):
    _, r, cc = cs.shape

    def body(q_ref, c_ref, r_ref, o_ref):
        o_ref[...] = ((c_ref[0].astype(F32) + r_ref[0].astype(F32)) + r_ref[1].astype(F32)) + r_ref[2].astype(F32)

    return pl.pallas_call(
        body, name=name,
        grid_spec=pltpu.PrefetchScalarGridSpec(
            num_scalar_prefetch=1, grid=(1,),
            in_specs=[pl.BlockSpec((1, r, cc), lambda i, q_ref: (q_ref[0], 0, 0)),
                      pl.BlockSpec((3, r, cc), lambda i, q_ref: (0, 0, 0))],
            out_specs=pl.BlockSpec((r, cc), lambda i, q_ref: (0, 0))),
        out_shape=jax.ShapeDtypeStruct((r, cc), F32),
        compiler_params=_params(("arbitrary",)),
    )(qidx, cs, rcv)


def _adamw(w, g, m, v, *, name):
    R, C = w.shape
    tr = R if R <= 512 else 256
    c1 = 1.0 / (1.0 - ADAM_B1 ** ADAM_STEP)
    c2 = 1.0 / (1.0 - ADAM_B2 ** ADAM_STEP)

    def body(w_ref, g_ref, m_ref, v_ref, d_ref, nm_ref, nv_ref):
        gv = g_ref[...]
        m2 = ADAM_B1 * m_ref[...] + (1.0 - ADAM_B1) * gv
        v2 = ADAM_B2 * v_ref[...] + (1.0 - ADAM_B2) * (gv * gv)
        nm_ref[...] = m2
        nv_ref[...] = v2
        d_ref[...] = -ADAM_LR * ((m2 * c1) / (jnp.sqrt(v2 * c2) + ADAM_EPS) + ADAM_WD * w_ref[...])

    blk = pl.BlockSpec((tr, C), lambda i: (i, 0))
    return pl.pallas_call(
        body, name=name, grid=(R // tr,), in_specs=[blk] * 4, out_specs=[blk] * 3,
        out_shape=[jax.ShapeDtypeStruct((R, C), F32)] * 3,
        compiler_params=_params(("parallel",)),
    )(w, g, m, v)


BIG = ("g1T", "u1T", "d1", "g3T", "u3T", "d3", "w_inT", "w_uqT", "w_ukv", "w_o")
TRANSPOSED = ("g1T", "u1T", "g3T", "u3T", "w_inT", "w_uqT")


def kernel(x, c, w_mod, b_mod, norm_ffn1, ffn1_gate, ffn1_up, ffn1_down, norm_mix, w_in, q_norm, kv_norm, w_uq, w_ukv, sinks, w_o, norm_ffn2, ffn2_gate, ffn2_up, ffn2_down, rel_bias, norm_final, loss_target, m_w_mod, m_b_mod, m_norm_ffn1, m_ffn1_gate, m_ffn1_up, m_ffn1_down, m_norm_mix, m_w_in, m_q_norm, m_kv_norm, m_w_uq, m_w_ukv, m_sinks, m_w_o, m_norm_ffn2, m_ffn2_gate, m_ffn2_up, m_ffn2_down, m_rel_bias, m_norm_final, v_w_mod, v_b_mod, v_norm_ffn1, v_ffn1_gate, v_ffn1_up, v_ffn1_down, v_norm_mix, v_w_in, v_q_norm, v_kv_norm, v_w_uq, v_w_ukv, v_sinks, v_w_o, v_norm_ffn2, v_ffn2_gate, v_ffn2_up, v_ffn2_down, v_rel_bias, v_norm_final):
    mx, my, mc = _coords()
    cidx = jnp.reshape(mc, (1,)).astype(jnp.int32)
    qidx = jnp.reshape(2 * mx + my, (1,)).astype(jnp.int32)
    WM = w_mod.shape[2]

    c_tile = jnp.pad(c, ((0, 7), (0, 0)))
    b_mod3 = jnp.pad(b_mod.reshape(N_DEV, 1, WM), ((0, 0), (0, 7), (0, 0)))
    mod3, ca = _mod_fwd(c_tile, w_mod[0], b_mod3)
    mod9 = mod3[:, 0, :].reshape(N_MOD, D)

    shards = {"g1T": ffn1_gate[0].T.astype(BF16), "u1T": ffn1_up[0].T.astype(BF16), "d1": ffn1_down[0].astype(BF16),
              "g3T": ffn2_gate[0].T.astype(BF16), "u3T": ffn2_up[0].T.astype(BF16), "d3": ffn2_down[0].astype(BF16),
              "w_inT": w_in[0].T, "w_uqT": w_uq[0].T.astype(BF16), "w_ukv": w_ukv[0].astype(BF16),
              "w_o": w_o[0].astype(BF16)}
    gathered = _wgather([shards[k] for k in BIG])
    W = {}
    for k, gth in zip(BIG, gathered):
        W[k] = gth if k == "w_ukv" else gth.reshape(N_DEV * gth.shape[1], gth.shape[2])

    norms = {"ffn1": norm_ffn1, "mix": norm_mix, "ffn2": norm_ffn2, "final": norm_final.reshape(1, D)}
    in_flight = {}

    def on_grads(group, g, vecs):
        names = list(g)
        by_dest = [g[k] if k == "w_ukv" else g[k].reshape((N_DEV, g[k].shape[0] // N_DEV) + g[k].shape[1:])
                   for k in names]
        from_sib = _rs_d2d(by_dest, name="rs_d2d_" + group)
        sums = [_chipsum(a, s, cidx, name="chipsum_" + k) for k, a, s in zip(names, by_dest, from_sib)]
        send, recv, sums, lands, token = _rs_ici_start(sums, name="rs_ici_start_" + group)
        in_flight[group] = (names, send, recv, sums, lands, token)
        return vecs + token[0:1, 0:1]

    loss_local, grad_x, _, small, dmod9 = _local_step(
        x[0], loss_target[0], mod9, norms, sinks, rel_bias, q_norm, kv_norm, W, on_grads=on_grads)
    loss = lax.psum(loss_local, ("x", "y", "c"))

    gw = {}

    def finish(group, after):
        names, send, recv, sums, lands, _ = in_flight[group]
        sums, lands = _rs_ici_wait(send, recv, sums, lands, after, name="rs_ici_wait_" + group)
        for k, cs, rc in zip(names, sums, lands):
            g = _gsum(cs, rc, qidx, name="gsum_" + k)
            gw[k] = g.T if k in TRANSPOSED else g

    ffn1_started = in_flight["ffn1"][5]
    finish("ffn2", [ffn1_started])
    finish("mixer", [ffn1_started])

    dmod3 = jnp.pad(dmod9.reshape(N_DEV, 1, WM), ((0, 0), (0, 7), (0, 0)))
    small_tile = jnp.pad(small.reshape(1, N_SMALL), ((0, 7), (0, 0)))
    g_wmod, gb3, ssum = _mod_bwd(dmod3, small_tile, ca)
    g_small = jnp.concatenate([ssum[0], gb3[:, 0, :].reshape(N_DEV * WM)])

    res = {}

    def update(items):
        for k, (wk, gk, mk, vk) in items.items():
            d, nm, nv = _adamw(wk[0], gk, mk[0], vk[0], name="adamw_" + k)
            res[k] = tuple(a[None] for a in (gk, d, nm, nv))

    update({"w_mod": (w_mod, g_wmod, m_w_mod, v_w_mod),
            "w_in": (w_in, gw["w_inT"], m_w_in, v_w_in),
            "w_uq": (w_uq, gw["w_uqT"], m_w_uq, v_w_uq),
            "w_ukv": (w_ukv, gw["w_ukv"], m_w_ukv, v_w_ukv),
            "w_o": (w_o, gw["w_o"], m_w_o, v_w_o),
            "ffn2_gate": (ffn2_gate, gw["g3T"], m_ffn2_gate, v_ffn2_gate),
            "ffn2_up": (ffn2_up, gw["u3T"], m_ffn2_up, v_ffn2_up),
            "ffn2_down": (ffn2_down, gw["d3"], m_ffn2_down, v_ffn2_down)})
    finish("ffn1", [res[k][3] for k in res])
    update({"ffn1_gate": (ffn1_gate, gw["g1T"], m_ffn1_gate, v_ffn1_gate),
            "ffn1_up": (ffn1_up, gw["u1T"], m_ffn1_up, v_ffn1_up),
            "ffn1_down": (ffn1_down, gw["d1"], m_ffn1_down, v_ffn1_down)})

    def pack(parts):
        flat = [parts["norm_ffn1"][0], parts["norm_mix"][0], parts["norm_ffn2"][0], parts["norm_final"],
                parts["q_norm"][0], parts["kv_norm"][0], jnp.pad(parts["sinks"][0], (0, 120)),
                parts["rel_bias"].reshape(NUM_BUCKETS * SWA_HEADS), parts["b_mod"][0]]
        return jnp.concatenate(flat).reshape(-1, 128)

    w_small = pack(dict(norm_ffn1=norm_ffn1, norm_mix=norm_mix, norm_ffn2=norm_ffn2, norm_final=norm_final,
                        q_norm=q_norm, kv_norm=kv_norm, sinks=sinks, rel_bias=rel_bias, b_mod=b_mod))
    m_small = pack(dict(norm_ffn1=m_norm_ffn1, norm_mix=m_norm_mix, norm_ffn2=m_norm_ffn2, norm_final=m_norm_final,
                        q_norm=m_q_norm, kv_norm=m_kv_norm, sinks=m_sinks, rel_bias=m_rel_bias, b_mod=m_b_mod))
    v_small = pack(dict(norm_ffn1=v_norm_ffn1, norm_mix=v_norm_mix, norm_ffn2=v_norm_ffn2, norm_final=v_norm_final,
                        q_norm=v_q_norm, kv_norm=v_kv_norm, sinks=v_sinks, rel_bias=v_rel_bias, b_mod=v_b_mod))
    packed = (g_small.reshape(-1, 128),) + tuple(_adamw(w_small, g_small.reshape(-1, 128), m_small, v_small,
                                                        name="adamw_small"))

    def unpack(flat):
        flat = flat.reshape(-1)
        out, off = {}, 0
        for nm, width in SMALL_LAYOUT + (("b_mod", N_DEV * WM),):
            out[nm] = flat[off:off + width]
            off += width
        return {"norm_ffn1": out["norm_ffn1"][None], "norm_mix": out["norm_mix"][None],
                "norm_ffn2": out["norm_ffn2"][None], "norm_final": out["norm_final"],
                "q_norm": out["q_norm"][None], "kv_norm": out["kv_norm"][None], "sinks": out["sinks"][None, :SWA_HEADS],
                "rel_bias": out["rel_bias"].reshape(NUM_BUCKETS, SWA_HEADS), "b_mod": out["b_mod"][None]}

    small_res = [unpack(p) for p in packed]
    order = ("w_mod", "b_mod", "norm_ffn1", "ffn1_gate", "ffn1_up", "ffn1_down", "norm_mix", "w_in", "q_norm",
             "kv_norm", "w_uq", "w_ukv", "sinks", "w_o", "norm_ffn2", "ffn2_gate", "ffn2_up", "ffn2_down",
             "rel_bias", "norm_final")
    outs = [loss, grad_x[None]]
    for kind in range(4):
        for nm in order:
            outs.append(res[nm][kind] if nm in res else small_res[kind][nm])
    return tuple(outs)
```

```python
import functools
import math

import numpy as np
import jax
import jax.numpy as jnp
from jax import lax
from jax.experimental import pallas as pl
from jax.experimental.pallas import tpu as pltpu

F32 = jnp.float32
BF16 = jnp.bfloat16
MESH = pl.DeviceIdType.MESH

N_DEV = 8
D = 1024
D_FF = 2816
EPS = 1e-6
N_MOD = 9
SWA_HEADS = 8
SWA_DH = 64
WINDOW = 128
MLA_HEADS = 4
MLA_NOPE = 128
MLA_ROPE = 64
MLA_V = 128
MLA_QR = 256
MLA_KVR = 128
ROPE_THETA = 10000.0
NUM_BUCKETS = 32
D_IN = 1216
D_IN_PAD = 1280
SWA_SCALE = SWA_DH ** -0.5
MLA_SCALE = (MLA_NOPE + MLA_ROPE) ** -0.5

ADAM_LR = 0.001
ADAM_B1 = 0.9
ADAM_B2 = 0.999
ADAM_EPS = 1e-08
ADAM_WD = 0.01
ADAM_STEP = 10

V7X_VMEM_LIMIT = 56 * 1024 * 1024

NT_DIMS = (((1,), (1,)), ((), ()))
TN_DIMS = (((0,), (0,)), ((), ()))


def _dot(a, b):
    return jnp.dot(a, b, preferred_element_type=F32)


def _dot_nt(a, b):
    return lax.dot_general(a, b, NT_DIMS, preferred_element_type=F32)


def _dot_tn(a, b):
    return lax.dot_general(a, b, TN_DIMS, preferred_element_type=F32)


def _params(sem=None):
    return pltpu.CompilerParams(dimension_semantics=sem, vmem_limit_bytes=V7X_VMEM_LIMIT)


def _rstd(x):
    return lax.rsqrt(jnp.mean(x * x, axis=-1, keepdims=True) + EPS)


def _rms_bwd(dy, xhat, r):
    return r * (dy - xhat * jnp.mean(dy * xhat, axis=-1, keepdims=True))


def _sigmoid(a):
    return 1.0 / (1.0 + jnp.exp(-a))


def _ffn_fwd(x, vecs, wgT, wuT, wd, *, name, tm=512, tf=256):
    S = x.shape[0]
    tm = min(tm, S)
    ni, nj = S // tm, D_FF // tf

    def body(x_ref, vec_ref, wg_ref, wu_ref, wd_ref, xo_ref, h_ref, a_ref, b_ref, f_ref, acc_ref):
        j = pl.program_id(1)

        @pl.when(j == 0)
        def _():
            xv = x_ref[...]
            hn = xv * _rstd(xv) * vec_ref[0:1, :]
            h_ref[...] = (hn * (1.0 + vec_ref[2:3, :]) + vec_ref[1:2, :]).astype(BF16)
            acc_ref[...] = jnp.zeros_like(acc_ref)

        h = h_ref[...]
        a = _dot_nt(h, wg_ref[...])
        b = _dot_nt(h, wu_ref[...])
        a_ref[...] = a.astype(BF16)
        b_ref[...] = b.astype(BF16)
        hsw = (a * _sigmoid(a) * b).astype(BF16)
        acc_ref[...] += _dot(hsw, wd_ref[...])

        @pl.when(j == nj - 1)
        def _():
            f = acc_ref[...]
            f_ref[...] = f
            xo_ref[...] = x_ref[...] + (0.5 * vec_ref[3:4, :]) * f

    row = pl.BlockSpec((tm, D), lambda i, j: (i, 0))
    wspec = pl.BlockSpec((tf, D), lambda i, j: (j, 0))
    act = pl.BlockSpec((tm, tf), lambda i, j: (i, j))
    return pl.pallas_call(
        body, name=name, grid=(ni, nj),
        in_specs=[row, pl.BlockSpec((8, D), lambda i, j: (0, 0)), wspec, wspec, wspec],
        out_specs=[row, row, act, act, row],
        out_shape=[jax.ShapeDtypeStruct((S, D), F32), jax.ShapeDtypeStruct((S, D), BF16),
                   jax.ShapeDtypeStruct((S, D_FF), BF16), jax.ShapeDtypeStruct((S, D_FF), BF16),
                   jax.ShapeDtypeStruct((S, D), F32)],
        scratch_shapes=[pltpu.VMEM((tm, D), F32)],
        compiler_params=_params(("parallel", "arbitrary")),
    )(x, vecs, wgT, wuT, wd)


def _ffn_bwd_pre(dxo, f, vecs, *, name, tm=256):
    S = dxo.shape[0]

    def body(dx_ref, f_ref, vec_ref, df_ref, part_ref):
        @pl.when(pl.program_id(0) == 0)
        def _():
            part_ref[...] = jnp.zeros_like(part_ref)

        dx = dx_ref[...]
        df_ref[...] = ((0.5 * vec_ref[3:4, :]) * dx).astype(BF16)
        part_ref[0:1, :] += 0.5 * jnp.sum(dx * f_ref[...], axis=0, keepdims=True)

    row = pl.BlockSpec((tm, D), lambda i: (i, 0))
    vec = pl.BlockSpec((8, D), lambda i: (0, 0))
    return pl.pallas_call(
        body, name=name, grid=(S // tm,), in_specs=[row, row, vec], out_specs=[row, vec],
        out_shape=[jax.ShapeDtypeStruct((S, D), BF16), jax.ShapeDtypeStruct((8, D), F32)],
        compiler_params=_params(("arbitrary",)),
    )(dxo, f, vecs)


def _ffn_bwd_main(h, df, a, b, wgT, wuT, wd, *, name, tm=512, tf=256):
    S = h.shape[0]
    tm = min(tm, S)
    ni, nj = S // tm, D_FF // tf

    def body(h_hbm, df_hbm, a_ref, b_ref, wg_ref, wu_ref, wd_ref,
             gg_ref, gu_ref, gd_ref, dh_hbm,
             h_v, df_v, dh_v, gg_acc, gu_acc, gd_acc, sem):
        j = pl.program_id(0)
        i = pl.program_id(1)

        @pl.when((j == 0) & (i == 0))
        def _():
            c1 = pltpu.make_async_copy(h_hbm, h_v, sem.at[0])
            c2 = pltpu.make_async_copy(df_hbm, df_v, sem.at[1])
            c1.start()
            c2.start()
            c1.wait()
            c2.wait()

        @pl.when(i == 0)
        def _():
            gg_acc[...] = jnp.zeros_like(gg_acc)
            gu_acc[...] = jnp.zeros_like(gu_acc)
            gd_acc[...] = jnp.zeros_like(gd_acc)

        rows = pl.ds(pl.multiple_of(i * tm, tm), tm)
        hi = h_v[rows, :]
        dfi = df_v[rows, :]
        av = a_ref[...].astype(F32)
        bv = b_ref[...].astype(F32)
        sg = _sigmoid(av)
        sa = av * sg
        hsw = (sa * bv).astype(BF16)
        dhsw = _dot_nt(dfi, wd_ref[...])
        da = (dhsw * bv * (sg * (1.0 + av * (1.0 - sg)))).astype(BF16)
        db = (dhsw * sa).astype(BF16)
        gd_acc[...] += _dot_tn(hsw, dfi)
        gg_acc[...] += _dot_tn(da, hi)
        gu_acc[...] += _dot_tn(db, hi)
        dh = _dot(da, wg_ref[...]) + _dot(db, wu_ref[...])

        @pl.when(j == 0)
        def _():
            dh_v[rows, :] = dh

        @pl.when(j > 0)
        def _():
            dh_v[rows, :] += dh

        @pl.when(i == ni - 1)
        def _():
            gg_ref[...] = gg_acc[...].astype(BF16)
            gu_ref[...] = gu_acc[...].astype(BF16)
            gd_ref[...] = gd_acc[...].astype(BF16)

        @pl.when((j == nj - 1) & (i == ni - 1))
        def _():
            c3 = pltpu.make_async_copy(dh_v, dh_hbm, sem.at[2])
            c3.start()
            c3.wait()

    anyspec = pl.BlockSpec(memory_space=pl.ANY)
    wspec = pl.BlockSpec((tf, D), lambda j, i: (j, 0))
    act = pl.BlockSpec((tm, tf), lambda j, i: (i, j))
    return pl.pallas_call(
        body, name=name, grid=(nj, ni),
        in_specs=[anyspec, anyspec, act, act, wspec, wspec, wspec],
        out_specs=[wspec, wspec, wspec, anyspec],
        out_shape=[jax.ShapeDtypeStruct((D_FF, D), BF16)] * 3 + [jax.ShapeDtypeStruct((S, D), F32)],
        scratch_shapes=[pltpu.VMEM((S, D), BF16), pltpu.VMEM((S, D), BF16), pltpu.VMEM((S, D), F32),
                        pltpu.VMEM((tf, D), F32), pltpu.VMEM((tf, D), F32), pltpu.VMEM((tf, D), F32),
                        pltpu.SemaphoreType.DMA((3,))],
        compiler_params=_params(("arbitrary", "arbitrary")),
    )(h, df, a, b, wgT, wuT, wd)


def _norm_bwd(dh, x, dxo, vecs, *, name, tm=256):
    S = x.shape[0]

    def body(dh_ref, x_ref, dxo_ref, vec_ref, dx_ref, part_ref):
        @pl.when(pl.program_id(0) == 0)
        def _():
            part_ref[...] = jnp.zeros_like(part_ref)

        dh = dh_ref[...]
        xv = x_ref[...]
        r = _rstd(xv)
        xhat = xv * r
        w = vec_ref[0:1, :]
        xn = xhat * w
        dxn = dh * (1.0 + vec_ref[2:3, :])
        part_ref[0:1, :] += jnp.sum(dxn * xhat, axis=0, keepdims=True)
        part_ref[1:2, :] += jnp.sum(dh, axis=0, keepdims=True)
        part_ref[2:3, :] += jnp.sum(dh * xn, axis=0, keepdims=True)
        dx_ref[...] = dxo_ref[...] + _rms_bwd(dxn * w, xhat, r)

    row = pl.BlockSpec((tm, D), lambda i: (i, 0))
    vec = pl.BlockSpec((8, D), lambda i: (0, 0))
    return pl.pallas_call(
        body, name=name, grid=(S // tm,), in_specs=[row, row, row, vec], out_specs=[row, vec],
        out_shape=[jax.ShapeDtypeStruct((S, D), F32), jax.ShapeDtypeStruct((8, D), F32)],
        compiler_params=_params(("arbitrary",)),
    )(dh, x, dxo, vecs)


def _head(x, tgt, nf, *, tm=256):
    S = x.shape[0]

    def body(x_ref, t_ref, nf_ref, dx_ref, part_ref):
        @pl.when(pl.program_id(0) == 0)
        def _():
            part_ref[...] = jnp.zeros_like(part_ref)

        xv = x_ref[...]
        r = _rstd(xv)
        xhat = xv * r
        w = nf_ref[...]
        e = xhat * w - t_ref[...]
        dy = e * (1.0 / D)
        part_ref[0:1, :] += jnp.sum(dy * xhat, axis=0, keepdims=True)
        part_ref[1:2, :] += jnp.sum(e * e) * (0.5 / D)
        dx_ref[...] = _rms_bwd(dy * w, xhat, r)

    row = pl.BlockSpec((tm, D), lambda i: (i, 0))
    return pl.pallas_call(
        body, name="head", grid=(S // tm,),
        in_specs=[row, row, pl.BlockSpec((1, D), lambda i: (0, 0))],
        out_specs=[row, pl.BlockSpec((8, D), lambda i: (0, 0))],
        out_shape=[jax.ShapeDtypeStruct((S, D), F32), jax.ShapeDtypeStruct((8, D), F32)],
        compiler_params=_params(("arbitrary",)),
    )(x, tgt, nf)


def _mix_in_fwd(x, vecs, w_inT, *, tm=256):
    S = x.shape[0]

    def body(x_ref, vec_ref, w_ref, h_ref, p_ref):
        xv = x_ref[...]
        hn = xv * _rstd(xv) * vec_ref[0:1, :]
        h = (hn * (1.0 + vec_ref[2:3, :]) + vec_ref[1:2, :]).astype(BF16)
        h_ref[...] = h
        p_ref[...] = _dot_nt(h, w_ref[...])

    row = pl.BlockSpec((tm, D), lambda i: (i, 0))
    return pl.pallas_call(
        body, name="mix_in_fwd", grid=(S // tm,),
        in_specs=[row, pl.BlockSpec((8, D), lambda i: (0, 0)), pl.BlockSpec((D_IN_PAD, D), lambda i: (0, 0))],
        out_specs=[row, pl.BlockSpec((tm, D_IN_PAD), lambda i: (i, 0))],
        out_shape=[jax.ShapeDtypeStruct((S, D), BF16), jax.ShapeDtypeStruct((S, D_IN_PAD), F32)],
        compiler_params=_params(("parallel",)),
    )(x, vecs, w_inT)


def _bucket_table():
    qi = np.arange(WINDOW)[:, None]
    kj = np.arange(2 * WINDOW)[None, :]
    dist = qi + WINDOW - kj
    max_exact = NUM_BUCKETS // 2
    n = np.maximum(dist, 0)
    nf = np.maximum(n, 1).astype(np.float32)
    large = max_exact + (np.log(nf / np.float32(max_exact)) / np.float32(math.log(WINDOW / max_exact))
                         * np.float32(NUM_BUCKETS - max_exact)).astype(np.int32)
    large = np.minimum(large, NUM_BUCKETS - 1)
    return np.where(n < max_exact, n, large).astype(np.int32)


def _bias_build(rel_bias, bucket):
    def body(rb_ref, bk_ref, out_ref):
        bk = bk_ref[...]
        for h in range(SWA_HEADS):
            acc = jnp.zeros((WINDOW, 2 * WINDOW), F32)
            for b in range(NUM_BUCKETS):
                acc = jnp.where(bk == b, rb_ref[b, h], acc)
            out_ref[h] = acc

    return pl.pallas_call(
        body, name="bias_build",
        in_specs=[pl.BlockSpec(memory_space=pltpu.SMEM), pl.BlockSpec(memory_space=pltpu.VMEM)],
        out_specs=pl.BlockSpec(memory_space=pltpu.VMEM),
        out_shape=jax.ShapeDtypeStruct((SWA_HEADS, WINDOW, 2 * WINDOW), F32),
    )(rel_bias, bucket)


def _swa_valid(n):
    row = lax.broadcasted_iota(jnp.int32, (WINDOW, 2 * WINDOW), 0)
    col = lax.broadcasted_iota(jnp.int32, (WINDOW, 2 * WINDOW), 1)
    dist = row + WINDOW - col
    return (dist >= 0) & (dist < WINDOW) & ((col >= WINDOW) | (n > 0))


def _swa_probs(qh, kk, bias_h, sink, valid):
    s = _dot_nt(qh, kk) * SWA_SCALE + bias_h
    s = jnp.where(valid, s, -jnp.inf)
    m = jnp.maximum(jnp.max(s, axis=-1, keepdims=True), sink)
    p = jnp.exp(s - m)
    ps = jnp.exp(sink - m)
    inv = 1.0 / (jnp.sum(p, axis=-1, keepdims=True) + ps)
    return p * inv, ps * inv


def _swa_specs():
    prev = lambda n: jnp.maximum(n - 1, 0)
    return [pl.BlockSpec((WINDOW, 512), lambda n: (n, 0)),
            pl.BlockSpec((WINDOW, 128), lambda n: (n, 4)),
            pl.BlockSpec((WINDOW, 128), lambda n: (prev(n), 4)),
            pl.BlockSpec((WINDOW, 128), lambda n: (n, 5)),
            pl.BlockSpec((WINDOW, 128), lambda n: (prev(n), 5)),
            pl.BlockSpec((SWA_HEADS, WINDOW, 2 * WINDOW), lambda n: (0, 0, 0)),
            pl.BlockSpec(memory_space=pltpu.SMEM)]


def _swa_fwd(proj, bias, sinks):
    S = proj.shape[0]

    def body(q_ref, kc_ref, kp_ref, vc_ref, vp_ref, bias_ref, sink_ref, o_ref):
        valid = _swa_valid(pl.program_id(0))
        q = q_ref[...].astype(BF16)
        kfull = jnp.concatenate([kp_ref[...], kc_ref[...]], axis=0).astype(BF16)
        vfull = jnp.concatenate([vp_ref[...], vc_ref[...]], axis=0).astype(BF16)
        outs = []
        for h in range(SWA_HEADS):
            g = h // 4
            kk = kfull[:, 64 * g:64 * g + 64]
            vv = vfull[:, 64 * g:64 * g + 64]
            pk, _ = _swa_probs(q[:, 64 * h:64 * h + 64], kk, bias_ref[h], sink_ref[0, h], valid)
            outs.append(_dot(pk.astype(BF16), vv))
        o_ref[...] = jnp.concatenate(outs, axis=1)

    return pl.pallas_call(
        body, name="swa_fwd", grid=(S // WINDOW,),
        in_specs=_swa_specs(),
        out_specs=pl.BlockSpec((WINDOW, 512), lambda n: (n, 0)),
        out_shape=jax.ShapeDtypeStruct((S, 512), F32),
        compiler_params=_params(("parallel",)),
    )(proj, proj, proj, proj, proj, bias, sinks)


def _swa_bwd(proj, bias, sinks, do, bucket):
    S = proj.shape[0]
    nb = S // WINDOW

    def body(q_ref, kc_ref, kp_ref, vc_ref, vp_ref, bias_ref, sink_ref, do_ref, bk_ref,
             dq_ref, dk_ref, dv_ref, drb_ref, dsk_ref, dbias_acc):
        n = pl.program_id(0)

        @pl.when(n == 0)
        def _():
            dk_ref[...] = jnp.zeros_like(dk_ref)
            dv_ref[...] = jnp.zeros_like(dv_ref)
            dsk_ref[...] = jnp.zeros_like(dsk_ref)
            dbias_acc[...] = jnp.zeros_like(dbias_acc)
            drb_ref[...] = jnp.zeros_like(drb_ref)

        valid = _swa_valid(n)
        q = q_ref[...].astype(BF16)
        dov = do_ref[...]
        kfull = jnp.concatenate([kp_ref[...], kc_ref[...]], axis=0).astype(BF16)
        vfull = jnp.concatenate([vp_ref[...], vc_ref[...]], axis=0).astype(BF16)
        prow = pl.ds(pl.multiple_of(jnp.maximum(n - 1, 0) * WINDOW, WINDOW), WINDOW)
        crow = pl.ds(pl.multiple_of(n * WINDOW, WINDOW), WINDOW)
        dqs = []
        for g in range(2):
            kk = kfull[:, 64 * g:64 * g + 64]
            vv = vfull[:, 64 * g:64 * g + 64]
            dkk = jnp.zeros((2 * WINDOW, SWA_DH), F32)
            dvv = jnp.zeros((2 * WINDOW, SWA_DH), F32)
            for h in range(4 * g, 4 * g + 4):
                qh = q[:, 64 * h:64 * h + 64]
                pk, psink = _swa_probs(qh, kk, bias_ref[h], sink_ref[0, h], valid)
                pkb = pk.astype(BF16)
                o = _dot(pkb, vv)
                doh = dov[:, 64 * h:64 * h + 64]
                dob = doh.astype(BF16)
                dp = _dot_nt(dob, vv)
                delta = jnp.sum(doh * o, axis=-1, keepdims=True)
                ds = pk * (dp - delta)
                dsk_ref[h:h + 1, :] += jnp.broadcast_to(jnp.sum(-psink * delta, keepdims=True), (1, 128))
                dbias_acc[h] += ds
                dsb = (ds * SWA_SCALE).astype(BF16)
                dqs.append(_dot(dsb, kk))
                dkk += _dot_tn(dsb, qh)
                dvv += _dot_tn(pkb, dob)
            dk_ref[prow, 64 * g:64 * g + 64] += dkk[:WINDOW]
            dk_ref[crow, 64 * g:64 * g + 64] += dkk[WINDOW:]
            dv_ref[prow, 64 * g:64 * g + 64] += dvv[:WINDOW]
            dv_ref[crow, 64 * g:64 * g + 64] += dvv[WINDOW:]
        dq_ref[...] = jnp.concatenate(dqs, axis=1)

        @pl.when(n == nb - 1)
        def _():
            bk = bk_ref[...]
            for h in range(SWA_HEADS):
                dbh = dbias_acc[h]
                for b in range(NUM_BUCKETS):
                    val = jnp.sum(jnp.where(bk == b, dbh, 0.0), keepdims=True)
                    drb_ref[b * 8 + h:b * 8 + h + 1, :] = jnp.broadcast_to(val, (1, 128))

    full = lambda shape: pl.BlockSpec(shape, lambda n: tuple(0 for _ in shape))
    return pl.pallas_call(
        body, name="swa_bwd", grid=(nb,),
        in_specs=_swa_specs() + [pl.BlockSpec((WINDOW, 512), lambda n: (n, 0)), full((WINDOW, 2 * WINDOW))],
        out_specs=[pl.BlockSpec((WINDOW, 512), lambda n: (n, 0)), full((S, 128)), full((S, 128)),
                   full((NUM_BUCKETS * 8, 128)), full((8, 128))],
        out_shape=[jax.ShapeDtypeStruct((S, 512), F32), jax.ShapeDtypeStruct((S, 128), F32),
                   jax.ShapeDtypeStruct((S, 128), F32), jax.ShapeDtypeStruct((NUM_BUCKETS * 8, 128), F32),
                   jax.ShapeDtypeStruct((8, 128), F32)],
        scratch_shapes=[pltpu.VMEM((SWA_HEADS, WINDOW, 2 * WINDOW), F32)],
        compiler_params=_params(("arbitrary",)),
    )(proj, proj, proj, proj, proj, bias, sinks, do, bucket)


def _rope_tables(S):
    inv = ROPE_THETA ** (-jnp.arange(0, MLA_ROPE, 2, dtype=F32) / MLA_ROPE)
    ang = jnp.arange(S, dtype=F32)[:, None] * inv[None, :]
    cos, sin = jnp.cos(ang), jnp.sin(ang)
    return jnp.tile(jnp.concatenate([cos, cos], axis=1), (1, 4)), jnp.tile(jnp.concatenate([-sin, sin], axis=1), (1, 4))


def _swap_halves(x):
    w = x.shape[-1]
    lane = lax.broadcasted_iota(jnp.int32, x.shape, x.ndim - 1)
    return jnp.where((lane % 64) < 32, pltpu.roll(x, w - 32, x.ndim - 1), pltpu.roll(x, 32, x.ndim - 1))


def _mla_pre_fwd(proj, qn_w, kvn_w, wuqT, wukv, cos, sin, *, tm=256):
    S = proj.shape[0]

    def body(ql_ref, kl_ref, kr_ref, qw_ref, kw_ref, wuq_ref, wukv_ref, cos_ref, sin_ref,
             qc_ref, kc_ref, vv_ref):
        ql = ql_ref[...]
        qn = (ql * _rstd(ql) * qw_ref[...]).astype(BF16)
        q = _dot_nt(qn, wuq_ref[...])
        cs, sn = cos_ref[...], sin_ref[...]
        qr = q[:, 512:768]
        qr = qr * cs + _swap_halves(qr) * sn
        half = lax.broadcasted_iota(jnp.int32, (tm, 128), 1) // 64
        kl = kl_ref[...]
        kvn = (kl * _rstd(kl) * kw_ref[...]).astype(BF16)
        kr = kr_ref[...]
        kr = kr * cs[:, :128] + _swap_halves(kr) * sn[:, :128]
        kr2 = (kr + pltpu.roll(kr, 64, 1)).astype(BF16)
        for h in range(MLA_HEADS):
            qc_ref[h, :, 0:128] = q[:, 128 * h:128 * h + 128].astype(BF16)
            chunk = qr[:, 128 * (h // 2):128 * (h // 2) + 128]
            qc_ref[h, :, 128:256] = jnp.where(half == (h % 2), chunk, 0.0).astype(BF16)
            kc_ref[h, :, 0:128] = _dot(kvn, wukv_ref[2 * h]).astype(BF16)
            kc_ref[h, :, 128:256] = kr2
            vv_ref[h] = _dot(kvn, wukv_ref[2 * h + 1]).astype(BF16)

    const = lambda shape: pl.BlockSpec(shape, lambda i: tuple(0 for _ in shape))
    return pl.pallas_call(
        body, name="mla_pre_fwd", grid=(S // tm,),
        in_specs=[pl.BlockSpec((tm, 256), lambda i: (i, 3)), pl.BlockSpec((tm, 128), lambda i: (i, 8)),
                  pl.BlockSpec((tm, 128), lambda i: (i, 9)), const((1, 256)), const((1, 128)),
                  const((768, 256)), const((8, 128, 128)),
                  pl.BlockSpec((tm, 256), lambda i: (i, 0)), pl.BlockSpec((tm, 256), lambda i: (i, 0))],
        out_specs=[pl.BlockSpec((MLA_HEADS, tm, 256), lambda i: (0, i, 0)),
                   pl.BlockSpec((MLA_HEADS, tm, 256), lambda i: (0, i, 0)),
                   pl.BlockSpec((MLA_HEADS, tm, 128), lambda i: (0, i, 0))],
        out_shape=[jax.ShapeDtypeStruct((MLA_HEADS, S, 256), BF16), jax.ShapeDtypeStruct((MLA_HEADS, S, 256), BF16),
                   jax.ShapeDtypeStruct((MLA_HEADS, S, 128), BF16)],
        compiler_params=_params(("parallel",)),
    )(proj, proj, proj, qn_w, kvn_w, wuqT, wukv, cos, sin)


def _causal(i, j, t):
    row = i * t + lax.broadcasted_iota(jnp.int32, (t, t), 0)
    col = j * t + lax.broadcasted_iota(jnp.int32, (t, t), 1)
    return col <= row


def _mla_attn_fwd(qc, kc, vv, *, t=256):
    S = qc.shape[1]

    def body(q_ref, k_ref, v_ref, o_ref, l_ref):
        i = pl.program_id(1)
        q = q_ref[0]

        def step(j, carry):
            m, l, acc = carry
            rows = pl.ds(pl.multiple_of(j * t, t), t)
            s = _dot_nt(q, k_ref[0, rows, :]) * MLA_SCALE
            s = jnp.where(_causal(i, j, t), s, -jnp.inf)
            m_new = jnp.maximum(m, jnp.max(s, axis=-1, keepdims=True))
            alpha = jnp.exp(m - m_new)
            p = jnp.exp(s - m_new)
            l = alpha * l + jnp.sum(p, axis=-1, keepdims=True)
            acc = alpha * acc + _dot(p.astype(BF16), v_ref[0, rows, :])
            return m_new, l, acc

        init = (jnp.full((t, 1), -jnp.inf, F32), jnp.zeros((t, 1), F32), jnp.zeros((t, MLA_V), F32))
        m, l, acc = lax.fori_loop(0, i + 1, step, init)
        o_ref[...] = acc / l
        l_ref[0] = jnp.broadcast_to(m + jnp.log(l), (t, 128))

    return pl.pallas_call(
        body, name="mla_attn_fwd", grid=(MLA_HEADS, S // t),
        in_specs=[pl.BlockSpec((1, t, 256), lambda h, i: (h, i, 0)),
                  pl.BlockSpec((1, S, 256), lambda h, i: (h, 0, 0)),
                  pl.BlockSpec((1, S, 128), lambda h, i: (h, 0, 0))],
        out_specs=[pl.BlockSpec((t, 128), lambda h, i: (i, h)),
                   pl.BlockSpec((1, t, 128), lambda h, i: (h, i, 0))],
        out_shape=[jax.ShapeDtypeStruct((S, 512), F32), jax.ShapeDtypeStruct((MLA_HEADS, S, 128), F32)],
        compiler_params=_params(("parallel", "parallel")),
    )(qc, kc, vv)


def _mla_attn_bwd(qc, kc, vv, o, lse, do, *, t=256):
    S = qc.shape[1]
    nblk = S // t

    def body(q_ref, k_ref, v_ref, o_ref, l_ref, do_ref, dq_ref, dk_ref, dv_ref):
        j = pl.program_id(1)

        @pl.when(j == 0)
        def _():
            dq_ref[...] = jnp.zeros_like(dq_ref)

        k = k_ref[0]
        v = v_ref[0]

        def step(i, carry):
            dk, dv = carry
            rows = pl.ds(pl.multiple_of(i * t, t), t)
            q = q_ref[0, rows, :]
            dov = do_ref[rows, :]
            lrow = l_ref[0, rows, :][:, 0:1]
            s = _dot_nt(q, k) * MLA_SCALE
            p = jnp.where(_causal(i, j, t), jnp.exp(s - lrow), 0.0)
            dob = dov.astype(BF16)
            dv = dv + _dot_tn(p.astype(BF16), dob)
            dp = _dot_nt(dob, v)
            delta = jnp.sum(dov * o_ref[rows, :], axis=-1, keepdims=True)
            ds = (p * (dp - delta) * MLA_SCALE).astype(BF16)
            dk = dk + _dot_tn(ds, q)
            dq_ref[0, rows, :] += _dot(ds, k)
            return dk, dv

        dk, dv = lax.fori_loop(j, nblk, step, (jnp.zeros((t, 256), F32), jnp.zeros((t, MLA_V), F32)))
        dk_ref[0] = dk
        dv_ref[0] = dv

    return pl.pallas_call(
        body, name="mla_attn_bwd", grid=(MLA_HEADS, nblk),
        in_specs=[pl.BlockSpec((1, S, 256), lambda h, j: (h, 0, 0)),
                  pl.BlockSpec((1, t, 256), lambda h, j: (h, j, 0)),
                  pl.BlockSpec((1, t, 128), lambda h, j: (h, j, 0)),
                  pl.BlockSpec((S, 128), lambda h, j: (0, h)),
                  pl.BlockSpec((1, S, 128), lambda h, j: (h, 0, 0)),
                  pl.BlockSpec((S, 128), lambda h, j: (0, h))],
        out_specs=[pl.BlockSpec((1, S, 256), lambda h, j: (h, 0, 0)),
                   pl.BlockSpec((1, t, 256), lambda h, j: (h, j, 0)),
                   pl.BlockSpec((1, t, 128), lambda h, j: (h, j, 0))],
        out_shape=[jax.ShapeDtypeStruct((MLA_HEADS, S, 256), F32), jax.ShapeDtypeStruct((MLA_HEADS, S, 256), F32),
                   jax.ShapeDtypeStruct((MLA_HEADS, S, 128), F32)],
        compiler_params=_params(("parallel", "arbitrary")),
    )(qc, kc, vv, o, lse, do)


def _mla_pre_bwd(proj, qn_w, kvn_w, wuqT, wukv, cos, sin, dqc, dkc, dvv, *, tm=256):
    S = proj.shape[0]

    def body(ql_ref, kl_ref, qw_ref, kw_ref, wuq_ref, wukv_ref, cos_ref, sin_ref, dqc_ref, dkc_ref, dvv_ref,
             dql_ref, dkl_ref, dkr_ref, gq_ref, gkv_ref, part_ref):
        @pl.when(pl.program_id(0) == 0)
        def _():
            gq_ref[...] = jnp.zeros_like(gq_ref)
            gkv_ref[...] = jnp.zeros_like(gkv_ref)
            part_ref[...] = jnp.zeros_like(part_ref)

        cs, sn = cos_ref[...], sin_ref[...]
        half = lax.broadcasted_iota(jnp.int32, (tm, 128), 1) // 64
        ql = ql_ref[...]
        rq = _rstd(ql)
        qhat = ql * rq
        qw = qw_ref[...]
        qn = (qhat * qw).astype(BF16)
        chunks = []
        for pair in range(2):
            chunks.append(jnp.where(half == 0, dqc_ref[2 * pair, :, 128:256], dqc_ref[2 * pair + 1, :, 128:256]))
        dqr = jnp.concatenate(chunks, axis=1)
        dqr = dqr * cs + _swap_halves(dqr * sn)
        dq = jnp.concatenate([dqc_ref[h, :, 0:128] for h in range(MLA_HEADS)] + [dqr], axis=1).astype(BF16)
        gq_ref[...] += _dot_tn(dq, qn)
        dqn = _dot(dq, wuq_ref[...])
        part_ref[0:1, :] += jnp.sum(dqn * qhat, axis=0, keepdims=True)
        dql_ref[...] = _rms_bwd(dqn * qw, qhat, rq)
        kl = kl_ref[...]
        rk = _rstd(kl)
        khat = kl * rk
        kw = kw_ref[...]
        kvn = (khat * kw).astype(BF16)
        dkvn = jnp.zeros((tm, MLA_KVR), F32)
        dkr2 = jnp.zeros((tm, 128), F32)
        for h in range(MLA_HEADS):
            dkn = dkc_ref[h, :, 0:128].astype(BF16)
            dvh = dvv_ref[h].astype(BF16)
            gkv_ref[2 * h] += _dot_tn(kvn, dkn)
            gkv_ref[2 * h + 1] += _dot_tn(kvn, dvh)
            dkvn += _dot_nt(dkn, wukv_ref[2 * h]) + _dot_nt(dvh, wukv_ref[2 * h + 1])
            dkr2 += dkc_ref[h, :, 128:256]
        part_ref[1:2, 0:128] += jnp.sum(dkvn * khat, axis=0, keepdims=True)
        dkl_ref[...] = _rms_bwd(dkvn * kw, khat, rk)
        dkr = jnp.where(half == 0, dkr2 + pltpu.roll(dkr2, 64, 1), 0.0)
        dkr_ref[...] = dkr * cs[:, :128] + _swap_halves(dkr * sn[:, :128])

    const = lambda shape: pl.BlockSpec(shape, lambda i: tuple(0 for _ in shape))
    heads = lambda w: pl.BlockSpec((MLA_HEADS, tm, w), lambda i: (0, i, 0))
    return pl.pallas_call(
        body, name="mla_pre_bwd", grid=(S // tm,),
        in_specs=[pl.BlockSpec((tm, 256), lambda i: (i, 3)), pl.BlockSpec((tm, 128), lambda i: (i, 8)),
                  const((1, 256)), const((1, 128)), const((768, 256)), const((8, 128, 128)),
                  pl.BlockSpec((tm, 256), lambda i: (i, 0)), pl.BlockSpec((tm, 256), lambda i: (i, 0)),
                  heads(256), heads(256), heads(128)],
        out_specs=[pl.BlockSpec((tm, 256), lambda i: (i, 0)), pl.BlockSpec((tm, 128), lambda i: (i, 0)),
                   pl.BlockSpec((tm, 128), lambda i: (i, 0)), const((768, 256)), const((8, 128, 128)), const((8, 256))],
        out_shape=[jax.ShapeDtypeStruct((S, 256), F32), jax.ShapeDtypeStruct((S, 128), F32),
                   jax.ShapeDtypeStruct((S, 128), F32), jax.ShapeDtypeStruct((768, 256), F32),
                   jax.ShapeDtypeStruct((8, 128, 128), F32), jax.ShapeDtypeStruct((8, 256), F32)],
        compiler_params=_params(("arbitrary",)),
    )(proj, proj, qn_w, kvn_w, wuqT, wukv, cos, sin, dqc, dkc, dvv)


def _mix_out_fwd(x, oa, ob, w_o, vecs, *, tm=256):
    S = x.shape[0]

    def body(x_ref, oa_ref, ob_ref, w_ref, vec_ref, xo_ref, mo_ref):
        mo = _dot(oa_ref[...].astype(BF16), w_ref[0:512, :]) + _dot(ob_ref[...].astype(BF16), w_ref[512:1024, :])
        mo_ref[...] = mo
        xo_ref[...] = x_ref[...] + vec_ref[3:4, :] * mo

    row = pl.BlockSpec((tm, D), lambda i: (i, 0))
    half = pl.BlockSpec((tm, 512), lambda i: (i, 0))
    return pl.pallas_call(
        body, name="mix_out_fwd", grid=(S // tm,),
        in_specs=[row, half, half, pl.BlockSpec((D, D), lambda i: (0, 0)), pl.BlockSpec((8, D), lambda i: (0, 0))],
        out_specs=[row, row],
        out_shape=[jax.ShapeDtypeStruct((S, D), F32), jax.ShapeDtypeStruct((S, D), F32)],
        compiler_params=_params(("parallel",)),
    )(x, oa, ob, w_o, vecs)


def _mix_out_bwd(dxo, mo, oa, ob, w_o, vecs, *, tm=256):
    S = dxo.shape[0]

    def body(dx_ref, mo_ref, oa_ref, ob_ref, w_ref, vec_ref, doa_ref, dob_ref, gw_ref, part_ref):
        @pl.when(pl.program_id(0) == 0)
        def _():
            gw_ref[...] = jnp.zeros_like(gw_ref)
            part_ref[...] = jnp.zeros_like(part_ref)

        dx = dx_ref[...]
        part_ref[0:1, :] += jnp.sum(dx * mo_ref[...], axis=0, keepdims=True)
        dmo = (vec_ref[3:4, :] * dx).astype(BF16)
        doa_ref[...] = _dot_nt(dmo, w_ref[0:512, :])
        dob_ref[...] = _dot_nt(dmo, w_ref[512:1024, :])
        gw_ref[0:512, :] += _dot_tn(oa_ref[...].astype(BF16), dmo)
        gw_ref[512:1024, :] += _dot_tn(ob_ref[...].astype(BF16), dmo)

    row = pl.BlockSpec((tm, D), lambda i: (i, 0))
    half = pl.BlockSpec((tm, 512), lambda i: (i, 0))
    return pl.pallas_call(
        body, name="mix_out_bwd", grid=(S // tm,),
        in_specs=[row, row, half, half, pl.BlockSpec((D, D), lambda i: (0, 0)), pl.BlockSpec((8, D), lambda i: (0, 0))],
        out_specs=[half, half, pl.BlockSpec((D, D), lambda i: (0, 0)), pl.BlockSpec((8, D), lambda i: (0, 0))],
        out_shape=[jax.ShapeDtypeStruct((S, 512), F32), jax.ShapeDtypeStruct((S, 512), F32),
                   jax.ShapeDtypeStruct((D, D), F32), jax.ShapeDtypeStruct((8, D), F32)],
        compiler_params=_params(("arbitrary",)),
    )(dxo, mo, oa, ob, w_o, vecs)


def _mix_in_bwd(h, w_inT, dq, dk, dv, dql, dkl, dkr, *, tm=256):
    S = h.shape[0]
    offs = (0, 512, 640, 768, 1024, 1152)
    wid = (512, 128, 128, 256, 128, 128)

    def body(h_ref, w_ref, dq_ref, dk_ref, dv_ref, dql_ref, dkl_ref, dkr_ref, dh_ref, gw_ref):
        @pl.when(pl.program_id(0) == 0)
        def _():
            gw_ref[...] = jnp.zeros_like(gw_ref)

        hv = h_ref[...]
        dh = jnp.zeros((tm, D), F32)
        for ref, o, w in zip((dq_ref, dk_ref, dv_ref, dql_ref, dkl_ref, dkr_ref), offs, wid):
            dpart = ref[...].astype(BF16)
            dh += _dot(dpart, w_ref[o:o + w, :])
            gw_ref[o:o + w, :] += _dot_tn(dpart, hv)
        dh_ref[...] = dh

    row = pl.BlockSpec((tm, D), lambda i: (i, 0))
    part = lambda w: pl.BlockSpec((tm, w), lambda i: (i, 0))
    return pl.pallas_call(
        body, name="mix_in_bwd", grid=(S // tm,),
        in_specs=[row, pl.BlockSpec((D_IN_PAD, D), lambda i: (0, 0))] + [part(w) for w in wid],
        out_specs=[row, pl.BlockSpec((D_IN_PAD, D), lambda i: (0, 0))],
        out_shape=[jax.ShapeDtypeStruct((S, D), F32), jax.ShapeDtypeStruct((D_IN_PAD, D), F32)],
        compiler_params=_params(("arbitrary",)),
    )(h, w_inT, dq, dk, dv, dql, dkl, dkr)


def _vecs(norm_w, mod9, k):
    return jnp.concatenate([norm_w.reshape(1, D), mod9[3 * k:3 * k + 3], jnp.zeros((4, D), F32)], axis=0)


def _uq_group_rows(wuqT):
    per = MLA_NOPE + MLA_ROPE
    nope = [wuqT[per * h:per * h + MLA_NOPE] for h in range(MLA_HEADS)]
    rope = [wuqT[per * h + MLA_NOPE:per * (h + 1)] for h in range(MLA_HEADS)]
    return jnp.concatenate(nope + rope, axis=0)


def _uq_ungroup_rows(g):
    parts = []
    for h in range(MLA_HEADS):
        parts += [g[MLA_NOPE * h:MLA_NOPE * (h + 1)], g[512 + MLA_ROPE * h:512 + MLA_ROPE * (h + 1)]]
    return jnp.concatenate(parts, axis=0)


def _local_step(x, tgt, mod9, norms, sinks, rel_bias, q_norm, kv_norm, W, on_grads=None):
    if on_grads is None:
        on_grads = lambda group, grads, vecs: vecs
    S = x.shape[0]
    v1 = _vecs(norms["ffn1"], mod9, 0)
    v2 = _vecs(norms["mix"], mod9, 1)
    v3 = _vecs(norms["ffn2"], mod9, 2)
    bucket = jnp.asarray(_bucket_table())
    cos, sin = _rope_tables(S)
    if isinstance(W, dict):
        full, W = W, (lambda group, after, vecs: (full, vecs))

    W1, v1 = W("ffn1", [], v1)
    x1, h1, a1, b1, f1 = _ffn_fwd(x, v1, W1["g1T"], W1["u1T"], W1["d1"], name="ffn1_fwd")
    W2, v2 = W("mixer", [x1], v2)
    w_inT = jnp.pad(W2["w_inT"], ((0, D_IN_PAD - D_IN), (0, 0))).astype(BF16)
    wuqT = _uq_group_rows(W2["w_uqT"])
    h2, proj = _mix_in_fwd(x1, v2, w_inT)
    bias = _bias_build(rel_bias, bucket)
    oa = _swa_fwd(proj, bias, sinks)
    qc, kc, vv = _mla_pre_fwd(proj, q_norm, kv_norm, wuqT, W2["w_ukv"], cos, sin)
    ob, lse = _mla_attn_fwd(qc, kc, vv)
    _, v2o = W("ffn2_on_its_way", [ob], v2)
    x2, mo = _mix_out_fwd(x1, oa, ob, W2["w_o"], v2o)
    W3, v3 = W("ffn2", [x2], v3)
    x3, h3, a3, b3, f3 = _ffn_fwd(x2, v3, W3["g3T"], W3["u3T"], W3["d3"], name="ffn2_fwd")
    dx3, head_part = _head(x3, tgt, norms["final"])

    df3, g3_part = _ffn_bwd_pre(dx3, f3, v3, name="ffn2_bwd_pre")
    gg3, gu3, gd3, dh3 = _ffn_bwd_main(h3, df3, a3, b3, W3["g3T"], W3["u3T"], W3["d3"], name="ffn2_bwd")
    ffn2 = {"g3T": gg3, "u3T": gu3, "d3": gd3}
    v2 = on_grads("ffn2", ffn2, v2)
    dx2, n3_part = _norm_bwd(dh3, x2, dx3, v3, name="ffn2_norm_bwd")
    doa, dob, g_wo, g2_part = _mix_out_bwd(dx2, mo, oa, ob, W2["w_o"], v2)
    dq, dk, dv, drb, dsk = _swa_bwd(proj, bias, sinks, doa, bucket)
    dqc, dkc, dvv = _mla_attn_bwd(qc, kc, vv, ob, lse, dob)
    dql, dkl, dkr, g_uq, g_ukv, mla_part = _mla_pre_bwd(proj, q_norm, kv_norm, wuqT, W2["w_ukv"], cos, sin, dqc, dkc, dvv)
    dh2, g_win = _mix_in_bwd(h2, w_inT, dq, dk, dv, dql, dkl, dkr)
    mixer = {"w_inT": g_win[:D_IN], "w_uqT": _uq_ungroup_rows(g_uq).astype(BF16),
             "w_ukv": g_ukv.astype(BF16), "w_o": g_wo.astype(BF16)}
    v1 = on_grads("mixer", mixer, v1)
    dx1, n2_part = _norm_bwd(dh2, x1, dx2, v2, name="mix_norm_bwd")
    df1, g1_part = _ffn_bwd_pre(dx1, f1, v1, name="ffn1_bwd_pre")
    gg1, gu1, gd1, dh1 = _ffn_bwd_main(h1, df1, a1, b1, W1["g1T"], W1["u1T"], W1["d1"], name="ffn1_bwd")
    ffn1 = {"g1T": gg1, "u1T": gu1, "d1": gd1}
    v1 = on_grads("ffn1", ffn1, v1)
    dx0, n1_part = _norm_bwd(dh1, x, dx1, v1, name="ffn1_norm_bwd")

    grads = {**ffn1, **ffn2, **mixer}
    dmod9 = jnp.concatenate([n1_part[1:3], g1_part[0:1], n2_part[1:3], g2_part[0:1],
                             n3_part[1:3], g3_part[0:1]], axis=0)
    small = jnp.concatenate([n1_part[0], n2_part[0], n3_part[0], head_part[0], mla_part[0],
                             mla_part[1, :128], jnp.pad(dsk[:, 0], (0, 120)), drb[:, 0]])
    return head_part[1, 0], dx0, grads, small, dmod9


SMALL_LAYOUT = (("norm_ffn1", 1024), ("norm_mix", 1024), ("norm_ffn2", 1024), ("norm_final", 1024),
                ("q_norm", 256), ("kv_norm", 128), ("sinks", 128), ("rel_bias", 256))
N_SMALL = sum(n for _, n in SMALL_LAYOUT)


def _coords():
    return lax.axis_index("x"), lax.axis_index("y"), lax.axis_index("c")


def _flip(v, bit):
    return 1 - v if bit else v


def _peer(r):
    x, y, c = _coords()
    return (_flip(x, r & 4), _flip(y, r & 2), _flip(c, r & 1))


def _mod_fwd(c_tile, w_mod, b_mod3):
    W = w_mod.shape[1]

    def body(c_ref, w_ref, b_ref, mod_ref, ca_ref, call_ref, part_ref, send_sems, recv_sems):
        x, y, c = _coords()
        me = 4 * x + 2 * y + c
        call_ref[me] = c_ref[...]
        sends = []
        for r in range(1, N_DEV):
            cp = pltpu.make_async_remote_copy(c_ref, call_ref.at[me], send_sems.at[0, r], recv_sems.at[0, r],
                                              device_id=_peer(r), device_id_type=MESH)
            cp.start()
            sends.append(cp)
        for r in range(1, N_DEV):
            pltpu.make_async_remote_copy(c_ref, call_ref.at[me], send_sems.at[0, r], recv_sems.at[0, r],
                                         device_id=_peer(r), device_id_type=MESH).wait_recv()
        cv = call_ref[...].reshape(8 * N_DEV, D)
        ca = (cv * _sigmoid(cv)).astype(BF16)
        ca_ref[...] = ca
        part_ref[...] = _dot(ca, w_ref[...].astype(BF16)).reshape(N_DEV, 8, W)
        mod_ref[me] = part_ref[me] + b_ref[me]
        for r in range(1, N_DEV):
            cp = pltpu.make_async_remote_copy(part_ref.at[me ^ r], mod_ref.at[me], send_sems.at[1, r],
                                              recv_sems.at[1, r], device_id=_peer(r), device_id_type=MESH)
            cp.start()
            sends.append(cp)
        for r in range(1, N_DEV):
            pltpu.make_async_remote_copy(part_ref.at[me ^ r], mod_ref.at[me], send_sems.at[1, r],
                                         recv_sems.at[1, r], device_id=_peer(r), device_id_type=MESH).wait_recv()
            mod_ref[me ^ r] = mod_ref[me ^ r] + b_ref[me ^ r]
        for cp in sends:
            cp.wait_send()

    vm = pl.BlockSpec(memory_space=pltpu.VMEM)
    return pl.pallas_call(
        body, name="mod_fwd", in_specs=[vm, vm, vm], out_specs=[vm, vm],
        out_shape=[jax.ShapeDtypeStruct((N_DEV, 8, W), F32), jax.ShapeDtypeStruct((8 * N_DEV, D), BF16)],
        scratch_shapes=[pltpu.VMEM((N_DEV, 8, D), F32), pltpu.VMEM((N_DEV, 8, W), F32),
                        pltpu.SemaphoreType.DMA((2, N_DEV)), pltpu.SemaphoreType.DMA((2, N_DEV))],
        compiler_params=_params(),
    )(c_tile, w_mod, b_mod3)


def _mod_bwd(dmod3, small_tile, ca):
    W = dmod3.shape[2]

    def body(dm_ref, sm_ref, ca_ref, gw_ref, gb_ref, ssum_ref, dmcols, sm_all, gb_mine, send_sems, recv_sems):
        x, y, c = _coords()
        me = 4 * x + 2 * y + c
        dmcols[me] = dm_ref[me]
        sm_all[me] = sm_ref[...]
        sends = []
        for r in range(1, N_DEV):
            cp = pltpu.make_async_remote_copy(dm_ref.at[me ^ r], dmcols.at[me], send_sems.at[0, r], recv_sems.at[0, r],
                                              device_id=_peer(r), device_id_type=MESH)
            cp.start()
            sends.append(cp)
            cp = pltpu.make_async_remote_copy(sm_ref, sm_all.at[me], send_sems.at[1, r], recv_sems.at[1, r],
                                              device_id=_peer(r), device_id_type=MESH)
            cp.start()
            sends.append(cp)
        for r in range(1, N_DEV):
            pltpu.make_async_remote_copy(dm_ref.at[me ^ r], dmcols.at[me], send_sems.at[0, r], recv_sems.at[0, r],
                                         device_id=_peer(r), device_id_type=MESH).wait_recv()
            pltpu.make_async_remote_copy(sm_ref, sm_all.at[me], send_sems.at[1, r], recv_sems.at[1, r],
                                         device_id=_peer(r), device_id_type=MESH).wait_recv()
        dm = dmcols[...].reshape(8 * N_DEV, W)
        gw_ref[...] = _dot_tn(ca_ref[...], dm.astype(BF16))
        first_row = lax.broadcasted_iota(jnp.int32, (8, W), 0) == 0
        gb_mine[...] = jnp.where(first_row, jnp.sum(dm, axis=0, keepdims=True), 0.0)
        gb_ref[me] = gb_mine[...]
        for r in range(1, N_DEV):
            cp = pltpu.make_async_remote_copy(gb_mine, gb_ref.at[me], send_sems.at[2, r], recv_sems.at[2, r],
                                              device_id=_peer(r), device_id_type=MESH)
            cp.start()
            sends.append(cp)
        total = sm_all[0]
        for k in range(1, N_DEV):
            total = total + sm_all[k]
        ssum_ref[...] = total
        for r in range(1, N_DEV):
            pltpu.make_async_remote_copy(gb_mine, gb_ref.at[me], send_sems.at[2, r], recv_sems.at[2, r],
                                         device_id=_peer(r), device_id_type=MESH).wait_recv()
        for cp in sends:
            cp.wait_send()

    vm = pl.BlockSpec(memory_space=pltpu.VMEM)
    return pl.pallas_call(
        body, name="mod_bwd", in_specs=[vm, vm, vm], out_specs=[vm, vm, vm],
        out_shape=[jax.ShapeDtypeStruct((D, W), F32), jax.ShapeDtypeStruct((N_DEV, 8, W), F32),
                   jax.ShapeDtypeStruct((8, N_SMALL), F32)],
        scratch_shapes=[pltpu.VMEM((N_DEV, 8, W), F32), pltpu.VMEM((N_DEV, 8, N_SMALL), F32), pltpu.VMEM((8, W), F32),
                        pltpu.SemaphoreType.DMA((3, N_DEV)), pltpu.SemaphoreType.DMA((3, N_DEV))],
        compiler_params=_params(),
    )(dmod3, small_tile, ca)


def _wgather(shards):
    n = len(shards)

    def body(*refs):
        ins, outs, token = refs[:n], refs[n:2 * n], refs[2 * n]
        send_sems, recv_sems, local_sems = refs[2 * n + 1:]
        token[...] = jnp.zeros_like(token)
        x, y, c = _coords()
        me = 4 * x + 2 * y + c
        sib = (x, y, 1 - c)
        chips = [(1 - x, y), (x, 1 - y), (1 - x, 1 - y)]

        def copy(k, slot, block, to, src=None):
            return pltpu.make_async_remote_copy(
                src_ref=outs[k].at[block] if src is None else src, dst_ref=outs[k].at[block],
                send_sem=send_sems.at[k, slot], recv_sem=recv_sems.at[k, slot], device_id=to, device_id_type=MESH)

        local = [pltpu.make_async_copy(ins[k], outs[k].at[me], local_sems.at[k]) for k in range(n)]
        for cp in local:
            cp.start()
        first = []
        for k in range(n):
            first.append(copy(k, 0, me, sib, src=ins[k]))
            for j, chip in enumerate(chips):
                first.append(copy(k, 1 + j, me, (*chip, c), src=ins[k]))
        for cp in first:
            cp.start()
        passed = []
        for j, (cx, cy) in enumerate(chips):
            for k in range(n):
                blk = 4 * cx + 2 * cy + c
                copy(k, 1 + j, blk, sib).wait_recv()
                cp = copy(k, 4 + j, blk, sib)
                cp.start()
                passed.append(cp)
        for k in range(n):
            copy(k, 0, 4 * x + 2 * y + (1 - c), sib).wait_recv()
            for j, (cx, cy) in enumerate(chips):
                copy(k, 4 + j, 4 * cx + 2 * cy + (1 - c), sib).wait_recv()
        for cp in first + passed:
            cp.wait_send()
        for cp in local:
            cp.wait()

    anyspec = pl.BlockSpec(memory_space=pl.ANY)
    return pl.pallas_call(
        body, name="wgather", in_specs=[anyspec] * n,
        out_specs=[anyspec] * n + [pl.BlockSpec(memory_space=pltpu.VMEM)],
        out_shape=[jax.ShapeDtypeStruct((N_DEV,) + s.shape, s.dtype) for s in shards]
        + [jax.ShapeDtypeStruct((8, 128), F32)],
        scratch_shapes=[pltpu.SemaphoreType.DMA((n, 7)), pltpu.SemaphoreType.DMA((n, 7)),
                        pltpu.SemaphoreType.DMA((n,))],
    )(*shards)


class _GatherCopies:
    def __init__(self, lands, send_sems, recv_sems):
        x, y, c = _coords()
        me = 4 * x + 2 * y + c
        sib = (x, y, 1 - c)
        chips = [(1 - x, y), (x, 1 - y), (1 - x, 1 - y)]

        def copy(k, slot, block, to):
            return pltpu.make_async_remote_copy(
                src_ref=lands[k].at[block], dst_ref=lands[k].at[block],
                send_sem=send_sems.at[7 * k + slot], recv_sem=recv_sems.at[7 * k + slot],
                device_id=to, device_id_type=MESH)

        n = len(lands)
        self.first = [copy(k, 0, me, sib) for k in range(n)]
        self.first += [copy(k, 1 + j, me, (cx, cy, c)) for j, (cx, cy) in enumerate(chips) for k in range(n)]
        self.landed = [copy(k, 1 + j, 4 * cx + 2 * cy + c, sib) for j, (cx, cy) in enumerate(chips) for k in range(n)]
        self.passed = [copy(k, 4 + j, 4 * cx + 2 * cy + c, sib) for j, (cx, cy) in enumerate(chips) for k in range(n)]
        self.from_sib = [copy(k, 0, 4 * x + 2 * y + (1 - c), sib) for k in range(n)]
        self.from_sib += [copy(k, 4 + j, 4 * cx + 2 * cy + (1 - c), sib) for j, (cx, cy) in enumerate(chips)
                          for k in range(n)]


def _gather_start(lands, *, name):
    n = len(lands)

    def body(*refs):
        for cp in _GatherCopies(refs[:n], refs[n], refs[n + 1]).first:
            cp.start()
        refs[-1][...] = jnp.zeros_like(refs[-1])

    out = pl.pallas_call(
        body, name=name,
        out_shape=(pltpu.SemaphoreType.DMA((7 * n,)), pltpu.SemaphoreType.DMA((7 * n,)),
                   *[pltpu.HBM(l.shape, l.dtype) for l in lands], jax.ShapeDtypeStruct((8, 128), F32)),
        in_specs=[HBM_SPEC] * n,
        out_specs=(SEM_SPEC, SEM_SPEC, *[HBM_SPEC] * n, pl.BlockSpec(memory_space=pltpu.VMEM)),
        input_output_aliases={i: 2 + i for i in range(n)},
        compiler_params=pltpu.CompilerParams(has_side_effects=DATAFLOW),
    )(*[_in_hbm(l) for l in lands])
    return out[0], out[1], list(out[2:2 + n]), out[-1]


def _gather_pass(send_sems, recv_sems, lands, after, *, name, stage):
    n = len(lands)

    def body(*refs):
        cps = _GatherCopies(refs[:n], refs[n], refs[n + 1])
        if stage == "landed":
            for cp in cps.landed:
                cp.wait_recv()
        else:
            for cp in cps.passed:
                cp.start()
        refs[-1][...] = jnp.zeros_like(refs[-1])

    out = pl.pallas_call(
        body, name=name,
        out_shape=(*[pltpu.HBM(l.shape, l.dtype) for l in lands], jax.ShapeDtypeStruct((8, 128), F32)),
        in_specs=[HBM_SPEC] * n + [SEM_SPEC, SEM_SPEC] + [pl.BlockSpec(memory_space=pl.ANY)] * len(after),
        out_specs=(*[HBM_SPEC] * n, pl.BlockSpec(memory_space=pltpu.VMEM)),
        input_output_aliases={i: i for i in range(n)},
        compiler_params=pltpu.CompilerParams(has_side_effects=DATAFLOW),
    )(*lands, send_sems, recv_sems, *after)
    return list(out[:n]), out[-1]


def _gather_end(send_sems, recv_sems, lands, after, *, name):
    n = len(lands)

    def body(*refs):
        cps = _GatherCopies(refs[:n], refs[n], refs[n + 1])
        for cp in cps.from_sib:
            cp.wait_recv()
        for cp in cps.first + cps.passed:
            cp.wait_send()

    out = pl.pallas_call(
        body, name=name,
        out_shape=[pltpu.HBM(l.shape, l.dtype) for l in lands],
        in_specs=[HBM_SPEC] * n + [SEM_SPEC, SEM_SPEC] + [pl.BlockSpec(memory_space=pl.ANY)] * len(after),
        out_specs=[HBM_SPEC] * n,
        input_output_aliases={i: i for i in range(n)},
        compiler_params=pltpu.CompilerParams(has_side_effects=DATAFLOW),
    )(*lands, send_sems, recv_sems, *after)
    return list(out)


def _rs_d2d(grads, *, name):
    n = len(grads)

    def body(*refs):
        ins, outs = refs[:n], refs[n:2 * n]
        send_sems, recv_sems = refs[2 * n:]
        x, y, c = _coords()
        sib = (x, y, 1 - c)
        cps = []
        for k in range(n):
            for q in range(4):
                cps.append(pltpu.make_async_remote_copy(
                    src_ref=ins[k].at[2 * q + (1 - c)], dst_ref=outs[k].at[q],
                    send_sem=send_sems.at[k, q], recv_sem=recv_sems.at[k, q], device_id=sib, device_id_type=MESH))
        for cp in cps:
            cp.start()
        for cp in cps:
            cp.wait_recv()
        for cp in cps:
            cp.wait_send()

    anyspec = pl.BlockSpec(memory_space=pl.ANY)
    return pl.pallas_call(
        body, name=name, in_specs=[anyspec] * n, out_specs=[anyspec] * n,
        out_shape=[jax.ShapeDtypeStruct((4,) + g.shape[1:], g.dtype) for g in grads],
        scratch_shapes=[pltpu.SemaphoreType.DMA((n, 4)), pltpu.SemaphoreType.DMA((n, 4))],
    )(*grads)


def _chipsum(g, sib, cidx, *, name):
    _, r, cc = g.shape

    def body(c_ref, g_ref, s_ref, o_ref):
        o_ref[...] = (g_ref[...].astype(F32) + s_ref[...].astype(F32)).astype(o_ref.dtype)

    return pl.pallas_call(
        body, name=name,
        grid_spec=pltpu.PrefetchScalarGridSpec(
            num_scalar_prefetch=1, grid=(4,),
            in_specs=[pl.BlockSpec((1, r, cc), lambda q, c_ref: (2 * q + c_ref[0], 0, 0)),
                      pl.BlockSpec((1, r, cc), lambda q, c_ref: (q, 0, 0))],
            out_specs=pl.BlockSpec((1, r, cc), lambda q, c_ref: (q, 0, 0))),
        out_shape=jax.ShapeDtypeStruct((4, r, cc), g.dtype),
        compiler_params=_params(("arbitrary",)),
    )(cidx, g, sib)


HBM_SPEC = pl.BlockSpec(memory_space=pltpu.HBM)
SEM_SPEC = pl.BlockSpec(memory_space=pltpu.SEMAPHORE)
DATAFLOW = pltpu.SideEffectType.DATAFLOW_SIDE_EFFECTING


def _in_hbm(a):
    return pltpu.with_memory_space_constraint(a, pltpu.HBM)


def _ici_copies(sums, lands, send_sems, recv_sems):
    x, y, c = _coords()
    chips = [(1 - x, y), (x, 1 - y), (1 - x, 1 - y)]
    cps = []
    for k in range(len(sums)):
        for j, (cx, cy) in enumerate(chips):
            cps.append(pltpu.make_async_remote_copy(
                src_ref=sums[k].at[2 * cx + cy], dst_ref=lands[k].at[j],
                send_sem=send_sems.at[3 * k + j], recv_sem=recv_sems.at[3 * k + j],
                device_id=(cx, cy, c), device_id_type=MESH))
    return cps


def _rs_ici_start(sums, *, name):
    n = len(sums)

    def body(*refs):
        token = refs[-1]
        for cp in _ici_copies(refs[:n], refs[n:2 * n], refs[2 * n], refs[2 * n + 1]):
            cp.start()
        token[...] = jnp.zeros_like(token)

    lands = [_in_hbm(lax.empty((3,) + s.shape[1:], s.dtype)) for s in sums]
    out = pl.pallas_call(
        body, name=name,
        out_shape=(pltpu.SemaphoreType.DMA((3 * n,)), pltpu.SemaphoreType.DMA((3 * n,)),
                   *[pltpu.HBM(s.shape, s.dtype) for s in sums], *[pltpu.HBM(l.shape, l.dtype) for l in lands],
                   jax.ShapeDtypeStruct((8, 128), F32)),
        in_specs=[HBM_SPEC] * (2 * n),
        out_specs=(SEM_SPEC, SEM_SPEC, *[HBM_SPEC] * (2 * n), pl.BlockSpec(memory_space=pltpu.VMEM)),
        input_output_aliases={i: 2 + i for i in range(2 * n)},
        compiler_params=pltpu.CompilerParams(has_side_effects=DATAFLOW),
    )(*[_in_hbm(s) for s in sums], *lands)
    return out[0], out[1], list(out[2:2 + n]), list(out[2 + n:2 + 2 * n]), out[-1]


def _rs_ici_wait(send_sems, recv_sems, sums, lands, after, *, name):
    n = len(sums)

    def body(*refs):
        for cp in _ici_copies(refs[:n], refs[n:2 * n], refs[2 * n], refs[2 * n + 1]):
            cp.wait_send()
            cp.wait_recv()

    out = pl.pallas_call(
        body, name=name,
        out_shape=[pltpu.HBM(a.shape, a.dtype) for a in list(sums) + list(lands)],
        in_specs=[HBM_SPEC] * (2 * n) + [SEM_SPEC, SEM_SPEC] + [pl.BlockSpec(memory_space=pl.ANY)] * len(after),
        out_specs=[HBM_SPEC] * (2 * n),
        input_output_aliases={i: i for i in range(2 * n)},
        compiler_params=pltpu.CompilerParams(has_side_effects=DATAFLOW),
    )(*sums, *lands, send_sems, recv_sems, *after)
    return list(out[:n]), list(out[n:])


def _gsum(cs, rcv, qidx, *, name):
    _, r, cc = cs.shape

    def body(q_ref, c_ref, r_ref, o_ref):
        o_ref[...] = ((c_ref[0].astype(F32) + r_ref[0].astype(F32)) + r_ref[1].astype(F32)) + r_ref[2].astype(F32)

    return pl.pallas_call(
        body, name=name,
        grid_spec=pltpu.PrefetchScalarGridSpec(
            num_scalar_prefetch=1, grid=(1,),
            in_specs=[pl.BlockSpec((1, r, cc), lambda i, q_ref: (q_ref[0], 0, 0)),
                      pl.BlockSpec((3, r, cc), lambda i, q_ref: (0, 0, 0))],
            out_specs=pl.BlockSpec((r, cc), lambda i, q_ref: (0, 0))),
        out_shape=jax.ShapeDtypeStruct((r, cc), F32),
        compiler_params=_params(("arbitrary",)),
    )(qidx, cs, rcv)


def _adamw(w, g, m, v, *, name):
    R, C = w.shape
    tr = R if R <= 512 else 256
    c1 = 1.0 / (1.0 - ADAM_B1 ** ADAM_STEP)
    c2 = 1.0 / (1.0 - ADAM_B2 ** ADAM_STEP)

    def body(w_ref, g_ref, m_ref, v_ref, d_ref, nm_ref, nv_ref):
        gv = g_ref[...]
        m2 = ADAM_B1 * m_ref[...] + (1.0 - ADAM_B1) * gv
        v2 = ADAM_B2 * v_ref[...] + (1.0 - ADAM_B2) * (gv * gv)
        nm_ref[...] = m2
        nv_ref[...] = v2
        d_ref[...] = -ADAM_LR * ((m2 * c1) / (jnp.sqrt(v2 * c2) + ADAM_EPS) + ADAM_WD * w_ref[...])

    blk = pl.BlockSpec((tr, C), lambda i: (i, 0))
    return pl.pallas_call(
        body, name=name, grid=(R // tr,), in_specs=[blk] * 4, out_specs=[blk] * 3,
        out_shape=[jax.ShapeDtypeStruct((R, C), F32)] * 3,
        compiler_params=_params(("parallel",)),
    )(w, g, m, v)


TRANSPOSED = ("g1T", "u1T", "g3T", "u3T", "w_inT", "w_uqT")


def kernel(x, c, w_mod, b_mod, norm_ffn1, ffn1_gate, ffn1_up, ffn1_down, norm_mix, w_in, q_norm, kv_norm, w_uq, w_ukv, sinks, w_o, norm_ffn2, ffn2_gate, ffn2_up, ffn2_down, rel_bias, norm_final, loss_target, m_w_mod, m_b_mod, m_norm_ffn1, m_ffn1_gate, m_ffn1_up, m_ffn1_down, m_norm_mix, m_w_in, m_q_norm, m_kv_norm, m_w_uq, m_w_ukv, m_sinks, m_w_o, m_norm_ffn2, m_ffn2_gate, m_ffn2_up, m_ffn2_down, m_rel_bias, m_norm_final, v_w_mod, v_b_mod, v_norm_ffn1, v_ffn1_gate, v_ffn1_up, v_ffn1_down, v_norm_mix, v_w_in, v_q_norm, v_kv_norm, v_w_uq, v_w_ukv, v_sinks, v_w_o, v_norm_ffn2, v_ffn2_gate, v_ffn2_up, v_ffn2_down, v_rel_bias, v_norm_final):
    mx, my, mc = _coords()
    cidx = jnp.reshape(mc, (1,)).astype(jnp.int32)
    qidx = jnp.reshape(2 * mx + my, (1,)).astype(jnp.int32)
    WM = w_mod.shape[2]

    c_tile = jnp.pad(c, ((0, 7), (0, 0)))
    b_mod3 = jnp.pad(b_mod.reshape(N_DEV, 1, WM), ((0, 0), (0, 7), (0, 0)))
    mod3, ca = _mod_fwd(c_tile, w_mod[0], b_mod3)
    mod9 = mod3[:, 0, :].reshape(N_MOD, D)

    shards = {"g1T": ffn1_gate[0].T.astype(BF16), "u1T": ffn1_up[0].T.astype(BF16), "d1": ffn1_down[0].astype(BF16),
              "g3T": ffn2_gate[0].T.astype(BF16), "u3T": ffn2_up[0].T.astype(BF16), "d3": ffn2_down[0].astype(BF16),
              "w_inT": w_in[0].T, "w_uqT": w_uq[0].T.astype(BF16), "w_ukv": w_ukv[0].astype(BF16),
              "w_o": w_o[0].astype(BF16)}
    me = 4 * mx + 2 * my + mc
    groups = {"ffn1": ("g1T", "u1T", "d1"), "mixer": ("w_inT", "w_uqT", "w_ukv", "w_o"), "ffn2": ("g3T", "u3T", "d3")}
    arriving = {}

    def as_weights(group, gathered):
        return {k: g if k == "w_ukv" else g.reshape(N_DEV * g.shape[1], g.shape[2])
                for k, g in zip(groups[group], gathered)}

    def start_gather(group, token):
        lands = []
        for k in groups[group]:
            sh = shards[k] + token[0, 0].astype(shards[k].dtype)
            lands.append(lax.dynamic_update_slice(lax.empty((N_DEV,) + sh.shape, sh.dtype), sh[None], (me, 0, 0)))
        send, recv, lands, started = _gather_start(lands, name="gather_start_" + group)
        arriving[group] = (send, recv, lands)
        return started

    def fetch(group, after, vecs):
        if group == "ffn1":
            *gathered, token = _wgather([shards[k] for k in groups["ffn1"]])
            token = start_gather("ffn2", start_gather("mixer", token))
            return as_weights("ffn1", gathered), vecs + token[0:1, 0:1]
        def pass_on(group, after):
            send, recv, lands = arriving[group]
            lands, token = _gather_pass(send, recv, lands, after, name="gather_landed_" + group, stage="landed")
            lands, token = _gather_pass(send, recv, lands, [token], name="gather_onward_" + group, stage="onward")
            arriving[group] = (send, recv, lands)
            return token

        if group == "ffn2_on_its_way":
            return None, vecs + pass_on("ffn2", after)[0:1, 0:1]
        if group == "mixer":
            after = [pass_on("mixer", after)]
        send, recv, lands = arriving[group]
        return as_weights(group, _gather_end(send, recv, lands, after, name="gather_end_" + group)), vecs

    norms ={"ffn1": norm_ffn1, "mix": norm_mix, "ffn2": norm_ffn2, "final": norm_final.reshape(1, D)}
    in_flight = {}

    def on_grads(group, g, vecs):
        names = list(g)
        by_dest = [g[k] if k == "w_ukv" else g[k].reshape((N_DEV, g[k].shape[0] // N_DEV) + g[k].shape[1:])
                   for k in names]
        from_sib = _rs_d2d(by_dest, name="rs_d2d_" + group)
        sums = [_chipsum(a, s, cidx, name="chipsum_" + k) for k, a, s in zip(names, by_dest, from_sib)]
        send, recv, sums, lands, token = _rs_ici_start(sums, name="rs_ici_start_" + group)
        in_flight[group] = (names, send, recv, sums, lands, token)
        return vecs + token[0:1, 0:1]

    loss_local, grad_x, _, small, dmod9 = _local_step(
        x[0], loss_target[0], mod9, norms, sinks, rel_bias, q_norm, kv_norm, fetch, on_grads=on_grads)
    loss = lax.psum(loss_local, ("x", "y", "c"))

    gw = {}

    def finish(group, after):
        names, send, recv, sums, lands, _ = in_flight[group]
        sums, lands = _rs_ici_wait(send, recv, sums, lands, after, name="rs_ici_wait_" + group)
        for k, cs, rc in zip(names, sums, lands):
            g = _gsum(cs, rc, qidx, name="gsum_" + k)
            gw[k] = g.T if k in TRANSPOSED else g

    ffn1_started = in_flight["ffn1"][5]
    finish("ffn2", [ffn1_started])
    finish("mixer", [ffn1_started])

    dmod3 = jnp.pad(dmod9.reshape(N_DEV, 1, WM), ((0, 0), (0, 7), (0, 0)))
    small_tile = jnp.pad(small.reshape(1, N_SMALL), ((0, 7), (0, 0)))
    g_wmod, gb3, ssum = _mod_bwd(dmod3, small_tile, ca)
    g_small = jnp.concatenate([ssum[0], gb3[:, 0, :].reshape(N_DEV * WM)])

    res = {}

    def update(items):
        for k, (wk, gk, mk, vk) in items.items():
            d, nm, nv = _adamw(wk[0], gk, mk[0], vk[0], name="adamw_" + k)
            res[k] = tuple(a[None] for a in (gk, d, nm, nv))

    update({"w_mod": (w_mod, g_wmod, m_w_mod, v_w_mod),
            "w_in": (w_in, gw["w_inT"], m_w_in, v_w_in),
            "w_uq": (w_uq, gw["w_uqT"], m_w_uq, v_w_uq),
            "w_ukv": (w_ukv, gw["w_ukv"], m_w_ukv, v_w_ukv),
            "w_o": (w_o, gw["w_o"], m_w_o, v_w_o),
            "ffn2_gate": (ffn2_gate, gw["g3T"], m_ffn2_gate, v_ffn2_gate),
            "ffn2_up": (ffn2_up, gw["u3T"], m_ffn2_up, v_ffn2_up),
            "ffn2_down": (ffn2_down, gw["d3"], m_ffn2_down, v_ffn2_down)})
    finish("ffn1", [res[k][3] for k in res])
    update({"ffn1_gate": (ffn1_gate, gw["g1T"], m_ffn1_gate, v_ffn1_gate),
            "ffn1_up": (ffn1_up, gw["u1T"], m_ffn1_up, v_ffn1_up),
            "ffn1_down": (ffn1_down, gw["d1"], m_ffn1_down, v_ffn1_down)})

    def pack(parts):
        flat = [parts["norm_ffn1"][0], parts["norm_mix"][0], parts["norm_ffn2"][0], parts["norm_final"],
                parts["q_norm"][0], parts["kv_norm"][0], jnp.pad(parts["sinks"][0], (0, 120)),
                parts["rel_bias"].reshape(NUM_BUCKETS * SWA_HEADS), parts["b_mod"][0]]
        return jnp.concatenate(flat).reshape(-1, 128)

    w_small = pack(dict(norm_ffn1=norm_ffn1, norm_mix=norm_mix, norm_ffn2=norm_ffn2, norm_final=norm_final,
                        q_norm=q_norm, kv_norm=kv_norm, sinks=sinks, rel_bias=rel_bias, b_mod=b_mod))
    m_small = pack(dict(norm_ffn1=m_norm_ffn1, norm_mix=m_norm_mix, norm_ffn2=m_norm_ffn2, norm_final=m_norm_final,
                        q_norm=m_q_norm, kv_norm=m_kv_norm, sinks=m_sinks, rel_bias=m_rel_bias, b_mod=m_b_mod))
    v_small = pack(dict(norm_ffn1=v_norm_ffn1, norm_mix=v_norm_mix, norm_ffn2=v_norm_ffn2, norm_final=v_norm_final,
                        q_norm=v_q_norm, kv_norm=v_kv_norm, sinks=v_sinks, rel_bias=v_rel_bias, b_mod=v_b_mod))
    packed = (g_small.reshape(-1, 128),) + tuple(_adamw(w_small, g_small.reshape(-1, 128), m_small, v_small,
                                                        name="adamw_small"))

    def unpack(flat):
        flat = flat.reshape(-1)
        out, off = {}, 0
        for nm, width in SMALL_LAYOUT + (("b_mod", N_DEV * WM),):
            out[nm] = flat[off:off + width]
            off += width
        return {"norm_ffn1": out["norm_ffn1"][None], "norm_mix": out["norm_mix"][None],
                "norm_ffn2": out["norm_ffn2"][None], "norm_final": out["norm_final"],
                "q_norm": out["q_norm"][None], "kv_norm": out["kv_norm"][None], "sinks": out["sinks"][None, :SWA_HEADS],
                "rel_bias": out["rel_bias"].reshape(NUM_BUCKETS, SWA_HEADS), "b_mod": out["b_mod"][None]}

    small_res = [unpack(p) for p in packed]
    order = ("w_mod", "b_mod", "norm_ffn1", "ffn1_gate", "ffn1_up", "ffn1_down", "norm_mix", "w_in", "q_norm",
             "kv_norm", "w_uq", "w_ukv", "sinks", "w_o", "norm_ffn2", "ffn2_gate", "ffn2_up", "ffn2_down",
             "rel_bias", "norm_final")
    outs = [loss, grad_x[None]]
    for kind in range(4):
        for nm in order:
            outs.append(res[nm][kind] if nm in res else small_res[kind][nm])
    return tuple(outs)
```

```python
import functools
import math

import numpy as np
import jax
import jax.numpy as jnp
from jax import lax
from jax.experimental import pallas as pl
from jax.experimental.pallas import tpu as pltpu

F32 = jnp.float32
BF16 = jnp.bfloat16
MESH = pl.DeviceIdType.MESH

N_DEV = 8
D = 1024
D_FF = 2816
EPS = 1e-6
N_MOD = 9
SWA_HEADS = 8
SWA_DH = 64
WINDOW = 128
MLA_HEADS = 4
MLA_NOPE = 128
MLA_ROPE = 64
MLA_V = 128
MLA_QR = 256
MLA_KVR = 128
ROPE_THETA = 10000.0
NUM_BUCKETS = 32
D_IN = 1216
D_IN_PAD = 1280
SWA_SCALE = SWA_DH ** -0.5
MLA_SCALE = (MLA_NOPE + MLA_ROPE) ** -0.5

ADAM_LR = 0.001
ADAM_B1 = 0.9
ADAM_B2 = 0.999
ADAM_EPS = 1e-08
ADAM_WD = 0.01
ADAM_STEP = 10

V7X_VMEM_LIMIT = 56 * 1024 * 1024

NT_DIMS = (((1,), (1,)), ((), ()))
TN_DIMS = (((0,), (0,)), ((), ()))


def _dot(a, b):
    return jnp.dot(a, b, preferred_element_type=F32)


def _dot_nt(a, b):
    return lax.dot_general(a, b, NT_DIMS, preferred_element_type=F32)


def _dot_tn(a, b):
    return lax.dot_general(a, b, TN_DIMS, preferred_element_type=F32)


def _params(sem=None):
    return pltpu.CompilerParams(dimension_semantics=sem, vmem_limit_bytes=V7X_VMEM_LIMIT)


def _rstd(x):
    return lax.rsqrt(jnp.mean(x * x, axis=-1, keepdims=True) + EPS)


def _rms_bwd(dy, xhat, r):
    return r * (dy - xhat * jnp.mean(dy * xhat, axis=-1, keepdims=True))


def _sigmoid(a):
    return 1.0 / (1.0 + jnp.exp(-a))


def _ffn_fwd(x, vecs, wgT, wuT, wd, *, name, tm=512, tf=256):
    S = x.shape[0]
    tm = min(tm, S)
    ni, nj = S // tm, D_FF // tf

    def body(x_ref, vec_ref, wg_ref, wu_ref, wd_ref, xo_ref, h_ref, a_ref, b_ref, f_ref, acc_ref):
        j = pl.program_id(1)

        @pl.when(j == 0)
        def _():
            xv = x_ref[...]
            hn = xv * _rstd(xv) * vec_ref[0:1, :]
            h_ref[...] = (hn * (1.0 + vec_ref[2:3, :]) + vec_ref[1:2, :]).astype(BF16)
            acc_ref[...] = jnp.zeros_like(acc_ref)

        h = h_ref[...]
        a = _dot_nt(h, wg_ref[...])
        b = _dot_nt(h, wu_ref[...])
        a_ref[...] = a.astype(BF16)
        b_ref[...] = b.astype(BF16)
        hsw = (a * _sigmoid(a) * b).astype(BF16)
        acc_ref[...] += _dot(hsw, wd_ref[...])

        @pl.when(j == nj - 1)
        def _():
            f = acc_ref[...]
            f_ref[...] = f
            xo_ref[...] = x_ref[...] + (0.5 * vec_ref[3:4, :]) * f

    row = pl.BlockSpec((tm, D), lambda i, j: (i, 0))
    wspec = pl.BlockSpec((tf, D), lambda i, j: (j, 0))
    act = pl.BlockSpec((tm, tf), lambda i, j: (i, j))
    return pl.pallas_call(
        body, name=name, grid=(ni, nj),
        in_specs=[row, pl.BlockSpec((8, D), lambda i, j: (0, 0)), wspec, wspec, wspec],
        out_specs=[row, row, act, act, row],
        out_shape=[jax.ShapeDtypeStruct((S, D), F32), jax.ShapeDtypeStruct((S, D), BF16),
                   jax.ShapeDtypeStruct((S, D_FF), BF16), jax.ShapeDtypeStruct((S, D_FF), BF16),
                   jax.ShapeDtypeStruct((S, D), F32)],
        scratch_shapes=[pltpu.VMEM((tm, D), F32)],
        compiler_params=_params(("parallel", "arbitrary")),
    )(x, vecs, wgT, wuT, wd)


def _ffn_bwd_pre(dxo, f, vecs, *, name, tm=256):
    S = dxo.shape[0]

    def body(dx_ref, f_ref, vec_ref, df_ref, part_ref):
        @pl.when(pl.program_id(0) == 0)
        def _():
            part_ref[...] = jnp.zeros_like(part_ref)

        dx = dx_ref[...]
        df_ref[...] = ((0.5 * vec_ref[3:4, :]) * dx).astype(BF16)
        part_ref[0:1, :] += 0.5 * jnp.sum(dx * f_ref[...], axis=0, keepdims=True)

    row = pl.BlockSpec((tm, D), lambda i: (i, 0))
    vec = pl.BlockSpec((8, D), lambda i: (0, 0))
    return pl.pallas_call(
        body, name=name, grid=(S // tm,), in_specs=[row, row, vec], out_specs=[row, vec],
        out_shape=[jax.ShapeDtypeStruct((S, D), BF16), jax.ShapeDtypeStruct((8, D), F32)],
        compiler_params=_params(("arbitrary",)),
    )(dxo, f, vecs)


def _ffn_bwd_main(h, df, a, b, wgT, wuT, wd, *, name, tm=512, tf=256):
    S = h.shape[0]
    tm = min(tm, S)
    ni, nj = S // tm, D_FF // tf

    def body(h_hbm, df_hbm, a_ref, b_ref, wg_ref, wu_ref, wd_ref,
             gg_ref, gu_ref, gd_ref, dh_hbm,
             h_v, df_v, dh_v, gg_acc, gu_acc, gd_acc, sem):
        j = pl.program_id(0)
        i = pl.program_id(1)

        @pl.when((j == 0) & (i == 0))
        def _():
            c1 = pltpu.make_async_copy(h_hbm, h_v, sem.at[0])
            c2 = pltpu.make_async_copy(df_hbm, df_v, sem.at[1])
            c1.start()
            c2.start()
            c1.wait()
            c2.wait()

        @pl.when(i == 0)
        def _():
            gg_acc[...] = jnp.zeros_like(gg_acc)
            gu_acc[...] = jnp.zeros_like(gu_acc)
            gd_acc[...] = jnp.zeros_like(gd_acc)

        rows = pl.ds(pl.multiple_of(i * tm, tm), tm)
        hi = h_v[rows, :]
        dfi = df_v[rows, :]
        av = a_ref[...].astype(F32)
        bv = b_ref[...].astype(F32)
        sg = _sigmoid(av)
        sa = av * sg
        hsw = (sa * bv).astype(BF16)
        dhsw = _dot_nt(dfi, wd_ref[...])
        da = (dhsw * bv * (sg * (1.0 + av * (1.0 - sg)))).astype(BF16)
        db = (dhsw * sa).astype(BF16)
        gd_acc[...] += _dot_tn(hsw, dfi)
        gg_acc[...] += _dot_tn(da, hi)
        gu_acc[...] += _dot_tn(db, hi)
        dh = _dot(da, wg_ref[...]) + _dot(db, wu_ref[...])

        @pl.when(j == 0)
        def _():
            dh_v[rows, :] = dh

        @pl.when(j > 0)
        def _():
            dh_v[rows, :] += dh

        @pl.when(i == ni - 1)
        def _():
            gg_ref[...] = gg_acc[...].astype(BF16)
            gu_ref[...] = gu_acc[...].astype(BF16)
            gd_ref[...] = gd_acc[...].astype(BF16)

        @pl.when((j == nj - 1) & (i == ni - 1))
        def _():
            c3 = pltpu.make_async_copy(dh_v, dh_hbm, sem.at[2])
            c3.start()
            c3.wait()

    anyspec = pl.BlockSpec(memory_space=pl.ANY)
    wspec = pl.BlockSpec((tf, D), lambda j, i: (j, 0))
    act = pl.BlockSpec((tm, tf), lambda j, i: (i, j))
    return pl.pallas_call(
        body, name=name, grid=(nj, ni),
        in_specs=[anyspec, anyspec, act, act, wspec, wspec, wspec],
        out_specs=[wspec, wspec, wspec, anyspec],
        out_shape=[jax.ShapeDtypeStruct((D_FF, D), BF16)] * 3 + [jax.ShapeDtypeStruct((S, D), F32)],
        scratch_shapes=[pltpu.VMEM((S, D), BF16), pltpu.VMEM((S, D), BF16), pltpu.VMEM((S, D), F32),
                        pltpu.VMEM((tf, D), F32), pltpu.VMEM((tf, D), F32), pltpu.VMEM((tf, D), F32),
                        pltpu.SemaphoreType.DMA((3,))],
        compiler_params=_params(("arbitrary", "arbitrary")),
    )(h, df, a, b, wgT, wuT, wd)


def _norm_bwd(dh, x, dxo, vecs, *, name, tm=256):
    S = x.shape[0]

    def body(dh_ref, x_ref, dxo_ref, vec_ref, dx_ref, part_ref):
        @pl.when(pl.program_id(0) == 0)
        def _():
            part_ref[...] = jnp.zeros_like(part_ref)

        dh = dh_ref[...]
        xv = x_ref[...]
        r = _rstd(xv)
        xhat = xv * r
        w = vec_ref[0:1, :]
        xn = xhat * w
        dxn = dh * (1.0 + vec_ref[2:3, :])
        part_ref[0:1, :] += jnp.sum(dxn * xhat, axis=0, keepdims=True)
        part_ref[1:2, :] += jnp.sum(dh, axis=0, keepdims=True)
        part_ref[2:3, :] += jnp.sum(dh * xn, axis=0, keepdims=True)
        dx_ref[...] = dxo_ref[...] + _rms_bwd(dxn * w, xhat, r)

    row = pl.BlockSpec((tm, D), lambda i: (i, 0))
    vec = pl.BlockSpec((8, D), lambda i: (0, 0))
    return pl.pallas_call(
        body, name=name, grid=(S // tm,), in_specs=[row, row, row, vec], out_specs=[row, vec],
        out_shape=[jax.ShapeDtypeStruct((S, D), F32), jax.ShapeDtypeStruct((8, D), F32)],
        compiler_params=_params(("arbitrary",)),
    )(dh, x, dxo, vecs)


def _head(x, tgt, nf, *, tm=256):
    S = x.shape[0]

    def body(x_ref, t_ref, nf_ref, dx_ref, part_ref):
        @pl.when(pl.program_id(0) == 0)
        def _():
            part_ref[...] = jnp.zeros_like(part_ref)

        xv = x_ref[...]
        r = _rstd(xv)
        xhat = xv * r
        w = nf_ref[...]
        e = xhat * w - t_ref[...]
        dy = e * (1.0 / D)
        part_ref[0:1, :] += jnp.sum(dy * xhat, axis=0, keepdims=True)
        part_ref[1:2, :] += jnp.sum(e * e) * (0.5 / D)
        dx_ref[...] = _rms_bwd(dy * w, xhat, r)

    row = pl.BlockSpec((tm, D), lambda i: (i, 0))
    return pl.pallas_call(
        body, name="head", grid=(S // tm,),
        in_specs=[row, row, pl.BlockSpec((1, D), lambda i: (0, 0))],
        out_specs=[row, pl.BlockSpec((8, D), lambda i: (0, 0))],
        out_shape=[jax.ShapeDtypeStruct((S, D), F32), jax.ShapeDtypeStruct((8, D), F32)],
        compiler_params=_params(("arbitrary",)),
    )(x, tgt, nf)


def _mix_in_fwd(x, vecs, w_inT, *, tm=256):
    S = x.shape[0]

    def body(x_ref, vec_ref, w_ref, h_ref, p_ref):
        xv = x_ref[...]
        hn = xv * _rstd(xv) * vec_ref[0:1, :]
        h = (hn * (1.0 + vec_ref[2:3, :]) + vec_ref[1:2, :]).astype(BF16)
        h_ref[...] = h
        p_ref[...] = _dot_nt(h, w_ref[...])

    row = pl.BlockSpec((tm, D), lambda i: (i, 0))
    return pl.pallas_call(
        body, name="mix_in_fwd", grid=(S // tm,),
        in_specs=[row, pl.BlockSpec((8, D), lambda i: (0, 0)), pl.BlockSpec((D_IN_PAD, D), lambda i: (0, 0))],
        out_specs=[row, pl.BlockSpec((tm, D_IN_PAD), lambda i: (i, 0))],
        out_shape=[jax.ShapeDtypeStruct((S, D), BF16), jax.ShapeDtypeStruct((S, D_IN_PAD), F32)],
        compiler_params=_params(("parallel",)),
    )(x, vecs, w_inT)


def _bucket_table():
    qi = np.arange(WINDOW)[:, None]
    kj = np.arange(2 * WINDOW)[None, :]
    dist = qi + WINDOW - kj
    max_exact = NUM_BUCKETS // 2
    n = np.maximum(dist, 0)
    nf = np.maximum(n, 1).astype(np.float32)
    large = max_exact + (np.log(nf / np.float32(max_exact)) / np.float32(math.log(WINDOW / max_exact))
                         * np.float32(NUM_BUCKETS - max_exact)).astype(np.int32)
    large = np.minimum(large, NUM_BUCKETS - 1)
    return np.where(n < max_exact, n, large).astype(np.int32)


def _bias_build(rel_bias, bucket):
    def body(rb_ref, bk_ref, out_ref):
        bk = bk_ref[...]
        for h in range(SWA_HEADS):
            acc = jnp.zeros((WINDOW, 2 * WINDOW), F32)
            for b in range(NUM_BUCKETS):
                acc = jnp.where(bk == b, rb_ref[b, h], acc)
            out_ref[h] = acc

    return pl.pallas_call(
        body, name="bias_build",
        in_specs=[pl.BlockSpec(memory_space=pltpu.SMEM), pl.BlockSpec(memory_space=pltpu.VMEM)],
        out_specs=pl.BlockSpec(memory_space=pltpu.VMEM),
        out_shape=jax.ShapeDtypeStruct((SWA_HEADS, WINDOW, 2 * WINDOW), F32),
    )(rel_bias, bucket)


def _swa_valid(n):
    row = lax.broadcasted_iota(jnp.int32, (WINDOW, 2 * WINDOW), 0)
    col = lax.broadcasted_iota(jnp.int32, (WINDOW, 2 * WINDOW), 1)
    dist = row + WINDOW - col
    return (dist >= 0) & (dist < WINDOW) & ((col >= WINDOW) | (n > 0))


def _swa_probs(qh, kk, bias_h, sink, valid):
    s = _dot_nt(qh, kk) * SWA_SCALE + bias_h
    s = jnp.where(valid, s, -jnp.inf)
    m = jnp.maximum(jnp.max(s, axis=-1, keepdims=True), sink)
    p = jnp.exp(s - m)
    ps = jnp.exp(sink - m)
    inv = 1.0 / (jnp.sum(p, axis=-1, keepdims=True) + ps)
    return p * inv, ps * inv


def _swa_specs():
    prev = lambda n: jnp.maximum(n - 1, 0)
    return [pl.BlockSpec((WINDOW, 512), lambda n: (n, 0)),
            pl.BlockSpec((WINDOW, 128), lambda n: (n, 4)),
            pl.BlockSpec((WINDOW, 128), lambda n: (prev(n), 4)),
            pl.BlockSpec((WINDOW, 128), lambda n: (n, 5)),
            pl.BlockSpec((WINDOW, 128), lambda n: (prev(n), 5)),
            pl.BlockSpec((SWA_HEADS, WINDOW, 2 * WINDOW), lambda n: (0, 0, 0)),
            pl.BlockSpec(memory_space=pltpu.SMEM)]


def _swa_fwd(proj, bias, sinks):
    S = proj.shape[0]

    def body(q_ref, kc_ref, kp_ref, vc_ref, vp_ref, bias_ref, sink_ref, o_ref):
        valid = _swa_valid(pl.program_id(0))
        q = q_ref[...].astype(BF16)
        kfull = jnp.concatenate([kp_ref[...], kc_ref[...]], axis=0).astype(BF16)
        vfull = jnp.concatenate([vp_ref[...], vc_ref[...]], axis=0).astype(BF16)
        outs = []
        for h in range(SWA_HEADS):
            g = h // 4
            kk = kfull[:, 64 * g:64 * g + 64]
            vv = vfull[:, 64 * g:64 * g + 64]
            pk, _ = _swa_probs(q[:, 64 * h:64 * h + 64], kk, bias_ref[h], sink_ref[0, h], valid)
            outs.append(_dot(pk.astype(BF16), vv))
        o_ref[...] = jnp.concatenate(outs, axis=1)

    return pl.pallas_call(
        body, name="swa_fwd", grid=(S // WINDOW,),
        in_specs=_swa_specs(),
        out_specs=pl.BlockSpec((WINDOW, 512), lambda n: (n, 0)),
        out_shape=jax.ShapeDtypeStruct((S, 512), F32),
        compiler_params=_params(("parallel",)),
    )(proj, proj, proj, proj, proj, bias, sinks)


def _swa_bwd(proj, bias, sinks, do, bucket):
    S = proj.shape[0]
    nb = S // WINDOW

    def body(q_ref, kc_ref, kp_ref, vc_ref, vp_ref, bias_ref, sink_ref, do_ref, bk_ref,
             dq_ref, dk_ref, dv_ref, drb_ref, dsk_ref, dbias_acc):
        n = pl.program_id(0)

        @pl.when(n == 0)
        def _():
            dk_ref[...] = jnp.zeros_like(dk_ref)
            dv_ref[...] = jnp.zeros_like(dv_ref)
            dsk_ref[...] = jnp.zeros_like(dsk_ref)
            dbias_acc[...] = jnp.zeros_like(dbias_acc)
            drb_ref[...] = jnp.zeros_like(drb_ref)

        valid = _swa_valid(n)
        q = q_ref[...].astype(BF16)
        dov = do_ref[...]
        kfull = jnp.concatenate([kp_ref[...], kc_ref[...]], axis=0).astype(BF16)
        vfull = jnp.concatenate([vp_ref[...], vc_ref[...]], axis=0).astype(BF16)
        prow = pl.ds(pl.multiple_of(jnp.maximum(n - 1, 0) * WINDOW, WINDOW), WINDOW)
        crow = pl.ds(pl.multiple_of(n * WINDOW, WINDOW), WINDOW)
        dqs = []
        for g in range(2):
            kk = kfull[:, 64 * g:64 * g + 64]
            vv = vfull[:, 64 * g:64 * g + 64]
            dkk = jnp.zeros((2 * WINDOW, SWA_DH), F32)
            dvv = jnp.zeros((2 * WINDOW, SWA_DH), F32)
            for h in range(4 * g, 4 * g + 4):
                qh = q[:, 64 * h:64 * h + 64]
                pk, psink = _swa_probs(qh, kk, bias_ref[h], sink_ref[0, h], valid)
                pkb = pk.astype(BF16)
                o = _dot(pkb, vv)
                doh = dov[:, 64 * h:64 * h + 64]
                dob = doh.astype(BF16)
                dp = _dot_nt(dob, vv)
                delta = jnp.sum(doh * o, axis=-1, keepdims=True)
                ds = pk * (dp - delta)
                dsk_ref[h:h + 1, :] += jnp.broadcast_to(jnp.sum(-psink * delta, keepdims=True), (1, 128))
                dbias_acc[h] += ds
                dsb = (ds * SWA_SCALE).astype(BF16)
                dqs.append(_dot(dsb, kk))
                dkk += _dot_tn(dsb, qh)
                dvv += _dot_tn(pkb, dob)
            dk_ref[prow, 64 * g:64 * g + 64] += dkk[:WINDOW]
            dk_ref[crow, 64 * g:64 * g + 64] += dkk[WINDOW:]
            dv_ref[prow, 64 * g:64 * g + 64] += dvv[:WINDOW]
            dv_ref[crow, 64 * g:64 * g + 64] += dvv[WINDOW:]
        dq_ref[...] = jnp.concatenate(dqs, axis=1)

        @pl.when(n == nb - 1)
        def _():
            bk = bk_ref[...]
            for h in range(SWA_HEADS):
                dbh = dbias_acc[h]
                for b in range(NUM_BUCKETS):
                    val = jnp.sum(jnp.where(bk == b, dbh, 0.0), keepdims=True)
                    drb_ref[b * 8 + h:b * 8 + h + 1, :] = jnp.broadcast_to(val, (1, 128))

    full = lambda shape: pl.BlockSpec(shape, lambda n: tuple(0 for _ in shape))
    return pl.pallas_call(
        body, name="swa_bwd", grid=(nb,),
        in_specs=_swa_specs() + [pl.BlockSpec((WINDOW, 512), lambda n: (n, 0)), full((WINDOW, 2 * WINDOW))],
        out_specs=[pl.BlockSpec((WINDOW, 512), lambda n: (n, 0)), full((S, 128)), full((S, 128)),
                   full((NUM_BUCKETS * 8, 128)), full((8, 128))],
        out_shape=[jax.ShapeDtypeStruct((S, 512), F32), jax.ShapeDtypeStruct((S, 128), F32),
                   jax.ShapeDtypeStruct((S, 128), F32), jax.ShapeDtypeStruct((NUM_BUCKETS * 8, 128), F32),
                   jax.ShapeDtypeStruct((8, 128), F32)],
        scratch_shapes=[pltpu.VMEM((SWA_HEADS, WINDOW, 2 * WINDOW), F32)],
        compiler_params=_params(("arbitrary",)),
    )(proj, proj, proj, proj, proj, bias, sinks, do, bucket)


def _rope_tables(S):
    inv = ROPE_THETA ** (-jnp.arange(0, MLA_ROPE, 2, dtype=F32) / MLA_ROPE)
    ang = jnp.arange(S, dtype=F32)[:, None] * inv[None, :]
    cos, sin = jnp.cos(ang), jnp.sin(ang)
    return jnp.tile(jnp.concatenate([cos, cos], axis=1), (1, 4)), jnp.tile(jnp.concatenate([-sin, sin], axis=1), (1, 4))


def _swap_halves(x):
    w = x.shape[-1]
    lane = lax.broadcasted_iota(jnp.int32, x.shape, x.ndim - 1)
    return jnp.where((lane % 64) < 32, pltpu.roll(x, w - 32, x.ndim - 1), pltpu.roll(x, 32, x.ndim - 1))


def _mla_pre_fwd(proj, qn_w, kvn_w, wuqT, wukv, cos, sin, *, tm=256):
    S = proj.shape[0]

    def body(ql_ref, kl_ref, kr_ref, qw_ref, kw_ref, wuq_ref, wukv_ref, cos_ref, sin_ref,
             qc_ref, kc_ref, vv_ref):
        ql = ql_ref[...]
        qn = (ql * _rstd(ql) * qw_ref[...]).astype(BF16)
        q = _dot_nt(qn, wuq_ref[...])
        cs, sn = cos_ref[...], sin_ref[...]
        qr = q[:, 512:768]
        qr = qr * cs + _swap_halves(qr) * sn
        half = lax.broadcasted_iota(jnp.int32, (tm, 128), 1) // 64
        kl = kl_ref[...]
        kvn = (kl * _rstd(kl) * kw_ref[...]).astype(BF16)
        kr = kr_ref[...]
        kr = kr * cs[:, :128] + _swap_halves(kr) * sn[:, :128]
        kr2 = (kr + pltpu.roll(kr, 64, 1)).astype(BF16)
        for h in range(MLA_HEADS):
            qc_ref[h, :, 0:128] = q[:, 128 * h:128 * h + 128].astype(BF16)
            chunk = qr[:, 128 * (h // 2):128 * (h // 2) + 128]
            qc_ref[h, :, 128:256] = jnp.where(half == (h % 2), chunk, 0.0).astype(BF16)
            kc_ref[h, :, 0:128] = _dot(kvn, wukv_ref[2 * h]).astype(BF16)
            kc_ref[h, :, 128:256] = kr2
            vv_ref[h] = _dot(kvn, wukv_ref[2 * h + 1]).astype(BF16)

    const = lambda shape: pl.BlockSpec(shape, lambda i: tuple(0 for _ in shape))
    return pl.pallas_call(
        body, name="mla_pre_fwd", grid=(S // tm,),
        in_specs=[pl.BlockSpec((tm, 256), lambda i: (i, 3)), pl.BlockSpec((tm, 128), lambda i: (i, 8)),
                  pl.BlockSpec((tm, 128), lambda i: (i, 9)), const((1, 256)), const((1, 128)),
                  const((768, 256)), const((8, 128, 128)),
                  pl.BlockSpec((tm, 256), lambda i: (i, 0)), pl.BlockSpec((tm, 256), lambda i: (i, 0))],
        out_specs=[pl.BlockSpec((MLA_HEADS, tm, 256), lambda i: (0, i, 0)),
                   pl.BlockSpec((MLA_HEADS, tm, 256), lambda i: (0, i, 0)),
                   pl.BlockSpec((MLA_HEADS, tm, 128), lambda i: (0, i, 0))],
        out_shape=[jax.ShapeDtypeStruct((MLA_HEADS, S, 256), BF16), jax.ShapeDtypeStruct((MLA_HEADS, S, 256), BF16),
                   jax.ShapeDtypeStruct((MLA_HEADS, S, 128), BF16)],
        compiler_params=_params(("parallel",)),
    )(proj, proj, proj, qn_w, kvn_w, wuqT, wukv, cos, sin)


def _causal(i, j, t):
    row = i * t + lax.broadcasted_iota(jnp.int32, (t, t), 0)
    col = j * t + lax.broadcasted_iota(jnp.int32, (t, t), 1)
    return col <= row


def _mla_attn_fwd(qc, kc, vv, *, t=256):
    S = qc.shape[1]
    t = min(t, S)

    def body(q_ref, k_ref, v_ref, o_ref, l_ref):
        i = pl.program_id(0)
        diag = _causal(0, 0, t)

        def step(j, carry, masked):
            rows = pl.ds(pl.multiple_of(j * t, t), t)
            out = []
            for h in range(MLA_HEADS):
                m, l, acc = carry[h]
                s = _dot_nt(q_ref[h], k_ref[h, rows, :]) * MLA_SCALE
                if masked:
                    s = jnp.where(diag, s, -jnp.inf)
                m_new = jnp.maximum(m, jnp.max(s, axis=-1, keepdims=True))
                alpha = jnp.exp(m - m_new)
                p = jnp.exp(s - m_new)
                l = alpha * l + jnp.sum(p, axis=-1, keepdims=True)
                acc = alpha * acc + _dot(p.astype(BF16), v_ref[h, rows, :])
                out.append((m_new, l, acc))
            return tuple(out)

        init = tuple((jnp.full((t, 1), -jnp.inf, F32), jnp.zeros((t, 1), F32), jnp.zeros((t, MLA_V), F32))
                     for _ in range(MLA_HEADS))
        carry = lax.fori_loop(0, i, lambda j, c: step(j, c, False), init)
        carry = step(i, carry, True)
        for h in range(MLA_HEADS):
            m, l, acc = carry[h]
            o_ref[:, 128 * h:128 * h + 128] = acc / l
            l_ref[h] = jnp.broadcast_to(m + jnp.log(l), (t, 128))

    return pl.pallas_call(
        body, name="mla_attn_fwd", grid=(S // t,),
        in_specs=[pl.BlockSpec((MLA_HEADS, t, 256), lambda i: (0, i, 0)),
                  pl.BlockSpec((MLA_HEADS, S, 256), lambda i: (0, 0, 0)),
                  pl.BlockSpec((MLA_HEADS, S, 128), lambda i: (0, 0, 0))],
        out_specs=[pl.BlockSpec((t, 512), lambda i: (i, 0)),
                   pl.BlockSpec((MLA_HEADS, t, 128), lambda i: (0, i, 0))],
        out_shape=[jax.ShapeDtypeStruct((S, 512), F32), jax.ShapeDtypeStruct((MLA_HEADS, S, 128), F32)],
        compiler_params=_params(("parallel",)),
    )(qc, kc, vv)


def _mla_attn_bwd(qc, kc, vv, o, lse, do, *, t=256):
    S = qc.shape[1]
    t = min(t, S)
    nblk = S // t
    hp = 2

    def body(q_ref, k_ref, v_ref, o_ref, l_ref, do_ref, dq_ref, dk_ref, dv_ref):
        j = pl.program_id(1)

        @pl.when(j == 0)
        def _():
            dq_ref[...] = jnp.zeros_like(dq_ref)

        diag = _causal(0, 0, t)

        def step(i, carry, masked):
            rows = pl.ds(pl.multiple_of(i * t, t), t)
            out = []
            for h in range(hp):
                dk, dv = carry[h]
                k = k_ref[h]
                q = q_ref[h, rows, :]
                dov = do_ref[rows, 128 * h:128 * h + 128]
                lrow = l_ref[h, rows, :][:, 0:1]
                p = jnp.exp(_dot_nt(q, k) * MLA_SCALE - lrow)
                if masked:
                    p = jnp.where(diag, p, 0.0)
                dob = dov.astype(BF16)
                dv = dv + _dot_tn(p.astype(BF16), dob)
                dp = _dot_nt(dob, v_ref[h])
                delta = jnp.sum(dov * o_ref[rows, 128 * h:128 * h + 128], axis=-1, keepdims=True)
                ds = (p * (dp - delta) * MLA_SCALE).astype(BF16)
                dk = dk + _dot_tn(ds, q)
                dq_ref[h, rows, :] += _dot(ds, k)
                out.append((dk, dv))
            return tuple(out)

        init = tuple((jnp.zeros((t, 256), F32), jnp.zeros((t, MLA_V), F32)) for _ in range(hp))
        carry = step(j, init, True)
        carry = lax.fori_loop(j + 1, nblk, lambda i, c: step(i, c, False), carry)
        for h in range(hp):
            dk_ref[h] = carry[h][0]
            dv_ref[h] = carry[h][1]

    return pl.pallas_call(
        body, name="mla_attn_bwd", grid=(MLA_HEADS // hp, nblk),
        in_specs=[pl.BlockSpec((hp, S, 256), lambda g, j: (g, 0, 0)),
                  pl.BlockSpec((hp, t, 256), lambda g, j: (g, j, 0)),
                  pl.BlockSpec((hp, t, 128), lambda g, j: (g, j, 0)),
                  pl.BlockSpec((S, 128 * hp), lambda g, j: (0, g)),
                  pl.BlockSpec((hp, S, 128), lambda g, j: (g, 0, 0)),
                  pl.BlockSpec((S, 128 * hp), lambda g, j: (0, g))],
        out_specs=[pl.BlockSpec((hp, S, 256), lambda g, j: (g, 0, 0)),
                   pl.BlockSpec((hp, t, 256), lambda g, j: (g, j, 0)),
                   pl.BlockSpec((hp, t, 128), lambda g, j: (g, j, 0))],
        out_shape=[jax.ShapeDtypeStruct((MLA_HEADS, S, 256), F32), jax.ShapeDtypeStruct((MLA_HEADS, S, 256), F32),
                   jax.ShapeDtypeStruct((MLA_HEADS, S, 128), F32)],
        compiler_params=_params(("parallel", "arbitrary")),
    )(qc, kc, vv, o, lse, do)


def _mla_pre_bwd(proj, qn_w, kvn_w, wuqT, wukv, cos, sin, dqc, dkc, dvv, *, tm=256):
    S = proj.shape[0]

    def body(ql_ref, kl_ref, qw_ref, kw_ref, wuq_ref, wukv_ref, cos_ref, sin_ref, dqc_ref, dkc_ref, dvv_ref,
             dql_ref, dkl_ref, dkr_ref, gq_ref, gkv_ref, part_ref):
        @pl.when(pl.program_id(0) == 0)
        def _():
            gq_ref[...] = jnp.zeros_like(gq_ref)
            gkv_ref[...] = jnp.zeros_like(gkv_ref)
            part_ref[...] = jnp.zeros_like(part_ref)

        cs, sn = cos_ref[...], sin_ref[...]
        half = lax.broadcasted_iota(jnp.int32, (tm, 128), 1) // 64
        ql = ql_ref[...]
        rq = _rstd(ql)
        qhat = ql * rq
        qw = qw_ref[...]
        qn = (qhat * qw).astype(BF16)
        chunks = []
        for pair in range(2):
            chunks.append(jnp.where(half == 0, dqc_ref[2 * pair, :, 128:256], dqc_ref[2 * pair + 1, :, 128:256]))
        dqr = jnp.concatenate(chunks, axis=1)
        dqr = dqr * cs + _swap_halves(dqr * sn)
        dq = jnp.concatenate([dqc_ref[h, :, 0:128] for h in range(MLA_HEADS)] + [dqr], axis=1).astype(BF16)
        gq_ref[...] += _dot_tn(dq, qn)
        dqn = _dot(dq, wuq_ref[...])
        part_ref[0:1, :] += jnp.sum(dqn * qhat, axis=0, keepdims=True)
        dql_ref[...] = _rms_bwd(dqn * qw, qhat, rq)
        kl = kl_ref[...]
        rk = _rstd(kl)
        khat = kl * rk
        kw = kw_ref[...]
        kvn = (khat * kw).astype(BF16)
        dkvn = jnp.zeros((tm, MLA_KVR), F32)
        dkr2 = jnp.zeros((tm, 128), F32)
        for h in range(MLA_HEADS):
            dkn = dkc_ref[h, :, 0:128].astype(BF16)
            dvh = dvv_ref[h].astype(BF16)
            gkv_ref[2 * h] += _dot_tn(kvn, dkn)
            gkv_ref[2 * h + 1] += _dot_tn(kvn, dvh)
            dkvn += _dot_nt(dkn, wukv_ref[2 * h]) + _dot_nt(dvh, wukv_ref[2 * h + 1])
            dkr2 += dkc_ref[h, :, 128:256]
        part_ref[1:2, 0:128] += jnp.sum(dkvn * khat, axis=0, keepdims=True)
        dkl_ref[...] = _rms_bwd(dkvn * kw, khat, rk)
        dkr = jnp.where(half == 0, dkr2 + pltpu.roll(dkr2, 64, 1), 0.0)
        dkr_ref[...] = dkr * cs[:, :128] + _swap_halves(dkr * sn[:, :128])

    const = lambda shape: pl.BlockSpec(shape, lambda i: tuple(0 for _ in shape))
    heads = lambda w: pl.BlockSpec((MLA_HEADS, tm, w), lambda i: (0, i, 0))
    return pl.pallas_call(
        body, name="mla_pre_bwd", grid=(S // tm,),
        in_specs=[pl.BlockSpec((tm, 256), lambda i: (i, 3)), pl.BlockSpec((tm, 128), lambda i: (i, 8)),
                  const((1, 256)), const((1, 128)), const((768, 256)), const((8, 128, 128)),
                  pl.BlockSpec((tm, 256), lambda i: (i, 0)), pl.BlockSpec((tm, 256), lambda i: (i, 0)),
                  heads(256), heads(256), heads(128)],
        out_specs=[pl.BlockSpec((tm, 256), lambda i: (i, 0)), pl.BlockSpec((tm, 128), lambda i: (i, 0)),
                   pl.BlockSpec((tm, 128), lambda i: (i, 0)), const((768, 256)), const((8, 128, 128)), const((8, 256))],
        out_shape=[jax.ShapeDtypeStruct((S, 256), F32), jax.ShapeDtypeStruct((S, 128), F32),
                   jax.ShapeDtypeStruct((S, 128), F32), jax.ShapeDtypeStruct((768, 256), F32),
                   jax.ShapeDtypeStruct((8, 128, 128), F32), jax.ShapeDtypeStruct((8, 256), F32)],
        compiler_params=_params(("arbitrary",)),
    )(proj, proj, qn_w, kvn_w, wuqT, wukv, cos, sin, dqc, dkc, dvv)


def _mix_out_fwd(x, oa, ob, w_o, vecs, *, tm=256):
    S = x.shape[0]

    def body(x_ref, oa_ref, ob_ref, w_ref, vec_ref, xo_ref, mo_ref):
        mo = _dot(oa_ref[...].astype(BF16), w_ref[0:512, :]) + _dot(ob_ref[...].astype(BF16), w_ref[512:1024, :])
        mo_ref[...] = mo
        xo_ref[...] = x_ref[...] + vec_ref[3:4, :] * mo

    row = pl.BlockSpec((tm, D), lambda i: (i, 0))
    half = pl.BlockSpec((tm, 512), lambda i: (i, 0))
    return pl.pallas_call(
        body, name="mix_out_fwd", grid=(S // tm,),
        in_specs=[row, half, half, pl.BlockSpec((D, D), lambda i: (0, 0)), pl.BlockSpec((8, D), lambda i: (0, 0))],
        out_specs=[row, row],
        out_shape=[jax.ShapeDtypeStruct((S, D), F32), jax.ShapeDtypeStruct((S, D), F32)],
        compiler_params=_params(("parallel",)),
    )(x, oa, ob, w_o, vecs)


def _mix_out_bwd(dxo, mo, oa, ob, w_o, vecs, *, tm=256):
    S = dxo.shape[0]

    def body(dx_ref, mo_ref, oa_ref, ob_ref, w_ref, vec_ref, doa_ref, dob_ref, gw_ref, part_ref):
        @pl.when(pl.program_id(0) == 0)
        def _():
            gw_ref[...] = jnp.zeros_like(gw_ref)
            part_ref[...] = jnp.zeros_like(part_ref)

        dx = dx_ref[...]
        part_ref[0:1, :] += jnp.sum(dx * mo_ref[...], axis=0, keepdims=True)
        dmo = (vec_ref[3:4, :] * dx).astype(BF16)
        doa_ref[...] = _dot_nt(dmo, w_ref[0:512, :])
        dob_ref[...] = _dot_nt(dmo, w_ref[512:1024, :])
        gw_ref[0:512, :] += _dot_tn(oa_ref[...].astype(BF16), dmo)
        gw_ref[512:1024, :] += _dot_tn(ob_ref[...].astype(BF16), dmo)

    row = pl.BlockSpec((tm, D), lambda i: (i, 0))
    half = pl.BlockSpec((tm, 512), lambda i: (i, 0))
    return pl.pallas_call(
        body, name="mix_out_bwd", grid=(S // tm,),
        in_specs=[row, row, half, half, pl.BlockSpec((D, D), lambda i: (0, 0)), pl.BlockSpec((8, D), lambda i: (0, 0))],
        out_specs=[half, half, pl.BlockSpec((D, D), lambda i: (0, 0)), pl.BlockSpec((8, D), lambda i: (0, 0))],
        out_shape=[jax.ShapeDtypeStruct((S, 512), F32), jax.ShapeDtypeStruct((S, 512), F32),
                   jax.ShapeDtypeStruct((D, D), F32), jax.ShapeDtypeStruct((8, D), F32)],
        compiler_params=_params(("arbitrary",)),
    )(dxo, mo, oa, ob, w_o, vecs)


def _mix_in_bwd(h, w_inT, dq, dk, dv, dql, dkl, dkr, *, tm=256):
    S = h.shape[0]
    offs = (0, 512, 640, 768, 1024, 1152)
    wid = (512, 128, 128, 256, 128, 128)

    def body(h_ref, w_ref, dq_ref, dk_ref, dv_ref, dql_ref, dkl_ref, dkr_ref, dh_ref, gw_ref):
        @pl.when(pl.program_id(0) == 0)
        def _():
            gw_ref[...] = jnp.zeros_like(gw_ref)

        hv = h_ref[...]
        dh = jnp.zeros((tm, D), F32)
        for ref, o, w in zip((dq_ref, dk_ref, dv_ref, dql_ref, dkl_ref, dkr_ref), offs, wid):
            w = min(w, D_IN - o)
            dpart = ref[...][:, :w].astype(BF16)
            dh += _dot(dpart, w_ref[o:o + w, :])
            gw_ref[o:o + w, :] += _dot_tn(dpart, hv)
        dh_ref[...] = dh

    row = pl.BlockSpec((tm, D), lambda i: (i, 0))
    part = lambda w: pl.BlockSpec((tm, w), lambda i: (i, 0))
    return pl.pallas_call(
        body, name="mix_in_bwd", grid=(S // tm,),
        in_specs=[row, pl.BlockSpec((D_IN_PAD, D), lambda i: (0, 0))] + [part(w) for w in wid],
        out_specs=[row, pl.BlockSpec((D_IN, D), lambda i: (0, 0))],
        out_shape=[jax.ShapeDtypeStruct((S, D), F32), jax.ShapeDtypeStruct((D_IN, D), F32)],
        compiler_params=_params(("arbitrary",)),
    )(h, w_inT, dq, dk, dv, dql, dkl, dkr)


def _vecs(norm_w, mod9, k):
    return jnp.concatenate([norm_w.reshape(1, D), mod9[3 * k:3 * k + 3], jnp.zeros((4, D), F32)], axis=0)


def _uq_group_rows(wuqT):
    per = MLA_NOPE + MLA_ROPE
    nope = [wuqT[per * h:per * h + MLA_NOPE] for h in range(MLA_HEADS)]
    rope = [wuqT[per * h + MLA_NOPE:per * (h + 1)] for h in range(MLA_HEADS)]
    return jnp.concatenate(nope + rope, axis=0)


def _uq_ungroup_rows(g):
    parts = []
    for h in range(MLA_HEADS):
        parts += [g[MLA_NOPE * h:MLA_NOPE * (h + 1)], g[512 + MLA_ROPE * h:512 + MLA_ROPE * (h + 1)]]
    return jnp.concatenate(parts, axis=0)


def _local_step(x, tgt, mod9, norms, sinks, rel_bias, q_norm, kv_norm, W, on_grads=None):
    if on_grads is None:
        on_grads = lambda group, grads, vecs: vecs
    S = x.shape[0]
    v1 = _vecs(norms["ffn1"], mod9, 0)
    v2 = _vecs(norms["mix"], mod9, 1)
    v3 = _vecs(norms["ffn2"], mod9, 2)
    bucket = jnp.asarray(_bucket_table())
    cos, sin = _rope_tables(S)
    if isinstance(W, dict):
        full, W = W, (lambda group, after, vecs: (full, vecs))

    W1, v1 = W("ffn1", [], v1)
    x1, h1, a1, b1, f1 = _ffn_fwd(x, v1, W1["g1T"], W1["u1T"], W1["d1"], name="ffn1_fwd")
    W2, v2 = W("mixer", [x1], v2)
    w_inT = jnp.pad(W2["w_inT"], ((0, D_IN_PAD - D_IN), (0, 0))).astype(BF16)
    wuqT = _uq_group_rows(W2["w_uqT"])
    h2, proj = _mix_in_fwd(x1, v2, w_inT)
    bias = _bias_build(rel_bias, bucket)
    oa = _swa_fwd(proj, bias, sinks)
    qc, kc, vv = _mla_pre_fwd(proj, q_norm, kv_norm, wuqT, W2["w_ukv"], cos, sin)
    ob, lse = _mla_attn_fwd(qc, kc, vv)
    _, v2o = W("ffn2_on_its_way", [ob], v2)
    x2, mo = _mix_out_fwd(x1, oa, ob, W2["w_o"], v2o)
    W3, v3 = W("ffn2", [x2], v3)
    x3, h3, a3, b3, f3 = _ffn_fwd(x2, v3, W3["g3T"], W3["u3T"], W3["d3"], name="ffn2_fwd")
    dx3, head_part = _head(x3, tgt, norms["final"])

    df3, g3_part = _ffn_bwd_pre(dx3, f3, v3, name="ffn2_bwd_pre")
    gg3, gu3, gd3, dh3 = _ffn_bwd_main(h3, df3, a3, b3, W3["g3T"], W3["u3T"], W3["d3"], name="ffn2_bwd")
    ffn2 = {"g3T": gg3, "u3T": gu3, "d3": gd3}
    v2 = on_grads("ffn2", ffn2, v2)
    dx2, n3_part = _norm_bwd(dh3, x2, dx3, v3, name="ffn2_norm_bwd")
    doa, dob, g_wo, g2_part = _mix_out_bwd(dx2, mo, oa, ob, W2["w_o"], v2)
    dq, dk, dv, drb, dsk = _swa_bwd(proj, bias, sinks, doa, bucket)
    dqc, dkc, dvv = _mla_attn_bwd(qc, kc, vv, ob, lse, dob)
    dql, dkl, dkr, g_uq, g_ukv, mla_part = _mla_pre_bwd(proj, q_norm, kv_norm, wuqT, W2["w_ukv"], cos, sin, dqc, dkc, dvv)
    dh2, g_win = _mix_in_bwd(h2, w_inT, dq, dk, dv, dql, dkl, dkr)
    mixer = {"w_inT": g_win, "w_uqT": _uq_ungroup_rows(g_uq).astype(BF16),
             "w_ukv": g_ukv.astype(BF16), "w_o": g_wo.astype(BF16)}
    v1 = on_grads("mixer", mixer, v1)
    dx1, n2_part = _norm_bwd(dh2, x1, dx2, v2, name="mix_norm_bwd")
    df1, g1_part = _ffn_bwd_pre(dx1, f1, v1, name="ffn1_bwd_pre")
    gg1, gu1, gd1, dh1 = _ffn_bwd_main(h1, df1, a1, b1, W1["g1T"], W1["u1T"], W1["d1"], name="ffn1_bwd")
    ffn1 = {"g1T": gg1, "u1T": gu1, "d1": gd1}
    v1 = on_grads("ffn1", ffn1, v1)
    dx0, n1_part = _norm_bwd(dh1, x, dx1, v1, name="ffn1_norm_bwd")

    grads = {**ffn1, **ffn2, **mixer}
    dmod9 = jnp.concatenate([n1_part[1:3], g1_part[0:1], n2_part[1:3], g2_part[0:1],
                             n3_part[1:3], g3_part[0:1]], axis=0)
    small = jnp.concatenate([n1_part[0], n2_part[0], n3_part[0], head_part[0], mla_part[0],
                             mla_part[1, :128], jnp.pad(dsk[:, 0], (0, 120)), drb[:, 0]])
    return head_part[1, 0], dx0, grads, small, dmod9


SMALL_LAYOUT = (("norm_ffn1", 1024), ("norm_mix", 1024), ("norm_ffn2", 1024), ("norm_final", 1024),
                ("q_norm", 256), ("kv_norm", 128), ("sinks", 128), ("rel_bias", 256))
N_SMALL = sum(n for _, n in SMALL_LAYOUT)


def _coords():
    return lax.axis_index("x"), lax.axis_index("y"), lax.axis_index("c")


def _flip(v, bit):
    return 1 - v if bit else v


def _peer(r):
    x, y, c = _coords()
    return (_flip(x, r & 4), _flip(y, r & 2), _flip(c, r & 1))


def _mod_fwd(c_tile, w_mod, b_mod3):
    W = w_mod.shape[1]

    def body(c_ref, w_ref, b_ref, mod_ref, ca_ref, call_ref, part_ref, send_sems, recv_sems):
        x, y, c = _coords()
        me = 4 * x + 2 * y + c
        call_ref[me] = c_ref[...]
        sends = []
        for r in range(1, N_DEV):
            cp = pltpu.make_async_remote_copy(c_ref, call_ref.at[me], send_sems.at[0, r], recv_sems.at[0, r],
                                              device_id=_peer(r), device_id_type=MESH)
            cp.start()
            sends.append(cp)
        for r in range(1, N_DEV):
            pltpu.make_async_remote_copy(c_ref, call_ref.at[me], send_sems.at[0, r], recv_sems.at[0, r],
                                         device_id=_peer(r), device_id_type=MESH).wait_recv()
        cv = call_ref[...].reshape(8 * N_DEV, D)
        ca = (cv * _sigmoid(cv)).astype(BF16)
        ca_ref[...] = ca
        part_ref[...] = _dot(ca, w_ref[...].astype(BF16)).reshape(N_DEV, 8, W)
        mod_ref[me] = part_ref[me] + b_ref[me]
        for r in range(1, N_DEV):
            cp = pltpu.make_async_remote_copy(part_ref.at[me ^ r], mod_ref.at[me], send_sems.at[1, r],
                                              recv_sems.at[1, r], device_id=_peer(r), device_id_type=MESH)
            cp.start()
            sends.append(cp)
        for r in range(1, N_DEV):
            pltpu.make_async_remote_copy(part_ref.at[me ^ r], mod_ref.at[me], send_sems.at[1, r],
                                         recv_sems.at[1, r], device_id=_peer(r), device_id_type=MESH).wait_recv()
            mod_ref[me ^ r] = mod_ref[me ^ r] + b_ref[me ^ r]
        for cp in sends:
            cp.wait_send()

    vm = pl.BlockSpec(memory_space=pltpu.VMEM)
    return pl.pallas_call(
        body, name="mod_fwd", in_specs=[vm, vm, vm], out_specs=[vm, vm],
        out_shape=[jax.ShapeDtypeStruct((N_DEV, 8, W), F32), jax.ShapeDtypeStruct((8 * N_DEV, D), BF16)],
        scratch_shapes=[pltpu.VMEM((N_DEV, 8, D), F32), pltpu.VMEM((N_DEV, 8, W), F32),
                        pltpu.SemaphoreType.DMA((2, N_DEV)), pltpu.SemaphoreType.DMA((2, N_DEV))],
        compiler_params=_params(),
    )(c_tile, w_mod, b_mod3)


def _mod_bwd(dmod3, small_tile, ca):
    W = dmod3.shape[2]

    def body(dm_ref, sm_ref, ca_ref, gw_ref, gb_ref, ssum_ref, dmcols, sm_all, gb_mine, send_sems, recv_sems):
        x, y, c = _coords()
        me = 4 * x + 2 * y + c
        dmcols[me] = dm_ref[me]
        sm_all[me] = sm_ref[...]
        sends = []
        for r in range(1, N_DEV):
            cp = pltpu.make_async_remote_copy(dm_ref.at[me ^ r], dmcols.at[me], send_sems.at[0, r], recv_sems.at[0, r],
                                              device_id=_peer(r), device_id_type=MESH)
            cp.start()
            sends.append(cp)
            cp = pltpu.make_async_remote_copy(sm_ref, sm_all.at[me], send_sems.at[1, r], recv_sems.at[1, r],
                                              device_id=_peer(r), device_id_type=MESH)
            cp.start()
            sends.append(cp)
        for r in range(1, N_DEV):
            pltpu.make_async_remote_copy(dm_ref.at[me ^ r], dmcols.at[me], send_sems.at[0, r], recv_sems.at[0, r],
                                         device_id=_peer(r), device_id_type=MESH).wait_recv()
            pltpu.make_async_remote_copy(sm_ref, sm_all.at[me], send_sems.at[1, r], recv_sems.at[1, r],
                                         device_id=_peer(r), device_id_type=MESH).wait_recv()
        dm = dmcols[...].reshape(8 * N_DEV, W)
        gw_ref[...] = _dot_tn(ca_ref[...], dm.astype(BF16))
        first_row = lax.broadcasted_iota(jnp.int32, (8, W), 0) == 0
        gb_mine[...] = jnp.where(first_row, jnp.sum(dm, axis=0, keepdims=True), 0.0)
        gb_ref[me] = gb_mine[...]
        for r in range(1, N_DEV):
            cp = pltpu.make_async_remote_copy(gb_mine, gb_ref.at[me], send_sems.at[2, r], recv_sems.at[2, r],
                                              device_id=_peer(r), device_id_type=MESH)
            cp.start()
            sends.append(cp)
        total = sm_all[0]
        for k in range(1, N_DEV):
            total = total + sm_all[k]
        ssum_ref[...] = total
        for r in range(1, N_DEV):
            pltpu.make_async_remote_copy(gb_mine, gb_ref.at[me], send_sems.at[2, r], recv_sems.at[2, r],
                                         device_id=_peer(r), device_id_type=MESH).wait_recv()
        for cp in sends:
            cp.wait_send()

    vm = pl.BlockSpec(memory_space=pltpu.VMEM)
    return pl.pallas_call(
        body, name="mod_bwd", in_specs=[vm, vm, vm], out_specs=[vm, vm, vm],
        out_shape=[jax.ShapeDtypeStruct((D, W), F32), jax.ShapeDtypeStruct((N_DEV, 8, W), F32),
                   jax.ShapeDtypeStruct((8, N_SMALL), F32)],
        scratch_shapes=[pltpu.VMEM((N_DEV, 8, W), F32), pltpu.VMEM((N_DEV, 8, N_SMALL), F32), pltpu.VMEM((8, W), F32),
                        pltpu.SemaphoreType.DMA((3, N_DEV)), pltpu.SemaphoreType.DMA((3, N_DEV))],
        compiler_params=_params(),
    )(dmod3, small_tile, ca)


def _wgather(shards):
    n = len(shards)

    def body(*refs):
        ins, outs, token = refs[:n], refs[n:2 * n], refs[2 * n]
        send_sems, recv_sems, local_sems = refs[2 * n + 1:]
        token[...] = jnp.zeros_like(token)
        x, y, c = _coords()
        me = 4 * x + 2 * y + c
        sib = (x, y, 1 - c)
        chips = [(1 - x, y), (x, 1 - y), (1 - x, 1 - y)]

        def copy(k, slot, block, to, src=None):
            return pltpu.make_async_remote_copy(
                src_ref=outs[k].at[block] if src is None else src, dst_ref=outs[k].at[block],
                send_sem=send_sems.at[k, slot], recv_sem=recv_sems.at[k, slot], device_id=to, device_id_type=MESH)

        local = [pltpu.make_async_copy(ins[k], outs[k].at[me], local_sems.at[k]) for k in range(n)]
        for cp in local:
            cp.start()
        first = []
        for k in range(n):
            first.append(copy(k, 0, me, sib, src=ins[k]))
            for j, chip in enumerate(chips):
                first.append(copy(k, 1 + j, me, (*chip, c), src=ins[k]))
        for cp in first:
            cp.start()
        passed = []
        for j, (cx, cy) in enumerate(chips):
            for k in range(n):
                blk = 4 * cx + 2 * cy + c
                copy(k, 1 + j, blk, sib).wait_recv()
                cp = copy(k, 4 + j, blk, sib)
                cp.start()
                passed.append(cp)
        for k in range(n):
            copy(k, 0, 4 * x + 2 * y + (1 - c), sib).wait_recv()
            for j, (cx, cy) in enumerate(chips):
                copy(k, 4 + j, 4 * cx + 2 * cy + (1 - c), sib).wait_recv()
        for cp in first + passed:
            cp.wait_send()
        for cp in local:
            cp.wait()

    anyspec = pl.BlockSpec(memory_space=pl.ANY)
    return pl.pallas_call(
        body, name="wgather", in_specs=[anyspec] * n,
        out_specs=[anyspec] * n + [pl.BlockSpec(memory_space=pltpu.VMEM)],
        out_shape=[jax.ShapeDtypeStruct((N_DEV,) + s.shape, s.dtype) for s in shards]
        + [jax.ShapeDtypeStruct((8, 128), F32)],
        scratch_shapes=[pltpu.SemaphoreType.DMA((n, 7)), pltpu.SemaphoreType.DMA((n, 7)),
                        pltpu.SemaphoreType.DMA((n,))],
    )(*shards)


class _GatherCopies:
    def __init__(self, lands, send_sems, recv_sems):
        x, y, c = _coords()
        me = 4 * x + 2 * y + c
        sib = (x, y, 1 - c)
        chips = [(1 - x, y), (x, 1 - y), (1 - x, 1 - y)]

        def copy(k, slot, block, to):
            return pltpu.make_async_remote_copy(
                src_ref=lands[k].at[block], dst_ref=lands[k].at[block],
                send_sem=send_sems.at[7 * k + slot], recv_sem=recv_sems.at[7 * k + slot],
                device_id=to, device_id_type=MESH)

        n = len(lands)
        self.first = [copy(k, 0, me, sib) for k in range(n)]
        self.first += [copy(k, 1 + j, me, (cx, cy, c)) for j, (cx, cy) in enumerate(chips) for k in range(n)]
        self.landed = [copy(k, 1 + j, 4 * cx + 2 * cy + c, sib) for j, (cx, cy) in enumerate(chips) for k in range(n)]
        self.passed = [copy(k, 4 + j, 4 * cx + 2 * cy + c, sib) for j, (cx, cy) in enumerate(chips) for k in range(n)]
        self.from_sib = [copy(k, 0, 4 * x + 2 * y + (1 - c), sib) for k in range(n)]
        self.from_sib += [copy(k, 4 + j, 4 * cx + 2 * cy + (1 - c), sib) for j, (cx, cy) in enumerate(chips)
                          for k in range(n)]


def _gather_start(lands, *, name):
    n = len(lands)

    def body(*refs):
        for cp in _GatherCopies(refs[:n], refs[n], refs[n + 1]).first:
            cp.start()
        refs[-1][...] = jnp.zeros_like(refs[-1])

    out = pl.pallas_call(
        body, name=name,
        out_shape=(pltpu.SemaphoreType.DMA((7 * n,)), pltpu.SemaphoreType.DMA((7 * n,)),
                   *[pltpu.HBM(l.shape, l.dtype) for l in lands], jax.ShapeDtypeStruct((8, 128), F32)),
        in_specs=[HBM_SPEC] * n,
        out_specs=(SEM_SPEC, SEM_SPEC, *[HBM_SPEC] * n, pl.BlockSpec(memory_space=pltpu.VMEM)),
        input_output_aliases={i: 2 + i for i in range(n)},
        compiler_params=pltpu.CompilerParams(has_side_effects=DATAFLOW),
    )(*[_in_hbm(l) for l in lands])
    return out[0], out[1], list(out[2:2 + n]), out[-1]


def _gather_pass(send_sems, recv_sems, lands, after, *, name, stage):
    n = len(lands)

    def body(*refs):
        cps = _GatherCopies(refs[:n], refs[n], refs[n + 1])
        if stage == "landed":
            for cp in cps.landed:
                cp.wait_recv()
        else:
            for cp in cps.passed:
                cp.start()
        refs[-1][...] = jnp.zeros_like(refs[-1])

    out = pl.pallas_call(
        body, name=name,
        out_shape=(*[pltpu.HBM(l.shape, l.dtype) for l in lands], jax.ShapeDtypeStruct((8, 128), F32)),
        in_specs=[HBM_SPEC] * n + [SEM_SPEC, SEM_SPEC] + [pl.BlockSpec(memory_space=pl.ANY)] * len(after),
        out_specs=(*[HBM_SPEC] * n, pl.BlockSpec(memory_space=pltpu.VMEM)),
        input_output_aliases={i: i for i in range(n)},
        compiler_params=pltpu.CompilerParams(has_side_effects=DATAFLOW),
    )(*lands, send_sems, recv_sems, *after)
    return list(out[:n]), out[-1]


def _gather_end(send_sems, recv_sems, lands, after, *, name):
    n = len(lands)

    def body(*refs):
        cps = _GatherCopies(refs[:n], refs[n], refs[n + 1])
        for cp in cps.from_sib:
            cp.wait_recv()
        for cp in cps.first + cps.passed:
            cp.wait_send()

    out = pl.pallas_call(
        body, name=name,
        out_shape=[pltpu.HBM(l.shape, l.dtype) for l in lands],
        in_specs=[HBM_SPEC] * n + [SEM_SPEC, SEM_SPEC] + [pl.BlockSpec(memory_space=pl.ANY)] * len(after),
        out_specs=[HBM_SPEC] * n,
        input_output_aliases={i: i for i in range(n)},
        compiler_params=pltpu.CompilerParams(has_side_effects=DATAFLOW),
    )(*lands, send_sems, recv_sems, *after)
    return list(out)


def _rs_d2d(grads, *, name):
    n = len(grads)

    def body(*refs):
        ins, outs = refs[:n], refs[n:2 * n]
        send_sems, recv_sems = refs[2 * n:]
        x, y, c = _coords()
        sib = (x, y, 1 - c)
        cps = []
        for k in range(n):
            for q in range(4):
                cps.append(pltpu.make_async_remote_copy(
                    src_ref=ins[k].at[2 * q + (1 - c)], dst_ref=outs[k].at[q],
                    send_sem=send_sems.at[k, q], recv_sem=recv_sems.at[k, q], device_id=sib, device_id_type=MESH))
        for cp in cps:
            cp.start()
        for cp in cps:
            cp.wait_recv()
        for cp in cps:
            cp.wait_send()

    anyspec = pl.BlockSpec(memory_space=pl.ANY)
    return pl.pallas_call(
        body, name=name, in_specs=[anyspec] * n, out_specs=[anyspec] * n,
        out_shape=[jax.ShapeDtypeStruct((4,) + g.shape[1:], g.dtype) for g in grads],
        scratch_shapes=[pltpu.SemaphoreType.DMA((n, 4)), pltpu.SemaphoreType.DMA((n, 4))],
    )(*grads)


def _chipsum(g, sib, cidx, *, name):
    _, r, cc = g.shape

    def body(c_ref, g_ref, s_ref, o_ref):
        o_ref[...] = (g_ref[...].astype(F32) + s_ref[...].astype(F32)).astype(o_ref.dtype)

    return pl.pallas_call(
        body, name=name,
        grid_spec=pltpu.PrefetchScalarGridSpec(
            num_scalar_prefetch=1, grid=(4,),
            in_specs=[pl.BlockSpec((1, r, cc), lambda q, c_ref: (2 * q + c_ref[0], 0, 0)),
                      pl.BlockSpec((1, r, cc), lambda q, c_ref: (q, 0, 0))],
            out_specs=pl.BlockSpec((1, r, cc), lambda q, c_ref: (q, 0, 0))),
        out_shape=jax.ShapeDtypeStruct((4, r, cc), g.dtype),
        compiler_params=_params(("arbitrary",)),
    )(cidx, g, sib)


HBM_SPEC = pl.BlockSpec(memory_space=pltpu.HBM)
SEM_SPEC = pl.BlockSpec(memory_space=pltpu.SEMAPHORE)
DATAFLOW = pltpu.SideEffectType.DATAFLOW_SIDE_EFFECTING


def _in_hbm(a):
    return pltpu.with_memory_space_constraint(a, pltpu.HBM)


def _ici_copies(sums, lands, send_sems, recv_sems):
    x, y, c = _coords()
    chips = [(1 - x, y), (x, 1 - y), (1 - x, 1 - y)]
    cps = []
    for k in range(len(sums)):
        for j, (cx, cy) in enumerate(chips):
            cps.append(pltpu.make_async_remote_copy(
                src_ref=sums[k].at[2 * cx + cy], dst_ref=lands[k].at[j],
                send_sem=send_sems.at[3 * k + j], recv_sem=recv_sems.at[3 * k + j],
                device_id=(cx, cy, c), device_id_type=MESH))
    return cps


def _rs_ici_start(sums, *, name):
    n = len(sums)

    def body(*refs):
        token = refs[-1]
        for cp in _ici_copies(refs[:n], refs[n:2 * n], refs[2 * n], refs[2 * n + 1]):
            cp.start()
        token[...] = jnp.zeros_like(token)

    lands = [_in_hbm(lax.empty((3,) + s.shape[1:], s.dtype)) for s in sums]
    out = pl.pallas_call(
        body, name=name,
        out_shape=(pltpu.SemaphoreType.DMA((3 * n,)), pltpu.SemaphoreType.DMA((3 * n,)),
                   *[pltpu.HBM(s.shape, s.dtype) for s in sums], *[pltpu.HBM(l.shape, l.dtype) for l in lands],
                   jax.ShapeDtypeStruct((8, 128), F32)),
        in_specs=[HBM_SPEC] * (2 * n),
        out_specs=(SEM_SPEC, SEM_SPEC, *[HBM_SPEC] * (2 * n), pl.BlockSpec(memory_space=pltpu.VMEM)),
        input_output_aliases={i: 2 + i for i in range(2 * n)},
        compiler_params=pltpu.CompilerParams(has_side_effects=DATAFLOW),
    )(*[_in_hbm(s) for s in sums], *lands)
    return out[0], out[1], list(out[2:2 + n]), list(out[2 + n:2 + 2 * n]), out[-1]


def _rs_ici_wait(send_sems, recv_sems, sums, lands, after, *, name):
    n = len(sums)

    def body(*refs):
        for cp in _ici_copies(refs[:n], refs[n:2 * n], refs[2 * n], refs[2 * n + 1]):
            cp.wait_send()
            cp.wait_recv()

    out = pl.pallas_call(
        body, name=name,
        out_shape=[pltpu.HBM(a.shape, a.dtype) for a in list(sums) + list(lands)],
        in_specs=[HBM_SPEC] * (2 * n) + [SEM_SPEC, SEM_SPEC] + [pl.BlockSpec(memory_space=pl.ANY)] * len(after),
        out_specs=[HBM_SPEC] * (2 * n),
        input_output_aliases={i: i for i in range(2 * n)},
        compiler_params=pltpu.CompilerParams(has_side_effects=DATAFLOW),
    )(*sums, *lands, send_sems, recv_sems, *after)
    return list(out[:n]), list(out[n:])


ADAM_C1 = 1.0 / (1.0 - ADAM_B1 ** ADAM_STEP)
ADAM_C2 = 1.0 / (1.0 - ADAM_B2 ** ADAM_STEP)


def _adam_math(w, g, m, v):
    m2 = ADAM_B1 * m + (1.0 - ADAM_B1) * g
    v2 = ADAM_B2 * v + (1.0 - ADAM_B2) * (g * g)
    return -ADAM_LR * ((m2 * ADAM_C1) / (jnp.sqrt(v2 * ADAM_C2) + ADAM_EPS) + ADAM_WD * w), m2, v2


def _adamw(w, g, m, v, *, name):
    R, C = w.shape
    tr = R if R <= 512 else 256

    def body(w_ref, g_ref, m_ref, v_ref, d_ref, nm_ref, nv_ref):
        d_ref[...], nm_ref[...], nv_ref[...] = _adam_math(w_ref[...], g_ref[...], m_ref[...], v_ref[...])

    blk = pl.BlockSpec((tr, C), lambda i: (i, 0))
    return pl.pallas_call(
        body, name=name, grid=(R // tr,), in_specs=[blk] * 4, out_specs=[blk] * 3,
        out_shape=[jax.ShapeDtypeStruct((R, C), F32)] * 3,
        compiler_params=_params(("parallel",)),
    )(w, g, m, v)


def _adamw_rs(w, m, v, cs, rcv, qidx, *, name):
    r, cc = w.shape
    tr = r // 2 if r % 32 == 0 and r > 128 else r

    def body(q_ref, w_ref, m_ref, v_ref, c_ref, r_ref, g_ref, d_ref, nm_ref, nv_ref):
        g = ((c_ref[0].astype(F32) + r_ref[0].astype(F32)) + r_ref[1].astype(F32)) + r_ref[2].astype(F32)
        g_ref[...] = g
        d_ref[...], nm_ref[...], nv_ref[...] = _adam_math(w_ref[...], g, m_ref[...], v_ref[...])

    blk = pl.BlockSpec((tr, cc), lambda i, q_ref: (i, 0))
    return pl.pallas_call(
        body, name=name,
        grid_spec=pltpu.PrefetchScalarGridSpec(
            num_scalar_prefetch=1, grid=(r // tr,),
            in_specs=[blk, blk, blk, pl.BlockSpec((1, tr, cc), lambda i, q_ref: (q_ref[0], i, 0)),
                      pl.BlockSpec((3, tr, cc), lambda i, q_ref: (0, i, 0))],
            out_specs=[blk] * 4),
        out_shape=[jax.ShapeDtypeStruct((r, cc), F32)] * 4,
        compiler_params=_params(("arbitrary",)),
    )(qidx, w, m, v, cs, rcv)


SMALL_PARAMS = ("norm_ffn1", "norm_mix", "norm_ffn2", "norm_final", "q_norm", "kv_norm", "sinks", "rel_bias", "b_mod")


def _adamw_small(ssum, gb3, wmv):
    widths = [wmv[3 * i].shape[1] for i in range(len(SMALL_PARAMS))]
    WM = gb3.shape[2]

    def body(*refs):
        ssum_ref, gb_ref = refs[0], refs[1]
        ins = refs[2:2 + 3 * len(SMALL_PARAMS)]
        outs = refs[2 + 3 * len(SMALL_PARAMS):]
        off = 0
        for i, name in enumerate(SMALL_PARAMS):
            n = widths[i]
            g_ref, d_ref, nm_ref, nv_ref = outs[4 * i:4 * i + 4]
            w_ref, m_ref, v_ref = ins[3 * i:3 * i + 3]
            if name == "b_mod":
                for k in range(N_DEV):
                    cols = slice(WM * k, WM * (k + 1))
                    g = gb_ref[k, 0:1, :]
                    g_ref[:, cols] = g
                    d_ref[:, cols], nm_ref[:, cols], nv_ref[:, cols] = _adam_math(
                        w_ref[:, cols], g, m_ref[:, cols], v_ref[:, cols])
            else:
                g = ssum_ref[0:1, off:off + n]
                g_ref[...] = g
                d_ref[...], nm_ref[...], nv_ref[...] = _adam_math(w_ref[...], g, m_ref[...], v_ref[...])
                off += dict(SMALL_LAYOUT)[name]

    vm = pl.BlockSpec(memory_space=pltpu.VMEM)
    n_out = 4 * len(SMALL_PARAMS)
    out = pl.pallas_call(
        body, name="adamw_small", in_specs=[vm] * (2 + len(wmv)), out_specs=[vm] * n_out,
        out_shape=[jax.ShapeDtypeStruct((1, widths[i // 4]), F32) for i in range(n_out)],
        compiler_params=_params(),
    )(ssum, gb3, *wmv)
    return {name: out[4 * i:4 * i + 4] for i, name in enumerate(SMALL_PARAMS)}


TRANSPOSED = ("g1T", "u1T", "g3T", "u3T", "w_inT", "w_uqT")


def kernel(x, c, w_mod, b_mod, norm_ffn1, ffn1_gate, ffn1_up, ffn1_down, norm_mix, w_in, q_norm, kv_norm, w_uq, w_ukv, sinks, w_o, norm_ffn2, ffn2_gate, ffn2_up, ffn2_down, rel_bias, norm_final, loss_target, m_w_mod, m_b_mod, m_norm_ffn1, m_ffn1_gate, m_ffn1_up, m_ffn1_down, m_norm_mix, m_w_in, m_q_norm, m_kv_norm, m_w_uq, m_w_ukv, m_sinks, m_w_o, m_norm_ffn2, m_ffn2_gate, m_ffn2_up, m_ffn2_down, m_rel_bias, m_norm_final, v_w_mod, v_b_mod, v_norm_ffn1, v_ffn1_gate, v_ffn1_up, v_ffn1_down, v_norm_mix, v_w_in, v_q_norm, v_kv_norm, v_w_uq, v_w_ukv, v_sinks, v_w_o, v_norm_ffn2, v_ffn2_gate, v_ffn2_up, v_ffn2_down, v_rel_bias, v_norm_final):
    mx, my, mc = _coords()
    cidx = jnp.reshape(mc, (1,)).astype(jnp.int32)
    qidx = jnp.reshape(2 * mx + my, (1,)).astype(jnp.int32)
    WM = w_mod.shape[2]

    c_tile = jnp.pad(c, ((0, 7), (0, 0)))
    b_mod3 = jnp.pad(b_mod.reshape(N_DEV, 1, WM), ((0, 0), (0, 7), (0, 0)))
    mod3, ca = _mod_fwd(c_tile, w_mod[0], b_mod3)
    mod9 = mod3[:, 0, :].reshape(N_MOD, D)

    shards = {"g1T": ffn1_gate[0].T.astype(BF16), "u1T": ffn1_up[0].T.astype(BF16), "d1": ffn1_down[0].astype(BF16),
              "g3T": ffn2_gate[0].T.astype(BF16), "u3T": ffn2_up[0].T.astype(BF16), "d3": ffn2_down[0].astype(BF16),
              "w_inT": w_in[0].T, "w_uqT": w_uq[0].T.astype(BF16), "w_ukv": w_ukv[0].astype(BF16),
              "w_o": w_o[0].astype(BF16)}
    me = 4 * mx + 2 * my + mc
    groups = {"ffn1": ("g1T", "u1T", "d1"), "mixer": ("w_inT", "w_uqT", "w_ukv", "w_o"), "ffn2": ("g3T", "u3T", "d3")}
    arriving = {}

    def as_weights(group, gathered):
        return {k: g if k == "w_ukv" else g.reshape(N_DEV * g.shape[1], g.shape[2])
                for k, g in zip(groups[group], gathered)}

    def start_gather(group, token):
        lands = []
        for k in groups[group]:
            sh = shards[k] + token[0, 0].astype(shards[k].dtype)
            lands.append(lax.dynamic_update_slice(lax.empty((N_DEV,) + sh.shape, sh.dtype), sh[None], (me, 0, 0)))
        send, recv, lands, started = _gather_start(lands, name="gather_start_" + group)
        arriving[group] = (send, recv, lands)
        return started

    def fetch(group, after, vecs):
        if group == "ffn1":
            *gathered, token = _wgather([shards[k] + ca[1, 0].astype(shards[k].dtype) for k in groups["ffn1"]])
            token = start_gather("ffn2", start_gather("mixer", token))
            return as_weights("ffn1", gathered), vecs + token[0:1, 0:1]
        def pass_on(group, after):
            send, recv, lands = arriving[group]
            lands, token = _gather_pass(send, recv, lands, after, name="gather_landed_" + group, stage="landed")
            lands, token = _gather_pass(send, recv, lands, [token], name="gather_onward_" + group, stage="onward")
            arriving[group] = (send, recv, lands)
            return token

        if group == "ffn2_on_its_way":
            return None, vecs + pass_on("ffn2", after)[0:1, 0:1]
        if group == "mixer":
            after = [pass_on("mixer", after)]
        send, recv, lands = arriving[group]
        return as_weights(group, _gather_end(send, recv, lands, after, name="gather_end_" + group)), vecs

    norms ={"ffn1": norm_ffn1, "mix": norm_mix, "ffn2": norm_ffn2, "final": norm_final.reshape(1, D)}
    in_flight = {}

    def on_grads(group, g, vecs):
        names = list(g)
        by_dest = [g[k] if k == "w_ukv" else g[k].reshape((N_DEV, g[k].shape[0] // N_DEV) + g[k].shape[1:])
                   for k in names]
        from_sib = _rs_d2d(by_dest, name="rs_d2d_" + group)
        sums = [_chipsum(a, s, cidx, name="chipsum_" + k) for k, a, s in zip(names, by_dest, from_sib)]
        send, recv, sums, lands, token = _rs_ici_start(sums, name="rs_ici_start_" + group)
        in_flight[group] = (names, send, recv, sums, lands, token)
        return vecs + token[0:1, 0:1]

    loss_local, grad_x, _, small, dmod9 = _local_step(
        x[0], loss_target[0], mod9, norms, sinks, rel_bias, q_norm, kv_norm, fetch, on_grads=on_grads)
    loss = lax.psum(loss_local, ("x", "y", "c"))

    owners = {"g1T": ("ffn1_gate", ffn1_gate, m_ffn1_gate, v_ffn1_gate), "u1T": ("ffn1_up", ffn1_up, m_ffn1_up, v_ffn1_up),
              "d1": ("ffn1_down", ffn1_down, m_ffn1_down, v_ffn1_down),
              "g3T": ("ffn2_gate", ffn2_gate, m_ffn2_gate, v_ffn2_gate), "u3T": ("ffn2_up", ffn2_up, m_ffn2_up, v_ffn2_up),
              "d3": ("ffn2_down", ffn2_down, m_ffn2_down, v_ffn2_down),
              "w_inT": ("w_in", w_in, m_w_in, v_w_in), "w_uqT": ("w_uq", w_uq, m_w_uq, v_w_uq),
              "w_ukv": ("w_ukv", w_ukv, m_w_ukv, v_w_ukv), "w_o": ("w_o", w_o, m_w_o, v_w_o)}
    res = {}

    def finish(group, after):
        names, send, recv, sums, lands, _ = in_flight[group]
        sums, lands = _rs_ici_wait(send, recv, sums, lands, after, name="rs_ici_wait_" + group)
        for k, cs, rc in zip(names, sums, lands):
            pname, wk, mk, vk = owners[k]
            there = (lambda a: a[0].T) if k in TRANSPOSED else (lambda a: a[0])
            back = (lambda a: a.T[None]) if k in TRANSPOSED else (lambda a: a[None])
            res[pname] = tuple(back(a) for a in _adamw_rs(there(wk), there(mk), there(vk), cs, rc, qidx,
                                                          name="adamw_" + pname))

    ffn1_started = in_flight["ffn1"][5]
    finish("ffn2", [ffn1_started])
    finish("mixer", [ffn1_started])

    dmod3 = jnp.pad(dmod9.reshape(N_DEV, 1, WM), ((0, 0), (0, 7), (0, 0)))
    small_tile = jnp.pad(small.reshape(1, N_SMALL), ((0, 7), (0, 0)))
    g_wmod, gb3, ssum = _mod_bwd(dmod3, small_tile, ca)
    res["w_mod"] = tuple(a[None] for a in (g_wmod,) + tuple(_adamw(w_mod[0], g_wmod, m_w_mod[0], v_w_mod[0],
                                                                    name="adamw_w_mod")))
    small_in = {"norm_ffn1": (norm_ffn1, m_norm_ffn1, v_norm_ffn1), "norm_mix": (norm_mix, m_norm_mix, v_norm_mix),
                "norm_ffn2": (norm_ffn2, m_norm_ffn2, v_norm_ffn2), "norm_final": (norm_final, m_norm_final, v_norm_final),
                "q_norm": (q_norm, m_q_norm, v_q_norm), "kv_norm": (kv_norm, m_kv_norm, v_kv_norm),
                "sinks": (sinks, m_sinks, v_sinks), "rel_bias": (rel_bias, m_rel_bias, v_rel_bias),
                "b_mod": (b_mod, m_b_mod, v_b_mod)}
    small_out = _adamw_small(ssum, gb3, [a.reshape(1, -1) for k in SMALL_PARAMS for a in small_in[k]])
    for k in SMALL_PARAMS:
        res[k] = tuple(a.reshape(small_in[k][0].shape) for a in small_out[k])

    finish("ffn1", [res[k][3] for k in ("w_mod", "w_o", "ffn2_down", "b_mod")])

    order = ("w_mod", "b_mod", "norm_ffn1", "ffn1_gate", "ffn1_up", "ffn1_down", "norm_mix", "w_in", "q_norm",
             "kv_norm", "w_uq", "w_ukv", "sinks", "w_o", "norm_ffn2", "ffn2_gate", "ffn2_up", "ffn2_down",
             "rel_bias", "norm_final")
    return (loss, grad_x[None]) + tuple(res[nm][kind] for kind in range(4) for nm in order)
```

```python
import functools
import math

import numpy as np
import jax
import jax.numpy as jnp
from jax import lax
from jax.experimental import pallas as pl
from jax.experimental.pallas import tpu as pltpu

F32 = jnp.float32
BF16 = jnp.bfloat16
MESH = pl.DeviceIdType.MESH

N_DEV = 8
D = 1024
D_FF = 2816
EPS = 1e-6
N_MOD = 9
SWA_HEADS = 8
SWA_DH = 64
WINDOW = 128
MLA_HEADS = 4
MLA_NOPE = 128
MLA_ROPE = 64
MLA_V = 128
MLA_QR = 256
MLA_KVR = 128
ROPE_THETA = 10000.0
NUM_BUCKETS = 32
D_IN = 1216
D_IN_PAD = 1280
SWA_SCALE = SWA_DH ** -0.5
MLA_SCALE = (MLA_NOPE + MLA_ROPE) ** -0.5

ADAM_LR = 0.001
ADAM_B1 = 0.9
ADAM_B2 = 0.999
ADAM_EPS = 1e-08
ADAM_WD = 0.01
ADAM_STEP = 10

V7X_VMEM_LIMIT = 56 * 1024 * 1024

NT_DIMS = (((1,), (1,)), ((), ()))
TN_DIMS = (((0,), (0,)), ((), ()))


def _dot(a, b):
    return jnp.dot(a, b, preferred_element_type=F32)


def _dot_nt(a, b):
    return lax.dot_general(a, b, NT_DIMS, preferred_element_type=F32)


def _dot_tn(a, b):
    return lax.dot_general(a, b, TN_DIMS, preferred_element_type=F32)


def _params(sem=None):
    return pltpu.CompilerParams(dimension_semantics=sem, vmem_limit_bytes=V7X_VMEM_LIMIT)


def _rstd(x):
    return lax.rsqrt(jnp.mean(x * x, axis=-1, keepdims=True) + EPS)


def _rms_bwd(dy, xhat, r):
    return r * (dy - xhat * jnp.mean(dy * xhat, axis=-1, keepdims=True))


def _sigmoid(a):
    return 1.0 / (1.0 + jnp.exp(-a))


def _ffn_fwd(x, vecs, wgT, wuT, wd, *, name, tm=512, tf=256):
    S = x.shape[0]
    tm = min(tm, S)
    ni, nj = S // tm, D_FF // tf

    def body(x_ref, vec_ref, wg_ref, wu_ref, wd_ref, xo_ref, h_ref, a_ref, b_ref, f_ref, acc_ref):
        j = pl.program_id(1)

        @pl.when(j == 0)
        def _():
            xv = x_ref[...]
            hn = xv * _rstd(xv) * vec_ref[0:1, :]
            h_ref[...] = (hn * (1.0 + vec_ref[2:3, :]) + vec_ref[1:2, :]).astype(BF16)
            acc_ref[...] = jnp.zeros_like(acc_ref)

        h = h_ref[...]
        a = _dot_nt(h, wg_ref[...])
        b = _dot_nt(h, wu_ref[...])
        a_ref[...] = a.astype(BF16)
        b_ref[...] = b.astype(BF16)
        hsw = (a * _sigmoid(a) * b).astype(BF16)
        acc_ref[...] += _dot(hsw, wd_ref[...])

        @pl.when(j == nj - 1)
        def _():
            f = acc_ref[...]
            f_ref[...] = f
            xo_ref[...] = x_ref[...] + (0.5 * vec_ref[3:4, :]) * f

    row = pl.BlockSpec((tm, D), lambda i, j: (i, 0))
    wspec = pl.BlockSpec((tf, D), lambda i, j: (j, 0))
    act = pl.BlockSpec((tm, tf), lambda i, j: (i, j))
    return pl.pallas_call(
        body, name=name, grid=(ni, nj),
        in_specs=[row, pl.BlockSpec((8, D), lambda i, j: (0, 0)), wspec, wspec, wspec],
        out_specs=[row, row, act, act, row],
        out_shape=[jax.ShapeDtypeStruct((S, D), F32), jax.ShapeDtypeStruct((S, D), BF16),
                   jax.ShapeDtypeStruct((S, D_FF), BF16), jax.ShapeDtypeStruct((S, D_FF), BF16),
                   jax.ShapeDtypeStruct((S, D), F32)],
        scratch_shapes=[pltpu.VMEM((tm, D), F32)],
        compiler_params=_params(("parallel", "arbitrary")),
    )(x, vecs, wgT, wuT, wd)


def _ffn_bwd_pre(dxo, f, vecs, *, name, tm=256):
    S = dxo.shape[0]

    def body(dx_ref, f_ref, vec_ref, df_ref, part_ref):
        @pl.when(pl.program_id(0) == 0)
        def _():
            part_ref[...] = jnp.zeros_like(part_ref)

        dx = dx_ref[...]
        df_ref[...] = ((0.5 * vec_ref[3:4, :]) * dx).astype(BF16)
        part_ref[0:1, :] += 0.5 * jnp.sum(dx * f_ref[...], axis=0, keepdims=True)

    row = pl.BlockSpec((tm, D), lambda i: (i, 0))
    vec = pl.BlockSpec((8, D), lambda i: (0, 0))
    return pl.pallas_call(
        body, name=name, grid=(S // tm,), in_specs=[row, row, vec], out_specs=[row, vec],
        out_shape=[jax.ShapeDtypeStruct((S, D), BF16), jax.ShapeDtypeStruct((8, D), F32)],
        compiler_params=_params(("arbitrary",)),
    )(dxo, f, vecs)


def _ffn_bwd_main(h, df, a, b, wgT, wuT, wd, *, name, tm=512, tf=256):
    S = h.shape[0]
    tm = min(tm, S)
    ni, nj = S // tm, D_FF // tf

    def body(h_hbm, df_hbm, a_ref, b_ref, wg_ref, wu_ref, wd_ref,
             gg_ref, gu_ref, gd_ref, dh_hbm,
             h_v, df_v, dh_v, gg_acc, gu_acc, gd_acc, sem):
        j = pl.program_id(0)
        i = pl.program_id(1)

        @pl.when((j == 0) & (i == 0))
        def _():
            c1 = pltpu.make_async_copy(h_hbm, h_v, sem.at[0])
            c2 = pltpu.make_async_copy(df_hbm, df_v, sem.at[1])
            c1.start()
            c2.start()
            c1.wait()
            c2.wait()

        @pl.when(i == 0)
        def _():
            gg_acc[...] = jnp.zeros_like(gg_acc)
            gu_acc[...] = jnp.zeros_like(gu_acc)
            gd_acc[...] = jnp.zeros_like(gd_acc)

        rows = pl.ds(pl.multiple_of(i * tm, tm), tm)
        hi = h_v[rows, :]
        dfi = df_v[rows, :]
        av = a_ref[...].astype(F32)
        bv = b_ref[...].astype(F32)
        sg = _sigmoid(av)
        sa = av * sg
        hsw = (sa * bv).astype(BF16)
        dhsw = _dot_nt(dfi, wd_ref[...])
        da = (dhsw * bv * (sg * (1.0 + av * (1.0 - sg)))).astype(BF16)
        db = (dhsw * sa).astype(BF16)
        gd_acc[...] += _dot_tn(hsw, dfi)
        gg_acc[...] += _dot_tn(da, hi)
        gu_acc[...] += _dot_tn(db, hi)
        dh = _dot(da, wg_ref[...]) + _dot(db, wu_ref[...])

        @pl.when(j == 0)
        def _():
            dh_v[rows, :] = dh

        @pl.when(j > 0)
        def _():
            dh_v[rows, :] += dh

        @pl.when(i == ni - 1)
        def _():
            gg_ref[...] = gg_acc[...].astype(BF16)
            gu_ref[...] = gu_acc[...].astype(BF16)
            gd_ref[...] = gd_acc[...].astype(BF16)

        @pl.when((j == nj - 1) & (i == ni - 1))
        def _():
            c3 = pltpu.make_async_copy(dh_v, dh_hbm, sem.at[2])
            c3.start()
            c3.wait()

    anyspec = pl.BlockSpec(memory_space=pl.ANY)
    wspec = pl.BlockSpec((tf, D), lambda j, i: (j, 0))
    act = pl.BlockSpec((tm, tf), lambda j, i: (i, j))
    return pl.pallas_call(
        body, name=name, grid=(nj, ni),
        in_specs=[anyspec, anyspec, act, act, wspec, wspec, wspec],
        out_specs=[wspec, wspec, wspec, anyspec],
        out_shape=[jax.ShapeDtypeStruct((D_FF, D), BF16)] * 3 + [jax.ShapeDtypeStruct((S, D), F32)],
        scratch_shapes=[pltpu.VMEM((S, D), BF16), pltpu.VMEM((S, D), BF16), pltpu.VMEM((S, D), F32),
                        pltpu.VMEM((tf, D), F32), pltpu.VMEM((tf, D), F32), pltpu.VMEM((tf, D), F32),
                        pltpu.SemaphoreType.DMA((3,))],
        compiler_params=_params(("arbitrary", "arbitrary")),
    )(h, df, a, b, wgT, wuT, wd)


def _norm_bwd(dh, x, dxo, vecs, *, name, tm=256):
    S = x.shape[0]

    def body(dh_ref, x_ref, dxo_ref, vec_ref, dx_ref, part_ref):
        @pl.when(pl.program_id(0) == 0)
        def _():
            part_ref[...] = jnp.zeros_like(part_ref)

        dh = dh_ref[...]
        xv = x_ref[...]
        r = _rstd(xv)
        xhat = xv * r
        w = vec_ref[0:1, :]
        xn = xhat * w
        dxn = dh * (1.0 + vec_ref[2:3, :])
        part_ref[0:1, :] += jnp.sum(dxn * xhat, axis=0, keepdims=True)
        part_ref[1:2, :] += jnp.sum(dh, axis=0, keepdims=True)
        part_ref[2:3, :] += jnp.sum(dh * xn, axis=0, keepdims=True)
        dx_ref[...] = dxo_ref[...] + _rms_bwd(dxn * w, xhat, r)

    row = pl.BlockSpec((tm, D), lambda i: (i, 0))
    vec = pl.BlockSpec((8, D), lambda i: (0, 0))
    return pl.pallas_call(
        body, name=name, grid=(S // tm,), in_specs=[row, row, row, vec], out_specs=[row, vec],
        out_shape=[jax.ShapeDtypeStruct((S, D), F32), jax.ShapeDtypeStruct((8, D), F32)],
        compiler_params=_params(("arbitrary",)),
    )(dh, x, dxo, vecs)


def _head(x, tgt, nf, *, tm=256):
    S = x.shape[0]

    def body(x_ref, t_ref, nf_ref, dx_ref, part_ref):
        @pl.when(pl.program_id(0) == 0)
        def _():
            part_ref[...] = jnp.zeros_like(part_ref)

        xv = x_ref[...]
        r = _rstd(xv)
        xhat = xv * r
        w = nf_ref[...]
        e = xhat * w - t_ref[...]
        dy = e * (1.0 / D)
        part_ref[0:1, :] += jnp.sum(dy * xhat, axis=0, keepdims=True)
        part_ref[1:2, :] += jnp.sum(e * e) * (0.5 / D)
        dx_ref[...] = _rms_bwd(dy * w, xhat, r)

    row = pl.BlockSpec((tm, D), lambda i: (i, 0))
    return pl.pallas_call(
        body, name="head", grid=(S // tm,),
        in_specs=[row, row, pl.BlockSpec((1, D), lambda i: (0, 0))],
        out_specs=[row, pl.BlockSpec((8, D), lambda i: (0, 0))],
        out_shape=[jax.ShapeDtypeStruct((S, D), F32), jax.ShapeDtypeStruct((8, D), F32)],
        compiler_params=_params(("arbitrary",)),
    )(x, tgt, nf)


def _mix_in_fwd(x, vecs, w_inT, *, tm=256):
    S = x.shape[0]

    def body(x_ref, vec_ref, w_ref, h_ref, p_ref):
        xv = x_ref[...]
        hn = xv * _rstd(xv) * vec_ref[0:1, :]
        h = (hn * (1.0 + vec_ref[2:3, :]) + vec_ref[1:2, :]).astype(BF16)
        h_ref[...] = h
        p_ref[...] = _dot_nt(h, w_ref[...])

    row = pl.BlockSpec((tm, D), lambda i: (i, 0))
    return pl.pallas_call(
        body, name="mix_in_fwd", grid=(S // tm,),
        in_specs=[row, pl.BlockSpec((8, D), lambda i: (0, 0)), pl.BlockSpec((D_IN_PAD, D), lambda i: (0, 0))],
        out_specs=[row, pl.BlockSpec((tm, D_IN_PAD), lambda i: (i, 0))],
        out_shape=[jax.ShapeDtypeStruct((S, D), BF16), jax.ShapeDtypeStruct((S, D_IN_PAD), F32)],
        compiler_params=_params(("parallel",)),
    )(x, vecs, w_inT)


def _bucket_table():
    qi = np.arange(WINDOW)[:, None]
    kj = np.arange(2 * WINDOW)[None, :]
    dist = qi + WINDOW - kj
    max_exact = NUM_BUCKETS // 2
    n = np.maximum(dist, 0)
    nf = np.maximum(n, 1).astype(np.float32)
    large = max_exact + (np.log(nf / np.float32(max_exact)) / np.float32(math.log(WINDOW / max_exact))
                         * np.float32(NUM_BUCKETS - max_exact)).astype(np.int32)
    large = np.minimum(large, NUM_BUCKETS - 1)
    return np.where(n < max_exact, n, large).astype(np.int32)


def _bias_build(rel_bias, bucket):
    def body(rb_ref, bk_ref, out_ref):
        bk = bk_ref[...]
        for h in range(SWA_HEADS):
            acc = jnp.zeros((WINDOW, 2 * WINDOW), F32)
            for b in range(NUM_BUCKETS):
                acc = jnp.where(bk == b, rb_ref[b, h], acc)
            out_ref[h] = acc

    return pl.pallas_call(
        body, name="bias_build",
        in_specs=[pl.BlockSpec(memory_space=pltpu.SMEM), pl.BlockSpec(memory_space=pltpu.VMEM)],
        out_specs=pl.BlockSpec(memory_space=pltpu.VMEM),
        out_shape=jax.ShapeDtypeStruct((SWA_HEADS, WINDOW, 2 * WINDOW), F32),
    )(rel_bias, bucket)


SWA_GROUP = 4
GROUP_ROWS = SWA_GROUP * WINDOW


def _swa_valid(n):
    row = lax.broadcasted_iota(jnp.int32, (GROUP_ROWS, 2 * WINDOW), 0) % WINDOW
    col = lax.broadcasted_iota(jnp.int32, (GROUP_ROWS, 2 * WINDOW), 1)
    dist = row + WINDOW - col
    return (dist >= 0) & (dist < WINDOW) & ((col >= WINDOW) | (n > 0))


def _stack_heads(x, g):
    return jnp.concatenate([x[:, 64 * h:64 * h + 64] for h in range(SWA_GROUP * g, SWA_GROUP * (g + 1))], axis=0)


def _unstack_heads(x4):
    return jnp.concatenate([x4[WINDOW * a:WINDOW * (a + 1)] for a in range(SWA_GROUP)], axis=1)


def _group_sinks(sink_ref, g):
    head = lax.broadcasted_iota(jnp.int32, (GROUP_ROWS, 1), 0) // WINDOW
    out = jnp.full((GROUP_ROWS, 1), sink_ref[0, SWA_GROUP * g], F32)
    for a in range(1, SWA_GROUP):
        out = jnp.where(head == a, sink_ref[0, SWA_GROUP * g + a], out)
    return out


def _swa_probs(qh, kk, bias_h, sink, valid):
    s = _dot_nt(qh, kk) * SWA_SCALE + bias_h
    s = jnp.where(valid, s, -jnp.inf)
    m = jnp.maximum(jnp.max(s, axis=-1, keepdims=True), sink)
    p = jnp.exp(s - m)
    ps = jnp.exp(sink - m)
    inv = 1.0 / (jnp.sum(p, axis=-1, keepdims=True) + ps)
    return p * inv, ps * inv


def _swa_specs():
    prev = lambda n: jnp.maximum(n - 1, 0)
    return [pl.BlockSpec((WINDOW, 512), lambda n: (n, 0)),
            pl.BlockSpec((WINDOW, 128), lambda n: (n, 4)),
            pl.BlockSpec((WINDOW, 128), lambda n: (prev(n), 4)),
            pl.BlockSpec((WINDOW, 128), lambda n: (n, 5)),
            pl.BlockSpec((WINDOW, 128), lambda n: (prev(n), 5)),
            pl.BlockSpec((SWA_HEADS, WINDOW, 2 * WINDOW), lambda n: (0, 0, 0)),
            pl.BlockSpec(memory_space=pltpu.SMEM)]


def _swa_fwd(proj, bias, sinks):
    S = proj.shape[0]

    def body(q_ref, kc_ref, kp_ref, vc_ref, vp_ref, bias_ref, sink_ref, o_ref):
        valid = _swa_valid(pl.program_id(0))
        q = q_ref[...].astype(BF16)
        kfull = jnp.concatenate([kp_ref[...], kc_ref[...]], axis=0).astype(BF16)
        vfull = jnp.concatenate([vp_ref[...], vc_ref[...]], axis=0).astype(BF16)
        for g in range(SWA_HEADS // SWA_GROUP):
            kk = kfull[:, 64 * g:64 * g + 64]
            vv = vfull[:, 64 * g:64 * g + 64]
            bias4 = bias_ref[SWA_GROUP * g:SWA_GROUP * (g + 1)].reshape(GROUP_ROWS, 2 * WINDOW)
            pk, _ = _swa_probs(_stack_heads(q, g), kk, bias4, _group_sinks(sink_ref, g), valid)
            o_ref[:, 256 * g:256 * (g + 1)] = _unstack_heads(_dot(pk.astype(BF16), vv))

    return pl.pallas_call(
        body, name="swa_fwd", grid=(S // WINDOW,),
        in_specs=_swa_specs(),
        out_specs=pl.BlockSpec((WINDOW, 512), lambda n: (n, 0)),
        out_shape=jax.ShapeDtypeStruct((S, 512), F32),
        compiler_params=_params(("parallel",)),
    )(proj, proj, proj, proj, proj, bias, sinks)


def _swa_bwd(proj, bias, sinks, o, do, bucket):
    S = proj.shape[0]
    nb = S // WINDOW

    def body(q_ref, kc_ref, kp_ref, vc_ref, vp_ref, bias_ref, sink_ref, o_ref, do_ref, bk_ref,
             dq_ref, dk_ref, dv_ref, drb_ref, dsk_ref, dbias_acc):
        n = pl.program_id(0)

        @pl.when(n == 0)
        def _():
            dk_ref[...] = jnp.zeros_like(dk_ref)
            dv_ref[...] = jnp.zeros_like(dv_ref)
            dsk_ref[...] = jnp.zeros_like(dsk_ref)
            dbias_acc[...] = jnp.zeros_like(dbias_acc)
            drb_ref[...] = jnp.zeros_like(drb_ref)

        valid = _swa_valid(n)
        q = q_ref[...].astype(BF16)
        dov = do_ref[...]
        kfull = jnp.concatenate([kp_ref[...], kc_ref[...]], axis=0).astype(BF16)
        vfull = jnp.concatenate([vp_ref[...], vc_ref[...]], axis=0).astype(BF16)
        prow = pl.ds(pl.multiple_of(jnp.maximum(n - 1, 0) * WINDOW, WINDOW), WINDOW)
        crow = pl.ds(pl.multiple_of(n * WINDOW, WINDOW), WINDOW)
        ov = o_ref[...]
        for g in range(SWA_HEADS // SWA_GROUP):
            heads = slice(SWA_GROUP * g, SWA_GROUP * (g + 1))
            kk = kfull[:, 64 * g:64 * g + 64]
            vv = vfull[:, 64 * g:64 * g + 64]
            q4 = _stack_heads(q, g)
            pk, psink = _swa_probs(q4, kk, bias_ref[heads].reshape(GROUP_ROWS, 2 * WINDOW), _group_sinks(sink_ref, g), valid)
            pkb = pk.astype(BF16)
            do4 = _stack_heads(dov, g)
            dob = do4.astype(BF16)
            dp = _dot_nt(dob, vv)
            delta = jnp.sum(do4 * _stack_heads(ov, g), axis=-1, keepdims=True)
            ds = pk * (dp - delta)
            dsink = -psink * delta
            for a in range(SWA_GROUP):
                h = SWA_GROUP * g + a
                part = jnp.sum(dsink[WINDOW * a:WINDOW * (a + 1)], keepdims=True)
                dsk_ref[h:h + 1, :] += jnp.broadcast_to(part, (1, 128))
            dbias_acc[heads] += ds.reshape(SWA_GROUP, WINDOW, 2 * WINDOW)
            dsb = (ds * SWA_SCALE).astype(BF16)
            dq_ref[:, 256 * g:256 * (g + 1)] = _unstack_heads(_dot(dsb, kk))
            dkk = _dot_tn(dsb, q4)
            dvv = _dot_tn(pkb, dob)
            dk_ref[prow, 64 * g:64 * g + 64] += dkk[:WINDOW]
            dk_ref[crow, 64 * g:64 * g + 64] += dkk[WINDOW:]
            dv_ref[prow, 64 * g:64 * g + 64] += dvv[:WINDOW]
            dv_ref[crow, 64 * g:64 * g + 64] += dvv[WINDOW:]

        @pl.when(n == nb - 1)
        def _():
            bk = bk_ref[...]
            for h in range(SWA_HEADS):
                dbh = dbias_acc[h]
                for b in range(NUM_BUCKETS):
                    val = jnp.sum(jnp.where(bk == b, dbh, 0.0), keepdims=True)
                    drb_ref[b * 8 + h:b * 8 + h + 1, :] = jnp.broadcast_to(val, (1, 128))

    full = lambda shape: pl.BlockSpec(shape, lambda n: tuple(0 for _ in shape))
    return pl.pallas_call(
        body, name="swa_bwd", grid=(nb,),
        in_specs=_swa_specs() + [pl.BlockSpec((WINDOW, 512), lambda n: (n, 0)),
                                 pl.BlockSpec((WINDOW, 512), lambda n: (n, 0)), full((WINDOW, 2 * WINDOW))],
        out_specs=[pl.BlockSpec((WINDOW, 512), lambda n: (n, 0)), full((S, 128)), full((S, 128)),
                   full((NUM_BUCKETS * 8, 128)), full((8, 128))],
        out_shape=[jax.ShapeDtypeStruct((S, 512), F32), jax.ShapeDtypeStruct((S, 128), F32),
                   jax.ShapeDtypeStruct((S, 128), F32), jax.ShapeDtypeStruct((NUM_BUCKETS * 8, 128), F32),
                   jax.ShapeDtypeStruct((8, 128), F32)],
        scratch_shapes=[pltpu.VMEM((SWA_HEADS, WINDOW, 2 * WINDOW), F32)],
        compiler_params=_params(("arbitrary",)),
    )(proj, proj, proj, proj, proj, bias, sinks, o, do, bucket)


def _rope_tables(S):
    inv = ROPE_THETA ** (-jnp.arange(0, MLA_ROPE, 2, dtype=F32) / MLA_ROPE)
    ang = jnp.arange(S, dtype=F32)[:, None] * inv[None, :]
    cos, sin = jnp.cos(ang), jnp.sin(ang)
    return jnp.tile(jnp.concatenate([cos, cos], axis=1), (1, 4)), jnp.tile(jnp.concatenate([-sin, sin], axis=1), (1, 4))


def _swap_halves(x):
    w = x.shape[-1]
    lane = lax.broadcasted_iota(jnp.int32, x.shape, x.ndim - 1)
    return jnp.where((lane % 64) < 32, pltpu.roll(x, w - 32, x.ndim - 1), pltpu.roll(x, 32, x.ndim - 1))


def _mla_pre_fwd(proj, qn_w, kvn_w, wuqT, wukv, cos, sin, *, tm=256):
    S = proj.shape[0]

    def body(ql_ref, kl_ref, kr_ref, qw_ref, kw_ref, wuq_ref, wukv_ref, cos_ref, sin_ref,
             qc_ref, kc_ref, vv_ref):
        ql = ql_ref[...]
        qn = (ql * _rstd(ql) * qw_ref[...]).astype(BF16)
        q = _dot_nt(qn, wuq_ref[...])
        cs, sn = cos_ref[...], sin_ref[...]
        qr = q[:, 512:768]
        qr = qr * cs + _swap_halves(qr) * sn
        half = lax.broadcasted_iota(jnp.int32, (tm, 128), 1) // 64
        kl = kl_ref[...]
        kvn = (kl * _rstd(kl) * kw_ref[...]).astype(BF16)
        kr = kr_ref[...]
        kr = kr * cs[:, :128] + _swap_halves(kr) * sn[:, :128]
        kr2 = (kr + pltpu.roll(kr, 64, 1)).astype(BF16)
        for h in range(MLA_HEADS):
            qc_ref[h, :, 0:128] = q[:, 128 * h:128 * h + 128].astype(BF16)
            chunk = qr[:, 128 * (h // 2):128 * (h // 2) + 128]
            qc_ref[h, :, 128:256] = jnp.where(half == (h % 2), chunk, 0.0).astype(BF16)
            kc_ref[h, :, 0:128] = _dot(kvn, wukv_ref[2 * h]).astype(BF16)
            kc_ref[h, :, 128:256] = kr2
            vv_ref[h] = _dot(kvn, wukv_ref[2 * h + 1]).astype(BF16)

    const = lambda shape: pl.BlockSpec(shape, lambda i: tuple(0 for _ in shape))
    return pl.pallas_call(
        body, name="mla_pre_fwd", grid=(S // tm,),
        in_specs=[pl.BlockSpec((tm, 256), lambda i: (i, 3)), pl.BlockSpec((tm, 128), lambda i: (i, 8)),
                  pl.BlockSpec((tm, 128), lambda i: (i, 9)), const((1, 256)), const((1, 128)),
                  const((768, 256)), const((8, 128, 128)),
                  pl.BlockSpec((tm, 256), lambda i: (i, 0)), pl.BlockSpec((tm, 256), lambda i: (i, 0))],
        out_specs=[pl.BlockSpec((MLA_HEADS, tm, 256), lambda i: (0, i, 0)),
                   pl.BlockSpec((MLA_HEADS, tm, 256), lambda i: (0, i, 0)),
                   pl.BlockSpec((MLA_HEADS, tm, 128), lambda i: (0, i, 0))],
        out_shape=[jax.ShapeDtypeStruct((MLA_HEADS, S, 256), BF16), jax.ShapeDtypeStruct((MLA_HEADS, S, 256), BF16),
                   jax.ShapeDtypeStruct((MLA_HEADS, S, 128), BF16)],
        compiler_params=_params(("parallel",)),
    )(proj, proj, proj, qn_w, kvn_w, wuqT, wukv, cos, sin)


def _causal(i, j, t):
    row = i * t + lax.broadcasted_iota(jnp.int32, (t, t), 0)
    col = j * t + lax.broadcasted_iota(jnp.int32, (t, t), 1)
    return col <= row


def _mla_attn_fwd(qc, kc, vv, *, t=256):
    S = qc.shape[1]
    t = min(t, S)

    def body(q_ref, k_ref, v_ref, o_ref, l_ref):
        i = pl.program_id(0)
        diag = _causal(0, 0, t)

        def step(j, carry, masked):
            rows = pl.ds(pl.multiple_of(j * t, t), t)
            out = []
            for h in range(MLA_HEADS):
                m, l, acc = carry[h]
                s = _dot_nt(q_ref[h], k_ref[h, rows, :]) * MLA_SCALE
                if masked:
                    s = jnp.where(diag, s, -jnp.inf)
                m_new = jnp.maximum(m, jnp.max(s, axis=-1, keepdims=True))
                alpha = jnp.exp(m - m_new)
                p = jnp.exp(s - m_new)
                l = alpha * l + jnp.sum(p, axis=-1, keepdims=True)
                acc = alpha * acc + _dot(p.astype(BF16), v_ref[h, rows, :])
                out.append((m_new, l, acc))
            return tuple(out)

        init = tuple((jnp.full((t, 1), -jnp.inf, F32), jnp.zeros((t, 1), F32), jnp.zeros((t, MLA_V), F32))
                     for _ in range(MLA_HEADS))
        carry = lax.fori_loop(0, i, lambda j, c: step(j, c, False), init)
        carry = step(i, carry, True)
        for h in range(MLA_HEADS):
            m, l, acc = carry[h]
            o_ref[:, 128 * h:128 * h + 128] = acc / l
            l_ref[h] = jnp.broadcast_to(m + jnp.log(l), (t, 128))

    return pl.pallas_call(
        body, name="mla_attn_fwd", grid=(S // t,),
        in_specs=[pl.BlockSpec((MLA_HEADS, t, 256), lambda i: (0, i, 0)),
                  pl.BlockSpec((MLA_HEADS, S, 256), lambda i: (0, 0, 0)),
                  pl.BlockSpec((MLA_HEADS, S, 128), lambda i: (0, 0, 0))],
        out_specs=[pl.BlockSpec((t, 512), lambda i: (i, 0)),
                   pl.BlockSpec((MLA_HEADS, t, 128), lambda i: (0, i, 0))],
        out_shape=[jax.ShapeDtypeStruct((S, 512), F32), jax.ShapeDtypeStruct((MLA_HEADS, S, 128), F32)],
        compiler_params=_params(("parallel",)),
    )(qc, kc, vv)


def _mla_attn_bwd(qc, kc, vv, o, lse, do, *, t=256):
    S = qc.shape[1]
    t = min(t, S)
    nblk = S // t
    hp = 2

    def body(q_ref, k_ref, v_ref, o_ref, l_ref, do_ref, dq_ref, dk_ref, dv_ref):
        j = pl.program_id(1)

        @pl.when(j == 0)
        def _():
            dq_ref[...] = jnp.zeros_like(dq_ref)

        diag = _causal(0, 0, t)

        def step(i, carry, masked):
            rows = pl.ds(pl.multiple_of(i * t, t), t)
            out = []
            for h in range(hp):
                dk, dv = carry[h]
                k = k_ref[h]
                q = q_ref[h, rows, :]
                dov = do_ref[rows, 128 * h:128 * h + 128]
                lrow = l_ref[h, rows, :][:, 0:1]
                p = jnp.exp(_dot_nt(q, k) * MLA_SCALE - lrow)
                if masked:
                    p = jnp.where(diag, p, 0.0)
                dob = dov.astype(BF16)
                dv = dv + _dot_tn(p.astype(BF16), dob)
                dp = _dot_nt(dob, v_ref[h])
                delta = jnp.sum(dov * o_ref[rows, 128 * h:128 * h + 128], axis=-1, keepdims=True)
                ds = (p * (dp - delta) * MLA_SCALE).astype(BF16)
                dk = dk + _dot_tn(ds, q)
                dq_ref[h, rows, :] += _dot(ds, k)
                out.append((dk, dv))
            return tuple(out)

        init = tuple((jnp.zeros((t, 256), F32), jnp.zeros((t, MLA_V), F32)) for _ in range(hp))
        carry = step(j, init, True)
        carry = lax.fori_loop(j + 1, nblk, lambda i, c: step(i, c, False), carry)
        for h in range(hp):
            dk_ref[h] = carry[h][0]
            dv_ref[h] = carry[h][1]

    return pl.pallas_call(
        body, name="mla_attn_bwd", grid=(MLA_HEADS // hp, nblk),
        in_specs=[pl.BlockSpec((hp, S, 256), lambda g, j: (g, 0, 0)),
                  pl.BlockSpec((hp, t, 256), lambda g, j: (g, j, 0)),
                  pl.BlockSpec((hp, t, 128), lambda g, j: (g, j, 0)),
                  pl.BlockSpec((S, 128 * hp), lambda g, j: (0, g)),
                  pl.BlockSpec((hp, S, 128), lambda g, j: (g, 0, 0)),
                  pl.BlockSpec((S, 128 * hp), lambda g, j: (0, g))],
        out_specs=[pl.BlockSpec((hp, S, 256), lambda g, j: (g, 0, 0)),
                   pl.BlockSpec((hp, t, 256), lambda g, j: (g, j, 0)),
                   pl.BlockSpec((hp, t, 128), lambda g, j: (g, j, 0))],
        out_shape=[jax.ShapeDtypeStruct((MLA_HEADS, S, 256), F32), jax.ShapeDtypeStruct((MLA_HEADS, S, 256), F32),
                   jax.ShapeDtypeStruct((MLA_HEADS, S, 128), F32)],
        compiler_params=_params(("parallel", "arbitrary")),
    )(qc, kc, vv, o, lse, do)


def _mla_pre_bwd(proj, qn_w, kvn_w, wuqT, wukv, cos, sin, dqc, dkc, dvv, *, tm=256):
    S = proj.shape[0]

    def body(ql_ref, kl_ref, qw_ref, kw_ref, wuq_ref, wukv_ref, cos_ref, sin_ref, dqc_ref, dkc_ref, dvv_ref,
             dql_ref, dkl_ref, dkr_ref, gq_ref, gkv_ref, part_ref):
        @pl.when(pl.program_id(0) == 0)
        def _():
            gq_ref[...] = jnp.zeros_like(gq_ref)
            gkv_ref[...] = jnp.zeros_like(gkv_ref)
            part_ref[...] = jnp.zeros_like(part_ref)

        cs, sn = cos_ref[...], sin_ref[...]
        half = lax.broadcasted_iota(jnp.int32, (tm, 128), 1) // 64
        ql = ql_ref[...]
        rq = _rstd(ql)
        qhat = ql * rq
        qw = qw_ref[...]
        qn = (qhat * qw).astype(BF16)
        chunks = []
        for pair in range(2):
            chunks.append(jnp.where(half == 0, dqc_ref[2 * pair, :, 128:256], dqc_ref[2 * pair + 1, :, 128:256]))
        dqr = jnp.concatenate(chunks, axis=1)
        dqr = dqr * cs + _swap_halves(dqr * sn)
        dq = jnp.concatenate([dqc_ref[h, :, 0:128] for h in range(MLA_HEADS)] + [dqr], axis=1).astype(BF16)
        gq_ref[...] += _dot_tn(dq, qn)
        dqn = _dot(dq, wuq_ref[...])
        part_ref[0:1, :] += jnp.sum(dqn * qhat, axis=0, keepdims=True)
        dql_ref[...] = _rms_bwd(dqn * qw, qhat, rq)
        kl = kl_ref[...]
        rk = _rstd(kl)
        khat = kl * rk
        kw = kw_ref[...]
        kvn = (khat * kw).astype(BF16)
        dkvn = jnp.zeros((tm, MLA_KVR), F32)
        dkr2 = jnp.zeros((tm, 128), F32)
        for h in range(MLA_HEADS):
            dkn = dkc_ref[h, :, 0:128].astype(BF16)
            dvh = dvv_ref[h].astype(BF16)
            gkv_ref[2 * h] += _dot_tn(kvn, dkn)
            gkv_ref[2 * h + 1] += _dot_tn(kvn, dvh)
            dkvn += _dot_nt(dkn, wukv_ref[2 * h]) + _dot_nt(dvh, wukv_ref[2 * h + 1])
            dkr2 += dkc_ref[h, :, 128:256]
        part_ref[1:2, 0:128] += jnp.sum(dkvn * khat, axis=0, keepdims=True)
        dkl_ref[...] = _rms_bwd(dkvn * kw, khat, rk)
        dkr = jnp.where(half == 0, dkr2 + pltpu.roll(dkr2, 64, 1), 0.0)
        dkr_ref[...] = dkr * cs[:, :128] + _swap_halves(dkr * sn[:, :128])

    const = lambda shape: pl.BlockSpec(shape, lambda i: tuple(0 for _ in shape))
    heads = lambda w: pl.BlockSpec((MLA_HEADS, tm, w), lambda i: (0, i, 0))
    return pl.pallas_call(
        body, name="mla_pre_bwd", grid=(S // tm,),
        in_specs=[pl.BlockSpec((tm, 256), lambda i: (i, 3)), pl.BlockSpec((tm, 128), lambda i: (i, 8)),
                  const((1, 256)), const((1, 128)), const((768, 256)), const((8, 128, 128)),
                  pl.BlockSpec((tm, 256), lambda i: (i, 0)), pl.BlockSpec((tm, 256), lambda i: (i, 0)),
                  heads(256), heads(256), heads(128)],
        out_specs=[pl.BlockSpec((tm, 256), lambda i: (i, 0)), pl.BlockSpec((tm, 128), lambda i: (i, 0)),
                   pl.BlockSpec((tm, 128), lambda i: (i, 0)), const((768, 256)), const((8, 128, 128)), const((8, 256))],
        out_shape=[jax.ShapeDtypeStruct((S, 256), F32), jax.ShapeDtypeStruct((S, 128), F32),
                   jax.ShapeDtypeStruct((S, 128), F32), jax.ShapeDtypeStruct((768, 256), F32),
                   jax.ShapeDtypeStruct((8, 128, 128), F32), jax.ShapeDtypeStruct((8, 256), F32)],
        compiler_params=_params(("arbitrary",)),
    )(proj, proj, qn_w, kvn_w, wuqT, wukv, cos, sin, dqc, dkc, dvv)


def _mix_out_fwd(x, oa, ob, w_o, vecs, *, tm=256):
    S = x.shape[0]

    def body(x_ref, oa_ref, ob_ref, w_ref, vec_ref, xo_ref, mo_ref):
        mo = _dot(oa_ref[...].astype(BF16), w_ref[0:512, :]) + _dot(ob_ref[...].astype(BF16), w_ref[512:1024, :])
        mo_ref[...] = mo
        xo_ref[...] = x_ref[...] + vec_ref[3:4, :] * mo

    row = pl.BlockSpec((tm, D), lambda i: (i, 0))
    half = pl.BlockSpec((tm, 512), lambda i: (i, 0))
    return pl.pallas_call(
        body, name="mix_out_fwd", grid=(S // tm,),
        in_specs=[row, half, half, pl.BlockSpec((D, D), lambda i: (0, 0)), pl.BlockSpec((8, D), lambda i: (0, 0))],
        out_specs=[row, row],
        out_shape=[jax.ShapeDtypeStruct((S, D), F32), jax.ShapeDtypeStruct((S, D), F32)],
        compiler_params=_params(("parallel",)),
    )(x, oa, ob, w_o, vecs)


def _mix_out_bwd(dxo, mo, oa, ob, w_o, vecs, *, tm=256):
    S = dxo.shape[0]

    def body(dx_ref, mo_ref, oa_ref, ob_ref, w_ref, vec_ref, doa_ref, dob_ref, gw_ref, part_ref):
        @pl.when(pl.program_id(0) == 0)
        def _():
            gw_ref[...] = jnp.zeros_like(gw_ref)
            part_ref[...] = jnp.zeros_like(part_ref)

        dx = dx_ref[...]
        part_ref[0:1, :] += jnp.sum(dx * mo_ref[...], axis=0, keepdims=True)
        dmo = (vec_ref[3:4, :] * dx).astype(BF16)
        doa_ref[...] = _dot_nt(dmo, w_ref[0:512, :])
        dob_ref[...] = _dot_nt(dmo, w_ref[512:1024, :])
        gw_ref[0:512, :] += _dot_tn(oa_ref[...].astype(BF16), dmo)
        gw_ref[512:1024, :] += _dot_tn(ob_ref[...].astype(BF16), dmo)

    row = pl.BlockSpec((tm, D), lambda i: (i, 0))
    half = pl.BlockSpec((tm, 512), lambda i: (i, 0))
    return pl.pallas_call(
        body, name="mix_out_bwd", grid=(S // tm,),
        in_specs=[row, row, half, half, pl.BlockSpec((D, D), lambda i: (0, 0)), pl.BlockSpec((8, D), lambda i: (0, 0))],
        out_specs=[half, half, pl.BlockSpec((D, D), lambda i: (0, 0)), pl.BlockSpec((8, D), lambda i: (0, 0))],
        out_shape=[jax.ShapeDtypeStruct((S, 512), F32), jax.ShapeDtypeStruct((S, 512), F32),
                   jax.ShapeDtypeStruct((D, D), F32), jax.ShapeDtypeStruct((8, D), F32)],
        compiler_params=_params(("arbitrary",)),
    )(dxo, mo, oa, ob, w_o, vecs)


def _mix_in_bwd(h, w_inT, dq, dk, dv, dql, dkl, dkr, *, tm=256):
    S = h.shape[0]
    offs = (0, 512, 640, 768, 1024, 1152)
    wid = (512, 128, 128, 256, 128, 128)

    def body(h_ref, w_ref, dq_ref, dk_ref, dv_ref, dql_ref, dkl_ref, dkr_ref, dh_ref, gw_ref):
        @pl.when(pl.program_id(0) == 0)
        def _():
            gw_ref[...] = jnp.zeros_like(gw_ref)

        hv = h_ref[...]
        dh = jnp.zeros((tm, D), F32)
        for ref, o, w in zip((dq_ref, dk_ref, dv_ref, dql_ref, dkl_ref, dkr_ref), offs, wid):
            w = min(w, D_IN - o)
            dpart = ref[...][:, :w].astype(BF16)
            dh += _dot(dpart, w_ref[o:o + w, :])
            gw_ref[o:o + w, :] += _dot_tn(dpart, hv)
        dh_ref[...] = dh

    row = pl.BlockSpec((tm, D), lambda i: (i, 0))
    part = lambda w: pl.BlockSpec((tm, w), lambda i: (i, 0))
    return pl.pallas_call(
        body, name="mix_in_bwd", grid=(S // tm,),
        in_specs=[row, pl.BlockSpec((D_IN_PAD, D), lambda i: (0, 0))] + [part(w) for w in wid],
        out_specs=[row, pl.BlockSpec((D_IN, D), lambda i: (0, 0))],
        out_shape=[jax.ShapeDtypeStruct((S, D), F32), jax.ShapeDtypeStruct((D_IN, D), F32)],
        compiler_params=_params(("arbitrary",)),
    )(h, w_inT, dq, dk, dv, dql, dkl, dkr)


def _vecs(norm_w, mod9, k):
    return jnp.concatenate([norm_w.reshape(1, D), mod9[3 * k:3 * k + 3], jnp.zeros((4, D), F32)], axis=0)


def _uq_group_rows(wuqT):
    per = MLA_NOPE + MLA_ROPE
    nope = [wuqT[per * h:per * h + MLA_NOPE] for h in range(MLA_HEADS)]
    rope = [wuqT[per * h + MLA_NOPE:per * (h + 1)] for h in range(MLA_HEADS)]
    return jnp.concatenate(nope + rope, axis=0)


def _uq_ungroup_rows(g):
    parts = []
    for h in range(MLA_HEADS):
        parts += [g[MLA_NOPE * h:MLA_NOPE * (h + 1)], g[512 + MLA_ROPE * h:512 + MLA_ROPE * (h + 1)]]
    return jnp.concatenate(parts, axis=0)


def _local_step(x, tgt, mod9, norms, sinks, rel_bias, q_norm, kv_norm, W, on_grads=None):
    if on_grads is None:
        on_grads = lambda group, grads, vecs: vecs
    S = x.shape[0]
    v1 = _vecs(norms["ffn1"], mod9, 0)
    v2 = _vecs(norms["mix"], mod9, 1)
    v3 = _vecs(norms["ffn2"], mod9, 2)
    bucket = jnp.asarray(_bucket_table())
    cos, sin = _rope_tables(S)
    if isinstance(W, dict):
        full, W = W, (lambda group, after, vecs: (full, vecs))

    W1, v1 = W("ffn1", [], v1)
    x1, h1, a1, b1, f1 = _ffn_fwd(x, v1, W1["g1T"], W1["u1T"], W1["d1"], name="ffn1_fwd", tm=512, tf=1408)
    W2, v2 = W("mixer", [x1], v2)
    w_inT = jnp.pad(W2["w_inT"], ((0, D_IN_PAD - D_IN), (0, 0))).astype(BF16)
    wuqT = _uq_group_rows(W2["w_uqT"])
    h2, proj = _mix_in_fwd(x1, v2, w_inT)
    bias = _bias_build(rel_bias, bucket)
    oa = _swa_fwd(proj, bias, sinks)
    qc, kc, vv = _mla_pre_fwd(proj, q_norm, kv_norm, wuqT, W2["w_ukv"], cos, sin)
    ob, lse = _mla_attn_fwd(qc, kc, vv)
    _, v2o = W("ffn2_on_its_way", [ob], v2)
    x2, mo = _mix_out_fwd(x1, oa, ob, W2["w_o"], v2o)
    W3, v3 = W("ffn2", [x2], v3)
    x3, h3, a3, b3, f3 = _ffn_fwd(x2, v3, W3["g3T"], W3["u3T"], W3["d3"], name="ffn2_fwd", tm=512, tf=1408)
    dx3, head_part = _head(x3, tgt, norms["final"])

    df3, g3_part = _ffn_bwd_pre(dx3, f3, v3, name="ffn2_bwd_pre")
    gg3, gu3, gd3, dh3 = _ffn_bwd_main(h3, df3, a3, b3, W3["g3T"], W3["u3T"], W3["d3"], name="ffn2_bwd", tm=1024, tf=256)
    ffn2 = {"g3T": gg3, "u3T": gu3, "d3": gd3}
    v2 = on_grads("ffn2", ffn2, v2)
    dx2, n3_part = _norm_bwd(dh3, x2, dx3, v3, name="ffn2_norm_bwd")
    doa, dob, g_wo, g2_part = _mix_out_bwd(dx2, mo, oa, ob, W2["w_o"], v2)
    dq, dk, dv, drb, dsk = _swa_bwd(proj, bias, sinks, oa, doa, bucket)
    dqc, dkc, dvv = _mla_attn_bwd(qc, kc, vv, ob, lse, dob)
    dql, dkl, dkr, g_uq, g_ukv, mla_part = _mla_pre_bwd(proj, q_norm, kv_norm, wuqT, W2["w_ukv"], cos, sin, dqc, dkc, dvv)
    dh2, g_win = _mix_in_bwd(h2, w_inT, dq, dk, dv, dql, dkl, dkr)
    mixer = {"w_inT": g_win, "w_uqT": _uq_ungroup_rows(g_uq).astype(BF16),
             "w_ukv": g_ukv.astype(BF16), "w_o": g_wo.astype(BF16)}
    v1 = on_grads("mixer", mixer, v1)
    dx1, n2_part = _norm_bwd(dh2, x1, dx2, v2, name="mix_norm_bwd")
    df1, g1_part = _ffn_bwd_pre(dx1, f1, v1, name="ffn1_bwd_pre")
    gg1, gu1, gd1, dh1 = _ffn_bwd_main(h1, df1, a1, b1, W1["g1T"], W1["u1T"], W1["d1"], name="ffn1_bwd", tm=2048, tf=256)
    ffn1 = {"g1T": gg1, "u1T": gu1, "d1": gd1}
    v1 = on_grads("ffn1", ffn1, v1)
    dx0, n1_part = _norm_bwd(dh1, x, dx1, v1, name="ffn1_norm_bwd")

    grads = {**ffn1, **ffn2, **mixer}
    dmod9 = jnp.concatenate([n1_part[1:3], g1_part[0:1], n2_part[1:3], g2_part[0:1],
                             n3_part[1:3], g3_part[0:1]], axis=0)
    small = jnp.concatenate([n1_part[0], n2_part[0], n3_part[0], head_part[0], mla_part[0],
                             mla_part[1, :128], dsk[:, 0], head_part[1, 0:1], jnp.zeros((119,), F32), drb[:, 0]])
    return head_part[1, 0], dx0, grads, small, dmod9


SMALL_LAYOUT = (("norm_ffn1", 1024), ("norm_mix", 1024), ("norm_ffn2", 1024), ("norm_final", 1024),
                ("q_norm", 256), ("kv_norm", 128), ("sinks", 128), ("rel_bias", 256))
N_SMALL = sum(n for _, n in SMALL_LAYOUT)
LOSS_SLOT = 4 * 1024 + 256 + 128 + SWA_HEADS


def _coords():
    return lax.axis_index("x"), lax.axis_index("y"), lax.axis_index("c")


def _flip(v, bit):
    return 1 - v if bit else v


def _peer(r):
    x, y, c = _coords()
    return (_flip(x, r & 4), _flip(y, r & 2), _flip(c, r & 1))


def _mod_fwd(c_tile, w_mod, b_mod3):
    W = w_mod.shape[1]

    def body(c_ref, w_ref, b_ref, mod_ref, ca_ref, call_ref, part_ref, send_sems, recv_sems):
        x, y, c = _coords()
        me = 4 * x + 2 * y + c
        call_ref[me] = c_ref[...]
        sends = []
        for r in range(1, N_DEV):
            cp = pltpu.make_async_remote_copy(c_ref, call_ref.at[me], send_sems.at[0, r], recv_sems.at[0, r],
                                              device_id=_peer(r), device_id_type=MESH)
            cp.start()
            sends.append(cp)
        for r in range(1, N_DEV):
            pltpu.make_async_remote_copy(c_ref, call_ref.at[me], send_sems.at[0, r], recv_sems.at[0, r],
                                         device_id=_peer(r), device_id_type=MESH).wait_recv()
        cv = call_ref[...].reshape(8 * N_DEV, D)
        ca = (cv * _sigmoid(cv)).astype(BF16)
        ca_ref[...] = ca
        part_ref[...] = _dot(ca, w_ref[...].astype(BF16)).reshape(N_DEV, 8, W)
        mod_ref[me] = part_ref[me] + b_ref[me]
        for r in range(1, N_DEV):
            cp = pltpu.make_async_remote_copy(part_ref.at[me ^ r], mod_ref.at[me], send_sems.at[1, r],
                                              recv_sems.at[1, r], device_id=_peer(r), device_id_type=MESH)
            cp.start()
            sends.append(cp)
        for r in range(1, N_DEV):
            pltpu.make_async_remote_copy(part_ref.at[me ^ r], mod_ref.at[me], send_sems.at[1, r],
                                         recv_sems.at[1, r], device_id=_peer(r), device_id_type=MESH).wait_recv()
            mod_ref[me ^ r] = mod_ref[me ^ r] + b_ref[me ^ r]
        for cp in sends:
            cp.wait_send()

    vm = pl.BlockSpec(memory_space=pltpu.VMEM)
    return pl.pallas_call(
        body, name="mod_fwd", in_specs=[vm, vm, vm], out_specs=[vm, vm],
        out_shape=[jax.ShapeDtypeStruct((N_DEV, 8, W), F32), jax.ShapeDtypeStruct((8 * N_DEV, D), BF16)],
        scratch_shapes=[pltpu.VMEM((N_DEV, 8, D), F32), pltpu.VMEM((N_DEV, 8, W), F32),
                        pltpu.SemaphoreType.DMA((2, N_DEV)), pltpu.SemaphoreType.DMA((2, N_DEV))],
        compiler_params=_params(),
    )(c_tile, w_mod, b_mod3)


def _mod_bwd(dmod3, small_tile, ca):
    W = dmod3.shape[2]

    def body(dm_ref, sm_ref, ca_ref, gw_ref, gb_ref, ssum_ref, dmcols, sm_all, gb_mine, send_sems, recv_sems):
        x, y, c = _coords()
        me = 4 * x + 2 * y + c
        dmcols[me] = dm_ref[me]
        sm_all[me] = sm_ref[...]
        sends = []
        for r in range(1, N_DEV):
            cp = pltpu.make_async_remote_copy(dm_ref.at[me ^ r], dmcols.at[me], send_sems.at[0, r], recv_sems.at[0, r],
                                              device_id=_peer(r), device_id_type=MESH)
            cp.start()
            sends.append(cp)
            cp = pltpu.make_async_remote_copy(sm_ref, sm_all.at[me], send_sems.at[1, r], recv_sems.at[1, r],
                                              device_id=_peer(r), device_id_type=MESH)
            cp.start()
            sends.append(cp)
        for r in range(1, N_DEV):
            pltpu.make_async_remote_copy(dm_ref.at[me ^ r], dmcols.at[me], send_sems.at[0, r], recv_sems.at[0, r],
                                         device_id=_peer(r), device_id_type=MESH).wait_recv()
            pltpu.make_async_remote_copy(sm_ref, sm_all.at[me], send_sems.at[1, r], recv_sems.at[1, r],
                                         device_id=_peer(r), device_id_type=MESH).wait_recv()
        dm = dmcols[...].reshape(8 * N_DEV, W)
        gw_ref[...] = _dot_tn(ca_ref[...], dm.astype(BF16))
        first_row = lax.broadcasted_iota(jnp.int32, (8, W), 0) == 0
        gb_mine[...] = jnp.where(first_row, jnp.sum(dm, axis=0, keepdims=True), 0.0)
        gb_ref[me] = gb_mine[...]
        for r in range(1, N_DEV):
            cp = pltpu.make_async_remote_copy(gb_mine, gb_ref.at[me], send_sems.at[2, r], recv_sems.at[2, r],
                                              device_id=_peer(r), device_id_type=MESH)
            cp.start()
            sends.append(cp)
        total = sm_all[0]
        for k in range(1, N_DEV):
            total = total + sm_all[k]
        ssum_ref[...] = total
        for r in range(1, N_DEV):
            pltpu.make_async_remote_copy(gb_mine, gb_ref.at[me], send_sems.at[2, r], recv_sems.at[2, r],
                                         device_id=_peer(r), device_id_type=MESH).wait_recv()
        for cp in sends:
            cp.wait_send()

    vm = pl.BlockSpec(memory_space=pltpu.VMEM)
    return pl.pallas_call(
        body, name="mod_bwd", in_specs=[vm, vm, vm], out_specs=[vm, vm, vm],
        out_shape=[jax.ShapeDtypeStruct((D, W), F32), jax.ShapeDtypeStruct((N_DEV, 8, W), F32),
                   jax.ShapeDtypeStruct((8, N_SMALL), F32)],
        scratch_shapes=[pltpu.VMEM((N_DEV, 8, W), F32), pltpu.VMEM((N_DEV, 8, N_SMALL), F32), pltpu.VMEM((8, W), F32),
                        pltpu.SemaphoreType.DMA((3, N_DEV)), pltpu.SemaphoreType.DMA((3, N_DEV))],
        compiler_params=_params(),
    )(dmod3, small_tile, ca)


def _wgather(shards):
    n = len(shards)

    def body(*refs):
        ins, outs, token = refs[:n], refs[n:2 * n], refs[2 * n]
        send_sems, recv_sems, local_sems = refs[2 * n + 1:]
        token[...] = jnp.zeros_like(token)
        x, y, c = _coords()
        me = 4 * x + 2 * y + c
        sib = (x, y, 1 - c)
        chips = [(1 - x, y), (x, 1 - y), (1 - x, 1 - y)]

        def copy(k, slot, block, to, src=None):
            return pltpu.make_async_remote_copy(
                src_ref=outs[k].at[block] if src is None else src, dst_ref=outs[k].at[block],
                send_sem=send_sems.at[k, slot], recv_sem=recv_sems.at[k, slot], device_id=to, device_id_type=MESH)

        local = [pltpu.make_async_copy(ins[k], outs[k].at[me], local_sems.at[k]) for k in range(n)]
        for cp in local:
            cp.start()
        first = []
        for k in range(n):
            first.append(copy(k, 0, me, sib, src=ins[k]))
            for j, chip in enumerate(chips):
                first.append(copy(k, 1 + j, me, (*chip, c), src=ins[k]))
        for cp in first:
            cp.start()
        passed = []
        for j, (cx, cy) in enumerate(chips):
            for k in range(n):
                blk = 4 * cx + 2 * cy + c
                copy(k, 1 + j, blk, sib).wait_recv()
                cp = copy(k, 4 + j, blk, sib)
                cp.start()
                passed.append(cp)
        for k in range(n):
            copy(k, 0, 4 * x + 2 * y + (1 - c), sib).wait_recv()
            for j, (cx, cy) in enumerate(chips):
                copy(k, 4 + j, 4 * cx + 2 * cy + (1 - c), sib).wait_recv()
        for cp in first + passed:
            cp.wait_send()
        for cp in local:
            cp.wait()

    anyspec = pl.BlockSpec(memory_space=pl.ANY)
    return pl.pallas_call(
        body, name="wgather", in_specs=[anyspec] * n,
        out_specs=[anyspec] * n + [pl.BlockSpec(memory_space=pltpu.VMEM)],
        out_shape=[jax.ShapeDtypeStruct((N_DEV,) + s.shape, s.dtype) for s in shards]
        + [jax.ShapeDtypeStruct((8, 128), F32)],
        scratch_shapes=[pltpu.SemaphoreType.DMA((n, 7)), pltpu.SemaphoreType.DMA((n, 7)),
                        pltpu.SemaphoreType.DMA((n,))],
    )(*shards)


class _GatherCopies:
    def __init__(self, lands, send_sems, recv_sems):
        x, y, c = _coords()
        me = 4 * x + 2 * y + c
        sib = (x, y, 1 - c)
        chips = [(1 - x, y), (x, 1 - y), (1 - x, 1 - y)]

        def copy(k, slot, block, to):
            return pltpu.make_async_remote_copy(
                src_ref=lands[k].at[block], dst_ref=lands[k].at[block],
                send_sem=send_sems.at[7 * k + slot], recv_sem=recv_sems.at[7 * k + slot],
                device_id=to, device_id_type=MESH)

        n = len(lands)
        self.first = [copy(k, 0, me, sib) for k in range(n)]
        self.first += [copy(k, 1 + j, me, (cx, cy, c)) for j, (cx, cy) in enumerate(chips) for k in range(n)]
        self.landed = [copy(k, 1 + j, 4 * cx + 2 * cy + c, sib) for j, (cx, cy) in enumerate(chips) for k in range(n)]
        self.passed = [copy(k, 4 + j, 4 * cx + 2 * cy + c, sib) for j, (cx, cy) in enumerate(chips) for k in range(n)]
        self.from_sib = [copy(k, 0, 4 * x + 2 * y + (1 - c), sib) for k in range(n)]
        self.from_sib += [copy(k, 4 + j, 4 * cx + 2 * cy + (1 - c), sib) for j, (cx, cy) in enumerate(chips)
                          for k in range(n)]


def _gather_start(lands, *, name):
    n = len(lands)

    def body(*refs):
        for cp in _GatherCopies(refs[:n], refs[n], refs[n + 1]).first:
            cp.start()
        refs[-1][...] = jnp.zeros_like(refs[-1])

    out = pl.pallas_call(
        body, name=name,
        out_shape=(pltpu.SemaphoreType.DMA((7 * n,)), pltpu.SemaphoreType.DMA((7 * n,)),
                   *[pltpu.HBM(l.shape, l.dtype) for l in lands], jax.ShapeDtypeStruct((8, 128), F32)),
        in_specs=[HBM_SPEC] * n,
        out_specs=(SEM_SPEC, SEM_SPEC, *[HBM_SPEC] * n, pl.BlockSpec(memory_space=pltpu.VMEM)),
        input_output_aliases={i: 2 + i for i in range(n)},
        compiler_params=pltpu.CompilerParams(has_side_effects=DATAFLOW),
    )(*[_in_hbm(l) for l in lands])
    return out[0], out[1], list(out[2:2 + n]), out[-1]


def _gather_pass(send_sems, recv_sems, lands, after, *, name, stage):
    n = len(lands)

    def body(*refs):
        cps = _GatherCopies(refs[:n], refs[n], refs[n + 1])
        if stage == "landed":
            for cp in cps.landed:
                cp.wait_recv()
        else:
            for cp in cps.passed:
                cp.start()
        refs[-1][...] = jnp.zeros_like(refs[-1])

    out = pl.pallas_call(
        body, name=name,
        out_shape=(*[pltpu.HBM(l.shape, l.dtype) for l in lands], jax.ShapeDtypeStruct((8, 128), F32)),
        in_specs=[HBM_SPEC] * n + [SEM_SPEC, SEM_SPEC] + [pl.BlockSpec(memory_space=pl.ANY)] * len(after),
        out_specs=(*[HBM_SPEC] * n, pl.BlockSpec(memory_space=pltpu.VMEM)),
        input_output_aliases={i: i for i in range(n)},
        compiler_params=pltpu.CompilerParams(has_side_effects=DATAFLOW),
    )(*lands, send_sems, recv_sems, *after)
    return list(out[:n]), out[-1]


def _gather_end(send_sems, recv_sems, lands, after, *, name):
    n = len(lands)

    def body(*refs):
        cps = _GatherCopies(refs[:n], refs[n], refs[n + 1])
        for cp in cps.from_sib:
            cp.wait_recv()
        for cp in cps.first + cps.passed:
            cp.wait_send()

    out = pl.pallas_call(
        body, name=name,
        out_shape=[pltpu.HBM(l.shape, l.dtype) for l in lands],
        in_specs=[HBM_SPEC] * n + [SEM_SPEC, SEM_SPEC] + [pl.BlockSpec(memory_space=pl.ANY)] * len(after),
        out_specs=[HBM_SPEC] * n,
        input_output_aliases={i: i for i in range(n)},
        compiler_params=pltpu.CompilerParams(has_side_effects=DATAFLOW),
    )(*lands, send_sems, recv_sems, *after)
    return list(out)


def _rs_d2d(grads, *, name):
    n = len(grads)

    def body(*refs):
        ins, outs = refs[:n], refs[n:2 * n]
        send_sems, recv_sems = refs[2 * n:]
        x, y, c = _coords()
        sib = (x, y, 1 - c)
        cps = []
        for k in range(n):
            for q in range(4):
                cps.append(pltpu.make_async_remote_copy(
                    src_ref=ins[k].at[2 * q + (1 - c)], dst_ref=outs[k].at[q],
                    send_sem=send_sems.at[k, q], recv_sem=recv_sems.at[k, q], device_id=sib, device_id_type=MESH))
        for cp in cps:
            cp.start()
        for cp in cps:
            cp.wait_recv()
        for cp in cps:
            cp.wait_send()

    anyspec = pl.BlockSpec(memory_space=pl.ANY)
    return pl.pallas_call(
        body, name=name, in_specs=[anyspec] * n, out_specs=[anyspec] * n,
        out_shape=[jax.ShapeDtypeStruct((4,) + g.shape[1:], g.dtype) for g in grads],
        scratch_shapes=[pltpu.SemaphoreType.DMA((n, 4)), pltpu.SemaphoreType.DMA((n, 4))],
    )(*grads)


def _chipsum(g, sib, cidx, *, name):
    _, r, cc = g.shape

    def body(c_ref, g_ref, s_ref, o_ref):
        o_ref[...] = (g_ref[...].astype(F32) + s_ref[...].astype(F32)).astype(o_ref.dtype)

    return pl.pallas_call(
        body, name=name,
        grid_spec=pltpu.PrefetchScalarGridSpec(
            num_scalar_prefetch=1, grid=(4,),
            in_specs=[pl.BlockSpec((1, r, cc), lambda q, c_ref: (2 * q + c_ref[0], 0, 0)),
                      pl.BlockSpec((1, r, cc), lambda q, c_ref: (q, 0, 0))],
            out_specs=pl.BlockSpec((1, r, cc), lambda q, c_ref: (q, 0, 0))),
        out_shape=jax.ShapeDtypeStruct((4, r, cc), g.dtype),
        compiler_params=_params(("arbitrary",)),
    )(cidx, g, sib)


HBM_SPEC = pl.BlockSpec(memory_space=pltpu.HBM)
SEM_SPEC = pl.BlockSpec(memory_space=pltpu.SEMAPHORE)
DATAFLOW = pltpu.SideEffectType.DATAFLOW_SIDE_EFFECTING


def _in_hbm(a):
    return pltpu.with_memory_space_constraint(a, pltpu.HBM)


def _ici_copies(sums, lands, send_sems, recv_sems):
    x, y, c = _coords()
    chips = [(1 - x, y), (x, 1 - y), (1 - x, 1 - y)]
    cps = []
    for k in range(len(sums)):
        for j, (cx, cy) in enumerate(chips):
            cps.append(pltpu.make_async_remote_copy(
                src_ref=sums[k].at[2 * cx + cy], dst_ref=lands[k].at[j],
                send_sem=send_sems.at[3 * k + j], recv_sem=recv_sems.at[3 * k + j],
                device_id=(cx, cy, c), device_id_type=MESH))
    return cps


def _rs_ici_start(sums, *, name):
    n = len(sums)

    def body(*refs):
        token = refs[-1]
        for cp in _ici_copies(refs[:n], refs[n:2 * n], refs[2 * n], refs[2 * n + 1]):
            cp.start()
        token[...] = jnp.zeros_like(token)

    lands = [_in_hbm(lax.empty((3,) + s.shape[1:], s.dtype)) for s in sums]
    out = pl.pallas_call(
        body, name=name,
        out_shape=(pltpu.SemaphoreType.DMA((3 * n,)), pltpu.SemaphoreType.DMA((3 * n,)),
                   *[pltpu.HBM(s.shape, s.dtype) for s in sums], *[pltpu.HBM(l.shape, l.dtype) for l in lands],
                   jax.ShapeDtypeStruct((8, 128), F32)),
        in_specs=[HBM_SPEC] * (2 * n),
        out_specs=(SEM_SPEC, SEM_SPEC, *[HBM_SPEC] * (2 * n), pl.BlockSpec(memory_space=pltpu.VMEM)),
        input_output_aliases={i: 2 + i for i in range(2 * n)},
        compiler_params=pltpu.CompilerParams(has_side_effects=DATAFLOW),
    )(*[_in_hbm(s) for s in sums], *lands)
    return out[0], out[1], list(out[2:2 + n]), list(out[2 + n:2 + 2 * n]), out[-1]


def _rs_ici_wait(send_sems, recv_sems, sums, lands, after, *, name):
    n = len(sums)

    def body(*refs):
        for cp in _ici_copies(refs[:n], refs[n:2 * n], refs[2 * n], refs[2 * n + 1]):
            cp.wait_send()
            cp.wait_recv()

    out = pl.pallas_call(
        body, name=name,
        out_shape=[pltpu.HBM(a.shape, a.dtype) for a in list(sums) + list(lands)],
        in_specs=[HBM_SPEC] * (2 * n) + [SEM_SPEC, SEM_SPEC] + [pl.BlockSpec(memory_space=pl.ANY)] * len(after),
        out_specs=[HBM_SPEC] * (2 * n),
        input_output_aliases={i: i for i in range(2 * n)},
        compiler_params=pltpu.CompilerParams(has_side_effects=DATAFLOW),
    )(*sums, *lands, send_sems, recv_sems, *after)
    return list(out[:n]), list(out[n:])


ADAM_C1 = 1.0 / (1.0 - ADAM_B1 ** ADAM_STEP)
ADAM_C2 = 1.0 / (1.0 - ADAM_B2 ** ADAM_STEP)


def _adam_math(w, g, m, v):
    m2 = ADAM_B1 * m + (1.0 - ADAM_B1) * g
    v2 = ADAM_B2 * v + (1.0 - ADAM_B2) * (g * g)
    return -ADAM_LR * ((m2 * ADAM_C1) / (jnp.sqrt(v2 * ADAM_C2) + ADAM_EPS) + ADAM_WD * w), m2, v2


def _adamw(w, g, m, v, *, name):
    R, C = w.shape
    tr = R if R <= 512 else 256

    def body(w_ref, g_ref, m_ref, v_ref, d_ref, nm_ref, nv_ref):
        d_ref[...], nm_ref[...], nv_ref[...] = _adam_math(w_ref[...], g_ref[...], m_ref[...], v_ref[...])

    blk = pl.BlockSpec((tr, C), lambda i: (i, 0))
    return pl.pallas_call(
        body, name=name, grid=(R // tr,), in_specs=[blk] * 4, out_specs=[blk] * 3,
        out_shape=[jax.ShapeDtypeStruct((R, C), F32)] * 3,
        compiler_params=_params(("parallel",)),
    )(w, g, m, v)


def _adamw_rs(w, m, v, cs, rcv, qidx, *, name):
    r, cc = w.shape
    tr = r // 2 if r % 32 == 0 and r > 128 else r

    def body(q_ref, w_ref, m_ref, v_ref, c_ref, r_ref, g_ref, d_ref, nm_ref, nv_ref):
        g = ((c_ref[0].astype(F32) + r_ref[0].astype(F32)) + r_ref[1].astype(F32)) + r_ref[2].astype(F32)
        g_ref[...] = g
        d_ref[...], nm_ref[...], nv_ref[...] = _adam_math(w_ref[...], g, m_ref[...], v_ref[...])

    blk = pl.BlockSpec((tr, cc), lambda i, q_ref: (i, 0))
    return pl.pallas_call(
        body, name=name,
        grid_spec=pltpu.PrefetchScalarGridSpec(
            num_scalar_prefetch=1, grid=(r // tr,),
            in_specs=[blk, blk, blk, pl.BlockSpec((1, tr, cc), lambda i, q_ref: (q_ref[0], i, 0)),
                      pl.BlockSpec((3, tr, cc), lambda i, q_ref: (0, i, 0))],
            out_specs=[blk] * 4),
        out_shape=[jax.ShapeDtypeStruct((r, cc), F32)] * 4,
        compiler_params=_params(("arbitrary",)),
    )(qidx, w, m, v, cs, rcv)


SMALL_PARAMS = ("norm_ffn1", "norm_mix", "norm_ffn2", "norm_final", "q_norm", "kv_norm", "sinks", "rel_bias", "b_mod")


def _adamw_small(ssum, gb3, wmv):
    widths = [wmv[3 * i].shape[1] for i in range(len(SMALL_PARAMS))]
    WM = gb3.shape[2]

    def body(*refs):
        ssum_ref, gb_ref = refs[0], refs[1]
        ins = refs[2:2 + 3 * len(SMALL_PARAMS)]
        outs = refs[2 + 3 * len(SMALL_PARAMS):]
        off = 0
        for i, name in enumerate(SMALL_PARAMS):
            n = widths[i]
            g_ref, d_ref, nm_ref, nv_ref = outs[4 * i:4 * i + 4]
            w_ref, m_ref, v_ref = ins[3 * i:3 * i + 3]
            if name == "b_mod":
                for k in range(N_DEV):
                    cols = slice(WM * k, WM * (k + 1))
                    g = gb_ref[k, 0:1, :]
                    g_ref[:, cols] = g
                    d_ref[:, cols], nm_ref[:, cols], nv_ref[:, cols] = _adam_math(
                        w_ref[:, cols], g, m_ref[:, cols], v_ref[:, cols])
            else:
                g = ssum_ref[0:1, off:off + n]
                g_ref[...] = g
                d_ref[...], nm_ref[...], nv_ref[...] = _adam_math(w_ref[...], g, m_ref[...], v_ref[...])
                off += dict(SMALL_LAYOUT)[name]

    vm = pl.BlockSpec(memory_space=pltpu.VMEM)
    n_out = 4 * len(SMALL_PARAMS)
    out = pl.pallas_call(
        body, name="adamw_small", in_specs=[vm] * (2 + len(wmv)), out_specs=[vm] * n_out,
        out_shape=[jax.ShapeDtypeStruct((1, widths[i // 4]), F32) for i in range(n_out)],
        compiler_params=_params(),
    )(ssum, gb3, *wmv)
    return {name: out[4 * i:4 * i + 4] for i, name in enumerate(SMALL_PARAMS)}


TRANSPOSED = ("g1T", "u1T", "g3T", "u3T", "w_inT", "w_uqT")


def kernel(x, c, w_mod, b_mod, norm_ffn1, ffn1_gate, ffn1_up, ffn1_down, norm_mix, w_in, q_norm, kv_norm, w_uq, w_ukv, sinks, w_o, norm_ffn2, ffn2_gate, ffn2_up, ffn2_down, rel_bias, norm_final, loss_target, m_w_mod, m_b_mod, m_norm_ffn1, m_ffn1_gate, m_ffn1_up, m_ffn1_down, m_norm_mix, m_w_in, m_q_norm, m_kv_norm, m_w_uq, m_w_ukv, m_sinks, m_w_o, m_norm_ffn2, m_ffn2_gate, m_ffn2_up, m_ffn2_down, m_rel_bias, m_norm_final, v_w_mod, v_b_mod, v_norm_ffn1, v_ffn1_gate, v_ffn1_up, v_ffn1_down, v_norm_mix, v_w_in, v_q_norm, v_kv_norm, v_w_uq, v_w_ukv, v_sinks, v_w_o, v_norm_ffn2, v_ffn2_gate, v_ffn2_up, v_ffn2_down, v_rel_bias, v_norm_final):
    mx, my, mc = _coords()
    cidx = jnp.reshape(mc, (1,)).astype(jnp.int32)
    qidx = jnp.reshape(2 * mx + my, (1,)).astype(jnp.int32)
    WM = w_mod.shape[2]

    c_tile = jnp.pad(c, ((0, 7), (0, 0)))
    b_mod3 = jnp.pad(b_mod.reshape(N_DEV, 1, WM), ((0, 0), (0, 7), (0, 0)))
    mod3, ca = _mod_fwd(c_tile, w_mod[0], b_mod3)
    mod9 = mod3[:, 0, :].reshape(N_MOD, D)

    shards = {"g1T": ffn1_gate[0].T.astype(BF16), "u1T": ffn1_up[0].T.astype(BF16), "d1": ffn1_down[0].astype(BF16),
              "g3T": ffn2_gate[0].T.astype(BF16), "u3T": ffn2_up[0].T.astype(BF16), "d3": ffn2_down[0].astype(BF16),
              "w_inT": w_in[0].T, "w_uqT": w_uq[0].T.astype(BF16), "w_ukv": w_ukv[0].astype(BF16),
              "w_o": w_o[0].astype(BF16)}
    me = 4 * mx + 2 * my + mc
    groups = {"ffn1": ("g1T", "u1T", "d1"), "mixer": ("w_inT", "w_uqT", "w_ukv", "w_o"), "ffn2": ("g3T", "u3T", "d3")}
    arriving = {}

    def as_weights(group, gathered):
        return {k: g if k == "w_ukv" else g.reshape(N_DEV * g.shape[1], g.shape[2])
                for k, g in zip(groups[group], gathered)}

    def start_gather(group, token):
        lands = []
        for k in groups[group]:
            sh = shards[k] + token[0, 0].astype(shards[k].dtype)
            lands.append(lax.dynamic_update_slice(lax.empty((N_DEV,) + sh.shape, sh.dtype), sh[None], (me, 0, 0)))
        send, recv, lands, started = _gather_start(lands, name="gather_start_" + group)
        arriving[group] = (send, recv, lands)
        return started

    def fetch(group, after, vecs):
        if group == "ffn1":
            *gathered, token = _wgather([shards[k] + ca[1, 0].astype(shards[k].dtype) for k in groups["ffn1"]])
            token = start_gather("ffn2", start_gather("mixer", token))
            return as_weights("ffn1", gathered), vecs + token[0:1, 0:1]
        def pass_on(group, after):
            send, recv, lands = arriving[group]
            lands, token = _gather_pass(send, recv, lands, after, name="gather_landed_" + group, stage="landed")
            lands, token = _gather_pass(send, recv, lands, [token], name="gather_onward_" + group, stage="onward")
            arriving[group] = (send, recv, lands)
            return token

        if group == "ffn2_on_its_way":
            return None, vecs + pass_on("ffn2", after)[0:1, 0:1]
        if group == "mixer":
            after = [pass_on("mixer", after)]
        send, recv, lands = arriving[group]
        return as_weights(group, _gather_end(send, recv, lands, after, name="gather_end_" + group)), vecs

    norms ={"ffn1": norm_ffn1, "mix": norm_mix, "ffn2": norm_ffn2, "final": norm_final.reshape(1, D)}
    in_flight = {}

    def on_grads(group, g, vecs):
        names = list(g)
        by_dest = [g[k] if k == "w_ukv" else g[k].reshape((N_DEV, g[k].shape[0] // N_DEV) + g[k].shape[1:])
                   for k in names]
        from_sib = _rs_d2d(by_dest, name="rs_d2d_" + group)
        sums = [_chipsum(a, s, cidx, name="chipsum_" + k) for k, a, s in zip(names, by_dest, from_sib)]
        send, recv, sums, lands, token = _rs_ici_start(sums, name="rs_ici_start_" + group)
        in_flight[group] = (names, send, recv, sums, lands, token)
        return vecs + token[0:1, 0:1]

    loss_local, grad_x, _, small, dmod9 = _local_step(
        x[0], loss_target[0], mod9, norms, sinks, rel_bias, q_norm, kv_norm, fetch, on_grads=on_grads)

    owners = {"g1T": ("ffn1_gate", ffn1_gate, m_ffn1_gate, v_ffn1_gate), "u1T": ("ffn1_up", ffn1_up, m_ffn1_up, v_ffn1_up),
              "d1": ("ffn1_down", ffn1_down, m_ffn1_down, v_ffn1_down),
              "g3T": ("ffn2_gate", ffn2_gate, m_ffn2_gate, v_ffn2_gate), "u3T": ("ffn2_up", ffn2_up, m_ffn2_up, v_ffn2_up),
              "d3": ("ffn2_down", ffn2_down, m_ffn2_down, v_ffn2_down),
              "w_inT": ("w_in", w_in, m_w_in, v_w_in), "w_uqT": ("w_uq", w_uq, m_w_uq, v_w_uq),
              "w_ukv": ("w_ukv", w_ukv, m_w_ukv, v_w_ukv), "w_o": ("w_o", w_o, m_w_o, v_w_o)}
    res = {}

    def finish(group, after):
        names, send, recv, sums, lands, _ = in_flight[group]
        sums, lands = _rs_ici_wait(send, recv, sums, lands, after, name="rs_ici_wait_" + group)
        for k, cs, rc in zip(names, sums, lands):
            pname, wk, mk, vk = owners[k]
            there = (lambda a: a[0].T) if k in TRANSPOSED else (lambda a: a[0])
            back = (lambda a: a.T[None]) if k in TRANSPOSED else (lambda a: a[None])
            res[pname] = tuple(back(a) for a in _adamw_rs(there(wk), there(mk), there(vk), cs, rc, qidx,
                                                          name="adamw_" + pname))

    ffn1_started = in_flight["ffn1"][5]
    finish("ffn2", [ffn1_started])
    finish("mixer", [ffn1_started])

    dmod3 = jnp.pad(dmod9.reshape(N_DEV, 1, WM), ((0, 0), (0, 7), (0, 0)))
    small_tile = jnp.pad(small.reshape(1, N_SMALL), ((0, 7), (0, 0)))
    g_wmod, gb3, ssum = _mod_bwd(dmod3, small_tile, ca)
    loss = ssum[0, LOSS_SLOT]
    res["w_mod"] = tuple(a[None] for a in (g_wmod,) + tuple(_adamw(w_mod[0], g_wmod, m_w_mod[0], v_w_mod[0],
                                                                    name="adamw_w_mod")))
    small_in = {"norm_ffn1": (norm_ffn1, m_norm_ffn1, v_norm_ffn1), "norm_mix": (norm_mix, m_norm_mix, v_norm_mix),
                "norm_ffn2": (norm_ffn2, m_norm_ffn2, v_norm_ffn2), "norm_final": (norm_final, m_norm_final, v_norm_final),
                "q_norm": (q_norm, m_q_norm, v_q_norm), "kv_norm": (kv_norm, m_kv_norm, v_kv_norm),
                "sinks": (sinks, m_sinks, v_sinks), "rel_bias": (rel_bias, m_rel_bias, v_rel_bias),
                "b_mod": (b_mod, m_b_mod, v_b_mod)}
    small_out = _adamw_small(ssum, gb3, [a.reshape(1, -1) for k in SMALL_PARAMS for a in small_in[k]])
    for k in SMALL_PARAMS:
        res[k] = tuple(a.reshape(small_in[k][0].shape) for a in small_out[k])

    finish("ffn1", [res[k][3] for k in ("w_mod", "w_o", "ffn2_down", "b_mod")])

    order = ("w_mod", "b_mod", "norm_ffn1", "ffn1_gate", "ffn1_up", "ffn1_down", "norm_mix", "w_in", "q_norm",
             "kv_norm", "w_uq", "w_ukv", "sinks", "w_o", "norm_ffn2", "ffn2_gate", "ffn2_up", "ffn2_down",
             "rel_bias", "norm_final")
    return (loss, grad_x[None]) + tuple(res[nm][kind] for kind in range(4) for nm in order)
```

```python
import functools
import math

import numpy as np
import jax
import jax.numpy as jnp
from jax import lax
from jax.experimental import pallas as pl
from jax.experimental.pallas import tpu as pltpu

F32 = jnp.float32
BF16 = jnp.bfloat16
MESH = pl.DeviceIdType.MESH

N_DEV = 8
D = 1024
D_FF = 2816
EPS = 1e-6
N_MOD = 9
SWA_HEADS = 8
SWA_DH = 64
WINDOW = 128
MLA_HEADS = 4
MLA_NOPE = 128
MLA_ROPE = 64
MLA_V = 128
MLA_QR = 256
MLA_KVR = 128
ROPE_THETA = 10000.0
NUM_BUCKETS = 32
D_IN = 1216
D_IN_PAD = 1280
SWA_SCALE = SWA_DH ** -0.5
MLA_SCALE = (MLA_NOPE + MLA_ROPE) ** -0.5

ADAM_LR = 0.001
ADAM_B1 = 0.9
ADAM_B2 = 0.999
ADAM_EPS = 1e-08
ADAM_WD = 0.01
ADAM_STEP = 10

V7X_VMEM_LIMIT = 56 * 1024 * 1024

NT_DIMS = (((1,), (1,)), ((), ()))
TN_DIMS = (((0,), (0,)), ((), ()))


def _dot(a, b):
    return jnp.dot(a, b, preferred_element_type=F32)


def _dot_nt(a, b):
    return lax.dot_general(a, b, NT_DIMS, preferred_element_type=F32)


def _dot_tn(a, b):
    return lax.dot_general(a, b, TN_DIMS, preferred_element_type=F32)


def _params(sem=None):
    return pltpu.CompilerParams(dimension_semantics=sem, vmem_limit_bytes=V7X_VMEM_LIMIT)


def _rstd(x):
    return lax.rsqrt(jnp.mean(x * x, axis=-1, keepdims=True) + EPS)


def _rms_bwd(dy, xhat, r):
    return r * (dy - xhat * jnp.mean(dy * xhat, axis=-1, keepdims=True))


def _sigmoid(a):
    return 1.0 / (1.0 + jnp.exp(-a))


def _ffn_fwd(x, vecs, wgT, wuT, wd, *, name, tm=512, tf=256):
    S = x.shape[0]
    tm = min(tm, S)
    ni, nj = S // tm, D_FF // tf

    def body(x_ref, vec_ref, wg_ref, wu_ref, wd_ref, xo_ref, h_ref, a_ref, b_ref, f_ref, acc_ref):
        j = pl.program_id(1)

        @pl.when(j == 0)
        def _():
            xv = x_ref[...]
            hn = xv * _rstd(xv) * vec_ref[0:1, :]
            h_ref[...] = (hn * (1.0 + vec_ref[2:3, :]) + vec_ref[1:2, :]).astype(BF16)
            acc_ref[...] = jnp.zeros_like(acc_ref)

        h = h_ref[...]
        a = _dot_nt(h, wg_ref[...])
        b = _dot_nt(h, wu_ref[...])
        a_ref[...] = a.astype(BF16)
        b_ref[...] = b.astype(BF16)
        hsw = (a * _sigmoid(a) * b).astype(BF16)
        acc_ref[...] += _dot(hsw, wd_ref[...])

        @pl.when(j == nj - 1)
        def _():
            f = acc_ref[...]
            f_ref[...] = f
            xo_ref[...] = x_ref[...] + (0.5 * vec_ref[3:4, :]) * f

    row = pl.BlockSpec((tm, D), lambda i, j: (i, 0))
    wspec = pl.BlockSpec((tf, D), lambda i, j: (j, 0))
    act = pl.BlockSpec((tm, tf), lambda i, j: (i, j))
    return pl.pallas_call(
        body, name=name, grid=(ni, nj),
        in_specs=[row, pl.BlockSpec((8, D), lambda i, j: (0, 0)), wspec, wspec, wspec],
        out_specs=[row, row, act, act, row],
        out_shape=[jax.ShapeDtypeStruct((S, D), F32), jax.ShapeDtypeStruct((S, D), BF16),
                   jax.ShapeDtypeStruct((S, D_FF), BF16), jax.ShapeDtypeStruct((S, D_FF), BF16),
                   jax.ShapeDtypeStruct((S, D), F32)],
        scratch_shapes=[pltpu.VMEM((tm, D), F32)],
        compiler_params=_params(("parallel", "arbitrary")),
    )(x, vecs, wgT, wuT, wd)


def _ffn_bwd_pre(dxo, f, vecs, *, name, tm=256):
    S = dxo.shape[0]

    def body(dx_ref, f_ref, vec_ref, df_ref, part_ref):
        @pl.when(pl.program_id(0) == 0)
        def _():
            part_ref[...] = jnp.zeros_like(part_ref)

        dx = dx_ref[...]
        df_ref[...] = ((0.5 * vec_ref[3:4, :]) * dx).astype(BF16)
        part_ref[0:1, :] += 0.5 * jnp.sum(dx * f_ref[...], axis=0, keepdims=True)

    row = pl.BlockSpec((tm, D), lambda i: (i, 0))
    vec = pl.BlockSpec((8, D), lambda i: (0, 0))
    return pl.pallas_call(
        body, name=name, grid=(S // tm,), in_specs=[row, row, vec], out_specs=[row, vec],
        out_shape=[jax.ShapeDtypeStruct((S, D), BF16), jax.ShapeDtypeStruct((8, D), F32)],
        compiler_params=_params(("arbitrary",)),
    )(dxo, f, vecs)


def _ffn_bwd_main(h, df, a, b, wgT, wuT, wd, *, name, tm=512, tf=256):
    S = h.shape[0]
    tm = min(tm, S)
    ni, nj = S // tm, D_FF // tf

    def body(h_hbm, df_hbm, a_ref, b_ref, wg_ref, wu_ref, wd_ref,
             gg_ref, gu_ref, gd_ref, dh_hbm,
             h_v, df_v, dh_v, gg_acc, gu_acc, gd_acc, sem):
        j = pl.program_id(0)
        i = pl.program_id(1)

        @pl.when((j == 0) & (i == 0))
        def _():
            c1 = pltpu.make_async_copy(h_hbm, h_v, sem.at[0])
            c2 = pltpu.make_async_copy(df_hbm, df_v, sem.at[1])
            c1.start()
            c2.start()
            c1.wait()
            c2.wait()

        @pl.when(i == 0)
        def _():
            gg_acc[...] = jnp.zeros_like(gg_acc)
            gu_acc[...] = jnp.zeros_like(gu_acc)
            gd_acc[...] = jnp.zeros_like(gd_acc)

        rows = pl.ds(pl.multiple_of(i * tm, tm), tm)
        hi = h_v[rows, :]
        dfi = df_v[rows, :]
        av = a_ref[...].astype(F32)
        bv = b_ref[...].astype(F32)
        sg = _sigmoid(av)
        sa = av * sg
        hsw = (sa * bv).astype(BF16)
        dhsw = _dot_nt(dfi, wd_ref[...])
        da = (dhsw * bv * (sg * (1.0 + av * (1.0 - sg)))).astype(BF16)
        db = (dhsw * sa).astype(BF16)
        gd_acc[...] += _dot_tn(hsw, dfi)
        gg_acc[...] += _dot_tn(da, hi)
        gu_acc[...] += _dot_tn(db, hi)
        dh = _dot(da, wg_ref[...]) + _dot(db, wu_ref[...])

        @pl.when(j == 0)
        def _():
            dh_v[rows, :] = dh

        @pl.when(j > 0)
        def _():
            dh_v[rows, :] += dh

        @pl.when(i == ni - 1)
        def _():
            gg_ref[...] = gg_acc[...].astype(BF16)
            gu_ref[...] = gu_acc[...].astype(BF16)
            gd_ref[...] = gd_acc[...].astype(BF16)

        @pl.when((j == nj - 1) & (i == ni - 1))
        def _():
            c3 = pltpu.make_async_copy(dh_v, dh_hbm, sem.at[2])
            c3.start()
            c3.wait()

    anyspec = pl.BlockSpec(memory_space=pl.ANY)
    wspec = pl.BlockSpec((tf, D), lambda j, i: (j, 0))
    act = pl.BlockSpec((tm, tf), lambda j, i: (i, j))
    return pl.pallas_call(
        body, name=name, grid=(nj, ni),
        in_specs=[anyspec, anyspec, act, act, wspec, wspec, wspec],
        out_specs=[wspec, wspec, wspec, anyspec],
        out_shape=[jax.ShapeDtypeStruct((D_FF, D), BF16)] * 3 + [jax.ShapeDtypeStruct((S, D), F32)],
        scratch_shapes=[pltpu.VMEM((S, D), BF16), pltpu.VMEM((S, D), BF16), pltpu.VMEM((S, D), F32),
                        pltpu.VMEM((tf, D), F32), pltpu.VMEM((tf, D), F32), pltpu.VMEM((tf, D), F32),
                        pltpu.SemaphoreType.DMA((3,))],
        compiler_params=_params(("arbitrary", "arbitrary")),
    )(h, df, a, b, wgT, wuT, wd)


def _norm_bwd(dh, x, dxo, vecs, *, name, tm=256):
    S = x.shape[0]

    def body(dh_ref, x_ref, dxo_ref, vec_ref, dx_ref, part_ref):
        @pl.when(pl.program_id(0) == 0)
        def _():
            part_ref[...] = jnp.zeros_like(part_ref)

        dh = dh_ref[...]
        xv = x_ref[...]
        r = _rstd(xv)
        xhat = xv * r
        w = vec_ref[0:1, :]
        xn = xhat * w
        dxn = dh * (1.0 + vec_ref[2:3, :])
        part_ref[0:1, :] += jnp.sum(dxn * xhat, axis=0, keepdims=True)
        part_ref[1:2, :] += jnp.sum(dh, axis=0, keepdims=True)
        part_ref[2:3, :] += jnp.sum(dh * xn, axis=0, keepdims=True)
        dx_ref[...] = dxo_ref[...] + _rms_bwd(dxn * w, xhat, r)

    row = pl.BlockSpec((tm, D), lambda i: (i, 0))
    vec = pl.BlockSpec((8, D), lambda i: (0, 0))
    return pl.pallas_call(
        body, name=name, grid=(S // tm,), in_specs=[row, row, row, vec], out_specs=[row, vec],
        out_shape=[jax.ShapeDtypeStruct((S, D), F32), jax.ShapeDtypeStruct((8, D), F32)],
        compiler_params=_params(("arbitrary",)),
    )(dh, x, dxo, vecs)


def _head(x, tgt, nf, *, tm=256):
    S = x.shape[0]

    def body(x_ref, t_ref, nf_ref, dx_ref, part_ref):
        @pl.when(pl.program_id(0) == 0)
        def _():
            part_ref[...] = jnp.zeros_like(part_ref)

        xv = x_ref[...]
        r = _rstd(xv)
        xhat = xv * r
        w = nf_ref[...]
        e = xhat * w - t_ref[...]
        dy = e * (1.0 / D)
        part_ref[0:1, :] += jnp.sum(dy * xhat, axis=0, keepdims=True)
        part_ref[1:2, :] += jnp.sum(e * e) * (0.5 / D)
        dx_ref[...] = _rms_bwd(dy * w, xhat, r)

    row = pl.BlockSpec((tm, D), lambda i: (i, 0))
    return pl.pallas_call(
        body, name="head", grid=(S // tm,),
        in_specs=[row, row, pl.BlockSpec((1, D), lambda i: (0, 0))],
        out_specs=[row, pl.BlockSpec((8, D), lambda i: (0, 0))],
        out_shape=[jax.ShapeDtypeStruct((S, D), F32), jax.ShapeDtypeStruct((8, D), F32)],
        compiler_params=_params(("arbitrary",)),
    )(x, tgt, nf)


def _mix_in_fwd(x, vecs, w_inT, *, tm=256):
    S = x.shape[0]

    def body(x_ref, vec_ref, w_ref, h_ref, p_ref):
        xv = x_ref[...]
        hn = xv * _rstd(xv) * vec_ref[0:1, :]
        h = (hn * (1.0 + vec_ref[2:3, :]) + vec_ref[1:2, :]).astype(BF16)
        h_ref[...] = h
        p_ref[...] = _dot_nt(h, w_ref[...])

    row = pl.BlockSpec((tm, D), lambda i: (i, 0))
    return pl.pallas_call(
        body, name="mix_in_fwd", grid=(S // tm,),
        in_specs=[row, pl.BlockSpec((8, D), lambda i: (0, 0)), pl.BlockSpec((D_IN_PAD, D), lambda i: (0, 0))],
        out_specs=[row, pl.BlockSpec((tm, D_IN_PAD), lambda i: (i, 0))],
        out_shape=[jax.ShapeDtypeStruct((S, D), BF16), jax.ShapeDtypeStruct((S, D_IN_PAD), F32)],
        compiler_params=_params(("parallel",)),
    )(x, vecs, w_inT)


def _bucket_table():
    qi = np.arange(WINDOW)[:, None]
    kj = np.arange(2 * WINDOW)[None, :]
    dist = qi + WINDOW - kj
    max_exact = NUM_BUCKETS // 2
    n = np.maximum(dist, 0)
    nf = np.maximum(n, 1).astype(np.float32)
    large = max_exact + (np.log(nf / np.float32(max_exact)) / np.float32(math.log(WINDOW / max_exact))
                         * np.float32(NUM_BUCKETS - max_exact)).astype(np.int32)
    large = np.minimum(large, NUM_BUCKETS - 1)
    return np.where(n < max_exact, n, large).astype(np.int32)


def _bias_build(rel_bias, bucket):
    def body(rb_ref, bk_ref, out_ref):
        bk = bk_ref[...]
        for h in range(SWA_HEADS):
            acc = jnp.zeros((WINDOW, 2 * WINDOW), F32)
            for b in range(NUM_BUCKETS):
                acc = jnp.where(bk == b, rb_ref[b, h], acc)
            out_ref[h] = acc

    return pl.pallas_call(
        body, name="bias_build",
        in_specs=[pl.BlockSpec(memory_space=pltpu.SMEM), pl.BlockSpec(memory_space=pltpu.VMEM)],
        out_specs=pl.BlockSpec(memory_space=pltpu.VMEM),
        out_shape=jax.ShapeDtypeStruct((SWA_HEADS, WINDOW, 2 * WINDOW), F32),
    )(rel_bias, bucket)


SWA_GROUP = 4
GROUP_ROWS = SWA_GROUP * WINDOW


def _swa_valid(n):
    row = lax.broadcasted_iota(jnp.int32, (GROUP_ROWS, 2 * WINDOW), 0) % WINDOW
    col = lax.broadcasted_iota(jnp.int32, (GROUP_ROWS, 2 * WINDOW), 1)
    dist = row + WINDOW - col
    return (dist >= 0) & (dist < WINDOW) & ((col >= WINDOW) | (n > 0))


def _stack_heads(x, g):
    return jnp.concatenate([x[:, 64 * h:64 * h + 64] for h in range(SWA_GROUP * g, SWA_GROUP * (g + 1))], axis=0)


def _unstack_heads(x4):
    return jnp.concatenate([x4[WINDOW * a:WINDOW * (a + 1)] for a in range(SWA_GROUP)], axis=1)


def _group_sinks(sink_ref, g):
    head = lax.broadcasted_iota(jnp.int32, (GROUP_ROWS, 1), 0) // WINDOW
    out = jnp.full((GROUP_ROWS, 1), sink_ref[0, SWA_GROUP * g], F32)
    for a in range(1, SWA_GROUP):
        out = jnp.where(head == a, sink_ref[0, SWA_GROUP * g + a], out)
    return out


def _swa_probs(qh, kk, bias_h, sink, valid):
    s = _dot_nt(qh, kk) * SWA_SCALE + bias_h
    s = jnp.where(valid, s, -jnp.inf)
    m = jnp.maximum(jnp.max(s, axis=-1, keepdims=True), sink)
    p = jnp.exp(s - m)
    ps = jnp.exp(sink - m)
    inv = 1.0 / (jnp.sum(p, axis=-1, keepdims=True) + ps)
    return p * inv, ps * inv


def _swa_specs():
    prev = lambda n: jnp.maximum(n - 1, 0)
    return [pl.BlockSpec((WINDOW, 512), lambda n: (n, 0)),
            pl.BlockSpec((WINDOW, 128), lambda n: (n, 4)),
            pl.BlockSpec((WINDOW, 128), lambda n: (prev(n), 4)),
            pl.BlockSpec((WINDOW, 128), lambda n: (n, 5)),
            pl.BlockSpec((WINDOW, 128), lambda n: (prev(n), 5)),
            pl.BlockSpec((SWA_HEADS, WINDOW, 2 * WINDOW), lambda n: (0, 0, 0)),
            pl.BlockSpec(memory_space=pltpu.SMEM)]


def _swa_fwd(proj, bias, sinks):
    S = proj.shape[0]

    def body(q_ref, kc_ref, kp_ref, vc_ref, vp_ref, bias_ref, sink_ref, o_ref):
        valid = _swa_valid(pl.program_id(0))
        q = q_ref[...].astype(BF16)
        kfull = jnp.concatenate([kp_ref[...], kc_ref[...]], axis=0).astype(BF16)
        vfull = jnp.concatenate([vp_ref[...], vc_ref[...]], axis=0).astype(BF16)
        for g in range(SWA_HEADS // SWA_GROUP):
            kk = kfull[:, 64 * g:64 * g + 64]
            vv = vfull[:, 64 * g:64 * g + 64]
            bias4 = bias_ref[SWA_GROUP * g:SWA_GROUP * (g + 1)].reshape(GROUP_ROWS, 2 * WINDOW)
            pk, _ = _swa_probs(_stack_heads(q, g), kk, bias4, _group_sinks(sink_ref, g), valid)
            o_ref[:, 256 * g:256 * (g + 1)] = _unstack_heads(_dot(pk.astype(BF16), vv))

    return pl.pallas_call(
        body, name="swa_fwd", grid=(S // WINDOW,),
        in_specs=_swa_specs(),
        out_specs=pl.BlockSpec((WINDOW, 512), lambda n: (n, 0)),
        out_shape=jax.ShapeDtypeStruct((S, 512), F32),
        compiler_params=_params(("parallel",)),
    )(proj, proj, proj, proj, proj, bias, sinks)


def _swa_bwd(proj, bias, sinks, o, do, bucket):
    S = proj.shape[0]
    nb = S // WINDOW

    def body(q_ref, kc_ref, kp_ref, vc_ref, vp_ref, bias_ref, sink_ref, o_ref, do_ref, bk_ref,
             dq_ref, dk_ref, dv_ref, drb_ref, dsk_ref, dbias_acc):
        n = pl.program_id(0)

        @pl.when(n == 0)
        def _():
            dk_ref[...] = jnp.zeros_like(dk_ref)
            dv_ref[...] = jnp.zeros_like(dv_ref)
            dsk_ref[...] = jnp.zeros_like(dsk_ref)
            dbias_acc[...] = jnp.zeros_like(dbias_acc)
            drb_ref[...] = jnp.zeros_like(drb_ref)

        valid = _swa_valid(n)
        q = q_ref[...].astype(BF16)
        dov = do_ref[...]
        kfull = jnp.concatenate([kp_ref[...], kc_ref[...]], axis=0).astype(BF16)
        vfull = jnp.concatenate([vp_ref[...], vc_ref[...]], axis=0).astype(BF16)
        prow = pl.ds(pl.multiple_of(jnp.maximum(n - 1, 0) * WINDOW, WINDOW), WINDOW)
        crow = pl.ds(pl.multiple_of(n * WINDOW, WINDOW), WINDOW)
        ov = o_ref[...]
        for g in range(SWA_HEADS // SWA_GROUP):
            heads = slice(SWA_GROUP * g, SWA_GROUP * (g + 1))
            kk = kfull[:, 64 * g:64 * g + 64]
            vv = vfull[:, 64 * g:64 * g + 64]
            q4 = _stack_heads(q, g)
            pk, psink = _swa_probs(q4, kk, bias_ref[heads].reshape(GROUP_ROWS, 2 * WINDOW), _group_sinks(sink_ref, g), valid)
            pkb = pk.astype(BF16)
            do4 = _stack_heads(dov, g)
            dob = do4.astype(BF16)
            dp = _dot_nt(dob, vv)
            delta = jnp.sum(do4 * _stack_heads(ov, g), axis=-1, keepdims=True)
            ds = pk * (dp - delta)
            dsink = -psink * delta
            for a in range(SWA_GROUP):
                h = SWA_GROUP * g + a
                part = jnp.sum(dsink[WINDOW * a:WINDOW * (a + 1)], keepdims=True)
                dsk_ref[h:h + 1, :] += jnp.broadcast_to(part, (1, 128))
            dbias_acc[heads] += ds.reshape(SWA_GROUP, WINDOW, 2 * WINDOW)
            dsb = (ds * SWA_SCALE).astype(BF16)
            dq_ref[:, 256 * g:256 * (g + 1)] = _unstack_heads(_dot(dsb, kk))
            dkk = _dot_tn(dsb, q4)
            dvv = _dot_tn(pkb, dob)
            dk_ref[prow, 64 * g:64 * g + 64] += dkk[:WINDOW]
            dk_ref[crow, 64 * g:64 * g + 64] += dkk[WINDOW:]
            dv_ref[prow, 64 * g:64 * g + 64] += dvv[:WINDOW]
            dv_ref[crow, 64 * g:64 * g + 64] += dvv[WINDOW:]

        @pl.when(n == nb - 1)
        def _():
            bk = bk_ref[...]
            for h in range(SWA_HEADS):
                dbh = dbias_acc[h]
                for b in range(NUM_BUCKETS):
                    val = jnp.sum(jnp.where(bk == b, dbh, 0.0), keepdims=True)
                    drb_ref[b * 8 + h:b * 8 + h + 1, :] = jnp.broadcast_to(val, (1, 128))

    full = lambda shape: pl.BlockSpec(shape, lambda n: tuple(0 for _ in shape))
    return pl.pallas_call(
        body, name="swa_bwd", grid=(nb,),
        in_specs=_swa_specs() + [pl.BlockSpec((WINDOW, 512), lambda n: (n, 0)),
                                 pl.BlockSpec((WINDOW, 512), lambda n: (n, 0)), full((WINDOW, 2 * WINDOW))],
        out_specs=[pl.BlockSpec((WINDOW, 512), lambda n: (n, 0)), full((S, 128)), full((S, 128)),
                   full((NUM_BUCKETS * 8, 128)), full((8, 128))],
        out_shape=[jax.ShapeDtypeStruct((S, 512), F32), jax.ShapeDtypeStruct((S, 128), F32),
                   jax.ShapeDtypeStruct((S, 128), F32), jax.ShapeDtypeStruct((NUM_BUCKETS * 8, 128), F32),
                   jax.ShapeDtypeStruct((8, 128), F32)],
        scratch_shapes=[pltpu.VMEM((SWA_HEADS, WINDOW, 2 * WINDOW), F32)],
        compiler_params=_params(("arbitrary",)),
    )(proj, proj, proj, proj, proj, bias, sinks, o, do, bucket)


def _rope_tables(S):
    inv = ROPE_THETA ** (-jnp.arange(0, MLA_ROPE, 2, dtype=F32) / MLA_ROPE)
    ang = jnp.arange(S, dtype=F32)[:, None] * inv[None, :]
    cos, sin = jnp.cos(ang), jnp.sin(ang)
    return jnp.tile(jnp.concatenate([cos, cos], axis=1), (1, 4)), jnp.tile(jnp.concatenate([-sin, sin], axis=1), (1, 4))


def _swap_halves(x):
    w = x.shape[-1]
    lane = lax.broadcasted_iota(jnp.int32, x.shape, x.ndim - 1)
    return jnp.where((lane % 64) < 32, pltpu.roll(x, w - 32, x.ndim - 1), pltpu.roll(x, 32, x.ndim - 1))


def _mla_pre_fwd(proj, qn_w, kvn_w, wuqT, wukv, cos, sin, *, tm=256):
    S = proj.shape[0]

    def body(ql_ref, kl_ref, kr_ref, qw_ref, kw_ref, wuq_ref, wukv_ref, cos_ref, sin_ref,
             qc_ref, kc_ref, vv_ref):
        ql = ql_ref[...]
        qn = (ql * _rstd(ql) * qw_ref[...]).astype(BF16)
        q = _dot_nt(qn, wuq_ref[...])
        cs, sn = cos_ref[...], sin_ref[...]
        qr = q[:, 512:768]
        qr = qr * cs + _swap_halves(qr) * sn
        half = lax.broadcasted_iota(jnp.int32, (tm, 128), 1) // 64
        kl = kl_ref[...]
        kvn = (kl * _rstd(kl) * kw_ref[...]).astype(BF16)
        kr = kr_ref[...]
        kr = kr * cs[:, :128] + _swap_halves(kr) * sn[:, :128]
        kr2 = (kr + pltpu.roll(kr, 64, 1)).astype(BF16)
        for h in range(MLA_HEADS):
            qc_ref[h, :, 0:128] = q[:, 128 * h:128 * h + 128].astype(BF16)
            chunk = qr[:, 128 * (h // 2):128 * (h // 2) + 128]
            qc_ref[h, :, 128:256] = jnp.where(half == (h % 2), chunk, 0.0).astype(BF16)
            kc_ref[h, :, 0:128] = _dot(kvn, wukv_ref[2 * h]).astype(BF16)
            kc_ref[h, :, 128:256] = kr2
            vv_ref[h] = _dot(kvn, wukv_ref[2 * h + 1]).astype(BF16)

    const = lambda shape: pl.BlockSpec(shape, lambda i: tuple(0 for _ in shape))
    return pl.pallas_call(
        body, name="mla_pre_fwd", grid=(S // tm,),
        in_specs=[pl.BlockSpec((tm, 256), lambda i: (i, 3)), pl.BlockSpec((tm, 128), lambda i: (i, 8)),
                  pl.BlockSpec((tm, 128), lambda i: (i, 9)), const((1, 256)), const((1, 128)),
                  const((768, 256)), const((8, 128, 128)),
                  pl.BlockSpec((tm, 256), lambda i: (i, 0)), pl.BlockSpec((tm, 256), lambda i: (i, 0))],
        out_specs=[pl.BlockSpec((MLA_HEADS, tm, 256), lambda i: (0, i, 0)),
                   pl.BlockSpec((MLA_HEADS, tm, 256), lambda i: (0, i, 0)),
                   pl.BlockSpec((MLA_HEADS, tm, 128), lambda i: (0, i, 0))],
        out_shape=[jax.ShapeDtypeStruct((MLA_HEADS, S, 256), BF16), jax.ShapeDtypeStruct((MLA_HEADS, S, 256), BF16),
                   jax.ShapeDtypeStruct((MLA_HEADS, S, 128), BF16)],
        compiler_params=_params(("parallel",)),
    )(proj, proj, proj, qn_w, kvn_w, wuqT, wukv, cos, sin)


def _causal(i, j, t):
    row = i * t + lax.broadcasted_iota(jnp.int32, (t, t), 0)
    col = j * t + lax.broadcasted_iota(jnp.int32, (t, t), 1)
    return col <= row


def _mla_attn_fwd(qc, kc, vv, *, t=256):
    S = qc.shape[1]
    t = min(t, S)

    def body(q_ref, k_ref, v_ref, o_ref, l_ref):
        i = pl.program_id(0)
        diag = _causal(0, 0, t)

        def step(j, carry, masked):
            rows = pl.ds(pl.multiple_of(j * t, t), t)
            out = []
            for h in range(MLA_HEADS):
                m, l, acc = carry[h]
                s = _dot_nt(q_ref[h], k_ref[h, rows, :]) * MLA_SCALE
                if masked:
                    s = jnp.where(diag, s, -jnp.inf)
                m_new = jnp.maximum(m, jnp.max(s, axis=-1, keepdims=True))
                alpha = jnp.exp(m - m_new)
                p = jnp.exp(s - m_new)
                l = alpha * l + jnp.sum(p, axis=-1, keepdims=True)
                acc = alpha * acc + _dot(p.astype(BF16), v_ref[h, rows, :])
                out.append((m_new, l, acc))
            return tuple(out)

        init = tuple((jnp.full((t, 1), -jnp.inf, F32), jnp.zeros((t, 1), F32), jnp.zeros((t, MLA_V), F32))
                     for _ in range(MLA_HEADS))
        carry = lax.fori_loop(0, i, lambda j, c: step(j, c, False), init)
        carry = step(i, carry, True)
        for h in range(MLA_HEADS):
            m, l, acc = carry[h]
            o_ref[:, 128 * h:128 * h + 128] = acc / l
            l_ref[h] = jnp.broadcast_to(m + jnp.log(l), (t, 128))

    return pl.pallas_call(
        body, name="mla_attn_fwd", grid=(S // t,),
        in_specs=[pl.BlockSpec((MLA_HEADS, t, 256), lambda i: (0, i, 0)),
                  pl.BlockSpec((MLA_HEADS, S, 256), lambda i: (0, 0, 0)),
                  pl.BlockSpec((MLA_HEADS, S, 128), lambda i: (0, 0, 0))],
        out_specs=[pl.BlockSpec((t, 512), lambda i: (i, 0)),
                   pl.BlockSpec((MLA_HEADS, t, 128), lambda i: (0, i, 0))],
        out_shape=[jax.ShapeDtypeStruct((S, 512), F32), jax.ShapeDtypeStruct((MLA_HEADS, S, 128), F32)],
        compiler_params=_params(("parallel",)),
    )(qc, kc, vv)


def _mla_attn_bwd(qc, kc, vv, o, lse, do, *, t=256):
    S = qc.shape[1]
    t = min(t, S)
    nblk = S // t
    hp = 2

    def body(q_ref, k_ref, v_ref, o_ref, l_ref, do_ref, dq_ref, dk_ref, dv_ref):
        j = pl.program_id(1)

        @pl.when(j == 0)
        def _():
            dq_ref[...] = jnp.zeros_like(dq_ref)

        diag = _causal(0, 0, t)

        def step(i, carry, masked):
            rows = pl.ds(pl.multiple_of(i * t, t), t)
            out = []
            for h in range(hp):
                dk, dv = carry[h]
                k = k_ref[h]
                q = q_ref[h, rows, :]
                dov = do_ref[rows, 128 * h:128 * h + 128]
                lrow = l_ref[h, rows, :][:, 0:1]
                p = jnp.exp(_dot_nt(q, k) * MLA_SCALE - lrow)
                if masked:
                    p = jnp.where(diag, p, 0.0)
                dob = dov.astype(BF16)
                dv = dv + _dot_tn(p.astype(BF16), dob)
                dp = _dot_nt(dob, v_ref[h])
                delta = jnp.sum(dov * o_ref[rows, 128 * h:128 * h + 128], axis=-1, keepdims=True)
                ds = (p * (dp - delta) * MLA_SCALE).astype(BF16)
                dk = dk + _dot_tn(ds, q)
                dq_ref[h, rows, :] += _dot(ds, k)
                out.append((dk, dv))
            return tuple(out)

        init = tuple((jnp.zeros((t, 256), F32), jnp.zeros((t, MLA_V), F32)) for _ in range(hp))
        carry = step(j, init, True)
        carry = lax.fori_loop(j + 1, nblk, lambda i, c: step(i, c, False), carry)
        for h in range(hp):
            dk_ref[h] = carry[h][0]
            dv_ref[h] = carry[h][1]

    return pl.pallas_call(
        body, name="mla_attn_bwd", grid=(MLA_HEADS // hp, nblk),
        in_specs=[pl.BlockSpec((hp, S, 256), lambda g, j: (g, 0, 0)),
                  pl.BlockSpec((hp, t, 256), lambda g, j: (g, j, 0)),
                  pl.BlockSpec((hp, t, 128), lambda g, j: (g, j, 0)),
                  pl.BlockSpec((S, 128 * hp), lambda g, j: (0, g)),
                  pl.BlockSpec((hp, S, 128), lambda g, j: (g, 0, 0)),
                  pl.BlockSpec((S, 128 * hp), lambda g, j: (0, g))],
        out_specs=[pl.BlockSpec((hp, S, 256), lambda g, j: (g, 0, 0)),
                   pl.BlockSpec((hp, t, 256), lambda g, j: (g, j, 0)),
                   pl.BlockSpec((hp, t, 128), lambda g, j: (g, j, 0))],
        out_shape=[jax.ShapeDtypeStruct((MLA_HEADS, S, 256), F32), jax.ShapeDtypeStruct((MLA_HEADS, S, 256), F32),
                   jax.ShapeDtypeStruct((MLA_HEADS, S, 128), F32)],
        compiler_params=_params(("parallel", "arbitrary")),
    )(qc, kc, vv, o, lse, do)


def _mla_pre_bwd(proj, qn_w, kvn_w, wuqT, wukv, cos, sin, dqc, dkc, dvv, *, tm=256):
    S = proj.shape[0]

    def body(ql_ref, kl_ref, qw_ref, kw_ref, wuq_ref, wukv_ref, cos_ref, sin_ref, dqc_ref, dkc_ref, dvv_ref,
             dql_ref, dkl_ref, dkr_ref, gq_ref, gkv_ref, part_ref):
        @pl.when(pl.program_id(0) == 0)
        def _():
            gq_ref[...] = jnp.zeros_like(gq_ref)
            gkv_ref[...] = jnp.zeros_like(gkv_ref)
            part_ref[...] = jnp.zeros_like(part_ref)

        cs, sn = cos_ref[...], sin_ref[...]
        half = lax.broadcasted_iota(jnp.int32, (tm, 128), 1) // 64
        ql = ql_ref[...]
        rq = _rstd(ql)
        qhat = ql * rq
        qw = qw_ref[...]
        qn = (qhat * qw).astype(BF16)
        chunks = []
        for pair in range(2):
            chunks.append(jnp.where(half == 0, dqc_ref[2 * pair, :, 128:256], dqc_ref[2 * pair + 1, :, 128:256]))
        dqr = jnp.concatenate(chunks, axis=1)
        dqr = dqr * cs + _swap_halves(dqr * sn)
        dq = jnp.concatenate([dqc_ref[h, :, 0:128] for h in range(MLA_HEADS)] + [dqr], axis=1).astype(BF16)
        gq_ref[...] += _dot_tn(dq, qn)
        dqn = _dot(dq, wuq_ref[...])
        part_ref[0:1, :] += jnp.sum(dqn * qhat, axis=0, keepdims=True)
        dql_ref[...] = _rms_bwd(dqn * qw, qhat, rq)
        kl = kl_ref[...]
        rk = _rstd(kl)
        khat = kl * rk
        kw = kw_ref[...]
        kvn = (khat * kw).astype(BF16)
        dkvn = jnp.zeros((tm, MLA_KVR), F32)
        dkr2 = jnp.zeros((tm, 128), F32)
        for h in range(MLA_HEADS):
            dkn = dkc_ref[h, :, 0:128].astype(BF16)
            dvh = dvv_ref[h].astype(BF16)
            gkv_ref[2 * h] += _dot_tn(kvn, dkn)
            gkv_ref[2 * h + 1] += _dot_tn(kvn, dvh)
            dkvn += _dot_nt(dkn, wukv_ref[2 * h]) + _dot_nt(dvh, wukv_ref[2 * h + 1])
            dkr2 += dkc_ref[h, :, 128:256]
        part_ref[1:2, 0:128] += jnp.sum(dkvn * khat, axis=0, keepdims=True)
        dkl_ref[...] = _rms_bwd(dkvn * kw, khat, rk)
        dkr = jnp.where(half == 0, dkr2 + pltpu.roll(dkr2, 64, 1), 0.0)
        dkr_ref[...] = dkr * cs[:, :128] + _swap_halves(dkr * sn[:, :128])

    const = lambda shape: pl.BlockSpec(shape, lambda i: tuple(0 for _ in shape))
    heads = lambda w: pl.BlockSpec((MLA_HEADS, tm, w), lambda i: (0, i, 0))
    return pl.pallas_call(
        body, name="mla_pre_bwd", grid=(S // tm,),
        in_specs=[pl.BlockSpec((tm, 256), lambda i: (i, 3)), pl.BlockSpec((tm, 128), lambda i: (i, 8)),
                  const((1, 256)), const((1, 128)), const((768, 256)), const((8, 128, 128)),
                  pl.BlockSpec((tm, 256), lambda i: (i, 0)), pl.BlockSpec((tm, 256), lambda i: (i, 0)),
                  heads(256), heads(256), heads(128)],
        out_specs=[pl.BlockSpec((tm, 256), lambda i: (i, 0)), pl.BlockSpec((tm, 128), lambda i: (i, 0)),
                   pl.BlockSpec((tm, 128), lambda i: (i, 0)), const((768, 256)), const((8, 128, 128)), const((8, 256))],
        out_shape=[jax.ShapeDtypeStruct((S, 256), F32), jax.ShapeDtypeStruct((S, 128), F32),
                   jax.ShapeDtypeStruct((S, 128), F32), jax.ShapeDtypeStruct((768, 256), F32),
                   jax.ShapeDtypeStruct((8, 128, 128), F32), jax.ShapeDtypeStruct((8, 256), F32)],
        compiler_params=_params(("arbitrary",)),
    )(proj, proj, qn_w, kvn_w, wuqT, wukv, cos, sin, dqc, dkc, dvv)


def _mix_out_fwd(x, oa, ob, w_o, vecs, *, tm=256):
    S = x.shape[0]

    def body(x_ref, oa_ref, ob_ref, w_ref, vec_ref, xo_ref, mo_ref):
        mo = _dot(oa_ref[...].astype(BF16), w_ref[0:512, :]) + _dot(ob_ref[...].astype(BF16), w_ref[512:1024, :])
        mo_ref[...] = mo
        xo_ref[...] = x_ref[...] + vec_ref[3:4, :] * mo

    row = pl.BlockSpec((tm, D), lambda i: (i, 0))
    half = pl.BlockSpec((tm, 512), lambda i: (i, 0))
    return pl.pallas_call(
        body, name="mix_out_fwd", grid=(S // tm,),
        in_specs=[row, half, half, pl.BlockSpec((D, D), lambda i: (0, 0)), pl.BlockSpec((8, D), lambda i: (0, 0))],
        out_specs=[row, row],
        out_shape=[jax.ShapeDtypeStruct((S, D), F32), jax.ShapeDtypeStruct((S, D), F32)],
        compiler_params=_params(("parallel",)),
    )(x, oa, ob, w_o, vecs)


def _mix_out_bwd(dxo, mo, oa, ob, w_o, vecs, *, tm=256):
    S = dxo.shape[0]

    def body(dx_ref, mo_ref, oa_ref, ob_ref, w_ref, vec_ref, doa_ref, dob_ref, gw_ref, part_ref):
        @pl.when(pl.program_id(0) == 0)
        def _():
            gw_ref[...] = jnp.zeros_like(gw_ref)
            part_ref[...] = jnp.zeros_like(part_ref)

        dx = dx_ref[...]
        part_ref[0:1, :] += jnp.sum(dx * mo_ref[...], axis=0, keepdims=True)
        dmo = (vec_ref[3:4, :] * dx).astype(BF16)
        doa_ref[...] = _dot_nt(dmo, w_ref[0:512, :])
        dob_ref[...] = _dot_nt(dmo, w_ref[512:1024, :])
        gw_ref[0:512, :] += _dot_tn(oa_ref[...].astype(BF16), dmo)
        gw_ref[512:1024, :] += _dot_tn(ob_ref[...].astype(BF16), dmo)

    row = pl.BlockSpec((tm, D), lambda i: (i, 0))
    half = pl.BlockSpec((tm, 512), lambda i: (i, 0))
    return pl.pallas_call(
        body, name="mix_out_bwd", grid=(S // tm,),
        in_specs=[row, row, half, half, pl.BlockSpec((D, D), lambda i: (0, 0)), pl.BlockSpec((8, D), lambda i: (0, 0))],
        out_specs=[half, half, pl.BlockSpec((D, D), lambda i: (0, 0)), pl.BlockSpec((8, D), lambda i: (0, 0))],
        out_shape=[jax.ShapeDtypeStruct((S, 512), F32), jax.ShapeDtypeStruct((S, 512), F32),
                   jax.ShapeDtypeStruct((D, D), F32), jax.ShapeDtypeStruct((8, D), F32)],
        compiler_params=_params(("arbitrary",)),
    )(dxo, mo, oa, ob, w_o, vecs)


def _mix_in_bwd(h, w_inT, dq, dk, dv, dql, dkl, dkr, *, tm=256):
    S = h.shape[0]
    offs = (0, 512, 640, 768, 1024, 1152)
    wid = (512, 128, 128, 256, 128, 128)

    def body(h_ref, w_ref, dq_ref, dk_ref, dv_ref, dql_ref, dkl_ref, dkr_ref, dh_ref, gw_ref):
        @pl.when(pl.program_id(0) == 0)
        def _():
            gw_ref[...] = jnp.zeros_like(gw_ref)

        hv = h_ref[...]
        dh = jnp.zeros((tm, D), F32)
        for ref, o, w in zip((dq_ref, dk_ref, dv_ref, dql_ref, dkl_ref, dkr_ref), offs, wid):
            w = min(w, D_IN - o)
            dpart = ref[...][:, :w].astype(BF16)
            dh += _dot(dpart, w_ref[o:o + w, :])
            gw_ref[o:o + w, :] += _dot_tn(dpart, hv)
        dh_ref[...] = dh

    row = pl.BlockSpec((tm, D), lambda i: (i, 0))
    part = lambda w: pl.BlockSpec((tm, w), lambda i: (i, 0))
    return pl.pallas_call(
        body, name="mix_in_bwd", grid=(S // tm,),
        in_specs=[row, pl.BlockSpec((D_IN_PAD, D), lambda i: (0, 0))] + [part(w) for w in wid],
        out_specs=[row, pl.BlockSpec((D_IN, D), lambda i: (0, 0))],
        out_shape=[jax.ShapeDtypeStruct((S, D), F32), jax.ShapeDtypeStruct((D_IN, D), F32)],
        compiler_params=_params(("arbitrary",)),
    )(h, w_inT, dq, dk, dv, dql, dkl, dkr)


def _vecs(norm_w, mod9, k):
    return jnp.concatenate([norm_w.reshape(1, D), mod9[3 * k:3 * k + 3], jnp.zeros((4, D), F32)], axis=0)


def _uq_group_rows(wuqT):
    per = MLA_NOPE + MLA_ROPE
    nope = [wuqT[per * h:per * h + MLA_NOPE] for h in range(MLA_HEADS)]
    rope = [wuqT[per * h + MLA_NOPE:per * (h + 1)] for h in range(MLA_HEADS)]
    return jnp.concatenate(nope + rope, axis=0)


def _uq_ungroup_rows(g):
    parts = []
    for h in range(MLA_HEADS):
        parts += [g[MLA_NOPE * h:MLA_NOPE * (h + 1)], g[512 + MLA_ROPE * h:512 + MLA_ROPE * (h + 1)]]
    return jnp.concatenate(parts, axis=0)


def _local_step(x, tgt, mod9, norms, sinks, rel_bias, q_norm, kv_norm, W, on_grads=None):
    if on_grads is None:
        on_grads = lambda group, grads, after, vecs: vecs
    S = x.shape[0]
    v1 = _vecs(norms["ffn1"], mod9, 0)
    v2 = _vecs(norms["mix"], mod9, 1)
    v3 = _vecs(norms["ffn2"], mod9, 2)
    bucket = jnp.asarray(_bucket_table())
    cos, sin = _rope_tables(S)
    if isinstance(W, dict):
        full, W = W, (lambda group, after, vecs: (full, vecs))

    W1, v1 = W("ffn1", [], v1)
    x1, h1, a1, b1, f1 = _ffn_fwd(x, v1, W1["g1T"], W1["u1T"], W1["d1"], name="ffn1_fwd", tm=512, tf=1408)
    W2, v2 = W("mixer", [x1], v2)
    w_inT = jnp.pad(W2["w_inT"], ((0, D_IN_PAD - D_IN), (0, 0))).astype(BF16)
    wuqT = _uq_group_rows(W2["w_uqT"])
    h2, proj = _mix_in_fwd(x1, v2, w_inT)
    bias = _bias_build(rel_bias, bucket)
    oa = _swa_fwd(proj, bias, sinks)
    qc, kc, vv = _mla_pre_fwd(proj, q_norm, kv_norm, wuqT, W2["w_ukv"], cos, sin)
    ob, lse = _mla_attn_fwd(qc, kc, vv)
    _, v2o = W("ffn2_on_its_way", [ob], v2)
    x2, mo = _mix_out_fwd(x1, oa, ob, W2["w_o"], v2o)
    W3, v3 = W("ffn2", [x2], v3)
    x3, h3, a3, b3, f3 = _ffn_fwd(x2, v3, W3["g3T"], W3["u3T"], W3["d3"], name="ffn2_fwd", tm=512, tf=1408)
    dx3, head_part = _head(x3, tgt, norms["final"])

    df3, g3_part = _ffn_bwd_pre(dx3, f3, v3, name="ffn2_bwd_pre")
    gg3, gu3, gd3, dh3 = _ffn_bwd_main(h3, df3, a3, b3, W3["g3T"], W3["u3T"], W3["d3"], name="ffn2_bwd", tm=2048, tf=256)
    ffn2 = {"g3T": gg3, "u3T": gu3, "d3": gd3}
    v3 = on_grads("ffn2", ffn2, [], v3)
    dx2, n3_part = _norm_bwd(dh3, x2, dx3, v3, name="ffn2_norm_bwd")
    v2 = on_grads("ffn2", None, [dx2], v2)
    doa, dob, g_wo, g2_part = _mix_out_bwd(dx2, mo, oa, ob, W2["w_o"], v2)
    dq, dk, dv, drb, dsk = _swa_bwd(proj, bias, sinks, oa, doa, bucket)
    dqc, dkc, dvv = _mla_attn_bwd(qc, kc, vv, ob, lse, dob)
    dql, dkl, dkr, g_uq, g_ukv, mla_part = _mla_pre_bwd(proj, q_norm, kv_norm, wuqT, W2["w_ukv"], cos, sin, dqc, dkc, dvv)
    dh2, g_win = _mix_in_bwd(h2, w_inT, dq, dk, dv, dql, dkl, dkr)
    mixer = {"w_inT": g_win, "w_uqT": _uq_ungroup_rows(g_uq).astype(BF16),
             "w_ukv": g_ukv.astype(BF16), "w_o": g_wo.astype(BF16)}
    v2 = on_grads("mixer", mixer, [], v2)
    dx1, n2_part = _norm_bwd(dh2, x1, dx2, v2, name="mix_norm_bwd")
    v1 = on_grads("mixer", None, [dx1], v1)
    df1, g1_part = _ffn_bwd_pre(dx1, f1, v1, name="ffn1_bwd_pre")
    gg1, gu1, gd1, dh1 = _ffn_bwd_main(h1, df1, a1, b1, W1["g1T"], W1["u1T"], W1["d1"], name="ffn1_bwd", tm=2048, tf=256)
    ffn1 = {"g1T": gg1, "u1T": gu1, "d1": gd1}
    v1 = on_grads("ffn1", ffn1, [], v1)
    dx0, n1_part = _norm_bwd(dh1, x, dx1, v1, name="ffn1_norm_bwd")

    grads = {**ffn1, **ffn2, **mixer}
    dmod9 = jnp.concatenate([n1_part[1:3], g1_part[0:1], n2_part[1:3], g2_part[0:1],
                             n3_part[1:3], g3_part[0:1]], axis=0)
    small = jnp.concatenate([n1_part[0], n2_part[0], n3_part[0], head_part[0], mla_part[0],
                             mla_part[1, :128], dsk[:, 0], head_part[1, 0:1], jnp.zeros((119,), F32), drb[:, 0]])
    return head_part[1, 0], dx0, grads, small, dmod9


SMALL_LAYOUT = (("norm_ffn1", 1024), ("norm_mix", 1024), ("norm_ffn2", 1024), ("norm_final", 1024),
                ("q_norm", 256), ("kv_norm", 128), ("sinks", 128), ("rel_bias", 256))
N_SMALL = sum(n for _, n in SMALL_LAYOUT)
LOSS_SLOT = 4 * 1024 + 256 + 128 + SWA_HEADS


def _coords():
    return lax.axis_index("x"), lax.axis_index("y"), lax.axis_index("c")


def _flip(v, bit):
    return 1 - v if bit else v


def _peer(r):
    x, y, c = _coords()
    return (_flip(x, r & 4), _flip(y, r & 2), _flip(c, r & 1))


def _mod_fwd(c_tile, w_mod, b_mod3):
    W = w_mod.shape[1]

    def body(c_ref, w_ref, b_ref, mod_ref, ca_ref, call_ref, part_ref, send_sems, recv_sems):
        x, y, c = _coords()
        me = 4 * x + 2 * y + c
        call_ref[me] = c_ref[...]
        sends = []
        for r in range(1, N_DEV):
            cp = pltpu.make_async_remote_copy(c_ref, call_ref.at[me], send_sems.at[0, r], recv_sems.at[0, r],
                                              device_id=_peer(r), device_id_type=MESH)
            cp.start()
            sends.append(cp)
        for r in range(1, N_DEV):
            pltpu.make_async_remote_copy(c_ref, call_ref.at[me], send_sems.at[0, r], recv_sems.at[0, r],
                                         device_id=_peer(r), device_id_type=MESH).wait_recv()
        cv = call_ref[...].reshape(8 * N_DEV, D)
        ca = (cv * _sigmoid(cv)).astype(BF16)
        ca_ref[...] = ca
        part_ref[...] = _dot(ca, w_ref[...].astype(BF16)).reshape(N_DEV, 8, W)
        mod_ref[me] = part_ref[me] + b_ref[me]
        for r in range(1, N_DEV):
            cp = pltpu.make_async_remote_copy(part_ref.at[me ^ r], mod_ref.at[me], send_sems.at[1, r],
                                              recv_sems.at[1, r], device_id=_peer(r), device_id_type=MESH)
            cp.start()
            sends.append(cp)
        for r in range(1, N_DEV):
            pltpu.make_async_remote_copy(part_ref.at[me ^ r], mod_ref.at[me], send_sems.at[1, r],
                                         recv_sems.at[1, r], device_id=_peer(r), device_id_type=MESH).wait_recv()
            mod_ref[me ^ r] = mod_ref[me ^ r] + b_ref[me ^ r]
        for cp in sends:
            cp.wait_send()

    vm = pl.BlockSpec(memory_space=pltpu.VMEM)
    return pl.pallas_call(
        body, name="mod_fwd", in_specs=[vm, vm, vm], out_specs=[vm, vm],
        out_shape=[jax.ShapeDtypeStruct((N_DEV, 8, W), F32), jax.ShapeDtypeStruct((8 * N_DEV, D), BF16)],
        scratch_shapes=[pltpu.VMEM((N_DEV, 8, D), F32), pltpu.VMEM((N_DEV, 8, W), F32),
                        pltpu.SemaphoreType.DMA((2, N_DEV)), pltpu.SemaphoreType.DMA((2, N_DEV))],
        compiler_params=_params(),
    )(c_tile, w_mod, b_mod3)


N_MODVEC = N_MOD * D
N_VEC = N_MODVEC + N_SMALL


def _mod_bwd(allvec, ca, me_idx):
    W = N_MODVEC // N_DEV

    def body(me_ref, all_ref, cols_ref, ca_ref, gw_ref, sum_ref):
        in_first_row = lax.broadcasted_iota(jnp.int32, (N_DEV, 8, W), 1) == 0
        dm = jnp.where(in_first_row, cols_ref[...], 0.0).reshape(8 * N_DEV, W)
        gw_ref[...] = _dot_tn(ca_ref[...], dm.astype(BF16))
        total = all_ref[0]
        for k in range(1, N_DEV):
            total = total + all_ref[k]
        sum_ref[...] = total

    return pl.pallas_call(
        body, name="mod_bwd",
        grid_spec=pltpu.PrefetchScalarGridSpec(
            num_scalar_prefetch=1, grid=(1,),
            in_specs=[pl.BlockSpec((N_DEV, 1, N_VEC), lambda i, me: (0, 0, 0)),
                      pl.BlockSpec((N_DEV, 1, W), lambda i, me: (0, 0, me[0])),
                      pl.BlockSpec((8 * N_DEV, D), lambda i, me: (0, 0))],
            out_specs=[pl.BlockSpec((D, W), lambda i, me: (0, 0)), pl.BlockSpec((1, N_VEC), lambda i, me: (0, 0))]),
        out_shape=[jax.ShapeDtypeStruct((D, W), F32), jax.ShapeDtypeStruct((1, N_VEC), F32)],
        compiler_params=_params(("arbitrary",)),
    )(me_idx, allvec, allvec, ca)


def _wgather(shards):
    n = len(shards)

    def body(*refs):
        ins, outs, token = refs[:n], refs[n:2 * n], refs[2 * n]
        send_sems, recv_sems, local_sems = refs[2 * n + 1:]
        token[...] = jnp.zeros_like(token)
        x, y, c = _coords()
        me = 4 * x + 2 * y + c
        sib = (x, y, 1 - c)
        chips = [(1 - x, y), (x, 1 - y), (1 - x, 1 - y)]

        def copy(k, slot, block, to, src=None):
            return pltpu.make_async_remote_copy(
                src_ref=outs[k].at[block] if src is None else src, dst_ref=outs[k].at[block],
                send_sem=send_sems.at[k, slot], recv_sem=recv_sems.at[k, slot], device_id=to, device_id_type=MESH)

        local = [pltpu.make_async_copy(ins[k], outs[k].at[me], local_sems.at[k]) for k in range(n)]
        for cp in local:
            cp.start()
        first = []
        for k in range(n):
            first.append(copy(k, 0, me, sib, src=ins[k]))
            for j, chip in enumerate(chips):
                first.append(copy(k, 1 + j, me, (*chip, c), src=ins[k]))
        for cp in first:
            cp.start()
        passed = []
        for j, (cx, cy) in enumerate(chips):
            for k in range(n):
                blk = 4 * cx + 2 * cy + c
                copy(k, 1 + j, blk, sib).wait_recv()
                cp = copy(k, 4 + j, blk, sib)
                cp.start()
                passed.append(cp)
        for k in range(n):
            copy(k, 0, 4 * x + 2 * y + (1 - c), sib).wait_recv()
            for j, (cx, cy) in enumerate(chips):
                copy(k, 4 + j, 4 * cx + 2 * cy + (1 - c), sib).wait_recv()
        for cp in first + passed:
            cp.wait_send()
        for cp in local:
            cp.wait()

    anyspec = pl.BlockSpec(memory_space=pl.ANY)
    return pl.pallas_call(
        body, name="wgather", in_specs=[anyspec] * n,
        out_specs=[anyspec] * n + [pl.BlockSpec(memory_space=pltpu.VMEM)],
        out_shape=[jax.ShapeDtypeStruct((N_DEV,) + s.shape, s.dtype) for s in shards]
        + [jax.ShapeDtypeStruct((8, 128), F32)],
        scratch_shapes=[pltpu.SemaphoreType.DMA((n, 7)), pltpu.SemaphoreType.DMA((n, 7)),
                        pltpu.SemaphoreType.DMA((n,))],
    )(*shards)


class _GatherCopies:
    def __init__(self, lands, send_sems, recv_sems):
        x, y, c = _coords()
        me = 4 * x + 2 * y + c
        sib = (x, y, 1 - c)
        chips = [(1 - x, y), (x, 1 - y), (1 - x, 1 - y)]

        def copy(k, slot, block, to):
            return pltpu.make_async_remote_copy(
                src_ref=lands[k].at[block], dst_ref=lands[k].at[block],
                send_sem=send_sems.at[7 * k + slot], recv_sem=recv_sems.at[7 * k + slot],
                device_id=to, device_id_type=MESH)

        n = len(lands)
        self.first = [copy(k, 0, me, sib) for k in range(n)]
        self.first += [copy(k, 1 + j, me, (cx, cy, c)) for j, (cx, cy) in enumerate(chips) for k in range(n)]
        self.landed = [copy(k, 1 + j, 4 * cx + 2 * cy + c, sib) for j, (cx, cy) in enumerate(chips) for k in range(n)]
        self.passed = [copy(k, 4 + j, 4 * cx + 2 * cy + c, sib) for j, (cx, cy) in enumerate(chips) for k in range(n)]
        self.from_sib = [copy(k, 0, 4 * x + 2 * y + (1 - c), sib) for k in range(n)]
        self.from_sib += [copy(k, 4 + j, 4 * cx + 2 * cy + (1 - c), sib) for j, (cx, cy) in enumerate(chips)
                          for k in range(n)]


def _gather_start(lands, *, name):
    n = len(lands)

    def body(*refs):
        for cp in _GatherCopies(refs[:n], refs[n], refs[n + 1]).first:
            cp.start()
        refs[-1][...] = jnp.zeros_like(refs[-1])

    out = pl.pallas_call(
        body, name=name,
        out_shape=(pltpu.SemaphoreType.DMA((7 * n,)), pltpu.SemaphoreType.DMA((7 * n,)),
                   *[pltpu.HBM(l.shape, l.dtype) for l in lands], jax.ShapeDtypeStruct((8, 128), F32)),
        in_specs=[HBM_SPEC] * n,
        out_specs=(SEM_SPEC, SEM_SPEC, *[HBM_SPEC] * n, pl.BlockSpec(memory_space=pltpu.VMEM)),
        input_output_aliases={i: 2 + i for i in range(n)},
        compiler_params=pltpu.CompilerParams(has_side_effects=DATAFLOW),
    )(*[_in_hbm(l) for l in lands])
    return out[0], out[1], list(out[2:2 + n]), out[-1]


def _gather_pass(send_sems, recv_sems, lands, after, *, name, stage):
    n = len(lands)

    def body(*refs):
        cps = _GatherCopies(refs[:n], refs[n], refs[n + 1])
        if stage == "landed":
            for cp in cps.landed:
                cp.wait_recv()
        else:
            for cp in cps.passed:
                cp.start()
        refs[-1][...] = jnp.zeros_like(refs[-1])

    out = pl.pallas_call(
        body, name=name,
        out_shape=(*[pltpu.HBM(l.shape, l.dtype) for l in lands], jax.ShapeDtypeStruct((8, 128), F32)),
        in_specs=[HBM_SPEC] * n + [SEM_SPEC, SEM_SPEC] + [pl.BlockSpec(memory_space=pl.ANY)] * len(after),
        out_specs=(*[HBM_SPEC] * n, pl.BlockSpec(memory_space=pltpu.VMEM)),
        input_output_aliases={i: i for i in range(n)},
        compiler_params=pltpu.CompilerParams(has_side_effects=DATAFLOW),
    )(*lands, send_sems, recv_sems, *after)
    return list(out[:n]), out[-1]


def _gather_end(send_sems, recv_sems, lands, after, *, name):
    n = len(lands)

    def body(*refs):
        cps = _GatherCopies(refs[:n], refs[n], refs[n + 1])
        for cp in cps.from_sib:
            cp.wait_recv()
        for cp in cps.first + cps.passed:
            cp.wait_send()

    out = pl.pallas_call(
        body, name=name,
        out_shape=[pltpu.HBM(l.shape, l.dtype) for l in lands],
        in_specs=[HBM_SPEC] * n + [SEM_SPEC, SEM_SPEC] + [pl.BlockSpec(memory_space=pl.ANY)] * len(after),
        out_specs=[HBM_SPEC] * n,
        input_output_aliases={i: i for i in range(n)},
        compiler_params=pltpu.CompilerParams(has_side_effects=DATAFLOW),
    )(*lands, send_sems, recv_sems, *after)
    return list(out)


def _d2d_copies(grads, lands, send_sems, recv_sems):
    x, y, c = _coords()
    return [pltpu.make_async_remote_copy(
        src_ref=grads[k].at[2 * q + (1 - c)], dst_ref=lands[k].at[q],
        send_sem=send_sems.at[4 * k + q], recv_sem=recv_sems.at[4 * k + q],
        device_id=(x, y, 1 - c), device_id_type=MESH) for k in range(len(grads)) for q in range(4)]


def _vec_copies(srcs, lands, send_sems, recv_sems):
    x, y, c = _coords()
    me = 4 * x + 2 * y + c
    return [pltpu.make_async_remote_copy(
        src_ref=lands[0].at[me], dst_ref=lands[0].at[me], send_sem=send_sems.at[r - 1], recv_sem=recv_sems.at[r - 1],
        device_id=_peer(r), device_id_type=MESH) for r in range(1, N_DEV)]


def _chipsum(g, sib, cidx, *, name):
    _, r, cc = g.shape

    def body(c_ref, g_ref, s_ref, o_ref):
        o_ref[...] = (g_ref[...].astype(F32) + s_ref[...].astype(F32)).astype(o_ref.dtype)

    return pl.pallas_call(
        body, name=name,
        grid_spec=pltpu.PrefetchScalarGridSpec(
            num_scalar_prefetch=1, grid=(4,),
            in_specs=[pl.BlockSpec((1, r, cc), lambda q, c_ref: (2 * q + c_ref[0], 0, 0)),
                      pl.BlockSpec((1, r, cc), lambda q, c_ref: (q, 0, 0))],
            out_specs=pl.BlockSpec((1, r, cc), lambda q, c_ref: (q, 0, 0))),
        out_shape=jax.ShapeDtypeStruct((4, r, cc), g.dtype),
        compiler_params=_params(("arbitrary",)),
    )(cidx, g, sib)


HBM_SPEC = pl.BlockSpec(memory_space=pltpu.HBM)
SEM_SPEC = pl.BlockSpec(memory_space=pltpu.SEMAPHORE)
DATAFLOW = pltpu.SideEffectType.DATAFLOW_SIDE_EFFECTING


def _in_hbm(a):
    return pltpu.with_memory_space_constraint(a, pltpu.HBM)


def _ici_copies(sums, lands, send_sems, recv_sems):
    x, y, c = _coords()
    chips = [(1 - x, y), (x, 1 - y), (1 - x, 1 - y)]
    cps = []
    for k in range(len(sums)):
        for j, (cx, cy) in enumerate(chips):
            cps.append(pltpu.make_async_remote_copy(
                src_ref=sums[k].at[2 * cx + cy], dst_ref=lands[k].at[j],
                send_sem=send_sems.at[3 * k + j], recv_sem=recv_sems.at[3 * k + j],
                device_id=(cx, cy, c), device_id_type=MESH))
    return cps


def _split_start(copies, srcs, lands, n_sems, after, *, name):
    ns, nl = len(srcs), len(lands)

    def body(*refs):
        for cp in copies(refs[:ns], refs[ns:ns + nl], refs[ns + nl + len(after)], refs[ns + nl + len(after) + 1]):
            cp.start()
        refs[-1][...] = jnp.zeros_like(refs[-1])

    bufs = [_in_hbm(a) for a in list(srcs) + list(lands)]
    out = pl.pallas_call(
        body, name=name,
        out_shape=(pltpu.SemaphoreType.DMA((n_sems,)), pltpu.SemaphoreType.DMA((n_sems,)),
                   *[pltpu.HBM(a.shape, a.dtype) for a in bufs], jax.ShapeDtypeStruct((8, 128), F32)),
        in_specs=[HBM_SPEC] * len(bufs) + [pl.BlockSpec(memory_space=pl.ANY)] * len(after),
        out_specs=(SEM_SPEC, SEM_SPEC, *[HBM_SPEC] * len(bufs), pl.BlockSpec(memory_space=pltpu.VMEM)),
        input_output_aliases={i: 2 + i for i in range(len(bufs))},
        compiler_params=pltpu.CompilerParams(has_side_effects=DATAFLOW),
    )(*bufs, *after)
    return out[0], out[1], list(out[2:2 + ns]), list(out[2 + ns:2 + ns + nl]), out[-1]


def _split_wait(copies, send_sems, recv_sems, srcs, lands, after, *, name):
    ns, nl = len(srcs), len(lands)

    def body(*refs):
        for cp in copies(refs[:ns], refs[ns:ns + nl], refs[ns + nl], refs[ns + nl + 1]):
            cp.wait_send()
            cp.wait_recv()

    out = pl.pallas_call(
        body, name=name,
        out_shape=[pltpu.HBM(a.shape, a.dtype) for a in list(srcs) + list(lands)],
        in_specs=[HBM_SPEC] * (ns + nl) + [SEM_SPEC, SEM_SPEC] + [pl.BlockSpec(memory_space=pl.ANY)] * len(after),
        out_specs=[HBM_SPEC] * (ns + nl),
        input_output_aliases={i: i for i in range(ns + nl)},
        compiler_params=pltpu.CompilerParams(has_side_effects=DATAFLOW),
    )(*srcs, *lands, send_sems, recv_sems, *after)
    return list(out[:ns]), list(out[ns:])


ADAM_C1 = 1.0 / (1.0 - ADAM_B1 ** ADAM_STEP)
ADAM_C2 = 1.0 / (1.0 - ADAM_B2 ** ADAM_STEP)


def _adam_math(w, g, m, v):
    m2 = ADAM_B1 * m + (1.0 - ADAM_B1) * g
    v2 = ADAM_B2 * v + (1.0 - ADAM_B2) * (g * g)
    return -ADAM_LR * ((m2 * ADAM_C1) / (jnp.sqrt(v2 * ADAM_C2) + ADAM_EPS) + ADAM_WD * w), m2, v2


def _adamw(w, g, m, v, *, name):
    R, C = w.shape
    tr = R if R <= 512 else 256

    def body(w_ref, g_ref, m_ref, v_ref, d_ref, nm_ref, nv_ref):
        d_ref[...], nm_ref[...], nv_ref[...] = _adam_math(w_ref[...], g_ref[...], m_ref[...], v_ref[...])

    blk = pl.BlockSpec((tr, C), lambda i: (i, 0))
    return pl.pallas_call(
        body, name=name, grid=(R // tr,), in_specs=[blk] * 4, out_specs=[blk] * 3,
        out_shape=[jax.ShapeDtypeStruct((R, C), F32)] * 3,
        compiler_params=_params(("parallel",)),
    )(w, g, m, v)


def _adamw_rs(w, m, v, cs, rcv, qidx, *, name):
    r, cc = w.shape
    tr = r // 2 if r % 32 == 0 and r > 128 else r

    def body(q_ref, w_ref, m_ref, v_ref, c_ref, r_ref, g_ref, d_ref, nm_ref, nv_ref):
        g = ((c_ref[0].astype(F32) + r_ref[0].astype(F32)) + r_ref[1].astype(F32)) + r_ref[2].astype(F32)
        g_ref[...] = g
        d_ref[...], nm_ref[...], nv_ref[...] = _adam_math(w_ref[...], g, m_ref[...], v_ref[...])

    blk = pl.BlockSpec((tr, cc), lambda i, q_ref: (i, 0))
    return pl.pallas_call(
        body, name=name,
        grid_spec=pltpu.PrefetchScalarGridSpec(
            num_scalar_prefetch=1, grid=(r // tr,),
            in_specs=[blk, blk, blk, pl.BlockSpec((1, tr, cc), lambda i, q_ref: (q_ref[0], i, 0)),
                      pl.BlockSpec((3, tr, cc), lambda i, q_ref: (0, i, 0))],
            out_specs=[blk] * 4),
        out_shape=[jax.ShapeDtypeStruct((r, cc), F32)] * 4,
        compiler_params=_params(("arbitrary",)),
    )(qidx, w, m, v, cs, rcv)


SMALL_PARAMS = ("norm_ffn1", "norm_mix", "norm_ffn2", "norm_final", "q_norm", "kv_norm", "sinks", "rel_bias", "b_mod")


def _adamw_small(gvec, wmv):
    widths = [wmv[3 * i].shape[1] for i in range(len(SMALL_PARAMS))]

    def body(*refs):
        g_all = refs[0]
        ins = refs[1:1 + 3 * len(SMALL_PARAMS)]
        outs = refs[1 + 3 * len(SMALL_PARAMS):]
        off = N_MODVEC
        for i, name in enumerate(SMALL_PARAMS):
            g_ref, d_ref, nm_ref, nv_ref = outs[4 * i:4 * i + 4]
            w_ref, m_ref, v_ref = ins[3 * i:3 * i + 3]
            start = 0 if name == "b_mod" else off
            g = g_all[:, start:start + widths[i]]
            g_ref[...] = g
            d_ref[...], nm_ref[...], nv_ref[...] = _adam_math(w_ref[...], g, m_ref[...], v_ref[...])
            if name != "b_mod":
                off += dict(SMALL_LAYOUT)[name]

    vm = pl.BlockSpec(memory_space=pltpu.VMEM)
    n_out = 4 * len(SMALL_PARAMS)
    out = pl.pallas_call(
        body, name="adamw_small", in_specs=[vm] * (1 + len(wmv)), out_specs=[vm] * n_out,
        out_shape=[jax.ShapeDtypeStruct((1, widths[i // 4]), F32) for i in range(n_out)],
        compiler_params=_params(),
    )(gvec, *wmv)
    return {name: out[4 * i:4 * i + 4] for i, name in enumerate(SMALL_PARAMS)}


TRANSPOSED = ("g1T", "u1T", "g3T", "u3T", "w_inT", "w_uqT")


def kernel(x, c, w_mod, b_mod, norm_ffn1, ffn1_gate, ffn1_up, ffn1_down, norm_mix, w_in, q_norm, kv_norm, w_uq, w_ukv, sinks, w_o, norm_ffn2, ffn2_gate, ffn2_up, ffn2_down, rel_bias, norm_final, loss_target, m_w_mod, m_b_mod, m_norm_ffn1, m_ffn1_gate, m_ffn1_up, m_ffn1_down, m_norm_mix, m_w_in, m_q_norm, m_kv_norm, m_w_uq, m_w_ukv, m_sinks, m_w_o, m_norm_ffn2, m_ffn2_gate, m_ffn2_up, m_ffn2_down, m_rel_bias, m_norm_final, v_w_mod, v_b_mod, v_norm_ffn1, v_ffn1_gate, v_ffn1_up, v_ffn1_down, v_norm_mix, v_w_in, v_q_norm, v_kv_norm, v_w_uq, v_w_ukv, v_sinks, v_w_o, v_norm_ffn2, v_ffn2_gate, v_ffn2_up, v_ffn2_down, v_rel_bias, v_norm_final):
    mx, my, mc = _coords()
    cidx = jnp.reshape(mc, (1,)).astype(jnp.int32)
    qidx = jnp.reshape(2 * mx + my, (1,)).astype(jnp.int32)
    WM = w_mod.shape[2]

    c_tile = jnp.pad(c, ((0, 7), (0, 0)))
    b_mod3 = jnp.pad(b_mod.reshape(N_DEV, 1, WM), ((0, 0), (0, 7), (0, 0)))
    mod3, ca = _mod_fwd(c_tile, w_mod[0], b_mod3)
    mod9 = mod3[:, 0, :].reshape(N_MOD, D)

    shards = {"g1T": ffn1_gate[0].T.astype(BF16), "u1T": ffn1_up[0].T.astype(BF16), "d1": ffn1_down[0].astype(BF16),
              "g3T": ffn2_gate[0].T.astype(BF16), "u3T": ffn2_up[0].T.astype(BF16), "d3": ffn2_down[0].astype(BF16),
              "w_inT": w_in[0].T, "w_uqT": w_uq[0].T.astype(BF16), "w_ukv": w_ukv[0].astype(BF16),
              "w_o": w_o[0].astype(BF16)}
    me = 4 * mx + 2 * my + mc
    groups = {"ffn1": ("g1T", "u1T", "d1"), "mixer": ("w_inT", "w_uqT", "w_ukv", "w_o"), "ffn2": ("g3T", "u3T", "d3")}
    arriving = {}

    def as_weights(group, gathered):
        return {k: g if k == "w_ukv" else g.reshape(N_DEV * g.shape[1], g.shape[2])
                for k, g in zip(groups[group], gathered)}

    def start_gather(group, token):
        lands = []
        for k in groups[group]:
            sh = shards[k] + token[0, 0].astype(shards[k].dtype)
            lands.append(lax.dynamic_update_slice(lax.empty((N_DEV,) + sh.shape, sh.dtype), sh[None], (me, 0, 0)))
        send, recv, lands, started = _gather_start(lands, name="gather_start_" + group)
        arriving[group] = (send, recv, lands)
        return started

    def fetch(group, after, vecs):
        if group == "ffn1":
            *gathered, token = _wgather([shards[k] + ca[1, 0].astype(shards[k].dtype) for k in groups["ffn1"]])
            token = start_gather("ffn2", start_gather("mixer", token))
            return as_weights("ffn1", gathered), vecs + token[0:1, 0:1]
        def pass_on(group, after):
            send, recv, lands = arriving[group]
            lands, token = _gather_pass(send, recv, lands, after, name="gather_landed_" + group, stage="landed")
            lands, token = _gather_pass(send, recv, lands, [token], name="gather_onward_" + group, stage="onward")
            arriving[group] = (send, recv, lands)
            return token

        if group == "ffn2_on_its_way":
            return None, vecs + pass_on("ffn2", after)[0:1, 0:1]
        if group == "mixer":
            after = [pass_on("mixer", after)]
        send, recv, lands = arriving[group]
        return as_weights(group, _gather_end(send, recv, lands, after, name="gather_end_" + group)), vecs

    norms ={"ffn1": norm_ffn1, "mix": norm_mix, "ffn2": norm_ffn2, "final": norm_final.reshape(1, D)}
    in_flight = {}

    def on_grads(group, g, after, vecs, before_ici=()):
        if g is not None:
            names = list(g)
            by_dest = [g[k] if k == "w_ukv" else g[k].reshape((N_DEV, g[k].shape[0] // N_DEV) + g[k].shape[1:])
                       for k in names]
            lands = [lax.empty((4,) + a.shape[1:], a.dtype) for a in by_dest]
            send, recv, by_dest, lands, token = _split_start(_d2d_copies, by_dest, lands, 4 * len(names), after,
                                                             name="rs_d2d_start_" + group)
            in_flight[group] = (names, send, recv, by_dest, lands)
            return vecs + token[0:1, 0:1]
        names, send, recv, by_dest, lands = in_flight[group]
        by_dest, from_sib = _split_wait(_d2d_copies, send, recv, by_dest, lands, after, name="rs_d2d_wait_" + group)
        sums = [_chipsum(a, s, cidx, name="chipsum_" + k) for k, a, s in zip(names, by_dest, from_sib)]
        lands = [lax.empty((3,) + s.shape[1:], s.dtype) for s in sums]
        send, recv, sums, lands, token = _split_start(_ici_copies, sums, lands, 3 * len(names), list(before_ici),
                                                      name="rs_ici_start_" + group)
        in_flight[group] = (names, send, recv, sums, lands, token)
        return vecs + token[0:1, 0:1]

    _, grad_x, _, small, dmod9 = _local_step(
        x[0], loss_target[0], mod9, norms, sinks, rel_bias, q_norm, kv_norm, fetch, on_grads=on_grads)

    vec = jnp.concatenate([dmod9.reshape(N_MODVEC), small]).reshape(1, 1, N_VEC)
    allvec = lax.dynamic_update_slice(lax.empty((N_DEV, 1, N_VEC), F32), vec, (me, 0, 0))
    vsend, vrecv, _, (allvec,), vec_started = _split_start(_vec_copies, [], [allvec], N_DEV - 1, [], name="vec_start")
    on_grads("ffn1", None, [grad_x], jnp.zeros((1, 1), F32), before_ici=[vec_started])

    owners = {"g1T": ("ffn1_gate", ffn1_gate, m_ffn1_gate, v_ffn1_gate), "u1T": ("ffn1_up", ffn1_up, m_ffn1_up, v_ffn1_up),
              "d1": ("ffn1_down", ffn1_down, m_ffn1_down, v_ffn1_down),
              "g3T": ("ffn2_gate", ffn2_gate, m_ffn2_gate, v_ffn2_gate), "u3T": ("ffn2_up", ffn2_up, m_ffn2_up, v_ffn2_up),
              "d3": ("ffn2_down", ffn2_down, m_ffn2_down, v_ffn2_down),
              "w_inT": ("w_in", w_in, m_w_in, v_w_in), "w_uqT": ("w_uq", w_uq, m_w_uq, v_w_uq),
              "w_ukv": ("w_ukv", w_ukv, m_w_ukv, v_w_ukv), "w_o": ("w_o", w_o, m_w_o, v_w_o)}
    res = {}

    def finish(group, after):
        names, send, recv, sums, lands, _ = in_flight[group]
        sums, lands = _split_wait(_ici_copies, send, recv, sums, lands, after, name="rs_ici_wait_" + group)
        for k, cs, rc in zip(names, sums, lands):
            pname, wk, mk, vk = owners[k]
            there = (lambda a: a[0].T) if k in TRANSPOSED else (lambda a: a[0])
            back = (lambda a: a.T[None]) if k in TRANSPOSED else (lambda a: a[None])
            res[pname] = tuple(back(a) for a in _adamw_rs(there(wk), there(mk), there(vk), cs, rc, qidx,
                                                          name="adamw_" + pname))

    ffn1_started = in_flight["ffn1"][5]
    finish("ffn2", [ffn1_started])
    finish("mixer", [ffn1_started])

    _, (allvec,) = _split_wait(_vec_copies, vsend, vrecv, [], [allvec], [ffn1_started], name="vec_wait")
    g_wmod, gvec = _mod_bwd(allvec, ca, jnp.reshape(me, (1,)).astype(jnp.int32))
    loss = gvec[0, N_MODVEC + LOSS_SLOT]
    res["w_mod"] = tuple(a[None] for a in (g_wmod,) + tuple(_adamw(w_mod[0], g_wmod, m_w_mod[0], v_w_mod[0],
                                                                    name="adamw_w_mod")))
    small_in = {"norm_ffn1": (norm_ffn1, m_norm_ffn1, v_norm_ffn1), "norm_mix": (norm_mix, m_norm_mix, v_norm_mix),
                "norm_ffn2": (norm_ffn2, m_norm_ffn2, v_norm_ffn2), "norm_final": (norm_final, m_norm_final, v_norm_final),
                "q_norm": (q_norm, m_q_norm, v_q_norm), "kv_norm": (kv_norm, m_kv_norm, v_kv_norm),
                "sinks": (sinks, m_sinks, v_sinks), "rel_bias": (rel_bias, m_rel_bias, v_rel_bias),
                "b_mod": (b_mod, m_b_mod, v_b_mod)}
    small_out = _adamw_small(gvec, [a.reshape(1, -1) for k in SMALL_PARAMS for a in small_in[k]])
    for k in SMALL_PARAMS:
        res[k] = tuple(a.reshape(small_in[k][0].shape) for a in small_out[k])

    finish("ffn1", [res[k][3] for k in ("w_mod", "w_o", "ffn2_down", "b_mod")])

    order = ("w_mod", "b_mod", "norm_ffn1", "ffn1_gate", "ffn1_up", "ffn1_down", "norm_mix", "w_in", "q_norm",
             "kv_norm", "w_uq", "w_ukv", "sinks", "w_o", "norm_ffn2", "ffn2_gate", "ffn2_up", "ffn2_down",
             "rel_bias", "norm_final")
    return (loss, grad_x[None]) + tuple(res[nm][kind] for kind in range(4) for nm in order)
```

```python
import functools
import math

import numpy as np
import jax
import jax.numpy as jnp
from jax import lax
from jax.experimental import pallas as pl
from jax.experimental.pallas import tpu as pltpu

F32 = jnp.float32
BF16 = jnp.bfloat16
MESH = pl.DeviceIdType.MESH

N_DEV = 8
D = 1024
D_FF = 2816
EPS = 1e-6
N_MOD = 9
SWA_HEADS = 8
SWA_DH = 64
WINDOW = 128
MLA_HEADS = 4
MLA_NOPE = 128
MLA_ROPE = 64
MLA_V = 128
MLA_QR = 256
MLA_KVR = 128
ROPE_THETA = 10000.0
NUM_BUCKETS = 32
D_IN = 1216
D_IN_PAD = 1280
SWA_SCALE = SWA_DH ** -0.5
MLA_SCALE = (MLA_NOPE + MLA_ROPE) ** -0.5

ADAM_LR = 0.001
ADAM_B1 = 0.9
ADAM_B2 = 0.999
ADAM_EPS = 1e-08
ADAM_WD = 0.01
ADAM_STEP = 10

V7X_VMEM_LIMIT = 56 * 1024 * 1024

NT_DIMS = (((1,), (1,)), ((), ()))
TN_DIMS = (((0,), (0,)), ((), ()))


def _dot(a, b):
    return jnp.dot(a, b, preferred_element_type=F32)


def _dot_nt(a, b):
    return lax.dot_general(a, b, NT_DIMS, preferred_element_type=F32)


def _dot_tn(a, b):
    return lax.dot_general(a, b, TN_DIMS, preferred_element_type=F32)


def _params(sem=None):
    return pltpu.CompilerParams(dimension_semantics=sem, vmem_limit_bytes=V7X_VMEM_LIMIT)


def _rstd(x):
    return lax.rsqrt(jnp.mean(x * x, axis=-1, keepdims=True) + EPS)


def _rms_bwd(dy, xhat, r):
    return r * (dy - xhat * jnp.mean(dy * xhat, axis=-1, keepdims=True))


def _sigmoid(a):
    return 1.0 / (1.0 + jnp.exp(-a))


def _ffn_fwd(x, vecs, wgT, wuT, wd, *, name, tm=512, tf=256):
    S = x.shape[0]
    tm = min(tm, S)
    ni, nj = S // tm, D_FF // tf

    def body(x_ref, vec_ref, wg_ref, wu_ref, wd_ref, xo_ref, h_ref, a_ref, b_ref, f_ref, acc_ref):
        j = pl.program_id(1)

        @pl.when(j == 0)
        def _():
            xv = x_ref[...]
            hn = xv * _rstd(xv) * vec_ref[0:1, :]
            h_ref[...] = (hn * (1.0 + vec_ref[2:3, :]) + vec_ref[1:2, :]).astype(BF16)
            acc_ref[...] = jnp.zeros_like(acc_ref)

        h = h_ref[...]
        a = _dot_nt(h, wg_ref[...])
        b = _dot_nt(h, wu_ref[...])
        a_ref[...] = a.astype(BF16)
        b_ref[...] = b.astype(BF16)
        hsw = (a * _sigmoid(a) * b).astype(BF16)
        acc_ref[...] += _dot(hsw, wd_ref[...])

        @pl.when(j == nj - 1)
        def _():
            f = acc_ref[...]
            f_ref[...] = f
            xo_ref[...] = x_ref[...] + (0.5 * vec_ref[3:4, :]) * f

    row = pl.BlockSpec((tm, D), lambda i, j: (i, 0))
    wspec = pl.BlockSpec((tf, D), lambda i, j: (j, 0))
    act = pl.BlockSpec((tm, tf), lambda i, j: (i, j))
    return pl.pallas_call(
        body, name=name, grid=(ni, nj),
        in_specs=[row, pl.BlockSpec((8, D), lambda i, j: (0, 0)), wspec, wspec, wspec],
        out_specs=[row, row, act, act, row],
        out_shape=[jax.ShapeDtypeStruct((S, D), F32), jax.ShapeDtypeStruct((S, D), BF16),
                   jax.ShapeDtypeStruct((S, D_FF), BF16), jax.ShapeDtypeStruct((S, D_FF), BF16),
                   jax.ShapeDtypeStruct((S, D), F32)],
        scratch_shapes=[pltpu.VMEM((tm, D), F32)],
        compiler_params=_params(("parallel", "arbitrary")),
    )(x, vecs, wgT, wuT, wd)


def _ffn_bwd_main(h, df, a, b, wgT, wuT, wd, *, name, after=(), tm=512, tf=256):
    S = h.shape[0]
    tm = min(tm, S)
    ni, nj = S // tm, D_FF // tf

    def body(h_hbm, df_hbm, a_ref, b_ref, wg_ref, wu_ref, wd_ref, *rest):
        gg_ref, gu_ref, gd_ref, dh_hbm, h_v, df_v, dh_v, gg_acc, gu_acc, gd_acc, sem = rest[len(after):]
        j = pl.program_id(0)
        i = pl.program_id(1)

        @pl.when((j == 0) & (i == 0))
        def _():
            c1 = pltpu.make_async_copy(h_hbm, h_v, sem.at[0])
            c2 = pltpu.make_async_copy(df_hbm, df_v, sem.at[1])
            c1.start()
            c2.start()
            c1.wait()
            c2.wait()

        @pl.when(i == 0)
        def _():
            gg_acc[...] = jnp.zeros_like(gg_acc)
            gu_acc[...] = jnp.zeros_like(gu_acc)
            gd_acc[...] = jnp.zeros_like(gd_acc)

        rows = pl.ds(pl.multiple_of(i * tm, tm), tm)
        hi = h_v[rows, :]
        dfi = df_v[rows, :]
        av = a_ref[...].astype(F32)
        bv = b_ref[...].astype(F32)
        sg = _sigmoid(av)
        sa = av * sg
        hsw = (sa * bv).astype(BF16)
        dhsw = _dot_nt(dfi, wd_ref[...])
        da = (dhsw * bv * (sg * (1.0 + av * (1.0 - sg)))).astype(BF16)
        db = (dhsw * sa).astype(BF16)
        gd_acc[...] += _dot_tn(hsw, dfi)
        gg_acc[...] += _dot_tn(da, hi)
        gu_acc[...] += _dot_tn(db, hi)
        dh = _dot(da, wg_ref[...]) + _dot(db, wu_ref[...])

        @pl.when(j == 0)
        def _():
            dh_v[rows, :] = dh

        @pl.when(j > 0)
        def _():
            dh_v[rows, :] += dh

        @pl.when(i == ni - 1)
        def _():
            gg_ref[...] = gg_acc[...].astype(BF16)
            gu_ref[...] = gu_acc[...].astype(BF16)
            gd_ref[...] = gd_acc[...].astype(BF16)

        @pl.when((j == nj - 1) & (i == ni - 1))
        def _():
            c3 = pltpu.make_async_copy(dh_v, dh_hbm, sem.at[2])
            c3.start()
            c3.wait()

    anyspec = pl.BlockSpec(memory_space=pl.ANY)
    wspec = pl.BlockSpec((tf, D), lambda j, i: (j, 0))
    act = pl.BlockSpec((tm, tf), lambda j, i: (i, j))
    return pl.pallas_call(
        body, name=name, grid=(nj, ni),
        in_specs=[anyspec, anyspec, act, act, wspec, wspec, wspec] + [anyspec] * len(after),
        out_specs=[wspec, wspec, wspec, anyspec],
        out_shape=[jax.ShapeDtypeStruct((D_FF, D), BF16)] * 3 + [jax.ShapeDtypeStruct((S, D), F32)],
        scratch_shapes=[pltpu.VMEM((S, D), BF16), pltpu.VMEM((S, D), BF16), pltpu.VMEM((S, D), F32),
                        pltpu.VMEM((tf, D), F32), pltpu.VMEM((tf, D), F32), pltpu.VMEM((tf, D), F32),
                        pltpu.SemaphoreType.DMA((3,))],
        compiler_params=_params(("arbitrary", "arbitrary")),
    )(h, df, a, b, wgT, wuT, wd, *after)


def _ffn_out_bwd(dx, f, gate, df_ref, part_ref):
    df_ref[...] = ((0.5 * gate) * dx).astype(BF16)
    part_ref[3:4, :] += 0.5 * jnp.sum(dx * f, axis=0, keepdims=True)


def _norm_bwd(dh, x, dxo, vecs, *, name, below=None, tm=256):
    S = x.shape[0]

    def body(dh_ref, x_ref, dxo_ref, vec_ref, *rest):
        dx_ref, part_ref = rest[-2 if below is None else -3], rest[-1 if below is None else -2]

        @pl.when(pl.program_id(0) == 0)
        def _():
            part_ref[...] = jnp.zeros_like(part_ref)

        dh = dh_ref[...]
        xv = x_ref[...]
        r = _rstd(xv)
        xhat = xv * r
        w = vec_ref[0:1, :]
        xn = xhat * w
        dxn = dh * (1.0 + vec_ref[2:3, :])
        part_ref[0:1, :] += jnp.sum(dxn * xhat, axis=0, keepdims=True)
        part_ref[1:2, :] += jnp.sum(dh, axis=0, keepdims=True)
        part_ref[2:3, :] += jnp.sum(dh * xn, axis=0, keepdims=True)
        dx = dxo_ref[...] + _rms_bwd(dxn * w, xhat, r)
        dx_ref[...] = dx
        if below is not None:
            _ffn_out_bwd(dx, rest[0][...], rest[1][3:4, :], rest[-1], part_ref)

    row = pl.BlockSpec((tm, D), lambda i: (i, 0))
    vec = pl.BlockSpec((8, D), lambda i: (0, 0))
    extra = [] if below is None else [row, vec]
    return pl.pallas_call(
        body, name=name, grid=(S // tm,), in_specs=[row, row, row, vec] + extra,
        out_specs=[row, vec] + ([] if below is None else [row]),
        out_shape=[jax.ShapeDtypeStruct((S, D), F32), jax.ShapeDtypeStruct((8, D), F32)]
        + ([] if below is None else [jax.ShapeDtypeStruct((S, D), BF16)]),
        compiler_params=_params(("arbitrary",)),
    )(dh, x, dxo, vecs, *([] if below is None else below))


def _head(x, tgt, nf, f, vecs, *, tm=256):
    S = x.shape[0]

    def body(x_ref, t_ref, nf_ref, f_ref, vec_ref, dx_ref, part_ref, df_ref):
        @pl.when(pl.program_id(0) == 0)
        def _():
            part_ref[...] = jnp.zeros_like(part_ref)

        xv = x_ref[...]
        r = _rstd(xv)
        xhat = xv * r
        w = nf_ref[...]
        e = xhat * w - t_ref[...]
        dy = e * (1.0 / D)
        part_ref[0:1, :] += jnp.sum(dy * xhat, axis=0, keepdims=True)
        part_ref[1:2, :] += jnp.sum(e * e) * (0.5 / D)
        dx = _rms_bwd(dy * w, xhat, r)
        dx_ref[...] = dx
        _ffn_out_bwd(dx, f_ref[...], vec_ref[3:4, :], df_ref, part_ref)

    row = pl.BlockSpec((tm, D), lambda i: (i, 0))
    vec = pl.BlockSpec((8, D), lambda i: (0, 0))
    return pl.pallas_call(
        body, name="head", grid=(S // tm,),
        in_specs=[row, row, pl.BlockSpec((1, D), lambda i: (0, 0)), row, vec],
        out_specs=[row, vec, row],
        out_shape=[jax.ShapeDtypeStruct((S, D), F32), jax.ShapeDtypeStruct((8, D), F32),
                   jax.ShapeDtypeStruct((S, D), BF16)],
        compiler_params=_params(("arbitrary",)),
    )(x, tgt, nf, f, vecs)


def _mix_in_fwd(x, vecs, w_inT, *, tm=256):
    S = x.shape[0]

    def body(x_ref, vec_ref, w_ref, h_ref, p_ref):
        xv = x_ref[...]
        hn = xv * _rstd(xv) * vec_ref[0:1, :]
        h = (hn * (1.0 + vec_ref[2:3, :]) + vec_ref[1:2, :]).astype(BF16)
        h_ref[...] = h
        p_ref[...] = _dot_nt(h, w_ref[...])

    row = pl.BlockSpec((tm, D), lambda i: (i, 0))
    return pl.pallas_call(
        body, name="mix_in_fwd", grid=(S // tm,),
        in_specs=[row, pl.BlockSpec((8, D), lambda i: (0, 0)), pl.BlockSpec((D_IN_PAD, D), lambda i: (0, 0))],
        out_specs=[row, pl.BlockSpec((tm, D_IN_PAD), lambda i: (i, 0))],
        out_shape=[jax.ShapeDtypeStruct((S, D), BF16), jax.ShapeDtypeStruct((S, D_IN_PAD), F32)],
        compiler_params=_params(("parallel",)),
    )(x, vecs, w_inT)


def _bucket_table():
    qi = np.arange(WINDOW)[:, None]
    kj = np.arange(2 * WINDOW)[None, :]
    dist = qi + WINDOW - kj
    max_exact = NUM_BUCKETS // 2
    n = np.maximum(dist, 0)
    nf = np.maximum(n, 1).astype(np.float32)
    large = max_exact + (np.log(nf / np.float32(max_exact)) / np.float32(math.log(WINDOW / max_exact))
                         * np.float32(NUM_BUCKETS - max_exact)).astype(np.int32)
    large = np.minimum(large, NUM_BUCKETS - 1)
    return np.where(n < max_exact, n, large).astype(np.int32)


def _bias_build(rel_bias, bucket):
    def body(rb_ref, bk_ref, out_ref):
        bk = bk_ref[...]
        for h in range(SWA_HEADS):
            acc = jnp.zeros((WINDOW, 2 * WINDOW), F32)
            for b in range(NUM_BUCKETS):
                acc = jnp.where(bk == b, rb_ref[b, h], acc)
            out_ref[h] = acc

    return pl.pallas_call(
        body, name="bias_build",
        in_specs=[pl.BlockSpec(memory_space=pltpu.SMEM), pl.BlockSpec(memory_space=pltpu.VMEM)],
        out_specs=pl.BlockSpec(memory_space=pltpu.VMEM),
        out_shape=jax.ShapeDtypeStruct((SWA_HEADS, WINDOW, 2 * WINDOW), F32),
    )(rel_bias, bucket)


SWA_GROUP = 4
GROUP_ROWS = SWA_GROUP * WINDOW


def _swa_valid(n):
    row = lax.broadcasted_iota(jnp.int32, (GROUP_ROWS, 2 * WINDOW), 0) % WINDOW
    col = lax.broadcasted_iota(jnp.int32, (GROUP_ROWS, 2 * WINDOW), 1)
    dist = row + WINDOW - col
    return (dist >= 0) & (dist < WINDOW) & ((col >= WINDOW) | (n > 0))


def _stack_heads(x, g):
    return jnp.concatenate([x[:, 64 * h:64 * h + 64] for h in range(SWA_GROUP * g, SWA_GROUP * (g + 1))], axis=0)


def _unstack_heads(x4):
    return jnp.concatenate([x4[WINDOW * a:WINDOW * (a + 1)] for a in range(SWA_GROUP)], axis=1)


def _group_sinks(sink_ref, g):
    head = lax.broadcasted_iota(jnp.int32, (GROUP_ROWS, 1), 0) // WINDOW
    out = jnp.full((GROUP_ROWS, 1), sink_ref[0, SWA_GROUP * g], F32)
    for a in range(1, SWA_GROUP):
        out = jnp.where(head == a, sink_ref[0, SWA_GROUP * g + a], out)
    return out


def _swa_probs(qh, kk, bias_h, sink, valid):
    s = _dot_nt(qh, kk) * SWA_SCALE + bias_h
    s = jnp.where(valid, s, -jnp.inf)
    m = jnp.maximum(jnp.max(s, axis=-1, keepdims=True), sink)
    p = jnp.exp(s - m)
    ps = jnp.exp(sink - m)
    inv = 1.0 / (jnp.sum(p, axis=-1, keepdims=True) + ps)
    return p * inv, ps * inv


def _swa_specs():
    prev = lambda n: jnp.maximum(n - 1, 0)
    return [pl.BlockSpec((WINDOW, 512), lambda n: (n, 0)),
            pl.BlockSpec((WINDOW, 128), lambda n: (n, 4)),
            pl.BlockSpec((WINDOW, 128), lambda n: (prev(n), 4)),
            pl.BlockSpec((WINDOW, 128), lambda n: (n, 5)),
            pl.BlockSpec((WINDOW, 128), lambda n: (prev(n), 5)),
            pl.BlockSpec((SWA_HEADS, WINDOW, 2 * WINDOW), lambda n: (0, 0, 0)),
            pl.BlockSpec(memory_space=pltpu.SMEM)]


def _swa_fwd(proj, bias, sinks):
    S = proj.shape[0]

    def body(q_ref, kc_ref, kp_ref, vc_ref, vp_ref, bias_ref, sink_ref, o_ref):
        valid = _swa_valid(pl.program_id(0))
        q = q_ref[...].astype(BF16)
        kfull = jnp.concatenate([kp_ref[...], kc_ref[...]], axis=0).astype(BF16)
        vfull = jnp.concatenate([vp_ref[...], vc_ref[...]], axis=0).astype(BF16)
        for g in range(SWA_HEADS // SWA_GROUP):
            kk = kfull[:, 64 * g:64 * g + 64]
            vv = vfull[:, 64 * g:64 * g + 64]
            bias4 = bias_ref[SWA_GROUP * g:SWA_GROUP * (g + 1)].reshape(GROUP_ROWS, 2 * WINDOW)
            pk, _ = _swa_probs(_stack_heads(q, g), kk, bias4, _group_sinks(sink_ref, g), valid)
            o_ref[:, 256 * g:256 * (g + 1)] = _unstack_heads(_dot(pk.astype(BF16), vv))

    return pl.pallas_call(
        body, name="swa_fwd", grid=(S // WINDOW,),
        in_specs=_swa_specs(),
        out_specs=pl.BlockSpec((WINDOW, 512), lambda n: (n, 0)),
        out_shape=jax.ShapeDtypeStruct((S, 512), F32),
        compiler_params=_params(("parallel",)),
    )(proj, proj, proj, proj, proj, bias, sinks)


def _swa_bwd(proj, bias, sinks, o, do, bucket):
    S = proj.shape[0]
    nb = S // WINDOW

    def body(q_ref, kc_ref, kp_ref, vc_ref, vp_ref, bias_ref, sink_ref, o_ref, do_ref, bk_ref,
             dq_ref, dk_ref, dv_ref, drb_ref, dsk_ref, dbias_acc):
        n = pl.program_id(0)

        @pl.when(n == 0)
        def _():
            dk_ref[...] = jnp.zeros_like(dk_ref)
            dv_ref[...] = jnp.zeros_like(dv_ref)
            dsk_ref[...] = jnp.zeros_like(dsk_ref)
            dbias_acc[...] = jnp.zeros_like(dbias_acc)
            drb_ref[...] = jnp.zeros_like(drb_ref)

        valid = _swa_valid(n)
        q = q_ref[...].astype(BF16)
        dov = do_ref[...]
        kfull = jnp.concatenate([kp_ref[...], kc_ref[...]], axis=0).astype(BF16)
        vfull = jnp.concatenate([vp_ref[...], vc_ref[...]], axis=0).astype(BF16)
        prow = pl.ds(pl.multiple_of(jnp.maximum(n - 1, 0) * WINDOW, WINDOW), WINDOW)
        crow = pl.ds(pl.multiple_of(n * WINDOW, WINDOW), WINDOW)
        ov = o_ref[...]
        for g in range(SWA_HEADS // SWA_GROUP):
            heads = slice(SWA_GROUP * g, SWA_GROUP * (g + 1))
            kk = kfull[:, 64 * g:64 * g + 64]
            vv = vfull[:, 64 * g:64 * g + 64]
            q4 = _stack_heads(q, g)
            pk, psink = _swa_probs(q4, kk, bias_ref[heads].reshape(GROUP_ROWS, 2 * WINDOW), _group_sinks(sink_ref, g), valid)
            pkb = pk.astype(BF16)
            do4 = _stack_heads(dov, g)
            dob = do4.astype(BF16)
            dp = _dot_nt(dob, vv)
            delta = jnp.sum(do4 * _stack_heads(ov, g), axis=-1, keepdims=True)
            ds = pk * (dp - delta)
            dsink = -psink * delta
            for a in range(SWA_GROUP):
                h = SWA_GROUP * g + a
                part = jnp.sum(dsink[WINDOW * a:WINDOW * (a + 1)], keepdims=True)
                dsk_ref[h:h + 1, :] += jnp.broadcast_to(part, (1, 128))
            dbias_acc[heads] += ds.reshape(SWA_GROUP, WINDOW, 2 * WINDOW)
            dsb = (ds * SWA_SCALE).astype(BF16)
            dq_ref[:, 256 * g:256 * (g + 1)] = _unstack_heads(_dot(dsb, kk))
            dkk = _dot_tn(dsb, q4)
            dvv = _dot_tn(pkb, dob)
            dk_ref[prow, 64 * g:64 * g + 64] += dkk[:WINDOW]
            dk_ref[crow, 64 * g:64 * g + 64] += dkk[WINDOW:]
            dv_ref[prow, 64 * g:64 * g + 64] += dvv[:WINDOW]
            dv_ref[crow, 64 * g:64 * g + 64] += dvv[WINDOW:]

        @pl.when(n == nb - 1)
        def _():
            bk = bk_ref[...]
            for h in range(SWA_HEADS):
                dbh = dbias_acc[h]
                for b in range(NUM_BUCKETS):
                    val = jnp.sum(jnp.where(bk == b, dbh, 0.0), keepdims=True)
                    drb_ref[b * 8 + h:b * 8 + h + 1, :] = jnp.broadcast_to(val, (1, 128))

    full = lambda shape: pl.BlockSpec(shape, lambda n: tuple(0 for _ in shape))
    return pl.pallas_call(
        body, name="swa_bwd", grid=(nb,),
        in_specs=_swa_specs() + [pl.BlockSpec((WINDOW, 512), lambda n: (n, 0)),
                                 pl.BlockSpec((WINDOW, 512), lambda n: (n, 0)), full((WINDOW, 2 * WINDOW))],
        out_specs=[pl.BlockSpec((WINDOW, 512), lambda n: (n, 0)), full((S, 128)), full((S, 128)),
                   full((NUM_BUCKETS * 8, 128)), full((8, 128))],
        out_shape=[jax.ShapeDtypeStruct((S, 512), F32), jax.ShapeDtypeStruct((S, 128), F32),
                   jax.ShapeDtypeStruct((S, 128), F32), jax.ShapeDtypeStruct((NUM_BUCKETS * 8, 128), F32),
                   jax.ShapeDtypeStruct((8, 128), F32)],
        scratch_shapes=[pltpu.VMEM((SWA_HEADS, WINDOW, 2 * WINDOW), F32)],
        compiler_params=_params(("arbitrary",)),
    )(proj, proj, proj, proj, proj, bias, sinks, o, do, bucket)


def _rope_tables(S):
    inv = ROPE_THETA ** (-jnp.arange(0, MLA_ROPE, 2, dtype=F32) / MLA_ROPE)
    ang = jnp.arange(S, dtype=F32)[:, None] * inv[None, :]
    cos, sin = jnp.cos(ang), jnp.sin(ang)
    return jnp.tile(jnp.concatenate([cos, cos], axis=1), (1, 4)), jnp.tile(jnp.concatenate([-sin, sin], axis=1), (1, 4))


def _swap_halves(x):
    w = x.shape[-1]
    lane = lax.broadcasted_iota(jnp.int32, x.shape, x.ndim - 1)
    return jnp.where((lane % 64) < 32, pltpu.roll(x, w - 32, x.ndim - 1), pltpu.roll(x, 32, x.ndim - 1))


def _mla_pre_fwd(proj, qn_w, kvn_w, wuqT, wukv, cos, sin, *, tm=256):
    S = proj.shape[0]

    def body(ql_ref, kl_ref, kr_ref, qw_ref, kw_ref, wuq_ref, wukv_ref, cos_ref, sin_ref,
             qc_ref, kc_ref, vv_ref):
        ql = ql_ref[...]
        qn = (ql * _rstd(ql) * qw_ref[...]).astype(BF16)
        q = _dot_nt(qn, wuq_ref[...])
        cs, sn = cos_ref[...], sin_ref[...]
        qr = q[:, 512:768]
        qr = qr * cs + _swap_halves(qr) * sn
        half = lax.broadcasted_iota(jnp.int32, (tm, 128), 1) // 64
        kl = kl_ref[...]
        kvn = (kl * _rstd(kl) * kw_ref[...]).astype(BF16)
        kr = kr_ref[...]
        kr = kr * cs[:, :128] + _swap_halves(kr) * sn[:, :128]
        kr2 = (kr + pltpu.roll(kr, 64, 1)).astype(BF16)
        for h in range(MLA_HEADS):
            qc_ref[h, :, 0:128] = q[:, 128 * h:128 * h + 128].astype(BF16)
            chunk = qr[:, 128 * (h // 2):128 * (h // 2) + 128]
            qc_ref[h, :, 128:256] = jnp.where(half == (h % 2), chunk, 0.0).astype(BF16)
            kc_ref[h, :, 0:128] = _dot(kvn, wukv_ref[2 * h]).astype(BF16)
            kc_ref[h, :, 128:256] = kr2
            vv_ref[h] = _dot(kvn, wukv_ref[2 * h + 1]).astype(BF16)

    const = lambda shape: pl.BlockSpec(shape, lambda i: tuple(0 for _ in shape))
    return pl.pallas_call(
        body, name="mla_pre_fwd", grid=(S // tm,),
        in_specs=[pl.BlockSpec((tm, 256), lambda i: (i, 3)), pl.BlockSpec((tm, 128), lambda i: (i, 8)),
                  pl.BlockSpec((tm, 128), lambda i: (i, 9)), const((1, 256)), const((1, 128)),
                  const((768, 256)), const((8, 128, 128)),
                  pl.BlockSpec((tm, 256), lambda i: (i, 0)), pl.BlockSpec((tm, 256), lambda i: (i, 0))],
        out_specs=[pl.BlockSpec((MLA_HEADS, tm, 256), lambda i: (0, i, 0)),
                   pl.BlockSpec((MLA_HEADS, tm, 256), lambda i: (0, i, 0)),
                   pl.BlockSpec((MLA_HEADS, tm, 128), lambda i: (0, i, 0))],
        out_shape=[jax.ShapeDtypeStruct((MLA_HEADS, S, 256), BF16), jax.ShapeDtypeStruct((MLA_HEADS, S, 256), BF16),
                   jax.ShapeDtypeStruct((MLA_HEADS, S, 128), BF16)],
        compiler_params=_params(("parallel",)),
    )(proj, proj, proj, qn_w, kvn_w, wuqT, wukv, cos, sin)


def _causal(i, j, t):
    row = i * t + lax.broadcasted_iota(jnp.int32, (t, t), 0)
    col = j * t + lax.broadcasted_iota(jnp.int32, (t, t), 1)
    return col <= row


def _mla_attn_fwd(qc, kc, vv, *, t=256):
    S = qc.shape[1]
    t = min(t, S)

    def body(q_ref, k_ref, v_ref, o_ref, l_ref):
        i = pl.program_id(0)
        diag = _causal(0, 0, t)

        def step(j, carry, masked):
            rows = pl.ds(pl.multiple_of(j * t, t), t)
            out = []
            for h in range(MLA_HEADS):
                m, l, acc = carry[h]
                s = _dot_nt(q_ref[h], k_ref[h, rows, :]) * MLA_SCALE
                if masked:
                    s = jnp.where(diag, s, -jnp.inf)
                m_new = jnp.maximum(m, jnp.max(s, axis=-1, keepdims=True))
                alpha = jnp.exp(m - m_new)
                p = jnp.exp(s - m_new)
                l = alpha * l + jnp.sum(p, axis=-1, keepdims=True)
                acc = alpha * acc + _dot(p.astype(BF16), v_ref[h, rows, :])
                out.append((m_new, l, acc))
            return tuple(out)

        init = tuple((jnp.full((t, 1), -jnp.inf, F32), jnp.zeros((t, 1), F32), jnp.zeros((t, MLA_V), F32))
                     for _ in range(MLA_HEADS))
        carry = lax.fori_loop(0, i, lambda j, c: step(j, c, False), init)
        carry = step(i, carry, True)
        for h in range(MLA_HEADS):
            m, l, acc = carry[h]
            o_ref[:, 128 * h:128 * h + 128] = acc / l
            l_ref[h] = jnp.broadcast_to(m + jnp.log(l), (t, 128))

    return pl.pallas_call(
        body, name="mla_attn_fwd", grid=(S // t,),
        in_specs=[pl.BlockSpec((MLA_HEADS, t, 256), lambda i: (0, i, 0)),
                  pl.BlockSpec((MLA_HEADS, S, 256), lambda i: (0, 0, 0)),
                  pl.BlockSpec((MLA_HEADS, S, 128), lambda i: (0, 0, 0))],
        out_specs=[pl.BlockSpec((t, 512), lambda i: (i, 0)),
                   pl.BlockSpec((MLA_HEADS, t, 128), lambda i: (0, i, 0))],
        out_shape=[jax.ShapeDtypeStruct((S, 512), F32), jax.ShapeDtypeStruct((MLA_HEADS, S, 128), F32)],
        compiler_params=_params(("parallel",)),
    )(qc, kc, vv)


def _mla_attn_bwd(qc, kc, vv, o, lse, do, *, t=256):
    S = qc.shape[1]
    t = min(t, S)
    nblk = S // t
    hp = MLA_HEADS
    once = pl.Buffered(1)

    def body(q_ref, k_ref, v_ref, o_ref, l_ref, do_ref, dq_ref, dk_ref, dv_ref):
        j = pl.program_id(1)

        @pl.when(j == 0)
        def _():
            dq_ref[...] = jnp.zeros_like(dq_ref)

        diag = _causal(0, 0, t)

        def step(i, carry, masked):
            rows = pl.ds(pl.multiple_of(i * t, t), t)
            out = []
            for h in range(hp):
                dk, dv = carry[h]
                k = k_ref[h]
                q = q_ref[h, rows, :]
                dov = do_ref[rows, 128 * h:128 * h + 128]
                lrow = l_ref[h, rows, :][:, 0:1]
                p = jnp.exp(_dot_nt(q, k) * MLA_SCALE - lrow)
                if masked:
                    p = jnp.where(diag, p, 0.0)
                dob = dov.astype(BF16)
                dv = dv + _dot_tn(p.astype(BF16), dob)
                dp = _dot_nt(dob, v_ref[h])
                delta = jnp.sum(dov * o_ref[rows, 128 * h:128 * h + 128], axis=-1, keepdims=True)
                ds = (p * (dp - delta) * MLA_SCALE).astype(BF16)
                dk = dk + _dot_tn(ds, q)
                dq_ref[h, rows, :] += _dot(ds, k)
                out.append((dk, dv))
            return tuple(out)

        init = tuple((jnp.zeros((t, 256), F32), jnp.zeros((t, MLA_V), F32)) for _ in range(hp))
        carry = step(j, init, True)
        carry = lax.fori_loop(j + 1, nblk, lambda i, c: step(i, c, False), carry)
        for h in range(hp):
            dk_ref[h] = carry[h][0]
            dv_ref[h] = carry[h][1]

    return pl.pallas_call(
        body, name="mla_attn_bwd", grid=(MLA_HEADS // hp, nblk),
        in_specs=[pl.BlockSpec((hp, S, 256), lambda g, j: (g, 0, 0), pipeline_mode=once),
                  pl.BlockSpec((hp, t, 256), lambda g, j: (g, j, 0)),
                  pl.BlockSpec((hp, t, 128), lambda g, j: (g, j, 0)),
                  pl.BlockSpec((S, 128 * hp), lambda g, j: (0, g), pipeline_mode=once),
                  pl.BlockSpec((hp, S, 128), lambda g, j: (g, 0, 0), pipeline_mode=once),
                  pl.BlockSpec((S, 128 * hp), lambda g, j: (0, g), pipeline_mode=once)],
        out_specs=[pl.BlockSpec((hp, S, 256), lambda g, j: (g, 0, 0)),
                   pl.BlockSpec((hp, t, 256), lambda g, j: (g, j, 0)),
                   pl.BlockSpec((hp, t, 128), lambda g, j: (g, j, 0))],
        out_shape=[jax.ShapeDtypeStruct((MLA_HEADS, S, 256), F32), jax.ShapeDtypeStruct((MLA_HEADS, S, 256), F32),
                   jax.ShapeDtypeStruct((MLA_HEADS, S, 128), F32)],
        compiler_params=_params(("parallel", "arbitrary")),
    )(qc, kc, vv, o, lse, do)


def _mla_pre_bwd(proj, qn_w, kvn_w, wuqT, wukv, cos, sin, dqc, dkc, dvv, *, tm=256):
    S = proj.shape[0]

    def body(ql_ref, kl_ref, qw_ref, kw_ref, wuq_ref, wukv_ref, cos_ref, sin_ref, dqc_ref, dkc_ref, dvv_ref,
             dql_ref, dkl_ref, dkr_ref, gq_ref, gkv_ref, part_ref):
        @pl.when(pl.program_id(0) == 0)
        def _():
            gq_ref[...] = jnp.zeros_like(gq_ref)
            gkv_ref[...] = jnp.zeros_like(gkv_ref)
            part_ref[...] = jnp.zeros_like(part_ref)

        cs, sn = cos_ref[...], sin_ref[...]
        half = lax.broadcasted_iota(jnp.int32, (tm, 128), 1) // 64
        ql = ql_ref[...]
        rq = _rstd(ql)
        qhat = ql * rq
        qw = qw_ref[...]
        qn = (qhat * qw).astype(BF16)
        chunks = []
        for pair in range(2):
            chunks.append(jnp.where(half == 0, dqc_ref[2 * pair, :, 128:256], dqc_ref[2 * pair + 1, :, 128:256]))
        dqr = jnp.concatenate(chunks, axis=1)
        dqr = dqr * cs + _swap_halves(dqr * sn)
        dq = jnp.concatenate([dqc_ref[h, :, 0:128] for h in range(MLA_HEADS)] + [dqr], axis=1).astype(BF16)
        gq_ref[...] += _dot_tn(dq, qn)
        dqn = _dot(dq, wuq_ref[...])
        part_ref[0:1, :] += jnp.sum(dqn * qhat, axis=0, keepdims=True)
        dql_ref[...] = _rms_bwd(dqn * qw, qhat, rq)
        kl = kl_ref[...]
        rk = _rstd(kl)
        khat = kl * rk
        kw = kw_ref[...]
        kvn = (khat * kw).astype(BF16)
        dkvn = jnp.zeros((tm, MLA_KVR), F32)
        dkr2 = jnp.zeros((tm, 128), F32)
        for h in range(MLA_HEADS):
            dkn = dkc_ref[h, :, 0:128].astype(BF16)
            dvh = dvv_ref[h].astype(BF16)
            gkv_ref[2 * h] += _dot_tn(kvn, dkn)
            gkv_ref[2 * h + 1] += _dot_tn(kvn, dvh)
            dkvn += _dot_nt(dkn, wukv_ref[2 * h]) + _dot_nt(dvh, wukv_ref[2 * h + 1])
            dkr2 += dkc_ref[h, :, 128:256]
        part_ref[1:2, 0:128] += jnp.sum(dkvn * khat, axis=0, keepdims=True)
        dkl_ref[...] = _rms_bwd(dkvn * kw, khat, rk)
        dkr = jnp.where(half == 0, dkr2 + pltpu.roll(dkr2, 64, 1), 0.0)
        dkr_ref[...] = dkr * cs[:, :128] + _swap_halves(dkr * sn[:, :128])

    const = lambda shape: pl.BlockSpec(shape, lambda i: tuple(0 for _ in shape))
    heads = lambda w: pl.BlockSpec((MLA_HEADS, tm, w), lambda i: (0, i, 0))
    return pl.pallas_call(
        body, name="mla_pre_bwd", grid=(S // tm,),
        in_specs=[pl.BlockSpec((tm, 256), lambda i: (i, 3)), pl.BlockSpec((tm, 128), lambda i: (i, 8)),
                  const((1, 256)), const((1, 128)), const((768, 256)), const((8, 128, 128)),
                  pl.BlockSpec((tm, 256), lambda i: (i, 0)), pl.BlockSpec((tm, 256), lambda i: (i, 0)),
                  heads(256), heads(256), heads(128)],
        out_specs=[pl.BlockSpec((tm, 256), lambda i: (i, 0)), pl.BlockSpec((tm, 128), lambda i: (i, 0)),
                   pl.BlockSpec((tm, 128), lambda i: (i, 0)), const((768, 256)), const((8, 128, 128)), const((8, 256))],
        out_shape=[jax.ShapeDtypeStruct((S, 256), F32), jax.ShapeDtypeStruct((S, 128), F32),
                   jax.ShapeDtypeStruct((S, 128), F32), jax.ShapeDtypeStruct((768, 256), F32),
                   jax.ShapeDtypeStruct((8, 128, 128), F32), jax.ShapeDtypeStruct((8, 256), F32)],
        compiler_params=_params(("arbitrary",)),
    )(proj, proj, qn_w, kvn_w, wuqT, wukv, cos, sin, dqc, dkc, dvv)


def _mix_out_fwd(x, oa, ob, w_o, vecs, *, tm=256):
    S = x.shape[0]

    def body(x_ref, oa_ref, ob_ref, w_ref, vec_ref, xo_ref, mo_ref):
        mo = _dot(oa_ref[...].astype(BF16), w_ref[0:512, :]) + _dot(ob_ref[...].astype(BF16), w_ref[512:1024, :])
        mo_ref[...] = mo
        xo_ref[...] = x_ref[...] + vec_ref[3:4, :] * mo

    row = pl.BlockSpec((tm, D), lambda i: (i, 0))
    half = pl.BlockSpec((tm, 512), lambda i: (i, 0))
    return pl.pallas_call(
        body, name="mix_out_fwd", grid=(S // tm,),
        in_specs=[row, half, half, pl.BlockSpec((D, D), lambda i: (0, 0)), pl.BlockSpec((8, D), lambda i: (0, 0))],
        out_specs=[row, row],
        out_shape=[jax.ShapeDtypeStruct((S, D), F32), jax.ShapeDtypeStruct((S, D), F32)],
        compiler_params=_params(("parallel",)),
    )(x, oa, ob, w_o, vecs)


def _mix_out_bwd(dxo, mo, oa, ob, w_o, vecs, *, tm=256):
    S = dxo.shape[0]

    def body(dx_ref, mo_ref, oa_ref, ob_ref, w_ref, vec_ref, doa_ref, dob_ref, gw_ref, part_ref):
        @pl.when(pl.program_id(0) == 0)
        def _():
            gw_ref[...] = jnp.zeros_like(gw_ref)
            part_ref[...] = jnp.zeros_like(part_ref)

        dx = dx_ref[...]
        part_ref[0:1, :] += jnp.sum(dx * mo_ref[...], axis=0, keepdims=True)
        dmo = (vec_ref[3:4, :] * dx).astype(BF16)
        doa_ref[...] = _dot_nt(dmo, w_ref[0:512, :])
        dob_ref[...] = _dot_nt(dmo, w_ref[512:1024, :])
        gw_ref[0:512, :] += _dot_tn(oa_ref[...].astype(BF16), dmo)
        gw_ref[512:1024, :] += _dot_tn(ob_ref[...].astype(BF16), dmo)

    row = pl.BlockSpec((tm, D), lambda i: (i, 0))
    half = pl.BlockSpec((tm, 512), lambda i: (i, 0))
    return pl.pallas_call(
        body, name="mix_out_bwd", grid=(S // tm,),
        in_specs=[row, row, half, half, pl.BlockSpec((D, D), lambda i: (0, 0)), pl.BlockSpec((8, D), lambda i: (0, 0))],
        out_specs=[half, half, pl.BlockSpec((D, D), lambda i: (0, 0)), pl.BlockSpec((8, D), lambda i: (0, 0))],
        out_shape=[jax.ShapeDtypeStruct((S, 512), F32), jax.ShapeDtypeStruct((S, 512), F32),
                   jax.ShapeDtypeStruct((D, D), F32), jax.ShapeDtypeStruct((8, D), F32)],
        compiler_params=_params(("arbitrary",)),
    )(dxo, mo, oa, ob, w_o, vecs)


def _mix_in_bwd(h, w_inT, dq, dk, dv, dql, dkl, dkr, *, tm=256):
    S = h.shape[0]
    offs = (0, 512, 640, 768, 1024, 1152)
    wid = (512, 128, 128, 256, 128, 128)

    def body(h_ref, w_ref, dq_ref, dk_ref, dv_ref, dql_ref, dkl_ref, dkr_ref, dh_ref, gw_ref):
        @pl.when(pl.program_id(0) == 0)
        def _():
            gw_ref[...] = jnp.zeros_like(gw_ref)

        hv = h_ref[...]
        dh = jnp.zeros((tm, D), F32)
        for ref, o, w in zip((dq_ref, dk_ref, dv_ref, dql_ref, dkl_ref, dkr_ref), offs, wid):
            w = min(w, D_IN - o)
            dpart = ref[...][:, :w].astype(BF16)
            dh += _dot(dpart, w_ref[o:o + w, :])
            gw_ref[o:o + w, :] += _dot_tn(dpart, hv)
        dh_ref[...] = dh

    row = pl.BlockSpec((tm, D), lambda i: (i, 0))
    part = lambda w: pl.BlockSpec((tm, w), lambda i: (i, 0))
    return pl.pallas_call(
        body, name="mix_in_bwd", grid=(S // tm,),
        in_specs=[row, pl.BlockSpec((D_IN_PAD, D), lambda i: (0, 0))] + [part(w) for w in wid],
        out_specs=[row, pl.BlockSpec((D_IN, D), lambda i: (0, 0))],
        out_shape=[jax.ShapeDtypeStruct((S, D), F32), jax.ShapeDtypeStruct((D_IN, D), F32)],
        compiler_params=_params(("arbitrary",)),
    )(h, w_inT, dq, dk, dv, dql, dkl, dkr)


def _vecs(norm_w, mod9, k):
    return jnp.concatenate([norm_w.reshape(1, D), mod9[3 * k:3 * k + 3], jnp.zeros((4, D), F32)], axis=0)


def _uq_group_rows(wuqT):
    per = MLA_NOPE + MLA_ROPE
    nope = [wuqT[per * h:per * h + MLA_NOPE] for h in range(MLA_HEADS)]
    rope = [wuqT[per * h + MLA_NOPE:per * (h + 1)] for h in range(MLA_HEADS)]
    return jnp.concatenate(nope + rope, axis=0)


def _uq_ungroup_rows(g):
    parts = []
    for h in range(MLA_HEADS):
        parts += [g[MLA_NOPE * h:MLA_NOPE * (h + 1)], g[512 + MLA_ROPE * h:512 + MLA_ROPE * (h + 1)]]
    return jnp.concatenate(parts, axis=0)


def _local_step(x, tgt, mod9, norms, sinks, rel_bias, q_norm, kv_norm, W, on_grads=None):
    if on_grads is None:
        on_grads = lambda group, grads, after, vecs: vecs
    S = x.shape[0]
    v1 = _vecs(norms["ffn1"], mod9, 0)
    v2 = _vecs(norms["mix"], mod9, 1)
    v3 = _vecs(norms["ffn2"], mod9, 2)
    bucket = jnp.asarray(_bucket_table())
    cos, sin = _rope_tables(S)
    if isinstance(W, dict):
        full, W = W, (lambda group, after, vecs: (full, vecs))

    W1, v1 = W("ffn1", [], v1)
    x1, h1, a1, b1, f1 = _ffn_fwd(x, v1, W1["g1T"], W1["u1T"], W1["d1"], name="ffn1_fwd", tm=512, tf=1408)
    W2, v2 = W("mixer", [x1], v2)
    w_inT = jnp.pad(W2["w_inT"], ((0, D_IN_PAD - D_IN), (0, 0))).astype(BF16)
    wuqT = _uq_group_rows(W2["w_uqT"])
    h2, proj = _mix_in_fwd(x1, v2, w_inT)
    bias = _bias_build(rel_bias, bucket)
    oa = _swa_fwd(proj, bias, sinks)
    qc, kc, vv = _mla_pre_fwd(proj, q_norm, kv_norm, wuqT, W2["w_ukv"], cos, sin)
    ob, lse = _mla_attn_fwd(qc, kc, vv)
    _, v2o = W("ffn2_on_its_way", [ob], v2)
    x2, mo = _mix_out_fwd(x1, oa, ob, W2["w_o"], v2o)
    W3, v3 = W("ffn2", [x2], v3)
    x3, h3, a3, b3, f3 = _ffn_fwd(x2, v3, W3["g3T"], W3["u3T"], W3["d3"], name="ffn2_fwd", tm=512, tf=1408)
    dx3, head_part, df3 = _head(x3, tgt, norms["final"], f3, v3)

    gg3, gu3, gd3, dh3 = _ffn_bwd_main(h3, df3, a3, b3, W3["g3T"], W3["u3T"], W3["d3"], name="ffn2_bwd", tm=2048, tf=256)
    ffn2 = {"g3T": gg3, "u3T": gu3, "d3": gd3}
    v3 = on_grads("ffn2", ffn2, [], v3)
    dx2, n3_part = _norm_bwd(dh3, x2, dx3, v3, name="ffn2_norm_bwd")
    v2 = on_grads("ffn2", None, [dx2], v2)
    doa, dob, g_wo, g2_part = _mix_out_bwd(dx2, mo, oa, ob, W2["w_o"], v2)
    dq, dk, dv, drb, dsk = _swa_bwd(proj, bias, sinks, oa, doa, bucket)
    dqc, dkc, dvv = _mla_attn_bwd(qc, kc, vv, ob, lse, dob)
    dql, dkl, dkr, g_uq, g_ukv, mla_part = _mla_pre_bwd(proj, q_norm, kv_norm, wuqT, W2["w_ukv"], cos, sin, dqc, dkc, dvv)
    dh2, g_win = _mix_in_bwd(h2, w_inT, dq, dk, dv, dql, dkl, dkr)
    mixer = {"w_inT": g_win, "w_uqT": _uq_ungroup_rows(g_uq).astype(BF16),
             "w_ukv": g_ukv.astype(BF16), "w_o": g_wo.astype(BF16)}
    v2 = on_grads("mixer", mixer, [], v2)
    dx1, n2_part, df1 = _norm_bwd(dh2, x1, dx2, v2, name="mix_norm_bwd", below=(f1, v1))
    started = on_grads("mixer", None, [dx1], jnp.zeros((1, 1), F32))
    gg1, gu1, gd1, dh1 = _ffn_bwd_main(h1, df1, a1, b1, W1["g1T"], W1["u1T"], W1["d1"], name="ffn1_bwd",
                                       after=[started], tm=2048, tf=256)
    ffn1 = {"g1T": gg1, "u1T": gu1, "d1": gd1}
    v1 = on_grads("ffn1", ffn1, [], v1)
    dx0, n1_part = _norm_bwd(dh1, x, dx1, v1, name="ffn1_norm_bwd")

    grads = {**ffn1, **ffn2, **mixer}
    dmod9 = jnp.concatenate([n1_part[1:3], n2_part[3:4], n2_part[1:3], g2_part[0:1],
                             n3_part[1:3], head_part[3:4]], axis=0)
    small = jnp.concatenate([n1_part[0], n2_part[0], n3_part[0], head_part[0], mla_part[0],
                             mla_part[1, :128], dsk[:, 0], head_part[1, 0:1], jnp.zeros((119,), F32), drb[:, 0]])
    return head_part[1, 0], dx0, grads, small, dmod9


SMALL_LAYOUT = (("norm_ffn1", 1024), ("norm_mix", 1024), ("norm_ffn2", 1024), ("norm_final", 1024),
                ("q_norm", 256), ("kv_norm", 128), ("sinks", 128), ("rel_bias", 256))
N_SMALL = sum(n for _, n in SMALL_LAYOUT)
LOSS_SLOT = 4 * 1024 + 256 + 128 + SWA_HEADS


def _coords():
    return lax.axis_index("x"), lax.axis_index("y"), lax.axis_index("c")


def _flip(v, bit):
    return 1 - v if bit else v


def _peer(r):
    x, y, c = _coords()
    return (_flip(x, r & 4), _flip(y, r & 2), _flip(c, r & 1))


def _mod_fwd(c_tile, w_mod, b_mod3):
    W = w_mod.shape[1]

    def body(c_ref, w_ref, b_ref, mod_ref, ca_ref, call_ref, part_ref, send_sems, recv_sems):
        x, y, c = _coords()
        me = 4 * x + 2 * y + c
        call_ref[me] = c_ref[...]
        sends = []
        for r in range(1, N_DEV):
            cp = pltpu.make_async_remote_copy(c_ref, call_ref.at[me], send_sems.at[0, r], recv_sems.at[0, r],
                                              device_id=_peer(r), device_id_type=MESH)
            cp.start()
            sends.append(cp)
        for r in range(1, N_DEV):
            pltpu.make_async_remote_copy(c_ref, call_ref.at[me], send_sems.at[0, r], recv_sems.at[0, r],
                                         device_id=_peer(r), device_id_type=MESH).wait_recv()
        cv = call_ref[...].reshape(8 * N_DEV, D)
        ca = (cv * _sigmoid(cv)).astype(BF16)
        ca_ref[...] = ca
        part_ref[...] = _dot(ca, w_ref[...].astype(BF16)).reshape(N_DEV, 8, W)
        mod_ref[me] = part_ref[me] + b_ref[me]
        for r in range(1, N_DEV):
            cp = pltpu.make_async_remote_copy(part_ref.at[me ^ r], mod_ref.at[me], send_sems.at[1, r],
                                              recv_sems.at[1, r], device_id=_peer(r), device_id_type=MESH)
            cp.start()
            sends.append(cp)
        for r in range(1, N_DEV):
            pltpu.make_async_remote_copy(part_ref.at[me ^ r], mod_ref.at[me], send_sems.at[1, r],
                                         recv_sems.at[1, r], device_id=_peer(r), device_id_type=MESH).wait_recv()
            mod_ref[me ^ r] = mod_ref[me ^ r] + b_ref[me ^ r]
        for cp in sends:
            cp.wait_send()

    vm = pl.BlockSpec(memory_space=pltpu.VMEM)
    return pl.pallas_call(
        body, name="mod_fwd", in_specs=[vm, vm, vm], out_specs=[vm, vm],
        out_shape=[jax.ShapeDtypeStruct((N_DEV, 8, W), F32), jax.ShapeDtypeStruct((8 * N_DEV, D), BF16)],
        scratch_shapes=[pltpu.VMEM((N_DEV, 8, D), F32), pltpu.VMEM((N_DEV, 8, W), F32),
                        pltpu.SemaphoreType.DMA((2, N_DEV)), pltpu.SemaphoreType.DMA((2, N_DEV))],
        compiler_params=_params(),
    )(c_tile, w_mod, b_mod3)


N_MODVEC = N_MOD * D
N_VEC = N_MODVEC + N_SMALL


def _mod_bwd(allvec, ca, me_idx):
    W = N_MODVEC // N_DEV

    def body(me_ref, all_ref, cols_ref, ca_ref, gw_ref, sum_ref):
        in_first_row = lax.broadcasted_iota(jnp.int32, (N_DEV, 8, W), 1) == 0
        dm = jnp.where(in_first_row, cols_ref[...], 0.0).reshape(8 * N_DEV, W)
        gw_ref[...] = _dot_tn(ca_ref[...], dm.astype(BF16))
        total = all_ref[0]
        for k in range(1, N_DEV):
            total = total + all_ref[k]
        sum_ref[...] = total

    return pl.pallas_call(
        body, name="mod_bwd",
        grid_spec=pltpu.PrefetchScalarGridSpec(
            num_scalar_prefetch=1, grid=(1,),
            in_specs=[pl.BlockSpec((N_DEV, 1, N_VEC), lambda i, me: (0, 0, 0)),
                      pl.BlockSpec((N_DEV, 1, W), lambda i, me: (0, 0, me[0])),
                      pl.BlockSpec((8 * N_DEV, D), lambda i, me: (0, 0))],
            out_specs=[pl.BlockSpec((D, W), lambda i, me: (0, 0)), pl.BlockSpec((1, N_VEC), lambda i, me: (0, 0))]),
        out_shape=[jax.ShapeDtypeStruct((D, W), F32), jax.ShapeDtypeStruct((1, N_VEC), F32)],
        compiler_params=_params(("arbitrary",)),
    )(me_idx, allvec, allvec, ca)


def _wgather(shards):
    n = len(shards)

    def body(*refs):
        ins, outs, token = refs[:n], refs[n:2 * n], refs[2 * n]
        send_sems, recv_sems, local_sems = refs[2 * n + 1:]
        token[...] = jnp.zeros_like(token)
        x, y, c = _coords()
        me = 4 * x + 2 * y + c
        sib = (x, y, 1 - c)
        chips = [(1 - x, y), (x, 1 - y), (1 - x, 1 - y)]

        def copy(k, slot, block, to, src=None):
            return pltpu.make_async_remote_copy(
                src_ref=outs[k].at[block] if src is None else src, dst_ref=outs[k].at[block],
                send_sem=send_sems.at[k, slot], recv_sem=recv_sems.at[k, slot], device_id=to, device_id_type=MESH)

        local = [pltpu.make_async_copy(ins[k], outs[k].at[me], local_sems.at[k]) for k in range(n)]
        for cp in local:
            cp.start()
        first = []
        for k in range(n):
            first.append(copy(k, 0, me, sib, src=ins[k]))
            for j, chip in enumerate(chips):
                first.append(copy(k, 1 + j, me, (*chip, c), src=ins[k]))
        for cp in first:
            cp.start()
        passed = []
        for j, (cx, cy) in enumerate(chips):
            for k in range(n):
                blk = 4 * cx + 2 * cy + c
                copy(k, 1 + j, blk, sib).wait_recv()
                cp = copy(k, 4 + j, blk, sib)
                cp.start()
                passed.append(cp)
        for k in range(n):
            copy(k, 0, 4 * x + 2 * y + (1 - c), sib).wait_recv()
            for j, (cx, cy) in enumerate(chips):
                copy(k, 4 + j, 4 * cx + 2 * cy + (1 - c), sib).wait_recv()
        for cp in first + passed:
            cp.wait_send()
        for cp in local:
            cp.wait()

    anyspec = pl.BlockSpec(memory_space=pl.ANY)
    return pl.pallas_call(
        body, name="wgather", in_specs=[anyspec] * n,
        out_specs=[anyspec] * n + [pl.BlockSpec(memory_space=pltpu.VMEM)],
        out_shape=[jax.ShapeDtypeStruct((N_DEV,) + s.shape, s.dtype) for s in shards]
        + [jax.ShapeDtypeStruct((8, 128), F32)],
        scratch_shapes=[pltpu.SemaphoreType.DMA((n, 7)), pltpu.SemaphoreType.DMA((n, 7)),
                        pltpu.SemaphoreType.DMA((n,))],
    )(*shards)


class _GatherCopies:
    def __init__(self, lands, send_sems, recv_sems):
        x, y, c = _coords()
        me = 4 * x + 2 * y + c
        sib = (x, y, 1 - c)
        chips = [(1 - x, y), (x, 1 - y), (1 - x, 1 - y)]

        def copy(k, slot, block, to):
            return pltpu.make_async_remote_copy(
                src_ref=lands[k].at[block], dst_ref=lands[k].at[block],
                send_sem=send_sems.at[7 * k + slot], recv_sem=recv_sems.at[7 * k + slot],
                device_id=to, device_id_type=MESH)

        n = len(lands)
        self.first = [copy(k, 0, me, sib) for k in range(n)]
        self.first += [copy(k, 1 + j, me, (cx, cy, c)) for j, (cx, cy) in enumerate(chips) for k in range(n)]
        self.landed = [copy(k, 1 + j, 4 * cx + 2 * cy + c, sib) for j, (cx, cy) in enumerate(chips) for k in range(n)]
        self.passed = [copy(k, 4 + j, 4 * cx + 2 * cy + c, sib) for j, (cx, cy) in enumerate(chips) for k in range(n)]
        self.from_sib = [copy(k, 0, 4 * x + 2 * y + (1 - c), sib) for k in range(n)]
        self.from_sib += [copy(k, 4 + j, 4 * cx + 2 * cy + (1 - c), sib) for j, (cx, cy) in enumerate(chips)
                          for k in range(n)]


def _gather_start(lands, *, name):
    n = len(lands)

    def body(*refs):
        for cp in _GatherCopies(refs[:n], refs[n], refs[n + 1]).first:
            cp.start()
        refs[-1][...] = jnp.zeros_like(refs[-1])

    out = pl.pallas_call(
        body, name=name,
        out_shape=(pltpu.SemaphoreType.DMA((7 * n,)), pltpu.SemaphoreType.DMA((7 * n,)),
                   *[pltpu.HBM(l.shape, l.dtype) for l in lands], jax.ShapeDtypeStruct((8, 128), F32)),
        in_specs=[HBM_SPEC] * n,
        out_specs=(SEM_SPEC, SEM_SPEC, *[HBM_SPEC] * n, pl.BlockSpec(memory_space=pltpu.VMEM)),
        input_output_aliases={i: 2 + i for i in range(n)},
        compiler_params=pltpu.CompilerParams(has_side_effects=DATAFLOW),
    )(*[_in_hbm(l) for l in lands])
    return out[0], out[1], list(out[2:2 + n]), out[-1]


def _gather_pass(send_sems, recv_sems, lands, after, *, name, stage):
    n = len(lands)

    def body(*refs):
        cps = _GatherCopies(refs[:n], refs[n], refs[n + 1])
        if stage == "landed":
            for cp in cps.landed:
                cp.wait_recv()
        else:
            for cp in cps.passed:
                cp.start()
        refs[-1][...] = jnp.zeros_like(refs[-1])

    out = pl.pallas_call(
        body, name=name,
        out_shape=(*[pltpu.HBM(l.shape, l.dtype) for l in lands], jax.ShapeDtypeStruct((8, 128), F32)),
        in_specs=[HBM_SPEC] * n + [SEM_SPEC, SEM_SPEC] + [pl.BlockSpec(memory_space=pl.ANY)] * len(after),
        out_specs=(*[HBM_SPEC] * n, pl.BlockSpec(memory_space=pltpu.VMEM)),
        input_output_aliases={i: i for i in range(n)},
        compiler_params=pltpu.CompilerParams(has_side_effects=DATAFLOW),
    )(*lands, send_sems, recv_sems, *after)
    return list(out[:n]), out[-1]


def _gather_end(send_sems, recv_sems, lands, after, *, name):
    n = len(lands)

    def body(*refs):
        cps = _GatherCopies(refs[:n], refs[n], refs[n + 1])
        for cp in cps.from_sib:
            cp.wait_recv()
        for cp in cps.first + cps.passed:
            cp.wait_send()

    out = pl.pallas_call(
        body, name=name,
        out_shape=[pltpu.HBM(l.shape, l.dtype) for l in lands],
        in_specs=[HBM_SPEC] * n + [SEM_SPEC, SEM_SPEC] + [pl.BlockSpec(memory_space=pl.ANY)] * len(after),
        out_specs=[HBM_SPEC] * n,
        input_output_aliases={i: i for i in range(n)},
        compiler_params=pltpu.CompilerParams(has_side_effects=DATAFLOW),
    )(*lands, send_sems, recv_sems, *after)
    return list(out)


def _d2d_copies(grads, lands, send_sems, recv_sems):
    x, y, c = _coords()
    return [pltpu.make_async_remote_copy(
        src_ref=grads[k].at[2 * q + (1 - c)], dst_ref=lands[k].at[q],
        send_sem=send_sems.at[4 * k + q], recv_sem=recv_sems.at[4 * k + q],
        device_id=(x, y, 1 - c), device_id_type=MESH) for k in range(len(grads)) for q in range(4)]


def _vec_copies(srcs, lands, send_sems, recv_sems):
    x, y, c = _coords()
    me = 4 * x + 2 * y + c
    return [pltpu.make_async_remote_copy(
        src_ref=lands[0].at[me], dst_ref=lands[0].at[me], send_sem=send_sems.at[r - 1], recv_sem=recv_sems.at[r - 1],
        device_id=_peer(r), device_id_type=MESH) for r in range(1, N_DEV)]


def _chipsum(g, sib, cidx, *, name):
    _, r, cc = g.shape

    def body(c_ref, g_ref, s_ref, o_ref):
        o_ref[...] = (g_ref[...].astype(F32) + s_ref[...].astype(F32)).astype(o_ref.dtype)

    return pl.pallas_call(
        body, name=name,
        grid_spec=pltpu.PrefetchScalarGridSpec(
            num_scalar_prefetch=1, grid=(4,),
            in_specs=[pl.BlockSpec((1, r, cc), lambda q, c_ref: (2 * q + c_ref[0], 0, 0)),
                      pl.BlockSpec((1, r, cc), lambda q, c_ref: (q, 0, 0))],
            out_specs=pl.BlockSpec((1, r, cc), lambda q, c_ref: (q, 0, 0))),
        out_shape=jax.ShapeDtypeStruct((4, r, cc), g.dtype),
        compiler_params=_params(("arbitrary",)),
    )(cidx, g, sib)


HBM_SPEC = pl.BlockSpec(memory_space=pltpu.HBM)
SEM_SPEC = pl.BlockSpec(memory_space=pltpu.SEMAPHORE)
DATAFLOW = pltpu.SideEffectType.DATAFLOW_SIDE_EFFECTING


def _in_hbm(a):
    return pltpu.with_memory_space_constraint(a, pltpu.HBM)


def _ici_copies(sums, lands, send_sems, recv_sems):
    x, y, c = _coords()
    chips = [(1 - x, y), (x, 1 - y), (1 - x, 1 - y)]
    cps = []
    for k in range(len(sums)):
        for j, (cx, cy) in enumerate(chips):
            cps.append(pltpu.make_async_remote_copy(
                src_ref=sums[k].at[2 * cx + cy], dst_ref=lands[k].at[j],
                send_sem=send_sems.at[3 * k + j], recv_sem=recv_sems.at[3 * k + j],
                device_id=(cx, cy, c), device_id_type=MESH))
    return cps


def _split_start(copies, srcs, lands, n_sems, after, *, name):
    ns, nl = len(srcs), len(lands)

    def body(*refs):
        for cp in copies(refs[:ns], refs[ns:ns + nl], refs[ns + nl + len(after)], refs[ns + nl + len(after) + 1]):
            cp.start()
        refs[-1][...] = jnp.zeros_like(refs[-1])

    bufs = [_in_hbm(a) for a in list(srcs) + list(lands)]
    out = pl.pallas_call(
        body, name=name,
        out_shape=(pltpu.SemaphoreType.DMA((n_sems,)), pltpu.SemaphoreType.DMA((n_sems,)),
                   *[pltpu.HBM(a.shape, a.dtype) for a in bufs], jax.ShapeDtypeStruct((8, 128), F32)),
        in_specs=[HBM_SPEC] * len(bufs) + [pl.BlockSpec(memory_space=pl.ANY)] * len(after),
        out_specs=(SEM_SPEC, SEM_SPEC, *[HBM_SPEC] * len(bufs), pl.BlockSpec(memory_space=pltpu.VMEM)),
        input_output_aliases={i: 2 + i for i in range(len(bufs))},
        compiler_params=pltpu.CompilerParams(has_side_effects=DATAFLOW),
    )(*bufs, *after)
    return out[0], out[1], list(out[2:2 + ns]), list(out[2 + ns:2 + ns + nl]), out[-1]


def _split_wait(copies, send_sems, recv_sems, srcs, lands, after, *, name):
    ns, nl = len(srcs), len(lands)

    def body(*refs):
        for cp in copies(refs[:ns], refs[ns:ns + nl], refs[ns + nl], refs[ns + nl + 1]):
            cp.wait_send()
            cp.wait_recv()

    out = pl.pallas_call(
        body, name=name,
        out_shape=[pltpu.HBM(a.shape, a.dtype) for a in list(srcs) + list(lands)],
        in_specs=[HBM_SPEC] * (ns + nl) + [SEM_SPEC, SEM_SPEC] + [pl.BlockSpec(memory_space=pl.ANY)] * len(after),
        out_specs=[HBM_SPEC] * (ns + nl),
        input_output_aliases={i: i for i in range(ns + nl)},
        compiler_params=pltpu.CompilerParams(has_side_effects=DATAFLOW),
    )(*srcs, *lands, send_sems, recv_sems, *after)
    return list(out[:ns]), list(out[ns:])


ADAM_C1 = 1.0 / (1.0 - ADAM_B1 ** ADAM_STEP)
ADAM_C2 = 1.0 / (1.0 - ADAM_B2 ** ADAM_STEP)


def _adam_math(w, g, m, v):
    m2 = ADAM_B1 * m + (1.0 - ADAM_B1) * g
    v2 = ADAM_B2 * v + (1.0 - ADAM_B2) * (g * g)
    return -ADAM_LR * ((m2 * ADAM_C1) / (jnp.sqrt(v2 * ADAM_C2) + ADAM_EPS) + ADAM_WD * w), m2, v2


def _adamw(w, g, m, v, *, name):
    R, C = w.shape
    tr = R if R <= 512 else 256

    def body(w_ref, g_ref, m_ref, v_ref, d_ref, nm_ref, nv_ref):
        d_ref[...], nm_ref[...], nv_ref[...] = _adam_math(w_ref[...], g_ref[...], m_ref[...], v_ref[...])

    blk = pl.BlockSpec((tr, C), lambda i: (i, 0))
    return pl.pallas_call(
        body, name=name, grid=(R // tr,), in_specs=[blk] * 4, out_specs=[blk] * 3,
        out_shape=[jax.ShapeDtypeStruct((R, C), F32)] * 3,
        compiler_params=_params(("parallel",)),
    )(w, g, m, v)


def _adamw_rs(w, m, v, cs, rcv, qidx, *, name):
    r, cc = w.shape
    tr = r // 2 if r % 32 == 0 and r > 128 else r

    def body(q_ref, w_ref, m_ref, v_ref, c_ref, r_ref, g_ref, d_ref, nm_ref, nv_ref):
        g = ((c_ref[0].astype(F32) + r_ref[0].astype(F32)) + r_ref[1].astype(F32)) + r_ref[2].astype(F32)
        g_ref[...] = g
        d_ref[...], nm_ref[...], nv_ref[...] = _adam_math(w_ref[...], g, m_ref[...], v_ref[...])

    blk = pl.BlockSpec((tr, cc), lambda i, q_ref: (i, 0))
    return pl.pallas_call(
        body, name=name,
        grid_spec=pltpu.PrefetchScalarGridSpec(
            num_scalar_prefetch=1, grid=(r // tr,),
            in_specs=[blk, blk, blk, pl.BlockSpec((1, tr, cc), lambda i, q_ref: (q_ref[0], i, 0)),
                      pl.BlockSpec((3, tr, cc), lambda i, q_ref: (0, i, 0))],
            out_specs=[blk] * 4),
        out_shape=[jax.ShapeDtypeStruct((r, cc), F32)] * 4,
        compiler_params=_params(("arbitrary",)),
    )(qidx, w, m, v, cs, rcv)


SMALL_PARAMS = ("norm_ffn1", "norm_mix", "norm_ffn2", "norm_final", "q_norm", "kv_norm", "sinks", "rel_bias", "b_mod")


def _adamw_small(gvec, wmv):
    widths = [wmv[3 * i].shape[1] for i in range(len(SMALL_PARAMS))]

    def body(*refs):
        g_all = refs[0]
        ins = refs[1:1 + 3 * len(SMALL_PARAMS)]
        outs = refs[1 + 3 * len(SMALL_PARAMS):]
        off = N_MODVEC
        for i, name in enumerate(SMALL_PARAMS):
            g_ref, d_ref, nm_ref, nv_ref = outs[4 * i:4 * i + 4]
            w_ref, m_ref, v_ref = ins[3 * i:3 * i + 3]
            start = 0 if name == "b_mod" else off
            g = g_all[:, start:start + widths[i]]
            g_ref[...] = g
            d_ref[...], nm_ref[...], nv_ref[...] = _adam_math(w_ref[...], g, m_ref[...], v_ref[...])
            if name != "b_mod":
                off += dict(SMALL_LAYOUT)[name]

    vm = pl.BlockSpec(memory_space=pltpu.VMEM)
    n_out = 4 * len(SMALL_PARAMS)
    out = pl.pallas_call(
        body, name="adamw_small", in_specs=[vm] * (1 + len(wmv)), out_specs=[vm] * n_out,
        out_shape=[jax.ShapeDtypeStruct((1, widths[i // 4]), F32) for i in range(n_out)],
        compiler_params=_params(),
    )(gvec, *wmv)
    return {name: out[4 * i:4 * i + 4] for i, name in enumerate(SMALL_PARAMS)}


TRANSPOSED = ("g1T", "u1T", "g3T", "u3T", "w_inT", "w_uqT")


def kernel(x, c, w_mod, b_mod, norm_ffn1, ffn1_gate, ffn1_up, ffn1_down, norm_mix, w_in, q_norm, kv_norm, w_uq, w_ukv, sinks, w_o, norm_ffn2, ffn2_gate, ffn2_up, ffn2_down, rel_bias, norm_final, loss_target, m_w_mod, m_b_mod, m_norm_ffn1, m_ffn1_gate, m_ffn1_up, m_ffn1_down, m_norm_mix, m_w_in, m_q_norm, m_kv_norm, m_w_uq, m_w_ukv, m_sinks, m_w_o, m_norm_ffn2, m_ffn2_gate, m_ffn2_up, m_ffn2_down, m_rel_bias, m_norm_final, v_w_mod, v_b_mod, v_norm_ffn1, v_ffn1_gate, v_ffn1_up, v_ffn1_down, v_norm_mix, v_w_in, v_q_norm, v_kv_norm, v_w_uq, v_w_ukv, v_sinks, v_w_o, v_norm_ffn2, v_ffn2_gate, v_ffn2_up, v_ffn2_down, v_rel_bias, v_norm_final):
    mx, my, mc = _coords()
    cidx = jnp.reshape(mc, (1,)).astype(jnp.int32)
    qidx = jnp.reshape(2 * mx + my, (1,)).astype(jnp.int32)
    WM = w_mod.shape[2]

    c_tile = jnp.pad(c, ((0, 7), (0, 0)))
    b_mod3 = jnp.pad(b_mod.reshape(N_DEV, 1, WM), ((0, 0), (0, 7), (0, 0)))
    mod3, ca = _mod_fwd(c_tile, w_mod[0], b_mod3)
    mod9 = mod3[:, 0, :].reshape(N_MOD, D)

    shards = {"g1T": ffn1_gate[0].T.astype(BF16), "u1T": ffn1_up[0].T.astype(BF16), "d1": ffn1_down[0].astype(BF16),
              "g3T": ffn2_gate[0].T.astype(BF16), "u3T": ffn2_up[0].T.astype(BF16), "d3": ffn2_down[0].astype(BF16),
              "w_inT": w_in[0].T, "w_uqT": w_uq[0].T.astype(BF16), "w_ukv": w_ukv[0].astype(BF16),
              "w_o": w_o[0].astype(BF16)}
    me = 4 * mx + 2 * my + mc
    groups = {"ffn1": ("g1T", "u1T", "d1"), "mixer": ("w_inT", "w_uqT", "w_ukv", "w_o"), "ffn2": ("g3T", "u3T", "d3")}
    arriving = {}

    def as_weights(group, gathered):
        return {k: g if k == "w_ukv" else g.reshape(N_DEV * g.shape[1], g.shape[2])
                for k, g in zip(groups[group], gathered)}

    def start_gather(group, token):
        lands = []
        for k in groups[group]:
            sh = shards[k] + token[0, 0].astype(shards[k].dtype)
            lands.append(lax.dynamic_update_slice(lax.empty((N_DEV,) + sh.shape, sh.dtype), sh[None], (me, 0, 0)))
        send, recv, lands, started = _gather_start(lands, name="gather_start_" + group)
        arriving[group] = (send, recv, lands)
        return started

    def fetch(group, after, vecs):
        if group == "ffn1":
            *gathered, token = _wgather([shards[k] + ca[1, 0].astype(shards[k].dtype) for k in groups["ffn1"]])
            token = start_gather("ffn2", start_gather("mixer", token))
            return as_weights("ffn1", gathered), vecs + token[0:1, 0:1]
        def pass_on(group, after):
            send, recv, lands = arriving[group]
            lands, token = _gather_pass(send, recv, lands, after, name="gather_landed_" + group, stage="landed")
            lands, token = _gather_pass(send, recv, lands, [token], name="gather_onward_" + group, stage="onward")
            arriving[group] = (send, recv, lands)
            return token

        if group == "ffn2_on_its_way":
            return None, vecs + pass_on("ffn2", after)[0:1, 0:1]
        if group == "mixer":
            after = [pass_on("mixer", after)]
        send, recv, lands = arriving[group]
        return as_weights(group, _gather_end(send, recv, lands, after, name="gather_end_" + group)), vecs

    norms ={"ffn1": norm_ffn1, "mix": norm_mix, "ffn2": norm_ffn2, "final": norm_final.reshape(1, D)}
    in_flight = {}

    def on_grads(group, g, after, vecs, before_ici=()):
        if g is not None:
            names = list(g)
            by_dest = [g[k] if k == "w_ukv" else g[k].reshape((N_DEV, g[k].shape[0] // N_DEV) + g[k].shape[1:])
                       for k in names]
            lands = [lax.empty((4,) + a.shape[1:], a.dtype) for a in by_dest]
            send, recv, by_dest, lands, token = _split_start(_d2d_copies, by_dest, lands, 4 * len(names), after,
                                                             name="rs_d2d_start_" + group)
            in_flight[group] = (names, send, recv, by_dest, lands)
            return vecs + token[0:1, 0:1]
        names, send, recv, by_dest, lands = in_flight[group]
        by_dest, from_sib = _split_wait(_d2d_copies, send, recv, by_dest, lands, after, name="rs_d2d_wait_" + group)
        sums = [_chipsum(a, s, cidx, name="chipsum_" + k) for k, a, s in zip(names, by_dest, from_sib)]
        lands = [lax.empty((3,) + s.shape[1:], s.dtype) for s in sums]
        send, recv, sums, lands, token = _split_start(_ici_copies, sums, lands, 3 * len(names), list(before_ici),
                                                      name="rs_ici_start_" + group)
        in_flight[group] = (names, send, recv, sums, lands, token)
        return vecs + token[0:1, 0:1]

    _, grad_x, _, small, dmod9 = _local_step(
        x[0], loss_target[0], mod9, norms, sinks, rel_bias, q_norm, kv_norm, fetch, on_grads=on_grads)

    vec = jnp.concatenate([dmod9.reshape(N_MODVEC), small]).reshape(1, 1, N_VEC)
    allvec = lax.dynamic_update_slice(lax.empty((N_DEV, 1, N_VEC), F32), vec, (me, 0, 0))
    vsend, vrecv, _, (allvec,), vec_started = _split_start(_vec_copies, [], [allvec], N_DEV - 1, [], name="vec_start")
    on_grads("ffn1", None, [grad_x], jnp.zeros((1, 1), F32), before_ici=[vec_started])

    owners = {"g1T": ("ffn1_gate", ffn1_gate, m_ffn1_gate, v_ffn1_gate), "u1T": ("ffn1_up", ffn1_up, m_ffn1_up, v_ffn1_up),
              "d1": ("ffn1_down", ffn1_down, m_ffn1_down, v_ffn1_down),
              "g3T": ("ffn2_gate", ffn2_gate, m_ffn2_gate, v_ffn2_gate), "u3T": ("ffn2_up", ffn2_up, m_ffn2_up, v_ffn2_up),
              "d3": ("ffn2_down", ffn2_down, m_ffn2_down, v_ffn2_down),
              "w_inT": ("w_in", w_in, m_w_in, v_w_in), "w_uqT": ("w_uq", w_uq, m_w_uq, v_w_uq),
              "w_ukv": ("w_ukv", w_ukv, m_w_ukv, v_w_ukv), "w_o": ("w_o", w_o, m_w_o, v_w_o)}
    res = {}

    def finish(group, after):
        names, send, recv, sums, lands, _ = in_flight[group]
        sums, lands = _split_wait(_ici_copies, send, recv, sums, lands, after, name="rs_ici_wait_" + group)
        for k, cs, rc in zip(names, sums, lands):
            pname, wk, mk, vk = owners[k]
            there = (lambda a: a[0].T) if k in TRANSPOSED else (lambda a: a[0])
            back = (lambda a: a.T[None]) if k in TRANSPOSED else (lambda a: a[None])
            res[pname] = tuple(back(a) for a in _adamw_rs(there(wk), there(mk), there(vk), cs, rc, qidx,
                                                          name="adamw_" + pname))

    ffn1_started = in_flight["ffn1"][5]
    finish("ffn2", [ffn1_started])
    finish("mixer", [ffn1_started])

    _, (allvec,) = _split_wait(_vec_copies, vsend, vrecv, [], [allvec], [ffn1_started], name="vec_wait")
    g_wmod, gvec = _mod_bwd(allvec, ca, jnp.reshape(me, (1,)).astype(jnp.int32))
    loss = gvec[0, N_MODVEC + LOSS_SLOT]
    res["w_mod"] = tuple(a[None] for a in (g_wmod,) + tuple(_adamw(w_mod[0], g_wmod, m_w_mod[0], v_w_mod[0],
                                                                    name="adamw_w_mod")))
    small_in = {"norm_ffn1": (norm_ffn1, m_norm_ffn1, v_norm_ffn1), "norm_mix": (norm_mix, m_norm_mix, v_norm_mix),
                "norm_ffn2": (norm_ffn2, m_norm_ffn2, v_norm_ffn2), "norm_final": (norm_final, m_norm_final, v_norm_final),
                "q_norm": (q_norm, m_q_norm, v_q_norm), "kv_norm": (kv_norm, m_kv_norm, v_kv_norm),
                "sinks": (sinks, m_sinks, v_sinks), "rel_bias": (rel_bias, m_rel_bias, v_rel_bias),
                "b_mod": (b_mod, m_b_mod, v_b_mod)}
    small_out = _adamw_small(gvec, [a.reshape(1, -1) for k in SMALL_PARAMS for a in small_in[k]])
    for k in SMALL_PARAMS:
        res[k] = tuple(a.reshape(small_in[k][0].shape) for a in small_out[k])

    finish("ffn1", [res[k][3] for k in res])

    order = ("w_mod", "b_mod", "norm_ffn1", "ffn1_gate", "ffn1_up", "ffn1_down", "norm_mix", "w_in", "q_norm",
             "kv_norm", "w_uq", "w_ukv", "sinks", "w_o", "norm_ffn2", "ffn2_gate", "ffn2_up", "ffn2_down",
             "rel_bias", "norm_final")
    return (loss, grad_x[None]) + tuple(res[nm][kind] for kind in range(4) for nm in order)
```

```python
import functools
import math

import numpy as np
import jax
import jax.numpy as jnp
from jax import lax
from jax.experimental import pallas as pl
from jax.experimental.pallas import tpu as pltpu

F32 = jnp.float32
BF16 = jnp.bfloat16
MESH = pl.DeviceIdType.MESH

N_DEV = 8
D = 1024
D_FF = 2816
EPS = 1e-6
N_MOD = 9
SWA_HEADS = 8
SWA_DH = 64
WINDOW = 128
MLA_HEADS = 4
MLA_NOPE = 128
MLA_ROPE = 64
MLA_V = 128
MLA_QR = 256
MLA_KVR = 128
ROPE_THETA = 10000.0
NUM_BUCKETS = 32
D_IN = 1216
D_IN_PAD = 1280
SWA_SCALE = SWA_DH ** -0.5
MLA_SCALE = (MLA_NOPE + MLA_ROPE) ** -0.5

ADAM_LR = 0.001
ADAM_B1 = 0.9
ADAM_B2 = 0.999
ADAM_EPS = 1e-08
ADAM_WD = 0.01
ADAM_STEP = 10

V7X_VMEM_LIMIT = 56 * 1024 * 1024

NT_DIMS = (((1,), (1,)), ((), ()))
TN_DIMS = (((0,), (0,)), ((), ()))


def _dot(a, b):
    return jnp.dot(a, b, preferred_element_type=F32)


def _dot_nt(a, b):
    return lax.dot_general(a, b, NT_DIMS, preferred_element_type=F32)


def _dot_tn(a, b):
    return lax.dot_general(a, b, TN_DIMS, preferred_element_type=F32)


def _params(sem=None):
    return pltpu.CompilerParams(dimension_semantics=sem, vmem_limit_bytes=V7X_VMEM_LIMIT)


def _rstd(x):
    return lax.rsqrt(jnp.mean(x * x, axis=-1, keepdims=True) + EPS)


def _rms_bwd(dy, xhat, r):
    return r * (dy - xhat * jnp.mean(dy * xhat, axis=-1, keepdims=True))


def _sigmoid(a):
    return 1.0 / (1.0 + jnp.exp(-a))


def _ffn_fwd(x, vecs, wgT, wuT, wd, *, name, tm=512, tf=256):
    S, F = x.shape[0], wd.shape[0]
    tm = min(tm, S)
    ni, nj = S // tm, F // tf

    def body(x_ref, vec_ref, wg_ref, wu_ref, wd_ref, xo_ref, h_ref, a_ref, b_ref, f_ref, acc_ref):
        j = pl.program_id(1)

        @pl.when(j == 0)
        def _():
            xv = x_ref[...]
            hn = xv * _rstd(xv) * vec_ref[0:1, :]
            h_ref[...] = (hn * (1.0 + vec_ref[2:3, :]) + vec_ref[1:2, :]).astype(BF16)

        h = h_ref[...]
        a = _dot_nt(h, wg_ref[...])
        b = _dot_nt(h, wu_ref[...])
        a_ref[...] = a.astype(BF16)
        b_ref[...] = b.astype(BF16)
        part = _dot((a * _sigmoid(a) * b).astype(BF16), wd_ref[...])

        def finish(f):
            f_ref[...] = f
            xo_ref[...] = x_ref[...] + (0.5 * vec_ref[3:4, :]) * f

        if nj == 1:
            finish(part)
        else:
            @pl.when(j == 0)
            def _():
                acc_ref[...] = part

            @pl.when((j > 0) & (j < nj - 1))
            def _():
                acc_ref[...] += part

            @pl.when(j == nj - 1)
            def _():
                finish(acc_ref[...] + part)

    row = pl.BlockSpec((tm, D), lambda i, j: (i, 0))
    wspec = pl.BlockSpec((tf, D), lambda i, j: (j, 0), pipeline_mode=pl.Buffered(1) if nj == 1 else None)
    act = pl.BlockSpec((tm, tf), lambda i, j: (i, j))
    return pl.pallas_call(
        body, name=name, grid=(ni, nj),
        in_specs=[row, pl.BlockSpec((8, D), lambda i, j: (0, 0)), wspec, wspec, wspec],
        out_specs=[row, row, act, act, row],
        out_shape=[jax.ShapeDtypeStruct((S, D), F32), jax.ShapeDtypeStruct((S, D), BF16),
                   jax.ShapeDtypeStruct((S, F), BF16), jax.ShapeDtypeStruct((S, F), BF16),
                   jax.ShapeDtypeStruct((S, D), F32)],
        scratch_shapes=[pltpu.VMEM((tm, D) if nj > 1 else (8, 128), F32)],
        compiler_params=_params(("parallel", "arbitrary")),
    )(x, vecs, wgT, wuT, wd)


def _ffn_bwd_main(h, df, a, b, wgT, wuT, wd, *, name, after=(), tm=512, tf=256):
    S = h.shape[0]
    tm = min(tm, S)
    ni, nj = S // tm, D_FF // tf

    def body(h_hbm, df_hbm, a_ref, b_ref, wg_ref, wu_ref, wd_ref, *rest):
        gg_ref, gu_ref, gd_ref, dh_hbm, h_v, df_v, dh_v, gg_acc, gu_acc, gd_acc, sem = rest[len(after):]
        j = pl.program_id(0)
        i = pl.program_id(1)

        @pl.when((j == 0) & (i == 0))
        def _():
            c1 = pltpu.make_async_copy(h_hbm, h_v, sem.at[0])
            c2 = pltpu.make_async_copy(df_hbm, df_v, sem.at[1])
            c1.start()
            c2.start()
            c1.wait()
            c2.wait()

        @pl.when(i == 0)
        def _():
            gg_acc[...] = jnp.zeros_like(gg_acc)
            gu_acc[...] = jnp.zeros_like(gu_acc)
            gd_acc[...] = jnp.zeros_like(gd_acc)

        rows = pl.ds(pl.multiple_of(i * tm, tm), tm)
        hi = h_v[rows, :]
        dfi = df_v[rows, :]
        av = a_ref[...].astype(F32)
        bv = b_ref[...].astype(F32)
        sg = _sigmoid(av)
        sa = av * sg
        hsw = (sa * bv).astype(BF16)
        dhsw = _dot_nt(dfi, wd_ref[...])
        da = (dhsw * bv * (sg * (1.0 + av * (1.0 - sg)))).astype(BF16)
        db = (dhsw * sa).astype(BF16)
        gd_acc[...] += _dot_tn(hsw, dfi)
        gg_acc[...] += _dot_tn(da, hi)
        gu_acc[...] += _dot_tn(db, hi)
        dh = _dot(da, wg_ref[...]) + _dot(db, wu_ref[...])

        @pl.when(j == 0)
        def _():
            dh_v[rows, :] = dh

        @pl.when(j > 0)
        def _():
            dh_v[rows, :] += dh

        @pl.when(i == ni - 1)
        def _():
            gg_ref[...] = gg_acc[...].astype(BF16)
            gu_ref[...] = gu_acc[...].astype(BF16)
            gd_ref[...] = gd_acc[...].astype(BF16)

        @pl.when((j == nj - 1) & (i == ni - 1))
        def _():
            c3 = pltpu.make_async_copy(dh_v, dh_hbm, sem.at[2])
            c3.start()
            c3.wait()

    anyspec = pl.BlockSpec(memory_space=pl.ANY)
    wspec = pl.BlockSpec((tf, D), lambda j, i: (j, 0))
    act = pl.BlockSpec((tm, tf), lambda j, i: (i, j))
    return pl.pallas_call(
        body, name=name, grid=(nj, ni),
        in_specs=[anyspec, anyspec, act, act, wspec, wspec, wspec] + [anyspec] * len(after),
        out_specs=[wspec, wspec, wspec, anyspec],
        out_shape=[jax.ShapeDtypeStruct((D_FF, D), BF16)] * 3 + [jax.ShapeDtypeStruct((S, D), F32)],
        scratch_shapes=[pltpu.VMEM((S, D), BF16), pltpu.VMEM((S, D), BF16), pltpu.VMEM((S, D), F32),
                        pltpu.VMEM((tf, D), F32), pltpu.VMEM((tf, D), F32), pltpu.VMEM((tf, D), F32),
                        pltpu.SemaphoreType.DMA((3,))],
        compiler_params=_params(("arbitrary", "arbitrary")),
    )(h, df, a, b, wgT, wuT, wd, *after)


def _ffn_out_bwd(dx, f, gate, df_ref, part_ref):
    df_ref[...] = ((0.5 * gate) * dx).astype(BF16)
    part_ref[3:4, :] += 0.5 * jnp.sum(dx * f, axis=0, keepdims=True)


def _norm_bwd(dh, x, dxo, vecs, *, name, below=None, tm=256):
    S = x.shape[0]

    def body(dh_ref, x_ref, dxo_ref, vec_ref, *rest):
        dx_ref, part_ref = rest[-2 if below is None else -3], rest[-1 if below is None else -2]

        @pl.when(pl.program_id(0) == 0)
        def _():
            part_ref[...] = jnp.zeros_like(part_ref)

        dh = dh_ref[...]
        xv = x_ref[...]
        r = _rstd(xv)
        xhat = xv * r
        w = vec_ref[0:1, :]
        xn = xhat * w
        dxn = dh * (1.0 + vec_ref[2:3, :])
        part_ref[0:1, :] += jnp.sum(dxn * xhat, axis=0, keepdims=True)
        part_ref[1:2, :] += jnp.sum(dh, axis=0, keepdims=True)
        part_ref[2:3, :] += jnp.sum(dh * xn, axis=0, keepdims=True)
        dx = dxo_ref[...] + _rms_bwd(dxn * w, xhat, r)
        dx_ref[...] = dx
        if below is not None:
            _ffn_out_bwd(dx, rest[0][...], rest[1][3:4, :], rest[-1], part_ref)

    row = pl.BlockSpec((tm, D), lambda i: (i, 0))
    vec = pl.BlockSpec((8, D), lambda i: (0, 0))
    extra = [] if below is None else [row, vec]
    return pl.pallas_call(
        body, name=name, grid=(S // tm,), in_specs=[row, row, row, vec] + extra,
        out_specs=[row, vec] + ([] if below is None else [row]),
        out_shape=[jax.ShapeDtypeStruct((S, D), F32), jax.ShapeDtypeStruct((8, D), F32)]
        + ([] if below is None else [jax.ShapeDtypeStruct((S, D), BF16)]),
        compiler_params=_params(("arbitrary",)),
    )(dh, x, dxo, vecs, *([] if below is None else below))


def _head(x, tgt, nf, f, vecs, *, tm=256):
    S = x.shape[0]

    def body(x_ref, t_ref, nf_ref, f_ref, vec_ref, dx_ref, part_ref, df_ref):
        @pl.when(pl.program_id(0) == 0)
        def _():
            part_ref[...] = jnp.zeros_like(part_ref)

        xv = x_ref[...]
        r = _rstd(xv)
        xhat = xv * r
        w = nf_ref[...]
        e = xhat * w - t_ref[...]
        dy = e * (1.0 / D)
        part_ref[0:1, :] += jnp.sum(dy * xhat, axis=0, keepdims=True)
        part_ref[1:2, :] += jnp.sum(e * e) * (0.5 / D)
        dx = _rms_bwd(dy * w, xhat, r)
        dx_ref[...] = dx
        _ffn_out_bwd(dx, f_ref[...], vec_ref[3:4, :], df_ref, part_ref)

    row = pl.BlockSpec((tm, D), lambda i: (i, 0))
    vec = pl.BlockSpec((8, D), lambda i: (0, 0))
    return pl.pallas_call(
        body, name="head", grid=(S // tm,),
        in_specs=[row, row, pl.BlockSpec((1, D), lambda i: (0, 0)), row, vec],
        out_specs=[row, vec, row],
        out_shape=[jax.ShapeDtypeStruct((S, D), F32), jax.ShapeDtypeStruct((8, D), F32),
                   jax.ShapeDtypeStruct((S, D), BF16)],
        compiler_params=_params(("arbitrary",)),
    )(x, tgt, nf, f, vecs)


def _mix_in_fwd(x, vecs, w_inT, *, tm=256):
    S = x.shape[0]

    def body(x_ref, vec_ref, w_ref, h_ref, p_ref):
        xv = x_ref[...]
        hn = xv * _rstd(xv) * vec_ref[0:1, :]
        h = (hn * (1.0 + vec_ref[2:3, :]) + vec_ref[1:2, :]).astype(BF16)
        h_ref[...] = h
        p_ref[...] = _dot_nt(h, w_ref[...])

    row = pl.BlockSpec((tm, D), lambda i: (i, 0))
    return pl.pallas_call(
        body, name="mix_in_fwd", grid=(S // tm,),
        in_specs=[row, pl.BlockSpec((8, D), lambda i: (0, 0)), pl.BlockSpec((D_IN_PAD, D), lambda i: (0, 0))],
        out_specs=[row, pl.BlockSpec((tm, D_IN_PAD), lambda i: (i, 0))],
        out_shape=[jax.ShapeDtypeStruct((S, D), BF16), jax.ShapeDtypeStruct((S, D_IN_PAD), F32)],
        compiler_params=_params(("parallel",)),
    )(x, vecs, w_inT)


def _bucket_table():
    qi = np.arange(WINDOW)[:, None]
    kj = np.arange(2 * WINDOW)[None, :]
    dist = qi + WINDOW - kj
    max_exact = NUM_BUCKETS // 2
    n = np.maximum(dist, 0)
    nf = np.maximum(n, 1).astype(np.float32)
    large = max_exact + (np.log(nf / np.float32(max_exact)) / np.float32(math.log(WINDOW / max_exact))
                         * np.float32(NUM_BUCKETS - max_exact)).astype(np.int32)
    large = np.minimum(large, NUM_BUCKETS - 1)
    return np.where(n < max_exact, n, large).astype(np.int32)


def _bias_build(rel_bias, bucket):
    def body(rb_ref, bk_ref, out_ref):
        bk = bk_ref[...]
        for h in range(SWA_HEADS):
            acc = jnp.zeros((WINDOW, 2 * WINDOW), F32)
            for b in range(NUM_BUCKETS):
                acc = jnp.where(bk == b, rb_ref[b, h], acc)
            out_ref[h] = acc

    return pl.pallas_call(
        body, name="bias_build",
        in_specs=[pl.BlockSpec(memory_space=pltpu.SMEM), pl.BlockSpec(memory_space=pltpu.VMEM)],
        out_specs=pl.BlockSpec(memory_space=pltpu.VMEM),
        out_shape=jax.ShapeDtypeStruct((SWA_HEADS, WINDOW, 2 * WINDOW), F32),
    )(rel_bias, bucket)


SWA_GROUP = 4
GROUP_ROWS = SWA_GROUP * WINDOW


def _swa_valid(n):
    row = lax.broadcasted_iota(jnp.int32, (GROUP_ROWS, 2 * WINDOW), 0) % WINDOW
    col = lax.broadcasted_iota(jnp.int32, (GROUP_ROWS, 2 * WINDOW), 1)
    dist = row + WINDOW - col
    return (dist >= 0) & (dist < WINDOW) & ((col >= WINDOW) | (n > 0))


def _stack_heads(x, g):
    return jnp.concatenate([x[:, 64 * h:64 * h + 64] for h in range(SWA_GROUP * g, SWA_GROUP * (g + 1))], axis=0)


def _unstack_heads(x4):
    return jnp.concatenate([x4[WINDOW * a:WINDOW * (a + 1)] for a in range(SWA_GROUP)], axis=1)


def _group_sinks(sink_ref, g):
    head = lax.broadcasted_iota(jnp.int32, (GROUP_ROWS, 1), 0) // WINDOW
    out = jnp.full((GROUP_ROWS, 1), sink_ref[0, SWA_GROUP * g], F32)
    for a in range(1, SWA_GROUP):
        out = jnp.where(head == a, sink_ref[0, SWA_GROUP * g + a], out)
    return out


def _swa_probs(qh, kk, bias_h, sink, valid):
    s = _dot_nt(qh, kk) * SWA_SCALE + bias_h
    s = jnp.where(valid, s, -jnp.inf)
    m = jnp.maximum(jnp.max(s, axis=-1, keepdims=True), sink)
    p = jnp.exp(s - m)
    ps = jnp.exp(sink - m)
    inv = 1.0 / (jnp.sum(p, axis=-1, keepdims=True) + ps)
    return p * inv, ps * inv


def _swa_specs():
    prev = lambda n: jnp.maximum(n - 1, 0)
    return [pl.BlockSpec((WINDOW, 512), lambda n: (n, 0)),
            pl.BlockSpec((WINDOW, 128), lambda n: (n, 4)),
            pl.BlockSpec((WINDOW, 128), lambda n: (prev(n), 4)),
            pl.BlockSpec((WINDOW, 128), lambda n: (n, 5)),
            pl.BlockSpec((WINDOW, 128), lambda n: (prev(n), 5)),
            pl.BlockSpec((SWA_HEADS, WINDOW, 2 * WINDOW), lambda n: (0, 0, 0)),
            pl.BlockSpec(memory_space=pltpu.SMEM)]


def _swa_fwd(proj, bias, sinks):
    S = proj.shape[0]

    def body(q_ref, kc_ref, kp_ref, vc_ref, vp_ref, bias_ref, sink_ref, o_ref):
        valid = _swa_valid(pl.program_id(0))
        q = q_ref[...].astype(BF16)
        kfull = jnp.concatenate([kp_ref[...], kc_ref[...]], axis=0).astype(BF16)
        vfull = jnp.concatenate([vp_ref[...], vc_ref[...]], axis=0).astype(BF16)
        for g in range(SWA_HEADS // SWA_GROUP):
            kk = kfull[:, 64 * g:64 * g + 64]
            vv = vfull[:, 64 * g:64 * g + 64]
            bias4 = bias_ref[SWA_GROUP * g:SWA_GROUP * (g + 1)].reshape(GROUP_ROWS, 2 * WINDOW)
            pk, _ = _swa_probs(_stack_heads(q, g), kk, bias4, _group_sinks(sink_ref, g), valid)
            o_ref[:, 256 * g:256 * (g + 1)] = _unstack_heads(_dot(pk.astype(BF16), vv))

    return pl.pallas_call(
        body, name="swa_fwd", grid=(S // WINDOW,),
        in_specs=_swa_specs(),
        out_specs=pl.BlockSpec((WINDOW, 512), lambda n: (n, 0)),
        out_shape=jax.ShapeDtypeStruct((S, 512), F32),
        compiler_params=_params(("parallel",)),
    )(proj, proj, proj, proj, proj, bias, sinks)


def _swa_bwd(proj, bias, sinks, o, do, bucket):
    S = proj.shape[0]
    nb = S // WINDOW

    def body(q_ref, kc_ref, kp_ref, vc_ref, vp_ref, bias_ref, sink_ref, o_ref, do_ref, bk_ref,
             dq_ref, dk_ref, dv_ref, drb_ref, dsk_ref, dbias_acc):
        n = pl.program_id(0)

        @pl.when(n == 0)
        def _():
            dk_ref[...] = jnp.zeros_like(dk_ref)
            dv_ref[...] = jnp.zeros_like(dv_ref)
            dsk_ref[...] = jnp.zeros_like(dsk_ref)
            dbias_acc[...] = jnp.zeros_like(dbias_acc)
            drb_ref[...] = jnp.zeros_like(drb_ref)

        valid = _swa_valid(n)
        q = q_ref[...].astype(BF16)
        dov = do_ref[...]
        kfull = jnp.concatenate([kp_ref[...], kc_ref[...]], axis=0).astype(BF16)
        vfull = jnp.concatenate([vp_ref[...], vc_ref[...]], axis=0).astype(BF16)
        prow = pl.ds(pl.multiple_of(jnp.maximum(n - 1, 0) * WINDOW, WINDOW), WINDOW)
        crow = pl.ds(pl.multiple_of(n * WINDOW, WINDOW), WINDOW)
        ov = o_ref[...]
        for g in range(SWA_HEADS // SWA_GROUP):
            heads = slice(SWA_GROUP * g, SWA_GROUP * (g + 1))
            kk = kfull[:, 64 * g:64 * g + 64]
            vv = vfull[:, 64 * g:64 * g + 64]
            q4 = _stack_heads(q, g)
            pk, psink = _swa_probs(q4, kk, bias_ref[heads].reshape(GROUP_ROWS, 2 * WINDOW), _group_sinks(sink_ref, g), valid)
            pkb = pk.astype(BF16)
            do4 = _stack_heads(dov, g)
            dob = do4.astype(BF16)
            dp = _dot_nt(dob, vv)
            delta = jnp.sum(do4 * _stack_heads(ov, g), axis=-1, keepdims=True)
            ds = pk * (dp - delta)
            dsink = -psink * delta
            for a in range(SWA_GROUP):
                h = SWA_GROUP * g + a
                part = jnp.sum(dsink[WINDOW * a:WINDOW * (a + 1)], keepdims=True)
                dsk_ref[h:h + 1, :] += jnp.broadcast_to(part, (1, 128))
            dbias_acc[heads] += ds.reshape(SWA_GROUP, WINDOW, 2 * WINDOW)
            dsb = (ds * SWA_SCALE).astype(BF16)
            dq_ref[:, 256 * g:256 * (g + 1)] = _unstack_heads(_dot(dsb, kk))
            dkk = _dot_tn(dsb, q4)
            dvv = _dot_tn(pkb, dob)
            dk_ref[prow, 64 * g:64 * g + 64] += dkk[:WINDOW]
            dk_ref[crow, 64 * g:64 * g + 64] += dkk[WINDOW:]
            dv_ref[prow, 64 * g:64 * g + 64] += dvv[:WINDOW]
            dv_ref[crow, 64 * g:64 * g + 64] += dvv[WINDOW:]

        @pl.when(n == nb - 1)
        def _():
            bk = bk_ref[...]
            for h in range(SWA_HEADS):
                dbh = dbias_acc[h]
                for b in range(NUM_BUCKETS):
                    val = jnp.sum(jnp.where(bk == b, dbh, 0.0), keepdims=True)
                    drb_ref[b * 8 + h:b * 8 + h + 1, :] = jnp.broadcast_to(val, (1, 128))

    full = lambda shape: pl.BlockSpec(shape, lambda n: tuple(0 for _ in shape))
    return pl.pallas_call(
        body, name="swa_bwd", grid=(nb,),
        in_specs=_swa_specs() + [pl.BlockSpec((WINDOW, 512), lambda n: (n, 0)),
                                 pl.BlockSpec((WINDOW, 512), lambda n: (n, 0)), full((WINDOW, 2 * WINDOW))],
        out_specs=[pl.BlockSpec((WINDOW, 512), lambda n: (n, 0)), full((S, 128)), full((S, 128)),
                   full((NUM_BUCKETS * 8, 128)), full((8, 128))],
        out_shape=[jax.ShapeDtypeStruct((S, 512), F32), jax.ShapeDtypeStruct((S, 128), F32),
                   jax.ShapeDtypeStruct((S, 128), F32), jax.ShapeDtypeStruct((NUM_BUCKETS * 8, 128), F32),
                   jax.ShapeDtypeStruct((8, 128), F32)],
        scratch_shapes=[pltpu.VMEM((SWA_HEADS, WINDOW, 2 * WINDOW), F32)],
        compiler_params=_params(("arbitrary",)),
    )(proj, proj, proj, proj, proj, bias, sinks, o, do, bucket)


def _rope_tables(S):
    inv = ROPE_THETA ** (-jnp.arange(0, MLA_ROPE, 2, dtype=F32) / MLA_ROPE)
    ang = jnp.arange(S, dtype=F32)[:, None] * inv[None, :]
    cos, sin = jnp.cos(ang), jnp.sin(ang)
    return jnp.tile(jnp.concatenate([cos, cos], axis=1), (1, 4)), jnp.tile(jnp.concatenate([-sin, sin], axis=1), (1, 4))


def _swap_halves(x):
    w = x.shape[-1]
    lane = lax.broadcasted_iota(jnp.int32, x.shape, x.ndim - 1)
    return jnp.where((lane % 64) < 32, pltpu.roll(x, w - 32, x.ndim - 1), pltpu.roll(x, 32, x.ndim - 1))


def _mla_pre_fwd(proj, qn_w, kvn_w, wuqT, wukv, cos, sin, *, tm=256):
    S = proj.shape[0]

    def body(ql_ref, kl_ref, kr_ref, qw_ref, kw_ref, wuq_ref, wukv_ref, cos_ref, sin_ref,
             qc_ref, kc_ref, vv_ref):
        ql = ql_ref[...]
        qn = (ql * _rstd(ql) * qw_ref[...]).astype(BF16)
        q = _dot_nt(qn, wuq_ref[...])
        cs, sn = cos_ref[...], sin_ref[...]
        qr = q[:, 512:768]
        qr = qr * cs + _swap_halves(qr) * sn
        half = lax.broadcasted_iota(jnp.int32, (tm, 128), 1) // 64
        kl = kl_ref[...]
        kvn = (kl * _rstd(kl) * kw_ref[...]).astype(BF16)
        kr = kr_ref[...]
        kr = kr * cs[:, :128] + _swap_halves(kr) * sn[:, :128]
        kr2 = (kr + pltpu.roll(kr, 64, 1)).astype(BF16)
        for h in range(MLA_HEADS):
            qc_ref[h, :, 0:128] = q[:, 128 * h:128 * h + 128].astype(BF16)
            chunk = qr[:, 128 * (h // 2):128 * (h // 2) + 128]
            qc_ref[h, :, 128:256] = jnp.where(half == (h % 2), chunk, 0.0).astype(BF16)
            kc_ref[h, :, 0:128] = _dot(kvn, wukv_ref[2 * h]).astype(BF16)
            kc_ref[h, :, 128:256] = kr2
            vv_ref[h] = _dot(kvn, wukv_ref[2 * h + 1]).astype(BF16)

    const = lambda shape: pl.BlockSpec(shape, lambda i: tuple(0 for _ in shape))
    return pl.pallas_call(
        body, name="mla_pre_fwd", grid=(S // tm,),
        in_specs=[pl.BlockSpec((tm, 256), lambda i: (i, 3)), pl.BlockSpec((tm, 128), lambda i: (i, 8)),
                  pl.BlockSpec((tm, 128), lambda i: (i, 9)), const((1, 256)), const((1, 128)),
                  const((768, 256)), const((8, 128, 128)),
                  pl.BlockSpec((tm, 256), lambda i: (i, 0)), pl.BlockSpec((tm, 256), lambda i: (i, 0))],
        out_specs=[pl.BlockSpec((MLA_HEADS, tm, 256), lambda i: (0, i, 0)),
                   pl.BlockSpec((MLA_HEADS, tm, 256), lambda i: (0, i, 0)),
                   pl.BlockSpec((MLA_HEADS, tm, 128), lambda i: (0, i, 0))],
        out_shape=[jax.ShapeDtypeStruct((MLA_HEADS, S, 256), BF16), jax.ShapeDtypeStruct((MLA_HEADS, S, 256), BF16),
                   jax.ShapeDtypeStruct((MLA_HEADS, S, 128), BF16)],
        compiler_params=_params(("parallel",)),
    )(proj, proj, proj, qn_w, kvn_w, wuqT, wukv, cos, sin)


def _causal(i, j, t):
    row = i * t + lax.broadcasted_iota(jnp.int32, (t, t), 0)
    col = j * t + lax.broadcasted_iota(jnp.int32, (t, t), 1)
    return col <= row


def _mla_attn_fwd(qc, kc, vv, *, t=256):
    S = qc.shape[1]
    t = min(t, S)

    def body(q_ref, k_ref, v_ref, o_ref, l_ref):
        i = pl.program_id(0)
        diag = _causal(0, 0, t)

        def step(j, carry, masked):
            rows = pl.ds(pl.multiple_of(j * t, t), t)
            out = []
            for h in range(MLA_HEADS):
                m, l, acc = carry[h]
                s = _dot_nt(q_ref[h], k_ref[h, rows, :]) * MLA_SCALE
                if masked:
                    s = jnp.where(diag, s, -jnp.inf)
                m_new = jnp.maximum(m, jnp.max(s, axis=-1, keepdims=True))
                alpha = jnp.exp(m - m_new)
                p = jnp.exp(s - m_new)
                l = alpha * l + jnp.sum(p, axis=-1, keepdims=True)
                acc = alpha * acc + _dot(p.astype(BF16), v_ref[h, rows, :])
                out.append((m_new, l, acc))
            return tuple(out)

        init = tuple((jnp.full((t, 1), -jnp.inf, F32), jnp.zeros((t, 1), F32), jnp.zeros((t, MLA_V), F32))
                     for _ in range(MLA_HEADS))
        carry = lax.fori_loop(0, i, lambda j, c: step(j, c, False), init)
        carry = step(i, carry, True)
        for h in range(MLA_HEADS):
            m, l, acc = carry[h]
            o_ref[:, 128 * h:128 * h + 128] = acc / l
            l_ref[h] = jnp.broadcast_to(m + jnp.log(l), (t, 128))

    return pl.pallas_call(
        body, name="mla_attn_fwd", grid=(S // t,),
        in_specs=[pl.BlockSpec((MLA_HEADS, t, 256), lambda i: (0, i, 0)),
                  pl.BlockSpec((MLA_HEADS, S, 256), lambda i: (0, 0, 0)),
                  pl.BlockSpec((MLA_HEADS, S, 128), lambda i: (0, 0, 0))],
        out_specs=[pl.BlockSpec((t, 512), lambda i: (i, 0)),
                   pl.BlockSpec((MLA_HEADS, t, 128), lambda i: (0, i, 0))],
        out_shape=[jax.ShapeDtypeStruct((S, 512), F32), jax.ShapeDtypeStruct((MLA_HEADS, S, 128), F32)],
        compiler_params=_params(("parallel",)),
    )(qc, kc, vv)


def _mla_attn_bwd(qc, kc, vv, o, lse, do, *, t=256, tq=512):
    S = qc.shape[1]
    t = min(t, S)
    tq = min(tq, S)
    nblk = S // t
    hp = MLA_HEADS
    once = pl.Buffered(1)

    def body(q_ref, k_ref, v_ref, o_ref, l_ref, do_ref, dq_ref, dk_ref, dv_ref):
        j = pl.program_id(1)

        @pl.when(j == 0)
        def _():
            dq_ref[...] = jnp.zeros_like(dq_ref)

        first = (j * t) // tq

        def step(i, carry, masked):
            rows = pl.ds(pl.multiple_of(i * tq, tq), tq)
            if masked:
                row = i * tq + lax.broadcasted_iota(jnp.int32, (tq, t), 0)
                col = j * t + lax.broadcasted_iota(jnp.int32, (tq, t), 1)
                visible = col <= row
            out = []
            for h in range(hp):
                dk, dv = carry[h]
                k = k_ref[h]
                q = q_ref[h, rows, :]
                dov = do_ref[rows, 128 * h:128 * h + 128]
                lrow = l_ref[h, rows, :][:, 0:1]
                p = jnp.exp(_dot_nt(q, k) * MLA_SCALE - lrow)
                if masked:
                    p = jnp.where(visible, p, 0.0)
                dob = dov.astype(BF16)
                dv = dv + _dot_tn(p.astype(BF16), dob)
                dp = _dot_nt(dob, v_ref[h])
                delta = jnp.sum(dov * o_ref[rows, 128 * h:128 * h + 128], axis=-1, keepdims=True)
                ds = (p * (dp - delta) * MLA_SCALE).astype(BF16)
                dk = dk + _dot_tn(ds, q)
                dq_ref[h, rows, :] += _dot(ds, k)
                out.append((dk, dv))
            return tuple(out)

        init = tuple((jnp.zeros((t, 256), F32), jnp.zeros((t, MLA_V), F32)) for _ in range(hp))
        carry = step(first, init, True)
        carry = lax.fori_loop(first + 1, S // tq, lambda i, c: step(i, c, False), carry)
        for h in range(hp):
            dk_ref[h] = carry[h][0]
            dv_ref[h] = carry[h][1]

    return pl.pallas_call(
        body, name="mla_attn_bwd", grid=(MLA_HEADS // hp, nblk),
        in_specs=[pl.BlockSpec((hp, S, 256), lambda g, j: (g, 0, 0), pipeline_mode=once),
                  pl.BlockSpec((hp, t, 256), lambda g, j: (g, j, 0)),
                  pl.BlockSpec((hp, t, 128), lambda g, j: (g, j, 0)),
                  pl.BlockSpec((S, 128 * hp), lambda g, j: (0, g), pipeline_mode=once),
                  pl.BlockSpec((hp, S, 128), lambda g, j: (g, 0, 0), pipeline_mode=once),
                  pl.BlockSpec((S, 128 * hp), lambda g, j: (0, g), pipeline_mode=once)],
        out_specs=[pl.BlockSpec((hp, S, 256), lambda g, j: (g, 0, 0)),
                   pl.BlockSpec((hp, t, 256), lambda g, j: (g, j, 0)),
                   pl.BlockSpec((hp, t, 128), lambda g, j: (g, j, 0))],
        out_shape=[jax.ShapeDtypeStruct((MLA_HEADS, S, 256), F32), jax.ShapeDtypeStruct((MLA_HEADS, S, 256), F32),
                   jax.ShapeDtypeStruct((MLA_HEADS, S, 128), F32)],
        compiler_params=_params(("parallel", "arbitrary")),
    )(qc, kc, vv, o, lse, do)


def _mla_pre_bwd(proj, qn_w, kvn_w, wuqT, wukv, cos, sin, dqc, dkc, dvv, *, tm=256):
    S = proj.shape[0]

    def body(ql_ref, kl_ref, qw_ref, kw_ref, wuq_ref, wukv_ref, cos_ref, sin_ref, dqc_ref, dkc_ref, dvv_ref,
             dql_ref, dkl_ref, dkr_ref, gq_ref, gkv_ref, part_ref):
        @pl.when(pl.program_id(0) == 0)
        def _():
            gq_ref[...] = jnp.zeros_like(gq_ref)
            gkv_ref[...] = jnp.zeros_like(gkv_ref)
            part_ref[...] = jnp.zeros_like(part_ref)

        cs, sn = cos_ref[...], sin_ref[...]
        half = lax.broadcasted_iota(jnp.int32, (tm, 128), 1) // 64
        ql = ql_ref[...]
        rq = _rstd(ql)
        qhat = ql * rq
        qw = qw_ref[...]
        qn = (qhat * qw).astype(BF16)
        chunks = []
        for pair in range(2):
            chunks.append(jnp.where(half == 0, dqc_ref[2 * pair, :, 128:256], dqc_ref[2 * pair + 1, :, 128:256]))
        dqr = jnp.concatenate(chunks, axis=1)
        dqr = dqr * cs + _swap_halves(dqr * sn)
        dq = jnp.concatenate([dqc_ref[h, :, 0:128] for h in range(MLA_HEADS)] + [dqr], axis=1).astype(BF16)
        gq_ref[...] += _dot_tn(dq, qn)
        dqn = _dot(dq, wuq_ref[...])
        part_ref[0:1, :] += jnp.sum(dqn * qhat, axis=0, keepdims=True)
        dql_ref[...] = _rms_bwd(dqn * qw, qhat, rq)
        kl = kl_ref[...]
        rk = _rstd(kl)
        khat = kl * rk
        kw = kw_ref[...]
        kvn = (khat * kw).astype(BF16)
        dkvn = jnp.zeros((tm, MLA_KVR), F32)
        dkr2 = jnp.zeros((tm, 128), F32)
        for h in range(MLA_HEADS):
            dkn = dkc_ref[h, :, 0:128].astype(BF16)
            dvh = dvv_ref[h].astype(BF16)
            gkv_ref[2 * h] += _dot_tn(kvn, dkn)
            gkv_ref[2 * h + 1] += _dot_tn(kvn, dvh)
            dkvn += _dot_nt(dkn, wukv_ref[2 * h]) + _dot_nt(dvh, wukv_ref[2 * h + 1])
            dkr2 += dkc_ref[h, :, 128:256]
        part_ref[1:2, 0:128] += jnp.sum(dkvn * khat, axis=0, keepdims=True)
        dkl_ref[...] = _rms_bwd(dkvn * kw, khat, rk)
        dkr = jnp.where(half == 0, dkr2 + pltpu.roll(dkr2, 64, 1), 0.0)
        dkr_ref[...] = dkr * cs[:, :128] + _swap_halves(dkr * sn[:, :128])

    const = lambda shape: pl.BlockSpec(shape, lambda i: tuple(0 for _ in shape))
    heads = lambda w: pl.BlockSpec((MLA_HEADS, tm, w), lambda i: (0, i, 0))
    return pl.pallas_call(
        body, name="mla_pre_bwd", grid=(S // tm,),
        in_specs=[pl.BlockSpec((tm, 256), lambda i: (i, 3)), pl.BlockSpec((tm, 128), lambda i: (i, 8)),
                  const((1, 256)), const((1, 128)), const((768, 256)), const((8, 128, 128)),
                  pl.BlockSpec((tm, 256), lambda i: (i, 0)), pl.BlockSpec((tm, 256), lambda i: (i, 0)),
                  heads(256), heads(256), heads(128)],
        out_specs=[pl.BlockSpec((tm, 256), lambda i: (i, 0)), pl.BlockSpec((tm, 128), lambda i: (i, 0)),
                   pl.BlockSpec((tm, 128), lambda i: (i, 0)), const((768, 256)), const((8, 128, 128)), const((8, 256))],
        out_shape=[jax.ShapeDtypeStruct((S, 256), F32), jax.ShapeDtypeStruct((S, 128), F32),
                   jax.ShapeDtypeStruct((S, 128), F32), jax.ShapeDtypeStruct((768, 256), F32),
                   jax.ShapeDtypeStruct((8, 128, 128), F32), jax.ShapeDtypeStruct((8, 256), F32)],
        compiler_params=_params(("arbitrary",)),
    )(proj, proj, qn_w, kvn_w, wuqT, wukv, cos, sin, dqc, dkc, dvv)


def _mix_out_fwd(x, oa, ob, w_o, vecs, *, tm=256):
    S = x.shape[0]

    def body(x_ref, oa_ref, ob_ref, w_ref, vec_ref, xo_ref, mo_ref):
        mo = _dot(oa_ref[...].astype(BF16), w_ref[0:512, :]) + _dot(ob_ref[...].astype(BF16), w_ref[512:1024, :])
        mo_ref[...] = mo
        xo_ref[...] = x_ref[...] + vec_ref[3:4, :] * mo

    row = pl.BlockSpec((tm, D), lambda i: (i, 0))
    half = pl.BlockSpec((tm, 512), lambda i: (i, 0))
    return pl.pallas_call(
        body, name="mix_out_fwd", grid=(S // tm,),
        in_specs=[row, half, half, pl.BlockSpec((D, D), lambda i: (0, 0)), pl.BlockSpec((8, D), lambda i: (0, 0))],
        out_specs=[row, row],
        out_shape=[jax.ShapeDtypeStruct((S, D), F32), jax.ShapeDtypeStruct((S, D), F32)],
        compiler_params=_params(("parallel",)),
    )(x, oa, ob, w_o, vecs)


def _mix_out_bwd(dxo, mo, oa, ob, w_o, vecs, *, tm=256):
    S = dxo.shape[0]

    def body(dx_ref, mo_ref, oa_ref, ob_ref, w_ref, vec_ref, doa_ref, dob_ref, gw_ref, part_ref):
        @pl.when(pl.program_id(0) == 0)
        def _():
            gw_ref[...] = jnp.zeros_like(gw_ref)
            part_ref[...] = jnp.zeros_like(part_ref)

        dx = dx_ref[...]
        part_ref[0:1, :] += jnp.sum(dx * mo_ref[...], axis=0, keepdims=True)
        dmo = (vec_ref[3:4, :] * dx).astype(BF16)
        doa_ref[...] = _dot_nt(dmo, w_ref[0:512, :])
        dob_ref[...] = _dot_nt(dmo, w_ref[512:1024, :])
        gw_ref[0:512, :] += _dot_tn(oa_ref[...].astype(BF16), dmo)
        gw_ref[512:1024, :] += _dot_tn(ob_ref[...].astype(BF16), dmo)

    row = pl.BlockSpec((tm, D), lambda i: (i, 0))
    half = pl.BlockSpec((tm, 512), lambda i: (i, 0))
    return pl.pallas_call(
        body, name="mix_out_bwd", grid=(S // tm,),
        in_specs=[row, row, half, half, pl.BlockSpec((D, D), lambda i: (0, 0)), pl.BlockSpec((8, D), lambda i: (0, 0))],
        out_specs=[half, half, pl.BlockSpec((D, D), lambda i: (0, 0)), pl.BlockSpec((8, D), lambda i: (0, 0))],
        out_shape=[jax.ShapeDtypeStruct((S, 512), F32), jax.ShapeDtypeStruct((S, 512), F32),
                   jax.ShapeDtypeStruct((D, D), F32), jax.ShapeDtypeStruct((8, D), F32)],
        compiler_params=_params(("arbitrary",)),
    )(dxo, mo, oa, ob, w_o, vecs)


def _mix_in_bwd(h, w_inT, dq, dk, dv, dql, dkl, dkr, *, tm=256):
    S = h.shape[0]
    offs = (0, 512, 640, 768, 1024, 1152)
    wid = (512, 128, 128, 256, 128, 128)

    def body(h_ref, w_ref, dq_ref, dk_ref, dv_ref, dql_ref, dkl_ref, dkr_ref, dh_ref, gw_ref):
        @pl.when(pl.program_id(0) == 0)
        def _():
            gw_ref[...] = jnp.zeros_like(gw_ref)

        hv = h_ref[...]
        dh = jnp.zeros((tm, D), F32)
        for ref, o, w in zip((dq_ref, dk_ref, dv_ref, dql_ref, dkl_ref, dkr_ref), offs, wid):
            w = min(w, D_IN - o)
            dpart = ref[...][:, :w].astype(BF16)
            dh += _dot(dpart, w_ref[o:o + w, :])
            gw_ref[o:o + w, :] += _dot_tn(dpart, hv)
        dh_ref[...] = dh

    row = pl.BlockSpec((tm, D), lambda i: (i, 0))
    part = lambda w: pl.BlockSpec((tm, w), lambda i: (i, 0))
    return pl.pallas_call(
        body, name="mix_in_bwd", grid=(S // tm,),
        in_specs=[row, pl.BlockSpec((D_IN_PAD, D), lambda i: (0, 0))] + [part(w) for w in wid],
        out_specs=[row, pl.BlockSpec((D_IN, D), lambda i: (0, 0))],
        out_shape=[jax.ShapeDtypeStruct((S, D), F32), jax.ShapeDtypeStruct((D_IN, D), F32)],
        compiler_params=_params(("arbitrary",)),
    )(h, w_inT, dq, dk, dv, dql, dkl, dkr)


def _vecs(norm_w, mod9, k):
    return jnp.concatenate([norm_w.reshape(1, D), mod9[3 * k:3 * k + 3], jnp.zeros((4, D), F32)], axis=0)


def _uq_group_rows(wuqT):
    per = MLA_NOPE + MLA_ROPE
    nope = [wuqT[per * h:per * h + MLA_NOPE] for h in range(MLA_HEADS)]
    rope = [wuqT[per * h + MLA_NOPE:per * (h + 1)] for h in range(MLA_HEADS)]
    return jnp.concatenate(nope + rope, axis=0)


def _uq_ungroup_rows(g):
    parts = []
    for h in range(MLA_HEADS):
        parts += [g[MLA_NOPE * h:MLA_NOPE * (h + 1)], g[512 + MLA_ROPE * h:512 + MLA_ROPE * (h + 1)]]
    return jnp.concatenate(parts, axis=0)


def _local_step(x, tgt, mod9, norms, sinks, rel_bias, q_norm, kv_norm, W, on_grads=None):
    if on_grads is None:
        on_grads = lambda group, grads, after, vecs: vecs
    S = x.shape[0]
    v1 = _vecs(norms["ffn1"], mod9, 0)
    v2 = _vecs(norms["mix"], mod9, 1)
    v3 = _vecs(norms["ffn2"], mod9, 2)
    bucket = jnp.asarray(_bucket_table())
    cos, sin = _rope_tables(S)
    if isinstance(W, dict):
        full, W = W, (lambda group, after, vecs: (full, vecs))

    W1, v1 = W("ffn1", [], v1)
    x1, h1, a1, b1, f1 = _ffn_fwd(x, v1, W1["g1T"], W1["u1T"], W1["d1"], name="ffn1_fwd", tm=256, tf=D_FF)
    W2, v2 = W("mixer", [x1], v2)
    w_inT = jnp.pad(W2["w_inT"], ((0, D_IN_PAD - D_IN), (0, 0))).astype(BF16)
    wuqT = _uq_group_rows(W2["w_uqT"])
    h2, proj = _mix_in_fwd(x1, v2, w_inT)
    bias = _bias_build(rel_bias, bucket)
    oa = _swa_fwd(proj, bias, sinks)
    qc, kc, vv = _mla_pre_fwd(proj, q_norm, kv_norm, wuqT, W2["w_ukv"], cos, sin)
    ob, lse = _mla_attn_fwd(qc, kc, vv)
    _, v2o = W("ffn2_on_its_way", [ob], v2)
    x2, mo = _mix_out_fwd(x1, oa, ob, W2["w_o"], v2o)
    W3, v3 = W("ffn2", [x2], v3)
    x3, h3, a3, b3, f3 = _ffn_fwd(x2, v3, W3["g3T"], W3["u3T"], W3["d3"], name="ffn2_fwd", tm=256, tf=D_FF)
    dx3, head_part, df3 = _head(x3, tgt, norms["final"], f3, v3)

    gg3, gu3, gd3, dh3 = _ffn_bwd_main(h3, df3, a3, b3, W3["g3T"], W3["u3T"], W3["d3"], name="ffn2_bwd", tm=2048, tf=256)
    ffn2 = {"g3T": gg3, "u3T": gu3, "d3": gd3}
    v3 = on_grads("ffn2", ffn2, [], v3)
    dx2, n3_part = _norm_bwd(dh3, x2, dx3, v3, name="ffn2_norm_bwd")
    v2 = on_grads("ffn2", None, [dx2], v2)
    doa, dob, g_wo, g2_part = _mix_out_bwd(dx2, mo, oa, ob, W2["w_o"], v2)
    dq, dk, dv, drb, dsk = _swa_bwd(proj, bias, sinks, oa, doa, bucket)
    dqc, dkc, dvv = _mla_attn_bwd(qc, kc, vv, ob, lse, dob)
    dql, dkl, dkr, g_uq, g_ukv, mla_part = _mla_pre_bwd(proj, q_norm, kv_norm, wuqT, W2["w_ukv"], cos, sin, dqc, dkc, dvv)
    dh2, g_win = _mix_in_bwd(h2, w_inT, dq, dk, dv, dql, dkl, dkr)
    mixer = {"w_inT": g_win, "w_uqT": _uq_ungroup_rows(g_uq).astype(BF16),
             "w_ukv": g_ukv.astype(BF16), "w_o": g_wo.astype(BF16)}
    v2 = on_grads("mixer", mixer, [], v2)
    dx1, n2_part, df1 = _norm_bwd(dh2, x1, dx2, v2, name="mix_norm_bwd", below=(f1, v1))
    started = on_grads("mixer", None, [dx1], jnp.zeros((1, 1), F32))
    gg1, gu1, gd1, dh1 = _ffn_bwd_main(h1, df1, a1, b1, W1["g1T"], W1["u1T"], W1["d1"], name="ffn1_bwd",
                                       after=[started], tm=2048, tf=256)
    ffn1 = {"g1T": gg1, "u1T": gu1, "d1": gd1}
    v1 = on_grads("ffn1", ffn1, [], v1)
    dx0, n1_part = _norm_bwd(dh1, x, dx1, v1, name="ffn1_norm_bwd")

    grads = {**ffn1, **ffn2, **mixer}
    dmod9 = jnp.concatenate([n1_part[1:3], n2_part[3:4], n2_part[1:3], g2_part[0:1],
                             n3_part[1:3], head_part[3:4]], axis=0)
    small = jnp.concatenate([n1_part[0], n2_part[0], n3_part[0], head_part[0], mla_part[0],
                             mla_part[1, :128], dsk[:, 0], head_part[1, 0:1], jnp.zeros((119,), F32), drb[:, 0]])
    return head_part[1, 0], dx0, grads, small, dmod9


SMALL_LAYOUT = (("norm_ffn1", 1024), ("norm_mix", 1024), ("norm_ffn2", 1024), ("norm_final", 1024),
                ("q_norm", 256), ("kv_norm", 128), ("sinks", 128), ("rel_bias", 256))
N_SMALL = sum(n for _, n in SMALL_LAYOUT)
LOSS_SLOT = 4 * 1024 + 256 + 128 + SWA_HEADS


def _coords():
    return lax.axis_index("x"), lax.axis_index("y"), lax.axis_index("c")


def _flip(v, bit):
    return 1 - v if bit else v


def _peer(r):
    x, y, c = _coords()
    return (_flip(x, r & 4), _flip(y, r & 2), _flip(c, r & 1))


def _mod_fwd(c_tile, w_mod, b_mod3):
    W = w_mod.shape[1]

    def body(c_ref, w_ref, b_ref, mod_ref, ca_ref, call_ref, part_ref, send_sems, recv_sems):
        x, y, c = _coords()
        me = 4 * x + 2 * y + c
        call_ref[me] = c_ref[...]
        sends = []
        for r in range(1, N_DEV):
            cp = pltpu.make_async_remote_copy(c_ref, call_ref.at[me], send_sems.at[0, r], recv_sems.at[0, r],
                                              device_id=_peer(r), device_id_type=MESH)
            cp.start()
            sends.append(cp)
        for r in range(1, N_DEV):
            pltpu.make_async_remote_copy(c_ref, call_ref.at[me], send_sems.at[0, r], recv_sems.at[0, r],
                                         device_id=_peer(r), device_id_type=MESH).wait_recv()
        cv = call_ref[...].reshape(8 * N_DEV, D)
        ca = (cv * _sigmoid(cv)).astype(BF16)
        ca_ref[...] = ca
        part_ref[...] = _dot(ca, w_ref[...].astype(BF16)).reshape(N_DEV, 8, W)
        mod_ref[me] = part_ref[me] + b_ref[me]
        for r in range(1, N_DEV):
            cp = pltpu.make_async_remote_copy(part_ref.at[me ^ r], mod_ref.at[me], send_sems.at[1, r],
                                              recv_sems.at[1, r], device_id=_peer(r), device_id_type=MESH)
            cp.start()
            sends.append(cp)
        for r in range(1, N_DEV):
            pltpu.make_async_remote_copy(part_ref.at[me ^ r], mod_ref.at[me], send_sems.at[1, r],
                                         recv_sems.at[1, r], device_id=_peer(r), device_id_type=MESH).wait_recv()
            mod_ref[me ^ r] = mod_ref[me ^ r] + b_ref[me ^ r]
        for cp in sends:
            cp.wait_send()

    vm = pl.BlockSpec(memory_space=pltpu.VMEM)
    return pl.pallas_call(
        body, name="mod_fwd", in_specs=[vm, vm, vm], out_specs=[vm, vm],
        out_shape=[jax.ShapeDtypeStruct((N_DEV, 8, W), F32), jax.ShapeDtypeStruct((8 * N_DEV, D), BF16)],
        scratch_shapes=[pltpu.VMEM((N_DEV, 8, D), F32), pltpu.VMEM((N_DEV, 8, W), F32),
                        pltpu.SemaphoreType.DMA((2, N_DEV)), pltpu.SemaphoreType.DMA((2, N_DEV))],
        compiler_params=_params(),
    )(c_tile, w_mod, b_mod3)


N_MODVEC = N_MOD * D
N_VEC = N_MODVEC + N_SMALL


def _mod_bwd(allvec, ca, me_idx):
    W = N_MODVEC // N_DEV

    def body(me_ref, all_ref, cols_ref, ca_ref, gw_ref, sum_ref):
        in_first_row = lax.broadcasted_iota(jnp.int32, (N_DEV, 8, W), 1) == 0
        dm = jnp.where(in_first_row, cols_ref[...], 0.0).reshape(8 * N_DEV, W)
        gw_ref[...] = _dot_tn(ca_ref[...], dm.astype(BF16))
        total = all_ref[0]
        for k in range(1, N_DEV):
            total = total + all_ref[k]
        sum_ref[...] = total

    return pl.pallas_call(
        body, name="mod_bwd",
        grid_spec=pltpu.PrefetchScalarGridSpec(
            num_scalar_prefetch=1, grid=(1,),
            in_specs=[pl.BlockSpec((N_DEV, 1, N_VEC), lambda i, me: (0, 0, 0)),
                      pl.BlockSpec((N_DEV, 1, W), lambda i, me: (0, 0, me[0])),
                      pl.BlockSpec((8 * N_DEV, D), lambda i, me: (0, 0))],
            out_specs=[pl.BlockSpec((D, W), lambda i, me: (0, 0)), pl.BlockSpec((1, N_VEC), lambda i, me: (0, 0))]),
        out_shape=[jax.ShapeDtypeStruct((D, W), F32), jax.ShapeDtypeStruct((1, N_VEC), F32)],
        compiler_params=_params(("arbitrary",)),
    )(me_idx, allvec, allvec, ca)


def _wgather(shards):
    n = len(shards)

    def body(*refs):
        ins, outs, token = refs[:n], refs[n:2 * n], refs[2 * n]
        send_sems, recv_sems, local_sems = refs[2 * n + 1:]
        token[...] = jnp.zeros_like(token)
        x, y, c = _coords()
        me = 4 * x + 2 * y + c
        sib = (x, y, 1 - c)
        chips = [(1 - x, y), (x, 1 - y), (1 - x, 1 - y)]

        def copy(k, slot, block, to, src=None):
            return pltpu.make_async_remote_copy(
                src_ref=outs[k].at[block] if src is None else src, dst_ref=outs[k].at[block],
                send_sem=send_sems.at[k, slot], recv_sem=recv_sems.at[k, slot], device_id=to, device_id_type=MESH)

        local = [pltpu.make_async_copy(ins[k], outs[k].at[me], local_sems.at[k]) for k in range(n)]
        for cp in local:
            cp.start()
        first = []
        for k in range(n):
            first.append(copy(k, 0, me, sib, src=ins[k]))
            for j, chip in enumerate(chips):
                first.append(copy(k, 1 + j, me, (*chip, c), src=ins[k]))
        for cp in first:
            cp.start()
        passed = []
        for j, (cx, cy) in enumerate(chips):
            for k in range(n):
                blk = 4 * cx + 2 * cy + c
                copy(k, 1 + j, blk, sib).wait_recv()
                cp = copy(k, 4 + j, blk, sib)
                cp.start()
                passed.append(cp)
        for k in range(n):
            copy(k, 0, 4 * x + 2 * y + (1 - c), sib).wait_recv()
            for j, (cx, cy) in enumerate(chips):
                copy(k, 4 + j, 4 * cx + 2 * cy + (1 - c), sib).wait_recv()
        for cp in first + passed:
            cp.wait_send()
        for cp in local:
            cp.wait()

    anyspec = pl.BlockSpec(memory_space=pl.ANY)
    return pl.pallas_call(
        body, name="wgather", in_specs=[anyspec] * n,
        out_specs=[anyspec] * n + [pl.BlockSpec(memory_space=pltpu.VMEM)],
        out_shape=[jax.ShapeDtypeStruct((N_DEV,) + s.shape, s.dtype) for s in shards]
        + [jax.ShapeDtypeStruct((8, 128), F32)],
        scratch_shapes=[pltpu.SemaphoreType.DMA((n, 7)), pltpu.SemaphoreType.DMA((n, 7)),
                        pltpu.SemaphoreType.DMA((n,))],
    )(*shards)


class _GatherCopies:
    def __init__(self, lands, send_sems, recv_sems):
        x, y, c = _coords()
        me = 4 * x + 2 * y + c
        sib = (x, y, 1 - c)
        chips = [(1 - x, y), (x, 1 - y), (1 - x, 1 - y)]

        def copy(k, slot, block, to):
            return pltpu.make_async_remote_copy(
                src_ref=lands[k].at[block], dst_ref=lands[k].at[block],
                send_sem=send_sems.at[7 * k + slot], recv_sem=recv_sems.at[7 * k + slot],
                device_id=to, device_id_type=MESH)

        n = len(lands)
        self.first = [copy(k, 0, me, sib) for k in range(n)]
        self.first += [copy(k, 1 + j, me, (cx, cy, c)) for j, (cx, cy) in enumerate(chips) for k in range(n)]
        self.landed = [copy(k, 1 + j, 4 * cx + 2 * cy + c, sib) for j, (cx, cy) in enumerate(chips) for k in range(n)]
        self.passed = [copy(k, 4 + j, 4 * cx + 2 * cy + c, sib) for j, (cx, cy) in enumerate(chips) for k in range(n)]
        self.from_sib = [copy(k, 0, 4 * x + 2 * y + (1 - c), sib) for k in range(n)]
        self.from_sib += [copy(k, 4 + j, 4 * cx + 2 * cy + (1 - c), sib) for j, (cx, cy) in enumerate(chips)
                          for k in range(n)]


def _gather_start(lands, *, name):
    n = len(lands)

    def body(*refs):
        for cp in _GatherCopies(refs[:n], refs[n], refs[n + 1]).first:
            cp.start()
        refs[-1][...] = jnp.zeros_like(refs[-1])

    out = pl.pallas_call(
        body, name=name,
        out_shape=(pltpu.SemaphoreType.DMA((7 * n,)), pltpu.SemaphoreType.DMA((7 * n,)),
                   *[pltpu.HBM(l.shape, l.dtype) for l in lands], jax.ShapeDtypeStruct((8, 128), F32)),
        in_specs=[HBM_SPEC] * n,
        out_specs=(SEM_SPEC, SEM_SPEC, *[HBM_SPEC] * n, pl.BlockSpec(memory_space=pltpu.VMEM)),
        input_output_aliases={i: 2 + i for i in range(n)},
        compiler_params=pltpu.CompilerParams(has_side_effects=DATAFLOW),
    )(*[_in_hbm(l) for l in lands])
    return out[0], out[1], list(out[2:2 + n]), out[-1]


def _gather_pass(send_sems, recv_sems, lands, after, *, name, stage):
    n = len(lands)

    def body(*refs):
        cps = _GatherCopies(refs[:n], refs[n], refs[n + 1])
        if stage == "landed":
            for cp in cps.landed:
                cp.wait_recv()
        else:
            for cp in cps.passed:
                cp.start()
        refs[-1][...] = jnp.zeros_like(refs[-1])

    out = pl.pallas_call(
        body, name=name,
        out_shape=(*[pltpu.HBM(l.shape, l.dtype) for l in lands], jax.ShapeDtypeStruct((8, 128), F32)),
        in_specs=[HBM_SPEC] * n + [SEM_SPEC, SEM_SPEC] + [pl.BlockSpec(memory_space=pl.ANY)] * len(after),
        out_specs=(*[HBM_SPEC] * n, pl.BlockSpec(memory_space=pltpu.VMEM)),
        input_output_aliases={i: i for i in range(n)},
        compiler_params=pltpu.CompilerParams(has_side_effects=DATAFLOW),
    )(*lands, send_sems, recv_sems, *after)
    return list(out[:n]), out[-1]


def _gather_end(send_sems, recv_sems, lands, after, *, name):
    n = len(lands)

    def body(*refs):
        cps = _GatherCopies(refs[:n], refs[n], refs[n + 1])
        for cp in cps.from_sib:
            cp.wait_recv()
        for cp in cps.first + cps.passed:
            cp.wait_send()

    out = pl.pallas_call(
        body, name=name,
        out_shape=[pltpu.HBM(l.shape, l.dtype) for l in lands],
        in_specs=[HBM_SPEC] * n + [SEM_SPEC, SEM_SPEC] + [pl.BlockSpec(memory_space=pl.ANY)] * len(after),
        out_specs=[HBM_SPEC] * n,
        input_output_aliases={i: i for i in range(n)},
        compiler_params=pltpu.CompilerParams(has_side_effects=DATAFLOW),
    )(*lands, send_sems, recv_sems, *after)
    return list(out)


def _d2d_copies(grads, lands, send_sems, recv_sems):
    x, y, c = _coords()
    return [pltpu.make_async_remote_copy(
        src_ref=grads[k].at[2 * q + (1 - c)], dst_ref=lands[k].at[q],
        send_sem=send_sems.at[4 * k + q], recv_sem=recv_sems.at[4 * k + q],
        device_id=(x, y, 1 - c), device_id_type=MESH) for k in range(len(grads)) for q in range(4)]


def _vec_copies(srcs, lands, send_sems, recv_sems):
    x, y, c = _coords()
    me = 4 * x + 2 * y + c
    return [pltpu.make_async_remote_copy(
        src_ref=lands[0].at[me], dst_ref=lands[0].at[me], send_sem=send_sems.at[r - 1], recv_sem=recv_sems.at[r - 1],
        device_id=_peer(r), device_id_type=MESH) for r in range(1, N_DEV)]


def _chipsum(g, sib, cidx, *, name):
    _, r, cc = g.shape

    def body(c_ref, g_ref, s_ref, o_ref):
        o_ref[...] = (g_ref[...].astype(F32) + s_ref[...].astype(F32)).astype(o_ref.dtype)

    return pl.pallas_call(
        body, name=name,
        grid_spec=pltpu.PrefetchScalarGridSpec(
            num_scalar_prefetch=1, grid=(4,),
            in_specs=[pl.BlockSpec((1, r, cc), lambda q, c_ref: (2 * q + c_ref[0], 0, 0)),
                      pl.BlockSpec((1, r, cc), lambda q, c_ref: (q, 0, 0))],
            out_specs=pl.BlockSpec((1, r, cc), lambda q, c_ref: (q, 0, 0))),
        out_shape=jax.ShapeDtypeStruct((4, r, cc), g.dtype),
        compiler_params=_params(("arbitrary",)),
    )(cidx, g, sib)


HBM_SPEC = pl.BlockSpec(memory_space=pltpu.HBM)
SEM_SPEC = pl.BlockSpec(memory_space=pltpu.SEMAPHORE)
DATAFLOW = pltpu.SideEffectType.DATAFLOW_SIDE_EFFECTING


def _in_hbm(a):
    return pltpu.with_memory_space_constraint(a, pltpu.HBM)


def _ici_copies(sums, lands, send_sems, recv_sems):
    x, y, c = _coords()
    chips = [(1 - x, y), (x, 1 - y), (1 - x, 1 - y)]
    cps = []
    for k in range(len(sums)):
        for j, (cx, cy) in enumerate(chips):
            cps.append(pltpu.make_async_remote_copy(
                src_ref=sums[k].at[2 * cx + cy], dst_ref=lands[k].at[j],
                send_sem=send_sems.at[3 * k + j], recv_sem=recv_sems.at[3 * k + j],
                device_id=(cx, cy, c), device_id_type=MESH))
    return cps


def _split_start(copies, srcs, lands, n_sems, after, *, name):
    ns, nl = len(srcs), len(lands)

    def body(*refs):
        for cp in copies(refs[:ns], refs[ns:ns + nl], refs[ns + nl + len(after)], refs[ns + nl + len(after) + 1]):
            cp.start()
        refs[-1][...] = jnp.zeros_like(refs[-1])

    bufs = [_in_hbm(a) for a in list(srcs) + list(lands)]
    out = pl.pallas_call(
        body, name=name,
        out_shape=(pltpu.SemaphoreType.DMA((n_sems,)), pltpu.SemaphoreType.DMA((n_sems,)),
                   *[pltpu.HBM(a.shape, a.dtype) for a in bufs], jax.ShapeDtypeStruct((8, 128), F32)),
        in_specs=[HBM_SPEC] * len(bufs) + [pl.BlockSpec(memory_space=pl.ANY)] * len(after),
        out_specs=(SEM_SPEC, SEM_SPEC, *[HBM_SPEC] * len(bufs), pl.BlockSpec(memory_space=pltpu.VMEM)),
        input_output_aliases={i: 2 + i for i in range(len(bufs))},
        compiler_params=pltpu.CompilerParams(has_side_effects=DATAFLOW),
    )(*bufs, *after)
    return out[0], out[1], list(out[2:2 + ns]), list(out[2 + ns:2 + ns + nl]), out[-1]


def _split_wait(copies, send_sems, recv_sems, srcs, lands, after, *, name):
    ns, nl = len(srcs), len(lands)

    def body(*refs):
        for cp in copies(refs[:ns], refs[ns:ns + nl], refs[ns + nl], refs[ns + nl + 1]):
            cp.wait_send()
            cp.wait_recv()

    out = pl.pallas_call(
        body, name=name,
        out_shape=[pltpu.HBM(a.shape, a.dtype) for a in list(srcs) + list(lands)],
        in_specs=[HBM_SPEC] * (ns + nl) + [SEM_SPEC, SEM_SPEC] + [pl.BlockSpec(memory_space=pl.ANY)] * len(after),
        out_specs=[HBM_SPEC] * (ns + nl),
        input_output_aliases={i: i for i in range(ns + nl)},
        compiler_params=pltpu.CompilerParams(has_side_effects=DATAFLOW),
    )(*srcs, *lands, send_sems, recv_sems, *after)
    return list(out[:ns]), list(out[ns:])


ADAM_C1 = 1.0 / (1.0 - ADAM_B1 ** ADAM_STEP)
ADAM_C2 = 1.0 / (1.0 - ADAM_B2 ** ADAM_STEP)


def _adam_math(w, g, m, v):
    m2 = ADAM_B1 * m + (1.0 - ADAM_B1) * g
    v2 = ADAM_B2 * v + (1.0 - ADAM_B2) * (g * g)
    return -ADAM_LR * ((m2 * ADAM_C1) / (jnp.sqrt(v2 * ADAM_C2) + ADAM_EPS) + ADAM_WD * w), m2, v2


def _adamw(w, g, m, v, *, name):
    R, C = w.shape
    tr = R if R <= 512 else 256

    def body(w_ref, g_ref, m_ref, v_ref, d_ref, nm_ref, nv_ref):
        d_ref[...], nm_ref[...], nv_ref[...] = _adam_math(w_ref[...], g_ref[...], m_ref[...], v_ref[...])

    blk = pl.BlockSpec((tr, C), lambda i: (i, 0))
    return pl.pallas_call(
        body, name=name, grid=(R // tr,), in_specs=[blk] * 4, out_specs=[blk] * 3,
        out_shape=[jax.ShapeDtypeStruct((R, C), F32)] * 3,
        compiler_params=_params(("parallel",)),
    )(w, g, m, v)


def _adamw_rs(w, m, v, cs, rcv, qidx, *, name):
    r, cc = w.shape
    tr = r // 2 if r % 32 == 0 and r > 128 else r

    def body(q_ref, w_ref, m_ref, v_ref, c_ref, r_ref, g_ref, d_ref, nm_ref, nv_ref):
        g = ((c_ref[0].astype(F32) + r_ref[0].astype(F32)) + r_ref[1].astype(F32)) + r_ref[2].astype(F32)
        g_ref[...] = g
        d_ref[...], nm_ref[...], nv_ref[...] = _adam_math(w_ref[...], g, m_ref[...], v_ref[...])

    blk = pl.BlockSpec((tr, cc), lambda i, q_ref: (i, 0))
    return pl.pallas_call(
        body, name=name,
        grid_spec=pltpu.PrefetchScalarGridSpec(
            num_scalar_prefetch=1, grid=(r // tr,),
            in_specs=[blk, blk, blk, pl.BlockSpec((1, tr, cc), lambda i, q_ref: (q_ref[0], i, 0)),
                      pl.BlockSpec((3, tr, cc), lambda i, q_ref: (0, i, 0))],
            out_specs=[blk] * 4),
        out_shape=[jax.ShapeDtypeStruct((r, cc), F32)] * 4,
        compiler_params=_params(("arbitrary",)),
    )(qidx, w, m, v, cs, rcv)


SMALL_PARAMS = ("norm_ffn1", "norm_mix", "norm_ffn2", "norm_final", "q_norm", "kv_norm", "sinks", "rel_bias", "b_mod")


def _adamw_small(gvec, wmv):
    widths = [wmv[3 * i].shape[1] for i in range(len(SMALL_PARAMS))]

    def body(*refs):
        g_all = refs[0]
        ins = refs[1:1 + 3 * len(SMALL_PARAMS)]
        outs = refs[1 + 3 * len(SMALL_PARAMS):]
        off = N_MODVEC
        for i, name in enumerate(SMALL_PARAMS):
            g_ref, d_ref, nm_ref, nv_ref = outs[4 * i:4 * i + 4]
            w_ref, m_ref, v_ref = ins[3 * i:3 * i + 3]
            start = 0 if name == "b_mod" else off
            g = g_all[:, start:start + widths[i]]
            g_ref[...] = g
            d_ref[...], nm_ref[...], nv_ref[...] = _adam_math(w_ref[...], g, m_ref[...], v_ref[...])
            if name != "b_mod":
                off += dict(SMALL_LAYOUT)[name]

    vm = pl.BlockSpec(memory_space=pltpu.VMEM)
    n_out = 4 * len(SMALL_PARAMS)
    out = pl.pallas_call(
        body, name="adamw_small", in_specs=[vm] * (1 + len(wmv)), out_specs=[vm] * n_out,
        out_shape=[jax.ShapeDtypeStruct((1, widths[i // 4]), F32) for i in range(n_out)],
        compiler_params=_params(),
    )(gvec, *wmv)
    return {name: out[4 * i:4 * i + 4] for i, name in enumerate(SMALL_PARAMS)}


TRANSPOSED = ("g1T", "u1T", "g3T", "u3T", "w_inT", "w_uqT")


def kernel(x, c, w_mod, b_mod, norm_ffn1, ffn1_gate, ffn1_up, ffn1_down, norm_mix, w_in, q_norm, kv_norm, w_uq, w_ukv, sinks, w_o, norm_ffn2, ffn2_gate, ffn2_up, ffn2_down, rel_bias, norm_final, loss_target, m_w_mod, m_b_mod, m_norm_ffn1, m_ffn1_gate, m_ffn1_up, m_ffn1_down, m_norm_mix, m_w_in, m_q_norm, m_kv_norm, m_w_uq, m_w_ukv, m_sinks, m_w_o, m_norm_ffn2, m_ffn2_gate, m_ffn2_up, m_ffn2_down, m_rel_bias, m_norm_final, v_w_mod, v_b_mod, v_norm_ffn1, v_ffn1_gate, v_ffn1_up, v_ffn1_down, v_norm_mix, v_w_in, v_q_norm, v_kv_norm, v_w_uq, v_w_ukv, v_sinks, v_w_o, v_norm_ffn2, v_ffn2_gate, v_ffn2_up, v_ffn2_down, v_rel_bias, v_norm_final):
    mx, my, mc = _coords()
    cidx = jnp.reshape(mc, (1,)).astype(jnp.int32)
    qidx = jnp.reshape(2 * mx + my, (1,)).astype(jnp.int32)
    WM = w_mod.shape[2]

    c_tile = jnp.pad(c, ((0, 7), (0, 0)))
    b_mod3 = jnp.pad(b_mod.reshape(N_DEV, 1, WM), ((0, 0), (0, 7), (0, 0)))
    mod3, ca = _mod_fwd(c_tile, w_mod[0], b_mod3)
    mod9 = mod3[:, 0, :].reshape(N_MOD, D)

    shards = {"g1T": ffn1_gate[0].T.astype(BF16), "u1T": ffn1_up[0].T.astype(BF16), "d1": ffn1_down[0].astype(BF16),
              "g3T": ffn2_gate[0].T.astype(BF16), "u3T": ffn2_up[0].T.astype(BF16), "d3": ffn2_down[0].astype(BF16),
              "w_inT": w_in[0].T, "w_uqT": w_uq[0].T.astype(BF16), "w_ukv": w_ukv[0].astype(BF16),
              "w_o": w_o[0].astype(BF16)}
    me = 4 * mx + 2 * my + mc
    groups = {"ffn1": ("g1T", "u1T", "d1"), "mixer": ("w_inT", "w_uqT", "w_ukv", "w_o"), "ffn2": ("g3T", "u3T", "d3")}
    arriving = {}

    def as_weights(group, gathered):
        return {k: g if k == "w_ukv" else g.reshape(N_DEV * g.shape[1], g.shape[2])
                for k, g in zip(groups[group], gathered)}

    def start_gather(group, token):
        lands = []
        for k in groups[group]:
            sh = shards[k] + token[0, 0].astype(shards[k].dtype)
            lands.append(lax.dynamic_update_slice(lax.empty((N_DEV,) + sh.shape, sh.dtype), sh[None], (me, 0, 0)))
        send, recv, lands, started = _gather_start(lands, name="gather_start_" + group)
        arriving[group] = (send, recv, lands)
        return started

    def fetch(group, after, vecs):
        if group == "ffn1":
            *gathered, token = _wgather([shards[k] + ca[1, 0].astype(shards[k].dtype) for k in groups["ffn1"]])
            token = start_gather("ffn2", start_gather("mixer", token))
            return as_weights("ffn1", gathered), vecs + token[0:1, 0:1]
        def pass_on(group, after):
            send, recv, lands = arriving[group]
            lands, token = _gather_pass(send, recv, lands, after, name="gather_landed_" + group, stage="landed")
            lands, token = _gather_pass(send, recv, lands, [token], name="gather_onward_" + group, stage="onward")
            arriving[group] = (send, recv, lands)
            return token

        if group == "ffn2_on_its_way":
            return None, vecs + pass_on("ffn2", after)[0:1, 0:1]
        if group == "mixer":
            after = [pass_on("mixer", after)]
        send, recv, lands = arriving[group]
        return as_weights(group, _gather_end(send, recv, lands, after, name="gather_end_" + group)), vecs

    norms ={"ffn1": norm_ffn1, "mix": norm_mix, "ffn2": norm_ffn2, "final": norm_final.reshape(1, D)}
    in_flight = {}

    def on_grads(group, g, after, vecs, before_ici=()):
        if g is not None:
            names = list(g)
            by_dest = [g[k] if k == "w_ukv" else g[k].reshape((N_DEV, g[k].shape[0] // N_DEV) + g[k].shape[1:])
                       for k in names]
            lands = [lax.empty((4,) + a.shape[1:], a.dtype) for a in by_dest]
            send, recv, by_dest, lands, token = _split_start(_d2d_copies, by_dest, lands, 4 * len(names), after,
                                                             name="rs_d2d_start_" + group)
            in_flight[group] = (names, send, recv, by_dest, lands)
            return vecs + token[0:1, 0:1]
        names, send, recv, by_dest, lands = in_flight[group]
        by_dest, from_sib = _split_wait(_d2d_copies, send, recv, by_dest, lands, after, name="rs_d2d_wait_" + group)
        sums = [_chipsum(a, s, cidx, name="chipsum_" + k) for k, a, s in zip(names, by_dest, from_sib)]
        lands = [lax.empty((3,) + s.shape[1:], s.dtype) for s in sums]
        send, recv, sums, lands, token = _split_start(_ici_copies, sums, lands, 3 * len(names), list(before_ici),
                                                      name="rs_ici_start_" + group)
        in_flight[group] = (names, send, recv, sums, lands, token)
        return vecs + token[0:1, 0:1]

    _, grad_x, _, small, dmod9 = _local_step(
        x[0], loss_target[0], mod9, norms, sinks, rel_bias, q_norm, kv_norm, fetch, on_grads=on_grads)

    vec = jnp.concatenate([dmod9.reshape(N_MODVEC), small]).reshape(1, 1, N_VEC)
    allvec = lax.dynamic_update_slice(lax.empty((N_DEV, 1, N_VEC), F32), vec, (me, 0, 0))
    vsend, vrecv, _, (allvec,), vec_started = _split_start(_vec_copies, [], [allvec], N_DEV - 1, [], name="vec_start")
    on_grads("ffn1", None, [grad_x], jnp.zeros((1, 1), F32), before_ici=[vec_started])

    owners = {"g1T": ("ffn1_gate", ffn1_gate, m_ffn1_gate, v_ffn1_gate), "u1T": ("ffn1_up", ffn1_up, m_ffn1_up, v_ffn1_up),
              "d1": ("ffn1_down", ffn1_down, m_ffn1_down, v_ffn1_down),
              "g3T": ("ffn2_gate", ffn2_gate, m_ffn2_gate, v_ffn2_gate), "u3T": ("ffn2_up", ffn2_up, m_ffn2_up, v_ffn2_up),
              "d3": ("ffn2_down", ffn2_down, m_ffn2_down, v_ffn2_down),
              "w_inT": ("w_in", w_in, m_w_in, v_w_in), "w_uqT": ("w_uq", w_uq, m_w_uq, v_w_uq),
              "w_ukv": ("w_ukv", w_ukv, m_w_ukv, v_w_ukv), "w_o": ("w_o", w_o, m_w_o, v_w_o)}
    res, done = {}, []

    def finish(group, after):
        names, send, recv, sums, lands, _ = in_flight[group]
        sums, lands = _split_wait(_ici_copies, send, recv, sums, lands, after, name="rs_ici_wait_" + group)
        for k, cs, rc in zip(names, sums, lands):
            pname, wk, mk, vk = owners[k]
            there = (lambda a: a[0].T) if k in TRANSPOSED else (lambda a: a[0])
            back = (lambda a: a.T[None]) if k in TRANSPOSED else (lambda a: a[None])
            out = _adamw_rs(there(wk), there(mk), there(vk), cs, rc, qidx, name="adamw_" + pname)
            done.append(out[3])
            res[pname] = tuple(back(a) for a in out)

    ffn1_started = in_flight["ffn1"][5]
    finish("ffn2", [ffn1_started])
    finish("mixer", [ffn1_started])

    _, (allvec,) = _split_wait(_vec_copies, vsend, vrecv, [], [allvec], [ffn1_started], name="vec_wait")
    g_wmod, gvec = _mod_bwd(allvec, ca, jnp.reshape(me, (1,)).astype(jnp.int32))
    loss = gvec[0, N_MODVEC + LOSS_SLOT]
    res["w_mod"] = tuple(a[None] for a in (g_wmod,) + tuple(_adamw(w_mod[0], g_wmod, m_w_mod[0], v_w_mod[0],
                                                                    name="adamw_w_mod")))
    small_in = {"norm_ffn1": (norm_ffn1, m_norm_ffn1, v_norm_ffn1), "norm_mix": (norm_mix, m_norm_mix, v_norm_mix),
                "norm_ffn2": (norm_ffn2, m_norm_ffn2, v_norm_ffn2), "norm_final": (norm_final, m_norm_final, v_norm_final),
                "q_norm": (q_norm, m_q_norm, v_q_norm), "kv_norm": (kv_norm, m_kv_norm, v_kv_norm),
                "sinks": (sinks, m_sinks, v_sinks), "rel_bias": (rel_bias, m_rel_bias, v_rel_bias),
                "b_mod": (b_mod, m_b_mod, v_b_mod)}
    small_out = _adamw_small(gvec, [a.reshape(1, -1) for k in SMALL_PARAMS for a in small_in[k]])
    for k in SMALL_PARAMS:
        res[k] = tuple(a.reshape(small_in[k][0].shape) for a in small_out[k])

    finish("ffn1", done + [res["w_mod"][3], small_out["b_mod"][3]])

    order = ("w_mod", "b_mod", "norm_ffn1", "ffn1_gate", "ffn1_up", "ffn1_down", "norm_mix", "w_in", "q_norm",
             "kv_norm", "w_uq", "w_ukv", "sinks", "w_o", "norm_ffn2", "ffn2_gate", "ffn2_up", "ffn2_down",
             "rel_bias", "norm_final")
    return (loss, grad_x[None]) + tuple(res[nm][kind] for kind in range(4) for nm in order)
```

```python
import functools
import math

import numpy as np
import jax
import jax.numpy as jnp
from jax import lax
from jax.experimental import pallas as pl
from jax.experimental.pallas import tpu as pltpu

F32 = jnp.float32
BF16 = jnp.bfloat16
MESH = pl.DeviceIdType.MESH

N_DEV = 8
D = 1024
D_FF = 2816
EPS = 1e-6
N_MOD = 9
SWA_HEADS = 8
SWA_DH = 64
WINDOW = 128
MLA_HEADS = 4
MLA_NOPE = 128
MLA_ROPE = 64
MLA_V = 128
MLA_QR = 256
MLA_KVR = 128
ROPE_THETA = 10000.0
NUM_BUCKETS = 32
D_IN = 1216
D_IN_PAD = 1280
SWA_SCALE = SWA_DH ** -0.5
MLA_SCALE = (MLA_NOPE + MLA_ROPE) ** -0.5

ADAM_LR = 0.001
ADAM_B1 = 0.9
ADAM_B2 = 0.999
ADAM_EPS = 1e-08
ADAM_WD = 0.01
ADAM_STEP = 10

V7X_VMEM_LIMIT = 56 * 1024 * 1024

NT_DIMS = (((1,), (1,)), ((), ()))
TN_DIMS = (((0,), (0,)), ((), ()))


def _dot(a, b):
    return jnp.dot(a, b, preferred_element_type=F32)


def _dot_nt(a, b):
    return lax.dot_general(a, b, NT_DIMS, preferred_element_type=F32)


def _dot_tn(a, b):
    return lax.dot_general(a, b, TN_DIMS, preferred_element_type=F32)


def _params(sem=None):
    return pltpu.CompilerParams(dimension_semantics=sem, vmem_limit_bytes=V7X_VMEM_LIMIT)


def _rstd(x):
    return lax.rsqrt(jnp.mean(x * x, axis=-1, keepdims=True) + EPS)


def _rms_bwd(dy, xhat, r):
    return r * (dy - xhat * jnp.mean(dy * xhat, axis=-1, keepdims=True))


def _sigmoid(a):
    return 1.0 / (1.0 + jnp.exp(-a))


def _ffn_fwd(x, vecs, wgT, wuT, wd, *, name, tm=512, tf=256):
    S, F = x.shape[0], wd.shape[0]
    tm = min(tm, S)
    ni, nj = S // tm, F // tf

    def body(x_ref, vec_ref, wg_ref, wu_ref, wd_ref, xo_ref, h_ref, a_ref, b_ref, f_ref, acc_ref):
        j = pl.program_id(1)

        @pl.when(j == 0)
        def _():
            xv = x_ref[...]
            hn = xv * _rstd(xv) * vec_ref[0:1, :]
            h_ref[...] = (hn * (1.0 + vec_ref[2:3, :]) + vec_ref[1:2, :]).astype(BF16)

        h = h_ref[...]
        a = _dot_nt(h, wg_ref[...])
        b = _dot_nt(h, wu_ref[...])
        a_ref[...] = a.astype(BF16)
        b_ref[...] = b.astype(BF16)
        part = _dot((a * _sigmoid(a) * b).astype(BF16), wd_ref[...])

        def finish(f):
            f_ref[...] = f
            xo_ref[...] = x_ref[...] + (0.5 * vec_ref[3:4, :]) * f

        if nj == 1:
            finish(part)
        else:
            @pl.when(j == 0)
            def _():
                acc_ref[...] = part

            @pl.when((j > 0) & (j < nj - 1))
            def _():
                acc_ref[...] += part

            @pl.when(j == nj - 1)
            def _():
                finish(acc_ref[...] + part)

    row = pl.BlockSpec((tm, D), lambda i, j: (i, 0))
    wspec = pl.BlockSpec((tf, D), lambda i, j: (j, 0), pipeline_mode=pl.Buffered(1) if nj == 1 else None)
    act = pl.BlockSpec((tm, tf), lambda i, j: (i, j))
    return pl.pallas_call(
        body, name=name, grid=(ni, nj),
        in_specs=[row, pl.BlockSpec((8, D), lambda i, j: (0, 0)), wspec, wspec, wspec],
        out_specs=[row, row, act, act, row],
        out_shape=[jax.ShapeDtypeStruct((S, D), F32), jax.ShapeDtypeStruct((S, D), BF16),
                   jax.ShapeDtypeStruct((S, F), BF16), jax.ShapeDtypeStruct((S, F), BF16),
                   jax.ShapeDtypeStruct((S, D), F32)],
        scratch_shapes=[pltpu.VMEM((tm, D) if nj > 1 else (8, 128), F32)],
        compiler_params=_params(("parallel", "arbitrary")),
    )(x, vecs, wgT, wuT, wd)


def _ffn_bwd_main(h, df, a, b, wgT, wuT, wd, *, name, after=(), tm=512, tf=256):
    S = h.shape[0]
    tm = min(tm, S)
    ni, nj = S // tm, D_FF // tf

    def body(h_hbm, df_hbm, a_ref, b_ref, wg_ref, wu_ref, wd_ref, *rest):
        gg_ref, gu_ref, gd_ref, dh_hbm, h_v, df_v, dh_v, gg_acc, gu_acc, gd_acc, sem = rest[len(after):]
        j = pl.program_id(0)
        i = pl.program_id(1)

        @pl.when((j == 0) & (i == 0))
        def _():
            c1 = pltpu.make_async_copy(h_hbm, h_v, sem.at[0])
            c2 = pltpu.make_async_copy(df_hbm, df_v, sem.at[1])
            c1.start()
            c2.start()
            c1.wait()
            c2.wait()

        @pl.when(i == 0)
        def _():
            gg_acc[...] = jnp.zeros_like(gg_acc)
            gu_acc[...] = jnp.zeros_like(gu_acc)
            gd_acc[...] = jnp.zeros_like(gd_acc)

        rows = pl.ds(pl.multiple_of(i * tm, tm), tm)
        hi = h_v[rows, :]
        dfi = df_v[rows, :]
        av = a_ref[...].astype(F32)
        bv = b_ref[...].astype(F32)
        sg = _sigmoid(av)
        sa = av * sg
        hsw = (sa * bv).astype(BF16)
        dhsw = _dot_nt(dfi, wd_ref[...])
        da = (dhsw * bv * (sg * (1.0 + av * (1.0 - sg)))).astype(BF16)
        db = (dhsw * sa).astype(BF16)
        gd_acc[...] += _dot_tn(hsw, dfi)
        gg_acc[...] += _dot_tn(da, hi)
        gu_acc[...] += _dot_tn(db, hi)
        dh = _dot(da, wg_ref[...]) + _dot(db, wu_ref[...])

        @pl.when(j == 0)
        def _():
            dh_v[rows, :] = dh

        @pl.when(j > 0)
        def _():
            dh_v[rows, :] += dh

        @pl.when(i == ni - 1)
        def _():
            gg_ref[...] = gg_acc[...].astype(BF16)
            gu_ref[...] = gu_acc[...].astype(BF16)
            gd_ref[...] = gd_acc[...].astype(BF16)

        @pl.when((j == nj - 1) & (i == ni - 1))
        def _():
            c3 = pltpu.make_async_copy(dh_v, dh_hbm, sem.at[2])
            c3.start()
            c3.wait()

    anyspec = pl.BlockSpec(memory_space=pl.ANY)
    wspec = pl.BlockSpec((tf, D), lambda j, i: (j, 0))
    act = pl.BlockSpec((tm, tf), lambda j, i: (i, j))
    return pl.pallas_call(
        body, name=name, grid=(nj, ni),
        in_specs=[anyspec, anyspec, act, act, wspec, wspec, wspec] + [anyspec] * len(after),
        out_specs=[wspec, wspec, wspec, anyspec],
        out_shape=[jax.ShapeDtypeStruct((D_FF, D), BF16)] * 3 + [jax.ShapeDtypeStruct((S, D), F32)],
        scratch_shapes=[pltpu.VMEM((S, D), BF16), pltpu.VMEM((S, D), BF16), pltpu.VMEM((S, D), F32),
                        pltpu.VMEM((tf, D), F32), pltpu.VMEM((tf, D), F32), pltpu.VMEM((tf, D), F32),
                        pltpu.SemaphoreType.DMA((3,))],
        compiler_params=_params(("arbitrary", "arbitrary")),
    )(h, df, a, b, wgT, wuT, wd, *after)


def _ffn_out_bwd(dx, f, gate, df_ref, part_ref):
    df_ref[...] = ((0.5 * gate) * dx).astype(BF16)
    part_ref[3:4, :] += 0.5 * jnp.sum(dx * f, axis=0, keepdims=True)


def _norm_bwd(dh, x, dxo, vecs, *, name, below=None, tm=256):
    S = x.shape[0]

    def body(dh_ref, x_ref, dxo_ref, vec_ref, *rest):
        dx_ref, part_ref = rest[-2 if below is None else -3], rest[-1 if below is None else -2]

        @pl.when(pl.program_id(0) == 0)
        def _():
            part_ref[...] = jnp.zeros_like(part_ref)

        dh = dh_ref[...]
        xv = x_ref[...]
        r = _rstd(xv)
        xhat = xv * r
        w = vec_ref[0:1, :]
        xn = xhat * w
        dxn = dh * (1.0 + vec_ref[2:3, :])
        part_ref[0:1, :] += jnp.sum(dxn * xhat, axis=0, keepdims=True)
        part_ref[1:2, :] += jnp.sum(dh, axis=0, keepdims=True)
        part_ref[2:3, :] += jnp.sum(dh * xn, axis=0, keepdims=True)
        dx = dxo_ref[...] + _rms_bwd(dxn * w, xhat, r)
        dx_ref[...] = dx
        if below is not None:
            _ffn_out_bwd(dx, rest[0][...], rest[1][3:4, :], rest[-1], part_ref)

    row = pl.BlockSpec((tm, D), lambda i: (i, 0))
    vec = pl.BlockSpec((8, D), lambda i: (0, 0))
    extra = [] if below is None else [row, vec]
    return pl.pallas_call(
        body, name=name, grid=(S // tm,), in_specs=[row, row, row, vec] + extra,
        out_specs=[row, vec] + ([] if below is None else [row]),
        out_shape=[jax.ShapeDtypeStruct((S, D), F32), jax.ShapeDtypeStruct((8, D), F32)]
        + ([] if below is None else [jax.ShapeDtypeStruct((S, D), BF16)]),
        compiler_params=_params(("arbitrary",)),
    )(dh, x, dxo, vecs, *([] if below is None else below))


def _head(x, tgt, nf, f, vecs, *, tm=256):
    S = x.shape[0]

    def body(x_ref, t_ref, nf_ref, f_ref, vec_ref, dx_ref, part_ref, df_ref):
        @pl.when(pl.program_id(0) == 0)
        def _():
            part_ref[...] = jnp.zeros_like(part_ref)

        xv = x_ref[...]
        r = _rstd(xv)
        xhat = xv * r
        w = nf_ref[...]
        e = xhat * w - t_ref[...]
        dy = e * (1.0 / D)
        part_ref[0:1, :] += jnp.sum(dy * xhat, axis=0, keepdims=True)
        part_ref[1:2, :] += jnp.sum(e * e) * (0.5 / D)
        dx = _rms_bwd(dy * w, xhat, r)
        dx_ref[...] = dx
        _ffn_out_bwd(dx, f_ref[...], vec_ref[3:4, :], df_ref, part_ref)

    row = pl.BlockSpec((tm, D), lambda i: (i, 0))
    vec = pl.BlockSpec((8, D), lambda i: (0, 0))
    return pl.pallas_call(
        body, name="head", grid=(S // tm,),
        in_specs=[row, row, pl.BlockSpec((1, D), lambda i: (0, 0)), row, vec],
        out_specs=[row, vec, row],
        out_shape=[jax.ShapeDtypeStruct((S, D), F32), jax.ShapeDtypeStruct((8, D), F32),
                   jax.ShapeDtypeStruct((S, D), BF16)],
        compiler_params=_params(("arbitrary",)),
    )(x, tgt, nf, f, vecs)


def _mix_in_fwd(x, vecs, w_inT, *, tm=256):
    S = x.shape[0]

    def body(x_ref, vec_ref, w_ref, h_ref, p_ref):
        xv = x_ref[...]
        hn = xv * _rstd(xv) * vec_ref[0:1, :]
        h = (hn * (1.0 + vec_ref[2:3, :]) + vec_ref[1:2, :]).astype(BF16)
        h_ref[...] = h
        p_ref[...] = _dot_nt(h, w_ref[...])

    row = pl.BlockSpec((tm, D), lambda i: (i, 0))
    return pl.pallas_call(
        body, name="mix_in_fwd", grid=(S // tm,),
        in_specs=[row, pl.BlockSpec((8, D), lambda i: (0, 0)), pl.BlockSpec((D_IN_PAD, D), lambda i: (0, 0))],
        out_specs=[row, pl.BlockSpec((tm, D_IN_PAD), lambda i: (i, 0))],
        out_shape=[jax.ShapeDtypeStruct((S, D), BF16), jax.ShapeDtypeStruct((S, D_IN_PAD), F32)],
        compiler_params=_params(("parallel",)),
    )(x, vecs, w_inT)


def _bucket_table():
    qi = np.arange(WINDOW)[:, None]
    kj = np.arange(2 * WINDOW)[None, :]
    dist = qi + WINDOW - kj
    max_exact = NUM_BUCKETS // 2
    n = np.maximum(dist, 0)
    nf = np.maximum(n, 1).astype(np.float32)
    large = max_exact + (np.log(nf / np.float32(max_exact)) / np.float32(math.log(WINDOW / max_exact))
                         * np.float32(NUM_BUCKETS - max_exact)).astype(np.int32)
    large = np.minimum(large, NUM_BUCKETS - 1)
    return np.where(n < max_exact, n, large).astype(np.int32)


def _bias_build(rel_bias, bucket):
    def body(rb_ref, bk_ref, out_ref):
        bk = bk_ref[...]
        for h in range(SWA_HEADS):
            acc = jnp.zeros((WINDOW, 2 * WINDOW), F32)
            for b in range(NUM_BUCKETS):
                acc = jnp.where(bk == b, rb_ref[b, h], acc)
            out_ref[h] = acc

    return pl.pallas_call(
        body, name="bias_build",
        in_specs=[pl.BlockSpec(memory_space=pltpu.SMEM), pl.BlockSpec(memory_space=pltpu.VMEM)],
        out_specs=pl.BlockSpec(memory_space=pltpu.VMEM),
        out_shape=jax.ShapeDtypeStruct((SWA_HEADS, WINDOW, 2 * WINDOW), F32),
    )(rel_bias, bucket)


SWA_GROUP = 4
GROUP_ROWS = SWA_GROUP * WINDOW


def _swa_valid(n):
    row = lax.broadcasted_iota(jnp.int32, (GROUP_ROWS, 2 * WINDOW), 0) % WINDOW
    col = lax.broadcasted_iota(jnp.int32, (GROUP_ROWS, 2 * WINDOW), 1)
    dist = row + WINDOW - col
    return (dist >= 0) & (dist < WINDOW) & ((col >= WINDOW) | (n > 0))


def _stack_heads(x, g):
    return jnp.concatenate([x[:, 64 * h:64 * h + 64] for h in range(SWA_GROUP * g, SWA_GROUP * (g + 1))], axis=0)


def _unstack_heads(x4):
    return jnp.concatenate([x4[WINDOW * a:WINDOW * (a + 1)] for a in range(SWA_GROUP)], axis=1)


def _group_sinks(sink_ref, g):
    head = lax.broadcasted_iota(jnp.int32, (GROUP_ROWS, 1), 0) // WINDOW
    out = jnp.full((GROUP_ROWS, 1), sink_ref[0, SWA_GROUP * g], F32)
    for a in range(1, SWA_GROUP):
        out = jnp.where(head == a, sink_ref[0, SWA_GROUP * g + a], out)
    return out


def _swa_probs(qh, kk, bias_h, sink, valid):
    s = _dot_nt(qh, kk) * SWA_SCALE + bias_h
    s = jnp.where(valid, s, -jnp.inf)
    m = jnp.maximum(jnp.max(s, axis=-1, keepdims=True), sink)
    p = jnp.exp(s - m)
    ps = jnp.exp(sink - m)
    inv = 1.0 / (jnp.sum(p, axis=-1, keepdims=True) + ps)
    return p * inv, ps * inv


def _swa_specs():
    prev = lambda n: jnp.maximum(n - 1, 0)
    return [pl.BlockSpec((WINDOW, 512), lambda n: (n, 0)),
            pl.BlockSpec((WINDOW, 128), lambda n: (n, 4)),
            pl.BlockSpec((WINDOW, 128), lambda n: (prev(n), 4)),
            pl.BlockSpec((WINDOW, 128), lambda n: (n, 5)),
            pl.BlockSpec((WINDOW, 128), lambda n: (prev(n), 5)),
            pl.BlockSpec((SWA_HEADS, WINDOW, 2 * WINDOW), lambda n: (0, 0, 0)),
            pl.BlockSpec(memory_space=pltpu.SMEM)]


def _swa_fwd(proj, bias, sinks):
    S = proj.shape[0]

    def body(q_ref, kc_ref, kp_ref, vc_ref, vp_ref, bias_ref, sink_ref, o_ref):
        valid = _swa_valid(pl.program_id(0))
        q = q_ref[...].astype(BF16)
        kfull = jnp.concatenate([kp_ref[...], kc_ref[...]], axis=0).astype(BF16)
        vfull = jnp.concatenate([vp_ref[...], vc_ref[...]], axis=0).astype(BF16)
        for g in range(SWA_HEADS // SWA_GROUP):
            kk = kfull[:, 64 * g:64 * g + 64]
            vv = vfull[:, 64 * g:64 * g + 64]
            bias4 = bias_ref[SWA_GROUP * g:SWA_GROUP * (g + 1)].reshape(GROUP_ROWS, 2 * WINDOW)
            pk, _ = _swa_probs(_stack_heads(q, g), kk, bias4, _group_sinks(sink_ref, g), valid)
            o_ref[:, 256 * g:256 * (g + 1)] = _unstack_heads(_dot(pk.astype(BF16), vv))

    return pl.pallas_call(
        body, name="swa_fwd", grid=(S // WINDOW,),
        in_specs=_swa_specs(),
        out_specs=pl.BlockSpec((WINDOW, 512), lambda n: (n, 0)),
        out_shape=jax.ShapeDtypeStruct((S, 512), F32),
        compiler_params=_params(("parallel",)),
    )(proj, proj, proj, proj, proj, bias, sinks)


def _swa_bwd(proj, bias, sinks, o, do, bucket):
    S = proj.shape[0]
    nb = S // WINDOW

    def body(q_ref, kc_ref, kp_ref, vc_ref, vp_ref, bias_ref, sink_ref, o_ref, do_ref, bk_ref,
             dq_ref, dk_ref, dv_ref, drb_ref, dsk_ref, dbias_acc):
        n = pl.program_id(0)

        @pl.when(n == 0)
        def _():
            dk_ref[...] = jnp.zeros_like(dk_ref)
            dv_ref[...] = jnp.zeros_like(dv_ref)
            dsk_ref[...] = jnp.zeros_like(dsk_ref)
            dbias_acc[...] = jnp.zeros_like(dbias_acc)
            drb_ref[...] = jnp.zeros_like(drb_ref)

        valid = _swa_valid(n)
        q = q_ref[...].astype(BF16)
        dov = do_ref[...]
        kfull = jnp.concatenate([kp_ref[...], kc_ref[...]], axis=0).astype(BF16)
        vfull = jnp.concatenate([vp_ref[...], vc_ref[...]], axis=0).astype(BF16)
        prow = pl.ds(pl.multiple_of(jnp.maximum(n - 1, 0) * WINDOW, WINDOW), WINDOW)
        crow = pl.ds(pl.multiple_of(n * WINDOW, WINDOW), WINDOW)
        ov = o_ref[...]
        for g in range(SWA_HEADS // SWA_GROUP):
            heads = slice(SWA_GROUP * g, SWA_GROUP * (g + 1))
            kk = kfull[:, 64 * g:64 * g + 64]
            vv = vfull[:, 64 * g:64 * g + 64]
            q4 = _stack_heads(q, g)
            pk, psink = _swa_probs(q4, kk, bias_ref[heads].reshape(GROUP_ROWS, 2 * WINDOW), _group_sinks(sink_ref, g), valid)
            pkb = pk.astype(BF16)
            do4 = _stack_heads(dov, g)
            dob = do4.astype(BF16)
            dp = _dot_nt(dob, vv)
            delta = jnp.sum(do4 * _stack_heads(ov, g), axis=-1, keepdims=True)
            ds = pk * (dp - delta)
            dsink = -psink * delta
            for a in range(SWA_GROUP):
                h = SWA_GROUP * g + a
                part = jnp.sum(dsink[WINDOW * a:WINDOW * (a + 1)], keepdims=True)
                dsk_ref[h:h + 1, :] += jnp.broadcast_to(part, (1, 128))
            dbias_acc[heads] += ds.reshape(SWA_GROUP, WINDOW, 2 * WINDOW)
            dsb = (ds * SWA_SCALE).astype(BF16)
            dq_ref[:, 256 * g:256 * (g + 1)] = _unstack_heads(_dot(dsb, kk))
            dkk = _dot_tn(dsb, q4)
            dvv = _dot_tn(pkb, dob)
            dk_ref[prow, 64 * g:64 * g + 64] += dkk[:WINDOW]
            dk_ref[crow, 64 * g:64 * g + 64] += dkk[WINDOW:]
            dv_ref[prow, 64 * g:64 * g + 64] += dvv[:WINDOW]
            dv_ref[crow, 64 * g:64 * g + 64] += dvv[WINDOW:]

        @pl.when(n == nb - 1)
        def _():
            bk = bk_ref[...]
            for h in range(SWA_HEADS):
                dbh = dbias_acc[h]
                for b in range(NUM_BUCKETS):
                    val = jnp.sum(jnp.where(bk == b, dbh, 0.0), keepdims=True)
                    drb_ref[b * 8 + h:b * 8 + h + 1, :] = jnp.broadcast_to(val, (1, 128))

    full = lambda shape: pl.BlockSpec(shape, lambda n: tuple(0 for _ in shape))
    return pl.pallas_call(
        body, name="swa_bwd", grid=(nb,),
        in_specs=_swa_specs() + [pl.BlockSpec((WINDOW, 512), lambda n: (n, 0)),
                                 pl.BlockSpec((WINDOW, 512), lambda n: (n, 0)), full((WINDOW, 2 * WINDOW))],
        out_specs=[pl.BlockSpec((WINDOW, 512), lambda n: (n, 0)), full((S, 128)), full((S, 128)),
                   full((NUM_BUCKETS * 8, 128)), full((8, 128))],
        out_shape=[jax.ShapeDtypeStruct((S, 512), F32), jax.ShapeDtypeStruct((S, 128), F32),
                   jax.ShapeDtypeStruct((S, 128), F32), jax.ShapeDtypeStruct((NUM_BUCKETS * 8, 128), F32),
                   jax.ShapeDtypeStruct((8, 128), F32)],
        scratch_shapes=[pltpu.VMEM((SWA_HEADS, WINDOW, 2 * WINDOW), F32)],
        compiler_params=_params(("arbitrary",)),
    )(proj, proj, proj, proj, proj, bias, sinks, o, do, bucket)


def _rope_tables(S):
    inv = ROPE_THETA ** (-jnp.arange(0, MLA_ROPE, 2, dtype=F32) / MLA_ROPE)
    ang = jnp.arange(S, dtype=F32)[:, None] * inv[None, :]
    cos, sin = jnp.cos(ang), jnp.sin(ang)
    return jnp.tile(jnp.concatenate([cos, cos], axis=1), (1, 4)), jnp.tile(jnp.concatenate([-sin, sin], axis=1), (1, 4))


def _swap_halves(x):
    w = x.shape[-1]
    lane = lax.broadcasted_iota(jnp.int32, x.shape, x.ndim - 1)
    return jnp.where((lane % 64) < 32, pltpu.roll(x, w - 32, x.ndim - 1), pltpu.roll(x, 32, x.ndim - 1))


def _mla_pre_fwd(proj, qn_w, kvn_w, wuqT, wukv, cos, sin, *, tm=256):
    S = proj.shape[0]

    def body(ql_ref, kl_ref, kr_ref, qw_ref, kw_ref, wuq_ref, wukv_ref, cos_ref, sin_ref,
             qc_ref, kc_ref, vv_ref):
        ql = ql_ref[...]
        qn = (ql * _rstd(ql) * qw_ref[...]).astype(BF16)
        q = _dot_nt(qn, wuq_ref[...])
        cs, sn = cos_ref[...], sin_ref[...]
        qr = q[:, 512:768]
        qr = qr * cs + _swap_halves(qr) * sn
        half = lax.broadcasted_iota(jnp.int32, (tm, 128), 1) // 64
        kl = kl_ref[...]
        kvn = (kl * _rstd(kl) * kw_ref[...]).astype(BF16)
        kr = kr_ref[...]
        kr = kr * cs[:, :128] + _swap_halves(kr) * sn[:, :128]
        kr2 = (kr + pltpu.roll(kr, 64, 1)).astype(BF16)
        for h in range(MLA_HEADS):
            qc_ref[h, :, 0:128] = q[:, 128 * h:128 * h + 128].astype(BF16)
            chunk = qr[:, 128 * (h // 2):128 * (h // 2) + 128]
            qc_ref[h, :, 128:256] = jnp.where(half == (h % 2), chunk, 0.0).astype(BF16)
            kc_ref[h, :, 0:128] = _dot(kvn, wukv_ref[2 * h]).astype(BF16)
            kc_ref[h, :, 128:256] = kr2
            vv_ref[h] = _dot(kvn, wukv_ref[2 * h + 1]).astype(BF16)

    const = lambda shape: pl.BlockSpec(shape, lambda i: tuple(0 for _ in shape))
    return pl.pallas_call(
        body, name="mla_pre_fwd", grid=(S // tm,),
        in_specs=[pl.BlockSpec((tm, 256), lambda i: (i, 3)), pl.BlockSpec((tm, 128), lambda i: (i, 8)),
                  pl.BlockSpec((tm, 128), lambda i: (i, 9)), const((1, 256)), const((1, 128)),
                  const((768, 256)), const((8, 128, 128)),
                  pl.BlockSpec((tm, 256), lambda i: (i, 0)), pl.BlockSpec((tm, 256), lambda i: (i, 0))],
        out_specs=[pl.BlockSpec((MLA_HEADS, tm, 256), lambda i: (0, i, 0)),
                   pl.BlockSpec((MLA_HEADS, tm, 256), lambda i: (0, i, 0)),
                   pl.BlockSpec((MLA_HEADS, tm, 128), lambda i: (0, i, 0))],
        out_shape=[jax.ShapeDtypeStruct((MLA_HEADS, S, 256), BF16), jax.ShapeDtypeStruct((MLA_HEADS, S, 256), BF16),
                   jax.ShapeDtypeStruct((MLA_HEADS, S, 128), BF16)],
        compiler_params=_params(("parallel",)),
    )(proj, proj, proj, qn_w, kvn_w, wuqT, wukv, cos, sin)


def _causal(i, j, t):
    row = i * t + lax.broadcasted_iota(jnp.int32, (t, t), 0)
    col = j * t + lax.broadcasted_iota(jnp.int32, (t, t), 1)
    return col <= row


def _mla_attn_fwd(qc, kc, vv, *, t=256):
    S = qc.shape[1]
    t = min(t, S)

    def body(q_ref, k_ref, v_ref, o_ref, l_ref):
        i = pl.program_id(0)
        diag = _causal(0, 0, t)

        def step(j, carry, masked):
            rows = pl.ds(pl.multiple_of(j * t, t), t)
            out = []
            for h in range(MLA_HEADS):
                m, l, acc = carry[h]
                s = _dot_nt(q_ref[h], k_ref[h, rows, :]) * MLA_SCALE
                if masked:
                    s = jnp.where(diag, s, -jnp.inf)
                m_new = jnp.maximum(m, jnp.max(s, axis=-1, keepdims=True))
                alpha = jnp.exp(m - m_new)
                p = jnp.exp(s - m_new)
                l = alpha * l + jnp.sum(p, axis=-1, keepdims=True)
                acc = alpha * acc + _dot(p.astype(BF16), v_ref[h, rows, :])
                out.append((m_new, l, acc))
            return tuple(out)

        init = tuple((jnp.full((t, 1), -jnp.inf, F32), jnp.zeros((t, 1), F32), jnp.zeros((t, MLA_V), F32))
                     for _ in range(MLA_HEADS))
        carry = lax.fori_loop(0, i, lambda j, c: step(j, c, False), init)
        carry = step(i, carry, True)
        for h in range(MLA_HEADS):
            m, l, acc = carry[h]
            o_ref[:, 128 * h:128 * h + 128] = acc / l
            l_ref[h] = jnp.broadcast_to(m + jnp.log(l), (t, 128))

    return pl.pallas_call(
        body, name="mla_attn_fwd", grid=(S // t,),
        in_specs=[pl.BlockSpec((MLA_HEADS, t, 256), lambda i: (0, i, 0)),
                  pl.BlockSpec((MLA_HEADS, S, 256), lambda i: (0, 0, 0)),
                  pl.BlockSpec((MLA_HEADS, S, 128), lambda i: (0, 0, 0))],
        out_specs=[pl.BlockSpec((t, 512), lambda i: (i, 0)),
                   pl.BlockSpec((MLA_HEADS, t, 128), lambda i: (0, i, 0))],
        out_shape=[jax.ShapeDtypeStruct((S, 512), F32), jax.ShapeDtypeStruct((MLA_HEADS, S, 128), F32)],
        compiler_params=_params(("parallel",)),
    )(qc, kc, vv)


def _mla_attn_bwd(qc, kc, vv, o, lse, do, *, t=256, tq=512):
    S = qc.shape[1]
    t = min(t, S)
    tq = min(tq, S)
    nblk = S // t
    hp = MLA_HEADS
    once = pl.Buffered(1)

    def body(q_ref, k_ref, v_ref, o_ref, l_ref, do_ref, dq_ref, dk_ref, dv_ref):
        j = pl.program_id(1)

        @pl.when(j == 0)
        def _():
            dq_ref[...] = jnp.zeros_like(dq_ref)

        first = (j * t) // tq

        def step(i, carry, masked):
            rows = pl.ds(pl.multiple_of(i * tq, tq), tq)
            if masked:
                row = i * tq + lax.broadcasted_iota(jnp.int32, (tq, t), 0)
                col = j * t + lax.broadcasted_iota(jnp.int32, (tq, t), 1)
                visible = col <= row
            out = []
            for h in range(hp):
                dk, dv = carry[h]
                k = k_ref[h]
                q = q_ref[h, rows, :]
                dov = do_ref[rows, 128 * h:128 * h + 128]
                lrow = l_ref[h, rows, :][:, 0:1]
                p = jnp.exp(_dot_nt(q, k) * MLA_SCALE - lrow)
                if masked:
                    p = jnp.where(visible, p, 0.0)
                dob = dov.astype(BF16)
                dv = dv + _dot_tn(p.astype(BF16), dob)
                dp = _dot_nt(dob, v_ref[h])
                delta = jnp.sum(dov * o_ref[rows, 128 * h:128 * h + 128], axis=-1, keepdims=True)
                ds = (p * (dp - delta) * MLA_SCALE).astype(BF16)
                dk = dk + _dot_tn(ds, q)
                dq_ref[h, rows, :] += _dot(ds, k)
                out.append((dk, dv))
            return tuple(out)

        init = tuple((jnp.zeros((t, 256), F32), jnp.zeros((t, MLA_V), F32)) for _ in range(hp))
        carry = step(first, init, True)
        carry = lax.fori_loop(first + 1, S // tq, lambda i, c: step(i, c, False), carry)
        for h in range(hp):
            dk_ref[h] = carry[h][0]
            dv_ref[h] = carry[h][1]

    return pl.pallas_call(
        body, name="mla_attn_bwd", grid=(MLA_HEADS // hp, nblk),
        in_specs=[pl.BlockSpec((hp, S, 256), lambda g, j: (g, 0, 0), pipeline_mode=once),
                  pl.BlockSpec((hp, t, 256), lambda g, j: (g, j, 0)),
                  pl.BlockSpec((hp, t, 128), lambda g, j: (g, j, 0)),
                  pl.BlockSpec((S, 128 * hp), lambda g, j: (0, g), pipeline_mode=once),
                  pl.BlockSpec((hp, S, 128), lambda g, j: (g, 0, 0), pipeline_mode=once),
                  pl.BlockSpec((S, 128 * hp), lambda g, j: (0, g), pipeline_mode=once)],
        out_specs=[pl.BlockSpec((hp, S, 256), lambda g, j: (g, 0, 0)),
                   pl.BlockSpec((hp, t, 256), lambda g, j: (g, j, 0)),
                   pl.BlockSpec((hp, t, 128), lambda g, j: (g, j, 0))],
        out_shape=[jax.ShapeDtypeStruct((MLA_HEADS, S, 256), F32), jax.ShapeDtypeStruct((MLA_HEADS, S, 256), F32),
                   jax.ShapeDtypeStruct((MLA_HEADS, S, 128), F32)],
        compiler_params=_params(("parallel", "arbitrary")),
    )(qc, kc, vv, o, lse, do)


def _mla_pre_bwd(proj, qn_w, kvn_w, wuqT, wukv, cos, sin, dqc, dkc, dvv, *, tm=256):
    S = proj.shape[0]

    def body(ql_ref, kl_ref, qw_ref, kw_ref, wuq_ref, wukv_ref, cos_ref, sin_ref, dqc_ref, dkc_ref, dvv_ref,
             dql_ref, dkl_ref, dkr_ref, gq_ref, gkv_ref, part_ref):
        @pl.when(pl.program_id(0) == 0)
        def _():
            gq_ref[...] = jnp.zeros_like(gq_ref)
            gkv_ref[...] = jnp.zeros_like(gkv_ref)
            part_ref[...] = jnp.zeros_like(part_ref)

        cs, sn = cos_ref[...], sin_ref[...]
        half = lax.broadcasted_iota(jnp.int32, (tm, 128), 1) // 64
        ql = ql_ref[...]
        rq = _rstd(ql)
        qhat = ql * rq
        qw = qw_ref[...]
        qn = (qhat * qw).astype(BF16)
        chunks = []
        for pair in range(2):
            chunks.append(jnp.where(half == 0, dqc_ref[2 * pair, :, 128:256], dqc_ref[2 * pair + 1, :, 128:256]))
        dqr = jnp.concatenate(chunks, axis=1)
        dqr = dqr * cs + _swap_halves(dqr * sn)
        dq = jnp.concatenate([dqc_ref[h, :, 0:128] for h in range(MLA_HEADS)] + [dqr], axis=1).astype(BF16)
        gq_ref[...] += _dot_tn(dq, qn)
        dqn = _dot(dq, wuq_ref[...])
        part_ref[0:1, :] += jnp.sum(dqn * qhat, axis=0, keepdims=True)
        dql_ref[...] = _rms_bwd(dqn * qw, qhat, rq)
        kl = kl_ref[...]
        rk = _rstd(kl)
        khat = kl * rk
        kw = kw_ref[...]
        kvn = (khat * kw).astype(BF16)
        dkvn = jnp.zeros((tm, MLA_KVR), F32)
        dkr2 = jnp.zeros((tm, 128), F32)
        for h in range(MLA_HEADS):
            dkn = dkc_ref[h, :, 0:128].astype(BF16)
            dvh = dvv_ref[h].astype(BF16)
            gkv_ref[2 * h] += _dot_tn(kvn, dkn)
            gkv_ref[2 * h + 1] += _dot_tn(kvn, dvh)
            dkvn += _dot_nt(dkn, wukv_ref[2 * h]) + _dot_nt(dvh, wukv_ref[2 * h + 1])
            dkr2 += dkc_ref[h, :, 128:256]
        part_ref[1:2, 0:128] += jnp.sum(dkvn * khat, axis=0, keepdims=True)
        dkl_ref[...] = _rms_bwd(dkvn * kw, khat, rk)
        dkr = jnp.where(half == 0, dkr2 + pltpu.roll(dkr2, 64, 1), 0.0)
        dkr_ref[...] = dkr * cs[:, :128] + _swap_halves(dkr * sn[:, :128])

    const = lambda shape: pl.BlockSpec(shape, lambda i: tuple(0 for _ in shape))
    heads = lambda w: pl.BlockSpec((MLA_HEADS, tm, w), lambda i: (0, i, 0))
    return pl.pallas_call(
        body, name="mla_pre_bwd", grid=(S // tm,),
        in_specs=[pl.BlockSpec((tm, 256), lambda i: (i, 3)), pl.BlockSpec((tm, 128), lambda i: (i, 8)),
                  const((1, 256)), const((1, 128)), const((768, 256)), const((8, 128, 128)),
                  pl.BlockSpec((tm, 256), lambda i: (i, 0)), pl.BlockSpec((tm, 256), lambda i: (i, 0)),
                  heads(256), heads(256), heads(128)],
        out_specs=[pl.BlockSpec((tm, 256), lambda i: (i, 0)), pl.BlockSpec((tm, 128), lambda i: (i, 0)),
                   pl.BlockSpec((tm, 128), lambda i: (i, 0)), const((768, 256)), const((8, 128, 128)), const((8, 256))],
        out_shape=[jax.ShapeDtypeStruct((S, 256), F32), jax.ShapeDtypeStruct((S, 128), F32),
                   jax.ShapeDtypeStruct((S, 128), F32), jax.ShapeDtypeStruct((768, 256), F32),
                   jax.ShapeDtypeStruct((8, 128, 128), F32), jax.ShapeDtypeStruct((8, 256), F32)],
        compiler_params=_params(("arbitrary",)),
    )(proj, proj, qn_w, kvn_w, wuqT, wukv, cos, sin, dqc, dkc, dvv)


def _mix_out_fwd(x, oa, ob, w_o, vecs, *, tm=256):
    S = x.shape[0]

    def body(x_ref, oa_ref, ob_ref, w_ref, vec_ref, xo_ref, mo_ref):
        mo = _dot(oa_ref[...].astype(BF16), w_ref[0:512, :]) + _dot(ob_ref[...].astype(BF16), w_ref[512:1024, :])
        mo_ref[...] = mo
        xo_ref[...] = x_ref[...] + vec_ref[3:4, :] * mo

    row = pl.BlockSpec((tm, D), lambda i: (i, 0))
    half = pl.BlockSpec((tm, 512), lambda i: (i, 0))
    return pl.pallas_call(
        body, name="mix_out_fwd", grid=(S // tm,),
        in_specs=[row, half, half, pl.BlockSpec((D, D), lambda i: (0, 0)), pl.BlockSpec((8, D), lambda i: (0, 0))],
        out_specs=[row, row],
        out_shape=[jax.ShapeDtypeStruct((S, D), F32), jax.ShapeDtypeStruct((S, D), F32)],
        compiler_params=_params(("parallel",)),
    )(x, oa, ob, w_o, vecs)


def _mix_out_bwd(dxo, mo, oa, ob, w_o, vecs, *, tm=256):
    S = dxo.shape[0]

    def body(dx_ref, mo_ref, oa_ref, ob_ref, w_ref, vec_ref, doa_ref, dob_ref, gw_ref, part_ref):
        @pl.when(pl.program_id(0) == 0)
        def _():
            gw_ref[...] = jnp.zeros_like(gw_ref)
            part_ref[...] = jnp.zeros_like(part_ref)

        dx = dx_ref[...]
        part_ref[0:1, :] += jnp.sum(dx * mo_ref[...], axis=0, keepdims=True)
        dmo = (vec_ref[3:4, :] * dx).astype(BF16)
        doa_ref[...] = _dot_nt(dmo, w_ref[0:512, :])
        dob_ref[...] = _dot_nt(dmo, w_ref[512:1024, :])
        gw_ref[0:512, :] += _dot_tn(oa_ref[...].astype(BF16), dmo)
        gw_ref[512:1024, :] += _dot_tn(ob_ref[...].astype(BF16), dmo)

    row = pl.BlockSpec((tm, D), lambda i: (i, 0))
    half = pl.BlockSpec((tm, 512), lambda i: (i, 0))
    return pl.pallas_call(
        body, name="mix_out_bwd", grid=(S // tm,),
        in_specs=[row, row, half, half, pl.BlockSpec((D, D), lambda i: (0, 0)), pl.BlockSpec((8, D), lambda i: (0, 0))],
        out_specs=[half, half, pl.BlockSpec((D, D), lambda i: (0, 0)), pl.BlockSpec((8, D), lambda i: (0, 0))],
        out_shape=[jax.ShapeDtypeStruct((S, 512), F32), jax.ShapeDtypeStruct((S, 512), F32),
                   jax.ShapeDtypeStruct((D, D), F32), jax.ShapeDtypeStruct((8, D), F32)],
        compiler_params=_params(("arbitrary",)),
    )(dxo, mo, oa, ob, w_o, vecs)


def _mix_in_bwd(h, w_inT, dq, dk, dv, dql, dkl, dkr, *, tm=256):
    S = h.shape[0]
    offs = (0, 512, 640, 768, 1024, 1152)
    wid = (512, 128, 128, 256, 128, 128)

    def body(h_ref, w_ref, dq_ref, dk_ref, dv_ref, dql_ref, dkl_ref, dkr_ref, dh_ref, gw_ref):
        @pl.when(pl.program_id(0) == 0)
        def _():
            gw_ref[...] = jnp.zeros_like(gw_ref)

        hv = h_ref[...]
        dh = jnp.zeros((tm, D), F32)
        for ref, o, w in zip((dq_ref, dk_ref, dv_ref, dql_ref, dkl_ref, dkr_ref), offs, wid):
            w = min(w, D_IN - o)
            dpart = ref[...][:, :w].astype(BF16)
            dh += _dot(dpart, w_ref[o:o + w, :])
            gw_ref[o:o + w, :] += _dot_tn(dpart, hv)
        dh_ref[...] = dh

    row = pl.BlockSpec((tm, D), lambda i: (i, 0))
    part = lambda w: pl.BlockSpec((tm, w), lambda i: (i, 0))
    return pl.pallas_call(
        body, name="mix_in_bwd", grid=(S // tm,),
        in_specs=[row, pl.BlockSpec((D_IN_PAD, D), lambda i: (0, 0))] + [part(w) for w in wid],
        out_specs=[row, pl.BlockSpec((D_IN, D), lambda i: (0, 0))],
        out_shape=[jax.ShapeDtypeStruct((S, D), F32), jax.ShapeDtypeStruct((D_IN, D), F32)],
        compiler_params=_params(("arbitrary",)),
    )(h, w_inT, dq, dk, dv, dql, dkl, dkr)


def _vecs(norm_w, mod9, k):
    return jnp.concatenate([norm_w.reshape(1, D), mod9[3 * k:3 * k + 3], jnp.zeros((4, D), F32)], axis=0)


def _uq_group_rows(wuqT):
    per = MLA_NOPE + MLA_ROPE
    nope = [wuqT[per * h:per * h + MLA_NOPE] for h in range(MLA_HEADS)]
    rope = [wuqT[per * h + MLA_NOPE:per * (h + 1)] for h in range(MLA_HEADS)]
    return jnp.concatenate(nope + rope, axis=0)


def _uq_ungroup_rows(g):
    parts = []
    for h in range(MLA_HEADS):
        parts += [g[MLA_NOPE * h:MLA_NOPE * (h + 1)], g[512 + MLA_ROPE * h:512 + MLA_ROPE * (h + 1)]]
    return jnp.concatenate(parts, axis=0)


def _local_step(x, tgt, mod9, norms, sinks, rel_bias, q_norm, kv_norm, W, on_grads=None):
    if on_grads is None:
        on_grads = lambda group, grads, after, vecs: vecs
    S = x.shape[0]
    v1 = _vecs(norms["ffn1"], mod9, 0)
    v2 = _vecs(norms["mix"], mod9, 1)
    v3 = _vecs(norms["ffn2"], mod9, 2)
    bucket = jnp.asarray(_bucket_table())
    cos, sin = _rope_tables(S)
    if isinstance(W, dict):
        full, W = W, (lambda group, after, vecs: (full, vecs))

    W1, v1 = W("ffn1", [], v1)
    x1, h1, a1, b1, f1 = _ffn_fwd(x, v1, W1["g1T"], W1["u1T"], W1["d1"], name="ffn1_fwd", tm=256, tf=D_FF)
    W2, v2 = W("mixer", [x1], v2)
    w_inT = jnp.pad(W2["w_inT"], ((0, D_IN_PAD - D_IN), (0, 0))).astype(BF16)
    wuqT = _uq_group_rows(W2["w_uqT"])
    h2, proj = _mix_in_fwd(x1, v2, w_inT)
    bias = _bias_build(rel_bias, bucket)
    oa = _swa_fwd(proj, bias, sinks)
    qc, kc, vv = _mla_pre_fwd(proj, q_norm, kv_norm, wuqT, W2["w_ukv"], cos, sin)
    ob, lse = _mla_attn_fwd(qc, kc, vv)
    _, v2o = W("ffn2_on_its_way", [ob], v2)
    x2, mo = _mix_out_fwd(x1, oa, ob, W2["w_o"], v2o)
    W3, v3 = W("ffn2", [x2], v3)
    x3, h3, a3, b3, f3 = _ffn_fwd(x2, v3, W3["g3T"], W3["u3T"], W3["d3"], name="ffn2_fwd", tm=256, tf=D_FF)
    dx3, head_part, df3 = _head(x3, tgt, norms["final"], f3, v3)

    gg3, gu3, gd3, dh3 = _ffn_bwd_main(h3, df3, a3, b3, W3["g3T"], W3["u3T"], W3["d3"], name="ffn2_bwd", tm=2048, tf=256)
    ffn2 = {"g3T": gg3, "u3T": gu3, "d3": gd3}
    v3 = on_grads("ffn2", ffn2, [], v3)
    dx2, n3_part = _norm_bwd(dh3, x2, dx3, v3, name="ffn2_norm_bwd")
    v2 = on_grads("ffn2", None, [dx2], v2)
    doa, dob, g_wo, g2_part = _mix_out_bwd(dx2, mo, oa, ob, W2["w_o"], v2)
    dq, dk, dv, drb, dsk = _swa_bwd(proj, bias, sinks, oa, doa, bucket)
    dqc, dkc, dvv = _mla_attn_bwd(qc, kc, vv, ob, lse, dob)
    dql, dkl, dkr, g_uq, g_ukv, mla_part = _mla_pre_bwd(proj, q_norm, kv_norm, wuqT, W2["w_ukv"], cos, sin, dqc, dkc, dvv)
    dh2, g_win = _mix_in_bwd(h2, w_inT, dq, dk, dv, dql, dkl, dkr)
    mixer = {"w_inT": g_win, "w_uqT": _uq_ungroup_rows(g_uq).astype(BF16),
             "w_ukv": g_ukv.astype(BF16), "w_o": g_wo.astype(BF16)}
    v2 = on_grads("mixer", mixer, [], v2)
    dx1, n2_part, df1 = _norm_bwd(dh2, x1, dx2, v2, name="mix_norm_bwd", below=(f1, v1))
    started = on_grads("mixer", None, [dx1], jnp.zeros((1, 1), F32))
    gg1, gu1, gd1, dh1 = _ffn_bwd_main(h1, df1, a1, b1, W1["g1T"], W1["u1T"], W1["d1"], name="ffn1_bwd",
                                       after=[started], tm=2048, tf=256)
    ffn1 = {"g1T": gg1, "u1T": gu1, "d1": gd1}
    v1 = on_grads("ffn1", ffn1, [], v1)
    dx0, n1_part = _norm_bwd(dh1, x, dx1, v1, name="ffn1_norm_bwd")

    grads = {**ffn1, **ffn2, **mixer}
    dmod9 = jnp.concatenate([n1_part[1:3], n2_part[3:4], n2_part[1:3], g2_part[0:1],
                             n3_part[1:3], head_part[3:4]], axis=0)
    small = jnp.concatenate([n1_part[0], n2_part[0], n3_part[0], head_part[0], mla_part[0],
                             mla_part[1, :128], dsk[:, 0], head_part[1, 0:1], jnp.zeros((119,), F32), drb[:, 0]])
    return head_part[1, 0], dx0, grads, small, dmod9


SMALL_LAYOUT = (("norm_ffn1", 1024), ("norm_mix", 1024), ("norm_ffn2", 1024), ("norm_final", 1024),
                ("q_norm", 256), ("kv_norm", 128), ("sinks", 128), ("rel_bias", 256))
N_SMALL = sum(n for _, n in SMALL_LAYOUT)
LOSS_SLOT = 4 * 1024 + 256 + 128 + SWA_HEADS


def _coords():
    return lax.axis_index("x"), lax.axis_index("y"), lax.axis_index("c")


def _flip(v, bit):
    return 1 - v if bit else v


def _peer(r):
    x, y, c = _coords()
    return (_flip(x, r & 4), _flip(y, r & 2), _flip(c, r & 1))


def _mod_fwd(c_tile, w_mod, b_mod3):
    W = w_mod.shape[1]

    def body(c_ref, w_ref, b_ref, mod_ref, ca_ref, call_ref, part_ref, send_sems, recv_sems):
        x, y, c = _coords()
        me = 4 * x + 2 * y + c
        call_ref[me] = c_ref[...]
        sends = []
        for r in range(1, N_DEV):
            cp = pltpu.make_async_remote_copy(c_ref, call_ref.at[me], send_sems.at[0, r], recv_sems.at[0, r],
                                              device_id=_peer(r), device_id_type=MESH)
            cp.start()
            sends.append(cp)
        for r in range(1, N_DEV):
            pltpu.make_async_remote_copy(c_ref, call_ref.at[me], send_sems.at[0, r], recv_sems.at[0, r],
                                         device_id=_peer(r), device_id_type=MESH).wait_recv()
        cv = call_ref[...].reshape(8 * N_DEV, D)
        ca = (cv * _sigmoid(cv)).astype(BF16)
        ca_ref[...] = ca
        part_ref[...] = _dot(ca, w_ref[...].astype(BF16)).reshape(N_DEV, 8, W)
        mod_ref[me] = part_ref[me] + b_ref[me]
        for r in range(1, N_DEV):
            cp = pltpu.make_async_remote_copy(part_ref.at[me ^ r], mod_ref.at[me], send_sems.at[1, r],
                                              recv_sems.at[1, r], device_id=_peer(r), device_id_type=MESH)
            cp.start()
            sends.append(cp)
        for r in range(1, N_DEV):
            pltpu.make_async_remote_copy(part_ref.at[me ^ r], mod_ref.at[me], send_sems.at[1, r],
                                         recv_sems.at[1, r], device_id=_peer(r), device_id_type=MESH).wait_recv()
            mod_ref[me ^ r] = mod_ref[me ^ r] + b_ref[me ^ r]
        for cp in sends:
            cp.wait_send()

    vm = pl.BlockSpec(memory_space=pltpu.VMEM)
    return pl.pallas_call(
        body, name="mod_fwd", in_specs=[vm, vm, vm], out_specs=[vm, vm],
        out_shape=[jax.ShapeDtypeStruct((N_DEV, 8, W), F32), jax.ShapeDtypeStruct((8 * N_DEV, D), BF16)],
        scratch_shapes=[pltpu.VMEM((N_DEV, 8, D), F32), pltpu.VMEM((N_DEV, 8, W), F32),
                        pltpu.SemaphoreType.DMA((2, N_DEV)), pltpu.SemaphoreType.DMA((2, N_DEV))],
        compiler_params=_params(),
    )(c_tile, w_mod, b_mod3)


N_MODVEC = N_MOD * D
N_VEC = N_MODVEC + N_SMALL


def _mod_bwd(allvec, ca, me_idx):
    W = N_MODVEC // N_DEV

    def body(me_ref, all_ref, cols_ref, ca_ref, gw_ref, sum_ref):
        in_first_row = lax.broadcasted_iota(jnp.int32, (N_DEV, 8, W), 1) == 0
        dm = jnp.where(in_first_row, cols_ref[...], 0.0).reshape(8 * N_DEV, W)
        gw_ref[...] = _dot_tn(ca_ref[...], dm.astype(BF16))
        total = all_ref[0]
        for k in range(1, N_DEV):
            total = total + all_ref[k]
        sum_ref[...] = total

    return pl.pallas_call(
        body, name="mod_bwd",
        grid_spec=pltpu.PrefetchScalarGridSpec(
            num_scalar_prefetch=1, grid=(1,),
            in_specs=[pl.BlockSpec((N_DEV, 1, N_VEC), lambda i, me: (0, 0, 0)),
                      pl.BlockSpec((N_DEV, 1, W), lambda i, me: (0, 0, me[0])),
                      pl.BlockSpec((8 * N_DEV, D), lambda i, me: (0, 0))],
            out_specs=[pl.BlockSpec((D, W), lambda i, me: (0, 0)), pl.BlockSpec((1, N_VEC), lambda i, me: (0, 0))]),
        out_shape=[jax.ShapeDtypeStruct((D, W), F32), jax.ShapeDtypeStruct((1, N_VEC), F32)],
        compiler_params=_params(("arbitrary",)),
    )(me_idx, allvec, allvec, ca)


def _wgather(shards):
    n = len(shards)

    def body(*refs):
        ins, outs, token = refs[:n], refs[n:2 * n], refs[2 * n]
        send_sems, recv_sems, local_sems = refs[2 * n + 1:]
        token[...] = jnp.zeros_like(token)
        x, y, c = _coords()
        me = 4 * x + 2 * y + c
        sib = (x, y, 1 - c)
        chips = [(1 - x, y), (x, 1 - y), (1 - x, 1 - y)]

        def copy(k, slot, block, to, src=None):
            return pltpu.make_async_remote_copy(
                src_ref=outs[k].at[block] if src is None else src, dst_ref=outs[k].at[block],
                send_sem=send_sems.at[k, slot], recv_sem=recv_sems.at[k, slot], device_id=to, device_id_type=MESH)

        local = [pltpu.make_async_copy(ins[k], outs[k].at[me], local_sems.at[k]) for k in range(n)]
        for cp in local:
            cp.start()
        first = []
        for k in range(n):
            first.append(copy(k, 0, me, sib, src=ins[k]))
            for j, chip in enumerate(chips):
                first.append(copy(k, 1 + j, me, (*chip, c), src=ins[k]))
        for cp in first:
            cp.start()
        passed = []
        for j, (cx, cy) in enumerate(chips):
            for k in range(n):
                blk = 4 * cx + 2 * cy + c
                copy(k, 1 + j, blk, sib).wait_recv()
                cp = copy(k, 4 + j, blk, sib)
                cp.start()
                passed.append(cp)
        for k in range(n):
            copy(k, 0, 4 * x + 2 * y + (1 - c), sib).wait_recv()
            for j, (cx, cy) in enumerate(chips):
                copy(k, 4 + j, 4 * cx + 2 * cy + (1 - c), sib).wait_recv()
        for cp in first + passed:
            cp.wait_send()
        for cp in local:
            cp.wait()

    anyspec = pl.BlockSpec(memory_space=pl.ANY)
    return pl.pallas_call(
        body, name="wgather", in_specs=[anyspec] * n,
        out_specs=[anyspec] * n + [pl.BlockSpec(memory_space=pltpu.VMEM)],
        out_shape=[jax.ShapeDtypeStruct((N_DEV,) + s.shape, s.dtype) for s in shards]
        + [jax.ShapeDtypeStruct((8, 128), F32)],
        scratch_shapes=[pltpu.SemaphoreType.DMA((n, 7)), pltpu.SemaphoreType.DMA((n, 7)),
                        pltpu.SemaphoreType.DMA((n,))],
    )(*shards)


class _GatherCopies:
    def __init__(self, lands, send_sems, recv_sems, k0=0):
        x, y, c = _coords()
        me = 4 * x + 2 * y + c
        sib = (x, y, 1 - c)
        chips = [(1 - x, y), (x, 1 - y), (1 - x, 1 - y)]

        def copy(k, slot, block, to):
            return pltpu.make_async_remote_copy(
                src_ref=lands[k].at[block], dst_ref=lands[k].at[block],
                send_sem=send_sems.at[7 * (k0 + k) + slot], recv_sem=recv_sems.at[7 * (k0 + k) + slot],
                device_id=to, device_id_type=MESH)

        n = len(lands)
        self.first = [copy(k, 0, me, sib) for k in range(n)]
        self.first += [copy(k, 1 + j, me, (cx, cy, c)) for j, (cx, cy) in enumerate(chips) for k in range(n)]
        self.landed = [copy(k, 1 + j, 4 * cx + 2 * cy + c, sib) for j, (cx, cy) in enumerate(chips) for k in range(n)]
        self.passed = [copy(k, 4 + j, 4 * cx + 2 * cy + c, sib) for j, (cx, cy) in enumerate(chips) for k in range(n)]
        self.from_sib = [copy(k, 0, 4 * x + 2 * y + (1 - c), sib) for k in range(n)]
        self.from_sib += [copy(k, 4 + j, 4 * cx + 2 * cy + (1 - c), sib) for j, (cx, cy) in enumerate(chips)
                          for k in range(n)]


def _gather_start(lands, *, name):
    n = len(lands)

    def body(*refs):
        for cp in _GatherCopies(refs[:n], refs[n], refs[n + 1]).first:
            cp.start()
        refs[-1][...] = jnp.zeros_like(refs[-1])

    out = pl.pallas_call(
        body, name=name,
        out_shape=(pltpu.SemaphoreType.DMA((7 * n,)), pltpu.SemaphoreType.DMA((7 * n,)),
                   *[pltpu.HBM(l.shape, l.dtype) for l in lands], jax.ShapeDtypeStruct((8, 128), F32)),
        in_specs=[HBM_SPEC] * n,
        out_specs=(SEM_SPEC, SEM_SPEC, *[HBM_SPEC] * n, pl.BlockSpec(memory_space=pltpu.VMEM)),
        input_output_aliases={i: 2 + i for i in range(n)},
        compiler_params=pltpu.CompilerParams(has_side_effects=DATAFLOW),
    )(*[_in_hbm(l) for l in lands])
    return out[0], out[1], list(out[2:2 + n]), out[-1]


def _gather_pass(send_sems, recv_sems, lands, after, *, name, stage, k0=0):
    n = len(lands)

    def body(*refs):
        cps = _GatherCopies(refs[:n], refs[n], refs[n + 1], k0)
        if stage == "landed":
            for cp in cps.landed:
                cp.wait_recv()
        else:
            for cp in cps.passed:
                cp.start()
        refs[-1][...] = jnp.zeros_like(refs[-1])

    out = pl.pallas_call(
        body, name=name,
        out_shape=(*[pltpu.HBM(l.shape, l.dtype) for l in lands], jax.ShapeDtypeStruct((8, 128), F32)),
        in_specs=[HBM_SPEC] * n + [SEM_SPEC, SEM_SPEC] + [pl.BlockSpec(memory_space=pl.ANY)] * len(after),
        out_specs=(*[HBM_SPEC] * n, pl.BlockSpec(memory_space=pltpu.VMEM)),
        input_output_aliases={i: i for i in range(n)},
        compiler_params=pltpu.CompilerParams(has_side_effects=DATAFLOW),
    )(*lands, send_sems, recv_sems, *after)
    return list(out[:n]), out[-1]


def _gather_end(send_sems, recv_sems, lands, after, *, name, k0=0):
    n = len(lands)

    def body(*refs):
        cps = _GatherCopies(refs[:n], refs[n], refs[n + 1], k0)
        for cp in cps.from_sib:
            cp.wait_recv()
        for cp in cps.first + cps.passed:
            cp.wait_send()

    out = pl.pallas_call(
        body, name=name,
        out_shape=[pltpu.HBM(l.shape, l.dtype) for l in lands],
        in_specs=[HBM_SPEC] * n + [SEM_SPEC, SEM_SPEC] + [pl.BlockSpec(memory_space=pl.ANY)] * len(after),
        out_specs=[HBM_SPEC] * n,
        input_output_aliases={i: i for i in range(n)},
        compiler_params=pltpu.CompilerParams(has_side_effects=DATAFLOW),
    )(*lands, send_sems, recv_sems, *after)
    return list(out)


def _d2d_copies(grads, lands, send_sems, recv_sems):
    x, y, c = _coords()
    return [pltpu.make_async_remote_copy(
        src_ref=grads[k].at[2 * q + (1 - c)], dst_ref=lands[k].at[q],
        send_sem=send_sems.at[4 * k + q], recv_sem=recv_sems.at[4 * k + q],
        device_id=(x, y, 1 - c), device_id_type=MESH) for k in range(len(grads)) for q in range(4)]


def _vec_copies(srcs, lands, send_sems, recv_sems):
    x, y, c = _coords()
    me = 4 * x + 2 * y + c
    return [pltpu.make_async_remote_copy(
        src_ref=lands[0].at[me], dst_ref=lands[0].at[me], send_sem=send_sems.at[r - 1], recv_sem=recv_sems.at[r - 1],
        device_id=_peer(r), device_id_type=MESH) for r in range(1, N_DEV)]


def _chipsum(gs, sibs, cidx, *, name):
    n = len(gs)

    def body(c_ref, *refs):
        for k in range(n):
            refs[2 * n + k][...] = (refs[k][...].astype(F32) + refs[n + k][...].astype(F32)).astype(refs[2 * n + k].dtype)

    mine = [pl.BlockSpec((1,) + g.shape[1:], lambda q, c_ref: (2 * q + c_ref[0], 0, 0)) for g in gs]
    other = [pl.BlockSpec((1,) + g.shape[1:], lambda q, c_ref: (q, 0, 0)) for g in gs]
    return pl.pallas_call(
        body, name=name,
        grid_spec=pltpu.PrefetchScalarGridSpec(num_scalar_prefetch=1, grid=(4,), in_specs=mine + other, out_specs=other),
        out_shape=[jax.ShapeDtypeStruct((4,) + g.shape[1:], g.dtype) for g in gs],
        compiler_params=_params(("arbitrary",)),
    )(cidx, *gs, *sibs)


HBM_SPEC = pl.BlockSpec(memory_space=pltpu.HBM)
SEM_SPEC = pl.BlockSpec(memory_space=pltpu.SEMAPHORE)
DATAFLOW = pltpu.SideEffectType.DATAFLOW_SIDE_EFFECTING


def _in_hbm(a):
    return pltpu.with_memory_space_constraint(a, pltpu.HBM)


def _ici_copies(sums, lands, send_sems, recv_sems):
    x, y, c = _coords()
    chips = [(1 - x, y), (x, 1 - y), (1 - x, 1 - y)]
    cps = []
    for k in range(len(sums)):
        for j, (cx, cy) in enumerate(chips):
            cps.append(pltpu.make_async_remote_copy(
                src_ref=sums[k].at[2 * cx + cy], dst_ref=lands[k].at[j],
                send_sem=send_sems.at[3 * k + j], recv_sem=recv_sems.at[3 * k + j],
                device_id=(cx, cy, c), device_id_type=MESH))
    return cps


def _split_start(copies, srcs, lands, n_sems, after, *, name):
    ns, nl = len(srcs), len(lands)

    def body(*refs):
        for cp in copies(refs[:ns], refs[ns:ns + nl], refs[ns + nl + len(after)], refs[ns + nl + len(after) + 1]):
            cp.start()
        refs[-1][...] = jnp.zeros_like(refs[-1])

    bufs = [_in_hbm(a) for a in list(srcs) + list(lands)]
    out = pl.pallas_call(
        body, name=name,
        out_shape=(pltpu.SemaphoreType.DMA((n_sems,)), pltpu.SemaphoreType.DMA((n_sems,)),
                   *[pltpu.HBM(a.shape, a.dtype) for a in bufs], jax.ShapeDtypeStruct((8, 128), F32)),
        in_specs=[HBM_SPEC] * len(bufs) + [pl.BlockSpec(memory_space=pl.ANY)] * len(after),
        out_specs=(SEM_SPEC, SEM_SPEC, *[HBM_SPEC] * len(bufs), pl.BlockSpec(memory_space=pltpu.VMEM)),
        input_output_aliases={i: 2 + i for i in range(len(bufs))},
        compiler_params=pltpu.CompilerParams(has_side_effects=DATAFLOW),
    )(*bufs, *after)
    return out[0], out[1], list(out[2:2 + ns]), list(out[2 + ns:2 + ns + nl]), out[-1]


def _split_wait(copies, send_sems, recv_sems, srcs, lands, after, *, name):
    ns, nl = len(srcs), len(lands)

    def body(*refs):
        for cp in copies(refs[:ns], refs[ns:ns + nl], refs[ns + nl], refs[ns + nl + 1]):
            cp.wait_send()
            cp.wait_recv()

    out = pl.pallas_call(
        body, name=name,
        out_shape=[pltpu.HBM(a.shape, a.dtype) for a in list(srcs) + list(lands)],
        in_specs=[HBM_SPEC] * (ns + nl) + [SEM_SPEC, SEM_SPEC] + [pl.BlockSpec(memory_space=pl.ANY)] * len(after),
        out_specs=[HBM_SPEC] * (ns + nl),
        input_output_aliases={i: i for i in range(ns + nl)},
        compiler_params=pltpu.CompilerParams(has_side_effects=DATAFLOW),
    )(*srcs, *lands, send_sems, recv_sems, *after)
    return list(out[:ns]), list(out[ns:])


ADAM_C1 = 1.0 / (1.0 - ADAM_B1 ** ADAM_STEP)
ADAM_C2 = 1.0 / (1.0 - ADAM_B2 ** ADAM_STEP)


def _adam_math(w, g, m, v):
    m2 = ADAM_B1 * m + (1.0 - ADAM_B1) * g
    v2 = ADAM_B2 * v + (1.0 - ADAM_B2) * (g * g)
    return -ADAM_LR * ((m2 * ADAM_C1) / (jnp.sqrt(v2 * ADAM_C2) + ADAM_EPS) + ADAM_WD * w), m2, v2


def _adamw(w, g, m, v, *, name):
    R, C = w.shape
    tr = R if R <= 512 else 256

    def body(w_ref, g_ref, m_ref, v_ref, d_ref, nm_ref, nv_ref):
        d_ref[...], nm_ref[...], nv_ref[...] = _adam_math(w_ref[...], g_ref[...], m_ref[...], v_ref[...])

    blk = pl.BlockSpec((tr, C), lambda i: (i, 0))
    return pl.pallas_call(
        body, name=name, grid=(R // tr,), in_specs=[blk] * 4, out_specs=[blk] * 3,
        out_shape=[jax.ShapeDtypeStruct((R, C), F32)] * 3,
        compiler_params=_params(("parallel",)),
    )(w, g, m, v)


def _adamw_rs(wmv, cs, rcv, qidx, *, name):
    n = len(wmv)
    r, cc = wmv[0][0].shape
    tr = r // 2 if r % 32 == 0 and r > 128 else r

    def body(q_ref, *refs):
        ins, outs = refs[:5 * n], refs[5 * n:]
        for k in range(n):
            w_ref, m_ref, v_ref, c_ref, r_ref = ins[5 * k:5 * k + 5]
            g_ref, d_ref, nm_ref, nv_ref = outs[4 * k:4 * k + 4]
            g = ((c_ref[0].astype(F32) + r_ref[0].astype(F32)) + r_ref[1].astype(F32)) + r_ref[2].astype(F32)
            g_ref[...] = g
            d_ref[...], nm_ref[...], nv_ref[...] = _adam_math(w_ref[...], g, m_ref[...], v_ref[...])

    blk = pl.BlockSpec((tr, cc), lambda i, q_ref: (i, 0))
    one = [blk, blk, blk, pl.BlockSpec((1, tr, cc), lambda i, q_ref: (q_ref[0], i, 0)),
           pl.BlockSpec((3, tr, cc), lambda i, q_ref: (0, i, 0))]
    out = pl.pallas_call(
        body, name=name,
        grid_spec=pltpu.PrefetchScalarGridSpec(num_scalar_prefetch=1, grid=(r // tr,), in_specs=one * n,
                                               out_specs=[blk] * (4 * n)),
        out_shape=[jax.ShapeDtypeStruct((r, cc), F32)] * (4 * n),
        compiler_params=_params(("arbitrary",)),
    )(qidx, *[a for (w, m, v), c, rc in zip(wmv, cs, rcv) for a in (w, m, v, c, rc)])
    return [tuple(out[4 * k:4 * k + 4]) for k in range(n)]


SMALL_PARAMS = ("norm_ffn1", "norm_mix", "norm_ffn2", "norm_final", "q_norm", "kv_norm", "sinks", "rel_bias", "b_mod")


def _adamw_small(gvec, wmv):
    widths = [wmv[3 * i].shape[1] for i in range(len(SMALL_PARAMS))]

    def body(*refs):
        g_all = refs[0]
        ins = refs[1:1 + 3 * len(SMALL_PARAMS)]
        outs = refs[1 + 3 * len(SMALL_PARAMS):]
        off = N_MODVEC
        for i, name in enumerate(SMALL_PARAMS):
            g_ref, d_ref, nm_ref, nv_ref = outs[4 * i:4 * i + 4]
            w_ref, m_ref, v_ref = ins[3 * i:3 * i + 3]
            start = 0 if name == "b_mod" else off
            g = g_all[:, start:start + widths[i]]
            g_ref[...] = g
            d_ref[...], nm_ref[...], nv_ref[...] = _adam_math(w_ref[...], g, m_ref[...], v_ref[...])
            if name != "b_mod":
                off += dict(SMALL_LAYOUT)[name]

    vm = pl.BlockSpec(memory_space=pltpu.VMEM)
    n_out = 4 * len(SMALL_PARAMS)
    out = pl.pallas_call(
        body, name="adamw_small", in_specs=[vm] * (1 + len(wmv)), out_specs=[vm] * n_out,
        out_shape=[jax.ShapeDtypeStruct((1, widths[i // 4]), F32) for i in range(n_out)],
        compiler_params=_params(),
    )(gvec, *wmv)
    return {name: out[4 * i:4 * i + 4] for i, name in enumerate(SMALL_PARAMS)}


TRANSPOSED = ("g1T", "u1T", "g3T", "u3T", "w_inT", "w_uqT")


def kernel(x, c, w_mod, b_mod, norm_ffn1, ffn1_gate, ffn1_up, ffn1_down, norm_mix, w_in, q_norm, kv_norm, w_uq, w_ukv, sinks, w_o, norm_ffn2, ffn2_gate, ffn2_up, ffn2_down, rel_bias, norm_final, loss_target, m_w_mod, m_b_mod, m_norm_ffn1, m_ffn1_gate, m_ffn1_up, m_ffn1_down, m_norm_mix, m_w_in, m_q_norm, m_kv_norm, m_w_uq, m_w_ukv, m_sinks, m_w_o, m_norm_ffn2, m_ffn2_gate, m_ffn2_up, m_ffn2_down, m_rel_bias, m_norm_final, v_w_mod, v_b_mod, v_norm_ffn1, v_ffn1_gate, v_ffn1_up, v_ffn1_down, v_norm_mix, v_w_in, v_q_norm, v_kv_norm, v_w_uq, v_w_ukv, v_sinks, v_w_o, v_norm_ffn2, v_ffn2_gate, v_ffn2_up, v_ffn2_down, v_rel_bias, v_norm_final):
    mx, my, mc = _coords()
    cidx = jnp.reshape(mc, (1,)).astype(jnp.int32)
    qidx = jnp.reshape(2 * mx + my, (1,)).astype(jnp.int32)
    WM = w_mod.shape[2]

    c_tile = jnp.pad(c, ((0, 7), (0, 0)))
    b_mod3 = jnp.pad(b_mod.reshape(N_DEV, 1, WM), ((0, 0), (0, 7), (0, 0)))
    mod3, ca = _mod_fwd(c_tile, w_mod[0], b_mod3)
    mod9 = mod3[:, 0, :].reshape(N_MOD, D)

    shards = {"g1T": ffn1_gate[0].T.astype(BF16), "u1T": ffn1_up[0].T.astype(BF16), "d1": ffn1_down[0].astype(BF16),
              "g3T": ffn2_gate[0].T.astype(BF16), "u3T": ffn2_up[0].T.astype(BF16), "d3": ffn2_down[0].astype(BF16),
              "w_inT": w_in[0].T, "w_uqT": w_uq[0].T.astype(BF16), "w_ukv": w_ukv[0].astype(BF16),
              "w_o": w_o[0].astype(BF16)}
    me = 4 * mx + 2 * my + mc
    groups = {"ffn1": ("g1T", "u1T", "d1"), "mixer": ("w_inT", "w_uqT", "w_ukv", "w_o"), "ffn2": ("g3T", "u3T", "d3")}
    arriving = {}

    def as_weights(group, gathered):
        return {k: g if k == "w_ukv" else g.reshape(N_DEV * g.shape[1], g.shape[2])
                for k, g in zip(groups[group], gathered)}

    later = groups["mixer"] + groups["ffn2"]
    place = {"mixer": 0, "ffn2": len(groups["mixer"])}

    def start_gather(token):
        lands = []
        for k in later:
            sh = shards[k] + token[0, 0].astype(shards[k].dtype)
            lands.append(lax.dynamic_update_slice(lax.empty((N_DEV,) + sh.shape, sh.dtype), sh[None], (me, 0, 0)))
        send, recv, lands, started = _gather_start(lands, name="gather_start")
        for group, k0 in place.items():
            arriving[group] = (send, recv, lands[k0:k0 + len(groups[group])])
        return started

    def fetch(group, after, vecs):
        if group == "ffn1":
            *gathered, token = _wgather([shards[k] + ca[1, 0].astype(shards[k].dtype) for k in groups["ffn1"]])
            return as_weights("ffn1", gathered), vecs + start_gather(token)[0:1, 0:1]

        def pass_on(group, after):
            send, recv, lands = arriving[group]
            lands, token = _gather_pass(send, recv, lands, after, name="gather_landed_" + group, stage="landed",
                                        k0=place[group])
            lands, token = _gather_pass(send, recv, lands, [token], name="gather_onward_" + group, stage="onward",
                                        k0=place[group])
            arriving[group] = (send, recv, lands)
            return token

        if group == "ffn2_on_its_way":
            return None, vecs + pass_on("ffn2", after)[0:1, 0:1]
        if group == "mixer":
            after = [pass_on("mixer", after)]
        send, recv, lands = arriving[group]
        return as_weights(group, _gather_end(send, recv, lands, after, name="gather_end_" + group,
                                             k0=place[group])), vecs

    norms ={"ffn1": norm_ffn1, "mix": norm_mix, "ffn2": norm_ffn2, "final": norm_final.reshape(1, D)}
    in_flight = {}

    def on_grads(group, g, after, vecs, before_ici=()):
        if g is not None:
            names = list(g)
            by_dest = [g[k] if k == "w_ukv" else g[k].reshape((N_DEV, g[k].shape[0] // N_DEV) + g[k].shape[1:])
                       for k in names]
            lands = [lax.empty((4,) + a.shape[1:], a.dtype) for a in by_dest]
            send, recv, by_dest, lands, token = _split_start(_d2d_copies, by_dest, lands, 4 * len(names), after,
                                                             name="rs_d2d_start_" + group)
            in_flight[group] = (names, send, recv, by_dest, lands)
            return vecs + token[0:1, 0:1]
        names, send, recv, by_dest, lands = in_flight[group]
        by_dest, from_sib = _split_wait(_d2d_copies, send, recv, by_dest, lands, after, name="rs_d2d_wait_" + group)
        sums = _chipsum(by_dest, from_sib, cidx, name="chipsum_" + group)
        lands = [lax.empty((3,) + s.shape[1:], s.dtype) for s in sums]
        send, recv, sums, lands, token = _split_start(_ici_copies, sums, lands, 3 * len(names), list(before_ici),
                                                      name="rs_ici_start_" + group)
        in_flight[group] = (names, send, recv, sums, lands, token)
        return vecs + token[0:1, 0:1]

    _, grad_x, _, small, dmod9 = _local_step(
        x[0], loss_target[0], mod9, norms, sinks, rel_bias, q_norm, kv_norm, fetch, on_grads=on_grads)

    vec = jnp.concatenate([dmod9.reshape(N_MODVEC), small]).reshape(1, 1, N_VEC)
    allvec = lax.dynamic_update_slice(lax.empty((N_DEV, 1, N_VEC), F32), vec, (me, 0, 0))
    vsend, vrecv, _, (allvec,), vec_started = _split_start(_vec_copies, [], [allvec], N_DEV - 1, [], name="vec_start")
    on_grads("ffn1", None, [grad_x], jnp.zeros((1, 1), F32), before_ici=[vec_started])

    owners = {"g1T": ("ffn1_gate", ffn1_gate, m_ffn1_gate, v_ffn1_gate), "u1T": ("ffn1_up", ffn1_up, m_ffn1_up, v_ffn1_up),
              "d1": ("ffn1_down", ffn1_down, m_ffn1_down, v_ffn1_down),
              "g3T": ("ffn2_gate", ffn2_gate, m_ffn2_gate, v_ffn2_gate), "u3T": ("ffn2_up", ffn2_up, m_ffn2_up, v_ffn2_up),
              "d3": ("ffn2_down", ffn2_down, m_ffn2_down, v_ffn2_down),
              "w_inT": ("w_in", w_in, m_w_in, v_w_in), "w_uqT": ("w_uq", w_uq, m_w_uq, v_w_uq),
              "w_ukv": ("w_ukv", w_ukv, m_w_ukv, v_w_ukv), "w_o": ("w_o", w_o, m_w_o, v_w_o)}
    res, done = {}, []

    def finish(group, after):
        names, send, recv, sums, lands, _ = in_flight[group]
        sums, lands = _split_wait(_ici_copies, send, recv, sums, lands, after, name="rs_ici_wait_" + group)
        there = lambda k, a: a[0].T if k in TRANSPOSED else a[0]
        back = lambda k, a: a.T[None] if k in TRANSPOSED else a[None]
        wmv = [tuple(there(k, a) for a in owners[k][1:]) for k in names]
        if len({w.shape for w, _, _ in wmv}) == 1:
            outs = _adamw_rs(wmv, sums, lands, qidx, name="adamw_" + group)
        else:
            outs = [_adamw_rs([t], [cs], [rc], qidx, name="adamw_" + owners[k][0])[0]
                    for k, t, cs, rc in zip(names, wmv, sums, lands)]
        for k, out in zip(names, outs):
            done.append(out[3])
            res[owners[k][0]] = tuple(back(k, a) for a in out)

    ffn1_started = in_flight["ffn1"][5]
    finish("ffn2", [ffn1_started])
    finish("mixer", [ffn1_started])

    _, (allvec,) = _split_wait(_vec_copies, vsend, vrecv, [], [allvec], [ffn1_started], name="vec_wait")
    g_wmod, gvec = _mod_bwd(allvec, ca, jnp.reshape(me, (1,)).astype(jnp.int32))
    loss = gvec[0, N_MODVEC + LOSS_SLOT]
    res["w_mod"] = tuple(a[None] for a in (g_wmod,) + tuple(_adamw(w_mod[0], g_wmod, m_w_mod[0], v_w_mod[0],
                                                                    name="adamw_w_mod")))
    small_in = {"norm_ffn1": (norm_ffn1, m_norm_ffn1, v_norm_ffn1), "norm_mix": (norm_mix, m_norm_mix, v_norm_mix),
                "norm_ffn2": (norm_ffn2, m_norm_ffn2, v_norm_ffn2), "norm_final": (norm_final, m_norm_final, v_norm_final),
                "q_norm": (q_norm, m_q_norm, v_q_norm), "kv_norm": (kv_norm, m_kv_norm, v_kv_norm),
                "sinks": (sinks, m_sinks, v_sinks), "rel_bias": (rel_bias, m_rel_bias, v_rel_bias),
                "b_mod": (b_mod, m_b_mod, v_b_mod)}
    small_out = _adamw_small(gvec, [a.reshape(1, -1) for k in SMALL_PARAMS for a in small_in[k]])
    for k in SMALL_PARAMS:
        res[k] = tuple(a.reshape(small_in[k][0].shape) for a in small_out[k])

    finish("ffn1", done + [res["w_mod"][3], small_out["b_mod"][3]])

    order = ("w_mod", "b_mod", "norm_ffn1", "ffn1_gate", "ffn1_up", "ffn1_down", "norm_mix", "w_in", "q_norm",
             "kv_norm", "w_uq", "w_ukv", "sinks", "w_o", "norm_ffn2", "ffn2_gate", "ffn2_up", "ffn2_down",
             "rel_bias", "norm_final")
    return (loss, grad_x[None]) + tuple(res[nm][kind] for kind in range(4) for nm in order)
```

```python
import functools
import math

import numpy as np
import jax
import jax.numpy as jnp
from jax import lax
from jax.experimental import pallas as pl
from jax.experimental.pallas import tpu as pltpu

F32 = jnp.float32
BF16 = jnp.bfloat16
MESH = pl.DeviceIdType.MESH

N_DEV = 8
D = 1024
D_FF = 2816
EPS = 1e-6
N_MOD = 9
SWA_HEADS = 8
SWA_DH = 64
WINDOW = 128
MLA_HEADS = 4
MLA_NOPE = 128
MLA_ROPE = 64
MLA_V = 128
MLA_QR = 256
MLA_KVR = 128
ROPE_THETA = 10000.0
NUM_BUCKETS = 32
D_IN = 1216
D_IN_PAD = 1280
SWA_SCALE = SWA_DH ** -0.5
MLA_SCALE = (MLA_NOPE + MLA_ROPE) ** -0.5

ADAM_LR = 0.001
ADAM_B1 = 0.9
ADAM_B2 = 0.999
ADAM_EPS = 1e-08
ADAM_WD = 0.01
ADAM_STEP = 10

V7X_VMEM_LIMIT = 56 * 1024 * 1024

NT_DIMS = (((1,), (1,)), ((), ()))
TN_DIMS = (((0,), (0,)), ((), ()))


def _dot(a, b):
    return jnp.dot(a, b, preferred_element_type=F32)


def _dot_nt(a, b):
    return lax.dot_general(a, b, NT_DIMS, preferred_element_type=F32)


def _dot_tn(a, b):
    return lax.dot_general(a, b, TN_DIMS, preferred_element_type=F32)


def _params(sem=None):
    return pltpu.CompilerParams(dimension_semantics=sem, vmem_limit_bytes=V7X_VMEM_LIMIT)


def _rstd(x):
    return lax.rsqrt(jnp.mean(x * x, axis=-1, keepdims=True) + EPS)


def _rms_bwd(dy, xhat, r):
    return r * (dy - xhat * jnp.mean(dy * xhat, axis=-1, keepdims=True))


def _sigmoid(a):
    return 1.0 / (1.0 + jnp.exp(-a))


def _ffn_fwd(x, vecs, wgT, wuT, wd, *, name, tm=512, tf=256):
    S, F = x.shape[0], wd.shape[0]
    tm = min(tm, S)
    ni, nj = S // tm, F // tf

    def body(x_ref, vec_ref, wg_ref, wu_ref, wd_ref, xo_ref, h_ref, a_ref, b_ref, f_ref, acc_ref):
        j = pl.program_id(1)

        @pl.when(j == 0)
        def _():
            xv = x_ref[...]
            hn = xv * _rstd(xv) * vec_ref[0:1, :]
            h_ref[...] = (hn * (1.0 + vec_ref[2:3, :]) + vec_ref[1:2, :]).astype(BF16)

        h = h_ref[...]
        a = _dot_nt(h, wg_ref[...])
        b = _dot_nt(h, wu_ref[...])
        a_ref[...] = a.astype(BF16)
        b_ref[...] = b.astype(BF16)
        part = _dot((a * _sigmoid(a) * b).astype(BF16), wd_ref[...])

        def finish(f):
            f_ref[...] = f
            xo_ref[...] = x_ref[...] + (0.5 * vec_ref[3:4, :]) * f

        if nj == 1:
            finish(part)
        else:
            @pl.when(j == 0)
            def _():
                acc_ref[...] = part

            @pl.when((j > 0) & (j < nj - 1))
            def _():
                acc_ref[...] += part

            @pl.when(j == nj - 1)
            def _():
                finish(acc_ref[...] + part)

    row = pl.BlockSpec((tm, D), lambda i, j: (i, 0))
    wspec = pl.BlockSpec((tf, D), lambda i, j: (j, 0), pipeline_mode=pl.Buffered(1) if nj == 1 else None)
    act = pl.BlockSpec((tm, tf), lambda i, j: (i, j))
    return pl.pallas_call(
        body, name=name, grid=(ni, nj),
        in_specs=[row, pl.BlockSpec((8, D), lambda i, j: (0, 0)), wspec, wspec, wspec],
        out_specs=[row, row, act, act, row],
        out_shape=[jax.ShapeDtypeStruct((S, D), F32), jax.ShapeDtypeStruct((S, D), BF16),
                   jax.ShapeDtypeStruct((S, F), BF16), jax.ShapeDtypeStruct((S, F), BF16),
                   jax.ShapeDtypeStruct((S, D), F32)],
        scratch_shapes=[pltpu.VMEM((tm, D) if nj > 1 else (8, 128), F32)],
        compiler_params=_params(("parallel", "arbitrary")),
    )(x, vecs, wgT, wuT, wd)


def _ffn_bwd_main(h, df, a, b, wgT, wuT, wd, *, name, after=(), tm=512, tf=256):
    S = h.shape[0]
    tm = min(tm, S)
    ni, nj = S // tm, D_FF // tf

    def body(h_hbm, df_hbm, a_ref, b_ref, wg_ref, wu_ref, wd_ref, *rest):
        gg_ref, gu_ref, gd_ref, dh_hbm, h_v, df_v, dh_v, gg_acc, gu_acc, gd_acc, sem = rest[len(after):]
        j = pl.program_id(0)
        i = pl.program_id(1)

        @pl.when((j == 0) & (i == 0))
        def _():
            c1 = pltpu.make_async_copy(h_hbm, h_v, sem.at[0])
            c2 = pltpu.make_async_copy(df_hbm, df_v, sem.at[1])
            c1.start()
            c2.start()
            c1.wait()
            c2.wait()

        @pl.when(i == 0)
        def _():
            gg_acc[...] = jnp.zeros_like(gg_acc)
            gu_acc[...] = jnp.zeros_like(gu_acc)
            gd_acc[...] = jnp.zeros_like(gd_acc)

        rows = pl.ds(pl.multiple_of(i * tm, tm), tm)
        hi = h_v[rows, :]
        dfi = df_v[rows, :]
        av = a_ref[...].astype(F32)
        bv = b_ref[...].astype(F32)
        sg = _sigmoid(av)
        sa = av * sg
        hsw = (sa * bv).astype(BF16)
        dhsw = _dot_nt(dfi, wd_ref[...])
        da = (dhsw * bv * (sg * (1.0 + av * (1.0 - sg)))).astype(BF16)
        db = (dhsw * sa).astype(BF16)
        gd_acc[...] += _dot_tn(hsw, dfi)
        gg_acc[...] += _dot_tn(da, hi)
        gu_acc[...] += _dot_tn(db, hi)
        dh = _dot(da, wg_ref[...]) + _dot(db, wu_ref[...])

        @pl.when(j == 0)
        def _():
            dh_v[rows, :] = dh

        @pl.when(j > 0)
        def _():
            dh_v[rows, :] += dh

        @pl.when(i == ni - 1)
        def _():
            gg_ref[...] = gg_acc[...].astype(BF16)
            gu_ref[...] = gu_acc[...].astype(BF16)
            gd_ref[...] = gd_acc[...].astype(BF16)

        @pl.when((j == nj - 1) & (i == ni - 1))
        def _():
            c3 = pltpu.make_async_copy(dh_v, dh_hbm, sem.at[2])
            c3.start()
            c3.wait()

    anyspec = pl.BlockSpec(memory_space=pl.ANY)
    wspec = pl.BlockSpec((tf, D), lambda j, i: (j, 0))
    act = pl.BlockSpec((tm, tf), lambda j, i: (i, j))
    return pl.pallas_call(
        body, name=name, grid=(nj, ni),
        in_specs=[anyspec, anyspec, act, act, wspec, wspec, wspec] + [anyspec] * len(after),
        out_specs=[wspec, wspec, wspec, anyspec],
        out_shape=[jax.ShapeDtypeStruct((D_FF, D), BF16)] * 3 + [jax.ShapeDtypeStruct((S, D), F32)],
        scratch_shapes=[pltpu.VMEM((S, D), BF16), pltpu.VMEM((S, D), BF16), pltpu.VMEM((S, D), F32),
                        pltpu.VMEM((tf, D), F32), pltpu.VMEM((tf, D), F32), pltpu.VMEM((tf, D), F32),
                        pltpu.SemaphoreType.DMA((3,))],
        compiler_params=_params(("arbitrary", "arbitrary")),
    )(h, df, a, b, wgT, wuT, wd, *after)


def _ffn_out_bwd(dx, f, gate, df_ref, part_ref):
    df_ref[...] = ((0.5 * gate) * dx).astype(BF16)
    part_ref[3:4, :] += 0.5 * jnp.sum(dx * f, axis=0, keepdims=True)


def _norm_bwd(dh, x, dxo, vecs, *, name, below=None, tm=256):
    S = x.shape[0]

    def body(dh_ref, x_ref, dxo_ref, vec_ref, *rest):
        dx_ref, part_ref = rest[-2 if below is None else -3], rest[-1 if below is None else -2]

        @pl.when(pl.program_id(0) == 0)
        def _():
            part_ref[...] = jnp.zeros_like(part_ref)

        dh = dh_ref[...]
        xv = x_ref[...]
        r = _rstd(xv)
        xhat = xv * r
        w = vec_ref[0:1, :]
        xn = xhat * w
        dxn = dh * (1.0 + vec_ref[2:3, :])
        part_ref[0:1, :] += jnp.sum(dxn * xhat, axis=0, keepdims=True)
        part_ref[1:2, :] += jnp.sum(dh, axis=0, keepdims=True)
        part_ref[2:3, :] += jnp.sum(dh * xn, axis=0, keepdims=True)
        dx = dxo_ref[...] + _rms_bwd(dxn * w, xhat, r)
        dx_ref[...] = dx
        if below is not None:
            _ffn_out_bwd(dx, rest[0][...], rest[1][3:4, :], rest[-1], part_ref)

    row = pl.BlockSpec((tm, D), lambda i: (i, 0))
    vec = pl.BlockSpec((8, D), lambda i: (0, 0))
    extra = [] if below is None else [row, vec]
    return pl.pallas_call(
        body, name=name, grid=(S // tm,), in_specs=[row, row, row, vec] + extra,
        out_specs=[row, vec] + ([] if below is None else [row]),
        out_shape=[jax.ShapeDtypeStruct((S, D), F32), jax.ShapeDtypeStruct((8, D), F32)]
        + ([] if below is None else [jax.ShapeDtypeStruct((S, D), BF16)]),
        compiler_params=_params(("arbitrary",)),
    )(dh, x, dxo, vecs, *([] if below is None else below))


def _head(x, tgt, nf, f, vecs, *, tm=256):
    S = x.shape[0]

    def body(x_ref, t_ref, nf_ref, f_ref, vec_ref, dx_ref, part_ref, df_ref):
        @pl.when(pl.program_id(0) == 0)
        def _():
            part_ref[...] = jnp.zeros_like(part_ref)

        xv = x_ref[...]
        r = _rstd(xv)
        xhat = xv * r
        w = nf_ref[...]
        e = xhat * w - t_ref[...]
        dy = e * (1.0 / D)
        part_ref[0:1, :] += jnp.sum(dy * xhat, axis=0, keepdims=True)
        part_ref[1:2, :] += jnp.sum(e * e) * (0.5 / D)
        dx = _rms_bwd(dy * w, xhat, r)
        dx_ref[...] = dx
        _ffn_out_bwd(dx, f_ref[...], vec_ref[3:4, :], df_ref, part_ref)

    row = pl.BlockSpec((tm, D), lambda i: (i, 0))
    vec = pl.BlockSpec((8, D), lambda i: (0, 0))
    return pl.pallas_call(
        body, name="head", grid=(S // tm,),
        in_specs=[row, row, pl.BlockSpec((1, D), lambda i: (0, 0)), row, vec],
        out_specs=[row, vec, row],
        out_shape=[jax.ShapeDtypeStruct((S, D), F32), jax.ShapeDtypeStruct((8, D), F32),
                   jax.ShapeDtypeStruct((S, D), BF16)],
        compiler_params=_params(("arbitrary",)),
    )(x, tgt, nf, f, vecs)


def _mix_in_fwd(x, vecs, w_inT, *, tm=256):
    S = x.shape[0]

    def body(x_ref, vec_ref, w_ref, h_ref, p_ref):
        xv = x_ref[...]
        hn = xv * _rstd(xv) * vec_ref[0:1, :]
        h = (hn * (1.0 + vec_ref[2:3, :]) + vec_ref[1:2, :]).astype(BF16)
        h_ref[...] = h
        p_ref[...] = _dot_nt(h, w_ref[...])

    row = pl.BlockSpec((tm, D), lambda i: (i, 0))
    return pl.pallas_call(
        body, name="mix_in_fwd", grid=(S // tm,),
        in_specs=[row, pl.BlockSpec((8, D), lambda i: (0, 0)), pl.BlockSpec((D_IN_PAD, D), lambda i: (0, 0))],
        out_specs=[row, pl.BlockSpec((tm, D_IN_PAD), lambda i: (i, 0))],
        out_shape=[jax.ShapeDtypeStruct((S, D), BF16), jax.ShapeDtypeStruct((S, D_IN_PAD), F32)],
        compiler_params=_params(("parallel",)),
    )(x, vecs, w_inT)


def _bucket_table():
    qi = np.arange(WINDOW)[:, None]
    kj = np.arange(2 * WINDOW)[None, :]
    dist = qi + WINDOW - kj
    max_exact = NUM_BUCKETS // 2
    n = np.maximum(dist, 0)
    nf = np.maximum(n, 1).astype(np.float32)
    large = max_exact + (np.log(nf / np.float32(max_exact)) / np.float32(math.log(WINDOW / max_exact))
                         * np.float32(NUM_BUCKETS - max_exact)).astype(np.int32)
    large = np.minimum(large, NUM_BUCKETS - 1)
    return np.where(n < max_exact, n, large).astype(np.int32)


def _bias_build(rel_bias, bucket):
    def body(rb_ref, bk_ref, out_ref):
        bk = bk_ref[...]
        for h in range(SWA_HEADS):
            acc = jnp.zeros((WINDOW, 2 * WINDOW), F32)
            for b in range(NUM_BUCKETS):
                acc = jnp.where(bk == b, rb_ref[b, h], acc)
            out_ref[h] = acc

    return pl.pallas_call(
        body, name="bias_build",
        in_specs=[pl.BlockSpec(memory_space=pltpu.SMEM), pl.BlockSpec(memory_space=pltpu.VMEM)],
        out_specs=pl.BlockSpec(memory_space=pltpu.VMEM),
        out_shape=jax.ShapeDtypeStruct((SWA_HEADS, WINDOW, 2 * WINDOW), F32),
    )(rel_bias, bucket)


SWA_GROUP = 4
GROUP_ROWS = SWA_GROUP * WINDOW


def _swa_valid(n):
    row = lax.broadcasted_iota(jnp.int32, (GROUP_ROWS, 2 * WINDOW), 0) % WINDOW
    col = lax.broadcasted_iota(jnp.int32, (GROUP_ROWS, 2 * WINDOW), 1)
    dist = row + WINDOW - col
    return (dist >= 0) & (dist < WINDOW) & ((col >= WINDOW) | (n > 0))


def _stack_heads(x, g):
    return jnp.concatenate([x[:, 64 * h:64 * h + 64] for h in range(SWA_GROUP * g, SWA_GROUP * (g + 1))], axis=0)


def _unstack_heads(x4):
    return jnp.concatenate([x4[WINDOW * a:WINDOW * (a + 1)] for a in range(SWA_GROUP)], axis=1)


def _group_sinks(sink_ref, g):
    head = lax.broadcasted_iota(jnp.int32, (GROUP_ROWS, 1), 0) // WINDOW
    out = jnp.full((GROUP_ROWS, 1), sink_ref[0, SWA_GROUP * g], F32)
    for a in range(1, SWA_GROUP):
        out = jnp.where(head == a, sink_ref[0, SWA_GROUP * g + a], out)
    return out


def _swa_probs(qh, kk, bias_h, sink, valid):
    s = _dot_nt(qh, kk) * SWA_SCALE + bias_h
    s = jnp.where(valid, s, -jnp.inf)
    m = jnp.maximum(jnp.max(s, axis=-1, keepdims=True), sink)
    p = jnp.exp(s - m)
    ps = jnp.exp(sink - m)
    inv = 1.0 / (jnp.sum(p, axis=-1, keepdims=True) + ps)
    return p * inv, ps * inv


def _swa_specs():
    prev = lambda n: jnp.maximum(n - 1, 0)
    return [pl.BlockSpec((WINDOW, 512), lambda n: (n, 0)),
            pl.BlockSpec((WINDOW, 128), lambda n: (n, 4)),
            pl.BlockSpec((WINDOW, 128), lambda n: (prev(n), 4)),
            pl.BlockSpec((WINDOW, 128), lambda n: (n, 5)),
            pl.BlockSpec((WINDOW, 128), lambda n: (prev(n), 5)),
            pl.BlockSpec((SWA_HEADS, WINDOW, 2 * WINDOW), lambda n: (0, 0, 0)),
            pl.BlockSpec(memory_space=pltpu.SMEM)]


def _swa_fwd(proj, bias, sinks):
    S = proj.shape[0]

    def body(q_ref, kc_ref, kp_ref, vc_ref, vp_ref, bias_ref, sink_ref, o_ref):
        valid = _swa_valid(pl.program_id(0))
        q = q_ref[...].astype(BF16)
        kfull = jnp.concatenate([kp_ref[...], kc_ref[...]], axis=0).astype(BF16)
        vfull = jnp.concatenate([vp_ref[...], vc_ref[...]], axis=0).astype(BF16)
        for g in range(SWA_HEADS // SWA_GROUP):
            kk = kfull[:, 64 * g:64 * g + 64]
            vv = vfull[:, 64 * g:64 * g + 64]
            bias4 = bias_ref[SWA_GROUP * g:SWA_GROUP * (g + 1)].reshape(GROUP_ROWS, 2 * WINDOW)
            pk, _ = _swa_probs(_stack_heads(q, g), kk, bias4, _group_sinks(sink_ref, g), valid)
            o_ref[:, 256 * g:256 * (g + 1)] = _unstack_heads(_dot(pk.astype(BF16), vv))

    return pl.pallas_call(
        body, name="swa_fwd", grid=(S // WINDOW,),
        in_specs=_swa_specs(),
        out_specs=pl.BlockSpec((WINDOW, 512), lambda n: (n, 0)),
        out_shape=jax.ShapeDtypeStruct((S, 512), F32),
        compiler_params=_params(("parallel",)),
    )(proj, proj, proj, proj, proj, bias, sinks)


def _swa_bwd(proj, bias, sinks, o, do, bucket):
    S = proj.shape[0]
    nb = S // WINDOW

    def body(q_ref, kc_ref, kp_ref, vc_ref, vp_ref, bias_ref, sink_ref, o_ref, do_ref, bk_ref,
             dq_ref, dk_ref, dv_ref, drb_ref, dsk_ref, dbias_acc):
        n = pl.program_id(0)

        @pl.when(n == 0)
        def _():
            dk_ref[...] = jnp.zeros_like(dk_ref)
            dv_ref[...] = jnp.zeros_like(dv_ref)
            dsk_ref[...] = jnp.zeros_like(dsk_ref)
            dbias_acc[...] = jnp.zeros_like(dbias_acc)
            drb_ref[...] = jnp.zeros_like(drb_ref)

        valid = _swa_valid(n)
        q = q_ref[...].astype(BF16)
        dov = do_ref[...]
        kfull = jnp.concatenate([kp_ref[...], kc_ref[...]], axis=0).astype(BF16)
        vfull = jnp.concatenate([vp_ref[...], vc_ref[...]], axis=0).astype(BF16)
        prow = pl.ds(pl.multiple_of(jnp.maximum(n - 1, 0) * WINDOW, WINDOW), WINDOW)
        crow = pl.ds(pl.multiple_of(n * WINDOW, WINDOW), WINDOW)
        ov = o_ref[...]
        for g in range(SWA_HEADS // SWA_GROUP):
            heads = slice(SWA_GROUP * g, SWA_GROUP * (g + 1))
            kk = kfull[:, 64 * g:64 * g + 64]
            vv = vfull[:, 64 * g:64 * g + 64]
            q4 = _stack_heads(q, g)
            pk, psink = _swa_probs(q4, kk, bias_ref[heads].reshape(GROUP_ROWS, 2 * WINDOW), _group_sinks(sink_ref, g), valid)
            pkb = pk.astype(BF16)
            do4 = _stack_heads(dov, g)
            dob = do4.astype(BF16)
            dp = _dot_nt(dob, vv)
            delta = jnp.sum(do4 * _stack_heads(ov, g), axis=-1, keepdims=True)
            ds = pk * (dp - delta)
            dsink = -psink * delta
            for a in range(SWA_GROUP):
                h = SWA_GROUP * g + a
                part = jnp.sum(dsink[WINDOW * a:WINDOW * (a + 1)], keepdims=True)
                dsk_ref[h:h + 1, :] += jnp.broadcast_to(part, (1, 128))
            dbias_acc[heads] += ds.reshape(SWA_GROUP, WINDOW, 2 * WINDOW)
            dsb = (ds * SWA_SCALE).astype(BF16)
            dq_ref[:, 256 * g:256 * (g + 1)] = _unstack_heads(_dot(dsb, kk))
            dkk = _dot_tn(dsb, q4)
            dvv = _dot_tn(pkb, dob)
            dk_ref[prow, 64 * g:64 * g + 64] += dkk[:WINDOW]
            dk_ref[crow, 64 * g:64 * g + 64] += dkk[WINDOW:]
            dv_ref[prow, 64 * g:64 * g + 64] += dvv[:WINDOW]
            dv_ref[crow, 64 * g:64 * g + 64] += dvv[WINDOW:]

        @pl.when(n == nb - 1)
        def _():
            bk = bk_ref[...]
            for h in range(SWA_HEADS):
                dbh = dbias_acc[h]
                for b in range(NUM_BUCKETS):
                    val = jnp.sum(jnp.where(bk == b, dbh, 0.0), keepdims=True)
                    drb_ref[b * 8 + h:b * 8 + h + 1, :] = jnp.broadcast_to(val, (1, 128))

    full = lambda shape: pl.BlockSpec(shape, lambda n: tuple(0 for _ in shape))
    return pl.pallas_call(
        body, name="swa_bwd", grid=(nb,),
        in_specs=_swa_specs() + [pl.BlockSpec((WINDOW, 512), lambda n: (n, 0)),
                                 pl.BlockSpec((WINDOW, 512), lambda n: (n, 0)), full((WINDOW, 2 * WINDOW))],
        out_specs=[pl.BlockSpec((WINDOW, 512), lambda n: (n, 0)), full((S, 128)), full((S, 128)),
                   full((NUM_BUCKETS * 8, 128)), full((8, 128))],
        out_shape=[jax.ShapeDtypeStruct((S, 512), F32), jax.ShapeDtypeStruct((S, 128), F32),
                   jax.ShapeDtypeStruct((S, 128), F32), jax.ShapeDtypeStruct((NUM_BUCKETS * 8, 128), F32),
                   jax.ShapeDtypeStruct((8, 128), F32)],
        scratch_shapes=[pltpu.VMEM((SWA_HEADS, WINDOW, 2 * WINDOW), F32)],
        compiler_params=_params(("arbitrary",)),
    )(proj, proj, proj, proj, proj, bias, sinks, o, do, bucket)


def _rope_tables(S):
    inv = ROPE_THETA ** (-jnp.arange(0, MLA_ROPE, 2, dtype=F32) / MLA_ROPE)
    ang = jnp.arange(S, dtype=F32)[:, None] * inv[None, :]
    cos, sin = jnp.cos(ang), jnp.sin(ang)
    return jnp.tile(jnp.concatenate([cos, cos], axis=1), (1, 4)), jnp.tile(jnp.concatenate([-sin, sin], axis=1), (1, 4))


def _swap_halves(x):
    w = x.shape[-1]
    lane = lax.broadcasted_iota(jnp.int32, x.shape, x.ndim - 1)
    return jnp.where((lane % 64) < 32, pltpu.roll(x, w - 32, x.ndim - 1), pltpu.roll(x, 32, x.ndim - 1))


def _mla_pre_fwd(proj, qn_w, kvn_w, wuqT, wukv, cos, sin, *, tm=256):
    S = proj.shape[0]

    def body(ql_ref, kl_ref, kr_ref, qw_ref, kw_ref, wuq_ref, wukv_ref, cos_ref, sin_ref,
             qc_ref, kc_ref, vv_ref):
        ql = ql_ref[...]
        qn = (ql * _rstd(ql) * qw_ref[...]).astype(BF16)
        q = _dot_nt(qn, wuq_ref[...])
        cs, sn = cos_ref[...], sin_ref[...]
        qr = q[:, 512:768]
        qr = qr * cs + _swap_halves(qr) * sn
        half = lax.broadcasted_iota(jnp.int32, (tm, 128), 1) // 64
        kl = kl_ref[...]
        kvn = (kl * _rstd(kl) * kw_ref[...]).astype(BF16)
        kr = kr_ref[...]
        kr = kr * cs[:, :128] + _swap_halves(kr) * sn[:, :128]
        kr2 = (kr + pltpu.roll(kr, 64, 1)).astype(BF16)
        for h in range(MLA_HEADS):
            qc_ref[h, :, 0:128] = q[:, 128 * h:128 * h + 128].astype(BF16)
            chunk = qr[:, 128 * (h // 2):128 * (h // 2) + 128]
            qc_ref[h, :, 128:256] = jnp.where(half == (h % 2), chunk, 0.0).astype(BF16)
            kc_ref[h, :, 0:128] = _dot(kvn, wukv_ref[2 * h]).astype(BF16)
            kc_ref[h, :, 128:256] = kr2
            vv_ref[h] = _dot(kvn, wukv_ref[2 * h + 1]).astype(BF16)

    const = lambda shape: pl.BlockSpec(shape, lambda i: tuple(0 for _ in shape))
    return pl.pallas_call(
        body, name="mla_pre_fwd", grid=(S // tm,),
        in_specs=[pl.BlockSpec((tm, 256), lambda i: (i, 3)), pl.BlockSpec((tm, 128), lambda i: (i, 8)),
                  pl.BlockSpec((tm, 128), lambda i: (i, 9)), const((1, 256)), const((1, 128)),
                  const((768, 256)), const((8, 128, 128)),
                  pl.BlockSpec((tm, 256), lambda i: (i, 0)), pl.BlockSpec((tm, 256), lambda i: (i, 0))],
        out_specs=[pl.BlockSpec((MLA_HEADS, tm, 256), lambda i: (0, i, 0)),
                   pl.BlockSpec((MLA_HEADS, tm, 256), lambda i: (0, i, 0)),
                   pl.BlockSpec((MLA_HEADS, tm, 128), lambda i: (0, i, 0))],
        out_shape=[jax.ShapeDtypeStruct((MLA_HEADS, S, 256), BF16), jax.ShapeDtypeStruct((MLA_HEADS, S, 256), BF16),
                   jax.ShapeDtypeStruct((MLA_HEADS, S, 128), BF16)],
        compiler_params=_params(("parallel",)),
    )(proj, proj, proj, qn_w, kvn_w, wuqT, wukv, cos, sin)


def _causal(i, j, t):
    row = i * t + lax.broadcasted_iota(jnp.int32, (t, t), 0)
    col = j * t + lax.broadcasted_iota(jnp.int32, (t, t), 1)
    return col <= row


def _mla_attn_fwd(qc, kc, vv, *, t=256):
    S = qc.shape[1]
    t = min(t, S)

    def body(q_ref, k_ref, v_ref, o_ref, l_ref):
        i = pl.program_id(0)
        diag = _causal(0, 0, t)

        def step(j, carry, masked):
            rows = pl.ds(pl.multiple_of(j * t, t), t)
            out = []
            for h in range(MLA_HEADS):
                m, l, acc = carry[h]
                s = _dot_nt(q_ref[h], k_ref[h, rows, :]) * MLA_SCALE
                if masked:
                    s = jnp.where(diag, s, -jnp.inf)
                m_new = jnp.maximum(m, jnp.max(s, axis=-1, keepdims=True))
                alpha = jnp.exp(m - m_new)
                p = jnp.exp(s - m_new)
                l = alpha * l + jnp.sum(p, axis=-1, keepdims=True)
                acc = alpha * acc + _dot(p.astype(BF16), v_ref[h, rows, :])
                out.append((m_new, l, acc))
            return tuple(out)

        init = tuple((jnp.full((t, 1), -jnp.inf, F32), jnp.zeros((t, 1), F32), jnp.zeros((t, MLA_V), F32))
                     for _ in range(MLA_HEADS))
        carry = lax.fori_loop(0, i, lambda j, c: step(j, c, False), init)
        carry = step(i, carry, True)
        for h in range(MLA_HEADS):
            m, l, acc = carry[h]
            o_ref[:, 128 * h:128 * h + 128] = acc / l
            l_ref[h] = jnp.broadcast_to(m + jnp.log(l), (t, 128))

    return pl.pallas_call(
        body, name="mla_attn_fwd", grid=(S // t,),
        in_specs=[pl.BlockSpec((MLA_HEADS, t, 256), lambda i: (0, i, 0)),
                  pl.BlockSpec((MLA_HEADS, S, 256), lambda i: (0, 0, 0)),
                  pl.BlockSpec((MLA_HEADS, S, 128), lambda i: (0, 0, 0))],
        out_specs=[pl.BlockSpec((t, 512), lambda i: (i, 0)),
                   pl.BlockSpec((MLA_HEADS, t, 128), lambda i: (0, i, 0))],
        out_shape=[jax.ShapeDtypeStruct((S, 512), F32), jax.ShapeDtypeStruct((MLA_HEADS, S, 128), F32)],
        compiler_params=_params(("parallel",)),
    )(qc, kc, vv)


def _mla_attn_bwd(qc, kc, vv, o, lse, do, *, t=256, tq=512):
    S = qc.shape[1]
    t = min(t, S)
    tq = min(tq, S)
    nblk = S // t
    hp = MLA_HEADS
    once = pl.Buffered(1)

    def body(q_ref, k_ref, v_ref, o_ref, l_ref, do_ref, dq_ref, dk_ref, dv_ref):
        j = pl.program_id(1)

        @pl.when(j == 0)
        def _():
            dq_ref[...] = jnp.zeros_like(dq_ref)

        first = (j * t) // tq

        def step(i, carry, masked):
            rows = pl.ds(pl.multiple_of(i * tq, tq), tq)
            if masked:
                row = i * tq + lax.broadcasted_iota(jnp.int32, (tq, t), 0)
                col = j * t + lax.broadcasted_iota(jnp.int32, (tq, t), 1)
                visible = col <= row
            out = []
            for h in range(hp):
                dk, dv = carry[h]
                k = k_ref[h]
                q = q_ref[h, rows, :]
                dov = do_ref[rows, 128 * h:128 * h + 128]
                lrow = l_ref[h, rows, :][:, 0:1]
                p = jnp.exp(_dot_nt(q, k) * MLA_SCALE - lrow)
                if masked:
                    p = jnp.where(visible, p, 0.0)
                dob = dov.astype(BF16)
                dv = dv + _dot_tn(p.astype(BF16), dob)
                dp = _dot_nt(dob, v_ref[h])
                delta = jnp.sum(dov * o_ref[rows, 128 * h:128 * h + 128], axis=-1, keepdims=True)
                ds = (p * (dp - delta) * MLA_SCALE).astype(BF16)
                dk = dk + _dot_tn(ds, q)
                dq_ref[h, rows, :] += _dot(ds, k)
                out.append((dk, dv))
            return tuple(out)

        init = tuple((jnp.zeros((t, 256), F32), jnp.zeros((t, MLA_V), F32)) for _ in range(hp))
        carry = step(first, init, True)
        carry = lax.fori_loop(first + 1, S // tq, lambda i, c: step(i, c, False), carry)
        for h in range(hp):
            dk_ref[h] = carry[h][0]
            dv_ref[h] = carry[h][1]

    return pl.pallas_call(
        body, name="mla_attn_bwd", grid=(MLA_HEADS // hp, nblk),
        in_specs=[pl.BlockSpec((hp, S, 256), lambda g, j: (g, 0, 0), pipeline_mode=once),
                  pl.BlockSpec((hp, t, 256), lambda g, j: (g, j, 0)),
                  pl.BlockSpec((hp, t, 128), lambda g, j: (g, j, 0)),
                  pl.BlockSpec((S, 128 * hp), lambda g, j: (0, g), pipeline_mode=once),
                  pl.BlockSpec((hp, S, 128), lambda g, j: (g, 0, 0), pipeline_mode=once),
                  pl.BlockSpec((S, 128 * hp), lambda g, j: (0, g), pipeline_mode=once)],
        out_specs=[pl.BlockSpec((hp, S, 256), lambda g, j: (g, 0, 0)),
                   pl.BlockSpec((hp, t, 256), lambda g, j: (g, j, 0)),
                   pl.BlockSpec((hp, t, 128), lambda g, j: (g, j, 0))],
        out_shape=[jax.ShapeDtypeStruct((MLA_HEADS, S, 256), F32), jax.ShapeDtypeStruct((MLA_HEADS, S, 256), F32),
                   jax.ShapeDtypeStruct((MLA_HEADS, S, 128), F32)],
        compiler_params=_params(("parallel", "arbitrary")),
    )(qc, kc, vv, o, lse, do)


def _mla_pre_bwd(proj, qn_w, kvn_w, wuqT, wukv, cos, sin, dqc, dkc, dvv, *, tm=256):
    S = proj.shape[0]

    def body(ql_ref, kl_ref, qw_ref, kw_ref, wuq_ref, wukv_ref, cos_ref, sin_ref, dqc_ref, dkc_ref, dvv_ref,
             dql_ref, dkl_ref, dkr_ref, gq_ref, gkv_ref, part_ref):
        @pl.when(pl.program_id(0) == 0)
        def _():
            gq_ref[...] = jnp.zeros_like(gq_ref)
            gkv_ref[...] = jnp.zeros_like(gkv_ref)
            part_ref[...] = jnp.zeros_like(part_ref)

        cs, sn = cos_ref[...], sin_ref[...]
        half = lax.broadcasted_iota(jnp.int32, (tm, 128), 1) // 64
        ql = ql_ref[...]
        rq = _rstd(ql)
        qhat = ql * rq
        qw = qw_ref[...]
        qn = (qhat * qw).astype(BF16)
        chunks = []
        for pair in range(2):
            chunks.append(jnp.where(half == 0, dqc_ref[2 * pair, :, 128:256], dqc_ref[2 * pair + 1, :, 128:256]))
        dqr = jnp.concatenate(chunks, axis=1)
        dqr = dqr * cs + _swap_halves(dqr * sn)
        dq = jnp.concatenate([dqc_ref[h, :, 0:128] for h in range(MLA_HEADS)] + [dqr], axis=1).astype(BF16)
        gq_ref[...] += _dot_tn(dq, qn)
        dqn = _dot(dq, wuq_ref[...])
        part_ref[0:1, :] += jnp.sum(dqn * qhat, axis=0, keepdims=True)
        dql_ref[...] = _rms_bwd(dqn * qw, qhat, rq)
        kl = kl_ref[...]
        rk = _rstd(kl)
        khat = kl * rk
        kw = kw_ref[...]
        kvn = (khat * kw).astype(BF16)
        dkvn = jnp.zeros((tm, MLA_KVR), F32)
        dkr2 = jnp.zeros((tm, 128), F32)
        for h in range(MLA_HEADS):
            dkn = dkc_ref[h, :, 0:128].astype(BF16)
            dvh = dvv_ref[h].astype(BF16)
            gkv_ref[2 * h] += _dot_tn(kvn, dkn)
            gkv_ref[2 * h + 1] += _dot_tn(kvn, dvh)
            dkvn += _dot_nt(dkn, wukv_ref[2 * h]) + _dot_nt(dvh, wukv_ref[2 * h + 1])
            dkr2 += dkc_ref[h, :, 128:256]
        part_ref[1:2, 0:128] += jnp.sum(dkvn * khat, axis=0, keepdims=True)
        dkl_ref[...] = _rms_bwd(dkvn * kw, khat, rk)
        dkr = jnp.where(half == 0, dkr2 + pltpu.roll(dkr2, 64, 1), 0.0)
        dkr_ref[...] = dkr * cs[:, :128] + _swap_halves(dkr * sn[:, :128])

    const = lambda shape: pl.BlockSpec(shape, lambda i: tuple(0 for _ in shape))
    heads = lambda w: pl.BlockSpec((MLA_HEADS, tm, w), lambda i: (0, i, 0))
    return pl.pallas_call(
        body, name="mla_pre_bwd", grid=(S // tm,),
        in_specs=[pl.BlockSpec((tm, 256), lambda i: (i, 3)), pl.BlockSpec((tm, 128), lambda i: (i, 8)),
                  const((1, 256)), const((1, 128)), const((768, 256)), const((8, 128, 128)),
                  pl.BlockSpec((tm, 256), lambda i: (i, 0)), pl.BlockSpec((tm, 256), lambda i: (i, 0)),
                  heads(256), heads(256), heads(128)],
        out_specs=[pl.BlockSpec((tm, 256), lambda i: (i, 0)), pl.BlockSpec((tm, 128), lambda i: (i, 0)),
                   pl.BlockSpec((tm, 128), lambda i: (i, 0)), const((768, 256)), const((8, 128, 128)), const((8, 256))],
        out_shape=[jax.ShapeDtypeStruct((S, 256), F32), jax.ShapeDtypeStruct((S, 128), F32),
                   jax.ShapeDtypeStruct((S, 128), F32), jax.ShapeDtypeStruct((768, 256), F32),
                   jax.ShapeDtypeStruct((8, 128, 128), F32), jax.ShapeDtypeStruct((8, 256), F32)],
        compiler_params=_params(("arbitrary",)),
    )(proj, proj, qn_w, kvn_w, wuqT, wukv, cos, sin, dqc, dkc, dvv)


def _mix_out_fwd(x, oa, ob, w_o, vecs, *, tm=256):
    S = x.shape[0]

    def body(x_ref, oa_ref, ob_ref, w_ref, vec_ref, xo_ref, mo_ref):
        mo = _dot(oa_ref[...].astype(BF16), w_ref[0:512, :]) + _dot(ob_ref[...].astype(BF16), w_ref[512:1024, :])
        mo_ref[...] = mo
        xo_ref[...] = x_ref[...] + vec_ref[3:4, :] * mo

    row = pl.BlockSpec((tm, D), lambda i: (i, 0))
    half = pl.BlockSpec((tm, 512), lambda i: (i, 0))
    return pl.pallas_call(
        body, name="mix_out_fwd", grid=(S // tm,),
        in_specs=[row, half, half, pl.BlockSpec((D, D), lambda i: (0, 0)), pl.BlockSpec((8, D), lambda i: (0, 0))],
        out_specs=[row, row],
        out_shape=[jax.ShapeDtypeStruct((S, D), F32), jax.ShapeDtypeStruct((S, D), F32)],
        compiler_params=_params(("parallel",)),
    )(x, oa, ob, w_o, vecs)


def _mix_out_bwd(dxo, mo, oa, ob, w_o, vecs, *, tm=256):
    S = dxo.shape[0]

    def body(dx_ref, mo_ref, oa_ref, ob_ref, w_ref, vec_ref, doa_ref, dob_ref, gw_ref, part_ref):
        @pl.when(pl.program_id(0) == 0)
        def _():
            gw_ref[...] = jnp.zeros_like(gw_ref)
            part_ref[...] = jnp.zeros_like(part_ref)

        dx = dx_ref[...]
        part_ref[0:1, :] += jnp.sum(dx * mo_ref[...], axis=0, keepdims=True)
        dmo = (vec_ref[3:4, :] * dx).astype(BF16)
        doa_ref[...] = _dot_nt(dmo, w_ref[0:512, :])
        dob_ref[...] = _dot_nt(dmo, w_ref[512:1024, :])
        gw_ref[0:512, :] += _dot_tn(oa_ref[...].astype(BF16), dmo)
        gw_ref[512:1024, :] += _dot_tn(ob_ref[...].astype(BF16), dmo)

    row = pl.BlockSpec((tm, D), lambda i: (i, 0))
    half = pl.BlockSpec((tm, 512), lambda i: (i, 0))
    return pl.pallas_call(
        body, name="mix_out_bwd", grid=(S // tm,),
        in_specs=[row, row, half, half, pl.BlockSpec((D, D), lambda i: (0, 0)), pl.BlockSpec((8, D), lambda i: (0, 0))],
        out_specs=[half, half, pl.BlockSpec((D, D), lambda i: (0, 0)), pl.BlockSpec((8, D), lambda i: (0, 0))],
        out_shape=[jax.ShapeDtypeStruct((S, 512), F32), jax.ShapeDtypeStruct((S, 512), F32),
                   jax.ShapeDtypeStruct((D, D), F32), jax.ShapeDtypeStruct((8, D), F32)],
        compiler_params=_params(("arbitrary",)),
    )(dxo, mo, oa, ob, w_o, vecs)


def _mix_in_bwd(h, w_inT, dq, dk, dv, dql, dkl, dkr, *, tm=256):
    S = h.shape[0]
    offs = (0, 512, 640, 768, 1024, 1152)
    wid = (512, 128, 128, 256, 128, 128)

    def body(h_ref, w_ref, dq_ref, dk_ref, dv_ref, dql_ref, dkl_ref, dkr_ref, dh_ref, gw_ref):
        @pl.when(pl.program_id(0) == 0)
        def _():
            gw_ref[...] = jnp.zeros_like(gw_ref)

        hv = h_ref[...]
        dh = jnp.zeros((tm, D), F32)
        for ref, o, w in zip((dq_ref, dk_ref, dv_ref, dql_ref, dkl_ref, dkr_ref), offs, wid):
            w = min(w, D_IN - o)
            dpart = ref[...][:, :w].astype(BF16)
            dh += _dot(dpart, w_ref[o:o + w, :])
            gw_ref[o:o + w, :] += _dot_tn(dpart, hv)
        dh_ref[...] = dh

    row = pl.BlockSpec((tm, D), lambda i: (i, 0))
    part = lambda w: pl.BlockSpec((tm, w), lambda i: (i, 0))
    return pl.pallas_call(
        body, name="mix_in_bwd", grid=(S // tm,),
        in_specs=[row, pl.BlockSpec((D_IN_PAD, D), lambda i: (0, 0))] + [part(w) for w in wid],
        out_specs=[row, pl.BlockSpec((D_IN, D), lambda i: (0, 0))],
        out_shape=[jax.ShapeDtypeStruct((S, D), F32), jax.ShapeDtypeStruct((D_IN, D), F32)],
        compiler_params=_params(("arbitrary",)),
    )(h, w_inT, dq, dk, dv, dql, dkl, dkr)


def _vecs(norm_w, mod9, k):
    return jnp.concatenate([norm_w.reshape(1, D), mod9[3 * k:3 * k + 3], jnp.zeros((4, D), F32)], axis=0)


def _uq_group_rows(wuqT):
    per = MLA_NOPE + MLA_ROPE
    nope = [wuqT[per * h:per * h + MLA_NOPE] for h in range(MLA_HEADS)]
    rope = [wuqT[per * h + MLA_NOPE:per * (h + 1)] for h in range(MLA_HEADS)]
    return jnp.concatenate(nope + rope, axis=0)


def _uq_ungroup_rows(g):
    parts = []
    for h in range(MLA_HEADS):
        parts += [g[MLA_NOPE * h:MLA_NOPE * (h + 1)], g[512 + MLA_ROPE * h:512 + MLA_ROPE * (h + 1)]]
    return jnp.concatenate(parts, axis=0)


def _local_step(x, tgt, mod9, norms, sinks, rel_bias, q_norm, kv_norm, W, on_grads=None):
    if on_grads is None:
        on_grads = lambda group, grads, after, vecs: vecs
    S = x.shape[0]
    v1 = _vecs(norms["ffn1"], mod9, 0)
    v2 = _vecs(norms["mix"], mod9, 1)
    v3 = _vecs(norms["ffn2"], mod9, 2)
    bucket = jnp.asarray(_bucket_table())
    cos, sin = _rope_tables(S)
    if isinstance(W, dict):
        full, W = W, (lambda group, after, vecs: (full, vecs))

    W1, v1 = W("ffn1", [], v1)
    x1, h1, a1, b1, f1 = _ffn_fwd(x, v1, W1["g1T"], W1["u1T"], W1["d1"], name="ffn1_fwd", tm=256, tf=D_FF)
    W2, v2 = W("mixer", [x1], v2)
    w_inT = jnp.pad(W2["w_inT"], ((0, D_IN_PAD - D_IN), (0, 0))).astype(BF16)
    wuqT = _uq_group_rows(W2["w_uqT"])
    h2, proj = _mix_in_fwd(x1, v2, w_inT)
    bias = _bias_build(rel_bias, bucket)
    oa = _swa_fwd(proj, bias, sinks)
    qc, kc, vv = _mla_pre_fwd(proj, q_norm, kv_norm, wuqT, W2["w_ukv"], cos, sin)
    ob, lse = _mla_attn_fwd(qc, kc, vv)
    _, v2o = W("ffn2_on_its_way", [ob], v2)
    x2, mo = _mix_out_fwd(x1, oa, ob, W2["w_o"], v2o)
    W3, v3 = W("ffn2", [x2], v3)
    x3, h3, a3, b3, f3 = _ffn_fwd(x2, v3, W3["g3T"], W3["u3T"], W3["d3"], name="ffn2_fwd", tm=256, tf=D_FF)
    dx3, head_part, df3 = _head(x3, tgt, norms["final"], f3, v3)

    gg3, gu3, gd3, dh3 = _ffn_bwd_main(h3, df3, a3, b3, W3["g3T"], W3["u3T"], W3["d3"], name="ffn2_bwd", tm=2048, tf=256)
    ffn2 = {"g3T": gg3, "u3T": gu3, "d3": gd3}
    v3 = on_grads("ffn2", ffn2, [], v3)
    dx2, n3_part = _norm_bwd(dh3, x2, dx3, v3, name="ffn2_norm_bwd")
    v2 = on_grads("ffn2", None, [dx2], v2)
    doa, dob, g_wo, g2_part = _mix_out_bwd(dx2, mo, oa, ob, W2["w_o"], v2)
    dq, dk, dv, drb, dsk = _swa_bwd(proj, bias, sinks, oa, doa, bucket)
    dqc, dkc, dvv = _mla_attn_bwd(qc, kc, vv, ob, lse, dob)
    dql, dkl, dkr, g_uq, g_ukv, mla_part = _mla_pre_bwd(proj, q_norm, kv_norm, wuqT, W2["w_ukv"], cos, sin, dqc, dkc, dvv)
    dh2, g_win = _mix_in_bwd(h2, w_inT, dq, dk, dv, dql, dkl, dkr)
    mixer = {"w_inT": g_win, "w_uqT": _uq_ungroup_rows(g_uq).astype(BF16),
             "w_ukv": g_ukv.astype(BF16), "w_o": g_wo.astype(BF16)}
    v2 = on_grads("mixer", mixer, [], v2)
    dx1, n2_part, df1 = _norm_bwd(dh2, x1, dx2, v2, name="mix_norm_bwd", below=(f1, v1))
    started = on_grads("mixer", None, [dx1], jnp.zeros((1, 1), F32))
    gg1, gu1, gd1, dh1 = _ffn_bwd_main(h1, df1, a1, b1, W1["g1T"], W1["u1T"], W1["d1"], name="ffn1_bwd",
                                       after=[started], tm=2048, tf=256)
    ffn1 = {"g1T": gg1, "u1T": gu1, "d1": gd1}
    v1 = on_grads("ffn1", ffn1, [], v1)
    dx0, n1_part = _norm_bwd(dh1, x, dx1, v1, name="ffn1_norm_bwd")

    grads = {**ffn1, **ffn2, **mixer}
    dmod9 = jnp.concatenate([n1_part[1:3], n2_part[3:4], n2_part[1:3], g2_part[0:1],
                             n3_part[1:3], head_part[3:4]], axis=0)
    small = jnp.concatenate([n1_part[0], n2_part[0], n3_part[0], head_part[0], mla_part[0],
                             mla_part[1, :128], dsk[:, 0], head_part[1, 0:1], jnp.zeros((119,), F32), drb[:, 0]])
    return head_part[1, 0], dx0, grads, small, dmod9


SMALL_LAYOUT = (("norm_ffn1", 1024), ("norm_mix", 1024), ("norm_ffn2", 1024), ("norm_final", 1024),
                ("q_norm", 256), ("kv_norm", 128), ("sinks", 128), ("rel_bias", 256))
N_SMALL = sum(n for _, n in SMALL_LAYOUT)
LOSS_SLOT = 4 * 1024 + 256 + 128 + SWA_HEADS


def _coords():
    return lax.axis_index("x"), lax.axis_index("y"), lax.axis_index("c")


def _flip(v, bit):
    return 1 - v if bit else v


def _peer(r):
    x, y, c = _coords()
    return (_flip(x, r & 4), _flip(y, r & 2), _flip(c, r & 1))


def _mod_fwd(c_tile, w_mod, b_mod3):
    W = w_mod.shape[1]

    def body(c_ref, w_ref, b_ref, mod_ref, ca_ref, call_ref, part_ref, send_sems, recv_sems):
        x, y, c = _coords()
        me = 4 * x + 2 * y + c
        call_ref[me] = c_ref[...]
        sends = []
        for r in range(1, N_DEV):
            cp = pltpu.make_async_remote_copy(c_ref, call_ref.at[me], send_sems.at[0, r], recv_sems.at[0, r],
                                              device_id=_peer(r), device_id_type=MESH)
            cp.start()
            sends.append(cp)
        for r in range(1, N_DEV):
            pltpu.make_async_remote_copy(c_ref, call_ref.at[me], send_sems.at[0, r], recv_sems.at[0, r],
                                         device_id=_peer(r), device_id_type=MESH).wait_recv()
        cv = call_ref[...].reshape(8 * N_DEV, D)
        ca = (cv * _sigmoid(cv)).astype(BF16)
        ca_ref[...] = ca
        part_ref[...] = _dot(ca, w_ref[...].astype(BF16)).reshape(N_DEV, 8, W)
        mod_ref[me] = part_ref[me] + b_ref[me]
        for r in range(1, N_DEV):
            cp = pltpu.make_async_remote_copy(part_ref.at[me ^ r], mod_ref.at[me], send_sems.at[1, r],
                                              recv_sems.at[1, r], device_id=_peer(r), device_id_type=MESH)
            cp.start()
            sends.append(cp)
        for r in range(1, N_DEV):
            pltpu.make_async_remote_copy(part_ref.at[me ^ r], mod_ref.at[me], send_sems.at[1, r],
                                         recv_sems.at[1, r], device_id=_peer(r), device_id_type=MESH).wait_recv()
            mod_ref[me ^ r] = mod_ref[me ^ r] + b_ref[me ^ r]
        for cp in sends:
            cp.wait_send()

    vm = pl.BlockSpec(memory_space=pltpu.VMEM)
    return pl.pallas_call(
        body, name="mod_fwd", in_specs=[vm, vm, vm], out_specs=[vm, vm],
        out_shape=[jax.ShapeDtypeStruct((N_DEV, 8, W), F32), jax.ShapeDtypeStruct((8 * N_DEV, D), BF16)],
        scratch_shapes=[pltpu.VMEM((N_DEV, 8, D), F32), pltpu.VMEM((N_DEV, 8, W), F32),
                        pltpu.SemaphoreType.DMA((2, N_DEV)), pltpu.SemaphoreType.DMA((2, N_DEV))],
        compiler_params=_params(),
    )(c_tile, w_mod, b_mod3)


N_MODVEC = N_MOD * D
N_VEC = N_MODVEC + N_SMALL


def _mod_bwd(allvec, ca, me_idx):
    W = N_MODVEC // N_DEV

    def body(me_ref, all_ref, cols_ref, ca_ref, gw_ref, sum_ref):
        in_first_row = lax.broadcasted_iota(jnp.int32, (N_DEV, 8, W), 1) == 0
        dm = jnp.where(in_first_row, cols_ref[...], 0.0).reshape(8 * N_DEV, W)
        gw_ref[...] = _dot_tn(ca_ref[...], dm.astype(BF16))
        total = all_ref[0]
        for k in range(1, N_DEV):
            total = total + all_ref[k]
        sum_ref[...] = total

    return pl.pallas_call(
        body, name="mod_bwd",
        grid_spec=pltpu.PrefetchScalarGridSpec(
            num_scalar_prefetch=1, grid=(1,),
            in_specs=[pl.BlockSpec((N_DEV, 1, N_VEC), lambda i, me: (0, 0, 0)),
                      pl.BlockSpec((N_DEV, 1, W), lambda i, me: (0, 0, me[0])),
                      pl.BlockSpec((8 * N_DEV, D), lambda i, me: (0, 0))],
            out_specs=[pl.BlockSpec((D, W), lambda i, me: (0, 0)), pl.BlockSpec((1, N_VEC), lambda i, me: (0, 0))]),
        out_shape=[jax.ShapeDtypeStruct((D, W), F32), jax.ShapeDtypeStruct((1, N_VEC), F32)],
        compiler_params=_params(("arbitrary",)),
    )(me_idx, allvec, allvec, ca)


def _wgather(shards):
    n = len(shards)

    def body(*refs):
        ins, outs, token = refs[:n], refs[n:2 * n], refs[2 * n]
        send_sems, recv_sems, local_sems = refs[2 * n + 1:]
        token[...] = jnp.zeros_like(token)
        x, y, c = _coords()
        me = 4 * x + 2 * y + c
        sib = (x, y, 1 - c)
        chips = [(1 - x, y), (x, 1 - y), (1 - x, 1 - y)]

        def copy(k, slot, block, to, src=None):
            return pltpu.make_async_remote_copy(
                src_ref=outs[k].at[block] if src is None else src, dst_ref=outs[k].at[block],
                send_sem=send_sems.at[k, slot], recv_sem=recv_sems.at[k, slot], device_id=to, device_id_type=MESH)

        local = [pltpu.make_async_copy(ins[k], outs[k].at[me], local_sems.at[k]) for k in range(n)]
        for cp in local:
            cp.start()
        first = []
        for k in range(n):
            first.append(copy(k, 0, me, sib, src=ins[k]))
            for j, chip in enumerate(chips):
                first.append(copy(k, 1 + j, me, (*chip, c), src=ins[k]))
        for cp in first:
            cp.start()
        passed = []
        for j, (cx, cy) in enumerate(chips):
            for k in range(n):
                blk = 4 * cx + 2 * cy + c
                copy(k, 1 + j, blk, sib).wait_recv()
                cp = copy(k, 4 + j, blk, sib)
                cp.start()
                passed.append(cp)
        for k in range(n):
            copy(k, 0, 4 * x + 2 * y + (1 - c), sib).wait_recv()
            for j, (cx, cy) in enumerate(chips):
                copy(k, 4 + j, 4 * cx + 2 * cy + (1 - c), sib).wait_recv()
        for cp in first + passed:
            cp.wait_send()
        for cp in local:
            cp.wait()

    anyspec = pl.BlockSpec(memory_space=pl.ANY)
    return pl.pallas_call(
        body, name="wgather", in_specs=[anyspec] * n,
        out_specs=[anyspec] * n + [pl.BlockSpec(memory_space=pltpu.VMEM)],
        out_shape=[jax.ShapeDtypeStruct((N_DEV,) + s.shape, s.dtype) for s in shards]
        + [jax.ShapeDtypeStruct((8, 128), F32)],
        scratch_shapes=[pltpu.SemaphoreType.DMA((n, 7)), pltpu.SemaphoreType.DMA((n, 7)),
                        pltpu.SemaphoreType.DMA((n,))],
    )(*shards)


class _GatherCopies:
    def __init__(self, lands, send_sems, recv_sems, k0=0, batches=None):
        x, y, c = _coords()
        me = 4 * x + 2 * y + c
        sib = (x, y, 1 - c)
        chips = [(1 - x, y), (x, 1 - y), (1 - x, 1 - y)]

        def copy(k, slot, block, to):
            return pltpu.make_async_remote_copy(
                src_ref=lands[k].at[block], dst_ref=lands[k].at[block],
                send_sem=send_sems.at[7 * (k0 + k) + slot], recv_sem=recv_sems.at[7 * (k0 + k) + slot],
                device_id=to, device_id_type=MESH)

        n = len(lands)
        self.first = [copy(k, 0, me, sib) for k in range(n)]
        for batch in batches or [range(n)]:
            self.first += [copy(k, 1 + j, me, (cx, cy, c)) for j, (cx, cy) in enumerate(chips) for k in batch]
        self.landed = [copy(k, 1 + j, 4 * cx + 2 * cy + c, sib) for j, (cx, cy) in enumerate(chips) for k in range(n)]
        self.passed = [copy(k, 4 + j, 4 * cx + 2 * cy + c, sib) for j, (cx, cy) in enumerate(chips) for k in range(n)]
        self.from_sib = [copy(k, 0, 4 * x + 2 * y + (1 - c), sib) for k in range(n)]
        self.from_sib += [copy(k, 4 + j, 4 * cx + 2 * cy + (1 - c), sib) for j, (cx, cy) in enumerate(chips)
                          for k in range(n)]


def _gather_start(lands, *, name, batches=None):
    n = len(lands)

    def body(*refs):
        for cp in _GatherCopies(refs[:n], refs[n], refs[n + 1], batches=batches).first:
            cp.start()
        refs[-1][...] = jnp.zeros_like(refs[-1])

    out = pl.pallas_call(
        body, name=name,
        out_shape=(pltpu.SemaphoreType.DMA((7 * n,)), pltpu.SemaphoreType.DMA((7 * n,)),
                   *[pltpu.HBM(l.shape, l.dtype) for l in lands], jax.ShapeDtypeStruct((8, 128), F32)),
        in_specs=[HBM_SPEC] * n,
        out_specs=(SEM_SPEC, SEM_SPEC, *[HBM_SPEC] * n, pl.BlockSpec(memory_space=pltpu.VMEM)),
        input_output_aliases={i: 2 + i for i in range(n)},
        compiler_params=pltpu.CompilerParams(has_side_effects=DATAFLOW),
    )(*[_in_hbm(l) for l in lands])
    return out[0], out[1], list(out[2:2 + n]), out[-1]


def _gather_pass(send_sems, recv_sems, lands, after, *, name, stage, k0=0):
    n = len(lands)

    def body(*refs):
        cps = _GatherCopies(refs[:n], refs[n], refs[n + 1], k0)
        if stage == "landed":
            for cp in cps.landed:
                cp.wait_recv()
        else:
            for cp in cps.passed:
                cp.start()
        refs[-1][...] = jnp.zeros_like(refs[-1])

    out = pl.pallas_call(
        body, name=name,
        out_shape=(*[pltpu.HBM(l.shape, l.dtype) for l in lands], jax.ShapeDtypeStruct((8, 128), F32)),
        in_specs=[HBM_SPEC] * n + [SEM_SPEC, SEM_SPEC] + [pl.BlockSpec(memory_space=pl.ANY)] * len(after),
        out_specs=(*[HBM_SPEC] * n, pl.BlockSpec(memory_space=pltpu.VMEM)),
        input_output_aliases={i: i for i in range(n)},
        compiler_params=pltpu.CompilerParams(has_side_effects=DATAFLOW),
    )(*lands, send_sems, recv_sems, *after)
    return list(out[:n]), out[-1]


def _gather_end(send_sems, recv_sems, lands, after, *, name, k0=0):
    n = len(lands)

    def body(*refs):
        cps = _GatherCopies(refs[:n], refs[n], refs[n + 1], k0)
        for cp in cps.from_sib:
            cp.wait_recv()
        for cp in cps.first + cps.passed:
            cp.wait_send()

    out = pl.pallas_call(
        body, name=name,
        out_shape=[pltpu.HBM(l.shape, l.dtype) for l in lands],
        in_specs=[HBM_SPEC] * n + [SEM_SPEC, SEM_SPEC] + [pl.BlockSpec(memory_space=pl.ANY)] * len(after),
        out_specs=[HBM_SPEC] * n,
        input_output_aliases={i: i for i in range(n)},
        compiler_params=pltpu.CompilerParams(has_side_effects=DATAFLOW),
    )(*lands, send_sems, recv_sems, *after)
    return list(out)


def _d2d_copies(grads, lands, send_sems, recv_sems):
    x, y, c = _coords()
    return [pltpu.make_async_remote_copy(
        src_ref=grads[k].at[2 * q + (1 - c)], dst_ref=lands[k].at[q],
        send_sem=send_sems.at[4 * k + q], recv_sem=recv_sems.at[4 * k + q],
        device_id=(x, y, 1 - c), device_id_type=MESH) for k in range(len(grads)) for q in range(4)]


def _vec_copies(srcs, lands, send_sems, recv_sems):
    x, y, c = _coords()
    me = 4 * x + 2 * y + c
    return [pltpu.make_async_remote_copy(
        src_ref=lands[0].at[me], dst_ref=lands[0].at[me], send_sem=send_sems.at[r - 1], recv_sem=recv_sems.at[r - 1],
        device_id=_peer(r), device_id_type=MESH) for r in range(1, N_DEV)]


def _chipsum(gs, sibs, cidx, *, name):
    n = len(gs)

    def body(c_ref, *refs):
        for k in range(n):
            refs[2 * n + k][...] = (refs[k][...].astype(F32) + refs[n + k][...].astype(F32)).astype(refs[2 * n + k].dtype)

    mine = [pl.BlockSpec((1,) + g.shape[1:], lambda q, c_ref: (2 * q + c_ref[0], 0, 0)) for g in gs]
    other = [pl.BlockSpec((1,) + g.shape[1:], lambda q, c_ref: (q, 0, 0)) for g in gs]
    return pl.pallas_call(
        body, name=name,
        grid_spec=pltpu.PrefetchScalarGridSpec(num_scalar_prefetch=1, grid=(4,), in_specs=mine + other, out_specs=other),
        out_shape=[jax.ShapeDtypeStruct((4,) + g.shape[1:], g.dtype) for g in gs],
        compiler_params=_params(("arbitrary",)),
    )(cidx, *gs, *sibs)


HBM_SPEC = pl.BlockSpec(memory_space=pltpu.HBM)
SEM_SPEC = pl.BlockSpec(memory_space=pltpu.SEMAPHORE)
DATAFLOW = pltpu.SideEffectType.DATAFLOW_SIDE_EFFECTING


def _in_hbm(a):
    return pltpu.with_memory_space_constraint(a, pltpu.HBM)


def _ici_copies(sums, lands, send_sems, recv_sems):
    x, y, c = _coords()
    chips = [(1 - x, y), (x, 1 - y), (1 - x, 1 - y)]
    cps = []
    for k in range(len(sums)):
        for j, (cx, cy) in enumerate(chips):
            cps.append(pltpu.make_async_remote_copy(
                src_ref=sums[k].at[2 * cx + cy], dst_ref=lands[k].at[j],
                send_sem=send_sems.at[3 * k + j], recv_sem=recv_sems.at[3 * k + j],
                device_id=(cx, cy, c), device_id_type=MESH))
    return cps


def _split_start(copies, srcs, lands, n_sems, after, *, name):
    ns, nl = len(srcs), len(lands)

    def body(*refs):
        for cp in copies(refs[:ns], refs[ns:ns + nl], refs[ns + nl + len(after)], refs[ns + nl + len(after) + 1]):
            cp.start()
        refs[-1][...] = jnp.zeros_like(refs[-1])

    bufs = [_in_hbm(a) for a in list(srcs) + list(lands)]
    out = pl.pallas_call(
        body, name=name,
        out_shape=(pltpu.SemaphoreType.DMA((n_sems,)), pltpu.SemaphoreType.DMA((n_sems,)),
                   *[pltpu.HBM(a.shape, a.dtype) for a in bufs], jax.ShapeDtypeStruct((8, 128), F32)),
        in_specs=[HBM_SPEC] * len(bufs) + [pl.BlockSpec(memory_space=pl.ANY)] * len(after),
        out_specs=(SEM_SPEC, SEM_SPEC, *[HBM_SPEC] * len(bufs), pl.BlockSpec(memory_space=pltpu.VMEM)),
        input_output_aliases={i: 2 + i for i in range(len(bufs))},
        compiler_params=pltpu.CompilerParams(has_side_effects=DATAFLOW),
    )(*bufs, *after)
    return out[0], out[1], list(out[2:2 + ns]), list(out[2 + ns:2 + ns + nl]), out[-1]


def _split_wait(copies, send_sems, recv_sems, srcs, lands, after, *, name):
    ns, nl = len(srcs), len(lands)

    def body(*refs):
        for cp in copies(refs[:ns], refs[ns:ns + nl], refs[ns + nl], refs[ns + nl + 1]):
            cp.wait_send()
            cp.wait_recv()

    out = pl.pallas_call(
        body, name=name,
        out_shape=[pltpu.HBM(a.shape, a.dtype) for a in list(srcs) + list(lands)],
        in_specs=[HBM_SPEC] * (ns + nl) + [SEM_SPEC, SEM_SPEC] + [pl.BlockSpec(memory_space=pl.ANY)] * len(after),
        out_specs=[HBM_SPEC] * (ns + nl),
        input_output_aliases={i: i for i in range(ns + nl)},
        compiler_params=pltpu.CompilerParams(has_side_effects=DATAFLOW),
    )(*srcs, *lands, send_sems, recv_sems, *after)
    return list(out[:ns]), list(out[ns:])


ADAM_C1 = 1.0 / (1.0 - ADAM_B1 ** ADAM_STEP)
ADAM_C2 = 1.0 / (1.0 - ADAM_B2 ** ADAM_STEP)


def _adam_math(w, g, m, v):
    m2 = ADAM_B1 * m + (1.0 - ADAM_B1) * g
    v2 = ADAM_B2 * v + (1.0 - ADAM_B2) * (g * g)
    return -ADAM_LR * ((m2 * ADAM_C1) / (jnp.sqrt(v2 * ADAM_C2) + ADAM_EPS) + ADAM_WD * w), m2, v2


def _adamw(w, g, m, v, *, name):
    R, C = w.shape
    tr = R if R <= 512 else 256

    def body(w_ref, g_ref, m_ref, v_ref, d_ref, nm_ref, nv_ref):
        d_ref[...], nm_ref[...], nv_ref[...] = _adam_math(w_ref[...], g_ref[...], m_ref[...], v_ref[...])

    blk = pl.BlockSpec((tr, C), lambda i: (i, 0))
    return pl.pallas_call(
        body, name=name, grid=(R // tr,), in_specs=[blk] * 4, out_specs=[blk] * 3,
        out_shape=[jax.ShapeDtypeStruct((R, C), F32)] * 3,
        compiler_params=_params(("parallel",)),
    )(w, g, m, v)


def _adamw_rs(wmv, cs, rcv, qidx, *, name):
    n = len(wmv)
    r, cc = wmv[0][0].shape
    tr = r // 2 if r % 32 == 0 and r > 128 else r

    def body(q_ref, *refs):
        ins, outs = refs[:5 * n], refs[5 * n:]
        for k in range(n):
            w_ref, m_ref, v_ref, c_ref, r_ref = ins[5 * k:5 * k + 5]
            g_ref, d_ref, nm_ref, nv_ref = outs[4 * k:4 * k + 4]
            g = ((c_ref[0].astype(F32) + r_ref[0].astype(F32)) + r_ref[1].astype(F32)) + r_ref[2].astype(F32)
            g_ref[...] = g
            d_ref[...], nm_ref[...], nv_ref[...] = _adam_math(w_ref[...], g, m_ref[...], v_ref[...])

    blk = pl.BlockSpec((tr, cc), lambda i, q_ref: (i, 0))
    one = [blk, blk, blk, pl.BlockSpec((1, tr, cc), lambda i, q_ref: (q_ref[0], i, 0)),
           pl.BlockSpec((3, tr, cc), lambda i, q_ref: (0, i, 0))]
    out = pl.pallas_call(
        body, name=name,
        grid_spec=pltpu.PrefetchScalarGridSpec(num_scalar_prefetch=1, grid=(r // tr,), in_specs=one * n,
                                               out_specs=[blk] * (4 * n)),
        out_shape=[jax.ShapeDtypeStruct((r, cc), F32)] * (4 * n),
        compiler_params=_params(("arbitrary",)),
    )(qidx, *[a for (w, m, v), c, rc in zip(wmv, cs, rcv) for a in (w, m, v, c, rc)])
    return [tuple(out[4 * k:4 * k + 4]) for k in range(n)]


SMALL_PARAMS = ("norm_ffn1", "norm_mix", "norm_ffn2", "norm_final", "q_norm", "kv_norm", "sinks", "rel_bias", "b_mod")


def _adamw_small(gvec, wmv):
    widths = [wmv[3 * i].shape[1] for i in range(len(SMALL_PARAMS))]

    def body(*refs):
        g_all = refs[0]
        ins = refs[1:1 + 3 * len(SMALL_PARAMS)]
        outs = refs[1 + 3 * len(SMALL_PARAMS):]
        off = N_MODVEC
        for i, name in enumerate(SMALL_PARAMS):
            g_ref, d_ref, nm_ref, nv_ref = outs[4 * i:4 * i + 4]
            w_ref, m_ref, v_ref = ins[3 * i:3 * i + 3]
            start = 0 if name == "b_mod" else off
            g = g_all[:, start:start + widths[i]]
            g_ref[...] = g
            d_ref[...], nm_ref[...], nv_ref[...] = _adam_math(w_ref[...], g, m_ref[...], v_ref[...])
            if name != "b_mod":
                off += dict(SMALL_LAYOUT)[name]

    vm = pl.BlockSpec(memory_space=pltpu.VMEM)
    n_out = 4 * len(SMALL_PARAMS)
    out = pl.pallas_call(
        body, name="adamw_small", in_specs=[vm] * (1 + len(wmv)), out_specs=[vm] * n_out,
        out_shape=[jax.ShapeDtypeStruct((1, widths[i // 4]), F32) for i in range(n_out)],
        compiler_params=_params(),
    )(gvec, *wmv)
    return {name: out[4 * i:4 * i + 4] for i, name in enumerate(SMALL_PARAMS)}


TRANSPOSED = ("g1T", "u1T", "g3T", "u3T", "w_inT", "w_uqT")


def kernel(x, c, w_mod, b_mod, norm_ffn1, ffn1_gate, ffn1_up, ffn1_down, norm_mix, w_in, q_norm, kv_norm, w_uq, w_ukv, sinks, w_o, norm_ffn2, ffn2_gate, ffn2_up, ffn2_down, rel_bias, norm_final, loss_target, m_w_mod, m_b_mod, m_norm_ffn1, m_ffn1_gate, m_ffn1_up, m_ffn1_down, m_norm_mix, m_w_in, m_q_norm, m_kv_norm, m_w_uq, m_w_ukv, m_sinks, m_w_o, m_norm_ffn2, m_ffn2_gate, m_ffn2_up, m_ffn2_down, m_rel_bias, m_norm_final, v_w_mod, v_b_mod, v_norm_ffn1, v_ffn1_gate, v_ffn1_up, v_ffn1_down, v_norm_mix, v_w_in, v_q_norm, v_kv_norm, v_w_uq, v_w_ukv, v_sinks, v_w_o, v_norm_ffn2, v_ffn2_gate, v_ffn2_up, v_ffn2_down, v_rel_bias, v_norm_final):
    mx, my, mc = _coords()
    cidx = jnp.reshape(mc, (1,)).astype(jnp.int32)
    qidx = jnp.reshape(2 * mx + my, (1,)).astype(jnp.int32)
    WM = w_mod.shape[2]

    c_tile = jnp.pad(c, ((0, 7), (0, 0)))
    b_mod3 = jnp.pad(b_mod.reshape(N_DEV, 1, WM), ((0, 0), (0, 7), (0, 0)))
    mod3, ca = _mod_fwd(c_tile, w_mod[0], b_mod3)
    mod9 = mod3[:, 0, :].reshape(N_MOD, D)

    shards = {"g1T": ffn1_gate[0].T.astype(BF16), "u1T": ffn1_up[0].T.astype(BF16), "d1": ffn1_down[0].astype(BF16),
              "g3T": ffn2_gate[0].T.astype(BF16), "u3T": ffn2_up[0].T.astype(BF16), "d3": ffn2_down[0].astype(BF16),
              "w_inT": w_in[0].T, "w_uqT": w_uq[0].T.astype(BF16), "w_ukv": w_ukv[0].astype(BF16),
              "w_o": w_o[0].astype(BF16)}
    me = 4 * mx + 2 * my + mc
    groups = {"ffn1": ("g1T", "u1T", "d1"), "mixer": ("w_inT", "w_uqT", "w_ukv", "w_o"), "ffn2": ("g3T", "u3T", "d3")}
    arriving = {}

    def as_weights(group, gathered):
        return {k: g if k == "w_ukv" else g.reshape(N_DEV * g.shape[1], g.shape[2])
                for k, g in zip(groups[group], gathered)}

    later = groups["mixer"] + groups["ffn2"]
    place = {"mixer": 0, "ffn2": len(groups["mixer"])}

    def start_gather(token):
        lands = []
        for k in later:
            sh = shards[k] + token[0, 0].astype(shards[k].dtype)
            lands.append(lax.dynamic_update_slice(lax.empty((N_DEV,) + sh.shape, sh.dtype), sh[None], (me, 0, 0)))
        batches = [range(k0, k0 + len(groups[group])) for group, k0 in place.items()]
        send, recv, lands, started = _gather_start(lands, name="gather_start", batches=batches)
        for group, k0 in place.items():
            arriving[group] = (send, recv, lands[k0:k0 + len(groups[group])])
        return started

    def fetch(group, after, vecs):
        if group == "ffn1":
            *gathered, token = _wgather([shards[k] + ca[1, 0].astype(shards[k].dtype) for k in groups["ffn1"]])
            return as_weights("ffn1", gathered), vecs + start_gather(token)[0:1, 0:1]

        def pass_on(group, after):
            send, recv, lands = arriving[group]
            lands, token = _gather_pass(send, recv, lands, after, name="gather_landed_" + group, stage="landed",
                                        k0=place[group])
            lands, token = _gather_pass(send, recv, lands, [token], name="gather_onward_" + group, stage="onward",
                                        k0=place[group])
            arriving[group] = (send, recv, lands)
            return token

        if group == "ffn2_on_its_way":
            return None, vecs + pass_on("ffn2", after)[0:1, 0:1]
        if group == "mixer":
            after = [pass_on("mixer", after)]
        send, recv, lands = arriving[group]
        return as_weights(group, _gather_end(send, recv, lands, after, name="gather_end_" + group,
                                             k0=place[group])), vecs

    norms ={"ffn1": norm_ffn1, "mix": norm_mix, "ffn2": norm_ffn2, "final": norm_final.reshape(1, D)}
    in_flight = {}

    def on_grads(group, g, after, vecs, before_ici=()):
        if g is not None:
            names = list(g)
            by_dest = [g[k] if k == "w_ukv" else g[k].reshape((N_DEV, g[k].shape[0] // N_DEV) + g[k].shape[1:])
                       for k in names]
            lands = [lax.empty((4,) + a.shape[1:], a.dtype) for a in by_dest]
            send, recv, by_dest, lands, token = _split_start(_d2d_copies, by_dest, lands, 4 * len(names), after,
                                                             name="rs_d2d_start_" + group)
            in_flight[group] = (names, send, recv, by_dest, lands)
            return vecs + token[0:1, 0:1]
        names, send, recv, by_dest, lands = in_flight[group]
        by_dest, from_sib = _split_wait(_d2d_copies, send, recv, by_dest, lands, after, name="rs_d2d_wait_" + group)
        sums = _chipsum(by_dest, from_sib, cidx, name="chipsum_" + group)
        lands = [lax.empty((3,) + s.shape[1:], s.dtype) for s in sums]
        send, recv, sums, lands, token = _split_start(_ici_copies, sums, lands, 3 * len(names), list(before_ici),
                                                      name="rs_ici_start_" + group)
        in_flight[group] = (names, send, recv, sums, lands, token)
        return vecs + token[0:1, 0:1]

    _, grad_x, _, small, dmod9 = _local_step(
        x[0], loss_target[0], mod9, norms, sinks, rel_bias, q_norm, kv_norm, fetch, on_grads=on_grads)

    vec = jnp.concatenate([dmod9.reshape(N_MODVEC), small]).reshape(1, 1, N_VEC)
    allvec = lax.dynamic_update_slice(lax.empty((N_DEV, 1, N_VEC), F32), vec, (me, 0, 0))
    vsend, vrecv, _, (allvec,), vec_started = _split_start(_vec_copies, [], [allvec], N_DEV - 1, [], name="vec_start")
    on_grads("ffn1", None, [grad_x], jnp.zeros((1, 1), F32), before_ici=[vec_started])

    owners = {"g1T": ("ffn1_gate", ffn1_gate, m_ffn1_gate, v_ffn1_gate), "u1T": ("ffn1_up", ffn1_up, m_ffn1_up, v_ffn1_up),
              "d1": ("ffn1_down", ffn1_down, m_ffn1_down, v_ffn1_down),
              "g3T": ("ffn2_gate", ffn2_gate, m_ffn2_gate, v_ffn2_gate), "u3T": ("ffn2_up", ffn2_up, m_ffn2_up, v_ffn2_up),
              "d3": ("ffn2_down", ffn2_down, m_ffn2_down, v_ffn2_down),
              "w_inT": ("w_in", w_in, m_w_in, v_w_in), "w_uqT": ("w_uq", w_uq, m_w_uq, v_w_uq),
              "w_ukv": ("w_ukv", w_ukv, m_w_ukv, v_w_ukv), "w_o": ("w_o", w_o, m_w_o, v_w_o)}
    res, done = {}, []

    def finish(group, after):
        names, send, recv, sums, lands, _ = in_flight[group]
        sums, lands = _split_wait(_ici_copies, send, recv, sums, lands, after, name="rs_ici_wait_" + group)
        there = lambda k, a: a[0].T if k in TRANSPOSED else a[0]
        back = lambda k, a: a.T[None] if k in TRANSPOSED else a[None]
        wmv = [tuple(there(k, a) for a in owners[k][1:]) for k in names]
        if len({w.shape for w, _, _ in wmv}) == 1:
            outs = _adamw_rs(wmv, sums, lands, qidx, name="adamw_" + group)
        else:
            outs = [_adamw_rs([t], [cs], [rc], qidx, name="adamw_" + owners[k][0])[0]
                    for k, t, cs, rc in zip(names, wmv, sums, lands)]
        for k, out in zip(names, outs):
            done.append(out[3])
            res[owners[k][0]] = tuple(back(k, a) for a in out)

    ffn1_started = in_flight["ffn1"][5]
    finish("ffn2", [ffn1_started])
    finish("mixer", [ffn1_started])

    _, (allvec,) = _split_wait(_vec_copies, vsend, vrecv, [], [allvec], [ffn1_started], name="vec_wait")
    g_wmod, gvec = _mod_bwd(allvec, ca, jnp.reshape(me, (1,)).astype(jnp.int32))
    loss = gvec[0, N_MODVEC + LOSS_SLOT]
    res["w_mod"] = tuple(a[None] for a in (g_wmod,) + tuple(_adamw(w_mod[0], g_wmod, m_w_mod[0], v_w_mod[0],
                                                                    name="adamw_w_mod")))
    small_in = {"norm_ffn1": (norm_ffn1, m_norm_ffn1, v_norm_ffn1), "norm_mix": (norm_mix, m_norm_mix, v_norm_mix),
                "norm_ffn2": (norm_ffn2, m_norm_ffn2, v_norm_ffn2), "norm_final": (norm_final, m_norm_final, v_norm_final),
                "q_norm": (q_norm, m_q_norm, v_q_norm), "kv_norm": (kv_norm, m_kv_norm, v_kv_norm),
                "sinks": (sinks, m_sinks, v_sinks), "rel_bias": (rel_bias, m_rel_bias, v_rel_bias),
                "b_mod": (b_mod, m_b_mod, v_b_mod)}
    small_out = _adamw_small(gvec, [a.reshape(1, -1) for k in SMALL_PARAMS for a in small_in[k]])
    for k in SMALL_PARAMS:
        res[k] = tuple(a.reshape(small_in[k][0].shape) for a in small_out[k])

    finish("ffn1", done + [res["w_mod"][3], small_out["b_mod"][3]])

    order = ("w_mod", "b_mod", "norm_ffn1", "ffn1_gate", "ffn1_up", "ffn1_down", "norm_mix", "w_in", "q_norm",
             "kv_norm", "w_uq", "w_ukv", "sinks", "w_o", "norm_ffn2", "ffn2_gate", "ffn2_up", "ffn2_down",
             "rel_bias", "norm_final")
    return (loss, grad_x[None]) + tuple(res[nm][kind] for kind in range(4) for nm in order)
```

```python
import functools
import math

import numpy as np
import jax
import jax.numpy as jnp
from jax import lax
from jax.experimental import pallas as pl
from jax.experimental.pallas import tpu as pltpu

F32 = jnp.float32
BF16 = jnp.bfloat16
MESH = pl.DeviceIdType.MESH

N_DEV = 8
D = 1024
D_FF = 2816
EPS = 1e-6
N_MOD = 9
SWA_HEADS = 8
SWA_DH = 64
WINDOW = 128
MLA_HEADS = 4
MLA_NOPE = 128
MLA_ROPE = 64
MLA_V = 128
MLA_QR = 256
MLA_KVR = 128
ROPE_THETA = 10000.0
NUM_BUCKETS = 32
D_IN = 1216
D_IN_PAD = 1280
SWA_SCALE = SWA_DH ** -0.5
MLA_SCALE = (MLA_NOPE + MLA_ROPE) ** -0.5

ADAM_LR = 0.001
ADAM_B1 = 0.9
ADAM_B2 = 0.999
ADAM_EPS = 1e-08
ADAM_WD = 0.01
ADAM_STEP = 10

V7X_VMEM_LIMIT = 56 * 1024 * 1024

NT_DIMS = (((1,), (1,)), ((), ()))
TN_DIMS = (((0,), (0,)), ((), ()))


def _dot(a, b):
    return jnp.dot(a, b, preferred_element_type=F32)


def _dot_nt(a, b):
    return lax.dot_general(a, b, NT_DIMS, preferred_element_type=F32)


def _dot_tn(a, b):
    return lax.dot_general(a, b, TN_DIMS, preferred_element_type=F32)


def _params(sem=None):
    return pltpu.CompilerParams(dimension_semantics=sem, vmem_limit_bytes=V7X_VMEM_LIMIT)


def _rstd(x):
    return lax.rsqrt(jnp.mean(x * x, axis=-1, keepdims=True) + EPS)


def _rms_bwd(dy, xhat, r):
    return r * (dy - xhat * jnp.mean(dy * xhat, axis=-1, keepdims=True))


def _sigmoid(a):
    return 1.0 / (1.0 + jnp.exp(-a))


def _ffn_fwd(x, vecs, wgT, wuT, wd, *, name, tm=512, tf=256):
    S, F = x.shape[0], wd.shape[0]
    tm = min(tm, S)
    ni, nj = S // tm, F // tf

    def body(x_ref, vec_ref, wg_ref, wu_ref, wd_ref, xo_ref, h_ref, a_ref, b_ref, f_ref, acc_ref):
        j = pl.program_id(1)

        @pl.when(j == 0)
        def _():
            xv = x_ref[...]
            hn = xv * _rstd(xv) * vec_ref[0:1, :]
            h_ref[...] = (hn * (1.0 + vec_ref[2:3, :]) + vec_ref[1:2, :]).astype(BF16)

        h = h_ref[...]
        a = _dot_nt(h, wg_ref[...])
        b = _dot_nt(h, wu_ref[...])
        a_ref[...] = a.astype(BF16)
        b_ref[...] = b.astype(BF16)
        part = _dot((a * _sigmoid(a) * b).astype(BF16), wd_ref[...])

        def finish(f):
            f_ref[...] = f
            xo_ref[...] = x_ref[...] + (0.5 * vec_ref[3:4, :]) * f

        if nj == 1:
            finish(part)
        else:
            @pl.when(j == 0)
            def _():
                acc_ref[...] = part

            @pl.when((j > 0) & (j < nj - 1))
            def _():
                acc_ref[...] += part

            @pl.when(j == nj - 1)
            def _():
                finish(acc_ref[...] + part)

    row = pl.BlockSpec((tm, D), lambda i, j: (i, 0))
    wspec = pl.BlockSpec((tf, D), lambda i, j: (j, 0), pipeline_mode=pl.Buffered(1) if nj == 1 else None)
    act = pl.BlockSpec((tm, tf), lambda i, j: (i, j))
    return pl.pallas_call(
        body, name=name, grid=(ni, nj),
        in_specs=[row, pl.BlockSpec((8, D), lambda i, j: (0, 0)), wspec, wspec, wspec],
        out_specs=[row, row, act, act, row],
        out_shape=[jax.ShapeDtypeStruct((S, D), F32), jax.ShapeDtypeStruct((S, D), BF16),
                   jax.ShapeDtypeStruct((S, F), BF16), jax.ShapeDtypeStruct((S, F), BF16),
                   jax.ShapeDtypeStruct((S, D), F32)],
        scratch_shapes=[pltpu.VMEM((tm, D) if nj > 1 else (8, 128), F32)],
        compiler_params=_params(("parallel", "arbitrary")),
    )(x, vecs, wgT, wuT, wd)


def _ffn_bwd_main(h, df, a, b, wgT, wuT, wd, *, name, after=(), tm=512, tf=256):
    S = h.shape[0]
    tm = min(tm, S)
    ni, nj = S // tm, D_FF // tf

    def body(h_hbm, df_hbm, a_ref, b_ref, wg_ref, wu_ref, wd_ref, *rest):
        gg_ref, gu_ref, gd_ref, dh_hbm, h_v, df_v, dh_v, gg_acc, gu_acc, gd_acc, sem = rest[len(after):]
        j = pl.program_id(0)
        i = pl.program_id(1)

        @pl.when((j == 0) & (i == 0))
        def _():
            c1 = pltpu.make_async_copy(h_hbm, h_v, sem.at[0])
            c2 = pltpu.make_async_copy(df_hbm, df_v, sem.at[1])
            c1.start()
            c2.start()
            c1.wait()
            c2.wait()

        @pl.when(i == 0)
        def _():
            gg_acc[...] = jnp.zeros_like(gg_acc)
            gu_acc[...] = jnp.zeros_like(gu_acc)
            gd_acc[...] = jnp.zeros_like(gd_acc)

        rows = pl.ds(pl.multiple_of(i * tm, tm), tm)
        hi = h_v[rows, :]
        dfi = df_v[rows, :]
        av = a_ref[...].astype(F32)
        bv = b_ref[...].astype(F32)
        sg = _sigmoid(av)
        sa = av * sg
        hsw = (sa * bv).astype(BF16)
        dhsw = _dot_nt(dfi, wd_ref[...])
        da = (dhsw * bv * (sg * (1.0 + av * (1.0 - sg)))).astype(BF16)
        db = (dhsw * sa).astype(BF16)
        gd_acc[...] += _dot_tn(hsw, dfi)
        gg_acc[...] += _dot_tn(da, hi)
        gu_acc[...] += _dot_tn(db, hi)
        dh = _dot(da, wg_ref[...]) + _dot(db, wu_ref[...])

        @pl.when(j == 0)
        def _():
            dh_v[rows, :] = dh

        @pl.when(j > 0)
        def _():
            dh_v[rows, :] += dh

        @pl.when(i == ni - 1)
        def _():
            gg_ref[...] = gg_acc[...].astype(BF16)
            gu_ref[...] = gu_acc[...].astype(BF16)
            gd_ref[...] = gd_acc[...].astype(BF16)

        @pl.when((j == nj - 1) & (i == ni - 1))
        def _():
            c3 = pltpu.make_async_copy(dh_v, dh_hbm, sem.at[2])
            c3.start()
            c3.wait()

    anyspec = pl.BlockSpec(memory_space=pl.ANY)
    wspec = pl.BlockSpec((tf, D), lambda j, i: (j, 0))
    act = pl.BlockSpec((tm, tf), lambda j, i: (i, j))
    return pl.pallas_call(
        body, name=name, grid=(nj, ni),
        in_specs=[anyspec, anyspec, act, act, wspec, wspec, wspec] + [anyspec] * len(after),
        out_specs=[wspec, wspec, wspec, anyspec],
        out_shape=[jax.ShapeDtypeStruct((D_FF, D), BF16)] * 3 + [jax.ShapeDtypeStruct((S, D), F32)],
        scratch_shapes=[pltpu.VMEM((S, D), BF16), pltpu.VMEM((S, D), BF16), pltpu.VMEM((S, D), F32),
                        pltpu.VMEM((tf, D), F32), pltpu.VMEM((tf, D), F32), pltpu.VMEM((tf, D), F32),
                        pltpu.SemaphoreType.DMA((3,))],
        compiler_params=_params(("arbitrary", "arbitrary")),
    )(h, df, a, b, wgT, wuT, wd, *after)


def _ffn_out_bwd(dx, f, gate, df_ref, part_ref):
    df_ref[...] = ((0.5 * gate) * dx).astype(BF16)
    part_ref[3:4, :] += 0.5 * jnp.sum(dx * f, axis=0, keepdims=True)


def _norm_bwd(dh, x, dxo, vecs, *, name, below=None, tm=256):
    S = x.shape[0]

    def body(dh_ref, x_ref, dxo_ref, vec_ref, *rest):
        dx_ref, part_ref = rest[-2 if below is None else -3], rest[-1 if below is None else -2]

        @pl.when(pl.program_id(0) == 0)
        def _():
            part_ref[...] = jnp.zeros_like(part_ref)

        dh = dh_ref[...]
        xv = x_ref[...]
        r = _rstd(xv)
        xhat = xv * r
        w = vec_ref[0:1, :]
        xn = xhat * w
        dxn = dh * (1.0 + vec_ref[2:3, :])
        part_ref[0:1, :] += jnp.sum(dxn * xhat, axis=0, keepdims=True)
        part_ref[1:2, :] += jnp.sum(dh, axis=0, keepdims=True)
        part_ref[2:3, :] += jnp.sum(dh * xn, axis=0, keepdims=True)
        dx = dxo_ref[...] + _rms_bwd(dxn * w, xhat, r)
        dx_ref[...] = dx
        if below is not None:
            _ffn_out_bwd(dx, rest[0][...], rest[1][3:4, :], rest[-1], part_ref)

    row = pl.BlockSpec((tm, D), lambda i: (i, 0))
    vec = pl.BlockSpec((8, D), lambda i: (0, 0))
    extra = [] if below is None else [row, vec]
    return pl.pallas_call(
        body, name=name, grid=(S // tm,), in_specs=[row, row, row, vec] + extra,
        out_specs=[row, vec] + ([] if below is None else [row]),
        out_shape=[jax.ShapeDtypeStruct((S, D), F32), jax.ShapeDtypeStruct((8, D), F32)]
        + ([] if below is None else [jax.ShapeDtypeStruct((S, D), BF16)]),
        compiler_params=_params(("arbitrary",)),
    )(dh, x, dxo, vecs, *([] if below is None else below))


def _head(x, tgt, nf, f, vecs, *, tm=256):
    S = x.shape[0]

    def body(x_ref, t_ref, nf_ref, f_ref, vec_ref, dx_ref, part_ref, df_ref):
        @pl.when(pl.program_id(0) == 0)
        def _():
            part_ref[...] = jnp.zeros_like(part_ref)

        xv = x_ref[...]
        r = _rstd(xv)
        xhat = xv * r
        w = nf_ref[...]
        e = xhat * w - t_ref[...]
        dy = e * (1.0 / D)
        part_ref[0:1, :] += jnp.sum(dy * xhat, axis=0, keepdims=True)
        part_ref[1:2, :] += jnp.sum(e * e) * (0.5 / D)
        dx = _rms_bwd(dy * w, xhat, r)
        dx_ref[...] = dx
        _ffn_out_bwd(dx, f_ref[...], vec_ref[3:4, :], df_ref, part_ref)

    row = pl.BlockSpec((tm, D), lambda i: (i, 0))
    vec = pl.BlockSpec((8, D), lambda i: (0, 0))
    return pl.pallas_call(
        body, name="head", grid=(S // tm,),
        in_specs=[row, row, pl.BlockSpec((1, D), lambda i: (0, 0)), row, vec],
        out_specs=[row, vec, row],
        out_shape=[jax.ShapeDtypeStruct((S, D), F32), jax.ShapeDtypeStruct((8, D), F32),
                   jax.ShapeDtypeStruct((S, D), BF16)],
        compiler_params=_params(("arbitrary",)),
    )(x, tgt, nf, f, vecs)


def _mix_in_fwd(x, vecs, w_inT, *, tm=256):
    S = x.shape[0]

    def body(x_ref, vec_ref, w_ref, h_ref, p_ref):
        xv = x_ref[...]
        hn = xv * _rstd(xv) * vec_ref[0:1, :]
        h = (hn * (1.0 + vec_ref[2:3, :]) + vec_ref[1:2, :]).astype(BF16)
        h_ref[...] = h
        p_ref[...] = _dot_nt(h, w_ref[...])

    row = pl.BlockSpec((tm, D), lambda i: (i, 0))
    return pl.pallas_call(
        body, name="mix_in_fwd", grid=(S // tm,),
        in_specs=[row, pl.BlockSpec((8, D), lambda i: (0, 0)), pl.BlockSpec((D_IN_PAD, D), lambda i: (0, 0))],
        out_specs=[row, pl.BlockSpec((tm, D_IN_PAD), lambda i: (i, 0))],
        out_shape=[jax.ShapeDtypeStruct((S, D), BF16), jax.ShapeDtypeStruct((S, D_IN_PAD), F32)],
        compiler_params=_params(("parallel",)),
    )(x, vecs, w_inT)


def _bucket_table():
    qi = np.arange(WINDOW)[:, None]
    kj = np.arange(2 * WINDOW)[None, :]
    dist = qi + WINDOW - kj
    max_exact = NUM_BUCKETS // 2
    n = np.maximum(dist, 0)
    nf = np.maximum(n, 1).astype(np.float32)
    large = max_exact + (np.log(nf / np.float32(max_exact)) / np.float32(math.log(WINDOW / max_exact))
                         * np.float32(NUM_BUCKETS - max_exact)).astype(np.int32)
    large = np.minimum(large, NUM_BUCKETS - 1)
    return np.where(n < max_exact, n, large).astype(np.int32)


def _bias_build(rel_bias, bucket):
    def body(rb_ref, bk_ref, out_ref):
        bk = bk_ref[...]
        for h in range(SWA_HEADS):
            acc = jnp.zeros((WINDOW, 2 * WINDOW), F32)
            for b in range(NUM_BUCKETS):
                acc = jnp.where(bk == b, rb_ref[b, h], acc)
            out_ref[h] = acc

    return pl.pallas_call(
        body, name="bias_build",
        in_specs=[pl.BlockSpec(memory_space=pltpu.SMEM), pl.BlockSpec(memory_space=pltpu.VMEM)],
        out_specs=pl.BlockSpec(memory_space=pltpu.VMEM),
        out_shape=jax.ShapeDtypeStruct((SWA_HEADS, WINDOW, 2 * WINDOW), F32),
    )(rel_bias, bucket)


SWA_GROUP = 4
GROUP_ROWS = SWA_GROUP * WINDOW


def _swa_valid(n):
    row = lax.broadcasted_iota(jnp.int32, (GROUP_ROWS, 2 * WINDOW), 0) % WINDOW
    col = lax.broadcasted_iota(jnp.int32, (GROUP_ROWS, 2 * WINDOW), 1)
    dist = row + WINDOW - col
    return (dist >= 0) & (dist < WINDOW) & ((col >= WINDOW) | (n > 0))


def _stack_heads(x, g):
    return jnp.concatenate([x[:, 64 * h:64 * h + 64] for h in range(SWA_GROUP * g, SWA_GROUP * (g + 1))], axis=0)


def _unstack_heads(x4):
    return jnp.concatenate([x4[WINDOW * a:WINDOW * (a + 1)] for a in range(SWA_GROUP)], axis=1)


def _group_sinks(sink_ref, g):
    head = lax.broadcasted_iota(jnp.int32, (GROUP_ROWS, 1), 0) // WINDOW
    out = jnp.full((GROUP_ROWS, 1), sink_ref[0, SWA_GROUP * g], F32)
    for a in range(1, SWA_GROUP):
        out = jnp.where(head == a, sink_ref[0, SWA_GROUP * g + a], out)
    return out


def _swa_probs(qh, kk, bias_h, sink, valid):
    s = _dot_nt(qh, kk) * SWA_SCALE + bias_h
    s = jnp.where(valid, s, -jnp.inf)
    m = jnp.maximum(jnp.max(s, axis=-1, keepdims=True), sink)
    p = jnp.exp(s - m)
    ps = jnp.exp(sink - m)
    inv = 1.0 / (jnp.sum(p, axis=-1, keepdims=True) + ps)
    return p * inv, ps * inv


def _swa_specs():
    prev = lambda n: jnp.maximum(n - 1, 0)
    return [pl.BlockSpec((WINDOW, 512), lambda n: (n, 0)),
            pl.BlockSpec((WINDOW, 128), lambda n: (n, 4)),
            pl.BlockSpec((WINDOW, 128), lambda n: (prev(n), 4)),
            pl.BlockSpec((WINDOW, 128), lambda n: (n, 5)),
            pl.BlockSpec((WINDOW, 128), lambda n: (prev(n), 5)),
            pl.BlockSpec((SWA_HEADS, WINDOW, 2 * WINDOW), lambda n: (0, 0, 0)),
            pl.BlockSpec(memory_space=pltpu.SMEM)]


def _swa_fwd(proj, bias, sinks):
    S = proj.shape[0]

    def body(q_ref, kc_ref, kp_ref, vc_ref, vp_ref, bias_ref, sink_ref, o_ref):
        valid = _swa_valid(pl.program_id(0))
        q = q_ref[...].astype(BF16)
        kfull = jnp.concatenate([kp_ref[...], kc_ref[...]], axis=0).astype(BF16)
        vfull = jnp.concatenate([vp_ref[...], vc_ref[...]], axis=0).astype(BF16)
        for g in range(SWA_HEADS // SWA_GROUP):
            kk = kfull[:, 64 * g:64 * g + 64]
            vv = vfull[:, 64 * g:64 * g + 64]
            bias4 = bias_ref[SWA_GROUP * g:SWA_GROUP * (g + 1)].reshape(GROUP_ROWS, 2 * WINDOW)
            pk, _ = _swa_probs(_stack_heads(q, g), kk, bias4, _group_sinks(sink_ref, g), valid)
            o_ref[:, 256 * g:256 * (g + 1)] = _unstack_heads(_dot(pk.astype(BF16), vv))

    return pl.pallas_call(
        body, name="swa_fwd", grid=(S // WINDOW,),
        in_specs=_swa_specs(),
        out_specs=pl.BlockSpec((WINDOW, 512), lambda n: (n, 0)),
        out_shape=jax.ShapeDtypeStruct((S, 512), F32),
        compiler_params=_params(("parallel",)),
    )(proj, proj, proj, proj, proj, bias, sinks)


def _swa_bwd(proj, bias, sinks, o, do, bucket):
    S = proj.shape[0]
    nb = S // WINDOW

    def body(q_ref, kc_ref, kp_ref, vc_ref, vp_ref, bias_ref, sink_ref, o_ref, do_ref, bk_ref,
             dq_ref, dk_ref, dv_ref, drb_ref, dsk_ref, dbias_acc):
        n = pl.program_id(0)

        @pl.when(n == 0)
        def _():
            dk_ref[...] = jnp.zeros_like(dk_ref)
            dv_ref[...] = jnp.zeros_like(dv_ref)
            dsk_ref[...] = jnp.zeros_like(dsk_ref)
            dbias_acc[...] = jnp.zeros_like(dbias_acc)
            drb_ref[...] = jnp.zeros_like(drb_ref)

        valid = _swa_valid(n)
        q = q_ref[...].astype(BF16)
        dov = do_ref[...]
        kfull = jnp.concatenate([kp_ref[...], kc_ref[...]], axis=0).astype(BF16)
        vfull = jnp.concatenate([vp_ref[...], vc_ref[...]], axis=0).astype(BF16)
        prow = pl.ds(pl.multiple_of(jnp.maximum(n - 1, 0) * WINDOW, WINDOW), WINDOW)
        crow = pl.ds(pl.multiple_of(n * WINDOW, WINDOW), WINDOW)
        ov = o_ref[...]
        for g in range(SWA_HEADS // SWA_GROUP):
            heads = slice(SWA_GROUP * g, SWA_GROUP * (g + 1))
            kk = kfull[:, 64 * g:64 * g + 64]
            vv = vfull[:, 64 * g:64 * g + 64]
            q4 = _stack_heads(q, g)
            pk, psink = _swa_probs(q4, kk, bias_ref[heads].reshape(GROUP_ROWS, 2 * WINDOW), _group_sinks(sink_ref, g), valid)
            pkb = pk.astype(BF16)
            do4 = _stack_heads(dov, g)
            dob = do4.astype(BF16)
            dp = _dot_nt(dob, vv)
            delta = jnp.sum(do4 * _stack_heads(ov, g), axis=-1, keepdims=True)
            ds = pk * (dp - delta)
            dsink = -psink * delta
            for a in range(SWA_GROUP):
                h = SWA_GROUP * g + a
                part = jnp.sum(dsink[WINDOW * a:WINDOW * (a + 1)], keepdims=True)
                dsk_ref[h:h + 1, :] += jnp.broadcast_to(part, (1, 128))
            dbias_acc[heads] += ds.reshape(SWA_GROUP, WINDOW, 2 * WINDOW)
            dsb = (ds * SWA_SCALE).astype(BF16)
            dq_ref[:, 256 * g:256 * (g + 1)] = _unstack_heads(_dot(dsb, kk))
            dkk = _dot_tn(dsb, q4)
            dvv = _dot_tn(pkb, dob)
            dk_ref[prow, 64 * g:64 * g + 64] += dkk[:WINDOW]
            dk_ref[crow, 64 * g:64 * g + 64] += dkk[WINDOW:]
            dv_ref[prow, 64 * g:64 * g + 64] += dvv[:WINDOW]
            dv_ref[crow, 64 * g:64 * g + 64] += dvv[WINDOW:]

        @pl.when(n == nb - 1)
        def _():
            bk = bk_ref[...]
            for h in range(SWA_HEADS):
                dbh = dbias_acc[h]
                for b in range(NUM_BUCKETS):
                    val = jnp.sum(jnp.where(bk == b, dbh, 0.0), keepdims=True)
                    drb_ref[b * 8 + h:b * 8 + h + 1, :] = jnp.broadcast_to(val, (1, 128))

    full = lambda shape: pl.BlockSpec(shape, lambda n: tuple(0 for _ in shape))
    return pl.pallas_call(
        body, name="swa_bwd", grid=(nb,),
        in_specs=_swa_specs() + [pl.BlockSpec((WINDOW, 512), lambda n: (n, 0)),
                                 pl.BlockSpec((WINDOW, 512), lambda n: (n, 0)), full((WINDOW, 2 * WINDOW))],
        out_specs=[pl.BlockSpec((WINDOW, 512), lambda n: (n, 0)), full((S, 128)), full((S, 128)),
                   full((NUM_BUCKETS * 8, 128)), full((8, 128))],
        out_shape=[jax.ShapeDtypeStruct((S, 512), F32), jax.ShapeDtypeStruct((S, 128), F32),
                   jax.ShapeDtypeStruct((S, 128), F32), jax.ShapeDtypeStruct((NUM_BUCKETS * 8, 128), F32),
                   jax.ShapeDtypeStruct((8, 128), F32)],
        scratch_shapes=[pltpu.VMEM((SWA_HEADS, WINDOW, 2 * WINDOW), F32)],
        compiler_params=_params(("arbitrary",)),
    )(proj, proj, proj, proj, proj, bias, sinks, o, do, bucket)


def _rope_tables(S):
    inv = ROPE_THETA ** (-jnp.arange(0, MLA_ROPE, 2, dtype=F32) / MLA_ROPE)
    ang = jnp.arange(S, dtype=F32)[:, None] * inv[None, :]
    cos, sin = jnp.cos(ang), jnp.sin(ang)
    return jnp.tile(jnp.concatenate([cos, cos], axis=1), (1, 4)), jnp.tile(jnp.concatenate([-sin, sin], axis=1), (1, 4))


def _swap_halves(x):
    w = x.shape[-1]
    lane = lax.broadcasted_iota(jnp.int32, x.shape, x.ndim - 1)
    return jnp.where((lane % 64) < 32, pltpu.roll(x, w - 32, x.ndim - 1), pltpu.roll(x, 32, x.ndim - 1))


def _mla_pre_fwd(proj, qn_w, kvn_w, wuqT, wukv, cos, sin, *, tm=256):
    S = proj.shape[0]

    def body(ql_ref, kl_ref, kr_ref, qw_ref, kw_ref, wuq_ref, wukv_ref, cos_ref, sin_ref,
             qc_ref, kc_ref, vv_ref):
        ql = ql_ref[...]
        qn = (ql * _rstd(ql) * qw_ref[...]).astype(BF16)
        q = _dot_nt(qn, wuq_ref[...])
        cs, sn = cos_ref[...], sin_ref[...]
        qr = q[:, 512:768]
        qr = qr * cs + _swap_halves(qr) * sn
        half = lax.broadcasted_iota(jnp.int32, (tm, 128), 1) // 64
        kl = kl_ref[...]
        kvn = (kl * _rstd(kl) * kw_ref[...]).astype(BF16)
        kr = kr_ref[...]
        kr = kr * cs[:, :128] + _swap_halves(kr) * sn[:, :128]
        kr2 = (kr + pltpu.roll(kr, 64, 1)).astype(BF16)
        for h in range(MLA_HEADS):
            qc_ref[h, :, 0:128] = q[:, 128 * h:128 * h + 128].astype(BF16)
            chunk = qr[:, 128 * (h // 2):128 * (h // 2) + 128]
            qc_ref[h, :, 128:256] = jnp.where(half == (h % 2), chunk, 0.0).astype(BF16)
            kc_ref[h, :, 0:128] = _dot(kvn, wukv_ref[2 * h]).astype(BF16)
            kc_ref[h, :, 128:256] = kr2
            vv_ref[h] = _dot(kvn, wukv_ref[2 * h + 1]).astype(BF16)

    const = lambda shape: pl.BlockSpec(shape, lambda i: tuple(0 for _ in shape))
    return pl.pallas_call(
        body, name="mla_pre_fwd", grid=(S // tm,),
        in_specs=[pl.BlockSpec((tm, 256), lambda i: (i, 3)), pl.BlockSpec((tm, 128), lambda i: (i, 8)),
                  pl.BlockSpec((tm, 128), lambda i: (i, 9)), const((1, 256)), const((1, 128)),
                  const((768, 256)), const((8, 128, 128)),
                  pl.BlockSpec((tm, 256), lambda i: (i, 0)), pl.BlockSpec((tm, 256), lambda i: (i, 0))],
        out_specs=[pl.BlockSpec((MLA_HEADS, tm, 256), lambda i: (0, i, 0)),
                   pl.BlockSpec((MLA_HEADS, tm, 256), lambda i: (0, i, 0)),
                   pl.BlockSpec((MLA_HEADS, tm, 128), lambda i: (0, i, 0))],
        out_shape=[jax.ShapeDtypeStruct((MLA_HEADS, S, 256), BF16), jax.ShapeDtypeStruct((MLA_HEADS, S, 256), BF16),
                   jax.ShapeDtypeStruct((MLA_HEADS, S, 128), BF16)],
        compiler_params=_params(("parallel",)),
    )(proj, proj, proj, qn_w, kvn_w, wuqT, wukv, cos, sin)


def _causal(i, j, t):
    row = i * t + lax.broadcasted_iota(jnp.int32, (t, t), 0)
    col = j * t + lax.broadcasted_iota(jnp.int32, (t, t), 1)
    return col <= row


def _mla_attn_fwd(qc, kc, vv, *, t=256):
    S = qc.shape[1]
    t = min(t, S)

    def body(q_ref, k_ref, v_ref, o_ref, l_ref):
        i = pl.program_id(0)
        diag = _causal(0, 0, t)

        def step(j, carry, masked):
            rows = pl.ds(pl.multiple_of(j * t, t), t)
            out = []
            for h in range(MLA_HEADS):
                m, l, acc = carry[h]
                s = _dot_nt(q_ref[h], k_ref[h, rows, :]) * MLA_SCALE
                if masked:
                    s = jnp.where(diag, s, -jnp.inf)
                m_new = jnp.maximum(m, jnp.max(s, axis=-1, keepdims=True))
                alpha = jnp.exp(m - m_new)
                p = jnp.exp(s - m_new)
                l = alpha * l + jnp.sum(p, axis=-1, keepdims=True)
                acc = alpha * acc + _dot(p.astype(BF16), v_ref[h, rows, :])
                out.append((m_new, l, acc))
            return tuple(out)

        init = tuple((jnp.full((t, 1), -jnp.inf, F32), jnp.zeros((t, 1), F32), jnp.zeros((t, MLA_V), F32))
                     for _ in range(MLA_HEADS))
        carry = lax.fori_loop(0, i, lambda j, c: step(j, c, False), init)
        carry = step(i, carry, True)
        for h in range(MLA_HEADS):
            m, l, acc = carry[h]
            o_ref[:, 128 * h:128 * h + 128] = acc / l
            l_ref[h] = jnp.broadcast_to(m + jnp.log(l), (t, 128))

    return pl.pallas_call(
        body, name="mla_attn_fwd", grid=(S // t,),
        in_specs=[pl.BlockSpec((MLA_HEADS, t, 256), lambda i: (0, i, 0)),
                  pl.BlockSpec((MLA_HEADS, S, 256), lambda i: (0, 0, 0)),
                  pl.BlockSpec((MLA_HEADS, S, 128), lambda i: (0, 0, 0))],
        out_specs=[pl.BlockSpec((t, 512), lambda i: (i, 0)),
                   pl.BlockSpec((MLA_HEADS, t, 128), lambda i: (0, i, 0))],
        out_shape=[jax.ShapeDtypeStruct((S, 512), F32), jax.ShapeDtypeStruct((MLA_HEADS, S, 128), F32)],
        compiler_params=_params(("parallel",)),
    )(qc, kc, vv)


def _mla_attn_bwd(qc, kc, vv, o, lse, do, *, t=256, tq=512):
    S = qc.shape[1]
    t = min(t, S)
    tq = min(tq, S)
    nblk = S // t
    hp = MLA_HEADS
    once = pl.Buffered(1)

    def body(q_ref, k_ref, v_ref, o_ref, l_ref, do_ref, dq_ref, dk_ref, dv_ref):
        j = pl.program_id(1)

        @pl.when(j == 0)
        def _():
            dq_ref[...] = jnp.zeros_like(dq_ref)

        first = (j * t) // tq

        def step(i, carry, masked):
            rows = pl.ds(pl.multiple_of(i * tq, tq), tq)
            if masked:
                row = i * tq + lax.broadcasted_iota(jnp.int32, (tq, t), 0)
                col = j * t + lax.broadcasted_iota(jnp.int32, (tq, t), 1)
                visible = col <= row
            out = []
            for h in range(hp):
                dk, dv = carry[h]
                k = k_ref[h]
                q = q_ref[h, rows, :]
                dov = do_ref[rows, 128 * h:128 * h + 128]
                lrow = l_ref[h, rows, :][:, 0:1]
                p = jnp.exp(_dot_nt(q, k) * MLA_SCALE - lrow)
                if masked:
                    p = jnp.where(visible, p, 0.0)
                dob = dov.astype(BF16)
                dv = dv + _dot_tn(p.astype(BF16), dob)
                dp = _dot_nt(dob, v_ref[h])
                delta = jnp.sum(dov * o_ref[rows, 128 * h:128 * h + 128], axis=-1, keepdims=True)
                ds = (p * (dp - delta) * MLA_SCALE).astype(BF16)
                dk = dk + _dot_tn(ds, q)
                dq_ref[h, rows, :] += _dot(ds, k)
                out.append((dk, dv))
            return tuple(out)

        init = tuple((jnp.zeros((t, 256), F32), jnp.zeros((t, MLA_V), F32)) for _ in range(hp))
        carry = step(first, init, True)
        carry = lax.fori_loop(first + 1, S // tq, lambda i, c: step(i, c, False), carry)
        for h in range(hp):
            dk_ref[h] = carry[h][0]
            dv_ref[h] = carry[h][1]

    return pl.pallas_call(
        body, name="mla_attn_bwd", grid=(MLA_HEADS // hp, nblk),
        in_specs=[pl.BlockSpec((hp, S, 256), lambda g, j: (g, 0, 0), pipeline_mode=once),
                  pl.BlockSpec((hp, t, 256), lambda g, j: (g, j, 0)),
                  pl.BlockSpec((hp, t, 128), lambda g, j: (g, j, 0)),
                  pl.BlockSpec((S, 128 * hp), lambda g, j: (0, g), pipeline_mode=once),
                  pl.BlockSpec((hp, S, 128), lambda g, j: (g, 0, 0), pipeline_mode=once),
                  pl.BlockSpec((S, 128 * hp), lambda g, j: (0, g), pipeline_mode=once)],
        out_specs=[pl.BlockSpec((hp, S, 256), lambda g, j: (g, 0, 0)),
                   pl.BlockSpec((hp, t, 256), lambda g, j: (g, j, 0)),
                   pl.BlockSpec((hp, t, 128), lambda g, j: (g, j, 0))],
        out_shape=[jax.ShapeDtypeStruct((MLA_HEADS, S, 256), F32), jax.ShapeDtypeStruct((MLA_HEADS, S, 256), F32),
                   jax.ShapeDtypeStruct((MLA_HEADS, S, 128), F32)],
        compiler_params=_params(("parallel", "arbitrary")),
    )(qc, kc, vv, o, lse, do)


def _mla_pre_bwd(proj, qn_w, kvn_w, wuqT, wukv, cos, sin, dqc, dkc, dvv, *, tm=256):
    S = proj.shape[0]

    def body(ql_ref, kl_ref, qw_ref, kw_ref, wuq_ref, wukv_ref, cos_ref, sin_ref, dqc_ref, dkc_ref, dvv_ref,
             dql_ref, dkl_ref, dkr_ref, gq_ref, gkv_ref, part_ref):
        @pl.when(pl.program_id(0) == 0)
        def _():
            gq_ref[...] = jnp.zeros_like(gq_ref)
            gkv_ref[...] = jnp.zeros_like(gkv_ref)
            part_ref[...] = jnp.zeros_like(part_ref)

        cs, sn = cos_ref[...], sin_ref[...]
        half = lax.broadcasted_iota(jnp.int32, (tm, 128), 1) // 64
        ql = ql_ref[...]
        rq = _rstd(ql)
        qhat = ql * rq
        qw = qw_ref[...]
        qn = (qhat * qw).astype(BF16)
        chunks = []
        for pair in range(2):
            chunks.append(jnp.where(half == 0, dqc_ref[2 * pair, :, 128:256], dqc_ref[2 * pair + 1, :, 128:256]))
        dqr = jnp.concatenate(chunks, axis=1)
        dqr = dqr * cs + _swap_halves(dqr * sn)
        dq = jnp.concatenate([dqc_ref[h, :, 0:128] for h in range(MLA_HEADS)] + [dqr], axis=1).astype(BF16)
        gq_ref[...] += _dot_tn(dq, qn)
        dqn = _dot(dq, wuq_ref[...])
        part_ref[0:1, :] += jnp.sum(dqn * qhat, axis=0, keepdims=True)
        dql_ref[...] = _rms_bwd(dqn * qw, qhat, rq)
        kl = kl_ref[...]
        rk = _rstd(kl)
        khat = kl * rk
        kw = kw_ref[...]
        kvn = (khat * kw).astype(BF16)
        dkvn = jnp.zeros((tm, MLA_KVR), F32)
        dkr2 = jnp.zeros((tm, 128), F32)
        for h in range(MLA_HEADS):
            dkn = dkc_ref[h, :, 0:128].astype(BF16)
            dvh = dvv_ref[h].astype(BF16)
            gkv_ref[2 * h] += _dot_tn(kvn, dkn)
            gkv_ref[2 * h + 1] += _dot_tn(kvn, dvh)
            dkvn += _dot_nt(dkn, wukv_ref[2 * h]) + _dot_nt(dvh, wukv_ref[2 * h + 1])
            dkr2 += dkc_ref[h, :, 128:256]
        part_ref[1:2, 0:128] += jnp.sum(dkvn * khat, axis=0, keepdims=True)
        dkl_ref[...] = _rms_bwd(dkvn * kw, khat, rk)
        dkr = jnp.where(half == 0, dkr2 + pltpu.roll(dkr2, 64, 1), 0.0)
        dkr_ref[...] = dkr * cs[:, :128] + _swap_halves(dkr * sn[:, :128])

    const = lambda shape: pl.BlockSpec(shape, lambda i: tuple(0 for _ in shape))
    heads = lambda w: pl.BlockSpec((MLA_HEADS, tm, w), lambda i: (0, i, 0))
    return pl.pallas_call(
        body, name="mla_pre_bwd", grid=(S // tm,),
        in_specs=[pl.BlockSpec((tm, 256), lambda i: (i, 3)), pl.BlockSpec((tm, 128), lambda i: (i, 8)),
                  const((1, 256)), const((1, 128)), const((768, 256)), const((8, 128, 128)),
                  pl.BlockSpec((tm, 256), lambda i: (i, 0)), pl.BlockSpec((tm, 256), lambda i: (i, 0)),
                  heads(256), heads(256), heads(128)],
        out_specs=[pl.BlockSpec((tm, 256), lambda i: (i, 0)), pl.BlockSpec((tm, 128), lambda i: (i, 0)),
                   pl.BlockSpec((tm, 128), lambda i: (i, 0)), const((768, 256)), const((8, 128, 128)), const((8, 256))],
        out_shape=[jax.ShapeDtypeStruct((S, 256), F32), jax.ShapeDtypeStruct((S, 128), F32),
                   jax.ShapeDtypeStruct((S, 128), F32), jax.ShapeDtypeStruct((768, 256), F32),
                   jax.ShapeDtypeStruct((8, 128, 128), F32), jax.ShapeDtypeStruct((8, 256), F32)],
        compiler_params=_params(("arbitrary",)),
    )(proj, proj, qn_w, kvn_w, wuqT, wukv, cos, sin, dqc, dkc, dvv)


def _mix_out_fwd(x, oa, ob, w_o, vecs, *, tm=256):
    S = x.shape[0]

    def body(x_ref, oa_ref, ob_ref, w_ref, vec_ref, xo_ref, mo_ref):
        mo = _dot(oa_ref[...].astype(BF16), w_ref[0:512, :]) + _dot(ob_ref[...].astype(BF16), w_ref[512:1024, :])
        mo_ref[...] = mo
        xo_ref[...] = x_ref[...] + vec_ref[3:4, :] * mo

    row = pl.BlockSpec((tm, D), lambda i: (i, 0))
    half = pl.BlockSpec((tm, 512), lambda i: (i, 0))
    return pl.pallas_call(
        body, name="mix_out_fwd", grid=(S // tm,),
        in_specs=[row, half, half, pl.BlockSpec((D, D), lambda i: (0, 0)), pl.BlockSpec((8, D), lambda i: (0, 0))],
        out_specs=[row, row],
        out_shape=[jax.ShapeDtypeStruct((S, D), F32), jax.ShapeDtypeStruct((S, D), F32)],
        compiler_params=_params(("parallel",)),
    )(x, oa, ob, w_o, vecs)


def _mix_out_bwd(dxo, mo, oa, ob, w_o, vecs, *, tm=256):
    S = dxo.shape[0]

    def body(dx_ref, mo_ref, oa_ref, ob_ref, w_ref, vec_ref, doa_ref, dob_ref, gw_ref, part_ref):
        @pl.when(pl.program_id(0) == 0)
        def _():
            gw_ref[...] = jnp.zeros_like(gw_ref)
            part_ref[...] = jnp.zeros_like(part_ref)

        dx = dx_ref[...]
        part_ref[0:1, :] += jnp.sum(dx * mo_ref[...], axis=0, keepdims=True)
        dmo = (vec_ref[3:4, :] * dx).astype(BF16)
        doa_ref[...] = _dot_nt(dmo, w_ref[0:512, :])
        dob_ref[...] = _dot_nt(dmo, w_ref[512:1024, :])
        gw_ref[0:512, :] += _dot_tn(oa_ref[...].astype(BF16), dmo)
        gw_ref[512:1024, :] += _dot_tn(ob_ref[...].astype(BF16), dmo)

    row = pl.BlockSpec((tm, D), lambda i: (i, 0))
    half = pl.BlockSpec((tm, 512), lambda i: (i, 0))
    return pl.pallas_call(
        body, name="mix_out_bwd", grid=(S // tm,),
        in_specs=[row, row, half, half, pl.BlockSpec((D, D), lambda i: (0, 0)), pl.BlockSpec((8, D), lambda i: (0, 0))],
        out_specs=[half, half, pl.BlockSpec((D, D), lambda i: (0, 0)), pl.BlockSpec((8, D), lambda i: (0, 0))],
        out_shape=[jax.ShapeDtypeStruct((S, 512), F32), jax.ShapeDtypeStruct((S, 512), F32),
                   jax.ShapeDtypeStruct((D, D), F32), jax.ShapeDtypeStruct((8, D), F32)],
        compiler_params=_params(("arbitrary",)),
    )(dxo, mo, oa, ob, w_o, vecs)


def _mix_in_bwd(h, w_inT, dq, dk, dv, dql, dkl, dkr, *, tm=256):
    S = h.shape[0]
    offs = (0, 512, 640, 768, 1024, 1152)
    wid = (512, 128, 128, 256, 128, 128)

    def body(h_ref, w_ref, dq_ref, dk_ref, dv_ref, dql_ref, dkl_ref, dkr_ref, dh_ref, gw_ref):
        @pl.when(pl.program_id(0) == 0)
        def _():
            gw_ref[...] = jnp.zeros_like(gw_ref)

        hv = h_ref[...]
        dh = jnp.zeros((tm, D), F32)
        for ref, o, w in zip((dq_ref, dk_ref, dv_ref, dql_ref, dkl_ref, dkr_ref), offs, wid):
            w = min(w, D_IN - o)
            dpart = ref[...][:, :w].astype(BF16)
            dh += _dot(dpart, w_ref[o:o + w, :])
            gw_ref[o:o + w, :] += _dot_tn(dpart, hv)
        dh_ref[...] = dh

    row = pl.BlockSpec((tm, D), lambda i: (i, 0))
    part = lambda w: pl.BlockSpec((tm, w), lambda i: (i, 0))
    return pl.pallas_call(
        body, name="mix_in_bwd", grid=(S // tm,),
        in_specs=[row, pl.BlockSpec((D_IN_PAD, D), lambda i: (0, 0))] + [part(w) for w in wid],
        out_specs=[row, pl.BlockSpec((D_IN, D), lambda i: (0, 0))],
        out_shape=[jax.ShapeDtypeStruct((S, D), F32), jax.ShapeDtypeStruct((D_IN, D), F32)],
        compiler_params=_params(("arbitrary",)),
    )(h, w_inT, dq, dk, dv, dql, dkl, dkr)


def _vecs(norm_w, mod9, k):
    return jnp.concatenate([norm_w.reshape(1, D), mod9[3 * k:3 * k + 3], jnp.zeros((4, D), F32)], axis=0)


def _uq_group_rows(wuqT):
    per = MLA_NOPE + MLA_ROPE
    nope = [wuqT[per * h:per * h + MLA_NOPE] for h in range(MLA_HEADS)]
    rope = [wuqT[per * h + MLA_NOPE:per * (h + 1)] for h in range(MLA_HEADS)]
    return jnp.concatenate(nope + rope, axis=0)


def _uq_ungroup_rows(g):
    parts = []
    for h in range(MLA_HEADS):
        parts += [g[MLA_NOPE * h:MLA_NOPE * (h + 1)], g[512 + MLA_ROPE * h:512 + MLA_ROPE * (h + 1)]]
    return jnp.concatenate(parts, axis=0)


def _local_step(x, tgt, mod9, norms, sinks, rel_bias, q_norm, kv_norm, W, on_grads=None):
    if on_grads is None:
        on_grads = lambda group, grads, after, vecs: vecs
    S = x.shape[0]
    v1 = _vecs(norms["ffn1"], mod9, 0)
    v2 = _vecs(norms["mix"], mod9, 1)
    v3 = _vecs(norms["ffn2"], mod9, 2)
    bucket = jnp.asarray(_bucket_table())
    cos, sin = _rope_tables(S)
    if isinstance(W, dict):
        full, W = W, (lambda group, after, vecs: (full, vecs))

    W1, v1 = W("ffn1", [], v1)
    x1, h1, a1, b1, f1 = _ffn_fwd(x, v1, W1["g1T"], W1["u1T"], W1["d1"], name="ffn1_fwd", tm=256, tf=D_FF)
    W2, v2 = W("mixer", [x1], v2)
    w_inT = jnp.pad(W2["w_inT"], ((0, D_IN_PAD - D_IN), (0, 0))).astype(BF16)
    wuqT = _uq_group_rows(W2["w_uqT"])
    h2, proj = _mix_in_fwd(x1, v2, w_inT)
    bias = _bias_build(rel_bias, bucket)
    oa = _swa_fwd(proj, bias, sinks)
    qc, kc, vv = _mla_pre_fwd(proj, q_norm, kv_norm, wuqT, W2["w_ukv"], cos, sin)
    ob, lse = _mla_attn_fwd(qc, kc, vv)
    _, v2o = W("ffn2_on_its_way", [ob], v2)
    x2, mo = _mix_out_fwd(x1, oa, ob, W2["w_o"], v2o)
    W3, v3 = W("ffn2", [x2], v3)
    x3, h3, a3, b3, f3 = _ffn_fwd(x2, v3, W3["g3T"], W3["u3T"], W3["d3"], name="ffn2_fwd", tm=256, tf=D_FF)
    dx3, head_part, df3 = _head(x3, tgt, norms["final"], f3, v3)

    gg3, gu3, gd3, dh3 = _ffn_bwd_main(h3, df3, a3, b3, W3["g3T"], W3["u3T"], W3["d3"], name="ffn2_bwd", tm=2048, tf=256)
    ffn2 = {"g3T": gg3, "u3T": gu3, "d3": gd3}
    v3 = on_grads("ffn2", ffn2, [], v3)
    dx2, n3_part = _norm_bwd(dh3, x2, dx3, v3, name="ffn2_norm_bwd")
    v2 = on_grads("ffn2", None, [dx2], v2)
    doa, dob, g_wo, g2_part = _mix_out_bwd(dx2, mo, oa, ob, W2["w_o"], v2)
    dq, dk, dv, drb, dsk = _swa_bwd(proj, bias, sinks, oa, doa, bucket)
    dqc, dkc, dvv = _mla_attn_bwd(qc, kc, vv, ob, lse, dob)
    dql, dkl, dkr, g_uq, g_ukv, mla_part = _mla_pre_bwd(proj, q_norm, kv_norm, wuqT, W2["w_ukv"], cos, sin, dqc, dkc, dvv)
    dh2, g_win = _mix_in_bwd(h2, w_inT, dq, dk, dv, dql, dkl, dkr)
    mixer = {"w_inT": g_win, "w_uqT": _uq_ungroup_rows(g_uq).astype(BF16),
             "w_ukv": g_ukv.astype(BF16), "w_o": g_wo.astype(BF16)}
    v2 = on_grads("mixer", mixer, [], v2)
    dx1, n2_part, df1 = _norm_bwd(dh2, x1, dx2, v2, name="mix_norm_bwd", below=(f1, v1))
    started = on_grads("mixer", None, [dx1], jnp.zeros((1, 1), F32))
    gg1, gu1, gd1, dh1 = _ffn_bwd_main(h1, df1, a1, b1, W1["g1T"], W1["u1T"], W1["d1"], name="ffn1_bwd",
                                       after=[started], tm=2048, tf=256)
    ffn1 = {"g1T": gg1, "u1T": gu1, "d1": gd1}
    v1 = on_grads("ffn1", ffn1, [], v1)
    dx0, n1_part = _norm_bwd(dh1, x, dx1, v1, name="ffn1_norm_bwd")

    grads = {**ffn1, **ffn2, **mixer}
    return head_part[1, 0], dx0, grads, _pack_vec(n1_part, n2_part, n3_part, head_part, g2_part, mla_part, dsk, drb)


SMALL_LAYOUT = (("norm_ffn1", 1024), ("norm_mix", 1024), ("norm_ffn2", 1024), ("norm_final", 1024),
                ("q_norm", 256), ("kv_norm", 128), ("sinks", 128), ("rel_bias", 256))
N_SMALL = sum(n for _, n in SMALL_LAYOUT)
LOSS_SLOT = 4 * 1024 + 256 + 128 + SWA_HEADS
N_MODVEC = N_MOD * D
N_VEC = N_MODVEC + N_SMALL


def _pack_vec(n1, n2, n3, head, g2, mla, dsk, drb):
    def body(n1_ref, n2_ref, n3_ref, head_ref, g2_ref, mla_ref, dsk_ref, drb_ref, out_ref):
        rows = [n1_ref[1:2, :], n1_ref[2:3, :], n2_ref[3:4, :], n2_ref[1:2, :], n2_ref[2:3, :], g2_ref[0:1, :],
                n3_ref[1:2, :], n3_ref[2:3, :], head_ref[3:4, :],
                n1_ref[0:1, :], n2_ref[0:1, :], n3_ref[0:1, :], head_ref[0:1, :]]
        for i, row in enumerate(rows):
            out_ref[:, D * i:D * (i + 1)] = row
        off = D * len(rows)
        out_ref[:, off:off + 256] = mla_ref[0:1, :]
        out_ref[:, off + 256:off + 384] = mla_ref[1:2, 0:128]

        def diagonal(block):
            r = lax.broadcasted_iota(jnp.int32, block.shape, 0)
            lane = lax.broadcasted_iota(jnp.int32, block.shape, 1)
            return jnp.sum(jnp.where(r == lane, block, 0.0), axis=0, keepdims=True)

        lane = lax.broadcasted_iota(jnp.int32, (1, 128), 1)
        out_ref[:, off + 384:off + 512] = jnp.where(lane == SWA_HEADS, head_ref[1:2, 0:128], diagonal(dsk_ref[...]))
        out_ref[:, off + 512:off + 640] = diagonal(drb_ref[0:128, :])
        out_ref[:, off + 640:off + 768] = diagonal(drb_ref[128:256, :])

    vm = pl.BlockSpec(memory_space=pltpu.VMEM)
    return pl.pallas_call(body, name="pack_vec", in_specs=[vm] * 8, out_specs=vm,
                          out_shape=jax.ShapeDtypeStruct((1, N_VEC), F32))(n1, n2, n3, head, g2, mla, dsk, drb)


def _coords():
    return lax.axis_index("x"), lax.axis_index("y"), lax.axis_index("c")


def _flip(v, bit):
    return 1 - v if bit else v


def _peer(r):
    x, y, c = _coords()
    return (_flip(x, r & 4), _flip(y, r & 2), _flip(c, r & 1))


def _mod_fwd(c_tile, w_mod, b_mod3):
    W = w_mod.shape[1]

    def body(c_ref, w_ref, b_ref, mod_ref, ca_ref, call_ref, part_ref, send_sems, recv_sems):
        x, y, c = _coords()
        me = 4 * x + 2 * y + c
        call_ref[me] = c_ref[...]
        sends = []
        for r in range(1, N_DEV):
            cp = pltpu.make_async_remote_copy(c_ref, call_ref.at[me], send_sems.at[0, r], recv_sems.at[0, r],
                                              device_id=_peer(r), device_id_type=MESH)
            cp.start()
            sends.append(cp)
        for r in range(1, N_DEV):
            pltpu.make_async_remote_copy(c_ref, call_ref.at[me], send_sems.at[0, r], recv_sems.at[0, r],
                                         device_id=_peer(r), device_id_type=MESH).wait_recv()
        cv = call_ref[...].reshape(8 * N_DEV, D)
        ca = (cv * _sigmoid(cv)).astype(BF16)
        ca_ref[...] = ca
        part_ref[...] = _dot(ca, w_ref[...].astype(BF16)).reshape(N_DEV, 8, W)
        mod_ref[me] = part_ref[me] + b_ref[me]
        for r in range(1, N_DEV):
            cp = pltpu.make_async_remote_copy(part_ref.at[me ^ r], mod_ref.at[me], send_sems.at[1, r],
                                              recv_sems.at[1, r], device_id=_peer(r), device_id_type=MESH)
            cp.start()
            sends.append(cp)
        for r in range(1, N_DEV):
            pltpu.make_async_remote_copy(part_ref.at[me ^ r], mod_ref.at[me], send_sems.at[1, r],
                                         recv_sems.at[1, r], device_id=_peer(r), device_id_type=MESH).wait_recv()
            mod_ref[me ^ r] = mod_ref[me ^ r] + b_ref[me ^ r]
        for cp in sends:
            cp.wait_send()

    vm = pl.BlockSpec(memory_space=pltpu.VMEM)
    return pl.pallas_call(
        body, name="mod_fwd", in_specs=[vm, vm, vm], out_specs=[vm, vm],
        out_shape=[jax.ShapeDtypeStruct((N_DEV, 8, W), F32), jax.ShapeDtypeStruct((8 * N_DEV, D), BF16)],
        scratch_shapes=[pltpu.VMEM((N_DEV, 8, D), F32), pltpu.VMEM((N_DEV, 8, W), F32),
                        pltpu.SemaphoreType.DMA((2, N_DEV)), pltpu.SemaphoreType.DMA((2, N_DEV))],
        compiler_params=_params(),
    )(c_tile, w_mod, b_mod3)


def _mod_bwd(allvec, ca, me_idx):
    W = N_MODVEC // N_DEV

    def body(me_ref, all_ref, cols_ref, ca_ref, gw_ref, sum_ref):
        in_first_row = lax.broadcasted_iota(jnp.int32, (N_DEV, 8, W), 1) == 0
        dm = jnp.where(in_first_row, cols_ref[...], 0.0).reshape(8 * N_DEV, W)
        gw_ref[...] = _dot_tn(ca_ref[...], dm.astype(BF16))
        total = all_ref[0]
        for k in range(1, N_DEV):
            total = total + all_ref[k]
        sum_ref[...] = total

    return pl.pallas_call(
        body, name="mod_bwd",
        grid_spec=pltpu.PrefetchScalarGridSpec(
            num_scalar_prefetch=1, grid=(1,),
            in_specs=[pl.BlockSpec((N_DEV, 1, N_VEC), lambda i, me: (0, 0, 0)),
                      pl.BlockSpec((N_DEV, 1, W), lambda i, me: (0, 0, me[0])),
                      pl.BlockSpec((8 * N_DEV, D), lambda i, me: (0, 0))],
            out_specs=[pl.BlockSpec((D, W), lambda i, me: (0, 0)), pl.BlockSpec((1, N_VEC), lambda i, me: (0, 0))]),
        out_shape=[jax.ShapeDtypeStruct((D, W), F32), jax.ShapeDtypeStruct((1, N_VEC), F32)],
        compiler_params=_params(("arbitrary",)),
    )(me_idx, allvec, allvec, ca)


def _wgather(shards):
    n = len(shards)

    def body(*refs):
        ins, outs, token = refs[:n], refs[n:2 * n], refs[2 * n]
        send_sems, recv_sems, local_sems = refs[2 * n + 1:]
        token[...] = jnp.zeros_like(token)
        x, y, c = _coords()
        me = 4 * x + 2 * y + c
        sib = (x, y, 1 - c)
        chips = [(1 - x, y), (x, 1 - y), (1 - x, 1 - y)]

        def copy(k, slot, block, to, src=None):
            return pltpu.make_async_remote_copy(
                src_ref=outs[k].at[block] if src is None else src, dst_ref=outs[k].at[block],
                send_sem=send_sems.at[k, slot], recv_sem=recv_sems.at[k, slot], device_id=to, device_id_type=MESH)

        local = [pltpu.make_async_copy(ins[k], outs[k].at[me], local_sems.at[k]) for k in range(n)]
        for cp in local:
            cp.start()
        first = []
        for k in range(n):
            first.append(copy(k, 0, me, sib, src=ins[k]))
            for j, chip in enumerate(chips):
                first.append(copy(k, 1 + j, me, (*chip, c), src=ins[k]))
        for cp in first:
            cp.start()
        passed = []
        for j, (cx, cy) in enumerate(chips):
            for k in range(n):
                blk = 4 * cx + 2 * cy + c
                copy(k, 1 + j, blk, sib).wait_recv()
                cp = copy(k, 4 + j, blk, sib)
                cp.start()
                passed.append(cp)
        for k in range(n):
            copy(k, 0, 4 * x + 2 * y + (1 - c), sib).wait_recv()
            for j, (cx, cy) in enumerate(chips):
                copy(k, 4 + j, 4 * cx + 2 * cy + (1 - c), sib).wait_recv()
        for cp in first + passed:
            cp.wait_send()
        for cp in local:
            cp.wait()

    anyspec = pl.BlockSpec(memory_space=pl.ANY)
    return pl.pallas_call(
        body, name="wgather", in_specs=[anyspec] * n,
        out_specs=[anyspec] * n + [pl.BlockSpec(memory_space=pltpu.VMEM)],
        out_shape=[jax.ShapeDtypeStruct((N_DEV,) + s.shape, s.dtype) for s in shards]
        + [jax.ShapeDtypeStruct((8, 128), F32)],
        scratch_shapes=[pltpu.SemaphoreType.DMA((n, 7)), pltpu.SemaphoreType.DMA((n, 7)),
                        pltpu.SemaphoreType.DMA((n,))],
    )(*shards)


class _GatherCopies:
    def __init__(self, lands, send_sems, recv_sems, k0=0, batches=None):
        x, y, c = _coords()
        me = 4 * x + 2 * y + c
        sib = (x, y, 1 - c)
        chips = [(1 - x, y), (x, 1 - y), (1 - x, 1 - y)]

        def copy(k, slot, block, to):
            return pltpu.make_async_remote_copy(
                src_ref=lands[k].at[block], dst_ref=lands[k].at[block],
                send_sem=send_sems.at[7 * (k0 + k) + slot], recv_sem=recv_sems.at[7 * (k0 + k) + slot],
                device_id=to, device_id_type=MESH)

        n = len(lands)
        self.first = [copy(k, 0, me, sib) for k in range(n)]
        for batch in batches or [range(n)]:
            self.first += [copy(k, 1 + j, me, (cx, cy, c)) for j, (cx, cy) in enumerate(chips) for k in batch]
        self.landed = [copy(k, 1 + j, 4 * cx + 2 * cy + c, sib) for j, (cx, cy) in enumerate(chips) for k in range(n)]
        self.passed = [copy(k, 4 + j, 4 * cx + 2 * cy + c, sib) for j, (cx, cy) in enumerate(chips) for k in range(n)]
        self.from_sib = [copy(k, 0, 4 * x + 2 * y + (1 - c), sib) for k in range(n)]
        self.from_sib += [copy(k, 4 + j, 4 * cx + 2 * cy + (1 - c), sib) for j, (cx, cy) in enumerate(chips)
                          for k in range(n)]


def _gather_start(lands, *, name, batches=None):
    n = len(lands)

    def body(*refs):
        for cp in _GatherCopies(refs[:n], refs[n], refs[n + 1], batches=batches).first:
            cp.start()
        refs[-1][...] = jnp.zeros_like(refs[-1])

    out = pl.pallas_call(
        body, name=name,
        out_shape=(pltpu.SemaphoreType.DMA((7 * n,)), pltpu.SemaphoreType.DMA((7 * n,)),
                   *[pltpu.HBM(l.shape, l.dtype) for l in lands], jax.ShapeDtypeStruct((8, 128), F32)),
        in_specs=[HBM_SPEC] * n,
        out_specs=(SEM_SPEC, SEM_SPEC, *[HBM_SPEC] * n, pl.BlockSpec(memory_space=pltpu.VMEM)),
        input_output_aliases={i: 2 + i for i in range(n)},
        compiler_params=pltpu.CompilerParams(has_side_effects=DATAFLOW),
    )(*[_in_hbm(l) for l in lands])
    return out[0], out[1], list(out[2:2 + n]), out[-1]


def _gather_pass(send_sems, recv_sems, lands, after, *, name, stage, k0=0):
    n = len(lands)

    def body(*refs):
        cps = _GatherCopies(refs[:n], refs[n], refs[n + 1], k0)
        if stage == "landed":
            for cp in cps.landed:
                cp.wait_recv()
        else:
            for cp in cps.passed:
                cp.start()
        refs[-1][...] = jnp.zeros_like(refs[-1])

    out = pl.pallas_call(
        body, name=name,
        out_shape=(*[pltpu.HBM(l.shape, l.dtype) for l in lands], jax.ShapeDtypeStruct((8, 128), F32)),
        in_specs=[HBM_SPEC] * n + [SEM_SPEC, SEM_SPEC] + [pl.BlockSpec(memory_space=pl.ANY)] * len(after),
        out_specs=(*[HBM_SPEC] * n, pl.BlockSpec(memory_space=pltpu.VMEM)),
        input_output_aliases={i: i for i in range(n)},
        compiler_params=pltpu.CompilerParams(has_side_effects=DATAFLOW),
    )(*lands, send_sems, recv_sems, *after)
    return list(out[:n]), out[-1]


def _gather_end(send_sems, recv_sems, lands, after, *, name, k0=0):
    n = len(lands)

    def body(*refs):
        cps = _GatherCopies(refs[:n], refs[n], refs[n + 1], k0)
        for cp in cps.from_sib:
            cp.wait_recv()
        for cp in cps.first + cps.passed:
            cp.wait_send()

    out = pl.pallas_call(
        body, name=name,
        out_shape=[pltpu.HBM(l.shape, l.dtype) for l in lands],
        in_specs=[HBM_SPEC] * n + [SEM_SPEC, SEM_SPEC] + [pl.BlockSpec(memory_space=pl.ANY)] * len(after),
        out_specs=[HBM_SPEC] * n,
        input_output_aliases={i: i for i in range(n)},
        compiler_params=pltpu.CompilerParams(has_side_effects=DATAFLOW),
    )(*lands, send_sems, recv_sems, *after)
    return list(out)


def _d2d_copies(grads, lands, send_sems, recv_sems):
    x, y, c = _coords()
    return [pltpu.make_async_remote_copy(
        src_ref=grads[k].at[2 * q + (1 - c)], dst_ref=lands[k].at[q],
        send_sem=send_sems.at[4 * k + q], recv_sem=recv_sems.at[4 * k + q],
        device_id=(x, y, 1 - c), device_id_type=MESH) for k in range(len(grads)) for q in range(4)]


def _vec_copies(srcs, lands, send_sems, recv_sems):
    x, y, c = _coords()
    me = 4 * x + 2 * y + c
    return [pltpu.make_async_remote_copy(
        src_ref=lands[0].at[me], dst_ref=lands[0].at[me], send_sem=send_sems.at[r - 1], recv_sem=recv_sems.at[r - 1],
        device_id=_peer(r), device_id_type=MESH) for r in range(1, N_DEV)]


def _chipsum(gs, sibs, cidx, *, name):
    n = len(gs)

    def body(c_ref, *refs):
        for k in range(n):
            refs[2 * n + k][...] = (refs[k][...].astype(F32) + refs[n + k][...].astype(F32)).astype(refs[2 * n + k].dtype)

    mine = [pl.BlockSpec((1,) + g.shape[1:], lambda q, c_ref: (2 * q + c_ref[0], 0, 0)) for g in gs]
    other = [pl.BlockSpec((1,) + g.shape[1:], lambda q, c_ref: (q, 0, 0)) for g in gs]
    return pl.pallas_call(
        body, name=name,
        grid_spec=pltpu.PrefetchScalarGridSpec(num_scalar_prefetch=1, grid=(4,), in_specs=mine + other, out_specs=other),
        out_shape=[jax.ShapeDtypeStruct((4,) + g.shape[1:], g.dtype) for g in gs],
        compiler_params=_params(("arbitrary",)),
    )(cidx, *gs, *sibs)


HBM_SPEC = pl.BlockSpec(memory_space=pltpu.HBM)
SEM_SPEC = pl.BlockSpec(memory_space=pltpu.SEMAPHORE)
DATAFLOW = pltpu.SideEffectType.DATAFLOW_SIDE_EFFECTING


def _in_hbm(a):
    return pltpu.with_memory_space_constraint(a, pltpu.HBM)


def _ici_copies(sums, lands, send_sems, recv_sems, k0=0):
    x, y, c = _coords()
    chips = [(1 - x, y), (x, 1 - y), (1 - x, 1 - y)]
    cps = []
    for k in range(len(sums)):
        for j, (cx, cy) in enumerate(chips):
            cps.append(pltpu.make_async_remote_copy(
                src_ref=sums[k].at[2 * cx + cy], dst_ref=lands[k].at[j],
                send_sem=send_sems.at[3 * (k0 + k) + j], recv_sem=recv_sems.at[3 * (k0 + k) + j],
                device_id=(cx, cy, c), device_id_type=MESH))
    return cps


def _split_start(copies, srcs, lands, n_sems, after, *, name):
    ns, nl = len(srcs), len(lands)

    def body(*refs):
        for cp in copies(refs[:ns], refs[ns:ns + nl], refs[ns + nl + len(after)], refs[ns + nl + len(after) + 1]):
            cp.start()
        refs[-1][...] = jnp.zeros_like(refs[-1])

    bufs = [_in_hbm(a) for a in list(srcs) + list(lands)]
    out = pl.pallas_call(
        body, name=name,
        out_shape=(pltpu.SemaphoreType.DMA((n_sems,)), pltpu.SemaphoreType.DMA((n_sems,)),
                   *[pltpu.HBM(a.shape, a.dtype) for a in bufs], jax.ShapeDtypeStruct((8, 128), F32)),
        in_specs=[HBM_SPEC] * len(bufs) + [pl.BlockSpec(memory_space=pl.ANY)] * len(after),
        out_specs=(SEM_SPEC, SEM_SPEC, *[HBM_SPEC] * len(bufs), pl.BlockSpec(memory_space=pltpu.VMEM)),
        input_output_aliases={i: 2 + i for i in range(len(bufs))},
        compiler_params=pltpu.CompilerParams(has_side_effects=DATAFLOW),
    )(*bufs, *after)
    return out[0], out[1], list(out[2:2 + ns]), list(out[2 + ns:2 + ns + nl]), out[-1]


def _split_wait(copies, send_sems, recv_sems, srcs, lands, after, *, name):
    ns, nl = len(srcs), len(lands)

    def body(*refs):
        for cp in copies(refs[:ns], refs[ns:ns + nl], refs[ns + nl], refs[ns + nl + 1]):
            cp.wait_send()
            cp.wait_recv()

    out = pl.pallas_call(
        body, name=name,
        out_shape=[pltpu.HBM(a.shape, a.dtype) for a in list(srcs) + list(lands)],
        in_specs=[HBM_SPEC] * (ns + nl) + [SEM_SPEC, SEM_SPEC] + [pl.BlockSpec(memory_space=pl.ANY)] * len(after),
        out_specs=[HBM_SPEC] * (ns + nl),
        input_output_aliases={i: i for i in range(ns + nl)},
        compiler_params=pltpu.CompilerParams(has_side_effects=DATAFLOW),
    )(*srcs, *lands, send_sems, recv_sems, *after)
    return list(out[:ns]), list(out[ns:])


ADAM_C1 = 1.0 / (1.0 - ADAM_B1 ** ADAM_STEP)
ADAM_C2 = 1.0 / (1.0 - ADAM_B2 ** ADAM_STEP)


def _adam_math(w, g, m, v):
    m2 = ADAM_B1 * m + (1.0 - ADAM_B1) * g
    v2 = ADAM_B2 * v + (1.0 - ADAM_B2) * (g * g)
    return -ADAM_LR * ((m2 * ADAM_C1) / (jnp.sqrt(v2 * ADAM_C2) + ADAM_EPS) + ADAM_WD * w), m2, v2


def _adamw(w, g, m, v, *, name):
    R, C = w.shape
    tr = R if R <= 512 else 256

    def body(w_ref, g_ref, m_ref, v_ref, d_ref, nm_ref, nv_ref):
        d_ref[...], nm_ref[...], nv_ref[...] = _adam_math(w_ref[...], g_ref[...], m_ref[...], v_ref[...])

    blk = pl.BlockSpec((tr, C), lambda i: (i, 0))
    return pl.pallas_call(
        body, name=name, grid=(R // tr,), in_specs=[blk] * 4, out_specs=[blk] * 3,
        out_shape=[jax.ShapeDtypeStruct((R, C), F32)] * 3,
        compiler_params=_params(("parallel",)),
    )(w, g, m, v)


def _adamw_rs(wmv, cs, rcv, qidx, *, name):
    n = len(wmv)
    r, cc = wmv[0][0].shape
    tr = r // 2 if r % 32 == 0 and r > 128 else r

    def body(q_ref, *refs):
        ins, outs = refs[:5 * n], refs[5 * n:]
        for k in range(n):
            w_ref, m_ref, v_ref, c_ref, r_ref = ins[5 * k:5 * k + 5]
            g_ref, d_ref, nm_ref, nv_ref = outs[4 * k:4 * k + 4]
            g = ((c_ref[0].astype(F32) + r_ref[0].astype(F32)) + r_ref[1].astype(F32)) + r_ref[2].astype(F32)
            g_ref[...] = g
            d_ref[...], nm_ref[...], nv_ref[...] = _adam_math(w_ref[...], g, m_ref[...], v_ref[...])

    blk = pl.BlockSpec((tr, cc), lambda i, q_ref: (i, 0))
    one = [blk, blk, blk, pl.BlockSpec((1, tr, cc), lambda i, q_ref: (q_ref[0], i, 0)),
           pl.BlockSpec((3, tr, cc), lambda i, q_ref: (0, i, 0))]
    out = pl.pallas_call(
        body, name=name,
        grid_spec=pltpu.PrefetchScalarGridSpec(num_scalar_prefetch=1, grid=(r // tr,), in_specs=one * n,
                                               out_specs=[blk] * (4 * n)),
        out_shape=[jax.ShapeDtypeStruct((r, cc), F32)] * (4 * n),
        compiler_params=_params(("arbitrary",)),
    )(qidx, *[a for (w, m, v), c, rc in zip(wmv, cs, rcv) for a in (w, m, v, c, rc)])
    return [tuple(out[4 * k:4 * k + 4]) for k in range(n)]


SMALL_PARAMS = ("norm_ffn1", "norm_mix", "norm_ffn2", "norm_final", "q_norm", "kv_norm", "sinks", "rel_bias", "b_mod")


def _adamw_small(gvec, wmv):
    widths = [wmv[3 * i].shape[1] for i in range(len(SMALL_PARAMS))]

    def body(*refs):
        g_all = refs[0]
        ins = refs[1:1 + 3 * len(SMALL_PARAMS)]
        outs = refs[1 + 3 * len(SMALL_PARAMS):]
        off = N_MODVEC
        for i, name in enumerate(SMALL_PARAMS):
            g_ref, d_ref, nm_ref, nv_ref = outs[4 * i:4 * i + 4]
            w_ref, m_ref, v_ref = ins[3 * i:3 * i + 3]
            start = 0 if name == "b_mod" else off
            g = g_all[:, start:start + widths[i]]
            g_ref[...] = g
            d_ref[...], nm_ref[...], nv_ref[...] = _adam_math(w_ref[...], g, m_ref[...], v_ref[...])
            if name != "b_mod":
                off += dict(SMALL_LAYOUT)[name]

    vm = pl.BlockSpec(memory_space=pltpu.VMEM)
    n_out = 4 * len(SMALL_PARAMS)
    out = pl.pallas_call(
        body, name="adamw_small", in_specs=[vm] * (1 + len(wmv)), out_specs=[vm] * n_out,
        out_shape=[jax.ShapeDtypeStruct((1, widths[i // 4]), F32) for i in range(n_out)],
        compiler_params=_params(),
    )(gvec, *wmv)
    return {name: out[4 * i:4 * i + 4] for i, name in enumerate(SMALL_PARAMS)}


TRANSPOSED = ("g1T", "u1T", "g3T", "u3T", "w_inT", "w_uqT")


def kernel(x, c, w_mod, b_mod, norm_ffn1, ffn1_gate, ffn1_up, ffn1_down, norm_mix, w_in, q_norm, kv_norm, w_uq, w_ukv, sinks, w_o, norm_ffn2, ffn2_gate, ffn2_up, ffn2_down, rel_bias, norm_final, loss_target, m_w_mod, m_b_mod, m_norm_ffn1, m_ffn1_gate, m_ffn1_up, m_ffn1_down, m_norm_mix, m_w_in, m_q_norm, m_kv_norm, m_w_uq, m_w_ukv, m_sinks, m_w_o, m_norm_ffn2, m_ffn2_gate, m_ffn2_up, m_ffn2_down, m_rel_bias, m_norm_final, v_w_mod, v_b_mod, v_norm_ffn1, v_ffn1_gate, v_ffn1_up, v_ffn1_down, v_norm_mix, v_w_in, v_q_norm, v_kv_norm, v_w_uq, v_w_ukv, v_sinks, v_w_o, v_norm_ffn2, v_ffn2_gate, v_ffn2_up, v_ffn2_down, v_rel_bias, v_norm_final):
    mx, my, mc = _coords()
    cidx = jnp.reshape(mc, (1,)).astype(jnp.int32)
    qidx = jnp.reshape(2 * mx + my, (1,)).astype(jnp.int32)
    WM = w_mod.shape[2]

    c_tile = jnp.pad(c, ((0, 7), (0, 0)))
    b_mod3 = jnp.pad(b_mod.reshape(N_DEV, 1, WM), ((0, 0), (0, 7), (0, 0)))
    mod3, ca = _mod_fwd(c_tile, w_mod[0], b_mod3)
    mod9 = mod3[:, 0, :].reshape(N_MOD, D)

    shards = {"g1T": ffn1_gate[0].T.astype(BF16), "u1T": ffn1_up[0].T.astype(BF16), "d1": ffn1_down[0].astype(BF16),
              "g3T": ffn2_gate[0].T.astype(BF16), "u3T": ffn2_up[0].T.astype(BF16), "d3": ffn2_down[0].astype(BF16),
              "w_inT": w_in[0].T, "w_uqT": w_uq[0].T.astype(BF16), "w_ukv": w_ukv[0].astype(BF16),
              "w_o": w_o[0].astype(BF16)}
    me = 4 * mx + 2 * my + mc
    groups = {"ffn1": ("g1T", "u1T", "d1"), "mixer": ("w_inT", "w_uqT", "w_ukv", "w_o"), "ffn2": ("g3T", "u3T", "d3")}
    arriving = {}

    def as_weights(group, gathered):
        return {k: g if k == "w_ukv" else g.reshape(N_DEV * g.shape[1], g.shape[2])
                for k, g in zip(groups[group], gathered)}

    later = groups["mixer"] + groups["ffn2"]
    place = {"mixer": 0, "ffn2": len(groups["mixer"])}

    def start_gather(token):
        lands = []
        for k in later:
            sh = shards[k] + token[0, 0].astype(shards[k].dtype)
            lands.append(lax.dynamic_update_slice(lax.empty((N_DEV,) + sh.shape, sh.dtype), sh[None], (me, 0, 0)))
        batches = [range(k0, k0 + len(groups[group])) for group, k0 in place.items()]
        send, recv, lands, started = _gather_start(lands, name="gather_start", batches=batches)
        for group, k0 in place.items():
            arriving[group] = (send, recv, lands[k0:k0 + len(groups[group])])
        return started

    def fetch(group, after, vecs):
        if group == "ffn1":
            *gathered, token = _wgather([shards[k] + ca[1, 0].astype(shards[k].dtype) for k in groups["ffn1"]])
            return as_weights("ffn1", gathered), vecs + start_gather(token)[0:1, 0:1]

        def pass_on(group, after):
            send, recv, lands = arriving[group]
            lands, token = _gather_pass(send, recv, lands, after, name="gather_landed_" + group, stage="landed",
                                        k0=place[group])
            lands, token = _gather_pass(send, recv, lands, [token], name="gather_onward_" + group, stage="onward",
                                        k0=place[group])
            arriving[group] = (send, recv, lands)
            return token

        if group == "ffn2_on_its_way":
            return None, vecs + pass_on("ffn2", after)[0:1, 0:1]
        if group == "mixer":
            after = [pass_on("mixer", after)]
        send, recv, lands = arriving[group]
        return as_weights(group, _gather_end(send, recv, lands, after, name="gather_end_" + group,
                                             k0=place[group])), vecs

    norms ={"ffn1": norm_ffn1, "mix": norm_mix, "ffn2": norm_ffn2, "final": norm_final.reshape(1, D)}
    in_flight = {}

    def on_grads(group, g, after, vecs, before_ici=()):
        if g is not None:
            names = list(g)
            by_dest = [g[k] if k == "w_ukv" else g[k].reshape((N_DEV, g[k].shape[0] // N_DEV) + g[k].shape[1:])
                       for k in names]
            lands = [lax.empty((4,) + a.shape[1:], a.dtype) for a in by_dest]
            send, recv, by_dest, lands, token = _split_start(_d2d_copies, by_dest, lands, 4 * len(names), after,
                                                             name="rs_d2d_start_" + group)
            in_flight[group] = (names, send, recv, by_dest, lands)
            return vecs + token[0:1, 0:1]
        names, send, recv, by_dest, lands = in_flight[group]
        by_dest, from_sib = _split_wait(_d2d_copies, send, recv, by_dest, lands, after, name="rs_d2d_wait_" + group)
        sums = _chipsum(by_dest, from_sib, cidx, name="chipsum_" + group)
        lands = [lax.empty((3,) + s.shape[1:], s.dtype) for s in sums]
        send, recv, sums, lands, token = _split_start(_ici_copies, sums, lands, 3 * len(names), list(before_ici),
                                                      name="rs_ici_start_" + group)
        in_flight[group] = (names, send, recv, sums, lands, token)
        return vecs + token[0:1, 0:1]

    _, grad_x, _, vec = _local_step(
        x[0], loss_target[0], mod9, norms, sinks, rel_bias, q_norm, kv_norm, fetch, on_grads=on_grads)

    vec = vec.reshape(1, 1, N_VEC)
    allvec = lax.dynamic_update_slice(lax.empty((N_DEV, 1, N_VEC), F32), vec, (me, 0, 0))
    vsend, vrecv, _, (allvec,), vec_started = _split_start(_vec_copies, [], [allvec], N_DEV - 1, [], name="vec_start")
    on_grads("ffn1", None, [grad_x], jnp.zeros((1, 1), F32), before_ici=[vec_started])

    owners = {"g1T": ("ffn1_gate", ffn1_gate, m_ffn1_gate, v_ffn1_gate), "u1T": ("ffn1_up", ffn1_up, m_ffn1_up, v_ffn1_up),
              "d1": ("ffn1_down", ffn1_down, m_ffn1_down, v_ffn1_down),
              "g3T": ("ffn2_gate", ffn2_gate, m_ffn2_gate, v_ffn2_gate), "u3T": ("ffn2_up", ffn2_up, m_ffn2_up, v_ffn2_up),
              "d3": ("ffn2_down", ffn2_down, m_ffn2_down, v_ffn2_down),
              "w_inT": ("w_in", w_in, m_w_in, v_w_in), "w_uqT": ("w_uq", w_uq, m_w_uq, v_w_uq),
              "w_ukv": ("w_ukv", w_ukv, m_w_ukv, v_w_ukv), "w_o": ("w_o", w_o, m_w_o, v_w_o)}
    res, done = {}, []

    def finish(group, after, one_by_one=False):
        names, send, recv, sums, lands, _ = in_flight[group]
        there = lambda k, a: a[0].T if k in TRANSPOSED else a[0]
        back = lambda k, a: a.T[None] if k in TRANSPOSED else a[None]
        wmv = [tuple(there(k, a) for a in owners[k][1:]) for k in names]
        if one_by_one:
            outs = []
            for i, k in enumerate(names):
                (cs,), (rc,) = _split_wait(functools.partial(_ici_copies, k0=i), send, recv, [sums[i]], [lands[i]],
                                           after, name="rs_ici_wait_" + k)
                outs.append(_adamw_rs([wmv[i]], [cs], [rc], qidx, name="adamw_" + owners[k][0])[0])
                after = [outs[-1][3]]
        else:
            sums, lands = _split_wait(_ici_copies, send, recv, sums, lands, after, name="rs_ici_wait_" + group)
            if len({w.shape for w, _, _ in wmv}) == 1:
                outs = _adamw_rs(wmv, sums, lands, qidx, name="adamw_" + group)
            else:
                outs = [_adamw_rs([t], [cs], [rc], qidx, name="adamw_" + owners[k][0])[0]
                        for k, t, cs, rc in zip(names, wmv, sums, lands)]
        for k, out in zip(names, outs):
            done.append(out[3])
            res[owners[k][0]] = tuple(back(k, a) for a in out)

    ffn1_started = in_flight["ffn1"][5]
    finish("ffn2", [ffn1_started])
    finish("mixer", [ffn1_started])

    _, (allvec,) = _split_wait(_vec_copies, vsend, vrecv, [], [allvec], [ffn1_started], name="vec_wait")
    g_wmod, gvec = _mod_bwd(allvec, ca, jnp.reshape(me, (1,)).astype(jnp.int32))
    loss = gvec[0, N_MODVEC + LOSS_SLOT]
    res["w_mod"] = tuple(a[None] for a in (g_wmod,) + tuple(_adamw(w_mod[0], g_wmod, m_w_mod[0], v_w_mod[0],
                                                                    name="adamw_w_mod")))
    small_in = {"norm_ffn1": (norm_ffn1, m_norm_ffn1, v_norm_ffn1), "norm_mix": (norm_mix, m_norm_mix, v_norm_mix),
                "norm_ffn2": (norm_ffn2, m_norm_ffn2, v_norm_ffn2), "norm_final": (norm_final, m_norm_final, v_norm_final),
                "q_norm": (q_norm, m_q_norm, v_q_norm), "kv_norm": (kv_norm, m_kv_norm, v_kv_norm),
                "sinks": (sinks, m_sinks, v_sinks), "rel_bias": (rel_bias, m_rel_bias, v_rel_bias),
                "b_mod": (b_mod, m_b_mod, v_b_mod)}
    small_out = _adamw_small(gvec, [a.reshape(1, -1) for k in SMALL_PARAMS for a in small_in[k]])
    for k in SMALL_PARAMS:
        res[k] = tuple(a.reshape(small_in[k][0].shape) for a in small_out[k])

    finish("ffn1", done + [res["w_mod"][3], small_out["b_mod"][3]], one_by_one=True)

    order = ("w_mod", "b_mod", "norm_ffn1", "ffn1_gate", "ffn1_up", "ffn1_down", "norm_mix", "w_in", "q_norm",
             "kv_norm", "w_uq", "w_ukv", "sinks", "w_o", "norm_ffn2", "ffn2_gate", "ffn2_up", "ffn2_down",
             "rel_bias", "norm_final")
    return (loss, grad_x[None]) + tuple(res[nm][kind] for kind in range(4) for nm in order)
```

```python
import functools
import math

import numpy as np
import jax
import jax.numpy as jnp
from jax import lax
from jax.experimental import pallas as pl
from jax.experimental.pallas import tpu as pltpu

F32 = jnp.float32
BF16 = jnp.bfloat16
MESH = pl.DeviceIdType.MESH

N_DEV = 8
D = 1024
D_FF = 2816
EPS = 1e-6
N_MOD = 9
SWA_HEADS = 8
SWA_DH = 64
WINDOW = 128
MLA_HEADS = 4
MLA_NOPE = 128
MLA_ROPE = 64
MLA_V = 128
MLA_QR = 256
MLA_KVR = 128
ROPE_THETA = 10000.0
NUM_BUCKETS = 32
D_IN = 1216
D_IN_PAD = 1280
SWA_SCALE = SWA_DH ** -0.5
MLA_SCALE = (MLA_NOPE + MLA_ROPE) ** -0.5

ADAM_LR = 0.001
ADAM_B1 = 0.9
ADAM_B2 = 0.999
ADAM_EPS = 1e-08
ADAM_WD = 0.01
ADAM_STEP = 10

V7X_VMEM_LIMIT = 56 * 1024 * 1024
ROW_TILE = 512

NT_DIMS = (((1,), (1,)), ((), ()))
TN_DIMS = (((0,), (0,)), ((), ()))


def _dot(a, b):
    return jnp.dot(a, b, preferred_element_type=F32)


def _dot_nt(a, b):
    return lax.dot_general(a, b, NT_DIMS, preferred_element_type=F32)


def _dot_tn(a, b):
    return lax.dot_general(a, b, TN_DIMS, preferred_element_type=F32)


def _params(sem=None):
    return pltpu.CompilerParams(dimension_semantics=sem, vmem_limit_bytes=V7X_VMEM_LIMIT)


def _rstd(x):
    return lax.rsqrt(jnp.mean(x * x, axis=-1, keepdims=True) + EPS)


def _rms_bwd(dy, xhat, r):
    return r * (dy - xhat * jnp.mean(dy * xhat, axis=-1, keepdims=True))


def _sigmoid(a):
    return 1.0 / (1.0 + jnp.exp(-a))


def _ffn_fwd(x, vecs, wgT, wuT, wd, *, name, tm=512, tf=256):
    S, F = x.shape[0], wd.shape[0]
    tm = min(tm, S)
    ni, nj = S // tm, F // tf

    def body(x_ref, vec_ref, wg_ref, wu_ref, wd_ref, xo_ref, h_ref, a_ref, b_ref, f_ref, acc_ref):
        j = pl.program_id(1)

        @pl.when(j == 0)
        def _():
            xv = x_ref[...]
            hn = xv * _rstd(xv) * vec_ref[0:1, :]
            h_ref[...] = (hn * (1.0 + vec_ref[2:3, :]) + vec_ref[1:2, :]).astype(BF16)

        h = h_ref[...]
        a = _dot_nt(h, wg_ref[...])
        b = _dot_nt(h, wu_ref[...])
        a_ref[...] = a.astype(BF16)
        b_ref[...] = b.astype(BF16)
        part = _dot((a * _sigmoid(a) * b).astype(BF16), wd_ref[...])

        def finish(f):
            f_ref[...] = f
            xo_ref[...] = x_ref[...] + (0.5 * vec_ref[3:4, :]) * f

        if nj == 1:
            finish(part)
        else:
            @pl.when(j == 0)
            def _():
                acc_ref[...] = part

            @pl.when((j > 0) & (j < nj - 1))
            def _():
                acc_ref[...] += part

            @pl.when(j == nj - 1)
            def _():
                finish(acc_ref[...] + part)

    row = pl.BlockSpec((tm, D), lambda i, j: (i, 0))
    wspec = pl.BlockSpec((tf, D), lambda i, j: (j, 0), pipeline_mode=pl.Buffered(1) if nj == 1 else None)
    act = pl.BlockSpec((tm, tf), lambda i, j: (i, j))
    return pl.pallas_call(
        body, name=name, grid=(ni, nj),
        in_specs=[row, pl.BlockSpec((8, D), lambda i, j: (0, 0)), wspec, wspec, wspec],
        out_specs=[row, row, act, act, row],
        out_shape=[jax.ShapeDtypeStruct((S, D), F32), jax.ShapeDtypeStruct((S, D), BF16),
                   jax.ShapeDtypeStruct((S, F), BF16), jax.ShapeDtypeStruct((S, F), BF16),
                   jax.ShapeDtypeStruct((S, D), F32)],
        scratch_shapes=[pltpu.VMEM((tm, D) if nj > 1 else (8, 128), F32)],
        compiler_params=_params(("parallel", "arbitrary")),
    )(x, vecs, wgT, wuT, wd)


def _ffn_bwd_main(h, df, a, b, wgT, wuT, wd, *, name, after=(), tm=512, tf=256):
    S = h.shape[0]
    tm = min(tm, S)
    ni, nj = S // tm, D_FF // tf

    def body(h_hbm, df_hbm, a_ref, b_ref, wg_ref, wu_ref, wd_ref, *rest):
        gg_ref, gu_ref, gd_ref, dh_hbm, h_v, df_v, dh_v, gg_acc, gu_acc, gd_acc, sem = rest[len(after):]
        j = pl.program_id(0)
        i = pl.program_id(1)

        @pl.when((j == 0) & (i == 0))
        def _():
            c1 = pltpu.make_async_copy(h_hbm, h_v, sem.at[0])
            c2 = pltpu.make_async_copy(df_hbm, df_v, sem.at[1])
            c1.start()
            c2.start()
            c1.wait()
            c2.wait()

        @pl.when(i == 0)
        def _():
            gg_acc[...] = jnp.zeros_like(gg_acc)
            gu_acc[...] = jnp.zeros_like(gu_acc)
            gd_acc[...] = jnp.zeros_like(gd_acc)

        rows = pl.ds(pl.multiple_of(i * tm, tm), tm)
        hi = h_v[rows, :]
        dfi = df_v[rows, :]
        av = a_ref[...].astype(F32)
        bv = b_ref[...].astype(F32)
        sg = _sigmoid(av)
        sa = av * sg
        hsw = (sa * bv).astype(BF16)
        dhsw = _dot_nt(dfi, wd_ref[...])
        da = (dhsw * bv * (sg * (1.0 + av * (1.0 - sg)))).astype(BF16)
        db = (dhsw * sa).astype(BF16)
        gd_acc[...] += _dot_tn(hsw, dfi)
        gg_acc[...] += _dot_tn(da, hi)
        gu_acc[...] += _dot_tn(db, hi)
        dh = _dot(da, wg_ref[...]) + _dot(db, wu_ref[...])

        @pl.when(j == 0)
        def _():
            dh_v[rows, :] = dh

        @pl.when(j > 0)
        def _():
            dh_v[rows, :] += dh

        @pl.when(i == ni - 1)
        def _():
            gg_ref[...] = gg_acc[...].astype(BF16)
            gu_ref[...] = gu_acc[...].astype(BF16)
            gd_ref[...] = gd_acc[...].astype(BF16)

        @pl.when((j == nj - 1) & (i == ni - 1))
        def _():
            c3 = pltpu.make_async_copy(dh_v, dh_hbm, sem.at[2])
            c3.start()
            c3.wait()

    anyspec = pl.BlockSpec(memory_space=pl.ANY)
    wspec = pl.BlockSpec((tf, D), lambda j, i: (j, 0))
    act = pl.BlockSpec((tm, tf), lambda j, i: (i, j))
    return pl.pallas_call(
        body, name=name, grid=(nj, ni),
        in_specs=[anyspec, anyspec, act, act, wspec, wspec, wspec] + [anyspec] * len(after),
        out_specs=[wspec, wspec, wspec, anyspec],
        out_shape=[jax.ShapeDtypeStruct((D_FF, D), BF16)] * 3 + [jax.ShapeDtypeStruct((S, D), F32)],
        scratch_shapes=[pltpu.VMEM((S, D), BF16), pltpu.VMEM((S, D), BF16), pltpu.VMEM((S, D), F32),
                        pltpu.VMEM((tf, D), F32), pltpu.VMEM((tf, D), F32), pltpu.VMEM((tf, D), F32),
                        pltpu.SemaphoreType.DMA((3,))],
        compiler_params=_params(("arbitrary", "arbitrary")),
    )(h, df, a, b, wgT, wuT, wd, *after)


def _ffn_out_bwd(dx, f, gate, df_ref, part_ref):
    df_ref[...] = ((0.5 * gate) * dx).astype(BF16)
    part_ref[3:4, :] += 0.5 * jnp.sum(dx * f, axis=0, keepdims=True)


def _norm_bwd(dh, x, dxo, vecs, *, name, below=None, tm=ROW_TILE):
    S = x.shape[0]
    tm = min(tm, S)

    def body(dh_ref, x_ref, dxo_ref, vec_ref, *rest):
        dx_ref, part_ref = rest[-2 if below is None else -3], rest[-1 if below is None else -2]

        @pl.when(pl.program_id(0) == 0)
        def _():
            part_ref[...] = jnp.zeros_like(part_ref)

        dh = dh_ref[...]
        xv = x_ref[...]
        r = _rstd(xv)
        xhat = xv * r
        w = vec_ref[0:1, :]
        xn = xhat * w
        dxn = dh * (1.0 + vec_ref[2:3, :])
        part_ref[0:1, :] += jnp.sum(dxn * xhat, axis=0, keepdims=True)
        part_ref[1:2, :] += jnp.sum(dh, axis=0, keepdims=True)
        part_ref[2:3, :] += jnp.sum(dh * xn, axis=0, keepdims=True)
        dx = dxo_ref[...] + _rms_bwd(dxn * w, xhat, r)
        dx_ref[...] = dx
        if below is not None:
            _ffn_out_bwd(dx, rest[0][...], rest[1][3:4, :], rest[-1], part_ref)

    row = pl.BlockSpec((tm, D), lambda i: (i, 0))
    vec = pl.BlockSpec((8, D), lambda i: (0, 0))
    extra = [] if below is None else [row, vec]
    return pl.pallas_call(
        body, name=name, grid=(S // tm,), in_specs=[row, row, row, vec] + extra,
        out_specs=[row, vec] + ([] if below is None else [row]),
        out_shape=[jax.ShapeDtypeStruct((S, D), F32), jax.ShapeDtypeStruct((8, D), F32)]
        + ([] if below is None else [jax.ShapeDtypeStruct((S, D), BF16)]),
        compiler_params=_params(("arbitrary",)),
    )(dh, x, dxo, vecs, *([] if below is None else below))


def _head(x, tgt, nf, f, vecs, *, tm=ROW_TILE):
    S = x.shape[0]
    tm = min(tm, S)

    def body(x_ref, t_ref, nf_ref, f_ref, vec_ref, dx_ref, part_ref, df_ref):
        @pl.when(pl.program_id(0) == 0)
        def _():
            part_ref[...] = jnp.zeros_like(part_ref)

        xv = x_ref[...]
        r = _rstd(xv)
        xhat = xv * r
        w = nf_ref[...]
        e = xhat * w - t_ref[...]
        dy = e * (1.0 / D)
        part_ref[0:1, :] += jnp.sum(dy * xhat, axis=0, keepdims=True)
        part_ref[1:2, :] += jnp.sum(e * e) * (0.5 / D)
        dx = _rms_bwd(dy * w, xhat, r)
        dx_ref[...] = dx
        _ffn_out_bwd(dx, f_ref[...], vec_ref[3:4, :], df_ref, part_ref)

    row = pl.BlockSpec((tm, D), lambda i: (i, 0))
    vec = pl.BlockSpec((8, D), lambda i: (0, 0))
    return pl.pallas_call(
        body, name="head", grid=(S // tm,),
        in_specs=[row, row, pl.BlockSpec((1, D), lambda i: (0, 0)), row, vec],
        out_specs=[row, vec, row],
        out_shape=[jax.ShapeDtypeStruct((S, D), F32), jax.ShapeDtypeStruct((8, D), F32),
                   jax.ShapeDtypeStruct((S, D), BF16)],
        compiler_params=_params(("arbitrary",)),
    )(x, tgt, nf, f, vecs)


def _mix_in_fwd(x, vecs, w_inT, *, tm=ROW_TILE):
    S = x.shape[0]
    tm = min(tm, S)

    def body(x_ref, vec_ref, w_ref, h_ref, p_ref):
        xv = x_ref[...]
        hn = xv * _rstd(xv) * vec_ref[0:1, :]
        h = (hn * (1.0 + vec_ref[2:3, :]) + vec_ref[1:2, :]).astype(BF16)
        h_ref[...] = h
        p_ref[...] = _dot_nt(h, w_ref[...])

    row = pl.BlockSpec((tm, D), lambda i: (i, 0))
    return pl.pallas_call(
        body, name="mix_in_fwd", grid=(S // tm,),
        in_specs=[row, pl.BlockSpec((8, D), lambda i: (0, 0)), pl.BlockSpec((D_IN_PAD, D), lambda i: (0, 0))],
        out_specs=[row, pl.BlockSpec((tm, D_IN_PAD), lambda i: (i, 0))],
        out_shape=[jax.ShapeDtypeStruct((S, D), BF16), jax.ShapeDtypeStruct((S, D_IN_PAD), F32)],
        compiler_params=_params(("parallel",)),
    )(x, vecs, w_inT)


def _bucket_table():
    qi = np.arange(WINDOW)[:, None]
    kj = np.arange(2 * WINDOW)[None, :]
    dist = qi + WINDOW - kj
    max_exact = NUM_BUCKETS // 2
    n = np.maximum(dist, 0)
    nf = np.maximum(n, 1).astype(np.float32)
    large = max_exact + (np.log(nf / np.float32(max_exact)) / np.float32(math.log(WINDOW / max_exact))
                         * np.float32(NUM_BUCKETS - max_exact)).astype(np.int32)
    large = np.minimum(large, NUM_BUCKETS - 1)
    return np.where(n < max_exact, n, large).astype(np.int32)


def _bias_build(rel_bias, bucket):
    def body(rb_ref, bk_ref, out_ref):
        bk = bk_ref[...]
        for h in range(SWA_HEADS):
            acc = jnp.zeros((WINDOW, 2 * WINDOW), F32)
            for b in range(NUM_BUCKETS):
                acc = jnp.where(bk == b, rb_ref[b, h], acc)
            out_ref[h] = acc

    return pl.pallas_call(
        body, name="bias_build",
        in_specs=[pl.BlockSpec(memory_space=pltpu.SMEM), pl.BlockSpec(memory_space=pltpu.VMEM)],
        out_specs=pl.BlockSpec(memory_space=pltpu.VMEM),
        out_shape=jax.ShapeDtypeStruct((SWA_HEADS, WINDOW, 2 * WINDOW), F32),
    )(rel_bias, bucket)


SWA_GROUP = 4
GROUP_ROWS = SWA_GROUP * WINDOW


SWA_SUB = 2


def _swa_valid(has_prev):
    row = lax.broadcasted_iota(jnp.int32, (GROUP_ROWS, 2 * WINDOW), 0) % WINDOW
    col = lax.broadcasted_iota(jnp.int32, (GROUP_ROWS, 2 * WINDOW), 1)
    dist = row + WINDOW - col
    return (dist >= 0) & (dist < WINDOW) & ((col >= WINDOW) | has_prev)


def _swa_keys(prev_ref, cur_ref, u):
    cur = cur_ref[...]
    before = prev_ref[...] if u == 0 else cur[WINDOW * (u - 1):WINDOW * u]
    return jnp.concatenate([before, cur[WINDOW * u:WINDOW * (u + 1)]], axis=0).astype(BF16)


def _stack_heads(x, g):
    return jnp.concatenate([x[:, 64 * h:64 * h + 64] for h in range(SWA_GROUP * g, SWA_GROUP * (g + 1))], axis=0)


def _unstack_heads(x4):
    return jnp.concatenate([x4[WINDOW * a:WINDOW * (a + 1)] for a in range(SWA_GROUP)], axis=1)


def _group_sinks(sink_ref, g):
    head = lax.broadcasted_iota(jnp.int32, (GROUP_ROWS, 1), 0) // WINDOW
    out = jnp.full((GROUP_ROWS, 1), sink_ref[0, SWA_GROUP * g], F32)
    for a in range(1, SWA_GROUP):
        out = jnp.where(head == a, sink_ref[0, SWA_GROUP * g + a], out)
    return out


def _swa_probs(qh, kk, bias_h, sink, valid):
    s = _dot_nt(qh, kk) * SWA_SCALE + bias_h
    s = jnp.where(valid, s, -jnp.inf)
    m = jnp.maximum(jnp.max(s, axis=-1, keepdims=True), sink)
    p = jnp.exp(s - m)
    ps = jnp.exp(sink - m)
    inv = 1.0 / (jnp.sum(p, axis=-1, keepdims=True) + ps)
    return p * inv, ps * inv


SWA_ROWS = SWA_SUB * WINDOW


def _swa_specs():
    prev = lambda n: jnp.maximum(SWA_SUB * n - 1, 0)
    return [pl.BlockSpec((SWA_ROWS, 512), lambda n: (n, 0)),
            pl.BlockSpec((SWA_ROWS, 128), lambda n: (n, 4)),
            pl.BlockSpec((WINDOW, 128), lambda n: (prev(n), 4)),
            pl.BlockSpec((SWA_ROWS, 128), lambda n: (n, 5)),
            pl.BlockSpec((WINDOW, 128), lambda n: (prev(n), 5)),
            pl.BlockSpec((SWA_HEADS, WINDOW, 2 * WINDOW), lambda n: (0, 0, 0)),
            pl.BlockSpec(memory_space=pltpu.SMEM)]


def _swa_fwd(proj, bias, sinks):
    S = proj.shape[0]

    def body(q_ref, kc_ref, kp_ref, vc_ref, vp_ref, bias_ref, sink_ref, o_ref):
        n = pl.program_id(0)
        for u in range(SWA_SUB):
            rows = slice(WINDOW * u, WINDOW * (u + 1))
            valid = _swa_valid(n > 0 if u == 0 else True)
            q = q_ref[rows, :].astype(BF16)
            kfull = _swa_keys(kp_ref, kc_ref, u)
            vfull = _swa_keys(vp_ref, vc_ref, u)
            for g in range(SWA_HEADS // SWA_GROUP):
                kk = kfull[:, 64 * g:64 * g + 64]
                vv = vfull[:, 64 * g:64 * g + 64]
                bias4 = bias_ref[SWA_GROUP * g:SWA_GROUP * (g + 1)].reshape(GROUP_ROWS, 2 * WINDOW)
                pk, _ = _swa_probs(_stack_heads(q, g), kk, bias4, _group_sinks(sink_ref, g), valid)
                o_ref[rows, 256 * g:256 * (g + 1)] = _unstack_heads(_dot(pk.astype(BF16), vv))

    return pl.pallas_call(
        body, name="swa_fwd", grid=(S // SWA_ROWS,),
        in_specs=_swa_specs(),
        out_specs=pl.BlockSpec((SWA_ROWS, 512), lambda n: (n, 0)),
        out_shape=jax.ShapeDtypeStruct((S, 512), F32),
        compiler_params=_params(("parallel",)),
    )(proj, proj, proj, proj, proj, bias, sinks)


def _swa_bwd(proj, bias, sinks, o, do, bucket):
    S = proj.shape[0]
    nb = S // SWA_ROWS

    def body(q_ref, kc_ref, kp_ref, vc_ref, vp_ref, bias_ref, sink_ref, o_ref, do_ref, bk_ref,
             dq_ref, dk_ref, dv_ref, drb_ref, dsk_ref, dbias_acc):
        n = pl.program_id(0)

        @pl.when(n == 0)
        def _():
            dk_ref[...] = jnp.zeros_like(dk_ref)
            dv_ref[...] = jnp.zeros_like(dv_ref)
            dsk_ref[...] = jnp.zeros_like(dsk_ref)
            dbias_acc[...] = jnp.zeros_like(dbias_acc)
            drb_ref[...] = jnp.zeros_like(drb_ref)

        for u in range(SWA_SUB):
            rows = slice(WINDOW * u, WINDOW * (u + 1))
            blk = SWA_SUB * n + u
            valid = _swa_valid(n > 0 if u == 0 else True)
            q = q_ref[rows, :].astype(BF16)
            dov = do_ref[rows, :]
            ov = o_ref[rows, :]
            kfull = _swa_keys(kp_ref, kc_ref, u)
            vfull = _swa_keys(vp_ref, vc_ref, u)
            prow = pl.ds(pl.multiple_of(jnp.maximum(blk - 1, 0) * WINDOW, WINDOW), WINDOW)
            crow = pl.ds(pl.multiple_of(blk * WINDOW, WINDOW), WINDOW)
            for g in range(SWA_HEADS // SWA_GROUP):
                heads = slice(SWA_GROUP * g, SWA_GROUP * (g + 1))
                kk = kfull[:, 64 * g:64 * g + 64]
                vv = vfull[:, 64 * g:64 * g + 64]
                q4 = _stack_heads(q, g)
                pk, psink = _swa_probs(q4, kk, bias_ref[heads].reshape(GROUP_ROWS, 2 * WINDOW),
                                       _group_sinks(sink_ref, g), valid)
                pkb = pk.astype(BF16)
                do4 = _stack_heads(dov, g)
                dob = do4.astype(BF16)
                dp = _dot_nt(dob, vv)
                delta = jnp.sum(do4 * _stack_heads(ov, g), axis=-1, keepdims=True)
                ds = pk * (dp - delta)
                dsink = -psink * delta
                for a in range(SWA_GROUP):
                    h = SWA_GROUP * g + a
                    part = jnp.sum(dsink[WINDOW * a:WINDOW * (a + 1)], keepdims=True)
                    dsk_ref[h:h + 1, :] += jnp.broadcast_to(part, (1, 128))
                dbias_acc[heads] += ds.reshape(SWA_GROUP, WINDOW, 2 * WINDOW)
                dsb = (ds * SWA_SCALE).astype(BF16)
                dq_ref[rows, 256 * g:256 * (g + 1)] = _unstack_heads(_dot(dsb, kk))
                dkk = _dot_tn(dsb, q4)
                dvv = _dot_tn(pkb, dob)
                dk_ref[prow, 64 * g:64 * g + 64] += dkk[:WINDOW]
                dk_ref[crow, 64 * g:64 * g + 64] += dkk[WINDOW:]
                dv_ref[prow, 64 * g:64 * g + 64] += dvv[:WINDOW]
                dv_ref[crow, 64 * g:64 * g + 64] += dvv[WINDOW:]

        @pl.when(n == nb - 1)
        def _():
            bk = bk_ref[...]
            for h in range(SWA_HEADS):
                dbh = dbias_acc[h]
                for b in range(NUM_BUCKETS):
                    val = jnp.sum(jnp.where(bk == b, dbh, 0.0), keepdims=True)
                    drb_ref[b * 8 + h:b * 8 + h + 1, :] = jnp.broadcast_to(val, (1, 128))

    full = lambda shape: pl.BlockSpec(shape, lambda n: tuple(0 for _ in shape))
    return pl.pallas_call(
        body, name="swa_bwd", grid=(nb,),
        in_specs=_swa_specs() + [pl.BlockSpec((SWA_ROWS, 512), lambda n: (n, 0)),
                                 pl.BlockSpec((SWA_ROWS, 512), lambda n: (n, 0)), full((WINDOW, 2 * WINDOW))],
        out_specs=[pl.BlockSpec((SWA_ROWS, 512), lambda n: (n, 0)), full((S, 128)), full((S, 128)),
                   full((NUM_BUCKETS * 8, 128)), full((8, 128))],
        out_shape=[jax.ShapeDtypeStruct((S, 512), F32), jax.ShapeDtypeStruct((S, 128), F32),
                   jax.ShapeDtypeStruct((S, 128), F32), jax.ShapeDtypeStruct((NUM_BUCKETS * 8, 128), F32),
                   jax.ShapeDtypeStruct((8, 128), F32)],
        scratch_shapes=[pltpu.VMEM((SWA_HEADS, WINDOW, 2 * WINDOW), F32)],
        compiler_params=_params(("arbitrary",)),
    )(proj, proj, proj, proj, proj, bias, sinks, o, do, bucket)


def _rope_tables(S):
    inv = ROPE_THETA ** (-jnp.arange(0, MLA_ROPE, 2, dtype=F32) / MLA_ROPE)
    ang = jnp.arange(S, dtype=F32)[:, None] * inv[None, :]
    cos, sin = jnp.cos(ang), jnp.sin(ang)
    return jnp.tile(jnp.concatenate([cos, cos], axis=1), (1, 4)), jnp.tile(jnp.concatenate([-sin, sin], axis=1), (1, 4))


def _swap_halves(x):
    w = x.shape[-1]
    lane = lax.broadcasted_iota(jnp.int32, x.shape, x.ndim - 1)
    return jnp.where((lane % 64) < 32, pltpu.roll(x, w - 32, x.ndim - 1), pltpu.roll(x, 32, x.ndim - 1))


def _mla_pre_fwd(proj, qn_w, kvn_w, wuqT, wukv, cos, sin, *, tm=ROW_TILE):
    S = proj.shape[0]
    tm = min(tm, S)

    def body(ql_ref, kl_ref, kr_ref, qw_ref, kw_ref, wuq_ref, wukv_ref, cos_ref, sin_ref,
             qc_ref, kc_ref, vv_ref):
        ql = ql_ref[...]
        qn = (ql * _rstd(ql) * qw_ref[...]).astype(BF16)
        q = _dot_nt(qn, wuq_ref[...])
        cs, sn = cos_ref[...], sin_ref[...]
        qr = q[:, 512:768]
        qr = qr * cs + _swap_halves(qr) * sn
        half = lax.broadcasted_iota(jnp.int32, (tm, 128), 1) // 64
        kl = kl_ref[...]
        kvn = (kl * _rstd(kl) * kw_ref[...]).astype(BF16)
        kr = kr_ref[...]
        kr = kr * cs[:, :128] + _swap_halves(kr) * sn[:, :128]
        kr2 = (kr + pltpu.roll(kr, 64, 1)).astype(BF16)
        for h in range(MLA_HEADS):
            qc_ref[h, :, 0:128] = q[:, 128 * h:128 * h + 128].astype(BF16)
            chunk = qr[:, 128 * (h // 2):128 * (h // 2) + 128]
            qc_ref[h, :, 128:256] = jnp.where(half == (h % 2), chunk, 0.0).astype(BF16)
            kc_ref[h, :, 0:128] = _dot(kvn, wukv_ref[2 * h]).astype(BF16)
            kc_ref[h, :, 128:256] = kr2
            vv_ref[h] = _dot(kvn, wukv_ref[2 * h + 1]).astype(BF16)

    const = lambda shape: pl.BlockSpec(shape, lambda i: tuple(0 for _ in shape))
    return pl.pallas_call(
        body, name="mla_pre_fwd", grid=(S // tm,),
        in_specs=[pl.BlockSpec((tm, 256), lambda i: (i, 3)), pl.BlockSpec((tm, 128), lambda i: (i, 8)),
                  pl.BlockSpec((tm, 128), lambda i: (i, 9)), const((1, 256)), const((1, 128)),
                  const((768, 256)), const((8, 128, 128)),
                  pl.BlockSpec((tm, 256), lambda i: (i, 0)), pl.BlockSpec((tm, 256), lambda i: (i, 0))],
        out_specs=[pl.BlockSpec((MLA_HEADS, tm, 256), lambda i: (0, i, 0)),
                   pl.BlockSpec((MLA_HEADS, tm, 256), lambda i: (0, i, 0)),
                   pl.BlockSpec((MLA_HEADS, tm, 128), lambda i: (0, i, 0))],
        out_shape=[jax.ShapeDtypeStruct((MLA_HEADS, S, 256), BF16), jax.ShapeDtypeStruct((MLA_HEADS, S, 256), BF16),
                   jax.ShapeDtypeStruct((MLA_HEADS, S, 128), BF16)],
        compiler_params=_params(("parallel",)),
    )(proj, proj, proj, qn_w, kvn_w, wuqT, wukv, cos, sin)


def _causal(i, j, t):
    row = i * t + lax.broadcasted_iota(jnp.int32, (t, t), 0)
    col = j * t + lax.broadcasted_iota(jnp.int32, (t, t), 1)
    return col <= row


def _mla_attn_fwd(qc, kc, vv, *, t=256):
    S = qc.shape[1]
    t = min(t, S)

    def body(q_ref, k_ref, v_ref, o_ref, l_ref):
        i = pl.program_id(0)
        diag = _causal(0, 0, t)

        def step(j, carry, masked):
            rows = pl.ds(pl.multiple_of(j * t, t), t)
            out = []
            for h in range(MLA_HEADS):
                m, l, acc = carry[h]
                s = _dot_nt(q_ref[h], k_ref[h, rows, :]) * MLA_SCALE
                if masked:
                    s = jnp.where(diag, s, -jnp.inf)
                m_new = jnp.maximum(m, jnp.max(s, axis=-1, keepdims=True))
                alpha = jnp.exp(m - m_new)
                p = jnp.exp(s - m_new)
                l = alpha * l + jnp.sum(p, axis=-1, keepdims=True)
                acc = alpha * acc + _dot(p.astype(BF16), v_ref[h, rows, :])
                out.append((m_new, l, acc))
            return tuple(out)

        init = tuple((jnp.full((t, 1), -jnp.inf, F32), jnp.zeros((t, 1), F32), jnp.zeros((t, MLA_V), F32))
                     for _ in range(MLA_HEADS))
        carry = lax.fori_loop(0, i, lambda j, c: step(j, c, False), init)
        carry = step(i, carry, True)
        for h in range(MLA_HEADS):
            m, l, acc = carry[h]
            o_ref[:, 128 * h:128 * h + 128] = acc / l
            l_ref[h] = jnp.broadcast_to(m + jnp.log(l), (t, 128))

    return pl.pallas_call(
        body, name="mla_attn_fwd", grid=(S // t,),
        in_specs=[pl.BlockSpec((MLA_HEADS, t, 256), lambda i: (0, i, 0)),
                  pl.BlockSpec((MLA_HEADS, S, 256), lambda i: (0, 0, 0)),
                  pl.BlockSpec((MLA_HEADS, S, 128), lambda i: (0, 0, 0))],
        out_specs=[pl.BlockSpec((t, 512), lambda i: (i, 0)),
                   pl.BlockSpec((MLA_HEADS, t, 128), lambda i: (0, i, 0))],
        out_shape=[jax.ShapeDtypeStruct((S, 512), F32), jax.ShapeDtypeStruct((MLA_HEADS, S, 128), F32)],
        compiler_params=_params(("parallel",)),
    )(qc, kc, vv)


def _mla_attn_bwd(qc, kc, vv, o, lse, do, *, t=256, tq=512):
    S = qc.shape[1]
    t = min(t, S)
    tq = min(tq, S)
    nblk = S // t
    hp = MLA_HEADS
    once = pl.Buffered(1)

    def body(q_ref, k_ref, v_ref, o_ref, l_ref, do_ref, dq_ref, dk_ref, dv_ref):
        j = pl.program_id(1)

        @pl.when(j == 0)
        def _():
            dq_ref[...] = jnp.zeros_like(dq_ref)

        first = (j * t) // tq

        def step(i, carry, masked):
            rows = pl.ds(pl.multiple_of(i * tq, tq), tq)
            if masked:
                row = i * tq + lax.broadcasted_iota(jnp.int32, (tq, t), 0)
                col = j * t + lax.broadcasted_iota(jnp.int32, (tq, t), 1)
                visible = col <= row
            out = []
            for h in range(hp):
                dk, dv = carry[h]
                k = k_ref[h]
                q = q_ref[h, rows, :]
                dov = do_ref[rows, 128 * h:128 * h + 128]
                lrow = l_ref[h, rows, :][:, 0:1]
                p = jnp.exp(_dot_nt(q, k) * MLA_SCALE - lrow)
                if masked:
                    p = jnp.where(visible, p, 0.0)
                dob = dov.astype(BF16)
                dv = dv + _dot_tn(p.astype(BF16), dob)
                dp = _dot_nt(dob, v_ref[h])
                delta = jnp.sum(dov * o_ref[rows, 128 * h:128 * h + 128], axis=-1, keepdims=True)
                ds = (p * (dp - delta) * MLA_SCALE).astype(BF16)
                dk = dk + _dot_tn(ds, q)
                dq_ref[h, rows, :] += _dot(ds, k)
                out.append((dk, dv))
            return tuple(out)

        init = tuple((jnp.zeros((t, 256), F32), jnp.zeros((t, MLA_V), F32)) for _ in range(hp))
        carry = step(first, init, True)
        carry = lax.fori_loop(first + 1, S // tq, lambda i, c: step(i, c, False), carry)
        for h in range(hp):
            dk_ref[h] = carry[h][0]
            dv_ref[h] = carry[h][1]

    return pl.pallas_call(
        body, name="mla_attn_bwd", grid=(MLA_HEADS // hp, nblk),
        in_specs=[pl.BlockSpec((hp, S, 256), lambda g, j: (g, 0, 0), pipeline_mode=once),
                  pl.BlockSpec((hp, t, 256), lambda g, j: (g, j, 0)),
                  pl.BlockSpec((hp, t, 128), lambda g, j: (g, j, 0)),
                  pl.BlockSpec((S, 128 * hp), lambda g, j: (0, g), pipeline_mode=once),
                  pl.BlockSpec((hp, S, 128), lambda g, j: (g, 0, 0), pipeline_mode=once),
                  pl.BlockSpec((S, 128 * hp), lambda g, j: (0, g), pipeline_mode=once)],
        out_specs=[pl.BlockSpec((hp, S, 256), lambda g, j: (g, 0, 0)),
                   pl.BlockSpec((hp, t, 256), lambda g, j: (g, j, 0)),
                   pl.BlockSpec((hp, t, 128), lambda g, j: (g, j, 0))],
        out_shape=[jax.ShapeDtypeStruct((MLA_HEADS, S, 256), F32), jax.ShapeDtypeStruct((MLA_HEADS, S, 256), F32),
                   jax.ShapeDtypeStruct((MLA_HEADS, S, 128), F32)],
        compiler_params=_params(("parallel", "arbitrary")),
    )(qc, kc, vv, o, lse, do)


def _mla_pre_bwd(proj, qn_w, kvn_w, wuqT, wukv, cos, sin, dqc, dkc, dvv, *, tm=ROW_TILE):
    S = proj.shape[0]
    tm = min(tm, S)

    def body(ql_ref, kl_ref, qw_ref, kw_ref, wuq_ref, wukv_ref, cos_ref, sin_ref, dqc_ref, dkc_ref, dvv_ref,
             dql_ref, dkl_ref, dkr_ref, gq_ref, gkv_ref, part_ref):
        @pl.when(pl.program_id(0) == 0)
        def _():
            gq_ref[...] = jnp.zeros_like(gq_ref)
            gkv_ref[...] = jnp.zeros_like(gkv_ref)
            part_ref[...] = jnp.zeros_like(part_ref)

        cs, sn = cos_ref[...], sin_ref[...]
        half = lax.broadcasted_iota(jnp.int32, (tm, 128), 1) // 64
        ql = ql_ref[...]
        rq = _rstd(ql)
        qhat = ql * rq
        qw = qw_ref[...]
        qn = (qhat * qw).astype(BF16)
        chunks = []
        for pair in range(2):
            chunks.append(jnp.where(half == 0, dqc_ref[2 * pair, :, 128:256], dqc_ref[2 * pair + 1, :, 128:256]))
        dqr = jnp.concatenate(chunks, axis=1)
        dqr = dqr * cs + _swap_halves(dqr * sn)
        dq = jnp.concatenate([dqc_ref[h, :, 0:128] for h in range(MLA_HEADS)] + [dqr], axis=1).astype(BF16)
        gq_ref[...] += _dot_tn(dq, qn)
        dqn = _dot(dq, wuq_ref[...])
        part_ref[0:1, :] += jnp.sum(dqn * qhat, axis=0, keepdims=True)
        dql_ref[...] = _rms_bwd(dqn * qw, qhat, rq)
        kl = kl_ref[...]
        rk = _rstd(kl)
        khat = kl * rk
        kw = kw_ref[...]
        kvn = (khat * kw).astype(BF16)
        dkvn = jnp.zeros((tm, MLA_KVR), F32)
        dkr2 = jnp.zeros((tm, 128), F32)
        for h in range(MLA_HEADS):
            dkn = dkc_ref[h, :, 0:128].astype(BF16)
            dvh = dvv_ref[h].astype(BF16)
            gkv_ref[2 * h] += _dot_tn(kvn, dkn)
            gkv_ref[2 * h + 1] += _dot_tn(kvn, dvh)
            dkvn += _dot_nt(dkn, wukv_ref[2 * h]) + _dot_nt(dvh, wukv_ref[2 * h + 1])
            dkr2 += dkc_ref[h, :, 128:256]
        part_ref[1:2, 0:128] += jnp.sum(dkvn * khat, axis=0, keepdims=True)
        dkl_ref[...] = _rms_bwd(dkvn * kw, khat, rk)
        dkr = jnp.where(half == 0, dkr2 + pltpu.roll(dkr2, 64, 1), 0.0)
        dkr_ref[...] = dkr * cs[:, :128] + _swap_halves(dkr * sn[:, :128])

    const = lambda shape: pl.BlockSpec(shape, lambda i: tuple(0 for _ in shape))
    heads = lambda w: pl.BlockSpec((MLA_HEADS, tm, w), lambda i: (0, i, 0))
    return pl.pallas_call(
        body, name="mla_pre_bwd", grid=(S // tm,),
        in_specs=[pl.BlockSpec((tm, 256), lambda i: (i, 3)), pl.BlockSpec((tm, 128), lambda i: (i, 8)),
                  const((1, 256)), const((1, 128)), const((768, 256)), const((8, 128, 128)),
                  pl.BlockSpec((tm, 256), lambda i: (i, 0)), pl.BlockSpec((tm, 256), lambda i: (i, 0)),
                  heads(256), heads(256), heads(128)],
        out_specs=[pl.BlockSpec((tm, 256), lambda i: (i, 0)), pl.BlockSpec((tm, 128), lambda i: (i, 0)),
                   pl.BlockSpec((tm, 128), lambda i: (i, 0)), const((768, 256)), const((8, 128, 128)), const((8, 256))],
        out_shape=[jax.ShapeDtypeStruct((S, 256), F32), jax.ShapeDtypeStruct((S, 128), F32),
                   jax.ShapeDtypeStruct((S, 128), F32), jax.ShapeDtypeStruct((768, 256), F32),
                   jax.ShapeDtypeStruct((8, 128, 128), F32), jax.ShapeDtypeStruct((8, 256), F32)],
        compiler_params=_params(("arbitrary",)),
    )(proj, proj, qn_w, kvn_w, wuqT, wukv, cos, sin, dqc, dkc, dvv)


def _mix_out_fwd(x, oa, ob, w_o, vecs, *, tm=ROW_TILE):
    S = x.shape[0]
    tm = min(tm, S)

    def body(x_ref, oa_ref, ob_ref, w_ref, vec_ref, xo_ref, mo_ref):
        mo = _dot(oa_ref[...].astype(BF16), w_ref[0:512, :]) + _dot(ob_ref[...].astype(BF16), w_ref[512:1024, :])
        mo_ref[...] = mo
        xo_ref[...] = x_ref[...] + vec_ref[3:4, :] * mo

    row = pl.BlockSpec((tm, D), lambda i: (i, 0))
    half = pl.BlockSpec((tm, 512), lambda i: (i, 0))
    return pl.pallas_call(
        body, name="mix_out_fwd", grid=(S // tm,),
        in_specs=[row, half, half, pl.BlockSpec((D, D), lambda i: (0, 0)), pl.BlockSpec((8, D), lambda i: (0, 0))],
        out_specs=[row, row],
        out_shape=[jax.ShapeDtypeStruct((S, D), F32), jax.ShapeDtypeStruct((S, D), F32)],
        compiler_params=_params(("parallel",)),
    )(x, oa, ob, w_o, vecs)


def _mix_out_bwd(dxo, mo, oa, ob, w_o, vecs, *, tm=ROW_TILE):
    S = dxo.shape[0]
    tm = min(tm, S)

    def body(dx_ref, mo_ref, oa_ref, ob_ref, w_ref, vec_ref, doa_ref, dob_ref, gw_ref, part_ref):
        @pl.when(pl.program_id(0) == 0)
        def _():
            gw_ref[...] = jnp.zeros_like(gw_ref)
            part_ref[...] = jnp.zeros_like(part_ref)

        dx = dx_ref[...]
        part_ref[0:1, :] += jnp.sum(dx * mo_ref[...], axis=0, keepdims=True)
        dmo = (vec_ref[3:4, :] * dx).astype(BF16)
        doa_ref[...] = _dot_nt(dmo, w_ref[0:512, :])
        dob_ref[...] = _dot_nt(dmo, w_ref[512:1024, :])
        gw_ref[0:512, :] += _dot_tn(oa_ref[...].astype(BF16), dmo)
        gw_ref[512:1024, :] += _dot_tn(ob_ref[...].astype(BF16), dmo)

    row = pl.BlockSpec((tm, D), lambda i: (i, 0))
    half = pl.BlockSpec((tm, 512), lambda i: (i, 0))
    return pl.pallas_call(
        body, name="mix_out_bwd", grid=(S // tm,),
        in_specs=[row, row, half, half, pl.BlockSpec((D, D), lambda i: (0, 0)), pl.BlockSpec((8, D), lambda i: (0, 0))],
        out_specs=[half, half, pl.BlockSpec((D, D), lambda i: (0, 0)), pl.BlockSpec((8, D), lambda i: (0, 0))],
        out_shape=[jax.ShapeDtypeStruct((S, 512), F32), jax.ShapeDtypeStruct((S, 512), F32),
                   jax.ShapeDtypeStruct((D, D), F32), jax.ShapeDtypeStruct((8, D), F32)],
        compiler_params=_params(("arbitrary",)),
    )(dxo, mo, oa, ob, w_o, vecs)


def _mix_in_bwd(h, w_inT, dq, dk, dv, dql, dkl, dkr, *, tm=ROW_TILE):
    S = h.shape[0]
    tm = min(tm, S)
    offs = (0, 512, 640, 768, 1024, 1152)
    wid = (512, 128, 128, 256, 128, 128)

    def body(h_ref, w_ref, dq_ref, dk_ref, dv_ref, dql_ref, dkl_ref, dkr_ref, dh_ref, gw_ref):
        @pl.when(pl.program_id(0) == 0)
        def _():
            gw_ref[...] = jnp.zeros_like(gw_ref)

        hv = h_ref[...]
        dh = jnp.zeros((tm, D), F32)
        for ref, o, w in zip((dq_ref, dk_ref, dv_ref, dql_ref, dkl_ref, dkr_ref), offs, wid):
            w = min(w, D_IN - o)
            dpart = ref[...][:, :w].astype(BF16)
            dh += _dot(dpart, w_ref[o:o + w, :])
            gw_ref[o:o + w, :] += _dot_tn(dpart, hv)
        dh_ref[...] = dh

    row = pl.BlockSpec((tm, D), lambda i: (i, 0))
    part = lambda w: pl.BlockSpec((tm, w), lambda i: (i, 0))
    return pl.pallas_call(
        body, name="mix_in_bwd", grid=(S // tm,),
        in_specs=[row, pl.BlockSpec((D_IN_PAD, D), lambda i: (0, 0))] + [part(w) for w in wid],
        out_specs=[row, pl.BlockSpec((D_IN, D), lambda i: (0, 0))],
        out_shape=[jax.ShapeDtypeStruct((S, D), F32), jax.ShapeDtypeStruct((D_IN, D), F32)],
        compiler_params=_params(("arbitrary",)),
    )(h, w_inT, dq, dk, dv, dql, dkl, dkr)


def _vecs(norm_w, mod9, k):
    return jnp.concatenate([norm_w.reshape(1, D), mod9[3 * k:3 * k + 3], jnp.zeros((4, D), F32)], axis=0)


def _uq_group_rows(wuqT):
    per = MLA_NOPE + MLA_ROPE
    nope = [wuqT[per * h:per * h + MLA_NOPE] for h in range(MLA_HEADS)]
    rope = [wuqT[per * h + MLA_NOPE:per * (h + 1)] for h in range(MLA_HEADS)]
    return jnp.concatenate(nope + rope, axis=0)


def _uq_ungroup_rows(g):
    parts = []
    for h in range(MLA_HEADS):
        parts += [g[MLA_NOPE * h:MLA_NOPE * (h + 1)], g[512 + MLA_ROPE * h:512 + MLA_ROPE * (h + 1)]]
    return jnp.concatenate(parts, axis=0)


def _local_step(x, tgt, mod9, norms, sinks, rel_bias, q_norm, kv_norm, W, on_grads=None):
    if on_grads is None:
        on_grads = lambda group, grads, after, vecs: vecs
    S = x.shape[0]
    v1 = _vecs(norms["ffn1"], mod9, 0)
    v2 = _vecs(norms["mix"], mod9, 1)
    v3 = _vecs(norms["ffn2"], mod9, 2)
    bucket = jnp.asarray(_bucket_table())
    cos, sin = _rope_tables(S)
    if isinstance(W, dict):
        full, W = W, (lambda group, after, vecs: (full, vecs))

    W1, v1 = W("ffn1", [], v1)
    x1, h1, a1, b1, f1 = _ffn_fwd(x, v1, W1["g1T"], W1["u1T"], W1["d1"], name="ffn1_fwd", tm=256, tf=D_FF)
    W2, v2 = W("mixer", [x1], v2)
    w_inT = jnp.pad(W2["w_inT"], ((0, D_IN_PAD - D_IN), (0, 0))).astype(BF16)
    wuqT = _uq_group_rows(W2["w_uqT"])
    h2, proj = _mix_in_fwd(x1, v2, w_inT)
    bias = _bias_build(rel_bias, bucket)
    oa = _swa_fwd(proj, bias, sinks)
    qc, kc, vv = _mla_pre_fwd(proj, q_norm, kv_norm, wuqT, W2["w_ukv"], cos, sin)
    ob, lse = _mla_attn_fwd(qc, kc, vv)
    _, v2o = W("ffn2_on_its_way", [ob], v2)
    x2, mo = _mix_out_fwd(x1, oa, ob, W2["w_o"], v2o)
    W3, v3 = W("ffn2", [x2], v3)
    x3, h3, a3, b3, f3 = _ffn_fwd(x2, v3, W3["g3T"], W3["u3T"], W3["d3"], name="ffn2_fwd", tm=256, tf=D_FF)
    dx3, head_part, df3 = _head(x3, tgt, norms["final"], f3, v3)

    gg3, gu3, gd3, dh3 = _ffn_bwd_main(h3, df3, a3, b3, W3["g3T"], W3["u3T"], W3["d3"], name="ffn2_bwd", tm=2048, tf=256)
    ffn2 = {"g3T": gg3, "u3T": gu3, "d3": gd3}
    v3 = on_grads("ffn2", ffn2, [], v3)
    dx2, n3_part = _norm_bwd(dh3, x2, dx3, v3, name="ffn2_norm_bwd")
    v2 = on_grads("ffn2", None, [dx2], v2)
    doa, dob, g_wo, g2_part = _mix_out_bwd(dx2, mo, oa, ob, W2["w_o"], v2)
    dq, dk, dv, drb, dsk = _swa_bwd(proj, bias, sinks, oa, doa, bucket)
    dqc, dkc, dvv = _mla_attn_bwd(qc, kc, vv, ob, lse, dob)
    dql, dkl, dkr, g_uq, g_ukv, mla_part = _mla_pre_bwd(proj, q_norm, kv_norm, wuqT, W2["w_ukv"], cos, sin, dqc, dkc, dvv)
    dh2, g_win = _mix_in_bwd(h2, w_inT, dq, dk, dv, dql, dkl, dkr)
    mixer = {"w_inT": g_win, "w_uqT": _uq_ungroup_rows(g_uq).astype(BF16),
             "w_ukv": g_ukv.astype(BF16), "w_o": g_wo.astype(BF16)}
    v2 = on_grads("mixer", mixer, [], v2)
    dx1, n2_part, df1 = _norm_bwd(dh2, x1, dx2, v2, name="mix_norm_bwd", below=(f1, v1))
    started = on_grads("mixer", None, [dx1], jnp.zeros((1, 1), F32))
    gg1, gu1, gd1, dh1 = _ffn_bwd_main(h1, df1, a1, b1, W1["g1T"], W1["u1T"], W1["d1"], name="ffn1_bwd",
                                       after=[started], tm=2048, tf=256)
    ffn1 = {"g1T": gg1, "u1T": gu1, "d1": gd1}
    v1 = on_grads("ffn1", ffn1, [], v1)
    dx0, n1_part = _norm_bwd(dh1, x, dx1, v1, name="ffn1_norm_bwd")

    grads = {**ffn1, **ffn2, **mixer}
    return head_part[1, 0], dx0, grads, _pack_vec(n1_part, n2_part, n3_part, head_part, g2_part, mla_part, dsk, drb)


SMALL_LAYOUT = (("norm_ffn1", 1024), ("norm_mix", 1024), ("norm_ffn2", 1024), ("norm_final", 1024),
                ("q_norm", 256), ("kv_norm", 128), ("sinks", 128), ("rel_bias", 256))
N_SMALL = sum(n for _, n in SMALL_LAYOUT)
LOSS_SLOT = 4 * 1024 + 256 + 128 + SWA_HEADS
N_MODVEC = N_MOD * D
N_VEC = N_MODVEC + N_SMALL


def _pack_vec(n1, n2, n3, head, g2, mla, dsk, drb):
    def body(n1_ref, n2_ref, n3_ref, head_ref, g2_ref, mla_ref, dsk_ref, drb_ref, out_ref):
        rows = [n1_ref[1:2, :], n1_ref[2:3, :], n2_ref[3:4, :], n2_ref[1:2, :], n2_ref[2:3, :], g2_ref[0:1, :],
                n3_ref[1:2, :], n3_ref[2:3, :], head_ref[3:4, :],
                n1_ref[0:1, :], n2_ref[0:1, :], n3_ref[0:1, :], head_ref[0:1, :]]
        for i, row in enumerate(rows):
            out_ref[:, D * i:D * (i + 1)] = row
        off = D * len(rows)
        out_ref[:, off:off + 256] = mla_ref[0:1, :]
        out_ref[:, off + 256:off + 384] = mla_ref[1:2, 0:128]

        def diagonal(block):
            r = lax.broadcasted_iota(jnp.int32, block.shape, 0)
            lane = lax.broadcasted_iota(jnp.int32, block.shape, 1)
            return jnp.sum(jnp.where(r == lane, block, 0.0), axis=0, keepdims=True)

        lane = lax.broadcasted_iota(jnp.int32, (1, 128), 1)
        out_ref[:, off + 384:off + 512] = jnp.where(lane == SWA_HEADS, head_ref[1:2, 0:128], diagonal(dsk_ref[...]))
        out_ref[:, off + 512:off + 640] = diagonal(drb_ref[0:128, :])
        out_ref[:, off + 640:off + 768] = diagonal(drb_ref[128:256, :])

    vm = pl.BlockSpec(memory_space=pltpu.VMEM)
    return pl.pallas_call(body, name="pack_vec", in_specs=[vm] * 8, out_specs=vm,
                          out_shape=jax.ShapeDtypeStruct((1, N_VEC), F32))(n1, n2, n3, head, g2, mla, dsk, drb)


def _coords():
    return lax.axis_index("x"), lax.axis_index("y"), lax.axis_index("c")


def _flip(v, bit):
    return 1 - v if bit else v


def _peer(r):
    x, y, c = _coords()
    return (_flip(x, r & 4), _flip(y, r & 2), _flip(c, r & 1))


def _mod_fwd(c_tile, w_mod, b_mod3):
    W = w_mod.shape[1]

    def body(c_ref, w_ref, b_ref, mod_ref, ca_ref, call_ref, part_ref, send_sems, recv_sems):
        x, y, c = _coords()
        me = 4 * x + 2 * y + c
        call_ref[me] = c_ref[...]
        sends = []
        for r in range(1, N_DEV):
            cp = pltpu.make_async_remote_copy(c_ref, call_ref.at[me], send_sems.at[0, r], recv_sems.at[0, r],
                                              device_id=_peer(r), device_id_type=MESH)
            cp.start()
            sends.append(cp)
        for r in range(1, N_DEV):
            pltpu.make_async_remote_copy(c_ref, call_ref.at[me], send_sems.at[0, r], recv_sems.at[0, r],
                                         device_id=_peer(r), device_id_type=MESH).wait_recv()
        cv = call_ref[...].reshape(8 * N_DEV, D)
        ca = (cv * _sigmoid(cv)).astype(BF16)
        ca_ref[...] = ca
        part_ref[...] = _dot(ca, w_ref[...].astype(BF16)).reshape(N_DEV, 8, W)
        mod_ref[me] = part_ref[me] + b_ref[me]
        for r in range(1, N_DEV):
            cp = pltpu.make_async_remote_copy(part_ref.at[me ^ r], mod_ref.at[me], send_sems.at[1, r],
                                              recv_sems.at[1, r], device_id=_peer(r), device_id_type=MESH)
            cp.start()
            sends.append(cp)
        for r in range(1, N_DEV):
            pltpu.make_async_remote_copy(part_ref.at[me ^ r], mod_ref.at[me], send_sems.at[1, r],
                                         recv_sems.at[1, r], device_id=_peer(r), device_id_type=MESH).wait_recv()
            mod_ref[me ^ r] = mod_ref[me ^ r] + b_ref[me ^ r]
        for cp in sends:
            cp.wait_send()

    vm = pl.BlockSpec(memory_space=pltpu.VMEM)
    return pl.pallas_call(
        body, name="mod_fwd", in_specs=[vm, vm, vm], out_specs=[vm, vm],
        out_shape=[jax.ShapeDtypeStruct((N_DEV, 8, W), F32), jax.ShapeDtypeStruct((8 * N_DEV, D), BF16)],
        scratch_shapes=[pltpu.VMEM((N_DEV, 8, D), F32), pltpu.VMEM((N_DEV, 8, W), F32),
                        pltpu.SemaphoreType.DMA((2, N_DEV)), pltpu.SemaphoreType.DMA((2, N_DEV))],
        compiler_params=_params(),
    )(c_tile, w_mod, b_mod3)


def _mod_bwd(allvec, ca, me_idx):
    W = N_MODVEC // N_DEV

    def body(me_ref, all_ref, cols_ref, ca_ref, gw_ref, sum_ref):
        in_first_row = lax.broadcasted_iota(jnp.int32, (N_DEV, 8, W), 1) == 0
        dm = jnp.where(in_first_row, cols_ref[...], 0.0).reshape(8 * N_DEV, W)
        gw_ref[...] = _dot_tn(ca_ref[...], dm.astype(BF16))
        total = all_ref[0]
        for k in range(1, N_DEV):
            total = total + all_ref[k]
        sum_ref[...] = total

    return pl.pallas_call(
        body, name="mod_bwd",
        grid_spec=pltpu.PrefetchScalarGridSpec(
            num_scalar_prefetch=1, grid=(1,),
            in_specs=[pl.BlockSpec((N_DEV, 1, N_VEC), lambda i, me: (0, 0, 0)),
                      pl.BlockSpec((N_DEV, 1, W), lambda i, me: (0, 0, me[0])),
                      pl.BlockSpec((8 * N_DEV, D), lambda i, me: (0, 0))],
            out_specs=[pl.BlockSpec((D, W), lambda i, me: (0, 0)), pl.BlockSpec((1, N_VEC), lambda i, me: (0, 0))]),
        out_shape=[jax.ShapeDtypeStruct((D, W), F32), jax.ShapeDtypeStruct((1, N_VEC), F32)],
        compiler_params=_params(("arbitrary",)),
    )(me_idx, allvec, allvec, ca)


def _wgather(shards):
    n = len(shards)

    def body(*refs):
        ins, outs, token = refs[:n], refs[n:2 * n], refs[2 * n]
        send_sems, recv_sems, local_sems = refs[2 * n + 1:]
        token[...] = jnp.zeros_like(token)
        x, y, c = _coords()
        me = 4 * x + 2 * y + c
        sib = (x, y, 1 - c)
        chips = [(1 - x, y), (x, 1 - y), (1 - x, 1 - y)]

        def copy(k, slot, block, to, src=None):
            return pltpu.make_async_remote_copy(
                src_ref=outs[k].at[block] if src is None else src, dst_ref=outs[k].at[block],
                send_sem=send_sems.at[k, slot], recv_sem=recv_sems.at[k, slot], device_id=to, device_id_type=MESH)

        local = [pltpu.make_async_copy(ins[k], outs[k].at[me], local_sems.at[k]) for k in range(n)]
        for cp in local:
            cp.start()
        first = []
        for k in range(n):
            first.append(copy(k, 0, me, sib, src=ins[k]))
            for j, chip in enumerate(chips):
                first.append(copy(k, 1 + j, me, (*chip, c), src=ins[k]))
        for cp in first:
            cp.start()
        passed = []
        for j, (cx, cy) in enumerate(chips):
            for k in range(n):
                blk = 4 * cx + 2 * cy + c
                copy(k, 1 + j, blk, sib).wait_recv()
                cp = copy(k, 4 + j, blk, sib)
                cp.start()
                passed.append(cp)
        for k in range(n):
            copy(k, 0, 4 * x + 2 * y + (1 - c), sib).wait_recv()
            for j, (cx, cy) in enumerate(chips):
                copy(k, 4 + j, 4 * cx + 2 * cy + (1 - c), sib).wait_recv()
        for cp in first + passed:
            cp.wait_send()
        for cp in local:
            cp.wait()

    anyspec = pl.BlockSpec(memory_space=pl.ANY)
    return pl.pallas_call(
        body, name="wgather", in_specs=[anyspec] * n,
        out_specs=[anyspec] * n + [pl.BlockSpec(memory_space=pltpu.VMEM)],
        out_shape=[jax.ShapeDtypeStruct((N_DEV,) + s.shape, s.dtype) for s in shards]
        + [jax.ShapeDtypeStruct((8, 128), F32)],
        scratch_shapes=[pltpu.SemaphoreType.DMA((n, 7)), pltpu.SemaphoreType.DMA((n, 7)),
                        pltpu.SemaphoreType.DMA((n,))],
    )(*shards)


class _GatherCopies:
    def __init__(self, lands, send_sems, recv_sems, k0=0, batches=None):
        x, y, c = _coords()
        me = 4 * x + 2 * y + c
        sib = (x, y, 1 - c)
        chips = [(1 - x, y), (x, 1 - y), (1 - x, 1 - y)]

        def copy(k, slot, block, to):
            return pltpu.make_async_remote_copy(
                src_ref=lands[k].at[block], dst_ref=lands[k].at[block],
                send_sem=send_sems.at[7 * (k0 + k) + slot], recv_sem=recv_sems.at[7 * (k0 + k) + slot],
                device_id=to, device_id_type=MESH)

        n = len(lands)
        self.first = [copy(k, 0, me, sib) for k in range(n)]
        for batch in batches or [range(n)]:
            self.first += [copy(k, 1 + j, me, (cx, cy, c)) for j, (cx, cy) in enumerate(chips) for k in batch]
        self.landed = [copy(k, 1 + j, 4 * cx + 2 * cy + c, sib) for j, (cx, cy) in enumerate(chips) for k in range(n)]
        self.passed = [copy(k, 4 + j, 4 * cx + 2 * cy + c, sib) for j, (cx, cy) in enumerate(chips) for k in range(n)]
        self.from_sib = [copy(k, 0, 4 * x + 2 * y + (1 - c), sib) for k in range(n)]
        self.from_sib += [copy(k, 4 + j, 4 * cx + 2 * cy + (1 - c), sib) for j, (cx, cy) in enumerate(chips)
                          for k in range(n)]


def _gather_start(lands, *, name, batches=None):
    n = len(lands)

    def body(*refs):
        for cp in _GatherCopies(refs[:n], refs[n], refs[n + 1], batches=batches).first:
            cp.start()
        refs[-1][...] = jnp.zeros_like(refs[-1])

    out = pl.pallas_call(
        body, name=name,
        out_shape=(pltpu.SemaphoreType.DMA((7 * n,)), pltpu.SemaphoreType.DMA((7 * n,)),
                   *[pltpu.HBM(l.shape, l.dtype) for l in lands], jax.ShapeDtypeStruct((8, 128), F32)),
        in_specs=[HBM_SPEC] * n,
        out_specs=(SEM_SPEC, SEM_SPEC, *[HBM_SPEC] * n, pl.BlockSpec(memory_space=pltpu.VMEM)),
        input_output_aliases={i: 2 + i for i in range(n)},
        compiler_params=pltpu.CompilerParams(has_side_effects=DATAFLOW),
    )(*[_in_hbm(l) for l in lands])
    return out[0], out[1], list(out[2:2 + n]), out[-1]


def _gather_pass(send_sems, recv_sems, lands, after, *, name, stage, k0=0):
    n = len(lands)

    def body(*refs):
        cps = _GatherCopies(refs[:n], refs[n], refs[n + 1], k0)
        if stage == "landed":
            for cp in cps.landed:
                cp.wait_recv()
        else:
            for cp in cps.passed:
                cp.start()
        refs[-1][...] = jnp.zeros_like(refs[-1])

    out = pl.pallas_call(
        body, name=name,
        out_shape=(*[pltpu.HBM(l.shape, l.dtype) for l in lands], jax.ShapeDtypeStruct((8, 128), F32)),
        in_specs=[HBM_SPEC] * n + [SEM_SPEC, SEM_SPEC] + [pl.BlockSpec(memory_space=pl.ANY)] * len(after),
        out_specs=(*[HBM_SPEC] * n, pl.BlockSpec(memory_space=pltpu.VMEM)),
        input_output_aliases={i: i for i in range(n)},
        compiler_params=pltpu.CompilerParams(has_side_effects=DATAFLOW),
    )(*lands, send_sems, recv_sems, *after)
    return list(out[:n]), out[-1]


def _gather_end(send_sems, recv_sems, lands, after, *, name, k0=0):
    n = len(lands)

    def body(*refs):
        cps = _GatherCopies(refs[:n], refs[n], refs[n + 1], k0)
        for cp in cps.from_sib:
            cp.wait_recv()
        for cp in cps.first + cps.passed:
            cp.wait_send()

    out = pl.pallas_call(
        body, name=name,
        out_shape=[pltpu.HBM(l.shape, l.dtype) for l in lands],
        in_specs=[HBM_SPEC] * n + [SEM_SPEC, SEM_SPEC] + [pl.BlockSpec(memory_space=pl.ANY)] * len(after),
        out_specs=[HBM_SPEC] * n,
        input_output_aliases={i: i for i in range(n)},
        compiler_params=pltpu.CompilerParams(has_side_effects=DATAFLOW),
    )(*lands, send_sems, recv_sems, *after)
    return list(out)


def _d2d_copies(grads, lands, send_sems, recv_sems):
    x, y, c = _coords()
    return [pltpu.make_async_remote_copy(
        src_ref=grads[k].at[2 * q + (1 - c)], dst_ref=lands[k].at[q],
        send_sem=send_sems.at[4 * k + q], recv_sem=recv_sems.at[4 * k + q],
        device_id=(x, y, 1 - c), device_id_type=MESH) for k in range(len(grads)) for q in range(4)]


def _vec_copies(srcs, lands, send_sems, recv_sems):
    x, y, c = _coords()
    me = 4 * x + 2 * y + c
    return [pltpu.make_async_remote_copy(
        src_ref=lands[0].at[me], dst_ref=lands[0].at[me], send_sem=send_sems.at[r - 1], recv_sem=recv_sems.at[r - 1],
        device_id=_peer(r), device_id_type=MESH) for r in range(1, N_DEV)]


def _chipsum(gs, sibs, cidx, *, name):
    n = len(gs)

    def body(c_ref, *refs):
        for k in range(n):
            refs[2 * n + k][...] = (refs[k][...].astype(F32) + refs[n + k][...].astype(F32)).astype(refs[2 * n + k].dtype)

    mine = [pl.BlockSpec((1,) + g.shape[1:], lambda q, c_ref: (2 * q + c_ref[0], 0, 0)) for g in gs]
    other = [pl.BlockSpec((1,) + g.shape[1:], lambda q, c_ref: (q, 0, 0)) for g in gs]
    return pl.pallas_call(
        body, name=name,
        grid_spec=pltpu.PrefetchScalarGridSpec(num_scalar_prefetch=1, grid=(4,), in_specs=mine + other, out_specs=other),
        out_shape=[jax.ShapeDtypeStruct((4,) + g.shape[1:], g.dtype) for g in gs],
        compiler_params=_params(("arbitrary",)),
    )(cidx, *gs, *sibs)


HBM_SPEC = pl.BlockSpec(memory_space=pltpu.HBM)
SEM_SPEC = pl.BlockSpec(memory_space=pltpu.SEMAPHORE)
DATAFLOW = pltpu.SideEffectType.DATAFLOW_SIDE_EFFECTING


def _in_hbm(a):
    return pltpu.with_memory_space_constraint(a, pltpu.HBM)


def _ici_copies(sums, lands, send_sems, recv_sems, k0=0):
    x, y, c = _coords()
    chips = [(1 - x, y), (x, 1 - y), (1 - x, 1 - y)]
    cps = []
    for k in range(len(sums)):
        for j, (cx, cy) in enumerate(chips):
            cps.append(pltpu.make_async_remote_copy(
                src_ref=sums[k].at[2 * cx + cy], dst_ref=lands[k].at[j],
                send_sem=send_sems.at[3 * (k0 + k) + j], recv_sem=recv_sems.at[3 * (k0 + k) + j],
                device_id=(cx, cy, c), device_id_type=MESH))
    return cps


def _split_start(copies, srcs, lands, n_sems, after, *, name):
    ns, nl = len(srcs), len(lands)

    def body(*refs):
        for cp in copies(refs[:ns], refs[ns:ns + nl], refs[ns + nl + len(after)], refs[ns + nl + len(after) + 1]):
            cp.start()
        refs[-1][...] = jnp.zeros_like(refs[-1])

    bufs = [_in_hbm(a) for a in list(srcs) + list(lands)]
    out = pl.pallas_call(
        body, name=name,
        out_shape=(pltpu.SemaphoreType.DMA((n_sems,)), pltpu.SemaphoreType.DMA((n_sems,)),
                   *[pltpu.HBM(a.shape, a.dtype) for a in bufs], jax.ShapeDtypeStruct((8, 128), F32)),
        in_specs=[HBM_SPEC] * len(bufs) + [pl.BlockSpec(memory_space=pl.ANY)] * len(after),
        out_specs=(SEM_SPEC, SEM_SPEC, *[HBM_SPEC] * len(bufs), pl.BlockSpec(memory_space=pltpu.VMEM)),
        input_output_aliases={i: 2 + i for i in range(len(bufs))},
        compiler_params=pltpu.CompilerParams(has_side_effects=DATAFLOW),
    )(*bufs, *after)
    return out[0], out[1], list(out[2:2 + ns]), list(out[2 + ns:2 + ns + nl]), out[-1]


def _split_wait(copies, send_sems, recv_sems, srcs, lands, after, *, name):
    ns, nl = len(srcs), len(lands)

    def body(*refs):
        for cp in copies(refs[:ns], refs[ns:ns + nl], refs[ns + nl], refs[ns + nl + 1]):
            cp.wait_send()
            cp.wait_recv()

    out = pl.pallas_call(
        body, name=name,
        out_shape=[pltpu.HBM(a.shape, a.dtype) for a in list(srcs) + list(lands)],
        in_specs=[HBM_SPEC] * (ns + nl) + [SEM_SPEC, SEM_SPEC] + [pl.BlockSpec(memory_space=pl.ANY)] * len(after),
        out_specs=[HBM_SPEC] * (ns + nl),
        input_output_aliases={i: i for i in range(ns + nl)},
        compiler_params=pltpu.CompilerParams(has_side_effects=DATAFLOW),
    )(*srcs, *lands, send_sems, recv_sems, *after)
    return list(out[:ns]), list(out[ns:])


ADAM_C1 = 1.0 / (1.0 - ADAM_B1 ** ADAM_STEP)
ADAM_C2 = 1.0 / (1.0 - ADAM_B2 ** ADAM_STEP)


def _adam_math(w, g, m, v):
    m2 = ADAM_B1 * m + (1.0 - ADAM_B1) * g
    v2 = ADAM_B2 * v + (1.0 - ADAM_B2) * (g * g)
    return -ADAM_LR * ((m2 * ADAM_C1) / (jnp.sqrt(v2 * ADAM_C2) + ADAM_EPS) + ADAM_WD * w), m2, v2


def _adamw(w, g, m, v, *, name):
    R, C = w.shape
    tr = R if R <= 512 else 256

    def body(w_ref, g_ref, m_ref, v_ref, d_ref, nm_ref, nv_ref):
        d_ref[...], nm_ref[...], nv_ref[...] = _adam_math(w_ref[...], g_ref[...], m_ref[...], v_ref[...])

    blk = pl.BlockSpec((tr, C), lambda i: (i, 0))
    return pl.pallas_call(
        body, name=name, grid=(R // tr,), in_specs=[blk] * 4, out_specs=[blk] * 3,
        out_shape=[jax.ShapeDtypeStruct((R, C), F32)] * 3,
        compiler_params=_params(("parallel",)),
    )(w, g, m, v)


def _adamw_rs(wmv, cs, rcv, qidx, *, name):
    n = len(wmv)
    r, cc = wmv[0][0].shape
    tr = r // 2 if r % 32 == 0 and r > 128 else r

    def body(q_ref, *refs):
        ins, outs = refs[:5 * n], refs[5 * n:]
        for k in range(n):
            w_ref, m_ref, v_ref, c_ref, r_ref = ins[5 * k:5 * k + 5]
            g_ref, d_ref, nm_ref, nv_ref = outs[4 * k:4 * k + 4]
            g = ((c_ref[0].astype(F32) + r_ref[0].astype(F32)) + r_ref[1].astype(F32)) + r_ref[2].astype(F32)
            g_ref[...] = g
            d_ref[...], nm_ref[...], nv_ref[...] = _adam_math(w_ref[...], g, m_ref[...], v_ref[...])

    blk = pl.BlockSpec((tr, cc), lambda i, q_ref: (i, 0))
    one = [blk, blk, blk, pl.BlockSpec((1, tr, cc), lambda i, q_ref: (q_ref[0], i, 0)),
           pl.BlockSpec((3, tr, cc), lambda i, q_ref: (0, i, 0))]
    out = pl.pallas_call(
        body, name=name,
        grid_spec=pltpu.PrefetchScalarGridSpec(num_scalar_prefetch=1, grid=(r // tr,), in_specs=one * n,
                                               out_specs=[blk] * (4 * n)),
        out_shape=[jax.ShapeDtypeStruct((r, cc), F32)] * (4 * n),
        compiler_params=_params(("arbitrary",)),
    )(qidx, *[a for (w, m, v), c, rc in zip(wmv, cs, rcv) for a in (w, m, v, c, rc)])
    return [tuple(out[4 * k:4 * k + 4]) for k in range(n)]


SMALL_PARAMS = ("norm_ffn1", "norm_mix", "norm_ffn2", "norm_final", "q_norm", "kv_norm", "sinks", "rel_bias", "b_mod")


def _adamw_small(gvec, wmv):
    widths = [wmv[3 * i].shape[1] for i in range(len(SMALL_PARAMS))]

    def body(*refs):
        g_all = refs[0]
        ins = refs[1:1 + 3 * len(SMALL_PARAMS)]
        outs = refs[1 + 3 * len(SMALL_PARAMS):]
        off = N_MODVEC
        for i, name in enumerate(SMALL_PARAMS):
            g_ref, d_ref, nm_ref, nv_ref = outs[4 * i:4 * i + 4]
            w_ref, m_ref, v_ref = ins[3 * i:3 * i + 3]
            start = 0 if name == "b_mod" else off
            g = g_all[:, start:start + widths[i]]
            g_ref[...] = g
            d_ref[...], nm_ref[...], nv_ref[...] = _adam_math(w_ref[...], g, m_ref[...], v_ref[...])
            if name != "b_mod":
                off += dict(SMALL_LAYOUT)[name]

    vm = pl.BlockSpec(memory_space=pltpu.VMEM)
    n_out = 4 * len(SMALL_PARAMS)
    out = pl.pallas_call(
        body, name="adamw_small", in_specs=[vm] * (1 + len(wmv)), out_specs=[vm] * n_out,
        out_shape=[jax.ShapeDtypeStruct((1, widths[i // 4]), F32) for i in range(n_out)],
        compiler_params=_params(),
    )(gvec, *wmv)
    return {name: out[4 * i:4 * i + 4] for i, name in enumerate(SMALL_PARAMS)}


TRANSPOSED = ("g1T", "u1T", "g3T", "u3T", "w_inT", "w_uqT")


def kernel(x, c, w_mod, b_mod, norm_ffn1, ffn1_gate, ffn1_up, ffn1_down, norm_mix, w_in, q_norm, kv_norm, w_uq, w_ukv, sinks, w_o, norm_ffn2, ffn2_gate, ffn2_up, ffn2_down, rel_bias, norm_final, loss_target, m_w_mod, m_b_mod, m_norm_ffn1, m_ffn1_gate, m_ffn1_up, m_ffn1_down, m_norm_mix, m_w_in, m_q_norm, m_kv_norm, m_w_uq, m_w_ukv, m_sinks, m_w_o, m_norm_ffn2, m_ffn2_gate, m_ffn2_up, m_ffn2_down, m_rel_bias, m_norm_final, v_w_mod, v_b_mod, v_norm_ffn1, v_ffn1_gate, v_ffn1_up, v_ffn1_down, v_norm_mix, v_w_in, v_q_norm, v_kv_norm, v_w_uq, v_w_ukv, v_sinks, v_w_o, v_norm_ffn2, v_ffn2_gate, v_ffn2_up, v_ffn2_down, v_rel_bias, v_norm_final):
    mx, my, mc = _coords()
    cidx = jnp.reshape(mc, (1,)).astype(jnp.int32)
    qidx = jnp.reshape(2 * mx + my, (1,)).astype(jnp.int32)
    WM = w_mod.shape[2]

    c_tile = jnp.pad(c, ((0, 7), (0, 0)))
    b_mod3 = jnp.pad(b_mod.reshape(N_DEV, 1, WM), ((0, 0), (0, 7), (0, 0)))
    mod3, ca = _mod_fwd(c_tile, w_mod[0], b_mod3)
    mod9 = mod3[:, 0, :].reshape(N_MOD, D)

    shards = {"g1T": ffn1_gate[0].T.astype(BF16), "u1T": ffn1_up[0].T.astype(BF16), "d1": ffn1_down[0].astype(BF16),
              "g3T": ffn2_gate[0].T.astype(BF16), "u3T": ffn2_up[0].T.astype(BF16), "d3": ffn2_down[0].astype(BF16),
              "w_inT": w_in[0].T, "w_uqT": w_uq[0].T.astype(BF16), "w_ukv": w_ukv[0].astype(BF16),
              "w_o": w_o[0].astype(BF16)}
    me = 4 * mx + 2 * my + mc
    groups = {"ffn1": ("g1T", "u1T", "d1"), "mixer": ("w_inT", "w_uqT", "w_ukv", "w_o"), "ffn2": ("g3T", "u3T", "d3")}
    arriving = {}

    def as_weights(group, gathered):
        return {k: g if k == "w_ukv" else g.reshape(N_DEV * g.shape[1], g.shape[2])
                for k, g in zip(groups[group], gathered)}

    later = groups["mixer"] + groups["ffn2"]
    place = {"mixer": 0, "ffn2": len(groups["mixer"])}

    def start_gather(token):
        lands = []
        for k in later:
            sh = shards[k] + token[0, 0].astype(shards[k].dtype)
            lands.append(lax.dynamic_update_slice(lax.empty((N_DEV,) + sh.shape, sh.dtype), sh[None], (me, 0, 0)))
        batches = [range(k0, k0 + len(groups[group])) for group, k0 in place.items()]
        send, recv, lands, started = _gather_start(lands, name="gather_start", batches=batches)
        for group, k0 in place.items():
            arriving[group] = (send, recv, lands[k0:k0 + len(groups[group])])
        return started

    def fetch(group, after, vecs):
        if group == "ffn1":
            *gathered, token = _wgather([shards[k] + ca[1, 0].astype(shards[k].dtype) for k in groups["ffn1"]])
            return as_weights("ffn1", gathered), vecs + start_gather(token)[0:1, 0:1]

        def pass_on(group, after):
            send, recv, lands = arriving[group]
            lands, token = _gather_pass(send, recv, lands, after, name="gather_landed_" + group, stage="landed",
                                        k0=place[group])
            lands, token = _gather_pass(send, recv, lands, [token], name="gather_onward_" + group, stage="onward",
                                        k0=place[group])
            arriving[group] = (send, recv, lands)
            return token

        if group == "ffn2_on_its_way":
            return None, vecs + pass_on("ffn2", after)[0:1, 0:1]
        if group == "mixer":
            after = [pass_on("mixer", after)]
        send, recv, lands = arriving[group]
        return as_weights(group, _gather_end(send, recv, lands, after, name="gather_end_" + group,
                                             k0=place[group])), vecs

    norms ={"ffn1": norm_ffn1, "mix": norm_mix, "ffn2": norm_ffn2, "final": norm_final.reshape(1, D)}
    in_flight = {}

    def on_grads(group, g, after, vecs, before_ici=()):
        if g is not None:
            names = list(g)
            by_dest = [g[k] if k == "w_ukv" else g[k].reshape((N_DEV, g[k].shape[0] // N_DEV) + g[k].shape[1:])
                       for k in names]
            lands = [lax.empty((4,) + a.shape[1:], a.dtype) for a in by_dest]
            send, recv, by_dest, lands, token = _split_start(_d2d_copies, by_dest, lands, 4 * len(names), after,
                                                             name="rs_d2d_start_" + group)
            in_flight[group] = (names, send, recv, by_dest, lands)
            return vecs + token[0:1, 0:1]
        names, send, recv, by_dest, lands = in_flight[group]
        by_dest, from_sib = _split_wait(_d2d_copies, send, recv, by_dest, lands, after, name="rs_d2d_wait_" + group)
        sums = _chipsum(by_dest, from_sib, cidx, name="chipsum_" + group)
        lands = [lax.empty((3,) + s.shape[1:], s.dtype) for s in sums]
        send, recv, sums, lands, token = _split_start(_ici_copies, sums, lands, 3 * len(names), list(before_ici),
                                                      name="rs_ici_start_" + group)
        in_flight[group] = (names, send, recv, sums, lands, token)
        return vecs + token[0:1, 0:1]

    _, grad_x, _, vec = _local_step(
        x[0], loss_target[0], mod9, norms, sinks, rel_bias, q_norm, kv_norm, fetch, on_grads=on_grads)

    vec = vec.reshape(1, 1, N_VEC)
    allvec = lax.dynamic_update_slice(lax.empty((N_DEV, 1, N_VEC), F32), vec, (me, 0, 0))
    vsend, vrecv, _, (allvec,), vec_started = _split_start(_vec_copies, [], [allvec], N_DEV - 1, [], name="vec_start")
    on_grads("ffn1", None, [grad_x], jnp.zeros((1, 1), F32), before_ici=[vec_started])

    owners = {"g1T": ("ffn1_gate", ffn1_gate, m_ffn1_gate, v_ffn1_gate), "u1T": ("ffn1_up", ffn1_up, m_ffn1_up, v_ffn1_up),
              "d1": ("ffn1_down", ffn1_down, m_ffn1_down, v_ffn1_down),
              "g3T": ("ffn2_gate", ffn2_gate, m_ffn2_gate, v_ffn2_gate), "u3T": ("ffn2_up", ffn2_up, m_ffn2_up, v_ffn2_up),
              "d3": ("ffn2_down", ffn2_down, m_ffn2_down, v_ffn2_down),
              "w_inT": ("w_in", w_in, m_w_in, v_w_in), "w_uqT": ("w_uq", w_uq, m_w_uq, v_w_uq),
              "w_ukv": ("w_ukv", w_ukv, m_w_ukv, v_w_ukv), "w_o": ("w_o", w_o, m_w_o, v_w_o)}
    res, done = {}, []

    def finish(group, after, one_by_one=False):
        names, send, recv, sums, lands, _ = in_flight[group]
        there = lambda k, a: a[0].T if k in TRANSPOSED else a[0]
        back = lambda k, a: a.T[None] if k in TRANSPOSED else a[None]
        wmv = [tuple(there(k, a) for a in owners[k][1:]) for k in names]
        if one_by_one:
            outs = []
            for i, k in enumerate(names):
                (cs,), (rc,) = _split_wait(functools.partial(_ici_copies, k0=i), send, recv, [sums[i]], [lands[i]],
                                           after, name="rs_ici_wait_" + k)
                outs.append(_adamw_rs([wmv[i]], [cs], [rc], qidx, name="adamw_" + owners[k][0])[0])
                after = [outs[-1][3]]
        else:
            sums, lands = _split_wait(_ici_copies, send, recv, sums, lands, after, name="rs_ici_wait_" + group)
            if len({w.shape for w, _, _ in wmv}) == 1:
                outs = _adamw_rs(wmv, sums, lands, qidx, name="adamw_" + group)
            else:
                outs = [_adamw_rs([t], [cs], [rc], qidx, name="adamw_" + owners[k][0])[0]
                        for k, t, cs, rc in zip(names, wmv, sums, lands)]
        for k, out in zip(names, outs):
            done.append(out[3])
            res[owners[k][0]] = tuple(back(k, a) for a in out)

    ffn1_started = in_flight["ffn1"][5]
    finish("ffn2", [ffn1_started])
    finish("mixer", [ffn1_started])

    _, (allvec,) = _split_wait(_vec_copies, vsend, vrecv, [], [allvec], [ffn1_started], name="vec_wait")
    g_wmod, gvec = _mod_bwd(allvec, ca, jnp.reshape(me, (1,)).astype(jnp.int32))
    loss = gvec[0, N_MODVEC + LOSS_SLOT]
    res["w_mod"] = tuple(a[None] for a in (g_wmod,) + tuple(_adamw(w_mod[0], g_wmod, m_w_mod[0], v_w_mod[0],
                                                                    name="adamw_w_mod")))
    small_in = {"norm_ffn1": (norm_ffn1, m_norm_ffn1, v_norm_ffn1), "norm_mix": (norm_mix, m_norm_mix, v_norm_mix),
                "norm_ffn2": (norm_ffn2, m_norm_ffn2, v_norm_ffn2), "norm_final": (norm_final, m_norm_final, v_norm_final),
                "q_norm": (q_norm, m_q_norm, v_q_norm), "kv_norm": (kv_norm, m_kv_norm, v_kv_norm),
                "sinks": (sinks, m_sinks, v_sinks), "rel_bias": (rel_bias, m_rel_bias, v_rel_bias),
                "b_mod": (b_mod, m_b_mod, v_b_mod)}
    small_out = _adamw_small(gvec, [a.reshape(1, -1) for k in SMALL_PARAMS for a in small_in[k]])
    for k in SMALL_PARAMS:
        res[k] = tuple(a.reshape(small_in[k][0].shape) for a in small_out[k])

    finish("ffn1", done + [res["w_mod"][3], small_out["b_mod"][3]], one_by_one=True)

    order = ("w_mod", "b_mod", "norm_ffn1", "ffn1_gate", "ffn1_up", "ffn1_down", "norm_mix", "w_in", "q_norm",
             "kv_norm", "w_uq", "w_ukv", "sinks", "w_o", "norm_ffn2", "ffn2_gate", "ffn2_up", "ffn2_down",
             "rel_bias", "norm_final")
    return (loss, grad_x[None]) + tuple(res[nm][kind] for kind in range(4) for nm in order)
```

```python
import functools
import math

import numpy as np
import jax
import jax.numpy as jnp
from jax import lax
from jax.experimental import pallas as pl
from jax.experimental.pallas import tpu as pltpu

F32 = jnp.float32
BF16 = jnp.bfloat16
MESH = pl.DeviceIdType.MESH

N_DEV = 8
D = 1024
D_FF = 2816
EPS = 1e-6
N_MOD = 9
SWA_HEADS = 8
SWA_DH = 64
WINDOW = 128
MLA_HEADS = 4
MLA_NOPE = 128
MLA_ROPE = 64
MLA_V = 128
MLA_QR = 256
MLA_KVR = 128
ROPE_THETA = 10000.0
NUM_BUCKETS = 32
D_IN = 1216
D_IN_PAD = 1280
SWA_SCALE = SWA_DH ** -0.5
MLA_SCALE = (MLA_NOPE + MLA_ROPE) ** -0.5

ADAM_LR = 0.001
ADAM_B1 = 0.9
ADAM_B2 = 0.999
ADAM_EPS = 1e-08
ADAM_WD = 0.01
ADAM_STEP = 10

V7X_VMEM_LIMIT = 56 * 1024 * 1024
ROW_TILE = 512

NT_DIMS = (((1,), (1,)), ((), ()))
TN_DIMS = (((0,), (0,)), ((), ()))


def _dot(a, b):
    return jnp.dot(a, b, preferred_element_type=F32)


def _dot_nt(a, b):
    return lax.dot_general(a, b, NT_DIMS, preferred_element_type=F32)


def _dot_tn(a, b):
    return lax.dot_general(a, b, TN_DIMS, preferred_element_type=F32)


def _params(sem=None):
    return pltpu.CompilerParams(dimension_semantics=sem, vmem_limit_bytes=V7X_VMEM_LIMIT)


def _rstd(x):
    return lax.rsqrt(jnp.mean(x * x, axis=-1, keepdims=True) + EPS)


def _rms_bwd(dy, xhat, r):
    return r * (dy - xhat * jnp.mean(dy * xhat, axis=-1, keepdims=True))


def _sigmoid(a):
    return 1.0 / (1.0 + jnp.exp(-a))


def _ffn_fwd(x, vecs, wgT, wuT, wd, *, name, tm=512, tf=256):
    S, F = x.shape[0], wd.shape[0]
    tm = min(tm, S)
    ni, nj = S // tm, F // tf

    def body(x_ref, vec_ref, wg_ref, wu_ref, wd_ref, xo_ref, h_ref, a_ref, b_ref, f_ref, acc_ref):
        j = pl.program_id(1)

        @pl.when(j == 0)
        def _():
            xv = x_ref[...]
            hn = xv * _rstd(xv) * vec_ref[0:1, :]
            h_ref[...] = (hn * (1.0 + vec_ref[2:3, :]) + vec_ref[1:2, :]).astype(BF16)

        h = h_ref[...]
        a = _dot_nt(h, wg_ref[...])
        b = _dot_nt(h, wu_ref[...])
        a_ref[...] = a.astype(BF16)
        b_ref[...] = b.astype(BF16)
        part = _dot((a * _sigmoid(a) * b).astype(BF16), wd_ref[...])

        def finish(f):
            f_ref[...] = f
            xo_ref[...] = x_ref[...] + (0.5 * vec_ref[3:4, :]) * f

        if nj == 1:
            finish(part)
        else:
            @pl.when(j == 0)
            def _():
                acc_ref[...] = part

            @pl.when((j > 0) & (j < nj - 1))
            def _():
                acc_ref[...] += part

            @pl.when(j == nj - 1)
            def _():
                finish(acc_ref[...] + part)

    row = pl.BlockSpec((tm, D), lambda i, j: (i, 0))
    wspec = pl.BlockSpec((tf, D), lambda i, j: (j, 0), pipeline_mode=pl.Buffered(1) if nj == 1 else None)
    act = pl.BlockSpec((tm, tf), lambda i, j: (i, j))
    return pl.pallas_call(
        body, name=name, grid=(ni, nj),
        in_specs=[row, pl.BlockSpec((8, D), lambda i, j: (0, 0)), wspec, wspec, wspec],
        out_specs=[row, row, act, act, row],
        out_shape=[jax.ShapeDtypeStruct((S, D), F32), jax.ShapeDtypeStruct((S, D), BF16),
                   jax.ShapeDtypeStruct((S, F), BF16), jax.ShapeDtypeStruct((S, F), BF16),
                   jax.ShapeDtypeStruct((S, D), F32)],
        scratch_shapes=[pltpu.VMEM((tm, D) if nj > 1 else (8, 128), F32)],
        compiler_params=_params(("parallel", "arbitrary")),
    )(x, vecs, wgT, wuT, wd)


def _ffn_bwd_main(h, df, a, b, wgT, wuT, wd, *, name, after=(), tm=512, tf=256):
    S = h.shape[0]
    tm = min(tm, S)
    ni, nj = S // tm, D_FF // tf

    def body(h_hbm, df_hbm, a_ref, b_ref, wg_ref, wu_ref, wd_ref, *rest):
        gg_ref, gu_ref, gd_ref, dh_hbm, h_v, df_v, dh_v, gg_acc, gu_acc, gd_acc, sem = rest[len(after):]
        j = pl.program_id(0)
        i = pl.program_id(1)

        @pl.when((j == 0) & (i == 0))
        def _():
            c1 = pltpu.make_async_copy(h_hbm, h_v, sem.at[0])
            c2 = pltpu.make_async_copy(df_hbm, df_v, sem.at[1])
            c1.start()
            c2.start()
            c1.wait()
            c2.wait()

        @pl.when(i == 0)
        def _():
            gg_acc[...] = jnp.zeros_like(gg_acc)
            gu_acc[...] = jnp.zeros_like(gu_acc)
            gd_acc[...] = jnp.zeros_like(gd_acc)

        rows = pl.ds(pl.multiple_of(i * tm, tm), tm)
        hi = h_v[rows, :]
        dfi = df_v[rows, :]
        av = a_ref[...].astype(F32)
        bv = b_ref[...].astype(F32)
        sg = _sigmoid(av)
        sa = av * sg
        hsw = (sa * bv).astype(BF16)
        dhsw = _dot_nt(dfi, wd_ref[...])
        da = (dhsw * bv * (sg * (1.0 + av * (1.0 - sg)))).astype(BF16)
        db = (dhsw * sa).astype(BF16)
        gd_acc[...] += _dot_tn(hsw, dfi)
        gg_acc[...] += _dot_tn(da, hi)
        gu_acc[...] += _dot_tn(db, hi)
        dh = _dot(da, wg_ref[...]) + _dot(db, wu_ref[...])

        @pl.when(j == 0)
        def _():
            dh_v[rows, :] = dh

        @pl.when(j > 0)
        def _():
            dh_v[rows, :] += dh

        @pl.when(i == ni - 1)
        def _():
            gg_ref[...] = gg_acc[...].astype(BF16)
            gu_ref[...] = gu_acc[...].astype(BF16)
            gd_ref[...] = gd_acc[...].astype(BF16)

        @pl.when((j == nj - 1) & (i == ni - 1))
        def _():
            c3 = pltpu.make_async_copy(dh_v, dh_hbm, sem.at[2])
            c3.start()
            c3.wait()

    anyspec = pl.BlockSpec(memory_space=pl.ANY)
    wspec = pl.BlockSpec((tf, D), lambda j, i: (j, 0))
    act = pl.BlockSpec((tm, tf), lambda j, i: (i, j))
    return pl.pallas_call(
        body, name=name, grid=(nj, ni),
        in_specs=[anyspec, anyspec, act, act, wspec, wspec, wspec] + [anyspec] * len(after),
        out_specs=[wspec, wspec, wspec, anyspec],
        out_shape=[jax.ShapeDtypeStruct((D_FF, D), BF16)] * 3 + [jax.ShapeDtypeStruct((S, D), F32)],
        scratch_shapes=[pltpu.VMEM((S, D), BF16), pltpu.VMEM((S, D), BF16), pltpu.VMEM((S, D), F32),
                        pltpu.VMEM((tf, D), F32), pltpu.VMEM((tf, D), F32), pltpu.VMEM((tf, D), F32),
                        pltpu.SemaphoreType.DMA((3,))],
        compiler_params=_params(("arbitrary", "arbitrary")),
    )(h, df, a, b, wgT, wuT, wd, *after)


def _ffn_out_bwd(dx, f, gate, df_ref, part_ref):
    df_ref[...] = ((0.5 * gate) * dx).astype(BF16)
    part_ref[3:4, :] += 0.5 * jnp.sum(dx * f, axis=0, keepdims=True)


def _norm_bwd(dh, x, dxo, vecs, *, name, below=None, tm=ROW_TILE):
    S = x.shape[0]
    tm = min(tm, S)

    def body(dh_ref, x_ref, dxo_ref, vec_ref, *rest):
        dx_ref, part_ref = rest[-2 if below is None else -3], rest[-1 if below is None else -2]

        @pl.when(pl.program_id(0) == 0)
        def _():
            part_ref[...] = jnp.zeros_like(part_ref)

        dh = dh_ref[...]
        xv = x_ref[...]
        r = _rstd(xv)
        xhat = xv * r
        w = vec_ref[0:1, :]
        xn = xhat * w
        dxn = dh * (1.0 + vec_ref[2:3, :])
        part_ref[0:1, :] += jnp.sum(dxn * xhat, axis=0, keepdims=True)
        part_ref[1:2, :] += jnp.sum(dh, axis=0, keepdims=True)
        part_ref[2:3, :] += jnp.sum(dh * xn, axis=0, keepdims=True)
        dx = dxo_ref[...] + _rms_bwd(dxn * w, xhat, r)
        dx_ref[...] = dx
        if below is not None:
            _ffn_out_bwd(dx, rest[0][...], rest[1][3:4, :], rest[-1], part_ref)

    row = pl.BlockSpec((tm, D), lambda i: (i, 0))
    vec = pl.BlockSpec((8, D), lambda i: (0, 0))
    extra = [] if below is None else [row, vec]
    return pl.pallas_call(
        body, name=name, grid=(S // tm,), in_specs=[row, row, row, vec] + extra,
        out_specs=[row, vec] + ([] if below is None else [row]),
        out_shape=[jax.ShapeDtypeStruct((S, D), F32), jax.ShapeDtypeStruct((8, D), F32)]
        + ([] if below is None else [jax.ShapeDtypeStruct((S, D), BF16)]),
        compiler_params=_params(("arbitrary",)),
    )(dh, x, dxo, vecs, *([] if below is None else below))


def _head(x, tgt, nf, f, vecs, *, tm=ROW_TILE):
    S = x.shape[0]
    tm = min(tm, S)

    def body(x_ref, t_ref, nf_ref, f_ref, vec_ref, dx_ref, part_ref, df_ref):
        @pl.when(pl.program_id(0) == 0)
        def _():
            part_ref[...] = jnp.zeros_like(part_ref)

        xv = x_ref[...]
        r = _rstd(xv)
        xhat = xv * r
        w = nf_ref[...]
        e = xhat * w - t_ref[...]
        dy = e * (1.0 / D)
        part_ref[0:1, :] += jnp.sum(dy * xhat, axis=0, keepdims=True)
        part_ref[1:2, :] += jnp.sum(e * e) * (0.5 / D)
        dx = _rms_bwd(dy * w, xhat, r)
        dx_ref[...] = dx
        _ffn_out_bwd(dx, f_ref[...], vec_ref[3:4, :], df_ref, part_ref)

    row = pl.BlockSpec((tm, D), lambda i: (i, 0))
    vec = pl.BlockSpec((8, D), lambda i: (0, 0))
    return pl.pallas_call(
        body, name="head", grid=(S // tm,),
        in_specs=[row, row, pl.BlockSpec((1, D), lambda i: (0, 0)), row, vec],
        out_specs=[row, vec, row],
        out_shape=[jax.ShapeDtypeStruct((S, D), F32), jax.ShapeDtypeStruct((8, D), F32),
                   jax.ShapeDtypeStruct((S, D), BF16)],
        compiler_params=_params(("arbitrary",)),
    )(x, tgt, nf, f, vecs)


def _mix_in_fwd(x, vecs, w_inT, *, tm=ROW_TILE):
    S = x.shape[0]
    tm = min(tm, S)

    def body(x_ref, vec_ref, w_ref, h_ref, p_ref):
        xv = x_ref[...]
        hn = xv * _rstd(xv) * vec_ref[0:1, :]
        h = (hn * (1.0 + vec_ref[2:3, :]) + vec_ref[1:2, :]).astype(BF16)
        h_ref[...] = h
        p_ref[...] = _dot_nt(h, w_ref[...])

    row = pl.BlockSpec((tm, D), lambda i: (i, 0))
    return pl.pallas_call(
        body, name="mix_in_fwd", grid=(S // tm,),
        in_specs=[row, pl.BlockSpec((8, D), lambda i: (0, 0)), pl.BlockSpec((D_IN_PAD, D), lambda i: (0, 0))],
        out_specs=[row, pl.BlockSpec((tm, D_IN_PAD), lambda i: (i, 0))],
        out_shape=[jax.ShapeDtypeStruct((S, D), BF16), jax.ShapeDtypeStruct((S, D_IN_PAD), F32)],
        compiler_params=_params(("parallel",)),
    )(x, vecs, w_inT)


def _bucket_table():
    qi = np.arange(WINDOW)[:, None]
    kj = np.arange(2 * WINDOW)[None, :]
    dist = qi + WINDOW - kj
    max_exact = NUM_BUCKETS // 2
    n = np.maximum(dist, 0)
    nf = np.maximum(n, 1).astype(np.float32)
    large = max_exact + (np.log(nf / np.float32(max_exact)) / np.float32(math.log(WINDOW / max_exact))
                         * np.float32(NUM_BUCKETS - max_exact)).astype(np.int32)
    large = np.minimum(large, NUM_BUCKETS - 1)
    return np.where(n < max_exact, n, large).astype(np.int32)


def _bias_build(rel_bias, bucket):
    def body(rb_ref, bk_ref, out_ref):
        bk = bk_ref[...]
        for h in range(SWA_HEADS):
            acc = jnp.zeros((WINDOW, 2 * WINDOW), F32)
            for b in range(NUM_BUCKETS):
                acc = jnp.where(bk == b, rb_ref[b, h], acc)
            out_ref[h] = acc

    return pl.pallas_call(
        body, name="bias_build",
        in_specs=[pl.BlockSpec(memory_space=pltpu.SMEM), pl.BlockSpec(memory_space=pltpu.VMEM)],
        out_specs=pl.BlockSpec(memory_space=pltpu.VMEM),
        out_shape=jax.ShapeDtypeStruct((SWA_HEADS, WINDOW, 2 * WINDOW), F32),
    )(rel_bias, bucket)


SWA_GROUP = 4
GROUP_ROWS = SWA_GROUP * WINDOW


SWA_SUB = 2


def _swa_valid(has_prev):
    row = lax.broadcasted_iota(jnp.int32, (GROUP_ROWS, 2 * WINDOW), 0) % WINDOW
    col = lax.broadcasted_iota(jnp.int32, (GROUP_ROWS, 2 * WINDOW), 1)
    dist = row + WINDOW - col
    return (dist >= 0) & (dist < WINDOW) & ((col >= WINDOW) | has_prev)


def _swa_keys(prev_ref, cur_ref, u):
    cur = cur_ref[...]
    before = prev_ref[...] if u == 0 else cur[WINDOW * (u - 1):WINDOW * u]
    return jnp.concatenate([before, cur[WINDOW * u:WINDOW * (u + 1)]], axis=0).astype(BF16)


def _stack_heads(x, g):
    return jnp.concatenate([x[:, 64 * h:64 * h + 64] for h in range(SWA_GROUP * g, SWA_GROUP * (g + 1))], axis=0)


def _unstack_heads(x4):
    return jnp.concatenate([x4[WINDOW * a:WINDOW * (a + 1)] for a in range(SWA_GROUP)], axis=1)


def _group_sinks(sink_ref, g):
    head = lax.broadcasted_iota(jnp.int32, (GROUP_ROWS, 1), 0) // WINDOW
    out = jnp.full((GROUP_ROWS, 1), sink_ref[0, SWA_GROUP * g], F32)
    for a in range(1, SWA_GROUP):
        out = jnp.where(head == a, sink_ref[0, SWA_GROUP * g + a], out)
    return out


def _swa_probs(qh, kk, bias_h, sink, valid):
    s = _dot_nt(qh, kk) * SWA_SCALE + bias_h
    s = jnp.where(valid, s, -jnp.inf)
    m = jnp.maximum(jnp.max(s, axis=-1, keepdims=True), sink)
    p = jnp.exp(s - m)
    ps = jnp.exp(sink - m)
    inv = 1.0 / (jnp.sum(p, axis=-1, keepdims=True) + ps)
    return p * inv, ps * inv


SWA_ROWS = SWA_SUB * WINDOW


def _swa_specs():
    prev = lambda n: jnp.maximum(SWA_SUB * n - 1, 0)
    return [pl.BlockSpec((SWA_ROWS, 512), lambda n: (n, 0)),
            pl.BlockSpec((SWA_ROWS, 128), lambda n: (n, 4)),
            pl.BlockSpec((WINDOW, 128), lambda n: (prev(n), 4)),
            pl.BlockSpec((SWA_ROWS, 128), lambda n: (n, 5)),
            pl.BlockSpec((WINDOW, 128), lambda n: (prev(n), 5)),
            pl.BlockSpec((SWA_HEADS, WINDOW, 2 * WINDOW), lambda n: (0, 0, 0)),
            pl.BlockSpec(memory_space=pltpu.SMEM)]


def _swa_fwd(proj, bias, sinks):
    S = proj.shape[0]

    def body(q_ref, kc_ref, kp_ref, vc_ref, vp_ref, bias_ref, sink_ref, o_ref):
        n = pl.program_id(0)
        for u in range(SWA_SUB):
            rows = slice(WINDOW * u, WINDOW * (u + 1))
            valid = _swa_valid(n > 0 if u == 0 else True)
            q = q_ref[rows, :].astype(BF16)
            kfull = _swa_keys(kp_ref, kc_ref, u)
            vfull = _swa_keys(vp_ref, vc_ref, u)
            for g in range(SWA_HEADS // SWA_GROUP):
                kk = kfull[:, 64 * g:64 * g + 64]
                vv = vfull[:, 64 * g:64 * g + 64]
                bias4 = bias_ref[SWA_GROUP * g:SWA_GROUP * (g + 1)].reshape(GROUP_ROWS, 2 * WINDOW)
                pk, _ = _swa_probs(_stack_heads(q, g), kk, bias4, _group_sinks(sink_ref, g), valid)
                o_ref[rows, 256 * g:256 * (g + 1)] = _unstack_heads(_dot(pk.astype(BF16), vv))

    return pl.pallas_call(
        body, name="swa_fwd", grid=(S // SWA_ROWS,),
        in_specs=_swa_specs(),
        out_specs=pl.BlockSpec((SWA_ROWS, 512), lambda n: (n, 0)),
        out_shape=jax.ShapeDtypeStruct((S, 512), F32),
        compiler_params=_params(("parallel",)),
    )(proj, proj, proj, proj, proj, bias, sinks)


def _swa_bwd(proj, bias, sinks, o, do, bucket):
    S = proj.shape[0]
    nb = S // SWA_ROWS

    def body(q_ref, kc_ref, kp_ref, vc_ref, vp_ref, bias_ref, sink_ref, o_ref, do_ref, bk_ref,
             dq_ref, dk_ref, dv_ref, drb_ref, dsk_ref, dbias_acc):
        n = pl.program_id(0)

        @pl.when(n == 0)
        def _():
            dk_ref[...] = jnp.zeros_like(dk_ref)
            dv_ref[...] = jnp.zeros_like(dv_ref)
            dsk_ref[...] = jnp.zeros_like(dsk_ref)
            dbias_acc[...] = jnp.zeros_like(dbias_acc)
            drb_ref[...] = jnp.zeros_like(drb_ref)

        for u in range(SWA_SUB):
            rows = slice(WINDOW * u, WINDOW * (u + 1))
            blk = SWA_SUB * n + u
            valid = _swa_valid(n > 0 if u == 0 else True)
            q = q_ref[rows, :].astype(BF16)
            dov = do_ref[rows, :]
            ov = o_ref[rows, :]
            kfull = _swa_keys(kp_ref, kc_ref, u)
            vfull = _swa_keys(vp_ref, vc_ref, u)
            prow = pl.ds(pl.multiple_of(jnp.maximum(blk - 1, 0) * WINDOW, WINDOW), WINDOW)
            crow = pl.ds(pl.multiple_of(blk * WINDOW, WINDOW), WINDOW)
            for g in range(SWA_HEADS // SWA_GROUP):
                heads = slice(SWA_GROUP * g, SWA_GROUP * (g + 1))
                kk = kfull[:, 64 * g:64 * g + 64]
                vv = vfull[:, 64 * g:64 * g + 64]
                q4 = _stack_heads(q, g)
                pk, psink = _swa_probs(q4, kk, bias_ref[heads].reshape(GROUP_ROWS, 2 * WINDOW),
                                       _group_sinks(sink_ref, g), valid)
                pkb = pk.astype(BF16)
                do4 = _stack_heads(dov, g)
                dob = do4.astype(BF16)
                dp = _dot_nt(dob, vv)
                delta = jnp.sum(do4 * _stack_heads(ov, g), axis=-1, keepdims=True)
                ds = pk * (dp - delta)
                dsink = -psink * delta
                for a in range(SWA_GROUP):
                    h = SWA_GROUP * g + a
                    part = jnp.sum(dsink[WINDOW * a:WINDOW * (a + 1)], keepdims=True)
                    dsk_ref[h:h + 1, :] += jnp.broadcast_to(part, (1, 128))
                dbias_acc[heads] += ds.reshape(SWA_GROUP, WINDOW, 2 * WINDOW)
                dsb = (ds * SWA_SCALE).astype(BF16)
                dq_ref[rows, 256 * g:256 * (g + 1)] = _unstack_heads(_dot(dsb, kk))
                dkk = _dot_tn(dsb, q4)
                dvv = _dot_tn(pkb, dob)
                dk_ref[prow, 64 * g:64 * g + 64] += dkk[:WINDOW]
                dk_ref[crow, 64 * g:64 * g + 64] += dkk[WINDOW:]
                dv_ref[prow, 64 * g:64 * g + 64] += dvv[:WINDOW]
                dv_ref[crow, 64 * g:64 * g + 64] += dvv[WINDOW:]

        @pl.when(n == nb - 1)
        def _():
            bk = bk_ref[...]
            for h in range(SWA_HEADS):
                dbh = dbias_acc[h]
                for b in range(NUM_BUCKETS):
                    val = jnp.sum(jnp.where(bk == b, dbh, 0.0), keepdims=True)
                    drb_ref[b * 8 + h:b * 8 + h + 1, :] = jnp.broadcast_to(val, (1, 128))

    full = lambda shape: pl.BlockSpec(shape, lambda n: tuple(0 for _ in shape))
    return pl.pallas_call(
        body, name="swa_bwd", grid=(nb,),
        in_specs=_swa_specs() + [pl.BlockSpec((SWA_ROWS, 512), lambda n: (n, 0)),
                                 pl.BlockSpec((SWA_ROWS, 512), lambda n: (n, 0)), full((WINDOW, 2 * WINDOW))],
        out_specs=[pl.BlockSpec((SWA_ROWS, 512), lambda n: (n, 0)), full((S, 128)), full((S, 128)),
                   full((NUM_BUCKETS * 8, 128)), full((8, 128))],
        out_shape=[jax.ShapeDtypeStruct((S, 512), F32), jax.ShapeDtypeStruct((S, 128), F32),
                   jax.ShapeDtypeStruct((S, 128), F32), jax.ShapeDtypeStruct((NUM_BUCKETS * 8, 128), F32),
                   jax.ShapeDtypeStruct((8, 128), F32)],
        scratch_shapes=[pltpu.VMEM((SWA_HEADS, WINDOW, 2 * WINDOW), F32)],
        compiler_params=_params(("arbitrary",)),
    )(proj, proj, proj, proj, proj, bias, sinks, o, do, bucket)


def _rope_tables(S):
    inv = np.float32(ROPE_THETA) ** (-np.arange(0, MLA_ROPE, 2, dtype=np.float32) / np.float32(MLA_ROPE))
    ang = np.arange(S, dtype=np.float32)[:, None] * inv[None, :]
    cos, sin = np.cos(ang), np.sin(ang)
    return (jnp.asarray(np.tile(np.concatenate([cos, cos], axis=1), (1, 2))),
            jnp.asarray(np.tile(np.concatenate([-sin, sin], axis=1), (1, 2))))


def _rope_wide(ref):
    t = ref[...]
    return jnp.concatenate([t, t], axis=1)


def _swap_halves(x):
    w = x.shape[-1]
    lane = lax.broadcasted_iota(jnp.int32, x.shape, x.ndim - 1)
    return jnp.where((lane % 64) < 32, pltpu.roll(x, w - 32, x.ndim - 1), pltpu.roll(x, 32, x.ndim - 1))


def _mla_pre_fwd(proj, qn_w, kvn_w, wuqT, wukv, cos, sin, *, tm=ROW_TILE):
    S = proj.shape[0]
    tm = min(tm, S)

    def body(ql_ref, kl_ref, kr_ref, qw_ref, kw_ref, wuq_ref, wukv_ref, cos_ref, sin_ref,
             qc_ref, kc_ref, vv_ref):
        ql = ql_ref[...]
        qn = (ql * _rstd(ql) * qw_ref[...]).astype(BF16)
        q = _dot_nt(qn, wuq_ref[...])
        cs, sn = _rope_wide(cos_ref), _rope_wide(sin_ref)
        qr = q[:, 512:768]
        qr = qr * cs + _swap_halves(qr) * sn
        half = lax.broadcasted_iota(jnp.int32, (tm, 128), 1) // 64
        kl = kl_ref[...]
        kvn = (kl * _rstd(kl) * kw_ref[...]).astype(BF16)
        kr = kr_ref[...]
        kr = kr * cs[:, :128] + _swap_halves(kr) * sn[:, :128]
        kr2 = (kr + pltpu.roll(kr, 64, 1)).astype(BF16)
        for h in range(MLA_HEADS):
            qc_ref[h, :, 0:128] = q[:, 128 * h:128 * h + 128].astype(BF16)
            chunk = qr[:, 128 * (h // 2):128 * (h // 2) + 128]
            qc_ref[h, :, 128:256] = jnp.where(half == (h % 2), chunk, 0.0).astype(BF16)
            kc_ref[h, :, 0:128] = _dot(kvn, wukv_ref[2 * h]).astype(BF16)
            kc_ref[h, :, 128:256] = kr2
            vv_ref[h] = _dot(kvn, wukv_ref[2 * h + 1]).astype(BF16)

    const = lambda shape: pl.BlockSpec(shape, lambda i: tuple(0 for _ in shape))
    return pl.pallas_call(
        body, name="mla_pre_fwd", grid=(S // tm,),
        in_specs=[pl.BlockSpec((tm, 256), lambda i: (i, 3)), pl.BlockSpec((tm, 128), lambda i: (i, 8)),
                  pl.BlockSpec((tm, 128), lambda i: (i, 9)), const((1, 256)), const((1, 128)),
                  const((768, 256)), const((8, 128, 128)),
                  pl.BlockSpec((tm, 128), lambda i: (i, 0)), pl.BlockSpec((tm, 128), lambda i: (i, 0))],
        out_specs=[pl.BlockSpec((MLA_HEADS, tm, 256), lambda i: (0, i, 0)),
                   pl.BlockSpec((MLA_HEADS, tm, 256), lambda i: (0, i, 0)),
                   pl.BlockSpec((MLA_HEADS, tm, 128), lambda i: (0, i, 0))],
        out_shape=[jax.ShapeDtypeStruct((MLA_HEADS, S, 256), BF16), jax.ShapeDtypeStruct((MLA_HEADS, S, 256), BF16),
                   jax.ShapeDtypeStruct((MLA_HEADS, S, 128), BF16)],
        compiler_params=_params(("parallel",)),
    )(proj, proj, proj, qn_w, kvn_w, wuqT, wukv, cos, sin)


def _causal(i, j, t):
    row = i * t + lax.broadcasted_iota(jnp.int32, (t, t), 0)
    col = j * t + lax.broadcasted_iota(jnp.int32, (t, t), 1)
    return col <= row


def _mla_attn_fwd(qc, kc, vv, *, t=512):
    S = qc.shape[1]
    t = min(t, S)

    def body(q_ref, k_ref, v_ref, o_ref, l_ref):
        i = pl.program_id(0)
        diag = _causal(0, 0, t)

        def step(j, carry, masked):
            rows = pl.ds(pl.multiple_of(j * t, t), t)
            out = []
            for h in range(MLA_HEADS):
                m, l, acc = carry[h]
                s = _dot_nt(q_ref[h], k_ref[h, rows, :]) * MLA_SCALE
                if masked:
                    s = jnp.where(diag, s, -jnp.inf)
                m_new = jnp.maximum(m, jnp.max(s, axis=-1, keepdims=True))
                alpha = jnp.exp(m - m_new)
                p = jnp.exp(s - m_new)
                l = alpha * l + jnp.sum(p, axis=-1, keepdims=True)
                acc = alpha * acc + _dot(p.astype(BF16), v_ref[h, rows, :])
                out.append((m_new, l, acc))
            return tuple(out)

        init = tuple((jnp.full((t, 1), -jnp.inf, F32), jnp.zeros((t, 1), F32), jnp.zeros((t, MLA_V), F32))
                     for _ in range(MLA_HEADS))
        carry = lax.fori_loop(0, i, lambda j, c: step(j, c, False), init)
        carry = step(i, carry, True)
        for h in range(MLA_HEADS):
            m, l, acc = carry[h]
            o_ref[:, 128 * h:128 * h + 128] = acc / l
            l_ref[h] = jnp.broadcast_to(m + jnp.log(l), (t, 128))

    return pl.pallas_call(
        body, name="mla_attn_fwd", grid=(S // t,),
        in_specs=[pl.BlockSpec((MLA_HEADS, t, 256), lambda i: (0, i, 0)),
                  pl.BlockSpec((MLA_HEADS, S, 256), lambda i: (0, 0, 0)),
                  pl.BlockSpec((MLA_HEADS, S, 128), lambda i: (0, 0, 0))],
        out_specs=[pl.BlockSpec((t, 512), lambda i: (i, 0)),
                   pl.BlockSpec((MLA_HEADS, t, 128), lambda i: (0, i, 0))],
        out_shape=[jax.ShapeDtypeStruct((S, 512), F32), jax.ShapeDtypeStruct((MLA_HEADS, S, 128), F32)],
        compiler_params=_params(("parallel",)),
    )(qc, kc, vv)


def _mla_attn_bwd(qc, kc, vv, o, lse, do, *, t=512, tq=512):
    S = qc.shape[1]
    t = min(t, S)
    tq = min(tq, S)
    nblk = S // t
    hp = MLA_HEADS
    once = pl.Buffered(1)

    def body(q_ref, k_ref, v_ref, o_ref, l_ref, do_ref, dq_ref, dk_ref, dv_ref):
        j = pl.program_id(1)

        @pl.when(j == 0)
        def _():
            dq_ref[...] = jnp.zeros_like(dq_ref)

        first = (j * t) // tq

        def step(i, carry, masked):
            rows = pl.ds(pl.multiple_of(i * tq, tq), tq)
            if masked:
                row = i * tq + lax.broadcasted_iota(jnp.int32, (tq, t), 0)
                col = j * t + lax.broadcasted_iota(jnp.int32, (tq, t), 1)
                visible = col <= row
            out = []
            for h in range(hp):
                dk, dv = carry[h]
                k = k_ref[h]
                q = q_ref[h, rows, :]
                dov = do_ref[rows, 128 * h:128 * h + 128]
                lrow = l_ref[h, rows, :][:, 0:1]
                p = jnp.exp(_dot_nt(q, k) * MLA_SCALE - lrow)
                if masked:
                    p = jnp.where(visible, p, 0.0)
                dob = dov.astype(BF16)
                dv = dv + _dot_tn(p.astype(BF16), dob)
                dp = _dot_nt(dob, v_ref[h])
                delta = jnp.sum(dov * o_ref[rows, 128 * h:128 * h + 128], axis=-1, keepdims=True)
                ds = (p * (dp - delta) * MLA_SCALE).astype(BF16)
                dk = dk + _dot_tn(ds, q)
                dq_ref[h, rows, :] += _dot(ds, k)
                out.append((dk, dv))
            return tuple(out)

        init = tuple((jnp.zeros((t, 256), F32), jnp.zeros((t, MLA_V), F32)) for _ in range(hp))
        carry = step(first, init, True)
        carry = lax.fori_loop(first + 1, S // tq, lambda i, c: step(i, c, False), carry)
        for h in range(hp):
            dk_ref[h] = carry[h][0]
            dv_ref[h] = carry[h][1]

    return pl.pallas_call(
        body, name="mla_attn_bwd", grid=(MLA_HEADS // hp, nblk),
        in_specs=[pl.BlockSpec((hp, S, 256), lambda g, j: (g, 0, 0), pipeline_mode=once),
                  pl.BlockSpec((hp, t, 256), lambda g, j: (g, j, 0)),
                  pl.BlockSpec((hp, t, 128), lambda g, j: (g, j, 0)),
                  pl.BlockSpec((S, 128 * hp), lambda g, j: (0, g), pipeline_mode=once),
                  pl.BlockSpec((hp, S, 128), lambda g, j: (g, 0, 0), pipeline_mode=once),
                  pl.BlockSpec((S, 128 * hp), lambda g, j: (0, g), pipeline_mode=once)],
        out_specs=[pl.BlockSpec((hp, S, 256), lambda g, j: (g, 0, 0)),
                   pl.BlockSpec((hp, t, 256), lambda g, j: (g, j, 0)),
                   pl.BlockSpec((hp, t, 128), lambda g, j: (g, j, 0))],
        out_shape=[jax.ShapeDtypeStruct((MLA_HEADS, S, 256), F32), jax.ShapeDtypeStruct((MLA_HEADS, S, 256), F32),
                   jax.ShapeDtypeStruct((MLA_HEADS, S, 128), F32)],
        compiler_params=_params(("parallel", "arbitrary")),
    )(qc, kc, vv, o, lse, do)


def _mla_pre_bwd(proj, qn_w, kvn_w, wuqT, wukv, cos, sin, dqc, dkc, dvv, *, tm=ROW_TILE):
    S = proj.shape[0]
    tm = min(tm, S)

    def body(ql_ref, kl_ref, qw_ref, kw_ref, wuq_ref, wukv_ref, cos_ref, sin_ref, dqc_ref, dkc_ref, dvv_ref,
             dql_ref, dkl_ref, dkr_ref, gq_ref, gkv_ref, part_ref):
        @pl.when(pl.program_id(0) == 0)
        def _():
            gq_ref[...] = jnp.zeros_like(gq_ref)
            gkv_ref[...] = jnp.zeros_like(gkv_ref)
            part_ref[...] = jnp.zeros_like(part_ref)

        cs, sn = _rope_wide(cos_ref), _rope_wide(sin_ref)
        half = lax.broadcasted_iota(jnp.int32, (tm, 128), 1) // 64
        ql = ql_ref[...]
        rq = _rstd(ql)
        qhat = ql * rq
        qw = qw_ref[...]
        qn = (qhat * qw).astype(BF16)
        chunks = []
        for pair in range(2):
            chunks.append(jnp.where(half == 0, dqc_ref[2 * pair, :, 128:256], dqc_ref[2 * pair + 1, :, 128:256]))
        dqr = jnp.concatenate(chunks, axis=1)
        dqr = dqr * cs + _swap_halves(dqr * sn)
        dq = jnp.concatenate([dqc_ref[h, :, 0:128] for h in range(MLA_HEADS)] + [dqr], axis=1).astype(BF16)
        gq_ref[...] += _dot_tn(dq, qn)
        dqn = _dot(dq, wuq_ref[...])
        part_ref[0:1, :] += jnp.sum(dqn * qhat, axis=0, keepdims=True)
        dql_ref[...] = _rms_bwd(dqn * qw, qhat, rq)
        kl = kl_ref[...]
        rk = _rstd(kl)
        khat = kl * rk
        kw = kw_ref[...]
        kvn = (khat * kw).astype(BF16)
        dkvn = jnp.zeros((tm, MLA_KVR), F32)
        dkr2 = jnp.zeros((tm, 128), F32)
        for h in range(MLA_HEADS):
            dkn = dkc_ref[h, :, 0:128].astype(BF16)
            dvh = dvv_ref[h].astype(BF16)
            gkv_ref[2 * h] += _dot_tn(kvn, dkn)
            gkv_ref[2 * h + 1] += _dot_tn(kvn, dvh)
            dkvn += _dot_nt(dkn, wukv_ref[2 * h]) + _dot_nt(dvh, wukv_ref[2 * h + 1])
            dkr2 += dkc_ref[h, :, 128:256]
        part_ref[1:2, 0:128] += jnp.sum(dkvn * khat, axis=0, keepdims=True)
        dkl_ref[...] = _rms_bwd(dkvn * kw, khat, rk)
        dkr = jnp.where(half == 0, dkr2 + pltpu.roll(dkr2, 64, 1), 0.0)
        dkr_ref[...] = dkr * cs[:, :128] + _swap_halves(dkr * sn[:, :128])

    const = lambda shape: pl.BlockSpec(shape, lambda i: tuple(0 for _ in shape))
    heads = lambda w: pl.BlockSpec((MLA_HEADS, tm, w), lambda i: (0, i, 0))
    return pl.pallas_call(
        body, name="mla_pre_bwd", grid=(S // tm,),
        in_specs=[pl.BlockSpec((tm, 256), lambda i: (i, 3)), pl.BlockSpec((tm, 128), lambda i: (i, 8)),
                  const((1, 256)), const((1, 128)), const((768, 256)), const((8, 128, 128)),
                  pl.BlockSpec((tm, 128), lambda i: (i, 0)), pl.BlockSpec((tm, 128), lambda i: (i, 0)),
                  heads(256), heads(256), heads(128)],
        out_specs=[pl.BlockSpec((tm, 256), lambda i: (i, 0)), pl.BlockSpec((tm, 128), lambda i: (i, 0)),
                   pl.BlockSpec((tm, 128), lambda i: (i, 0)), const((768, 256)), const((8, 128, 128)), const((8, 256))],
        out_shape=[jax.ShapeDtypeStruct((S, 256), F32), jax.ShapeDtypeStruct((S, 128), F32),
                   jax.ShapeDtypeStruct((S, 128), F32), jax.ShapeDtypeStruct((768, 256), F32),
                   jax.ShapeDtypeStruct((8, 128, 128), F32), jax.ShapeDtypeStruct((8, 256), F32)],
        compiler_params=_params(("arbitrary",)),
    )(proj, proj, qn_w, kvn_w, wuqT, wukv, cos, sin, dqc, dkc, dvv)


def _mix_out_fwd(x, oa, ob, w_o, vecs, *, tm=ROW_TILE):
    S = x.shape[0]
    tm = min(tm, S)

    def body(x_ref, oa_ref, ob_ref, w_ref, vec_ref, xo_ref, mo_ref):
        mo = _dot(oa_ref[...].astype(BF16), w_ref[0:512, :]) + _dot(ob_ref[...].astype(BF16), w_ref[512:1024, :])
        mo_ref[...] = mo
        xo_ref[...] = x_ref[...] + vec_ref[3:4, :] * mo

    row = pl.BlockSpec((tm, D), lambda i: (i, 0))
    half = pl.BlockSpec((tm, 512), lambda i: (i, 0))
    return pl.pallas_call(
        body, name="mix_out_fwd", grid=(S // tm,),
        in_specs=[row, half, half, pl.BlockSpec((D, D), lambda i: (0, 0)), pl.BlockSpec((8, D), lambda i: (0, 0))],
        out_specs=[row, row],
        out_shape=[jax.ShapeDtypeStruct((S, D), F32), jax.ShapeDtypeStruct((S, D), F32)],
        compiler_params=_params(("parallel",)),
    )(x, oa, ob, w_o, vecs)


def _mix_out_bwd(dxo, mo, oa, ob, w_o, vecs, *, tm=ROW_TILE):
    S = dxo.shape[0]
    tm = min(tm, S)

    def body(dx_ref, mo_ref, oa_ref, ob_ref, w_ref, vec_ref, doa_ref, dob_ref, gw_ref, part_ref):
        @pl.when(pl.program_id(0) == 0)
        def _():
            gw_ref[...] = jnp.zeros_like(gw_ref)
            part_ref[...] = jnp.zeros_like(part_ref)

        dx = dx_ref[...]
        part_ref[0:1, :] += jnp.sum(dx * mo_ref[...], axis=0, keepdims=True)
        dmo = (vec_ref[3:4, :] * dx).astype(BF16)
        doa_ref[...] = _dot_nt(dmo, w_ref[0:512, :])
        dob_ref[...] = _dot_nt(dmo, w_ref[512:1024, :])
        gw_ref[0:512, :] += _dot_tn(oa_ref[...].astype(BF16), dmo)
        gw_ref[512:1024, :] += _dot_tn(ob_ref[...].astype(BF16), dmo)

    row = pl.BlockSpec((tm, D), lambda i: (i, 0))
    half = pl.BlockSpec((tm, 512), lambda i: (i, 0))
    return pl.pallas_call(
        body, name="mix_out_bwd", grid=(S // tm,),
        in_specs=[row, row, half, half, pl.BlockSpec((D, D), lambda i: (0, 0)), pl.BlockSpec((8, D), lambda i: (0, 0))],
        out_specs=[half, half, pl.BlockSpec((D, D), lambda i: (0, 0)), pl.BlockSpec((8, D), lambda i: (0, 0))],
        out_shape=[jax.ShapeDtypeStruct((S, 512), F32), jax.ShapeDtypeStruct((S, 512), F32),
                   jax.ShapeDtypeStruct((D, D), F32), jax.ShapeDtypeStruct((8, D), F32)],
        compiler_params=_params(("arbitrary",)),
    )(dxo, mo, oa, ob, w_o, vecs)


def _mix_in_bwd(h, w_inT, dq, dk, dv, dql, dkl, dkr, *, tm=ROW_TILE):
    S = h.shape[0]
    tm = min(tm, S)
    offs = (0, 512, 640, 768, 1024, 1152)
    wid = (512, 128, 128, 256, 128, 128)

    def body(h_ref, w_ref, dq_ref, dk_ref, dv_ref, dql_ref, dkl_ref, dkr_ref, dh_ref, gw_ref):
        @pl.when(pl.program_id(0) == 0)
        def _():
            gw_ref[...] = jnp.zeros_like(gw_ref)

        hv = h_ref[...]
        dh = jnp.zeros((tm, D), F32)
        for ref, o, w in zip((dq_ref, dk_ref, dv_ref, dql_ref, dkl_ref, dkr_ref), offs, wid):
            w = min(w, D_IN - o)
            dpart = ref[...][:, :w].astype(BF16)
            dh += _dot(dpart, w_ref[o:o + w, :])
            gw_ref[o:o + w, :] += _dot_tn(dpart, hv)
        dh_ref[...] = dh

    row = pl.BlockSpec((tm, D), lambda i: (i, 0))
    part = lambda w: pl.BlockSpec((tm, w), lambda i: (i, 0))
    return pl.pallas_call(
        body, name="mix_in_bwd", grid=(S // tm,),
        in_specs=[row, pl.BlockSpec((D_IN_PAD, D), lambda i: (0, 0))] + [part(w) for w in wid],
        out_specs=[row, pl.BlockSpec((D_IN, D), lambda i: (0, 0))],
        out_shape=[jax.ShapeDtypeStruct((S, D), F32), jax.ShapeDtypeStruct((D_IN, D), F32)],
        compiler_params=_params(("arbitrary",)),
    )(h, w_inT, dq, dk, dv, dql, dkl, dkr)


def _vecs(norm_w, mod9, k):
    return jnp.concatenate([norm_w.reshape(1, D), mod9[3 * k:3 * k + 3], jnp.zeros((4, D), F32)], axis=0)


def _uq_group_rows(wuqT):
    per = MLA_NOPE + MLA_ROPE
    nope = [wuqT[per * h:per * h + MLA_NOPE] for h in range(MLA_HEADS)]
    rope = [wuqT[per * h + MLA_NOPE:per * (h + 1)] for h in range(MLA_HEADS)]
    return jnp.concatenate(nope + rope, axis=0)


def _uq_ungroup_rows(g):
    parts = []
    for h in range(MLA_HEADS):
        parts += [g[MLA_NOPE * h:MLA_NOPE * (h + 1)], g[512 + MLA_ROPE * h:512 + MLA_ROPE * (h + 1)]]
    return jnp.concatenate(parts, axis=0)


def _local_step(x, tgt, mod9, norms, sinks, rel_bias, q_norm, kv_norm, W, on_grads=None):
    if on_grads is None:
        on_grads = lambda group, grads, after, vecs: vecs
    S = x.shape[0]
    v1 = _vecs(norms["ffn1"], mod9, 0)
    v2 = _vecs(norms["mix"], mod9, 1)
    v3 = _vecs(norms["ffn2"], mod9, 2)
    bucket = jnp.asarray(_bucket_table())
    cos, sin = _rope_tables(S)
    if isinstance(W, dict):
        full, W = W, (lambda group, after, vecs: (full, vecs))

    W1, v1 = W("ffn1", [], v1)
    x1, h1, a1, b1, f1 = _ffn_fwd(x, v1, W1["g1T"], W1["u1T"], W1["d1"], name="ffn1_fwd", tm=256, tf=D_FF)
    W2, v2 = W("mixer", [x1], v2)
    w_inT = jnp.pad(W2["w_inT"], ((0, D_IN_PAD - D_IN), (0, 0))).astype(BF16)
    wuqT = _uq_group_rows(W2["w_uqT"])
    h2, proj = _mix_in_fwd(x1, v2, w_inT)
    bias = _bias_build(rel_bias, bucket)
    oa = _swa_fwd(proj, bias, sinks)
    qc, kc, vv = _mla_pre_fwd(proj, q_norm, kv_norm, wuqT, W2["w_ukv"], cos, sin)
    ob, lse = _mla_attn_fwd(qc, kc, vv)
    _, v2o = W("ffn2_on_its_way", [ob], v2)
    x2, mo = _mix_out_fwd(x1, oa, ob, W2["w_o"], v2o)
    W3, v3 = W("ffn2", [x2], v3)
    x3, h3, a3, b3, f3 = _ffn_fwd(x2, v3, W3["g3T"], W3["u3T"], W3["d3"], name="ffn2_fwd", tm=256, tf=D_FF)
    dx3, head_part, df3 = _head(x3, tgt, norms["final"], f3, v3)

    gg3, gu3, gd3, dh3 = _ffn_bwd_main(h3, df3, a3, b3, W3["g3T"], W3["u3T"], W3["d3"], name="ffn2_bwd", tm=2048, tf=256)
    ffn2 = {"g3T": gg3, "u3T": gu3, "d3": gd3}
    v3 = on_grads("ffn2", ffn2, [], v3)
    dx2, n3_part = _norm_bwd(dh3, x2, dx3, v3, name="ffn2_norm_bwd")
    v2 = on_grads("ffn2", None, [dx2], v2)
    doa, dob, g_wo, g2_part = _mix_out_bwd(dx2, mo, oa, ob, W2["w_o"], v2)
    dq, dk, dv, drb, dsk = _swa_bwd(proj, bias, sinks, oa, doa, bucket)
    dqc, dkc, dvv = _mla_attn_bwd(qc, kc, vv, ob, lse, dob)
    dql, dkl, dkr, g_uq, g_ukv, mla_part = _mla_pre_bwd(proj, q_norm, kv_norm, wuqT, W2["w_ukv"], cos, sin, dqc, dkc, dvv)
    dh2, g_win = _mix_in_bwd(h2, w_inT, dq, dk, dv, dql, dkl, dkr)
    mixer = {"w_inT": g_win, "w_uqT": _uq_ungroup_rows(g_uq).astype(BF16),
             "w_ukv": g_ukv.astype(BF16), "w_o": g_wo.astype(BF16)}
    v2 = on_grads("mixer", mixer, [], v2)
    dx1, n2_part, df1 = _norm_bwd(dh2, x1, dx2, v2, name="mix_norm_bwd", below=(f1, v1))
    started = on_grads("mixer", None, [dx1], jnp.zeros((1, 1), F32))
    gg1, gu1, gd1, dh1 = _ffn_bwd_main(h1, df1, a1, b1, W1["g1T"], W1["u1T"], W1["d1"], name="ffn1_bwd",
                                       after=[started], tm=2048, tf=256)
    ffn1 = {"g1T": gg1, "u1T": gu1, "d1": gd1}
    v1 = on_grads("ffn1", ffn1, [], v1)
    dx0, n1_part = _norm_bwd(dh1, x, dx1, v1, name="ffn1_norm_bwd")

    grads = {**ffn1, **ffn2, **mixer}
    return head_part[1, 0], dx0, grads, _pack_vec(n1_part, n2_part, n3_part, head_part, g2_part, mla_part, dsk, drb)


SMALL_LAYOUT = (("norm_ffn1", 1024), ("norm_mix", 1024), ("norm_ffn2", 1024), ("norm_final", 1024),
                ("q_norm", 256), ("kv_norm", 128), ("sinks", 128), ("rel_bias", 256))
N_SMALL = sum(n for _, n in SMALL_LAYOUT)
LOSS_SLOT = 4 * 1024 + 256 + 128 + SWA_HEADS
N_MODVEC = N_MOD * D
N_VEC = N_MODVEC + N_SMALL


def _pack_vec(n1, n2, n3, head, g2, mla, dsk, drb):
    def body(n1_ref, n2_ref, n3_ref, head_ref, g2_ref, mla_ref, dsk_ref, drb_ref, out_ref):
        rows = [n1_ref[1:2, :], n1_ref[2:3, :], n2_ref[3:4, :], n2_ref[1:2, :], n2_ref[2:3, :], g2_ref[0:1, :],
                n3_ref[1:2, :], n3_ref[2:3, :], head_ref[3:4, :],
                n1_ref[0:1, :], n2_ref[0:1, :], n3_ref[0:1, :], head_ref[0:1, :]]
        for i, row in enumerate(rows):
            out_ref[:, D * i:D * (i + 1)] = row
        off = D * len(rows)
        out_ref[:, off:off + 256] = mla_ref[0:1, :]
        out_ref[:, off + 256:off + 384] = mla_ref[1:2, 0:128]

        def diagonal(block):
            r = lax.broadcasted_iota(jnp.int32, block.shape, 0)
            lane = lax.broadcasted_iota(jnp.int32, block.shape, 1)
            return jnp.sum(jnp.where(r == lane, block, 0.0), axis=0, keepdims=True)

        lane = lax.broadcasted_iota(jnp.int32, (1, 128), 1)
        out_ref[:, off + 384:off + 512] = jnp.where(lane == SWA_HEADS, head_ref[1:2, 0:128], diagonal(dsk_ref[...]))
        out_ref[:, off + 512:off + 640] = diagonal(drb_ref[0:128, :])
        out_ref[:, off + 640:off + 768] = diagonal(drb_ref[128:256, :])

    vm = pl.BlockSpec(memory_space=pltpu.VMEM)
    return pl.pallas_call(body, name="pack_vec", in_specs=[vm] * 8, out_specs=vm,
                          out_shape=jax.ShapeDtypeStruct((1, N_VEC), F32))(n1, n2, n3, head, g2, mla, dsk, drb)


def _coords():
    return lax.axis_index("x"), lax.axis_index("y"), lax.axis_index("c")


def _flip(v, bit):
    return 1 - v if bit else v


def _peer(r):
    x, y, c = _coords()
    return (_flip(x, r & 4), _flip(y, r & 2), _flip(c, r & 1))


def _mod_fwd(c_tile, w_mod, b_mod3):
    W = w_mod.shape[1]

    def body(c_ref, w_ref, b_ref, mod_ref, ca_ref, call_ref, part_ref, send_sems, recv_sems):
        x, y, c = _coords()
        me = 4 * x + 2 * y + c
        call_ref[me] = c_ref[...]
        sends = []
        for r in range(1, N_DEV):
            cp = pltpu.make_async_remote_copy(c_ref, call_ref.at[me], send_sems.at[0, r], recv_sems.at[0, r],
                                              device_id=_peer(r), device_id_type=MESH)
            cp.start()
            sends.append(cp)
        for r in range(1, N_DEV):
            pltpu.make_async_remote_copy(c_ref, call_ref.at[me], send_sems.at[0, r], recv_sems.at[0, r],
                                         device_id=_peer(r), device_id_type=MESH).wait_recv()
        cv = call_ref[...].reshape(8 * N_DEV, D)
        ca = (cv * _sigmoid(cv)).astype(BF16)
        ca_ref[...] = ca
        part_ref[...] = _dot(ca, w_ref[...].astype(BF16)).reshape(N_DEV, 8, W)
        mod_ref[me] = part_ref[me] + b_ref[me]
        for r in range(1, N_DEV):
            cp = pltpu.make_async_remote_copy(part_ref.at[me ^ r], mod_ref.at[me], send_sems.at[1, r],
                                              recv_sems.at[1, r], device_id=_peer(r), device_id_type=MESH)
            cp.start()
            sends.append(cp)
        for r in range(1, N_DEV):
            pltpu.make_async_remote_copy(part_ref.at[me ^ r], mod_ref.at[me], send_sems.at[1, r],
                                         recv_sems.at[1, r], device_id=_peer(r), device_id_type=MESH).wait_recv()
            mod_ref[me ^ r] = mod_ref[me ^ r] + b_ref[me ^ r]
        for cp in sends:
            cp.wait_send()

    vm = pl.BlockSpec(memory_space=pltpu.VMEM)
    return pl.pallas_call(
        body, name="mod_fwd", in_specs=[vm, vm, vm], out_specs=[vm, vm],
        out_shape=[jax.ShapeDtypeStruct((N_DEV, 8, W), F32), jax.ShapeDtypeStruct((8 * N_DEV, D), BF16)],
        scratch_shapes=[pltpu.VMEM((N_DEV, 8, D), F32), pltpu.VMEM((N_DEV, 8, W), F32),
                        pltpu.SemaphoreType.DMA((2, N_DEV)), pltpu.SemaphoreType.DMA((2, N_DEV))],
        compiler_params=_params(),
    )(c_tile, w_mod, b_mod3)


def _mod_bwd(allvec, ca, me_idx):
    W = N_MODVEC // N_DEV

    def body(me_ref, all_ref, cols_ref, ca_ref, gw_ref, sum_ref):
        in_first_row = lax.broadcasted_iota(jnp.int32, (N_DEV, 8, W), 1) == 0
        dm = jnp.where(in_first_row, cols_ref[...], 0.0).reshape(8 * N_DEV, W)
        gw_ref[...] = _dot_tn(ca_ref[...], dm.astype(BF16))
        total = all_ref[0]
        for k in range(1, N_DEV):
            total = total + all_ref[k]
        sum_ref[...] = total

    return pl.pallas_call(
        body, name="mod_bwd",
        grid_spec=pltpu.PrefetchScalarGridSpec(
            num_scalar_prefetch=1, grid=(1,),
            in_specs=[pl.BlockSpec((N_DEV, 1, N_VEC), lambda i, me: (0, 0, 0)),
                      pl.BlockSpec((N_DEV, 1, W), lambda i, me: (0, 0, me[0])),
                      pl.BlockSpec((8 * N_DEV, D), lambda i, me: (0, 0))],
            out_specs=[pl.BlockSpec((D, W), lambda i, me: (0, 0)), pl.BlockSpec((1, N_VEC), lambda i, me: (0, 0))]),
        out_shape=[jax.ShapeDtypeStruct((D, W), F32), jax.ShapeDtypeStruct((1, N_VEC), F32)],
        compiler_params=_params(("arbitrary",)),
    )(me_idx, allvec, allvec, ca)


def _wgather(shards):
    n = len(shards)

    def body(*refs):
        ins, outs, token = refs[:n], refs[n:2 * n], refs[2 * n]
        send_sems, recv_sems, local_sems = refs[2 * n + 1:]
        token[...] = jnp.zeros_like(token)
        x, y, c = _coords()
        me = 4 * x + 2 * y + c
        sib = (x, y, 1 - c)
        chips = [(1 - x, y), (x, 1 - y), (1 - x, 1 - y)]

        def copy(k, slot, block, to, src=None):
            return pltpu.make_async_remote_copy(
                src_ref=outs[k].at[block] if src is None else src, dst_ref=outs[k].at[block],
                send_sem=send_sems.at[k, slot], recv_sem=recv_sems.at[k, slot], device_id=to, device_id_type=MESH)

        local = [pltpu.make_async_copy(ins[k], outs[k].at[me], local_sems.at[k]) for k in range(n)]
        for cp in local:
            cp.start()
        first = []
        for k in range(n):
            first.append(copy(k, 0, me, sib, src=ins[k]))
            for j, chip in enumerate(chips):
                first.append(copy(k, 1 + j, me, (*chip, c), src=ins[k]))
        for cp in first:
            cp.start()
        passed = []
        for j, (cx, cy) in enumerate(chips):
            for k in range(n):
                blk = 4 * cx + 2 * cy + c
                copy(k, 1 + j, blk, sib).wait_recv()
                cp = copy(k, 4 + j, blk, sib)
                cp.start()
                passed.append(cp)
        for k in range(n):
            copy(k, 0, 4 * x + 2 * y + (1 - c), sib).wait_recv()
            for j, (cx, cy) in enumerate(chips):
                copy(k, 4 + j, 4 * cx + 2 * cy + (1 - c), sib).wait_recv()
        for cp in first + passed:
            cp.wait_send()
        for cp in local:
            cp.wait()

    anyspec = pl.BlockSpec(memory_space=pl.ANY)
    return pl.pallas_call(
        body, name="wgather", in_specs=[anyspec] * n,
        out_specs=[anyspec] * n + [pl.BlockSpec(memory_space=pltpu.VMEM)],
        out_shape=[jax.ShapeDtypeStruct((N_DEV,) + s.shape, s.dtype) for s in shards]
        + [jax.ShapeDtypeStruct((8, 128), F32)],
        scratch_shapes=[pltpu.SemaphoreType.DMA((n, 7)), pltpu.SemaphoreType.DMA((n, 7)),
                        pltpu.SemaphoreType.DMA((n,))],
    )(*shards)


class _GatherCopies:
    def __init__(self, lands, send_sems, recv_sems, k0=0, batches=None):
        x, y, c = _coords()
        me = 4 * x + 2 * y + c
        sib = (x, y, 1 - c)
        chips = [(1 - x, y), (x, 1 - y), (1 - x, 1 - y)]

        def copy(k, slot, block, to):
            return pltpu.make_async_remote_copy(
                src_ref=lands[k].at[block], dst_ref=lands[k].at[block],
                send_sem=send_sems.at[7 * (k0 + k) + slot], recv_sem=recv_sems.at[7 * (k0 + k) + slot],
                device_id=to, device_id_type=MESH)

        n = len(lands)
        self.first = [copy(k, 0, me, sib) for k in range(n)]
        for batch in batches or [range(n)]:
            self.first += [copy(k, 1 + j, me, (cx, cy, c)) for j, (cx, cy) in enumerate(chips) for k in batch]
        self.landed = [copy(k, 1 + j, 4 * cx + 2 * cy + c, sib) for j, (cx, cy) in enumerate(chips) for k in range(n)]
        self.passed = [copy(k, 4 + j, 4 * cx + 2 * cy + c, sib) for j, (cx, cy) in enumerate(chips) for k in range(n)]
        self.from_sib = [copy(k, 0, 4 * x + 2 * y + (1 - c), sib) for k in range(n)]
        self.from_sib += [copy(k, 4 + j, 4 * cx + 2 * cy + (1 - c), sib) for j, (cx, cy) in enumerate(chips)
                          for k in range(n)]


def _gather_start(lands, *, name, batches=None):
    n = len(lands)

    def body(*refs):
        for cp in _GatherCopies(refs[:n], refs[n], refs[n + 1], batches=batches).first:
            cp.start()
        refs[-1][...] = jnp.zeros_like(refs[-1])

    out = pl.pallas_call(
        body, name=name,
        out_shape=(pltpu.SemaphoreType.DMA((7 * n,)), pltpu.SemaphoreType.DMA((7 * n,)),
                   *[pltpu.HBM(l.shape, l.dtype) for l in lands], jax.ShapeDtypeStruct((8, 128), F32)),
        in_specs=[HBM_SPEC] * n,
        out_specs=(SEM_SPEC, SEM_SPEC, *[HBM_SPEC] * n, pl.BlockSpec(memory_space=pltpu.VMEM)),
        input_output_aliases={i: 2 + i for i in range(n)},
        compiler_params=pltpu.CompilerParams(has_side_effects=DATAFLOW),
    )(*[_in_hbm(l) for l in lands])
    return out[0], out[1], list(out[2:2 + n]), out[-1]


def _gather_pass(send_sems, recv_sems, lands, after, *, name, stage, k0=0):
    n = len(lands)

    def body(*refs):
        cps = _GatherCopies(refs[:n], refs[n], refs[n + 1], k0)
        if stage == "landed":
            for cp in cps.landed:
                cp.wait_recv()
        else:
            for cp in cps.passed:
                cp.start()
        refs[-1][...] = jnp.zeros_like(refs[-1])

    out = pl.pallas_call(
        body, name=name,
        out_shape=(*[pltpu.HBM(l.shape, l.dtype) for l in lands], jax.ShapeDtypeStruct((8, 128), F32)),
        in_specs=[HBM_SPEC] * n + [SEM_SPEC, SEM_SPEC] + [pl.BlockSpec(memory_space=pl.ANY)] * len(after),
        out_specs=(*[HBM_SPEC] * n, pl.BlockSpec(memory_space=pltpu.VMEM)),
        input_output_aliases={i: i for i in range(n)},
        compiler_params=pltpu.CompilerParams(has_side_effects=DATAFLOW),
    )(*lands, send_sems, recv_sems, *after)
    return list(out[:n]), out[-1]


def _gather_end(send_sems, recv_sems, lands, after, *, name, k0=0):
    n = len(lands)

    def body(*refs):
        cps = _GatherCopies(refs[:n], refs[n], refs[n + 1], k0)
        for cp in cps.from_sib:
            cp.wait_recv()
        for cp in cps.first + cps.passed:
            cp.wait_send()

    out = pl.pallas_call(
        body, name=name,
        out_shape=[pltpu.HBM(l.shape, l.dtype) for l in lands],
        in_specs=[HBM_SPEC] * n + [SEM_SPEC, SEM_SPEC] + [pl.BlockSpec(memory_space=pl.ANY)] * len(after),
        out_specs=[HBM_SPEC] * n,
        input_output_aliases={i: i for i in range(n)},
        compiler_params=pltpu.CompilerParams(has_side_effects=DATAFLOW),
    )(*lands, send_sems, recv_sems, *after)
    return list(out)


def _d2d_copies(grads, lands, send_sems, recv_sems):
    x, y, c = _coords()
    return [pltpu.make_async_remote_copy(
        src_ref=grads[k].at[2 * q + (1 - c)], dst_ref=lands[k].at[q],
        send_sem=send_sems.at[4 * k + q], recv_sem=recv_sems.at[4 * k + q],
        device_id=(x, y, 1 - c), device_id_type=MESH) for k in range(len(grads)) for q in range(4)]


def _vec_copies(srcs, lands, send_sems, recv_sems):
    x, y, c = _coords()
    me = 4 * x + 2 * y + c
    return [pltpu.make_async_remote_copy(
        src_ref=lands[0].at[me], dst_ref=lands[0].at[me], send_sem=send_sems.at[r - 1], recv_sem=recv_sems.at[r - 1],
        device_id=_peer(r), device_id_type=MESH) for r in range(1, N_DEV)]


def _chipsum(gs, sibs, cidx, *, name):
    n = len(gs)

    def body(c_ref, *refs):
        for k in range(n):
            refs[2 * n + k][...] = (refs[k][...].astype(F32) + refs[n + k][...].astype(F32)).astype(refs[2 * n + k].dtype)

    mine = [pl.BlockSpec((1,) + g.shape[1:], lambda q, c_ref: (2 * q + c_ref[0], 0, 0)) for g in gs]
    other = [pl.BlockSpec((1,) + g.shape[1:], lambda q, c_ref: (q, 0, 0)) for g in gs]
    return pl.pallas_call(
        body, name=name,
        grid_spec=pltpu.PrefetchScalarGridSpec(num_scalar_prefetch=1, grid=(4,), in_specs=mine + other, out_specs=other),
        out_shape=[jax.ShapeDtypeStruct((4,) + g.shape[1:], g.dtype) for g in gs],
        compiler_params=_params(("arbitrary",)),
    )(cidx, *gs, *sibs)


HBM_SPEC = pl.BlockSpec(memory_space=pltpu.HBM)
SEM_SPEC = pl.BlockSpec(memory_space=pltpu.SEMAPHORE)
DATAFLOW = pltpu.SideEffectType.DATAFLOW_SIDE_EFFECTING


def _in_hbm(a):
    return pltpu.with_memory_space_constraint(a, pltpu.HBM)


def _ici_copies(sums, lands, send_sems, recv_sems, k0=0):
    x, y, c = _coords()
    chips = [(1 - x, y), (x, 1 - y), (1 - x, 1 - y)]
    cps = []
    for k in range(len(sums)):
        for j, (cx, cy) in enumerate(chips):
            cps.append(pltpu.make_async_remote_copy(
                src_ref=sums[k].at[2 * cx + cy], dst_ref=lands[k].at[j],
                send_sem=send_sems.at[3 * (k0 + k) + j], recv_sem=recv_sems.at[3 * (k0 + k) + j],
                device_id=(cx, cy, c), device_id_type=MESH))
    return cps


def _split_start(copies, srcs, lands, n_sems, after, *, name):
    ns, nl = len(srcs), len(lands)

    def body(*refs):
        for cp in copies(refs[:ns], refs[ns:ns + nl], refs[ns + nl + len(after)], refs[ns + nl + len(after) + 1]):
            cp.start()
        refs[-1][...] = jnp.zeros_like(refs[-1])

    bufs = [_in_hbm(a) for a in list(srcs) + list(lands)]
    out = pl.pallas_call(
        body, name=name,
        out_shape=(pltpu.SemaphoreType.DMA((n_sems,)), pltpu.SemaphoreType.DMA((n_sems,)),
                   *[pltpu.HBM(a.shape, a.dtype) for a in bufs], jax.ShapeDtypeStruct((8, 128), F32)),
        in_specs=[HBM_SPEC] * len(bufs) + [pl.BlockSpec(memory_space=pl.ANY)] * len(after),
        out_specs=(SEM_SPEC, SEM_SPEC, *[HBM_SPEC] * len(bufs), pl.BlockSpec(memory_space=pltpu.VMEM)),
        input_output_aliases={i: 2 + i for i in range(len(bufs))},
        compiler_params=pltpu.CompilerParams(has_side_effects=DATAFLOW),
    )(*bufs, *after)
    return out[0], out[1], list(out[2:2 + ns]), list(out[2 + ns:2 + ns + nl]), out[-1]


def _split_wait(copies, send_sems, recv_sems, srcs, lands, after, *, name):
    ns, nl = len(srcs), len(lands)

    def body(*refs):
        for cp in copies(refs[:ns], refs[ns:ns + nl], refs[ns + nl], refs[ns + nl + 1]):
            cp.wait_send()
            cp.wait_recv()

    out = pl.pallas_call(
        body, name=name,
        out_shape=[pltpu.HBM(a.shape, a.dtype) for a in list(srcs) + list(lands)],
        in_specs=[HBM_SPEC] * (ns + nl) + [SEM_SPEC, SEM_SPEC] + [pl.BlockSpec(memory_space=pl.ANY)] * len(after),
        out_specs=[HBM_SPEC] * (ns + nl),
        input_output_aliases={i: i for i in range(ns + nl)},
        compiler_params=pltpu.CompilerParams(has_side_effects=DATAFLOW),
    )(*srcs, *lands, send_sems, recv_sems, *after)
    return list(out[:ns]), list(out[ns:])


ADAM_C1 = 1.0 / (1.0 - ADAM_B1 ** ADAM_STEP)
ADAM_C2 = 1.0 / (1.0 - ADAM_B2 ** ADAM_STEP)


def _adam_math(w, g, m, v):
    m2 = ADAM_B1 * m + (1.0 - ADAM_B1) * g
    v2 = ADAM_B2 * v + (1.0 - ADAM_B2) * (g * g)
    return -ADAM_LR * ((m2 * ADAM_C1) / (jnp.sqrt(v2 * ADAM_C2) + ADAM_EPS) + ADAM_WD * w), m2, v2


def _adamw(w, g, m, v, *, name):
    R, C = w.shape
    tr = R if R <= 512 else 256

    def body(w_ref, g_ref, m_ref, v_ref, d_ref, nm_ref, nv_ref):
        d_ref[...], nm_ref[...], nv_ref[...] = _adam_math(w_ref[...], g_ref[...], m_ref[...], v_ref[...])

    blk = pl.BlockSpec((tr, C), lambda i: (i, 0))
    return pl.pallas_call(
        body, name=name, grid=(R // tr,), in_specs=[blk] * 4, out_specs=[blk] * 3,
        out_shape=[jax.ShapeDtypeStruct((R, C), F32)] * 3,
        compiler_params=_params(("parallel",)),
    )(w, g, m, v)


def _adamw_rs(wmv, cs, rcv, qidx, *, name):
    n = len(wmv)
    r, cc = wmv[0][0].shape
    tr = r // 2 if r % 32 == 0 and r > 128 else r

    def body(q_ref, *refs):
        ins, outs = refs[:5 * n], refs[5 * n:]
        for k in range(n):
            w_ref, m_ref, v_ref, c_ref, r_ref = ins[5 * k:5 * k + 5]
            g_ref, d_ref, nm_ref, nv_ref = outs[4 * k:4 * k + 4]
            g = ((c_ref[0].astype(F32) + r_ref[0].astype(F32)) + r_ref[1].astype(F32)) + r_ref[2].astype(F32)
            g_ref[...] = g
            d_ref[...], nm_ref[...], nv_ref[...] = _adam_math(w_ref[...], g, m_ref[...], v_ref[...])

    blk = pl.BlockSpec((tr, cc), lambda i, q_ref: (i, 0))
    one = [blk, blk, blk, pl.BlockSpec((1, tr, cc), lambda i, q_ref: (q_ref[0], i, 0)),
           pl.BlockSpec((3, tr, cc), lambda i, q_ref: (0, i, 0))]
    out = pl.pallas_call(
        body, name=name,
        grid_spec=pltpu.PrefetchScalarGridSpec(num_scalar_prefetch=1, grid=(r // tr,), in_specs=one * n,
                                               out_specs=[blk] * (4 * n)),
        out_shape=[jax.ShapeDtypeStruct((r, cc), F32)] * (4 * n),
        compiler_params=_params(("arbitrary",)),
    )(qidx, *[a for (w, m, v), c, rc in zip(wmv, cs, rcv) for a in (w, m, v, c, rc)])
    return [tuple(out[4 * k:4 * k + 4]) for k in range(n)]


SMALL_PARAMS = ("norm_ffn1", "norm_mix", "norm_ffn2", "norm_final", "q_norm", "kv_norm", "sinks", "rel_bias", "b_mod")


def _adamw_small(gvec, wmv):
    widths = [wmv[3 * i].shape[1] for i in range(len(SMALL_PARAMS))]

    def body(*refs):
        g_all = refs[0]
        ins = refs[1:1 + 3 * len(SMALL_PARAMS)]
        outs = refs[1 + 3 * len(SMALL_PARAMS):]
        off = N_MODVEC
        for i, name in enumerate(SMALL_PARAMS):
            g_ref, d_ref, nm_ref, nv_ref = outs[4 * i:4 * i + 4]
            w_ref, m_ref, v_ref = ins[3 * i:3 * i + 3]
            start = 0 if name == "b_mod" else off
            g = g_all[:, start:start + widths[i]]
            g_ref[...] = g
            d_ref[...], nm_ref[...], nv_ref[...] = _adam_math(w_ref[...], g, m_ref[...], v_ref[...])
            if name != "b_mod":
                off += dict(SMALL_LAYOUT)[name]

    vm = pl.BlockSpec(memory_space=pltpu.VMEM)
    n_out = 4 * len(SMALL_PARAMS)
    out = pl.pallas_call(
        body, name="adamw_small", in_specs=[vm] * (1 + len(wmv)), out_specs=[vm] * n_out,
        out_shape=[jax.ShapeDtypeStruct((1, widths[i // 4]), F32) for i in range(n_out)],
        compiler_params=_params(),
    )(gvec, *wmv)
    return {name: out[4 * i:4 * i + 4] for i, name in enumerate(SMALL_PARAMS)}


TRANSPOSED = ("g1T", "u1T", "g3T", "u3T", "w_inT", "w_uqT")


def kernel(x, c, w_mod, b_mod, norm_ffn1, ffn1_gate, ffn1_up, ffn1_down, norm_mix, w_in, q_norm, kv_norm, w_uq, w_ukv, sinks, w_o, norm_ffn2, ffn2_gate, ffn2_up, ffn2_down, rel_bias, norm_final, loss_target, m_w_mod, m_b_mod, m_norm_ffn1, m_ffn1_gate, m_ffn1_up, m_ffn1_down, m_norm_mix, m_w_in, m_q_norm, m_kv_norm, m_w_uq, m_w_ukv, m_sinks, m_w_o, m_norm_ffn2, m_ffn2_gate, m_ffn2_up, m_ffn2_down, m_rel_bias, m_norm_final, v_w_mod, v_b_mod, v_norm_ffn1, v_ffn1_gate, v_ffn1_up, v_ffn1_down, v_norm_mix, v_w_in, v_q_norm, v_kv_norm, v_w_uq, v_w_ukv, v_sinks, v_w_o, v_norm_ffn2, v_ffn2_gate, v_ffn2_up, v_ffn2_down, v_rel_bias, v_norm_final):
    mx, my, mc = _coords()
    cidx = jnp.reshape(mc, (1,)).astype(jnp.int32)
    qidx = jnp.reshape(2 * mx + my, (1,)).astype(jnp.int32)
    WM = w_mod.shape[2]

    c_tile = jnp.pad(c, ((0, 7), (0, 0)))
    b_mod3 = jnp.pad(b_mod.reshape(N_DEV, 1, WM), ((0, 0), (0, 7), (0, 0)))
    mod3, ca = _mod_fwd(c_tile, w_mod[0], b_mod3)
    mod9 = mod3[:, 0, :].reshape(N_MOD, D)

    shards = {"g1T": ffn1_gate[0].T.astype(BF16), "u1T": ffn1_up[0].T.astype(BF16), "d1": ffn1_down[0].astype(BF16),
              "g3T": ffn2_gate[0].T.astype(BF16), "u3T": ffn2_up[0].T.astype(BF16), "d3": ffn2_down[0].astype(BF16),
              "w_inT": w_in[0].T, "w_uqT": w_uq[0].T.astype(BF16), "w_ukv": w_ukv[0].astype(BF16),
              "w_o": w_o[0].astype(BF16)}
    me = 4 * mx + 2 * my + mc
    groups = {"ffn1": ("g1T", "u1T", "d1"), "mixer": ("w_inT", "w_uqT", "w_ukv", "w_o"), "ffn2": ("g3T", "u3T", "d3")}
    arriving = {}

    def as_weights(group, gathered):
        return {k: g if k == "w_ukv" else g.reshape(N_DEV * g.shape[1], g.shape[2])
                for k, g in zip(groups[group], gathered)}

    later = groups["mixer"] + groups["ffn2"]
    place = {"mixer": 0, "ffn2": len(groups["mixer"])}

    def start_gather(token):
        lands = []
        for k in later:
            sh = shards[k] + token[0, 0].astype(shards[k].dtype)
            lands.append(lax.dynamic_update_slice(lax.empty((N_DEV,) + sh.shape, sh.dtype), sh[None], (me, 0, 0)))
        batches = [range(k0, k0 + len(groups[group])) for group, k0 in place.items()]
        send, recv, lands, started = _gather_start(lands, name="gather_start", batches=batches)
        for group, k0 in place.items():
            arriving[group] = (send, recv, lands[k0:k0 + len(groups[group])])
        return started

    def fetch(group, after, vecs):
        if group == "ffn1":
            *gathered, token = _wgather([shards[k] + ca[1, 0].astype(shards[k].dtype) for k in groups["ffn1"]])
            return as_weights("ffn1", gathered), vecs + start_gather(token)[0:1, 0:1]

        def pass_on(group, after):
            send, recv, lands = arriving[group]
            lands, token = _gather_pass(send, recv, lands, after, name="gather_landed_" + group, stage="landed",
                                        k0=place[group])
            lands, token = _gather_pass(send, recv, lands, [token], name="gather_onward_" + group, stage="onward",
                                        k0=place[group])
            arriving[group] = (send, recv, lands)
            return token

        if group == "ffn2_on_its_way":
            return None, vecs + pass_on("ffn2", after)[0:1, 0:1]
        if group == "mixer":
            after = [pass_on("mixer", after)]
        send, recv, lands = arriving[group]
        return as_weights(group, _gather_end(send, recv, lands, after, name="gather_end_" + group,
                                             k0=place[group])), vecs

    norms ={"ffn1": norm_ffn1, "mix": norm_mix, "ffn2": norm_ffn2, "final": norm_final.reshape(1, D)}
    in_flight = {}

    def on_grads(group, g, after, vecs, before_ici=()):
        if g is not None:
            names = list(g)
            by_dest = [g[k] if k == "w_ukv" else g[k].reshape((N_DEV, g[k].shape[0] // N_DEV) + g[k].shape[1:])
                       for k in names]
            lands = [lax.empty((4,) + a.shape[1:], a.dtype) for a in by_dest]
            send, recv, by_dest, lands, token = _split_start(_d2d_copies, by_dest, lands, 4 * len(names), after,
                                                             name="rs_d2d_start_" + group)
            in_flight[group] = (names, send, recv, by_dest, lands)
            return vecs + token[0:1, 0:1]
        names, send, recv, by_dest, lands = in_flight[group]
        by_dest, from_sib = _split_wait(_d2d_copies, send, recv, by_dest, lands, after, name="rs_d2d_wait_" + group)
        sums = _chipsum(by_dest, from_sib, cidx, name="chipsum_" + group)
        lands = [lax.empty((3,) + s.shape[1:], s.dtype) for s in sums]
        send, recv, sums, lands, token = _split_start(_ici_copies, sums, lands, 3 * len(names), list(before_ici),
                                                      name="rs_ici_start_" + group)
        in_flight[group] = (names, send, recv, sums, lands, token)
        return vecs + token[0:1, 0:1]

    _, grad_x, _, vec = _local_step(
        x[0], loss_target[0], mod9, norms, sinks, rel_bias, q_norm, kv_norm, fetch, on_grads=on_grads)

    vec = vec.reshape(1, 1, N_VEC)
    allvec = lax.dynamic_update_slice(lax.empty((N_DEV, 1, N_VEC), F32), vec, (me, 0, 0))
    vsend, vrecv, _, (allvec,), vec_started = _split_start(_vec_copies, [], [allvec], N_DEV - 1, [], name="vec_start")
    on_grads("ffn1", None, [grad_x], jnp.zeros((1, 1), F32), before_ici=[vec_started])

    owners = {"g1T": ("ffn1_gate", ffn1_gate, m_ffn1_gate, v_ffn1_gate), "u1T": ("ffn1_up", ffn1_up, m_ffn1_up, v_ffn1_up),
              "d1": ("ffn1_down", ffn1_down, m_ffn1_down, v_ffn1_down),
              "g3T": ("ffn2_gate", ffn2_gate, m_ffn2_gate, v_ffn2_gate), "u3T": ("ffn2_up", ffn2_up, m_ffn2_up, v_ffn2_up),
              "d3": ("ffn2_down", ffn2_down, m_ffn2_down, v_ffn2_down),
              "w_inT": ("w_in", w_in, m_w_in, v_w_in), "w_uqT": ("w_uq", w_uq, m_w_uq, v_w_uq),
              "w_ukv": ("w_ukv", w_ukv, m_w_ukv, v_w_ukv), "w_o": ("w_o", w_o, m_w_o, v_w_o)}
    res, done = {}, []

    def finish(group, after, one_by_one=False):
        names, send, recv, sums, lands, _ = in_flight[group]
        there = lambda k, a: a[0].T if k in TRANSPOSED else a[0]
        back = lambda k, a: a.T[None] if k in TRANSPOSED else a[None]
        wmv = [tuple(there(k, a) for a in owners[k][1:]) for k in names]
        if one_by_one:
            outs = []
            for i, k in enumerate(names):
                (cs,), (rc,) = _split_wait(functools.partial(_ici_copies, k0=i), send, recv, [sums[i]], [lands[i]],
                                           after, name="rs_ici_wait_" + k)
                outs.append(_adamw_rs([wmv[i]], [cs], [rc], qidx, name="adamw_" + owners[k][0])[0])
                after = [outs[-1][3]]
        else:
            sums, lands = _split_wait(_ici_copies, send, recv, sums, lands, after, name="rs_ici_wait_" + group)
            if len({w.shape for w, _, _ in wmv}) == 1:
                outs = _adamw_rs(wmv, sums, lands, qidx, name="adamw_" + group)
            else:
                outs = [_adamw_rs([t], [cs], [rc], qidx, name="adamw_" + owners[k][0])[0]
                        for k, t, cs, rc in zip(names, wmv, sums, lands)]
        for k, out in zip(names, outs):
            done.append(out[3])
            res[owners[k][0]] = tuple(back(k, a) for a in out)

    ffn1_started = in_flight["ffn1"][5]
    finish("ffn2", [ffn1_started])
    finish("mixer", [ffn1_started])

    _, (allvec,) = _split_wait(_vec_copies, vsend, vrecv, [], [allvec], [ffn1_started], name="vec_wait")
    g_wmod, gvec = _mod_bwd(allvec, ca, jnp.reshape(me, (1,)).astype(jnp.int32))
    loss = gvec[0, N_MODVEC + LOSS_SLOT]
    res["w_mod"] = tuple(a[None] for a in (g_wmod,) + tuple(_adamw(w_mod[0], g_wmod, m_w_mod[0], v_w_mod[0],
                                                                    name="adamw_w_mod")))
    small_in = {"norm_ffn1": (norm_ffn1, m_norm_ffn1, v_norm_ffn1), "norm_mix": (norm_mix, m_norm_mix, v_norm_mix),
                "norm_ffn2": (norm_ffn2, m_norm_ffn2, v_norm_ffn2), "norm_final": (norm_final, m_norm_final, v_norm_final),
                "q_norm": (q_norm, m_q_norm, v_q_norm), "kv_norm": (kv_norm, m_kv_norm, v_kv_norm),
                "sinks": (sinks, m_sinks, v_sinks), "rel_bias": (rel_bias, m_rel_bias, v_rel_bias),
                "b_mod": (b_mod, m_b_mod, v_b_mod)}
    small_out = _adamw_small(gvec, [a.reshape(1, -1) for k in SMALL_PARAMS for a in small_in[k]])
    for k in SMALL_PARAMS:
        res[k] = tuple(a.reshape(small_in[k][0].shape) for a in small_out[k])

    finish("ffn1", done + [res["w_mod"][3], small_out["b_mod"][3]], one_by_one=True)

    order = ("w_mod", "b_mod", "norm_ffn1", "ffn1_gate", "ffn1_up", "ffn1_down", "norm_mix", "w_in", "q_norm",
             "kv_norm", "w_uq", "w_ukv", "sinks", "w_o", "norm_ffn2", "ffn2_gate", "ffn2_up", "ffn2_down",
             "rel_bias", "norm_final")
    return (loss, grad_x[None]) + tuple(res[nm][kind] for kind in range(4) for nm in order)
```

```python
import functools
import math

import numpy as np
import jax
import jax.numpy as jnp
from jax import lax
from jax.experimental import pallas as pl
from jax.experimental.pallas import tpu as pltpu

F32 = jnp.float32
BF16 = jnp.bfloat16
MESH = pl.DeviceIdType.MESH

N_DEV = 8
D = 1024
D_FF = 2816
EPS = 1e-6
N_MOD = 9
SWA_HEADS = 8
SWA_DH = 64
WINDOW = 128
MLA_HEADS = 4
MLA_NOPE = 128
MLA_ROPE = 64
MLA_V = 128
MLA_QR = 256
MLA_KVR = 128
ROPE_THETA = 10000.0
NUM_BUCKETS = 32
D_IN = 1216
D_IN_PAD = 1280
SWA_SCALE = SWA_DH ** -0.5
MLA_SCALE = (MLA_NOPE + MLA_ROPE) ** -0.5

ADAM_LR = 0.001
ADAM_B1 = 0.9
ADAM_B2 = 0.999
ADAM_EPS = 1e-08
ADAM_WD = 0.01
ADAM_STEP = 10

V7X_VMEM_LIMIT = 56 * 1024 * 1024
ROW_TILE = 512

NT_DIMS = (((1,), (1,)), ((), ()))
TN_DIMS = (((0,), (0,)), ((), ()))


def _dot(a, b):
    return jnp.dot(a, b, preferred_element_type=F32)


def _dot_nt(a, b):
    return lax.dot_general(a, b, NT_DIMS, preferred_element_type=F32)


def _dot_tn(a, b):
    return lax.dot_general(a, b, TN_DIMS, preferred_element_type=F32)


def _params(sem=None):
    return pltpu.CompilerParams(dimension_semantics=sem, vmem_limit_bytes=V7X_VMEM_LIMIT)


def _rstd(x):
    return lax.rsqrt(jnp.mean(x * x, axis=-1, keepdims=True) + EPS)


def _rms_bwd(dy, xhat, r):
    return r * (dy - xhat * jnp.mean(dy * xhat, axis=-1, keepdims=True))


def _sigmoid(a):
    return 1.0 / (1.0 + jnp.exp(-a))


def _ffn_fwd(x, vecs, wgT, wuT, wd, *, name, tm=512, tf=256):
    S, F = x.shape[0], wd.shape[0]
    tm = min(tm, S)
    ni, nj = S // tm, F // tf

    def body(x_ref, vec_ref, wg_ref, wu_ref, wd_ref, xo_ref, h_ref, a_ref, b_ref, f_ref, acc_ref):
        j = pl.program_id(1)

        @pl.when(j == 0)
        def _():
            xv = x_ref[...]
            hn = xv * _rstd(xv) * vec_ref[0:1, :]
            h_ref[...] = (hn * (1.0 + vec_ref[2:3, :]) + vec_ref[1:2, :]).astype(BF16)

        h = h_ref[...]
        a = _dot_nt(h, wg_ref[...])
        b = _dot_nt(h, wu_ref[...])
        a_ref[...] = a.astype(BF16)
        b_ref[...] = b.astype(BF16)
        part = _dot((a * _sigmoid(a) * b).astype(BF16), wd_ref[...])

        def finish(f):
            f_ref[...] = f
            xo_ref[...] = x_ref[...] + (0.5 * vec_ref[3:4, :]) * f

        if nj == 1:
            finish(part)
        else:
            @pl.when(j == 0)
            def _():
                acc_ref[...] = part

            @pl.when((j > 0) & (j < nj - 1))
            def _():
                acc_ref[...] += part

            @pl.when(j == nj - 1)
            def _():
                finish(acc_ref[...] + part)

    row = pl.BlockSpec((tm, D), lambda i, j: (i, 0))
    wspec = pl.BlockSpec((tf, D), lambda i, j: (j, 0), pipeline_mode=pl.Buffered(1) if nj == 1 else None)
    act = pl.BlockSpec((tm, tf), lambda i, j: (i, j))
    return pl.pallas_call(
        body, name=name, grid=(ni, nj),
        in_specs=[row, pl.BlockSpec((8, D), lambda i, j: (0, 0)), wspec, wspec, wspec],
        out_specs=[row, row, act, act, row],
        out_shape=[jax.ShapeDtypeStruct((S, D), F32), jax.ShapeDtypeStruct((S, D), BF16),
                   jax.ShapeDtypeStruct((S, F), BF16), jax.ShapeDtypeStruct((S, F), BF16),
                   jax.ShapeDtypeStruct((S, D), F32)],
        scratch_shapes=[pltpu.VMEM((tm, D) if nj > 1 else (8, 128), F32)],
        compiler_params=_params(("parallel", "arbitrary")),
    )(x, vecs, wgT, wuT, wd)


def _ffn_bwd_main(h, df, a, b, wgT, wuT, wd, *, name, after=(), tm=512, tf=256):
    S = h.shape[0]
    tm = min(tm, S)
    ni, nj = S // tm, D_FF // tf

    def body(h_hbm, df_hbm, a_ref, b_ref, wg_ref, wu_ref, wd_ref, *rest):
        gg_ref, gu_ref, gd_ref, dh_hbm, h_v, df_v, dh_v, gg_acc, gu_acc, gd_acc, sem = rest[len(after):]
        j = pl.program_id(0)
        i = pl.program_id(1)

        @pl.when((j == 0) & (i == 0))
        def _():
            c1 = pltpu.make_async_copy(h_hbm, h_v, sem.at[0])
            c2 = pltpu.make_async_copy(df_hbm, df_v, sem.at[1])
            c1.start()
            c2.start()
            c1.wait()
            c2.wait()

        @pl.when(i == 0)
        def _():
            gg_acc[...] = jnp.zeros_like(gg_acc)
            gu_acc[...] = jnp.zeros_like(gu_acc)
            gd_acc[...] = jnp.zeros_like(gd_acc)

        rows = pl.ds(pl.multiple_of(i * tm, tm), tm)
        hi = h_v[rows, :]
        dfi = df_v[rows, :]
        av = a_ref[...].astype(F32)
        bv = b_ref[...].astype(F32)
        sg = _sigmoid(av)
        sa = av * sg
        hsw = (sa * bv).astype(BF16)
        dhsw = _dot_nt(dfi, wd_ref[...])
        da = (dhsw * bv * (sg * (1.0 + av * (1.0 - sg)))).astype(BF16)
        db = (dhsw * sa).astype(BF16)
        gd_acc[...] += _dot_tn(hsw, dfi)
        gg_acc[...] += _dot_tn(da, hi)
        gu_acc[...] += _dot_tn(db, hi)
        dh = _dot(da, wg_ref[...]) + _dot(db, wu_ref[...])

        @pl.when(j == 0)
        def _():
            dh_v[rows, :] = dh

        @pl.when(j > 0)
        def _():
            dh_v[rows, :] += dh

        @pl.when(i == ni - 1)
        def _():
            gg_ref[...] = gg_acc[...].astype(BF16)
            gu_ref[...] = gu_acc[...].astype(BF16)
            gd_ref[...] = gd_acc[...].astype(BF16)

        @pl.when((j == nj - 1) & (i == ni - 1))
        def _():
            c3 = pltpu.make_async_copy(dh_v, dh_hbm, sem.at[2])
            c3.start()
            c3.wait()

    anyspec = pl.BlockSpec(memory_space=pl.ANY)
    wspec = pl.BlockSpec((tf, D), lambda j, i: (j, 0))
    act = pl.BlockSpec((tm, tf), lambda j, i: (i, j))
    return pl.pallas_call(
        body, name=name, grid=(nj, ni),
        in_specs=[anyspec, anyspec, act, act, wspec, wspec, wspec] + [anyspec] * len(after),
        out_specs=[wspec, wspec, wspec, anyspec],
        out_shape=[jax.ShapeDtypeStruct((D_FF, D), BF16)] * 3 + [jax.ShapeDtypeStruct((S, D), F32)],
        scratch_shapes=[pltpu.VMEM((S, D), BF16), pltpu.VMEM((S, D), BF16), pltpu.VMEM((S, D), F32),
                        pltpu.VMEM((tf, D), F32), pltpu.VMEM((tf, D), F32), pltpu.VMEM((tf, D), F32),
                        pltpu.SemaphoreType.DMA((3,))],
        compiler_params=_params(("arbitrary", "arbitrary")),
    )(h, df, a, b, wgT, wuT, wd, *after)


def _ffn_out_bwd(dx, f, gate, df_ref, part_ref):
    df_ref[...] = ((0.5 * gate) * dx).astype(BF16)
    part_ref[3:4, :] += 0.5 * jnp.sum(dx * f, axis=0, keepdims=True)


def _norm_bwd(dh, x, dxo, vecs, *, name, below=None, tm=ROW_TILE):
    S = x.shape[0]
    tm = min(tm, S)

    def body(dh_ref, x_ref, dxo_ref, vec_ref, *rest):
        dx_ref, part_ref = rest[-2 if below is None else -3], rest[-1 if below is None else -2]

        @pl.when(pl.program_id(0) == 0)
        def _():
            part_ref[...] = jnp.zeros_like(part_ref)

        dh = dh_ref[...]
        xv = x_ref[...]
        r = _rstd(xv)
        xhat = xv * r
        w = vec_ref[0:1, :]
        xn = xhat * w
        dxn = dh * (1.0 + vec_ref[2:3, :])
        part_ref[0:1, :] += jnp.sum(dxn * xhat, axis=0, keepdims=True)
        part_ref[1:2, :] += jnp.sum(dh, axis=0, keepdims=True)
        part_ref[2:3, :] += jnp.sum(dh * xn, axis=0, keepdims=True)
        dx = dxo_ref[...] + _rms_bwd(dxn * w, xhat, r)
        dx_ref[...] = dx
        if below is not None:
            _ffn_out_bwd(dx, rest[0][...], rest[1][3:4, :], rest[-1], part_ref)

    row = pl.BlockSpec((tm, D), lambda i: (i, 0))
    vec = pl.BlockSpec((8, D), lambda i: (0, 0))
    extra = [] if below is None else [row, vec]
    return pl.pallas_call(
        body, name=name, grid=(S // tm,), in_specs=[row, row, row, vec] + extra,
        out_specs=[row, vec] + ([] if below is None else [row]),
        out_shape=[jax.ShapeDtypeStruct((S, D), F32), jax.ShapeDtypeStruct((8, D), F32)]
        + ([] if below is None else [jax.ShapeDtypeStruct((S, D), BF16)]),
        compiler_params=_params(("arbitrary",)),
    )(dh, x, dxo, vecs, *([] if below is None else below))


def _head(x, tgt, nf, f, vecs, *, tm=ROW_TILE):
    S = x.shape[0]
    tm = min(tm, S)

    def body(x_ref, t_ref, nf_ref, f_ref, vec_ref, dx_ref, part_ref, df_ref):
        @pl.when(pl.program_id(0) == 0)
        def _():
            part_ref[...] = jnp.zeros_like(part_ref)

        xv = x_ref[...]
        r = _rstd(xv)
        xhat = xv * r
        w = nf_ref[...]
        e = xhat * w - t_ref[...]
        dy = e * (1.0 / D)
        part_ref[0:1, :] += jnp.sum(dy * xhat, axis=0, keepdims=True)
        part_ref[1:2, :] += jnp.sum(e * e) * (0.5 / D)
        dx = _rms_bwd(dy * w, xhat, r)
        dx_ref[...] = dx
        _ffn_out_bwd(dx, f_ref[...], vec_ref[3:4, :], df_ref, part_ref)

    row = pl.BlockSpec((tm, D), lambda i: (i, 0))
    vec = pl.BlockSpec((8, D), lambda i: (0, 0))
    return pl.pallas_call(
        body, name="head", grid=(S // tm,),
        in_specs=[row, row, pl.BlockSpec((1, D), lambda i: (0, 0)), row, vec],
        out_specs=[row, vec, row],
        out_shape=[jax.ShapeDtypeStruct((S, D), F32), jax.ShapeDtypeStruct((8, D), F32),
                   jax.ShapeDtypeStruct((S, D), BF16)],
        compiler_params=_params(("arbitrary",)),
    )(x, tgt, nf, f, vecs)


def _mix_in_fwd(x, vecs, w_inT, *, tm=ROW_TILE):
    S = x.shape[0]
    tm = min(tm, S)

    def body(x_ref, vec_ref, w_ref, h_ref, p_ref):
        xv = x_ref[...]
        hn = xv * _rstd(xv) * vec_ref[0:1, :]
        h = (hn * (1.0 + vec_ref[2:3, :]) + vec_ref[1:2, :]).astype(BF16)
        h_ref[...] = h
        p_ref[...] = _dot_nt(h, w_ref[...])

    row = pl.BlockSpec((tm, D), lambda i: (i, 0))
    return pl.pallas_call(
        body, name="mix_in_fwd", grid=(S // tm,),
        in_specs=[row, pl.BlockSpec((8, D), lambda i: (0, 0)), pl.BlockSpec((D_IN_PAD, D), lambda i: (0, 0))],
        out_specs=[row, pl.BlockSpec((tm, D_IN_PAD), lambda i: (i, 0))],
        out_shape=[jax.ShapeDtypeStruct((S, D), BF16), jax.ShapeDtypeStruct((S, D_IN_PAD), F32)],
        compiler_params=_params(("parallel",)),
    )(x, vecs, w_inT)


def _bucket_table():
    qi = np.arange(WINDOW)[:, None]
    kj = np.arange(2 * WINDOW)[None, :]
    dist = qi + WINDOW - kj
    max_exact = NUM_BUCKETS // 2
    n = np.maximum(dist, 0)
    nf = np.maximum(n, 1).astype(np.float32)
    large = max_exact + (np.log(nf / np.float32(max_exact)) / np.float32(math.log(WINDOW / max_exact))
                         * np.float32(NUM_BUCKETS - max_exact)).astype(np.int32)
    large = np.minimum(large, NUM_BUCKETS - 1)
    return np.where(n < max_exact, n, large).astype(np.int32)


def _bias_build(rel_bias, bucket):
    def body(rb_ref, bk_ref, out_ref):
        bk = bk_ref[...]
        for h in range(SWA_HEADS):
            acc = jnp.zeros((WINDOW, 2 * WINDOW), F32)
            for b in range(NUM_BUCKETS):
                acc = jnp.where(bk == b, rb_ref[b, h], acc)
            out_ref[h] = acc

    return pl.pallas_call(
        body, name="bias_build",
        in_specs=[pl.BlockSpec(memory_space=pltpu.SMEM), pl.BlockSpec(memory_space=pltpu.VMEM)],
        out_specs=pl.BlockSpec(memory_space=pltpu.VMEM),
        out_shape=jax.ShapeDtypeStruct((SWA_HEADS, WINDOW, 2 * WINDOW), F32),
    )(rel_bias, bucket)


SWA_GROUP = 4
GROUP_ROWS = SWA_GROUP * WINDOW


SWA_SUB = 2


def _swa_valid(has_prev):
    row = lax.broadcasted_iota(jnp.int32, (GROUP_ROWS, 2 * WINDOW), 0) % WINDOW
    col = lax.broadcasted_iota(jnp.int32, (GROUP_ROWS, 2 * WINDOW), 1)
    dist = row + WINDOW - col
    return (dist >= 0) & (dist < WINDOW) & ((col >= WINDOW) | has_prev)


def _swa_keys(prev_ref, cur_ref, u):
    cur = cur_ref[...]
    before = prev_ref[...] if u == 0 else cur[WINDOW * (u - 1):WINDOW * u]
    return jnp.concatenate([before, cur[WINDOW * u:WINDOW * (u + 1)]], axis=0).astype(BF16)


def _stack_heads(x, g):
    return jnp.concatenate([x[:, 64 * h:64 * h + 64] for h in range(SWA_GROUP * g, SWA_GROUP * (g + 1))], axis=0)


def _unstack_heads(x4):
    return jnp.concatenate([x4[WINDOW * a:WINDOW * (a + 1)] for a in range(SWA_GROUP)], axis=1)


def _group_sinks(sink_ref, g):
    head = lax.broadcasted_iota(jnp.int32, (GROUP_ROWS, 1), 0) // WINDOW
    out = jnp.full((GROUP_ROWS, 1), sink_ref[0, SWA_GROUP * g], F32)
    for a in range(1, SWA_GROUP):
        out = jnp.where(head == a, sink_ref[0, SWA_GROUP * g + a], out)
    return out


def _swa_probs(qh, kk, bias_h, sink, valid):
    s = _dot_nt(qh, kk) * SWA_SCALE + bias_h
    s = jnp.where(valid, s, -jnp.inf)
    m = jnp.maximum(jnp.max(s, axis=-1, keepdims=True), sink)
    p = jnp.exp(s - m)
    ps = jnp.exp(sink - m)
    inv = 1.0 / (jnp.sum(p, axis=-1, keepdims=True) + ps)
    return p * inv, ps * inv


SWA_ROWS = SWA_SUB * WINDOW


def _swa_specs():
    prev = lambda n: jnp.maximum(SWA_SUB * n - 1, 0)
    return [pl.BlockSpec((SWA_ROWS, 512), lambda n: (n, 0)),
            pl.BlockSpec((SWA_ROWS, 128), lambda n: (n, 4)),
            pl.BlockSpec((WINDOW, 128), lambda n: (prev(n), 4)),
            pl.BlockSpec((SWA_ROWS, 128), lambda n: (n, 5)),
            pl.BlockSpec((WINDOW, 128), lambda n: (prev(n), 5)),
            pl.BlockSpec((SWA_HEADS, WINDOW, 2 * WINDOW), lambda n: (0, 0, 0)),
            pl.BlockSpec(memory_space=pltpu.SMEM)]


def _swa_fwd(proj, bias, sinks):
    S = proj.shape[0]

    def body(q_ref, kc_ref, kp_ref, vc_ref, vp_ref, bias_ref, sink_ref, o_ref):
        n = pl.program_id(0)
        for u in range(SWA_SUB):
            rows = slice(WINDOW * u, WINDOW * (u + 1))
            valid = _swa_valid(n > 0 if u == 0 else True)
            q = q_ref[rows, :].astype(BF16)
            kfull = _swa_keys(kp_ref, kc_ref, u)
            vfull = _swa_keys(vp_ref, vc_ref, u)
            for g in range(SWA_HEADS // SWA_GROUP):
                kk = kfull[:, 64 * g:64 * g + 64]
                vv = vfull[:, 64 * g:64 * g + 64]
                bias4 = bias_ref[SWA_GROUP * g:SWA_GROUP * (g + 1)].reshape(GROUP_ROWS, 2 * WINDOW)
                pk, _ = _swa_probs(_stack_heads(q, g), kk, bias4, _group_sinks(sink_ref, g), valid)
                o_ref[rows, 256 * g:256 * (g + 1)] = _unstack_heads(_dot(pk.astype(BF16), vv))

    return pl.pallas_call(
        body, name="swa_fwd", grid=(S // SWA_ROWS,),
        in_specs=_swa_specs(),
        out_specs=pl.BlockSpec((SWA_ROWS, 512), lambda n: (n, 0)),
        out_shape=jax.ShapeDtypeStruct((S, 512), F32),
        compiler_params=_params(("parallel",)),
    )(proj, proj, proj, proj, proj, bias, sinks)


def _swa_bwd(proj, bias, sinks, o, do, bucket):
    S = proj.shape[0]
    nb = S // SWA_ROWS

    def body(q_ref, kc_ref, kp_ref, vc_ref, vp_ref, bias_ref, sink_ref, o_ref, do_ref, bk_ref,
             dq_ref, dk_ref, dv_ref, drb_ref, dsk_ref, dbias_acc):
        n = pl.program_id(0)

        @pl.when(n == 0)
        def _():
            dk_ref[...] = jnp.zeros_like(dk_ref)
            dv_ref[...] = jnp.zeros_like(dv_ref)
            dsk_ref[...] = jnp.zeros_like(dsk_ref)
            dbias_acc[...] = jnp.zeros_like(dbias_acc)
            drb_ref[...] = jnp.zeros_like(drb_ref)

        for u in range(SWA_SUB):
            rows = slice(WINDOW * u, WINDOW * (u + 1))
            blk = SWA_SUB * n + u
            valid = _swa_valid(n > 0 if u == 0 else True)
            q = q_ref[rows, :].astype(BF16)
            dov = do_ref[rows, :]
            ov = o_ref[rows, :]
            kfull = _swa_keys(kp_ref, kc_ref, u)
            vfull = _swa_keys(vp_ref, vc_ref, u)
            prow = pl.ds(pl.multiple_of(jnp.maximum(blk - 1, 0) * WINDOW, WINDOW), WINDOW)
            crow = pl.ds(pl.multiple_of(blk * WINDOW, WINDOW), WINDOW)
            for g in range(SWA_HEADS // SWA_GROUP):
                heads = slice(SWA_GROUP * g, SWA_GROUP * (g + 1))
                kk = kfull[:, 64 * g:64 * g + 64]
                vv = vfull[:, 64 * g:64 * g + 64]
                q4 = _stack_heads(q, g)
                pk, psink = _swa_probs(q4, kk, bias_ref[heads].reshape(GROUP_ROWS, 2 * WINDOW),
                                       _group_sinks(sink_ref, g), valid)
                pkb = pk.astype(BF16)
                do4 = _stack_heads(dov, g)
                dob = do4.astype(BF16)
                dp = _dot_nt(dob, vv)
                delta = jnp.sum(do4 * _stack_heads(ov, g), axis=-1, keepdims=True)
                ds = pk * (dp - delta)
                dsink = -psink * delta
                for a in range(SWA_GROUP):
                    h = SWA_GROUP * g + a
                    part = jnp.sum(dsink[WINDOW * a:WINDOW * (a + 1)], keepdims=True)
                    dsk_ref[h:h + 1, :] += jnp.broadcast_to(part, (1, 128))
                dbias_acc[heads] += ds.reshape(SWA_GROUP, WINDOW, 2 * WINDOW)
                dsb = (ds * SWA_SCALE).astype(BF16)
                dq_ref[rows, 256 * g:256 * (g + 1)] = _unstack_heads(_dot(dsb, kk))
                dkk = _dot_tn(dsb, q4)
                dvv = _dot_tn(pkb, dob)
                dk_ref[prow, 64 * g:64 * g + 64] += dkk[:WINDOW]
                dk_ref[crow, 64 * g:64 * g + 64] += dkk[WINDOW:]
                dv_ref[prow, 64 * g:64 * g + 64] += dvv[:WINDOW]
                dv_ref[crow, 64 * g:64 * g + 64] += dvv[WINDOW:]

        @pl.when(n == nb - 1)
        def _():
            bk = bk_ref[...]
            for h in range(SWA_HEADS):
                dbh = dbias_acc[h]
                for b in range(NUM_BUCKETS):
                    val = jnp.sum(jnp.where(bk == b, dbh, 0.0), keepdims=True)
                    drb_ref[b * 8 + h:b * 8 + h + 1, :] = jnp.broadcast_to(val, (1, 128))

    full = lambda shape: pl.BlockSpec(shape, lambda n: tuple(0 for _ in shape))
    return pl.pallas_call(
        body, name="swa_bwd", grid=(nb,),
        in_specs=_swa_specs() + [pl.BlockSpec((SWA_ROWS, 512), lambda n: (n, 0)),
                                 pl.BlockSpec((SWA_ROWS, 512), lambda n: (n, 0)), full((WINDOW, 2 * WINDOW))],
        out_specs=[pl.BlockSpec((SWA_ROWS, 512), lambda n: (n, 0)), full((S, 128)), full((S, 128)),
                   full((NUM_BUCKETS * 8, 128)), full((8, 128))],
        out_shape=[jax.ShapeDtypeStruct((S, 512), F32), jax.ShapeDtypeStruct((S, 128), F32),
                   jax.ShapeDtypeStruct((S, 128), F32), jax.ShapeDtypeStruct((NUM_BUCKETS * 8, 128), F32),
                   jax.ShapeDtypeStruct((8, 128), F32)],
        scratch_shapes=[pltpu.VMEM((SWA_HEADS, WINDOW, 2 * WINDOW), F32)],
        compiler_params=_params(("arbitrary",)),
    )(proj, proj, proj, proj, proj, bias, sinks, o, do, bucket)


def _rope_tables(S):
    inv = np.float32(ROPE_THETA) ** (-np.arange(0, MLA_ROPE, 2, dtype=np.float32) / np.float32(MLA_ROPE))
    ang = np.arange(S, dtype=np.float32)[:, None] * inv[None, :]
    cos, sin = np.cos(ang), np.sin(ang)
    return (jnp.asarray(np.tile(np.concatenate([cos, cos], axis=1), (1, 2))),
            jnp.asarray(np.tile(np.concatenate([-sin, sin], axis=1), (1, 2))))


def _rope_wide(ref):
    t = ref[...]
    return jnp.concatenate([t, t], axis=1)


def _swap_halves(x):
    w = x.shape[-1]
    lane = lax.broadcasted_iota(jnp.int32, x.shape, x.ndim - 1)
    return jnp.where((lane % 64) < 32, pltpu.roll(x, w - 32, x.ndim - 1), pltpu.roll(x, 32, x.ndim - 1))


def _mla_pre_fwd(proj, qn_w, kvn_w, wuqT, wukv, cos, sin, *, tm=ROW_TILE):
    S = proj.shape[0]
    tm = min(tm, S)

    def body(ql_ref, kl_ref, kr_ref, qw_ref, kw_ref, wuq_ref, wukv_ref, cos_ref, sin_ref,
             qc_ref, kc_ref, vv_ref):
        ql = ql_ref[...]
        qn = (ql * _rstd(ql) * qw_ref[...]).astype(BF16)
        q = _dot_nt(qn, wuq_ref[...])
        cs, sn = _rope_wide(cos_ref), _rope_wide(sin_ref)
        qr = q[:, 512:768]
        qr = qr * cs + _swap_halves(qr) * sn
        half = lax.broadcasted_iota(jnp.int32, (tm, 128), 1) // 64
        kl = kl_ref[...]
        kvn = (kl * _rstd(kl) * kw_ref[...]).astype(BF16)
        kr = kr_ref[...]
        kr = kr * cs[:, :128] + _swap_halves(kr) * sn[:, :128]
        kr2 = (kr + pltpu.roll(kr, 64, 1)).astype(BF16)
        for h in range(MLA_HEADS):
            qc_ref[h, :, 0:128] = q[:, 128 * h:128 * h + 128].astype(BF16)
            chunk = qr[:, 128 * (h // 2):128 * (h // 2) + 128]
            qc_ref[h, :, 128:256] = jnp.where(half == (h % 2), chunk, 0.0).astype(BF16)
            kc_ref[h, :, 0:128] = _dot(kvn, wukv_ref[2 * h]).astype(BF16)
            kc_ref[h, :, 128:256] = kr2
            vv_ref[h] = _dot(kvn, wukv_ref[2 * h + 1]).astype(BF16)

    const = lambda shape: pl.BlockSpec(shape, lambda i: tuple(0 for _ in shape))
    return pl.pallas_call(
        body, name="mla_pre_fwd", grid=(S // tm,),
        in_specs=[pl.BlockSpec((tm, 256), lambda i: (i, 3)), pl.BlockSpec((tm, 128), lambda i: (i, 8)),
                  pl.BlockSpec((tm, 128), lambda i: (i, 9)), const((1, 256)), const((1, 128)),
                  const((768, 256)), const((8, 128, 128)),
                  pl.BlockSpec((tm, 128), lambda i: (i, 0)), pl.BlockSpec((tm, 128), lambda i: (i, 0))],
        out_specs=[pl.BlockSpec((MLA_HEADS, tm, 256), lambda i: (0, i, 0)),
                   pl.BlockSpec((MLA_HEADS, tm, 256), lambda i: (0, i, 0)),
                   pl.BlockSpec((MLA_HEADS, tm, 128), lambda i: (0, i, 0))],
        out_shape=[jax.ShapeDtypeStruct((MLA_HEADS, S, 256), BF16), jax.ShapeDtypeStruct((MLA_HEADS, S, 256), BF16),
                   jax.ShapeDtypeStruct((MLA_HEADS, S, 128), BF16)],
        compiler_params=_params(("parallel",)),
    )(proj, proj, proj, qn_w, kvn_w, wuqT, wukv, cos, sin)


def _causal(i, j, t):
    row = i * t + lax.broadcasted_iota(jnp.int32, (t, t), 0)
    col = j * t + lax.broadcasted_iota(jnp.int32, (t, t), 1)
    return col <= row


def _mla_attn_fwd(qc, kc, vv, *, t=512):
    S = qc.shape[1]
    t = min(t, S)

    def body(q_ref, k_ref, v_ref, o_ref, l_ref):
        i = pl.program_id(0)
        diag = _causal(0, 0, t)

        def step(j, carry, masked):
            rows = pl.ds(pl.multiple_of(j * t, t), t)
            out = []
            for h in range(MLA_HEADS):
                m, l, acc = carry[h]
                s = _dot_nt(q_ref[h], k_ref[h, rows, :]) * MLA_SCALE
                if masked:
                    s = jnp.where(diag, s, -jnp.inf)
                m_new = jnp.maximum(m, jnp.max(s, axis=-1, keepdims=True))
                alpha = jnp.exp(m - m_new)
                p = jnp.exp(s - m_new)
                l = alpha * l + jnp.sum(p, axis=-1, keepdims=True)
                acc = alpha * acc + _dot(p.astype(BF16), v_ref[h, rows, :])
                out.append((m_new, l, acc))
            return tuple(out)

        init = tuple((jnp.full((t, 1), -jnp.inf, F32), jnp.zeros((t, 1), F32), jnp.zeros((t, MLA_V), F32))
                     for _ in range(MLA_HEADS))
        carry = lax.fori_loop(0, i, lambda j, c: step(j, c, False), init)
        carry = step(i, carry, True)
        for h in range(MLA_HEADS):
            m, l, acc = carry[h]
            o_ref[:, 128 * h:128 * h + 128] = acc / l
            l_ref[h] = jnp.broadcast_to(m + jnp.log(l), (t, 128))

    return pl.pallas_call(
        body, name="mla_attn_fwd", grid=(S // t,),
        in_specs=[pl.BlockSpec((MLA_HEADS, t, 256), lambda i: (0, i, 0)),
                  pl.BlockSpec((MLA_HEADS, S, 256), lambda i: (0, 0, 0)),
                  pl.BlockSpec((MLA_HEADS, S, 128), lambda i: (0, 0, 0))],
        out_specs=[pl.BlockSpec((t, 512), lambda i: (i, 0)),
                   pl.BlockSpec((MLA_HEADS, t, 128), lambda i: (0, i, 0))],
        out_shape=[jax.ShapeDtypeStruct((S, 512), F32), jax.ShapeDtypeStruct((MLA_HEADS, S, 128), F32)],
        compiler_params=_params(("parallel",)),
    )(qc, kc, vv)


def _mla_attn_bwd(qc, kc, vv, o, lse, do, *, t=512, tq=512):
    S = qc.shape[1]
    t = min(t, S)
    tq = min(tq, S)
    nblk = S // t
    hp = MLA_HEADS
    once = pl.Buffered(1)

    def body(q_ref, k_ref, v_ref, o_ref, l_ref, do_ref, dq_ref, dk_ref, dv_ref):
        j = pl.program_id(1)

        @pl.when(j == 0)
        def _():
            dq_ref[...] = jnp.zeros_like(dq_ref)

        first = (j * t) // tq

        def step(i, carry, masked):
            rows = pl.ds(pl.multiple_of(i * tq, tq), tq)
            if masked:
                row = i * tq + lax.broadcasted_iota(jnp.int32, (tq, t), 0)
                col = j * t + lax.broadcasted_iota(jnp.int32, (tq, t), 1)
                visible = col <= row
            out = []
            for h in range(hp):
                dk, dv = carry[h]
                k = k_ref[h]
                q = q_ref[h, rows, :]
                dov = do_ref[rows, 128 * h:128 * h + 128]
                lrow = l_ref[h, rows, :][:, 0:1]
                p = jnp.exp(_dot_nt(q, k) * MLA_SCALE - lrow)
                if masked:
                    p = jnp.where(visible, p, 0.0)
                dob = dov.astype(BF16)
                dv = dv + _dot_tn(p.astype(BF16), dob)
                dp = _dot_nt(dob, v_ref[h])
                delta = jnp.sum(dov * o_ref[rows, 128 * h:128 * h + 128], axis=-1, keepdims=True)
                ds = (p * (dp - delta) * MLA_SCALE).astype(BF16)
                dk = dk + _dot_tn(ds, q)
                dq_ref[h, rows, :] += _dot(ds, k)
                out.append((dk, dv))
            return tuple(out)

        init = tuple((jnp.zeros((t, 256), F32), jnp.zeros((t, MLA_V), F32)) for _ in range(hp))
        carry = step(first, init, True)
        carry = lax.fori_loop(first + 1, S // tq, lambda i, c: step(i, c, False), carry)
        for h in range(hp):
            dk_ref[h] = carry[h][0]
            dv_ref[h] = carry[h][1]

    return pl.pallas_call(
        body, name="mla_attn_bwd", grid=(MLA_HEADS // hp, nblk),
        in_specs=[pl.BlockSpec((hp, S, 256), lambda g, j: (g, 0, 0), pipeline_mode=once),
                  pl.BlockSpec((hp, t, 256), lambda g, j: (g, j, 0)),
                  pl.BlockSpec((hp, t, 128), lambda g, j: (g, j, 0)),
                  pl.BlockSpec((S, 128 * hp), lambda g, j: (0, g), pipeline_mode=once),
                  pl.BlockSpec((hp, S, 128), lambda g, j: (g, 0, 0), pipeline_mode=once),
                  pl.BlockSpec((S, 128 * hp), lambda g, j: (0, g), pipeline_mode=once)],
        out_specs=[pl.BlockSpec((hp, S, 256), lambda g, j: (g, 0, 0)),
                   pl.BlockSpec((hp, t, 256), lambda g, j: (g, j, 0)),
                   pl.BlockSpec((hp, t, 128), lambda g, j: (g, j, 0))],
        out_shape=[jax.ShapeDtypeStruct((MLA_HEADS, S, 256), F32), jax.ShapeDtypeStruct((MLA_HEADS, S, 256), F32),
                   jax.ShapeDtypeStruct((MLA_HEADS, S, 128), F32)],
        compiler_params=_params(("parallel", "arbitrary")),
    )(qc, kc, vv, o, lse, do)


def _mla_pre_bwd(proj, qn_w, kvn_w, wuqT, wukv, cos, sin, dqc, dkc, dvv, *, tm=ROW_TILE):
    S = proj.shape[0]
    tm = min(tm, S)

    def body(ql_ref, kl_ref, qw_ref, kw_ref, wuq_ref, wukv_ref, cos_ref, sin_ref, dqc_ref, dkc_ref, dvv_ref,
             dql_ref, dkl_ref, dkr_ref, gq_ref, gkv_ref, part_ref):
        @pl.when(pl.program_id(0) == 0)
        def _():
            gq_ref[...] = jnp.zeros_like(gq_ref)
            gkv_ref[...] = jnp.zeros_like(gkv_ref)
            part_ref[...] = jnp.zeros_like(part_ref)

        cs, sn = _rope_wide(cos_ref), _rope_wide(sin_ref)
        half = lax.broadcasted_iota(jnp.int32, (tm, 128), 1) // 64
        ql = ql_ref[...]
        rq = _rstd(ql)
        qhat = ql * rq
        qw = qw_ref[...]
        qn = (qhat * qw).astype(BF16)
        chunks = []
        for pair in range(2):
            chunks.append(jnp.where(half == 0, dqc_ref[2 * pair, :, 128:256], dqc_ref[2 * pair + 1, :, 128:256]))
        dqr = jnp.concatenate(chunks, axis=1)
        dqr = dqr * cs + _swap_halves(dqr * sn)
        dq = jnp.concatenate([dqc_ref[h, :, 0:128] for h in range(MLA_HEADS)] + [dqr], axis=1).astype(BF16)
        gq_ref[...] += _dot_tn(dq, qn)
        dqn = _dot(dq, wuq_ref[...])
        part_ref[0:1, :] += jnp.sum(dqn * qhat, axis=0, keepdims=True)
        dql_ref[...] = _rms_bwd(dqn * qw, qhat, rq)
        kl = kl_ref[...]
        rk = _rstd(kl)
        khat = kl * rk
        kw = kw_ref[...]
        kvn = (khat * kw).astype(BF16)
        dkvn = jnp.zeros((tm, MLA_KVR), F32)
        dkr2 = jnp.zeros((tm, 128), F32)
        for h in range(MLA_HEADS):
            dkn = dkc_ref[h, :, 0:128].astype(BF16)
            dvh = dvv_ref[h].astype(BF16)
            gkv_ref[2 * h] += _dot_tn(kvn, dkn)
            gkv_ref[2 * h + 1] += _dot_tn(kvn, dvh)
            dkvn += _dot_nt(dkn, wukv_ref[2 * h]) + _dot_nt(dvh, wukv_ref[2 * h + 1])
            dkr2 += dkc_ref[h, :, 128:256]
        part_ref[1:2, 0:128] += jnp.sum(dkvn * khat, axis=0, keepdims=True)
        dkl_ref[...] = _rms_bwd(dkvn * kw, khat, rk)
        dkr = jnp.where(half == 0, dkr2 + pltpu.roll(dkr2, 64, 1), 0.0)
        dkr_ref[...] = dkr * cs[:, :128] + _swap_halves(dkr * sn[:, :128])

    const = lambda shape: pl.BlockSpec(shape, lambda i: tuple(0 for _ in shape))
    heads = lambda w: pl.BlockSpec((MLA_HEADS, tm, w), lambda i: (0, i, 0))
    return pl.pallas_call(
        body, name="mla_pre_bwd", grid=(S // tm,),
        in_specs=[pl.BlockSpec((tm, 256), lambda i: (i, 3)), pl.BlockSpec((tm, 128), lambda i: (i, 8)),
                  const((1, 256)), const((1, 128)), const((768, 256)), const((8, 128, 128)),
                  pl.BlockSpec((tm, 128), lambda i: (i, 0)), pl.BlockSpec((tm, 128), lambda i: (i, 0)),
                  heads(256), heads(256), heads(128)],
        out_specs=[pl.BlockSpec((tm, 256), lambda i: (i, 0)), pl.BlockSpec((tm, 128), lambda i: (i, 0)),
                   pl.BlockSpec((tm, 128), lambda i: (i, 0)), const((768, 256)), const((8, 128, 128)), const((8, 256))],
        out_shape=[jax.ShapeDtypeStruct((S, 256), F32), jax.ShapeDtypeStruct((S, 128), F32),
                   jax.ShapeDtypeStruct((S, 128), F32), jax.ShapeDtypeStruct((768, 256), F32),
                   jax.ShapeDtypeStruct((8, 128, 128), F32), jax.ShapeDtypeStruct((8, 256), F32)],
        compiler_params=_params(("arbitrary",)),
    )(proj, proj, qn_w, kvn_w, wuqT, wukv, cos, sin, dqc, dkc, dvv)


def _mix_out_fwd(x, oa, ob, w_o, vecs, *, tm=ROW_TILE):
    S = x.shape[0]
    tm = min(tm, S)

    def body(x_ref, oa_ref, ob_ref, w_ref, vec_ref, xo_ref, mo_ref):
        mo = _dot(oa_ref[...].astype(BF16), w_ref[0:512, :]) + _dot(ob_ref[...].astype(BF16), w_ref[512:1024, :])
        mo_ref[...] = mo
        xo_ref[...] = x_ref[...] + vec_ref[3:4, :] * mo

    row = pl.BlockSpec((tm, D), lambda i: (i, 0))
    half = pl.BlockSpec((tm, 512), lambda i: (i, 0))
    return pl.pallas_call(
        body, name="mix_out_fwd", grid=(S // tm,),
        in_specs=[row, half, half, pl.BlockSpec((D, D), lambda i: (0, 0)), pl.BlockSpec((8, D), lambda i: (0, 0))],
        out_specs=[row, row],
        out_shape=[jax.ShapeDtypeStruct((S, D), F32), jax.ShapeDtypeStruct((S, D), F32)],
        compiler_params=_params(("parallel",)),
    )(x, oa, ob, w_o, vecs)


def _mix_out_bwd(dxo, mo, oa, ob, w_o, vecs, *, tm=ROW_TILE):
    S = dxo.shape[0]
    tm = min(tm, S)

    def body(dx_ref, mo_ref, oa_ref, ob_ref, w_ref, vec_ref, doa_ref, dob_ref, gw_ref, part_ref):
        @pl.when(pl.program_id(0) == 0)
        def _():
            gw_ref[...] = jnp.zeros_like(gw_ref)
            part_ref[...] = jnp.zeros_like(part_ref)

        dx = dx_ref[...]
        part_ref[0:1, :] += jnp.sum(dx * mo_ref[...], axis=0, keepdims=True)
        dmo = (vec_ref[3:4, :] * dx).astype(BF16)
        doa_ref[...] = _dot_nt(dmo, w_ref[0:512, :])
        dob_ref[...] = _dot_nt(dmo, w_ref[512:1024, :])
        gw_ref[0:512, :] += _dot_tn(oa_ref[...].astype(BF16), dmo)
        gw_ref[512:1024, :] += _dot_tn(ob_ref[...].astype(BF16), dmo)

    row = pl.BlockSpec((tm, D), lambda i: (i, 0))
    half = pl.BlockSpec((tm, 512), lambda i: (i, 0))
    return pl.pallas_call(
        body, name="mix_out_bwd", grid=(S // tm,),
        in_specs=[row, row, half, half, pl.BlockSpec((D, D), lambda i: (0, 0)), pl.BlockSpec((8, D), lambda i: (0, 0))],
        out_specs=[half, half, pl.BlockSpec((D, D), lambda i: (0, 0)), pl.BlockSpec((8, D), lambda i: (0, 0))],
        out_shape=[jax.ShapeDtypeStruct((S, 512), F32), jax.ShapeDtypeStruct((S, 512), F32),
                   jax.ShapeDtypeStruct((D, D), F32), jax.ShapeDtypeStruct((8, D), F32)],
        compiler_params=_params(("arbitrary",)),
    )(dxo, mo, oa, ob, w_o, vecs)


def _mix_in_bwd(h, w_inT, dq, dk, dv, dql, dkl, dkr, *, tm=ROW_TILE):
    S = h.shape[0]
    tm = min(tm, S)
    offs = (0, 512, 640, 768, 1024, 1152)
    wid = (512, 128, 128, 256, 128, 128)

    def body(h_ref, w_ref, dq_ref, dk_ref, dv_ref, dql_ref, dkl_ref, dkr_ref, dh_ref, gw_ref):
        @pl.when(pl.program_id(0) == 0)
        def _():
            gw_ref[...] = jnp.zeros_like(gw_ref)

        hv = h_ref[...]
        dh = jnp.zeros((tm, D), F32)
        for ref, o, w in zip((dq_ref, dk_ref, dv_ref, dql_ref, dkl_ref, dkr_ref), offs, wid):
            w = min(w, D_IN - o)
            dpart = ref[...][:, :w].astype(BF16)
            dh += _dot(dpart, w_ref[o:o + w, :])
            gw_ref[o:o + w, :] += _dot_tn(dpart, hv)
        dh_ref[...] = dh

    row = pl.BlockSpec((tm, D), lambda i: (i, 0))
    part = lambda w: pl.BlockSpec((tm, w), lambda i: (i, 0))
    return pl.pallas_call(
        body, name="mix_in_bwd", grid=(S // tm,),
        in_specs=[row, pl.BlockSpec((D_IN_PAD, D), lambda i: (0, 0))] + [part(w) for w in wid],
        out_specs=[row, pl.BlockSpec((D_IN, D), lambda i: (0, 0))],
        out_shape=[jax.ShapeDtypeStruct((S, D), F32), jax.ShapeDtypeStruct((D_IN, D), F32)],
        compiler_params=_params(("arbitrary",)),
    )(h, w_inT, dq, dk, dv, dql, dkl, dkr)


def _vecs(norm_w, mod9, k):
    return jnp.concatenate([norm_w.reshape(1, D), mod9[3 * k:3 * k + 3], jnp.zeros((4, D), F32)], axis=0)


def _uq_group_rows(wuqT):
    per = MLA_NOPE + MLA_ROPE
    nope = [wuqT[per * h:per * h + MLA_NOPE] for h in range(MLA_HEADS)]
    rope = [wuqT[per * h + MLA_NOPE:per * (h + 1)] for h in range(MLA_HEADS)]
    return jnp.concatenate(nope + rope, axis=0)


def _uq_ungroup_rows(g):
    parts = []
    for h in range(MLA_HEADS):
        parts += [g[MLA_NOPE * h:MLA_NOPE * (h + 1)], g[512 + MLA_ROPE * h:512 + MLA_ROPE * (h + 1)]]
    return jnp.concatenate(parts, axis=0)


def _local_step(x, tgt, mod9, norms, sinks, rel_bias, q_norm, kv_norm, W, on_grads=None):
    if on_grads is None:
        on_grads = lambda group, grads, after, vecs: vecs
    S = x.shape[0]
    v1 = _vecs(norms["ffn1"], mod9, 0)
    v2 = _vecs(norms["mix"], mod9, 1)
    v3 = _vecs(norms["ffn2"], mod9, 2)
    bucket = jnp.asarray(_bucket_table())
    cos, sin = _rope_tables(S)
    if isinstance(W, dict):
        full, W = W, (lambda group, after, vecs: (full, vecs))

    W1, v1 = W("ffn1", [], v1)
    x1, h1, a1, b1, f1 = _ffn_fwd(x, v1, W1["g1T"], W1["u1T"], W1["d1"], name="ffn1_fwd", tm=256, tf=D_FF)
    W2, v2 = W("mixer", [x1], v2)
    w_inT = jnp.pad(W2["w_inT"], ((0, D_IN_PAD - D_IN), (0, 0))).astype(BF16)
    wuqT = _uq_group_rows(W2["w_uqT"])
    h2, proj = _mix_in_fwd(x1, v2, w_inT)
    bias = _bias_build(rel_bias, bucket)
    oa = _swa_fwd(proj, bias, sinks)
    qc, kc, vv = _mla_pre_fwd(proj, q_norm, kv_norm, wuqT, W2["w_ukv"], cos, sin)
    ob, lse = _mla_attn_fwd(qc, kc, vv)
    _, v2o = W("ffn2_on_its_way", [ob], v2)
    x2, mo = _mix_out_fwd(x1, oa, ob, W2["w_o"], v2o)
    W3, v3 = W("ffn2", [x2], v3)
    x3, h3, a3, b3, f3 = _ffn_fwd(x2, v3, W3["g3T"], W3["u3T"], W3["d3"], name="ffn2_fwd", tm=256, tf=D_FF)
    dx3, head_part, df3 = _head(x3, tgt, norms["final"], f3, v3)

    gg3, gu3, gd3, dh3 = _ffn_bwd_main(h3, df3, a3, b3, W3["g3T"], W3["u3T"], W3["d3"], name="ffn2_bwd", tm=2048, tf=256)
    ffn2 = {"g3T": gg3, "u3T": gu3, "d3": gd3}
    v3 = on_grads("ffn2", ffn2, [], v3)
    dx2, n3_part = _norm_bwd(dh3, x2, dx3, v3, name="ffn2_norm_bwd")
    v2 = on_grads("ffn2", None, [dx2], v2)
    doa, dob, g_wo, g2_part = _mix_out_bwd(dx2, mo, oa, ob, W2["w_o"], v2)
    dq, dk, dv, drb, dsk = _swa_bwd(proj, bias, sinks, oa, doa, bucket)
    dqc, dkc, dvv = _mla_attn_bwd(qc, kc, vv, ob, lse, dob)
    dql, dkl, dkr, g_uq, g_ukv, mla_part = _mla_pre_bwd(proj, q_norm, kv_norm, wuqT, W2["w_ukv"], cos, sin, dqc, dkc, dvv)
    dh2, g_win = _mix_in_bwd(h2, w_inT, dq, dk, dv, dql, dkl, dkr)
    mixer = {"w_inT": g_win, "w_uqT": _uq_ungroup_rows(g_uq).astype(BF16),
             "w_ukv": g_ukv.astype(BF16), "w_o": g_wo.astype(BF16)}
    v2 = on_grads("mixer", mixer, [], v2)
    dx1, n2_part, df1 = _norm_bwd(dh2, x1, dx2, v2, name="mix_norm_bwd", below=(f1, v1))
    started = on_grads("mixer", None, [dx1], jnp.zeros((1, 1), F32))
    gg1, gu1, gd1, dh1 = _ffn_bwd_main(h1, df1, a1, b1, W1["g1T"], W1["u1T"], W1["d1"], name="ffn1_bwd",
                                       after=[started], tm=2048, tf=256)
    ffn1 = {"g1T": gg1, "u1T": gu1, "d1": gd1}
    v1 = on_grads("ffn1", ffn1, [], v1)
    dx0, n1_part = _norm_bwd(dh1, x, dx1, v1, name="ffn1_norm_bwd")

    grads = {**ffn1, **ffn2, **mixer}
    return head_part[1, 0], dx0, grads, _pack_vec(n1_part, n2_part, n3_part, head_part, g2_part, mla_part, dsk, drb)


SMALL_LAYOUT = (("norm_ffn1", 1024), ("norm_mix", 1024), ("norm_ffn2", 1024), ("norm_final", 1024),
                ("q_norm", 256), ("kv_norm", 128), ("sinks", 128), ("rel_bias", 256))
N_SMALL = sum(n for _, n in SMALL_LAYOUT)
LOSS_SLOT = 4 * 1024 + 256 + 128 + SWA_HEADS
N_MODVEC = N_MOD * D
N_VEC = N_MODVEC + N_SMALL


def _pack_vec(n1, n2, n3, head, g2, mla, dsk, drb):
    def body(n1_ref, n2_ref, n3_ref, head_ref, g2_ref, mla_ref, dsk_ref, drb_ref, out_ref):
        rows = [n1_ref[1:2, :], n1_ref[2:3, :], n2_ref[3:4, :], n2_ref[1:2, :], n2_ref[2:3, :], g2_ref[0:1, :],
                n3_ref[1:2, :], n3_ref[2:3, :], head_ref[3:4, :],
                n1_ref[0:1, :], n2_ref[0:1, :], n3_ref[0:1, :], head_ref[0:1, :]]
        for i, row in enumerate(rows):
            out_ref[:, D * i:D * (i + 1)] = row
        off = D * len(rows)
        out_ref[:, off:off + 256] = mla_ref[0:1, :]
        out_ref[:, off + 256:off + 384] = mla_ref[1:2, 0:128]

        def diagonal(block):
            r = lax.broadcasted_iota(jnp.int32, block.shape, 0)
            lane = lax.broadcasted_iota(jnp.int32, block.shape, 1)
            return jnp.sum(jnp.where(r == lane, block, 0.0), axis=0, keepdims=True)

        lane = lax.broadcasted_iota(jnp.int32, (1, 128), 1)
        out_ref[:, off + 384:off + 512] = jnp.where(lane == SWA_HEADS, head_ref[1:2, 0:128], diagonal(dsk_ref[...]))
        out_ref[:, off + 512:off + 640] = diagonal(drb_ref[0:128, :])
        out_ref[:, off + 640:off + 768] = diagonal(drb_ref[128:256, :])

    vm = pl.BlockSpec(memory_space=pltpu.VMEM)
    return pl.pallas_call(body, name="pack_vec", in_specs=[vm] * 8, out_specs=vm,
                          out_shape=jax.ShapeDtypeStruct((1, N_VEC), F32))(n1, n2, n3, head, g2, mla, dsk, drb)


def _coords():
    return lax.axis_index("x"), lax.axis_index("y"), lax.axis_index("c")


def _flip(v, bit):
    return 1 - v if bit else v


def _peer(r):
    x, y, c = _coords()
    return (_flip(x, r & 4), _flip(y, r & 2), _flip(c, r & 1))


def _mod_fwd(c_tile, w_mod, b_mod3):
    W = w_mod.shape[1]

    def body(c_ref, w_ref, b_ref, mod_ref, ca_ref, call_ref, part_ref, send_sems, recv_sems):
        x, y, c = _coords()
        me = 4 * x + 2 * y + c
        call_ref[me] = c_ref[...]
        sends = []
        for r in range(1, N_DEV):
            cp = pltpu.make_async_remote_copy(c_ref, call_ref.at[me], send_sems.at[0, r], recv_sems.at[0, r],
                                              device_id=_peer(r), device_id_type=MESH)
            cp.start()
            sends.append(cp)
        for r in range(1, N_DEV):
            pltpu.make_async_remote_copy(c_ref, call_ref.at[me], send_sems.at[0, r], recv_sems.at[0, r],
                                         device_id=_peer(r), device_id_type=MESH).wait_recv()
        cv = call_ref[...].reshape(8 * N_DEV, D)
        ca = (cv * _sigmoid(cv)).astype(BF16)
        ca_ref[...] = ca
        part_ref[...] = _dot(ca, w_ref[...].astype(BF16)).reshape(N_DEV, 8, W)
        mod_ref[me] = part_ref[me] + b_ref[me]
        for r in range(1, N_DEV):
            cp = pltpu.make_async_remote_copy(part_ref.at[me ^ r], mod_ref.at[me], send_sems.at[1, r],
                                              recv_sems.at[1, r], device_id=_peer(r), device_id_type=MESH)
            cp.start()
            sends.append(cp)
        for r in range(1, N_DEV):
            pltpu.make_async_remote_copy(part_ref.at[me ^ r], mod_ref.at[me], send_sems.at[1, r],
                                         recv_sems.at[1, r], device_id=_peer(r), device_id_type=MESH).wait_recv()
            mod_ref[me ^ r] = mod_ref[me ^ r] + b_ref[me ^ r]
        for cp in sends:
            cp.wait_send()

    vm = pl.BlockSpec(memory_space=pltpu.VMEM)
    return pl.pallas_call(
        body, name="mod_fwd", in_specs=[vm, vm, vm], out_specs=[vm, vm],
        out_shape=[jax.ShapeDtypeStruct((N_DEV, 8, W), F32), jax.ShapeDtypeStruct((8 * N_DEV, D), BF16)],
        scratch_shapes=[pltpu.VMEM((N_DEV, 8, D), F32), pltpu.VMEM((N_DEV, 8, W), F32),
                        pltpu.SemaphoreType.DMA((2, N_DEV)), pltpu.SemaphoreType.DMA((2, N_DEV))],
        compiler_params=_params(),
    )(c_tile, w_mod, b_mod3)


def _mod_bwd(allvec, ca, me_idx):
    W = N_MODVEC // N_DEV

    def body(me_ref, all_ref, cols_ref, ca_ref, gw_ref, sum_ref):
        in_first_row = lax.broadcasted_iota(jnp.int32, (N_DEV, 8, W), 1) == 0
        dm = jnp.where(in_first_row, cols_ref[...], 0.0).reshape(8 * N_DEV, W)
        gw_ref[...] = _dot_tn(ca_ref[...], dm.astype(BF16))
        total = all_ref[0]
        for k in range(1, N_DEV):
            total = total + all_ref[k]
        sum_ref[...] = total

    return pl.pallas_call(
        body, name="mod_bwd",
        grid_spec=pltpu.PrefetchScalarGridSpec(
            num_scalar_prefetch=1, grid=(1,),
            in_specs=[pl.BlockSpec((N_DEV, 1, N_VEC), lambda i, me: (0, 0, 0)),
                      pl.BlockSpec((N_DEV, 1, W), lambda i, me: (0, 0, me[0])),
                      pl.BlockSpec((8 * N_DEV, D), lambda i, me: (0, 0))],
            out_specs=[pl.BlockSpec((D, W), lambda i, me: (0, 0)), pl.BlockSpec((1, N_VEC), lambda i, me: (0, 0))]),
        out_shape=[jax.ShapeDtypeStruct((D, W), F32), jax.ShapeDtypeStruct((1, N_VEC), F32)],
        compiler_params=_params(("arbitrary",)),
    )(me_idx, allvec, allvec, ca)


def _wgather(shards):
    n = len(shards)
    rows = [s.shape[0] for s in shards]

    def body(*refs):
        ins, outs, token = refs[:n], refs[n:2 * n], refs[2 * n]
        send_sems, recv_sems, local_sems = refs[2 * n + 1:]
        token[...] = jnp.zeros_like(token)
        x, y, c = _coords()
        me = 4 * x + 2 * y + c
        sib, xn, yn = (x, y, 1 - c), (1 - x, y, c), (x, 1 - y, c)
        block = lambda px, py, pc: 4 * px + 2 * py + pc

        def part(k, blk, half):
            if half is None:
                return outs[k].at[blk]
            return outs[k].at[blk, pl.ds(half * (rows[k] // 2), rows[k] // 2)]

        def copy(k, slot, blk, to, half=None, src=None):
            ref = part(k, blk, half)
            return pltpu.make_async_remote_copy(
                src_ref=ref if src is None else src, dst_ref=ref, send_sem=send_sems.at[k, slot],
                recv_sem=recv_sems.at[k, slot], device_id=to, device_id_type=MESH)

        local = [pltpu.make_async_copy(ins[k], outs[k].at[me], local_sems.at[k]) for k in range(n)]
        for cp in local:
            cp.start()
        sent = [copy(k, slot, me, to, src=ins[k]) for k in range(n) for slot, to in ((0, sib), (1, xn), (2, yn))]
        for cp in sent:
            cp.start()
        bx, by, bd = block(1 - x, y, c), block(x, 1 - y, c), block(1 - x, 1 - y, c)
        for k in range(n):
            copy(k, 1, bx, sib).wait_recv()
            sent += [copy(k, 4, bx, yn, half=1), copy(k, 5, bx, sib)]
            sent[-2].start()
            sent[-1].start()
        for k in range(n):
            copy(k, 2, by, sib).wait_recv()
            sent += [copy(k, 3, by, xn, half=0), copy(k, 6, by, sib)]
            sent[-2].start()
            sent[-1].start()
        for k in range(n):
            copy(k, 3, bd, sib, half=0).wait_recv()
            copy(k, 4, bd, sib, half=1).wait_recv()
            sent.append(copy(k, 7, bd, sib))
            sent[-1].start()
        for k in range(n):
            copy(k, 0, block(x, y, 1 - c), sib).wait_recv()
            for slot, blk in ((5, block(1 - x, y, 1 - c)), (6, block(x, 1 - y, 1 - c)), (7, block(1 - x, 1 - y, 1 - c))):
                copy(k, slot, blk, sib).wait_recv()
        for cp in sent:
            cp.wait_send()
        for cp in local:
            cp.wait()

    anyspec = pl.BlockSpec(memory_space=pl.ANY)
    return pl.pallas_call(
        body, name="wgather", in_specs=[anyspec] * n,
        out_specs=[anyspec] * n + [pl.BlockSpec(memory_space=pltpu.VMEM)],
        out_shape=[jax.ShapeDtypeStruct((N_DEV,) + s.shape, s.dtype) for s in shards]
        + [jax.ShapeDtypeStruct((8, 128), F32)],
        scratch_shapes=[pltpu.SemaphoreType.DMA((n, 8)), pltpu.SemaphoreType.DMA((n, 8)),
                        pltpu.SemaphoreType.DMA((n,))],
    )(*shards)


class _GatherCopies:
    def __init__(self, lands, send_sems, recv_sems, k0=0, batches=None):
        x, y, c = _coords()
        me = 4 * x + 2 * y + c
        sib = (x, y, 1 - c)
        chips = [(1 - x, y), (x, 1 - y), (1 - x, 1 - y)]

        def copy(k, slot, block, to):
            return pltpu.make_async_remote_copy(
                src_ref=lands[k].at[block], dst_ref=lands[k].at[block],
                send_sem=send_sems.at[7 * (k0 + k) + slot], recv_sem=recv_sems.at[7 * (k0 + k) + slot],
                device_id=to, device_id_type=MESH)

        n = len(lands)
        self.first = [copy(k, 0, me, sib) for k in range(n)]
        for batch in batches or [range(n)]:
            self.first += [copy(k, 1 + j, me, (cx, cy, c)) for j, (cx, cy) in enumerate(chips) for k in batch]
        self.landed = [copy(k, 1 + j, 4 * cx + 2 * cy + c, sib) for j, (cx, cy) in enumerate(chips) for k in range(n)]
        self.passed = [copy(k, 4 + j, 4 * cx + 2 * cy + c, sib) for j, (cx, cy) in enumerate(chips) for k in range(n)]
        self.from_sib = [copy(k, 0, 4 * x + 2 * y + (1 - c), sib) for k in range(n)]
        self.from_sib += [copy(k, 4 + j, 4 * cx + 2 * cy + (1 - c), sib) for j, (cx, cy) in enumerate(chips)
                          for k in range(n)]


def _gather_start(lands, *, name, batches=None):
    n = len(lands)

    def body(*refs):
        for cp in _GatherCopies(refs[:n], refs[n], refs[n + 1], batches=batches).first:
            cp.start()
        refs[-1][...] = jnp.zeros_like(refs[-1])

    out = pl.pallas_call(
        body, name=name,
        out_shape=(pltpu.SemaphoreType.DMA((7 * n,)), pltpu.SemaphoreType.DMA((7 * n,)),
                   *[pltpu.HBM(l.shape, l.dtype) for l in lands], jax.ShapeDtypeStruct((8, 128), F32)),
        in_specs=[HBM_SPEC] * n,
        out_specs=(SEM_SPEC, SEM_SPEC, *[HBM_SPEC] * n, pl.BlockSpec(memory_space=pltpu.VMEM)),
        input_output_aliases={i: 2 + i for i in range(n)},
        compiler_params=pltpu.CompilerParams(has_side_effects=DATAFLOW),
    )(*[_in_hbm(l) for l in lands])
    return out[0], out[1], list(out[2:2 + n]), out[-1]


def _gather_pass(send_sems, recv_sems, lands, after, *, name, stage, k0=0):
    n = len(lands)

    def body(*refs):
        cps = _GatherCopies(refs[:n], refs[n], refs[n + 1], k0)
        if stage == "landed":
            for cp in cps.landed:
                cp.wait_recv()
        else:
            for cp in cps.passed:
                cp.start()
        refs[-1][...] = jnp.zeros_like(refs[-1])

    out = pl.pallas_call(
        body, name=name,
        out_shape=(*[pltpu.HBM(l.shape, l.dtype) for l in lands], jax.ShapeDtypeStruct((8, 128), F32)),
        in_specs=[HBM_SPEC] * n + [SEM_SPEC, SEM_SPEC] + [pl.BlockSpec(memory_space=pl.ANY)] * len(after),
        out_specs=(*[HBM_SPEC] * n, pl.BlockSpec(memory_space=pltpu.VMEM)),
        input_output_aliases={i: i for i in range(n)},
        compiler_params=pltpu.CompilerParams(has_side_effects=DATAFLOW),
    )(*lands, send_sems, recv_sems, *after)
    return list(out[:n]), out[-1]


def _gather_end(send_sems, recv_sems, lands, after, *, name, k0=0):
    n = len(lands)

    def body(*refs):
        cps = _GatherCopies(refs[:n], refs[n], refs[n + 1], k0)
        for cp in cps.from_sib:
            cp.wait_recv()
        for cp in cps.first + cps.passed:
            cp.wait_send()

    out = pl.pallas_call(
        body, name=name,
        out_shape=[pltpu.HBM(l.shape, l.dtype) for l in lands],
        in_specs=[HBM_SPEC] * n + [SEM_SPEC, SEM_SPEC] + [pl.BlockSpec(memory_space=pl.ANY)] * len(after),
        out_specs=[HBM_SPEC] * n,
        input_output_aliases={i: i for i in range(n)},
        compiler_params=pltpu.CompilerParams(has_side_effects=DATAFLOW),
    )(*lands, send_sems, recv_sems, *after)
    return list(out)


def _d2d_copies(grads, lands, send_sems, recv_sems):
    x, y, c = _coords()
    return [pltpu.make_async_remote_copy(
        src_ref=grads[k].at[2 * q + (1 - c)], dst_ref=lands[k].at[q],
        send_sem=send_sems.at[4 * k + q], recv_sem=recv_sems.at[4 * k + q],
        device_id=(x, y, 1 - c), device_id_type=MESH) for k in range(len(grads)) for q in range(4)]


def _vec_copies(srcs, lands, send_sems, recv_sems):
    x, y, c = _coords()
    me = 4 * x + 2 * y + c
    return [pltpu.make_async_remote_copy(
        src_ref=lands[0].at[me], dst_ref=lands[0].at[me], send_sem=send_sems.at[r - 1], recv_sem=recv_sems.at[r - 1],
        device_id=_peer(r), device_id_type=MESH) for r in range(1, N_DEV)]


def _chipsum(gs, sibs, cidx, *, name):
    n = len(gs)

    def body(c_ref, *refs):
        for k in range(n):
            refs[2 * n + k][...] = (refs[k][...].astype(F32) + refs[n + k][...].astype(F32)).astype(refs[2 * n + k].dtype)

    mine = [pl.BlockSpec((1,) + g.shape[1:], lambda q, c_ref: (2 * q + c_ref[0], 0, 0)) for g in gs]
    other = [pl.BlockSpec((1,) + g.shape[1:], lambda q, c_ref: (q, 0, 0)) for g in gs]
    return pl.pallas_call(
        body, name=name,
        grid_spec=pltpu.PrefetchScalarGridSpec(num_scalar_prefetch=1, grid=(4,), in_specs=mine + other, out_specs=other),
        out_shape=[jax.ShapeDtypeStruct((4,) + g.shape[1:], g.dtype) for g in gs],
        compiler_params=_params(("arbitrary",)),
    )(cidx, *gs, *sibs)


HBM_SPEC = pl.BlockSpec(memory_space=pltpu.HBM)
SEM_SPEC = pl.BlockSpec(memory_space=pltpu.SEMAPHORE)
DATAFLOW = pltpu.SideEffectType.DATAFLOW_SIDE_EFFECTING


def _in_hbm(a):
    return pltpu.with_memory_space_constraint(a, pltpu.HBM)


def _ici_copies(sums, lands, send_sems, recv_sems, k0=0):
    x, y, c = _coords()
    chips = [(1 - x, y), (x, 1 - y), (1 - x, 1 - y)]
    cps = []
    for k in range(len(sums)):
        for j, (cx, cy) in enumerate(chips):
            cps.append(pltpu.make_async_remote_copy(
                src_ref=sums[k].at[2 * cx + cy], dst_ref=lands[k].at[j],
                send_sem=send_sems.at[3 * (k0 + k) + j], recv_sem=recv_sems.at[3 * (k0 + k) + j],
                device_id=(cx, cy, c), device_id_type=MESH))
    return cps


def _split_start(copies, srcs, lands, n_sems, after, *, name):
    ns, nl = len(srcs), len(lands)

    def body(*refs):
        for cp in copies(refs[:ns], refs[ns:ns + nl], refs[ns + nl + len(after)], refs[ns + nl + len(after) + 1]):
            cp.start()
        refs[-1][...] = jnp.zeros_like(refs[-1])

    bufs = [_in_hbm(a) for a in list(srcs) + list(lands)]
    out = pl.pallas_call(
        body, name=name,
        out_shape=(pltpu.SemaphoreType.DMA((n_sems,)), pltpu.SemaphoreType.DMA((n_sems,)),
                   *[pltpu.HBM(a.shape, a.dtype) for a in bufs], jax.ShapeDtypeStruct((8, 128), F32)),
        in_specs=[HBM_SPEC] * len(bufs) + [pl.BlockSpec(memory_space=pl.ANY)] * len(after),
        out_specs=(SEM_SPEC, SEM_SPEC, *[HBM_SPEC] * len(bufs), pl.BlockSpec(memory_space=pltpu.VMEM)),
        input_output_aliases={i: 2 + i for i in range(len(bufs))},
        compiler_params=pltpu.CompilerParams(has_side_effects=DATAFLOW),
    )(*bufs, *after)
    return out[0], out[1], list(out[2:2 + ns]), list(out[2 + ns:2 + ns + nl]), out[-1]


def _split_wait(copies, send_sems, recv_sems, srcs, lands, after, *, name):
    ns, nl = len(srcs), len(lands)

    def body(*refs):
        for cp in copies(refs[:ns], refs[ns:ns + nl], refs[ns + nl], refs[ns + nl + 1]):
            cp.wait_send()
            cp.wait_recv()

    out = pl.pallas_call(
        body, name=name,
        out_shape=[pltpu.HBM(a.shape, a.dtype) for a in list(srcs) + list(lands)],
        in_specs=[HBM_SPEC] * (ns + nl) + [SEM_SPEC, SEM_SPEC] + [pl.BlockSpec(memory_space=pl.ANY)] * len(after),
        out_specs=[HBM_SPEC] * (ns + nl),
        input_output_aliases={i: i for i in range(ns + nl)},
        compiler_params=pltpu.CompilerParams(has_side_effects=DATAFLOW),
    )(*srcs, *lands, send_sems, recv_sems, *after)
    return list(out[:ns]), list(out[ns:])


ADAM_C1 = 1.0 / (1.0 - ADAM_B1 ** ADAM_STEP)
ADAM_C2 = 1.0 / (1.0 - ADAM_B2 ** ADAM_STEP)


def _adam_math(w, g, m, v):
    m2 = ADAM_B1 * m + (1.0 - ADAM_B1) * g
    v2 = ADAM_B2 * v + (1.0 - ADAM_B2) * (g * g)
    return -ADAM_LR * ((m2 * ADAM_C1) / (jnp.sqrt(v2 * ADAM_C2) + ADAM_EPS) + ADAM_WD * w), m2, v2


def _adamw(w, g, m, v, *, name):
    R, C = w.shape
    tr = R if R <= 512 else 256

    def body(w_ref, g_ref, m_ref, v_ref, d_ref, nm_ref, nv_ref):
        d_ref[...], nm_ref[...], nv_ref[...] = _adam_math(w_ref[...], g_ref[...], m_ref[...], v_ref[...])

    blk = pl.BlockSpec((tr, C), lambda i: (i, 0))
    return pl.pallas_call(
        body, name=name, grid=(R // tr,), in_specs=[blk] * 4, out_specs=[blk] * 3,
        out_shape=[jax.ShapeDtypeStruct((R, C), F32)] * 3,
        compiler_params=_params(("parallel",)),
    )(w, g, m, v)


def _adamw_rs(wmv, cs, rcv, qidx, *, name):
    n = len(wmv)
    r, cc = wmv[0][0].shape
    tr = r // 2 if r % 32 == 0 and r > 128 else r

    def body(q_ref, *refs):
        ins, outs = refs[:5 * n], refs[5 * n:]
        for k in range(n):
            w_ref, m_ref, v_ref, c_ref, r_ref = ins[5 * k:5 * k + 5]
            g_ref, d_ref, nm_ref, nv_ref = outs[4 * k:4 * k + 4]
            g = ((c_ref[0].astype(F32) + r_ref[0].astype(F32)) + r_ref[1].astype(F32)) + r_ref[2].astype(F32)
            g_ref[...] = g
            d_ref[...], nm_ref[...], nv_ref[...] = _adam_math(w_ref[...], g, m_ref[...], v_ref[...])

    blk = pl.BlockSpec((tr, cc), lambda i, q_ref: (i, 0))
    one = [blk, blk, blk, pl.BlockSpec((1, tr, cc), lambda i, q_ref: (q_ref[0], i, 0)),
           pl.BlockSpec((3, tr, cc), lambda i, q_ref: (0, i, 0))]
    out = pl.pallas_call(
        body, name=name,
        grid_spec=pltpu.PrefetchScalarGridSpec(num_scalar_prefetch=1, grid=(r // tr,), in_specs=one * n,
                                               out_specs=[blk] * (4 * n)),
        out_shape=[jax.ShapeDtypeStruct((r, cc), F32)] * (4 * n),
        compiler_params=_params(("arbitrary",)),
    )(qidx, *[a for (w, m, v), c, rc in zip(wmv, cs, rcv) for a in (w, m, v, c, rc)])
    return [tuple(out[4 * k:4 * k + 4]) for k in range(n)]


SMALL_PARAMS = ("norm_ffn1", "norm_mix", "norm_ffn2", "norm_final", "q_norm", "kv_norm", "sinks", "rel_bias", "b_mod")


def _adamw_small(gvec, wmv):
    widths = [wmv[3 * i].shape[1] for i in range(len(SMALL_PARAMS))]

    def body(*refs):
        g_all = refs[0]
        ins = refs[1:1 + 3 * len(SMALL_PARAMS)]
        outs = refs[1 + 3 * len(SMALL_PARAMS):]
        off = N_MODVEC
        for i, name in enumerate(SMALL_PARAMS):
            g_ref, d_ref, nm_ref, nv_ref = outs[4 * i:4 * i + 4]
            w_ref, m_ref, v_ref = ins[3 * i:3 * i + 3]
            start = 0 if name == "b_mod" else off
            g = g_all[:, start:start + widths[i]]
            g_ref[...] = g
            d_ref[...], nm_ref[...], nv_ref[...] = _adam_math(w_ref[...], g, m_ref[...], v_ref[...])
            if name != "b_mod":
                off += dict(SMALL_LAYOUT)[name]

    vm = pl.BlockSpec(memory_space=pltpu.VMEM)
    n_out = 4 * len(SMALL_PARAMS)
    out = pl.pallas_call(
        body, name="adamw_small", in_specs=[vm] * (1 + len(wmv)), out_specs=[vm] * n_out,
        out_shape=[jax.ShapeDtypeStruct((1, widths[i // 4]), F32) for i in range(n_out)],
        compiler_params=_params(),
    )(gvec, *wmv)
    return {name: out[4 * i:4 * i + 4] for i, name in enumerate(SMALL_PARAMS)}


TRANSPOSED = ("g1T", "u1T", "g3T", "u3T", "w_inT", "w_uqT")


def kernel(x, c, w_mod, b_mod, norm_ffn1, ffn1_gate, ffn1_up, ffn1_down, norm_mix, w_in, q_norm, kv_norm, w_uq, w_ukv, sinks, w_o, norm_ffn2, ffn2_gate, ffn2_up, ffn2_down, rel_bias, norm_final, loss_target, m_w_mod, m_b_mod, m_norm_ffn1, m_ffn1_gate, m_ffn1_up, m_ffn1_down, m_norm_mix, m_w_in, m_q_norm, m_kv_norm, m_w_uq, m_w_ukv, m_sinks, m_w_o, m_norm_ffn2, m_ffn2_gate, m_ffn2_up, m_ffn2_down, m_rel_bias, m_norm_final, v_w_mod, v_b_mod, v_norm_ffn1, v_ffn1_gate, v_ffn1_up, v_ffn1_down, v_norm_mix, v_w_in, v_q_norm, v_kv_norm, v_w_uq, v_w_ukv, v_sinks, v_w_o, v_norm_ffn2, v_ffn2_gate, v_ffn2_up, v_ffn2_down, v_rel_bias, v_norm_final):
    mx, my, mc = _coords()
    cidx = jnp.reshape(mc, (1,)).astype(jnp.int32)
    qidx = jnp.reshape(2 * mx + my, (1,)).astype(jnp.int32)
    WM = w_mod.shape[2]

    c_tile = jnp.pad(c, ((0, 7), (0, 0)))
    b_mod3 = jnp.pad(b_mod.reshape(N_DEV, 1, WM), ((0, 0), (0, 7), (0, 0)))
    mod3, ca = _mod_fwd(c_tile, w_mod[0], b_mod3)
    mod9 = mod3[:, 0, :].reshape(N_MOD, D)

    shards = {"g1T": ffn1_gate[0].T.astype(BF16), "u1T": ffn1_up[0].T.astype(BF16), "d1": ffn1_down[0].astype(BF16),
              "g3T": ffn2_gate[0].T.astype(BF16), "u3T": ffn2_up[0].T.astype(BF16), "d3": ffn2_down[0].astype(BF16),
              "w_inT": w_in[0].T, "w_uqT": w_uq[0].T.astype(BF16), "w_ukv": w_ukv[0].astype(BF16),
              "w_o": w_o[0].astype(BF16)}
    me = 4 * mx + 2 * my + mc
    groups = {"ffn1": ("g1T", "u1T", "d1"), "mixer": ("w_inT", "w_uqT", "w_ukv", "w_o"), "ffn2": ("g3T", "u3T", "d3")}
    arriving = {}

    def as_weights(group, gathered):
        return {k: g if k == "w_ukv" else g.reshape(N_DEV * g.shape[1], g.shape[2])
                for k, g in zip(groups[group], gathered)}

    later = groups["mixer"] + groups["ffn2"]
    place = {"mixer": 0, "ffn2": len(groups["mixer"])}

    def start_gather(token):
        lands = []
        for k in later:
            sh = shards[k] + token[0, 0].astype(shards[k].dtype)
            lands.append(lax.dynamic_update_slice(lax.empty((N_DEV,) + sh.shape, sh.dtype), sh[None], (me, 0, 0)))
        batches = [range(k0, k0 + len(groups[group])) for group, k0 in place.items()]
        send, recv, lands, started = _gather_start(lands, name="gather_start", batches=batches)
        for group, k0 in place.items():
            arriving[group] = (send, recv, lands[k0:k0 + len(groups[group])])
        return started

    def fetch(group, after, vecs):
        if group == "ffn1":
            *gathered, token = _wgather([shards[k] + ca[1, 0].astype(shards[k].dtype) for k in groups["ffn1"]])
            return as_weights("ffn1", gathered), vecs + start_gather(token)[0:1, 0:1]

        def pass_on(group, after):
            send, recv, lands = arriving[group]
            lands, token = _gather_pass(send, recv, lands, after, name="gather_landed_" + group, stage="landed",
                                        k0=place[group])
            lands, token = _gather_pass(send, recv, lands, [token], name="gather_onward_" + group, stage="onward",
                                        k0=place[group])
            arriving[group] = (send, recv, lands)
            return token

        if group == "ffn2_on_its_way":
            return None, vecs + pass_on("ffn2", after)[0:1, 0:1]
        if group == "mixer":
            after = [pass_on("mixer", after)]
        send, recv, lands = arriving[group]
        return as_weights(group, _gather_end(send, recv, lands, after, name="gather_end_" + group,
                                             k0=place[group])), vecs

    norms ={"ffn1": norm_ffn1, "mix": norm_mix, "ffn2": norm_ffn2, "final": norm_final.reshape(1, D)}
    in_flight = {}

    def on_grads(group, g, after, vecs, before_ici=()):
        if g is not None:
            names = list(g)
            by_dest = [g[k] if k == "w_ukv" else g[k].reshape((N_DEV, g[k].shape[0] // N_DEV) + g[k].shape[1:])
                       for k in names]
            lands = [lax.empty((4,) + a.shape[1:], a.dtype) for a in by_dest]
            send, recv, by_dest, lands, token = _split_start(_d2d_copies, by_dest, lands, 4 * len(names), after,
                                                             name="rs_d2d_start_" + group)
            in_flight[group] = (names, send, recv, by_dest, lands)
            return vecs + token[0:1, 0:1]
        names, send, recv, by_dest, lands = in_flight[group]
        by_dest, from_sib = _split_wait(_d2d_copies, send, recv, by_dest, lands, after, name="rs_d2d_wait_" + group)
        sums = _chipsum(by_dest, from_sib, cidx, name="chipsum_" + group)
        lands = [lax.empty((3,) + s.shape[1:], s.dtype) for s in sums]
        send, recv, sums, lands, token = _split_start(_ici_copies, sums, lands, 3 * len(names), list(before_ici),
                                                      name="rs_ici_start_" + group)
        in_flight[group] = (names, send, recv, sums, lands, token)
        return vecs + token[0:1, 0:1]

    _, grad_x, _, vec = _local_step(
        x[0], loss_target[0], mod9, norms, sinks, rel_bias, q_norm, kv_norm, fetch, on_grads=on_grads)

    vec = vec.reshape(1, 1, N_VEC)
    allvec = lax.dynamic_update_slice(lax.empty((N_DEV, 1, N_VEC), F32), vec, (me, 0, 0))
    vsend, vrecv, _, (allvec,), vec_started = _split_start(_vec_copies, [], [allvec], N_DEV - 1, [], name="vec_start")
    on_grads("ffn1", None, [grad_x], jnp.zeros((1, 1), F32), before_ici=[vec_started])

    owners = {"g1T": ("ffn1_gate", ffn1_gate, m_ffn1_gate, v_ffn1_gate), "u1T": ("ffn1_up", ffn1_up, m_ffn1_up, v_ffn1_up),
              "d1": ("ffn1_down", ffn1_down, m_ffn1_down, v_ffn1_down),
              "g3T": ("ffn2_gate", ffn2_gate, m_ffn2_gate, v_ffn2_gate), "u3T": ("ffn2_up", ffn2_up, m_ffn2_up, v_ffn2_up),
              "d3": ("ffn2_down", ffn2_down, m_ffn2_down, v_ffn2_down),
              "w_inT": ("w_in", w_in, m_w_in, v_w_in), "w_uqT": ("w_uq", w_uq, m_w_uq, v_w_uq),
              "w_ukv": ("w_ukv", w_ukv, m_w_ukv, v_w_ukv), "w_o": ("w_o", w_o, m_w_o, v_w_o)}
    res, done = {}, []

    def finish(group, after, one_by_one=False):
        names, send, recv, sums, lands, _ = in_flight[group]
        there = lambda k, a: a[0].T if k in TRANSPOSED else a[0]
        back = lambda k, a: a.T[None] if k in TRANSPOSED else a[None]
        wmv = [tuple(there(k, a) for a in owners[k][1:]) for k in names]
        if one_by_one:
            outs = []
            for i, k in enumerate(names):
                (cs,), (rc,) = _split_wait(functools.partial(_ici_copies, k0=i), send, recv, [sums[i]], [lands[i]],
                                           after, name="rs_ici_wait_" + k)
                outs.append(_adamw_rs([wmv[i]], [cs], [rc], qidx, name="adamw_" + owners[k][0])[0])
                after = [outs[-1][3]]
        else:
            sums, lands = _split_wait(_ici_copies, send, recv, sums, lands, after, name="rs_ici_wait_" + group)
            if len({w.shape for w, _, _ in wmv}) == 1:
                outs = _adamw_rs(wmv, sums, lands, qidx, name="adamw_" + group)
            else:
                outs = [_adamw_rs([t], [cs], [rc], qidx, name="adamw_" + owners[k][0])[0]
                        for k, t, cs, rc in zip(names, wmv, sums, lands)]
        for k, out in zip(names, outs):
            done.append(out[3])
            res[owners[k][0]] = tuple(back(k, a) for a in out)

    ffn1_started = in_flight["ffn1"][5]
    finish("ffn2", [ffn1_started])
    finish("mixer", [ffn1_started])

    _, (allvec,) = _split_wait(_vec_copies, vsend, vrecv, [], [allvec], [ffn1_started], name="vec_wait")
    g_wmod, gvec = _mod_bwd(allvec, ca, jnp.reshape(me, (1,)).astype(jnp.int32))
    loss = gvec[0, N_MODVEC + LOSS_SLOT]
    res["w_mod"] = tuple(a[None] for a in (g_wmod,) + tuple(_adamw(w_mod[0], g_wmod, m_w_mod[0], v_w_mod[0],
                                                                    name="adamw_w_mod")))
    small_in = {"norm_ffn1": (norm_ffn1, m_norm_ffn1, v_norm_ffn1), "norm_mix": (norm_mix, m_norm_mix, v_norm_mix),
                "norm_ffn2": (norm_ffn2, m_norm_ffn2, v_norm_ffn2), "norm_final": (norm_final, m_norm_final, v_norm_final),
                "q_norm": (q_norm, m_q_norm, v_q_norm), "kv_norm": (kv_norm, m_kv_norm, v_kv_norm),
                "sinks": (sinks, m_sinks, v_sinks), "rel_bias": (rel_bias, m_rel_bias, v_rel_bias),
                "b_mod": (b_mod, m_b_mod, v_b_mod)}
    small_out = _adamw_small(gvec, [a.reshape(1, -1) for k in SMALL_PARAMS for a in small_in[k]])
    for k in SMALL_PARAMS:
        res[k] = tuple(a.reshape(small_in[k][0].shape) for a in small_out[k])

    finish("ffn1", done + [res["w_mod"][3], small_out["b_mod"][3]], one_by_one=True)

    order = ("w_mod", "b_mod", "norm_ffn1", "ffn1_gate", "ffn1_up", "ffn1_down", "norm_mix", "w_in", "q_norm",
             "kv_norm", "w_uq", "w_ukv", "sinks", "w_o", "norm_ffn2", "ffn2_gate", "ffn2_up", "ffn2_down",
             "rel_bias", "norm_final")
    return (loss, grad_x[None]) + tuple(res[nm][kind] for kind in range(4) for nm in order)
```

```python
import functools
import math

import numpy as np
import jax
import jax.numpy as jnp
from jax import lax
from jax.experimental import pallas as pl
from jax.experimental.pallas import tpu as pltpu

F32 = jnp.float32
BF16 = jnp.bfloat16
MESH = pl.DeviceIdType.MESH

N_DEV = 8
D = 1024
D_FF = 2816
EPS = 1e-6
N_MOD = 9
SWA_HEADS = 8
SWA_DH = 64
WINDOW = 128
MLA_HEADS = 4
MLA_NOPE = 128
MLA_ROPE = 64
MLA_V = 128
MLA_QR = 256
MLA_KVR = 128
ROPE_THETA = 10000.0
NUM_BUCKETS = 32
D_IN = 1216
D_IN_PAD = 1280
SWA_SCALE = SWA_DH ** -0.5
MLA_SCALE = (MLA_NOPE + MLA_ROPE) ** -0.5

ADAM_LR = 0.001
ADAM_B1 = 0.9
ADAM_B2 = 0.999
ADAM_EPS = 1e-08
ADAM_WD = 0.01
ADAM_STEP = 10

V7X_VMEM_LIMIT = 56 * 1024 * 1024
ROW_TILE = 512

NT_DIMS = (((1,), (1,)), ((), ()))
TN_DIMS = (((0,), (0,)), ((), ()))


def _dot(a, b):
    return jnp.dot(a, b, preferred_element_type=F32)


def _dot_nt(a, b):
    return lax.dot_general(a, b, NT_DIMS, preferred_element_type=F32)


def _dot_tn(a, b):
    return lax.dot_general(a, b, TN_DIMS, preferred_element_type=F32)


def _params(sem=None):
    return pltpu.CompilerParams(dimension_semantics=sem, vmem_limit_bytes=V7X_VMEM_LIMIT)


def _rstd(x):
    return lax.rsqrt(jnp.mean(x * x, axis=-1, keepdims=True) + EPS)


def _rms_bwd(dy, xhat, r):
    return r * (dy - xhat * jnp.mean(dy * xhat, axis=-1, keepdims=True))


def _sigmoid(a):
    return 1.0 / (1.0 + jnp.exp(-a))


def _ffn_fwd(x, vecs, wgT, wuT, wd, *, name, tm=256, tf=D_FF):
    S, F = x.shape[0], wd.shape[0]
    tm = min(tm, S)
    ni, nj = S // tm, F // tf

    def body(x_ref, vec_ref, wg_ref, wu_ref, wd_ref, xo_ref, h_ref, a_ref, b_ref, f_ref, acc_ref):
        j = pl.program_id(1)

        @pl.when(j == 0)
        def _():
            xv = x_ref[...]
            hn = xv * _rstd(xv) * vec_ref[0:1, :]
            h_ref[...] = (hn * (1.0 + vec_ref[2:3, :]) + vec_ref[1:2, :]).astype(BF16)

        h = h_ref[...]
        a = _dot_nt(h, wg_ref[...])
        b = _dot_nt(h, wu_ref[...])
        a_ref[...] = a.astype(BF16)
        b_ref[...] = b.astype(BF16)
        part = _dot((a * _sigmoid(a) * b).astype(BF16), wd_ref[...])

        def finish(f):
            f_ref[...] = f
            xo_ref[...] = x_ref[...] + (0.5 * vec_ref[3:4, :]) * f

        if nj == 1:
            finish(part)
        else:
            @pl.when(j == 0)
            def _():
                acc_ref[...] = part

            @pl.when((j > 0) & (j < nj - 1))
            def _():
                acc_ref[...] += part

            @pl.when(j == nj - 1)
            def _():
                finish(acc_ref[...] + part)

    row = pl.BlockSpec((tm, D), lambda i, j: (i, 0))
    wspec = pl.BlockSpec((tf, D), lambda i, j: (j, 0), pipeline_mode=pl.Buffered(1) if nj == 1 else None)
    act = pl.BlockSpec((tm, tf), lambda i, j: (i, j))
    return pl.pallas_call(
        body, name=name, grid=(ni, nj),
        in_specs=[row, pl.BlockSpec((8, D), lambda i, j: (0, 0)), wspec, wspec, wspec],
        out_specs=[row, row, act, act, row],
        out_shape=[jax.ShapeDtypeStruct((S, D), F32), jax.ShapeDtypeStruct((S, D), BF16),
                   jax.ShapeDtypeStruct((S, F), BF16), jax.ShapeDtypeStruct((S, F), BF16),
                   jax.ShapeDtypeStruct((S, D), F32)],
        scratch_shapes=[pltpu.VMEM((tm, D) if nj > 1 else (8, 128), F32)],
        compiler_params=_params(("parallel", "arbitrary")),
    )(x, vecs, wgT, wuT, wd)


def _ffn_bwd_main(h, df, a, b, wgT, wuT, wd, *, name, after=(), tm=2048, tf=256):
    S = h.shape[0]
    tm = min(tm, S)
    ni, nj = S // tm, D_FF // tf

    def body(h_hbm, df_hbm, a_ref, b_ref, wg_ref, wu_ref, wd_ref, *rest):
        gg_ref, gu_ref, gd_ref, dh_hbm, h_v, df_v, dh_v, gg_acc, gu_acc, gd_acc, sem = rest[len(after):]
        j = pl.program_id(0)
        i = pl.program_id(1)

        @pl.when((j == 0) & (i == 0))
        def _():
            c1 = pltpu.make_async_copy(h_hbm, h_v, sem.at[0])
            c2 = pltpu.make_async_copy(df_hbm, df_v, sem.at[1])
            c1.start()
            c2.start()
            c1.wait()
            c2.wait()

        @pl.when(i == 0)
        def _():
            gg_acc[...] = jnp.zeros_like(gg_acc)
            gu_acc[...] = jnp.zeros_like(gu_acc)
            gd_acc[...] = jnp.zeros_like(gd_acc)

        rows = pl.ds(pl.multiple_of(i * tm, tm), tm)
        hi = h_v[rows, :]
        dfi = df_v[rows, :]
        av = a_ref[...].astype(F32)
        bv = b_ref[...].astype(F32)
        sg = _sigmoid(av)
        sa = av * sg
        hsw = (sa * bv).astype(BF16)
        dhsw = _dot_nt(dfi, wd_ref[...])
        da = (dhsw * bv * (sg * (1.0 + av * (1.0 - sg)))).astype(BF16)
        db = (dhsw * sa).astype(BF16)
        gd_acc[...] += _dot_tn(hsw, dfi)
        gg_acc[...] += _dot_tn(da, hi)
        gu_acc[...] += _dot_tn(db, hi)
        dh = _dot(da, wg_ref[...]) + _dot(db, wu_ref[...])

        @pl.when(j == 0)
        def _():
            dh_v[rows, :] = dh

        @pl.when(j > 0)
        def _():
            dh_v[rows, :] += dh

        @pl.when(i == ni - 1)
        def _():
            gg_ref[...] = gg_acc[...].astype(BF16)
            gu_ref[...] = gu_acc[...].astype(BF16)
            gd_ref[...] = gd_acc[...].astype(BF16)

        @pl.when((j == nj - 1) & (i == ni - 1))
        def _():
            c3 = pltpu.make_async_copy(dh_v, dh_hbm, sem.at[2])
            c3.start()
            c3.wait()

    anyspec = pl.BlockSpec(memory_space=pl.ANY)
    wspec = pl.BlockSpec((tf, D), lambda j, i: (j, 0))
    act = pl.BlockSpec((tm, tf), lambda j, i: (i, j))
    return pl.pallas_call(
        body, name=name, grid=(nj, ni),
        in_specs=[anyspec, anyspec, act, act, wspec, wspec, wspec] + [anyspec] * len(after),
        out_specs=[wspec, wspec, wspec, anyspec],
        out_shape=[jax.ShapeDtypeStruct((D_FF, D), BF16)] * 3 + [jax.ShapeDtypeStruct((S, D), F32)],
        scratch_shapes=[pltpu.VMEM((S, D), BF16), pltpu.VMEM((S, D), BF16), pltpu.VMEM((S, D), F32),
                        pltpu.VMEM((tf, D), F32), pltpu.VMEM((tf, D), F32), pltpu.VMEM((tf, D), F32),
                        pltpu.SemaphoreType.DMA((3,))],
        compiler_params=_params(("arbitrary", "arbitrary")),
    )(h, df, a, b, wgT, wuT, wd, *after)


def _ffn_out_bwd(dx, f, gate, df_ref, part_ref):
    df_ref[...] = ((0.5 * gate) * dx).astype(BF16)
    part_ref[3:4, :] += 0.5 * jnp.sum(dx * f, axis=0, keepdims=True)


def _norm_bwd(dh, x, dxo, vecs, *, name, below=None, tm=ROW_TILE):
    S = x.shape[0]
    tm = min(tm, S)

    def body(dh_ref, x_ref, dxo_ref, vec_ref, *rest):
        dx_ref, part_ref = rest[-2 if below is None else -3], rest[-1 if below is None else -2]

        @pl.when(pl.program_id(0) == 0)
        def _():
            part_ref[...] = jnp.zeros_like(part_ref)

        dh = dh_ref[...]
        xv = x_ref[...]
        r = _rstd(xv)
        xhat = xv * r
        w = vec_ref[0:1, :]
        xn = xhat * w
        dxn = dh * (1.0 + vec_ref[2:3, :])
        part_ref[0:1, :] += jnp.sum(dxn * xhat, axis=0, keepdims=True)
        part_ref[1:2, :] += jnp.sum(dh, axis=0, keepdims=True)
        part_ref[2:3, :] += jnp.sum(dh * xn, axis=0, keepdims=True)
        dx = dxo_ref[...] + _rms_bwd(dxn * w, xhat, r)
        dx_ref[...] = dx
        if below is not None:
            _ffn_out_bwd(dx, rest[0][...], rest[1][3:4, :], rest[-1], part_ref)

    row = pl.BlockSpec((tm, D), lambda i: (i, 0))
    vec = pl.BlockSpec((8, D), lambda i: (0, 0))
    extra = [] if below is None else [row, vec]
    return pl.pallas_call(
        body, name=name, grid=(S // tm,), in_specs=[row, row, row, vec] + extra,
        out_specs=[row, vec] + ([] if below is None else [row]),
        out_shape=[jax.ShapeDtypeStruct((S, D), F32), jax.ShapeDtypeStruct((8, D), F32)]
        + ([] if below is None else [jax.ShapeDtypeStruct((S, D), BF16)]),
        compiler_params=_params(("arbitrary",)),
    )(dh, x, dxo, vecs, *([] if below is None else below))


def _head(x, tgt, nf, f, vecs, *, tm=ROW_TILE):
    S = x.shape[0]
    tm = min(tm, S)

    def body(x_ref, t_ref, nf_ref, f_ref, vec_ref, dx_ref, part_ref, df_ref):
        @pl.when(pl.program_id(0) == 0)
        def _():
            part_ref[...] = jnp.zeros_like(part_ref)

        xv = x_ref[...]
        r = _rstd(xv)
        xhat = xv * r
        w = nf_ref[...]
        e = xhat * w - t_ref[...]
        dy = e * (1.0 / D)
        part_ref[0:1, :] += jnp.sum(dy * xhat, axis=0, keepdims=True)
        part_ref[1:2, :] += jnp.sum(e * e) * (0.5 / D)
        dx = _rms_bwd(dy * w, xhat, r)
        dx_ref[...] = dx
        _ffn_out_bwd(dx, f_ref[...], vec_ref[3:4, :], df_ref, part_ref)

    row = pl.BlockSpec((tm, D), lambda i: (i, 0))
    vec = pl.BlockSpec((8, D), lambda i: (0, 0))
    return pl.pallas_call(
        body, name="head", grid=(S // tm,),
        in_specs=[row, row, pl.BlockSpec((1, D), lambda i: (0, 0)), row, vec],
        out_specs=[row, vec, row],
        out_shape=[jax.ShapeDtypeStruct((S, D), F32), jax.ShapeDtypeStruct((8, D), F32),
                   jax.ShapeDtypeStruct((S, D), BF16)],
        compiler_params=_params(("arbitrary",)),
    )(x, tgt, nf, f, vecs)


def _mix_in_fwd(x, vecs, w_inT, *, tm=ROW_TILE):
    S = x.shape[0]
    tm = min(tm, S)

    def body(x_ref, vec_ref, w_ref, h_ref, p_ref):
        xv = x_ref[...]
        hn = xv * _rstd(xv) * vec_ref[0:1, :]
        h = (hn * (1.0 + vec_ref[2:3, :]) + vec_ref[1:2, :]).astype(BF16)
        h_ref[...] = h
        p_ref[...] = _dot_nt(h, w_ref[...])

    row = pl.BlockSpec((tm, D), lambda i: (i, 0))
    return pl.pallas_call(
        body, name="mix_in_fwd", grid=(S // tm,),
        in_specs=[row, pl.BlockSpec((8, D), lambda i: (0, 0)), pl.BlockSpec((D_IN_PAD, D), lambda i: (0, 0))],
        out_specs=[row, pl.BlockSpec((tm, D_IN_PAD), lambda i: (i, 0))],
        out_shape=[jax.ShapeDtypeStruct((S, D), BF16), jax.ShapeDtypeStruct((S, D_IN_PAD), F32)],
        compiler_params=_params(("parallel",)),
    )(x, vecs, w_inT)


def _bucket_table():
    qi = np.arange(WINDOW)[:, None]
    kj = np.arange(2 * WINDOW)[None, :]
    dist = qi + WINDOW - kj
    max_exact = NUM_BUCKETS // 2
    n = np.maximum(dist, 0)
    nf = np.maximum(n, 1).astype(np.float32)
    large = max_exact + (np.log(nf / np.float32(max_exact)) / np.float32(math.log(WINDOW / max_exact))
                         * np.float32(NUM_BUCKETS - max_exact)).astype(np.int32)
    large = np.minimum(large, NUM_BUCKETS - 1)
    return np.where(n < max_exact, n, large).astype(np.int32)


def _bias_build(rel_bias, bucket):
    def body(rb_ref, bk_ref, out_ref):
        bk = bk_ref[...]
        for h in range(SWA_HEADS):
            acc = jnp.zeros((WINDOW, 2 * WINDOW), F32)
            for b in range(NUM_BUCKETS):
                acc = jnp.where(bk == b, rb_ref[b, h], acc)
            out_ref[h] = acc

    return pl.pallas_call(
        body, name="bias_build",
        in_specs=[pl.BlockSpec(memory_space=pltpu.SMEM), pl.BlockSpec(memory_space=pltpu.VMEM)],
        out_specs=pl.BlockSpec(memory_space=pltpu.VMEM),
        out_shape=jax.ShapeDtypeStruct((SWA_HEADS, WINDOW, 2 * WINDOW), F32),
    )(rel_bias, bucket)


SWA_GROUP = 4
GROUP_ROWS = SWA_GROUP * WINDOW


SWA_SUB = 2


def _swa_valid(has_prev):
    row = lax.broadcasted_iota(jnp.int32, (GROUP_ROWS, 2 * WINDOW), 0) % WINDOW
    col = lax.broadcasted_iota(jnp.int32, (GROUP_ROWS, 2 * WINDOW), 1)
    dist = row + WINDOW - col
    return (dist >= 0) & (dist < WINDOW) & ((col >= WINDOW) | has_prev)


def _swa_keys(prev_ref, cur_ref, u):
    cur = cur_ref[...]
    before = prev_ref[...] if u == 0 else cur[WINDOW * (u - 1):WINDOW * u]
    return jnp.concatenate([before, cur[WINDOW * u:WINDOW * (u + 1)]], axis=0).astype(BF16)


def _stack_heads(x, g):
    return jnp.concatenate([x[:, 64 * h:64 * h + 64] for h in range(SWA_GROUP * g, SWA_GROUP * (g + 1))], axis=0)


def _unstack_heads(x4):
    return jnp.concatenate([x4[WINDOW * a:WINDOW * (a + 1)] for a in range(SWA_GROUP)], axis=1)


def _group_sinks(sink_ref, g):
    head = lax.broadcasted_iota(jnp.int32, (GROUP_ROWS, 1), 0) // WINDOW
    out = jnp.full((GROUP_ROWS, 1), sink_ref[0, SWA_GROUP * g], F32)
    for a in range(1, SWA_GROUP):
        out = jnp.where(head == a, sink_ref[0, SWA_GROUP * g + a], out)
    return out


def _swa_probs(qh, kk, bias_h, sink, valid):
    s = _dot_nt(qh, kk) * SWA_SCALE + bias_h
    s = jnp.where(valid, s, -jnp.inf)
    m = jnp.maximum(jnp.max(s, axis=-1, keepdims=True), sink)
    p = jnp.exp(s - m)
    ps = jnp.exp(sink - m)
    inv = 1.0 / (jnp.sum(p, axis=-1, keepdims=True) + ps)
    return p * inv, ps * inv


SWA_ROWS = SWA_SUB * WINDOW


def _swa_specs():
    prev = lambda n: jnp.maximum(SWA_SUB * n - 1, 0)
    return [pl.BlockSpec((SWA_ROWS, 512), lambda n: (n, 0)),
            pl.BlockSpec((SWA_ROWS, 128), lambda n: (n, 4)),
            pl.BlockSpec((WINDOW, 128), lambda n: (prev(n), 4)),
            pl.BlockSpec((SWA_ROWS, 128), lambda n: (n, 5)),
            pl.BlockSpec((WINDOW, 128), lambda n: (prev(n), 5)),
            pl.BlockSpec((SWA_HEADS, WINDOW, 2 * WINDOW), lambda n: (0, 0, 0)),
            pl.BlockSpec(memory_space=pltpu.SMEM)]


def _swa_fwd(proj, bias, sinks):
    S = proj.shape[0]

    def body(q_ref, kc_ref, kp_ref, vc_ref, vp_ref, bias_ref, sink_ref, o_ref):
        n = pl.program_id(0)
        for u in range(SWA_SUB):
            rows = slice(WINDOW * u, WINDOW * (u + 1))
            valid = _swa_valid(n > 0 if u == 0 else True)
            q = q_ref[rows, :].astype(BF16)
            kfull = _swa_keys(kp_ref, kc_ref, u)
            vfull = _swa_keys(vp_ref, vc_ref, u)
            for g in range(SWA_HEADS // SWA_GROUP):
                kk = kfull[:, 64 * g:64 * g + 64]
                vv = vfull[:, 64 * g:64 * g + 64]
                bias4 = bias_ref[SWA_GROUP * g:SWA_GROUP * (g + 1)].reshape(GROUP_ROWS, 2 * WINDOW)
                pk, _ = _swa_probs(_stack_heads(q, g), kk, bias4, _group_sinks(sink_ref, g), valid)
                o_ref[rows, 256 * g:256 * (g + 1)] = _unstack_heads(_dot(pk.astype(BF16), vv))

    return pl.pallas_call(
        body, name="swa_fwd", grid=(S // SWA_ROWS,),
        in_specs=_swa_specs(),
        out_specs=pl.BlockSpec((SWA_ROWS, 512), lambda n: (n, 0)),
        out_shape=jax.ShapeDtypeStruct((S, 512), F32),
        compiler_params=_params(("parallel",)),
    )(proj, proj, proj, proj, proj, bias, sinks)


def _swa_bwd(proj, bias, sinks, o, do, bucket):
    S = proj.shape[0]
    nb = S // SWA_ROWS

    def body(q_ref, kc_ref, kp_ref, vc_ref, vp_ref, bias_ref, sink_ref, o_ref, do_ref, bk_ref,
             dq_ref, dk_ref, dv_ref, drb_ref, dsk_ref, dbias_acc):
        n = pl.program_id(0)

        @pl.when(n == 0)
        def _():
            dk_ref[...] = jnp.zeros_like(dk_ref)
            dv_ref[...] = jnp.zeros_like(dv_ref)
            dsk_ref[...] = jnp.zeros_like(dsk_ref)
            dbias_acc[...] = jnp.zeros_like(dbias_acc)
            drb_ref[...] = jnp.zeros_like(drb_ref)

        for u in range(SWA_SUB):
            rows = slice(WINDOW * u, WINDOW * (u + 1))
            blk = SWA_SUB * n + u
            valid = _swa_valid(n > 0 if u == 0 else True)
            q = q_ref[rows, :].astype(BF16)
            dov = do_ref[rows, :]
            ov = o_ref[rows, :]
            kfull = _swa_keys(kp_ref, kc_ref, u)
            vfull = _swa_keys(vp_ref, vc_ref, u)
            prow = pl.ds(pl.multiple_of(jnp.maximum(blk - 1, 0) * WINDOW, WINDOW), WINDOW)
            crow = pl.ds(pl.multiple_of(blk * WINDOW, WINDOW), WINDOW)
            for g in range(SWA_HEADS // SWA_GROUP):
                heads = slice(SWA_GROUP * g, SWA_GROUP * (g + 1))
                kk = kfull[:, 64 * g:64 * g + 64]
                vv = vfull[:, 64 * g:64 * g + 64]
                q4 = _stack_heads(q, g)
                pk, psink = _swa_probs(q4, kk, bias_ref[heads].reshape(GROUP_ROWS, 2 * WINDOW),
                                       _group_sinks(sink_ref, g), valid)
                pkb = pk.astype(BF16)
                do4 = _stack_heads(dov, g)
                dob = do4.astype(BF16)
                dp = _dot_nt(dob, vv)
                delta = jnp.sum(do4 * _stack_heads(ov, g), axis=-1, keepdims=True)
                ds = pk * (dp - delta)
                dsink = -psink * delta
                for a in range(SWA_GROUP):
                    h = SWA_GROUP * g + a
                    part = jnp.sum(dsink[WINDOW * a:WINDOW * (a + 1)], keepdims=True)
                    dsk_ref[h:h + 1, :] += jnp.broadcast_to(part, (1, 128))
                dbias_acc[heads] += ds.reshape(SWA_GROUP, WINDOW, 2 * WINDOW)
                dsb = (ds * SWA_SCALE).astype(BF16)
                dq_ref[rows, 256 * g:256 * (g + 1)] = _unstack_heads(_dot(dsb, kk))
                dkk = _dot_tn(dsb, q4)
                dvv = _dot_tn(pkb, dob)
                dk_ref[prow, 64 * g:64 * g + 64] += dkk[:WINDOW]
                dk_ref[crow, 64 * g:64 * g + 64] += dkk[WINDOW:]
                dv_ref[prow, 64 * g:64 * g + 64] += dvv[:WINDOW]
                dv_ref[crow, 64 * g:64 * g + 64] += dvv[WINDOW:]

        @pl.when(n == nb - 1)
        def _():
            bk = bk_ref[...]
            for h in range(SWA_HEADS):
                dbh = dbias_acc[h]
                for b in range(NUM_BUCKETS):
                    val = jnp.sum(jnp.where(bk == b, dbh, 0.0), keepdims=True)
                    drb_ref[b * 8 + h:b * 8 + h + 1, :] = jnp.broadcast_to(val, (1, 128))

    full = lambda shape: pl.BlockSpec(shape, lambda n: tuple(0 for _ in shape))
    return pl.pallas_call(
        body, name="swa_bwd", grid=(nb,),
        in_specs=_swa_specs() + [pl.BlockSpec((SWA_ROWS, 512), lambda n: (n, 0)),
                                 pl.BlockSpec((SWA_ROWS, 512), lambda n: (n, 0)), full((WINDOW, 2 * WINDOW))],
        out_specs=[pl.BlockSpec((SWA_ROWS, 512), lambda n: (n, 0)), full((S, 128)), full((S, 128)),
                   full((NUM_BUCKETS * 8, 128)), full((8, 128))],
        out_shape=[jax.ShapeDtypeStruct((S, 512), F32), jax.ShapeDtypeStruct((S, 128), F32),
                   jax.ShapeDtypeStruct((S, 128), F32), jax.ShapeDtypeStruct((NUM_BUCKETS * 8, 128), F32),
                   jax.ShapeDtypeStruct((8, 128), F32)],
        scratch_shapes=[pltpu.VMEM((SWA_HEADS, WINDOW, 2 * WINDOW), F32)],
        compiler_params=_params(("arbitrary",)),
    )(proj, proj, proj, proj, proj, bias, sinks, o, do, bucket)


def _rope_tables(S):
    inv = np.float32(ROPE_THETA) ** (-np.arange(0, MLA_ROPE, 2, dtype=np.float32) / np.float32(MLA_ROPE))
    ang = np.arange(S, dtype=np.float32)[:, None] * inv[None, :]
    cos, sin = np.cos(ang), np.sin(ang)
    return (jnp.asarray(np.tile(np.concatenate([cos, cos], axis=1), (1, 2))),
            jnp.asarray(np.tile(np.concatenate([-sin, sin], axis=1), (1, 2))))


def _rope_wide(ref):
    t = ref[...]
    return jnp.concatenate([t, t], axis=1)


def _swap_halves(x):
    w = x.shape[-1]
    lane = lax.broadcasted_iota(jnp.int32, x.shape, x.ndim - 1)
    return jnp.where((lane % 64) < 32, pltpu.roll(x, w - 32, x.ndim - 1), pltpu.roll(x, 32, x.ndim - 1))


def _mla_pre_fwd(proj, qn_w, kvn_w, wuqT, wukv, cos, sin, *, tm=ROW_TILE):
    S = proj.shape[0]
    tm = min(tm, S)

    def body(ql_ref, kl_ref, kr_ref, qw_ref, kw_ref, wuq_ref, wukv_ref, cos_ref, sin_ref,
             qc_ref, kc_ref, vv_ref):
        ql = ql_ref[...]
        qn = (ql * _rstd(ql) * qw_ref[...]).astype(BF16)
        q = _dot_nt(qn, wuq_ref[...])
        cs, sn = _rope_wide(cos_ref), _rope_wide(sin_ref)
        qr = q[:, 512:768]
        qr = qr * cs + _swap_halves(qr) * sn
        half = lax.broadcasted_iota(jnp.int32, (tm, 128), 1) // 64
        kl = kl_ref[...]
        kvn = (kl * _rstd(kl) * kw_ref[...]).astype(BF16)
        kr = kr_ref[...]
        kr = kr * cs[:, :128] + _swap_halves(kr) * sn[:, :128]
        kr2 = (kr + pltpu.roll(kr, 64, 1)).astype(BF16)
        for h in range(MLA_HEADS):
            qc_ref[h, :, 0:128] = q[:, 128 * h:128 * h + 128].astype(BF16)
            chunk = qr[:, 128 * (h // 2):128 * (h // 2) + 128]
            qc_ref[h, :, 128:256] = jnp.where(half == (h % 2), chunk, 0.0).astype(BF16)
            kc_ref[h, :, 0:128] = _dot(kvn, wukv_ref[2 * h]).astype(BF16)
            kc_ref[h, :, 128:256] = kr2
            vv_ref[h] = _dot(kvn, wukv_ref[2 * h + 1]).astype(BF16)

    const = lambda shape: pl.BlockSpec(shape, lambda i: tuple(0 for _ in shape))
    return pl.pallas_call(
        body, name="mla_pre_fwd", grid=(S // tm,),
        in_specs=[pl.BlockSpec((tm, 256), lambda i: (i, 3)), pl.BlockSpec((tm, 128), lambda i: (i, 8)),
                  pl.BlockSpec((tm, 128), lambda i: (i, 9)), const((1, 256)), const((1, 128)),
                  const((768, 256)), const((8, 128, 128)),
                  pl.BlockSpec((tm, 128), lambda i: (i, 0)), pl.BlockSpec((tm, 128), lambda i: (i, 0))],
        out_specs=[pl.BlockSpec((MLA_HEADS, tm, 256), lambda i: (0, i, 0)),
                   pl.BlockSpec((MLA_HEADS, tm, 256), lambda i: (0, i, 0)),
                   pl.BlockSpec((MLA_HEADS, tm, 128), lambda i: (0, i, 0))],
        out_shape=[jax.ShapeDtypeStruct((MLA_HEADS, S, 256), BF16), jax.ShapeDtypeStruct((MLA_HEADS, S, 256), BF16),
                   jax.ShapeDtypeStruct((MLA_HEADS, S, 128), BF16)],
        compiler_params=_params(("parallel",)),
    )(proj, proj, proj, qn_w, kvn_w, wuqT, wukv, cos, sin)


def _causal(i, j, t):
    row = i * t + lax.broadcasted_iota(jnp.int32, (t, t), 0)
    col = j * t + lax.broadcasted_iota(jnp.int32, (t, t), 1)
    return col <= row


def _mla_attn_fwd(qc, kc, vv, *, t=512):
    S = qc.shape[1]
    t = min(t, S)

    def body(q_ref, k_ref, v_ref, o_ref, l_ref):
        i = pl.program_id(0)
        diag = _causal(0, 0, t)

        def step(j, carry, masked):
            rows = pl.ds(pl.multiple_of(j * t, t), t)
            out = []
            for h in range(MLA_HEADS):
                m, l, acc = carry[h]
                s = _dot_nt(q_ref[h], k_ref[h, rows, :]) * MLA_SCALE
                if masked:
                    s = jnp.where(diag, s, -jnp.inf)
                m_new = jnp.maximum(m, jnp.max(s, axis=-1, keepdims=True))
                alpha = jnp.exp(m - m_new)
                p = jnp.exp(s - m_new)
                l = alpha * l + jnp.sum(p, axis=-1, keepdims=True)
                acc = alpha * acc + _dot(p.astype(BF16), v_ref[h, rows, :])
                out.append((m_new, l, acc))
            return tuple(out)

        init = tuple((jnp.full((t, 1), -jnp.inf, F32), jnp.zeros((t, 1), F32), jnp.zeros((t, MLA_V), F32))
                     for _ in range(MLA_HEADS))
        carry = lax.fori_loop(0, i, lambda j, c: step(j, c, False), init)
        carry = step(i, carry, True)
        for h in range(MLA_HEADS):
            m, l, acc = carry[h]
            o_ref[:, 128 * h:128 * h + 128] = acc / l
            l_ref[h] = jnp.broadcast_to(m + jnp.log(l), (t, 128))

    return pl.pallas_call(
        body, name="mla_attn_fwd", grid=(S // t,),
        in_specs=[pl.BlockSpec((MLA_HEADS, t, 256), lambda i: (0, i, 0)),
                  pl.BlockSpec((MLA_HEADS, S, 256), lambda i: (0, 0, 0)),
                  pl.BlockSpec((MLA_HEADS, S, 128), lambda i: (0, 0, 0))],
        out_specs=[pl.BlockSpec((t, 512), lambda i: (i, 0)),
                   pl.BlockSpec((MLA_HEADS, t, 128), lambda i: (0, i, 0))],
        out_shape=[jax.ShapeDtypeStruct((S, 512), F32), jax.ShapeDtypeStruct((MLA_HEADS, S, 128), F32)],
        compiler_params=_params(("parallel",)),
    )(qc, kc, vv)


def _mla_attn_bwd(qc, kc, vv, o, lse, do, *, t=512, tq=1024):
    S = qc.shape[1]
    t = min(t, S)
    tq = min(tq, S)
    nblk = S // t
    hp = MLA_HEADS
    once = pl.Buffered(1)

    def body(q_ref, k_ref, v_ref, o_ref, l_ref, do_ref, dq_ref, dk_ref, dv_ref):
        j = pl.program_id(1)

        @pl.when(j == 0)
        def _():
            dq_ref[...] = jnp.zeros_like(dq_ref)

        first = (j * t) // tq

        def step(i, carry, masked):
            rows = pl.ds(pl.multiple_of(i * tq, tq), tq)
            if masked:
                row = i * tq + lax.broadcasted_iota(jnp.int32, (tq, t), 0)
                col = j * t + lax.broadcasted_iota(jnp.int32, (tq, t), 1)
                visible = col <= row
            out = []
            for h in range(hp):
                dk, dv = carry[h]
                k = k_ref[h]
                q = q_ref[h, rows, :]
                dov = do_ref[rows, 128 * h:128 * h + 128]
                lrow = l_ref[h, rows, :][:, 0:1]
                p = jnp.exp(_dot_nt(q, k) * MLA_SCALE - lrow)
                if masked:
                    p = jnp.where(visible, p, 0.0)
                dob = dov.astype(BF16)
                dv = dv + _dot_tn(p.astype(BF16), dob)
                dp = _dot_nt(dob, v_ref[h])
                delta = jnp.sum(dov * o_ref[rows, 128 * h:128 * h + 128], axis=-1, keepdims=True)
                ds = (p * (dp - delta) * MLA_SCALE).astype(BF16)
                dk = dk + _dot_tn(ds, q)
                dq_ref[h, rows, :] += _dot(ds, k)
                out.append((dk, dv))
            return tuple(out)

        init = tuple((jnp.zeros((t, 256), F32), jnp.zeros((t, MLA_V), F32)) for _ in range(hp))
        carry = step(first, init, True)
        carry = lax.fori_loop(first + 1, S // tq, lambda i, c: step(i, c, False), carry)
        for h in range(hp):
            dk_ref[h] = carry[h][0]
            dv_ref[h] = carry[h][1]

    return pl.pallas_call(
        body, name="mla_attn_bwd", grid=(MLA_HEADS // hp, nblk),
        in_specs=[pl.BlockSpec((hp, S, 256), lambda g, j: (g, 0, 0), pipeline_mode=once),
                  pl.BlockSpec((hp, t, 256), lambda g, j: (g, j, 0)),
                  pl.BlockSpec((hp, t, 128), lambda g, j: (g, j, 0)),
                  pl.BlockSpec((S, 128 * hp), lambda g, j: (0, g), pipeline_mode=once),
                  pl.BlockSpec((hp, S, 128), lambda g, j: (g, 0, 0), pipeline_mode=once),
                  pl.BlockSpec((S, 128 * hp), lambda g, j: (0, g), pipeline_mode=once)],
        out_specs=[pl.BlockSpec((hp, S, 256), lambda g, j: (g, 0, 0)),
                   pl.BlockSpec((hp, t, 256), lambda g, j: (g, j, 0)),
                   pl.BlockSpec((hp, t, 128), lambda g, j: (g, j, 0))],
        out_shape=[jax.ShapeDtypeStruct((MLA_HEADS, S, 256), F32), jax.ShapeDtypeStruct((MLA_HEADS, S, 256), F32),
                   jax.ShapeDtypeStruct((MLA_HEADS, S, 128), F32)],
        compiler_params=_params(("parallel", "arbitrary")),
    )(qc, kc, vv, o, lse, do)


def _mla_pre_bwd(proj, qn_w, kvn_w, wuqT, wukv, cos, sin, dqc, dkc, dvv, *, tm=ROW_TILE):
    S = proj.shape[0]
    tm = min(tm, S)

    def body(ql_ref, kl_ref, qw_ref, kw_ref, wuq_ref, wukv_ref, cos_ref, sin_ref, dqc_ref, dkc_ref, dvv_ref,
             dql_ref, dkl_ref, dkr_ref, gq_ref, gkv_ref, part_ref):
        @pl.when(pl.program_id(0) == 0)
        def _():
            gq_ref[...] = jnp.zeros_like(gq_ref)
            gkv_ref[...] = jnp.zeros_like(gkv_ref)
            part_ref[...] = jnp.zeros_like(part_ref)

        cs, sn = _rope_wide(cos_ref), _rope_wide(sin_ref)
        half = lax.broadcasted_iota(jnp.int32, (tm, 128), 1) // 64
        ql = ql_ref[...]
        rq = _rstd(ql)
        qhat = ql * rq
        qw = qw_ref[...]
        qn = (qhat * qw).astype(BF16)
        chunks = []
        for pair in range(2):
            chunks.append(jnp.where(half == 0, dqc_ref[2 * pair, :, 128:256], dqc_ref[2 * pair + 1, :, 128:256]))
        dqr = jnp.concatenate(chunks, axis=1)
        dqr = dqr * cs + _swap_halves(dqr * sn)
        dq = jnp.concatenate([dqc_ref[h, :, 0:128] for h in range(MLA_HEADS)] + [dqr], axis=1).astype(BF16)
        gq_ref[...] += _dot_tn(dq, qn)
        dqn = _dot(dq, wuq_ref[...])
        part_ref[0:1, :] += jnp.sum(dqn * qhat, axis=0, keepdims=True)
        dql_ref[...] = _rms_bwd(dqn * qw, qhat, rq)
        kl = kl_ref[...]
        rk = _rstd(kl)
        khat = kl * rk
        kw = kw_ref[...]
        kvn = (khat * kw).astype(BF16)
        dkvn = jnp.zeros((tm, MLA_KVR), F32)
        dkr2 = jnp.zeros((tm, 128), F32)
        for h in range(MLA_HEADS):
            dkn = dkc_ref[h, :, 0:128].astype(BF16)
            dvh = dvv_ref[h].astype(BF16)
            gkv_ref[2 * h] += _dot_tn(kvn, dkn)
            gkv_ref[2 * h + 1] += _dot_tn(kvn, dvh)
            dkvn += _dot_nt(dkn, wukv_ref[2 * h]) + _dot_nt(dvh, wukv_ref[2 * h + 1])
            dkr2 += dkc_ref[h, :, 128:256]
        part_ref[1:2, 0:128] += jnp.sum(dkvn * khat, axis=0, keepdims=True)
        dkl_ref[...] = _rms_bwd(dkvn * kw, khat, rk)
        dkr = jnp.where(half == 0, dkr2 + pltpu.roll(dkr2, 64, 1), 0.0)
        dkr_ref[...] = dkr * cs[:, :128] + _swap_halves(dkr * sn[:, :128])

    const = lambda shape: pl.BlockSpec(shape, lambda i: tuple(0 for _ in shape))
    heads = lambda w: pl.BlockSpec((MLA_HEADS, tm, w), lambda i: (0, i, 0))
    return pl.pallas_call(
        body, name="mla_pre_bwd", grid=(S // tm,),
        in_specs=[pl.BlockSpec((tm, 256), lambda i: (i, 3)), pl.BlockSpec((tm, 128), lambda i: (i, 8)),
                  const((1, 256)), const((1, 128)), const((768, 256)), const((8, 128, 128)),
                  pl.BlockSpec((tm, 128), lambda i: (i, 0)), pl.BlockSpec((tm, 128), lambda i: (i, 0)),
                  heads(256), heads(256), heads(128)],
        out_specs=[pl.BlockSpec((tm, 256), lambda i: (i, 0)), pl.BlockSpec((tm, 128), lambda i: (i, 0)),
                   pl.BlockSpec((tm, 128), lambda i: (i, 0)), const((768, 256)), const((8, 128, 128)), const((8, 256))],
        out_shape=[jax.ShapeDtypeStruct((S, 256), F32), jax.ShapeDtypeStruct((S, 128), F32),
                   jax.ShapeDtypeStruct((S, 128), F32), jax.ShapeDtypeStruct((768, 256), F32),
                   jax.ShapeDtypeStruct((8, 128, 128), F32), jax.ShapeDtypeStruct((8, 256), F32)],
        compiler_params=_params(("arbitrary",)),
    )(proj, proj, qn_w, kvn_w, wuqT, wukv, cos, sin, dqc, dkc, dvv)


def _mix_out_fwd(x, oa, ob, w_o, vecs, *, tm=ROW_TILE):
    S = x.shape[0]
    tm = min(tm, S)

    def body(x_ref, oa_ref, ob_ref, w_ref, vec_ref, xo_ref, mo_ref):
        mo = _dot(oa_ref[...].astype(BF16), w_ref[0:512, :]) + _dot(ob_ref[...].astype(BF16), w_ref[512:1024, :])
        mo_ref[...] = mo
        xo_ref[...] = x_ref[...] + vec_ref[3:4, :] * mo

    row = pl.BlockSpec((tm, D), lambda i: (i, 0))
    half = pl.BlockSpec((tm, 512), lambda i: (i, 0))
    return pl.pallas_call(
        body, name="mix_out_fwd", grid=(S // tm,),
        in_specs=[row, half, half, pl.BlockSpec((D, D), lambda i: (0, 0)), pl.BlockSpec((8, D), lambda i: (0, 0))],
        out_specs=[row, row],
        out_shape=[jax.ShapeDtypeStruct((S, D), F32), jax.ShapeDtypeStruct((S, D), F32)],
        compiler_params=_params(("parallel",)),
    )(x, oa, ob, w_o, vecs)


def _mix_out_bwd(dxo, mo, oa, ob, w_o, vecs, *, tm=ROW_TILE):
    S = dxo.shape[0]
    tm = min(tm, S)

    def body(dx_ref, mo_ref, oa_ref, ob_ref, w_ref, vec_ref, doa_ref, dob_ref, gw_ref, part_ref):
        @pl.when(pl.program_id(0) == 0)
        def _():
            gw_ref[...] = jnp.zeros_like(gw_ref)
            part_ref[...] = jnp.zeros_like(part_ref)

        dx = dx_ref[...]
        part_ref[0:1, :] += jnp.sum(dx * mo_ref[...], axis=0, keepdims=True)
        dmo = (vec_ref[3:4, :] * dx).astype(BF16)
        doa_ref[...] = _dot_nt(dmo, w_ref[0:512, :])
        dob_ref[...] = _dot_nt(dmo, w_ref[512:1024, :])
        gw_ref[0:512, :] += _dot_tn(oa_ref[...].astype(BF16), dmo)
        gw_ref[512:1024, :] += _dot_tn(ob_ref[...].astype(BF16), dmo)

    row = pl.BlockSpec((tm, D), lambda i: (i, 0))
    half = pl.BlockSpec((tm, 512), lambda i: (i, 0))
    return pl.pallas_call(
        body, name="mix_out_bwd", grid=(S // tm,),
        in_specs=[row, row, half, half, pl.BlockSpec((D, D), lambda i: (0, 0)), pl.BlockSpec((8, D), lambda i: (0, 0))],
        out_specs=[half, half, pl.BlockSpec((D, D), lambda i: (0, 0)), pl.BlockSpec((8, D), lambda i: (0, 0))],
        out_shape=[jax.ShapeDtypeStruct((S, 512), F32), jax.ShapeDtypeStruct((S, 512), F32),
                   jax.ShapeDtypeStruct((D, D), F32), jax.ShapeDtypeStruct((8, D), F32)],
        compiler_params=_params(("arbitrary",)),
    )(dxo, mo, oa, ob, w_o, vecs)


def _mix_in_bwd(h, w_inT, dq, dk, dv, dql, dkl, dkr, *, tm=ROW_TILE):
    S = h.shape[0]
    tm = min(tm, S)
    offs = (0, 512, 640, 768, 1024, 1152)
    wid = (512, 128, 128, 256, 128, 128)

    def body(h_ref, w_ref, dq_ref, dk_ref, dv_ref, dql_ref, dkl_ref, dkr_ref, dh_ref, gw_ref):
        @pl.when(pl.program_id(0) == 0)
        def _():
            gw_ref[...] = jnp.zeros_like(gw_ref)

        hv = h_ref[...]
        dh = jnp.zeros((tm, D), F32)
        for ref, o, w in zip((dq_ref, dk_ref, dv_ref, dql_ref, dkl_ref, dkr_ref), offs, wid):
            w = min(w, D_IN - o)
            dpart = ref[...][:, :w].astype(BF16)
            dh += _dot(dpart, w_ref[o:o + w, :])
            gw_ref[o:o + w, :] += _dot_tn(dpart, hv)
        dh_ref[...] = dh

    row = pl.BlockSpec((tm, D), lambda i: (i, 0))
    part = lambda w: pl.BlockSpec((tm, w), lambda i: (i, 0))
    return pl.pallas_call(
        body, name="mix_in_bwd", grid=(S // tm,),
        in_specs=[row, pl.BlockSpec((D_IN_PAD, D), lambda i: (0, 0))] + [part(w) for w in wid],
        out_specs=[row, pl.BlockSpec((D_IN, D), lambda i: (0, 0))],
        out_shape=[jax.ShapeDtypeStruct((S, D), F32), jax.ShapeDtypeStruct((D_IN, D), F32)],
        compiler_params=_params(("arbitrary",)),
    )(h, w_inT, dq, dk, dv, dql, dkl, dkr)


def _vecs(norm_w, mod9, k):
    return jnp.concatenate([norm_w.reshape(1, D), mod9[3 * k:3 * k + 3], jnp.zeros((4, D), F32)], axis=0)


def _uq_group_rows(wuqT):
    per = MLA_NOPE + MLA_ROPE
    nope = [wuqT[per * h:per * h + MLA_NOPE] for h in range(MLA_HEADS)]
    rope = [wuqT[per * h + MLA_NOPE:per * (h + 1)] for h in range(MLA_HEADS)]
    return jnp.concatenate(nope + rope, axis=0)


def _uq_ungroup_rows(g):
    parts = []
    for h in range(MLA_HEADS):
        parts += [g[MLA_NOPE * h:MLA_NOPE * (h + 1)], g[512 + MLA_ROPE * h:512 + MLA_ROPE * (h + 1)]]
    return jnp.concatenate(parts, axis=0)


def _local_step(x, tgt, mod9, norms, sinks, rel_bias, q_norm, kv_norm, W, on_grads=None):
    if on_grads is None:
        on_grads = lambda group, grads, after, vecs: vecs
    S = x.shape[0]
    v1 = _vecs(norms["ffn1"], mod9, 0)
    v2 = _vecs(norms["mix"], mod9, 1)
    v3 = _vecs(norms["ffn2"], mod9, 2)
    bucket = jnp.asarray(_bucket_table())
    cos, sin = _rope_tables(S)
    if isinstance(W, dict):
        full, W = W, (lambda group, after, vecs: (full, vecs))

    W1, v1 = W("ffn1", [], v1)
    x1, h1, a1, b1, f1 = _ffn_fwd(x, v1, W1["g1T"], W1["u1T"], W1["d1"], name="ffn1_fwd")
    W2, v2 = W("mixer", [x1], v2)
    w_inT = jnp.pad(W2["w_inT"], ((0, D_IN_PAD - D_IN), (0, 0))).astype(BF16)
    wuqT = _uq_group_rows(W2["w_uqT"])
    h2, proj = _mix_in_fwd(x1, v2, w_inT)
    bias = _bias_build(rel_bias, bucket)
    oa = _swa_fwd(proj, bias, sinks)
    qc, kc, vv = _mla_pre_fwd(proj, q_norm, kv_norm, wuqT, W2["w_ukv"], cos, sin)
    ob, lse = _mla_attn_fwd(qc, kc, vv)
    _, v2o = W("ffn2_on_its_way", [ob], v2)
    x2, mo = _mix_out_fwd(x1, oa, ob, W2["w_o"], v2o)
    W3, v3 = W("ffn2", [x2], v3)
    x3, h3, a3, b3, f3 = _ffn_fwd(x2, v3, W3["g3T"], W3["u3T"], W3["d3"], name="ffn2_fwd")
    dx3, head_part, df3 = _head(x3, tgt, norms["final"], f3, v3)

    gg3, gu3, gd3, dh3 = _ffn_bwd_main(h3, df3, a3, b3, W3["g3T"], W3["u3T"], W3["d3"], name="ffn2_bwd")
    ffn2 = {"g3T": gg3, "u3T": gu3, "d3": gd3}
    v3 = on_grads("ffn2", ffn2, [], v3)
    dx2, n3_part = _norm_bwd(dh3, x2, dx3, v3, name="ffn2_norm_bwd")
    v2 = on_grads("ffn2", None, [dx2], v2)
    doa, dob, g_wo, g2_part = _mix_out_bwd(dx2, mo, oa, ob, W2["w_o"], v2)
    dq, dk, dv, drb, dsk = _swa_bwd(proj, bias, sinks, oa, doa, bucket)
    dqc, dkc, dvv = _mla_attn_bwd(qc, kc, vv, ob, lse, dob)
    dql, dkl, dkr, g_uq, g_ukv, mla_part = _mla_pre_bwd(proj, q_norm, kv_norm, wuqT, W2["w_ukv"], cos, sin, dqc, dkc, dvv)
    dh2, g_win = _mix_in_bwd(h2, w_inT, dq, dk, dv, dql, dkl, dkr)
    mixer = {"w_inT": g_win, "w_uqT": _uq_ungroup_rows(g_uq).astype(BF16),
             "w_ukv": g_ukv.astype(BF16), "w_o": g_wo.astype(BF16)}
    v2 = on_grads("mixer", mixer, [], v2)
    dx1, n2_part, df1 = _norm_bwd(dh2, x1, dx2, v2, name="mix_norm_bwd", below=(f1, v1))
    started = on_grads("mixer", None, [dx1], jnp.zeros((1, 1), F32))
    gg1, gu1, gd1, dh1 = _ffn_bwd_main(h1, df1, a1, b1, W1["g1T"], W1["u1T"], W1["d1"], name="ffn1_bwd",
                                       after=[started])
    ffn1 = {"g1T": gg1, "u1T": gu1, "d1": gd1}
    v1 = on_grads("ffn1", ffn1, [], v1)
    dx0, n1_part = _norm_bwd(dh1, x, dx1, v1, name="ffn1_norm_bwd")

    grads = {**ffn1, **ffn2, **mixer}
    return head_part[1, 0], dx0, grads, _pack_vec(n1_part, n2_part, n3_part, head_part, g2_part, mla_part, dsk, drb)


SMALL_LAYOUT = (("norm_ffn1", 1024), ("norm_mix", 1024), ("norm_ffn2", 1024), ("norm_final", 1024),
                ("q_norm", 256), ("kv_norm", 128), ("sinks", 128), ("rel_bias", 256))
N_SMALL = sum(n for _, n in SMALL_LAYOUT)
LOSS_SLOT = 4 * 1024 + 256 + 128 + SWA_HEADS
N_MODVEC = N_MOD * D
N_VEC = N_MODVEC + N_SMALL


def _pack_vec(n1, n2, n3, head, g2, mla, dsk, drb):
    def body(n1_ref, n2_ref, n3_ref, head_ref, g2_ref, mla_ref, dsk_ref, drb_ref, out_ref):
        rows = [n1_ref[1:2, :], n1_ref[2:3, :], n2_ref[3:4, :], n2_ref[1:2, :], n2_ref[2:3, :], g2_ref[0:1, :],
                n3_ref[1:2, :], n3_ref[2:3, :], head_ref[3:4, :],
                n1_ref[0:1, :], n2_ref[0:1, :], n3_ref[0:1, :], head_ref[0:1, :]]
        for i, row in enumerate(rows):
            out_ref[:, D * i:D * (i + 1)] = row
        off = D * len(rows)
        out_ref[:, off:off + 256] = mla_ref[0:1, :]
        out_ref[:, off + 256:off + 384] = mla_ref[1:2, 0:128]

        def diagonal(block):
            r = lax.broadcasted_iota(jnp.int32, block.shape, 0)
            lane = lax.broadcasted_iota(jnp.int32, block.shape, 1)
            return jnp.sum(jnp.where(r == lane, block, 0.0), axis=0, keepdims=True)

        lane = lax.broadcasted_iota(jnp.int32, (1, 128), 1)
        out_ref[:, off + 384:off + 512] = jnp.where(lane == SWA_HEADS, head_ref[1:2, 0:128], diagonal(dsk_ref[...]))
        out_ref[:, off + 512:off + 640] = diagonal(drb_ref[0:128, :])
        out_ref[:, off + 640:off + 768] = diagonal(drb_ref[128:256, :])

    vm = pl.BlockSpec(memory_space=pltpu.VMEM)
    return pl.pallas_call(body, name="pack_vec", in_specs=[vm] * 8, out_specs=vm,
                          out_shape=jax.ShapeDtypeStruct((1, N_VEC), F32))(n1, n2, n3, head, g2, mla, dsk, drb)


def _coords():
    return lax.axis_index("x"), lax.axis_index("y"), lax.axis_index("c")


def _flip(v, bit):
    return 1 - v if bit else v


def _peer(r):
    x, y, c = _coords()
    return (_flip(x, r & 4), _flip(y, r & 2), _flip(c, r & 1))


def _mod_fwd(c_tile, w_mod, b_mod3):
    W = w_mod.shape[1]

    def body(c_ref, w_ref, b_ref, mod_ref, ca_ref, call_ref, part_ref, send_sems, recv_sems):
        x, y, c = _coords()
        me = 4 * x + 2 * y + c
        call_ref[me] = c_ref[...]
        sends = []
        for r in range(1, N_DEV):
            cp = pltpu.make_async_remote_copy(c_ref, call_ref.at[me], send_sems.at[0, r], recv_sems.at[0, r],
                                              device_id=_peer(r), device_id_type=MESH)
            cp.start()
            sends.append(cp)
        for r in range(1, N_DEV):
            pltpu.make_async_remote_copy(c_ref, call_ref.at[me], send_sems.at[0, r], recv_sems.at[0, r],
                                         device_id=_peer(r), device_id_type=MESH).wait_recv()
        cv = call_ref[...].reshape(8 * N_DEV, D)
        ca = (cv * _sigmoid(cv)).astype(BF16)
        ca_ref[...] = ca
        part_ref[...] = _dot(ca, w_ref[...].astype(BF16)).reshape(N_DEV, 8, W)
        mod_ref[me] = part_ref[me] + b_ref[me]
        for r in range(1, N_DEV):
            cp = pltpu.make_async_remote_copy(part_ref.at[me ^ r], mod_ref.at[me], send_sems.at[1, r],
                                              recv_sems.at[1, r], device_id=_peer(r), device_id_type=MESH)
            cp.start()
            sends.append(cp)
        for r in range(1, N_DEV):
            pltpu.make_async_remote_copy(part_ref.at[me ^ r], mod_ref.at[me], send_sems.at[1, r],
                                         recv_sems.at[1, r], device_id=_peer(r), device_id_type=MESH).wait_recv()
            mod_ref[me ^ r] = mod_ref[me ^ r] + b_ref[me ^ r]
        for cp in sends:
            cp.wait_send()

    vm = pl.BlockSpec(memory_space=pltpu.VMEM)
    return pl.pallas_call(
        body, name="mod_fwd", in_specs=[vm, vm, vm], out_specs=[vm, vm],
        out_shape=[jax.ShapeDtypeStruct((N_DEV, 8, W), F32), jax.ShapeDtypeStruct((8 * N_DEV, D), BF16)],
        scratch_shapes=[pltpu.VMEM((N_DEV, 8, D), F32), pltpu.VMEM((N_DEV, 8, W), F32),
                        pltpu.SemaphoreType.DMA((2, N_DEV)), pltpu.SemaphoreType.DMA((2, N_DEV))],
        compiler_params=_params(),
    )(c_tile, w_mod, b_mod3)


def _mod_bwd(allvec, ca, me_idx):
    W = N_MODVEC // N_DEV

    def body(me_ref, all_ref, cols_ref, ca_ref, gw_ref, sum_ref):
        in_first_row = lax.broadcasted_iota(jnp.int32, (N_DEV, 8, W), 1) == 0
        dm = jnp.where(in_first_row, cols_ref[...], 0.0).reshape(8 * N_DEV, W)
        gw_ref[...] = _dot_tn(ca_ref[...], dm.astype(BF16))
        total = all_ref[0]
        for k in range(1, N_DEV):
            total = total + all_ref[k]
        sum_ref[...] = total

    return pl.pallas_call(
        body, name="mod_bwd",
        grid_spec=pltpu.PrefetchScalarGridSpec(
            num_scalar_prefetch=1, grid=(1,),
            in_specs=[pl.BlockSpec((N_DEV, 1, N_VEC), lambda i, me: (0, 0, 0)),
                      pl.BlockSpec((N_DEV, 1, W), lambda i, me: (0, 0, me[0])),
                      pl.BlockSpec((8 * N_DEV, D), lambda i, me: (0, 0))],
            out_specs=[pl.BlockSpec((D, W), lambda i, me: (0, 0)), pl.BlockSpec((1, N_VEC), lambda i, me: (0, 0))]),
        out_shape=[jax.ShapeDtypeStruct((D, W), F32), jax.ShapeDtypeStruct((1, N_VEC), F32)],
        compiler_params=_params(("arbitrary",)),
    )(me_idx, allvec, allvec, ca)


def _wgather(shards):
    n = len(shards)
    rows = [s.shape[0] for s in shards]

    def body(*refs):
        ins, outs, token = refs[:n], refs[n:2 * n], refs[2 * n]
        send_sems, recv_sems, local_sems = refs[2 * n + 1:]
        token[...] = jnp.zeros_like(token)
        x, y, c = _coords()
        me = 4 * x + 2 * y + c
        sib, xn, yn = (x, y, 1 - c), (1 - x, y, c), (x, 1 - y, c)
        block = lambda px, py, pc: 4 * px + 2 * py + pc

        def part(k, blk, half):
            if half is None:
                return outs[k].at[blk]
            return outs[k].at[blk, pl.ds(half * (rows[k] // 2), rows[k] // 2)]

        def copy(k, slot, blk, to, half=None, src=None):
            ref = part(k, blk, half)
            return pltpu.make_async_remote_copy(
                src_ref=ref if src is None else src, dst_ref=ref, send_sem=send_sems.at[k, slot],
                recv_sem=recv_sems.at[k, slot], device_id=to, device_id_type=MESH)

        local = [pltpu.make_async_copy(ins[k], outs[k].at[me], local_sems.at[k]) for k in range(n)]
        for cp in local:
            cp.start()
        sent = [copy(k, slot, me, to, src=ins[k]) for k in range(n) for slot, to in ((0, sib), (1, xn), (2, yn))]
        for cp in sent:
            cp.start()
        bx, by, bd = block(1 - x, y, c), block(x, 1 - y, c), block(1 - x, 1 - y, c)
        for k in range(n):
            copy(k, 1, bx, sib).wait_recv()
            sent += [copy(k, 4, bx, yn, half=1), copy(k, 5, bx, sib)]
            sent[-2].start()
            sent[-1].start()
        for k in range(n):
            copy(k, 2, by, sib).wait_recv()
            sent += [copy(k, 3, by, xn, half=0), copy(k, 6, by, sib)]
            sent[-2].start()
            sent[-1].start()
        for k in range(n):
            copy(k, 3, bd, sib, half=0).wait_recv()
            copy(k, 4, bd, sib, half=1).wait_recv()
            sent.append(copy(k, 7, bd, sib))
            sent[-1].start()
        for k in range(n):
            copy(k, 0, block(x, y, 1 - c), sib).wait_recv()
            for slot, blk in ((5, block(1 - x, y, 1 - c)), (6, block(x, 1 - y, 1 - c)), (7, block(1 - x, 1 - y, 1 - c))):
                copy(k, slot, blk, sib).wait_recv()
        for cp in sent:
            cp.wait_send()
        for cp in local:
            cp.wait()

    anyspec = pl.BlockSpec(memory_space=pl.ANY)
    return pl.pallas_call(
        body, name="wgather", in_specs=[anyspec] * n,
        out_specs=[anyspec] * n + [pl.BlockSpec(memory_space=pltpu.VMEM)],
        out_shape=[jax.ShapeDtypeStruct((N_DEV,) + s.shape, s.dtype) for s in shards]
        + [jax.ShapeDtypeStruct((8, 128), F32)],
        scratch_shapes=[pltpu.SemaphoreType.DMA((n, 8)), pltpu.SemaphoreType.DMA((n, 8)),
                        pltpu.SemaphoreType.DMA((n,))],
    )(*shards)


class _GatherCopies:
    def __init__(self, lands, send_sems, recv_sems, k0=0, batches=None):
        x, y, c = _coords()
        me = 4 * x + 2 * y + c
        sib = (x, y, 1 - c)
        chips = [(1 - x, y), (x, 1 - y), (1 - x, 1 - y)]

        def copy(k, slot, block, to):
            return pltpu.make_async_remote_copy(
                src_ref=lands[k].at[block], dst_ref=lands[k].at[block],
                send_sem=send_sems.at[7 * (k0 + k) + slot], recv_sem=recv_sems.at[7 * (k0 + k) + slot],
                device_id=to, device_id_type=MESH)

        n = len(lands)
        self.first = [copy(k, 0, me, sib) for k in range(n)]
        for batch in batches or [range(n)]:
            self.first += [copy(k, 1 + j, me, (cx, cy, c)) for j, (cx, cy) in enumerate(chips) for k in batch]
        self.landed = [copy(k, 1 + j, 4 * cx + 2 * cy + c, sib) for j, (cx, cy) in enumerate(chips) for k in range(n)]
        self.passed = [copy(k, 4 + j, 4 * cx + 2 * cy + c, sib) for j, (cx, cy) in enumerate(chips) for k in range(n)]
        self.from_sib = [copy(k, 0, 4 * x + 2 * y + (1 - c), sib) for k in range(n)]
        self.from_sib += [copy(k, 4 + j, 4 * cx + 2 * cy + (1 - c), sib) for j, (cx, cy) in enumerate(chips)
                          for k in range(n)]


def _gather_start(lands, *, name, batches=None):
    n = len(lands)

    def body(*refs):
        for cp in _GatherCopies(refs[:n], refs[n], refs[n + 1], batches=batches).first:
            cp.start()
        refs[-1][...] = jnp.zeros_like(refs[-1])

    out = pl.pallas_call(
        body, name=name,
        out_shape=(pltpu.SemaphoreType.DMA((7 * n,)), pltpu.SemaphoreType.DMA((7 * n,)),
                   *[pltpu.HBM(l.shape, l.dtype) for l in lands], jax.ShapeDtypeStruct((8, 128), F32)),
        in_specs=[HBM_SPEC] * n,
        out_specs=(SEM_SPEC, SEM_SPEC, *[HBM_SPEC] * n, pl.BlockSpec(memory_space=pltpu.VMEM)),
        input_output_aliases={i: 2 + i for i in range(n)},
        compiler_params=pltpu.CompilerParams(has_side_effects=DATAFLOW),
    )(*[_in_hbm(l) for l in lands])
    return out[0], out[1], list(out[2:2 + n]), out[-1]


def _gather_pass(send_sems, recv_sems, lands, after, *, name, stage, k0=0):
    n = len(lands)

    def body(*refs):
        cps = _GatherCopies(refs[:n], refs[n], refs[n + 1], k0)
        if stage == "landed":
            for cp in cps.landed:
                cp.wait_recv()
        else:
            for cp in cps.passed:
                cp.start()
        refs[-1][...] = jnp.zeros_like(refs[-1])

    out = pl.pallas_call(
        body, name=name,
        out_shape=(*[pltpu.HBM(l.shape, l.dtype) for l in lands], jax.ShapeDtypeStruct((8, 128), F32)),
        in_specs=[HBM_SPEC] * n + [SEM_SPEC, SEM_SPEC] + [pl.BlockSpec(memory_space=pl.ANY)] * len(after),
        out_specs=(*[HBM_SPEC] * n, pl.BlockSpec(memory_space=pltpu.VMEM)),
        input_output_aliases={i: i for i in range(n)},
        compiler_params=pltpu.CompilerParams(has_side_effects=DATAFLOW),
    )(*lands, send_sems, recv_sems, *after)
    return list(out[:n]), out[-1]


def _gather_end(send_sems, recv_sems, lands, after, *, name, k0=0):
    n = len(lands)

    def body(*refs):
        cps = _GatherCopies(refs[:n], refs[n], refs[n + 1], k0)
        for cp in cps.from_sib:
            cp.wait_recv()
        for cp in cps.first + cps.passed:
            cp.wait_send()

    out = pl.pallas_call(
        body, name=name,
        out_shape=[pltpu.HBM(l.shape, l.dtype) for l in lands],
        in_specs=[HBM_SPEC] * n + [SEM_SPEC, SEM_SPEC] + [pl.BlockSpec(memory_space=pl.ANY)] * len(after),
        out_specs=[HBM_SPEC] * n,
        input_output_aliases={i: i for i in range(n)},
        compiler_params=pltpu.CompilerParams(has_side_effects=DATAFLOW),
    )(*lands, send_sems, recv_sems, *after)
    return list(out)


def _d2d_copies(grads, lands, send_sems, recv_sems):
    x, y, c = _coords()
    return [pltpu.make_async_remote_copy(
        src_ref=grads[k].at[2 * q + (1 - c)], dst_ref=lands[k].at[q],
        send_sem=send_sems.at[4 * k + q], recv_sem=recv_sems.at[4 * k + q],
        device_id=(x, y, 1 - c), device_id_type=MESH) for k in range(len(grads)) for q in range(4)]


def _vec_copies(srcs, lands, send_sems, recv_sems):
    x, y, c = _coords()
    me = 4 * x + 2 * y + c
    return [pltpu.make_async_remote_copy(
        src_ref=lands[0].at[me], dst_ref=lands[0].at[me], send_sem=send_sems.at[r - 1], recv_sem=recv_sems.at[r - 1],
        device_id=_peer(r), device_id_type=MESH) for r in range(1, N_DEV)]


def _chipsum(gs, sibs, cidx, *, name):
    n = len(gs)

    def body(c_ref, *refs):
        for k in range(n):
            refs[2 * n + k][...] = (refs[k][...].astype(F32) + refs[n + k][...].astype(F32)).astype(refs[2 * n + k].dtype)

    mine = [pl.BlockSpec((1,) + g.shape[1:], lambda q, c_ref: (2 * q + c_ref[0], 0, 0)) for g in gs]
    other = [pl.BlockSpec((1,) + g.shape[1:], lambda q, c_ref: (q, 0, 0)) for g in gs]
    return pl.pallas_call(
        body, name=name,
        grid_spec=pltpu.PrefetchScalarGridSpec(num_scalar_prefetch=1, grid=(4,), in_specs=mine + other, out_specs=other),
        out_shape=[jax.ShapeDtypeStruct((4,) + g.shape[1:], g.dtype) for g in gs],
        compiler_params=_params(("arbitrary",)),
    )(cidx, *gs, *sibs)


HBM_SPEC = pl.BlockSpec(memory_space=pltpu.HBM)
SEM_SPEC = pl.BlockSpec(memory_space=pltpu.SEMAPHORE)
DATAFLOW = pltpu.SideEffectType.DATAFLOW_SIDE_EFFECTING


def _in_hbm(a):
    return pltpu.with_memory_space_constraint(a, pltpu.HBM)


def _ici_copies(sums, lands, send_sems, recv_sems, k0=0):
    x, y, c = _coords()
    chips = [(1 - x, y), (x, 1 - y), (1 - x, 1 - y)]
    cps = []
    for k in range(len(sums)):
        for j, (cx, cy) in enumerate(chips):
            cps.append(pltpu.make_async_remote_copy(
                src_ref=sums[k].at[2 * cx + cy], dst_ref=lands[k].at[j],
                send_sem=send_sems.at[3 * (k0 + k) + j], recv_sem=recv_sems.at[3 * (k0 + k) + j],
                device_id=(cx, cy, c), device_id_type=MESH))
    return cps


def _split_start(copies, srcs, lands, n_sems, after, *, name):
    ns, nl = len(srcs), len(lands)

    def body(*refs):
        for cp in copies(refs[:ns], refs[ns:ns + nl], refs[ns + nl + len(after)], refs[ns + nl + len(after) + 1]):
            cp.start()
        refs[-1][...] = jnp.zeros_like(refs[-1])

    bufs = [_in_hbm(a) for a in list(srcs) + list(lands)]
    out = pl.pallas_call(
        body, name=name,
        out_shape=(pltpu.SemaphoreType.DMA((n_sems,)), pltpu.SemaphoreType.DMA((n_sems,)),
                   *[pltpu.HBM(a.shape, a.dtype) for a in bufs], jax.ShapeDtypeStruct((8, 128), F32)),
        in_specs=[HBM_SPEC] * len(bufs) + [pl.BlockSpec(memory_space=pl.ANY)] * len(after),
        out_specs=(SEM_SPEC, SEM_SPEC, *[HBM_SPEC] * len(bufs), pl.BlockSpec(memory_space=pltpu.VMEM)),
        input_output_aliases={i: 2 + i for i in range(len(bufs))},
        compiler_params=pltpu.CompilerParams(has_side_effects=DATAFLOW),
    )(*bufs, *after)
    return out[0], out[1], list(out[2:2 + ns]), list(out[2 + ns:2 + ns + nl]), out[-1]


def _split_wait(copies, send_sems, recv_sems, srcs, lands, after, *, name):
    ns, nl = len(srcs), len(lands)

    def body(*refs):
        for cp in copies(refs[:ns], refs[ns:ns + nl], refs[ns + nl], refs[ns + nl + 1]):
            cp.wait_send()
            cp.wait_recv()

    out = pl.pallas_call(
        body, name=name,
        out_shape=[pltpu.HBM(a.shape, a.dtype) for a in list(srcs) + list(lands)],
        in_specs=[HBM_SPEC] * (ns + nl) + [SEM_SPEC, SEM_SPEC] + [pl.BlockSpec(memory_space=pl.ANY)] * len(after),
        out_specs=[HBM_SPEC] * (ns + nl),
        input_output_aliases={i: i for i in range(ns + nl)},
        compiler_params=pltpu.CompilerParams(has_side_effects=DATAFLOW),
    )(*srcs, *lands, send_sems, recv_sems, *after)
    return list(out[:ns]), list(out[ns:])


ADAM_C1 = 1.0 / (1.0 - ADAM_B1 ** ADAM_STEP)
ADAM_C2 = 1.0 / (1.0 - ADAM_B2 ** ADAM_STEP)


def _adam_math(w, g, m, v):
    m2 = ADAM_B1 * m + (1.0 - ADAM_B1) * g
    v2 = ADAM_B2 * v + (1.0 - ADAM_B2) * (g * g)
    return -ADAM_LR * ((m2 * ADAM_C1) / (jnp.sqrt(v2 * ADAM_C2) + ADAM_EPS) + ADAM_WD * w), m2, v2


def _adamw(w, g, m, v, *, name):
    R, C = w.shape
    tr = R if R <= 512 else 256

    def body(w_ref, g_ref, m_ref, v_ref, d_ref, nm_ref, nv_ref):
        d_ref[...], nm_ref[...], nv_ref[...] = _adam_math(w_ref[...], g_ref[...], m_ref[...], v_ref[...])

    blk = pl.BlockSpec((tr, C), lambda i: (i, 0))
    return pl.pallas_call(
        body, name=name, grid=(R // tr,), in_specs=[blk] * 4, out_specs=[blk] * 3,
        out_shape=[jax.ShapeDtypeStruct((R, C), F32)] * 3,
        compiler_params=_params(("parallel",)),
    )(w, g, m, v)


def _adamw_rs(wmv, cs, rcv, qidx, *, name):
    n = len(wmv)
    r, cc = wmv[0][0].shape
    tr = r // 2 if r % 32 == 0 and r > 128 else r

    def body(q_ref, *refs):
        ins, outs = refs[:5 * n], refs[5 * n:]
        for k in range(n):
            w_ref, m_ref, v_ref, c_ref, r_ref = ins[5 * k:5 * k + 5]
            g_ref, d_ref, nm_ref, nv_ref = outs[4 * k:4 * k + 4]
            g = ((c_ref[0].astype(F32) + r_ref[0].astype(F32)) + r_ref[1].astype(F32)) + r_ref[2].astype(F32)
            g_ref[...] = g
            d_ref[...], nm_ref[...], nv_ref[...] = _adam_math(w_ref[...], g, m_ref[...], v_ref[...])

    blk = pl.BlockSpec((tr, cc), lambda i, q_ref: (i, 0))
    one = [blk, blk, blk, pl.BlockSpec((1, tr, cc), lambda i, q_ref: (q_ref[0], i, 0)),
           pl.BlockSpec((3, tr, cc), lambda i, q_ref: (0, i, 0))]
    out = pl.pallas_call(
        body, name=name,
        grid_spec=pltpu.PrefetchScalarGridSpec(num_scalar_prefetch=1, grid=(r // tr,), in_specs=one * n,
                                               out_specs=[blk] * (4 * n)),
        out_shape=[jax.ShapeDtypeStruct((r, cc), F32)] * (4 * n),
        compiler_params=_params(("arbitrary",)),
    )(qidx, *[a for (w, m, v), c, rc in zip(wmv, cs, rcv) for a in (w, m, v, c, rc)])
    return [tuple(out[4 * k:4 * k + 4]) for k in range(n)]


SMALL_PARAMS = ("norm_ffn1", "norm_mix", "norm_ffn2", "norm_final", "q_norm", "kv_norm", "sinks", "rel_bias", "b_mod")


def _adamw_small(gvec, wmv):
    widths = [wmv[3 * i].shape[1] for i in range(len(SMALL_PARAMS))]

    def body(*refs):
        g_all = refs[0]
        ins = refs[1:1 + 3 * len(SMALL_PARAMS)]
        outs = refs[1 + 3 * len(SMALL_PARAMS):]
        off = N_MODVEC
        for i, name in enumerate(SMALL_PARAMS):
            g_ref, d_ref, nm_ref, nv_ref = outs[4 * i:4 * i + 4]
            w_ref, m_ref, v_ref = ins[3 * i:3 * i + 3]
            start = 0 if name == "b_mod" else off
            g = g_all[:, start:start + widths[i]]
            g_ref[...] = g
            d_ref[...], nm_ref[...], nv_ref[...] = _adam_math(w_ref[...], g, m_ref[...], v_ref[...])
            if name != "b_mod":
                off += dict(SMALL_LAYOUT)[name]

    vm = pl.BlockSpec(memory_space=pltpu.VMEM)
    n_out = 4 * len(SMALL_PARAMS)
    out = pl.pallas_call(
        body, name="adamw_small", in_specs=[vm] * (1 + len(wmv)), out_specs=[vm] * n_out,
        out_shape=[jax.ShapeDtypeStruct((1, widths[i // 4]), F32) for i in range(n_out)],
        compiler_params=_params(),
    )(gvec, *wmv)
    return {name: out[4 * i:4 * i + 4] for i, name in enumerate(SMALL_PARAMS)}


TRANSPOSED = ("g1T", "u1T", "g3T", "u3T", "w_inT", "w_uqT")


def kernel(x, c, w_mod, b_mod, norm_ffn1, ffn1_gate, ffn1_up, ffn1_down, norm_mix, w_in, q_norm, kv_norm, w_uq, w_ukv, sinks, w_o, norm_ffn2, ffn2_gate, ffn2_up, ffn2_down, rel_bias, norm_final, loss_target, m_w_mod, m_b_mod, m_norm_ffn1, m_ffn1_gate, m_ffn1_up, m_ffn1_down, m_norm_mix, m_w_in, m_q_norm, m_kv_norm, m_w_uq, m_w_ukv, m_sinks, m_w_o, m_norm_ffn2, m_ffn2_gate, m_ffn2_up, m_ffn2_down, m_rel_bias, m_norm_final, v_w_mod, v_b_mod, v_norm_ffn1, v_ffn1_gate, v_ffn1_up, v_ffn1_down, v_norm_mix, v_w_in, v_q_norm, v_kv_norm, v_w_uq, v_w_ukv, v_sinks, v_w_o, v_norm_ffn2, v_ffn2_gate, v_ffn2_up, v_ffn2_down, v_rel_bias, v_norm_final):
    mx, my, mc = _coords()
    cidx = jnp.reshape(mc, (1,)).astype(jnp.int32)
    qidx = jnp.reshape(2 * mx + my, (1,)).astype(jnp.int32)
    WM = w_mod.shape[2]

    c_tile = jnp.pad(c, ((0, 7), (0, 0)))
    b_mod3 = jnp.pad(b_mod.reshape(N_DEV, 1, WM), ((0, 0), (0, 7), (0, 0)))
    mod3, ca = _mod_fwd(c_tile, w_mod[0], b_mod3)
    mod9 = mod3[:, 0, :].reshape(N_MOD, D)

    shards = {"g1T": ffn1_gate[0].T.astype(BF16), "u1T": ffn1_up[0].T.astype(BF16), "d1": ffn1_down[0].astype(BF16),
              "g3T": ffn2_gate[0].T.astype(BF16), "u3T": ffn2_up[0].T.astype(BF16), "d3": ffn2_down[0].astype(BF16),
              "w_inT": w_in[0].T, "w_uqT": w_uq[0].T.astype(BF16), "w_ukv": w_ukv[0].astype(BF16),
              "w_o": w_o[0].astype(BF16)}
    me = 4 * mx + 2 * my + mc
    groups = {"ffn1": ("g1T", "u1T", "d1"), "mixer": ("w_inT", "w_uqT", "w_ukv", "w_o"), "ffn2": ("g3T", "u3T", "d3")}
    arriving = {}

    def as_weights(group, gathered):
        return {k: g if k == "w_ukv" else g.reshape(N_DEV * g.shape[1], g.shape[2])
                for k, g in zip(groups[group], gathered)}

    later = groups["mixer"] + groups["ffn2"]
    place = {"mixer": 0, "ffn2": len(groups["mixer"])}

    def start_gather(token):
        lands = []
        for k in later:
            sh = shards[k] + token[0, 0].astype(shards[k].dtype)
            lands.append(lax.dynamic_update_slice(lax.empty((N_DEV,) + sh.shape, sh.dtype), sh[None], (me, 0, 0)))
        batches = [range(k0, k0 + len(groups[group])) for group, k0 in place.items()]
        send, recv, lands, started = _gather_start(lands, name="gather_start", batches=batches)
        for group, k0 in place.items():
            arriving[group] = (send, recv, lands[k0:k0 + len(groups[group])])
        return started

    def fetch(group, after, vecs):
        if group == "ffn1":
            *gathered, token = _wgather([shards[k] + ca[1, 0].astype(shards[k].dtype) for k in groups["ffn1"]])
            return as_weights("ffn1", gathered), vecs + start_gather(token)[0:1, 0:1]

        def pass_on(group, after):
            send, recv, lands = arriving[group]
            lands, token = _gather_pass(send, recv, lands, after, name="gather_landed_" + group, stage="landed",
                                        k0=place[group])
            lands, token = _gather_pass(send, recv, lands, [token], name="gather_onward_" + group, stage="onward",
                                        k0=place[group])
            arriving[group] = (send, recv, lands)
            return token

        if group == "ffn2_on_its_way":
            return None, vecs + pass_on("ffn2", after)[0:1, 0:1]
        if group == "mixer":
            after = [pass_on("mixer", after)]
        send, recv, lands = arriving[group]
        return as_weights(group, _gather_end(send, recv, lands, after, name="gather_end_" + group,
                                             k0=place[group])), vecs

    norms ={"ffn1": norm_ffn1, "mix": norm_mix, "ffn2": norm_ffn2, "final": norm_final.reshape(1, D)}
    in_flight = {}

    def on_grads(group, g, after, vecs, before_ici=()):
        if g is not None:
            names = list(g)
            by_dest = [g[k] if k == "w_ukv" else g[k].reshape((N_DEV, g[k].shape[0] // N_DEV) + g[k].shape[1:])
                       for k in names]
            lands = [lax.empty((4,) + a.shape[1:], a.dtype) for a in by_dest]
            send, recv, by_dest, lands, token = _split_start(_d2d_copies, by_dest, lands, 4 * len(names), after,
                                                             name="rs_d2d_start_" + group)
            in_flight[group] = (names, send, recv, by_dest, lands)
            return vecs + token[0:1, 0:1]
        names, send, recv, by_dest, lands = in_flight[group]
        by_dest, from_sib = _split_wait(_d2d_copies, send, recv, by_dest, lands, after, name="rs_d2d_wait_" + group)
        sums = _chipsum(by_dest, from_sib, cidx, name="chipsum_" + group)
        lands = [lax.empty((3,) + s.shape[1:], s.dtype) for s in sums]
        send, recv, sums, lands, token = _split_start(_ici_copies, sums, lands, 3 * len(names), list(before_ici),
                                                      name="rs_ici_start_" + group)
        in_flight[group] = (names, send, recv, sums, lands, token)
        return vecs + token[0:1, 0:1]

    _, grad_x, _, vec = _local_step(
        x[0], loss_target[0], mod9, norms, sinks, rel_bias, q_norm, kv_norm, fetch, on_grads=on_grads)

    vec = vec.reshape(1, 1, N_VEC)
    allvec = lax.dynamic_update_slice(lax.empty((N_DEV, 1, N_VEC), F32), vec, (me, 0, 0))
    vsend, vrecv, _, (allvec,), vec_started = _split_start(_vec_copies, [], [allvec], N_DEV - 1, [], name="vec_start")
    on_grads("ffn1", None, [grad_x], jnp.zeros((1, 1), F32), before_ici=[vec_started])

    owners = {"g1T": ("ffn1_gate", ffn1_gate, m_ffn1_gate, v_ffn1_gate), "u1T": ("ffn1_up", ffn1_up, m_ffn1_up, v_ffn1_up),
              "d1": ("ffn1_down", ffn1_down, m_ffn1_down, v_ffn1_down),
              "g3T": ("ffn2_gate", ffn2_gate, m_ffn2_gate, v_ffn2_gate), "u3T": ("ffn2_up", ffn2_up, m_ffn2_up, v_ffn2_up),
              "d3": ("ffn2_down", ffn2_down, m_ffn2_down, v_ffn2_down),
              "w_inT": ("w_in", w_in, m_w_in, v_w_in), "w_uqT": ("w_uq", w_uq, m_w_uq, v_w_uq),
              "w_ukv": ("w_ukv", w_ukv, m_w_ukv, v_w_ukv), "w_o": ("w_o", w_o, m_w_o, v_w_o)}
    res, done = {}, []

    def finish(group, after, one_by_one=False):
        names, send, recv, sums, lands, _ = in_flight[group]
        there = lambda k, a: a[0].T if k in TRANSPOSED else a[0]
        back = lambda k, a: a.T[None] if k in TRANSPOSED else a[None]
        wmv = [tuple(there(k, a) for a in owners[k][1:]) for k in names]
        if one_by_one:
            outs = []
            for i, k in enumerate(names):
                (cs,), (rc,) = _split_wait(functools.partial(_ici_copies, k0=i), send, recv, [sums[i]], [lands[i]],
                                           after, name="rs_ici_wait_" + k)
                outs.append(_adamw_rs([wmv[i]], [cs], [rc], qidx, name="adamw_" + owners[k][0])[0])
                after = [outs[-1][3]]
        else:
            sums, lands = _split_wait(_ici_copies, send, recv, sums, lands, after, name="rs_ici_wait_" + group)
            if len({w.shape for w, _, _ in wmv}) == 1:
                outs = _adamw_rs(wmv, sums, lands, qidx, name="adamw_" + group)
            else:
                outs = [_adamw_rs([t], [cs], [rc], qidx, name="adamw_" + owners[k][0])[0]
                        for k, t, cs, rc in zip(names, wmv, sums, lands)]
        for k, out in zip(names, outs):
            done.append(out[3])
            res[owners[k][0]] = tuple(back(k, a) for a in out)

    ffn1_started = in_flight["ffn1"][5]
    finish("ffn2", [ffn1_started])
    finish("mixer", [ffn1_started])

    _, (allvec,) = _split_wait(_vec_copies, vsend, vrecv, [], [allvec], [ffn1_started], name="vec_wait")
    g_wmod, gvec = _mod_bwd(allvec, ca, jnp.reshape(me, (1,)).astype(jnp.int32))
    loss = gvec[0, N_MODVEC + LOSS_SLOT]
    res["w_mod"] = tuple(a[None] for a in (g_wmod,) + tuple(_adamw(w_mod[0], g_wmod, m_w_mod[0], v_w_mod[0],
                                                                    name="adamw_w_mod")))
    small_in = {"norm_ffn1": (norm_ffn1, m_norm_ffn1, v_norm_ffn1), "norm_mix": (norm_mix, m_norm_mix, v_norm_mix),
                "norm_ffn2": (norm_ffn2, m_norm_ffn2, v_norm_ffn2), "norm_final": (norm_final, m_norm_final, v_norm_final),
                "q_norm": (q_norm, m_q_norm, v_q_norm), "kv_norm": (kv_norm, m_kv_norm, v_kv_norm),
                "sinks": (sinks, m_sinks, v_sinks), "rel_bias": (rel_bias, m_rel_bias, v_rel_bias),
                "b_mod": (b_mod, m_b_mod, v_b_mod)}
    small_out = _adamw_small(gvec, [a.reshape(1, -1) for k in SMALL_PARAMS for a in small_in[k]])
    for k in SMALL_PARAMS:
        res[k] = tuple(a.reshape(small_in[k][0].shape) for a in small_out[k])

    finish("ffn1", done + [res["w_mod"][3]] + [a for k in SMALL_PARAMS for a in res[k]], one_by_one=True)

    order = ("w_mod", "b_mod", "norm_ffn1", "ffn1_gate", "ffn1_up", "ffn1_down", "norm_mix", "w_in", "q_norm",
             "kv_norm", "w_uq", "w_ukv", "sinks", "w_o", "norm_ffn2", "ffn2_gate", "ffn2_up", "ffn2_down",
             "rel_bias", "norm_final")
    return (loss, grad_x[None]) + tuple(res[nm][kind] for kind in range(4) for nm in order)
```

```python
import functools
import math

import numpy as np
import jax
import jax.numpy as jnp
from jax import lax
from jax.experimental import pallas as pl
from jax.experimental.pallas import tpu as pltpu

F32 = jnp.float32
BF16 = jnp.bfloat16
MESH = pl.DeviceIdType.MESH

N_DEV = 8
D = 1024
D_FF = 2816
EPS = 1e-6
N_MOD = 9
SWA_HEADS = 8
SWA_DH = 64
WINDOW = 128
MLA_HEADS = 4
MLA_NOPE = 128
MLA_ROPE = 64
MLA_V = 128
MLA_QR = 256
MLA_KVR = 128
ROPE_THETA = 10000.0
NUM_BUCKETS = 32
D_IN = 1216
D_IN_PAD = 1280
SWA_SCALE = SWA_DH ** -0.5
MLA_SCALE = (MLA_NOPE + MLA_ROPE) ** -0.5

ADAM_LR = 0.001
ADAM_B1 = 0.9
ADAM_B2 = 0.999
ADAM_EPS = 1e-08
ADAM_WD = 0.01
ADAM_STEP = 10

V7X_VMEM_LIMIT = 56 * 1024 * 1024
ROW_TILE = 512

NT_DIMS = (((1,), (1,)), ((), ()))
TN_DIMS = (((0,), (0,)), ((), ()))


def _dot(a, b):
    return jnp.dot(a, b, preferred_element_type=F32)


def _dot_nt(a, b):
    return lax.dot_general(a, b, NT_DIMS, preferred_element_type=F32)


def _dot_tn(a, b):
    return lax.dot_general(a, b, TN_DIMS, preferred_element_type=F32)


def _params(sem=None):
    return pltpu.CompilerParams(dimension_semantics=sem, vmem_limit_bytes=V7X_VMEM_LIMIT)


def _rstd(x):
    return lax.rsqrt(jnp.mean(x * x, axis=-1, keepdims=True) + EPS)


def _rms_bwd(dy, xhat, r):
    return r * (dy - xhat * jnp.mean(dy * xhat, axis=-1, keepdims=True))


def _sigmoid(a):
    return 1.0 / (1.0 + jnp.exp(-a))


def _ffn_fwd(x, vecs, wgT, wuT, wd, *, name, tm=256, tf=D_FF):
    S, F = x.shape[0], wd.shape[0]
    tm = min(tm, S)
    ni, nj = S // tm, F // tf

    def body(x_ref, vec_ref, wg_ref, wu_ref, wd_ref, xo_ref, h_ref, a_ref, b_ref, f_ref, acc_ref):
        j = pl.program_id(1)

        @pl.when(j == 0)
        def _():
            xv = x_ref[...]
            hn = xv * _rstd(xv) * vec_ref[0:1, :]
            h_ref[...] = (hn * (1.0 + vec_ref[2:3, :]) + vec_ref[1:2, :]).astype(BF16)

        h = h_ref[...]
        a = _dot_nt(h, wg_ref[...])
        b = _dot_nt(h, wu_ref[...])
        a_ref[...] = a.astype(BF16)
        b_ref[...] = b.astype(BF16)
        part = _dot((a * _sigmoid(a) * b).astype(BF16), wd_ref[...])

        def finish(f):
            f_ref[...] = f
            xo_ref[...] = x_ref[...] + (0.5 * vec_ref[3:4, :]) * f

        if nj == 1:
            finish(part)
        else:
            @pl.when(j == 0)
            def _():
                acc_ref[...] = part

            @pl.when((j > 0) & (j < nj - 1))
            def _():
                acc_ref[...] += part

            @pl.when(j == nj - 1)
            def _():
                finish(acc_ref[...] + part)

    row = pl.BlockSpec((tm, D), lambda i, j: (i, 0))
    wspec = pl.BlockSpec((tf, D), lambda i, j: (j, 0), pipeline_mode=pl.Buffered(1) if nj == 1 else None)
    act = pl.BlockSpec((tm, tf), lambda i, j: (i, j))
    return pl.pallas_call(
        body, name=name, grid=(ni, nj),
        in_specs=[row, pl.BlockSpec((8, D), lambda i, j: (0, 0)), wspec, wspec, wspec],
        out_specs=[row, row, act, act, row],
        out_shape=[jax.ShapeDtypeStruct((S, D), F32), jax.ShapeDtypeStruct((S, D), BF16),
                   jax.ShapeDtypeStruct((S, F), BF16), jax.ShapeDtypeStruct((S, F), BF16),
                   jax.ShapeDtypeStruct((S, D), F32)],
        scratch_shapes=[pltpu.VMEM((tm, D) if nj > 1 else (8, 128), F32)],
        compiler_params=_params(("parallel", "arbitrary")),
    )(x, vecs, wgT, wuT, wd)


def _ffn_bwd_main(h, df, a, b, wgT, wuT, wd, *, name, after=(), tm=2048, tf=256):
    S = h.shape[0]
    tm = min(tm, S)
    ni, nj = S // tm, D_FF // tf

    def body(h_hbm, df_hbm, a_ref, b_ref, wg_ref, wu_ref, wd_ref, *rest):
        gg_ref, gu_ref, gd_ref, dh_hbm, h_v, df_v, dh_v, gg_acc, gu_acc, gd_acc, sem = rest[len(after):]
        j = pl.program_id(0)
        i = pl.program_id(1)

        @pl.when((j == 0) & (i == 0))
        def _():
            c1 = pltpu.make_async_copy(h_hbm, h_v, sem.at[0])
            c2 = pltpu.make_async_copy(df_hbm, df_v, sem.at[1])
            c1.start()
            c2.start()
            c1.wait()
            c2.wait()

        @pl.when(i == 0)
        def _():
            gg_acc[...] = jnp.zeros_like(gg_acc)
            gu_acc[...] = jnp.zeros_like(gu_acc)
            gd_acc[...] = jnp.zeros_like(gd_acc)

        rows = pl.ds(pl.multiple_of(i * tm, tm), tm)
        hi = h_v[rows, :]
        dfi = df_v[rows, :]
        av = a_ref[...].astype(F32)
        bv = b_ref[...].astype(F32)
        sg = _sigmoid(av)
        sa = av * sg
        hsw = (sa * bv).astype(BF16)
        dhsw = _dot_nt(dfi, wd_ref[...])
        da = (dhsw * bv * (sg * (1.0 + av * (1.0 - sg)))).astype(BF16)
        db = (dhsw * sa).astype(BF16)
        gd_acc[...] += _dot_tn(hsw, dfi)
        gg_acc[...] += _dot_tn(da, hi)
        gu_acc[...] += _dot_tn(db, hi)
        dh = _dot(da, wg_ref[...]) + _dot(db, wu_ref[...])

        @pl.when(j == 0)
        def _():
            dh_v[rows, :] = dh

        @pl.when(j > 0)
        def _():
            dh_v[rows, :] += dh

        @pl.when(i == ni - 1)
        def _():
            gg_ref[...] = gg_acc[...].astype(BF16)
            gu_ref[...] = gu_acc[...].astype(BF16)
            gd_ref[...] = gd_acc[...].astype(BF16)

        @pl.when((j == nj - 1) & (i == ni - 1))
        def _():
            c3 = pltpu.make_async_copy(dh_v, dh_hbm, sem.at[2])
            c3.start()
            c3.wait()

    anyspec = pl.BlockSpec(memory_space=pl.ANY)
    wspec = pl.BlockSpec((tf, D), lambda j, i: (j, 0))
    act = pl.BlockSpec((tm, tf), lambda j, i: (i, j))
    return pl.pallas_call(
        body, name=name, grid=(nj, ni),
        in_specs=[anyspec, anyspec, act, act, wspec, wspec, wspec] + [anyspec] * len(after),
        out_specs=[wspec, wspec, wspec, anyspec],
        out_shape=[jax.ShapeDtypeStruct((D_FF, D), BF16)] * 3 + [jax.ShapeDtypeStruct((S, D), F32)],
        scratch_shapes=[pltpu.VMEM((S, D), BF16), pltpu.VMEM((S, D), BF16), pltpu.VMEM((S, D), F32),
                        pltpu.VMEM((tf, D), F32), pltpu.VMEM((tf, D), F32), pltpu.VMEM((tf, D), F32),
                        pltpu.SemaphoreType.DMA((3,))],
        compiler_params=_params(("arbitrary", "arbitrary")),
    )(h, df, a, b, wgT, wuT, wd, *after)


def _ffn_out_bwd(dx, f, gate, df_ref, part_ref):
    df_ref[...] = ((0.5 * gate) * dx).astype(BF16)
    part_ref[3:4, :] += 0.5 * jnp.sum(dx * f, axis=0, keepdims=True)


def _norm_bwd(dh, x, dxo, vecs, *, name, below=None, tm=ROW_TILE):
    S = x.shape[0]
    tm = min(tm, S)

    def body(dh_ref, x_ref, dxo_ref, vec_ref, *rest):
        dx_ref, part_ref = rest[-2 if below is None else -3], rest[-1 if below is None else -2]

        @pl.when(pl.program_id(0) == 0)
        def _():
            part_ref[...] = jnp.zeros_like(part_ref)

        dh = dh_ref[...]
        xv = x_ref[...]
        r = _rstd(xv)
        xhat = xv * r
        w = vec_ref[0:1, :]
        xn = xhat * w
        dxn = dh * (1.0 + vec_ref[2:3, :])
        part_ref[0:1, :] += jnp.sum(dxn * xhat, axis=0, keepdims=True)
        part_ref[1:2, :] += jnp.sum(dh, axis=0, keepdims=True)
        part_ref[2:3, :] += jnp.sum(dh * xn, axis=0, keepdims=True)
        dx = dxo_ref[...] + _rms_bwd(dxn * w, xhat, r)
        dx_ref[...] = dx
        if below is not None:
            _ffn_out_bwd(dx, rest[0][...], rest[1][3:4, :], rest[-1], part_ref)

    row = pl.BlockSpec((tm, D), lambda i: (i, 0))
    vec = pl.BlockSpec((8, D), lambda i: (0, 0))
    extra = [] if below is None else [row, vec]
    return pl.pallas_call(
        body, name=name, grid=(S // tm,), in_specs=[row, row, row, vec] + extra,
        out_specs=[row, vec] + ([] if below is None else [row]),
        out_shape=[jax.ShapeDtypeStruct((S, D), F32), jax.ShapeDtypeStruct((8, D), F32)]
        + ([] if below is None else [jax.ShapeDtypeStruct((S, D), BF16)]),
        compiler_params=_params(("arbitrary",)),
    )(dh, x, dxo, vecs, *([] if below is None else below))


def _head(x, tgt, nf, f, vecs, *, tm=ROW_TILE):
    S = x.shape[0]
    tm = min(tm, S)

    def body(x_ref, t_ref, nf_ref, f_ref, vec_ref, dx_ref, part_ref, df_ref):
        @pl.when(pl.program_id(0) == 0)
        def _():
            part_ref[...] = jnp.zeros_like(part_ref)

        xv = x_ref[...]
        r = _rstd(xv)
        xhat = xv * r
        w = nf_ref[...]
        e = xhat * w - t_ref[...]
        dy = e * (1.0 / D)
        part_ref[0:1, :] += jnp.sum(dy * xhat, axis=0, keepdims=True)
        part_ref[1:2, :] += jnp.sum(e * e) * (0.5 / D)
        dx = _rms_bwd(dy * w, xhat, r)
        dx_ref[...] = dx
        _ffn_out_bwd(dx, f_ref[...], vec_ref[3:4, :], df_ref, part_ref)

    row = pl.BlockSpec((tm, D), lambda i: (i, 0))
    vec = pl.BlockSpec((8, D), lambda i: (0, 0))
    return pl.pallas_call(
        body, name="head", grid=(S // tm,),
        in_specs=[row, row, pl.BlockSpec((1, D), lambda i: (0, 0)), row, vec],
        out_specs=[row, vec, row],
        out_shape=[jax.ShapeDtypeStruct((S, D), F32), jax.ShapeDtypeStruct((8, D), F32),
                   jax.ShapeDtypeStruct((S, D), BF16)],
        compiler_params=_params(("arbitrary",)),
    )(x, tgt, nf, f, vecs)


def _mix_in_fwd(x, vecs, w_inT, *, tm=ROW_TILE):
    S = x.shape[0]
    tm = min(tm, S)

    def body(x_ref, vec_ref, w_ref, h_ref, p_ref):
        xv = x_ref[...]
        hn = xv * _rstd(xv) * vec_ref[0:1, :]
        h = (hn * (1.0 + vec_ref[2:3, :]) + vec_ref[1:2, :]).astype(BF16)
        h_ref[...] = h
        p_ref[...] = _dot_nt(h, w_ref[...])

    row = pl.BlockSpec((tm, D), lambda i: (i, 0))
    return pl.pallas_call(
        body, name="mix_in_fwd", grid=(S // tm,),
        in_specs=[row, pl.BlockSpec((8, D), lambda i: (0, 0)), pl.BlockSpec((D_IN_PAD, D), lambda i: (0, 0))],
        out_specs=[row, pl.BlockSpec((tm, D_IN_PAD), lambda i: (i, 0))],
        out_shape=[jax.ShapeDtypeStruct((S, D), BF16), jax.ShapeDtypeStruct((S, D_IN_PAD), F32)],
        compiler_params=_params(("parallel",)),
    )(x, vecs, w_inT)


def _bucket_table():
    qi = np.arange(WINDOW)[:, None]
    kj = np.arange(2 * WINDOW)[None, :]
    dist = qi + WINDOW - kj
    max_exact = NUM_BUCKETS // 2
    n = np.maximum(dist, 0)
    nf = np.maximum(n, 1).astype(np.float32)
    large = max_exact + (np.log(nf / np.float32(max_exact)) / np.float32(math.log(WINDOW / max_exact))
                         * np.float32(NUM_BUCKETS - max_exact)).astype(np.int32)
    large = np.minimum(large, NUM_BUCKETS - 1)
    return np.where(n < max_exact, n, large).astype(np.int32)


def _bias_build(rel_bias, bucket):
    def body(rb_ref, bk_ref, out_ref):
        bk = bk_ref[...]
        for h in range(SWA_HEADS):
            acc = jnp.zeros((WINDOW, 2 * WINDOW), F32)
            for b in range(NUM_BUCKETS):
                acc = jnp.where(bk == b, rb_ref[b, h], acc)
            out_ref[h] = acc

    return pl.pallas_call(
        body, name="bias_build",
        in_specs=[pl.BlockSpec(memory_space=pltpu.SMEM), pl.BlockSpec(memory_space=pltpu.VMEM)],
        out_specs=pl.BlockSpec(memory_space=pltpu.VMEM),
        out_shape=jax.ShapeDtypeStruct((SWA_HEADS, WINDOW, 2 * WINDOW), F32),
    )(rel_bias, bucket)


SWA_GROUP = 4
GROUP_ROWS = SWA_GROUP * WINDOW


SWA_SUB = 2


def _swa_valid(has_prev):
    row = lax.broadcasted_iota(jnp.int32, (GROUP_ROWS, 2 * WINDOW), 0) % WINDOW
    col = lax.broadcasted_iota(jnp.int32, (GROUP_ROWS, 2 * WINDOW), 1)
    dist = row + WINDOW - col
    return (dist >= 0) & (dist < WINDOW) & ((col >= WINDOW) | has_prev)


def _swa_keys(prev_ref, cur_ref, u):
    cur = cur_ref[...]
    before = prev_ref[...] if u == 0 else cur[WINDOW * (u - 1):WINDOW * u]
    return jnp.concatenate([before, cur[WINDOW * u:WINDOW * (u + 1)]], axis=0).astype(BF16)


def _stack_heads(x, g):
    return jnp.concatenate([x[:, 64 * h:64 * h + 64] for h in range(SWA_GROUP * g, SWA_GROUP * (g + 1))], axis=0)


def _unstack_heads(x4):
    return jnp.concatenate([x4[WINDOW * a:WINDOW * (a + 1)] for a in range(SWA_GROUP)], axis=1)


def _group_sinks(sink_ref, g):
    head = lax.broadcasted_iota(jnp.int32, (GROUP_ROWS, 1), 0) // WINDOW
    out = jnp.full((GROUP_ROWS, 1), sink_ref[0, SWA_GROUP * g], F32)
    for a in range(1, SWA_GROUP):
        out = jnp.where(head == a, sink_ref[0, SWA_GROUP * g + a], out)
    return out


def _swa_probs(qh, kk, bias_h, sink, valid):
    s = _dot_nt(qh, kk) * SWA_SCALE + bias_h
    s = jnp.where(valid, s, -jnp.inf)
    m = jnp.maximum(jnp.max(s, axis=-1, keepdims=True), sink)
    p = jnp.exp(s - m)
    ps = jnp.exp(sink - m)
    inv = 1.0 / (jnp.sum(p, axis=-1, keepdims=True) + ps)
    return p * inv, ps * inv


SWA_ROWS = SWA_SUB * WINDOW


def _swa_specs():
    prev = lambda n: jnp.maximum(SWA_SUB * n - 1, 0)
    return [pl.BlockSpec((SWA_ROWS, 512), lambda n: (n, 0)),
            pl.BlockSpec((SWA_ROWS, 128), lambda n: (n, 4)),
            pl.BlockSpec((WINDOW, 128), lambda n: (prev(n), 4)),
            pl.BlockSpec((SWA_ROWS, 128), lambda n: (n, 5)),
            pl.BlockSpec((WINDOW, 128), lambda n: (prev(n), 5)),
            pl.BlockSpec((SWA_HEADS, WINDOW, 2 * WINDOW), lambda n: (0, 0, 0)),
            pl.BlockSpec(memory_space=pltpu.SMEM)]


def _swa_fwd(proj, bias, sinks):
    S = proj.shape[0]

    def body(q_ref, kc_ref, kp_ref, vc_ref, vp_ref, bias_ref, sink_ref, o_ref):
        n = pl.program_id(0)
        for u in range(SWA_SUB):
            rows = slice(WINDOW * u, WINDOW * (u + 1))
            valid = _swa_valid(n > 0 if u == 0 else True)
            q = q_ref[rows, :].astype(BF16)
            kfull = _swa_keys(kp_ref, kc_ref, u)
            vfull = _swa_keys(vp_ref, vc_ref, u)
            for g in range(SWA_HEADS // SWA_GROUP):
                kk = kfull[:, 64 * g:64 * g + 64]
                vv = vfull[:, 64 * g:64 * g + 64]
                bias4 = bias_ref[SWA_GROUP * g:SWA_GROUP * (g + 1)].reshape(GROUP_ROWS, 2 * WINDOW)
                pk, _ = _swa_probs(_stack_heads(q, g), kk, bias4, _group_sinks(sink_ref, g), valid)
                o_ref[rows, 256 * g:256 * (g + 1)] = _unstack_heads(_dot(pk.astype(BF16), vv))

    return pl.pallas_call(
        body, name="swa_fwd", grid=(S // SWA_ROWS,),
        in_specs=_swa_specs(),
        out_specs=pl.BlockSpec((SWA_ROWS, 512), lambda n: (n, 0)),
        out_shape=jax.ShapeDtypeStruct((S, 512), F32),
        compiler_params=_params(("parallel",)),
    )(proj, proj, proj, proj, proj, bias, sinks)


def _swa_bwd(proj, bias, sinks, o, do, bucket):
    S = proj.shape[0]
    nb = S // SWA_ROWS

    def body(q_ref, kc_ref, kp_ref, vc_ref, vp_ref, bias_ref, sink_ref, o_ref, do_ref, bk_ref,
             dq_ref, dk_ref, dv_ref, drb_ref, dsk_ref, dbias_acc):
        n = pl.program_id(0)

        @pl.when(n == 0)
        def _():
            dk_ref[...] = jnp.zeros_like(dk_ref)
            dv_ref[...] = jnp.zeros_like(dv_ref)
            dsk_ref[...] = jnp.zeros_like(dsk_ref)
            dbias_acc[...] = jnp.zeros_like(dbias_acc)
            drb_ref[...] = jnp.zeros_like(drb_ref)

        for u in range(SWA_SUB):
            rows = slice(WINDOW * u, WINDOW * (u + 1))
            blk = SWA_SUB * n + u
            valid = _swa_valid(n > 0 if u == 0 else True)
            q = q_ref[rows, :].astype(BF16)
            dov = do_ref[rows, :]
            ov = o_ref[rows, :]
            kfull = _swa_keys(kp_ref, kc_ref, u)
            vfull = _swa_keys(vp_ref, vc_ref, u)
            prow = pl.ds(pl.multiple_of(jnp.maximum(blk - 1, 0) * WINDOW, WINDOW), WINDOW)
            crow = pl.ds(pl.multiple_of(blk * WINDOW, WINDOW), WINDOW)
            for g in range(SWA_HEADS // SWA_GROUP):
                heads = slice(SWA_GROUP * g, SWA_GROUP * (g + 1))
                kk = kfull[:, 64 * g:64 * g + 64]
                vv = vfull[:, 64 * g:64 * g + 64]
                q4 = _stack_heads(q, g)
                pk, psink = _swa_probs(q4, kk, bias_ref[heads].reshape(GROUP_ROWS, 2 * WINDOW),
                                       _group_sinks(sink_ref, g), valid)
                pkb = pk.astype(BF16)
                do4 = _stack_heads(dov, g)
                dob = do4.astype(BF16)
                dp = _dot_nt(dob, vv)
                delta = jnp.sum(do4 * _stack_heads(ov, g), axis=-1, keepdims=True)
                ds = pk * (dp - delta)
                dsink = -psink * delta
                for a in range(SWA_GROUP):
                    h = SWA_GROUP * g + a
                    part = jnp.sum(dsink[WINDOW * a:WINDOW * (a + 1)], keepdims=True)
                    dsk_ref[h:h + 1, :] += jnp.broadcast_to(part, (1, 128))
                dbias_acc[heads] += ds.reshape(SWA_GROUP, WINDOW, 2 * WINDOW)
                dsb = (ds * SWA_SCALE).astype(BF16)
                dq_ref[rows, 256 * g:256 * (g + 1)] = _unstack_heads(_dot(dsb, kk))
                dkk = _dot_tn(dsb, q4)
                dvv = _dot_tn(pkb, dob)
                dk_ref[prow, 64 * g:64 * g + 64] += dkk[:WINDOW]
                dk_ref[crow, 64 * g:64 * g + 64] += dkk[WINDOW:]
                dv_ref[prow, 64 * g:64 * g + 64] += dvv[:WINDOW]
                dv_ref[crow, 64 * g:64 * g + 64] += dvv[WINDOW:]

        @pl.when(n == nb - 1)
        def _():
            bk = bk_ref[...]
            for h in range(SWA_HEADS):
                dbh = dbias_acc[h]
                for b in range(NUM_BUCKETS):
                    val = jnp.sum(jnp.where(bk == b, dbh, 0.0), keepdims=True)
                    drb_ref[b * 8 + h:b * 8 + h + 1, :] = jnp.broadcast_to(val, (1, 128))

    full = lambda shape: pl.BlockSpec(shape, lambda n: tuple(0 for _ in shape))
    return pl.pallas_call(
        body, name="swa_bwd", grid=(nb,),
        in_specs=_swa_specs() + [pl.BlockSpec((SWA_ROWS, 512), lambda n: (n, 0)),
                                 pl.BlockSpec((SWA_ROWS, 512), lambda n: (n, 0)), full((WINDOW, 2 * WINDOW))],
        out_specs=[pl.BlockSpec((SWA_ROWS, 512), lambda n: (n, 0)), full((S, 128)), full((S, 128)),
                   full((NUM_BUCKETS * 8, 128)), full((8, 128))],
        out_shape=[jax.ShapeDtypeStruct((S, 512), F32), jax.ShapeDtypeStruct((S, 128), F32),
                   jax.ShapeDtypeStruct((S, 128), F32), jax.ShapeDtypeStruct((NUM_BUCKETS * 8, 128), F32),
                   jax.ShapeDtypeStruct((8, 128), F32)],
        scratch_shapes=[pltpu.VMEM((SWA_HEADS, WINDOW, 2 * WINDOW), F32)],
        compiler_params=_params(("arbitrary",)),
    )(proj, proj, proj, proj, proj, bias, sinks, o, do, bucket)


def _rope_tables(S):
    inv = np.float32(ROPE_THETA) ** (-np.arange(0, MLA_ROPE, 2, dtype=np.float32) / np.float32(MLA_ROPE))
    ang = np.arange(S, dtype=np.float32)[:, None] * inv[None, :]
    cos, sin = np.cos(ang), np.sin(ang)
    return (jnp.asarray(np.tile(np.concatenate([cos, cos], axis=1), (1, 2))),
            jnp.asarray(np.tile(np.concatenate([-sin, sin], axis=1), (1, 2))))


def _rope_wide(ref):
    t = ref[...]
    return jnp.concatenate([t, t], axis=1)


def _swap_halves(x):
    w = x.shape[-1]
    lane = lax.broadcasted_iota(jnp.int32, x.shape, x.ndim - 1)
    return jnp.where((lane % 64) < 32, pltpu.roll(x, w - 32, x.ndim - 1), pltpu.roll(x, 32, x.ndim - 1))


def _mla_pre_fwd(proj, qn_w, kvn_w, wuqT, wukv, cos, sin, *, tm=ROW_TILE):
    S = proj.shape[0]
    tm = min(tm, S)

    def body(ql_ref, kl_ref, kr_ref, qw_ref, kw_ref, wuq_ref, wukv_ref, cos_ref, sin_ref,
             qc_ref, kc_ref, vv_ref):
        ql = ql_ref[...]
        qn = (ql * _rstd(ql) * qw_ref[...]).astype(BF16)
        q = _dot_nt(qn, wuq_ref[...])
        cs, sn = _rope_wide(cos_ref), _rope_wide(sin_ref)
        qr = q[:, 512:768]
        qr = qr * cs + _swap_halves(qr) * sn
        half = lax.broadcasted_iota(jnp.int32, (tm, 128), 1) // 64
        kl = kl_ref[...]
        kvn = (kl * _rstd(kl) * kw_ref[...]).astype(BF16)
        kr = kr_ref[...]
        kr = kr * cs[:, :128] + _swap_halves(kr) * sn[:, :128]
        kr2 = (kr + pltpu.roll(kr, 64, 1)).astype(BF16)
        for h in range(MLA_HEADS):
            qc_ref[h, :, 0:128] = q[:, 128 * h:128 * h + 128].astype(BF16)
            chunk = qr[:, 128 * (h // 2):128 * (h // 2) + 128]
            qc_ref[h, :, 128:256] = jnp.where(half == (h % 2), chunk, 0.0).astype(BF16)
            kc_ref[h, :, 0:128] = _dot(kvn, wukv_ref[2 * h]).astype(BF16)
            kc_ref[h, :, 128:256] = kr2
            vv_ref[h] = _dot(kvn, wukv_ref[2 * h + 1]).astype(BF16)

    const = lambda shape: pl.BlockSpec(shape, lambda i: tuple(0 for _ in shape))
    return pl.pallas_call(
        body, name="mla_pre_fwd", grid=(S // tm,),
        in_specs=[pl.BlockSpec((tm, 256), lambda i: (i, 3)), pl.BlockSpec((tm, 128), lambda i: (i, 8)),
                  pl.BlockSpec((tm, 128), lambda i: (i, 9)), const((1, 256)), const((1, 128)),
                  const((768, 256)), const((8, 128, 128)),
                  pl.BlockSpec((tm, 128), lambda i: (i, 0)), pl.BlockSpec((tm, 128), lambda i: (i, 0))],
        out_specs=[pl.BlockSpec((MLA_HEADS, tm, 256), lambda i: (0, i, 0)),
                   pl.BlockSpec((MLA_HEADS, tm, 256), lambda i: (0, i, 0)),
                   pl.BlockSpec((MLA_HEADS, tm, 128), lambda i: (0, i, 0))],
        out_shape=[jax.ShapeDtypeStruct((MLA_HEADS, S, 256), BF16), jax.ShapeDtypeStruct((MLA_HEADS, S, 256), BF16),
                   jax.ShapeDtypeStruct((MLA_HEADS, S, 128), BF16)],
        compiler_params=_params(("parallel",)),
    )(proj, proj, proj, qn_w, kvn_w, wuqT, wukv, cos, sin)


def _causal(i, j, t):
    row = i * t + lax.broadcasted_iota(jnp.int32, (t, t), 0)
    col = j * t + lax.broadcasted_iota(jnp.int32, (t, t), 1)
    return col <= row


def _mla_attn_fwd(qc, kc, vv, *, t=512):
    S = qc.shape[1]
    t = min(t, S)

    def body(q_ref, k_ref, v_ref, o_ref, l_ref):
        i = pl.program_id(0)
        diag = _causal(0, 0, t)

        def step(j, carry, masked):
            rows = pl.ds(pl.multiple_of(j * t, t), t)
            out = []
            for h in range(MLA_HEADS):
                m, l, acc = carry[h]
                s = _dot_nt(q_ref[h], k_ref[h, rows, :]) * MLA_SCALE
                if masked:
                    s = jnp.where(diag, s, -jnp.inf)
                m_new = jnp.maximum(m, jnp.max(s, axis=-1, keepdims=True))
                alpha = jnp.exp(m - m_new)
                p = jnp.exp(s - m_new)
                l = alpha * l + jnp.sum(p, axis=-1, keepdims=True)
                acc = alpha * acc + _dot(p.astype(BF16), v_ref[h, rows, :])
                out.append((m_new, l, acc))
            return tuple(out)

        init = tuple((jnp.full((t, 1), -jnp.inf, F32), jnp.zeros((t, 1), F32), jnp.zeros((t, MLA_V), F32))
                     for _ in range(MLA_HEADS))
        carry = lax.fori_loop(0, i, lambda j, c: step(j, c, False), init)
        carry = step(i, carry, True)
        for h in range(MLA_HEADS):
            m, l, acc = carry[h]
            o_ref[:, 128 * h:128 * h + 128] = acc / l
            l_ref[h] = jnp.broadcast_to(m + jnp.log(l), (t, 128))

    return pl.pallas_call(
        body, name="mla_attn_fwd", grid=(S // t,),
        in_specs=[pl.BlockSpec((MLA_HEADS, t, 256), lambda i: (0, i, 0)),
                  pl.BlockSpec((MLA_HEADS, S, 256), lambda i: (0, 0, 0)),
                  pl.BlockSpec((MLA_HEADS, S, 128), lambda i: (0, 0, 0))],
        out_specs=[pl.BlockSpec((t, 512), lambda i: (i, 0)),
                   pl.BlockSpec((MLA_HEADS, t, 128), lambda i: (0, i, 0))],
        out_shape=[jax.ShapeDtypeStruct((S, 512), F32), jax.ShapeDtypeStruct((MLA_HEADS, S, 128), F32)],
        compiler_params=_params(("parallel",)),
    )(qc, kc, vv)


def _mla_attn_bwd(qc, kc, vv, o, lse, do, *, t=512, tq=1024):
    S = qc.shape[1]
    t = min(t, S)
    tq = min(tq, S)
    nblk = S // t
    hp = MLA_HEADS
    once = pl.Buffered(1)

    def body(q_ref, k_ref, v_ref, o_ref, l_ref, do_ref, dq_ref, dk_ref, dv_ref):
        j = pl.program_id(1)

        @pl.when(j == 0)
        def _():
            dq_ref[...] = jnp.zeros_like(dq_ref)

        first = (j * t) // tq

        def step(i, carry, masked):
            rows = pl.ds(pl.multiple_of(i * tq, tq), tq)
            if masked:
                row = i * tq + lax.broadcasted_iota(jnp.int32, (tq, t), 0)
                col = j * t + lax.broadcasted_iota(jnp.int32, (tq, t), 1)
                visible = col <= row
            out = []
            for h in range(hp):
                dk, dv = carry[h]
                k = k_ref[h]
                q = q_ref[h, rows, :]
                dov = do_ref[rows, 128 * h:128 * h + 128]
                lrow = l_ref[h, rows, :][:, 0:1]
                p = jnp.exp(_dot_nt(q, k) * MLA_SCALE - lrow)
                if masked:
                    p = jnp.where(visible, p, 0.0)
                dob = dov.astype(BF16)
                dv = dv + _dot_tn(p.astype(BF16), dob)
                dp = _dot_nt(dob, v_ref[h])
                delta = jnp.sum(dov * o_ref[rows, 128 * h:128 * h + 128], axis=-1, keepdims=True)
                ds = (p * (dp - delta) * MLA_SCALE).astype(BF16)
                dk = dk + _dot_tn(ds, q)
                dq_ref[h, rows, :] += _dot(ds, k)
                out.append((dk, dv))
            return tuple(out)

        init = tuple((jnp.zeros((t, 256), F32), jnp.zeros((t, MLA_V), F32)) for _ in range(hp))
        carry = step(first, init, True)
        carry = lax.fori_loop(first + 1, S // tq, lambda i, c: step(i, c, False), carry)
        for h in range(hp):
            dk_ref[h] = carry[h][0]
            dv_ref[h] = carry[h][1]

    return pl.pallas_call(
        body, name="mla_attn_bwd", grid=(MLA_HEADS // hp, nblk),
        in_specs=[pl.BlockSpec((hp, S, 256), lambda g, j: (g, 0, 0), pipeline_mode=once),
                  pl.BlockSpec((hp, t, 256), lambda g, j: (g, j, 0)),
                  pl.BlockSpec((hp, t, 128), lambda g, j: (g, j, 0)),
                  pl.BlockSpec((S, 128 * hp), lambda g, j: (0, g), pipeline_mode=once),
                  pl.BlockSpec((hp, S, 128), lambda g, j: (g, 0, 0), pipeline_mode=once),
                  pl.BlockSpec((S, 128 * hp), lambda g, j: (0, g), pipeline_mode=once)],
        out_specs=[pl.BlockSpec((hp, S, 256), lambda g, j: (g, 0, 0)),
                   pl.BlockSpec((hp, t, 256), lambda g, j: (g, j, 0)),
                   pl.BlockSpec((hp, t, 128), lambda g, j: (g, j, 0))],
        out_shape=[jax.ShapeDtypeStruct((MLA_HEADS, S, 256), F32), jax.ShapeDtypeStruct((MLA_HEADS, S, 256), F32),
                   jax.ShapeDtypeStruct((MLA_HEADS, S, 128), F32)],
        compiler_params=_params(("parallel", "arbitrary")),
    )(qc, kc, vv, o, lse, do)


def _mla_pre_bwd(proj, qn_w, kvn_w, wuqT, wukv, cos, sin, dqc, dkc, dvv, *, tm=ROW_TILE):
    S = proj.shape[0]
    tm = min(tm, S)

    def body(ql_ref, kl_ref, qw_ref, kw_ref, wuq_ref, wukv_ref, cos_ref, sin_ref, dqc_ref, dkc_ref, dvv_ref,
             dql_ref, dkl_ref, dkr_ref, gq_ref, gkv_ref, part_ref):
        @pl.when(pl.program_id(0) == 0)
        def _():
            gq_ref[...] = jnp.zeros_like(gq_ref)
            gkv_ref[...] = jnp.zeros_like(gkv_ref)
            part_ref[...] = jnp.zeros_like(part_ref)

        cs, sn = _rope_wide(cos_ref), _rope_wide(sin_ref)
        half = lax.broadcasted_iota(jnp.int32, (tm, 128), 1) // 64
        ql = ql_ref[...]
        rq = _rstd(ql)
        qhat = ql * rq
        qw = qw_ref[...]
        qn = (qhat * qw).astype(BF16)
        chunks = []
        for pair in range(2):
            chunks.append(jnp.where(half == 0, dqc_ref[2 * pair, :, 128:256], dqc_ref[2 * pair + 1, :, 128:256]))
        dqr = jnp.concatenate(chunks, axis=1)
        dqr = dqr * cs + _swap_halves(dqr * sn)
        dq = jnp.concatenate([dqc_ref[h, :, 0:128] for h in range(MLA_HEADS)] + [dqr], axis=1).astype(BF16)
        gq_ref[...] += _dot_tn(dq, qn)
        dqn = _dot(dq, wuq_ref[...])
        part_ref[0:1, :] += jnp.sum(dqn * qhat, axis=0, keepdims=True)
        dql_ref[...] = _rms_bwd(dqn * qw, qhat, rq)
        kl = kl_ref[...]
        rk = _rstd(kl)
        khat = kl * rk
        kw = kw_ref[...]
        kvn = (khat * kw).astype(BF16)
        dkvn = jnp.zeros((tm, MLA_KVR), F32)
        dkr2 = jnp.zeros((tm, 128), F32)
        for h in range(MLA_HEADS):
            dkn = dkc_ref[h, :, 0:128].astype(BF16)
            dvh = dvv_ref[h].astype(BF16)
            gkv_ref[2 * h] += _dot_tn(kvn, dkn)
            gkv_ref[2 * h + 1] += _dot_tn(kvn, dvh)
            dkvn += _dot_nt(dkn, wukv_ref[2 * h]) + _dot_nt(dvh, wukv_ref[2 * h + 1])
            dkr2 += dkc_ref[h, :, 128:256]
        part_ref[1:2, 0:128] += jnp.sum(dkvn * khat, axis=0, keepdims=True)
        dkl_ref[...] = _rms_bwd(dkvn * kw, khat, rk)
        dkr = jnp.where(half == 0, dkr2 + pltpu.roll(dkr2, 64, 1), 0.0)
        dkr_ref[...] = dkr * cs[:, :128] + _swap_halves(dkr * sn[:, :128])

    const = lambda shape: pl.BlockSpec(shape, lambda i: tuple(0 for _ in shape))
    heads = lambda w: pl.BlockSpec((MLA_HEADS, tm, w), lambda i: (0, i, 0))
    return pl.pallas_call(
        body, name="mla_pre_bwd", grid=(S // tm,),
        in_specs=[pl.BlockSpec((tm, 256), lambda i: (i, 3)), pl.BlockSpec((tm, 128), lambda i: (i, 8)),
                  const((1, 256)), const((1, 128)), const((768, 256)), const((8, 128, 128)),
                  pl.BlockSpec((tm, 128), lambda i: (i, 0)), pl.BlockSpec((tm, 128), lambda i: (i, 0)),
                  heads(256), heads(256), heads(128)],
        out_specs=[pl.BlockSpec((tm, 256), lambda i: (i, 0)), pl.BlockSpec((tm, 128), lambda i: (i, 0)),
                   pl.BlockSpec((tm, 128), lambda i: (i, 0)), const((768, 256)), const((8, 128, 128)), const((8, 256))],
        out_shape=[jax.ShapeDtypeStruct((S, 256), F32), jax.ShapeDtypeStruct((S, 128), F32),
                   jax.ShapeDtypeStruct((S, 128), F32), jax.ShapeDtypeStruct((768, 256), F32),
                   jax.ShapeDtypeStruct((8, 128, 128), F32), jax.ShapeDtypeStruct((8, 256), F32)],
        compiler_params=_params(("arbitrary",)),
    )(proj, proj, qn_w, kvn_w, wuqT, wukv, cos, sin, dqc, dkc, dvv)


def _mix_out_fwd(x, oa, ob, w_o, vecs, *, tm=ROW_TILE):
    S = x.shape[0]
    tm = min(tm, S)

    def body(x_ref, oa_ref, ob_ref, w_ref, vec_ref, xo_ref, mo_ref):
        mo = _dot(oa_ref[...].astype(BF16), w_ref[0:512, :]) + _dot(ob_ref[...].astype(BF16), w_ref[512:1024, :])
        mo_ref[...] = mo
        xo_ref[...] = x_ref[...] + vec_ref[3:4, :] * mo

    row = pl.BlockSpec((tm, D), lambda i: (i, 0))
    half = pl.BlockSpec((tm, 512), lambda i: (i, 0))
    return pl.pallas_call(
        body, name="mix_out_fwd", grid=(S // tm,),
        in_specs=[row, half, half, pl.BlockSpec((D, D), lambda i: (0, 0)), pl.BlockSpec((8, D), lambda i: (0, 0))],
        out_specs=[row, row],
        out_shape=[jax.ShapeDtypeStruct((S, D), F32), jax.ShapeDtypeStruct((S, D), F32)],
        compiler_params=_params(("parallel",)),
    )(x, oa, ob, w_o, vecs)


def _mix_out_bwd(dxo, mo, oa, ob, w_o, vecs, *, tm=ROW_TILE):
    S = dxo.shape[0]
    tm = min(tm, S)

    def body(dx_ref, mo_ref, oa_ref, ob_ref, w_ref, vec_ref, doa_ref, dob_ref, gw_ref, part_ref):
        @pl.when(pl.program_id(0) == 0)
        def _():
            gw_ref[...] = jnp.zeros_like(gw_ref)
            part_ref[...] = jnp.zeros_like(part_ref)

        dx = dx_ref[...]
        part_ref[0:1, :] += jnp.sum(dx * mo_ref[...], axis=0, keepdims=True)
        dmo = (vec_ref[3:4, :] * dx).astype(BF16)
        doa_ref[...] = _dot_nt(dmo, w_ref[0:512, :])
        dob_ref[...] = _dot_nt(dmo, w_ref[512:1024, :])
        gw_ref[0:512, :] += _dot_tn(oa_ref[...].astype(BF16), dmo)
        gw_ref[512:1024, :] += _dot_tn(ob_ref[...].astype(BF16), dmo)

    row = pl.BlockSpec((tm, D), lambda i: (i, 0))
    half = pl.BlockSpec((tm, 512), lambda i: (i, 0))
    return pl.pallas_call(
        body, name="mix_out_bwd", grid=(S // tm,),
        in_specs=[row, row, half, half, pl.BlockSpec((D, D), lambda i: (0, 0)), pl.BlockSpec((8, D), lambda i: (0, 0))],
        out_specs=[half, half, pl.BlockSpec((D, D), lambda i: (0, 0)), pl.BlockSpec((8, D), lambda i: (0, 0))],
        out_shape=[jax.ShapeDtypeStruct((S, 512), F32), jax.ShapeDtypeStruct((S, 512), F32),
                   jax.ShapeDtypeStruct((D, D), F32), jax.ShapeDtypeStruct((8, D), F32)],
        compiler_params=_params(("arbitrary",)),
    )(dxo, mo, oa, ob, w_o, vecs)


def _mix_in_bwd(h, w_inT, dq, dk, dv, dql, dkl, dkr, *, tm=ROW_TILE):
    S = h.shape[0]
    tm = min(tm, S)
    offs = (0, 512, 640, 768, 1024, 1152)
    wid = (512, 128, 128, 256, 128, 128)

    def body(h_ref, w_ref, dq_ref, dk_ref, dv_ref, dql_ref, dkl_ref, dkr_ref, dh_ref, gw_ref):
        @pl.when(pl.program_id(0) == 0)
        def _():
            gw_ref[...] = jnp.zeros_like(gw_ref)

        hv = h_ref[...]
        dh = jnp.zeros((tm, D), F32)
        for ref, o, w in zip((dq_ref, dk_ref, dv_ref, dql_ref, dkl_ref, dkr_ref), offs, wid):
            w = min(w, D_IN - o)
            dpart = ref[...][:, :w].astype(BF16)
            dh += _dot(dpart, w_ref[o:o + w, :])
            gw_ref[o:o + w, :] += _dot_tn(dpart, hv)
        dh_ref[...] = dh

    row = pl.BlockSpec((tm, D), lambda i: (i, 0))
    part = lambda w: pl.BlockSpec((tm, w), lambda i: (i, 0))
    return pl.pallas_call(
        body, name="mix_in_bwd", grid=(S // tm,),
        in_specs=[row, pl.BlockSpec((D_IN_PAD, D), lambda i: (0, 0))] + [part(w) for w in wid],
        out_specs=[row, pl.BlockSpec((D_IN, D), lambda i: (0, 0))],
        out_shape=[jax.ShapeDtypeStruct((S, D), F32), jax.ShapeDtypeStruct((D_IN, D), F32)],
        compiler_params=_params(("arbitrary",)),
    )(h, w_inT, dq, dk, dv, dql, dkl, dkr)


def _vecs(norm_w, mod9, k):
    return jnp.concatenate([norm_w.reshape(1, D), mod9[3 * k:3 * k + 3], jnp.zeros((4, D), F32)], axis=0)


def _uq_group_rows(wuqT):
    per = MLA_NOPE + MLA_ROPE
    nope = [wuqT[per * h:per * h + MLA_NOPE] for h in range(MLA_HEADS)]
    rope = [wuqT[per * h + MLA_NOPE:per * (h + 1)] for h in range(MLA_HEADS)]
    return jnp.concatenate(nope + rope, axis=0)


def _uq_ungroup_rows(g):
    parts = []
    for h in range(MLA_HEADS):
        parts += [g[MLA_NOPE * h:MLA_NOPE * (h + 1)], g[512 + MLA_ROPE * h:512 + MLA_ROPE * (h + 1)]]
    return jnp.concatenate(parts, axis=0)


def _local_step(x, tgt, mod9, norms, sinks, rel_bias, q_norm, kv_norm, W, on_grads=None):
    if on_grads is None:
        on_grads = lambda group, grads, after, vecs: vecs
    S = x.shape[0]
    v1 = _vecs(norms["ffn1"], mod9, 0)
    v2 = _vecs(norms["mix"], mod9, 1)
    v3 = _vecs(norms["ffn2"], mod9, 2)
    bucket = jnp.asarray(_bucket_table())
    cos, sin = _rope_tables(S)
    if isinstance(W, dict):
        full, W = W, (lambda group, after, vecs: (full, vecs))

    W1, v1 = W("ffn1", [], v1)
    x1, h1, a1, b1, f1 = _ffn_fwd(x, v1, W1["g1T"], W1["u1T"], W1["d1"], name="ffn1_fwd")
    W2, v2 = W("mixer", [x1], v2)
    w_inT = jnp.pad(W2["w_inT"], ((0, D_IN_PAD - D_IN), (0, 0))).astype(BF16)
    wuqT = _uq_group_rows(W2["w_uqT"])
    h2, proj = _mix_in_fwd(x1, v2, w_inT)
    bias = _bias_build(rel_bias, bucket)
    oa = _swa_fwd(proj, bias, sinks)
    qc, kc, vv = _mla_pre_fwd(proj, q_norm, kv_norm, wuqT, W2["w_ukv"], cos, sin)
    ob, lse = _mla_attn_fwd(qc, kc, vv)
    _, v2o = W("ffn2_on_its_way", [ob], v2)
    x2, mo = _mix_out_fwd(x1, oa, ob, W2["w_o"], v2o)
    W3, v3 = W("ffn2", [x2], v3)
    x3, h3, a3, b3, f3 = _ffn_fwd(x2, v3, W3["g3T"], W3["u3T"], W3["d3"], name="ffn2_fwd")
    dx3, head_part, df3 = _head(x3, tgt, norms["final"], f3, v3)

    gg3, gu3, gd3, dh3 = _ffn_bwd_main(h3, df3, a3, b3, W3["g3T"], W3["u3T"], W3["d3"], name="ffn2_bwd")
    ffn2 = {"g3T": gg3, "u3T": gu3, "d3": gd3}
    v3 = on_grads("ffn2", ffn2, [], v3)
    dx2, n3_part = _norm_bwd(dh3, x2, dx3, v3, name="ffn2_norm_bwd")
    v2 = on_grads("ffn2", None, [dx2], v2)
    doa, dob, g_wo, g2_part = _mix_out_bwd(dx2, mo, oa, ob, W2["w_o"], v2)
    dq, dk, dv, drb, dsk = _swa_bwd(proj, bias, sinks, oa, doa, bucket)
    dqc, dkc, dvv = _mla_attn_bwd(qc, kc, vv, ob, lse, dob)
    dql, dkl, dkr, g_uq, g_ukv, mla_part = _mla_pre_bwd(proj, q_norm, kv_norm, wuqT, W2["w_ukv"], cos, sin, dqc, dkc, dvv)
    dh2, g_win = _mix_in_bwd(h2, w_inT, dq, dk, dv, dql, dkl, dkr)
    mixer = {"w_inT": g_win, "w_uqT": _uq_ungroup_rows(g_uq).astype(BF16),
             "w_ukv": g_ukv.astype(BF16), "w_o": g_wo.astype(BF16)}
    v2 = on_grads("mixer", mixer, [], v2)
    dx1, n2_part, df1 = _norm_bwd(dh2, x1, dx2, v2, name="mix_norm_bwd", below=(f1, v1))
    started = on_grads("mixer", None, [dx1], jnp.zeros((1, 1), F32))
    gg1, gu1, gd1, dh1 = _ffn_bwd_main(h1, df1, a1, b1, W1["g1T"], W1["u1T"], W1["d1"], name="ffn1_bwd",
                                       after=[started])
    ffn1 = {"g1T": gg1, "u1T": gu1, "d1": gd1}
    v1 = on_grads("ffn1", ffn1, [], v1)
    dx0, n1_part = _norm_bwd(dh1, x, dx1, v1, name="ffn1_norm_bwd")

    grads = {**ffn1, **ffn2, **mixer}
    return head_part[1, 0], dx0, grads, _pack_vec(n1_part, n2_part, n3_part, head_part, g2_part, mla_part, dsk, drb)


SMALL_LAYOUT = (("norm_ffn1", 1024), ("norm_mix", 1024), ("norm_ffn2", 1024), ("norm_final", 1024),
                ("q_norm", 256), ("kv_norm", 128), ("sinks", 128), ("rel_bias", 256))
N_SMALL = sum(n for _, n in SMALL_LAYOUT)
LOSS_SLOT = 4 * 1024 + 256 + 128 + SWA_HEADS
N_MODVEC = N_MOD * D
N_VEC = N_MODVEC + N_SMALL


def _pack_vec(n1, n2, n3, head, g2, mla, dsk, drb):
    def body(n1_ref, n2_ref, n3_ref, head_ref, g2_ref, mla_ref, dsk_ref, drb_ref, out_ref):
        rows = [n1_ref[1:2, :], n1_ref[2:3, :], n2_ref[3:4, :], n2_ref[1:2, :], n2_ref[2:3, :], g2_ref[0:1, :],
                n3_ref[1:2, :], n3_ref[2:3, :], head_ref[3:4, :],
                n1_ref[0:1, :], n2_ref[0:1, :], n3_ref[0:1, :], head_ref[0:1, :]]
        for i, row in enumerate(rows):
            out_ref[:, D * i:D * (i + 1)] = row
        off = D * len(rows)
        out_ref[:, off:off + 256] = mla_ref[0:1, :]
        out_ref[:, off + 256:off + 384] = mla_ref[1:2, 0:128]

        def diagonal(block):
            r = lax.broadcasted_iota(jnp.int32, block.shape, 0)
            lane = lax.broadcasted_iota(jnp.int32, block.shape, 1)
            return jnp.sum(jnp.where(r == lane, block, 0.0), axis=0, keepdims=True)

        lane = lax.broadcasted_iota(jnp.int32, (1, 128), 1)
        out_ref[:, off + 384:off + 512] = jnp.where(lane == SWA_HEADS, head_ref[1:2, 0:128], diagonal(dsk_ref[...]))
        out_ref[:, off + 512:off + 640] = diagonal(drb_ref[0:128, :])
        out_ref[:, off + 640:off + 768] = diagonal(drb_ref[128:256, :])

    vm = pl.BlockSpec(memory_space=pltpu.VMEM)
    return pl.pallas_call(body, name="pack_vec", in_specs=[vm] * 8, out_specs=vm,
                          out_shape=jax.ShapeDtypeStruct((1, N_VEC), F32))(n1, n2, n3, head, g2, mla, dsk, drb)


def _coords():
    return lax.axis_index("x"), lax.axis_index("y"), lax.axis_index("c")


def _flip(v, bit):
    return 1 - v if bit else v


def _peer(r):
    x, y, c = _coords()
    return (_flip(x, r & 4), _flip(y, r & 2), _flip(c, r & 1))


def _mod_fwd(c_tile, w_mod, b_mod3):
    W = w_mod.shape[1]

    def body(c_ref, w_ref, b_ref, mod_ref, ca_ref, call_ref, part_ref, send_sems, recv_sems):
        x, y, c = _coords()
        me = 4 * x + 2 * y + c
        call_ref[me] = c_ref[...]
        sends = []
        for r in range(1, N_DEV):
            cp = pltpu.make_async_remote_copy(c_ref, call_ref.at[me], send_sems.at[0, r], recv_sems.at[0, r],
                                              device_id=_peer(r), device_id_type=MESH)
            cp.start()
            sends.append(cp)
        for r in range(1, N_DEV):
            pltpu.make_async_remote_copy(c_ref, call_ref.at[me], send_sems.at[0, r], recv_sems.at[0, r],
                                         device_id=_peer(r), device_id_type=MESH).wait_recv()
        cv = call_ref[...].reshape(8 * N_DEV, D)
        ca = (cv * _sigmoid(cv)).astype(BF16)
        ca_ref[...] = ca
        part_ref[...] = _dot(ca, w_ref[...].astype(BF16)).reshape(N_DEV, 8, W)
        mod_ref[me] = part_ref[me] + b_ref[me]
        for r in range(1, N_DEV):
            cp = pltpu.make_async_remote_copy(part_ref.at[me ^ r], mod_ref.at[me], send_sems.at[1, r],
                                              recv_sems.at[1, r], device_id=_peer(r), device_id_type=MESH)
            cp.start()
            sends.append(cp)
        for r in range(1, N_DEV):
            pltpu.make_async_remote_copy(part_ref.at[me ^ r], mod_ref.at[me], send_sems.at[1, r],
                                         recv_sems.at[1, r], device_id=_peer(r), device_id_type=MESH).wait_recv()
            mod_ref[me ^ r] = mod_ref[me ^ r] + b_ref[me ^ r]
        for cp in sends:
            cp.wait_send()

    vm = pl.BlockSpec(memory_space=pltpu.VMEM)
    return pl.pallas_call(
        body, name="mod_fwd", in_specs=[vm, vm, vm], out_specs=[vm, vm],
        out_shape=[jax.ShapeDtypeStruct((N_DEV, 8, W), F32), jax.ShapeDtypeStruct((8 * N_DEV, D), BF16)],
        scratch_shapes=[pltpu.VMEM((N_DEV, 8, D), F32), pltpu.VMEM((N_DEV, 8, W), F32),
                        pltpu.SemaphoreType.DMA((2, N_DEV)), pltpu.SemaphoreType.DMA((2, N_DEV))],
        compiler_params=_params(),
    )(c_tile, w_mod, b_mod3)


def _mod_bwd(allvec, ca, me_idx):
    W = N_MODVEC // N_DEV

    def body(me_ref, all_ref, cols_ref, ca_ref, gw_ref, sum_ref):
        in_first_row = lax.broadcasted_iota(jnp.int32, (N_DEV, 8, W), 1) == 0
        dm = jnp.where(in_first_row, cols_ref[...], 0.0).reshape(8 * N_DEV, W)
        gw_ref[...] = _dot_tn(ca_ref[...], dm.astype(BF16))
        total = all_ref[0]
        for k in range(1, N_DEV):
            total = total + all_ref[k]
        sum_ref[...] = total

    return pl.pallas_call(
        body, name="mod_bwd",
        grid_spec=pltpu.PrefetchScalarGridSpec(
            num_scalar_prefetch=1, grid=(1,),
            in_specs=[pl.BlockSpec((N_DEV, 1, N_VEC), lambda i, me: (0, 0, 0)),
                      pl.BlockSpec((N_DEV, 1, W), lambda i, me: (0, 0, me[0])),
                      pl.BlockSpec((8 * N_DEV, D), lambda i, me: (0, 0))],
            out_specs=[pl.BlockSpec((D, W), lambda i, me: (0, 0)), pl.BlockSpec((1, N_VEC), lambda i, me: (0, 0))]),
        out_shape=[jax.ShapeDtypeStruct((D, W), F32), jax.ShapeDtypeStruct((1, N_VEC), F32)],
        compiler_params=_params(("arbitrary",)),
    )(me_idx, allvec, allvec, ca)


def _wgather(shards):
    n = len(shards)
    rows = [s.shape[0] for s in shards]

    def body(*refs):
        ins, outs, token = refs[:n], refs[n:2 * n], refs[2 * n]
        send_sems, recv_sems, local_sems = refs[2 * n + 1:]
        token[...] = jnp.zeros_like(token)
        x, y, c = _coords()
        me = 4 * x + 2 * y + c
        sib, xn, yn = (x, y, 1 - c), (1 - x, y, c), (x, 1 - y, c)
        block = lambda px, py, pc: 4 * px + 2 * py + pc

        def part(k, blk, half):
            if half is None:
                return outs[k].at[blk]
            return outs[k].at[blk, pl.ds(half * (rows[k] // 2), rows[k] // 2)]

        def copy(k, slot, blk, to, half=None, src=None):
            ref = part(k, blk, half)
            return pltpu.make_async_remote_copy(
                src_ref=ref if src is None else src, dst_ref=ref, send_sem=send_sems.at[k, slot],
                recv_sem=recv_sems.at[k, slot], device_id=to, device_id_type=MESH)

        local = [pltpu.make_async_copy(ins[k], outs[k].at[me], local_sems.at[k]) for k in range(n)]
        for cp in local:
            cp.start()
        sent = [copy(k, slot, me, to, src=ins[k]) for k in range(n) for slot, to in ((0, sib), (1, xn), (2, yn))]
        for cp in sent:
            cp.start()
        bx, by, bd = block(1 - x, y, c), block(x, 1 - y, c), block(1 - x, 1 - y, c)
        for k in range(n):
            copy(k, 1, bx, sib).wait_recv()
            sent += [copy(k, 4, bx, yn, half=1), copy(k, 5, bx, sib)]
            sent[-2].start()
            sent[-1].start()
        for k in range(n):
            copy(k, 2, by, sib).wait_recv()
            sent += [copy(k, 3, by, xn, half=0), copy(k, 6, by, sib)]
            sent[-2].start()
            sent[-1].start()
        for k in range(n):
            copy(k, 3, bd, sib, half=0).wait_recv()
            copy(k, 4, bd, sib, half=1).wait_recv()
            sent.append(copy(k, 7, bd, sib))
            sent[-1].start()
        for k in range(n):
            copy(k, 0, block(x, y, 1 - c), sib).wait_recv()
            for slot, blk in ((5, block(1 - x, y, 1 - c)), (6, block(x, 1 - y, 1 - c)), (7, block(1 - x, 1 - y, 1 - c))):
                copy(k, slot, blk, sib).wait_recv()
        for cp in sent:
            cp.wait_send()
        for cp in local:
            cp.wait()

    anyspec = pl.BlockSpec(memory_space=pl.ANY)
    return pl.pallas_call(
        body, name="wgather", in_specs=[anyspec] * n,
        out_specs=[anyspec] * n + [pl.BlockSpec(memory_space=pltpu.VMEM)],
        out_shape=[jax.ShapeDtypeStruct((N_DEV,) + s.shape, s.dtype) for s in shards]
        + [jax.ShapeDtypeStruct((8, 128), F32)],
        scratch_shapes=[pltpu.SemaphoreType.DMA((n, 8)), pltpu.SemaphoreType.DMA((n, 8)),
                        pltpu.SemaphoreType.DMA((n,))],
    )(*shards)


class _GatherCopies:
    def __init__(self, lands, send_sems, recv_sems, k0=0, batches=None):
        x, y, c = _coords()
        me = 4 * x + 2 * y + c
        sib = (x, y, 1 - c)
        chips = [(1 - x, y), (x, 1 - y), (1 - x, 1 - y)]

        def copy(k, slot, block, to):
            return pltpu.make_async_remote_copy(
                src_ref=lands[k].at[block], dst_ref=lands[k].at[block],
                send_sem=send_sems.at[7 * (k0 + k) + slot], recv_sem=recv_sems.at[7 * (k0 + k) + slot],
                device_id=to, device_id_type=MESH)

        n = len(lands)
        self.first = [copy(k, 0, me, sib) for k in range(n)]
        for batch in batches or [range(n)]:
            self.first += [copy(k, 1 + j, me, (cx, cy, c)) for j, (cx, cy) in enumerate(chips) for k in batch]
        self.landed = [copy(k, 1 + j, 4 * cx + 2 * cy + c, sib) for j, (cx, cy) in enumerate(chips) for k in range(n)]
        self.passed = [copy(k, 4 + j, 4 * cx + 2 * cy + c, sib) for j, (cx, cy) in enumerate(chips) for k in range(n)]
        self.from_sib = [copy(k, 0, 4 * x + 2 * y + (1 - c), sib) for k in range(n)]
        self.from_sib += [copy(k, 4 + j, 4 * cx + 2 * cy + (1 - c), sib) for j, (cx, cy) in enumerate(chips)
                          for k in range(n)]


def _gather_start(lands, *, name, batches=None):
    n = len(lands)

    def body(*refs):
        for cp in _GatherCopies(refs[:n], refs[n], refs[n + 1], batches=batches).first:
            cp.start()
        refs[-1][...] = jnp.zeros_like(refs[-1])

    out = pl.pallas_call(
        body, name=name,
        out_shape=(pltpu.SemaphoreType.DMA((7 * n,)), pltpu.SemaphoreType.DMA((7 * n,)),
                   *[pltpu.HBM(l.shape, l.dtype) for l in lands], jax.ShapeDtypeStruct((8, 128), F32)),
        in_specs=[HBM_SPEC] * n,
        out_specs=(SEM_SPEC, SEM_SPEC, *[HBM_SPEC] * n, pl.BlockSpec(memory_space=pltpu.VMEM)),
        input_output_aliases={i: 2 + i for i in range(n)},
        compiler_params=pltpu.CompilerParams(has_side_effects=DATAFLOW),
    )(*[_in_hbm(l) for l in lands])
    return out[0], out[1], list(out[2:2 + n]), out[-1]


def _gather_pass(send_sems, recv_sems, lands, after, *, name, stage, k0=0):
    n = len(lands)

    def body(*refs):
        cps = _GatherCopies(refs[:n], refs[n], refs[n + 1], k0)
        if stage == "landed":
            for cp in cps.landed:
                cp.wait_recv()
        else:
            for cp in cps.passed:
                cp.start()
        refs[-1][...] = jnp.zeros_like(refs[-1])

    out = pl.pallas_call(
        body, name=name,
        out_shape=(*[pltpu.HBM(l.shape, l.dtype) for l in lands], jax.ShapeDtypeStruct((8, 128), F32)),
        in_specs=[HBM_SPEC] * n + [SEM_SPEC, SEM_SPEC] + [pl.BlockSpec(memory_space=pl.ANY)] * len(after),
        out_specs=(*[HBM_SPEC] * n, pl.BlockSpec(memory_space=pltpu.VMEM)),
        input_output_aliases={i: i for i in range(n)},
        compiler_params=pltpu.CompilerParams(has_side_effects=DATAFLOW),
    )(*lands, send_sems, recv_sems, *after)
    return list(out[:n]), out[-1]


def _gather_end(send_sems, recv_sems, lands, after, *, name, k0=0):
    n = len(lands)

    def body(*refs):
        cps = _GatherCopies(refs[:n], refs[n], refs[n + 1], k0)
        for cp in cps.from_sib:
            cp.wait_recv()
        for cp in cps.first + cps.passed:
            cp.wait_send()

    out = pl.pallas_call(
        body, name=name,
        out_shape=[pltpu.HBM(l.shape, l.dtype) for l in lands],
        in_specs=[HBM_SPEC] * n + [SEM_SPEC, SEM_SPEC] + [pl.BlockSpec(memory_space=pl.ANY)] * len(after),
        out_specs=[HBM_SPEC] * n,
        input_output_aliases={i: i for i in range(n)},
        compiler_params=pltpu.CompilerParams(has_side_effects=DATAFLOW),
    )(*lands, send_sems, recv_sems, *after)
    return list(out)


def _d2d_copies(grads, lands, send_sems, recv_sems):
    x, y, c = _coords()
    return [pltpu.make_async_remote_copy(
        src_ref=grads[k].at[2 * q + (1 - c)], dst_ref=lands[k].at[q],
        send_sem=send_sems.at[4 * k + q], recv_sem=recv_sems.at[4 * k + q],
        device_id=(x, y, 1 - c), device_id_type=MESH) for k in range(len(grads)) for q in range(4)]


def _direct_copies(grads, lands, send_sems, recv_sems):
    x, y, c = _coords()
    me = 4 * x + 2 * y + c
    return [pltpu.make_async_remote_copy(
        src_ref=grads[k].at[me ^ r], dst_ref=lands[k].at[r - 1],
        send_sem=send_sems.at[7 * k + r - 1], recv_sem=recv_sems.at[7 * k + r - 1],
        device_id=_peer(r), device_id_type=MESH) for k in range(len(grads)) for r in range(1, N_DEV)]


def _vec_copies(srcs, lands, send_sems, recv_sems):
    x, y, c = _coords()
    me = 4 * x + 2 * y + c
    return [pltpu.make_async_remote_copy(
        src_ref=lands[0].at[me], dst_ref=lands[0].at[me], send_sem=send_sems.at[r - 1], recv_sem=recv_sems.at[r - 1],
        device_id=_peer(r), device_id_type=MESH) for r in range(1, N_DEV)]


def _chipsum(gs, sibs, cidx, *, name):
    n = len(gs)

    def body(c_ref, *refs):
        for k in range(n):
            refs[2 * n + k][...] = (refs[k][...].astype(F32) + refs[n + k][...].astype(F32)).astype(refs[2 * n + k].dtype)

    mine = [pl.BlockSpec((1,) + g.shape[1:], lambda q, c_ref: (2 * q + c_ref[0], 0, 0)) for g in gs]
    other = [pl.BlockSpec((1,) + g.shape[1:], lambda q, c_ref: (q, 0, 0)) for g in gs]
    return pl.pallas_call(
        body, name=name,
        grid_spec=pltpu.PrefetchScalarGridSpec(num_scalar_prefetch=1, grid=(4,), in_specs=mine + other, out_specs=other),
        out_shape=[jax.ShapeDtypeStruct((4,) + g.shape[1:], g.dtype) for g in gs],
        compiler_params=_params(("arbitrary",)),
    )(cidx, *gs, *sibs)


HBM_SPEC = pl.BlockSpec(memory_space=pltpu.HBM)
SEM_SPEC = pl.BlockSpec(memory_space=pltpu.SEMAPHORE)
DATAFLOW = pltpu.SideEffectType.DATAFLOW_SIDE_EFFECTING


def _in_hbm(a):
    return pltpu.with_memory_space_constraint(a, pltpu.HBM)


def _ici_copies(sums, lands, send_sems, recv_sems, k0=0):
    x, y, c = _coords()
    chips = [(1 - x, y), (x, 1 - y), (1 - x, 1 - y)]
    cps = []
    for k in range(len(sums)):
        for j, (cx, cy) in enumerate(chips):
            cps.append(pltpu.make_async_remote_copy(
                src_ref=sums[k].at[2 * cx + cy], dst_ref=lands[k].at[j],
                send_sem=send_sems.at[3 * (k0 + k) + j], recv_sem=recv_sems.at[3 * (k0 + k) + j],
                device_id=(cx, cy, c), device_id_type=MESH))
    return cps


def _split_start(copies, srcs, lands, n_sems, after, *, name):
    ns, nl = len(srcs), len(lands)

    def body(*refs):
        for cp in copies(refs[:ns], refs[ns:ns + nl], refs[ns + nl + len(after)], refs[ns + nl + len(after) + 1]):
            cp.start()
        refs[-1][...] = jnp.zeros_like(refs[-1])

    bufs = [_in_hbm(a) for a in list(srcs) + list(lands)]
    out = pl.pallas_call(
        body, name=name,
        out_shape=(pltpu.SemaphoreType.DMA((n_sems,)), pltpu.SemaphoreType.DMA((n_sems,)),
                   *[pltpu.HBM(a.shape, a.dtype) for a in bufs], jax.ShapeDtypeStruct((8, 128), F32)),
        in_specs=[HBM_SPEC] * len(bufs) + [pl.BlockSpec(memory_space=pl.ANY)] * len(after),
        out_specs=(SEM_SPEC, SEM_SPEC, *[HBM_SPEC] * len(bufs), pl.BlockSpec(memory_space=pltpu.VMEM)),
        input_output_aliases={i: 2 + i for i in range(len(bufs))},
        compiler_params=pltpu.CompilerParams(has_side_effects=DATAFLOW),
    )(*bufs, *after)
    return out[0], out[1], list(out[2:2 + ns]), list(out[2 + ns:2 + ns + nl]), out[-1]


def _split_wait(copies, send_sems, recv_sems, srcs, lands, after, *, name):
    ns, nl = len(srcs), len(lands)

    def body(*refs):
        for cp in copies(refs[:ns], refs[ns:ns + nl], refs[ns + nl], refs[ns + nl + 1]):
            cp.wait_send()
            cp.wait_recv()

    out = pl.pallas_call(
        body, name=name,
        out_shape=[pltpu.HBM(a.shape, a.dtype) for a in list(srcs) + list(lands)],
        in_specs=[HBM_SPEC] * (ns + nl) + [SEM_SPEC, SEM_SPEC] + [pl.BlockSpec(memory_space=pl.ANY)] * len(after),
        out_specs=[HBM_SPEC] * (ns + nl),
        input_output_aliases={i: i for i in range(ns + nl)},
        compiler_params=pltpu.CompilerParams(has_side_effects=DATAFLOW),
    )(*srcs, *lands, send_sems, recv_sems, *after)
    return list(out[:ns]), list(out[ns:])


ADAM_C1 = 1.0 / (1.0 - ADAM_B1 ** ADAM_STEP)
ADAM_C2 = 1.0 / (1.0 - ADAM_B2 ** ADAM_STEP)


def _adam_math(w, g, m, v):
    m2 = ADAM_B1 * m + (1.0 - ADAM_B1) * g
    v2 = ADAM_B2 * v + (1.0 - ADAM_B2) * (g * g)
    return -ADAM_LR * ((m2 * ADAM_C1) / (jnp.sqrt(v2 * ADAM_C2) + ADAM_EPS) + ADAM_WD * w), m2, v2


def _adamw(w, g, m, v, *, name):
    R, C = w.shape
    tr = R if R <= 512 else 256

    def body(w_ref, g_ref, m_ref, v_ref, d_ref, nm_ref, nv_ref):
        d_ref[...], nm_ref[...], nv_ref[...] = _adam_math(w_ref[...], g_ref[...], m_ref[...], v_ref[...])

    blk = pl.BlockSpec((tr, C), lambda i: (i, 0))
    return pl.pallas_call(
        body, name=name, grid=(R // tr,), in_specs=[blk] * 4, out_specs=[blk] * 3,
        out_shape=[jax.ShapeDtypeStruct((R, C), F32)] * 3,
        compiler_params=_params(("parallel",)),
    )(w, g, m, v)


def _adamw_rs(wmv, cs, rcv, qidx, *, name):
    n = len(wmv)
    r, cc = wmv[0][0].shape
    n_rcv = rcv[0].shape[0]
    tr = r // 2 if r % 32 == 0 and r > 128 else r

    def body(q_ref, *refs):
        ins, outs = refs[:5 * n], refs[5 * n:]
        for k in range(n):
            w_ref, m_ref, v_ref, c_ref, r_ref = ins[5 * k:5 * k + 5]
            g_ref, d_ref, nm_ref, nv_ref = outs[4 * k:4 * k + 4]
            g = c_ref[0].astype(F32)
            for j in range(n_rcv):
                g = g + r_ref[j].astype(F32)
            g_ref[...] = g
            d_ref[...], nm_ref[...], nv_ref[...] = _adam_math(w_ref[...], g, m_ref[...], v_ref[...])

    blk = pl.BlockSpec((tr, cc), lambda i, q_ref: (i, 0))
    one = [blk, blk, blk, pl.BlockSpec((1, tr, cc), lambda i, q_ref: (q_ref[0], i, 0)),
           pl.BlockSpec((n_rcv, tr, cc), lambda i, q_ref: (0, i, 0))]
    out = pl.pallas_call(
        body, name=name,
        grid_spec=pltpu.PrefetchScalarGridSpec(num_scalar_prefetch=1, grid=(r // tr,), in_specs=one * n,
                                               out_specs=[blk] * (4 * n)),
        out_shape=[jax.ShapeDtypeStruct((r, cc), F32)] * (4 * n),
        compiler_params=_params(("arbitrary",)),
    )(qidx, *[a for (w, m, v), c, rc in zip(wmv, cs, rcv) for a in (w, m, v, c, rc)])
    return [tuple(out[4 * k:4 * k + 4]) for k in range(n)]


SMALL_PARAMS = ("norm_ffn1", "norm_mix", "norm_ffn2", "norm_final", "q_norm", "kv_norm", "sinks", "rel_bias", "b_mod")


def _adamw_small(gvec, wmv):
    widths = [wmv[3 * i].shape[1] for i in range(len(SMALL_PARAMS))]

    def body(*refs):
        g_all = refs[0]
        ins = refs[1:1 + 3 * len(SMALL_PARAMS)]
        outs = refs[1 + 3 * len(SMALL_PARAMS):]
        off = N_MODVEC
        for i, name in enumerate(SMALL_PARAMS):
            g_ref, d_ref, nm_ref, nv_ref = outs[4 * i:4 * i + 4]
            w_ref, m_ref, v_ref = ins[3 * i:3 * i + 3]
            start = 0 if name == "b_mod" else off
            g = g_all[:, start:start + widths[i]]
            g_ref[...] = g
            d_ref[...], nm_ref[...], nv_ref[...] = _adam_math(w_ref[...], g, m_ref[...], v_ref[...])
            if name != "b_mod":
                off += dict(SMALL_LAYOUT)[name]

    vm = pl.BlockSpec(memory_space=pltpu.VMEM)
    n_out = 4 * len(SMALL_PARAMS)
    out = pl.pallas_call(
        body, name="adamw_small", in_specs=[vm] * (1 + len(wmv)), out_specs=[vm] * n_out,
        out_shape=[jax.ShapeDtypeStruct((1, widths[i // 4]), F32) for i in range(n_out)],
        compiler_params=_params(),
    )(gvec, *wmv)
    return {name: out[4 * i:4 * i + 4] for i, name in enumerate(SMALL_PARAMS)}


TRANSPOSED = ("g1T", "u1T", "g3T", "u3T", "w_inT", "w_uqT")


def kernel(x, c, w_mod, b_mod, norm_ffn1, ffn1_gate, ffn1_up, ffn1_down, norm_mix, w_in, q_norm, kv_norm, w_uq, w_ukv, sinks, w_o, norm_ffn2, ffn2_gate, ffn2_up, ffn2_down, rel_bias, norm_final, loss_target, m_w_mod, m_b_mod, m_norm_ffn1, m_ffn1_gate, m_ffn1_up, m_ffn1_down, m_norm_mix, m_w_in, m_q_norm, m_kv_norm, m_w_uq, m_w_ukv, m_sinks, m_w_o, m_norm_ffn2, m_ffn2_gate, m_ffn2_up, m_ffn2_down, m_rel_bias, m_norm_final, v_w_mod, v_b_mod, v_norm_ffn1, v_ffn1_gate, v_ffn1_up, v_ffn1_down, v_norm_mix, v_w_in, v_q_norm, v_kv_norm, v_w_uq, v_w_ukv, v_sinks, v_w_o, v_norm_ffn2, v_ffn2_gate, v_ffn2_up, v_ffn2_down, v_rel_bias, v_norm_final):
    mx, my, mc = _coords()
    cidx = jnp.reshape(mc, (1,)).astype(jnp.int32)
    qidx = jnp.reshape(2 * mx + my, (1,)).astype(jnp.int32)
    WM = w_mod.shape[2]

    c_tile = jnp.pad(c, ((0, 7), (0, 0)))
    b_mod3 = jnp.pad(b_mod.reshape(N_DEV, 1, WM), ((0, 0), (0, 7), (0, 0)))
    mod3, ca = _mod_fwd(c_tile, w_mod[0], b_mod3)
    mod9 = mod3[:, 0, :].reshape(N_MOD, D)

    shards = {"g1T": ffn1_gate[0].T.astype(BF16), "u1T": ffn1_up[0].T.astype(BF16), "d1": ffn1_down[0].astype(BF16),
              "g3T": ffn2_gate[0].T.astype(BF16), "u3T": ffn2_up[0].T.astype(BF16), "d3": ffn2_down[0].astype(BF16),
              "w_inT": w_in[0].T, "w_uqT": w_uq[0].T.astype(BF16), "w_ukv": w_ukv[0].astype(BF16),
              "w_o": w_o[0].astype(BF16)}
    me = 4 * mx + 2 * my + mc
    groups = {"ffn1": ("g1T", "u1T", "d1"), "mixer": ("w_inT", "w_uqT", "w_ukv", "w_o"), "ffn2": ("g3T", "u3T", "d3")}
    arriving = {}

    def as_weights(group, gathered):
        return {k: g if k == "w_ukv" else g.reshape(N_DEV * g.shape[1], g.shape[2])
                for k, g in zip(groups[group], gathered)}

    later = groups["mixer"] + groups["ffn2"]
    place = {"mixer": 0, "ffn2": len(groups["mixer"])}

    def start_gather(token):
        lands = []
        for k in later:
            sh = shards[k] + token[0, 0].astype(shards[k].dtype)
            lands.append(lax.dynamic_update_slice(lax.empty((N_DEV,) + sh.shape, sh.dtype), sh[None], (me, 0, 0)))
        batches = [range(k0, k0 + len(groups[group])) for group, k0 in place.items()]
        send, recv, lands, started = _gather_start(lands, name="gather_start", batches=batches)
        for group, k0 in place.items():
            arriving[group] = (send, recv, lands[k0:k0 + len(groups[group])])
        return started

    def fetch(group, after, vecs):
        if group == "ffn1":
            *gathered, token = _wgather([shards[k] + ca[1, 0].astype(shards[k].dtype) for k in groups["ffn1"]])
            return as_weights("ffn1", gathered), vecs + start_gather(token)[0:1, 0:1]

        def pass_on(group, after):
            send, recv, lands = arriving[group]
            lands, token = _gather_pass(send, recv, lands, after, name="gather_landed_" + group, stage="landed",
                                        k0=place[group])
            lands, token = _gather_pass(send, recv, lands, [token], name="gather_onward_" + group, stage="onward",
                                        k0=place[group])
            arriving[group] = (send, recv, lands)
            return token

        if group == "ffn2_on_its_way":
            return None, vecs + pass_on("ffn2", after)[0:1, 0:1]
        if group == "mixer":
            after = [pass_on("mixer", after)]
        send, recv, lands = arriving[group]
        return as_weights(group, _gather_end(send, recv, lands, after, name="gather_end_" + group,
                                             k0=place[group])), vecs

    norms ={"ffn1": norm_ffn1, "mix": norm_mix, "ffn2": norm_ffn2, "final": norm_final.reshape(1, D)}
    in_flight = {}

    def on_grads(group, g, after, vecs, before_ici=()):
        if group != "ffn1":
            if g is None:
                return vecs
            names = list(g)
            by_dest = [g[k] if k == "w_ukv" else g[k].reshape((N_DEV, g[k].shape[0] // N_DEV) + g[k].shape[1:])
                       for k in names]
            lands = [lax.empty((N_DEV - 1,) + a.shape[1:], a.dtype) for a in by_dest]
            send, recv, by_dest, lands, token = _split_start(_direct_copies, by_dest, lands, 7 * len(names), after,
                                                             name="rs_start_" + group)
            in_flight[group] = (names, send, recv, by_dest, lands, token)
            return vecs + token[0:1, 0:1]
        if g is not None:
            names = list(g)
            by_dest = [g[k] if k == "w_ukv" else g[k].reshape((N_DEV, g[k].shape[0] // N_DEV) + g[k].shape[1:])
                       for k in names]
            lands = [lax.empty((4,) + a.shape[1:], a.dtype) for a in by_dest]
            send, recv, by_dest, lands, token = _split_start(_d2d_copies, by_dest, lands, 4 * len(names), after,
                                                             name="rs_d2d_start_" + group)
            in_flight[group] = (names, send, recv, by_dest, lands)
            return vecs + token[0:1, 0:1]
        names, send, recv, by_dest, lands = in_flight[group]
        by_dest, from_sib = _split_wait(_d2d_copies, send, recv, by_dest, lands, after, name="rs_d2d_wait_" + group)
        sums = _chipsum(by_dest, from_sib, cidx, name="chipsum_" + group)
        lands = [lax.empty((3,) + s.shape[1:], s.dtype) for s in sums]
        send, recv, sums, lands, token = _split_start(_ici_copies, sums, lands, 3 * len(names), list(before_ici),
                                                      name="rs_ici_start_" + group)
        in_flight[group] = (names, send, recv, sums, lands, token)
        return vecs + token[0:1, 0:1]

    _, grad_x, _, vec = _local_step(
        x[0], loss_target[0], mod9, norms, sinks, rel_bias, q_norm, kv_norm, fetch, on_grads=on_grads)

    vec = vec.reshape(1, 1, N_VEC)
    allvec = lax.dynamic_update_slice(lax.empty((N_DEV, 1, N_VEC), F32), vec, (me, 0, 0))
    vsend, vrecv, _, (allvec,), vec_started = _split_start(_vec_copies, [], [allvec], N_DEV - 1, [], name="vec_start")
    on_grads("ffn1", None, [grad_x], jnp.zeros((1, 1), F32), before_ici=[vec_started])

    owners = {"g1T": ("ffn1_gate", ffn1_gate, m_ffn1_gate, v_ffn1_gate), "u1T": ("ffn1_up", ffn1_up, m_ffn1_up, v_ffn1_up),
              "d1": ("ffn1_down", ffn1_down, m_ffn1_down, v_ffn1_down),
              "g3T": ("ffn2_gate", ffn2_gate, m_ffn2_gate, v_ffn2_gate), "u3T": ("ffn2_up", ffn2_up, m_ffn2_up, v_ffn2_up),
              "d3": ("ffn2_down", ffn2_down, m_ffn2_down, v_ffn2_down),
              "w_inT": ("w_in", w_in, m_w_in, v_w_in), "w_uqT": ("w_uq", w_uq, m_w_uq, v_w_uq),
              "w_ukv": ("w_ukv", w_ukv, m_w_ukv, v_w_ukv), "w_o": ("w_o", w_o, m_w_o, v_w_o)}
    res, done = {}, []

    def finish(group, after, one_by_one=False):
        names, send, recv, sums, lands, _ = in_flight[group]
        there = lambda k, a: a[0].T if k in TRANSPOSED else a[0]
        back = lambda k, a: a.T[None] if k in TRANSPOSED else a[None]
        wmv = [tuple(there(k, a) for a in owners[k][1:]) for k in names]
        if one_by_one:
            outs = []
            for i, k in enumerate(names):
                (cs,), (rc,) = _split_wait(functools.partial(_ici_copies, k0=i), send, recv, [sums[i]], [lands[i]],
                                           after, name="rs_ici_wait_" + k)
                outs.append(_adamw_rs([wmv[i]], [cs], [rc], qidx, name="adamw_" + owners[k][0])[0])
                after = [outs[-1][3]]
        else:
            own = jnp.reshape(me, (1,)).astype(jnp.int32)
            sums, lands = _split_wait(_direct_copies, send, recv, sums, lands, after, name="rs_wait_" + group)
            if len({w.shape for w, _, _ in wmv}) == 1:
                outs = _adamw_rs(wmv, sums, lands, own, name="adamw_" + group)
            else:
                outs = [_adamw_rs([t], [cs], [rc], own, name="adamw_" + owners[k][0])[0]
                        for k, t, cs, rc in zip(names, wmv, sums, lands)]
        for k, out in zip(names, outs):
            done.append(out[3])
            res[owners[k][0]] = tuple(back(k, a) for a in out)

    ffn1_started = in_flight["ffn1"][5]
    finish("ffn2", [ffn1_started])
    finish("mixer", [ffn1_started])

    _, (allvec,) = _split_wait(_vec_copies, vsend, vrecv, [], [allvec], [ffn1_started], name="vec_wait")
    g_wmod, gvec = _mod_bwd(allvec, ca, jnp.reshape(me, (1,)).astype(jnp.int32))
    loss = gvec[0, N_MODVEC + LOSS_SLOT]
    res["w_mod"] = tuple(a[None] for a in (g_wmod,) + tuple(_adamw(w_mod[0], g_wmod, m_w_mod[0], v_w_mod[0],
                                                                    name="adamw_w_mod")))
    small_in = {"norm_ffn1": (norm_ffn1, m_norm_ffn1, v_norm_ffn1), "norm_mix": (norm_mix, m_norm_mix, v_norm_mix),
                "norm_ffn2": (norm_ffn2, m_norm_ffn2, v_norm_ffn2), "norm_final": (norm_final, m_norm_final, v_norm_final),
                "q_norm": (q_norm, m_q_norm, v_q_norm), "kv_norm": (kv_norm, m_kv_norm, v_kv_norm),
                "sinks": (sinks, m_sinks, v_sinks), "rel_bias": (rel_bias, m_rel_bias, v_rel_bias),
                "b_mod": (b_mod, m_b_mod, v_b_mod)}
    small_out = _adamw_small(gvec, [a.reshape(1, -1) for k in SMALL_PARAMS for a in small_in[k]])
    for k in SMALL_PARAMS:
        res[k] = tuple(a.reshape(small_in[k][0].shape) for a in small_out[k])

    finish("ffn1", done + [res["w_mod"][3]] + [a for k in SMALL_PARAMS for a in res[k]], one_by_one=True)

    order = ("w_mod", "b_mod", "norm_ffn1", "ffn1_gate", "ffn1_up", "ffn1_down", "norm_mix", "w_in", "q_norm",
             "kv_norm", "w_uq", "w_ukv", "sinks", "w_o", "norm_ffn2", "ffn2_gate", "ffn2_up", "ffn2_down",
             "rel_bias", "norm_final")
    return (loss, grad_x[None]) + tuple(res[nm][kind] for kind in range(4) for nm in order)
```

```python
import functools
import math

import numpy as np
import jax
import jax.numpy as jnp
from jax import lax
from jax.experimental import pallas as pl
from jax.experimental.pallas import tpu as pltpu

F32 = jnp.float32
BF16 = jnp.bfloat16
MESH = pl.DeviceIdType.MESH

N_DEV = 8
D = 1024
D_FF = 2816
EPS = 1e-6
N_MOD = 9
SWA_HEADS = 8
SWA_DH = 64
WINDOW = 128
MLA_HEADS = 4
MLA_NOPE = 128
MLA_ROPE = 64
MLA_V = 128
MLA_QR = 256
MLA_KVR = 128
ROPE_THETA = 10000.0
NUM_BUCKETS = 32
D_IN = 1216
D_IN_PAD = 1280
SWA_SCALE = SWA_DH ** -0.5
MLA_SCALE = (MLA_NOPE + MLA_ROPE) ** -0.5

ADAM_LR = 0.001
ADAM_B1 = 0.9
ADAM_B2 = 0.999
ADAM_EPS = 1e-08
ADAM_WD = 0.01
ADAM_STEP = 10

V7X_VMEM_LIMIT = 56 * 1024 * 1024
ROW_TILE = 512

NT_DIMS = (((1,), (1,)), ((), ()))
TN_DIMS = (((0,), (0,)), ((), ()))


def _dot(a, b):
    return jnp.dot(a, b, preferred_element_type=F32)


def _dot_nt(a, b):
    return lax.dot_general(a, b, NT_DIMS, preferred_element_type=F32)


def _dot_tn(a, b):
    return lax.dot_general(a, b, TN_DIMS, preferred_element_type=F32)


def _params(sem=None):
    return pltpu.CompilerParams(dimension_semantics=sem, vmem_limit_bytes=V7X_VMEM_LIMIT)


def _rstd(x):
    return lax.rsqrt(jnp.mean(x * x, axis=-1, keepdims=True) + EPS)


def _rms_bwd(dy, xhat, r):
    return r * (dy - xhat * jnp.mean(dy * xhat, axis=-1, keepdims=True))


def _sigmoid(a):
    return 1.0 / (1.0 + jnp.exp(-a))


def _ffn_fwd(x, vecs, wgT, wuT, wd, *, name, tm=256, tf=D_FF):
    S, F = x.shape[0], wd.shape[0]
    tm = min(tm, S)
    ni, nj = S // tm, F // tf

    def body(x_ref, vec_ref, wg_ref, wu_ref, wd_ref, xo_ref, h_ref, a_ref, b_ref, f_ref, acc_ref):
        j = pl.program_id(1)

        @pl.when(j == 0)
        def _():
            xv = x_ref[...]
            hn = xv * _rstd(xv) * vec_ref[0:1, :]
            h_ref[...] = (hn * (1.0 + vec_ref[2:3, :]) + vec_ref[1:2, :]).astype(BF16)

        h = h_ref[...]
        a = _dot_nt(h, wg_ref[...])
        b = _dot_nt(h, wu_ref[...])
        a_ref[...] = a.astype(BF16)
        b_ref[...] = b.astype(BF16)
        part = _dot((a * _sigmoid(a) * b).astype(BF16), wd_ref[...])

        def finish(f):
            f_ref[...] = f
            xo_ref[...] = x_ref[...] + (0.5 * vec_ref[3:4, :]) * f

        if nj == 1:
            finish(part)
        else:
            @pl.when(j == 0)
            def _():
                acc_ref[...] = part

            @pl.when((j > 0) & (j < nj - 1))
            def _():
                acc_ref[...] += part

            @pl.when(j == nj - 1)
            def _():
                finish(acc_ref[...] + part)

    row = pl.BlockSpec((tm, D), lambda i, j: (i, 0))
    wspec = pl.BlockSpec((tf, D), lambda i, j: (j, 0), pipeline_mode=pl.Buffered(1) if nj == 1 else None)
    act = pl.BlockSpec((tm, tf), lambda i, j: (i, j))
    return pl.pallas_call(
        body, name=name, grid=(ni, nj),
        in_specs=[row, pl.BlockSpec((8, D), lambda i, j: (0, 0)), wspec, wspec, wspec],
        out_specs=[row, row, act, act, row],
        out_shape=[jax.ShapeDtypeStruct((S, D), F32), jax.ShapeDtypeStruct((S, D), BF16),
                   jax.ShapeDtypeStruct((S, F), BF16), jax.ShapeDtypeStruct((S, F), BF16),
                   jax.ShapeDtypeStruct((S, D), F32)],
        scratch_shapes=[pltpu.VMEM((tm, D) if nj > 1 else (8, 128), F32)],
        compiler_params=_params(("parallel", "arbitrary")),
    )(x, vecs, wgT, wuT, wd)


def _ffn_bwd_main(h, df, a, b, wgT, wuT, wd, *, name, after=(), tm=2048, tf=256):
    S = h.shape[0]
    tm = min(tm, S)
    ni, nj = S // tm, D_FF // tf

    def body(h_hbm, df_hbm, a_ref, b_ref, wg_ref, wu_ref, wd_ref, *rest):
        gg_ref, gu_ref, gd_ref, dh_hbm, h_v, df_v, dh_v, gg_acc, gu_acc, gd_acc, sem = rest[len(after):]
        j = pl.program_id(0)
        i = pl.program_id(1)

        @pl.when((j == 0) & (i == 0))
        def _():
            c1 = pltpu.make_async_copy(h_hbm, h_v, sem.at[0])
            c2 = pltpu.make_async_copy(df_hbm, df_v, sem.at[1])
            c1.start()
            c2.start()
            c1.wait()
            c2.wait()

        @pl.when(i == 0)
        def _():
            gg_acc[...] = jnp.zeros_like(gg_acc)
            gu_acc[...] = jnp.zeros_like(gu_acc)
            gd_acc[...] = jnp.zeros_like(gd_acc)

        rows = pl.ds(pl.multiple_of(i * tm, tm), tm)
        hi = h_v[rows, :]
        dfi = df_v[rows, :]
        av = a_ref[...].astype(F32)
        bv = b_ref[...].astype(F32)
        sg = _sigmoid(av)
        sa = av * sg
        hsw = (sa * bv).astype(BF16)
        dhsw = _dot_nt(dfi, wd_ref[...])
        da = (dhsw * bv * (sg * (1.0 + av * (1.0 - sg)))).astype(BF16)
        db = (dhsw * sa).astype(BF16)
        gd_acc[...] += _dot_tn(hsw, dfi)
        gg_acc[...] += _dot_tn(da, hi)
        gu_acc[...] += _dot_tn(db, hi)
        dh = _dot(da, wg_ref[...]) + _dot(db, wu_ref[...])

        @pl.when(j == 0)
        def _():
            dh_v[rows, :] = dh

        @pl.when(j > 0)
        def _():
            dh_v[rows, :] += dh

        @pl.when(i == ni - 1)
        def _():
            gg_ref[...] = gg_acc[...].astype(BF16)
            gu_ref[...] = gu_acc[...].astype(BF16)
            gd_ref[...] = gd_acc[...].astype(BF16)

        @pl.when((j == nj - 1) & (i == ni - 1))
        def _():
            c3 = pltpu.make_async_copy(dh_v, dh_hbm, sem.at[2])
            c3.start()
            c3.wait()

    anyspec = pl.BlockSpec(memory_space=pl.ANY)
    wspec = pl.BlockSpec((tf, D), lambda j, i: (j, 0))
    act = pl.BlockSpec((tm, tf), lambda j, i: (i, j))
    return pl.pallas_call(
        body, name=name, grid=(nj, ni),
        in_specs=[anyspec, anyspec, act, act, wspec, wspec, wspec] + [anyspec] * len(after),
        out_specs=[wspec, wspec, wspec, anyspec],
        out_shape=[jax.ShapeDtypeStruct((D_FF, D), BF16)] * 3 + [jax.ShapeDtypeStruct((S, D), F32)],
        scratch_shapes=[pltpu.VMEM((S, D), BF16), pltpu.VMEM((S, D), BF16), pltpu.VMEM((S, D), F32),
                        pltpu.VMEM((tf, D), F32), pltpu.VMEM((tf, D), F32), pltpu.VMEM((tf, D), F32),
                        pltpu.SemaphoreType.DMA((3,))],
        compiler_params=_params(("arbitrary", "arbitrary")),
    )(h, df, a, b, wgT, wuT, wd, *after)


def _ffn_out_bwd(dx, f, gate, df_ref, part_ref):
    df_ref[...] = ((0.5 * gate) * dx).astype(BF16)
    part_ref[3:4, :] += 0.5 * jnp.sum(dx * f, axis=0, keepdims=True)


def _norm_bwd(dh, x, dxo, vecs, *, name, below=None, tm=ROW_TILE):
    S = x.shape[0]
    tm = min(tm, S)

    def body(dh_ref, x_ref, dxo_ref, vec_ref, *rest):
        dx_ref, part_ref = rest[-2 if below is None else -3], rest[-1 if below is None else -2]

        @pl.when(pl.program_id(0) == 0)
        def _():
            part_ref[...] = jnp.zeros_like(part_ref)

        dh = dh_ref[...]
        xv = x_ref[...]
        r = _rstd(xv)
        xhat = xv * r
        w = vec_ref[0:1, :]
        xn = xhat * w
        dxn = dh * (1.0 + vec_ref[2:3, :])
        part_ref[0:1, :] += jnp.sum(dxn * xhat, axis=0, keepdims=True)
        part_ref[1:2, :] += jnp.sum(dh, axis=0, keepdims=True)
        part_ref[2:3, :] += jnp.sum(dh * xn, axis=0, keepdims=True)
        dx = dxo_ref[...] + _rms_bwd(dxn * w, xhat, r)
        dx_ref[...] = dx
        if below is not None:
            _ffn_out_bwd(dx, rest[0][...], rest[1][3:4, :], rest[-1], part_ref)

    row = pl.BlockSpec((tm, D), lambda i: (i, 0))
    vec = pl.BlockSpec((8, D), lambda i: (0, 0))
    extra = [] if below is None else [row, vec]
    return pl.pallas_call(
        body, name=name, grid=(S // tm,), in_specs=[row, row, row, vec] + extra,
        out_specs=[row, vec] + ([] if below is None else [row]),
        out_shape=[jax.ShapeDtypeStruct((S, D), F32), jax.ShapeDtypeStruct((8, D), F32)]
        + ([] if below is None else [jax.ShapeDtypeStruct((S, D), BF16)]),
        compiler_params=_params(("arbitrary",)),
    )(dh, x, dxo, vecs, *([] if below is None else below))


def _head(x, tgt, nf, f, vecs, *, tm=ROW_TILE):
    S = x.shape[0]
    tm = min(tm, S)

    def body(x_ref, t_ref, nf_ref, f_ref, vec_ref, dx_ref, part_ref, df_ref):
        @pl.when(pl.program_id(0) == 0)
        def _():
            part_ref[...] = jnp.zeros_like(part_ref)

        xv = x_ref[...]
        r = _rstd(xv)
        xhat = xv * r
        w = nf_ref[...]
        e = xhat * w - t_ref[...]
        dy = e * (1.0 / D)
        part_ref[0:1, :] += jnp.sum(dy * xhat, axis=0, keepdims=True)
        part_ref[1:2, :] += jnp.sum(e * e) * (0.5 / D)
        dx = _rms_bwd(dy * w, xhat, r)
        dx_ref[...] = dx
        _ffn_out_bwd(dx, f_ref[...], vec_ref[3:4, :], df_ref, part_ref)

    row = pl.BlockSpec((tm, D), lambda i: (i, 0))
    vec = pl.BlockSpec((8, D), lambda i: (0, 0))
    return pl.pallas_call(
        body, name="head", grid=(S // tm,),
        in_specs=[row, row, pl.BlockSpec((1, D), lambda i: (0, 0)), row, vec],
        out_specs=[row, vec, row],
        out_shape=[jax.ShapeDtypeStruct((S, D), F32), jax.ShapeDtypeStruct((8, D), F32),
                   jax.ShapeDtypeStruct((S, D), BF16)],
        compiler_params=_params(("arbitrary",)),
    )(x, tgt, nf, f, vecs)


def _mix_in_fwd(x, vecs, w_inT, *, tm=ROW_TILE):
    S = x.shape[0]
    tm = min(tm, S)

    def body(x_ref, vec_ref, w_ref, h_ref, p_ref):
        xv = x_ref[...]
        hn = xv * _rstd(xv) * vec_ref[0:1, :]
        h = (hn * (1.0 + vec_ref[2:3, :]) + vec_ref[1:2, :]).astype(BF16)
        h_ref[...] = h
        p_ref[...] = _dot_nt(h, w_ref[...])

    row = pl.BlockSpec((tm, D), lambda i: (i, 0))
    return pl.pallas_call(
        body, name="mix_in_fwd", grid=(S // tm,),
        in_specs=[row, pl.BlockSpec((8, D), lambda i: (0, 0)), pl.BlockSpec((D_IN_PAD, D), lambda i: (0, 0))],
        out_specs=[row, pl.BlockSpec((tm, D_IN_PAD), lambda i: (i, 0))],
        out_shape=[jax.ShapeDtypeStruct((S, D), BF16), jax.ShapeDtypeStruct((S, D_IN_PAD), F32)],
        compiler_params=_params(("parallel",)),
    )(x, vecs, w_inT)


def _bucket_table():
    qi = np.arange(WINDOW)[:, None]
    kj = np.arange(2 * WINDOW)[None, :]
    dist = qi + WINDOW - kj
    max_exact = NUM_BUCKETS // 2
    n = np.maximum(dist, 0)
    nf = np.maximum(n, 1).astype(np.float32)
    large = max_exact + (np.log(nf / np.float32(max_exact)) / np.float32(math.log(WINDOW / max_exact))
                         * np.float32(NUM_BUCKETS - max_exact)).astype(np.int32)
    large = np.minimum(large, NUM_BUCKETS - 1)
    return np.where(n < max_exact, n, large).astype(np.int32)


def _bias_build(rel_bias, bucket):
    def body(rb_ref, bk_ref, out_ref):
        bk = bk_ref[...]
        for h in range(SWA_HEADS):
            acc = jnp.zeros((WINDOW, 2 * WINDOW), F32)
            for b in range(NUM_BUCKETS):
                acc = jnp.where(bk == b, rb_ref[b, h], acc)
            out_ref[h] = acc

    return pl.pallas_call(
        body, name="bias_build",
        in_specs=[pl.BlockSpec(memory_space=pltpu.SMEM), pl.BlockSpec(memory_space=pltpu.VMEM)],
        out_specs=pl.BlockSpec(memory_space=pltpu.VMEM),
        out_shape=jax.ShapeDtypeStruct((SWA_HEADS, WINDOW, 2 * WINDOW), F32),
    )(rel_bias, bucket)


SWA_GROUP = 4
GROUP_ROWS = SWA_GROUP * WINDOW


SWA_SUB = 2


def _swa_valid(has_prev):
    row = lax.broadcasted_iota(jnp.int32, (GROUP_ROWS, 2 * WINDOW), 0) % WINDOW
    col = lax.broadcasted_iota(jnp.int32, (GROUP_ROWS, 2 * WINDOW), 1)
    dist = row + WINDOW - col
    return (dist >= 0) & (dist < WINDOW) & ((col >= WINDOW) | has_prev)


def _swa_keys(prev_ref, cur_ref, u):
    cur = cur_ref[...]
    before = prev_ref[...] if u == 0 else cur[WINDOW * (u - 1):WINDOW * u]
    return jnp.concatenate([before, cur[WINDOW * u:WINDOW * (u + 1)]], axis=0).astype(BF16)


def _stack_heads(x, g):
    return jnp.concatenate([x[:, 64 * h:64 * h + 64] for h in range(SWA_GROUP * g, SWA_GROUP * (g + 1))], axis=0)


def _unstack_heads(x4):
    return jnp.concatenate([x4[WINDOW * a:WINDOW * (a + 1)] for a in range(SWA_GROUP)], axis=1)


def _group_sinks(sink_ref, g):
    head = lax.broadcasted_iota(jnp.int32, (GROUP_ROWS, 1), 0) // WINDOW
    out = jnp.full((GROUP_ROWS, 1), sink_ref[0, SWA_GROUP * g], F32)
    for a in range(1, SWA_GROUP):
        out = jnp.where(head == a, sink_ref[0, SWA_GROUP * g + a], out)
    return out


def _swa_probs(qh, kk, bias_h, sink, valid):
    s = _dot_nt(qh, kk) * SWA_SCALE + bias_h
    s = jnp.where(valid, s, -jnp.inf)
    m = jnp.maximum(jnp.max(s, axis=-1, keepdims=True), sink)
    p = jnp.exp(s - m)
    ps = jnp.exp(sink - m)
    inv = 1.0 / (jnp.sum(p, axis=-1, keepdims=True) + ps)
    return p * inv, ps * inv


SWA_ROWS = SWA_SUB * WINDOW


def _swa_specs():
    prev = lambda n: jnp.maximum(SWA_SUB * n - 1, 0)
    return [pl.BlockSpec((SWA_ROWS, 512), lambda n: (n, 0)),
            pl.BlockSpec((SWA_ROWS, 128), lambda n: (n, 4)),
            pl.BlockSpec((WINDOW, 128), lambda n: (prev(n), 4)),
            pl.BlockSpec((SWA_ROWS, 128), lambda n: (n, 5)),
            pl.BlockSpec((WINDOW, 128), lambda n: (prev(n), 5)),
            pl.BlockSpec((SWA_HEADS, WINDOW, 2 * WINDOW), lambda n: (0, 0, 0)),
            pl.BlockSpec(memory_space=pltpu.SMEM)]


def _swa_fwd(proj, bias, sinks):
    S = proj.shape[0]

    def body(q_ref, kc_ref, kp_ref, vc_ref, vp_ref, bias_ref, sink_ref, o_ref):
        n = pl.program_id(0)
        for u in range(SWA_SUB):
            rows = slice(WINDOW * u, WINDOW * (u + 1))
            valid = _swa_valid(n > 0 if u == 0 else True)
            q = q_ref[rows, :].astype(BF16)
            kfull = _swa_keys(kp_ref, kc_ref, u)
            vfull = _swa_keys(vp_ref, vc_ref, u)
            for g in range(SWA_HEADS // SWA_GROUP):
                kk = kfull[:, 64 * g:64 * g + 64]
                vv = vfull[:, 64 * g:64 * g + 64]
                bias4 = bias_ref[SWA_GROUP * g:SWA_GROUP * (g + 1)].reshape(GROUP_ROWS, 2 * WINDOW)
                pk, _ = _swa_probs(_stack_heads(q, g), kk, bias4, _group_sinks(sink_ref, g), valid)
                o_ref[rows, 256 * g:256 * (g + 1)] = _unstack_heads(_dot(pk.astype(BF16), vv))

    return pl.pallas_call(
        body, name="swa_fwd", grid=(S // SWA_ROWS,),
        in_specs=_swa_specs(),
        out_specs=pl.BlockSpec((SWA_ROWS, 512), lambda n: (n, 0)),
        out_shape=jax.ShapeDtypeStruct((S, 512), F32),
        compiler_params=_params(("parallel",)),
    )(proj, proj, proj, proj, proj, bias, sinks)


def _swa_bwd(proj, bias, sinks, o, do, bucket):
    S = proj.shape[0]
    nb = S // SWA_ROWS

    def body(q_ref, kc_ref, kp_ref, vc_ref, vp_ref, bias_ref, sink_ref, o_ref, do_ref, bk_ref,
             dq_ref, dk_ref, dv_ref, drb_ref, dsk_ref, dbias_acc):
        n = pl.program_id(0)

        @pl.when(n == 0)
        def _():
            dk_ref[...] = jnp.zeros_like(dk_ref)
            dv_ref[...] = jnp.zeros_like(dv_ref)
            dsk_ref[...] = jnp.zeros_like(dsk_ref)
            dbias_acc[...] = jnp.zeros_like(dbias_acc)
            drb_ref[...] = jnp.zeros_like(drb_ref)

        for u in range(SWA_SUB):
            rows = slice(WINDOW * u, WINDOW * (u + 1))
            blk = SWA_SUB * n + u
            valid = _swa_valid(n > 0 if u == 0 else True)
            q = q_ref[rows, :].astype(BF16)
            dov = do_ref[rows, :]
            ov = o_ref[rows, :]
            kfull = _swa_keys(kp_ref, kc_ref, u)
            vfull = _swa_keys(vp_ref, vc_ref, u)
            prow = pl.ds(pl.multiple_of(jnp.maximum(blk - 1, 0) * WINDOW, WINDOW), WINDOW)
            crow = pl.ds(pl.multiple_of(blk * WINDOW, WINDOW), WINDOW)
            for g in range(SWA_HEADS // SWA_GROUP):
                heads = slice(SWA_GROUP * g, SWA_GROUP * (g + 1))
                kk = kfull[:, 64 * g:64 * g + 64]
                vv = vfull[:, 64 * g:64 * g + 64]
                q4 = _stack_heads(q, g)
                pk, psink = _swa_probs(q4, kk, bias_ref[heads].reshape(GROUP_ROWS, 2 * WINDOW),
                                       _group_sinks(sink_ref, g), valid)
                pkb = pk.astype(BF16)
                do4 = _stack_heads(dov, g)
                dob = do4.astype(BF16)
                dp = _dot_nt(dob, vv)
                delta = jnp.sum(do4 * _stack_heads(ov, g), axis=-1, keepdims=True)
                ds = pk * (dp - delta)
                dsink = -psink * delta
                for a in range(SWA_GROUP):
                    h = SWA_GROUP * g + a
                    part = jnp.sum(dsink[WINDOW * a:WINDOW * (a + 1)], keepdims=True)
                    dsk_ref[h:h + 1, :] += jnp.broadcast_to(part, (1, 128))
                dbias_acc[heads] += ds.reshape(SWA_GROUP, WINDOW, 2 * WINDOW)
                dsb = (ds * SWA_SCALE).astype(BF16)
                dq_ref[rows, 256 * g:256 * (g + 1)] = _unstack_heads(_dot(dsb, kk))
                dkk = _dot_tn(dsb, q4)
                dvv = _dot_tn(pkb, dob)
                dk_ref[prow, 64 * g:64 * g + 64] += dkk[:WINDOW]
                dk_ref[crow, 64 * g:64 * g + 64] += dkk[WINDOW:]
                dv_ref[prow, 64 * g:64 * g + 64] += dvv[:WINDOW]
                dv_ref[crow, 64 * g:64 * g + 64] += dvv[WINDOW:]

        @pl.when(n == nb - 1)
        def _():
            bk = bk_ref[...]
            for h in range(SWA_HEADS):
                dbh = dbias_acc[h]
                for b in range(NUM_BUCKETS):
                    val = jnp.sum(jnp.where(bk == b, dbh, 0.0), keepdims=True)
                    drb_ref[b * 8 + h:b * 8 + h + 1, :] = jnp.broadcast_to(val, (1, 128))

    full = lambda shape: pl.BlockSpec(shape, lambda n: tuple(0 for _ in shape))
    return pl.pallas_call(
        body, name="swa_bwd", grid=(nb,),
        in_specs=_swa_specs() + [pl.BlockSpec((SWA_ROWS, 512), lambda n: (n, 0)),
                                 pl.BlockSpec((SWA_ROWS, 512), lambda n: (n, 0)), full((WINDOW, 2 * WINDOW))],
        out_specs=[pl.BlockSpec((SWA_ROWS, 512), lambda n: (n, 0)), full((S, 128)), full((S, 128)),
                   full((NUM_BUCKETS * 8, 128)), full((8, 128))],
        out_shape=[jax.ShapeDtypeStruct((S, 512), F32), jax.ShapeDtypeStruct((S, 128), F32),
                   jax.ShapeDtypeStruct((S, 128), F32), jax.ShapeDtypeStruct((NUM_BUCKETS * 8, 128), F32),
                   jax.ShapeDtypeStruct((8, 128), F32)],
        scratch_shapes=[pltpu.VMEM((SWA_HEADS, WINDOW, 2 * WINDOW), F32)],
        compiler_params=_params(("arbitrary",)),
    )(proj, proj, proj, proj, proj, bias, sinks, o, do, bucket)


def _rope_tables(S):
    inv = np.float32(ROPE_THETA) ** (-np.arange(0, MLA_ROPE, 2, dtype=np.float32) / np.float32(MLA_ROPE))
    ang = np.arange(S, dtype=np.float32)[:, None] * inv[None, :]
    cos, sin = np.cos(ang), np.sin(ang)
    return (jnp.asarray(np.tile(np.concatenate([cos, cos], axis=1), (1, 2))),
            jnp.asarray(np.tile(np.concatenate([-sin, sin], axis=1), (1, 2))))


def _rope_wide(ref):
    t = ref[...]
    return jnp.concatenate([t, t], axis=1)


def _swap_halves(x):
    w = x.shape[-1]
    lane = lax.broadcasted_iota(jnp.int32, x.shape, x.ndim - 1)
    return jnp.where((lane % 64) < 32, pltpu.roll(x, w - 32, x.ndim - 1), pltpu.roll(x, 32, x.ndim - 1))


def _mla_pre_fwd(proj, qn_w, kvn_w, wuqT, wukv, cos, sin, *, tm=ROW_TILE):
    S = proj.shape[0]
    tm = min(tm, S)

    def body(ql_ref, kl_ref, kr_ref, qw_ref, kw_ref, wuq_ref, wukv_ref, cos_ref, sin_ref,
             qc_ref, kc_ref, vv_ref):
        ql = ql_ref[...]
        qn = (ql * _rstd(ql) * qw_ref[...]).astype(BF16)
        q = _dot_nt(qn, wuq_ref[...])
        cs, sn = _rope_wide(cos_ref), _rope_wide(sin_ref)
        qr = q[:, 512:768]
        qr = qr * cs + _swap_halves(qr) * sn
        half = lax.broadcasted_iota(jnp.int32, (tm, 128), 1) // 64
        kl = kl_ref[...]
        kvn = (kl * _rstd(kl) * kw_ref[...]).astype(BF16)
        kr = kr_ref[...]
        kr = kr * cs[:, :128] + _swap_halves(kr) * sn[:, :128]
        kr2 = (kr + pltpu.roll(kr, 64, 1)).astype(BF16)
        for h in range(MLA_HEADS):
            qc_ref[h, :, 0:128] = q[:, 128 * h:128 * h + 128].astype(BF16)
            chunk = qr[:, 128 * (h // 2):128 * (h // 2) + 128]
            qc_ref[h, :, 128:256] = jnp.where(half == (h % 2), chunk, 0.0).astype(BF16)
            kc_ref[h, :, 0:128] = _dot(kvn, wukv_ref[2 * h]).astype(BF16)
            kc_ref[h, :, 128:256] = kr2
            vv_ref[h] = _dot(kvn, wukv_ref[2 * h + 1]).astype(BF16)

    const = lambda shape: pl.BlockSpec(shape, lambda i: tuple(0 for _ in shape))
    return pl.pallas_call(
        body, name="mla_pre_fwd", grid=(S // tm,),
        in_specs=[pl.BlockSpec((tm, 256), lambda i: (i, 3)), pl.BlockSpec((tm, 128), lambda i: (i, 8)),
                  pl.BlockSpec((tm, 128), lambda i: (i, 9)), const((1, 256)), const((1, 128)),
                  const((768, 256)), const((8, 128, 128)),
                  pl.BlockSpec((tm, 128), lambda i: (i, 0)), pl.BlockSpec((tm, 128), lambda i: (i, 0))],
        out_specs=[pl.BlockSpec((MLA_HEADS, tm, 256), lambda i: (0, i, 0)),
                   pl.BlockSpec((MLA_HEADS, tm, 256), lambda i: (0, i, 0)),
                   pl.BlockSpec((MLA_HEADS, tm, 128), lambda i: (0, i, 0))],
        out_shape=[jax.ShapeDtypeStruct((MLA_HEADS, S, 256), BF16), jax.ShapeDtypeStruct((MLA_HEADS, S, 256), BF16),
                   jax.ShapeDtypeStruct((MLA_HEADS, S, 128), BF16)],
        compiler_params=_params(("parallel",)),
    )(proj, proj, proj, qn_w, kvn_w, wuqT, wukv, cos, sin)


def _causal(i, j, t):
    row = i * t + lax.broadcasted_iota(jnp.int32, (t, t), 0)
    col = j * t + lax.broadcasted_iota(jnp.int32, (t, t), 1)
    return col <= row


def _mla_attn_fwd(qc, kc, vv, *, t=512):
    S = qc.shape[1]
    t = min(t, S)

    def body(q_ref, k_ref, v_ref, o_ref, l_ref):
        i = pl.program_id(0)
        diag = _causal(0, 0, t)

        def step(j, carry, masked):
            rows = pl.ds(pl.multiple_of(j * t, t), t)
            out = []
            for h in range(MLA_HEADS):
                m, l, acc = carry[h]
                s = _dot_nt(q_ref[h], k_ref[h, rows, :]) * MLA_SCALE
                if masked:
                    s = jnp.where(diag, s, -jnp.inf)
                m_new = jnp.maximum(m, jnp.max(s, axis=-1, keepdims=True))
                alpha = jnp.exp(m - m_new)
                p = jnp.exp(s - m_new)
                l = alpha * l + jnp.sum(p, axis=-1, keepdims=True)
                acc = alpha * acc + _dot(p.astype(BF16), v_ref[h, rows, :])
                out.append((m_new, l, acc))
            return tuple(out)

        init = tuple((jnp.full((t, 1), -jnp.inf, F32), jnp.zeros((t, 1), F32), jnp.zeros((t, MLA_V), F32))
                     for _ in range(MLA_HEADS))
        carry = lax.fori_loop(0, i, lambda j, c: step(j, c, False), init)
        carry = step(i, carry, True)
        for h in range(MLA_HEADS):
            m, l, acc = carry[h]
            o_ref[:, 128 * h:128 * h + 128] = acc / l
            l_ref[h] = jnp.broadcast_to(m + jnp.log(l), (t, 128))

    return pl.pallas_call(
        body, name="mla_attn_fwd", grid=(S // t,),
        in_specs=[pl.BlockSpec((MLA_HEADS, t, 256), lambda i: (0, i, 0)),
                  pl.BlockSpec((MLA_HEADS, S, 256), lambda i: (0, 0, 0)),
                  pl.BlockSpec((MLA_HEADS, S, 128), lambda i: (0, 0, 0))],
        out_specs=[pl.BlockSpec((t, 512), lambda i: (i, 0)),
                   pl.BlockSpec((MLA_HEADS, t, 128), lambda i: (0, i, 0))],
        out_shape=[jax.ShapeDtypeStruct((S, 512), F32), jax.ShapeDtypeStruct((MLA_HEADS, S, 128), F32)],
        compiler_params=_params(("parallel",)),
    )(qc, kc, vv)


def _mla_attn_bwd(qc, kc, vv, o, lse, do, *, t=512, tq=1024):
    S = qc.shape[1]
    t = min(t, S)
    tq = min(tq, S)
    nblk = S // t
    hp = MLA_HEADS
    once = pl.Buffered(1)

    def body(q_ref, k_ref, v_ref, o_ref, l_ref, do_ref, dq_ref, dk_ref, dv_ref):
        j = pl.program_id(1)

        @pl.when(j == 0)
        def _():
            dq_ref[...] = jnp.zeros_like(dq_ref)

        first = (j * t) // tq

        def step(i, carry, masked):
            rows = pl.ds(pl.multiple_of(i * tq, tq), tq)
            if masked:
                row = i * tq + lax.broadcasted_iota(jnp.int32, (tq, t), 0)
                col = j * t + lax.broadcasted_iota(jnp.int32, (tq, t), 1)
                visible = col <= row
            out = []
            for h in range(hp):
                dk, dv = carry[h]
                k = k_ref[h]
                q = q_ref[h, rows, :]
                dov = do_ref[rows, 128 * h:128 * h + 128]
                lrow = l_ref[h, rows, :][:, 0:1]
                p = jnp.exp(_dot_nt(q, k) * MLA_SCALE - lrow)
                if masked:
                    p = jnp.where(visible, p, 0.0)
                dob = dov.astype(BF16)
                dv = dv + _dot_tn(p.astype(BF16), dob)
                dp = _dot_nt(dob, v_ref[h])
                delta = jnp.sum(dov * o_ref[rows, 128 * h:128 * h + 128], axis=-1, keepdims=True)
                ds = (p * (dp - delta) * MLA_SCALE).astype(BF16)
                dk = dk + _dot_tn(ds, q)
                dq_ref[h, rows, :] += _dot(ds, k)
                out.append((dk, dv))
            return tuple(out)

        init = tuple((jnp.zeros((t, 256), F32), jnp.zeros((t, MLA_V), F32)) for _ in range(hp))
        carry = step(first, init, True)
        carry = lax.fori_loop(first + 1, S // tq, lambda i, c: step(i, c, False), carry)
        for h in range(hp):
            dk_ref[h] = carry[h][0]
            dv_ref[h] = carry[h][1]

    return pl.pallas_call(
        body, name="mla_attn_bwd", grid=(MLA_HEADS // hp, nblk),
        in_specs=[pl.BlockSpec((hp, S, 256), lambda g, j: (g, 0, 0), pipeline_mode=once),
                  pl.BlockSpec((hp, t, 256), lambda g, j: (g, j, 0)),
                  pl.BlockSpec((hp, t, 128), lambda g, j: (g, j, 0)),
                  pl.BlockSpec((S, 128 * hp), lambda g, j: (0, g), pipeline_mode=once),
                  pl.BlockSpec((hp, S, 128), lambda g, j: (g, 0, 0), pipeline_mode=once),
                  pl.BlockSpec((S, 128 * hp), lambda g, j: (0, g), pipeline_mode=once)],
        out_specs=[pl.BlockSpec((hp, S, 256), lambda g, j: (g, 0, 0)),
                   pl.BlockSpec((hp, t, 256), lambda g, j: (g, j, 0)),
                   pl.BlockSpec((hp, t, 128), lambda g, j: (g, j, 0))],
        out_shape=[jax.ShapeDtypeStruct((MLA_HEADS, S, 256), F32), jax.ShapeDtypeStruct((MLA_HEADS, S, 256), F32),
                   jax.ShapeDtypeStruct((MLA_HEADS, S, 128), F32)],
        compiler_params=_params(("parallel", "arbitrary")),
    )(qc, kc, vv, o, lse, do)


def _mla_pre_bwd(proj, qn_w, kvn_w, wuqT, wukv, cos, sin, dqc, dkc, dvv, *, tm=ROW_TILE):
    S = proj.shape[0]
    tm = min(tm, S)

    def body(ql_ref, kl_ref, qw_ref, kw_ref, wuq_ref, wukv_ref, cos_ref, sin_ref, dqc_ref, dkc_ref, dvv_ref,
             dql_ref, dkl_ref, dkr_ref, gq_ref, gkv_ref, part_ref):
        @pl.when(pl.program_id(0) == 0)
        def _():
            gq_ref[...] = jnp.zeros_like(gq_ref)
            gkv_ref[...] = jnp.zeros_like(gkv_ref)
            part_ref[...] = jnp.zeros_like(part_ref)

        cs, sn = _rope_wide(cos_ref), _rope_wide(sin_ref)
        half = lax.broadcasted_iota(jnp.int32, (tm, 128), 1) // 64
        ql = ql_ref[...]
        rq = _rstd(ql)
        qhat = ql * rq
        qw = qw_ref[...]
        qn = (qhat * qw).astype(BF16)
        chunks = []
        for pair in range(2):
            chunks.append(jnp.where(half == 0, dqc_ref[2 * pair, :, 128:256], dqc_ref[2 * pair + 1, :, 128:256]))
        dqr = jnp.concatenate(chunks, axis=1)
        dqr = dqr * cs + _swap_halves(dqr * sn)
        dq = jnp.concatenate([dqc_ref[h, :, 0:128] for h in range(MLA_HEADS)] + [dqr], axis=1).astype(BF16)
        gq_ref[...] += _dot_tn(dq, qn)
        dqn = _dot(dq, wuq_ref[...])
        part_ref[0:1, :] += jnp.sum(dqn * qhat, axis=0, keepdims=True)
        dql_ref[...] = _rms_bwd(dqn * qw, qhat, rq)
        kl = kl_ref[...]
        rk = _rstd(kl)
        khat = kl * rk
        kw = kw_ref[...]
        kvn = (khat * kw).astype(BF16)
        dkvn = jnp.zeros((tm, MLA_KVR), F32)
        dkr2 = jnp.zeros((tm, 128), F32)
        for h in range(MLA_HEADS):
            dkn = dkc_ref[h, :, 0:128].astype(BF16)
            dvh = dvv_ref[h].astype(BF16)
            gkv_ref[2 * h] += _dot_tn(kvn, dkn)
            gkv_ref[2 * h + 1] += _dot_tn(kvn, dvh)
            dkvn += _dot_nt(dkn, wukv_ref[2 * h]) + _dot_nt(dvh, wukv_ref[2 * h + 1])
            dkr2 += dkc_ref[h, :, 128:256]
        part_ref[1:2, 0:128] += jnp.sum(dkvn * khat, axis=0, keepdims=True)
        dkl_ref[...] = _rms_bwd(dkvn * kw, khat, rk)
        dkr = jnp.where(half == 0, dkr2 + pltpu.roll(dkr2, 64, 1), 0.0)
        dkr_ref[...] = dkr * cs[:, :128] + _swap_halves(dkr * sn[:, :128])

    const = lambda shape: pl.BlockSpec(shape, lambda i: tuple(0 for _ in shape))
    heads = lambda w: pl.BlockSpec((MLA_HEADS, tm, w), lambda i: (0, i, 0))
    return pl.pallas_call(
        body, name="mla_pre_bwd", grid=(S // tm,),
        in_specs=[pl.BlockSpec((tm, 256), lambda i: (i, 3)), pl.BlockSpec((tm, 128), lambda i: (i, 8)),
                  const((1, 256)), const((1, 128)), const((768, 256)), const((8, 128, 128)),
                  pl.BlockSpec((tm, 128), lambda i: (i, 0)), pl.BlockSpec((tm, 128), lambda i: (i, 0)),
                  heads(256), heads(256), heads(128)],
        out_specs=[pl.BlockSpec((tm, 256), lambda i: (i, 0)), pl.BlockSpec((tm, 128), lambda i: (i, 0)),
                   pl.BlockSpec((tm, 128), lambda i: (i, 0)), const((768, 256)), const((8, 128, 128)), const((8, 256))],
        out_shape=[jax.ShapeDtypeStruct((S, 256), F32), jax.ShapeDtypeStruct((S, 128), F32),
                   jax.ShapeDtypeStruct((S, 128), F32), jax.ShapeDtypeStruct((768, 256), F32),
                   jax.ShapeDtypeStruct((8, 128, 128), F32), jax.ShapeDtypeStruct((8, 256), F32)],
        compiler_params=_params(("arbitrary",)),
    )(proj, proj, qn_w, kvn_w, wuqT, wukv, cos, sin, dqc, dkc, dvv)


def _mix_out_fwd(x, oa, ob, w_o, vecs, *, tm=ROW_TILE):
    S = x.shape[0]
    tm = min(tm, S)

    def body(x_ref, oa_ref, ob_ref, w_ref, vec_ref, xo_ref, mo_ref):
        mo = _dot(oa_ref[...].astype(BF16), w_ref[0:512, :]) + _dot(ob_ref[...].astype(BF16), w_ref[512:1024, :])
        mo_ref[...] = mo
        xo_ref[...] = x_ref[...] + vec_ref[3:4, :] * mo

    row = pl.BlockSpec((tm, D), lambda i: (i, 0))
    half = pl.BlockSpec((tm, 512), lambda i: (i, 0))
    return pl.pallas_call(
        body, name="mix_out_fwd", grid=(S // tm,),
        in_specs=[row, half, half, pl.BlockSpec((D, D), lambda i: (0, 0)), pl.BlockSpec((8, D), lambda i: (0, 0))],
        out_specs=[row, row],
        out_shape=[jax.ShapeDtypeStruct((S, D), F32), jax.ShapeDtypeStruct((S, D), F32)],
        compiler_params=_params(("parallel",)),
    )(x, oa, ob, w_o, vecs)


def _mix_out_bwd(dxo, mo, oa, ob, w_o, vecs, *, tm=ROW_TILE):
    S = dxo.shape[0]
    tm = min(tm, S)

    def body(dx_ref, mo_ref, oa_ref, ob_ref, w_ref, vec_ref, doa_ref, dob_ref, gw_ref, part_ref):
        @pl.when(pl.program_id(0) == 0)
        def _():
            gw_ref[...] = jnp.zeros_like(gw_ref)
            part_ref[...] = jnp.zeros_like(part_ref)

        dx = dx_ref[...]
        part_ref[0:1, :] += jnp.sum(dx * mo_ref[...], axis=0, keepdims=True)
        dmo = (vec_ref[3:4, :] * dx).astype(BF16)
        doa_ref[...] = _dot_nt(dmo, w_ref[0:512, :])
        dob_ref[...] = _dot_nt(dmo, w_ref[512:1024, :])
        gw_ref[0:512, :] += _dot_tn(oa_ref[...].astype(BF16), dmo)
        gw_ref[512:1024, :] += _dot_tn(ob_ref[...].astype(BF16), dmo)

    row = pl.BlockSpec((tm, D), lambda i: (i, 0))
    half = pl.BlockSpec((tm, 512), lambda i: (i, 0))
    return pl.pallas_call(
        body, name="mix_out_bwd", grid=(S // tm,),
        in_specs=[row, row, half, half, pl.BlockSpec((D, D), lambda i: (0, 0)), pl.BlockSpec((8, D), lambda i: (0, 0))],
        out_specs=[half, half, pl.BlockSpec((D, D), lambda i: (0, 0)), pl.BlockSpec((8, D), lambda i: (0, 0))],
        out_shape=[jax.ShapeDtypeStruct((S, 512), F32), jax.ShapeDtypeStruct((S, 512), F32),
                   jax.ShapeDtypeStruct((D, D), F32), jax.ShapeDtypeStruct((8, D), F32)],
        compiler_params=_params(("arbitrary",)),
    )(dxo, mo, oa, ob, w_o, vecs)


def _mix_in_bwd(h, w_inT, dq, dk, dv, dql, dkl, dkr, *, tm=ROW_TILE):
    S = h.shape[0]
    tm = min(tm, S)
    offs = (0, 512, 640, 768, 1024, 1152)
    wid = (512, 128, 128, 256, 128, 128)

    def body(h_ref, w_ref, dq_ref, dk_ref, dv_ref, dql_ref, dkl_ref, dkr_ref, dh_ref, gw_ref):
        @pl.when(pl.program_id(0) == 0)
        def _():
            gw_ref[...] = jnp.zeros_like(gw_ref)

        hv = h_ref[...]
        dh = jnp.zeros((tm, D), F32)
        for ref, o, w in zip((dq_ref, dk_ref, dv_ref, dql_ref, dkl_ref, dkr_ref), offs, wid):
            w = min(w, D_IN - o)
            dpart = ref[...][:, :w].astype(BF16)
            dh += _dot(dpart, w_ref[o:o + w, :])
            gw_ref[o:o + w, :] += _dot_tn(dpart, hv)
        dh_ref[...] = dh

    row = pl.BlockSpec((tm, D), lambda i: (i, 0))
    part = lambda w: pl.BlockSpec((tm, w), lambda i: (i, 0))
    return pl.pallas_call(
        body, name="mix_in_bwd", grid=(S // tm,),
        in_specs=[row, pl.BlockSpec((D_IN_PAD, D), lambda i: (0, 0))] + [part(w) for w in wid],
        out_specs=[row, pl.BlockSpec((D_IN, D), lambda i: (0, 0))],
        out_shape=[jax.ShapeDtypeStruct((S, D), F32), jax.ShapeDtypeStruct((D_IN, D), F32)],
        compiler_params=_params(("arbitrary",)),
    )(h, w_inT, dq, dk, dv, dql, dkl, dkr)


def _vecs(norm_w, mod9, k):
    return jnp.concatenate([norm_w.reshape(1, D), mod9[3 * k:3 * k + 3], jnp.zeros((4, D), F32)], axis=0)


def _uq_group_rows(wuqT):
    per = MLA_NOPE + MLA_ROPE
    nope = [wuqT[per * h:per * h + MLA_NOPE] for h in range(MLA_HEADS)]
    rope = [wuqT[per * h + MLA_NOPE:per * (h + 1)] for h in range(MLA_HEADS)]
    return jnp.concatenate(nope + rope, axis=0)


def _uq_ungroup_rows(g):
    parts = []
    for h in range(MLA_HEADS):
        parts += [g[MLA_NOPE * h:MLA_NOPE * (h + 1)], g[512 + MLA_ROPE * h:512 + MLA_ROPE * (h + 1)]]
    return jnp.concatenate(parts, axis=0)


def _local_step(x, tgt, mod9, norms, sinks, rel_bias, q_norm, kv_norm, W, on_grads=None):
    if on_grads is None:
        on_grads = lambda group, grads, after, vecs: vecs
    S = x.shape[0]
    v1 = _vecs(norms["ffn1"], mod9, 0)
    v2 = _vecs(norms["mix"], mod9, 1)
    v3 = _vecs(norms["ffn2"], mod9, 2)
    bucket = jnp.asarray(_bucket_table())
    cos, sin = _rope_tables(S)
    if isinstance(W, dict):
        full, W = W, (lambda group, after, vecs: (full, vecs))

    W1, v1 = W("ffn1", [], v1)
    x1, h1, a1, b1, f1 = _ffn_fwd(x, v1, W1["g1T"], W1["u1T"], W1["d1"], name="ffn1_fwd")
    W2, v2 = W("mixer", [x1], v2)
    w_inT = jnp.pad(W2["w_inT"], ((0, D_IN_PAD - D_IN), (0, 0))).astype(BF16)
    wuqT = _uq_group_rows(W2["w_uqT"])
    h2, proj = _mix_in_fwd(x1, v2, w_inT)
    bias = _bias_build(rel_bias, bucket)
    oa = _swa_fwd(proj, bias, sinks)
    qc, kc, vv = _mla_pre_fwd(proj, q_norm, kv_norm, wuqT, W2["w_ukv"], cos, sin)
    ob, lse = _mla_attn_fwd(qc, kc, vv)
    _, v2o = W("ffn2_on_its_way", [ob], v2)
    x2, mo = _mix_out_fwd(x1, oa, ob, W2["w_o"], v2o)
    W3, v3 = W("ffn2", [x2], v3)
    x3, h3, a3, b3, f3 = _ffn_fwd(x2, v3, W3["g3T"], W3["u3T"], W3["d3"], name="ffn2_fwd")
    dx3, head_part, df3 = _head(x3, tgt, norms["final"], f3, v3)

    gg3, gu3, gd3, dh3 = _ffn_bwd_main(h3, df3, a3, b3, W3["g3T"], W3["u3T"], W3["d3"], name="ffn2_bwd")
    ffn2 = {"g3T": gg3, "u3T": gu3, "d3": gd3}
    v3 = on_grads("ffn2", ffn2, [], v3)
    dx2, n3_part = _norm_bwd(dh3, x2, dx3, v3, name="ffn2_norm_bwd")
    v2 = on_grads("ffn2", None, [dx2], v2)
    doa, dob, g_wo, g2_part = _mix_out_bwd(dx2, mo, oa, ob, W2["w_o"], v2)
    dq, dk, dv, drb, dsk = _swa_bwd(proj, bias, sinks, oa, doa, bucket)
    dqc, dkc, dvv = _mla_attn_bwd(qc, kc, vv, ob, lse, dob)
    dql, dkl, dkr, g_uq, g_ukv, mla_part = _mla_pre_bwd(proj, q_norm, kv_norm, wuqT, W2["w_ukv"], cos, sin, dqc, dkc, dvv)
    dh2, g_win = _mix_in_bwd(h2, w_inT, dq, dk, dv, dql, dkl, dkr)
    mixer = {"w_inT": g_win, "w_uqT": _uq_ungroup_rows(g_uq).astype(BF16),
             "w_ukv": g_ukv.astype(BF16), "w_o": g_wo.astype(BF16)}
    v2 = on_grads("mixer", mixer, [], v2)
    dx1, n2_part, df1 = _norm_bwd(dh2, x1, dx2, v2, name="mix_norm_bwd", below=(f1, v1))
    started = on_grads("mixer", None, [dx1], jnp.zeros((1, 1), F32))
    gg1, gu1, gd1, dh1 = _ffn_bwd_main(h1, df1, a1, b1, W1["g1T"], W1["u1T"], W1["d1"], name="ffn1_bwd",
                                       after=[started])
    ffn1 = {"g1T": gg1, "u1T": gu1, "d1": gd1}
    v1 = on_grads("ffn1", ffn1, [], v1)
    dx0, n1_part = _norm_bwd(dh1, x, dx1, v1, name="ffn1_norm_bwd")

    grads = {**ffn1, **ffn2, **mixer}
    return head_part[1, 0], dx0, grads, _pack_vec(n1_part, n2_part, n3_part, head_part, g2_part, mla_part, dsk, drb)


SMALL_LAYOUT = (("norm_ffn1", 1024), ("norm_mix", 1024), ("norm_ffn2", 1024), ("norm_final", 1024),
                ("q_norm", 256), ("kv_norm", 128), ("sinks", 128), ("rel_bias", 256))
N_SMALL = sum(n for _, n in SMALL_LAYOUT)
LOSS_SLOT = 4 * 1024 + 256 + 128 + SWA_HEADS
N_MODVEC = N_MOD * D
N_VEC = N_MODVEC + N_SMALL


def _pack_vec(n1, n2, n3, head, g2, mla, dsk, drb):
    def body(n1_ref, n2_ref, n3_ref, head_ref, g2_ref, mla_ref, dsk_ref, drb_ref, out_ref):
        rows = [n1_ref[1:2, :], n1_ref[2:3, :], n2_ref[3:4, :], n2_ref[1:2, :], n2_ref[2:3, :], g2_ref[0:1, :],
                n3_ref[1:2, :], n3_ref[2:3, :], head_ref[3:4, :],
                n1_ref[0:1, :], n2_ref[0:1, :], n3_ref[0:1, :], head_ref[0:1, :]]
        for i, row in enumerate(rows):
            out_ref[:, D * i:D * (i + 1)] = row
        off = D * len(rows)
        out_ref[:, off:off + 256] = mla_ref[0:1, :]
        out_ref[:, off + 256:off + 384] = mla_ref[1:2, 0:128]

        def diagonal(block):
            r = lax.broadcasted_iota(jnp.int32, block.shape, 0)
            lane = lax.broadcasted_iota(jnp.int32, block.shape, 1)
            return jnp.sum(jnp.where(r == lane, block, 0.0), axis=0, keepdims=True)

        lane = lax.broadcasted_iota(jnp.int32, (1, 128), 1)
        out_ref[:, off + 384:off + 512] = jnp.where(lane == SWA_HEADS, head_ref[1:2, 0:128], diagonal(dsk_ref[...]))
        out_ref[:, off + 512:off + 640] = diagonal(drb_ref[0:128, :])
        out_ref[:, off + 640:off + 768] = diagonal(drb_ref[128:256, :])

    vm = pl.BlockSpec(memory_space=pltpu.VMEM)
    return pl.pallas_call(body, name="pack_vec", in_specs=[vm] * 8, out_specs=vm,
                          out_shape=jax.ShapeDtypeStruct((1, N_VEC), F32))(n1, n2, n3, head, g2, mla, dsk, drb)


def _coords():
    return lax.axis_index("x"), lax.axis_index("y"), lax.axis_index("c")


def _flip(v, bit):
    return 1 - v if bit else v


def _peer(r):
    x, y, c = _coords()
    return (_flip(x, r & 4), _flip(y, r & 2), _flip(c, r & 1))


def _mod_fwd(c_tile, w_mod, b_mod3):
    W = w_mod.shape[1]

    def body(c_ref, w_ref, b_ref, mod_ref, ca_ref, call_ref, part_ref, send_sems, recv_sems):
        x, y, c = _coords()
        me = 4 * x + 2 * y + c
        call_ref[me] = c_ref[...]
        sends = []
        for r in range(1, N_DEV):
            cp = pltpu.make_async_remote_copy(c_ref, call_ref.at[me], send_sems.at[0, r], recv_sems.at[0, r],
                                              device_id=_peer(r), device_id_type=MESH)
            cp.start()
            sends.append(cp)
        for r in range(1, N_DEV):
            pltpu.make_async_remote_copy(c_ref, call_ref.at[me], send_sems.at[0, r], recv_sems.at[0, r],
                                         device_id=_peer(r), device_id_type=MESH).wait_recv()
        cv = call_ref[...].reshape(8 * N_DEV, D)
        ca = (cv * _sigmoid(cv)).astype(BF16)
        ca_ref[...] = ca
        part_ref[...] = _dot(ca, w_ref[...].astype(BF16)).reshape(N_DEV, 8, W)
        mod_ref[me] = part_ref[me] + b_ref[me]
        for r in range(1, N_DEV):
            cp = pltpu.make_async_remote_copy(part_ref.at[me ^ r], mod_ref.at[me], send_sems.at[1, r],
                                              recv_sems.at[1, r], device_id=_peer(r), device_id_type=MESH)
            cp.start()
            sends.append(cp)
        for r in range(1, N_DEV):
            pltpu.make_async_remote_copy(part_ref.at[me ^ r], mod_ref.at[me], send_sems.at[1, r],
                                         recv_sems.at[1, r], device_id=_peer(r), device_id_type=MESH).wait_recv()
            mod_ref[me ^ r] = mod_ref[me ^ r] + b_ref[me ^ r]
        for cp in sends:
            cp.wait_send()

    vm = pl.BlockSpec(memory_space=pltpu.VMEM)
    return pl.pallas_call(
        body, name="mod_fwd", in_specs=[vm, vm, vm], out_specs=[vm, vm],
        out_shape=[jax.ShapeDtypeStruct((N_DEV, 8, W), F32), jax.ShapeDtypeStruct((8 * N_DEV, D), BF16)],
        scratch_shapes=[pltpu.VMEM((N_DEV, 8, D), F32), pltpu.VMEM((N_DEV, 8, W), F32),
                        pltpu.SemaphoreType.DMA((2, N_DEV)), pltpu.SemaphoreType.DMA((2, N_DEV))],
        compiler_params=_params(),
    )(c_tile, w_mod, b_mod3)


def _mod_bwd(allvec, ca, me_idx):
    W = N_MODVEC // N_DEV

    def body(me_ref, all_ref, cols_ref, ca_ref, gw_ref, sum_ref):
        in_first_row = lax.broadcasted_iota(jnp.int32, (N_DEV, 8, W), 1) == 0
        dm = jnp.where(in_first_row, cols_ref[...], 0.0).reshape(8 * N_DEV, W)
        gw_ref[...] = _dot_tn(ca_ref[...], dm.astype(BF16))
        total = all_ref[0]
        for k in range(1, N_DEV):
            total = total + all_ref[k]
        sum_ref[...] = total

    return pl.pallas_call(
        body, name="mod_bwd",
        grid_spec=pltpu.PrefetchScalarGridSpec(
            num_scalar_prefetch=1, grid=(1,),
            in_specs=[pl.BlockSpec((N_DEV, 1, N_VEC), lambda i, me: (0, 0, 0)),
                      pl.BlockSpec((N_DEV, 1, W), lambda i, me: (0, 0, me[0])),
                      pl.BlockSpec((8 * N_DEV, D), lambda i, me: (0, 0))],
            out_specs=[pl.BlockSpec((D, W), lambda i, me: (0, 0)), pl.BlockSpec((1, N_VEC), lambda i, me: (0, 0))]),
        out_shape=[jax.ShapeDtypeStruct((D, W), F32), jax.ShapeDtypeStruct((1, N_VEC), F32)],
        compiler_params=_params(("arbitrary",)),
    )(me_idx, allvec, allvec, ca)


def _wgather(shards):
    n = len(shards)
    rows = [s.shape[0] for s in shards]

    def body(*refs):
        ins, outs, token = refs[:n], refs[n:2 * n], refs[2 * n]
        send_sems, recv_sems, local_sems = refs[2 * n + 1:]
        token[...] = jnp.zeros_like(token)
        x, y, c = _coords()
        me = 4 * x + 2 * y + c
        sib, xn, yn = (x, y, 1 - c), (1 - x, y, c), (x, 1 - y, c)
        block = lambda px, py, pc: 4 * px + 2 * py + pc

        def part(k, blk, half):
            if half is None:
                return outs[k].at[blk]
            return outs[k].at[blk, pl.ds(half * (rows[k] // 2), rows[k] // 2)]

        def copy(k, slot, blk, to, half=None, src=None):
            ref = part(k, blk, half)
            return pltpu.make_async_remote_copy(
                src_ref=ref if src is None else src, dst_ref=ref, send_sem=send_sems.at[k, slot],
                recv_sem=recv_sems.at[k, slot], device_id=to, device_id_type=MESH)

        local = [pltpu.make_async_copy(ins[k], outs[k].at[me], local_sems.at[k]) for k in range(n)]
        for cp in local:
            cp.start()
        sent = [copy(k, slot, me, to, src=ins[k]) for k in range(n) for slot, to in ((0, sib), (1, xn), (2, yn))]
        for cp in sent:
            cp.start()
        bx, by, bd = block(1 - x, y, c), block(x, 1 - y, c), block(1 - x, 1 - y, c)
        for k in range(n):
            copy(k, 1, bx, sib).wait_recv()
            sent += [copy(k, 4, bx, yn, half=1), copy(k, 5, bx, sib)]
            sent[-2].start()
            sent[-1].start()
        for k in range(n):
            copy(k, 2, by, sib).wait_recv()
            sent += [copy(k, 3, by, xn, half=0), copy(k, 6, by, sib)]
            sent[-2].start()
            sent[-1].start()
        for k in range(n):
            copy(k, 3, bd, sib, half=0).wait_recv()
            copy(k, 4, bd, sib, half=1).wait_recv()
            sent.append(copy(k, 7, bd, sib))
            sent[-1].start()
        for k in range(n):
            copy(k, 0, block(x, y, 1 - c), sib).wait_recv()
            for slot, blk in ((5, block(1 - x, y, 1 - c)), (6, block(x, 1 - y, 1 - c)), (7, block(1 - x, 1 - y, 1 - c))):
                copy(k, slot, blk, sib).wait_recv()
        for cp in sent:
            cp.wait_send()
        for cp in local:
            cp.wait()

    anyspec = pl.BlockSpec(memory_space=pl.ANY)
    return pl.pallas_call(
        body, name="wgather", in_specs=[anyspec] * n,
        out_specs=[anyspec] * n + [pl.BlockSpec(memory_space=pltpu.VMEM)],
        out_shape=[jax.ShapeDtypeStruct((N_DEV,) + s.shape, s.dtype) for s in shards]
        + [jax.ShapeDtypeStruct((8, 128), F32)],
        scratch_shapes=[pltpu.SemaphoreType.DMA((n, 8)), pltpu.SemaphoreType.DMA((n, 8)),
                        pltpu.SemaphoreType.DMA((n,))],
    )(*shards)


class _GatherCopies:
    def __init__(self, lands, send_sems, recv_sems, k0=0, batches=None):
        x, y, c = _coords()
        me = 4 * x + 2 * y + c
        sib = (x, y, 1 - c)
        chips = [(1 - x, y), (x, 1 - y), (1 - x, 1 - y)]

        def copy(k, slot, block, to):
            return pltpu.make_async_remote_copy(
                src_ref=lands[k].at[block], dst_ref=lands[k].at[block],
                send_sem=send_sems.at[7 * (k0 + k) + slot], recv_sem=recv_sems.at[7 * (k0 + k) + slot],
                device_id=to, device_id_type=MESH)

        n = len(lands)
        self.first = [copy(k, 0, me, sib) for k in range(n)]
        for batch in batches or [range(n)]:
            self.first += [copy(k, 1 + j, me, (cx, cy, c)) for j, (cx, cy) in enumerate(chips) for k in batch]
        self.landed = [copy(k, 1 + j, 4 * cx + 2 * cy + c, sib) for j, (cx, cy) in enumerate(chips) for k in range(n)]
        self.passed = [copy(k, 4 + j, 4 * cx + 2 * cy + c, sib) for j, (cx, cy) in enumerate(chips) for k in range(n)]
        self.from_sib = [copy(k, 0, 4 * x + 2 * y + (1 - c), sib) for k in range(n)]
        self.from_sib += [copy(k, 4 + j, 4 * cx + 2 * cy + (1 - c), sib) for j, (cx, cy) in enumerate(chips)
                          for k in range(n)]


def _gather_start(lands, *, name, batches=None):
    n = len(lands)

    def body(*refs):
        for cp in _GatherCopies(refs[:n], refs[n], refs[n + 1], batches=batches).first:
            cp.start()
        refs[-1][...] = jnp.zeros_like(refs[-1])

    out = pl.pallas_call(
        body, name=name,
        out_shape=(pltpu.SemaphoreType.DMA((7 * n,)), pltpu.SemaphoreType.DMA((7 * n,)),
                   *[pltpu.HBM(l.shape, l.dtype) for l in lands], jax.ShapeDtypeStruct((8, 128), F32)),
        in_specs=[HBM_SPEC] * n,
        out_specs=(SEM_SPEC, SEM_SPEC, *[HBM_SPEC] * n, pl.BlockSpec(memory_space=pltpu.VMEM)),
        input_output_aliases={i: 2 + i for i in range(n)},
        compiler_params=pltpu.CompilerParams(has_side_effects=DATAFLOW),
    )(*[_in_hbm(l) for l in lands])
    return out[0], out[1], list(out[2:2 + n]), out[-1]


def _gather_pass(send_sems, recv_sems, lands, after, *, name, stage, k0=0):
    n = len(lands)

    def body(*refs):
        cps = _GatherCopies(refs[:n], refs[n], refs[n + 1], k0)
        if stage == "landed":
            for cp in cps.landed:
                cp.wait_recv()
        else:
            for cp in cps.passed:
                cp.start()
        refs[-1][...] = jnp.zeros_like(refs[-1])

    out = pl.pallas_call(
        body, name=name,
        out_shape=(*[pltpu.HBM(l.shape, l.dtype) for l in lands], jax.ShapeDtypeStruct((8, 128), F32)),
        in_specs=[HBM_SPEC] * n + [SEM_SPEC, SEM_SPEC] + [pl.BlockSpec(memory_space=pl.ANY)] * len(after),
        out_specs=(*[HBM_SPEC] * n, pl.BlockSpec(memory_space=pltpu.VMEM)),
        input_output_aliases={i: i for i in range(n)},
        compiler_params=pltpu.CompilerParams(has_side_effects=DATAFLOW),
    )(*lands, send_sems, recv_sems, *after)
    return list(out[:n]), out[-1]


def _gather_end(send_sems, recv_sems, lands, after, *, name, k0=0):
    n = len(lands)

    def body(*refs):
        cps = _GatherCopies(refs[:n], refs[n], refs[n + 1], k0)
        for cp in cps.from_sib:
            cp.wait_recv()
        for cp in cps.first + cps.passed:
            cp.wait_send()

    out = pl.pallas_call(
        body, name=name,
        out_shape=[pltpu.HBM(l.shape, l.dtype) for l in lands],
        in_specs=[HBM_SPEC] * n + [SEM_SPEC, SEM_SPEC] + [pl.BlockSpec(memory_space=pl.ANY)] * len(after),
        out_specs=[HBM_SPEC] * n,
        input_output_aliases={i: i for i in range(n)},
        compiler_params=pltpu.CompilerParams(has_side_effects=DATAFLOW),
    )(*lands, send_sems, recv_sems, *after)
    return list(out)


def _d2d_copies(grads, lands, send_sems, recv_sems):
    x, y, c = _coords()
    return [pltpu.make_async_remote_copy(
        src_ref=grads[k].at[2 * q + (1 - c)], dst_ref=lands[k].at[q],
        send_sem=send_sems.at[4 * k + q], recv_sem=recv_sems.at[4 * k + q],
        device_id=(x, y, 1 - c), device_id_type=MESH) for k in range(len(grads)) for q in range(4)]


def _direct_copies(grads, lands, send_sems, recv_sems):
    x, y, c = _coords()
    me = 4 * x + 2 * y + c
    return [pltpu.make_async_remote_copy(
        src_ref=grads[k].at[me ^ r], dst_ref=lands[k].at[r - 1],
        send_sem=send_sems.at[7 * k + r - 1], recv_sem=recv_sems.at[7 * k + r - 1],
        device_id=_peer(r), device_id_type=MESH) for k in range(len(grads)) for r in range(1, N_DEV)]


def _vec_copies(srcs, lands, send_sems, recv_sems):
    x, y, c = _coords()
    me = 4 * x + 2 * y + c
    return [pltpu.make_async_remote_copy(
        src_ref=lands[0].at[me], dst_ref=lands[0].at[me], send_sem=send_sems.at[r - 1], recv_sem=recv_sems.at[r - 1],
        device_id=_peer(r), device_id_type=MESH) for r in range(1, N_DEV)]


def _chipsum(gs, sibs, cidx, *, name):
    n = len(gs)

    def body(c_ref, *refs):
        for k in range(n):
            refs[2 * n + k][...] = (refs[k][...].astype(F32) + refs[n + k][...].astype(F32)).astype(refs[2 * n + k].dtype)

    mine = [pl.BlockSpec((1,) + g.shape[1:], lambda q, c_ref: (2 * q + c_ref[0], 0, 0)) for g in gs]
    other = [pl.BlockSpec((1,) + g.shape[1:], lambda q, c_ref: (q, 0, 0)) for g in gs]
    return pl.pallas_call(
        body, name=name,
        grid_spec=pltpu.PrefetchScalarGridSpec(num_scalar_prefetch=1, grid=(4,), in_specs=mine + other, out_specs=other),
        out_shape=[jax.ShapeDtypeStruct((4,) + g.shape[1:], g.dtype) for g in gs],
        compiler_params=_params(("arbitrary",)),
    )(cidx, *gs, *sibs)


HBM_SPEC = pl.BlockSpec(memory_space=pltpu.HBM)
SEM_SPEC = pl.BlockSpec(memory_space=pltpu.SEMAPHORE)
DATAFLOW = pltpu.SideEffectType.DATAFLOW_SIDE_EFFECTING


def _in_hbm(a):
    return pltpu.with_memory_space_constraint(a, pltpu.HBM)


def _rs_step1_copies(sums, lands, send_sems, recv_sems):
    n = len(sums)
    direct, relay = lands[:n], lands[n:]
    x, y, c = _coords()
    xn, yn = (1 - x, y, c), (x, 1 - y, c)
    qx, qy, qd = 2 * (1 - x) + y, 2 * x + (1 - y), 2 * (1 - x) + (1 - y)
    cps = []
    for k in range(n):
        h = sums[k].shape[1] // 2
        a, b = pl.ds(0, h), pl.ds(h, h)
        moves = ((sums[k].at[qx, a], direct[k].at[0], xn), (sums[k].at[qy, b], direct[k].at[1], yn),
                 (sums[k].at[qd, a], relay[k].at[0], xn), (sums[k].at[qd, b], relay[k].at[1], yn))
        for s, (src, dst, to) in enumerate(moves):
            cps.append(pltpu.make_async_remote_copy(
                src_ref=src, dst_ref=dst, send_sem=send_sems.at[4 * k + s], recv_sem=recv_sems.at[4 * k + s],
                device_id=to, device_id_type=MESH))
    return cps


def _rs_step2_copies(relayed, lands, send_sems, recv_sems, k0=0):
    x, y, c = _coords()
    cps = []
    for k in range(len(relayed)):
        for s, to in enumerate(((1 - x, y, c), (x, 1 - y, c))):
            cps.append(pltpu.make_async_remote_copy(
                src_ref=relayed[k].at[s], dst_ref=lands[k].at[s], send_sem=send_sems.at[2 * (k0 + k) + s],
                recv_sem=recv_sems.at[2 * (k0 + k) + s], device_id=to, device_id_type=MESH))
    return cps


def _relay_sum(sums, relay, qxy, *, name):
    n = len(sums)

    def body(q_ref, *refs):
        for k in range(n):
            refs[2 * n + k][...] = (refs[k][...].astype(F32) + refs[n + k][...].astype(F32)).astype(refs[2 * n + k].dtype)

    half = lambda s: (1, s.shape[1] // 2) + s.shape[2:]
    return pl.pallas_call(
        body, name=name,
        grid_spec=pltpu.PrefetchScalarGridSpec(
            num_scalar_prefetch=1, grid=(2,),
            in_specs=[pl.BlockSpec(half(s), lambda t, q_ref: (q_ref[t], 1 - t, 0)) for s in sums]
            + [pl.BlockSpec(half(s), lambda t, q_ref: (1 - t, 0, 0)) for s in sums],
            out_specs=[pl.BlockSpec(half(s), lambda t, q_ref: (t, 0, 0)) for s in sums]),
        out_shape=[jax.ShapeDtypeStruct((2,) + half(s)[1:], s.dtype) for s in sums],
        compiler_params=_params(("arbitrary",)),
    )(qxy, *sums, *relay)


def _split_start(copies, srcs, lands, n_sems, after, *, name):
    ns, nl = len(srcs), len(lands)

    def body(*refs):
        for cp in copies(refs[:ns], refs[ns:ns + nl], refs[ns + nl + len(after)], refs[ns + nl + len(after) + 1]):
            cp.start()
        refs[-1][...] = jnp.zeros_like(refs[-1])

    bufs = [_in_hbm(a) for a in list(srcs) + list(lands)]
    out = pl.pallas_call(
        body, name=name,
        out_shape=(pltpu.SemaphoreType.DMA((n_sems,)), pltpu.SemaphoreType.DMA((n_sems,)),
                   *[pltpu.HBM(a.shape, a.dtype) for a in bufs], jax.ShapeDtypeStruct((8, 128), F32)),
        in_specs=[HBM_SPEC] * len(bufs) + [pl.BlockSpec(memory_space=pl.ANY)] * len(after),
        out_specs=(SEM_SPEC, SEM_SPEC, *[HBM_SPEC] * len(bufs), pl.BlockSpec(memory_space=pltpu.VMEM)),
        input_output_aliases={i: 2 + i for i in range(len(bufs))},
        compiler_params=pltpu.CompilerParams(has_side_effects=DATAFLOW),
    )(*bufs, *after)
    return out[0], out[1], list(out[2:2 + ns]), list(out[2 + ns:2 + ns + nl]), out[-1]


def _split_wait(copies, send_sems, recv_sems, srcs, lands, after, *, name):
    ns, nl = len(srcs), len(lands)

    def body(*refs):
        for cp in copies(refs[:ns], refs[ns:ns + nl], refs[ns + nl], refs[ns + nl + 1]):
            cp.wait_send()
            cp.wait_recv()

    out = pl.pallas_call(
        body, name=name,
        out_shape=[pltpu.HBM(a.shape, a.dtype) for a in list(srcs) + list(lands)],
        in_specs=[HBM_SPEC] * (ns + nl) + [SEM_SPEC, SEM_SPEC] + [pl.BlockSpec(memory_space=pl.ANY)] * len(after),
        out_specs=[HBM_SPEC] * (ns + nl),
        input_output_aliases={i: i for i in range(ns + nl)},
        compiler_params=pltpu.CompilerParams(has_side_effects=DATAFLOW),
    )(*srcs, *lands, send_sems, recv_sems, *after)
    return list(out[:ns]), list(out[ns:])


ADAM_C1 = 1.0 / (1.0 - ADAM_B1 ** ADAM_STEP)
ADAM_C2 = 1.0 / (1.0 - ADAM_B2 ** ADAM_STEP)


def _adam_math(w, g, m, v):
    m2 = ADAM_B1 * m + (1.0 - ADAM_B1) * g
    v2 = ADAM_B2 * v + (1.0 - ADAM_B2) * (g * g)
    return -ADAM_LR * ((m2 * ADAM_C1) / (jnp.sqrt(v2 * ADAM_C2) + ADAM_EPS) + ADAM_WD * w), m2, v2


def _adamw(w, g, m, v, *, name, after=()):
    R, C = w.shape
    tr = R if R <= 512 else 256

    def body(w_ref, g_ref, m_ref, v_ref, *rest):
        d_ref, nm_ref, nv_ref = rest[len(after):]
        d_ref[...], nm_ref[...], nv_ref[...] = _adam_math(w_ref[...], g_ref[...], m_ref[...], v_ref[...])

    blk = pl.BlockSpec((tr, C), lambda i: (i, 0))
    return pl.pallas_call(
        body, name=name, grid=(R // tr,), in_specs=[blk] * 4 + [pl.BlockSpec(memory_space=pl.ANY)] * len(after),
        out_specs=[blk] * 3, out_shape=[jax.ShapeDtypeStruct((R, C), F32)] * 3,
        compiler_params=_params(("parallel",)),
    )(w, g, m, v, *after)


def _adamw_rs2(w, m, v, cs, direct, second, qidx, *, name):
    r, cc = w.shape
    h = r // 2

    def body(q_ref, w_ref, m_ref, v_ref, c_ref, d1_ref, d2_ref, g_ref, d_ref, nm_ref, nv_ref):
        g = (c_ref[0].astype(F32) + d1_ref[0].astype(F32)) + d2_ref[0].astype(F32)
        g_ref[...] = g
        d_ref[...], nm_ref[...], nv_ref[...] = _adam_math(w_ref[...], g, m_ref[...], v_ref[...])

    blk = pl.BlockSpec((h, cc), lambda i, q_ref: (i, 0))
    return pl.pallas_call(
        body, name=name,
        grid_spec=pltpu.PrefetchScalarGridSpec(
            num_scalar_prefetch=1, grid=(2,),
            in_specs=[blk, blk, blk, pl.BlockSpec((1, h, cc), lambda i, q_ref: (q_ref[0], i, 0)),
                      pl.BlockSpec((1, h, cc), lambda i, q_ref: (i, 0, 0)),
                      pl.BlockSpec((1, h, cc), lambda i, q_ref: (1 - i, 0, 0))],
            out_specs=[blk] * 4),
        out_shape=[jax.ShapeDtypeStruct((r, cc), F32)] * 4,
        compiler_params=_params(("arbitrary",)),
    )(qidx, w, m, v, cs, direct, second)


def _adamw_rs(wmv, cs, rcv, qidx, *, name):
    n = len(wmv)
    r, cc = wmv[0][0].shape
    n_rcv = rcv[0].shape[0]
    tr = r // 2 if r % 32 == 0 and r > 128 else r

    def body(q_ref, *refs):
        ins, outs = refs[:5 * n], refs[5 * n:]
        for k in range(n):
            w_ref, m_ref, v_ref, c_ref, r_ref = ins[5 * k:5 * k + 5]
            g_ref, d_ref, nm_ref, nv_ref = outs[4 * k:4 * k + 4]
            g = c_ref[0].astype(F32)
            for j in range(n_rcv):
                g = g + r_ref[j].astype(F32)
            g_ref[...] = g
            d_ref[...], nm_ref[...], nv_ref[...] = _adam_math(w_ref[...], g, m_ref[...], v_ref[...])

    blk = pl.BlockSpec((tr, cc), lambda i, q_ref: (i, 0))
    one = [blk, blk, blk, pl.BlockSpec((1, tr, cc), lambda i, q_ref: (q_ref[0], i, 0)),
           pl.BlockSpec((n_rcv, tr, cc), lambda i, q_ref: (0, i, 0))]
    out = pl.pallas_call(
        body, name=name,
        grid_spec=pltpu.PrefetchScalarGridSpec(num_scalar_prefetch=1, grid=(r // tr,), in_specs=one * n,
                                               out_specs=[blk] * (4 * n)),
        out_shape=[jax.ShapeDtypeStruct((r, cc), F32)] * (4 * n),
        compiler_params=_params(("arbitrary",)),
    )(qidx, *[a for (w, m, v), c, rc in zip(wmv, cs, rcv) for a in (w, m, v, c, rc)])
    return [tuple(out[4 * k:4 * k + 4]) for k in range(n)]


SMALL_PARAMS = ("norm_ffn1", "norm_mix", "norm_ffn2", "norm_final", "q_norm", "kv_norm", "sinks", "rel_bias", "b_mod")


def _adamw_small(gvec, wmv):
    widths = [wmv[3 * i].shape[1] for i in range(len(SMALL_PARAMS))]

    def body(*refs):
        g_all = refs[0]
        ins = refs[1:1 + 3 * len(SMALL_PARAMS)]
        outs = refs[1 + 3 * len(SMALL_PARAMS):]
        off = N_MODVEC
        for i, name in enumerate(SMALL_PARAMS):
            g_ref, d_ref, nm_ref, nv_ref = outs[4 * i:4 * i + 4]
            w_ref, m_ref, v_ref = ins[3 * i:3 * i + 3]
            start = 0 if name == "b_mod" else off
            g = g_all[:, start:start + widths[i]]
            g_ref[...] = g
            d_ref[...], nm_ref[...], nv_ref[...] = _adam_math(w_ref[...], g, m_ref[...], v_ref[...])
            if name != "b_mod":
                off += dict(SMALL_LAYOUT)[name]

    vm = pl.BlockSpec(memory_space=pltpu.VMEM)
    n_out = 4 * len(SMALL_PARAMS)
    out = pl.pallas_call(
        body, name="adamw_small", in_specs=[vm] * (1 + len(wmv)), out_specs=[vm] * n_out,
        out_shape=[jax.ShapeDtypeStruct((1, widths[i // 4]), F32) for i in range(n_out)],
        compiler_params=_params(),
    )(gvec, *wmv)
    return {name: out[4 * i:4 * i + 4] for i, name in enumerate(SMALL_PARAMS)}


TRANSPOSED = ("g1T", "u1T", "g3T", "u3T", "w_inT", "w_uqT")


def kernel(x, c, w_mod, b_mod, norm_ffn1, ffn1_gate, ffn1_up, ffn1_down, norm_mix, w_in, q_norm, kv_norm, w_uq, w_ukv, sinks, w_o, norm_ffn2, ffn2_gate, ffn2_up, ffn2_down, rel_bias, norm_final, loss_target, m_w_mod, m_b_mod, m_norm_ffn1, m_ffn1_gate, m_ffn1_up, m_ffn1_down, m_norm_mix, m_w_in, m_q_norm, m_kv_norm, m_w_uq, m_w_ukv, m_sinks, m_w_o, m_norm_ffn2, m_ffn2_gate, m_ffn2_up, m_ffn2_down, m_rel_bias, m_norm_final, v_w_mod, v_b_mod, v_norm_ffn1, v_ffn1_gate, v_ffn1_up, v_ffn1_down, v_norm_mix, v_w_in, v_q_norm, v_kv_norm, v_w_uq, v_w_ukv, v_sinks, v_w_o, v_norm_ffn2, v_ffn2_gate, v_ffn2_up, v_ffn2_down, v_rel_bias, v_norm_final):
    mx, my, mc = _coords()
    cidx = jnp.reshape(mc, (1,)).astype(jnp.int32)
    qidx = jnp.reshape(2 * mx + my, (1,)).astype(jnp.int32)
    WM = w_mod.shape[2]

    c_tile = jnp.pad(c, ((0, 7), (0, 0)))
    b_mod3 = jnp.pad(b_mod.reshape(N_DEV, 1, WM), ((0, 0), (0, 7), (0, 0)))
    mod3, ca = _mod_fwd(c_tile, w_mod[0], b_mod3)
    mod9 = mod3[:, 0, :].reshape(N_MOD, D)

    shards = {"g1T": ffn1_gate[0].T.astype(BF16), "u1T": ffn1_up[0].T.astype(BF16), "d1": ffn1_down[0].astype(BF16),
              "g3T": ffn2_gate[0].T.astype(BF16), "u3T": ffn2_up[0].T.astype(BF16), "d3": ffn2_down[0].astype(BF16),
              "w_inT": w_in[0].T, "w_uqT": w_uq[0].T.astype(BF16), "w_ukv": w_ukv[0].astype(BF16),
              "w_o": w_o[0].astype(BF16)}
    me = 4 * mx + 2 * my + mc
    groups = {"ffn1": ("g1T", "u1T", "d1"), "mixer": ("w_inT", "w_uqT", "w_ukv", "w_o"), "ffn2": ("g3T", "u3T", "d3")}
    arriving = {}

    def as_weights(group, gathered):
        return {k: g if k == "w_ukv" else g.reshape(N_DEV * g.shape[1], g.shape[2])
                for k, g in zip(groups[group], gathered)}

    later = groups["mixer"] + groups["ffn2"]
    place = {"mixer": 0, "ffn2": len(groups["mixer"])}

    def start_gather(token):
        lands = []
        for k in later:
            sh = shards[k] + token[0, 0].astype(shards[k].dtype)
            lands.append(lax.dynamic_update_slice(lax.empty((N_DEV,) + sh.shape, sh.dtype), sh[None], (me, 0, 0)))
        batches = [range(k0, k0 + len(groups[group])) for group, k0 in place.items()]
        send, recv, lands, started = _gather_start(lands, name="gather_start", batches=batches)
        for group, k0 in place.items():
            arriving[group] = (send, recv, lands[k0:k0 + len(groups[group])])
        return started

    def fetch(group, after, vecs):
        if group == "ffn1":
            *gathered, token = _wgather([shards[k] + ca[1, 0].astype(shards[k].dtype) for k in groups["ffn1"]])
            return as_weights("ffn1", gathered), vecs + start_gather(token)[0:1, 0:1]

        def pass_on(group, after):
            send, recv, lands = arriving[group]
            lands, token = _gather_pass(send, recv, lands, after, name="gather_landed_" + group, stage="landed",
                                        k0=place[group])
            lands, token = _gather_pass(send, recv, lands, [token], name="gather_onward_" + group, stage="onward",
                                        k0=place[group])
            arriving[group] = (send, recv, lands)
            return token

        if group == "ffn2_on_its_way":
            return None, vecs + pass_on("ffn2", after)[0:1, 0:1]
        if group == "mixer":
            after = [pass_on("mixer", after)]
        send, recv, lands = arriving[group]
        return as_weights(group, _gather_end(send, recv, lands, after, name="gather_end_" + group,
                                             k0=place[group])), vecs

    norms ={"ffn1": norm_ffn1, "mix": norm_mix, "ffn2": norm_ffn2, "final": norm_final.reshape(1, D)}
    in_flight = {}

    def on_grads(group, g, after, vecs, before_ici=()):
        if group != "ffn1":
            if g is None:
                return vecs
            names = list(g)
            by_dest = [g[k] if k == "w_ukv" else g[k].reshape((N_DEV, g[k].shape[0] // N_DEV) + g[k].shape[1:])
                       for k in names]
            lands = [lax.empty((N_DEV - 1,) + a.shape[1:], a.dtype) for a in by_dest]
            send, recv, by_dest, lands, token = _split_start(_direct_copies, by_dest, lands, 7 * len(names), after,
                                                             name="rs_start_" + group)
            in_flight[group] = (names, send, recv, by_dest, lands, token)
            return vecs + token[0:1, 0:1]
        if g is not None:
            names = list(g)
            by_dest = [g[k] if k == "w_ukv" else g[k].reshape((N_DEV, g[k].shape[0] // N_DEV) + g[k].shape[1:])
                       for k in names]
            lands = [lax.empty((4,) + a.shape[1:], a.dtype) for a in by_dest]
            send, recv, by_dest, lands, token = _split_start(_d2d_copies, by_dest, lands, 4 * len(names), after,
                                                             name="rs_d2d_start_" + group)
            in_flight[group] = (names, send, recv, by_dest, lands)
            return vecs + token[0:1, 0:1]
        names, send, recv, by_dest, lands = in_flight[group]
        by_dest, from_sib = _split_wait(_d2d_copies, send, recv, by_dest, lands, after, name="rs_d2d_wait_" + group)
        sums = _chipsum(by_dest, from_sib, cidx, name="chipsum_" + group)
        halves = lambda: [lax.empty((2, s.shape[1] // 2) + s.shape[2:], s.dtype) for s in sums]
        send, recv, sums, lands, token = _split_start(_rs_step1_copies, sums, halves() + halves(), 4 * len(names),
                                                      list(before_ici), name="rs_ici_start_" + group)
        in_flight[group] = (names, send, recv, sums, lands, token)
        return vecs + token[0:1, 0:1]

    _, grad_x, _, vec = _local_step(
        x[0], loss_target[0], mod9, norms, sinks, rel_bias, q_norm, kv_norm, fetch, on_grads=on_grads)

    vec = vec.reshape(1, 1, N_VEC)
    allvec = lax.dynamic_update_slice(lax.empty((N_DEV, 1, N_VEC), F32), vec, (me, 0, 0))
    vsend, vrecv, _, (allvec,), vec_started = _split_start(_vec_copies, [], [allvec], N_DEV - 1, [], name="vec_start")
    on_grads("ffn1", None, [grad_x], jnp.zeros((1, 1), F32), before_ici=[vec_started])

    owners = {"g1T": ("ffn1_gate", ffn1_gate, m_ffn1_gate, v_ffn1_gate), "u1T": ("ffn1_up", ffn1_up, m_ffn1_up, v_ffn1_up),
              "d1": ("ffn1_down", ffn1_down, m_ffn1_down, v_ffn1_down),
              "g3T": ("ffn2_gate", ffn2_gate, m_ffn2_gate, v_ffn2_gate), "u3T": ("ffn2_up", ffn2_up, m_ffn2_up, v_ffn2_up),
              "d3": ("ffn2_down", ffn2_down, m_ffn2_down, v_ffn2_down),
              "w_inT": ("w_in", w_in, m_w_in, v_w_in), "w_uqT": ("w_uq", w_uq, m_w_uq, v_w_uq),
              "w_ukv": ("w_ukv", w_ukv, m_w_ukv, v_w_ukv), "w_o": ("w_o", w_o, m_w_o, v_w_o)}
    res, done = {}, []

    def finish(group, after, behind_step2=None):
        names, send, recv, sums, lands, _ = in_flight[group]
        there = lambda k, a: a[0].T if k in TRANSPOSED else a[0]
        back = lambda k, a: a.T[None] if k in TRANSPOSED else a[None]
        wmv = [tuple(there(k, a) for a in owners[k][1:]) for k in names]
        if behind_step2 is not None:
            n = len(names)
            sums, lands = _split_wait(_rs_step1_copies, send, recv, sums, lands, after, name="rs_ici_wait_" + group)
            direct, relay = lands[:n], lands[n:]
            qxy = jnp.stack([2 * (1 - mx) + my, 2 * mx + (1 - my)]).astype(jnp.int32)
            relayed = _relay_sum(sums, relay, qxy, name="relay_sum_" + group)
            second = [lax.empty(a.shape, a.dtype) for a in relayed]
            send, recv, relayed, second, token = _split_start(_rs_step2_copies, relayed, second, 2 * n, [],
                                                              name="rs_ici_start2_" + group)
            after = behind_step2(token)
            outs = []
            for i, k in enumerate(names):
                _, (sec,) = _split_wait(functools.partial(_rs_step2_copies, k0=i), send, recv, [relayed[i]],
                                        [second[i]], after, name="rs_ici_wait2_" + k)
                outs.append(_adamw_rs2(*wmv[i], sums[i], direct[i], sec, qidx, name="adamw_" + owners[k][0]))
                after = [outs[-1][3]]
        else:
            own = jnp.reshape(me, (1,)).astype(jnp.int32)
            sums, lands = _split_wait(_direct_copies, send, recv, sums, lands, after, name="rs_wait_" + group)
            if len({w.shape for w, _, _ in wmv}) == 1:
                outs = _adamw_rs(wmv, sums, lands, own, name="adamw_" + group)
            else:
                outs = [_adamw_rs([t], [cs], [rc], own, name="adamw_" + owners[k][0])[0]
                        for k, t, cs, rc in zip(names, wmv, sums, lands)]
        for k, out in zip(names, outs):
            done.append(out[3])
            res[owners[k][0]] = tuple(back(k, a) for a in out)

    ffn1_started = in_flight["ffn1"][5]
    finish("ffn2", [ffn1_started])
    finish("mixer", [ffn1_started])

    _, (allvec,) = _split_wait(_vec_copies, vsend, vrecv, [], [allvec], [ffn1_started], name="vec_wait")
    g_wmod, gvec = _mod_bwd(allvec, ca, jnp.reshape(me, (1,)).astype(jnp.int32))
    loss = gvec[0, N_MODVEC + LOSS_SLOT]
    small_in = {"norm_ffn1": (norm_ffn1, m_norm_ffn1, v_norm_ffn1), "norm_mix": (norm_mix, m_norm_mix, v_norm_mix),
                "norm_ffn2": (norm_ffn2, m_norm_ffn2, v_norm_ffn2), "norm_final": (norm_final, m_norm_final, v_norm_final),
                "q_norm": (q_norm, m_q_norm, v_q_norm), "kv_norm": (kv_norm, m_kv_norm, v_kv_norm),
                "sinks": (sinks, m_sinks, v_sinks), "rel_bias": (rel_bias, m_rel_bias, v_rel_bias),
                "b_mod": (b_mod, m_b_mod, v_b_mod)}
    small_out = _adamw_small(gvec, [a.reshape(1, -1) for k in SMALL_PARAMS for a in small_in[k]])
    for k in SMALL_PARAMS:
        res[k] = tuple(a.reshape(small_in[k][0].shape) for a in small_out[k])

    def update_w_mod(step2_started):
        out = _adamw(w_mod[0], g_wmod, m_w_mod[0], v_w_mod[0], name="adamw_w_mod", after=[step2_started])
        res["w_mod"] = tuple(a[None] for a in (g_wmod,) + tuple(out))
        return [out[2]]

    finish("ffn1", done + [a for k in SMALL_PARAMS for a in res[k]], behind_step2=update_w_mod)

    order = ("w_mod", "b_mod", "norm_ffn1", "ffn1_gate", "ffn1_up", "ffn1_down", "norm_mix", "w_in", "q_norm",
             "kv_norm", "w_uq", "w_ukv", "sinks", "w_o", "norm_ffn2", "ffn2_gate", "ffn2_up", "ffn2_down",
             "rel_bias", "norm_final")
    return (loss, grad_x[None]) + tuple(res[nm][kind] for kind in range(4) for nm in order)
```

```python
import functools
import math

import numpy as np
import jax
import jax.numpy as jnp
from jax import lax
from jax.experimental import pallas as pl
from jax.experimental.pallas import tpu as pltpu

F32 = jnp.float32
BF16 = jnp.bfloat16
MESH = pl.DeviceIdType.MESH

N_DEV = 8
D = 1024
D_FF = 2816
EPS = 1e-6
N_MOD = 9
SWA_HEADS = 8
SWA_DH = 64
WINDOW = 128
MLA_HEADS = 4
MLA_NOPE = 128
MLA_ROPE = 64
MLA_V = 128
MLA_QR = 256
MLA_KVR = 128
ROPE_THETA = 10000.0
NUM_BUCKETS = 32
D_IN = 1216
D_IN_PAD = 1280
SWA_SCALE = SWA_DH ** -0.5
MLA_SCALE = (MLA_NOPE + MLA_ROPE) ** -0.5

ADAM_LR = 0.001
ADAM_B1 = 0.9
ADAM_B2 = 0.999
ADAM_EPS = 1e-08
ADAM_WD = 0.01
ADAM_STEP = 10

V7X_VMEM_LIMIT = 56 * 1024 * 1024
ROW_TILE = 512

NT_DIMS = (((1,), (1,)), ((), ()))
TN_DIMS = (((0,), (0,)), ((), ()))


def _dot(a, b):
    return jnp.dot(a, b, preferred_element_type=F32)


def _dot_nt(a, b):
    return lax.dot_general(a, b, NT_DIMS, preferred_element_type=F32)


def _dot_tn(a, b):
    return lax.dot_general(a, b, TN_DIMS, preferred_element_type=F32)


def _params(sem=None):
    return pltpu.CompilerParams(dimension_semantics=sem, vmem_limit_bytes=V7X_VMEM_LIMIT)


def _rstd(x):
    return lax.rsqrt(jnp.mean(x * x, axis=-1, keepdims=True) + EPS)


def _rms_bwd(dy, xhat, r):
    return r * (dy - xhat * jnp.mean(dy * xhat, axis=-1, keepdims=True))


def _sigmoid(a):
    return 1.0 / (1.0 + jnp.exp(-a))


def _ffn_fwd(x, vecs, wgT, wuT, wd, *, name, tm=256, tf=D_FF):
    S, F = x.shape[0], wd.shape[0]
    tm = min(tm, S)
    ni, nj = S // tm, F // tf

    def body(x_ref, vec_ref, wg_ref, wu_ref, wd_ref, xo_ref, h_ref, a_ref, b_ref, f_ref, acc_ref):
        j = pl.program_id(1)

        @pl.when(j == 0)
        def _():
            xv = x_ref[...]
            hn = xv * _rstd(xv) * vec_ref[0:1, :]
            h_ref[...] = (hn * (1.0 + vec_ref[2:3, :]) + vec_ref[1:2, :]).astype(BF16)

        h = h_ref[...]
        a = _dot_nt(h, wg_ref[...])
        b = _dot_nt(h, wu_ref[...])
        a_ref[...] = a.astype(BF16)
        b_ref[...] = b.astype(BF16)
        part = _dot((a * _sigmoid(a) * b).astype(BF16), wd_ref[...])

        def finish(f):
            f_ref[...] = f
            xo_ref[...] = x_ref[...] + (0.5 * vec_ref[3:4, :]) * f

        if nj == 1:
            finish(part)
        else:
            @pl.when(j == 0)
            def _():
                acc_ref[...] = part

            @pl.when((j > 0) & (j < nj - 1))
            def _():
                acc_ref[...] += part

            @pl.when(j == nj - 1)
            def _():
                finish(acc_ref[...] + part)

    row = pl.BlockSpec((tm, D), lambda i, j: (i, 0))
    wspec = pl.BlockSpec((tf, D), lambda i, j: (j, 0), pipeline_mode=pl.Buffered(1) if nj == 1 else None)
    act = pl.BlockSpec((tm, tf), lambda i, j: (i, j))
    return pl.pallas_call(
        body, name=name, grid=(ni, nj),
        in_specs=[row, pl.BlockSpec((8, D), lambda i, j: (0, 0)), wspec, wspec, wspec],
        out_specs=[row, row, act, act, row],
        out_shape=[jax.ShapeDtypeStruct((S, D), F32), jax.ShapeDtypeStruct((S, D), BF16),
                   jax.ShapeDtypeStruct((S, F), BF16), jax.ShapeDtypeStruct((S, F), BF16),
                   jax.ShapeDtypeStruct((S, D), F32)],
        scratch_shapes=[pltpu.VMEM((tm, D) if nj > 1 else (8, 128), F32)],
        compiler_params=_params(("parallel", "arbitrary")),
    )(x, vecs, wgT, wuT, wd)


def _ffn_bwd_main(h, df, a, b, wgT, wuT, wd, *, name, after=(), tm=2048, tf=256):
    S = h.shape[0]
    tm = min(tm, S)
    ni, nj = S // tm, D_FF // tf

    def body(h_hbm, df_hbm, a_ref, b_ref, wg_ref, wu_ref, wd_ref, *rest):
        gg_ref, gu_ref, gd_ref, dh_hbm, h_v, df_v, dh_v, gg_acc, gu_acc, gd_acc, sem = rest[len(after):]
        j = pl.program_id(0)
        i = pl.program_id(1)

        @pl.when((j == 0) & (i == 0))
        def _():
            c1 = pltpu.make_async_copy(h_hbm, h_v, sem.at[0])
            c2 = pltpu.make_async_copy(df_hbm, df_v, sem.at[1])
            c1.start()
            c2.start()
            c1.wait()
            c2.wait()

        @pl.when(i == 0)
        def _():
            gg_acc[...] = jnp.zeros_like(gg_acc)
            gu_acc[...] = jnp.zeros_like(gu_acc)
            gd_acc[...] = jnp.zeros_like(gd_acc)

        rows = pl.ds(pl.multiple_of(i * tm, tm), tm)
        hi = h_v[rows, :]
        dfi = df_v[rows, :]
        av = a_ref[...].astype(F32)
        bv = b_ref[...].astype(F32)
        sg = _sigmoid(av)
        sa = av * sg
        hsw = (sa * bv).astype(BF16)
        dhsw = _dot_nt(dfi, wd_ref[...])
        da = (dhsw * bv * (sg * (1.0 + av * (1.0 - sg)))).astype(BF16)
        db = (dhsw * sa).astype(BF16)
        gd_acc[...] += _dot_tn(hsw, dfi)
        gg_acc[...] += _dot_tn(da, hi)
        gu_acc[...] += _dot_tn(db, hi)
        dh = _dot(da, wg_ref[...]) + _dot(db, wu_ref[...])

        @pl.when(j == 0)
        def _():
            dh_v[rows, :] = dh

        @pl.when(j > 0)
        def _():
            dh_v[rows, :] += dh

        @pl.when(i == ni - 1)
        def _():
            gg_ref[...] = gg_acc[...].astype(BF16)
            gu_ref[...] = gu_acc[...].astype(BF16)
            gd_ref[...] = gd_acc[...].astype(BF16)

        @pl.when((j == nj - 1) & (i == ni - 1))
        def _():
            c3 = pltpu.make_async_copy(dh_v, dh_hbm, sem.at[2])
            c3.start()
            c3.wait()

    anyspec = pl.BlockSpec(memory_space=pl.ANY)
    wspec = pl.BlockSpec((tf, D), lambda j, i: (j, 0))
    act = pl.BlockSpec((tm, tf), lambda j, i: (i, j))
    return pl.pallas_call(
        body, name=name, grid=(nj, ni),
        in_specs=[anyspec, anyspec, act, act, wspec, wspec, wspec] + [anyspec] * len(after),
        out_specs=[wspec, wspec, wspec, anyspec],
        out_shape=[jax.ShapeDtypeStruct((D_FF, D), BF16)] * 3 + [jax.ShapeDtypeStruct((S, D), F32)],
        scratch_shapes=[pltpu.VMEM((S, D), BF16), pltpu.VMEM((S, D), BF16), pltpu.VMEM((S, D), F32),
                        pltpu.VMEM((tf, D), F32), pltpu.VMEM((tf, D), F32), pltpu.VMEM((tf, D), F32),
                        pltpu.SemaphoreType.DMA((3,))],
        compiler_params=_params(("arbitrary", "arbitrary")),
    )(h, df, a, b, wgT, wuT, wd, *after)


def _ffn_out_bwd(dx, f, gate, df_ref, part_ref):
    df_ref[...] = ((0.5 * gate) * dx).astype(BF16)
    part_ref[3:4, :] += 0.5 * jnp.sum(dx * f, axis=0, keepdims=True)


def _norm_bwd(dh, x, dxo, vecs, *, name, below=None, tm=ROW_TILE):
    S = x.shape[0]
    tm = min(tm, S)

    def body(dh_ref, x_ref, dxo_ref, vec_ref, *rest):
        dx_ref, part_ref = rest[-2 if below is None else -3], rest[-1 if below is None else -2]

        @pl.when(pl.program_id(0) == 0)
        def _():
            part_ref[...] = jnp.zeros_like(part_ref)

        dh = dh_ref[...]
        xv = x_ref[...]
        r = _rstd(xv)
        xhat = xv * r
        w = vec_ref[0:1, :]
        xn = xhat * w
        dxn = dh * (1.0 + vec_ref[2:3, :])
        part_ref[0:1, :] += jnp.sum(dxn * xhat, axis=0, keepdims=True)
        part_ref[1:2, :] += jnp.sum(dh, axis=0, keepdims=True)
        part_ref[2:3, :] += jnp.sum(dh * xn, axis=0, keepdims=True)
        dx = dxo_ref[...] + _rms_bwd(dxn * w, xhat, r)
        dx_ref[...] = dx
        if below is not None:
            _ffn_out_bwd(dx, rest[0][...], rest[1][3:4, :], rest[-1], part_ref)

    row = pl.BlockSpec((tm, D), lambda i: (i, 0))
    vec = pl.BlockSpec((8, D), lambda i: (0, 0))
    extra = [] if below is None else [row, vec]
    return pl.pallas_call(
        body, name=name, grid=(S // tm,), in_specs=[row, row, row, vec] + extra,
        out_specs=[row, vec] + ([] if below is None else [row]),
        out_shape=[jax.ShapeDtypeStruct((S, D), F32), jax.ShapeDtypeStruct((8, D), F32)]
        + ([] if below is None else [jax.ShapeDtypeStruct((S, D), BF16)]),
        compiler_params=_params(("arbitrary",)),
    )(dh, x, dxo, vecs, *([] if below is None else below))


def _head(x, tgt, nf, f, vecs, *, tm=ROW_TILE):
    S = x.shape[0]
    tm = min(tm, S)

    def body(x_ref, t_ref, nf_ref, f_ref, vec_ref, dx_ref, part_ref, df_ref):
        @pl.when(pl.program_id(0) == 0)
        def _():
            part_ref[...] = jnp.zeros_like(part_ref)

        xv = x_ref[...]
        r = _rstd(xv)
        xhat = xv * r
        w = nf_ref[...]
        e = xhat * w - t_ref[...]
        dy = e * (1.0 / D)
        part_ref[0:1, :] += jnp.sum(dy * xhat, axis=0, keepdims=True)
        part_ref[1:2, :] += jnp.sum(e * e) * (0.5 / D)
        dx = _rms_bwd(dy * w, xhat, r)
        dx_ref[...] = dx
        _ffn_out_bwd(dx, f_ref[...], vec_ref[3:4, :], df_ref, part_ref)

    row = pl.BlockSpec((tm, D), lambda i: (i, 0))
    vec = pl.BlockSpec((8, D), lambda i: (0, 0))
    return pl.pallas_call(
        body, name="head", grid=(S // tm,),
        in_specs=[row, row, pl.BlockSpec((1, D), lambda i: (0, 0)), row, vec],
        out_specs=[row, vec, row],
        out_shape=[jax.ShapeDtypeStruct((S, D), F32), jax.ShapeDtypeStruct((8, D), F32),
                   jax.ShapeDtypeStruct((S, D), BF16)],
        compiler_params=_params(("arbitrary",)),
    )(x, tgt, nf, f, vecs)


def _mix_in_fwd(x, vecs, w_inT, *, tm=ROW_TILE):
    S = x.shape[0]
    tm = min(tm, S)

    def body(x_ref, vec_ref, w_ref, h_ref, p_ref):
        xv = x_ref[...]
        hn = xv * _rstd(xv) * vec_ref[0:1, :]
        h = (hn * (1.0 + vec_ref[2:3, :]) + vec_ref[1:2, :]).astype(BF16)
        h_ref[...] = h
        p_ref[...] = _dot_nt(h, w_ref[...])

    row = pl.BlockSpec((tm, D), lambda i: (i, 0))
    return pl.pallas_call(
        body, name="mix_in_fwd", grid=(S // tm,),
        in_specs=[row, pl.BlockSpec((8, D), lambda i: (0, 0)), pl.BlockSpec((D_IN_PAD, D), lambda i: (0, 0))],
        out_specs=[row, pl.BlockSpec((tm, D_IN_PAD), lambda i: (i, 0))],
        out_shape=[jax.ShapeDtypeStruct((S, D), BF16), jax.ShapeDtypeStruct((S, D_IN_PAD), F32)],
        compiler_params=_params(("parallel",)),
    )(x, vecs, w_inT)


def _bucket_table():
    qi = np.arange(WINDOW)[:, None]
    kj = np.arange(2 * WINDOW)[None, :]
    dist = qi + WINDOW - kj
    max_exact = NUM_BUCKETS // 2
    n = np.maximum(dist, 0)
    nf = np.maximum(n, 1).astype(np.float32)
    large = max_exact + (np.log(nf / np.float32(max_exact)) / np.float32(math.log(WINDOW / max_exact))
                         * np.float32(NUM_BUCKETS - max_exact)).astype(np.int32)
    large = np.minimum(large, NUM_BUCKETS - 1)
    return np.where(n < max_exact, n, large).astype(np.int32)


def _bias_build(rel_bias, bucket):
    def body(rb_ref, bk_ref, out_ref):
        bk = bk_ref[...]
        for h in range(SWA_HEADS):
            acc = jnp.zeros((WINDOW, 2 * WINDOW), F32)
            for b in range(NUM_BUCKETS):
                acc = jnp.where(bk == b, rb_ref[b, h], acc)
            out_ref[h] = acc

    return pl.pallas_call(
        body, name="bias_build",
        in_specs=[pl.BlockSpec(memory_space=pltpu.SMEM), pl.BlockSpec(memory_space=pltpu.VMEM)],
        out_specs=pl.BlockSpec(memory_space=pltpu.VMEM),
        out_shape=jax.ShapeDtypeStruct((SWA_HEADS, WINDOW, 2 * WINDOW), F32),
    )(rel_bias, bucket)


SWA_GROUP = 4
GROUP_ROWS = SWA_GROUP * WINDOW


SWA_SUB = 2


def _swa_valid(has_prev):
    row = lax.broadcasted_iota(jnp.int32, (GROUP_ROWS, 2 * WINDOW), 0) % WINDOW
    col = lax.broadcasted_iota(jnp.int32, (GROUP_ROWS, 2 * WINDOW), 1)
    dist = row + WINDOW - col
    return (dist >= 0) & (dist < WINDOW) & ((col >= WINDOW) | has_prev)


def _swa_keys(prev_ref, cur_ref, u):
    cur = cur_ref[...]
    before = prev_ref[...] if u == 0 else cur[WINDOW * (u - 1):WINDOW * u]
    return jnp.concatenate([before, cur[WINDOW * u:WINDOW * (u + 1)]], axis=0).astype(BF16)


def _stack_heads(x, g):
    return jnp.concatenate([x[:, 64 * h:64 * h + 64] for h in range(SWA_GROUP * g, SWA_GROUP * (g + 1))], axis=0)


def _unstack_heads(x4):
    return jnp.concatenate([x4[WINDOW * a:WINDOW * (a + 1)] for a in range(SWA_GROUP)], axis=1)


def _group_sinks(sink_ref, g):
    head = lax.broadcasted_iota(jnp.int32, (GROUP_ROWS, 1), 0) // WINDOW
    out = jnp.full((GROUP_ROWS, 1), sink_ref[0, SWA_GROUP * g], F32)
    for a in range(1, SWA_GROUP):
        out = jnp.where(head == a, sink_ref[0, SWA_GROUP * g + a], out)
    return out


def _swa_probs(qh, kk, bias_h, sink, valid):
    s = _dot_nt(qh, kk) * SWA_SCALE + bias_h
    s = jnp.where(valid, s, -jnp.inf)
    m = jnp.maximum(jnp.max(s, axis=-1, keepdims=True), sink)
    p = jnp.exp(s - m)
    ps = jnp.exp(sink - m)
    inv = 1.0 / (jnp.sum(p, axis=-1, keepdims=True) + ps)
    return p * inv, ps * inv


SWA_ROWS = SWA_SUB * WINDOW


def _swa_specs():
    prev = lambda n: jnp.maximum(SWA_SUB * n - 1, 0)
    return [pl.BlockSpec((SWA_ROWS, 512), lambda n: (n, 0)),
            pl.BlockSpec((SWA_ROWS, 128), lambda n: (n, 4)),
            pl.BlockSpec((WINDOW, 128), lambda n: (prev(n), 4)),
            pl.BlockSpec((SWA_ROWS, 128), lambda n: (n, 5)),
            pl.BlockSpec((WINDOW, 128), lambda n: (prev(n), 5)),
            pl.BlockSpec((SWA_HEADS, WINDOW, 2 * WINDOW), lambda n: (0, 0, 0)),
            pl.BlockSpec(memory_space=pltpu.SMEM)]


def _swa_fwd(proj, bias, sinks):
    S = proj.shape[0]

    def body(q_ref, kc_ref, kp_ref, vc_ref, vp_ref, bias_ref, sink_ref, o_ref):
        n = pl.program_id(0)
        for u in range(SWA_SUB):
            rows = slice(WINDOW * u, WINDOW * (u + 1))
            valid = _swa_valid(n > 0 if u == 0 else True)
            q = q_ref[rows, :].astype(BF16)
            kfull = _swa_keys(kp_ref, kc_ref, u)
            vfull = _swa_keys(vp_ref, vc_ref, u)
            for g in range(SWA_HEADS // SWA_GROUP):
                kk = kfull[:, 64 * g:64 * g + 64]
                vv = vfull[:, 64 * g:64 * g + 64]
                bias4 = bias_ref[SWA_GROUP * g:SWA_GROUP * (g + 1)].reshape(GROUP_ROWS, 2 * WINDOW)
                pk, _ = _swa_probs(_stack_heads(q, g), kk, bias4, _group_sinks(sink_ref, g), valid)
                o_ref[rows, 256 * g:256 * (g + 1)] = _unstack_heads(_dot(pk.astype(BF16), vv))

    return pl.pallas_call(
        body, name="swa_fwd", grid=(S // SWA_ROWS,),
        in_specs=_swa_specs(),
        out_specs=pl.BlockSpec((SWA_ROWS, 512), lambda n: (n, 0)),
        out_shape=jax.ShapeDtypeStruct((S, 512), F32),
        compiler_params=_params(("parallel",)),
    )(proj, proj, proj, proj, proj, bias, sinks)


def _swa_bwd(proj, bias, sinks, o, do, bucket):
    S = proj.shape[0]
    nb = S // SWA_ROWS

    def body(q_ref, kc_ref, kp_ref, vc_ref, vp_ref, bias_ref, sink_ref, o_ref, do_ref, bk_ref,
             dq_ref, dk_ref, dv_ref, drb_ref, dsk_ref, dbias_acc):
        n = pl.program_id(0)

        @pl.when(n == 0)
        def _():
            dk_ref[...] = jnp.zeros_like(dk_ref)
            dv_ref[...] = jnp.zeros_like(dv_ref)
            dsk_ref[...] = jnp.zeros_like(dsk_ref)
            dbias_acc[...] = jnp.zeros_like(dbias_acc)
            drb_ref[...] = jnp.zeros_like(drb_ref)

        for u in range(SWA_SUB):
            rows = slice(WINDOW * u, WINDOW * (u + 1))
            blk = SWA_SUB * n + u
            valid = _swa_valid(n > 0 if u == 0 else True)
            q = q_ref[rows, :].astype(BF16)
            dov = do_ref[rows, :]
            ov = o_ref[rows, :]
            kfull = _swa_keys(kp_ref, kc_ref, u)
            vfull = _swa_keys(vp_ref, vc_ref, u)
            prow = pl.ds(pl.multiple_of(jnp.maximum(blk - 1, 0) * WINDOW, WINDOW), WINDOW)
            crow = pl.ds(pl.multiple_of(blk * WINDOW, WINDOW), WINDOW)
            for g in range(SWA_HEADS // SWA_GROUP):
                heads = slice(SWA_GROUP * g, SWA_GROUP * (g + 1))
                kk = kfull[:, 64 * g:64 * g + 64]
                vv = vfull[:, 64 * g:64 * g + 64]
                q4 = _stack_heads(q, g)
                pk, psink = _swa_probs(q4, kk, bias_ref[heads].reshape(GROUP_ROWS, 2 * WINDOW),
                                       _group_sinks(sink_ref, g), valid)
                pkb = pk.astype(BF16)
                do4 = _stack_heads(dov, g)
                dob = do4.astype(BF16)
                dp = _dot_nt(dob, vv)
                delta = jnp.sum(do4 * _stack_heads(ov, g), axis=-1, keepdims=True)
                ds = pk * (dp - delta)
                dsink = -psink * delta
                for a in range(SWA_GROUP):
                    h = SWA_GROUP * g + a
                    part = jnp.sum(dsink[WINDOW * a:WINDOW * (a + 1)], keepdims=True)
                    dsk_ref[h:h + 1, :] += jnp.broadcast_to(part, (1, 128))
                dbias_acc[heads] += ds.reshape(SWA_GROUP, WINDOW, 2 * WINDOW)
                dsb = (ds * SWA_SCALE).astype(BF16)
                dq_ref[rows, 256 * g:256 * (g + 1)] = _unstack_heads(_dot(dsb, kk))
                dkk = _dot_tn(dsb, q4)
                dvv = _dot_tn(pkb, dob)
                dk_ref[prow, 64 * g:64 * g + 64] += dkk[:WINDOW]
                dk_ref[crow, 64 * g:64 * g + 64] += dkk[WINDOW:]
                dv_ref[prow, 64 * g:64 * g + 64] += dvv[:WINDOW]
                dv_ref[crow, 64 * g:64 * g + 64] += dvv[WINDOW:]

        @pl.when(n == nb - 1)
        def _():
            bk = bk_ref[...]
            for h in range(SWA_HEADS):
                dbh = dbias_acc[h]
                for b in range(NUM_BUCKETS):
                    val = jnp.sum(jnp.where(bk == b, dbh, 0.0), keepdims=True)
                    row = h * NUM_BUCKETS + b
                    drb_ref[row:row + 1, :] = jnp.broadcast_to(val, (1, 128))

    full = lambda shape: pl.BlockSpec(shape, lambda n: tuple(0 for _ in shape))
    return pl.pallas_call(
        body, name="swa_bwd", grid=(nb,),
        in_specs=_swa_specs() + [pl.BlockSpec((SWA_ROWS, 512), lambda n: (n, 0)),
                                 pl.BlockSpec((SWA_ROWS, 512), lambda n: (n, 0)), full((WINDOW, 2 * WINDOW))],
        out_specs=[pl.BlockSpec((SWA_ROWS, 512), lambda n: (n, 0)), full((S, 128)), full((S, 128)),
                   full((NUM_BUCKETS * 8, 128)), full((8, 128))],
        out_shape=[jax.ShapeDtypeStruct((S, 512), F32), jax.ShapeDtypeStruct((S, 128), F32),
                   jax.ShapeDtypeStruct((S, 128), F32), jax.ShapeDtypeStruct((NUM_BUCKETS * 8, 128), F32),
                   jax.ShapeDtypeStruct((8, 128), F32)],
        scratch_shapes=[pltpu.VMEM((SWA_HEADS, WINDOW, 2 * WINDOW), F32)],
        compiler_params=_params(("arbitrary",)),
    )(proj, proj, proj, proj, proj, bias, sinks, o, do, bucket)


def _rope_tables(S):
    inv = np.float32(ROPE_THETA) ** (-np.arange(0, MLA_ROPE, 2, dtype=np.float32) / np.float32(MLA_ROPE))
    ang = np.arange(S, dtype=np.float32)[:, None] * inv[None, :]
    cos, sin = np.cos(ang), np.sin(ang)
    return (jnp.asarray(np.tile(np.concatenate([cos, cos], axis=1), (1, 2))),
            jnp.asarray(np.tile(np.concatenate([-sin, sin], axis=1), (1, 2))))


def _rope_wide(ref):
    t = ref[...]
    return jnp.concatenate([t, t], axis=1)


def _swap_halves(x):
    w = x.shape[-1]
    lane = lax.broadcasted_iota(jnp.int32, x.shape, x.ndim - 1)
    return jnp.where((lane % 64) < 32, pltpu.roll(x, w - 32, x.ndim - 1), pltpu.roll(x, 32, x.ndim - 1))


def _mla_pre_fwd(proj, qn_w, kvn_w, wuqT, wukv, cos, sin, *, tm=ROW_TILE):
    S = proj.shape[0]
    tm = min(tm, S)

    def body(ql_ref, kl_ref, kr_ref, qw_ref, kw_ref, wuq_ref, wukv_ref, cos_ref, sin_ref,
             qc_ref, kc_ref, vv_ref):
        ql = ql_ref[...]
        qn = (ql * _rstd(ql) * qw_ref[...]).astype(BF16)
        q = _dot_nt(qn, wuq_ref[...])
        cs, sn = _rope_wide(cos_ref), _rope_wide(sin_ref)
        qr = q[:, 512:768]
        qr = qr * cs + _swap_halves(qr) * sn
        half = lax.broadcasted_iota(jnp.int32, (tm, 128), 1) // 64
        kl = kl_ref[...]
        kvn = (kl * _rstd(kl) * kw_ref[...]).astype(BF16)
        kr = kr_ref[...]
        kr = kr * cs[:, :128] + _swap_halves(kr) * sn[:, :128]
        kr2 = (kr + pltpu.roll(kr, 64, 1)).astype(BF16)
        for h in range(MLA_HEADS):
            qc_ref[h, :, 0:128] = q[:, 128 * h:128 * h + 128].astype(BF16)
            chunk = qr[:, 128 * (h // 2):128 * (h // 2) + 128]
            qc_ref[h, :, 128:256] = jnp.where(half == (h % 2), chunk, 0.0).astype(BF16)
            kc_ref[h, :, 0:128] = _dot(kvn, wukv_ref[2 * h]).astype(BF16)
            kc_ref[h, :, 128:256] = kr2
            vv_ref[h] = _dot(kvn, wukv_ref[2 * h + 1]).astype(BF16)

    const = lambda shape: pl.BlockSpec(shape, lambda i: tuple(0 for _ in shape))
    return pl.pallas_call(
        body, name="mla_pre_fwd", grid=(S // tm,),
        in_specs=[pl.BlockSpec((tm, 256), lambda i: (i, 3)), pl.BlockSpec((tm, 128), lambda i: (i, 8)),
                  pl.BlockSpec((tm, 128), lambda i: (i, 9)), const((1, 256)), const((1, 128)),
                  const((768, 256)), const((8, 128, 128)),
                  pl.BlockSpec((tm, 128), lambda i: (i, 0)), pl.BlockSpec((tm, 128), lambda i: (i, 0))],
        out_specs=[pl.BlockSpec((MLA_HEADS, tm, 256), lambda i: (0, i, 0)),
                   pl.BlockSpec((MLA_HEADS, tm, 256), lambda i: (0, i, 0)),
                   pl.BlockSpec((MLA_HEADS, tm, 128), lambda i: (0, i, 0))],
        out_shape=[jax.ShapeDtypeStruct((MLA_HEADS, S, 256), BF16), jax.ShapeDtypeStruct((MLA_HEADS, S, 256), BF16),
                   jax.ShapeDtypeStruct((MLA_HEADS, S, 128), BF16)],
        compiler_params=_params(("parallel",)),
    )(proj, proj, proj, qn_w, kvn_w, wuqT, wukv, cos, sin)


def _causal(i, j, t):
    row = i * t + lax.broadcasted_iota(jnp.int32, (t, t), 0)
    col = j * t + lax.broadcasted_iota(jnp.int32, (t, t), 1)
    return col <= row


def _mla_attn_fwd(qc, kc, vv, *, t=512):
    S = qc.shape[1]
    t = min(t, S)

    def body(q_ref, k_ref, v_ref, o_ref, l_ref):
        i = pl.program_id(0)
        diag = _causal(0, 0, t)

        def step(j, carry, masked):
            rows = pl.ds(pl.multiple_of(j * t, t), t)
            out = []
            for h in range(MLA_HEADS):
                m, l, acc = carry[h]
                s = _dot_nt(q_ref[h], k_ref[h, rows, :]) * MLA_SCALE
                if masked:
                    s = jnp.where(diag, s, -jnp.inf)
                m_new = jnp.maximum(m, jnp.max(s, axis=-1, keepdims=True))
                alpha = jnp.exp(m - m_new)
                p = jnp.exp(s - m_new)
                l = alpha * l + jnp.sum(p, axis=-1, keepdims=True)
                acc = alpha * acc + _dot(p.astype(BF16), v_ref[h, rows, :])
                out.append((m_new, l, acc))
            return tuple(out)

        init = tuple((jnp.full((t, 1), -jnp.inf, F32), jnp.zeros((t, 1), F32), jnp.zeros((t, MLA_V), F32))
                     for _ in range(MLA_HEADS))
        carry = lax.fori_loop(0, i, lambda j, c: step(j, c, False), init)
        carry = step(i, carry, True)
        for h in range(MLA_HEADS):
            m, l, acc = carry[h]
            o_ref[:, 128 * h:128 * h + 128] = acc / l
            l_ref[h] = jnp.broadcast_to(m + jnp.log(l), (t, 128))

    return pl.pallas_call(
        body, name="mla_attn_fwd", grid=(S // t,),
        in_specs=[pl.BlockSpec((MLA_HEADS, t, 256), lambda i: (0, i, 0)),
                  pl.BlockSpec((MLA_HEADS, S, 256), lambda i: (0, 0, 0)),
                  pl.BlockSpec((MLA_HEADS, S, 128), lambda i: (0, 0, 0))],
        out_specs=[pl.BlockSpec((t, 512), lambda i: (i, 0)),
                   pl.BlockSpec((MLA_HEADS, t, 128), lambda i: (0, i, 0))],
        out_shape=[jax.ShapeDtypeStruct((S, 512), F32), jax.ShapeDtypeStruct((MLA_HEADS, S, 128), F32)],
        compiler_params=_params(("parallel",)),
    )(qc, kc, vv)


def _mla_attn_bwd(qc, kc, vv, o, lse, do, *, t=512, tq=1024):
    S = qc.shape[1]
    t = min(t, S)
    tq = min(tq, S)
    nblk = S // t
    hp = MLA_HEADS
    once = pl.Buffered(1)

    def body(q_ref, k_ref, v_ref, o_ref, l_ref, do_ref, dq_ref, dk_ref, dv_ref):
        j = pl.program_id(1)

        @pl.when(j == 0)
        def _():
            dq_ref[...] = jnp.zeros_like(dq_ref)

        first = (j * t) // tq

        def step(i, carry, masked):
            rows = pl.ds(pl.multiple_of(i * tq, tq), tq)
            if masked:
                row = i * tq + lax.broadcasted_iota(jnp.int32, (tq, t), 0)
                col = j * t + lax.broadcasted_iota(jnp.int32, (tq, t), 1)
                visible = col <= row
            out = []
            for h in range(hp):
                dk, dv = carry[h]
                k = k_ref[h]
                q = q_ref[h, rows, :]
                dov = do_ref[rows, 128 * h:128 * h + 128]
                lrow = l_ref[h, rows, :][:, 0:1]
                p = jnp.exp(_dot_nt(q, k) * MLA_SCALE - lrow)
                if masked:
                    p = jnp.where(visible, p, 0.0)
                dob = dov.astype(BF16)
                dv = dv + _dot_tn(p.astype(BF16), dob)
                dp = _dot_nt(dob, v_ref[h])
                delta = jnp.sum(dov * o_ref[rows, 128 * h:128 * h + 128], axis=-1, keepdims=True)
                ds = (p * (dp - delta) * MLA_SCALE).astype(BF16)
                dk = dk + _dot_tn(ds, q)
                dq_ref[h, rows, :] += _dot(ds, k)
                out.append((dk, dv))
            return tuple(out)

        init = tuple((jnp.zeros((t, 256), F32), jnp.zeros((t, MLA_V), F32)) for _ in range(hp))
        carry = step(first, init, True)
        carry = lax.fori_loop(first + 1, S // tq, lambda i, c: step(i, c, False), carry)
        for h in range(hp):
            dk_ref[h] = carry[h][0]
            dv_ref[h] = carry[h][1]

    return pl.pallas_call(
        body, name="mla_attn_bwd", grid=(MLA_HEADS // hp, nblk),
        in_specs=[pl.BlockSpec((hp, S, 256), lambda g, j: (g, 0, 0), pipeline_mode=once),
                  pl.BlockSpec((hp, t, 256), lambda g, j: (g, j, 0)),
                  pl.BlockSpec((hp, t, 128), lambda g, j: (g, j, 0)),
                  pl.BlockSpec((S, 128 * hp), lambda g, j: (0, g), pipeline_mode=once),
                  pl.BlockSpec((hp, S, 128), lambda g, j: (g, 0, 0), pipeline_mode=once),
                  pl.BlockSpec((S, 128 * hp), lambda g, j: (0, g), pipeline_mode=once)],
        out_specs=[pl.BlockSpec((hp, S, 256), lambda g, j: (g, 0, 0)),
                   pl.BlockSpec((hp, t, 256), lambda g, j: (g, j, 0)),
                   pl.BlockSpec((hp, t, 128), lambda g, j: (g, j, 0))],
        out_shape=[jax.ShapeDtypeStruct((MLA_HEADS, S, 256), F32), jax.ShapeDtypeStruct((MLA_HEADS, S, 256), F32),
                   jax.ShapeDtypeStruct((MLA_HEADS, S, 128), F32)],
        compiler_params=_params(("parallel", "arbitrary")),
    )(qc, kc, vv, o, lse, do)


def _mla_pre_bwd(proj, qn_w, kvn_w, wuqT, wukv, cos, sin, dqc, dkc, dvv, *, tm=ROW_TILE):
    S = proj.shape[0]
    tm = min(tm, S)

    def body(ql_ref, kl_ref, qw_ref, kw_ref, wuq_ref, wukv_ref, cos_ref, sin_ref, dqc_ref, dkc_ref, dvv_ref,
             dql_ref, dkl_ref, dkr_ref, gq_ref, gkv_ref, part_ref):
        @pl.when(pl.program_id(0) == 0)
        def _():
            gq_ref[...] = jnp.zeros_like(gq_ref)
            gkv_ref[...] = jnp.zeros_like(gkv_ref)
            part_ref[...] = jnp.zeros_like(part_ref)

        cs, sn = _rope_wide(cos_ref), _rope_wide(sin_ref)
        half = lax.broadcasted_iota(jnp.int32, (tm, 128), 1) // 64
        ql = ql_ref[...]
        rq = _rstd(ql)
        qhat = ql * rq
        qw = qw_ref[...]
        qn = (qhat * qw).astype(BF16)
        chunks = []
        for pair in range(2):
            chunks.append(jnp.where(half == 0, dqc_ref[2 * pair, :, 128:256], dqc_ref[2 * pair + 1, :, 128:256]))
        dqr = jnp.concatenate(chunks, axis=1)
        dqr = dqr * cs + _swap_halves(dqr * sn)
        dq = jnp.concatenate([dqc_ref[h, :, 0:128] for h in range(MLA_HEADS)] + [dqr], axis=1).astype(BF16)
        gq_ref[...] += _dot_tn(dq, qn)
        dqn = _dot(dq, wuq_ref[...])
        part_ref[0:1, :] += jnp.sum(dqn * qhat, axis=0, keepdims=True)
        dql_ref[...] = _rms_bwd(dqn * qw, qhat, rq)
        kl = kl_ref[...]
        rk = _rstd(kl)
        khat = kl * rk
        kw = kw_ref[...]
        kvn = (khat * kw).astype(BF16)
        dkvn = jnp.zeros((tm, MLA_KVR), F32)
        dkr2 = jnp.zeros((tm, 128), F32)
        for h in range(MLA_HEADS):
            dkn = dkc_ref[h, :, 0:128].astype(BF16)
            dvh = dvv_ref[h].astype(BF16)
            gkv_ref[2 * h] += _dot_tn(kvn, dkn)
            gkv_ref[2 * h + 1] += _dot_tn(kvn, dvh)
            dkvn += _dot_nt(dkn, wukv_ref[2 * h]) + _dot_nt(dvh, wukv_ref[2 * h + 1])
            dkr2 += dkc_ref[h, :, 128:256]
        part_ref[1:2, 0:128] += jnp.sum(dkvn * khat, axis=0, keepdims=True)
        dkl_ref[...] = _rms_bwd(dkvn * kw, khat, rk)
        dkr = jnp.where(half == 0, dkr2 + pltpu.roll(dkr2, 64, 1), 0.0)
        dkr_ref[...] = dkr * cs[:, :128] + _swap_halves(dkr * sn[:, :128])

    const = lambda shape: pl.BlockSpec(shape, lambda i: tuple(0 for _ in shape))
    heads = lambda w: pl.BlockSpec((MLA_HEADS, tm, w), lambda i: (0, i, 0))
    return pl.pallas_call(
        body, name="mla_pre_bwd", grid=(S // tm,),
        in_specs=[pl.BlockSpec((tm, 256), lambda i: (i, 3)), pl.BlockSpec((tm, 128), lambda i: (i, 8)),
                  const((1, 256)), const((1, 128)), const((768, 256)), const((8, 128, 128)),
                  pl.BlockSpec((tm, 128), lambda i: (i, 0)), pl.BlockSpec((tm, 128), lambda i: (i, 0)),
                  heads(256), heads(256), heads(128)],
        out_specs=[pl.BlockSpec((tm, 256), lambda i: (i, 0)), pl.BlockSpec((tm, 128), lambda i: (i, 0)),
                   pl.BlockSpec((tm, 128), lambda i: (i, 0)), const((768, 256)), const((8, 128, 128)), const((8, 256))],
        out_shape=[jax.ShapeDtypeStruct((S, 256), F32), jax.ShapeDtypeStruct((S, 128), F32),
                   jax.ShapeDtypeStruct((S, 128), F32), jax.ShapeDtypeStruct((768, 256), F32),
                   jax.ShapeDtypeStruct((8, 128, 128), F32), jax.ShapeDtypeStruct((8, 256), F32)],
        compiler_params=_params(("arbitrary",)),
    )(proj, proj, qn_w, kvn_w, wuqT, wukv, cos, sin, dqc, dkc, dvv)


def _mix_out_fwd(x, oa, ob, w_o, vecs, *, tm=ROW_TILE):
    S = x.shape[0]
    tm = min(tm, S)

    def body(x_ref, oa_ref, ob_ref, w_ref, vec_ref, xo_ref, mo_ref):
        mo = _dot(oa_ref[...].astype(BF16), w_ref[0:512, :]) + _dot(ob_ref[...].astype(BF16), w_ref[512:1024, :])
        mo_ref[...] = mo
        xo_ref[...] = x_ref[...] + vec_ref[3:4, :] * mo

    row = pl.BlockSpec((tm, D), lambda i: (i, 0))
    half = pl.BlockSpec((tm, 512), lambda i: (i, 0))
    return pl.pallas_call(
        body, name="mix_out_fwd", grid=(S // tm,),
        in_specs=[row, half, half, pl.BlockSpec((D, D), lambda i: (0, 0)), pl.BlockSpec((8, D), lambda i: (0, 0))],
        out_specs=[row, row],
        out_shape=[jax.ShapeDtypeStruct((S, D), F32), jax.ShapeDtypeStruct((S, D), F32)],
        compiler_params=_params(("parallel",)),
    )(x, oa, ob, w_o, vecs)


def _mix_out_bwd(dxo, mo, oa, ob, w_o, vecs, *, tm=ROW_TILE):
    S = dxo.shape[0]
    tm = min(tm, S)

    def body(dx_ref, mo_ref, oa_ref, ob_ref, w_ref, vec_ref, doa_ref, dob_ref, gw_ref, part_ref):
        @pl.when(pl.program_id(0) == 0)
        def _():
            gw_ref[...] = jnp.zeros_like(gw_ref)
            part_ref[...] = jnp.zeros_like(part_ref)

        dx = dx_ref[...]
        part_ref[0:1, :] += jnp.sum(dx * mo_ref[...], axis=0, keepdims=True)
        dmo = (vec_ref[3:4, :] * dx).astype(BF16)
        doa_ref[...] = _dot_nt(dmo, w_ref[0:512, :])
        dob_ref[...] = _dot_nt(dmo, w_ref[512:1024, :])
        gw_ref[0:512, :] += _dot_tn(oa_ref[...].astype(BF16), dmo)
        gw_ref[512:1024, :] += _dot_tn(ob_ref[...].astype(BF16), dmo)

    row = pl.BlockSpec((tm, D), lambda i: (i, 0))
    half = pl.BlockSpec((tm, 512), lambda i: (i, 0))
    return pl.pallas_call(
        body, name="mix_out_bwd", grid=(S // tm,),
        in_specs=[row, row, half, half, pl.BlockSpec((D, D), lambda i: (0, 0)), pl.BlockSpec((8, D), lambda i: (0, 0))],
        out_specs=[half, half, pl.BlockSpec((D, D), lambda i: (0, 0)), pl.BlockSpec((8, D), lambda i: (0, 0))],
        out_shape=[jax.ShapeDtypeStruct((S, 512), F32), jax.ShapeDtypeStruct((S, 512), F32),
                   jax.ShapeDtypeStruct((D, D), F32), jax.ShapeDtypeStruct((8, D), F32)],
        compiler_params=_params(("arbitrary",)),
    )(dxo, mo, oa, ob, w_o, vecs)


def _mix_in_bwd(h, w_inT, dq, dk, dv, dql, dkl, dkr, *, tm=ROW_TILE):
    S = h.shape[0]
    tm = min(tm, S)
    offs = (0, 512, 640, 768, 1024, 1152)
    wid = (512, 128, 128, 256, 128, 128)

    def body(h_ref, w_ref, dq_ref, dk_ref, dv_ref, dql_ref, dkl_ref, dkr_ref, dh_ref, gw_ref):
        @pl.when(pl.program_id(0) == 0)
        def _():
            gw_ref[...] = jnp.zeros_like(gw_ref)

        hv = h_ref[...]
        dh = jnp.zeros((tm, D), F32)
        for ref, o, w in zip((dq_ref, dk_ref, dv_ref, dql_ref, dkl_ref, dkr_ref), offs, wid):
            w = min(w, D_IN - o)
            dpart = ref[...][:, :w].astype(BF16)
            dh += _dot(dpart, w_ref[o:o + w, :])
            gw_ref[o:o + w, :] += _dot_tn(dpart, hv)
        dh_ref[...] = dh

    row = pl.BlockSpec((tm, D), lambda i: (i, 0))
    part = lambda w: pl.BlockSpec((tm, w), lambda i: (i, 0))
    return pl.pallas_call(
        body, name="mix_in_bwd", grid=(S // tm,),
        in_specs=[row, pl.BlockSpec((D_IN_PAD, D), lambda i: (0, 0))] + [part(w) for w in wid],
        out_specs=[row, pl.BlockSpec((D_IN, D), lambda i: (0, 0))],
        out_shape=[jax.ShapeDtypeStruct((S, D), F32), jax.ShapeDtypeStruct((D_IN, D), F32)],
        compiler_params=_params(("arbitrary",)),
    )(h, w_inT, dq, dk, dv, dql, dkl, dkr)


def _vecs(norm_w, mod9, k):
    return jnp.concatenate([norm_w.reshape(1, D), mod9[3 * k:3 * k + 3], jnp.zeros((4, D), F32)], axis=0)


def _uq_group_rows(wuqT):
    per = MLA_NOPE + MLA_ROPE
    nope = [wuqT[per * h:per * h + MLA_NOPE] for h in range(MLA_HEADS)]
    rope = [wuqT[per * h + MLA_NOPE:per * (h + 1)] for h in range(MLA_HEADS)]
    return jnp.concatenate(nope + rope, axis=0)


def _uq_ungroup_rows(g):
    parts = []
    for h in range(MLA_HEADS):
        parts += [g[MLA_NOPE * h:MLA_NOPE * (h + 1)], g[512 + MLA_ROPE * h:512 + MLA_ROPE * (h + 1)]]
    return jnp.concatenate(parts, axis=0)


def _local_step(x, tgt, mod9, norms, sinks, rel_bias, q_norm, kv_norm, W, on_grads=None):
    if on_grads is None:
        on_grads = lambda group, grads, after, vecs: vecs
    S = x.shape[0]
    v1 = _vecs(norms["ffn1"], mod9, 0)
    v2 = _vecs(norms["mix"], mod9, 1)
    v3 = _vecs(norms["ffn2"], mod9, 2)
    bucket = jnp.asarray(_bucket_table())
    cos, sin = _rope_tables(S)
    if isinstance(W, dict):
        full, W = W, (lambda group, after, vecs: (full, vecs))

    W1, v1 = W("ffn1", [], v1)
    x1, h1, a1, b1, f1 = _ffn_fwd(x, v1, W1["g1T"], W1["u1T"], W1["d1"], name="ffn1_fwd")
    W2, v2 = W("mixer", [x1], v2)
    w_inT = jnp.pad(W2["w_inT"], ((0, D_IN_PAD - D_IN), (0, 0))).astype(BF16)
    wuqT = _uq_group_rows(W2["w_uqT"])
    h2, proj = _mix_in_fwd(x1, v2, w_inT)
    bias = _bias_build(rel_bias, bucket)
    oa = _swa_fwd(proj, bias, sinks)
    qc, kc, vv = _mla_pre_fwd(proj, q_norm, kv_norm, wuqT, W2["w_ukv"], cos, sin)
    ob, lse = _mla_attn_fwd(qc, kc, vv)
    _, v2o = W("ffn2_on_its_way", [ob], v2)
    x2, mo = _mix_out_fwd(x1, oa, ob, W2["w_o"], v2o)
    W3, v3 = W("ffn2", [x2], v3)
    x3, h3, a3, b3, f3 = _ffn_fwd(x2, v3, W3["g3T"], W3["u3T"], W3["d3"], name="ffn2_fwd")
    dx3, head_part, df3 = _head(x3, tgt, norms["final"], f3, v3)

    gg3, gu3, gd3, dh3 = _ffn_bwd_main(h3, df3, a3, b3, W3["g3T"], W3["u3T"], W3["d3"], name="ffn2_bwd")
    ffn2 = {"g3T": gg3, "u3T": gu3, "d3": gd3}
    v3 = on_grads("ffn2", ffn2, [], v3)
    dx2, n3_part = _norm_bwd(dh3, x2, dx3, v3, name="ffn2_norm_bwd")
    v2 = on_grads("ffn2", None, [dx2], v2)
    doa, dob, g_wo, g2_part = _mix_out_bwd(dx2, mo, oa, ob, W2["w_o"], v2)
    dq, dk, dv, drb, dsk = _swa_bwd(proj, bias, sinks, oa, doa, bucket)
    dqc, dkc, dvv = _mla_attn_bwd(qc, kc, vv, ob, lse, dob)
    dql, dkl, dkr, g_uq, g_ukv, mla_part = _mla_pre_bwd(proj, q_norm, kv_norm, wuqT, W2["w_ukv"], cos, sin, dqc, dkc, dvv)
    dh2, g_win = _mix_in_bwd(h2, w_inT, dq, dk, dv, dql, dkl, dkr)
    mixer = {"w_inT": g_win, "w_uqT": _uq_ungroup_rows(g_uq).astype(BF16),
             "w_ukv": g_ukv.astype(BF16), "w_o": g_wo.astype(BF16)}
    v2 = on_grads("mixer", mixer, [], v2)
    dx1, n2_part, df1 = _norm_bwd(dh2, x1, dx2, v2, name="mix_norm_bwd", below=(f1, v1))
    started = on_grads("mixer", None, [dx1], jnp.zeros((1, 1), F32))
    gg1, gu1, gd1, dh1 = _ffn_bwd_main(h1, df1, a1, b1, W1["g1T"], W1["u1T"], W1["d1"], name="ffn1_bwd",
                                       after=[started])
    ffn1 = {"g1T": gg1, "u1T": gu1, "d1": gd1}
    v1 = on_grads("ffn1", ffn1, [], v1)
    dx0, n1_part = _norm_bwd(dh1, x, dx1, v1, name="ffn1_norm_bwd")

    grads = {**ffn1, **ffn2, **mixer}
    return head_part[1, 0], dx0, grads, _pack_vec(n1_part, n2_part, n3_part, head_part, g2_part, mla_part, dsk, drb)


SMALL_LAYOUT = (("norm_ffn1", 1024), ("norm_mix", 1024), ("norm_ffn2", 1024), ("norm_final", 1024),
                ("q_norm", 256), ("kv_norm", 128), ("sinks", 128), ("rel_bias", 256))
N_SMALL = sum(n for _, n in SMALL_LAYOUT)
LOSS_SLOT = 4 * 1024 + 256 + 128 + SWA_HEADS
N_MODVEC = N_MOD * D
N_VEC = N_MODVEC + N_SMALL


def _pack_vec(n1, n2, n3, head, g2, mla, dsk, drb):
    def body(n1_ref, n2_ref, n3_ref, head_ref, g2_ref, mla_ref, dsk_ref, drb_ref, out_ref):
        rows = [n1_ref[1:2, :], n1_ref[2:3, :], n2_ref[3:4, :], n2_ref[1:2, :], n2_ref[2:3, :], g2_ref[0:1, :],
                n3_ref[1:2, :], n3_ref[2:3, :], head_ref[3:4, :],
                n1_ref[0:1, :], n2_ref[0:1, :], n3_ref[0:1, :], head_ref[0:1, :]]
        for i, row in enumerate(rows):
            out_ref[:, D * i:D * (i + 1)] = row
        off = D * len(rows)
        out_ref[:, off:off + 256] = mla_ref[0:1, :]
        out_ref[:, off + 256:off + 384] = mla_ref[1:2, 0:128]

        def diagonal(block):
            r = lax.broadcasted_iota(jnp.int32, block.shape, 0)
            lane = lax.broadcasted_iota(jnp.int32, block.shape, 1)
            return jnp.sum(jnp.where(r == lane, block, 0.0), axis=0, keepdims=True)

        lane = lax.broadcasted_iota(jnp.int32, (1, 128), 1)
        out_ref[:, off + 384:off + 512] = jnp.where(lane == SWA_HEADS, head_ref[1:2, 0:128], diagonal(dsk_ref[...]))
        out_ref[:, off + 512:off + 640] = diagonal(drb_ref[0:128, :])
        out_ref[:, off + 640:off + 768] = diagonal(drb_ref[128:256, :])

    vm = pl.BlockSpec(memory_space=pltpu.VMEM)
    return pl.pallas_call(body, name="pack_vec", in_specs=[vm] * 8, out_specs=vm,
                          out_shape=jax.ShapeDtypeStruct((1, N_VEC), F32))(n1, n2, n3, head, g2, mla, dsk, drb)


def _coords():
    return lax.axis_index("x"), lax.axis_index("y"), lax.axis_index("c")


def _flip(v, bit):
    return 1 - v if bit else v


def _peer(r):
    x, y, c = _coords()
    return (_flip(x, r & 4), _flip(y, r & 2), _flip(c, r & 1))


def _mod_fwd(c_tile, w_mod, b_mod3):
    W = w_mod.shape[1]

    def body(c_ref, w_ref, b_ref, mod_ref, ca_ref, call_ref, part_ref, send_sems, recv_sems):
        x, y, c = _coords()
        me = 4 * x + 2 * y + c
        call_ref[me] = c_ref[...]
        sends = []
        for r in range(1, N_DEV):
            cp = pltpu.make_async_remote_copy(c_ref, call_ref.at[me], send_sems.at[0, r], recv_sems.at[0, r],
                                              device_id=_peer(r), device_id_type=MESH)
            cp.start()
            sends.append(cp)
        for r in range(1, N_DEV):
            pltpu.make_async_remote_copy(c_ref, call_ref.at[me], send_sems.at[0, r], recv_sems.at[0, r],
                                         device_id=_peer(r), device_id_type=MESH).wait_recv()
        cv = call_ref[...].reshape(8 * N_DEV, D)
        ca = (cv * _sigmoid(cv)).astype(BF16)
        ca_ref[...] = ca
        part_ref[...] = _dot(ca, w_ref[...].astype(BF16)).reshape(N_DEV, 8, W)
        mod_ref[me] = part_ref[me] + b_ref[me]
        for r in range(1, N_DEV):
            cp = pltpu.make_async_remote_copy(part_ref.at[me ^ r], mod_ref.at[me], send_sems.at[1, r],
                                              recv_sems.at[1, r], device_id=_peer(r), device_id_type=MESH)
            cp.start()
            sends.append(cp)
        for r in range(1, N_DEV):
            pltpu.make_async_remote_copy(part_ref.at[me ^ r], mod_ref.at[me], send_sems.at[1, r],
                                         recv_sems.at[1, r], device_id=_peer(r), device_id_type=MESH).wait_recv()
            mod_ref[me ^ r] = mod_ref[me ^ r] + b_ref[me ^ r]
        for cp in sends:
            cp.wait_send()

    vm = pl.BlockSpec(memory_space=pltpu.VMEM)
    return pl.pallas_call(
        body, name="mod_fwd", in_specs=[vm, vm, vm], out_specs=[vm, vm],
        out_shape=[jax.ShapeDtypeStruct((N_DEV, 8, W), F32), jax.ShapeDtypeStruct((8 * N_DEV, D), BF16)],
        scratch_shapes=[pltpu.VMEM((N_DEV, 8, D), F32), pltpu.VMEM((N_DEV, 8, W), F32),
                        pltpu.SemaphoreType.DMA((2, N_DEV)), pltpu.SemaphoreType.DMA((2, N_DEV))],
        compiler_params=_params(),
    )(c_tile, w_mod, b_mod3)


def _mod_bwd(allvec, ca, me_idx):
    W = N_MODVEC // N_DEV

    def body(me_ref, all_ref, cols_ref, ca_ref, gw_ref, sum_ref):
        in_first_row = lax.broadcasted_iota(jnp.int32, (N_DEV, 8, W), 1) == 0
        dm = jnp.where(in_first_row, cols_ref[...], 0.0).reshape(8 * N_DEV, W)
        gw_ref[...] = _dot_tn(ca_ref[...], dm.astype(BF16))
        total = all_ref[0]
        for k in range(1, N_DEV):
            total = total + all_ref[k]
        sum_ref[...] = total

    return pl.pallas_call(
        body, name="mod_bwd",
        grid_spec=pltpu.PrefetchScalarGridSpec(
            num_scalar_prefetch=1, grid=(1,),
            in_specs=[pl.BlockSpec((N_DEV, 1, N_VEC), lambda i, me: (0, 0, 0)),
                      pl.BlockSpec((N_DEV, 1, W), lambda i, me: (0, 0, me[0])),
                      pl.BlockSpec((8 * N_DEV, D), lambda i, me: (0, 0))],
            out_specs=[pl.BlockSpec((D, W), lambda i, me: (0, 0)), pl.BlockSpec((1, N_VEC), lambda i, me: (0, 0))]),
        out_shape=[jax.ShapeDtypeStruct((D, W), F32), jax.ShapeDtypeStruct((1, N_VEC), F32)],
        compiler_params=_params(("arbitrary",)),
    )(me_idx, allvec, allvec, ca)


def _wgather(shards):
    n = len(shards)
    rows = [s.shape[0] for s in shards]

    def body(*refs):
        ins, outs, token = refs[:n], refs[n:2 * n], refs[2 * n]
        send_sems, recv_sems, local_sems = refs[2 * n + 1:]
        token[...] = jnp.zeros_like(token)
        x, y, c = _coords()
        me = 4 * x + 2 * y + c
        sib, xn, yn = (x, y, 1 - c), (1 - x, y, c), (x, 1 - y, c)
        block = lambda px, py, pc: 4 * px + 2 * py + pc

        def part(k, blk, half):
            if half is None:
                return outs[k].at[blk]
            return outs[k].at[blk, pl.ds(half * (rows[k] // 2), rows[k] // 2)]

        def copy(k, slot, blk, to, half=None, src=None):
            ref = part(k, blk, half)
            return pltpu.make_async_remote_copy(
                src_ref=ref if src is None else src, dst_ref=ref, send_sem=send_sems.at[k, slot],
                recv_sem=recv_sems.at[k, slot], device_id=to, device_id_type=MESH)

        local = [pltpu.make_async_copy(ins[k], outs[k].at[me], local_sems.at[k]) for k in range(n)]
        for cp in local:
            cp.start()
        sent = [copy(k, slot, me, to, src=ins[k]) for k in range(n) for slot, to in ((0, sib), (1, xn), (2, yn))]
        for cp in sent:
            cp.start()
        bx, by, bd = block(1 - x, y, c), block(x, 1 - y, c), block(1 - x, 1 - y, c)
        for k in range(n):
            copy(k, 1, bx, sib).wait_recv()
            sent += [copy(k, 4, bx, yn, half=1), copy(k, 5, bx, sib)]
            sent[-2].start()
            sent[-1].start()
        for k in range(n):
            copy(k, 2, by, sib).wait_recv()
            sent += [copy(k, 3, by, xn, half=0), copy(k, 6, by, sib)]
            sent[-2].start()
            sent[-1].start()
        for k in range(n):
            copy(k, 3, bd, sib, half=0).wait_recv()
            copy(k, 4, bd, sib, half=1).wait_recv()
            sent.append(copy(k, 7, bd, sib))
            sent[-1].start()
        for k in range(n):
            copy(k, 0, block(x, y, 1 - c), sib).wait_recv()
            for slot, blk in ((5, block(1 - x, y, 1 - c)), (6, block(x, 1 - y, 1 - c)), (7, block(1 - x, 1 - y, 1 - c))):
                copy(k, slot, blk, sib).wait_recv()
        for cp in sent:
            cp.wait_send()
        for cp in local:
            cp.wait()

    anyspec = pl.BlockSpec(memory_space=pl.ANY)
    return pl.pallas_call(
        body, name="wgather", in_specs=[anyspec] * n,
        out_specs=[anyspec] * n + [pl.BlockSpec(memory_space=pltpu.VMEM)],
        out_shape=[jax.ShapeDtypeStruct((N_DEV,) + s.shape, s.dtype) for s in shards]
        + [jax.ShapeDtypeStruct((8, 128), F32)],
        scratch_shapes=[pltpu.SemaphoreType.DMA((n, 8)), pltpu.SemaphoreType.DMA((n, 8)),
                        pltpu.SemaphoreType.DMA((n,))],
    )(*shards)


class _GatherCopies:
    def __init__(self, lands, send_sems, recv_sems, k0=0, batches=None):
        x, y, c = _coords()
        me = 4 * x + 2 * y + c
        sib = (x, y, 1 - c)
        chips = [(1 - x, y), (x, 1 - y), (1 - x, 1 - y)]

        def copy(k, slot, block, to):
            return pltpu.make_async_remote_copy(
                src_ref=lands[k].at[block], dst_ref=lands[k].at[block],
                send_sem=send_sems.at[7 * (k0 + k) + slot], recv_sem=recv_sems.at[7 * (k0 + k) + slot],
                device_id=to, device_id_type=MESH)

        n = len(lands)
        self.first = [copy(k, 0, me, sib) for k in range(n)]
        for batch in batches or [range(n)]:
            self.first += [copy(k, 1 + j, me, (cx, cy, c)) for j, (cx, cy) in enumerate(chips) for k in batch]
        self.landed = [copy(k, 1 + j, 4 * cx + 2 * cy + c, sib) for j, (cx, cy) in enumerate(chips) for k in range(n)]
        self.passed = [copy(k, 4 + j, 4 * cx + 2 * cy + c, sib) for j, (cx, cy) in enumerate(chips) for k in range(n)]
        self.from_sib = [copy(k, 0, 4 * x + 2 * y + (1 - c), sib) for k in range(n)]
        self.from_sib += [copy(k, 4 + j, 4 * cx + 2 * cy + (1 - c), sib) for j, (cx, cy) in enumerate(chips)
                          for k in range(n)]


def _gather_start(lands, *, name, batches=None):
    n = len(lands)

    def body(*refs):
        for cp in _GatherCopies(refs[:n], refs[n], refs[n + 1], batches=batches).first:
            cp.start()
        refs[-1][...] = jnp.zeros_like(refs[-1])

    out = pl.pallas_call(
        body, name=name,
        out_shape=(pltpu.SemaphoreType.DMA((7 * n,)), pltpu.SemaphoreType.DMA((7 * n,)),
                   *[pltpu.HBM(l.shape, l.dtype) for l in lands], jax.ShapeDtypeStruct((8, 128), F32)),
        in_specs=[HBM_SPEC] * n,
        out_specs=(SEM_SPEC, SEM_SPEC, *[HBM_SPEC] * n, pl.BlockSpec(memory_space=pltpu.VMEM)),
        input_output_aliases={i: 2 + i for i in range(n)},
        compiler_params=pltpu.CompilerParams(has_side_effects=DATAFLOW),
    )(*[_in_hbm(l) for l in lands])
    return out[0], out[1], list(out[2:2 + n]), out[-1]


def _gather_pass(send_sems, recv_sems, lands, after, *, name, stage, k0=0):
    n = len(lands)

    def body(*refs):
        cps = _GatherCopies(refs[:n], refs[n], refs[n + 1], k0)
        if stage == "landed":
            for cp in cps.landed:
                cp.wait_recv()
        else:
            for cp in cps.passed:
                cp.start()
        refs[-1][...] = jnp.zeros_like(refs[-1])

    out = pl.pallas_call(
        body, name=name,
        out_shape=(*[pltpu.HBM(l.shape, l.dtype) for l in lands], jax.ShapeDtypeStruct((8, 128), F32)),
        in_specs=[HBM_SPEC] * n + [SEM_SPEC, SEM_SPEC] + [pl.BlockSpec(memory_space=pl.ANY)] * len(after),
        out_specs=(*[HBM_SPEC] * n, pl.BlockSpec(memory_space=pltpu.VMEM)),
        input_output_aliases={i: i for i in range(n)},
        compiler_params=pltpu.CompilerParams(has_side_effects=DATAFLOW),
    )(*lands, send_sems, recv_sems, *after)
    return list(out[:n]), out[-1]


def _gather_end(send_sems, recv_sems, lands, after, *, name, k0=0):
    n = len(lands)

    def body(*refs):
        cps = _GatherCopies(refs[:n], refs[n], refs[n + 1], k0)
        for cp in cps.from_sib:
            cp.wait_recv()
        for cp in cps.first + cps.passed:
            cp.wait_send()

    out = pl.pallas_call(
        body, name=name,
        out_shape=[pltpu.HBM(l.shape, l.dtype) for l in lands],
        in_specs=[HBM_SPEC] * n + [SEM_SPEC, SEM_SPEC] + [pl.BlockSpec(memory_space=pl.ANY)] * len(after),
        out_specs=[HBM_SPEC] * n,
        input_output_aliases={i: i for i in range(n)},
        compiler_params=pltpu.CompilerParams(has_side_effects=DATAFLOW),
    )(*lands, send_sems, recv_sems, *after)
    return list(out)


def _d2d_copies(grads, lands, send_sems, recv_sems):
    x, y, c = _coords()
    return [pltpu.make_async_remote_copy(
        src_ref=grads[k].at[2 * q + (1 - c)], dst_ref=lands[k].at[q],
        send_sem=send_sems.at[4 * k + q], recv_sem=recv_sems.at[4 * k + q],
        device_id=(x, y, 1 - c), device_id_type=MESH) for k in range(len(grads)) for q in range(4)]


def _direct_copies(grads, lands, send_sems, recv_sems):
    x, y, c = _coords()
    me = 4 * x + 2 * y + c
    return [pltpu.make_async_remote_copy(
        src_ref=grads[k].at[me ^ r], dst_ref=lands[k].at[r - 1],
        send_sem=send_sems.at[7 * k + r - 1], recv_sem=recv_sems.at[7 * k + r - 1],
        device_id=_peer(r), device_id_type=MESH) for k in range(len(grads)) for r in range(1, N_DEV)]


def _vec_copies(srcs, lands, send_sems, recv_sems):
    x, y, c = _coords()
    me = 4 * x + 2 * y + c
    return [pltpu.make_async_remote_copy(
        src_ref=lands[0].at[me], dst_ref=lands[0].at[me], send_sem=send_sems.at[r - 1], recv_sem=recv_sems.at[r - 1],
        device_id=_peer(r), device_id_type=MESH) for r in range(1, N_DEV)]


def _chipsum(gs, sibs, cidx, *, name):
    n = len(gs)

    def body(c_ref, *refs):
        for k in range(n):
            refs[2 * n + k][...] = (refs[k][...].astype(F32) + refs[n + k][...].astype(F32)).astype(refs[2 * n + k].dtype)

    mine = [pl.BlockSpec((1,) + g.shape[1:], lambda q, c_ref: (2 * q + c_ref[0], 0, 0)) for g in gs]
    other = [pl.BlockSpec((1,) + g.shape[1:], lambda q, c_ref: (q, 0, 0)) for g in gs]
    return pl.pallas_call(
        body, name=name,
        grid_spec=pltpu.PrefetchScalarGridSpec(num_scalar_prefetch=1, grid=(4,), in_specs=mine + other, out_specs=other),
        out_shape=[jax.ShapeDtypeStruct((4,) + g.shape[1:], g.dtype) for g in gs],
        compiler_params=_params(("arbitrary",)),
    )(cidx, *gs, *sibs)


HBM_SPEC = pl.BlockSpec(memory_space=pltpu.HBM)
SEM_SPEC = pl.BlockSpec(memory_space=pltpu.SEMAPHORE)
DATAFLOW = pltpu.SideEffectType.DATAFLOW_SIDE_EFFECTING


def _in_hbm(a):
    return pltpu.with_memory_space_constraint(a, pltpu.HBM)


def _rs_step1_copies(sums, lands, send_sems, recv_sems):
    n = len(sums)
    direct, relay = lands[:n], lands[n:]
    x, y, c = _coords()
    xn, yn = (1 - x, y, c), (x, 1 - y, c)
    qx, qy, qd = 2 * (1 - x) + y, 2 * x + (1 - y), 2 * (1 - x) + (1 - y)
    cps = []
    for k in range(n):
        h = sums[k].shape[1] // 2
        a, b = pl.ds(0, h), pl.ds(h, h)
        moves = ((sums[k].at[qx, a], direct[k].at[0], xn), (sums[k].at[qy, b], direct[k].at[1], yn),
                 (sums[k].at[qd, a], relay[k].at[0], xn), (sums[k].at[qd, b], relay[k].at[1], yn))
        for s, (src, dst, to) in enumerate(moves):
            cps.append(pltpu.make_async_remote_copy(
                src_ref=src, dst_ref=dst, send_sem=send_sems.at[4 * k + s], recv_sem=recv_sems.at[4 * k + s],
                device_id=to, device_id_type=MESH))
    return cps


def _rs_step2_copies(relayed, lands, send_sems, recv_sems, k0=0):
    x, y, c = _coords()
    cps = []
    for k in range(len(relayed)):
        for s, to in enumerate(((1 - x, y, c), (x, 1 - y, c))):
            cps.append(pltpu.make_async_remote_copy(
                src_ref=relayed[k].at[s], dst_ref=lands[k].at[s], send_sem=send_sems.at[2 * (k0 + k) + s],
                recv_sem=recv_sems.at[2 * (k0 + k) + s], device_id=to, device_id_type=MESH))
    return cps


def _relay_sum(sums, relay, qxy, *, name):
    n = len(sums)

    def body(q_ref, *refs):
        for k in range(n):
            refs[2 * n + k][...] = (refs[k][...].astype(F32) + refs[n + k][...].astype(F32)).astype(refs[2 * n + k].dtype)

    half = lambda s: (1, s.shape[1] // 2) + s.shape[2:]
    return pl.pallas_call(
        body, name=name,
        grid_spec=pltpu.PrefetchScalarGridSpec(
            num_scalar_prefetch=1, grid=(2,),
            in_specs=[pl.BlockSpec(half(s), lambda t, q_ref: (q_ref[t], 1 - t, 0)) for s in sums]
            + [pl.BlockSpec(half(s), lambda t, q_ref: (1 - t, 0, 0)) for s in sums],
            out_specs=[pl.BlockSpec(half(s), lambda t, q_ref: (t, 0, 0)) for s in sums]),
        out_shape=[jax.ShapeDtypeStruct((2,) + half(s)[1:], s.dtype) for s in sums],
        compiler_params=_params(("arbitrary",)),
    )(qxy, *sums, *relay)


def _split_start(copies, srcs, lands, n_sems, after, *, name):
    ns, nl = len(srcs), len(lands)

    def body(*refs):
        for cp in copies(refs[:ns], refs[ns:ns + nl], refs[ns + nl + len(after)], refs[ns + nl + len(after) + 1]):
            cp.start()
        refs[-1][...] = jnp.zeros_like(refs[-1])

    bufs = [_in_hbm(a) for a in list(srcs) + list(lands)]
    out = pl.pallas_call(
        body, name=name,
        out_shape=(pltpu.SemaphoreType.DMA((n_sems,)), pltpu.SemaphoreType.DMA((n_sems,)),
                   *[pltpu.HBM(a.shape, a.dtype) for a in bufs], jax.ShapeDtypeStruct((8, 128), F32)),
        in_specs=[HBM_SPEC] * len(bufs) + [pl.BlockSpec(memory_space=pl.ANY)] * len(after),
        out_specs=(SEM_SPEC, SEM_SPEC, *[HBM_SPEC] * len(bufs), pl.BlockSpec(memory_space=pltpu.VMEM)),
        input_output_aliases={i: 2 + i for i in range(len(bufs))},
        compiler_params=pltpu.CompilerParams(has_side_effects=DATAFLOW),
    )(*bufs, *after)
    return out[0], out[1], list(out[2:2 + ns]), list(out[2 + ns:2 + ns + nl]), out[-1]


def _split_wait(copies, send_sems, recv_sems, srcs, lands, after, *, name):
    ns, nl = len(srcs), len(lands)

    def body(*refs):
        for cp in copies(refs[:ns], refs[ns:ns + nl], refs[ns + nl], refs[ns + nl + 1]):
            cp.wait_send()
            cp.wait_recv()

    out = pl.pallas_call(
        body, name=name,
        out_shape=[pltpu.HBM(a.shape, a.dtype) for a in list(srcs) + list(lands)],
        in_specs=[HBM_SPEC] * (ns + nl) + [SEM_SPEC, SEM_SPEC] + [pl.BlockSpec(memory_space=pl.ANY)] * len(after),
        out_specs=[HBM_SPEC] * (ns + nl),
        input_output_aliases={i: i for i in range(ns + nl)},
        compiler_params=pltpu.CompilerParams(has_side_effects=DATAFLOW),
    )(*srcs, *lands, send_sems, recv_sems, *after)
    return list(out[:ns]), list(out[ns:])


ADAM_C1 = 1.0 / (1.0 - ADAM_B1 ** ADAM_STEP)
ADAM_C2 = 1.0 / (1.0 - ADAM_B2 ** ADAM_STEP)


def _adam_math(w, g, m, v):
    m2 = ADAM_B1 * m + (1.0 - ADAM_B1) * g
    v2 = ADAM_B2 * v + (1.0 - ADAM_B2) * (g * g)
    return -ADAM_LR * ((m2 * ADAM_C1) / (jnp.sqrt(v2 * ADAM_C2) + ADAM_EPS) + ADAM_WD * w), m2, v2


def _adamw(w, g, m, v, *, name, after=()):
    R, C = w.shape
    tr = R if R <= 512 else 256

    def body(w_ref, g_ref, m_ref, v_ref, *rest):
        d_ref, nm_ref, nv_ref = rest[len(after):]
        d_ref[...], nm_ref[...], nv_ref[...] = _adam_math(w_ref[...], g_ref[...], m_ref[...], v_ref[...])

    blk = pl.BlockSpec((tr, C), lambda i: (i, 0))
    return pl.pallas_call(
        body, name=name, grid=(R // tr,), in_specs=[blk] * 4 + [pl.BlockSpec(memory_space=pl.ANY)] * len(after),
        out_specs=[blk] * 3, out_shape=[jax.ShapeDtypeStruct((R, C), F32)] * 3,
        compiler_params=_params(("parallel",)),
    )(w, g, m, v, *after)


def _adamw_rs2(wmv, cs, direct, second, qidx, *, name):
    n = len(wmv)
    r, cc = wmv[0][0].shape
    h = r // 2

    def body(q_ref, *refs):
        ins, outs = refs[:6 * n], refs[6 * n:]
        for k in range(n):
            w_ref, m_ref, v_ref, c_ref, d1_ref, d2_ref = ins[6 * k:6 * k + 6]
            g_ref, d_ref, nm_ref, nv_ref = outs[4 * k:4 * k + 4]
            g = (c_ref[0].astype(F32) + d1_ref[0].astype(F32)) + d2_ref[0].astype(F32)
            g_ref[...] = g
            d_ref[...], nm_ref[...], nv_ref[...] = _adam_math(w_ref[...], g, m_ref[...], v_ref[...])

    blk = pl.BlockSpec((h, cc), lambda i, q_ref: (i, 0))
    one = [blk, blk, blk, pl.BlockSpec((1, h, cc), lambda i, q_ref: (q_ref[0], i, 0)),
           pl.BlockSpec((1, h, cc), lambda i, q_ref: (i, 0, 0)),
           pl.BlockSpec((1, h, cc), lambda i, q_ref: (1 - i, 0, 0))]
    out = pl.pallas_call(
        body, name=name,
        grid_spec=pltpu.PrefetchScalarGridSpec(num_scalar_prefetch=1, grid=(2,), in_specs=one * n,
                                               out_specs=[blk] * (4 * n)),
        out_shape=[jax.ShapeDtypeStruct((r, cc), F32)] * (4 * n),
        compiler_params=_params(("arbitrary",)),
    )(qidx, *[a for (w, m, v), c, d1, d2 in zip(wmv, cs, direct, second) for a in (w, m, v, c, d1, d2)])
    return [tuple(out[4 * k:4 * k + 4]) for k in range(n)]


def _adamw_rs(wmv, cs, rcv, qidx, *, name):
    n = len(wmv)
    r, cc = wmv[0][0].shape
    n_rcv = rcv[0].shape[0]
    tr = r // 2 if r % 32 == 0 and r > 128 else r

    def body(q_ref, *refs):
        ins, outs = refs[:5 * n], refs[5 * n:]
        for k in range(n):
            w_ref, m_ref, v_ref, c_ref, r_ref = ins[5 * k:5 * k + 5]
            g_ref, d_ref, nm_ref, nv_ref = outs[4 * k:4 * k + 4]
            g = c_ref[0].astype(F32)
            for j in range(n_rcv):
                g = g + r_ref[j].astype(F32)
            g_ref[...] = g
            d_ref[...], nm_ref[...], nv_ref[...] = _adam_math(w_ref[...], g, m_ref[...], v_ref[...])

    blk = pl.BlockSpec((tr, cc), lambda i, q_ref: (i, 0))
    one = [blk, blk, blk, pl.BlockSpec((1, tr, cc), lambda i, q_ref: (q_ref[0], i, 0)),
           pl.BlockSpec((n_rcv, tr, cc), lambda i, q_ref: (0, i, 0))]
    out = pl.pallas_call(
        body, name=name,
        grid_spec=pltpu.PrefetchScalarGridSpec(num_scalar_prefetch=1, grid=(r // tr,), in_specs=one * n,
                                               out_specs=[blk] * (4 * n)),
        out_shape=[jax.ShapeDtypeStruct((r, cc), F32)] * (4 * n),
        compiler_params=_params(("arbitrary",)),
    )(qidx, *[a for (w, m, v), c, rc in zip(wmv, cs, rcv) for a in (w, m, v, c, rc)])
    return [tuple(out[4 * k:4 * k + 4]) for k in range(n)]


SMALL_PARAMS = ("norm_ffn1", "norm_mix", "norm_ffn2", "norm_final", "q_norm", "kv_norm", "sinks", "rel_bias", "b_mod")


def _adamw_small(gvec, wmv):
    widths = [wmv[3 * i].shape[1] for i in range(len(SMALL_PARAMS))]

    def body(*refs):
        g_all = refs[0]
        ins = refs[1:1 + 3 * len(SMALL_PARAMS)]
        outs = refs[1 + 3 * len(SMALL_PARAMS):]
        off = N_MODVEC
        for i, name in enumerate(SMALL_PARAMS):
            g_ref, d_ref, nm_ref, nv_ref = outs[4 * i:4 * i + 4]
            w_ref, m_ref, v_ref = ins[3 * i:3 * i + 3]
            start = 0 if name == "b_mod" else off
            g = g_all[:, start:start + widths[i]]
            g_ref[...] = g
            d_ref[...], nm_ref[...], nv_ref[...] = _adam_math(w_ref[...], g, m_ref[...], v_ref[...])
            if name != "b_mod":
                off += dict(SMALL_LAYOUT)[name]

    vm = pl.BlockSpec(memory_space=pltpu.VMEM)
    n_out = 4 * len(SMALL_PARAMS)
    out = pl.pallas_call(
        body, name="adamw_small", in_specs=[vm] * (1 + len(wmv)), out_specs=[vm] * n_out,
        out_shape=[jax.ShapeDtypeStruct((1, widths[i // 4]), F32) for i in range(n_out)],
        compiler_params=_params(),
    )(gvec, *wmv)
    return {name: out[4 * i:4 * i + 4] for i, name in enumerate(SMALL_PARAMS)}


TRANSPOSED = ("g1T", "u1T", "g3T", "u3T", "w_inT", "w_uqT")


def kernel(x, c, w_mod, b_mod, norm_ffn1, ffn1_gate, ffn1_up, ffn1_down, norm_mix, w_in, q_norm, kv_norm, w_uq, w_ukv, sinks, w_o, norm_ffn2, ffn2_gate, ffn2_up, ffn2_down, rel_bias, norm_final, loss_target, m_w_mod, m_b_mod, m_norm_ffn1, m_ffn1_gate, m_ffn1_up, m_ffn1_down, m_norm_mix, m_w_in, m_q_norm, m_kv_norm, m_w_uq, m_w_ukv, m_sinks, m_w_o, m_norm_ffn2, m_ffn2_gate, m_ffn2_up, m_ffn2_down, m_rel_bias, m_norm_final, v_w_mod, v_b_mod, v_norm_ffn1, v_ffn1_gate, v_ffn1_up, v_ffn1_down, v_norm_mix, v_w_in, v_q_norm, v_kv_norm, v_w_uq, v_w_ukv, v_sinks, v_w_o, v_norm_ffn2, v_ffn2_gate, v_ffn2_up, v_ffn2_down, v_rel_bias, v_norm_final):
    mx, my, mc = _coords()
    cidx = jnp.reshape(mc, (1,)).astype(jnp.int32)
    qidx = jnp.reshape(2 * mx + my, (1,)).astype(jnp.int32)
    WM = w_mod.shape[2]

    c_tile = jnp.pad(c, ((0, 7), (0, 0)))
    b_mod3 = jnp.pad(b_mod.reshape(N_DEV, 1, WM), ((0, 0), (0, 7), (0, 0)))
    mod3, ca = _mod_fwd(c_tile, w_mod[0], b_mod3)
    mod9 = mod3[:, 0, :].reshape(N_MOD, D)

    shards = {"g1T": ffn1_gate[0].T.astype(BF16), "u1T": ffn1_up[0].T.astype(BF16), "d1": ffn1_down[0].astype(BF16),
              "g3T": ffn2_gate[0].T.astype(BF16), "u3T": ffn2_up[0].T.astype(BF16), "d3": ffn2_down[0].astype(BF16),
              "w_inT": w_in[0].T, "w_uqT": w_uq[0].T.astype(BF16), "w_ukv": w_ukv[0].astype(BF16),
              "w_o": w_o[0].astype(BF16)}
    me = 4 * mx + 2 * my + mc
    groups = {"ffn1": ("g1T", "u1T", "d1"), "mixer": ("w_inT", "w_uqT", "w_ukv", "w_o"), "ffn2": ("g3T", "u3T", "d3")}
    arriving = {}

    def as_weights(group, gathered):
        return {k: g if k == "w_ukv" else g.reshape(N_DEV * g.shape[1], g.shape[2])
                for k, g in zip(groups[group], gathered)}

    later = groups["mixer"] + groups["ffn2"]
    place = {"mixer": 0, "ffn2": len(groups["mixer"])}

    def start_gather(token):
        lands = []
        for k in later:
            sh = shards[k] + token[0, 0].astype(shards[k].dtype)
            lands.append(lax.dynamic_update_slice(lax.empty((N_DEV,) + sh.shape, sh.dtype), sh[None], (me, 0, 0)))
        batches = [range(k0, k0 + len(groups[group])) for group, k0 in place.items()]
        send, recv, lands, started = _gather_start(lands, name="gather_start", batches=batches)
        for group, k0 in place.items():
            arriving[group] = (send, recv, lands[k0:k0 + len(groups[group])])
        return started

    def fetch(group, after, vecs):
        if group == "ffn1":
            *gathered, token = _wgather([shards[k] + ca[1, 0].astype(shards[k].dtype) for k in groups["ffn1"]])
            return as_weights("ffn1", gathered), vecs + start_gather(token)[0:1, 0:1]

        def pass_on(group, after):
            send, recv, lands = arriving[group]
            lands, token = _gather_pass(send, recv, lands, after, name="gather_landed_" + group, stage="landed",
                                        k0=place[group])
            lands, token = _gather_pass(send, recv, lands, [token], name="gather_onward_" + group, stage="onward",
                                        k0=place[group])
            arriving[group] = (send, recv, lands)
            return token

        if group == "ffn2_on_its_way":
            return None, vecs + pass_on("ffn2", after)[0:1, 0:1]
        if group == "mixer":
            after = [pass_on("mixer", after)]
        send, recv, lands = arriving[group]
        return as_weights(group, _gather_end(send, recv, lands, after, name="gather_end_" + group,
                                             k0=place[group])), vecs

    norms ={"ffn1": norm_ffn1, "mix": norm_mix, "ffn2": norm_ffn2, "final": norm_final.reshape(1, D)}
    in_flight = {}

    def on_grads(group, g, after, vecs, before_ici=()):
        if group != "ffn1":
            if g is None:
                return vecs
            names = list(g)
            by_dest = [g[k] if k == "w_ukv" else g[k].reshape((N_DEV, g[k].shape[0] // N_DEV) + g[k].shape[1:])
                       for k in names]
            lands = [lax.empty((N_DEV - 1,) + a.shape[1:], a.dtype) for a in by_dest]
            send, recv, by_dest, lands, token = _split_start(_direct_copies, by_dest, lands, 7 * len(names), after,
                                                             name="rs_start_" + group)
            in_flight[group] = (names, send, recv, by_dest, lands, token)
            return vecs + token[0:1, 0:1]
        if g is not None:
            names = list(g)
            by_dest = [g[k] if k == "w_ukv" else g[k].reshape((N_DEV, g[k].shape[0] // N_DEV) + g[k].shape[1:])
                       for k in names]
            lands = [lax.empty((4,) + a.shape[1:], a.dtype) for a in by_dest]
            send, recv, by_dest, lands, token = _split_start(_d2d_copies, by_dest, lands, 4 * len(names), after,
                                                             name="rs_d2d_start_" + group)
            in_flight[group] = (names, send, recv, by_dest, lands)
            return vecs + token[0:1, 0:1]
        names, send, recv, by_dest, lands = in_flight[group]
        by_dest, from_sib = _split_wait(_d2d_copies, send, recv, by_dest, lands, after, name="rs_d2d_wait_" + group)
        sums = _chipsum(by_dest, from_sib, cidx, name="chipsum_" + group)
        halves = lambda: [lax.empty((2, s.shape[1] // 2) + s.shape[2:], s.dtype) for s in sums]
        send, recv, sums, lands, token = _split_start(_rs_step1_copies, sums, halves() + halves(), 4 * len(names),
                                                      list(before_ici), name="rs_ici_start_" + group)
        in_flight[group] = (names, send, recv, sums, lands, token)
        return vecs + token[0:1, 0:1]

    _, grad_x, _, vec = _local_step(
        x[0], loss_target[0], mod9, norms, sinks, rel_bias, q_norm, kv_norm, fetch, on_grads=on_grads)

    vec = vec.reshape(1, 1, N_VEC)
    allvec = lax.dynamic_update_slice(lax.empty((N_DEV, 1, N_VEC), F32), vec, (me, 0, 0))
    vsend, vrecv, _, (allvec,), vec_started = _split_start(_vec_copies, [], [allvec], N_DEV - 1, [], name="vec_start")
    on_grads("ffn1", None, [grad_x], jnp.zeros((1, 1), F32), before_ici=[vec_started])

    owners = {"g1T": ("ffn1_gate", ffn1_gate, m_ffn1_gate, v_ffn1_gate), "u1T": ("ffn1_up", ffn1_up, m_ffn1_up, v_ffn1_up),
              "d1": ("ffn1_down", ffn1_down, m_ffn1_down, v_ffn1_down),
              "g3T": ("ffn2_gate", ffn2_gate, m_ffn2_gate, v_ffn2_gate), "u3T": ("ffn2_up", ffn2_up, m_ffn2_up, v_ffn2_up),
              "d3": ("ffn2_down", ffn2_down, m_ffn2_down, v_ffn2_down),
              "w_inT": ("w_in", w_in, m_w_in, v_w_in), "w_uqT": ("w_uq", w_uq, m_w_uq, v_w_uq),
              "w_ukv": ("w_ukv", w_ukv, m_w_ukv, v_w_ukv), "w_o": ("w_o", w_o, m_w_o, v_w_o)}
    res, done = {}, []

    def finish(group, after, behind_step2=None):
        names, send, recv, sums, lands, _ = in_flight[group]
        there = lambda k, a: a[0].T if k in TRANSPOSED else a[0]
        back = lambda k, a: a.T[None] if k in TRANSPOSED else a[None]
        wmv = [tuple(there(k, a) for a in owners[k][1:]) for k in names]
        if behind_step2 is not None:
            n = len(names)
            sums, lands = _split_wait(_rs_step1_copies, send, recv, sums, lands, after, name="rs_ici_wait_" + group)
            direct, relay = lands[:n], lands[n:]
            qxy = jnp.stack([2 * (1 - mx) + my, 2 * mx + (1 - my)]).astype(jnp.int32)
            relayed = _relay_sum(sums, relay, qxy, name="relay_sum_" + group)
            second = [lax.empty(a.shape, a.dtype) for a in relayed]
            send, recv, relayed, second, token = _split_start(_rs_step2_copies, relayed, second, 2 * n, [],
                                                              name="rs_ici_start2_" + group)
            _, second = _split_wait(_rs_step2_copies, send, recv, relayed, second, behind_step2(token),
                                    name="rs_ici_wait2_" + group)
            outs = _adamw_rs2(wmv, sums, direct, second, qidx, name="adamw_" + group)
        else:
            own = jnp.reshape(me, (1,)).astype(jnp.int32)
            sums, lands = _split_wait(_direct_copies, send, recv, sums, lands, after, name="rs_wait_" + group)
            if len({w.shape for w, _, _ in wmv}) == 1:
                outs = _adamw_rs(wmv, sums, lands, own, name="adamw_" + group)
            else:
                outs = [_adamw_rs([t], [cs], [rc], own, name="adamw_" + owners[k][0])[0]
                        for k, t, cs, rc in zip(names, wmv, sums, lands)]
        for k, out in zip(names, outs):
            done.append(out[3])
            res[owners[k][0]] = tuple(back(k, a) for a in out)

    ffn1_started = in_flight["ffn1"][5]
    finish("ffn2", [ffn1_started])
    finish("mixer", [ffn1_started])

    _, (allvec,) = _split_wait(_vec_copies, vsend, vrecv, [], [allvec], [ffn1_started], name="vec_wait")
    g_wmod, gvec = _mod_bwd(allvec, ca, jnp.reshape(me, (1,)).astype(jnp.int32))
    loss = gvec[0, N_MODVEC + LOSS_SLOT]
    small_in = {"norm_ffn1": (norm_ffn1, m_norm_ffn1, v_norm_ffn1), "norm_mix": (norm_mix, m_norm_mix, v_norm_mix),
                "norm_ffn2": (norm_ffn2, m_norm_ffn2, v_norm_ffn2), "norm_final": (norm_final, m_norm_final, v_norm_final),
                "q_norm": (q_norm, m_q_norm, v_q_norm), "kv_norm": (kv_norm, m_kv_norm, v_kv_norm),
                "sinks": (sinks, m_sinks, v_sinks), "rel_bias": (rel_bias, m_rel_bias, v_rel_bias),
                "b_mod": (b_mod, m_b_mod, v_b_mod)}
    as_row = lambda k, a: (a.T if k == "rel_bias" else a).reshape(1, -1)
    from_row = lambda k, a: a.reshape(SWA_HEADS, NUM_BUCKETS).T if k == "rel_bias" else a.reshape(small_in[k][0].shape)
    small_out = _adamw_small(gvec, [as_row(k, a) for k in SMALL_PARAMS for a in small_in[k]])
    for k in SMALL_PARAMS:
        res[k] = tuple(from_row(k, a) for a in small_out[k])

    def update_w_mod(step2_started):
        out = _adamw(w_mod[0], g_wmod, m_w_mod[0], v_w_mod[0], name="adamw_w_mod", after=[step2_started])
        res["w_mod"] = tuple(a[None] for a in (g_wmod,) + tuple(out))
        return [out[2]]

    finish("ffn1", done + [a for k in SMALL_PARAMS for a in res[k]], behind_step2=update_w_mod)

    order = ("w_mod", "b_mod", "norm_ffn1", "ffn1_gate", "ffn1_up", "ffn1_down", "norm_mix", "w_in", "q_norm",
             "kv_norm", "w_uq", "w_ukv", "sinks", "w_o", "norm_ffn2", "ffn2_gate", "ffn2_up", "ffn2_down",
             "rel_bias", "norm_final")
    return (loss, grad_x[None]) + tuple(res[nm][kind] for kind in range(4) for nm in order)
```

```python
import functools
import math

import numpy as np
import jax
import jax.numpy as jnp
from jax import lax
from jax.experimental import pallas as pl
from jax.experimental.pallas import tpu as pltpu

F32 = jnp.float32
BF16 = jnp.bfloat16
MESH = pl.DeviceIdType.MESH

N_DEV = 8
D = 1024
D_FF = 2816
EPS = 1e-6
N_MOD = 9
SWA_HEADS = 8
SWA_DH = 64
WINDOW = 128
MLA_HEADS = 4
MLA_NOPE = 128
MLA_ROPE = 64
MLA_V = 128
MLA_QR = 256
MLA_KVR = 128
ROPE_THETA = 10000.0
NUM_BUCKETS = 32
D_IN = 1216
D_IN_PAD = 1280
SWA_SCALE = SWA_DH ** -0.5
MLA_SCALE = (MLA_NOPE + MLA_ROPE) ** -0.5

ADAM_LR = 0.001
ADAM_B1 = 0.9
ADAM_B2 = 0.999
ADAM_EPS = 1e-08
ADAM_WD = 0.01
ADAM_STEP = 10

V7X_VMEM_LIMIT = 56 * 1024 * 1024
ROW_TILE = 512

NT_DIMS = (((1,), (1,)), ((), ()))
TN_DIMS = (((0,), (0,)), ((), ()))


def _dot(a, b):
    return jnp.dot(a, b, preferred_element_type=F32)


def _dot_nt(a, b):
    return lax.dot_general(a, b, NT_DIMS, preferred_element_type=F32)


def _dot_tn(a, b):
    return lax.dot_general(a, b, TN_DIMS, preferred_element_type=F32)


def _params(sem=None):
    return pltpu.CompilerParams(dimension_semantics=sem, vmem_limit_bytes=V7X_VMEM_LIMIT)


def _rstd(x):
    return lax.rsqrt(jnp.mean(x * x, axis=-1, keepdims=True) + EPS)


def _rms_bwd(dy, xhat, r):
    return r * (dy - xhat * jnp.mean(dy * xhat, axis=-1, keepdims=True))


def _sigmoid(a):
    return 1.0 / (1.0 + jnp.exp(-a))


def _ffn_fwd(x, vecs, wgT, wuT, wd, *, name, tm=256, tf=D_FF):
    S, F = x.shape[0], wd.shape[0]
    tm = min(tm, S)
    ni, nj = S // tm, F // tf

    def body(x_ref, vec_ref, wg_ref, wu_ref, wd_ref, xo_ref, h_ref, a_ref, b_ref, f_ref, acc_ref):
        j = pl.program_id(1)

        @pl.when(j == 0)
        def _():
            xv = x_ref[...]
            hn = xv * _rstd(xv) * vec_ref[0:1, :]
            h_ref[...] = (hn * (1.0 + vec_ref[2:3, :]) + vec_ref[1:2, :]).astype(BF16)

        h = h_ref[...]
        a = _dot_nt(h, wg_ref[...])
        b = _dot_nt(h, wu_ref[...])
        a_ref[...] = a.astype(BF16)
        b_ref[...] = b.astype(BF16)
        part = _dot((a * _sigmoid(a) * b).astype(BF16), wd_ref[...])

        def finish(f):
            f_ref[...] = f
            xo_ref[...] = x_ref[...] + (0.5 * vec_ref[3:4, :]) * f

        if nj == 1:
            finish(part)
        else:
            @pl.when(j == 0)
            def _():
                acc_ref[...] = part

            @pl.when((j > 0) & (j < nj - 1))
            def _():
                acc_ref[...] += part

            @pl.when(j == nj - 1)
            def _():
                finish(acc_ref[...] + part)

    row = pl.BlockSpec((tm, D), lambda i, j: (i, 0))
    wspec = pl.BlockSpec((tf, D), lambda i, j: (j, 0), pipeline_mode=pl.Buffered(1) if nj == 1 else None)
    act = pl.BlockSpec((tm, tf), lambda i, j: (i, j))
    return pl.pallas_call(
        body, name=name, grid=(ni, nj),
        in_specs=[row, pl.BlockSpec((8, D), lambda i, j: (0, 0)), wspec, wspec, wspec],
        out_specs=[row, row, act, act, row],
        out_shape=[jax.ShapeDtypeStruct((S, D), F32), jax.ShapeDtypeStruct((S, D), BF16),
                   jax.ShapeDtypeStruct((S, F), BF16), jax.ShapeDtypeStruct((S, F), BF16),
                   jax.ShapeDtypeStruct((S, D), F32)],
        scratch_shapes=[pltpu.VMEM((tm, D) if nj > 1 else (8, 128), F32)],
        compiler_params=_params(("parallel", "arbitrary")),
    )(x, vecs, wgT, wuT, wd)


def _ffn_bwd_main(h, df, a, b, wgT, wuT, wd, *, name, after=(), tm=2048, tf=256):
    S = h.shape[0]
    tm = min(tm, S)
    ni, nj = S // tm, D_FF // tf

    def body(h_hbm, df_hbm, a_ref, b_ref, wg_ref, wu_ref, wd_ref, *rest):
        gg_ref, gu_ref, gd_ref, dh_hbm, h_v, df_v, dh_v, gg_acc, gu_acc, gd_acc, sem = rest[len(after):]
        j = pl.program_id(0)
        i = pl.program_id(1)

        @pl.when((j == 0) & (i == 0))
        def _():
            c1 = pltpu.make_async_copy(h_hbm, h_v, sem.at[0])
            c2 = pltpu.make_async_copy(df_hbm, df_v, sem.at[1])
            c1.start()
            c2.start()
            c1.wait()
            c2.wait()

        @pl.when(i == 0)
        def _():
            gg_acc[...] = jnp.zeros_like(gg_acc)
            gu_acc[...] = jnp.zeros_like(gu_acc)
            gd_acc[...] = jnp.zeros_like(gd_acc)

        rows = pl.ds(pl.multiple_of(i * tm, tm), tm)
        hi = h_v[rows, :]
        dfi = df_v[rows, :]
        av = a_ref[...].astype(F32)
        bv = b_ref[...].astype(F32)
        sg = _sigmoid(av)
        sa = av * sg
        hsw = (sa * bv).astype(BF16)
        dhsw = _dot_nt(dfi, wd_ref[...])
        da = (dhsw * bv * (sg * (1.0 + av * (1.0 - sg)))).astype(BF16)
        db = (dhsw * sa).astype(BF16)
        gd_acc[...] += _dot_tn(hsw, dfi)
        gg_acc[...] += _dot_tn(da, hi)
        gu_acc[...] += _dot_tn(db, hi)
        dh = _dot(da, wg_ref[...]) + _dot(db, wu_ref[...])

        @pl.when(j == 0)
        def _():
            dh_v[rows, :] = dh

        @pl.when(j > 0)
        def _():
            dh_v[rows, :] += dh

        @pl.when(i == ni - 1)
        def _():
            gg_ref[...] = gg_acc[...].astype(BF16)
            gu_ref[...] = gu_acc[...].astype(BF16)
            gd_ref[...] = gd_acc[...].astype(BF16)

        @pl.when((j == nj - 1) & (i == ni - 1))
        def _():
            c3 = pltpu.make_async_copy(dh_v, dh_hbm, sem.at[2])
            c3.start()
            c3.wait()

    anyspec = pl.BlockSpec(memory_space=pl.ANY)
    wspec = pl.BlockSpec((tf, D), lambda j, i: (j, 0))
    act = pl.BlockSpec((tm, tf), lambda j, i: (i, j))
    return pl.pallas_call(
        body, name=name, grid=(nj, ni),
        in_specs=[anyspec, anyspec, act, act, wspec, wspec, wspec] + [anyspec] * len(after),
        out_specs=[wspec, wspec, wspec, anyspec],
        out_shape=[jax.ShapeDtypeStruct((D_FF, D), BF16)] * 3 + [jax.ShapeDtypeStruct((S, D), F32)],
        scratch_shapes=[pltpu.VMEM((S, D), BF16), pltpu.VMEM((S, D), BF16), pltpu.VMEM((S, D), F32),
                        pltpu.VMEM((tf, D), F32), pltpu.VMEM((tf, D), F32), pltpu.VMEM((tf, D), F32),
                        pltpu.SemaphoreType.DMA((3,))],
        compiler_params=_params(("arbitrary", "arbitrary")),
    )(h, df, a, b, wgT, wuT, wd, *after)


def _ffn_out_bwd(dx, f, gate, df_ref, part_ref):
    df_ref[...] = ((0.5 * gate) * dx).astype(BF16)
    part_ref[3:4, :] += 0.5 * jnp.sum(dx * f, axis=0, keepdims=True)


def _norm_bwd(dh, x, dxo, vecs, *, name, below=None, tm=ROW_TILE):
    S = x.shape[0]
    tm = min(tm, S)

    def body(dh_ref, x_ref, dxo_ref, vec_ref, *rest):
        dx_ref, part_ref = rest[-2 if below is None else -3], rest[-1 if below is None else -2]

        @pl.when(pl.program_id(0) == 0)
        def _():
            part_ref[...] = jnp.zeros_like(part_ref)

        dh = dh_ref[...]
        xv = x_ref[...]
        r = _rstd(xv)
        xhat = xv * r
        w = vec_ref[0:1, :]
        xn = xhat * w
        dxn = dh * (1.0 + vec_ref[2:3, :])
        part_ref[0:1, :] += jnp.sum(dxn * xhat, axis=0, keepdims=True)
        part_ref[1:2, :] += jnp.sum(dh, axis=0, keepdims=True)
        part_ref[2:3, :] += jnp.sum(dh * xn, axis=0, keepdims=True)
        dx = dxo_ref[...] + _rms_bwd(dxn * w, xhat, r)
        dx_ref[...] = dx
        if below is not None:
            _ffn_out_bwd(dx, rest[0][...], rest[1][3:4, :], rest[-1], part_ref)

    row = pl.BlockSpec((tm, D), lambda i: (i, 0))
    vec = pl.BlockSpec((8, D), lambda i: (0, 0))
    extra = [] if below is None else [row, vec]
    return pl.pallas_call(
        body, name=name, grid=(S // tm,), in_specs=[row, row, row, vec] + extra,
        out_specs=[row, vec] + ([] if below is None else [row]),
        out_shape=[jax.ShapeDtypeStruct((S, D), F32), jax.ShapeDtypeStruct((8, D), F32)]
        + ([] if below is None else [jax.ShapeDtypeStruct((S, D), BF16)]),
        compiler_params=_params(("arbitrary",)),
    )(dh, x, dxo, vecs, *([] if below is None else below))


def _head(x, tgt, nf, f, vecs, *, tm=ROW_TILE):
    S = x.shape[0]
    tm = min(tm, S)

    def body(x_ref, t_ref, nf_ref, f_ref, vec_ref, dx_ref, part_ref, df_ref):
        @pl.when(pl.program_id(0) == 0)
        def _():
            part_ref[...] = jnp.zeros_like(part_ref)

        xv = x_ref[...]
        r = _rstd(xv)
        xhat = xv * r
        w = nf_ref[...]
        e = xhat * w - t_ref[...]
        dy = e * (1.0 / D)
        part_ref[0:1, :] += jnp.sum(dy * xhat, axis=0, keepdims=True)
        part_ref[1:2, :] += jnp.sum(e * e) * (0.5 / D)
        dx = _rms_bwd(dy * w, xhat, r)
        dx_ref[...] = dx
        _ffn_out_bwd(dx, f_ref[...], vec_ref[3:4, :], df_ref, part_ref)

    row = pl.BlockSpec((tm, D), lambda i: (i, 0))
    vec = pl.BlockSpec((8, D), lambda i: (0, 0))
    return pl.pallas_call(
        body, name="head", grid=(S // tm,),
        in_specs=[row, row, pl.BlockSpec((1, D), lambda i: (0, 0)), row, vec],
        out_specs=[row, vec, row],
        out_shape=[jax.ShapeDtypeStruct((S, D), F32), jax.ShapeDtypeStruct((8, D), F32),
                   jax.ShapeDtypeStruct((S, D), BF16)],
        compiler_params=_params(("arbitrary",)),
    )(x, tgt, nf, f, vecs)


def _mix_in_fwd(x, vecs, w_inT, *, tm=ROW_TILE):
    S = x.shape[0]
    tm = min(tm, S)

    def body(x_ref, vec_ref, w_ref, h_ref, p_ref):
        xv = x_ref[...]
        hn = xv * _rstd(xv) * vec_ref[0:1, :]
        h = (hn * (1.0 + vec_ref[2:3, :]) + vec_ref[1:2, :]).astype(BF16)
        h_ref[...] = h
        p_ref[...] = _dot_nt(h, w_ref[...])

    row = pl.BlockSpec((tm, D), lambda i: (i, 0))
    return pl.pallas_call(
        body, name="mix_in_fwd", grid=(S // tm,),
        in_specs=[row, pl.BlockSpec((8, D), lambda i: (0, 0)), pl.BlockSpec((D_IN_PAD, D), lambda i: (0, 0))],
        out_specs=[row, pl.BlockSpec((tm, D_IN_PAD), lambda i: (i, 0))],
        out_shape=[jax.ShapeDtypeStruct((S, D), BF16), jax.ShapeDtypeStruct((S, D_IN_PAD), F32)],
        compiler_params=_params(("parallel",)),
    )(x, vecs, w_inT)


def _bucket_table():
    qi = np.arange(WINDOW)[:, None]
    kj = np.arange(2 * WINDOW)[None, :]
    dist = qi + WINDOW - kj
    max_exact = NUM_BUCKETS // 2
    n = np.maximum(dist, 0)
    nf = np.maximum(n, 1).astype(np.float32)
    large = max_exact + (np.log(nf / np.float32(max_exact)) / np.float32(math.log(WINDOW / max_exact))
                         * np.float32(NUM_BUCKETS - max_exact)).astype(np.int32)
    large = np.minimum(large, NUM_BUCKETS - 1)
    return np.where(n < max_exact, n, large).astype(np.int32)


def _bias_build(rel_bias, bucket):
    def body(rb_ref, bk_ref, out_ref):
        bk = bk_ref[...]
        for h in range(SWA_HEADS):
            acc = jnp.zeros((WINDOW, 2 * WINDOW), F32)
            for b in range(NUM_BUCKETS):
                acc = jnp.where(bk == b, rb_ref[b, h], acc)
            out_ref[h] = acc

    return pl.pallas_call(
        body, name="bias_build",
        in_specs=[pl.BlockSpec(memory_space=pltpu.SMEM), pl.BlockSpec(memory_space=pltpu.VMEM)],
        out_specs=pl.BlockSpec(memory_space=pltpu.VMEM),
        out_shape=jax.ShapeDtypeStruct((SWA_HEADS, WINDOW, 2 * WINDOW), F32),
    )(rel_bias, bucket)


SWA_GROUP = 4
GROUP_ROWS = SWA_GROUP * WINDOW


SWA_SUB = 2


def _swa_valid(has_prev):
    row = lax.broadcasted_iota(jnp.int32, (GROUP_ROWS, 2 * WINDOW), 0) % WINDOW
    col = lax.broadcasted_iota(jnp.int32, (GROUP_ROWS, 2 * WINDOW), 1)
    dist = row + WINDOW - col
    return (dist >= 0) & (dist < WINDOW) & ((col >= WINDOW) | has_prev)


def _swa_keys(prev_ref, cur_ref, u):
    cur = cur_ref[...]
    before = prev_ref[...] if u == 0 else cur[WINDOW * (u - 1):WINDOW * u]
    return jnp.concatenate([before, cur[WINDOW * u:WINDOW * (u + 1)]], axis=0).astype(BF16)


def _stack_heads(x, g):
    return jnp.concatenate([x[:, 64 * h:64 * h + 64] for h in range(SWA_GROUP * g, SWA_GROUP * (g + 1))], axis=0)


def _unstack_heads(x4):
    return jnp.concatenate([x4[WINDOW * a:WINDOW * (a + 1)] for a in range(SWA_GROUP)], axis=1)


def _group_sinks(sink_ref, g):
    head = lax.broadcasted_iota(jnp.int32, (GROUP_ROWS, 1), 0) // WINDOW
    out = jnp.full((GROUP_ROWS, 1), sink_ref[0, SWA_GROUP * g], F32)
    for a in range(1, SWA_GROUP):
        out = jnp.where(head == a, sink_ref[0, SWA_GROUP * g + a], out)
    return out


def _swa_probs(qh, kk, bias_h, sink, valid):
    s = _dot_nt(qh, kk) * SWA_SCALE + bias_h
    s = jnp.where(valid, s, -jnp.inf)
    m = jnp.maximum(jnp.max(s, axis=-1, keepdims=True), sink)
    p = jnp.exp(s - m)
    ps = jnp.exp(sink - m)
    inv = 1.0 / (jnp.sum(p, axis=-1, keepdims=True) + ps)
    return p * inv, ps * inv


SWA_ROWS = SWA_SUB * WINDOW


def _swa_specs():
    prev = lambda n: jnp.maximum(SWA_SUB * n - 1, 0)
    return [pl.BlockSpec((SWA_ROWS, 512), lambda n: (n, 0)),
            pl.BlockSpec((SWA_ROWS, 128), lambda n: (n, 4)),
            pl.BlockSpec((WINDOW, 128), lambda n: (prev(n), 4)),
            pl.BlockSpec((SWA_ROWS, 128), lambda n: (n, 5)),
            pl.BlockSpec((WINDOW, 128), lambda n: (prev(n), 5)),
            pl.BlockSpec((SWA_HEADS, WINDOW, 2 * WINDOW), lambda n: (0, 0, 0)),
            pl.BlockSpec(memory_space=pltpu.SMEM)]


def _swa_fwd(proj, bias, sinks):
    S = proj.shape[0]

    def body(q_ref, kc_ref, kp_ref, vc_ref, vp_ref, bias_ref, sink_ref, o_ref):
        n = pl.program_id(0)
        for u in range(SWA_SUB):
            rows = slice(WINDOW * u, WINDOW * (u + 1))
            valid = _swa_valid(n > 0 if u == 0 else True)
            q = q_ref[rows, :].astype(BF16)
            kfull = _swa_keys(kp_ref, kc_ref, u)
            vfull = _swa_keys(vp_ref, vc_ref, u)
            for g in range(SWA_HEADS // SWA_GROUP):
                kk = kfull[:, 64 * g:64 * g + 64]
                vv = vfull[:, 64 * g:64 * g + 64]
                bias4 = bias_ref[SWA_GROUP * g:SWA_GROUP * (g + 1)].reshape(GROUP_ROWS, 2 * WINDOW)
                pk, _ = _swa_probs(_stack_heads(q, g), kk, bias4, _group_sinks(sink_ref, g), valid)
                o_ref[rows, 256 * g:256 * (g + 1)] = _unstack_heads(_dot(pk.astype(BF16), vv))

    return pl.pallas_call(
        body, name="swa_fwd", grid=(S // SWA_ROWS,),
        in_specs=_swa_specs(),
        out_specs=pl.BlockSpec((SWA_ROWS, 512), lambda n: (n, 0)),
        out_shape=jax.ShapeDtypeStruct((S, 512), F32),
        compiler_params=_params(("parallel",)),
    )(proj, proj, proj, proj, proj, bias, sinks)


def _swa_bwd(proj, bias, sinks, o, do, bucket):
    S = proj.shape[0]
    nb = S // SWA_ROWS

    def body(q_ref, kc_ref, kp_ref, vc_ref, vp_ref, bias_ref, sink_ref, o_ref, do_ref, bk_ref,
             dq_ref, dk_ref, dv_ref, drb_ref, dsk_ref, dbias_acc):
        n = pl.program_id(0)

        @pl.when(n == 0)
        def _():
            dk_ref[...] = jnp.zeros_like(dk_ref)
            dv_ref[...] = jnp.zeros_like(dv_ref)
            dsk_ref[...] = jnp.zeros_like(dsk_ref)
            dbias_acc[...] = jnp.zeros_like(dbias_acc)
            drb_ref[...] = jnp.zeros_like(drb_ref)

        for u in range(SWA_SUB):
            rows = slice(WINDOW * u, WINDOW * (u + 1))
            blk = SWA_SUB * n + u
            valid = _swa_valid(n > 0 if u == 0 else True)
            q = q_ref[rows, :].astype(BF16)
            dov = do_ref[rows, :]
            ov = o_ref[rows, :]
            kfull = _swa_keys(kp_ref, kc_ref, u)
            vfull = _swa_keys(vp_ref, vc_ref, u)
            prow = pl.ds(pl.multiple_of(jnp.maximum(blk - 1, 0) * WINDOW, WINDOW), WINDOW)
            crow = pl.ds(pl.multiple_of(blk * WINDOW, WINDOW), WINDOW)
            for g in range(SWA_HEADS // SWA_GROUP):
                heads = slice(SWA_GROUP * g, SWA_GROUP * (g + 1))
                kk = kfull[:, 64 * g:64 * g + 64]
                vv = vfull[:, 64 * g:64 * g + 64]
                q4 = _stack_heads(q, g)
                pk, psink = _swa_probs(q4, kk, bias_ref[heads].reshape(GROUP_ROWS, 2 * WINDOW),
                                       _group_sinks(sink_ref, g), valid)
                pkb = pk.astype(BF16)
                do4 = _stack_heads(dov, g)
                dob = do4.astype(BF16)
                dp = _dot_nt(dob, vv)
                delta = jnp.sum(do4 * _stack_heads(ov, g), axis=-1, keepdims=True)
                ds = pk * (dp - delta)
                dsink = -psink * delta
                for a in range(SWA_GROUP):
                    h = SWA_GROUP * g + a
                    part = jnp.sum(dsink[WINDOW * a:WINDOW * (a + 1)], keepdims=True)
                    dsk_ref[h:h + 1, :] += jnp.broadcast_to(part, (1, 128))
                dbias_acc[heads] += ds.reshape(SWA_GROUP, WINDOW, 2 * WINDOW)
                dsb = (ds * SWA_SCALE).astype(BF16)
                dq_ref[rows, 256 * g:256 * (g + 1)] = _unstack_heads(_dot(dsb, kk))
                dkk = _dot_tn(dsb, q4)
                dvv = _dot_tn(pkb, dob)
                dk_ref[prow, 64 * g:64 * g + 64] += dkk[:WINDOW]
                dk_ref[crow, 64 * g:64 * g + 64] += dkk[WINDOW:]
                dv_ref[prow, 64 * g:64 * g + 64] += dvv[:WINDOW]
                dv_ref[crow, 64 * g:64 * g + 64] += dvv[WINDOW:]

        @pl.when(n == nb - 1)
        def _():
            bk = bk_ref[...]
            for h in range(SWA_HEADS):
                dbh = dbias_acc[h]
                for b in range(NUM_BUCKETS):
                    val = jnp.sum(jnp.where(bk == b, dbh, 0.0), keepdims=True)
                    row = h * NUM_BUCKETS + b
                    drb_ref[row:row + 1, :] = jnp.broadcast_to(val, (1, 128))

    full = lambda shape: pl.BlockSpec(shape, lambda n: tuple(0 for _ in shape))
    return pl.pallas_call(
        body, name="swa_bwd", grid=(nb,),
        in_specs=_swa_specs() + [pl.BlockSpec((SWA_ROWS, 512), lambda n: (n, 0)),
                                 pl.BlockSpec((SWA_ROWS, 512), lambda n: (n, 0)), full((WINDOW, 2 * WINDOW))],
        out_specs=[pl.BlockSpec((SWA_ROWS, 512), lambda n: (n, 0)), full((S, 128)), full((S, 128)),
                   full((NUM_BUCKETS * 8, 128)), full((8, 128))],
        out_shape=[jax.ShapeDtypeStruct((S, 512), F32), jax.ShapeDtypeStruct((S, 128), F32),
                   jax.ShapeDtypeStruct((S, 128), F32), jax.ShapeDtypeStruct((NUM_BUCKETS * 8, 128), F32),
                   jax.ShapeDtypeStruct((8, 128), F32)],
        scratch_shapes=[pltpu.VMEM((SWA_HEADS, WINDOW, 2 * WINDOW), F32)],
        compiler_params=_params(("arbitrary",)),
    )(proj, proj, proj, proj, proj, bias, sinks, o, do, bucket)


def _rope_tables(S):
    inv = np.float32(ROPE_THETA) ** (-np.arange(0, MLA_ROPE, 2, dtype=np.float32) / np.float32(MLA_ROPE))
    ang = np.arange(S, dtype=np.float32)[:, None] * inv[None, :]
    cos, sin = np.cos(ang), np.sin(ang)
    return (jnp.asarray(np.tile(np.concatenate([cos, cos], axis=1), (1, 2))),
            jnp.asarray(np.tile(np.concatenate([-sin, sin], axis=1), (1, 2))))


def _rope_wide(ref):
    t = ref[...]
    return jnp.concatenate([t, t], axis=1)


def _swap_halves(x):
    w = x.shape[-1]
    lane = lax.broadcasted_iota(jnp.int32, x.shape, x.ndim - 1)
    return jnp.where((lane % 64) < 32, pltpu.roll(x, w - 32, x.ndim - 1), pltpu.roll(x, 32, x.ndim - 1))


def _mla_pre_fwd(proj, qn_w, kvn_w, wuqT, wukv, cos, sin, *, tm=ROW_TILE):
    S = proj.shape[0]
    tm = min(tm, S)

    def body(ql_ref, kl_ref, kr_ref, qw_ref, kw_ref, wuq_ref, wukv_ref, cos_ref, sin_ref,
             qc_ref, kc_ref, vv_ref):
        ql = ql_ref[...]
        qn = (ql * _rstd(ql) * qw_ref[...]).astype(BF16)
        q = _dot_nt(qn, wuq_ref[...])
        cs, sn = _rope_wide(cos_ref), _rope_wide(sin_ref)
        qr = q[:, 512:768]
        qr = qr * cs + _swap_halves(qr) * sn
        half = lax.broadcasted_iota(jnp.int32, (tm, 128), 1) // 64
        kl = kl_ref[...]
        kvn = (kl * _rstd(kl) * kw_ref[...]).astype(BF16)
        kr = kr_ref[...]
        kr = kr * cs[:, :128] + _swap_halves(kr) * sn[:, :128]
        kr2 = (kr + pltpu.roll(kr, 64, 1)).astype(BF16)
        for h in range(MLA_HEADS):
            qc_ref[h, :, 0:128] = q[:, 128 * h:128 * h + 128].astype(BF16)
            chunk = qr[:, 128 * (h // 2):128 * (h // 2) + 128]
            qc_ref[h, :, 128:256] = jnp.where(half == (h % 2), chunk, 0.0).astype(BF16)
            kc_ref[h, :, 0:128] = _dot(kvn, wukv_ref[2 * h]).astype(BF16)
            kc_ref[h, :, 128:256] = kr2
            vv_ref[h] = _dot(kvn, wukv_ref[2 * h + 1]).astype(BF16)

    const = lambda shape: pl.BlockSpec(shape, lambda i: tuple(0 for _ in shape))
    return pl.pallas_call(
        body, name="mla_pre_fwd", grid=(S // tm,),
        in_specs=[pl.BlockSpec((tm, 256), lambda i: (i, 3)), pl.BlockSpec((tm, 128), lambda i: (i, 8)),
                  pl.BlockSpec((tm, 128), lambda i: (i, 9)), const((1, 256)), const((1, 128)),
                  const((768, 256)), const((8, 128, 128)),
                  pl.BlockSpec((tm, 128), lambda i: (i, 0)), pl.BlockSpec((tm, 128), lambda i: (i, 0))],
        out_specs=[pl.BlockSpec((MLA_HEADS, tm, 256), lambda i: (0, i, 0)),
                   pl.BlockSpec((MLA_HEADS, tm, 256), lambda i: (0, i, 0)),
                   pl.BlockSpec((MLA_HEADS, tm, 128), lambda i: (0, i, 0))],
        out_shape=[jax.ShapeDtypeStruct((MLA_HEADS, S, 256), BF16), jax.ShapeDtypeStruct((MLA_HEADS, S, 256), BF16),
                   jax.ShapeDtypeStruct((MLA_HEADS, S, 128), BF16)],
        compiler_params=_params(("parallel",)),
    )(proj, proj, proj, qn_w, kvn_w, wuqT, wukv, cos, sin)


def _causal(i, j, t):
    row = i * t + lax.broadcasted_iota(jnp.int32, (t, t), 0)
    col = j * t + lax.broadcasted_iota(jnp.int32, (t, t), 1)
    return col <= row


def _mla_attn_fwd(qc, kc, vv, *, t=512):
    S = qc.shape[1]
    t = min(t, S)

    def body(q_ref, k_ref, v_ref, o_ref, l_ref):
        i = pl.program_id(0)
        diag = _causal(0, 0, t)

        def step(j, carry, masked):
            rows = pl.ds(pl.multiple_of(j * t, t), t)
            out = []
            for h in range(MLA_HEADS):
                m, l, acc = carry[h]
                s = _dot_nt(q_ref[h], k_ref[h, rows, :]) * MLA_SCALE
                if masked:
                    s = jnp.where(diag, s, -jnp.inf)
                m_new = jnp.maximum(m, jnp.max(s, axis=-1, keepdims=True))
                alpha = jnp.exp(m - m_new)
                p = jnp.exp(s - m_new)
                l = alpha * l + jnp.sum(p, axis=-1, keepdims=True)
                acc = alpha * acc + _dot(p.astype(BF16), v_ref[h, rows, :])
                out.append((m_new, l, acc))
            return tuple(out)

        init = tuple((jnp.full((t, 1), -jnp.inf, F32), jnp.zeros((t, 1), F32), jnp.zeros((t, MLA_V), F32))
                     for _ in range(MLA_HEADS))
        carry = lax.fori_loop(0, i, lambda j, c: step(j, c, False), init)
        carry = step(i, carry, True)
        for h in range(MLA_HEADS):
            m, l, acc = carry[h]
            o_ref[:, 128 * h:128 * h + 128] = acc / l
            l_ref[h] = jnp.broadcast_to(m + jnp.log(l), (t, 128))

    return pl.pallas_call(
        body, name="mla_attn_fwd", grid=(S // t,),
        in_specs=[pl.BlockSpec((MLA_HEADS, t, 256), lambda i: (0, i, 0)),
                  pl.BlockSpec((MLA_HEADS, S, 256), lambda i: (0, 0, 0)),
                  pl.BlockSpec((MLA_HEADS, S, 128), lambda i: (0, 0, 0))],
        out_specs=[pl.BlockSpec((t, 512), lambda i: (i, 0)),
                   pl.BlockSpec((MLA_HEADS, t, 128), lambda i: (0, i, 0))],
        out_shape=[jax.ShapeDtypeStruct((S, 512), F32), jax.ShapeDtypeStruct((MLA_HEADS, S, 128), F32)],
        compiler_params=_params(("parallel",)),
    )(qc, kc, vv)


def _mla_attn_bwd(qc, kc, vv, o, lse, do, *, t=512, tq=1024):
    S = qc.shape[1]
    t = min(t, S)
    tq = min(tq, S)
    nblk = S // t
    hp = MLA_HEADS
    once = pl.Buffered(1)

    def body(q_ref, k_ref, v_ref, o_ref, l_ref, do_ref, dq_ref, dk_ref, dv_ref):
        j = pl.program_id(1)

        @pl.when(j == 0)
        def _():
            dq_ref[...] = jnp.zeros_like(dq_ref)

        first = (j * t) // tq

        def step(i, carry, masked):
            rows = pl.ds(pl.multiple_of(i * tq, tq), tq)
            if masked:
                row = i * tq + lax.broadcasted_iota(jnp.int32, (tq, t), 0)
                col = j * t + lax.broadcasted_iota(jnp.int32, (tq, t), 1)
                visible = col <= row
            out = []
            for h in range(hp):
                dk, dv = carry[h]
                k = k_ref[h]
                q = q_ref[h, rows, :]
                dov = do_ref[rows, 128 * h:128 * h + 128]
                lrow = l_ref[h, rows, :][:, 0:1]
                p = jnp.exp(_dot_nt(q, k) * MLA_SCALE - lrow)
                if masked:
                    p = jnp.where(visible, p, 0.0)
                dob = dov.astype(BF16)
                dv = dv + _dot_tn(p.astype(BF16), dob)
                dp = _dot_nt(dob, v_ref[h])
                delta = jnp.sum(dov * o_ref[rows, 128 * h:128 * h + 128], axis=-1, keepdims=True)
                ds = (p * (dp - delta) * MLA_SCALE).astype(BF16)
                dk = dk + _dot_tn(ds, q)
                dq_ref[h, rows, :] += _dot(ds, k)
                out.append((dk, dv))
            return tuple(out)

        init = tuple((jnp.zeros((t, 256), F32), jnp.zeros((t, MLA_V), F32)) for _ in range(hp))
        carry = step(first, init, True)
        carry = lax.fori_loop(first + 1, S // tq, lambda i, c: step(i, c, False), carry)
        for h in range(hp):
            dk_ref[h] = carry[h][0]
            dv_ref[h] = carry[h][1]

    return pl.pallas_call(
        body, name="mla_attn_bwd", grid=(MLA_HEADS // hp, nblk),
        in_specs=[pl.BlockSpec((hp, S, 256), lambda g, j: (g, 0, 0), pipeline_mode=once),
                  pl.BlockSpec((hp, t, 256), lambda g, j: (g, j, 0)),
                  pl.BlockSpec((hp, t, 128), lambda g, j: (g, j, 0)),
                  pl.BlockSpec((S, 128 * hp), lambda g, j: (0, g), pipeline_mode=once),
                  pl.BlockSpec((hp, S, 128), lambda g, j: (g, 0, 0), pipeline_mode=once),
                  pl.BlockSpec((S, 128 * hp), lambda g, j: (0, g), pipeline_mode=once)],
        out_specs=[pl.BlockSpec((hp, S, 256), lambda g, j: (g, 0, 0)),
                   pl.BlockSpec((hp, t, 256), lambda g, j: (g, j, 0)),
                   pl.BlockSpec((hp, t, 128), lambda g, j: (g, j, 0))],
        out_shape=[jax.ShapeDtypeStruct((MLA_HEADS, S, 256), F32), jax.ShapeDtypeStruct((MLA_HEADS, S, 256), F32),
                   jax.ShapeDtypeStruct((MLA_HEADS, S, 128), F32)],
        compiler_params=_params(("parallel", "arbitrary")),
    )(qc, kc, vv, o, lse, do)


def _mla_pre_bwd(proj, qn_w, kvn_w, wuqT, wukv, cos, sin, dqc, dkc, dvv, *, tm=ROW_TILE):
    S = proj.shape[0]
    tm = min(tm, S)

    def body(ql_ref, kl_ref, qw_ref, kw_ref, wuq_ref, wukv_ref, cos_ref, sin_ref, dqc_ref, dkc_ref, dvv_ref,
             dql_ref, dkl_ref, dkr_ref, gq_ref, gkv_ref, part_ref):
        @pl.when(pl.program_id(0) == 0)
        def _():
            gq_ref[...] = jnp.zeros_like(gq_ref)
            gkv_ref[...] = jnp.zeros_like(gkv_ref)
            part_ref[...] = jnp.zeros_like(part_ref)

        cs, sn = _rope_wide(cos_ref), _rope_wide(sin_ref)
        half = lax.broadcasted_iota(jnp.int32, (tm, 128), 1) // 64
        ql = ql_ref[...]
        rq = _rstd(ql)
        qhat = ql * rq
        qw = qw_ref[...]
        qn = (qhat * qw).astype(BF16)
        chunks = []
        for pair in range(2):
            chunks.append(jnp.where(half == 0, dqc_ref[2 * pair, :, 128:256], dqc_ref[2 * pair + 1, :, 128:256]))
        dqr = jnp.concatenate(chunks, axis=1)
        dqr = dqr * cs + _swap_halves(dqr * sn)
        dq = jnp.concatenate([dqc_ref[h, :, 0:128] for h in range(MLA_HEADS)] + [dqr], axis=1).astype(BF16)
        gq_ref[...] += _dot_tn(dq, qn)
        dqn = _dot(dq, wuq_ref[...])
        part_ref[0:1, :] += jnp.sum(dqn * qhat, axis=0, keepdims=True)
        dql_ref[...] = _rms_bwd(dqn * qw, qhat, rq)
        kl = kl_ref[...]
        rk = _rstd(kl)
        khat = kl * rk
        kw = kw_ref[...]
        kvn = (khat * kw).astype(BF16)
        dkvn = jnp.zeros((tm, MLA_KVR), F32)
        dkr2 = jnp.zeros((tm, 128), F32)
        for h in range(MLA_HEADS):
            dkn = dkc_ref[h, :, 0:128].astype(BF16)
            dvh = dvv_ref[h].astype(BF16)
            gkv_ref[2 * h] += _dot_tn(kvn, dkn)
            gkv_ref[2 * h + 1] += _dot_tn(kvn, dvh)
            dkvn += _dot_nt(dkn, wukv_ref[2 * h]) + _dot_nt(dvh, wukv_ref[2 * h + 1])
            dkr2 += dkc_ref[h, :, 128:256]
        part_ref[1:2, 0:128] += jnp.sum(dkvn * khat, axis=0, keepdims=True)
        dkl_ref[...] = _rms_bwd(dkvn * kw, khat, rk)
        dkr = jnp.where(half == 0, dkr2 + pltpu.roll(dkr2, 64, 1), 0.0)
        dkr_ref[...] = dkr * cs[:, :128] + _swap_halves(dkr * sn[:, :128])

    const = lambda shape: pl.BlockSpec(shape, lambda i: tuple(0 for _ in shape))
    heads = lambda w: pl.BlockSpec((MLA_HEADS, tm, w), lambda i: (0, i, 0))
    return pl.pallas_call(
        body, name="mla_pre_bwd", grid=(S // tm,),
        in_specs=[pl.BlockSpec((tm, 256), lambda i: (i, 3)), pl.BlockSpec((tm, 128), lambda i: (i, 8)),
                  const((1, 256)), const((1, 128)), const((768, 256)), const((8, 128, 128)),
                  pl.BlockSpec((tm, 128), lambda i: (i, 0)), pl.BlockSpec((tm, 128), lambda i: (i, 0)),
                  heads(256), heads(256), heads(128)],
        out_specs=[pl.BlockSpec((tm, 256), lambda i: (i, 0)), pl.BlockSpec((tm, 128), lambda i: (i, 0)),
                   pl.BlockSpec((tm, 128), lambda i: (i, 0)), const((768, 256)), const((8, 128, 128)), const((8, 256))],
        out_shape=[jax.ShapeDtypeStruct((S, 256), F32), jax.ShapeDtypeStruct((S, 128), F32),
                   jax.ShapeDtypeStruct((S, 128), F32), jax.ShapeDtypeStruct((768, 256), F32),
                   jax.ShapeDtypeStruct((8, 128, 128), F32), jax.ShapeDtypeStruct((8, 256), F32)],
        compiler_params=_params(("arbitrary",)),
    )(proj, proj, qn_w, kvn_w, wuqT, wukv, cos, sin, dqc, dkc, dvv)


def _mix_out_fwd(x, oa, ob, w_o, vecs, *, tm=ROW_TILE):
    S = x.shape[0]
    tm = min(tm, S)

    def body(x_ref, oa_ref, ob_ref, w_ref, vec_ref, xo_ref, mo_ref):
        mo = _dot(oa_ref[...].astype(BF16), w_ref[0:512, :]) + _dot(ob_ref[...].astype(BF16), w_ref[512:1024, :])
        mo_ref[...] = mo
        xo_ref[...] = x_ref[...] + vec_ref[3:4, :] * mo

    row = pl.BlockSpec((tm, D), lambda i: (i, 0))
    half = pl.BlockSpec((tm, 512), lambda i: (i, 0))
    return pl.pallas_call(
        body, name="mix_out_fwd", grid=(S // tm,),
        in_specs=[row, half, half, pl.BlockSpec((D, D), lambda i: (0, 0)), pl.BlockSpec((8, D), lambda i: (0, 0))],
        out_specs=[row, row],
        out_shape=[jax.ShapeDtypeStruct((S, D), F32), jax.ShapeDtypeStruct((S, D), F32)],
        compiler_params=_params(("parallel",)),
    )(x, oa, ob, w_o, vecs)


def _mix_out_bwd(dxo, mo, oa, ob, w_o, vecs, *, tm=ROW_TILE):
    S = dxo.shape[0]
    tm = min(tm, S)

    def body(dx_ref, mo_ref, oa_ref, ob_ref, w_ref, vec_ref, doa_ref, dob_ref, gw_ref, part_ref):
        @pl.when(pl.program_id(0) == 0)
        def _():
            gw_ref[...] = jnp.zeros_like(gw_ref)
            part_ref[...] = jnp.zeros_like(part_ref)

        dx = dx_ref[...]
        part_ref[0:1, :] += jnp.sum(dx * mo_ref[...], axis=0, keepdims=True)
        dmo = (vec_ref[3:4, :] * dx).astype(BF16)
        doa_ref[...] = _dot_nt(dmo, w_ref[0:512, :])
        dob_ref[...] = _dot_nt(dmo, w_ref[512:1024, :])
        gw_ref[0:512, :] += _dot_tn(oa_ref[...].astype(BF16), dmo)
        gw_ref[512:1024, :] += _dot_tn(ob_ref[...].astype(BF16), dmo)

    row = pl.BlockSpec((tm, D), lambda i: (i, 0))
    half = pl.BlockSpec((tm, 512), lambda i: (i, 0))
    return pl.pallas_call(
        body, name="mix_out_bwd", grid=(S // tm,),
        in_specs=[row, row, half, half, pl.BlockSpec((D, D), lambda i: (0, 0)), pl.BlockSpec((8, D), lambda i: (0, 0))],
        out_specs=[half, half, pl.BlockSpec((D, D), lambda i: (0, 0)), pl.BlockSpec((8, D), lambda i: (0, 0))],
        out_shape=[jax.ShapeDtypeStruct((S, 512), F32), jax.ShapeDtypeStruct((S, 512), F32),
                   jax.ShapeDtypeStruct((D, D), F32), jax.ShapeDtypeStruct((8, D), F32)],
        compiler_params=_params(("arbitrary",)),
    )(dxo, mo, oa, ob, w_o, vecs)


def _mix_in_bwd(h, w_inT, dq, dk, dv, dql, dkl, dkr, *, tm=ROW_TILE):
    S = h.shape[0]
    tm = min(tm, S)
    offs = (0, 512, 640, 768, 1024, 1152)
    wid = (512, 128, 128, 256, 128, 128)

    def body(h_ref, w_ref, dq_ref, dk_ref, dv_ref, dql_ref, dkl_ref, dkr_ref, dh_ref, gw_ref):
        @pl.when(pl.program_id(0) == 0)
        def _():
            gw_ref[...] = jnp.zeros_like(gw_ref)

        hv = h_ref[...]
        dh = jnp.zeros((tm, D), F32)
        for ref, o, w in zip((dq_ref, dk_ref, dv_ref, dql_ref, dkl_ref, dkr_ref), offs, wid):
            w = min(w, D_IN - o)
            dpart = ref[...][:, :w].astype(BF16)
            dh += _dot(dpart, w_ref[o:o + w, :])
            gw_ref[o:o + w, :] += _dot_tn(dpart, hv)
        dh_ref[...] = dh

    row = pl.BlockSpec((tm, D), lambda i: (i, 0))
    part = lambda w: pl.BlockSpec((tm, w), lambda i: (i, 0))
    return pl.pallas_call(
        body, name="mix_in_bwd", grid=(S // tm,),
        in_specs=[row, pl.BlockSpec((D_IN_PAD, D), lambda i: (0, 0))] + [part(w) for w in wid],
        out_specs=[row, pl.BlockSpec((D_IN, D), lambda i: (0, 0))],
        out_shape=[jax.ShapeDtypeStruct((S, D), F32), jax.ShapeDtypeStruct((D_IN, D), F32)],
        compiler_params=_params(("arbitrary",)),
    )(h, w_inT, dq, dk, dv, dql, dkl, dkr)


def _vecs(norm_w, mod9, k):
    return jnp.concatenate([norm_w.reshape(1, D), mod9[3 * k:3 * k + 3], jnp.zeros((4, D), F32)], axis=0)


def _uq_group_rows(wuqT):
    per = MLA_NOPE + MLA_ROPE
    nope = [wuqT[per * h:per * h + MLA_NOPE] for h in range(MLA_HEADS)]
    rope = [wuqT[per * h + MLA_NOPE:per * (h + 1)] for h in range(MLA_HEADS)]
    return jnp.concatenate(nope + rope, axis=0)


def _uq_ungroup_rows(g):
    parts = []
    for h in range(MLA_HEADS):
        parts += [g[MLA_NOPE * h:MLA_NOPE * (h + 1)], g[512 + MLA_ROPE * h:512 + MLA_ROPE * (h + 1)]]
    return jnp.concatenate(parts, axis=0)


def _local_step(x, tgt, mod9, norms, sinks, rel_bias, q_norm, kv_norm, W, on_grads=None):
    if on_grads is None:
        on_grads = lambda group, grads, after, vecs: vecs
    S = x.shape[0]
    v1 = _vecs(norms["ffn1"], mod9, 0)
    v2 = _vecs(norms["mix"], mod9, 1)
    v3 = _vecs(norms["ffn2"], mod9, 2)
    bucket = jnp.asarray(_bucket_table())
    cos, sin = _rope_tables(S)
    if isinstance(W, dict):
        full, W = W, (lambda group, after, vecs: (full, vecs))

    W1, v1 = W("ffn1", [], v1)
    x1, h1, a1, b1, f1 = _ffn_fwd(x, v1, W1["g1T"], W1["u1T"], W1["d1"], name="ffn1_fwd")
    W2, v2 = W("mixer", [x1], v2)
    w_inT = jnp.pad(W2["w_inT"], ((0, D_IN_PAD - D_IN), (0, 0))).astype(BF16)
    wuqT = _uq_group_rows(W2["w_uqT"])
    h2, proj = _mix_in_fwd(x1, v2, w_inT)
    bias = _bias_build(rel_bias, bucket)
    oa = _swa_fwd(proj, bias, sinks)
    qc, kc, vv = _mla_pre_fwd(proj, q_norm, kv_norm, wuqT, W2["w_ukv"], cos, sin)
    ob, lse = _mla_attn_fwd(qc, kc, vv)
    _, v2o = W("ffn2_on_its_way", [ob], v2)
    x2, mo = _mix_out_fwd(x1, oa, ob, W2["w_o"], v2o)
    W3, v3 = W("ffn2", [x2], v3)
    x3, h3, a3, b3, f3 = _ffn_fwd(x2, v3, W3["g3T"], W3["u3T"], W3["d3"], name="ffn2_fwd")
    dx3, head_part, df3 = _head(x3, tgt, norms["final"], f3, v3)

    gg3, gu3, gd3, dh3 = _ffn_bwd_main(h3, df3, a3, b3, W3["g3T"], W3["u3T"], W3["d3"], name="ffn2_bwd")
    ffn2 = {"g3T": gg3, "u3T": gu3, "d3": gd3}
    v3 = on_grads("ffn2", ffn2, [], v3)
    dx2, n3_part = _norm_bwd(dh3, x2, dx3, v3, name="ffn2_norm_bwd")
    v2 = on_grads("ffn2", None, [dx2], v2)
    doa, dob, g_wo, g2_part = _mix_out_bwd(dx2, mo, oa, ob, W2["w_o"], v2)
    dq, dk, dv, drb, dsk = _swa_bwd(proj, bias, sinks, oa, doa, bucket)
    dqc, dkc, dvv = _mla_attn_bwd(qc, kc, vv, ob, lse, dob)
    dql, dkl, dkr, g_uq, g_ukv, mla_part = _mla_pre_bwd(proj, q_norm, kv_norm, wuqT, W2["w_ukv"], cos, sin, dqc, dkc, dvv)
    dh2, g_win = _mix_in_bwd(h2, w_inT, dq, dk, dv, dql, dkl, dkr)
    mixer = {"w_inT": g_win, "w_uqT": _uq_ungroup_rows(g_uq).astype(BF16),
             "w_ukv": g_ukv.astype(BF16), "w_o": g_wo.astype(BF16)}
    v2 = on_grads("mixer", mixer, [], v2)
    dx1, n2_part, df1 = _norm_bwd(dh2, x1, dx2, v2, name="mix_norm_bwd", below=(f1, v1))
    started = on_grads("mixer", None, [dx1], jnp.zeros((1, 1), F32))
    gg1, gu1, gd1, dh1 = _ffn_bwd_main(h1, df1, a1, b1, W1["g1T"], W1["u1T"], W1["d1"], name="ffn1_bwd",
                                       after=[started])
    ffn1 = {"g1T": gg1, "u1T": gu1, "d1": gd1}
    v1 = on_grads("ffn1", ffn1, [], v1)
    dx0, n1_part = _norm_bwd(dh1, x, dx1, v1, name="ffn1_norm_bwd")

    grads = {**ffn1, **ffn2, **mixer}
    return head_part[1, 0], dx0, grads, _pack_vec(n1_part, n2_part, n3_part, head_part, g2_part, mla_part, dsk, drb)


SMALL_LAYOUT = (("norm_ffn1", 1024), ("norm_mix", 1024), ("norm_ffn2", 1024), ("norm_final", 1024),
                ("q_norm", 256), ("kv_norm", 128), ("sinks", 128), ("rel_bias", 256))
N_SMALL = sum(n for _, n in SMALL_LAYOUT)
LOSS_SLOT = 4 * 1024 + 256 + 128 + SWA_HEADS
N_MODVEC = N_MOD * D
N_VEC = N_MODVEC + N_SMALL


def _pack_vec(n1, n2, n3, head, g2, mla, dsk, drb):
    def body(n1_ref, n2_ref, n3_ref, head_ref, g2_ref, mla_ref, dsk_ref, drb_ref, out_ref):
        rows = [n1_ref[1:2, :], n1_ref[2:3, :], n2_ref[3:4, :], n2_ref[1:2, :], n2_ref[2:3, :], g2_ref[0:1, :],
                n3_ref[1:2, :], n3_ref[2:3, :], head_ref[3:4, :],
                n1_ref[0:1, :], n2_ref[0:1, :], n3_ref[0:1, :], head_ref[0:1, :]]
        for i, row in enumerate(rows):
            out_ref[:, D * i:D * (i + 1)] = row
        off = D * len(rows)
        out_ref[:, off:off + 256] = mla_ref[0:1, :]
        out_ref[:, off + 256:off + 384] = mla_ref[1:2, 0:128]

        def diagonal(block):
            r = lax.broadcasted_iota(jnp.int32, block.shape, 0)
            lane = lax.broadcasted_iota(jnp.int32, block.shape, 1)
            return jnp.sum(jnp.where(r == lane, block, 0.0), axis=0, keepdims=True)

        lane = lax.broadcasted_iota(jnp.int32, (1, 128), 1)
        out_ref[:, off + 384:off + 512] = jnp.where(lane == SWA_HEADS, head_ref[1:2, 0:128], diagonal(dsk_ref[...]))
        out_ref[:, off + 512:off + 640] = diagonal(drb_ref[0:128, :])
        out_ref[:, off + 640:off + 768] = diagonal(drb_ref[128:256, :])

    vm = pl.BlockSpec(memory_space=pltpu.VMEM)
    return pl.pallas_call(body, name="pack_vec", in_specs=[vm] * 8, out_specs=vm,
                          out_shape=jax.ShapeDtypeStruct((1, N_VEC), F32))(n1, n2, n3, head, g2, mla, dsk, drb)


def _coords():
    return lax.axis_index("x"), lax.axis_index("y"), lax.axis_index("c")


def _flip(v, bit):
    return 1 - v if bit else v


def _peer(r):
    x, y, c = _coords()
    return (_flip(x, r & 4), _flip(y, r & 2), _flip(c, r & 1))


def _mod_fwd(c_tile, w_mod, b_mod3):
    W = w_mod.shape[1]

    def body(c_ref, w_ref, b_ref, mod_ref, ca_ref, call_ref, part_ref, send_sems, recv_sems):
        x, y, c = _coords()
        me = 4 * x + 2 * y + c
        call_ref[me] = c_ref[...]
        sends = []
        for r in range(1, N_DEV):
            cp = pltpu.make_async_remote_copy(c_ref, call_ref.at[me], send_sems.at[0, r], recv_sems.at[0, r],
                                              device_id=_peer(r), device_id_type=MESH)
            cp.start()
            sends.append(cp)
        for r in range(1, N_DEV):
            pltpu.make_async_remote_copy(c_ref, call_ref.at[me], send_sems.at[0, r], recv_sems.at[0, r],
                                         device_id=_peer(r), device_id_type=MESH).wait_recv()
        cv = call_ref[...].reshape(8 * N_DEV, D)
        ca = (cv * _sigmoid(cv)).astype(BF16)
        ca_ref[...] = ca
        part_ref[...] = _dot(ca, w_ref[...].astype(BF16)).reshape(N_DEV, 8, W)
        mod_ref[me] = part_ref[me] + b_ref[me]
        for r in range(1, N_DEV):
            cp = pltpu.make_async_remote_copy(part_ref.at[me ^ r], mod_ref.at[me], send_sems.at[1, r],
                                              recv_sems.at[1, r], device_id=_peer(r), device_id_type=MESH)
            cp.start()
            sends.append(cp)
        for r in range(1, N_DEV):
            pltpu.make_async_remote_copy(part_ref.at[me ^ r], mod_ref.at[me], send_sems.at[1, r],
                                         recv_sems.at[1, r], device_id=_peer(r), device_id_type=MESH).wait_recv()
            mod_ref[me ^ r] = mod_ref[me ^ r] + b_ref[me ^ r]
        for cp in sends:
            cp.wait_send()

    vm = pl.BlockSpec(memory_space=pltpu.VMEM)
    return pl.pallas_call(
        body, name="mod_fwd", in_specs=[vm, vm, vm], out_specs=[vm, vm],
        out_shape=[jax.ShapeDtypeStruct((N_DEV, 8, W), F32), jax.ShapeDtypeStruct((8 * N_DEV, D), BF16)],
        scratch_shapes=[pltpu.VMEM((N_DEV, 8, D), F32), pltpu.VMEM((N_DEV, 8, W), F32),
                        pltpu.SemaphoreType.DMA((2, N_DEV)), pltpu.SemaphoreType.DMA((2, N_DEV))],
        compiler_params=_params(),
    )(c_tile, w_mod, b_mod3)


def _mod_bwd(allvec, ca, me_idx):
    W = N_MODVEC // N_DEV

    def body(me_ref, all_ref, cols_ref, ca_ref, gw_ref, sum_ref):
        in_first_row = lax.broadcasted_iota(jnp.int32, (N_DEV, 8, W), 1) == 0
        dm = jnp.where(in_first_row, cols_ref[...], 0.0).reshape(8 * N_DEV, W)
        gw_ref[...] = _dot_tn(ca_ref[...], dm.astype(BF16))
        total = all_ref[0]
        for k in range(1, N_DEV):
            total = total + all_ref[k]
        sum_ref[...] = total

    return pl.pallas_call(
        body, name="mod_bwd",
        grid_spec=pltpu.PrefetchScalarGridSpec(
            num_scalar_prefetch=1, grid=(1,),
            in_specs=[pl.BlockSpec((N_DEV, 1, N_VEC), lambda i, me: (0, 0, 0)),
                      pl.BlockSpec((N_DEV, 1, W), lambda i, me: (0, 0, me[0])),
                      pl.BlockSpec((8 * N_DEV, D), lambda i, me: (0, 0))],
            out_specs=[pl.BlockSpec((D, W), lambda i, me: (0, 0)), pl.BlockSpec((1, N_VEC), lambda i, me: (0, 0))]),
        out_shape=[jax.ShapeDtypeStruct((D, W), F32), jax.ShapeDtypeStruct((1, N_VEC), F32)],
        compiler_params=_params(("arbitrary",)),
    )(me_idx, allvec, allvec, ca)


def _wgather(shards):
    n = len(shards)
    rows = [s.shape[0] for s in shards]

    def body(*refs):
        ins, outs, token = refs[:n], refs[n:2 * n], refs[2 * n]
        send_sems, recv_sems, local_sems = refs[2 * n + 1:]
        token[...] = jnp.zeros_like(token)
        x, y, c = _coords()
        me = 4 * x + 2 * y + c
        sib, xn, yn = (x, y, 1 - c), (1 - x, y, c), (x, 1 - y, c)
        block = lambda px, py, pc: 4 * px + 2 * py + pc

        def part(k, blk, half):
            if half is None:
                return outs[k].at[blk]
            return outs[k].at[blk, pl.ds(half * (rows[k] // 2), rows[k] // 2)]

        def copy(k, slot, blk, to, half=None, src=None):
            ref = part(k, blk, half)
            return pltpu.make_async_remote_copy(
                src_ref=ref if src is None else src, dst_ref=ref, send_sem=send_sems.at[k, slot],
                recv_sem=recv_sems.at[k, slot], device_id=to, device_id_type=MESH)

        local = [pltpu.make_async_copy(ins[k], outs[k].at[me], local_sems.at[k]) for k in range(n)]
        for cp in local:
            cp.start()
        sent = [copy(k, slot, me, to, src=ins[k]) for k in range(n) for slot, to in ((0, sib), (1, xn), (2, yn))]
        for cp in sent:
            cp.start()
        bx, by, bd = block(1 - x, y, c), block(x, 1 - y, c), block(1 - x, 1 - y, c)
        for k in range(n):
            copy(k, 1, bx, sib).wait_recv()
            sent += [copy(k, 4, bx, yn, half=1), copy(k, 5, bx, sib)]
            sent[-2].start()
            sent[-1].start()
        for k in range(n):
            copy(k, 2, by, sib).wait_recv()
            sent += [copy(k, 3, by, xn, half=0), copy(k, 6, by, sib)]
            sent[-2].start()
            sent[-1].start()
        for k in range(n):
            copy(k, 3, bd, sib, half=0).wait_recv()
            copy(k, 4, bd, sib, half=1).wait_recv()
            sent.append(copy(k, 7, bd, sib))
            sent[-1].start()
        for k in range(n):
            copy(k, 0, block(x, y, 1 - c), sib).wait_recv()
            for slot, blk in ((5, block(1 - x, y, 1 - c)), (6, block(x, 1 - y, 1 - c)), (7, block(1 - x, 1 - y, 1 - c))):
                copy(k, slot, blk, sib).wait_recv()
        for cp in sent:
            cp.wait_send()
        for cp in local:
            cp.wait()

    anyspec = pl.BlockSpec(memory_space=pl.ANY)
    return pl.pallas_call(
        body, name="wgather", in_specs=[anyspec] * n,
        out_specs=[anyspec] * n + [pl.BlockSpec(memory_space=pltpu.VMEM)],
        out_shape=[jax.ShapeDtypeStruct((N_DEV,) + s.shape, s.dtype) for s in shards]
        + [jax.ShapeDtypeStruct((8, 128), F32)],
        scratch_shapes=[pltpu.SemaphoreType.DMA((n, 8)), pltpu.SemaphoreType.DMA((n, 8)),
                        pltpu.SemaphoreType.DMA((n,))],
    )(*shards)


class _GatherCopies:
    def __init__(self, lands, send_sems, recv_sems, k0=0, batches=None):
        x, y, c = _coords()
        me = 4 * x + 2 * y + c
        sib = (x, y, 1 - c)
        chips = [(1 - x, y), (x, 1 - y), (1 - x, 1 - y)]

        def copy(k, slot, block, to):
            return pltpu.make_async_remote_copy(
                src_ref=lands[k].at[block], dst_ref=lands[k].at[block],
                send_sem=send_sems.at[7 * (k0 + k) + slot], recv_sem=recv_sems.at[7 * (k0 + k) + slot],
                device_id=to, device_id_type=MESH)

        n = len(lands)
        self.first = [copy(k, 0, me, sib) for k in range(n)]
        for batch in batches or [range(n)]:
            self.first += [copy(k, 1 + j, me, (cx, cy, c)) for j, (cx, cy) in enumerate(chips) for k in batch]
        self.landed = [copy(k, 1 + j, 4 * cx + 2 * cy + c, sib) for j, (cx, cy) in enumerate(chips) for k in range(n)]
        self.passed = [copy(k, 4 + j, 4 * cx + 2 * cy + c, sib) for j, (cx, cy) in enumerate(chips) for k in range(n)]
        self.from_sib = [copy(k, 0, 4 * x + 2 * y + (1 - c), sib) for k in range(n)]
        self.from_sib += [copy(k, 4 + j, 4 * cx + 2 * cy + (1 - c), sib) for j, (cx, cy) in enumerate(chips)
                          for k in range(n)]


def _gather_start(lands, *, name, batches=None):
    n = len(lands)

    def body(*refs):
        for cp in _GatherCopies(refs[:n], refs[n], refs[n + 1], batches=batches).first:
            cp.start()
        refs[-1][...] = jnp.zeros_like(refs[-1])

    out = pl.pallas_call(
        body, name=name,
        out_shape=(pltpu.SemaphoreType.DMA((7 * n,)), pltpu.SemaphoreType.DMA((7 * n,)),
                   *[pltpu.HBM(l.shape, l.dtype) for l in lands], jax.ShapeDtypeStruct((8, 128), F32)),
        in_specs=[HBM_SPEC] * n,
        out_specs=(SEM_SPEC, SEM_SPEC, *[HBM_SPEC] * n, pl.BlockSpec(memory_space=pltpu.VMEM)),
        input_output_aliases={i: 2 + i for i in range(n)},
        compiler_params=pltpu.CompilerParams(has_side_effects=DATAFLOW),
    )(*[_in_hbm(l) for l in lands])
    return out[0], out[1], list(out[2:2 + n]), out[-1]


def _gather_pass(send_sems, recv_sems, lands, after, *, name, stage, k0=0):
    n = len(lands)

    def body(*refs):
        cps = _GatherCopies(refs[:n], refs[n], refs[n + 1], k0)
        if stage == "landed":
            for cp in cps.landed:
                cp.wait_recv()
        else:
            for cp in cps.passed:
                cp.start()
        refs[-1][...] = jnp.zeros_like(refs[-1])

    out = pl.pallas_call(
        body, name=name,
        out_shape=(*[pltpu.HBM(l.shape, l.dtype) for l in lands], jax.ShapeDtypeStruct((8, 128), F32)),
        in_specs=[HBM_SPEC] * n + [SEM_SPEC, SEM_SPEC] + [pl.BlockSpec(memory_space=pl.ANY)] * len(after),
        out_specs=(*[HBM_SPEC] * n, pl.BlockSpec(memory_space=pltpu.VMEM)),
        input_output_aliases={i: i for i in range(n)},
        compiler_params=pltpu.CompilerParams(has_side_effects=DATAFLOW),
    )(*lands, send_sems, recv_sems, *after)
    return list(out[:n]), out[-1]


def _gather_end(send_sems, recv_sems, lands, after, *, name, k0=0):
    n = len(lands)

    def body(*refs):
        cps = _GatherCopies(refs[:n], refs[n], refs[n + 1], k0)
        for cp in cps.from_sib:
            cp.wait_recv()
        for cp in cps.first + cps.passed:
            cp.wait_send()

    out = pl.pallas_call(
        body, name=name,
        out_shape=[pltpu.HBM(l.shape, l.dtype) for l in lands],
        in_specs=[HBM_SPEC] * n + [SEM_SPEC, SEM_SPEC] + [pl.BlockSpec(memory_space=pl.ANY)] * len(after),
        out_specs=[HBM_SPEC] * n,
        input_output_aliases={i: i for i in range(n)},
        compiler_params=pltpu.CompilerParams(has_side_effects=DATAFLOW),
    )(*lands, send_sems, recv_sems, *after)
    return list(out)


def _d2d_copies(grads, lands, send_sems, recv_sems):
    x, y, c = _coords()
    return [pltpu.make_async_remote_copy(
        src_ref=grads[k].at[2 * q + (1 - c)], dst_ref=lands[k].at[q],
        send_sem=send_sems.at[4 * k + q], recv_sem=recv_sems.at[4 * k + q],
        device_id=(x, y, 1 - c), device_id_type=MESH) for k in range(len(grads)) for q in range(4)]


def _direct_copies(grads, lands, send_sems, recv_sems):
    x, y, c = _coords()
    me = 4 * x + 2 * y + c
    return [pltpu.make_async_remote_copy(
        src_ref=grads[k].at[me ^ r], dst_ref=lands[k].at[r - 1],
        send_sem=send_sems.at[7 * k + r - 1], recv_sem=recv_sems.at[7 * k + r - 1],
        device_id=_peer(r), device_id_type=MESH) for k in range(len(grads)) for r in range(1, N_DEV)]


def _vec_copies(srcs, lands, send_sems, recv_sems):
    x, y, c = _coords()
    me = 4 * x + 2 * y + c
    return [pltpu.make_async_remote_copy(
        src_ref=lands[0].at[me], dst_ref=lands[0].at[me], send_sem=send_sems.at[r - 1], recv_sem=recv_sems.at[r - 1],
        device_id=_peer(r), device_id_type=MESH) for r in range(1, N_DEV)]


def _chipsum(gs, sibs, cidx, *, name):
    n = len(gs)

    def body(c_ref, *refs):
        for k in range(n):
            refs[2 * n + k][...] = (refs[k][...].astype(F32) + refs[n + k][...].astype(F32)).astype(refs[2 * n + k].dtype)

    mine = [pl.BlockSpec((1,) + g.shape[1:], lambda q, c_ref: (2 * q + c_ref[0], 0, 0)) for g in gs]
    other = [pl.BlockSpec((1,) + g.shape[1:], lambda q, c_ref: (q, 0, 0)) for g in gs]
    return pl.pallas_call(
        body, name=name,
        grid_spec=pltpu.PrefetchScalarGridSpec(num_scalar_prefetch=1, grid=(4,), in_specs=mine + other, out_specs=other),
        out_shape=[jax.ShapeDtypeStruct((4,) + g.shape[1:], g.dtype) for g in gs],
        compiler_params=_params(("arbitrary",)),
    )(cidx, *gs, *sibs)


HBM_SPEC = pl.BlockSpec(memory_space=pltpu.HBM)
SEM_SPEC = pl.BlockSpec(memory_space=pltpu.SEMAPHORE)
DATAFLOW = pltpu.SideEffectType.DATAFLOW_SIDE_EFFECTING


def _in_hbm(a):
    return pltpu.with_memory_space_constraint(a, pltpu.HBM)


def _rs_step1_copies(sums, lands, send_sems, recv_sems):
    n = len(sums)
    direct, relay = lands[:n], lands[n:]
    x, y, c = _coords()
    xn, yn = (1 - x, y, c), (x, 1 - y, c)
    qx, qy, qd = 2 * (1 - x) + y, 2 * x + (1 - y), 2 * (1 - x) + (1 - y)
    cps = []
    for k in range(n):
        h = sums[k].shape[1] // 2
        a, b = pl.ds(0, h), pl.ds(h, h)
        moves = ((sums[k].at[qx, a], direct[k].at[0], xn), (sums[k].at[qy, b], direct[k].at[1], yn),
                 (sums[k].at[qd, a], relay[k].at[0], xn), (sums[k].at[qd, b], relay[k].at[1], yn))
        for s, (src, dst, to) in enumerate(moves):
            cps.append(pltpu.make_async_remote_copy(
                src_ref=src, dst_ref=dst, send_sem=send_sems.at[4 * k + s], recv_sem=recv_sems.at[4 * k + s],
                device_id=to, device_id_type=MESH))
    return cps


def _rs_step2_copies(relayed, lands, send_sems, recv_sems, k0=0):
    x, y, c = _coords()
    cps = []
    for k in range(len(relayed)):
        for s, to in enumerate(((1 - x, y, c), (x, 1 - y, c))):
            cps.append(pltpu.make_async_remote_copy(
                src_ref=relayed[k].at[s], dst_ref=lands[k].at[s], send_sem=send_sems.at[2 * (k0 + k) + s],
                recv_sem=recv_sems.at[2 * (k0 + k) + s], device_id=to, device_id_type=MESH))
    return cps


def _relay_sum(sums, relay, qxy, *, name):
    n = len(sums)

    def body(q_ref, *refs):
        for k in range(n):
            refs[2 * n + k][...] = (refs[k][...].astype(F32) + refs[n + k][...].astype(F32)).astype(refs[2 * n + k].dtype)

    half = lambda s: (1, s.shape[1] // 2) + s.shape[2:]
    return pl.pallas_call(
        body, name=name,
        grid_spec=pltpu.PrefetchScalarGridSpec(
            num_scalar_prefetch=1, grid=(2,),
            in_specs=[pl.BlockSpec(half(s), lambda t, q_ref: (q_ref[t], 1 - t, 0)) for s in sums]
            + [pl.BlockSpec(half(s), lambda t, q_ref: (1 - t, 0, 0)) for s in sums],
            out_specs=[pl.BlockSpec(half(s), lambda t, q_ref: (t, 0, 0)) for s in sums]),
        out_shape=[jax.ShapeDtypeStruct((2,) + half(s)[1:], s.dtype) for s in sums],
        compiler_params=_params(("arbitrary",)),
    )(qxy, *sums, *relay)


def _split_start(copies, srcs, lands, n_sems, after, *, name):
    ns, nl = len(srcs), len(lands)

    def body(*refs):
        for cp in copies(refs[:ns], refs[ns:ns + nl], refs[ns + nl + len(after)], refs[ns + nl + len(after) + 1]):
            cp.start()
        refs[-1][...] = jnp.zeros_like(refs[-1])

    bufs = [_in_hbm(a) for a in list(srcs) + list(lands)]
    out = pl.pallas_call(
        body, name=name,
        out_shape=(pltpu.SemaphoreType.DMA((n_sems,)), pltpu.SemaphoreType.DMA((n_sems,)),
                   *[pltpu.HBM(a.shape, a.dtype) for a in bufs], jax.ShapeDtypeStruct((8, 128), F32)),
        in_specs=[HBM_SPEC] * len(bufs) + [pl.BlockSpec(memory_space=pl.ANY)] * len(after),
        out_specs=(SEM_SPEC, SEM_SPEC, *[HBM_SPEC] * len(bufs), pl.BlockSpec(memory_space=pltpu.VMEM)),
        input_output_aliases={i: 2 + i for i in range(len(bufs))},
        compiler_params=pltpu.CompilerParams(has_side_effects=DATAFLOW),
    )(*bufs, *after)
    return out[0], out[1], list(out[2:2 + ns]), list(out[2 + ns:2 + ns + nl]), out[-1]


def _split_wait(copies, send_sems, recv_sems, srcs, lands, after, *, name):
    ns, nl = len(srcs), len(lands)

    def body(*refs):
        for cp in copies(refs[:ns], refs[ns:ns + nl], refs[ns + nl], refs[ns + nl + 1]):
            cp.wait_send()
            cp.wait_recv()

    out = pl.pallas_call(
        body, name=name,
        out_shape=[pltpu.HBM(a.shape, a.dtype) for a in list(srcs) + list(lands)],
        in_specs=[HBM_SPEC] * (ns + nl) + [SEM_SPEC, SEM_SPEC] + [pl.BlockSpec(memory_space=pl.ANY)] * len(after),
        out_specs=[HBM_SPEC] * (ns + nl),
        input_output_aliases={i: i for i in range(ns + nl)},
        compiler_params=pltpu.CompilerParams(has_side_effects=DATAFLOW),
    )(*srcs, *lands, send_sems, recv_sems, *after)
    return list(out[:ns]), list(out[ns:])


ADAM_C1 = 1.0 / (1.0 - ADAM_B1 ** ADAM_STEP)
ADAM_C2 = 1.0 / (1.0 - ADAM_B2 ** ADAM_STEP)


def _adam_math(w, g, m, v):
    m2 = ADAM_B1 * m + (1.0 - ADAM_B1) * g
    v2 = ADAM_B2 * v + (1.0 - ADAM_B2) * (g * g)
    return -ADAM_LR * ((m2 * ADAM_C1) / (jnp.sqrt(v2 * ADAM_C2) + ADAM_EPS) + ADAM_WD * w), m2, v2


def _adamw(w, g, m, v, *, name, after=()):
    R, C = w.shape
    tr = R if R <= 512 else 256

    def body(w_ref, g_ref, m_ref, v_ref, *rest):
        d_ref, nm_ref, nv_ref = rest[len(after):]
        d_ref[...], nm_ref[...], nv_ref[...] = _adam_math(w_ref[...], g_ref[...], m_ref[...], v_ref[...])

    blk = pl.BlockSpec((tr, C), lambda i: (i, 0))
    return pl.pallas_call(
        body, name=name, grid=(R // tr,), in_specs=[blk] * 4 + [pl.BlockSpec(memory_space=pl.ANY)] * len(after),
        out_specs=[blk] * 3, out_shape=[jax.ShapeDtypeStruct((R, C), F32)] * 3,
        compiler_params=_params(("parallel",)),
    )(w, g, m, v, *after)


def _adamw_rs2(wmv, cs, direct, second, qidx, *, name):
    n = len(wmv)
    r, cc = wmv[0][0].shape
    h = r // 2

    def body(q_ref, *refs):
        ins, outs = refs[:6 * n], refs[6 * n:]
        for k in range(n):
            w_ref, m_ref, v_ref, c_ref, d1_ref, d2_ref = ins[6 * k:6 * k + 6]
            g_ref, d_ref, nm_ref, nv_ref = outs[4 * k:4 * k + 4]
            g = (c_ref[0].astype(F32) + d1_ref[0].astype(F32)) + d2_ref[0].astype(F32)
            g_ref[...] = g
            d_ref[...], nm_ref[...], nv_ref[...] = _adam_math(w_ref[...], g, m_ref[...], v_ref[...])

    blk = pl.BlockSpec((h, cc), lambda i, q_ref: (i, 0))
    one = [blk, blk, blk, pl.BlockSpec((1, h, cc), lambda i, q_ref: (q_ref[0], i, 0)),
           pl.BlockSpec((1, h, cc), lambda i, q_ref: (i, 0, 0)),
           pl.BlockSpec((1, h, cc), lambda i, q_ref: (1 - i, 0, 0))]
    out = pl.pallas_call(
        body, name=name,
        grid_spec=pltpu.PrefetchScalarGridSpec(num_scalar_prefetch=1, grid=(2,), in_specs=one * n,
                                               out_specs=[blk] * (4 * n)),
        out_shape=[jax.ShapeDtypeStruct((r, cc), F32)] * (4 * n),
        compiler_params=_params(("arbitrary",)),
    )(qidx, *[a for (w, m, v), c, d1, d2 in zip(wmv, cs, direct, second) for a in (w, m, v, c, d1, d2)])
    return [tuple(out[4 * k:4 * k + 4]) for k in range(n)]


def _adamw_rs(wmv, cs, rcv, qidx, *, name):
    n = len(wmv)
    shapes = [w.shape for w, _, _ in wmv]
    n_rcv = rcv[0].shape[0]
    halved = len(set(shapes)) == 1 and shapes[0][0] % 32 == 0 and shapes[0][0] > 128
    tiles = 2 if halved else 1

    def body(q_ref, *refs):
        ins, outs = refs[:5 * n], refs[5 * n:]
        for k in range(n):
            w_ref, m_ref, v_ref, c_ref, r_ref = ins[5 * k:5 * k + 5]
            g_ref, d_ref, nm_ref, nv_ref = outs[4 * k:4 * k + 4]
            g = c_ref[0].astype(F32)
            for j in range(n_rcv):
                g = g + r_ref[j].astype(F32)
            g_ref[...] = g
            d_ref[...], nm_ref[...], nv_ref[...] = _adam_math(w_ref[...], g, m_ref[...], v_ref[...])

    in_specs, out_specs = [], []
    for r, cc in shapes:
        blk = pl.BlockSpec((r // tiles, cc), lambda i, q_ref: (i, 0))
        in_specs += [blk, blk, blk, pl.BlockSpec((1, r // tiles, cc), lambda i, q_ref: (q_ref[0], i, 0)),
                     pl.BlockSpec((n_rcv, r // tiles, cc), lambda i, q_ref: (0, i, 0))]
        out_specs += [blk] * 4
    out = pl.pallas_call(
        body, name=name,
        grid_spec=pltpu.PrefetchScalarGridSpec(num_scalar_prefetch=1, grid=(tiles,), in_specs=in_specs,
                                               out_specs=out_specs),
        out_shape=[jax.ShapeDtypeStruct(s, F32) for s in shapes for _ in range(4)],
        compiler_params=_params(("arbitrary",)),
    )(qidx, *[a for (w, m, v), c, rc in zip(wmv, cs, rcv) for a in (w, m, v, c, rc)])
    return [tuple(out[4 * k:4 * k + 4]) for k in range(n)]


SMALL_PARAMS = ("norm_ffn1", "norm_mix", "norm_ffn2", "norm_final", "q_norm", "kv_norm", "sinks", "rel_bias", "b_mod")


def _adamw_small(gvec, wmv):
    shapes = [wmv[3 * i].shape for i in range(len(SMALL_PARAMS))]

    def body(*refs):
        g_all = refs[0]
        ins = refs[1:1 + 3 * len(SMALL_PARAMS)]
        outs = refs[1 + 3 * len(SMALL_PARAMS):]
        off = N_MODVEC
        for i, name in enumerate(SMALL_PARAMS):
            g_ref, d_ref, nm_ref, nv_ref = outs[4 * i:4 * i + 4]
            w_ref, m_ref, v_ref = ins[3 * i:3 * i + 3]
            start = 0 if name == "b_mod" else off
            rows, width = shapes[i]
            g = jnp.concatenate([g_all[:, start + width * r:start + width * (r + 1)] for r in range(rows)], axis=0)
            g_ref[...] = g
            d_ref[...], nm_ref[...], nv_ref[...] = _adam_math(w_ref[...], g, m_ref[...], v_ref[...])
            if name != "b_mod":
                off += dict(SMALL_LAYOUT)[name]

    vm = pl.BlockSpec(memory_space=pltpu.VMEM)
    n_out = 4 * len(SMALL_PARAMS)
    out = pl.pallas_call(
        body, name="adamw_small", in_specs=[vm] * (1 + len(wmv)), out_specs=[vm] * n_out,
        out_shape=[jax.ShapeDtypeStruct(shapes[i // 4], F32) for i in range(n_out)],
        compiler_params=_params(),
    )(gvec, *wmv)
    return {name: out[4 * i:4 * i + 4] for i, name in enumerate(SMALL_PARAMS)}


TRANSPOSED = ("g1T", "u1T", "g3T", "u3T", "w_inT", "w_uqT")


def kernel(x, c, w_mod, b_mod, norm_ffn1, ffn1_gate, ffn1_up, ffn1_down, norm_mix, w_in, q_norm, kv_norm, w_uq, w_ukv, sinks, w_o, norm_ffn2, ffn2_gate, ffn2_up, ffn2_down, rel_bias, norm_final, loss_target, m_w_mod, m_b_mod, m_norm_ffn1, m_ffn1_gate, m_ffn1_up, m_ffn1_down, m_norm_mix, m_w_in, m_q_norm, m_kv_norm, m_w_uq, m_w_ukv, m_sinks, m_w_o, m_norm_ffn2, m_ffn2_gate, m_ffn2_up, m_ffn2_down, m_rel_bias, m_norm_final, v_w_mod, v_b_mod, v_norm_ffn1, v_ffn1_gate, v_ffn1_up, v_ffn1_down, v_norm_mix, v_w_in, v_q_norm, v_kv_norm, v_w_uq, v_w_ukv, v_sinks, v_w_o, v_norm_ffn2, v_ffn2_gate, v_ffn2_up, v_ffn2_down, v_rel_bias, v_norm_final):
    mx, my, mc = _coords()
    cidx = jnp.reshape(mc, (1,)).astype(jnp.int32)
    qidx = jnp.reshape(2 * mx + my, (1,)).astype(jnp.int32)
    WM = w_mod.shape[2]

    c_tile = jnp.pad(c, ((0, 7), (0, 0)))
    b_mod3 = jnp.pad(b_mod.reshape(N_DEV, 1, WM), ((0, 0), (0, 7), (0, 0)))
    mod3, ca = _mod_fwd(c_tile, w_mod[0], b_mod3)
    mod9 = mod3[:, 0, :].reshape(N_MOD, D)

    shards = {"g1T": ffn1_gate[0].T.astype(BF16), "u1T": ffn1_up[0].T.astype(BF16), "d1": ffn1_down[0].astype(BF16),
              "g3T": ffn2_gate[0].T.astype(BF16), "u3T": ffn2_up[0].T.astype(BF16), "d3": ffn2_down[0].astype(BF16),
              "w_inT": w_in[0].T, "w_uqT": w_uq[0].T.astype(BF16), "w_ukv": w_ukv[0].astype(BF16),
              "w_o": w_o[0].astype(BF16)}
    me = 4 * mx + 2 * my + mc
    groups = {"ffn1": ("g1T", "u1T", "d1"), "mixer": ("w_inT", "w_uqT", "w_ukv", "w_o"), "ffn2": ("g3T", "u3T", "d3")}
    arriving = {}

    def as_weights(group, gathered):
        return {k: g if k == "w_ukv" else g.reshape(N_DEV * g.shape[1], g.shape[2])
                for k, g in zip(groups[group], gathered)}

    later = groups["mixer"] + groups["ffn2"]
    place = {"mixer": 0, "ffn2": len(groups["mixer"])}

    def start_gather(token):
        lands = []
        for k in later:
            sh = shards[k] + token[0, 0].astype(shards[k].dtype)
            lands.append(lax.dynamic_update_slice(lax.empty((N_DEV,) + sh.shape, sh.dtype), sh[None], (me, 0, 0)))
        batches = [range(k0, k0 + len(groups[group])) for group, k0 in place.items()]
        send, recv, lands, started = _gather_start(lands, name="gather_start", batches=batches)
        for group, k0 in place.items():
            arriving[group] = (send, recv, lands[k0:k0 + len(groups[group])])
        return started

    def fetch(group, after, vecs):
        if group == "ffn1":
            *gathered, token = _wgather([shards[k] + ca[1, 0].astype(shards[k].dtype) for k in groups["ffn1"]])
            return as_weights("ffn1", gathered), vecs + start_gather(token)[0:1, 0:1]

        def pass_on(group, after):
            send, recv, lands = arriving[group]
            lands, token = _gather_pass(send, recv, lands, after, name="gather_landed_" + group, stage="landed",
                                        k0=place[group])
            lands, token = _gather_pass(send, recv, lands, [token], name="gather_onward_" + group, stage="onward",
                                        k0=place[group])
            arriving[group] = (send, recv, lands)
            return token

        if group == "ffn2_on_its_way":
            return None, vecs + pass_on("ffn2", after)[0:1, 0:1]
        if group == "mixer":
            after = [pass_on("mixer", after)]
        send, recv, lands = arriving[group]
        return as_weights(group, _gather_end(send, recv, lands, after, name="gather_end_" + group,
                                             k0=place[group])), vecs

    norms ={"ffn1": norm_ffn1, "mix": norm_mix, "ffn2": norm_ffn2, "final": norm_final.reshape(1, D)}
    in_flight = {}

    def on_grads(group, g, after, vecs, before_ici=()):
        if group != "ffn1":
            if g is None:
                return vecs
            names = list(g)
            by_dest = [g[k] if k == "w_ukv" else g[k].reshape((N_DEV, g[k].shape[0] // N_DEV) + g[k].shape[1:])
                       for k in names]
            lands = [lax.empty((N_DEV - 1,) + a.shape[1:], a.dtype) for a in by_dest]
            send, recv, by_dest, lands, token = _split_start(_direct_copies, by_dest, lands, 7 * len(names), after,
                                                             name="rs_start_" + group)
            in_flight[group] = (names, send, recv, by_dest, lands, token)
            return vecs + token[0:1, 0:1]
        if g is not None:
            names = list(g)
            by_dest = [g[k] if k == "w_ukv" else g[k].reshape((N_DEV, g[k].shape[0] // N_DEV) + g[k].shape[1:])
                       for k in names]
            lands = [lax.empty((4,) + a.shape[1:], a.dtype) for a in by_dest]
            send, recv, by_dest, lands, token = _split_start(_d2d_copies, by_dest, lands, 4 * len(names), after,
                                                             name="rs_d2d_start_" + group)
            in_flight[group] = (names, send, recv, by_dest, lands)
            return vecs + token[0:1, 0:1]
        names, send, recv, by_dest, lands = in_flight[group]
        by_dest, from_sib = _split_wait(_d2d_copies, send, recv, by_dest, lands, after, name="rs_d2d_wait_" + group)
        sums = _chipsum(by_dest, from_sib, cidx, name="chipsum_" + group)
        halves = lambda: [lax.empty((2, s.shape[1] // 2) + s.shape[2:], s.dtype) for s in sums]
        send, recv, sums, lands, token = _split_start(_rs_step1_copies, sums, halves() + halves(), 4 * len(names),
                                                      list(before_ici), name="rs_ici_start_" + group)
        in_flight[group] = (names, send, recv, sums, lands, token)
        return vecs + token[0:1, 0:1]

    _, grad_x, _, vec = _local_step(
        x[0], loss_target[0], mod9, norms, sinks, rel_bias, q_norm, kv_norm, fetch, on_grads=on_grads)

    vec = vec.reshape(1, 1, N_VEC)
    allvec = lax.dynamic_update_slice(lax.empty((N_DEV, 1, N_VEC), F32), vec, (me, 0, 0))
    vsend, vrecv, _, (allvec,), vec_started = _split_start(_vec_copies, [], [allvec], N_DEV - 1, [], name="vec_start")
    on_grads("ffn1", None, [grad_x], jnp.zeros((1, 1), F32), before_ici=[vec_started])

    owners = {"g1T": ("ffn1_gate", ffn1_gate, m_ffn1_gate, v_ffn1_gate), "u1T": ("ffn1_up", ffn1_up, m_ffn1_up, v_ffn1_up),
              "d1": ("ffn1_down", ffn1_down, m_ffn1_down, v_ffn1_down),
              "g3T": ("ffn2_gate", ffn2_gate, m_ffn2_gate, v_ffn2_gate), "u3T": ("ffn2_up", ffn2_up, m_ffn2_up, v_ffn2_up),
              "d3": ("ffn2_down", ffn2_down, m_ffn2_down, v_ffn2_down),
              "w_inT": ("w_in", w_in, m_w_in, v_w_in), "w_uqT": ("w_uq", w_uq, m_w_uq, v_w_uq),
              "w_ukv": ("w_ukv", w_ukv, m_w_ukv, v_w_ukv), "w_o": ("w_o", w_o, m_w_o, v_w_o)}
    res, done = {}, []

    def finish(group, after, behind_step2=None):
        names, send, recv, sums, lands, _ = in_flight[group]
        there = lambda k, a: a[0].T if k in TRANSPOSED else a[0]
        back = lambda k, a: a.T[None] if k in TRANSPOSED else a[None]
        wmv = [tuple(there(k, a) for a in owners[k][1:]) for k in names]
        if behind_step2 is not None:
            n = len(names)
            sums, lands = _split_wait(_rs_step1_copies, send, recv, sums, lands, after, name="rs_ici_wait_" + group)
            direct, relay = lands[:n], lands[n:]
            qxy = jnp.stack([2 * (1 - mx) + my, 2 * mx + (1 - my)]).astype(jnp.int32)
            relayed = _relay_sum(sums, relay, qxy, name="relay_sum_" + group)
            second = [lax.empty(a.shape, a.dtype) for a in relayed]
            send, recv, relayed, second, token = _split_start(_rs_step2_copies, relayed, second, 2 * n, [],
                                                              name="rs_ici_start2_" + group)
            after = behind_step2(token)
            outs = []
            for i, k in enumerate(names):
                _, (sec,) = _split_wait(functools.partial(_rs_step2_copies, k0=i), send, recv, [relayed[i]],
                                        [second[i]], after, name="rs_ici_wait2_" + k)
                outs.append(_adamw_rs2([wmv[i]], [sums[i]], [direct[i]], [sec], qidx,
                                       name="adamw_" + owners[k][0])[0])
                after = [outs[-1][3]]
        else:
            own = jnp.reshape(me, (1,)).astype(jnp.int32)
            sums, lands = _split_wait(_direct_copies, send, recv, sums, lands, after, name="rs_wait_" + group)
            outs = _adamw_rs(wmv, sums, lands, own, name="adamw_" + group)
        for k, out in zip(names, outs):
            done.append(out[3])
            res[owners[k][0]] = tuple(back(k, a) for a in out)

    ffn1_started = in_flight["ffn1"][5]
    finish("ffn2", [ffn1_started])
    finish("mixer", [ffn1_started])

    _, (allvec,) = _split_wait(_vec_copies, vsend, vrecv, [], [allvec], [ffn1_started], name="vec_wait")
    g_wmod, gvec = _mod_bwd(allvec, ca, jnp.reshape(me, (1,)).astype(jnp.int32))
    loss = gvec[0, N_MODVEC + LOSS_SLOT]
    small_in = {"norm_ffn1": (norm_ffn1, m_norm_ffn1, v_norm_ffn1), "norm_mix": (norm_mix, m_norm_mix, v_norm_mix),
                "norm_ffn2": (norm_ffn2, m_norm_ffn2, v_norm_ffn2), "norm_final": (norm_final, m_norm_final, v_norm_final),
                "q_norm": (q_norm, m_q_norm, v_q_norm), "kv_norm": (kv_norm, m_kv_norm, v_kv_norm),
                "sinks": (sinks, m_sinks, v_sinks), "rel_bias": (rel_bias, m_rel_bias, v_rel_bias),
                "b_mod": (b_mod, m_b_mod, v_b_mod)}
    as_row = lambda k, a: a.T if k == "rel_bias" else a.reshape(1, -1)
    from_row = lambda k, a: a.T if k == "rel_bias" else a.reshape(small_in[k][0].shape)
    small_out = _adamw_small(gvec, [as_row(k, a) for k in SMALL_PARAMS for a in small_in[k]])
    for k in SMALL_PARAMS:
        res[k] = tuple(from_row(k, a) for a in small_out[k])

    def update_w_mod(step2_started):
        out = _adamw(w_mod[0], g_wmod, m_w_mod[0], v_w_mod[0], name="adamw_w_mod", after=[step2_started])
        res["w_mod"] = tuple(a[None] for a in (g_wmod,) + tuple(out))
        return [out[2]]

    finish("ffn1", done + [a for k in SMALL_PARAMS for a in res[k]], behind_step2=update_w_mod)

    order = ("w_mod", "b_mod", "norm_ffn1", "ffn1_gate", "ffn1_up", "ffn1_down", "norm_mix", "w_in", "q_norm",
             "kv_norm", "w_uq", "w_ukv", "sinks", "w_o", "norm_ffn2", "ffn2_gate", "ffn2_up", "ffn2_down",
             "rel_bias", "norm_final")
    return (loss, grad_x[None]) + tuple(res[nm][kind] for kind in range(4) for nm in order)
```

```python
import functools
import math

import numpy as np
import jax
import jax.numpy as jnp
from jax import lax
from jax.experimental import pallas as pl
from jax.experimental.pallas import tpu as pltpu

F32 = jnp.float32
BF16 = jnp.bfloat16
MESH = pl.DeviceIdType.MESH

N_DEV = 8
D = 1024
D_FF = 2816
EPS = 1e-6
N_MOD = 9
SWA_HEADS = 8
SWA_DH = 64
WINDOW = 128
MLA_HEADS = 4
MLA_NOPE = 128
MLA_ROPE = 64
MLA_V = 128
MLA_QR = 256
MLA_KVR = 128
ROPE_THETA = 10000.0
NUM_BUCKETS = 32
D_IN = 1216
D_IN_PAD = 1280
SWA_SCALE = SWA_DH ** -0.5
MLA_SCALE = (MLA_NOPE + MLA_ROPE) ** -0.5

ADAM_LR = 0.001
ADAM_B1 = 0.9
ADAM_B2 = 0.999
ADAM_EPS = 1e-08
ADAM_WD = 0.01
ADAM_STEP = 10

V7X_VMEM_LIMIT = 56 * 1024 * 1024
ROW_TILE = 512

NT_DIMS = (((1,), (1,)), ((), ()))
TN_DIMS = (((0,), (0,)), ((), ()))


def _dot(a, b):
    return jnp.dot(a, b, preferred_element_type=F32)


def _dot_nt(a, b):
    return lax.dot_general(a, b, NT_DIMS, preferred_element_type=F32)


def _dot_tn(a, b):
    return lax.dot_general(a, b, TN_DIMS, preferred_element_type=F32)


def _params(sem=None):
    return pltpu.CompilerParams(dimension_semantics=sem, vmem_limit_bytes=V7X_VMEM_LIMIT)


def _rstd(x):
    return lax.rsqrt(jnp.mean(x * x, axis=-1, keepdims=True) + EPS)


def _rms_bwd(dy, xhat, r):
    return r * (dy - xhat * jnp.mean(dy * xhat, axis=-1, keepdims=True))


def _sigmoid(a):
    return 1.0 / (1.0 + jnp.exp(-a))


def _ffn_fwd(x, vecs, wgT, wuT, wd, *, name, tm=256, tf=D_FF):
    S, F = x.shape[0], wd.shape[0]
    tm = min(tm, S)
    ni, nj = S // tm, F // tf

    def body(x_ref, vec_ref, wg_ref, wu_ref, wd_ref, xo_ref, h_ref, a_ref, b_ref, f_ref, acc_ref):
        j = pl.program_id(1)

        @pl.when(j == 0)
        def _():
            xv = x_ref[...]
            hn = xv * _rstd(xv) * vec_ref[0:1, :]
            h_ref[...] = (hn * (1.0 + vec_ref[2:3, :]) + vec_ref[1:2, :]).astype(BF16)

        h = h_ref[...]
        a = _dot_nt(h, wg_ref[...])
        b = _dot_nt(h, wu_ref[...])
        a_ref[...] = a.astype(BF16)
        b_ref[...] = b.astype(BF16)
        part = _dot((a * _sigmoid(a) * b).astype(BF16), wd_ref[...])

        def finish(f):
            f_ref[...] = f
            xo_ref[...] = x_ref[...] + (0.5 * vec_ref[3:4, :]) * f

        if nj == 1:
            finish(part)
        else:
            @pl.when(j == 0)
            def _():
                acc_ref[...] = part

            @pl.when((j > 0) & (j < nj - 1))
            def _():
                acc_ref[...] += part

            @pl.when(j == nj - 1)
            def _():
                finish(acc_ref[...] + part)

    row = pl.BlockSpec((tm, D), lambda i, j: (i, 0))
    wspec = pl.BlockSpec((tf, D), lambda i, j: (j, 0), pipeline_mode=pl.Buffered(1) if nj == 1 else None)
    act = pl.BlockSpec((tm, tf), lambda i, j: (i, j))
    return pl.pallas_call(
        body, name=name, grid=(ni, nj),
        in_specs=[row, pl.BlockSpec((8, D), lambda i, j: (0, 0)), wspec, wspec, wspec],
        out_specs=[row, row, act, act, row],
        out_shape=[jax.ShapeDtypeStruct((S, D), F32), jax.ShapeDtypeStruct((S, D), BF16),
                   jax.ShapeDtypeStruct((S, F), BF16), jax.ShapeDtypeStruct((S, F), BF16),
                   jax.ShapeDtypeStruct((S, D), F32)],
        scratch_shapes=[pltpu.VMEM((tm, D) if nj > 1 else (8, 128), F32)],
        compiler_params=_params(("parallel", "arbitrary")),
    )(x, vecs, wgT, wuT, wd)


def _ffn_bwd_main(h, df, a, b, wgT, wuT, wd, *, name, after=(), tm=2048, tf=256):
    S = h.shape[0]
    tm = min(tm, S)
    ni, nj = S // tm, D_FF // tf

    def body(h_hbm, df_hbm, a_ref, b_ref, wg_ref, wu_ref, wd_ref, *rest):
        gg_ref, gu_ref, gd_ref, dh_hbm, h_v, df_v, dh_v, gg_acc, gu_acc, gd_acc, sem = rest[len(after):]
        j = pl.program_id(0)
        i = pl.program_id(1)

        @pl.when((j == 0) & (i == 0))
        def _():
            c1 = pltpu.make_async_copy(h_hbm, h_v, sem.at[0])
            c2 = pltpu.make_async_copy(df_hbm, df_v, sem.at[1])
            c1.start()
            c2.start()
            c1.wait()
            c2.wait()

        @pl.when(i == 0)
        def _():
            gg_acc[...] = jnp.zeros_like(gg_acc)
            gu_acc[...] = jnp.zeros_like(gu_acc)
            gd_acc[...] = jnp.zeros_like(gd_acc)

        rows = pl.ds(pl.multiple_of(i * tm, tm), tm)
        hi = h_v[rows, :]
        dfi = df_v[rows, :]
        av = a_ref[...].astype(F32)
        bv = b_ref[...].astype(F32)
        sg = _sigmoid(av)
        sa = av * sg
        hsw = (sa * bv).astype(BF16)
        dhsw = _dot_nt(dfi, wd_ref[...])
        da = (dhsw * bv * (sg * (1.0 + av * (1.0 - sg)))).astype(BF16)
        db = (dhsw * sa).astype(BF16)
        gd_acc[...] += _dot_tn(hsw, dfi)
        gg_acc[...] += _dot_tn(da, hi)
        gu_acc[...] += _dot_tn(db, hi)
        dh = _dot(da, wg_ref[...]) + _dot(db, wu_ref[...])

        @pl.when(j == 0)
        def _():
            dh_v[rows, :] = dh

        @pl.when(j > 0)
        def _():
            dh_v[rows, :] += dh

        @pl.when(i == ni - 1)
        def _():
            gg_ref[...] = gg_acc[...].astype(BF16)
            gu_ref[...] = gu_acc[...].astype(BF16)
            gd_ref[...] = gd_acc[...].astype(BF16)

        @pl.when((j == nj - 1) & (i == ni - 1))
        def _():
            c3 = pltpu.make_async_copy(dh_v, dh_hbm, sem.at[2])
            c3.start()
            c3.wait()

    anyspec = pl.BlockSpec(memory_space=pl.ANY)
    wspec = pl.BlockSpec((tf, D), lambda j, i: (j, 0))
    act = pl.BlockSpec((tm, tf), lambda j, i: (i, j))
    return pl.pallas_call(
        body, name=name, grid=(nj, ni),
        in_specs=[anyspec, anyspec, act, act, wspec, wspec, wspec] + [anyspec] * len(after),
        out_specs=[wspec, wspec, wspec, anyspec],
        out_shape=[jax.ShapeDtypeStruct((D_FF, D), BF16)] * 3 + [jax.ShapeDtypeStruct((S, D), F32)],
        scratch_shapes=[pltpu.VMEM((S, D), BF16), pltpu.VMEM((S, D), BF16), pltpu.VMEM((S, D), F32),
                        pltpu.VMEM((tf, D), F32), pltpu.VMEM((tf, D), F32), pltpu.VMEM((tf, D), F32),
                        pltpu.SemaphoreType.DMA((3,))],
        compiler_params=_params(("arbitrary", "arbitrary")),
    )(h, df, a, b, wgT, wuT, wd, *after)


def _ffn_out_bwd(dx, f, gate, df_ref, part_ref):
    df_ref[...] = ((0.5 * gate) * dx).astype(BF16)
    part_ref[3:4, :] += 0.5 * jnp.sum(dx * f, axis=0, keepdims=True)


def _norm_bwd(dh, x, dxo, vecs, *, name, below=None, tm=ROW_TILE):
    S = x.shape[0]
    tm = min(tm, S)

    def body(dh_ref, x_ref, dxo_ref, vec_ref, *rest):
        dx_ref, part_ref = rest[-2 if below is None else -3], rest[-1 if below is None else -2]

        @pl.when(pl.program_id(0) == 0)
        def _():
            part_ref[...] = jnp.zeros_like(part_ref)

        dh = dh_ref[...]
        xv = x_ref[...]
        r = _rstd(xv)
        xhat = xv * r
        w = vec_ref[0:1, :]
        xn = xhat * w
        dxn = dh * (1.0 + vec_ref[2:3, :])
        part_ref[0:1, :] += jnp.sum(dxn * xhat, axis=0, keepdims=True)
        part_ref[1:2, :] += jnp.sum(dh, axis=0, keepdims=True)
        part_ref[2:3, :] += jnp.sum(dh * xn, axis=0, keepdims=True)
        dx = dxo_ref[...] + _rms_bwd(dxn * w, xhat, r)
        dx_ref[...] = dx
        if below is not None:
            _ffn_out_bwd(dx, rest[0][...], rest[1][3:4, :], rest[-1], part_ref)

    row = pl.BlockSpec((tm, D), lambda i: (i, 0))
    vec = pl.BlockSpec((8, D), lambda i: (0, 0))
    extra = [] if below is None else [row, vec]
    return pl.pallas_call(
        body, name=name, grid=(S // tm,), in_specs=[row, row, row, vec] + extra,
        out_specs=[row, vec] + ([] if below is None else [row]),
        out_shape=[jax.ShapeDtypeStruct((S, D), F32), jax.ShapeDtypeStruct((8, D), F32)]
        + ([] if below is None else [jax.ShapeDtypeStruct((S, D), BF16)]),
        compiler_params=_params(("arbitrary",)),
    )(dh, x, dxo, vecs, *([] if below is None else below))


def _head(x, tgt, nf, f, vecs, *, tm=ROW_TILE):
    S = x.shape[0]
    tm = min(tm, S)

    def body(x_ref, t_ref, nf_ref, f_ref, vec_ref, dx_ref, part_ref, df_ref):
        @pl.when(pl.program_id(0) == 0)
        def _():
            part_ref[...] = jnp.zeros_like(part_ref)

        xv = x_ref[...]
        r = _rstd(xv)
        xhat = xv * r
        w = nf_ref[...]
        e = xhat * w - t_ref[...]
        dy = e * (1.0 / D)
        part_ref[0:1, :] += jnp.sum(dy * xhat, axis=0, keepdims=True)
        part_ref[1:2, :] += jnp.sum(e * e) * (0.5 / D)
        dx = _rms_bwd(dy * w, xhat, r)
        dx_ref[...] = dx
        _ffn_out_bwd(dx, f_ref[...], vec_ref[3:4, :], df_ref, part_ref)

    row = pl.BlockSpec((tm, D), lambda i: (i, 0))
    vec = pl.BlockSpec((8, D), lambda i: (0, 0))
    return pl.pallas_call(
        body, name="head", grid=(S // tm,),
        in_specs=[row, row, pl.BlockSpec((1, D), lambda i: (0, 0)), row, vec],
        out_specs=[row, vec, row],
        out_shape=[jax.ShapeDtypeStruct((S, D), F32), jax.ShapeDtypeStruct((8, D), F32),
                   jax.ShapeDtypeStruct((S, D), BF16)],
        compiler_params=_params(("arbitrary",)),
    )(x, tgt, nf, f, vecs)


def _mix_in_fwd(x, vecs, w_inT, *, tm=ROW_TILE):
    S = x.shape[0]
    tm = min(tm, S)

    def body(x_ref, vec_ref, w_ref, h_ref, p_ref):
        xv = x_ref[...]
        hn = xv * _rstd(xv) * vec_ref[0:1, :]
        h = (hn * (1.0 + vec_ref[2:3, :]) + vec_ref[1:2, :]).astype(BF16)
        h_ref[...] = h
        p_ref[...] = _dot_nt(h, w_ref[...])

    row = pl.BlockSpec((tm, D), lambda i: (i, 0))
    return pl.pallas_call(
        body, name="mix_in_fwd", grid=(S // tm,),
        in_specs=[row, pl.BlockSpec((8, D), lambda i: (0, 0)), pl.BlockSpec((D_IN_PAD, D), lambda i: (0, 0))],
        out_specs=[row, pl.BlockSpec((tm, D_IN_PAD), lambda i: (i, 0))],
        out_shape=[jax.ShapeDtypeStruct((S, D), BF16), jax.ShapeDtypeStruct((S, D_IN_PAD), F32)],
        compiler_params=_params(("parallel",)),
    )(x, vecs, w_inT)


def _bucket_table():
    qi = np.arange(WINDOW)[:, None]
    kj = np.arange(2 * WINDOW)[None, :]
    dist = qi + WINDOW - kj
    max_exact = NUM_BUCKETS // 2
    n = np.maximum(dist, 0)
    nf = np.maximum(n, 1).astype(np.float32)
    large = max_exact + (np.log(nf / np.float32(max_exact)) / np.float32(math.log(WINDOW / max_exact))
                         * np.float32(NUM_BUCKETS - max_exact)).astype(np.int32)
    large = np.minimum(large, NUM_BUCKETS - 1)
    return np.where(n < max_exact, n, large).astype(np.int32)


def _bias_build(rel_bias, bucket):
    def body(rb_ref, bk_ref, out_ref):
        bk = bk_ref[...]
        for h in range(SWA_HEADS):
            acc = jnp.zeros((WINDOW, 2 * WINDOW), F32)
            for b in range(NUM_BUCKETS):
                acc = jnp.where(bk == b, rb_ref[b, h], acc)
            out_ref[h] = acc

    return pl.pallas_call(
        body, name="bias_build",
        in_specs=[pl.BlockSpec(memory_space=pltpu.SMEM), pl.BlockSpec(memory_space=pltpu.VMEM)],
        out_specs=pl.BlockSpec(memory_space=pltpu.VMEM),
        out_shape=jax.ShapeDtypeStruct((SWA_HEADS, WINDOW, 2 * WINDOW), F32),
    )(rel_bias, bucket)


SWA_GROUP = 4
GROUP_ROWS = SWA_GROUP * WINDOW


SWA_SUB = 2


def _swa_valid(has_prev):
    row = lax.broadcasted_iota(jnp.int32, (GROUP_ROWS, 2 * WINDOW), 0) % WINDOW
    col = lax.broadcasted_iota(jnp.int32, (GROUP_ROWS, 2 * WINDOW), 1)
    dist = row + WINDOW - col
    return (dist >= 0) & (dist < WINDOW) & ((col >= WINDOW) | has_prev)


def _swa_keys(prev_ref, cur_ref, u):
    cur = cur_ref[...]
    before = prev_ref[...] if u == 0 else cur[WINDOW * (u - 1):WINDOW * u]
    return jnp.concatenate([before, cur[WINDOW * u:WINDOW * (u + 1)]], axis=0).astype(BF16)


def _stack_heads(x, g):
    return jnp.concatenate([x[:, 64 * h:64 * h + 64] for h in range(SWA_GROUP * g, SWA_GROUP * (g + 1))], axis=0)


def _unstack_heads(x4):
    return jnp.concatenate([x4[WINDOW * a:WINDOW * (a + 1)] for a in range(SWA_GROUP)], axis=1)


def _group_sinks(sink_ref, g):
    head = lax.broadcasted_iota(jnp.int32, (GROUP_ROWS, 1), 0) // WINDOW
    out = jnp.full((GROUP_ROWS, 1), sink_ref[0, SWA_GROUP * g], F32)
    for a in range(1, SWA_GROUP):
        out = jnp.where(head == a, sink_ref[0, SWA_GROUP * g + a], out)
    return out


def _swa_probs(qh, kk, bias_h, sink, valid):
    s = _dot_nt(qh, kk) * SWA_SCALE + bias_h
    s = jnp.where(valid, s, -jnp.inf)
    m = jnp.maximum(jnp.max(s, axis=-1, keepdims=True), sink)
    p = jnp.exp(s - m)
    ps = jnp.exp(sink - m)
    inv = 1.0 / (jnp.sum(p, axis=-1, keepdims=True) + ps)
    return p * inv, ps * inv


SWA_ROWS = SWA_SUB * WINDOW


def _swa_specs():
    prev = lambda n: jnp.maximum(SWA_SUB * n - 1, 0)
    return [pl.BlockSpec((SWA_ROWS, 512), lambda n: (n, 0)),
            pl.BlockSpec((SWA_ROWS, 128), lambda n: (n, 4)),
            pl.BlockSpec((WINDOW, 128), lambda n: (prev(n), 4)),
            pl.BlockSpec((SWA_ROWS, 128), lambda n: (n, 5)),
            pl.BlockSpec((WINDOW, 128), lambda n: (prev(n), 5)),
            pl.BlockSpec((SWA_HEADS, WINDOW, 2 * WINDOW), lambda n: (0, 0, 0)),
            pl.BlockSpec(memory_space=pltpu.SMEM)]


def _swa_fwd(proj, bias, sinks):
    S = proj.shape[0]

    def body(q_ref, kc_ref, kp_ref, vc_ref, vp_ref, bias_ref, sink_ref, o_ref):
        n = pl.program_id(0)
        for u in range(SWA_SUB):
            rows = slice(WINDOW * u, WINDOW * (u + 1))
            valid = _swa_valid(n > 0 if u == 0 else True)
            q = q_ref[rows, :].astype(BF16)
            kfull = _swa_keys(kp_ref, kc_ref, u)
            vfull = _swa_keys(vp_ref, vc_ref, u)
            for g in range(SWA_HEADS // SWA_GROUP):
                kk = kfull[:, 64 * g:64 * g + 64]
                vv = vfull[:, 64 * g:64 * g + 64]
                bias4 = bias_ref[SWA_GROUP * g:SWA_GROUP * (g + 1)].reshape(GROUP_ROWS, 2 * WINDOW)
                pk, _ = _swa_probs(_stack_heads(q, g), kk, bias4, _group_sinks(sink_ref, g), valid)
                o_ref[rows, 256 * g:256 * (g + 1)] = _unstack_heads(_dot(pk.astype(BF16), vv))

    return pl.pallas_call(
        body, name="swa_fwd", grid=(S // SWA_ROWS,),
        in_specs=_swa_specs(),
        out_specs=pl.BlockSpec((SWA_ROWS, 512), lambda n: (n, 0)),
        out_shape=jax.ShapeDtypeStruct((S, 512), F32),
        compiler_params=_params(("parallel",)),
    )(proj, proj, proj, proj, proj, bias, sinks)


def _swa_bwd(proj, bias, sinks, o, do, bucket):
    S = proj.shape[0]
    nb = S // SWA_ROWS

    def body(q_ref, kc_ref, kp_ref, vc_ref, vp_ref, bias_ref, sink_ref, o_ref, do_ref, bk_ref,
             dq_ref, dk_ref, dv_ref, drb_ref, dsk_ref, dbias_acc):
        n = pl.program_id(0)

        @pl.when(n == 0)
        def _():
            dk_ref[...] = jnp.zeros_like(dk_ref)
            dv_ref[...] = jnp.zeros_like(dv_ref)
            dsk_ref[...] = jnp.zeros_like(dsk_ref)
            dbias_acc[...] = jnp.zeros_like(dbias_acc)
            drb_ref[...] = jnp.zeros_like(drb_ref)

        for u in range(SWA_SUB):
            rows = slice(WINDOW * u, WINDOW * (u + 1))
            blk = SWA_SUB * n + u
            valid = _swa_valid(n > 0 if u == 0 else True)
            q = q_ref[rows, :].astype(BF16)
            dov = do_ref[rows, :]
            ov = o_ref[rows, :]
            kfull = _swa_keys(kp_ref, kc_ref, u)
            vfull = _swa_keys(vp_ref, vc_ref, u)
            prow = pl.ds(pl.multiple_of(jnp.maximum(blk - 1, 0) * WINDOW, WINDOW), WINDOW)
            crow = pl.ds(pl.multiple_of(blk * WINDOW, WINDOW), WINDOW)
            for g in range(SWA_HEADS // SWA_GROUP):
                heads = slice(SWA_GROUP * g, SWA_GROUP * (g + 1))
                kk = kfull[:, 64 * g:64 * g + 64]
                vv = vfull[:, 64 * g:64 * g + 64]
                q4 = _stack_heads(q, g)
                pk, psink = _swa_probs(q4, kk, bias_ref[heads].reshape(GROUP_ROWS, 2 * WINDOW),
                                       _group_sinks(sink_ref, g), valid)
                pkb = pk.astype(BF16)
                do4 = _stack_heads(dov, g)
                dob = do4.astype(BF16)
                dp = _dot_nt(dob, vv)
                delta = jnp.sum(do4 * _stack_heads(ov, g), axis=-1, keepdims=True)
                ds = pk * (dp - delta)
                dsink = -psink * delta
                for a in range(SWA_GROUP):
                    h = SWA_GROUP * g + a
                    part = jnp.sum(dsink[WINDOW * a:WINDOW * (a + 1)], keepdims=True)
                    dsk_ref[h:h + 1, :] += jnp.broadcast_to(part, (1, 128))
                dbias_acc[heads] += ds.reshape(SWA_GROUP, WINDOW, 2 * WINDOW)
                dsb = (ds * SWA_SCALE).astype(BF16)
                dq_ref[rows, 256 * g:256 * (g + 1)] = _unstack_heads(_dot(dsb, kk))
                dkk = _dot_tn(dsb, q4)
                dvv = _dot_tn(pkb, dob)
                dk_ref[prow, 64 * g:64 * g + 64] += dkk[:WINDOW]
                dk_ref[crow, 64 * g:64 * g + 64] += dkk[WINDOW:]
                dv_ref[prow, 64 * g:64 * g + 64] += dvv[:WINDOW]
                dv_ref[crow, 64 * g:64 * g + 64] += dvv[WINDOW:]

        @pl.when(n == nb - 1)
        def _():
            bk = bk_ref[...]
            for h in range(SWA_HEADS):
                dbh = dbias_acc[h]
                for b in range(NUM_BUCKETS):
                    val = jnp.sum(jnp.where(bk == b, dbh, 0.0), keepdims=True)
                    row = h * NUM_BUCKETS + b
                    drb_ref[row:row + 1, :] = jnp.broadcast_to(val, (1, 128))

    full = lambda shape: pl.BlockSpec(shape, lambda n: tuple(0 for _ in shape))
    return pl.pallas_call(
        body, name="swa_bwd", grid=(nb,),
        in_specs=_swa_specs() + [pl.BlockSpec((SWA_ROWS, 512), lambda n: (n, 0)),
                                 pl.BlockSpec((SWA_ROWS, 512), lambda n: (n, 0)), full((WINDOW, 2 * WINDOW))],
        out_specs=[pl.BlockSpec((SWA_ROWS, 512), lambda n: (n, 0)), full((S, 128)), full((S, 128)),
                   full((NUM_BUCKETS * 8, 128)), full((8, 128))],
        out_shape=[jax.ShapeDtypeStruct((S, 512), F32), jax.ShapeDtypeStruct((S, 128), F32),
                   jax.ShapeDtypeStruct((S, 128), F32), jax.ShapeDtypeStruct((NUM_BUCKETS * 8, 128), F32),
                   jax.ShapeDtypeStruct((8, 128), F32)],
        scratch_shapes=[pltpu.VMEM((SWA_HEADS, WINDOW, 2 * WINDOW), F32)],
        compiler_params=_params(("arbitrary",)),
    )(proj, proj, proj, proj, proj, bias, sinks, o, do, bucket)


def _rope_tables(S):
    inv = np.float32(ROPE_THETA) ** (-np.arange(0, MLA_ROPE, 2, dtype=np.float32) / np.float32(MLA_ROPE))
    ang = np.arange(S, dtype=np.float32)[:, None] * inv[None, :]
    cos, sin = np.cos(ang), np.sin(ang)
    return (jnp.asarray(np.tile(np.concatenate([cos, cos], axis=1), (1, 2))),
            jnp.asarray(np.tile(np.concatenate([-sin, sin], axis=1), (1, 2))))


def _rope_wide(ref):
    t = ref[...]
    return jnp.concatenate([t, t], axis=1)


def _swap_halves(x):
    w = x.shape[-1]
    lane = lax.broadcasted_iota(jnp.int32, x.shape, x.ndim - 1)
    return jnp.where((lane % 64) < 32, pltpu.roll(x, w - 32, x.ndim - 1), pltpu.roll(x, 32, x.ndim - 1))


def _mla_pre_fwd(proj, qn_w, kvn_w, wuqT, wukv, cos, sin, *, tm=ROW_TILE):
    S = proj.shape[0]
    tm = min(tm, S)

    def body(ql_ref, kl_ref, kr_ref, qw_ref, kw_ref, wuq_ref, wukv_ref, cos_ref, sin_ref,
             qc_ref, kc_ref, vv_ref):
        ql = ql_ref[...]
        qn = (ql * _rstd(ql) * qw_ref[...]).astype(BF16)
        q = _dot_nt(qn, wuq_ref[...])
        cs, sn = _rope_wide(cos_ref), _rope_wide(sin_ref)
        qr = q[:, 512:768]
        qr = qr * cs + _swap_halves(qr) * sn
        half = lax.broadcasted_iota(jnp.int32, (tm, 128), 1) // 64
        kl = kl_ref[...]
        kvn = (kl * _rstd(kl) * kw_ref[...]).astype(BF16)
        kr = kr_ref[...]
        kr = kr * cs[:, :128] + _swap_halves(kr) * sn[:, :128]
        kr2 = (kr + pltpu.roll(kr, 64, 1)).astype(BF16)
        for h in range(MLA_HEADS):
            qc_ref[h, :, 0:128] = q[:, 128 * h:128 * h + 128].astype(BF16)
            chunk = qr[:, 128 * (h // 2):128 * (h // 2) + 128]
            qc_ref[h, :, 128:256] = jnp.where(half == (h % 2), chunk, 0.0).astype(BF16)
            kc_ref[h, :, 0:128] = _dot(kvn, wukv_ref[2 * h]).astype(BF16)
            kc_ref[h, :, 128:256] = kr2
            vv_ref[h] = _dot(kvn, wukv_ref[2 * h + 1]).astype(BF16)

    const = lambda shape: pl.BlockSpec(shape, lambda i: tuple(0 for _ in shape))
    return pl.pallas_call(
        body, name="mla_pre_fwd", grid=(S // tm,),
        in_specs=[pl.BlockSpec((tm, 256), lambda i: (i, 3)), pl.BlockSpec((tm, 128), lambda i: (i, 8)),
                  pl.BlockSpec((tm, 128), lambda i: (i, 9)), const((1, 256)), const((1, 128)),
                  const((768, 256)), const((8, 128, 128)),
                  pl.BlockSpec((tm, 128), lambda i: (i, 0)), pl.BlockSpec((tm, 128), lambda i: (i, 0))],
        out_specs=[pl.BlockSpec((MLA_HEADS, tm, 256), lambda i: (0, i, 0)),
                   pl.BlockSpec((MLA_HEADS, tm, 256), lambda i: (0, i, 0)),
                   pl.BlockSpec((MLA_HEADS, tm, 128), lambda i: (0, i, 0))],
        out_shape=[jax.ShapeDtypeStruct((MLA_HEADS, S, 256), BF16), jax.ShapeDtypeStruct((MLA_HEADS, S, 256), BF16),
                   jax.ShapeDtypeStruct((MLA_HEADS, S, 128), BF16)],
        compiler_params=_params(("parallel",)),
    )(proj, proj, proj, qn_w, kvn_w, wuqT, wukv, cos, sin)


def _causal(i, j, t):
    row = i * t + lax.broadcasted_iota(jnp.int32, (t, t), 0)
    col = j * t + lax.broadcasted_iota(jnp.int32, (t, t), 1)
    return col <= row


def _mla_attn_fwd(qc, kc, vv, *, t=512):
    S = qc.shape[1]
    t = min(t, S)

    def body(q_ref, k_ref, v_ref, o_ref, l_ref):
        i = pl.program_id(0)
        diag = _causal(0, 0, t)

        def step(j, carry, masked):
            rows = pl.ds(pl.multiple_of(j * t, t), t)
            out = []
            for h in range(MLA_HEADS):
                m, l, acc = carry[h]
                s = _dot_nt(q_ref[h], k_ref[h, rows, :]) * MLA_SCALE
                if masked:
                    s = jnp.where(diag, s, -jnp.inf)
                m_new = jnp.maximum(m, jnp.max(s, axis=-1, keepdims=True))
                alpha = jnp.exp(m - m_new)
                p = jnp.exp(s - m_new)
                l = alpha * l + jnp.sum(p, axis=-1, keepdims=True)
                acc = alpha * acc + _dot(p.astype(BF16), v_ref[h, rows, :])
                out.append((m_new, l, acc))
            return tuple(out)

        init = tuple((jnp.full((t, 1), -jnp.inf, F32), jnp.zeros((t, 1), F32), jnp.zeros((t, MLA_V), F32))
                     for _ in range(MLA_HEADS))
        carry = lax.fori_loop(0, i, lambda j, c: step(j, c, False), init)
        carry = step(i, carry, True)
        for h in range(MLA_HEADS):
            m, l, acc = carry[h]
            o_ref[:, 128 * h:128 * h + 128] = acc / l
            l_ref[h] = jnp.broadcast_to(m + jnp.log(l), (t, 128))

    return pl.pallas_call(
        body, name="mla_attn_fwd", grid=(S // t,),
        in_specs=[pl.BlockSpec((MLA_HEADS, t, 256), lambda i: (0, i, 0)),
                  pl.BlockSpec((MLA_HEADS, S, 256), lambda i: (0, 0, 0)),
                  pl.BlockSpec((MLA_HEADS, S, 128), lambda i: (0, 0, 0))],
        out_specs=[pl.BlockSpec((t, 512), lambda i: (i, 0)),
                   pl.BlockSpec((MLA_HEADS, t, 128), lambda i: (0, i, 0))],
        out_shape=[jax.ShapeDtypeStruct((S, 512), F32), jax.ShapeDtypeStruct((MLA_HEADS, S, 128), F32)],
        compiler_params=_params(("parallel",)),
    )(qc, kc, vv)


def _mla_attn_bwd(qc, kc, vv, o, lse, do, *, t=512, tq=1024):
    S = qc.shape[1]
    t = min(t, S)
    tq = min(tq, S)
    nblk = S // t
    hp = MLA_HEADS
    once = pl.Buffered(1)

    def body(q_ref, k_ref, v_ref, o_ref, l_ref, do_ref, dq_ref, dk_ref, dv_ref):
        j = pl.program_id(1)

        @pl.when(j == 0)
        def _():
            dq_ref[...] = jnp.zeros_like(dq_ref)

        first = (j * t) // tq

        def step(i, carry, masked):
            rows = pl.ds(pl.multiple_of(i * tq, tq), tq)
            if masked:
                row = i * tq + lax.broadcasted_iota(jnp.int32, (tq, t), 0)
                col = j * t + lax.broadcasted_iota(jnp.int32, (tq, t), 1)
                visible = col <= row
            out = []
            for h in range(hp):
                dk, dv = carry[h]
                k = k_ref[h]
                q = q_ref[h, rows, :]
                dov = do_ref[rows, 128 * h:128 * h + 128]
                lrow = l_ref[h, rows, :][:, 0:1]
                p = jnp.exp(_dot_nt(q, k) * MLA_SCALE - lrow)
                if masked:
                    p = jnp.where(visible, p, 0.0)
                dob = dov.astype(BF16)
                dv = dv + _dot_tn(p.astype(BF16), dob)
                dp = _dot_nt(dob, v_ref[h])
                delta = jnp.sum(dov * o_ref[rows, 128 * h:128 * h + 128], axis=-1, keepdims=True)
                ds = (p * (dp - delta) * MLA_SCALE).astype(BF16)
                dk = dk + _dot_tn(ds, q)
                dq_ref[h, rows, :] += _dot(ds, k)
                out.append((dk, dv))
            return tuple(out)

        init = tuple((jnp.zeros((t, 256), F32), jnp.zeros((t, MLA_V), F32)) for _ in range(hp))
        carry = step(first, init, True)
        carry = lax.fori_loop(first + 1, S // tq, lambda i, c: step(i, c, False), carry)
        for h in range(hp):
            dk_ref[h] = carry[h][0]
            dv_ref[h] = carry[h][1]

    return pl.pallas_call(
        body, name="mla_attn_bwd", grid=(MLA_HEADS // hp, nblk),
        in_specs=[pl.BlockSpec((hp, S, 256), lambda g, j: (g, 0, 0), pipeline_mode=once),
                  pl.BlockSpec((hp, t, 256), lambda g, j: (g, j, 0)),
                  pl.BlockSpec((hp, t, 128), lambda g, j: (g, j, 0)),
                  pl.BlockSpec((S, 128 * hp), lambda g, j: (0, g), pipeline_mode=once),
                  pl.BlockSpec((hp, S, 128), lambda g, j: (g, 0, 0), pipeline_mode=once),
                  pl.BlockSpec((S, 128 * hp), lambda g, j: (0, g), pipeline_mode=once)],
        out_specs=[pl.BlockSpec((hp, S, 256), lambda g, j: (g, 0, 0)),
                   pl.BlockSpec((hp, t, 256), lambda g, j: (g, j, 0)),
                   pl.BlockSpec((hp, t, 128), lambda g, j: (g, j, 0))],
        out_shape=[jax.ShapeDtypeStruct((MLA_HEADS, S, 256), F32), jax.ShapeDtypeStruct((MLA_HEADS, S, 256), F32),
                   jax.ShapeDtypeStruct((MLA_HEADS, S, 128), F32)],
        compiler_params=_params(("parallel", "arbitrary")),
    )(qc, kc, vv, o, lse, do)


def _mla_pre_bwd(proj, qn_w, kvn_w, wuqT, wukv, cos, sin, dqc, dkc, dvv, *, tm=ROW_TILE):
    S = proj.shape[0]
    tm = min(tm, S)

    def body(ql_ref, kl_ref, qw_ref, kw_ref, wuq_ref, wukv_ref, cos_ref, sin_ref, dqc_ref, dkc_ref, dvv_ref,
             dql_ref, dkl_ref, dkr_ref, gq_ref, gkv_ref, part_ref):
        @pl.when(pl.program_id(0) == 0)
        def _():
            gq_ref[...] = jnp.zeros_like(gq_ref)
            gkv_ref[...] = jnp.zeros_like(gkv_ref)
            part_ref[...] = jnp.zeros_like(part_ref)

        cs, sn = _rope_wide(cos_ref), _rope_wide(sin_ref)
        half = lax.broadcasted_iota(jnp.int32, (tm, 128), 1) // 64
        ql = ql_ref[...]
        rq = _rstd(ql)
        qhat = ql * rq
        qw = qw_ref[...]
        qn = (qhat * qw).astype(BF16)
        chunks = []
        for pair in range(2):
            chunks.append(jnp.where(half == 0, dqc_ref[2 * pair, :, 128:256], dqc_ref[2 * pair + 1, :, 128:256]))
        dqr = jnp.concatenate(chunks, axis=1)
        dqr = dqr * cs + _swap_halves(dqr * sn)
        dq = jnp.concatenate([dqc_ref[h, :, 0:128] for h in range(MLA_HEADS)] + [dqr], axis=1).astype(BF16)
        gq_ref[...] += _dot_tn(dq, qn)
        dqn = _dot(dq, wuq_ref[...])
        part_ref[0:1, :] += jnp.sum(dqn * qhat, axis=0, keepdims=True)
        dql_ref[...] = _rms_bwd(dqn * qw, qhat, rq)
        kl = kl_ref[...]
        rk = _rstd(kl)
        khat = kl * rk
        kw = kw_ref[...]
        kvn = (khat * kw).astype(BF16)
        dkvn = jnp.zeros((tm, MLA_KVR), F32)
        dkr2 = jnp.zeros((tm, 128), F32)
        for h in range(MLA_HEADS):
            dkn = dkc_ref[h, :, 0:128].astype(BF16)
            dvh = dvv_ref[h].astype(BF16)
            gkv_ref[2 * h] += _dot_tn(kvn, dkn)
            gkv_ref[2 * h + 1] += _dot_tn(kvn, dvh)
            dkvn += _dot_nt(dkn, wukv_ref[2 * h]) + _dot_nt(dvh, wukv_ref[2 * h + 1])
            dkr2 += dkc_ref[h, :, 128:256]
        part_ref[1:2, 0:128] += jnp.sum(dkvn * khat, axis=0, keepdims=True)
        dkl_ref[...] = _rms_bwd(dkvn * kw, khat, rk)
        dkr = jnp.where(half == 0, dkr2 + pltpu.roll(dkr2, 64, 1), 0.0)
        dkr_ref[...] = dkr * cs[:, :128] + _swap_halves(dkr * sn[:, :128])

    const = lambda shape: pl.BlockSpec(shape, lambda i: tuple(0 for _ in shape))
    heads = lambda w: pl.BlockSpec((MLA_HEADS, tm, w), lambda i: (0, i, 0))
    return pl.pallas_call(
        body, name="mla_pre_bwd", grid=(S // tm,),
        in_specs=[pl.BlockSpec((tm, 256), lambda i: (i, 3)), pl.BlockSpec((tm, 128), lambda i: (i, 8)),
                  const((1, 256)), const((1, 128)), const((768, 256)), const((8, 128, 128)),
                  pl.BlockSpec((tm, 128), lambda i: (i, 0)), pl.BlockSpec((tm, 128), lambda i: (i, 0)),
                  heads(256), heads(256), heads(128)],
        out_specs=[pl.BlockSpec((tm, 256), lambda i: (i, 0)), pl.BlockSpec((tm, 128), lambda i: (i, 0)),
                   pl.BlockSpec((tm, 128), lambda i: (i, 0)), const((768, 256)), const((8, 128, 128)), const((8, 256))],
        out_shape=[jax.ShapeDtypeStruct((S, 256), F32), jax.ShapeDtypeStruct((S, 128), F32),
                   jax.ShapeDtypeStruct((S, 128), F32), jax.ShapeDtypeStruct((768, 256), F32),
                   jax.ShapeDtypeStruct((8, 128, 128), F32), jax.ShapeDtypeStruct((8, 256), F32)],
        compiler_params=_params(("arbitrary",)),
    )(proj, proj, qn_w, kvn_w, wuqT, wukv, cos, sin, dqc, dkc, dvv)


def _mix_out_fwd(x, oa, ob, w_o, vecs, *, tm=ROW_TILE):
    S = x.shape[0]
    tm = min(tm, S)

    def body(x_ref, oa_ref, ob_ref, w_ref, vec_ref, xo_ref, mo_ref):
        mo = _dot(oa_ref[...].astype(BF16), w_ref[0:512, :]) + _dot(ob_ref[...].astype(BF16), w_ref[512:1024, :])
        mo_ref[...] = mo
        xo_ref[...] = x_ref[...] + vec_ref[3:4, :] * mo

    row = pl.BlockSpec((tm, D), lambda i: (i, 0))
    half = pl.BlockSpec((tm, 512), lambda i: (i, 0))
    return pl.pallas_call(
        body, name="mix_out_fwd", grid=(S // tm,),
        in_specs=[row, half, half, pl.BlockSpec((D, D), lambda i: (0, 0)), pl.BlockSpec((8, D), lambda i: (0, 0))],
        out_specs=[row, row],
        out_shape=[jax.ShapeDtypeStruct((S, D), F32), jax.ShapeDtypeStruct((S, D), F32)],
        compiler_params=_params(("parallel",)),
    )(x, oa, ob, w_o, vecs)


def _mix_out_bwd(dxo, mo, oa, ob, w_o, vecs, *, tm=ROW_TILE):
    S = dxo.shape[0]
    tm = min(tm, S)

    def body(dx_ref, mo_ref, oa_ref, ob_ref, w_ref, vec_ref, doa_ref, dob_ref, gw_ref, part_ref):
        @pl.when(pl.program_id(0) == 0)
        def _():
            gw_ref[...] = jnp.zeros_like(gw_ref)
            part_ref[...] = jnp.zeros_like(part_ref)

        dx = dx_ref[...]
        part_ref[0:1, :] += jnp.sum(dx * mo_ref[...], axis=0, keepdims=True)
        dmo = (vec_ref[3:4, :] * dx).astype(BF16)
        doa_ref[...] = _dot_nt(dmo, w_ref[0:512, :])
        dob_ref[...] = _dot_nt(dmo, w_ref[512:1024, :])
        gw_ref[0:512, :] += _dot_tn(oa_ref[...].astype(BF16), dmo)
        gw_ref[512:1024, :] += _dot_tn(ob_ref[...].astype(BF16), dmo)

    row = pl.BlockSpec((tm, D), lambda i: (i, 0))
    half = pl.BlockSpec((tm, 512), lambda i: (i, 0))
    return pl.pallas_call(
        body, name="mix_out_bwd", grid=(S // tm,),
        in_specs=[row, row, half, half, pl.BlockSpec((D, D), lambda i: (0, 0)), pl.BlockSpec((8, D), lambda i: (0, 0))],
        out_specs=[half, half, pl.BlockSpec((D, D), lambda i: (0, 0)), pl.BlockSpec((8, D), lambda i: (0, 0))],
        out_shape=[jax.ShapeDtypeStruct((S, 512), F32), jax.ShapeDtypeStruct((S, 512), F32),
                   jax.ShapeDtypeStruct((D, D), F32), jax.ShapeDtypeStruct((8, D), F32)],
        compiler_params=_params(("arbitrary",)),
    )(dxo, mo, oa, ob, w_o, vecs)


def _mix_in_bwd(h, w_inT, dq, dk, dv, dql, dkl, dkr, *, tm=ROW_TILE):
    S = h.shape[0]
    tm = min(tm, S)
    offs = (0, 512, 640, 768, 1024, 1152)
    wid = (512, 128, 128, 256, 128, 128)

    def body(h_ref, w_ref, dq_ref, dk_ref, dv_ref, dql_ref, dkl_ref, dkr_ref, dh_ref, gw_ref):
        @pl.when(pl.program_id(0) == 0)
        def _():
            gw_ref[...] = jnp.zeros_like(gw_ref)

        hv = h_ref[...]
        dh = jnp.zeros((tm, D), F32)
        for ref, o, w in zip((dq_ref, dk_ref, dv_ref, dql_ref, dkl_ref, dkr_ref), offs, wid):
            w = min(w, D_IN - o)
            dpart = ref[...][:, :w].astype(BF16)
            dh += _dot(dpart, w_ref[o:o + w, :])
            gw_ref[o:o + w, :] += _dot_tn(dpart, hv)
        dh_ref[...] = dh

    row = pl.BlockSpec((tm, D), lambda i: (i, 0))
    part = lambda w: pl.BlockSpec((tm, w), lambda i: (i, 0))
    return pl.pallas_call(
        body, name="mix_in_bwd", grid=(S // tm,),
        in_specs=[row, pl.BlockSpec((D_IN_PAD, D), lambda i: (0, 0))] + [part(w) for w in wid],
        out_specs=[row, pl.BlockSpec((D_IN, D), lambda i: (0, 0))],
        out_shape=[jax.ShapeDtypeStruct((S, D), F32), jax.ShapeDtypeStruct((D_IN, D), F32)],
        compiler_params=_params(("arbitrary",)),
    )(h, w_inT, dq, dk, dv, dql, dkl, dkr)


def _vecs(norm_w, mod9, k):
    return jnp.concatenate([norm_w.reshape(1, D), mod9[3 * k:3 * k + 3], jnp.zeros((4, D), F32)], axis=0)


def _uq_group_rows(wuqT):
    per = MLA_NOPE + MLA_ROPE
    nope = [wuqT[per * h:per * h + MLA_NOPE] for h in range(MLA_HEADS)]
    rope = [wuqT[per * h + MLA_NOPE:per * (h + 1)] for h in range(MLA_HEADS)]
    return jnp.concatenate(nope + rope, axis=0)


def _uq_ungroup_rows(g):
    parts = []
    for h in range(MLA_HEADS):
        parts += [g[MLA_NOPE * h:MLA_NOPE * (h + 1)], g[512 + MLA_ROPE * h:512 + MLA_ROPE * (h + 1)]]
    return jnp.concatenate(parts, axis=0)


def _local_step(x, tgt, mod9, norms, sinks, rel_bias, q_norm, kv_norm, W, on_grads=None):
    if on_grads is None:
        on_grads = lambda group, grads, after, vecs: vecs
    S = x.shape[0]
    v1 = _vecs(norms["ffn1"], mod9, 0)
    v2 = _vecs(norms["mix"], mod9, 1)
    v3 = _vecs(norms["ffn2"], mod9, 2)
    bucket = jnp.asarray(_bucket_table())
    cos, sin = _rope_tables(S)
    if isinstance(W, dict):
        full, W = W, (lambda group, after, vecs: (full, vecs))

    W1, v1 = W("ffn1", [], v1)
    x1, h1, a1, b1, f1 = _ffn_fwd(x, v1, W1["g1T"], W1["u1T"], W1["d1"], name="ffn1_fwd")
    W2, v2 = W("mixer", [x1], v2)
    w_inT = jnp.pad(W2["w_inT"], ((0, D_IN_PAD - D_IN), (0, 0))).astype(BF16)
    wuqT = _uq_group_rows(W2["w_uqT"])
    h2, proj = _mix_in_fwd(x1, v2, w_inT)
    bias = _bias_build(rel_bias, bucket)
    oa = _swa_fwd(proj, bias, sinks)
    qc, kc, vv = _mla_pre_fwd(proj, q_norm, kv_norm, wuqT, W2["w_ukv"], cos, sin)
    ob, lse = _mla_attn_fwd(qc, kc, vv)
    _, v2o = W("ffn2_on_its_way", [ob], v2)
    x2, mo = _mix_out_fwd(x1, oa, ob, W2["w_o"], v2o)
    W3, v3 = W("ffn2", [x2], v3)
    x3, h3, a3, b3, f3 = _ffn_fwd(x2, v3, W3["g3T"], W3["u3T"], W3["d3"], name="ffn2_fwd")
    dx3, head_part, df3 = _head(x3, tgt, norms["final"], f3, v3)

    gg3, gu3, gd3, dh3 = _ffn_bwd_main(h3, df3, a3, b3, W3["g3T"], W3["u3T"], W3["d3"], name="ffn2_bwd")
    ffn2 = {"g3T": gg3, "u3T": gu3, "d3": gd3}
    v3 = on_grads("ffn2", ffn2, [], v3)
    dx2, n3_part = _norm_bwd(dh3, x2, dx3, v3, name="ffn2_norm_bwd")
    v2 = on_grads("ffn2", None, [dx2], v2)
    doa, dob, g_wo, g2_part = _mix_out_bwd(dx2, mo, oa, ob, W2["w_o"], v2)
    dq, dk, dv, drb, dsk = _swa_bwd(proj, bias, sinks, oa, doa, bucket)
    dqc, dkc, dvv = _mla_attn_bwd(qc, kc, vv, ob, lse, dob)
    dql, dkl, dkr, g_uq, g_ukv, mla_part = _mla_pre_bwd(proj, q_norm, kv_norm, wuqT, W2["w_ukv"], cos, sin, dqc, dkc, dvv)
    dh2, g_win = _mix_in_bwd(h2, w_inT, dq, dk, dv, dql, dkl, dkr)
    mixer = {"w_inT": g_win, "w_uqT": _uq_ungroup_rows(g_uq).astype(BF16),
             "w_ukv": g_ukv.astype(BF16), "w_o": g_wo.astype(BF16)}
    v2 = on_grads("mixer", mixer, [], v2)
    dx1, n2_part, df1 = _norm_bwd(dh2, x1, dx2, v2, name="mix_norm_bwd", below=(f1, v1))
    started = on_grads("mixer", None, [dx1], jnp.zeros((1, 1), F32))
    gg1, gu1, gd1, dh1 = _ffn_bwd_main(h1, df1, a1, b1, W1["g1T"], W1["u1T"], W1["d1"], name="ffn1_bwd",
                                       after=[started])
    ffn1 = {"g1T": gg1, "u1T": gu1, "d1": gd1}
    v1 = on_grads("ffn1", ffn1, [], v1)
    dx0, n1_part = _norm_bwd(dh1, x, dx1, v1, name="ffn1_norm_bwd")

    grads = {**ffn1, **ffn2, **mixer}
    return head_part[1, 0], dx0, grads, _pack_vec(n1_part, n2_part, n3_part, head_part, g2_part, mla_part, dsk, drb)


SMALL_LAYOUT = (("norm_ffn1", 1024), ("norm_mix", 1024), ("norm_ffn2", 1024), ("norm_final", 1024),
                ("q_norm", 256), ("kv_norm", 128), ("sinks", 128), ("rel_bias", 256))
N_SMALL = sum(n for _, n in SMALL_LAYOUT)
LOSS_SLOT = 4 * 1024 + 256 + 128 + SWA_HEADS
N_MODVEC = N_MOD * D
N_VEC = N_MODVEC + N_SMALL


def _pack_vec(n1, n2, n3, head, g2, mla, dsk, drb):
    def body(n1_ref, n2_ref, n3_ref, head_ref, g2_ref, mla_ref, dsk_ref, drb_ref, out_ref):
        rows = [n1_ref[1:2, :], n1_ref[2:3, :], n2_ref[3:4, :], n2_ref[1:2, :], n2_ref[2:3, :], g2_ref[0:1, :],
                n3_ref[1:2, :], n3_ref[2:3, :], head_ref[3:4, :],
                n1_ref[0:1, :], n2_ref[0:1, :], n3_ref[0:1, :], head_ref[0:1, :]]
        for i, row in enumerate(rows):
            out_ref[:, D * i:D * (i + 1)] = row
        off = D * len(rows)
        out_ref[:, off:off + 256] = mla_ref[0:1, :]
        out_ref[:, off + 256:off + 384] = mla_ref[1:2, 0:128]

        def diagonal(block):
            r = lax.broadcasted_iota(jnp.int32, block.shape, 0)
            lane = lax.broadcasted_iota(jnp.int32, block.shape, 1)
            return jnp.sum(jnp.where(r == lane, block, 0.0), axis=0, keepdims=True)

        lane = lax.broadcasted_iota(jnp.int32, (1, 128), 1)
        out_ref[:, off + 384:off + 512] = jnp.where(lane == SWA_HEADS, head_ref[1:2, 0:128], diagonal(dsk_ref[...]))
        out_ref[:, off + 512:off + 640] = diagonal(drb_ref[0:128, :])
        out_ref[:, off + 640:off + 768] = diagonal(drb_ref[128:256, :])

    vm = pl.BlockSpec(memory_space=pltpu.VMEM)
    return pl.pallas_call(body, name="pack_vec", in_specs=[vm] * 8, out_specs=vm,
                          out_shape=jax.ShapeDtypeStruct((1, N_VEC), F32))(n1, n2, n3, head, g2, mla, dsk, drb)


def _coords():
    return lax.axis_index("x"), lax.axis_index("y"), lax.axis_index("c")


def _flip(v, bit):
    return 1 - v if bit else v


def _peer(r):
    x, y, c = _coords()
    return (_flip(x, r & 4), _flip(y, r & 2), _flip(c, r & 1))


def _mod_fwd(c_tile, w_mod, b_mod3):
    W = w_mod.shape[1]

    def body(c_ref, w_ref, b_ref, mod_ref, ca_ref, call_ref, part_ref, send_sems, recv_sems):
        x, y, c = _coords()
        me = 4 * x + 2 * y + c
        call_ref[me] = c_ref[...]
        sends = []
        for r in range(1, N_DEV):
            cp = pltpu.make_async_remote_copy(c_ref, call_ref.at[me], send_sems.at[0, r], recv_sems.at[0, r],
                                              device_id=_peer(r), device_id_type=MESH)
            cp.start()
            sends.append(cp)
        for r in range(1, N_DEV):
            pltpu.make_async_remote_copy(c_ref, call_ref.at[me], send_sems.at[0, r], recv_sems.at[0, r],
                                         device_id=_peer(r), device_id_type=MESH).wait_recv()
        cv = call_ref[...].reshape(8 * N_DEV, D)
        ca = (cv * _sigmoid(cv)).astype(BF16)
        ca_ref[...] = ca
        part_ref[...] = _dot(ca, w_ref[...].astype(BF16)).reshape(N_DEV, 8, W)
        mod_ref[me] = part_ref[me] + b_ref[me]
        for r in range(1, N_DEV):
            cp = pltpu.make_async_remote_copy(part_ref.at[me ^ r], mod_ref.at[me], send_sems.at[1, r],
                                              recv_sems.at[1, r], device_id=_peer(r), device_id_type=MESH)
            cp.start()
            sends.append(cp)
        for r in range(1, N_DEV):
            pltpu.make_async_remote_copy(part_ref.at[me ^ r], mod_ref.at[me], send_sems.at[1, r],
                                         recv_sems.at[1, r], device_id=_peer(r), device_id_type=MESH).wait_recv()
            mod_ref[me ^ r] = mod_ref[me ^ r] + b_ref[me ^ r]
        for cp in sends:
            cp.wait_send()

    vm = pl.BlockSpec(memory_space=pltpu.VMEM)
    return pl.pallas_call(
        body, name="mod_fwd", in_specs=[vm, vm, vm], out_specs=[vm, vm],
        out_shape=[jax.ShapeDtypeStruct((N_DEV, 8, W), F32), jax.ShapeDtypeStruct((8 * N_DEV, D), BF16)],
        scratch_shapes=[pltpu.VMEM((N_DEV, 8, D), F32), pltpu.VMEM((N_DEV, 8, W), F32),
                        pltpu.SemaphoreType.DMA((2, N_DEV)), pltpu.SemaphoreType.DMA((2, N_DEV))],
        compiler_params=_params(),
    )(c_tile, w_mod, b_mod3)


def _mod_bwd(allvec, ca, me_idx):
    W = N_MODVEC // N_DEV

    def body(me_ref, all_ref, cols_ref, ca_ref, gw_ref, sum_ref):
        in_first_row = lax.broadcasted_iota(jnp.int32, (N_DEV, 8, W), 1) == 0
        dm = jnp.where(in_first_row, cols_ref[...], 0.0).reshape(8 * N_DEV, W)
        gw_ref[...] = _dot_tn(ca_ref[...], dm.astype(BF16))
        total = all_ref[0]
        for k in range(1, N_DEV):
            total = total + all_ref[k]
        sum_ref[...] = total

    return pl.pallas_call(
        body, name="mod_bwd",
        grid_spec=pltpu.PrefetchScalarGridSpec(
            num_scalar_prefetch=1, grid=(1,),
            in_specs=[pl.BlockSpec((N_DEV, 1, N_VEC), lambda i, me: (0, 0, 0)),
                      pl.BlockSpec((N_DEV, 1, W), lambda i, me: (0, 0, me[0])),
                      pl.BlockSpec((8 * N_DEV, D), lambda i, me: (0, 0))],
            out_specs=[pl.BlockSpec((D, W), lambda i, me: (0, 0)), pl.BlockSpec((1, N_VEC), lambda i, me: (0, 0))]),
        out_shape=[jax.ShapeDtypeStruct((D, W), F32), jax.ShapeDtypeStruct((1, N_VEC), F32)],
        compiler_params=_params(("arbitrary",)),
    )(me_idx, allvec, allvec, ca)


def _wgather(shards):
    n = len(shards)
    rows = [s.shape[0] for s in shards]

    def body(*refs):
        ins, outs, token = refs[:n], refs[n:2 * n], refs[2 * n]
        send_sems, recv_sems, local_sems = refs[2 * n + 1:]
        token[...] = jnp.zeros_like(token)
        x, y, c = _coords()
        me = 4 * x + 2 * y + c
        sib, xn, yn = (x, y, 1 - c), (1 - x, y, c), (x, 1 - y, c)
        block = lambda px, py, pc: 4 * px + 2 * py + pc

        def part(k, blk, half):
            if half is None:
                return outs[k].at[blk]
            return outs[k].at[blk, pl.ds(half * (rows[k] // 2), rows[k] // 2)]

        def copy(k, slot, blk, to, half=None, src=None):
            ref = part(k, blk, half)
            return pltpu.make_async_remote_copy(
                src_ref=ref if src is None else src, dst_ref=ref, send_sem=send_sems.at[k, slot],
                recv_sem=recv_sems.at[k, slot], device_id=to, device_id_type=MESH)

        local = [pltpu.make_async_copy(ins[k], outs[k].at[me], local_sems.at[k]) for k in range(n)]
        for cp in local:
            cp.start()
        sent = [copy(k, slot, me, to, src=ins[k]) for k in range(n) for slot, to in ((0, sib), (1, xn), (2, yn))]
        for cp in sent:
            cp.start()
        bx, by, bd = block(1 - x, y, c), block(x, 1 - y, c), block(1 - x, 1 - y, c)
        for k in range(n):
            copy(k, 1, bx, sib).wait_recv()
            sent += [copy(k, 4, bx, yn, half=1), copy(k, 5, bx, sib)]
            sent[-2].start()
            sent[-1].start()
        for k in range(n):
            copy(k, 2, by, sib).wait_recv()
            sent += [copy(k, 3, by, xn, half=0), copy(k, 6, by, sib)]
            sent[-2].start()
            sent[-1].start()
        for k in range(n):
            copy(k, 3, bd, sib, half=0).wait_recv()
            copy(k, 4, bd, sib, half=1).wait_recv()
            sent.append(copy(k, 7, bd, sib))
            sent[-1].start()
        for k in range(n):
            copy(k, 0, block(x, y, 1 - c), sib).wait_recv()
            for slot, blk in ((5, block(1 - x, y, 1 - c)), (6, block(x, 1 - y, 1 - c)), (7, block(1 - x, 1 - y, 1 - c))):
                copy(k, slot, blk, sib).wait_recv()
        for cp in sent:
            cp.wait_send()
        for cp in local:
            cp.wait()

    anyspec = pl.BlockSpec(memory_space=pl.ANY)
    return pl.pallas_call(
        body, name="wgather", in_specs=[anyspec] * n,
        out_specs=[anyspec] * n + [pl.BlockSpec(memory_space=pltpu.VMEM)],
        out_shape=[jax.ShapeDtypeStruct((N_DEV,) + s.shape, s.dtype) for s in shards]
        + [jax.ShapeDtypeStruct((8, 128), F32)],
        scratch_shapes=[pltpu.SemaphoreType.DMA((n, 8)), pltpu.SemaphoreType.DMA((n, 8)),
                        pltpu.SemaphoreType.DMA((n,))],
    )(*shards)


class _GatherCopies:
    def __init__(self, lands, send_sems, recv_sems, k0=0, batches=None):
        x, y, c = _coords()
        me = 4 * x + 2 * y + c
        sib = (x, y, 1 - c)
        chips = [(1 - x, y), (x, 1 - y), (1 - x, 1 - y)]

        def copy(k, slot, block, to):
            return pltpu.make_async_remote_copy(
                src_ref=lands[k].at[block], dst_ref=lands[k].at[block],
                send_sem=send_sems.at[7 * (k0 + k) + slot], recv_sem=recv_sems.at[7 * (k0 + k) + slot],
                device_id=to, device_id_type=MESH)

        n = len(lands)
        self.first = [copy(k, 0, me, sib) for k in range(n)]
        for batch in batches or [range(n)]:
            self.first += [copy(k, 1 + j, me, (cx, cy, c)) for j, (cx, cy) in enumerate(chips) for k in batch]
        self.landed = [copy(k, 1 + j, 4 * cx + 2 * cy + c, sib) for j, (cx, cy) in enumerate(chips) for k in range(n)]
        self.passed = [copy(k, 4 + j, 4 * cx + 2 * cy + c, sib) for j, (cx, cy) in enumerate(chips) for k in range(n)]
        self.from_sib = [copy(k, 0, 4 * x + 2 * y + (1 - c), sib) for k in range(n)]
        self.from_sib += [copy(k, 4 + j, 4 * cx + 2 * cy + (1 - c), sib) for j, (cx, cy) in enumerate(chips)
                          for k in range(n)]


def _gather_start(lands, *, name, batches=None):
    n = len(lands)

    def body(*refs):
        for cp in _GatherCopies(refs[:n], refs[n], refs[n + 1], batches=batches).first:
            cp.start()
        refs[-1][...] = jnp.zeros_like(refs[-1])

    out = pl.pallas_call(
        body, name=name,
        out_shape=(pltpu.SemaphoreType.DMA((7 * n,)), pltpu.SemaphoreType.DMA((7 * n,)),
                   *[pltpu.HBM(l.shape, l.dtype) for l in lands], jax.ShapeDtypeStruct((8, 128), F32)),
        in_specs=[HBM_SPEC] * n,
        out_specs=(SEM_SPEC, SEM_SPEC, *[HBM_SPEC] * n, pl.BlockSpec(memory_space=pltpu.VMEM)),
        input_output_aliases={i: 2 + i for i in range(n)},
        compiler_params=pltpu.CompilerParams(has_side_effects=DATAFLOW),
    )(*[_in_hbm(l) for l in lands])
    return out[0], out[1], list(out[2:2 + n]), out[-1]


def _gather_pass(send_sems, recv_sems, lands, after, *, name, stage, k0=0):
    n = len(lands)

    def body(*refs):
        cps = _GatherCopies(refs[:n], refs[n], refs[n + 1], k0)
        if stage == "landed":
            for cp in cps.landed:
                cp.wait_recv()
        else:
            for cp in cps.passed:
                cp.start()
        refs[-1][...] = jnp.zeros_like(refs[-1])

    out = pl.pallas_call(
        body, name=name,
        out_shape=(*[pltpu.HBM(l.shape, l.dtype) for l in lands], jax.ShapeDtypeStruct((8, 128), F32)),
        in_specs=[HBM_SPEC] * n + [SEM_SPEC, SEM_SPEC] + [pl.BlockSpec(memory_space=pl.ANY)] * len(after),
        out_specs=(*[HBM_SPEC] * n, pl.BlockSpec(memory_space=pltpu.VMEM)),
        input_output_aliases={i: i for i in range(n)},
        compiler_params=pltpu.CompilerParams(has_side_effects=DATAFLOW),
    )(*lands, send_sems, recv_sems, *after)
    return list(out[:n]), out[-1]


def _gather_end(send_sems, recv_sems, lands, after, *, name, k0=0):
    n = len(lands)

    def body(*refs):
        cps = _GatherCopies(refs[:n], refs[n], refs[n + 1], k0)
        for cp in cps.from_sib:
            cp.wait_recv()
        for cp in cps.first + cps.passed:
            cp.wait_send()

    out = pl.pallas_call(
        body, name=name,
        out_shape=[pltpu.HBM(l.shape, l.dtype) for l in lands],
        in_specs=[HBM_SPEC] * n + [SEM_SPEC, SEM_SPEC] + [pl.BlockSpec(memory_space=pl.ANY)] * len(after),
        out_specs=[HBM_SPEC] * n,
        input_output_aliases={i: i for i in range(n)},
        compiler_params=pltpu.CompilerParams(has_side_effects=DATAFLOW),
    )(*lands, send_sems, recv_sems, *after)
    return list(out)


def _d2d_copies(grads, lands, send_sems, recv_sems):
    x, y, c = _coords()
    return [pltpu.make_async_remote_copy(
        src_ref=grads[k].at[2 * q + (1 - c)], dst_ref=lands[k].at[q],
        send_sem=send_sems.at[4 * k + q], recv_sem=recv_sems.at[4 * k + q],
        device_id=(x, y, 1 - c), device_id_type=MESH) for k in range(len(grads)) for q in range(4)]


def _direct_copies(grads, lands, send_sems, recv_sems):
    x, y, c = _coords()
    me = 4 * x + 2 * y + c
    return [pltpu.make_async_remote_copy(
        src_ref=grads[k].at[me ^ r], dst_ref=lands[k].at[r - 1],
        send_sem=send_sems.at[7 * k + r - 1], recv_sem=recv_sems.at[7 * k + r - 1],
        device_id=_peer(r), device_id_type=MESH) for k in range(len(grads)) for r in range(1, N_DEV)]


def _vec_copies(srcs, lands, send_sems, recv_sems):
    x, y, c = _coords()
    me = 4 * x + 2 * y + c
    return [pltpu.make_async_remote_copy(
        src_ref=lands[0].at[me], dst_ref=lands[0].at[me], send_sem=send_sems.at[r - 1], recv_sem=recv_sems.at[r - 1],
        device_id=_peer(r), device_id_type=MESH) for r in range(1, N_DEV)]


def _chipsum(gs, sibs, cidx, *, name):
    n = len(gs)

    def body(c_ref, *refs):
        for k in range(n):
            refs[2 * n + k][...] = (refs[k][...].astype(F32) + refs[n + k][...].astype(F32)).astype(refs[2 * n + k].dtype)

    mine = [pl.BlockSpec((1,) + g.shape[1:], lambda q, c_ref: (2 * q + c_ref[0], 0, 0)) for g in gs]
    other = [pl.BlockSpec((1,) + g.shape[1:], lambda q, c_ref: (q, 0, 0)) for g in gs]
    return pl.pallas_call(
        body, name=name,
        grid_spec=pltpu.PrefetchScalarGridSpec(num_scalar_prefetch=1, grid=(4,), in_specs=mine + other, out_specs=other),
        out_shape=[jax.ShapeDtypeStruct((4,) + g.shape[1:], g.dtype) for g in gs],
        compiler_params=_params(("arbitrary",)),
    )(cidx, *gs, *sibs)


HBM_SPEC = pl.BlockSpec(memory_space=pltpu.HBM)
SEM_SPEC = pl.BlockSpec(memory_space=pltpu.SEMAPHORE)
DATAFLOW = pltpu.SideEffectType.DATAFLOW_SIDE_EFFECTING


def _in_hbm(a):
    return pltpu.with_memory_space_constraint(a, pltpu.HBM)


def _rs_step1_copies(sums, lands, send_sems, recv_sems):
    n = len(sums)
    direct, relay = lands[:n], lands[n:]
    x, y, c = _coords()
    xn, yn = (1 - x, y, c), (x, 1 - y, c)
    qx, qy, qd = 2 * (1 - x) + y, 2 * x + (1 - y), 2 * (1 - x) + (1 - y)
    cps = []
    for k in range(n):
        h = sums[k].shape[1] // 2
        a, b = pl.ds(0, h), pl.ds(h, h)
        moves = ((sums[k].at[qx, a], direct[k].at[0], xn), (sums[k].at[qy, b], direct[k].at[1], yn),
                 (sums[k].at[qd, a], relay[k].at[0], xn), (sums[k].at[qd, b], relay[k].at[1], yn))
        for s, (src, dst, to) in enumerate(moves):
            cps.append(pltpu.make_async_remote_copy(
                src_ref=src, dst_ref=dst, send_sem=send_sems.at[4 * k + s], recv_sem=recv_sems.at[4 * k + s],
                device_id=to, device_id_type=MESH))
    return cps


def _rs_step2_copies(relayed, lands, send_sems, recv_sems, k0=0):
    x, y, c = _coords()
    cps = []
    for k in range(len(relayed)):
        for s, to in enumerate(((1 - x, y, c), (x, 1 - y, c))):
            cps.append(pltpu.make_async_remote_copy(
                src_ref=relayed[k].at[s], dst_ref=lands[k].at[s], send_sem=send_sems.at[2 * (k0 + k) + s],
                recv_sem=recv_sems.at[2 * (k0 + k) + s], device_id=to, device_id_type=MESH))
    return cps


def _relay_sum(sums, relay, qxy, *, name):
    n = len(sums)

    def body(q_ref, *refs):
        for k in range(n):
            refs[2 * n + k][...] = (refs[k][...].astype(F32) + refs[n + k][...].astype(F32)).astype(refs[2 * n + k].dtype)

    half = lambda s: (1, s.shape[1] // 2) + s.shape[2:]
    return pl.pallas_call(
        body, name=name,
        grid_spec=pltpu.PrefetchScalarGridSpec(
            num_scalar_prefetch=1, grid=(2,),
            in_specs=[pl.BlockSpec(half(s), lambda t, q_ref: (q_ref[t], 1 - t, 0)) for s in sums]
            + [pl.BlockSpec(half(s), lambda t, q_ref: (1 - t, 0, 0)) for s in sums],
            out_specs=[pl.BlockSpec(half(s), lambda t, q_ref: (t, 0, 0)) for s in sums]),
        out_shape=[jax.ShapeDtypeStruct((2,) + half(s)[1:], s.dtype) for s in sums],
        compiler_params=_params(("arbitrary",)),
    )(qxy, *sums, *relay)


def _split_start(copies, srcs, lands, n_sems, after, *, name):
    ns, nl = len(srcs), len(lands)

    def body(*refs):
        for cp in copies(refs[:ns], refs[ns:ns + nl], refs[ns + nl + len(after)], refs[ns + nl + len(after) + 1]):
            cp.start()
        refs[-1][...] = jnp.zeros_like(refs[-1])

    bufs = [_in_hbm(a) for a in list(srcs) + list(lands)]
    out = pl.pallas_call(
        body, name=name,
        out_shape=(pltpu.SemaphoreType.DMA((n_sems,)), pltpu.SemaphoreType.DMA((n_sems,)),
                   *[pltpu.HBM(a.shape, a.dtype) for a in bufs], jax.ShapeDtypeStruct((8, 128), F32)),
        in_specs=[HBM_SPEC] * len(bufs) + [pl.BlockSpec(memory_space=pl.ANY)] * len(after),
        out_specs=(SEM_SPEC, SEM_SPEC, *[HBM_SPEC] * len(bufs), pl.BlockSpec(memory_space=pltpu.VMEM)),
        input_output_aliases={i: 2 + i for i in range(len(bufs))},
        compiler_params=pltpu.CompilerParams(has_side_effects=DATAFLOW),
    )(*bufs, *after)
    return out[0], out[1], list(out[2:2 + ns]), list(out[2 + ns:2 + ns + nl]), out[-1]


def _split_wait(copies, send_sems, recv_sems, srcs, lands, after, *, name):
    ns, nl = len(srcs), len(lands)

    def body(*refs):
        for cp in copies(refs[:ns], refs[ns:ns + nl], refs[ns + nl], refs[ns + nl + 1]):
            cp.wait_send()
            cp.wait_recv()

    out = pl.pallas_call(
        body, name=name,
        out_shape=[pltpu.HBM(a.shape, a.dtype) for a in list(srcs) + list(lands)],
        in_specs=[HBM_SPEC] * (ns + nl) + [SEM_SPEC, SEM_SPEC] + [pl.BlockSpec(memory_space=pl.ANY)] * len(after),
        out_specs=[HBM_SPEC] * (ns + nl),
        input_output_aliases={i: i for i in range(ns + nl)},
        compiler_params=pltpu.CompilerParams(has_side_effects=DATAFLOW),
    )(*srcs, *lands, send_sems, recv_sems, *after)
    return list(out[:ns]), list(out[ns:])


ADAM_C1 = 1.0 / (1.0 - ADAM_B1 ** ADAM_STEP)
ADAM_C2 = 1.0 / (1.0 - ADAM_B2 ** ADAM_STEP)


def _adam_math(w, g, m, v):
    m2 = ADAM_B1 * m + (1.0 - ADAM_B1) * g
    v2 = ADAM_B2 * v + (1.0 - ADAM_B2) * (g * g)
    return -ADAM_LR * ((m2 * ADAM_C1) / (jnp.sqrt(v2 * ADAM_C2) + ADAM_EPS) + ADAM_WD * w), m2, v2


def _adamw(w, g, m, v, *, name, after=()):
    R, C = w.shape
    tr = R if R <= 512 else 256

    def body(w_ref, g_ref, m_ref, v_ref, *rest):
        d_ref, nm_ref, nv_ref = rest[len(after):]
        d_ref[...], nm_ref[...], nv_ref[...] = _adam_math(w_ref[...], g_ref[...], m_ref[...], v_ref[...])

    blk = pl.BlockSpec((tr, C), lambda i: (i, 0))
    return pl.pallas_call(
        body, name=name, grid=(R // tr,), in_specs=[blk] * 4 + [pl.BlockSpec(memory_space=pl.ANY)] * len(after),
        out_specs=[blk] * 3, out_shape=[jax.ShapeDtypeStruct((R, C), F32)] * 3,
        compiler_params=_params(("parallel",)),
    )(w, g, m, v, *after)


def _adamw_rs2(wmv, cs, direct, second, qidx, *, name):
    n = len(wmv)
    r, cc = wmv[0][0].shape
    h = r // 2

    def body(q_ref, *refs):
        ins, outs = refs[:6 * n], refs[6 * n:]
        for k in range(n):
            w_ref, m_ref, v_ref, c_ref, d1_ref, d2_ref = ins[6 * k:6 * k + 6]
            g_ref, d_ref, nm_ref, nv_ref = outs[4 * k:4 * k + 4]
            g = (c_ref[0].astype(F32) + d1_ref[0].astype(F32)) + d2_ref[0].astype(F32)
            g_ref[...] = g
            d_ref[...], nm_ref[...], nv_ref[...] = _adam_math(w_ref[...], g, m_ref[...], v_ref[...])

    blk = pl.BlockSpec((h, cc), lambda i, q_ref: (i, 0))
    one = [blk, blk, blk, pl.BlockSpec((1, h, cc), lambda i, q_ref: (q_ref[0], i, 0)),
           pl.BlockSpec((1, h, cc), lambda i, q_ref: (i, 0, 0)),
           pl.BlockSpec((1, h, cc), lambda i, q_ref: (1 - i, 0, 0))]
    out = pl.pallas_call(
        body, name=name,
        grid_spec=pltpu.PrefetchScalarGridSpec(num_scalar_prefetch=1, grid=(2,), in_specs=one * n,
                                               out_specs=[blk] * (4 * n)),
        out_shape=[jax.ShapeDtypeStruct((r, cc), F32)] * (4 * n),
        compiler_params=_params(("arbitrary",)),
    )(qidx, *[a for (w, m, v), c, d1, d2 in zip(wmv, cs, direct, second) for a in (w, m, v, c, d1, d2)])
    return [tuple(out[4 * k:4 * k + 4]) for k in range(n)]


def _adamw_rs(wmv, cs, rcv, qidx, *, name):
    n = len(wmv)
    shapes = [w.shape for w, _, _ in wmv]
    n_rcv = rcv[0].shape[0]
    halved = len(set(shapes)) == 1 and shapes[0][0] % 32 == 0 and shapes[0][0] > 128
    tiles = 2 if halved else 1

    def body(q_ref, *refs):
        ins, outs = refs[:5 * n], refs[5 * n:]
        for k in range(n):
            w_ref, m_ref, v_ref, c_ref, r_ref = ins[5 * k:5 * k + 5]
            g_ref, d_ref, nm_ref, nv_ref = outs[4 * k:4 * k + 4]
            g = c_ref[0].astype(F32)
            for j in range(n_rcv):
                g = g + r_ref[j].astype(F32)
            g_ref[...] = g
            d_ref[...], nm_ref[...], nv_ref[...] = _adam_math(w_ref[...], g, m_ref[...], v_ref[...])

    in_specs, out_specs = [], []
    for r, cc in shapes:
        blk = pl.BlockSpec((r // tiles, cc), lambda i, q_ref: (i, 0))
        in_specs += [blk, blk, blk, pl.BlockSpec((1, r // tiles, cc), lambda i, q_ref: (q_ref[0], i, 0)),
                     pl.BlockSpec((n_rcv, r // tiles, cc), lambda i, q_ref: (0, i, 0))]
        out_specs += [blk] * 4
    out = pl.pallas_call(
        body, name=name,
        grid_spec=pltpu.PrefetchScalarGridSpec(num_scalar_prefetch=1, grid=(tiles,), in_specs=in_specs,
                                               out_specs=out_specs),
        out_shape=[jax.ShapeDtypeStruct(s, F32) for s in shapes for _ in range(4)],
        compiler_params=_params(("arbitrary",)),
    )(qidx, *[a for (w, m, v), c, rc in zip(wmv, cs, rcv) for a in (w, m, v, c, rc)])
    return [tuple(out[4 * k:4 * k + 4]) for k in range(n)]


SMALL_PARAMS = ("norm_ffn1", "norm_mix", "norm_ffn2", "norm_final", "q_norm", "kv_norm", "sinks", "rel_bias", "b_mod")


def _adamw_small(gvec, wmv):
    shapes = [wmv[3 * i].shape for i in range(len(SMALL_PARAMS))]

    def body(*refs):
        g_all = refs[0]
        ins = refs[1:1 + 3 * len(SMALL_PARAMS)]
        outs = refs[1 + 3 * len(SMALL_PARAMS):]
        off = N_MODVEC
        for i, name in enumerate(SMALL_PARAMS):
            g_ref, d_ref, nm_ref, nv_ref = outs[4 * i:4 * i + 4]
            w_ref, m_ref, v_ref = ins[3 * i:3 * i + 3]
            start = 0 if name == "b_mod" else off
            rows, width = shapes[i]
            g = jnp.concatenate([g_all[:, start + width * r:start + width * (r + 1)] for r in range(rows)], axis=0)
            g_ref[...] = g
            d_ref[...], nm_ref[...], nv_ref[...] = _adam_math(w_ref[...], g, m_ref[...], v_ref[...])
            if name != "b_mod":
                off += dict(SMALL_LAYOUT)[name]

    vm = pl.BlockSpec(memory_space=pltpu.VMEM)
    n_out = 4 * len(SMALL_PARAMS)
    out = pl.pallas_call(
        body, name="adamw_small", in_specs=[vm] * (1 + len(wmv)), out_specs=[vm] * n_out,
        out_shape=[jax.ShapeDtypeStruct(shapes[i // 4], F32) for i in range(n_out)],
        compiler_params=_params(),
    )(gvec, *wmv)
    return {name: out[4 * i:4 * i + 4] for i, name in enumerate(SMALL_PARAMS)}


TRANSPOSED = ("g1T", "u1T", "g3T", "u3T", "w_inT", "w_uqT")


def kernel(x, c, w_mod, b_mod, norm_ffn1, ffn1_gate, ffn1_up, ffn1_down, norm_mix, w_in, q_norm, kv_norm, w_uq, w_ukv, sinks, w_o, norm_ffn2, ffn2_gate, ffn2_up, ffn2_down, rel_bias, norm_final, loss_target, m_w_mod, m_b_mod, m_norm_ffn1, m_ffn1_gate, m_ffn1_up, m_ffn1_down, m_norm_mix, m_w_in, m_q_norm, m_kv_norm, m_w_uq, m_w_ukv, m_sinks, m_w_o, m_norm_ffn2, m_ffn2_gate, m_ffn2_up, m_ffn2_down, m_rel_bias, m_norm_final, v_w_mod, v_b_mod, v_norm_ffn1, v_ffn1_gate, v_ffn1_up, v_ffn1_down, v_norm_mix, v_w_in, v_q_norm, v_kv_norm, v_w_uq, v_w_ukv, v_sinks, v_w_o, v_norm_ffn2, v_ffn2_gate, v_ffn2_up, v_ffn2_down, v_rel_bias, v_norm_final):
    mx, my, mc = _coords()
    cidx = jnp.reshape(mc, (1,)).astype(jnp.int32)
    qidx = jnp.reshape(2 * mx + my, (1,)).astype(jnp.int32)
    WM = w_mod.shape[2]

    c_tile = jnp.pad(c, ((0, 7), (0, 0)))
    b_mod3 = jnp.pad(b_mod.reshape(N_DEV, 1, WM), ((0, 0), (0, 7), (0, 0)))
    mod3, ca = _mod_fwd(c_tile, w_mod[0], b_mod3)
    mod9 = mod3[:, 0, :].reshape(N_MOD, D)

    shards = {"g1T": ffn1_gate[0].T.astype(BF16), "u1T": ffn1_up[0].T.astype(BF16), "d1": ffn1_down[0].astype(BF16),
              "g3T": ffn2_gate[0].T.astype(BF16), "u3T": ffn2_up[0].T.astype(BF16), "d3": ffn2_down[0].astype(BF16),
              "w_inT": w_in[0].T, "w_uqT": w_uq[0].T.astype(BF16), "w_ukv": w_ukv[0].astype(BF16),
              "w_o": w_o[0].astype(BF16)}
    me = 4 * mx + 2 * my + mc
    groups = {"ffn1": ("g1T", "u1T", "d1"), "mixer": ("w_inT", "w_uqT", "w_ukv", "w_o"), "ffn2": ("g3T", "u3T", "d3")}
    arriving = {}

    def as_weights(group, gathered):
        return {k: g if k == "w_ukv" else g.reshape(N_DEV * g.shape[1], g.shape[2])
                for k, g in zip(groups[group], gathered)}

    later = groups["mixer"] + groups["ffn2"]
    place = {"mixer": 0, "ffn2": len(groups["mixer"])}

    def start_gather(token):
        lands = []
        for k in later:
            sh = shards[k] + token[0, 0].astype(shards[k].dtype)
            lands.append(lax.dynamic_update_slice(lax.empty((N_DEV,) + sh.shape, sh.dtype), sh[None], (me, 0, 0)))
        batches = [range(k0, k0 + len(groups[group])) for group, k0 in place.items()]
        send, recv, lands, started = _gather_start(lands, name="gather_start", batches=batches)
        for group, k0 in place.items():
            arriving[group] = (send, recv, lands[k0:k0 + len(groups[group])])
        return started

    def fetch(group, after, vecs):
        if group == "ffn1":
            *gathered, token = _wgather([shards[k] + ca[1, 0].astype(shards[k].dtype) for k in groups["ffn1"]])
            return as_weights("ffn1", gathered), vecs + start_gather(token)[0:1, 0:1]

        def pass_on(group, after):
            send, recv, lands = arriving[group]
            lands, token = _gather_pass(send, recv, lands, after, name="gather_landed_" + group, stage="landed",
                                        k0=place[group])
            lands, token = _gather_pass(send, recv, lands, [token], name="gather_onward_" + group, stage="onward",
                                        k0=place[group])
            arriving[group] = (send, recv, lands)
            return token

        if group == "ffn2_on_its_way":
            return None, vecs + pass_on("ffn2", after)[0:1, 0:1]
        if group == "mixer":
            after = [pass_on("mixer", after)]
        send, recv, lands = arriving[group]
        return as_weights(group, _gather_end(send, recv, lands, after, name="gather_end_" + group,
                                             k0=place[group])), vecs

    norms ={"ffn1": norm_ffn1, "mix": norm_mix, "ffn2": norm_ffn2, "final": norm_final.reshape(1, D)}
    in_flight = {}

    def on_grads(group, g, after, vecs):
        if group != "ffn1":
            if g is None:
                return vecs
            names = list(g)
            by_dest = [g[k] if k == "w_ukv" else g[k].reshape((N_DEV, g[k].shape[0] // N_DEV) + g[k].shape[1:])
                       for k in names]
            lands = [lax.empty((N_DEV - 1,) + a.shape[1:], a.dtype) for a in by_dest]
            send, recv, by_dest, lands, token = _split_start(_direct_copies, by_dest, lands, 7 * len(names), after,
                                                             name="rs_start_" + group)
            in_flight[group] = (names, send, recv, by_dest, lands, token)
            return vecs + token[0:1, 0:1]
        names = list(g)
        by_dest = [g[k].reshape((N_DEV, g[k].shape[0] // N_DEV) + g[k].shape[1:]) for k in names]
        lands = [lax.empty((4,) + a.shape[1:], a.dtype) for a in by_dest]
        send, recv, by_dest, lands, token = _split_start(_d2d_copies, by_dest, lands, 4 * len(names), after,
                                                         name="rs_d2d_start_" + group)
        finish("ffn2", [token])
        by_dest, from_sib = _split_wait(_d2d_copies, send, recv, by_dest, lands, [done[-1]], name="rs_d2d_wait_" + group)
        sums = _chipsum(by_dest, from_sib, cidx, name="chipsum_" + group)
        halves = lambda: [lax.empty((2, s.shape[1] // 2) + s.shape[2:], s.dtype) for s in sums]
        send, recv, sums, lands, token = _split_start(_rs_step1_copies, sums, halves() + halves(), 4 * len(names), [],
                                                      name="rs_ici_start_" + group)
        in_flight[group] = (names, send, recv, sums, lands, token)
        return vecs + token[0:1, 0:1]

    owners = {"g1T": ("ffn1_gate", ffn1_gate, m_ffn1_gate, v_ffn1_gate), "u1T": ("ffn1_up", ffn1_up, m_ffn1_up, v_ffn1_up),
              "d1": ("ffn1_down", ffn1_down, m_ffn1_down, v_ffn1_down),
              "g3T": ("ffn2_gate", ffn2_gate, m_ffn2_gate, v_ffn2_gate), "u3T": ("ffn2_up", ffn2_up, m_ffn2_up, v_ffn2_up),
              "d3": ("ffn2_down", ffn2_down, m_ffn2_down, v_ffn2_down),
              "w_inT": ("w_in", w_in, m_w_in, v_w_in), "w_uqT": ("w_uq", w_uq, m_w_uq, v_w_uq),
              "w_ukv": ("w_ukv", w_ukv, m_w_ukv, v_w_ukv), "w_o": ("w_o", w_o, m_w_o, v_w_o)}
    res, done = {}, []

    there = lambda k, a: a[0].T if k in TRANSPOSED else a[0]
    back = lambda k, a: a.T[None] if k in TRANSPOSED else a[None]

    def record(names, outs):
        for k, out in zip(names, outs):
            done.append(out[3])
            res[owners[k][0]] = tuple(back(k, a) for a in out)

    def finish(group, after):
        names, send, recv, sums, lands, _ = in_flight[group]
        wmv = [tuple(there(k, a) for a in owners[k][1:]) for k in names]
        own = jnp.reshape(me, (1,)).astype(jnp.int32)
        sums, lands = _split_wait(_direct_copies, send, recv, sums, lands, after, name="rs_wait_" + group)
        record(names, _adamw_rs(wmv, sums, lands, own, name="adamw_" + group))

    _, grad_x, _, vec = _local_step(
        x[0], loss_target[0], mod9, norms, sinks, rel_bias, q_norm, kv_norm, fetch, on_grads=on_grads)

    names, send, recv, sums, lands, step1_started = in_flight["ffn1"]
    vec = vec.reshape(1, 1, N_VEC)
    allvec = lax.dynamic_update_slice(lax.empty((N_DEV, 1, N_VEC), F32), vec, (me, 0, 0))
    vsend, vrecv, _, (allvec,), vec_started = _split_start(_vec_copies, [], [allvec], N_DEV - 1, [step1_started],
                                                           name="vec_start")
    finish("mixer", [vec_started])
    n = len(names)
    sums, lands = _split_wait(_rs_step1_copies, send, recv, sums, lands, [done[-1]], name="rs_ici_wait_ffn1")
    direct, relay = lands[:n], lands[n:]
    qxy = jnp.stack([2 * (1 - mx) + my, 2 * mx + (1 - my)]).astype(jnp.int32)
    relayed = _relay_sum(sums, relay, qxy, name="relay_sum_ffn1")
    second = [lax.empty(a.shape, a.dtype) for a in relayed]
    send, recv, relayed, second, step2_started = _split_start(_rs_step2_copies, relayed, second, 2 * n, [],
                                                              name="rs_ici_start2_ffn1")

    _, (allvec,) = _split_wait(_vec_copies, vsend, vrecv, [], [allvec], [step2_started], name="vec_wait")
    g_wmod, gvec = _mod_bwd(allvec, ca, jnp.reshape(me, (1,)).astype(jnp.int32))
    loss = gvec[0, N_MODVEC + LOSS_SLOT]
    small_in = {"norm_ffn1": (norm_ffn1, m_norm_ffn1, v_norm_ffn1), "norm_mix": (norm_mix, m_norm_mix, v_norm_mix),
                "norm_ffn2": (norm_ffn2, m_norm_ffn2, v_norm_ffn2), "norm_final": (norm_final, m_norm_final, v_norm_final),
                "q_norm": (q_norm, m_q_norm, v_q_norm), "kv_norm": (kv_norm, m_kv_norm, v_kv_norm),
                "sinks": (sinks, m_sinks, v_sinks), "rel_bias": (rel_bias, m_rel_bias, v_rel_bias),
                "b_mod": (b_mod, m_b_mod, v_b_mod)}
    as_row = lambda k, a: a.T if k == "rel_bias" else a.reshape(1, -1)
    from_row = lambda k, a: a.T if k == "rel_bias" else a.reshape(small_in[k][0].shape)
    small_out = _adamw_small(gvec, [as_row(k, a) for k in SMALL_PARAMS for a in small_in[k]])
    for k in SMALL_PARAMS:
        res[k] = tuple(from_row(k, a) for a in small_out[k])

    out = _adamw(w_mod[0], g_wmod, m_w_mod[0], v_w_mod[0], name="adamw_w_mod")
    res["w_mod"] = tuple(a[None] for a in (g_wmod,) + tuple(out))

    wmv = [tuple(there(k, a) for a in owners[k][1:]) for k in names]
    after = done + [out[2]] + [a for k in SMALL_PARAMS for a in res[k]]
    outs = []
    for i, k in enumerate(names):
        _, (sec,) = _split_wait(functools.partial(_rs_step2_copies, k0=i), send, recv, [relayed[i]], [second[i]],
                                after, name="rs_ici_wait2_" + k)
        outs.append(_adamw_rs2([wmv[i]], [sums[i]], [direct[i]], [sec], qidx, name="adamw_" + owners[k][0])[0])
        after = [outs[-1][3]]
    record(names, outs)

    order = ("w_mod", "b_mod", "norm_ffn1", "ffn1_gate", "ffn1_up", "ffn1_down", "norm_mix", "w_in", "q_norm",
             "kv_norm", "w_uq", "w_ukv", "sinks", "w_o", "norm_ffn2", "ffn2_gate", "ffn2_up", "ffn2_down",
             "rel_bias", "norm_final")
    return (loss, grad_x[None]) + tuple(res[nm][kind] for kind in range(4) for nm in order)
```

```python
import functools
import math

import numpy as np
import jax
import jax.numpy as jnp
from jax import lax
from jax.experimental import pallas as pl
from jax.experimental.pallas import tpu as pltpu

F32 = jnp.float32
BF16 = jnp.bfloat16
MESH = pl.DeviceIdType.MESH

N_DEV = 8
D = 1024
D_FF = 2816
EPS = 1e-6
N_MOD = 9
SWA_HEADS = 8
SWA_DH = 64
WINDOW = 128
MLA_HEADS = 4
MLA_NOPE = 128
MLA_ROPE = 64
MLA_V = 128
MLA_QR = 256
MLA_KVR = 128
ROPE_THETA = 10000.0
NUM_BUCKETS = 32
D_IN = 1216
D_IN_PAD = 1280
SWA_SCALE = SWA_DH ** -0.5
MLA_SCALE = (MLA_NOPE + MLA_ROPE) ** -0.5

ADAM_LR = 0.001
ADAM_B1 = 0.9
ADAM_B2 = 0.999
ADAM_EPS = 1e-08
ADAM_WD = 0.01
ADAM_STEP = 10

V7X_VMEM_LIMIT = 56 * 1024 * 1024
ROW_TILE = 512

NT_DIMS = (((1,), (1,)), ((), ()))
TN_DIMS = (((0,), (0,)), ((), ()))


def _dot(a, b):
    return jnp.dot(a, b, preferred_element_type=F32)


def _dot_nt(a, b):
    return lax.dot_general(a, b, NT_DIMS, preferred_element_type=F32)


def _dot_tn(a, b):
    return lax.dot_general(a, b, TN_DIMS, preferred_element_type=F32)


def _params(sem=None):
    return pltpu.CompilerParams(dimension_semantics=sem, vmem_limit_bytes=V7X_VMEM_LIMIT)


def _rstd(x):
    return lax.rsqrt(jnp.mean(x * x, axis=-1, keepdims=True) + EPS)


def _rms_bwd(dy, xhat, r):
    return r * (dy - xhat * jnp.mean(dy * xhat, axis=-1, keepdims=True))


def _sigmoid(a):
    return 1.0 / (1.0 + jnp.exp(-a))


def _ffn_fwd(x, vecs, wgT, wuT, wd, *, name, tm=256, tf=D_FF):
    S, F = x.shape[0], wd.shape[0]
    tm = min(tm, S)
    ni, nj = S // tm, F // tf

    def body(x_ref, vec_ref, wg_ref, wu_ref, wd_ref, xo_ref, h_ref, a_ref, b_ref, f_ref, acc_ref):
        j = pl.program_id(1)

        @pl.when(j == 0)
        def _():
            xv = x_ref[...]
            hn = xv * _rstd(xv) * vec_ref[0:1, :]
            h_ref[...] = (hn * (1.0 + vec_ref[2:3, :]) + vec_ref[1:2, :]).astype(BF16)

        h = h_ref[...]
        a = _dot_nt(h, wg_ref[...])
        b = _dot_nt(h, wu_ref[...])
        a_ref[...] = a.astype(BF16)
        b_ref[...] = b.astype(BF16)
        part = _dot((a * _sigmoid(a) * b).astype(BF16), wd_ref[...])

        def finish(f):
            f_ref[...] = f
            xo_ref[...] = x_ref[...] + (0.5 * vec_ref[3:4, :]) * f

        if nj == 1:
            finish(part)
        else:
            @pl.when(j == 0)
            def _():
                acc_ref[...] = part

            @pl.when((j > 0) & (j < nj - 1))
            def _():
                acc_ref[...] += part

            @pl.when(j == nj - 1)
            def _():
                finish(acc_ref[...] + part)

    row = pl.BlockSpec((tm, D), lambda i, j: (i, 0))
    wspec = pl.BlockSpec((tf, D), lambda i, j: (j, 0), pipeline_mode=pl.Buffered(1) if nj == 1 else None)
    act = pl.BlockSpec((tm, tf), lambda i, j: (i, j))
    return pl.pallas_call(
        body, name=name, grid=(ni, nj),
        in_specs=[row, pl.BlockSpec((8, D), lambda i, j: (0, 0)), wspec, wspec, wspec],
        out_specs=[row, row, act, act, row],
        out_shape=[jax.ShapeDtypeStruct((S, D), F32), jax.ShapeDtypeStruct((S, D), BF16),
                   jax.ShapeDtypeStruct((S, F), BF16), jax.ShapeDtypeStruct((S, F), BF16),
                   jax.ShapeDtypeStruct((S, D), F32)],
        scratch_shapes=[pltpu.VMEM((tm, D) if nj > 1 else (8, 128), F32)],
        compiler_params=_params(("parallel", "arbitrary")),
    )(x, vecs, wgT, wuT, wd)


def _ffn_bwd_main(h, df, a, b, wgT, wuT, wd, *, name, after=(), tm=2048, tf=256):
    S = h.shape[0]
    tm = min(tm, S)
    ni, nj = S // tm, D_FF // tf

    def body(h_hbm, df_hbm, a_ref, b_ref, wg_ref, wu_ref, wd_ref, *rest):
        gg_ref, gu_ref, gd_ref, dh_hbm, h_v, df_v, dh_v, gg_acc, gu_acc, gd_acc, sem = rest[len(after):]
        j = pl.program_id(0)
        i = pl.program_id(1)

        @pl.when((j == 0) & (i == 0))
        def _():
            c1 = pltpu.make_async_copy(h_hbm, h_v, sem.at[0])
            c2 = pltpu.make_async_copy(df_hbm, df_v, sem.at[1])
            c1.start()
            c2.start()
            c1.wait()
            c2.wait()

        @pl.when(i == 0)
        def _():
            gg_acc[...] = jnp.zeros_like(gg_acc)
            gu_acc[...] = jnp.zeros_like(gu_acc)
            gd_acc[...] = jnp.zeros_like(gd_acc)

        rows = pl.ds(pl.multiple_of(i * tm, tm), tm)
        hi = h_v[rows, :]
        dfi = df_v[rows, :]
        av = a_ref[...].astype(F32)
        bv = b_ref[...].astype(F32)
        sg = _sigmoid(av)
        sa = av * sg
        hsw = (sa * bv).astype(BF16)
        dhsw = _dot_nt(dfi, wd_ref[...])
        da = (dhsw * bv * (sg * (1.0 + av * (1.0 - sg)))).astype(BF16)
        db = (dhsw * sa).astype(BF16)
        gd_acc[...] += _dot_tn(hsw, dfi)
        gg_acc[...] += _dot_tn(da, hi)
        gu_acc[...] += _dot_tn(db, hi)
        dh = _dot(da, wg_ref[...]) + _dot(db, wu_ref[...])

        @pl.when(j == 0)
        def _():
            dh_v[rows, :] = dh

        @pl.when(j > 0)
        def _():
            dh_v[rows, :] += dh

        @pl.when(i == ni - 1)
        def _():
            gg_ref[...] = gg_acc[...].astype(BF16)
            gu_ref[...] = gu_acc[...].astype(BF16)
            gd_ref[...] = gd_acc[...].astype(BF16)

        @pl.when((j == nj - 1) & (i == ni - 1))
        def _():
            c3 = pltpu.make_async_copy(dh_v, dh_hbm, sem.at[2])
            c3.start()
            c3.wait()

    anyspec = pl.BlockSpec(memory_space=pl.ANY)
    wspec = pl.BlockSpec((tf, D), lambda j, i: (j, 0))
    act = pl.BlockSpec((tm, tf), lambda j, i: (i, j))
    return pl.pallas_call(
        body, name=name, grid=(nj, ni),
        in_specs=[anyspec, anyspec, act, act, wspec, wspec, wspec] + [anyspec] * len(after),
        out_specs=[wspec, wspec, wspec, anyspec],
        out_shape=[jax.ShapeDtypeStruct((D_FF, D), BF16)] * 3 + [jax.ShapeDtypeStruct((S, D), F32)],
        scratch_shapes=[pltpu.VMEM((S, D), BF16), pltpu.VMEM((S, D), BF16), pltpu.VMEM((S, D), F32),
                        pltpu.VMEM((tf, D), F32), pltpu.VMEM((tf, D), F32), pltpu.VMEM((tf, D), F32),
                        pltpu.SemaphoreType.DMA((3,))],
        compiler_params=_params(("arbitrary", "arbitrary")),
    )(h, df, a, b, wgT, wuT, wd, *after)


def _ffn_out_bwd(dx, f, gate, df_ref, part_ref):
    df_ref[...] = ((0.5 * gate) * dx).astype(BF16)
    part_ref[3:4, :] += 0.5 * jnp.sum(dx * f, axis=0, keepdims=True)


def _norm_bwd(dh, x, dxo, vecs, *, name, below=None, tm=ROW_TILE):
    S = x.shape[0]
    tm = min(tm, S)

    def body(dh_ref, x_ref, dxo_ref, vec_ref, *rest):
        dx_ref, part_ref = rest[-2 if below is None else -3], rest[-1 if below is None else -2]

        @pl.when(pl.program_id(0) == 0)
        def _():
            part_ref[...] = jnp.zeros_like(part_ref)

        dh = dh_ref[...]
        xv = x_ref[...]
        r = _rstd(xv)
        xhat = xv * r
        w = vec_ref[0:1, :]
        xn = xhat * w
        dxn = dh * (1.0 + vec_ref[2:3, :])
        part_ref[0:1, :] += jnp.sum(dxn * xhat, axis=0, keepdims=True)
        part_ref[1:2, :] += jnp.sum(dh, axis=0, keepdims=True)
        part_ref[2:3, :] += jnp.sum(dh * xn, axis=0, keepdims=True)
        dx = dxo_ref[...] + _rms_bwd(dxn * w, xhat, r)
        dx_ref[...] = dx
        if below is not None:
            _ffn_out_bwd(dx, rest[0][...], rest[1][3:4, :], rest[-1], part_ref)

    row = pl.BlockSpec((tm, D), lambda i: (i, 0))
    vec = pl.BlockSpec((8, D), lambda i: (0, 0))
    extra = [] if below is None else [row, vec]
    return pl.pallas_call(
        body, name=name, grid=(S // tm,), in_specs=[row, row, row, vec] + extra,
        out_specs=[row, vec] + ([] if below is None else [row]),
        out_shape=[jax.ShapeDtypeStruct((S, D), F32), jax.ShapeDtypeStruct((8, D), F32)]
        + ([] if below is None else [jax.ShapeDtypeStruct((S, D), BF16)]),
        compiler_params=_params(("arbitrary",)),
    )(dh, x, dxo, vecs, *([] if below is None else below))


def _head(x, tgt, nf, f, vecs, *, tm=ROW_TILE):
    S = x.shape[0]
    tm = min(tm, S)

    def body(x_ref, t_ref, nf_ref, f_ref, vec_ref, dx_ref, part_ref, df_ref):
        @pl.when(pl.program_id(0) == 0)
        def _():
            part_ref[...] = jnp.zeros_like(part_ref)

        xv = x_ref[...]
        r = _rstd(xv)
        xhat = xv * r
        w = nf_ref[...]
        e = xhat * w - t_ref[...]
        dy = e * (1.0 / D)
        part_ref[0:1, :] += jnp.sum(dy * xhat, axis=0, keepdims=True)
        part_ref[1:2, :] += jnp.sum(e * e) * (0.5 / D)
        dx = _rms_bwd(dy * w, xhat, r)
        dx_ref[...] = dx
        _ffn_out_bwd(dx, f_ref[...], vec_ref[3:4, :], df_ref, part_ref)

    row = pl.BlockSpec((tm, D), lambda i: (i, 0))
    vec = pl.BlockSpec((8, D), lambda i: (0, 0))
    return pl.pallas_call(
        body, name="head", grid=(S // tm,),
        in_specs=[row, row, pl.BlockSpec((1, D), lambda i: (0, 0)), row, vec],
        out_specs=[row, vec, row],
        out_shape=[jax.ShapeDtypeStruct((S, D), F32), jax.ShapeDtypeStruct((8, D), F32),
                   jax.ShapeDtypeStruct((S, D), BF16)],
        compiler_params=_params(("arbitrary",)),
    )(x, tgt, nf, f, vecs)


def _mix_in_fwd(x, vecs, w_inT, *, tm=ROW_TILE):
    S = x.shape[0]
    tm = min(tm, S)

    def body(x_ref, vec_ref, w_ref, h_ref, p_ref):
        xv = x_ref[...]
        hn = xv * _rstd(xv) * vec_ref[0:1, :]
        h = (hn * (1.0 + vec_ref[2:3, :]) + vec_ref[1:2, :]).astype(BF16)
        h_ref[...] = h
        p_ref[...] = _dot_nt(h, w_ref[...])

    row = pl.BlockSpec((tm, D), lambda i: (i, 0))
    return pl.pallas_call(
        body, name="mix_in_fwd", grid=(S // tm,),
        in_specs=[row, pl.BlockSpec((8, D), lambda i: (0, 0)), pl.BlockSpec((D_IN_PAD, D), lambda i: (0, 0))],
        out_specs=[row, pl.BlockSpec((tm, D_IN_PAD), lambda i: (i, 0))],
        out_shape=[jax.ShapeDtypeStruct((S, D), BF16), jax.ShapeDtypeStruct((S, D_IN_PAD), F32)],
        compiler_params=_params(("parallel",)),
    )(x, vecs, w_inT)


def _bucket_table():
    qi = np.arange(WINDOW)[:, None]
    kj = np.arange(2 * WINDOW)[None, :]
    dist = qi + WINDOW - kj
    max_exact = NUM_BUCKETS // 2
    n = np.maximum(dist, 0)
    nf = np.maximum(n, 1).astype(np.float32)
    large = max_exact + (np.log(nf / np.float32(max_exact)) / np.float32(math.log(WINDOW / max_exact))
                         * np.float32(NUM_BUCKETS - max_exact)).astype(np.int32)
    large = np.minimum(large, NUM_BUCKETS - 1)
    return np.where(n < max_exact, n, large).astype(np.int32)


def _bias_build(rel_bias, bucket):
    def body(rb_ref, bk_ref, out_ref):
        bk = bk_ref[...]
        for h in range(SWA_HEADS):
            acc = jnp.zeros((WINDOW, 2 * WINDOW), F32)
            for b in range(NUM_BUCKETS):
                acc = jnp.where(bk == b, rb_ref[b, h], acc)
            out_ref[h] = acc

    return pl.pallas_call(
        body, name="bias_build",
        in_specs=[pl.BlockSpec(memory_space=pltpu.SMEM), pl.BlockSpec(memory_space=pltpu.VMEM)],
        out_specs=pl.BlockSpec(memory_space=pltpu.VMEM),
        out_shape=jax.ShapeDtypeStruct((SWA_HEADS, WINDOW, 2 * WINDOW), F32),
    )(rel_bias, bucket)


SWA_GROUP = 4
GROUP_ROWS = SWA_GROUP * WINDOW


SWA_SUB = 2


def _swa_valid(has_prev):
    row = lax.broadcasted_iota(jnp.int32, (GROUP_ROWS, 2 * WINDOW), 0) % WINDOW
    col = lax.broadcasted_iota(jnp.int32, (GROUP_ROWS, 2 * WINDOW), 1)
    dist = row + WINDOW - col
    return (dist >= 0) & (dist < WINDOW) & ((col >= WINDOW) | has_prev)


def _swa_keys(prev_ref, cur_ref, u):
    cur = cur_ref[...]
    before = prev_ref[...] if u == 0 else cur[WINDOW * (u - 1):WINDOW * u]
    return jnp.concatenate([before, cur[WINDOW * u:WINDOW * (u + 1)]], axis=0).astype(BF16)


def _stack_heads(x, g):
    return jnp.concatenate([x[:, 64 * h:64 * h + 64] for h in range(SWA_GROUP * g, SWA_GROUP * (g + 1))], axis=0)


def _unstack_heads(x4):
    return jnp.concatenate([x4[WINDOW * a:WINDOW * (a + 1)] for a in range(SWA_GROUP)], axis=1)


def _group_sinks(sink_ref, g):
    head = lax.broadcasted_iota(jnp.int32, (GROUP_ROWS, 1), 0) // WINDOW
    out = jnp.full((GROUP_ROWS, 1), sink_ref[0, SWA_GROUP * g], F32)
    for a in range(1, SWA_GROUP):
        out = jnp.where(head == a, sink_ref[0, SWA_GROUP * g + a], out)
    return out


def _swa_probs(qh, kk, bias_h, sink, valid):
    s = _dot_nt(qh, kk) * SWA_SCALE + bias_h
    s = jnp.where(valid, s, -jnp.inf)
    m = jnp.maximum(jnp.max(s, axis=-1, keepdims=True), sink)
    p = jnp.exp(s - m)
    ps = jnp.exp(sink - m)
    inv = 1.0 / (jnp.sum(p, axis=-1, keepdims=True) + ps)
    return p * inv, ps * inv


SWA_ROWS = SWA_SUB * WINDOW


def _swa_specs():
    prev = lambda n: jnp.maximum(SWA_SUB * n - 1, 0)
    return [pl.BlockSpec((SWA_ROWS, 512), lambda n: (n, 0)),
            pl.BlockSpec((SWA_ROWS, 128), lambda n: (n, 4)),
            pl.BlockSpec((WINDOW, 128), lambda n: (prev(n), 4)),
            pl.BlockSpec((SWA_ROWS, 128), lambda n: (n, 5)),
            pl.BlockSpec((WINDOW, 128), lambda n: (prev(n), 5)),
            pl.BlockSpec((SWA_HEADS, WINDOW, 2 * WINDOW), lambda n: (0, 0, 0)),
            pl.BlockSpec(memory_space=pltpu.SMEM)]


def _swa_fwd(proj, bias, sinks):
    S = proj.shape[0]

    def body(q_ref, kc_ref, kp_ref, vc_ref, vp_ref, bias_ref, sink_ref, o_ref):
        n = pl.program_id(0)
        for u in range(SWA_SUB):
            rows = slice(WINDOW * u, WINDOW * (u + 1))
            valid = _swa_valid(n > 0 if u == 0 else True)
            q = q_ref[rows, :].astype(BF16)
            kfull = _swa_keys(kp_ref, kc_ref, u)
            vfull = _swa_keys(vp_ref, vc_ref, u)
            for g in range(SWA_HEADS // SWA_GROUP):
                kk = kfull[:, 64 * g:64 * g + 64]
                vv = vfull[:, 64 * g:64 * g + 64]
                bias4 = bias_ref[SWA_GROUP * g:SWA_GROUP * (g + 1)].reshape(GROUP_ROWS, 2 * WINDOW)
                pk, _ = _swa_probs(_stack_heads(q, g), kk, bias4, _group_sinks(sink_ref, g), valid)
                o_ref[rows, 256 * g:256 * (g + 1)] = _unstack_heads(_dot(pk.astype(BF16), vv))

    return pl.pallas_call(
        body, name="swa_fwd", grid=(S // SWA_ROWS,),
        in_specs=_swa_specs(),
        out_specs=pl.BlockSpec((SWA_ROWS, 512), lambda n: (n, 0)),
        out_shape=jax.ShapeDtypeStruct((S, 512), F32),
        compiler_params=_params(("parallel",)),
    )(proj, proj, proj, proj, proj, bias, sinks)


def _swa_bwd(proj, bias, sinks, o, do, bucket):
    S = proj.shape[0]
    nb = S // SWA_ROWS

    def body(q_ref, kc_ref, kp_ref, vc_ref, vp_ref, bias_ref, sink_ref, o_ref, do_ref, bk_ref,
             dq_ref, dk_ref, dv_ref, drb_ref, dsk_ref, dbias_acc):
        n = pl.program_id(0)

        @pl.when(n == 0)
        def _():
            dk_ref[...] = jnp.zeros_like(dk_ref)
            dv_ref[...] = jnp.zeros_like(dv_ref)
            dsk_ref[...] = jnp.zeros_like(dsk_ref)
            dbias_acc[...] = jnp.zeros_like(dbias_acc)
            drb_ref[...] = jnp.zeros_like(drb_ref)

        for u in range(SWA_SUB):
            rows = slice(WINDOW * u, WINDOW * (u + 1))
            blk = SWA_SUB * n + u
            valid = _swa_valid(n > 0 if u == 0 else True)
            q = q_ref[rows, :].astype(BF16)
            dov = do_ref[rows, :]
            ov = o_ref[rows, :]
            kfull = _swa_keys(kp_ref, kc_ref, u)
            vfull = _swa_keys(vp_ref, vc_ref, u)
            prow = pl.ds(pl.multiple_of(jnp.maximum(blk - 1, 0) * WINDOW, WINDOW), WINDOW)
            crow = pl.ds(pl.multiple_of(blk * WINDOW, WINDOW), WINDOW)
            for g in range(SWA_HEADS // SWA_GROUP):
                heads = slice(SWA_GROUP * g, SWA_GROUP * (g + 1))
                kk = kfull[:, 64 * g:64 * g + 64]
                vv = vfull[:, 64 * g:64 * g + 64]
                q4 = _stack_heads(q, g)
                pk, psink = _swa_probs(q4, kk, bias_ref[heads].reshape(GROUP_ROWS, 2 * WINDOW),
                                       _group_sinks(sink_ref, g), valid)
                pkb = pk.astype(BF16)
                do4 = _stack_heads(dov, g)
                dob = do4.astype(BF16)
                dp = _dot_nt(dob, vv)
                delta = jnp.sum(do4 * _stack_heads(ov, g), axis=-1, keepdims=True)
                ds = pk * (dp - delta)
                dsink = -psink * delta
                for a in range(SWA_GROUP):
                    h = SWA_GROUP * g + a
                    part = jnp.sum(dsink[WINDOW * a:WINDOW * (a + 1)], keepdims=True)
                    dsk_ref[h:h + 1, :] += jnp.broadcast_to(part, (1, 128))
                dbias_acc[heads] += ds.reshape(SWA_GROUP, WINDOW, 2 * WINDOW)
                dsb = (ds * SWA_SCALE).astype(BF16)
                dq_ref[rows, 256 * g:256 * (g + 1)] = _unstack_heads(_dot(dsb, kk))
                dkk = _dot_tn(dsb, q4)
                dvv = _dot_tn(pkb, dob)
                dk_ref[prow, 64 * g:64 * g + 64] += dkk[:WINDOW]
                dk_ref[crow, 64 * g:64 * g + 64] += dkk[WINDOW:]
                dv_ref[prow, 64 * g:64 * g + 64] += dvv[:WINDOW]
                dv_ref[crow, 64 * g:64 * g + 64] += dvv[WINDOW:]

        @pl.when(n == nb - 1)
        def _():
            bk = bk_ref[...]
            for h in range(SWA_HEADS):
                dbh = dbias_acc[h]
                for b in range(NUM_BUCKETS):
                    val = jnp.sum(jnp.where(bk == b, dbh, 0.0), keepdims=True)
                    row = h * NUM_BUCKETS + b
                    drb_ref[row:row + 1, :] = jnp.broadcast_to(val, (1, 128))

    full = lambda shape: pl.BlockSpec(shape, lambda n: tuple(0 for _ in shape))
    return pl.pallas_call(
        body, name="swa_bwd", grid=(nb,),
        in_specs=_swa_specs() + [pl.BlockSpec((SWA_ROWS, 512), lambda n: (n, 0)),
                                 pl.BlockSpec((SWA_ROWS, 512), lambda n: (n, 0)), full((WINDOW, 2 * WINDOW))],
        out_specs=[pl.BlockSpec((SWA_ROWS, 512), lambda n: (n, 0)), full((S, 128)), full((S, 128)),
                   full((NUM_BUCKETS * 8, 128)), full((8, 128))],
        out_shape=[jax.ShapeDtypeStruct((S, 512), F32), jax.ShapeDtypeStruct((S, 128), F32),
                   jax.ShapeDtypeStruct((S, 128), F32), jax.ShapeDtypeStruct((NUM_BUCKETS * 8, 128), F32),
                   jax.ShapeDtypeStruct((8, 128), F32)],
        scratch_shapes=[pltpu.VMEM((SWA_HEADS, WINDOW, 2 * WINDOW), F32)],
        compiler_params=_params(("arbitrary",)),
    )(proj, proj, proj, proj, proj, bias, sinks, o, do, bucket)


def _rope_tables(S):
    inv = np.float32(ROPE_THETA) ** (-np.arange(0, MLA_ROPE, 2, dtype=np.float32) / np.float32(MLA_ROPE))
    ang = np.arange(S, dtype=np.float32)[:, None] * inv[None, :]
    cos, sin = np.cos(ang), np.sin(ang)
    return (jnp.asarray(np.tile(np.concatenate([cos, cos], axis=1), (1, 2))),
            jnp.asarray(np.tile(np.concatenate([-sin, sin], axis=1), (1, 2))))


def _rope_wide(ref):
    t = ref[...]
    return jnp.concatenate([t, t], axis=1)


def _swap_halves(x):
    w = x.shape[-1]
    lane = lax.broadcasted_iota(jnp.int32, x.shape, x.ndim - 1)
    return jnp.where((lane % 64) < 32, pltpu.roll(x, w - 32, x.ndim - 1), pltpu.roll(x, 32, x.ndim - 1))


def _mla_pre_fwd(proj, qn_w, kvn_w, wuqT, wukv, cos, sin, *, tm=ROW_TILE):
    S = proj.shape[0]
    tm = min(tm, S)

    def body(ql_ref, kl_ref, kr_ref, qw_ref, kw_ref, wuq_ref, wukv_ref, cos_ref, sin_ref,
             qc_ref, kc_ref, vv_ref):
        ql = ql_ref[...]
        qn = (ql * _rstd(ql) * qw_ref[...]).astype(BF16)
        q = _dot_nt(qn, wuq_ref[...])
        cs, sn = _rope_wide(cos_ref), _rope_wide(sin_ref)
        qr = q[:, 512:768]
        qr = qr * cs + _swap_halves(qr) * sn
        half = lax.broadcasted_iota(jnp.int32, (tm, 128), 1) // 64
        kl = kl_ref[...]
        kvn = (kl * _rstd(kl) * kw_ref[...]).astype(BF16)
        kr = kr_ref[...]
        kr = kr * cs[:, :128] + _swap_halves(kr) * sn[:, :128]
        kr2 = (kr + pltpu.roll(kr, 64, 1)).astype(BF16)
        for h in range(MLA_HEADS):
            qc_ref[h, :, 0:128] = q[:, 128 * h:128 * h + 128].astype(BF16)
            chunk = qr[:, 128 * (h // 2):128 * (h // 2) + 128]
            qc_ref[h, :, 128:256] = jnp.where(half == (h % 2), chunk, 0.0).astype(BF16)
            kc_ref[h, :, 0:128] = _dot(kvn, wukv_ref[2 * h]).astype(BF16)
            kc_ref[h, :, 128:256] = kr2
            vv_ref[h] = _dot(kvn, wukv_ref[2 * h + 1]).astype(BF16)

    const = lambda shape: pl.BlockSpec(shape, lambda i: tuple(0 for _ in shape))
    return pl.pallas_call(
        body, name="mla_pre_fwd", grid=(S // tm,),
        in_specs=[pl.BlockSpec((tm, 256), lambda i: (i, 3)), pl.BlockSpec((tm, 128), lambda i: (i, 8)),
                  pl.BlockSpec((tm, 128), lambda i: (i, 9)), const((1, 256)), const((1, 128)),
                  const((768, 256)), const((8, 128, 128)),
                  pl.BlockSpec((tm, 128), lambda i: (i, 0)), pl.BlockSpec((tm, 128), lambda i: (i, 0))],
        out_specs=[pl.BlockSpec((MLA_HEADS, tm, 256), lambda i: (0, i, 0)),
                   pl.BlockSpec((MLA_HEADS, tm, 256), lambda i: (0, i, 0)),
                   pl.BlockSpec((MLA_HEADS, tm, 128), lambda i: (0, i, 0))],
        out_shape=[jax.ShapeDtypeStruct((MLA_HEADS, S, 256), BF16), jax.ShapeDtypeStruct((MLA_HEADS, S, 256), BF16),
                   jax.ShapeDtypeStruct((MLA_HEADS, S, 128), BF16)],
        compiler_params=_params(("parallel",)),
    )(proj, proj, proj, qn_w, kvn_w, wuqT, wukv, cos, sin)


def _causal(i, j, t):
    row = i * t + lax.broadcasted_iota(jnp.int32, (t, t), 0)
    col = j * t + lax.broadcasted_iota(jnp.int32, (t, t), 1)
    return col <= row


def _mla_attn_fwd(qc, kc, vv, *, t=512):
    S = qc.shape[1]
    t = min(t, S)

    def body(q_ref, k_ref, v_ref, o_ref, l_ref):
        i = pl.program_id(0)
        diag = _causal(0, 0, t)

        def step(j, carry, masked):
            rows = pl.ds(pl.multiple_of(j * t, t), t)
            out = []
            for h in range(MLA_HEADS):
                m, l, acc = carry[h]
                s = _dot_nt(q_ref[h], k_ref[h, rows, :]) * MLA_SCALE
                if masked:
                    s = jnp.where(diag, s, -jnp.inf)
                m_new = jnp.maximum(m, jnp.max(s, axis=-1, keepdims=True))
                alpha = jnp.exp(m - m_new)
                p = jnp.exp(s - m_new)
                l = alpha * l + jnp.sum(p, axis=-1, keepdims=True)
                acc = alpha * acc + _dot(p.astype(BF16), v_ref[h, rows, :])
                out.append((m_new, l, acc))
            return tuple(out)

        init = tuple((jnp.full((t, 1), -jnp.inf, F32), jnp.zeros((t, 1), F32), jnp.zeros((t, MLA_V), F32))
                     for _ in range(MLA_HEADS))
        carry = lax.fori_loop(0, i, lambda j, c: step(j, c, False), init)
        carry = step(i, carry, True)
        for h in range(MLA_HEADS):
            m, l, acc = carry[h]
            o_ref[:, 128 * h:128 * h + 128] = acc / l
            l_ref[h] = jnp.broadcast_to(m + jnp.log(l), (t, 128))

    return pl.pallas_call(
        body, name="mla_attn_fwd", grid=(S // t,),
        in_specs=[pl.BlockSpec((MLA_HEADS, t, 256), lambda i: (0, i, 0)),
                  pl.BlockSpec((MLA_HEADS, S, 256), lambda i: (0, 0, 0)),
                  pl.BlockSpec((MLA_HEADS, S, 128), lambda i: (0, 0, 0))],
        out_specs=[pl.BlockSpec((t, 512), lambda i: (i, 0)),
                   pl.BlockSpec((MLA_HEADS, t, 128), lambda i: (0, i, 0))],
        out_shape=[jax.ShapeDtypeStruct((S, 512), F32), jax.ShapeDtypeStruct((MLA_HEADS, S, 128), F32)],
        compiler_params=_params(("parallel",)),
    )(qc, kc, vv)


def _mla_attn_bwd(qc, kc, vv, o, lse, do, *, t=512, tq=1024):
    S = qc.shape[1]
    t = min(t, S)
    tq = min(tq, S)
    nblk = S // t
    hp = MLA_HEADS
    once = pl.Buffered(1)

    def body(q_ref, k_ref, v_ref, o_ref, l_ref, do_ref, dq_ref, dk_ref, dv_ref):
        j = pl.program_id(1)

        @pl.when(j == 0)
        def _():
            dq_ref[...] = jnp.zeros_like(dq_ref)

        first = (j * t) // tq

        def step(i, carry, masked):
            rows = pl.ds(pl.multiple_of(i * tq, tq), tq)
            if masked:
                row = i * tq + lax.broadcasted_iota(jnp.int32, (tq, t), 0)
                col = j * t + lax.broadcasted_iota(jnp.int32, (tq, t), 1)
                visible = col <= row
            out = []
            for h in range(hp):
                dk, dv = carry[h]
                k = k_ref[h]
                q = q_ref[h, rows, :]
                dov = do_ref[rows, 128 * h:128 * h + 128]
                lrow = l_ref[h, rows, :][:, 0:1]
                p = jnp.exp(_dot_nt(q, k) * MLA_SCALE - lrow)
                if masked:
                    p = jnp.where(visible, p, 0.0)
                dob = dov.astype(BF16)
                dv = dv + _dot_tn(p.astype(BF16), dob)
                dp = _dot_nt(dob, v_ref[h])
                delta = jnp.sum(dov * o_ref[rows, 128 * h:128 * h + 128], axis=-1, keepdims=True)
                ds = (p * (dp - delta) * MLA_SCALE).astype(BF16)
                dk = dk + _dot_tn(ds, q)
                dq_ref[h, rows, :] += _dot(ds, k)
                out.append((dk, dv))
            return tuple(out)

        init = tuple((jnp.zeros((t, 256), F32), jnp.zeros((t, MLA_V), F32)) for _ in range(hp))
        carry = step(first, init, True)
        carry = lax.fori_loop(first + 1, S // tq, lambda i, c: step(i, c, False), carry)
        for h in range(hp):
            dk_ref[h] = carry[h][0]
            dv_ref[h] = carry[h][1]

    return pl.pallas_call(
        body, name="mla_attn_bwd", grid=(MLA_HEADS // hp, nblk),
        in_specs=[pl.BlockSpec((hp, S, 256), lambda g, j: (g, 0, 0), pipeline_mode=once),
                  pl.BlockSpec((hp, t, 256), lambda g, j: (g, j, 0)),
                  pl.BlockSpec((hp, t, 128), lambda g, j: (g, j, 0)),
                  pl.BlockSpec((S, 128 * hp), lambda g, j: (0, g), pipeline_mode=once),
                  pl.BlockSpec((hp, S, 128), lambda g, j: (g, 0, 0), pipeline_mode=once),
                  pl.BlockSpec((S, 128 * hp), lambda g, j: (0, g), pipeline_mode=once)],
        out_specs=[pl.BlockSpec((hp, S, 256), lambda g, j: (g, 0, 0)),
                   pl.BlockSpec((hp, t, 256), lambda g, j: (g, j, 0)),
                   pl.BlockSpec((hp, t, 128), lambda g, j: (g, j, 0))],
        out_shape=[jax.ShapeDtypeStruct((MLA_HEADS, S, 256), F32), jax.ShapeDtypeStruct((MLA_HEADS, S, 256), F32),
                   jax.ShapeDtypeStruct((MLA_HEADS, S, 128), F32)],
        compiler_params=_params(("parallel", "arbitrary")),
    )(qc, kc, vv, o, lse, do)


def _mla_pre_bwd(proj, qn_w, kvn_w, wuqT, wukv, cos, sin, dqc, dkc, dvv, *, tm=ROW_TILE):
    S = proj.shape[0]
    tm = min(tm, S)

    def body(ql_ref, kl_ref, qw_ref, kw_ref, wuq_ref, wukv_ref, cos_ref, sin_ref, dqc_ref, dkc_ref, dvv_ref,
             dql_ref, dkl_ref, dkr_ref, gq_ref, gkv_ref, part_ref):
        @pl.when(pl.program_id(0) == 0)
        def _():
            gq_ref[...] = jnp.zeros_like(gq_ref)
            gkv_ref[...] = jnp.zeros_like(gkv_ref)
            part_ref[...] = jnp.zeros_like(part_ref)

        cs, sn = _rope_wide(cos_ref), _rope_wide(sin_ref)
        half = lax.broadcasted_iota(jnp.int32, (tm, 128), 1) // 64
        ql = ql_ref[...]
        rq = _rstd(ql)
        qhat = ql * rq
        qw = qw_ref[...]
        qn = (qhat * qw).astype(BF16)
        chunks = []
        for pair in range(2):
            chunks.append(jnp.where(half == 0, dqc_ref[2 * pair, :, 128:256], dqc_ref[2 * pair + 1, :, 128:256]))
        dqr = jnp.concatenate(chunks, axis=1)
        dqr = dqr * cs + _swap_halves(dqr * sn)
        dq = jnp.concatenate([dqc_ref[h, :, 0:128] for h in range(MLA_HEADS)] + [dqr], axis=1).astype(BF16)
        gq_ref[...] += _dot_tn(dq, qn)
        dqn = _dot(dq, wuq_ref[...])
        part_ref[0:1, :] += jnp.sum(dqn * qhat, axis=0, keepdims=True)
        dql_ref[...] = _rms_bwd(dqn * qw, qhat, rq)
        kl = kl_ref[...]
        rk = _rstd(kl)
        khat = kl * rk
        kw = kw_ref[...]
        kvn = (khat * kw).astype(BF16)
        dkvn = jnp.zeros((tm, MLA_KVR), F32)
        dkr2 = jnp.zeros((tm, 128), F32)
        for h in range(MLA_HEADS):
            dkn = dkc_ref[h, :, 0:128].astype(BF16)
            dvh = dvv_ref[h].astype(BF16)
            gkv_ref[2 * h] += _dot_tn(kvn, dkn)
            gkv_ref[2 * h + 1] += _dot_tn(kvn, dvh)
            dkvn += _dot_nt(dkn, wukv_ref[2 * h]) + _dot_nt(dvh, wukv_ref[2 * h + 1])
            dkr2 += dkc_ref[h, :, 128:256]
        part_ref[1:2, 0:128] += jnp.sum(dkvn * khat, axis=0, keepdims=True)
        dkl_ref[...] = _rms_bwd(dkvn * kw, khat, rk)
        dkr = jnp.where(half == 0, dkr2 + pltpu.roll(dkr2, 64, 1), 0.0)
        dkr_ref[...] = dkr * cs[:, :128] + _swap_halves(dkr * sn[:, :128])

    const = lambda shape: pl.BlockSpec(shape, lambda i: tuple(0 for _ in shape))
    heads = lambda w: pl.BlockSpec((MLA_HEADS, tm, w), lambda i: (0, i, 0))
    return pl.pallas_call(
        body, name="mla_pre_bwd", grid=(S // tm,),
        in_specs=[pl.BlockSpec((tm, 256), lambda i: (i, 3)), pl.BlockSpec((tm, 128), lambda i: (i, 8)),
                  const((1, 256)), const((1, 128)), const((768, 256)), const((8, 128, 128)),
                  pl.BlockSpec((tm, 128), lambda i: (i, 0)), pl.BlockSpec((tm, 128), lambda i: (i, 0)),
                  heads(256), heads(256), heads(128)],
        out_specs=[pl.BlockSpec((tm, 256), lambda i: (i, 0)), pl.BlockSpec((tm, 128), lambda i: (i, 0)),
                   pl.BlockSpec((tm, 128), lambda i: (i, 0)), const((768, 256)), const((8, 128, 128)), const((8, 256))],
        out_shape=[jax.ShapeDtypeStruct((S, 256), F32), jax.ShapeDtypeStruct((S, 128), F32),
                   jax.ShapeDtypeStruct((S, 128), F32), jax.ShapeDtypeStruct((768, 256), F32),
                   jax.ShapeDtypeStruct((8, 128, 128), F32), jax.ShapeDtypeStruct((8, 256), F32)],
        compiler_params=_params(("arbitrary",)),
    )(proj, proj, qn_w, kvn_w, wuqT, wukv, cos, sin, dqc, dkc, dvv)


def _mix_out_fwd(x, oa, ob, w_o, vecs, *, tm=ROW_TILE):
    S = x.shape[0]
    tm = min(tm, S)

    def body(x_ref, oa_ref, ob_ref, w_ref, vec_ref, xo_ref, mo_ref):
        mo = _dot(oa_ref[...].astype(BF16), w_ref[0:512, :]) + _dot(ob_ref[...].astype(BF16), w_ref[512:1024, :])
        mo_ref[...] = mo
        xo_ref[...] = x_ref[...] + vec_ref[3:4, :] * mo

    row = pl.BlockSpec((tm, D), lambda i: (i, 0))
    half = pl.BlockSpec((tm, 512), lambda i: (i, 0))
    return pl.pallas_call(
        body, name="mix_out_fwd", grid=(S // tm,),
        in_specs=[row, half, half, pl.BlockSpec((D, D), lambda i: (0, 0)), pl.BlockSpec((8, D), lambda i: (0, 0))],
        out_specs=[row, row],
        out_shape=[jax.ShapeDtypeStruct((S, D), F32), jax.ShapeDtypeStruct((S, D), F32)],
        compiler_params=_params(("parallel",)),
    )(x, oa, ob, w_o, vecs)


def _mix_out_bwd(dxo, mo, oa, ob, w_o, vecs, *, tm=ROW_TILE):
    S = dxo.shape[0]
    tm = min(tm, S)

    def body(dx_ref, mo_ref, oa_ref, ob_ref, w_ref, vec_ref, doa_ref, dob_ref, gw_ref, part_ref):
        @pl.when(pl.program_id(0) == 0)
        def _():
            gw_ref[...] = jnp.zeros_like(gw_ref)
            part_ref[...] = jnp.zeros_like(part_ref)

        dx = dx_ref[...]
        part_ref[0:1, :] += jnp.sum(dx * mo_ref[...], axis=0, keepdims=True)
        dmo = (vec_ref[3:4, :] * dx).astype(BF16)
        doa_ref[...] = _dot_nt(dmo, w_ref[0:512, :])
        dob_ref[...] = _dot_nt(dmo, w_ref[512:1024, :])
        gw_ref[0:512, :] += _dot_tn(oa_ref[...].astype(BF16), dmo)
        gw_ref[512:1024, :] += _dot_tn(ob_ref[...].astype(BF16), dmo)

    row = pl.BlockSpec((tm, D), lambda i: (i, 0))
    half = pl.BlockSpec((tm, 512), lambda i: (i, 0))
    return pl.pallas_call(
        body, name="mix_out_bwd", grid=(S // tm,),
        in_specs=[row, row, half, half, pl.BlockSpec((D, D), lambda i: (0, 0)), pl.BlockSpec((8, D), lambda i: (0, 0))],
        out_specs=[half, half, pl.BlockSpec((D, D), lambda i: (0, 0)), pl.BlockSpec((8, D), lambda i: (0, 0))],
        out_shape=[jax.ShapeDtypeStruct((S, 512), F32), jax.ShapeDtypeStruct((S, 512), F32),
                   jax.ShapeDtypeStruct((D, D), F32), jax.ShapeDtypeStruct((8, D), F32)],
        compiler_params=_params(("arbitrary",)),
    )(dxo, mo, oa, ob, w_o, vecs)


def _mix_in_bwd(h, w_inT, dq, dk, dv, dql, dkl, dkr, *, tm=ROW_TILE):
    S = h.shape[0]
    tm = min(tm, S)
    offs = (0, 512, 640, 768, 1024, 1152)
    wid = (512, 128, 128, 256, 128, 128)

    def body(h_ref, w_ref, dq_ref, dk_ref, dv_ref, dql_ref, dkl_ref, dkr_ref, dh_ref, gw_ref):
        @pl.when(pl.program_id(0) == 0)
        def _():
            gw_ref[...] = jnp.zeros_like(gw_ref)

        hv = h_ref[...]
        dh = jnp.zeros((tm, D), F32)
        for ref, o, w in zip((dq_ref, dk_ref, dv_ref, dql_ref, dkl_ref, dkr_ref), offs, wid):
            w = min(w, D_IN - o)
            dpart = ref[...][:, :w].astype(BF16)
            dh += _dot(dpart, w_ref[o:o + w, :])
            gw_ref[o:o + w, :] += _dot_tn(dpart, hv)
        dh_ref[...] = dh

    row = pl.BlockSpec((tm, D), lambda i: (i, 0))
    part = lambda w: pl.BlockSpec((tm, w), lambda i: (i, 0))
    return pl.pallas_call(
        body, name="mix_in_bwd", grid=(S // tm,),
        in_specs=[row, pl.BlockSpec((D_IN_PAD, D), lambda i: (0, 0))] + [part(w) for w in wid],
        out_specs=[row, pl.BlockSpec((D_IN, D), lambda i: (0, 0))],
        out_shape=[jax.ShapeDtypeStruct((S, D), F32), jax.ShapeDtypeStruct((D_IN, D), F32)],
        compiler_params=_params(("arbitrary",)),
    )(h, w_inT, dq, dk, dv, dql, dkl, dkr)


def _vecs(norm_w, mod9, k):
    return jnp.concatenate([norm_w.reshape(1, D), mod9[3 * k:3 * k + 3], jnp.zeros((4, D), F32)], axis=0)


def _uq_group_rows(wuqT):
    per = MLA_NOPE + MLA_ROPE
    nope = [wuqT[per * h:per * h + MLA_NOPE] for h in range(MLA_HEADS)]
    rope = [wuqT[per * h + MLA_NOPE:per * (h + 1)] for h in range(MLA_HEADS)]
    return jnp.concatenate(nope + rope, axis=0)


def _uq_ungroup_rows(g):
    parts = []
    for h in range(MLA_HEADS):
        parts += [g[MLA_NOPE * h:MLA_NOPE * (h + 1)], g[512 + MLA_ROPE * h:512 + MLA_ROPE * (h + 1)]]
    return jnp.concatenate(parts, axis=0)


def _local_step(x, tgt, mod9, norms, sinks, rel_bias, q_norm, kv_norm, W, on_grads=None):
    if on_grads is None:
        on_grads = lambda group, grads, after, vecs: vecs
    S = x.shape[0]
    v1 = _vecs(norms["ffn1"], mod9, 0)
    v2 = _vecs(norms["mix"], mod9, 1)
    v3 = _vecs(norms["ffn2"], mod9, 2)
    bucket = jnp.asarray(_bucket_table())
    cos, sin = _rope_tables(S)
    if isinstance(W, dict):
        full, W = W, (lambda group, after, vecs: (full, vecs))

    W1, v1 = W("ffn1", [], v1)
    x1, h1, a1, b1, f1 = _ffn_fwd(x, v1, W1["g1T"], W1["u1T"], W1["d1"], name="ffn1_fwd")
    W2, v2 = W("mixer", [x1], v2)
    w_inT = jnp.pad(W2["w_inT"], ((0, D_IN_PAD - D_IN), (0, 0))).astype(BF16)
    wuqT = _uq_group_rows(W2["w_uqT"])
    h2, proj = _mix_in_fwd(x1, v2, w_inT)
    bias = _bias_build(rel_bias, bucket)
    oa = _swa_fwd(proj, bias, sinks)
    qc, kc, vv = _mla_pre_fwd(proj, q_norm, kv_norm, wuqT, W2["w_ukv"], cos, sin)
    ob, lse = _mla_attn_fwd(qc, kc, vv)
    _, v2o = W("ffn2_on_its_way", [ob], v2)
    x2, mo = _mix_out_fwd(x1, oa, ob, W2["w_o"], v2o)
    W3, v3 = W("ffn2", [x2], v3)
    x3, h3, a3, b3, f3 = _ffn_fwd(x2, v3, W3["g3T"], W3["u3T"], W3["d3"], name="ffn2_fwd")
    dx3, head_part, df3 = _head(x3, tgt, norms["final"], f3, v3)

    gg3, gu3, gd3, dh3 = _ffn_bwd_main(h3, df3, a3, b3, W3["g3T"], W3["u3T"], W3["d3"], name="ffn2_bwd")
    ffn2 = {"g3T": gg3, "u3T": gu3, "d3": gd3}
    v3 = on_grads("ffn2", ffn2, [], v3)
    dx2, n3_part = _norm_bwd(dh3, x2, dx3, v3, name="ffn2_norm_bwd")
    v2 = on_grads("ffn2", None, [dx2], v2)
    doa, dob, g_wo, g2_part = _mix_out_bwd(dx2, mo, oa, ob, W2["w_o"], v2)
    dq, dk, dv, drb, dsk = _swa_bwd(proj, bias, sinks, oa, doa, bucket)
    dqc, dkc, dvv = _mla_attn_bwd(qc, kc, vv, ob, lse, dob)
    dql, dkl, dkr, g_uq, g_ukv, mla_part = _mla_pre_bwd(proj, q_norm, kv_norm, wuqT, W2["w_ukv"], cos, sin, dqc, dkc, dvv)
    dh2, g_win = _mix_in_bwd(h2, w_inT, dq, dk, dv, dql, dkl, dkr)
    mixer = {"w_inT": g_win, "w_uqT": _uq_ungroup_rows(g_uq).astype(BF16),
             "w_ukv": g_ukv.astype(BF16), "w_o": g_wo.astype(BF16)}
    v2 = on_grads("mixer", mixer, [], v2)
    dx1, n2_part, df1 = _norm_bwd(dh2, x1, dx2, v2, name="mix_norm_bwd", below=(f1, v1))
    started = on_grads("mixer", None, [dx1], jnp.zeros((1, 1), F32))
    gg1, gu1, gd1, dh1 = _ffn_bwd_main(h1, df1, a1, b1, W1["g1T"], W1["u1T"], W1["d1"], name="ffn1_bwd",
                                       after=[started])
    ffn1 = {"g1T": gg1, "u1T": gu1, "d1": gd1}
    v1 = on_grads("ffn1", ffn1, [], v1)
    dx0, n1_part = _norm_bwd(dh1, x, dx1, v1, name="ffn1_norm_bwd")

    grads = {**ffn1, **ffn2, **mixer}
    return head_part[1, 0], dx0, grads, _pack_vec(n1_part, n2_part, n3_part, head_part, g2_part, mla_part, dsk, drb)


SMALL_LAYOUT = (("norm_ffn1", 1024), ("norm_mix", 1024), ("norm_ffn2", 1024), ("norm_final", 1024),
                ("q_norm", 256), ("kv_norm", 128), ("sinks", 128), ("rel_bias", 256))
N_SMALL = sum(n for _, n in SMALL_LAYOUT)
LOSS_SLOT = 4 * 1024 + 256 + 128 + SWA_HEADS
N_MODVEC = N_MOD * D
N_VEC = N_MODVEC + N_SMALL


def _pack_vec(n1, n2, n3, head, g2, mla, dsk, drb):
    def body(n1_ref, n2_ref, n3_ref, head_ref, g2_ref, mla_ref, dsk_ref, drb_ref, out_ref):
        rows = [n1_ref[1:2, :], n1_ref[2:3, :], n2_ref[3:4, :], n2_ref[1:2, :], n2_ref[2:3, :], g2_ref[0:1, :],
                n3_ref[1:2, :], n3_ref[2:3, :], head_ref[3:4, :],
                n1_ref[0:1, :], n2_ref[0:1, :], n3_ref[0:1, :], head_ref[0:1, :]]
        for i, row in enumerate(rows):
            out_ref[:, D * i:D * (i + 1)] = row
        off = D * len(rows)
        out_ref[:, off:off + 256] = mla_ref[0:1, :]
        out_ref[:, off + 256:off + 384] = mla_ref[1:2, 0:128]

        def diagonal(block):
            r = lax.broadcasted_iota(jnp.int32, block.shape, 0)
            lane = lax.broadcasted_iota(jnp.int32, block.shape, 1)
            return jnp.sum(jnp.where(r == lane, block, 0.0), axis=0, keepdims=True)

        lane = lax.broadcasted_iota(jnp.int32, (1, 128), 1)
        out_ref[:, off + 384:off + 512] = jnp.where(lane == SWA_HEADS, head_ref[1:2, 0:128], diagonal(dsk_ref[...]))
        out_ref[:, off + 512:off + 640] = diagonal(drb_ref[0:128, :])
        out_ref[:, off + 640:off + 768] = diagonal(drb_ref[128:256, :])

    vm = pl.BlockSpec(memory_space=pltpu.VMEM)
    return pl.pallas_call(body, name="pack_vec", in_specs=[vm] * 8, out_specs=vm,
                          out_shape=jax.ShapeDtypeStruct((1, N_VEC), F32))(n1, n2, n3, head, g2, mla, dsk, drb)


def _coords():
    return lax.axis_index("x"), lax.axis_index("y"), lax.axis_index("c")


def _flip(v, bit):
    return 1 - v if bit else v


def _peer(r):
    x, y, c = _coords()
    return (_flip(x, r & 4), _flip(y, r & 2), _flip(c, r & 1))


def _mod_fwd(c_tile, w_mod, b_mod3):
    W = w_mod.shape[1]

    def body(c_ref, w_ref, b_ref, mod_ref, ca_ref, call_ref, part_ref, send_sems, recv_sems):
        x, y, c = _coords()
        me = 4 * x + 2 * y + c
        call_ref[me] = c_ref[...]
        sends = []
        for r in range(1, N_DEV):
            cp = pltpu.make_async_remote_copy(c_ref, call_ref.at[me], send_sems.at[0, r], recv_sems.at[0, r],
                                              device_id=_peer(r), device_id_type=MESH)
            cp.start()
            sends.append(cp)
        for r in range(1, N_DEV):
            pltpu.make_async_remote_copy(c_ref, call_ref.at[me], send_sems.at[0, r], recv_sems.at[0, r],
                                         device_id=_peer(r), device_id_type=MESH).wait_recv()
        cv = call_ref[...].reshape(8 * N_DEV, D)
        ca = (cv * _sigmoid(cv)).astype(BF16)
        ca_ref[...] = ca
        part_ref[...] = _dot(ca, w_ref[...].astype(BF16)).reshape(N_DEV, 8, W)
        mod_ref[me] = part_ref[me] + b_ref[me]
        for r in range(1, N_DEV):
            cp = pltpu.make_async_remote_copy(part_ref.at[me ^ r], mod_ref.at[me], send_sems.at[1, r],
                                              recv_sems.at[1, r], device_id=_peer(r), device_id_type=MESH)
            cp.start()
            sends.append(cp)
        for r in range(1, N_DEV):
            pltpu.make_async_remote_copy(part_ref.at[me ^ r], mod_ref.at[me], send_sems.at[1, r],
                                         recv_sems.at[1, r], device_id=_peer(r), device_id_type=MESH).wait_recv()
            mod_ref[me ^ r] = mod_ref[me ^ r] + b_ref[me ^ r]
        for cp in sends:
            cp.wait_send()

    vm = pl.BlockSpec(memory_space=pltpu.VMEM)
    return pl.pallas_call(
        body, name="mod_fwd", in_specs=[vm, vm, vm], out_specs=[vm, vm],
        out_shape=[jax.ShapeDtypeStruct((N_DEV, 8, W), F32), jax.ShapeDtypeStruct((8 * N_DEV, D), BF16)],
        scratch_shapes=[pltpu.VMEM((N_DEV, 8, D), F32), pltpu.VMEM((N_DEV, 8, W), F32),
                        pltpu.SemaphoreType.DMA((2, N_DEV)), pltpu.SemaphoreType.DMA((2, N_DEV))],
        compiler_params=_params(),
    )(c_tile, w_mod, b_mod3)


def _mod_bwd(allvec, ca, me_idx):
    W = N_MODVEC // N_DEV

    def body(me_ref, all_ref, cols_ref, ca_ref, gw_ref, sum_ref):
        in_first_row = lax.broadcasted_iota(jnp.int32, (N_DEV, 8, W), 1) == 0
        dm = jnp.where(in_first_row, cols_ref[...], 0.0).reshape(8 * N_DEV, W)
        gw_ref[...] = _dot_tn(ca_ref[...], dm.astype(BF16))
        total = all_ref[0]
        for k in range(1, N_DEV):
            total = total + all_ref[k]
        sum_ref[...] = total

    return pl.pallas_call(
        body, name="mod_bwd",
        grid_spec=pltpu.PrefetchScalarGridSpec(
            num_scalar_prefetch=1, grid=(1,),
            in_specs=[pl.BlockSpec((N_DEV, 1, N_VEC), lambda i, me: (0, 0, 0)),
                      pl.BlockSpec((N_DEV, 1, W), lambda i, me: (0, 0, me[0])),
                      pl.BlockSpec((8 * N_DEV, D), lambda i, me: (0, 0))],
            out_specs=[pl.BlockSpec((D, W), lambda i, me: (0, 0)), pl.BlockSpec((1, N_VEC), lambda i, me: (0, 0))]),
        out_shape=[jax.ShapeDtypeStruct((D, W), F32), jax.ShapeDtypeStruct((1, N_VEC), F32)],
        compiler_params=_params(("arbitrary",)),
    )(me_idx, allvec, allvec, ca)


def _wgather(shards):
    n = len(shards)
    rows = [s.shape[0] for s in shards]

    def body(*refs):
        ins, outs, token = refs[:n], refs[n:2 * n], refs[2 * n]
        send_sems, recv_sems, local_sems = refs[2 * n + 1:]
        token[...] = jnp.zeros_like(token)
        x, y, c = _coords()
        me = 4 * x + 2 * y + c
        sib, xn, yn = (x, y, 1 - c), (1 - x, y, c), (x, 1 - y, c)
        block = lambda px, py, pc: 4 * px + 2 * py + pc

        def part(k, blk, half):
            if half is None:
                return outs[k].at[blk]
            return outs[k].at[blk, pl.ds(half * (rows[k] // 2), rows[k] // 2)]

        def copy(k, slot, blk, to, half=None, src=None):
            ref = part(k, blk, half)
            return pltpu.make_async_remote_copy(
                src_ref=ref if src is None else src, dst_ref=ref, send_sem=send_sems.at[k, slot],
                recv_sem=recv_sems.at[k, slot], device_id=to, device_id_type=MESH)

        local = [pltpu.make_async_copy(ins[k], outs[k].at[me], local_sems.at[k]) for k in range(n)]
        for cp in local:
            cp.start()
        sent = [copy(k, slot, me, to, src=ins[k]) for k in range(n) for slot, to in ((0, sib), (1, xn), (2, yn))]
        for cp in sent:
            cp.start()
        bx, by, bd = block(1 - x, y, c), block(x, 1 - y, c), block(1 - x, 1 - y, c)
        for k in range(n):
            copy(k, 1, bx, sib).wait_recv()
            sent += [copy(k, 4, bx, yn, half=1), copy(k, 5, bx, sib)]
            sent[-2].start()
            sent[-1].start()
        for k in range(n):
            copy(k, 2, by, sib).wait_recv()
            sent += [copy(k, 3, by, xn, half=0), copy(k, 6, by, sib)]
            sent[-2].start()
            sent[-1].start()
        for k in range(n):
            copy(k, 3, bd, sib, half=0).wait_recv()
            copy(k, 4, bd, sib, half=1).wait_recv()
            sent.append(copy(k, 7, bd, sib))
            sent[-1].start()
        for k in range(n):
            copy(k, 0, block(x, y, 1 - c), sib).wait_recv()
            for slot, blk in ((5, block(1 - x, y, 1 - c)), (6, block(x, 1 - y, 1 - c)), (7, block(1 - x, 1 - y, 1 - c))):
                copy(k, slot, blk, sib).wait_recv()
        for cp in sent:
            cp.wait_send()
        for cp in local:
            cp.wait()

    anyspec = pl.BlockSpec(memory_space=pl.ANY)
    return pl.pallas_call(
        body, name="wgather", in_specs=[anyspec] * n,
        out_specs=[anyspec] * n + [pl.BlockSpec(memory_space=pltpu.VMEM)],
        out_shape=[jax.ShapeDtypeStruct((N_DEV,) + s.shape, s.dtype) for s in shards]
        + [jax.ShapeDtypeStruct((8, 128), F32)],
        scratch_shapes=[pltpu.SemaphoreType.DMA((n, 8)), pltpu.SemaphoreType.DMA((n, 8)),
                        pltpu.SemaphoreType.DMA((n,))],
    )(*shards)


class _GatherCopies:
    def __init__(self, lands, send_sems, recv_sems, k0=0, batches=None):
        x, y, c = _coords()
        me = 4 * x + 2 * y + c
        sib = (x, y, 1 - c)
        chips = [(1 - x, y), (x, 1 - y), (1 - x, 1 - y)]

        def copy(k, slot, block, to):
            return pltpu.make_async_remote_copy(
                src_ref=lands[k].at[block], dst_ref=lands[k].at[block],
                send_sem=send_sems.at[7 * (k0 + k) + slot], recv_sem=recv_sems.at[7 * (k0 + k) + slot],
                device_id=to, device_id_type=MESH)

        n = len(lands)
        self.first = [copy(k, 0, me, sib) for k in range(n)]
        for batch in batches or [range(n)]:
            self.first += [copy(k, 1 + j, me, (cx, cy, c)) for j, (cx, cy) in enumerate(chips) for k in batch]
        self.landed = [copy(k, 1 + j, 4 * cx + 2 * cy + c, sib) for j, (cx, cy) in enumerate(chips) for k in range(n)]
        self.passed = [copy(k, 4 + j, 4 * cx + 2 * cy + c, sib) for j, (cx, cy) in enumerate(chips) for k in range(n)]
        self.from_sib = [copy(k, 0, 4 * x + 2 * y + (1 - c), sib) for k in range(n)]
        self.from_sib += [copy(k, 4 + j, 4 * cx + 2 * cy + (1 - c), sib) for j, (cx, cy) in enumerate(chips)
                          for k in range(n)]


def _gather_start(lands, *, name, batches=None):
    n = len(lands)

    def body(*refs):
        for cp in _GatherCopies(refs[:n], refs[n], refs[n + 1], batches=batches).first:
            cp.start()
        refs[-1][...] = jnp.zeros_like(refs[-1])

    out = pl.pallas_call(
        body, name=name,
        out_shape=(pltpu.SemaphoreType.DMA((7 * n,)), pltpu.SemaphoreType.DMA((7 * n,)),
                   *[pltpu.HBM(l.shape, l.dtype) for l in lands], jax.ShapeDtypeStruct((8, 128), F32)),
        in_specs=[HBM_SPEC] * n,
        out_specs=(SEM_SPEC, SEM_SPEC, *[HBM_SPEC] * n, pl.BlockSpec(memory_space=pltpu.VMEM)),
        input_output_aliases={i: 2 + i for i in range(n)},
        compiler_params=pltpu.CompilerParams(has_side_effects=DATAFLOW),
    )(*[_in_hbm(l) for l in lands])
    return out[0], out[1], list(out[2:2 + n]), out[-1]


def _gather_pass(send_sems, recv_sems, lands, after, *, name, stage, k0=0):
    n = len(lands)

    def body(*refs):
        cps = _GatherCopies(refs[:n], refs[n], refs[n + 1], k0)
        if stage == "landed":
            for cp in cps.landed:
                cp.wait_recv()
        else:
            for cp in cps.passed:
                cp.start()
        refs[-1][...] = jnp.zeros_like(refs[-1])

    out = pl.pallas_call(
        body, name=name,
        out_shape=(*[pltpu.HBM(l.shape, l.dtype) for l in lands], jax.ShapeDtypeStruct((8, 128), F32)),
        in_specs=[HBM_SPEC] * n + [SEM_SPEC, SEM_SPEC] + [pl.BlockSpec(memory_space=pl.ANY)] * len(after),
        out_specs=(*[HBM_SPEC] * n, pl.BlockSpec(memory_space=pltpu.VMEM)),
        input_output_aliases={i: i for i in range(n)},
        compiler_params=pltpu.CompilerParams(has_side_effects=DATAFLOW),
    )(*lands, send_sems, recv_sems, *after)
    return list(out[:n]), out[-1]


def _gather_end(send_sems, recv_sems, lands, after, *, name, k0=0):
    n = len(lands)

    def body(*refs):
        cps = _GatherCopies(refs[:n], refs[n], refs[n + 1], k0)
        for cp in cps.from_sib:
            cp.wait_recv()
        for cp in cps.first + cps.passed:
            cp.wait_send()

    out = pl.pallas_call(
        body, name=name,
        out_shape=[pltpu.HBM(l.shape, l.dtype) for l in lands],
        in_specs=[HBM_SPEC] * n + [SEM_SPEC, SEM_SPEC] + [pl.BlockSpec(memory_space=pl.ANY)] * len(after),
        out_specs=[HBM_SPEC] * n,
        input_output_aliases={i: i for i in range(n)},
        compiler_params=pltpu.CompilerParams(has_side_effects=DATAFLOW),
    )(*lands, send_sems, recv_sems, *after)
    return list(out)


def _d2d_copies(grads, lands, send_sems, recv_sems):
    x, y, c = _coords()
    return [pltpu.make_async_remote_copy(
        src_ref=grads[k].at[2 * q + (1 - c)], dst_ref=lands[k].at[q],
        send_sem=send_sems.at[4 * k + q], recv_sem=recv_sems.at[4 * k + q],
        device_id=(x, y, 1 - c), device_id_type=MESH) for k in range(len(grads)) for q in range(4)]


def _direct_copies(grads, lands, send_sems, recv_sems):
    x, y, c = _coords()
    me = 4 * x + 2 * y + c
    return [pltpu.make_async_remote_copy(
        src_ref=grads[k].at[me ^ r], dst_ref=lands[k].at[r - 1],
        send_sem=send_sems.at[7 * k + r - 1], recv_sem=recv_sems.at[7 * k + r - 1],
        device_id=_peer(r), device_id_type=MESH) for k in range(len(grads)) for r in range(1, N_DEV)]


def _vec_copies(srcs, lands, send_sems, recv_sems):
    x, y, c = _coords()
    me = 4 * x + 2 * y + c
    return [pltpu.make_async_remote_copy(
        src_ref=lands[0].at[me], dst_ref=lands[0].at[me], send_sem=send_sems.at[r - 1], recv_sem=recv_sems.at[r - 1],
        device_id=_peer(r), device_id_type=MESH) for r in range(1, N_DEV)]


def _chipsum(gs, sibs, cidx, *, name):
    n = len(gs)

    def body(c_ref, *refs):
        for k in range(n):
            refs[2 * n + k][...] = (refs[k][...].astype(F32) + refs[n + k][...].astype(F32)).astype(refs[2 * n + k].dtype)

    mine = [pl.BlockSpec((1,) + g.shape[1:], lambda q, c_ref: (2 * q + c_ref[0], 0, 0)) for g in gs]
    other = [pl.BlockSpec((1,) + g.shape[1:], lambda q, c_ref: (q, 0, 0)) for g in gs]
    return pl.pallas_call(
        body, name=name,
        grid_spec=pltpu.PrefetchScalarGridSpec(num_scalar_prefetch=1, grid=(4,), in_specs=mine + other, out_specs=other),
        out_shape=[jax.ShapeDtypeStruct((4,) + g.shape[1:], g.dtype) for g in gs],
        compiler_params=_params(("arbitrary",)),
    )(cidx, *gs, *sibs)


HBM_SPEC = pl.BlockSpec(memory_space=pltpu.HBM)
SEM_SPEC = pl.BlockSpec(memory_space=pltpu.SEMAPHORE)
DATAFLOW = pltpu.SideEffectType.DATAFLOW_SIDE_EFFECTING


def _in_hbm(a):
    return pltpu.with_memory_space_constraint(a, pltpu.HBM)


def _rs_step1_copies(sums, lands, send_sems, recv_sems):
    n = len(sums)
    direct, relay = lands[:n], lands[n:]
    x, y, c = _coords()
    xn, yn = (1 - x, y, c), (x, 1 - y, c)
    qx, qy, qd = 2 * (1 - x) + y, 2 * x + (1 - y), 2 * (1 - x) + (1 - y)
    cps = []
    for k in range(n):
        h = sums[k].shape[1] // 2
        a, b = pl.ds(0, h), pl.ds(h, h)
        moves = ((sums[k].at[qx, a], direct[k].at[0], xn), (sums[k].at[qy, b], direct[k].at[1], yn),
                 (sums[k].at[qd, a], relay[k].at[0], xn), (sums[k].at[qd, b], relay[k].at[1], yn))
        for s, (src, dst, to) in enumerate(moves):
            cps.append(pltpu.make_async_remote_copy(
                src_ref=src, dst_ref=dst, send_sem=send_sems.at[4 * k + s], recv_sem=recv_sems.at[4 * k + s],
                device_id=to, device_id_type=MESH))
    return cps


def _rs_step2_copies(relayed, lands, send_sems, recv_sems, k0=0):
    x, y, c = _coords()
    cps = []
    for k in range(len(relayed)):
        for s, to in enumerate(((1 - x, y, c), (x, 1 - y, c))):
            cps.append(pltpu.make_async_remote_copy(
                src_ref=relayed[k].at[s], dst_ref=lands[k].at[s], send_sem=send_sems.at[2 * (k0 + k) + s],
                recv_sem=recv_sems.at[2 * (k0 + k) + s], device_id=to, device_id_type=MESH))
    return cps


def _relay_sum(sums, relay, qxy, *, name):
    n = len(sums)

    def body(q_ref, *refs):
        for k in range(n):
            refs[2 * n + k][...] = (refs[k][...].astype(F32) + refs[n + k][...].astype(F32)).astype(refs[2 * n + k].dtype)

    half = lambda s: (1, s.shape[1] // 2) + s.shape[2:]
    return pl.pallas_call(
        body, name=name,
        grid_spec=pltpu.PrefetchScalarGridSpec(
            num_scalar_prefetch=1, grid=(2,),
            in_specs=[pl.BlockSpec(half(s), lambda t, q_ref: (q_ref[t], 1 - t, 0)) for s in sums]
            + [pl.BlockSpec(half(s), lambda t, q_ref: (1 - t, 0, 0)) for s in sums],
            out_specs=[pl.BlockSpec(half(s), lambda t, q_ref: (t, 0, 0)) for s in sums]),
        out_shape=[jax.ShapeDtypeStruct((2,) + half(s)[1:], s.dtype) for s in sums],
        compiler_params=_params(("arbitrary",)),
    )(qxy, *sums, *relay)


def _split_start(copies, srcs, lands, n_sems, after, *, name):
    ns, nl = len(srcs), len(lands)

    def body(*refs):
        for cp in copies(refs[:ns], refs[ns:ns + nl], refs[ns + nl + len(after)], refs[ns + nl + len(after) + 1]):
            cp.start()
        refs[-1][...] = jnp.zeros_like(refs[-1])

    bufs = [_in_hbm(a) for a in list(srcs) + list(lands)]
    out = pl.pallas_call(
        body, name=name,
        out_shape=(pltpu.SemaphoreType.DMA((n_sems,)), pltpu.SemaphoreType.DMA((n_sems,)),
                   *[pltpu.HBM(a.shape, a.dtype) for a in bufs], jax.ShapeDtypeStruct((8, 128), F32)),
        in_specs=[HBM_SPEC] * len(bufs) + [pl.BlockSpec(memory_space=pl.ANY)] * len(after),
        out_specs=(SEM_SPEC, SEM_SPEC, *[HBM_SPEC] * len(bufs), pl.BlockSpec(memory_space=pltpu.VMEM)),
        input_output_aliases={i: 2 + i for i in range(len(bufs))},
        compiler_params=pltpu.CompilerParams(has_side_effects=DATAFLOW),
    )(*bufs, *after)
    return out[0], out[1], list(out[2:2 + ns]), list(out[2 + ns:2 + ns + nl]), out[-1]


def _split_wait(copies, send_sems, recv_sems, srcs, lands, after, *, name):
    ns, nl = len(srcs), len(lands)

    def body(*refs):
        for cp in copies(refs[:ns], refs[ns:ns + nl], refs[ns + nl], refs[ns + nl + 1]):
            cp.wait_send()
            cp.wait_recv()

    out = pl.pallas_call(
        body, name=name,
        out_shape=[pltpu.HBM(a.shape, a.dtype) for a in list(srcs) + list(lands)],
        in_specs=[HBM_SPEC] * (ns + nl) + [SEM_SPEC, SEM_SPEC] + [pl.BlockSpec(memory_space=pl.ANY)] * len(after),
        out_specs=[HBM_SPEC] * (ns + nl),
        input_output_aliases={i: i for i in range(ns + nl)},
        compiler_params=pltpu.CompilerParams(has_side_effects=DATAFLOW),
    )(*srcs, *lands, send_sems, recv_sems, *after)
    return list(out[:ns]), list(out[ns:])


ADAM_C1 = 1.0 / (1.0 - ADAM_B1 ** ADAM_STEP)
ADAM_C2 = 1.0 / (1.0 - ADAM_B2 ** ADAM_STEP)


def _adam_math(w, g, m, v):
    m2 = ADAM_B1 * m + (1.0 - ADAM_B1) * g
    v2 = ADAM_B2 * v + (1.0 - ADAM_B2) * (g * g)
    return -ADAM_LR * ((m2 * ADAM_C1) / (jnp.sqrt(v2 * ADAM_C2) + ADAM_EPS) + ADAM_WD * w), m2, v2


def _adamw(w, g, m, v, *, name, after=()):
    R, C = w.shape
    tr = R if R <= 512 else 256

    def body(w_ref, g_ref, m_ref, v_ref, *rest):
        d_ref, nm_ref, nv_ref = rest[len(after):]
        d_ref[...], nm_ref[...], nv_ref[...] = _adam_math(w_ref[...], g_ref[...], m_ref[...], v_ref[...])

    blk = pl.BlockSpec((tr, C), lambda i: (i, 0))
    return pl.pallas_call(
        body, name=name, grid=(R // tr,), in_specs=[blk] * 4 + [pl.BlockSpec(memory_space=pl.ANY)] * len(after),
        out_specs=[blk] * 3, out_shape=[jax.ShapeDtypeStruct((R, C), F32)] * 3,
        compiler_params=_params(("parallel",)),
    )(w, g, m, v, *after)


def _adamw_rs2(wmv, cs, direct, second, qidx, *, name):
    n = len(wmv)
    r, cc = wmv[0][0].shape
    h = r // 2

    def body(q_ref, *refs):
        ins, outs = refs[:6 * n], refs[6 * n:]
        for k in range(n):
            w_ref, m_ref, v_ref, c_ref, d1_ref, d2_ref = ins[6 * k:6 * k + 6]
            g_ref, d_ref, nm_ref, nv_ref = outs[4 * k:4 * k + 4]
            g = (c_ref[0].astype(F32) + d1_ref[0].astype(F32)) + d2_ref[0].astype(F32)
            g_ref[...] = g
            d_ref[...], nm_ref[...], nv_ref[...] = _adam_math(w_ref[...], g, m_ref[...], v_ref[...])

    blk = pl.BlockSpec((h, cc), lambda i, q_ref: (i, 0))
    one = [blk, blk, blk, pl.BlockSpec((1, h, cc), lambda i, q_ref: (q_ref[0], i, 0)),
           pl.BlockSpec((1, h, cc), lambda i, q_ref: (i, 0, 0)),
           pl.BlockSpec((1, h, cc), lambda i, q_ref: (1 - i, 0, 0))]
    out = pl.pallas_call(
        body, name=name,
        grid_spec=pltpu.PrefetchScalarGridSpec(num_scalar_prefetch=1, grid=(2,), in_specs=one * n,
                                               out_specs=[blk] * (4 * n)),
        out_shape=[jax.ShapeDtypeStruct((r, cc), F32)] * (4 * n),
        compiler_params=_params(("arbitrary",)),
    )(qidx, *[a for (w, m, v), c, d1, d2 in zip(wmv, cs, direct, second) for a in (w, m, v, c, d1, d2)])
    return [tuple(out[4 * k:4 * k + 4]) for k in range(n)]


def _adamw_rs(wmv, cs, rcv, qidx, *, name):
    n = len(wmv)
    shapes = [w.shape for w, _, _ in wmv]
    n_rcv = rcv[0].shape[0]
    halved = len(set(shapes)) == 1 and shapes[0][0] % 32 == 0 and shapes[0][0] > 128
    tiles = 2 if halved else 1

    def body(q_ref, *refs):
        ins, outs = refs[:5 * n], refs[5 * n:]
        for k in range(n):
            w_ref, m_ref, v_ref, c_ref, r_ref = ins[5 * k:5 * k + 5]
            g_ref, d_ref, nm_ref, nv_ref = outs[4 * k:4 * k + 4]
            g = c_ref[0].astype(F32)
            for j in range(n_rcv):
                g = g + r_ref[j].astype(F32)
            g_ref[...] = g
            d_ref[...], nm_ref[...], nv_ref[...] = _adam_math(w_ref[...], g, m_ref[...], v_ref[...])

    in_specs, out_specs = [], []
    for r, cc in shapes:
        blk = pl.BlockSpec((r // tiles, cc), lambda i, q_ref: (i, 0))
        in_specs += [blk, blk, blk, pl.BlockSpec((1, r // tiles, cc), lambda i, q_ref: (q_ref[0], i, 0)),
                     pl.BlockSpec((n_rcv, r // tiles, cc), lambda i, q_ref: (0, i, 0))]
        out_specs += [blk] * 4
    out = pl.pallas_call(
        body, name=name,
        grid_spec=pltpu.PrefetchScalarGridSpec(num_scalar_prefetch=1, grid=(tiles,), in_specs=in_specs,
                                               out_specs=out_specs),
        out_shape=[jax.ShapeDtypeStruct(s, F32) for s in shapes for _ in range(4)],
        compiler_params=_params(("arbitrary",)),
    )(qidx, *[a for (w, m, v), c, rc in zip(wmv, cs, rcv) for a in (w, m, v, c, rc)])
    return [tuple(out[4 * k:4 * k + 4]) for k in range(n)]


SMALL_PARAMS = ("norm_ffn1", "norm_mix", "norm_ffn2", "norm_final", "q_norm", "kv_norm", "sinks", "rel_bias", "b_mod")


def _adamw_small(gvec, wmv):
    shapes = [wmv[3 * i].shape for i in range(len(SMALL_PARAMS))]

    def body(*refs):
        g_all = refs[0]
        ins = refs[1:1 + 3 * len(SMALL_PARAMS)]
        outs = refs[1 + 3 * len(SMALL_PARAMS):]
        off = N_MODVEC
        for i, name in enumerate(SMALL_PARAMS):
            g_ref, d_ref, nm_ref, nv_ref = outs[4 * i:4 * i + 4]
            w_ref, m_ref, v_ref = ins[3 * i:3 * i + 3]
            start = 0 if name == "b_mod" else off
            rows, width = shapes[i]
            g = jnp.concatenate([g_all[:, start + width * r:start + width * (r + 1)] for r in range(rows)], axis=0)
            g_ref[...] = g
            d_ref[...], nm_ref[...], nv_ref[...] = _adam_math(w_ref[...], g, m_ref[...], v_ref[...])
            if name != "b_mod":
                off += dict(SMALL_LAYOUT)[name]

    vm = pl.BlockSpec(memory_space=pltpu.VMEM)
    n_out = 4 * len(SMALL_PARAMS)
    out = pl.pallas_call(
        body, name="adamw_small", in_specs=[vm] * (1 + len(wmv)), out_specs=[vm] * n_out,
        out_shape=[jax.ShapeDtypeStruct(shapes[i // 4], F32) for i in range(n_out)],
        compiler_params=_params(),
    )(gvec, *wmv)
    return {name: out[4 * i:4 * i + 4] for i, name in enumerate(SMALL_PARAMS)}


TRANSPOSED = ("g1T", "u1T", "g3T", "u3T", "w_inT", "w_uqT")


def kernel(x, c, w_mod, b_mod, norm_ffn1, ffn1_gate, ffn1_up, ffn1_down, norm_mix, w_in, q_norm, kv_norm, w_uq, w_ukv, sinks, w_o, norm_ffn2, ffn2_gate, ffn2_up, ffn2_down, rel_bias, norm_final, loss_target, m_w_mod, m_b_mod, m_norm_ffn1, m_ffn1_gate, m_ffn1_up, m_ffn1_down, m_norm_mix, m_w_in, m_q_norm, m_kv_norm, m_w_uq, m_w_ukv, m_sinks, m_w_o, m_norm_ffn2, m_ffn2_gate, m_ffn2_up, m_ffn2_down, m_rel_bias, m_norm_final, v_w_mod, v_b_mod, v_norm_ffn1, v_ffn1_gate, v_ffn1_up, v_ffn1_down, v_norm_mix, v_w_in, v_q_norm, v_kv_norm, v_w_uq, v_w_ukv, v_sinks, v_w_o, v_norm_ffn2, v_ffn2_gate, v_ffn2_up, v_ffn2_down, v_rel_bias, v_norm_final):
    mx, my, mc = _coords()
    cidx = jnp.reshape(mc, (1,)).astype(jnp.int32)
    qidx = jnp.reshape(2 * mx + my, (1,)).astype(jnp.int32)
    WM = w_mod.shape[2]

    c_tile = jnp.pad(c, ((0, 7), (0, 0)))
    b_mod3 = jnp.pad(b_mod.reshape(N_DEV, 1, WM), ((0, 0), (0, 7), (0, 0)))
    mod3, ca = _mod_fwd(c_tile, w_mod[0], b_mod3)
    mod9 = mod3[:, 0, :].reshape(N_MOD, D)

    shards = {"g1T": ffn1_gate[0].T.astype(BF16), "u1T": ffn1_up[0].T.astype(BF16), "d1": ffn1_down[0].astype(BF16),
              "g3T": ffn2_gate[0].T.astype(BF16), "u3T": ffn2_up[0].T.astype(BF16), "d3": ffn2_down[0].astype(BF16),
              "w_inT": w_in[0].T, "w_uqT": w_uq[0].T.astype(BF16), "w_ukv": w_ukv[0].astype(BF16),
              "w_o": w_o[0].astype(BF16)}
    me = 4 * mx + 2 * my + mc
    groups = {"ffn1": ("g1T", "u1T", "d1"), "mixer": ("w_inT", "w_uqT", "w_ukv", "w_o"), "ffn2": ("g3T", "u3T", "d3")}
    arriving = {}

    def as_weights(group, gathered):
        return {k: g if k == "w_ukv" else g.reshape(N_DEV * g.shape[1], g.shape[2])
                for k, g in zip(groups[group], gathered)}

    later = groups["mixer"] + groups["ffn2"]
    place = {"mixer": 0, "ffn2": len(groups["mixer"])}

    def start_gather(token):
        lands = []
        for k in later:
            sh = shards[k] + token[0, 0].astype(shards[k].dtype)
            lands.append(lax.dynamic_update_slice(lax.empty((N_DEV,) + sh.shape, sh.dtype), sh[None], (me, 0, 0)))
        batches = [range(k0, k0 + len(groups[group])) for group, k0 in place.items()]
        send, recv, lands, started = _gather_start(lands, name="gather_start", batches=batches)
        for group, k0 in place.items():
            arriving[group] = (send, recv, lands[k0:k0 + len(groups[group])])
        return started

    def fetch(group, after, vecs):
        if group == "ffn1":
            *gathered, token = _wgather([shards[k] + ca[1, 0].astype(shards[k].dtype) for k in groups["ffn1"]])
            return as_weights("ffn1", gathered), vecs + start_gather(token)[0:1, 0:1]

        def pass_on(group, after):
            send, recv, lands = arriving[group]
            lands, token = _gather_pass(send, recv, lands, after, name="gather_landed_" + group, stage="landed",
                                        k0=place[group])
            lands, token = _gather_pass(send, recv, lands, [token], name="gather_onward_" + group, stage="onward",
                                        k0=place[group])
            arriving[group] = (send, recv, lands)
            return token

        if group == "ffn2_on_its_way":
            return None, vecs + pass_on("ffn2", after)[0:1, 0:1]
        if group == "mixer":
            after = [pass_on("mixer", after)]
        send, recv, lands = arriving[group]
        return as_weights(group, _gather_end(send, recv, lands, after, name="gather_end_" + group,
                                             k0=place[group])), vecs

    norms ={"ffn1": norm_ffn1, "mix": norm_mix, "ffn2": norm_ffn2, "final": norm_final.reshape(1, D)}
    in_flight = {}

    def on_grads(group, g, after, vecs):
        if group != "ffn1":
            if g is None:
                return vecs
            names = list(g)
            by_dest = [g[k] if k == "w_ukv" else g[k].reshape((N_DEV, g[k].shape[0] // N_DEV) + g[k].shape[1:])
                       for k in names]
            lands = [lax.empty((N_DEV - 1,) + a.shape[1:], a.dtype) for a in by_dest]
            send, recv, by_dest, lands, token = _split_start(_direct_copies, by_dest, lands, 7 * len(names), after,
                                                             name="rs_start_" + group)
            in_flight[group] = (names, send, recv, by_dest, lands, token)
            return vecs + token[0:1, 0:1]
        names = list(g)
        by_dest = [g[k].reshape((N_DEV, g[k].shape[0] // N_DEV) + g[k].shape[1:]) for k in names]
        lands = [lax.empty((4,) + a.shape[1:], a.dtype) for a in by_dest]
        send, recv, by_dest, lands, token = _split_start(_d2d_copies, by_dest, lands, 4 * len(names), after,
                                                         name="rs_d2d_start_" + group)
        finish("mixer", [token])
        by_dest, from_sib = _split_wait(_d2d_copies, send, recv, by_dest, lands, [done[-1]], name="rs_d2d_wait_" + group)
        sums = _chipsum(by_dest, from_sib, cidx, name="chipsum_" + group)
        halves = lambda: [lax.empty((2, s.shape[1] // 2) + s.shape[2:], s.dtype) for s in sums]
        send, recv, sums, lands, token = _split_start(_rs_step1_copies, sums, halves() + halves(), 4 * len(names), [],
                                                      name="rs_ici_start_" + group)
        in_flight[group] = (names, send, recv, sums, lands, token)
        return vecs + token[0:1, 0:1]

    owners = {"g1T": ("ffn1_gate", ffn1_gate, m_ffn1_gate, v_ffn1_gate), "u1T": ("ffn1_up", ffn1_up, m_ffn1_up, v_ffn1_up),
              "d1": ("ffn1_down", ffn1_down, m_ffn1_down, v_ffn1_down),
              "g3T": ("ffn2_gate", ffn2_gate, m_ffn2_gate, v_ffn2_gate), "u3T": ("ffn2_up", ffn2_up, m_ffn2_up, v_ffn2_up),
              "d3": ("ffn2_down", ffn2_down, m_ffn2_down, v_ffn2_down),
              "w_inT": ("w_in", w_in, m_w_in, v_w_in), "w_uqT": ("w_uq", w_uq, m_w_uq, v_w_uq),
              "w_ukv": ("w_ukv", w_ukv, m_w_ukv, v_w_ukv), "w_o": ("w_o", w_o, m_w_o, v_w_o)}
    res, done = {}, []

    there = lambda k, a: a[0].T if k in TRANSPOSED else a[0]
    back = lambda k, a: a.T[None] if k in TRANSPOSED else a[None]

    def record(names, outs):
        for k, out in zip(names, outs):
            done.append(out[3])
            res[owners[k][0]] = tuple(back(k, a) for a in out)

    def finish(group, after):
        names, send, recv, sums, lands, _ = in_flight[group]
        wmv = [tuple(there(k, a) for a in owners[k][1:]) for k in names]
        own = jnp.reshape(me, (1,)).astype(jnp.int32)
        sums, lands = _split_wait(_direct_copies, send, recv, sums, lands, after, name="rs_wait_" + group)
        record(names, _adamw_rs(wmv, sums, lands, own, name="adamw_" + group))

    _, grad_x, _, vec = _local_step(
        x[0], loss_target[0], mod9, norms, sinks, rel_bias, q_norm, kv_norm, fetch, on_grads=on_grads)

    names, send, recv, sums, lands, step1_started = in_flight["ffn1"]
    vec = vec.reshape(1, 1, N_VEC)
    allvec = lax.dynamic_update_slice(lax.empty((N_DEV, 1, N_VEC), F32), vec, (me, 0, 0))
    vsend, vrecv, _, (allvec,), vec_started = _split_start(_vec_copies, [], [allvec], N_DEV - 1, [step1_started],
                                                           name="vec_start")
    finish("ffn2", [vec_started])
    n = len(names)
    sums, lands = _split_wait(_rs_step1_copies, send, recv, sums, lands, [done[-1]], name="rs_ici_wait_ffn1")
    direct, relay = lands[:n], lands[n:]
    qxy = jnp.stack([2 * (1 - mx) + my, 2 * mx + (1 - my)]).astype(jnp.int32)
    relayed = _relay_sum(sums, relay, qxy, name="relay_sum_ffn1")
    second = [lax.empty(a.shape, a.dtype) for a in relayed]
    send, recv, relayed, second, step2_started = _split_start(_rs_step2_copies, relayed, second, 2 * n, [],
                                                              name="rs_ici_start2_ffn1")

    _, (allvec,) = _split_wait(_vec_copies, vsend, vrecv, [], [allvec], [step2_started], name="vec_wait")
    g_wmod, gvec = _mod_bwd(allvec, ca, jnp.reshape(me, (1,)).astype(jnp.int32))
    loss = gvec[0, N_MODVEC + LOSS_SLOT]
    small_in = {"norm_ffn1": (norm_ffn1, m_norm_ffn1, v_norm_ffn1), "norm_mix": (norm_mix, m_norm_mix, v_norm_mix),
                "norm_ffn2": (norm_ffn2, m_norm_ffn2, v_norm_ffn2), "norm_final": (norm_final, m_norm_final, v_norm_final),
                "q_norm": (q_norm, m_q_norm, v_q_norm), "kv_norm": (kv_norm, m_kv_norm, v_kv_norm),
                "sinks": (sinks, m_sinks, v_sinks), "rel_bias": (rel_bias, m_rel_bias, v_rel_bias),
                "b_mod": (b_mod, m_b_mod, v_b_mod)}
    as_row = lambda k, a: a.T if k == "rel_bias" else a.reshape(1, -1)
    from_row = lambda k, a: a.T if k == "rel_bias" else a.reshape(small_in[k][0].shape)
    small_out = _adamw_small(gvec, [as_row(k, a) for k in SMALL_PARAMS for a in small_in[k]])
    for k in SMALL_PARAMS:
        res[k] = tuple(from_row(k, a) for a in small_out[k])

    out = _adamw(w_mod[0], g_wmod, m_w_mod[0], v_w_mod[0], name="adamw_w_mod")
    res["w_mod"] = tuple(a[None] for a in (g_wmod,) + tuple(out))

    wmv = [tuple(there(k, a) for a in owners[k][1:]) for k in names]
    after = done + [out[2]] + [a for k in SMALL_PARAMS for a in res[k]]
    outs = []
    for i, k in enumerate(names):
        _, (sec,) = _split_wait(functools.partial(_rs_step2_copies, k0=i), send, recv, [relayed[i]], [second[i]],
                                after, name="rs_ici_wait2_" + k)
        outs.append(_adamw_rs2([wmv[i]], [sums[i]], [direct[i]], [sec], qidx, name="adamw_" + owners[k][0])[0])
        after = [outs[-1][3]]
    record(names, outs)

    order = ("w_mod", "b_mod", "norm_ffn1", "ffn1_gate", "ffn1_up", "ffn1_down", "norm_mix", "w_in", "q_norm",
             "kv_norm", "w_uq", "w_ukv", "sinks", "w_o", "norm_ffn2", "ffn2_gate", "ffn2_up", "ffn2_down",
             "rel_bias", "norm_final")
    return (loss, grad_x[None]) + tuple(res[nm][kind] for kind in range(4) for nm in order)
```

```python
import functools
import math

import numpy as np
import jax
import jax.numpy as jnp
from jax import lax
from jax.experimental import pallas as pl
from jax.experimental.pallas import tpu as pltpu

F32 = jnp.float32
BF16 = jnp.bfloat16
MESH = pl.DeviceIdType.MESH

N_DEV = 8
D = 1024
D_FF = 2816
EPS = 1e-6
N_MOD = 9
SWA_HEADS = 8
SWA_DH = 64
WINDOW = 128
MLA_HEADS = 4
MLA_NOPE = 128
MLA_ROPE = 64
MLA_V = 128
MLA_QR = 256
MLA_KVR = 128
ROPE_THETA = 10000.0
NUM_BUCKETS = 32
D_IN = 1216
D_IN_PAD = 1280
SWA_SCALE = SWA_DH ** -0.5
MLA_SCALE = (MLA_NOPE + MLA_ROPE) ** -0.5

ADAM_LR = 0.001
ADAM_B1 = 0.9
ADAM_B2 = 0.999
ADAM_EPS = 1e-08
ADAM_WD = 0.01
ADAM_STEP = 10

V7X_VMEM_LIMIT = 56 * 1024 * 1024
ROW_TILE = 512

NT_DIMS = (((1,), (1,)), ((), ()))
TN_DIMS = (((0,), (0,)), ((), ()))


def _dot(a, b):
    return jnp.dot(a, b, preferred_element_type=F32)


def _dot_nt(a, b):
    return lax.dot_general(a, b, NT_DIMS, preferred_element_type=F32)


def _dot_tn(a, b):
    return lax.dot_general(a, b, TN_DIMS, preferred_element_type=F32)


def _params(sem=None):
    return pltpu.CompilerParams(dimension_semantics=sem, vmem_limit_bytes=V7X_VMEM_LIMIT)


def _rstd(x):
    return lax.rsqrt(jnp.mean(x * x, axis=-1, keepdims=True) + EPS)


def _rms_bwd(dy, xhat, r):
    return r * (dy - xhat * jnp.mean(dy * xhat, axis=-1, keepdims=True))


def _sigmoid(a):
    return 1.0 / (1.0 + jnp.exp(-a))


def _ffn_fwd(x, vecs, wgT, wuT, wd, *, name, tm=256, tf=D_FF):
    S, F = x.shape[0], wd.shape[0]
    tm = min(tm, S)
    ni, nj = S // tm, F // tf

    def body(x_ref, vec_ref, wg_ref, wu_ref, wd_ref, xo_ref, h_ref, a_ref, b_ref, f_ref, acc_ref):
        j = pl.program_id(1)

        @pl.when(j == 0)
        def _():
            xv = x_ref[...]
            hn = xv * _rstd(xv) * vec_ref[0:1, :]
            h_ref[...] = (hn * (1.0 + vec_ref[2:3, :]) + vec_ref[1:2, :]).astype(BF16)

        h = h_ref[...]
        a = _dot_nt(h, wg_ref[...])
        b = _dot_nt(h, wu_ref[...])
        a_ref[...] = a.astype(BF16)
        b_ref[...] = b.astype(BF16)
        part = _dot((a * _sigmoid(a) * b).astype(BF16), wd_ref[...])

        def finish(f):
            f_ref[...] = f
            xo_ref[...] = x_ref[...] + (0.5 * vec_ref[3:4, :]) * f

        if nj == 1:
            finish(part)
        else:
            @pl.when(j == 0)
            def _():
                acc_ref[...] = part

            @pl.when((j > 0) & (j < nj - 1))
            def _():
                acc_ref[...] += part

            @pl.when(j == nj - 1)
            def _():
                finish(acc_ref[...] + part)

    row = pl.BlockSpec((tm, D), lambda i, j: (i, 0))
    wspec = pl.BlockSpec((tf, D), lambda i, j: (j, 0), pipeline_mode=pl.Buffered(1) if nj == 1 else None)
    act = pl.BlockSpec((tm, tf), lambda i, j: (i, j))
    return pl.pallas_call(
        body, name=name, grid=(ni, nj),
        in_specs=[row, pl.BlockSpec((8, D), lambda i, j: (0, 0)), wspec, wspec, wspec],
        out_specs=[row, row, act, act, row],
        out_shape=[jax.ShapeDtypeStruct((S, D), F32), jax.ShapeDtypeStruct((S, D), BF16),
                   jax.ShapeDtypeStruct((S, F), BF16), jax.ShapeDtypeStruct((S, F), BF16),
                   jax.ShapeDtypeStruct((S, D), F32)],
        scratch_shapes=[pltpu.VMEM((tm, D) if nj > 1 else (8, 128), F32)],
        compiler_params=_params(("parallel", "arbitrary")),
    )(x, vecs, wgT, wuT, wd)


def _ffn_bwd_main(h, df, a, b, wgT, wuT, wd, *, name, after=(), tm=2048, tf=256):
    S = h.shape[0]
    tm = min(tm, S)
    ni, nj = S // tm, D_FF // tf

    def body(h_hbm, df_hbm, a_ref, b_ref, wg_ref, wu_ref, wd_ref, *rest):
        gg_ref, gu_ref, gd_ref, dh_hbm, h_v, df_v, dh_v, gg_acc, gu_acc, gd_acc, sem = rest[len(after):]
        j = pl.program_id(0)
        i = pl.program_id(1)

        @pl.when((j == 0) & (i == 0))
        def _():
            c1 = pltpu.make_async_copy(h_hbm, h_v, sem.at[0])
            c2 = pltpu.make_async_copy(df_hbm, df_v, sem.at[1])
            c1.start()
            c2.start()
            c1.wait()
            c2.wait()

        @pl.when(i == 0)
        def _():
            gg_acc[...] = jnp.zeros_like(gg_acc)
            gu_acc[...] = jnp.zeros_like(gu_acc)
            gd_acc[...] = jnp.zeros_like(gd_acc)

        rows = pl.ds(pl.multiple_of(i * tm, tm), tm)
        hi = h_v[rows, :]
        dfi = df_v[rows, :]
        av = a_ref[...].astype(F32)
        bv = b_ref[...].astype(F32)
        sg = _sigmoid(av)
        sa = av * sg
        hsw = (sa * bv).astype(BF16)
        dhsw = _dot_nt(dfi, wd_ref[...])
        da = (dhsw * bv * (sg * (1.0 + av * (1.0 - sg)))).astype(BF16)
        db = (dhsw * sa).astype(BF16)
        gd_acc[...] += _dot_tn(hsw, dfi)
        gg_acc[...] += _dot_tn(da, hi)
        gu_acc[...] += _dot_tn(db, hi)
        dh = _dot(da, wg_ref[...]) + _dot(db, wu_ref[...])

        @pl.when(j == 0)
        def _():
            dh_v[rows, :] = dh

        @pl.when(j > 0)
        def _():
            dh_v[rows, :] += dh

        @pl.when(i == ni - 1)
        def _():
            gg_ref[...] = gg_acc[...].astype(BF16)
            gu_ref[...] = gu_acc[...].astype(BF16)
            gd_ref[...] = gd_acc[...].astype(BF16)

        @pl.when((j == nj - 1) & (i == ni - 1))
        def _():
            c3 = pltpu.make_async_copy(dh_v, dh_hbm, sem.at[2])
            c3.start()
            c3.wait()

    anyspec = pl.BlockSpec(memory_space=pl.ANY)
    wspec = pl.BlockSpec((tf, D), lambda j, i: (j, 0))
    act = pl.BlockSpec((tm, tf), lambda j, i: (i, j))
    return pl.pallas_call(
        body, name=name, grid=(nj, ni),
        in_specs=[anyspec, anyspec, act, act, wspec, wspec, wspec] + [anyspec] * len(after),
        out_specs=[wspec, wspec, wspec, anyspec],
        out_shape=[jax.ShapeDtypeStruct((D_FF, D), BF16)] * 3 + [jax.ShapeDtypeStruct((S, D), F32)],
        scratch_shapes=[pltpu.VMEM((S, D), BF16), pltpu.VMEM((S, D), BF16), pltpu.VMEM((S, D), F32),
                        pltpu.VMEM((tf, D), F32), pltpu.VMEM((tf, D), F32), pltpu.VMEM((tf, D), F32),
                        pltpu.SemaphoreType.DMA((3,))],
        compiler_params=_params(("arbitrary", "arbitrary")),
    )(h, df, a, b, wgT, wuT, wd, *after)


def _ffn_out_bwd(dx, f, gate, df_ref, part_ref):
    df_ref[...] = ((0.5 * gate) * dx).astype(BF16)
    part_ref[3:4, :] += 0.5 * jnp.sum(dx * f, axis=0, keepdims=True)


def _norm_bwd(dh, x, dxo, vecs, *, name, below=None, tm=ROW_TILE):
    S = x.shape[0]
    tm = min(tm, S)

    def body(dh_ref, x_ref, dxo_ref, vec_ref, *rest):
        dx_ref, part_ref = rest[-2 if below is None else -3], rest[-1 if below is None else -2]

        @pl.when(pl.program_id(0) == 0)
        def _():
            part_ref[...] = jnp.zeros_like(part_ref)

        dh = dh_ref[...]
        xv = x_ref[...]
        r = _rstd(xv)
        xhat = xv * r
        w = vec_ref[0:1, :]
        xn = xhat * w
        dxn = dh * (1.0 + vec_ref[2:3, :])
        part_ref[0:1, :] += jnp.sum(dxn * xhat, axis=0, keepdims=True)
        part_ref[1:2, :] += jnp.sum(dh, axis=0, keepdims=True)
        part_ref[2:3, :] += jnp.sum(dh * xn, axis=0, keepdims=True)
        dx = dxo_ref[...] + _rms_bwd(dxn * w, xhat, r)
        dx_ref[...] = dx
        if below is not None:
            _ffn_out_bwd(dx, rest[0][...], rest[1][3:4, :], rest[-1], part_ref)

    row = pl.BlockSpec((tm, D), lambda i: (i, 0))
    vec = pl.BlockSpec((8, D), lambda i: (0, 0))
    extra = [] if below is None else [row, vec]
    return pl.pallas_call(
        body, name=name, grid=(S // tm,), in_specs=[row, row, row, vec] + extra,
        out_specs=[row, vec] + ([] if below is None else [row]),
        out_shape=[jax.ShapeDtypeStruct((S, D), F32), jax.ShapeDtypeStruct((8, D), F32)]
        + ([] if below is None else [jax.ShapeDtypeStruct((S, D), BF16)]),
        compiler_params=_params(("arbitrary",)),
    )(dh, x, dxo, vecs, *([] if below is None else below))


def _head(x, tgt, nf, f, vecs, *, tm=ROW_TILE):
    S = x.shape[0]
    tm = min(tm, S)

    def body(x_ref, t_ref, nf_ref, f_ref, vec_ref, dx_ref, part_ref, df_ref):
        @pl.when(pl.program_id(0) == 0)
        def _():
            part_ref[...] = jnp.zeros_like(part_ref)

        xv = x_ref[...]
        r = _rstd(xv)
        xhat = xv * r
        w = nf_ref[...]
        e = xhat * w - t_ref[...]
        dy = e * (1.0 / D)
        part_ref[0:1, :] += jnp.sum(dy * xhat, axis=0, keepdims=True)
        part_ref[1:2, :] += jnp.sum(e * e) * (0.5 / D)
        dx = _rms_bwd(dy * w, xhat, r)
        dx_ref[...] = dx
        _ffn_out_bwd(dx, f_ref[...], vec_ref[3:4, :], df_ref, part_ref)

    row = pl.BlockSpec((tm, D), lambda i: (i, 0))
    vec = pl.BlockSpec((8, D), lambda i: (0, 0))
    return pl.pallas_call(
        body, name="head", grid=(S // tm,),
        in_specs=[row, row, pl.BlockSpec((1, D), lambda i: (0, 0)), row, vec],
        out_specs=[row, vec, row],
        out_shape=[jax.ShapeDtypeStruct((S, D), F32), jax.ShapeDtypeStruct((8, D), F32),
                   jax.ShapeDtypeStruct((S, D), BF16)],
        compiler_params=_params(("arbitrary",)),
    )(x, tgt, nf, f, vecs)


def _mix_in_fwd(x, vecs, w_inT, *, tm=ROW_TILE):
    S = x.shape[0]
    tm = min(tm, S)

    def body(x_ref, vec_ref, w_ref, h_ref, p_ref, wb_ref):
        @pl.when(pl.program_id(0) == 0)
        def _():
            wb_ref[0:D_IN, :] = w_ref[...].astype(BF16)
            wb_ref[D_IN:D_IN_PAD, :] = jnp.zeros((D_IN_PAD - D_IN, D), BF16)

        xv = x_ref[...]
        hn = xv * _rstd(xv) * vec_ref[0:1, :]
        h = (hn * (1.0 + vec_ref[2:3, :]) + vec_ref[1:2, :]).astype(BF16)
        h_ref[...] = h
        p_ref[...] = _dot_nt(h, wb_ref[...])

    row = pl.BlockSpec((tm, D), lambda i: (i, 0))
    return pl.pallas_call(
        body, name="mix_in_fwd", grid=(S // tm,),
        in_specs=[row, pl.BlockSpec((8, D), lambda i: (0, 0)),
                  pl.BlockSpec((D_IN, D), lambda i: (0, 0), pipeline_mode=pl.Buffered(1))],
        out_specs=[row, pl.BlockSpec((tm, D_IN_PAD), lambda i: (i, 0)), pl.BlockSpec((D_IN_PAD, D), lambda i: (0, 0))],
        out_shape=[jax.ShapeDtypeStruct((S, D), BF16), jax.ShapeDtypeStruct((S, D_IN_PAD), F32),
                   jax.ShapeDtypeStruct((D_IN_PAD, D), BF16)],
        compiler_params=_params(("arbitrary",)),
    )(x, vecs, w_inT)


def _bucket_table():
    qi = np.arange(WINDOW)[:, None]
    kj = np.arange(2 * WINDOW)[None, :]
    dist = qi + WINDOW - kj
    max_exact = NUM_BUCKETS // 2
    n = np.maximum(dist, 0)
    nf = np.maximum(n, 1).astype(np.float32)
    large = max_exact + (np.log(nf / np.float32(max_exact)) / np.float32(math.log(WINDOW / max_exact))
                         * np.float32(NUM_BUCKETS - max_exact)).astype(np.int32)
    large = np.minimum(large, NUM_BUCKETS - 1)
    return np.where(n < max_exact, n, large).astype(np.int32)


SWA_GROUP = 4
GROUP_ROWS = SWA_GROUP * WINDOW


SWA_SUB = 2


def _swa_valid(has_prev):
    row = lax.broadcasted_iota(jnp.int32, (GROUP_ROWS, 2 * WINDOW), 0) % WINDOW
    col = lax.broadcasted_iota(jnp.int32, (GROUP_ROWS, 2 * WINDOW), 1)
    dist = row + WINDOW - col
    return (dist >= 0) & (dist < WINDOW) & ((col >= WINDOW) | has_prev)


def _swa_keys(prev_ref, cur_ref, u):
    cur = cur_ref[...]
    before = prev_ref[...] if u == 0 else cur[WINDOW * (u - 1):WINDOW * u]
    return jnp.concatenate([before, cur[WINDOW * u:WINDOW * (u + 1)]], axis=0).astype(BF16)


def _stack_heads(x, g):
    return jnp.concatenate([x[:, 64 * h:64 * h + 64] for h in range(SWA_GROUP * g, SWA_GROUP * (g + 1))], axis=0)


def _unstack_heads(x4):
    return jnp.concatenate([x4[WINDOW * a:WINDOW * (a + 1)] for a in range(SWA_GROUP)], axis=1)


def _group_sinks(sink_ref, g):
    head = lax.broadcasted_iota(jnp.int32, (GROUP_ROWS, 1), 0) // WINDOW
    out = jnp.full((GROUP_ROWS, 1), sink_ref[0, SWA_GROUP * g], F32)
    for a in range(1, SWA_GROUP):
        out = jnp.where(head == a, sink_ref[0, SWA_GROUP * g + a], out)
    return out


def _swa_probs(qh, kk, bias_h, sink, valid):
    s = _dot_nt(qh, kk) * SWA_SCALE + bias_h
    s = jnp.where(valid, s, -jnp.inf)
    m = jnp.maximum(jnp.max(s, axis=-1, keepdims=True), sink)
    p = jnp.exp(s - m)
    ps = jnp.exp(sink - m)
    inv = 1.0 / (jnp.sum(p, axis=-1, keepdims=True) + ps)
    return p * inv, ps * inv


SWA_ROWS = SWA_SUB * WINDOW


def _swa_specs():
    prev = lambda n: jnp.maximum(SWA_SUB * n - 1, 0)
    return [pl.BlockSpec((SWA_ROWS, 512), lambda n: (n, 0)),
            pl.BlockSpec((SWA_ROWS, 128), lambda n: (n, 4)),
            pl.BlockSpec((WINDOW, 128), lambda n: (prev(n), 4)),
            pl.BlockSpec((SWA_ROWS, 128), lambda n: (n, 5)),
            pl.BlockSpec((WINDOW, 128), lambda n: (prev(n), 5)),
            pl.BlockSpec((SWA_HEADS, WINDOW, 2 * WINDOW), lambda n: (0, 0, 0)),
            pl.BlockSpec(memory_space=pltpu.SMEM)]


def _swa_fwd(proj, rel_bias, bucket, sinks):
    S = proj.shape[0]

    def body(q_ref, kc_ref, kp_ref, vc_ref, vp_ref, rb_ref, sink_ref, bk_ref, o_ref, bias_ref):
        n = pl.program_id(0)

        @pl.when(n == 0)
        def _():
            bk = bk_ref[...]
            for h in range(SWA_HEADS):
                acc = jnp.zeros((WINDOW, 2 * WINDOW), F32)
                for b in range(NUM_BUCKETS):
                    acc = jnp.where(bk == b, rb_ref[b, h], acc)
                bias_ref[h] = acc

        for u in range(SWA_SUB):
            rows = slice(WINDOW * u, WINDOW * (u + 1))
            valid = _swa_valid(n > 0 if u == 0 else True)
            q = q_ref[rows, :].astype(BF16)
            kfull = _swa_keys(kp_ref, kc_ref, u)
            vfull = _swa_keys(vp_ref, vc_ref, u)
            for g in range(SWA_HEADS // SWA_GROUP):
                kk = kfull[:, 64 * g:64 * g + 64]
                vv = vfull[:, 64 * g:64 * g + 64]
                bias4 = bias_ref[SWA_GROUP * g:SWA_GROUP * (g + 1)].reshape(GROUP_ROWS, 2 * WINDOW)
                pk, _ = _swa_probs(_stack_heads(q, g), kk, bias4, _group_sinks(sink_ref, g), valid)
                o_ref[rows, 256 * g:256 * (g + 1)] = _unstack_heads(_dot(pk.astype(BF16), vv))

    specs = _swa_specs()
    whole = pl.BlockSpec((SWA_HEADS, WINDOW, 2 * WINDOW), lambda n: (0, 0, 0))
    return pl.pallas_call(
        body, name="swa_fwd", grid=(S // SWA_ROWS,),
        in_specs=specs[:5] + [pl.BlockSpec(memory_space=pltpu.SMEM), specs[6],
                              pl.BlockSpec((WINDOW, 2 * WINDOW), lambda n: (0, 0))],
        out_specs=[pl.BlockSpec((SWA_ROWS, 512), lambda n: (n, 0)), whole],
        out_shape=[jax.ShapeDtypeStruct((S, 512), F32), jax.ShapeDtypeStruct((SWA_HEADS, WINDOW, 2 * WINDOW), F32)],
        compiler_params=_params(("arbitrary",)),
    )(proj, proj, proj, proj, proj, rel_bias, sinks, bucket)


def _swa_bwd(proj, bias, sinks, o, do, bucket):
    S = proj.shape[0]
    nb = S // SWA_ROWS

    def body(q_ref, kc_ref, kp_ref, vc_ref, vp_ref, bias_ref, sink_ref, o_ref, do_ref, bk_ref,
             dq_ref, dk_ref, dv_ref, drb_ref, dsk_ref, dbias_acc):
        n = pl.program_id(0)

        @pl.when(n == 0)
        def _():
            dk_ref[...] = jnp.zeros_like(dk_ref)
            dv_ref[...] = jnp.zeros_like(dv_ref)
            dsk_ref[...] = jnp.zeros_like(dsk_ref)
            dbias_acc[...] = jnp.zeros_like(dbias_acc)
            drb_ref[...] = jnp.zeros_like(drb_ref)

        for u in range(SWA_SUB):
            rows = slice(WINDOW * u, WINDOW * (u + 1))
            blk = SWA_SUB * n + u
            valid = _swa_valid(n > 0 if u == 0 else True)
            q = q_ref[rows, :].astype(BF16)
            dov = do_ref[rows, :]
            ov = o_ref[rows, :]
            kfull = _swa_keys(kp_ref, kc_ref, u)
            vfull = _swa_keys(vp_ref, vc_ref, u)
            prow = pl.ds(pl.multiple_of(jnp.maximum(blk - 1, 0) * WINDOW, WINDOW), WINDOW)
            crow = pl.ds(pl.multiple_of(blk * WINDOW, WINDOW), WINDOW)
            for g in range(SWA_HEADS // SWA_GROUP):
                heads = slice(SWA_GROUP * g, SWA_GROUP * (g + 1))
                kk = kfull[:, 64 * g:64 * g + 64]
                vv = vfull[:, 64 * g:64 * g + 64]
                q4 = _stack_heads(q, g)
                pk, psink = _swa_probs(q4, kk, bias_ref[heads].reshape(GROUP_ROWS, 2 * WINDOW),
                                       _group_sinks(sink_ref, g), valid)
                pkb = pk.astype(BF16)
                do4 = _stack_heads(dov, g)
                dob = do4.astype(BF16)
                dp = _dot_nt(dob, vv)
                delta = jnp.sum(do4 * _stack_heads(ov, g), axis=-1, keepdims=True)
                ds = pk * (dp - delta)
                dsink = -psink * delta
                for a in range(SWA_GROUP):
                    h = SWA_GROUP * g + a
                    part = jnp.sum(dsink[WINDOW * a:WINDOW * (a + 1)], keepdims=True)
                    dsk_ref[h:h + 1, :] += jnp.broadcast_to(part, (1, 128))
                dbias_acc[heads] += ds.reshape(SWA_GROUP, WINDOW, 2 * WINDOW)
                dsb = (ds * SWA_SCALE).astype(BF16)
                dq_ref[rows, 256 * g:256 * (g + 1)] = _unstack_heads(_dot(dsb, kk))
                dkk = _dot_tn(dsb, q4)
                dvv = _dot_tn(pkb, dob)
                dk_ref[prow, 64 * g:64 * g + 64] += dkk[:WINDOW]
                dk_ref[crow, 64 * g:64 * g + 64] += dkk[WINDOW:]
                dv_ref[prow, 64 * g:64 * g + 64] += dvv[:WINDOW]
                dv_ref[crow, 64 * g:64 * g + 64] += dvv[WINDOW:]

        @pl.when(n == nb - 1)
        def _():
            bk = bk_ref[...]
            for h in range(SWA_HEADS):
                dbh = dbias_acc[h]
                for b in range(NUM_BUCKETS):
                    val = jnp.sum(jnp.where(bk == b, dbh, 0.0), keepdims=True)
                    row = h * NUM_BUCKETS + b
                    drb_ref[row:row + 1, :] = jnp.broadcast_to(val, (1, 128))

    full = lambda shape: pl.BlockSpec(shape, lambda n: tuple(0 for _ in shape))
    return pl.pallas_call(
        body, name="swa_bwd", grid=(nb,),
        in_specs=_swa_specs() + [pl.BlockSpec((SWA_ROWS, 512), lambda n: (n, 0)),
                                 pl.BlockSpec((SWA_ROWS, 512), lambda n: (n, 0)), full((WINDOW, 2 * WINDOW))],
        out_specs=[pl.BlockSpec((SWA_ROWS, 512), lambda n: (n, 0)), full((S, 128)), full((S, 128)),
                   full((NUM_BUCKETS * 8, 128)), full((8, 128))],
        out_shape=[jax.ShapeDtypeStruct((S, 512), F32), jax.ShapeDtypeStruct((S, 128), F32),
                   jax.ShapeDtypeStruct((S, 128), F32), jax.ShapeDtypeStruct((NUM_BUCKETS * 8, 128), F32),
                   jax.ShapeDtypeStruct((8, 128), F32)],
        scratch_shapes=[pltpu.VMEM((SWA_HEADS, WINDOW, 2 * WINDOW), F32)],
        compiler_params=_params(("arbitrary",)),
    )(proj, proj, proj, proj, proj, bias, sinks, o, do, bucket)


def _rope_tables(S):
    inv = np.float32(ROPE_THETA) ** (-np.arange(0, MLA_ROPE, 2, dtype=np.float32) / np.float32(MLA_ROPE))
    ang = np.arange(S, dtype=np.float32)[:, None] * inv[None, :]
    cos, sin = np.cos(ang), np.sin(ang)
    return (jnp.asarray(np.tile(np.concatenate([cos, cos], axis=1), (1, 2))),
            jnp.asarray(np.tile(np.concatenate([-sin, sin], axis=1), (1, 2))))


def _rope_wide(ref):
    t = ref[...]
    return jnp.concatenate([t, t], axis=1)


def _swap_halves(x):
    w = x.shape[-1]
    lane = lax.broadcasted_iota(jnp.int32, x.shape, x.ndim - 1)
    return jnp.where((lane % 64) < 32, pltpu.roll(x, w - 32, x.ndim - 1), pltpu.roll(x, 32, x.ndim - 1))


def _mla_pre_fwd(proj, qn_w, kvn_w, wuqT, wukv, cos, sin, *, tm=ROW_TILE):
    S = proj.shape[0]
    tm = min(tm, S)

    def body(ql_ref, kl_ref, kr_ref, qw_ref, kw_ref, wuq_ref, wukv_ref, cos_ref, sin_ref,
             qc_ref, kc_ref, vv_ref):
        ql = ql_ref[...]
        qn = (ql * _rstd(ql) * qw_ref[...]).astype(BF16)
        q = _dot_nt(qn, wuq_ref[...])
        cs, sn = _rope_wide(cos_ref), _rope_wide(sin_ref)
        qr = q[:, 512:768]
        qr = qr * cs + _swap_halves(qr) * sn
        half = lax.broadcasted_iota(jnp.int32, (tm, 128), 1) // 64
        kl = kl_ref[...]
        kvn = (kl * _rstd(kl) * kw_ref[...]).astype(BF16)
        kr = kr_ref[...]
        kr = kr * cs[:, :128] + _swap_halves(kr) * sn[:, :128]
        kr2 = (kr + pltpu.roll(kr, 64, 1)).astype(BF16)
        for h in range(MLA_HEADS):
            qc_ref[h, :, 0:128] = q[:, 128 * h:128 * h + 128].astype(BF16)
            chunk = qr[:, 128 * (h // 2):128 * (h // 2) + 128]
            qc_ref[h, :, 128:256] = jnp.where(half == (h % 2), chunk, 0.0).astype(BF16)
            kc_ref[h, :, 0:128] = _dot(kvn, wukv_ref[2 * h]).astype(BF16)
            kc_ref[h, :, 128:256] = kr2
            vv_ref[h] = _dot(kvn, wukv_ref[2 * h + 1]).astype(BF16)

    const = lambda shape: pl.BlockSpec(shape, lambda i: tuple(0 for _ in shape))
    return pl.pallas_call(
        body, name="mla_pre_fwd", grid=(S // tm,),
        in_specs=[pl.BlockSpec((tm, 256), lambda i: (i, 3)), pl.BlockSpec((tm, 128), lambda i: (i, 8)),
                  pl.BlockSpec((tm, 128), lambda i: (i, 9)), const((1, 256)), const((1, 128)),
                  const((768, 256)), const((8, 128, 128)),
                  pl.BlockSpec((tm, 128), lambda i: (i, 0)), pl.BlockSpec((tm, 128), lambda i: (i, 0))],
        out_specs=[pl.BlockSpec((MLA_HEADS, tm, 256), lambda i: (0, i, 0)),
                   pl.BlockSpec((MLA_HEADS, tm, 256), lambda i: (0, i, 0)),
                   pl.BlockSpec((MLA_HEADS, tm, 128), lambda i: (0, i, 0))],
        out_shape=[jax.ShapeDtypeStruct((MLA_HEADS, S, 256), BF16), jax.ShapeDtypeStruct((MLA_HEADS, S, 256), BF16),
                   jax.ShapeDtypeStruct((MLA_HEADS, S, 128), BF16)],
        compiler_params=_params(("parallel",)),
    )(proj, proj, proj, qn_w, kvn_w, wuqT, wukv, cos, sin)


def _causal(i, j, t):
    row = i * t + lax.broadcasted_iota(jnp.int32, (t, t), 0)
    col = j * t + lax.broadcasted_iota(jnp.int32, (t, t), 1)
    return col <= row


def _mla_attn_fwd(qc, kc, vv, *, t=512):
    S = qc.shape[1]
    t = min(t, S)

    def body(q_ref, k_ref, v_ref, o_ref, l_ref):
        i = pl.program_id(0)
        diag = _causal(0, 0, t)

        def step(j, carry, masked):
            rows = pl.ds(pl.multiple_of(j * t, t), t)
            out = []
            for h in range(MLA_HEADS):
                m, l, acc = carry[h]
                s = _dot_nt(q_ref[h], k_ref[h, rows, :]) * MLA_SCALE
                if masked:
                    s = jnp.where(diag, s, -jnp.inf)
                m_new = jnp.maximum(m, jnp.max(s, axis=-1, keepdims=True))
                alpha = jnp.exp(m - m_new)
                p = jnp.exp(s - m_new)
                l = alpha * l + jnp.sum(p, axis=-1, keepdims=True)
                acc = alpha * acc + _dot(p.astype(BF16), v_ref[h, rows, :])
                out.append((m_new, l, acc))
            return tuple(out)

        init = tuple((jnp.full((t, 1), -jnp.inf, F32), jnp.zeros((t, 1), F32), jnp.zeros((t, MLA_V), F32))
                     for _ in range(MLA_HEADS))
        carry = lax.fori_loop(0, i, lambda j, c: step(j, c, False), init)
        carry = step(i, carry, True)
        for h in range(MLA_HEADS):
            m, l, acc = carry[h]
            o_ref[:, 128 * h:128 * h + 128] = acc / l
            l_ref[h] = jnp.broadcast_to(m + jnp.log(l), (t, 128))

    return pl.pallas_call(
        body, name="mla_attn_fwd", grid=(S // t,),
        in_specs=[pl.BlockSpec((MLA_HEADS, t, 256), lambda i: (0, i, 0)),
                  pl.BlockSpec((MLA_HEADS, S, 256), lambda i: (0, 0, 0)),
                  pl.BlockSpec((MLA_HEADS, S, 128), lambda i: (0, 0, 0))],
        out_specs=[pl.BlockSpec((t, 512), lambda i: (i, 0)),
                   pl.BlockSpec((MLA_HEADS, t, 128), lambda i: (0, i, 0))],
        out_shape=[jax.ShapeDtypeStruct((S, 512), F32), jax.ShapeDtypeStruct((MLA_HEADS, S, 128), F32)],
        compiler_params=_params(("parallel",)),
    )(qc, kc, vv)


def _mla_attn_bwd(qc, kc, vv, o, lse, do, *, t=512, tq=1024):
    S = qc.shape[1]
    t = min(t, S)
    tq = min(tq, S)
    nblk = S // t
    hp = MLA_HEADS
    once = pl.Buffered(1)

    def body(q_ref, k_ref, v_ref, o_ref, l_ref, do_ref, dq_ref, dk_ref, dv_ref):
        j = pl.program_id(1)

        @pl.when(j == 0)
        def _():
            dq_ref[...] = jnp.zeros_like(dq_ref)

        first = (j * t) // tq

        def step(i, carry, masked):
            rows = pl.ds(pl.multiple_of(i * tq, tq), tq)
            if masked:
                row = i * tq + lax.broadcasted_iota(jnp.int32, (tq, t), 0)
                col = j * t + lax.broadcasted_iota(jnp.int32, (tq, t), 1)
                visible = col <= row
            out = []
            for h in range(hp):
                dk, dv = carry[h]
                k = k_ref[h]
                q = q_ref[h, rows, :]
                dov = do_ref[rows, 128 * h:128 * h + 128]
                lrow = l_ref[h, rows, :][:, 0:1]
                p = jnp.exp(_dot_nt(q, k) * MLA_SCALE - lrow)
                if masked:
                    p = jnp.where(visible, p, 0.0)
                dob = dov.astype(BF16)
                dv = dv + _dot_tn(p.astype(BF16), dob)
                dp = _dot_nt(dob, v_ref[h])
                delta = jnp.sum(dov * o_ref[rows, 128 * h:128 * h + 128], axis=-1, keepdims=True)
                ds = (p * (dp - delta) * MLA_SCALE).astype(BF16)
                dk = dk + _dot_tn(ds, q)
                dq_ref[h, rows, :] += _dot(ds, k)
                out.append((dk, dv))
            return tuple(out)

        init = tuple((jnp.zeros((t, 256), F32), jnp.zeros((t, MLA_V), F32)) for _ in range(hp))
        carry = step(first, init, True)
        carry = lax.fori_loop(first + 1, S // tq, lambda i, c: step(i, c, False), carry)
        for h in range(hp):
            dk_ref[h] = carry[h][0]
            dv_ref[h] = carry[h][1]

    return pl.pallas_call(
        body, name="mla_attn_bwd", grid=(MLA_HEADS // hp, nblk),
        in_specs=[pl.BlockSpec((hp, S, 256), lambda g, j: (g, 0, 0), pipeline_mode=once),
                  pl.BlockSpec((hp, t, 256), lambda g, j: (g, j, 0)),
                  pl.BlockSpec((hp, t, 128), lambda g, j: (g, j, 0)),
                  pl.BlockSpec((S, 128 * hp), lambda g, j: (0, g), pipeline_mode=once),
                  pl.BlockSpec((hp, S, 128), lambda g, j: (g, 0, 0), pipeline_mode=once),
                  pl.BlockSpec((S, 128 * hp), lambda g, j: (0, g), pipeline_mode=once)],
        out_specs=[pl.BlockSpec((hp, S, 256), lambda g, j: (g, 0, 0)),
                   pl.BlockSpec((hp, t, 256), lambda g, j: (g, j, 0)),
                   pl.BlockSpec((hp, t, 128), lambda g, j: (g, j, 0))],
        out_shape=[jax.ShapeDtypeStruct((MLA_HEADS, S, 256), F32), jax.ShapeDtypeStruct((MLA_HEADS, S, 256), F32),
                   jax.ShapeDtypeStruct((MLA_HEADS, S, 128), F32)],
        compiler_params=_params(("parallel", "arbitrary")),
    )(qc, kc, vv, o, lse, do)


def _mla_pre_bwd(proj, qn_w, kvn_w, wuqT, wukv, cos, sin, dqc, dkc, dvv, *, tm=ROW_TILE):
    S = proj.shape[0]
    tm = min(tm, S)

    def body(ql_ref, kl_ref, qw_ref, kw_ref, wuq_ref, wukv_ref, cos_ref, sin_ref, dqc_ref, dkc_ref, dvv_ref,
             dql_ref, dkl_ref, dkr_ref, gq_ref, gkv_ref, part_ref):
        @pl.when(pl.program_id(0) == 0)
        def _():
            gq_ref[...] = jnp.zeros_like(gq_ref)
            gkv_ref[...] = jnp.zeros_like(gkv_ref)
            part_ref[...] = jnp.zeros_like(part_ref)

        cs, sn = _rope_wide(cos_ref), _rope_wide(sin_ref)
        half = lax.broadcasted_iota(jnp.int32, (tm, 128), 1) // 64
        ql = ql_ref[...]
        rq = _rstd(ql)
        qhat = ql * rq
        qw = qw_ref[...]
        qn = (qhat * qw).astype(BF16)
        chunks = []
        for pair in range(2):
            chunks.append(jnp.where(half == 0, dqc_ref[2 * pair, :, 128:256], dqc_ref[2 * pair + 1, :, 128:256]))
        dqr = jnp.concatenate(chunks, axis=1)
        dqr = dqr * cs + _swap_halves(dqr * sn)
        dq = jnp.concatenate([dqc_ref[h, :, 0:128] for h in range(MLA_HEADS)] + [dqr], axis=1).astype(BF16)
        gq_ref[...] += _dot_tn(dq, qn)
        dqn = _dot(dq, wuq_ref[...])
        part_ref[0:1, :] += jnp.sum(dqn * qhat, axis=0, keepdims=True)
        dql_ref[...] = _rms_bwd(dqn * qw, qhat, rq)
        kl = kl_ref[...]
        rk = _rstd(kl)
        khat = kl * rk
        kw = kw_ref[...]
        kvn = (khat * kw).astype(BF16)
        dkvn = jnp.zeros((tm, MLA_KVR), F32)
        dkr2 = jnp.zeros((tm, 128), F32)
        for h in range(MLA_HEADS):
            dkn = dkc_ref[h, :, 0:128].astype(BF16)
            dvh = dvv_ref[h].astype(BF16)
            gkv_ref[2 * h] += _dot_tn(kvn, dkn)
            gkv_ref[2 * h + 1] += _dot_tn(kvn, dvh)
            dkvn += _dot_nt(dkn, wukv_ref[2 * h]) + _dot_nt(dvh, wukv_ref[2 * h + 1])
            dkr2 += dkc_ref[h, :, 128:256]
        part_ref[1:2, 0:128] += jnp.sum(dkvn * khat, axis=0, keepdims=True)
        dkl_ref[...] = _rms_bwd(dkvn * kw, khat, rk)
        dkr = jnp.where(half == 0, dkr2 + pltpu.roll(dkr2, 64, 1), 0.0)
        dkr_ref[...] = dkr * cs[:, :128] + _swap_halves(dkr * sn[:, :128])

    const = lambda shape: pl.BlockSpec(shape, lambda i: tuple(0 for _ in shape))
    heads = lambda w: pl.BlockSpec((MLA_HEADS, tm, w), lambda i: (0, i, 0))
    return pl.pallas_call(
        body, name="mla_pre_bwd", grid=(S // tm,),
        in_specs=[pl.BlockSpec((tm, 256), lambda i: (i, 3)), pl.BlockSpec((tm, 128), lambda i: (i, 8)),
                  const((1, 256)), const((1, 128)), const((768, 256)), const((8, 128, 128)),
                  pl.BlockSpec((tm, 128), lambda i: (i, 0)), pl.BlockSpec((tm, 128), lambda i: (i, 0)),
                  heads(256), heads(256), heads(128)],
        out_specs=[pl.BlockSpec((tm, 256), lambda i: (i, 0)), pl.BlockSpec((tm, 128), lambda i: (i, 0)),
                   pl.BlockSpec((tm, 128), lambda i: (i, 0)), const((768, 256)), const((8, 128, 128)), const((8, 256))],
        out_shape=[jax.ShapeDtypeStruct((S, 256), F32), jax.ShapeDtypeStruct((S, 128), F32),
                   jax.ShapeDtypeStruct((S, 128), F32), jax.ShapeDtypeStruct((768, 256), F32),
                   jax.ShapeDtypeStruct((8, 128, 128), F32), jax.ShapeDtypeStruct((8, 256), F32)],
        compiler_params=_params(("arbitrary",)),
    )(proj, proj, qn_w, kvn_w, wuqT, wukv, cos, sin, dqc, dkc, dvv)


def _mix_out_fwd(x, oa, ob, w_o, vecs, *, tm=ROW_TILE):
    S = x.shape[0]
    tm = min(tm, S)

    def body(x_ref, oa_ref, ob_ref, w_ref, vec_ref, xo_ref, mo_ref):
        mo = _dot(oa_ref[...].astype(BF16), w_ref[0:512, :]) + _dot(ob_ref[...].astype(BF16), w_ref[512:1024, :])
        mo_ref[...] = mo
        xo_ref[...] = x_ref[...] + vec_ref[3:4, :] * mo

    row = pl.BlockSpec((tm, D), lambda i: (i, 0))
    half = pl.BlockSpec((tm, 512), lambda i: (i, 0))
    return pl.pallas_call(
        body, name="mix_out_fwd", grid=(S // tm,),
        in_specs=[row, half, half, pl.BlockSpec((D, D), lambda i: (0, 0)), pl.BlockSpec((8, D), lambda i: (0, 0))],
        out_specs=[row, row],
        out_shape=[jax.ShapeDtypeStruct((S, D), F32), jax.ShapeDtypeStruct((S, D), F32)],
        compiler_params=_params(("parallel",)),
    )(x, oa, ob, w_o, vecs)


def _mix_out_bwd(dxo, mo, oa, ob, w_o, vecs, *, tm=ROW_TILE):
    S = dxo.shape[0]
    tm = min(tm, S)

    def body(dx_ref, mo_ref, oa_ref, ob_ref, w_ref, vec_ref, doa_ref, dob_ref, gw_ref, part_ref):
        @pl.when(pl.program_id(0) == 0)
        def _():
            gw_ref[...] = jnp.zeros_like(gw_ref)
            part_ref[...] = jnp.zeros_like(part_ref)

        dx = dx_ref[...]
        part_ref[0:1, :] += jnp.sum(dx * mo_ref[...], axis=0, keepdims=True)
        dmo = (vec_ref[3:4, :] * dx).astype(BF16)
        doa_ref[...] = _dot_nt(dmo, w_ref[0:512, :])
        dob_ref[...] = _dot_nt(dmo, w_ref[512:1024, :])
        gw_ref[0:512, :] += _dot_tn(oa_ref[...].astype(BF16), dmo)
        gw_ref[512:1024, :] += _dot_tn(ob_ref[...].astype(BF16), dmo)

    row = pl.BlockSpec((tm, D), lambda i: (i, 0))
    half = pl.BlockSpec((tm, 512), lambda i: (i, 0))
    return pl.pallas_call(
        body, name="mix_out_bwd", grid=(S // tm,),
        in_specs=[row, row, half, half, pl.BlockSpec((D, D), lambda i: (0, 0)), pl.BlockSpec((8, D), lambda i: (0, 0))],
        out_specs=[half, half, pl.BlockSpec((D, D), lambda i: (0, 0)), pl.BlockSpec((8, D), lambda i: (0, 0))],
        out_shape=[jax.ShapeDtypeStruct((S, 512), F32), jax.ShapeDtypeStruct((S, 512), F32),
                   jax.ShapeDtypeStruct((D, D), F32), jax.ShapeDtypeStruct((8, D), F32)],
        compiler_params=_params(("arbitrary",)),
    )(dxo, mo, oa, ob, w_o, vecs)


def _mix_in_bwd(h, w_inT, dq, dk, dv, dql, dkl, dkr, *, tm=ROW_TILE):
    S = h.shape[0]
    tm = min(tm, S)
    offs = (0, 512, 640, 768, 1024, 1152)
    wid = (512, 128, 128, 256, 128, 128)

    def body(h_ref, w_ref, dq_ref, dk_ref, dv_ref, dql_ref, dkl_ref, dkr_ref, dh_ref, gw_ref):
        @pl.when(pl.program_id(0) == 0)
        def _():
            gw_ref[...] = jnp.zeros_like(gw_ref)

        hv = h_ref[...]
        dh = jnp.zeros((tm, D), F32)
        for ref, o, w in zip((dq_ref, dk_ref, dv_ref, dql_ref, dkl_ref, dkr_ref), offs, wid):
            w = min(w, D_IN - o)
            dpart = ref[...][:, :w].astype(BF16)
            dh += _dot(dpart, w_ref[o:o + w, :])
            gw_ref[o:o + w, :] += _dot_tn(dpart, hv)
        dh_ref[...] = dh

    row = pl.BlockSpec((tm, D), lambda i: (i, 0))
    part = lambda w: pl.BlockSpec((tm, w), lambda i: (i, 0))
    return pl.pallas_call(
        body, name="mix_in_bwd", grid=(S // tm,),
        in_specs=[row, pl.BlockSpec((D_IN_PAD, D), lambda i: (0, 0))] + [part(w) for w in wid],
        out_specs=[row, pl.BlockSpec((D_IN, D), lambda i: (0, 0))],
        out_shape=[jax.ShapeDtypeStruct((S, D), F32), jax.ShapeDtypeStruct((D_IN, D), F32)],
        compiler_params=_params(("arbitrary",)),
    )(h, w_inT, dq, dk, dv, dql, dkl, dkr)


def _vecs(norm_w, mod9, k):
    return jnp.concatenate([norm_w.reshape(1, D), mod9[3 * k:3 * k + 3], jnp.zeros((4, D), F32)], axis=0)


def _uq_group_rows(wuqT):
    per = MLA_NOPE + MLA_ROPE
    nope = [wuqT[per * h:per * h + MLA_NOPE] for h in range(MLA_HEADS)]
    rope = [wuqT[per * h + MLA_NOPE:per * (h + 1)] for h in range(MLA_HEADS)]
    return jnp.concatenate(nope + rope, axis=0)


def _uq_ungroup_rows(g):
    parts = []
    for h in range(MLA_HEADS):
        parts += [g[MLA_NOPE * h:MLA_NOPE * (h + 1)], g[512 + MLA_ROPE * h:512 + MLA_ROPE * (h + 1)]]
    return jnp.concatenate(parts, axis=0)


def _local_step(x, tgt, mod9, norms, sinks, rel_bias, q_norm, kv_norm, W, on_grads=None):
    if on_grads is None:
        on_grads = lambda group, grads, after, vecs: vecs
    S = x.shape[0]
    v1 = _vecs(norms["ffn1"], mod9, 0)
    v2 = _vecs(norms["mix"], mod9, 1)
    v3 = _vecs(norms["ffn2"], mod9, 2)
    bucket = jnp.asarray(_bucket_table())
    cos, sin = _rope_tables(S)
    if isinstance(W, dict):
        full, W = W, (lambda group, after, vecs: (full, vecs))

    W1, v1 = W("ffn1", [], v1)
    x1, h1, a1, b1, f1 = _ffn_fwd(x, v1, W1["g1T"], W1["u1T"], W1["d1"], name="ffn1_fwd")
    W2, v2 = W("mixer", [x1], v2)
    wuqT = _uq_group_rows(W2["w_uqT"])
    h2, proj, w_inT = _mix_in_fwd(x1, v2, W2["w_inT"])
    oa, bias = _swa_fwd(proj, rel_bias, bucket, sinks)
    qc, kc, vv = _mla_pre_fwd(proj, q_norm, kv_norm, wuqT, W2["w_ukv"], cos, sin)
    ob, lse = _mla_attn_fwd(qc, kc, vv)
    _, v2o = W("ffn2_on_its_way", [ob], v2)
    x2, mo = _mix_out_fwd(x1, oa, ob, W2["w_o"], v2o)
    W3, v3 = W("ffn2", [x2], v3)
    x3, h3, a3, b3, f3 = _ffn_fwd(x2, v3, W3["g3T"], W3["u3T"], W3["d3"], name="ffn2_fwd")
    dx3, head_part, df3 = _head(x3, tgt, norms["final"], f3, v3)

    gg3, gu3, gd3, dh3 = _ffn_bwd_main(h3, df3, a3, b3, W3["g3T"], W3["u3T"], W3["d3"], name="ffn2_bwd")
    ffn2 = {"g3T": gg3, "u3T": gu3, "d3": gd3}
    v3 = on_grads("ffn2", ffn2, [], v3)
    dx2, n3_part = _norm_bwd(dh3, x2, dx3, v3, name="ffn2_norm_bwd")
    v2 = on_grads("ffn2", None, [dx2], v2)
    doa, dob, g_wo, g2_part = _mix_out_bwd(dx2, mo, oa, ob, W2["w_o"], v2)
    dq, dk, dv, drb, dsk = _swa_bwd(proj, bias, sinks, oa, doa, bucket)
    dqc, dkc, dvv = _mla_attn_bwd(qc, kc, vv, ob, lse, dob)
    dql, dkl, dkr, g_uq, g_ukv, mla_part = _mla_pre_bwd(proj, q_norm, kv_norm, wuqT, W2["w_ukv"], cos, sin, dqc, dkc, dvv)
    dh2, g_win = _mix_in_bwd(h2, w_inT, dq, dk, dv, dql, dkl, dkr)
    mixer = {"w_inT": g_win, "w_uqT": _uq_ungroup_rows(g_uq).astype(BF16),
             "w_ukv": g_ukv.astype(BF16), "w_o": g_wo.astype(BF16)}
    v2 = on_grads("mixer", mixer, [], v2)
    dx1, n2_part, df1 = _norm_bwd(dh2, x1, dx2, v2, name="mix_norm_bwd", below=(f1, v1))
    started = on_grads("mixer", None, [dx1], jnp.zeros((1, 1), F32))
    gg1, gu1, gd1, dh1 = _ffn_bwd_main(h1, df1, a1, b1, W1["g1T"], W1["u1T"], W1["d1"], name="ffn1_bwd",
                                       after=[started])
    ffn1 = {"g1T": gg1, "u1T": gu1, "d1": gd1}
    v1 = on_grads("ffn1", ffn1, [], v1)
    dx0, n1_part = _norm_bwd(dh1, x, dx1, v1, name="ffn1_norm_bwd")

    grads = {**ffn1, **ffn2, **mixer}
    return head_part[1, 0], dx0, grads, _pack_vec(n1_part, n2_part, n3_part, head_part, g2_part, mla_part, dsk, drb)


SMALL_LAYOUT = (("norm_ffn1", 1024), ("norm_mix", 1024), ("norm_ffn2", 1024), ("norm_final", 1024),
                ("q_norm", 256), ("kv_norm", 128), ("sinks", 128), ("rel_bias", 256))
N_SMALL = sum(n for _, n in SMALL_LAYOUT)
LOSS_SLOT = 4 * 1024 + 256 + 128 + SWA_HEADS
N_MODVEC = N_MOD * D
N_VEC = N_MODVEC + N_SMALL


def _pack_vec(n1, n2, n3, head, g2, mla, dsk, drb):
    def body(n1_ref, n2_ref, n3_ref, head_ref, g2_ref, mla_ref, dsk_ref, drb_ref, out_ref):
        rows = [n1_ref[1:2, :], n1_ref[2:3, :], n2_ref[3:4, :], n2_ref[1:2, :], n2_ref[2:3, :], g2_ref[0:1, :],
                n3_ref[1:2, :], n3_ref[2:3, :], head_ref[3:4, :],
                n1_ref[0:1, :], n2_ref[0:1, :], n3_ref[0:1, :], head_ref[0:1, :]]
        for i, row in enumerate(rows):
            out_ref[:, D * i:D * (i + 1)] = row
        off = D * len(rows)
        out_ref[:, off:off + 256] = mla_ref[0:1, :]
        out_ref[:, off + 256:off + 384] = mla_ref[1:2, 0:128]

        def diagonal(block):
            r = lax.broadcasted_iota(jnp.int32, block.shape, 0)
            lane = lax.broadcasted_iota(jnp.int32, block.shape, 1)
            return jnp.sum(jnp.where(r == lane, block, 0.0), axis=0, keepdims=True)

        lane = lax.broadcasted_iota(jnp.int32, (1, 128), 1)
        out_ref[:, off + 384:off + 512] = jnp.where(lane == SWA_HEADS, head_ref[1:2, 0:128], diagonal(dsk_ref[...]))
        out_ref[:, off + 512:off + 640] = diagonal(drb_ref[0:128, :])
        out_ref[:, off + 640:off + 768] = diagonal(drb_ref[128:256, :])

    vm = pl.BlockSpec(memory_space=pltpu.VMEM)
    return pl.pallas_call(body, name="pack_vec", in_specs=[vm] * 8, out_specs=vm,
                          out_shape=jax.ShapeDtypeStruct((1, N_VEC), F32))(n1, n2, n3, head, g2, mla, dsk, drb)


def _coords():
    return lax.axis_index("x"), lax.axis_index("y"), lax.axis_index("c")


def _flip(v, bit):
    return 1 - v if bit else v


def _peer(r):
    x, y, c = _coords()
    return (_flip(x, r & 4), _flip(y, r & 2), _flip(c, r & 1))


def _mod_fwd(c_tile, w_mod, b_mod3):
    W = w_mod.shape[1]

    def body(c_ref, w_ref, b_ref, mod_ref, ca_ref, call_ref, part_ref, send_sems, recv_sems):
        x, y, c = _coords()
        me = 4 * x + 2 * y + c
        call_ref[me] = c_ref[...]
        sends = []
        for r in range(1, N_DEV):
            cp = pltpu.make_async_remote_copy(c_ref, call_ref.at[me], send_sems.at[0, r], recv_sems.at[0, r],
                                              device_id=_peer(r), device_id_type=MESH)
            cp.start()
            sends.append(cp)
        for r in range(1, N_DEV):
            pltpu.make_async_remote_copy(c_ref, call_ref.at[me], send_sems.at[0, r], recv_sems.at[0, r],
                                         device_id=_peer(r), device_id_type=MESH).wait_recv()
        cv = call_ref[...].reshape(8 * N_DEV, D)
        ca = (cv * _sigmoid(cv)).astype(BF16)
        ca_ref[...] = ca
        part_ref[...] = _dot(ca, w_ref[...].astype(BF16)).reshape(N_DEV, 8, W)
        mod_ref[me] = part_ref[me] + b_ref[me]
        for r in range(1, N_DEV):
            cp = pltpu.make_async_remote_copy(part_ref.at[me ^ r], mod_ref.at[me], send_sems.at[1, r],
                                              recv_sems.at[1, r], device_id=_peer(r), device_id_type=MESH)
            cp.start()
            sends.append(cp)
        for r in range(1, N_DEV):
            pltpu.make_async_remote_copy(part_ref.at[me ^ r], mod_ref.at[me], send_sems.at[1, r],
                                         recv_sems.at[1, r], device_id=_peer(r), device_id_type=MESH).wait_recv()
            mod_ref[me ^ r] = mod_ref[me ^ r] + b_ref[me ^ r]
        for cp in sends:
            cp.wait_send()

    vm = pl.BlockSpec(memory_space=pltpu.VMEM)
    return pl.pallas_call(
        body, name="mod_fwd", in_specs=[vm, vm, vm], out_specs=[vm, vm],
        out_shape=[jax.ShapeDtypeStruct((N_DEV, 8, W), F32), jax.ShapeDtypeStruct((8 * N_DEV, D), BF16)],
        scratch_shapes=[pltpu.VMEM((N_DEV, 8, D), F32), pltpu.VMEM((N_DEV, 8, W), F32),
                        pltpu.SemaphoreType.DMA((2, N_DEV)), pltpu.SemaphoreType.DMA((2, N_DEV))],
        compiler_params=_params(),
    )(c_tile, w_mod, b_mod3)


def _mod_bwd(allvec, ca, me_idx):
    W = N_MODVEC // N_DEV

    def body(me_ref, all_ref, cols_ref, ca_ref, gw_ref, sum_ref):
        in_first_row = lax.broadcasted_iota(jnp.int32, (N_DEV, 8, W), 1) == 0
        dm = jnp.where(in_first_row, cols_ref[...], 0.0).reshape(8 * N_DEV, W)
        gw_ref[...] = _dot_tn(ca_ref[...], dm.astype(BF16))
        total = all_ref[0]
        for k in range(1, N_DEV):
            total = total + all_ref[k]
        sum_ref[...] = total

    return pl.pallas_call(
        body, name="mod_bwd",
        grid_spec=pltpu.PrefetchScalarGridSpec(
            num_scalar_prefetch=1, grid=(1,),
            in_specs=[pl.BlockSpec((N_DEV, 1, N_VEC), lambda i, me: (0, 0, 0)),
                      pl.BlockSpec((N_DEV, 1, W), lambda i, me: (0, 0, me[0])),
                      pl.BlockSpec((8 * N_DEV, D), lambda i, me: (0, 0))],
            out_specs=[pl.BlockSpec((D, W), lambda i, me: (0, 0)), pl.BlockSpec((1, N_VEC), lambda i, me: (0, 0))]),
        out_shape=[jax.ShapeDtypeStruct((D, W), F32), jax.ShapeDtypeStruct((1, N_VEC), F32)],
        compiler_params=_params(("arbitrary",)),
    )(me_idx, allvec, allvec, ca)


def _wgather(shards):
    n = len(shards)
    rows = [s.shape[0] for s in shards]

    def body(*refs):
        ins, outs, token = refs[:n], refs[n:2 * n], refs[2 * n]
        send_sems, recv_sems, local_sems = refs[2 * n + 1:]
        token[...] = jnp.zeros_like(token)
        x, y, c = _coords()
        me = 4 * x + 2 * y + c
        sib, xn, yn = (x, y, 1 - c), (1 - x, y, c), (x, 1 - y, c)
        block = lambda px, py, pc: 4 * px + 2 * py + pc

        def part(k, blk, half):
            if half is None:
                return outs[k].at[blk]
            return outs[k].at[blk, pl.ds(half * (rows[k] // 2), rows[k] // 2)]

        def copy(k, slot, blk, to, half=None, src=None):
            ref = part(k, blk, half)
            return pltpu.make_async_remote_copy(
                src_ref=ref if src is None else src, dst_ref=ref, send_sem=send_sems.at[k, slot],
                recv_sem=recv_sems.at[k, slot], device_id=to, device_id_type=MESH)

        local = [pltpu.make_async_copy(ins[k], outs[k].at[me], local_sems.at[k]) for k in range(n)]
        for cp in local:
            cp.start()
        sent = [copy(k, slot, me, to, src=ins[k]) for k in range(n) for slot, to in ((0, sib), (1, xn), (2, yn))]
        for cp in sent:
            cp.start()
        bx, by, bd = block(1 - x, y, c), block(x, 1 - y, c), block(1 - x, 1 - y, c)
        for k in range(n):
            copy(k, 1, bx, sib).wait_recv()
            sent += [copy(k, 4, bx, yn, half=1), copy(k, 5, bx, sib)]
            sent[-2].start()
            sent[-1].start()
        for k in range(n):
            copy(k, 2, by, sib).wait_recv()
            sent += [copy(k, 3, by, xn, half=0), copy(k, 6, by, sib)]
            sent[-2].start()
            sent[-1].start()
        for k in range(n):
            copy(k, 3, bd, sib, half=0).wait_recv()
            copy(k, 4, bd, sib, half=1).wait_recv()
            sent.append(copy(k, 7, bd, sib))
            sent[-1].start()
        for k in range(n):
            copy(k, 0, block(x, y, 1 - c), sib).wait_recv()
            for slot, blk in ((5, block(1 - x, y, 1 - c)), (6, block(x, 1 - y, 1 - c)), (7, block(1 - x, 1 - y, 1 - c))):
                copy(k, slot, blk, sib).wait_recv()
        for cp in sent:
            cp.wait_send()
        for cp in local:
            cp.wait()

    anyspec = pl.BlockSpec(memory_space=pl.ANY)
    return pl.pallas_call(
        body, name="wgather", in_specs=[anyspec] * n,
        out_specs=[anyspec] * n + [pl.BlockSpec(memory_space=pltpu.VMEM)],
        out_shape=[jax.ShapeDtypeStruct((N_DEV,) + s.shape, s.dtype) for s in shards]
        + [jax.ShapeDtypeStruct((8, 128), F32)],
        scratch_shapes=[pltpu.SemaphoreType.DMA((n, 8)), pltpu.SemaphoreType.DMA((n, 8)),
                        pltpu.SemaphoreType.DMA((n,))],
    )(*shards)


class _GatherCopies:
    def __init__(self, lands, send_sems, recv_sems, k0=0, batches=None):
        x, y, c = _coords()
        me = 4 * x + 2 * y + c
        sib = (x, y, 1 - c)
        chips = [(1 - x, y), (x, 1 - y), (1 - x, 1 - y)]

        def copy(k, slot, block, to):
            return pltpu.make_async_remote_copy(
                src_ref=lands[k].at[block], dst_ref=lands[k].at[block],
                send_sem=send_sems.at[7 * (k0 + k) + slot], recv_sem=recv_sems.at[7 * (k0 + k) + slot],
                device_id=to, device_id_type=MESH)

        n = len(lands)
        self.first = [copy(k, 0, me, sib) for k in range(n)]
        for batch in batches or [range(n)]:
            self.first += [copy(k, 1 + j, me, (cx, cy, c)) for j, (cx, cy) in enumerate(chips) for k in batch]
        self.landed = [copy(k, 1 + j, 4 * cx + 2 * cy + c, sib) for j, (cx, cy) in enumerate(chips) for k in range(n)]
        self.passed = [copy(k, 4 + j, 4 * cx + 2 * cy + c, sib) for j, (cx, cy) in enumerate(chips) for k in range(n)]
        self.from_sib = [copy(k, 0, 4 * x + 2 * y + (1 - c), sib) for k in range(n)]
        self.from_sib += [copy(k, 4 + j, 4 * cx + 2 * cy + (1 - c), sib) for j, (cx, cy) in enumerate(chips)
                          for k in range(n)]


def _gather_start(lands, *, name, batches=None):
    n = len(lands)

    def body(*refs):
        for cp in _GatherCopies(refs[:n], refs[n], refs[n + 1], batches=batches).first:
            cp.start()
        refs[-1][...] = jnp.zeros_like(refs[-1])

    out = pl.pallas_call(
        body, name=name,
        out_shape=(pltpu.SemaphoreType.DMA((7 * n,)), pltpu.SemaphoreType.DMA((7 * n,)),
                   *[pltpu.HBM(l.shape, l.dtype) for l in lands], jax.ShapeDtypeStruct((8, 128), F32)),
        in_specs=[HBM_SPEC] * n,
        out_specs=(SEM_SPEC, SEM_SPEC, *[HBM_SPEC] * n, pl.BlockSpec(memory_space=pltpu.VMEM)),
        input_output_aliases={i: 2 + i for i in range(n)},
        compiler_params=pltpu.CompilerParams(has_side_effects=DATAFLOW),
    )(*[_in_hbm(l) for l in lands])
    return out[0], out[1], list(out[2:2 + n]), out[-1]


def _gather_pass(send_sems, recv_sems, lands, after, *, name, stage, k0=0):
    n = len(lands)

    def body(*refs):
        cps = _GatherCopies(refs[:n], refs[n], refs[n + 1], k0)
        if stage == "landed":
            for cp in cps.landed:
                cp.wait_recv()
        else:
            for cp in cps.passed:
                cp.start()
        refs[-1][...] = jnp.zeros_like(refs[-1])

    out = pl.pallas_call(
        body, name=name,
        out_shape=(*[pltpu.HBM(l.shape, l.dtype) for l in lands], jax.ShapeDtypeStruct((8, 128), F32)),
        in_specs=[HBM_SPEC] * n + [SEM_SPEC, SEM_SPEC] + [pl.BlockSpec(memory_space=pl.ANY)] * len(after),
        out_specs=(*[HBM_SPEC] * n, pl.BlockSpec(memory_space=pltpu.VMEM)),
        input_output_aliases={i: i for i in range(n)},
        compiler_params=pltpu.CompilerParams(has_side_effects=DATAFLOW),
    )(*lands, send_sems, recv_sems, *after)
    return list(out[:n]), out[-1]


def _gather_end(send_sems, recv_sems, lands, after, *, name, k0=0):
    n = len(lands)

    def body(*refs):
        cps = _GatherCopies(refs[:n], refs[n], refs[n + 1], k0)
        for cp in cps.from_sib:
            cp.wait_recv()
        for cp in cps.first + cps.passed:
            cp.wait_send()

    out = pl.pallas_call(
        body, name=name,
        out_shape=[pltpu.HBM(l.shape, l.dtype) for l in lands],
        in_specs=[HBM_SPEC] * n + [SEM_SPEC, SEM_SPEC] + [pl.BlockSpec(memory_space=pl.ANY)] * len(after),
        out_specs=[HBM_SPEC] * n,
        input_output_aliases={i: i for i in range(n)},
        compiler_params=pltpu.CompilerParams(has_side_effects=DATAFLOW),
    )(*lands, send_sems, recv_sems, *after)
    return list(out)


def _d2d_copies(grads, lands, send_sems, recv_sems):
    x, y, c = _coords()
    return [pltpu.make_async_remote_copy(
        src_ref=grads[k].at[2 * q + (1 - c)], dst_ref=lands[k].at[q],
        send_sem=send_sems.at[4 * k + q], recv_sem=recv_sems.at[4 * k + q],
        device_id=(x, y, 1 - c), device_id_type=MESH) for k in range(len(grads)) for q in range(4)]


def _direct_copies(grads, lands, send_sems, recv_sems):
    x, y, c = _coords()
    me = 4 * x + 2 * y + c
    return [pltpu.make_async_remote_copy(
        src_ref=grads[k].at[me ^ r], dst_ref=lands[k].at[r - 1],
        send_sem=send_sems.at[7 * k + r - 1], recv_sem=recv_sems.at[7 * k + r - 1],
        device_id=_peer(r), device_id_type=MESH) for k in range(len(grads)) for r in range(1, N_DEV)]


def _vec_copies(srcs, lands, send_sems, recv_sems):
    x, y, c = _coords()
    me = 4 * x + 2 * y + c
    return [pltpu.make_async_remote_copy(
        src_ref=lands[0].at[me], dst_ref=lands[0].at[me], send_sem=send_sems.at[r - 1], recv_sem=recv_sems.at[r - 1],
        device_id=_peer(r), device_id_type=MESH) for r in range(1, N_DEV)]


def _chipsum(gs, sibs, cidx, *, name):
    n = len(gs)

    def body(c_ref, *refs):
        for k in range(n):
            refs[2 * n + k][...] = (refs[k][...].astype(F32) + refs[n + k][...].astype(F32)).astype(refs[2 * n + k].dtype)

    mine = [pl.BlockSpec((1,) + g.shape[1:], lambda q, c_ref: (2 * q + c_ref[0], 0, 0)) for g in gs]
    other = [pl.BlockSpec((1,) + g.shape[1:], lambda q, c_ref: (q, 0, 0)) for g in gs]
    return pl.pallas_call(
        body, name=name,
        grid_spec=pltpu.PrefetchScalarGridSpec(num_scalar_prefetch=1, grid=(4,), in_specs=mine + other, out_specs=other),
        out_shape=[jax.ShapeDtypeStruct((4,) + g.shape[1:], g.dtype) for g in gs],
        compiler_params=_params(("arbitrary",)),
    )(cidx, *gs, *sibs)


HBM_SPEC = pl.BlockSpec(memory_space=pltpu.HBM)
SEM_SPEC = pl.BlockSpec(memory_space=pltpu.SEMAPHORE)
DATAFLOW = pltpu.SideEffectType.DATAFLOW_SIDE_EFFECTING


def _in_hbm(a):
    return pltpu.with_memory_space_constraint(a, pltpu.HBM)


def _rs_step1_copies(sums, lands, send_sems, recv_sems):
    n = len(sums)
    direct, relay = lands[:n], lands[n:]
    x, y, c = _coords()
    xn, yn = (1 - x, y, c), (x, 1 - y, c)
    qx, qy, qd = 2 * (1 - x) + y, 2 * x + (1 - y), 2 * (1 - x) + (1 - y)
    cps = []
    for k in range(n):
        h = sums[k].shape[1] // 2
        a, b = pl.ds(0, h), pl.ds(h, h)
        moves = ((sums[k].at[qx, a], direct[k].at[0], xn), (sums[k].at[qy, b], direct[k].at[1], yn),
                 (sums[k].at[qd, a], relay[k].at[0], xn), (sums[k].at[qd, b], relay[k].at[1], yn))
        for s, (src, dst, to) in enumerate(moves):
            cps.append(pltpu.make_async_remote_copy(
                src_ref=src, dst_ref=dst, send_sem=send_sems.at[4 * k + s], recv_sem=recv_sems.at[4 * k + s],
                device_id=to, device_id_type=MESH))
    return cps


def _rs_step2_copies(relayed, lands, send_sems, recv_sems, k0=0):
    x, y, c = _coords()
    cps = []
    for k in range(len(relayed)):
        for s, to in enumerate(((1 - x, y, c), (x, 1 - y, c))):
            cps.append(pltpu.make_async_remote_copy(
                src_ref=relayed[k].at[s], dst_ref=lands[k].at[s], send_sem=send_sems.at[2 * (k0 + k) + s],
                recv_sem=recv_sems.at[2 * (k0 + k) + s], device_id=to, device_id_type=MESH))
    return cps


def _relay_sum(sums, relay, qxy, *, name):
    n = len(sums)

    def body(q_ref, *refs):
        for k in range(n):
            refs[2 * n + k][...] = (refs[k][...].astype(F32) + refs[n + k][...].astype(F32)).astype(refs[2 * n + k].dtype)

    half = lambda s: (1, s.shape[1] // 2) + s.shape[2:]
    return pl.pallas_call(
        body, name=name,
        grid_spec=pltpu.PrefetchScalarGridSpec(
            num_scalar_prefetch=1, grid=(2,),
            in_specs=[pl.BlockSpec(half(s), lambda t, q_ref: (q_ref[t], 1 - t, 0)) for s in sums]
            + [pl.BlockSpec(half(s), lambda t, q_ref: (1 - t, 0, 0)) for s in sums],
            out_specs=[pl.BlockSpec(half(s), lambda t, q_ref: (t, 0, 0)) for s in sums]),
        out_shape=[jax.ShapeDtypeStruct((2,) + half(s)[1:], s.dtype) for s in sums],
        compiler_params=_params(("arbitrary",)),
    )(qxy, *sums, *relay)


def _split_start(copies, srcs, lands, n_sems, after, *, name):
    ns, nl = len(srcs), len(lands)

    def body(*refs):
        for cp in copies(refs[:ns], refs[ns:ns + nl], refs[ns + nl + len(after)], refs[ns + nl + len(after) + 1]):
            cp.start()
        refs[-1][...] = jnp.zeros_like(refs[-1])

    bufs = [_in_hbm(a) for a in list(srcs) + list(lands)]
    out = pl.pallas_call(
        body, name=name,
        out_shape=(pltpu.SemaphoreType.DMA((n_sems,)), pltpu.SemaphoreType.DMA((n_sems,)),
                   *[pltpu.HBM(a.shape, a.dtype) for a in bufs], jax.ShapeDtypeStruct((8, 128), F32)),
        in_specs=[HBM_SPEC] * len(bufs) + [pl.BlockSpec(memory_space=pl.ANY)] * len(after),
        out_specs=(SEM_SPEC, SEM_SPEC, *[HBM_SPEC] * len(bufs), pl.BlockSpec(memory_space=pltpu.VMEM)),
        input_output_aliases={i: 2 + i for i in range(len(bufs))},
        compiler_params=pltpu.CompilerParams(has_side_effects=DATAFLOW),
    )(*bufs, *after)
    return out[0], out[1], list(out[2:2 + ns]), list(out[2 + ns:2 + ns + nl]), out[-1]


def _split_wait(copies, send_sems, recv_sems, srcs, lands, after, *, name):
    ns, nl = len(srcs), len(lands)

    def body(*refs):
        for cp in copies(refs[:ns], refs[ns:ns + nl], refs[ns + nl], refs[ns + nl + 1]):
            cp.wait_send()
            cp.wait_recv()

    out = pl.pallas_call(
        body, name=name,
        out_shape=[pltpu.HBM(a.shape, a.dtype) for a in list(srcs) + list(lands)],
        in_specs=[HBM_SPEC] * (ns + nl) + [SEM_SPEC, SEM_SPEC] + [pl.BlockSpec(memory_space=pl.ANY)] * len(after),
        out_specs=[HBM_SPEC] * (ns + nl),
        input_output_aliases={i: i for i in range(ns + nl)},
        compiler_params=pltpu.CompilerParams(has_side_effects=DATAFLOW),
    )(*srcs, *lands, send_sems, recv_sems, *after)
    return list(out[:ns]), list(out[ns:])


ADAM_C1 = 1.0 / (1.0 - ADAM_B1 ** ADAM_STEP)
ADAM_C2 = 1.0 / (1.0 - ADAM_B2 ** ADAM_STEP)


def _adam_math(w, g, m, v):
    m2 = ADAM_B1 * m + (1.0 - ADAM_B1) * g
    v2 = ADAM_B2 * v + (1.0 - ADAM_B2) * (g * g)
    return -ADAM_LR * ((m2 * ADAM_C1) / (jnp.sqrt(v2 * ADAM_C2) + ADAM_EPS) + ADAM_WD * w), m2, v2


def _adamw(w, g, m, v, *, name, after=()):
    R, C = w.shape
    tr = R if R <= 512 else 256

    def body(w_ref, g_ref, m_ref, v_ref, *rest):
        d_ref, nm_ref, nv_ref = rest[len(after):]
        d_ref[...], nm_ref[...], nv_ref[...] = _adam_math(w_ref[...], g_ref[...], m_ref[...], v_ref[...])

    blk = pl.BlockSpec((tr, C), lambda i: (i, 0))
    return pl.pallas_call(
        body, name=name, grid=(R // tr,), in_specs=[blk] * 4 + [pl.BlockSpec(memory_space=pl.ANY)] * len(after),
        out_specs=[blk] * 3, out_shape=[jax.ShapeDtypeStruct((R, C), F32)] * 3,
        compiler_params=_params(("parallel",)),
    )(w, g, m, v, *after)


def _adamw_rs2(wmv, cs, direct, second, qidx, *, name):
    n = len(wmv)
    r, cc = wmv[0][0].shape
    h = r // 2

    def body(q_ref, *refs):
        ins, outs = refs[:6 * n], refs[6 * n:]
        for k in range(n):
            w_ref, m_ref, v_ref, c_ref, d1_ref, d2_ref = ins[6 * k:6 * k + 6]
            g_ref, d_ref, nm_ref, nv_ref = outs[4 * k:4 * k + 4]
            g = (c_ref[0].astype(F32) + d1_ref[0].astype(F32)) + d2_ref[0].astype(F32)
            g_ref[...] = g
            d_ref[...], nm_ref[...], nv_ref[...] = _adam_math(w_ref[...], g, m_ref[...], v_ref[...])

    blk = pl.BlockSpec((h, cc), lambda i, q_ref: (i, 0))
    one = [blk, blk, blk, pl.BlockSpec((1, h, cc), lambda i, q_ref: (q_ref[0], i, 0)),
           pl.BlockSpec((1, h, cc), lambda i, q_ref: (i, 0, 0)),
           pl.BlockSpec((1, h, cc), lambda i, q_ref: (1 - i, 0, 0))]
    out = pl.pallas_call(
        body, name=name,
        grid_spec=pltpu.PrefetchScalarGridSpec(num_scalar_prefetch=1, grid=(2,), in_specs=one * n,
                                               out_specs=[blk] * (4 * n)),
        out_shape=[jax.ShapeDtypeStruct((r, cc), F32)] * (4 * n),
        compiler_params=_params(("arbitrary",)),
    )(qidx, *[a for (w, m, v), c, d1, d2 in zip(wmv, cs, direct, second) for a in (w, m, v, c, d1, d2)])
    return [tuple(out[4 * k:4 * k + 4]) for k in range(n)]


def _adamw_rs(wmv, cs, rcv, qidx, *, name):
    n = len(wmv)
    shapes = [w.shape for w, _, _ in wmv]
    n_rcv = rcv[0].shape[0]
    halved = len(set(shapes)) == 1 and shapes[0][0] % 32 == 0 and shapes[0][0] > 128
    tiles = 2 if halved else 1

    def body(q_ref, *refs):
        ins, outs = refs[:5 * n], refs[5 * n:]
        for k in range(n):
            w_ref, m_ref, v_ref, c_ref, r_ref = ins[5 * k:5 * k + 5]
            g_ref, d_ref, nm_ref, nv_ref = outs[4 * k:4 * k + 4]
            g = c_ref[0].astype(F32)
            for j in range(n_rcv):
                g = g + r_ref[j].astype(F32)
            g_ref[...] = g
            d_ref[...], nm_ref[...], nv_ref[...] = _adam_math(w_ref[...], g, m_ref[...], v_ref[...])

    in_specs, out_specs = [], []
    for r, cc in shapes:
        blk = pl.BlockSpec((r // tiles, cc), lambda i, q_ref: (i, 0))
        in_specs += [blk, blk, blk, pl.BlockSpec((1, r // tiles, cc), lambda i, q_ref: (q_ref[0], i, 0)),
                     pl.BlockSpec((n_rcv, r // tiles, cc), lambda i, q_ref: (0, i, 0))]
        out_specs += [blk] * 4
    out = pl.pallas_call(
        body, name=name,
        grid_spec=pltpu.PrefetchScalarGridSpec(num_scalar_prefetch=1, grid=(tiles,), in_specs=in_specs,
                                               out_specs=out_specs),
        out_shape=[jax.ShapeDtypeStruct(s, F32) for s in shapes for _ in range(4)],
        compiler_params=_params(("arbitrary",)),
    )(qidx, *[a for (w, m, v), c, rc in zip(wmv, cs, rcv) for a in (w, m, v, c, rc)])
    return [tuple(out[4 * k:4 * k + 4]) for k in range(n)]


SMALL_PARAMS = ("norm_ffn1", "norm_mix", "norm_ffn2", "norm_final", "q_norm", "kv_norm", "sinks", "rel_bias", "b_mod")


def _adamw_small(gvec, wmv):
    shapes = [wmv[3 * i].shape for i in range(len(SMALL_PARAMS))]

    def body(*refs):
        g_all = refs[0]
        ins = refs[1:1 + 3 * len(SMALL_PARAMS)]
        outs = refs[1 + 3 * len(SMALL_PARAMS):]
        off = N_MODVEC
        for i, name in enumerate(SMALL_PARAMS):
            g_ref, d_ref, nm_ref, nv_ref = outs[4 * i:4 * i + 4]
            w_ref, m_ref, v_ref = ins[3 * i:3 * i + 3]
            start = 0 if name == "b_mod" else off
            rows, width = shapes[i]
            g = jnp.concatenate([g_all[:, start + width * r:start + width * (r + 1)] for r in range(rows)], axis=0)
            g_ref[...] = g
            d_ref[...], nm_ref[...], nv_ref[...] = _adam_math(w_ref[...], g, m_ref[...], v_ref[...])
            if name != "b_mod":
                off += dict(SMALL_LAYOUT)[name]

    vm = pl.BlockSpec(memory_space=pltpu.VMEM)
    n_out = 4 * len(SMALL_PARAMS)
    out = pl.pallas_call(
        body, name="adamw_small", in_specs=[vm] * (1 + len(wmv)), out_specs=[vm] * n_out,
        out_shape=[jax.ShapeDtypeStruct(shapes[i // 4], F32) for i in range(n_out)],
        compiler_params=_params(),
    )(gvec, *wmv)
    return {name: out[4 * i:4 * i + 4] for i, name in enumerate(SMALL_PARAMS)}


TRANSPOSED = ("g1T", "u1T", "g3T", "u3T", "w_inT", "w_uqT")


def kernel(x, c, w_mod, b_mod, norm_ffn1, ffn1_gate, ffn1_up, ffn1_down, norm_mix, w_in, q_norm, kv_norm, w_uq, w_ukv, sinks, w_o, norm_ffn2, ffn2_gate, ffn2_up, ffn2_down, rel_bias, norm_final, loss_target, m_w_mod, m_b_mod, m_norm_ffn1, m_ffn1_gate, m_ffn1_up, m_ffn1_down, m_norm_mix, m_w_in, m_q_norm, m_kv_norm, m_w_uq, m_w_ukv, m_sinks, m_w_o, m_norm_ffn2, m_ffn2_gate, m_ffn2_up, m_ffn2_down, m_rel_bias, m_norm_final, v_w_mod, v_b_mod, v_norm_ffn1, v_ffn1_gate, v_ffn1_up, v_ffn1_down, v_norm_mix, v_w_in, v_q_norm, v_kv_norm, v_w_uq, v_w_ukv, v_sinks, v_w_o, v_norm_ffn2, v_ffn2_gate, v_ffn2_up, v_ffn2_down, v_rel_bias, v_norm_final):
    mx, my, mc = _coords()
    cidx = jnp.reshape(mc, (1,)).astype(jnp.int32)
    qidx = jnp.reshape(2 * mx + my, (1,)).astype(jnp.int32)
    WM = w_mod.shape[2]

    c_tile = jnp.pad(c, ((0, 7), (0, 0)))
    b_mod3 = jnp.pad(b_mod.reshape(N_DEV, 1, WM), ((0, 0), (0, 7), (0, 0)))
    mod3, ca = _mod_fwd(c_tile, w_mod[0], b_mod3)
    mod9 = mod3[:, 0, :].reshape(N_MOD, D)

    shards = {"g1T": ffn1_gate[0].T.astype(BF16), "u1T": ffn1_up[0].T.astype(BF16), "d1": ffn1_down[0].astype(BF16),
              "g3T": ffn2_gate[0].T.astype(BF16), "u3T": ffn2_up[0].T.astype(BF16), "d3": ffn2_down[0].astype(BF16),
              "w_inT": w_in[0].T, "w_uqT": w_uq[0].T.astype(BF16), "w_ukv": w_ukv[0].astype(BF16),
              "w_o": w_o[0].astype(BF16)}
    me = 4 * mx + 2 * my + mc
    groups = {"ffn1": ("g1T", "u1T", "d1"), "mixer": ("w_inT", "w_uqT", "w_ukv", "w_o"), "ffn2": ("g3T", "u3T", "d3")}
    arriving = {}

    def as_weights(group, gathered):
        return {k: g if k == "w_ukv" else g.reshape(N_DEV * g.shape[1], g.shape[2])
                for k, g in zip(groups[group], gathered)}

    later = groups["mixer"] + groups["ffn2"]
    place = {"mixer": 0, "ffn2": len(groups["mixer"])}

    def start_gather(token):
        lands = []
        for k in later:
            sh = shards[k] + token[0, 0].astype(shards[k].dtype)
            lands.append(lax.dynamic_update_slice(lax.empty((N_DEV,) + sh.shape, sh.dtype), sh[None], (me, 0, 0)))
        batches = [range(k0, k0 + len(groups[group])) for group, k0 in place.items()]
        send, recv, lands, started = _gather_start(lands, name="gather_start", batches=batches)
        for group, k0 in place.items():
            arriving[group] = (send, recv, lands[k0:k0 + len(groups[group])])
        return started

    def fetch(group, after, vecs):
        if group == "ffn1":
            *gathered, token = _wgather([shards[k] + ca[1, 0].astype(shards[k].dtype) for k in groups["ffn1"]])
            return as_weights("ffn1", gathered), vecs + start_gather(token)[0:1, 0:1]

        def pass_on(group, after):
            send, recv, lands = arriving[group]
            lands, token = _gather_pass(send, recv, lands, after, name="gather_landed_" + group, stage="landed",
                                        k0=place[group])
            lands, token = _gather_pass(send, recv, lands, [token], name="gather_onward_" + group, stage="onward",
                                        k0=place[group])
            arriving[group] = (send, recv, lands)
            return token

        if group == "ffn2_on_its_way":
            return None, vecs + pass_on("ffn2", after)[0:1, 0:1]
        if group == "mixer":
            after = [pass_on("mixer", after)]
        send, recv, lands = arriving[group]
        return as_weights(group, _gather_end(send, recv, lands, after, name="gather_end_" + group,
                                             k0=place[group])), vecs

    norms ={"ffn1": norm_ffn1, "mix": norm_mix, "ffn2": norm_ffn2, "final": norm_final.reshape(1, D)}
    in_flight = {}

    def on_grads(group, g, after, vecs):
        if group != "ffn1":
            if g is None:
                return vecs
            names = list(g)
            by_dest = [g[k] if k == "w_ukv" else g[k].reshape((N_DEV, g[k].shape[0] // N_DEV) + g[k].shape[1:])
                       for k in names]
            lands = [lax.empty((N_DEV - 1,) + a.shape[1:], a.dtype) for a in by_dest]
            send, recv, by_dest, lands, token = _split_start(_direct_copies, by_dest, lands, 7 * len(names), after,
                                                             name="rs_start_" + group)
            in_flight[group] = (names, send, recv, by_dest, lands, token)
            return vecs + token[0:1, 0:1]
        names = list(g)
        by_dest = [g[k].reshape((N_DEV, g[k].shape[0] // N_DEV) + g[k].shape[1:]) for k in names]
        lands = [lax.empty((4,) + a.shape[1:], a.dtype) for a in by_dest]
        send, recv, by_dest, lands, token = _split_start(_d2d_copies, by_dest, lands, 4 * len(names), after,
                                                         name="rs_d2d_start_" + group)
        finish("mixer", [token])
        by_dest, from_sib = _split_wait(_d2d_copies, send, recv, by_dest, lands, [done[-1]], name="rs_d2d_wait_" + group)
        sums = _chipsum(by_dest, from_sib, cidx, name="chipsum_" + group)
        halves = lambda: [lax.empty((2, s.shape[1] // 2) + s.shape[2:], s.dtype) for s in sums]
        send, recv, sums, lands, token = _split_start(_rs_step1_copies, sums, halves() + halves(), 4 * len(names), [],
                                                      name="rs_ici_start_" + group)
        in_flight[group] = (names, send, recv, sums, lands, token)
        return vecs + token[0:1, 0:1]

    owners = {"g1T": ("ffn1_gate", ffn1_gate, m_ffn1_gate, v_ffn1_gate), "u1T": ("ffn1_up", ffn1_up, m_ffn1_up, v_ffn1_up),
              "d1": ("ffn1_down", ffn1_down, m_ffn1_down, v_ffn1_down),
              "g3T": ("ffn2_gate", ffn2_gate, m_ffn2_gate, v_ffn2_gate), "u3T": ("ffn2_up", ffn2_up, m_ffn2_up, v_ffn2_up),
              "d3": ("ffn2_down", ffn2_down, m_ffn2_down, v_ffn2_down),
              "w_inT": ("w_in", w_in, m_w_in, v_w_in), "w_uqT": ("w_uq", w_uq, m_w_uq, v_w_uq),
              "w_ukv": ("w_ukv", w_ukv, m_w_ukv, v_w_ukv), "w_o": ("w_o", w_o, m_w_o, v_w_o)}
    res, done = {}, []

    there = lambda k, a: a[0].T if k in TRANSPOSED else a[0]
    back = lambda k, a: a.T[None] if k in TRANSPOSED else a[None]

    def record(names, outs):
        for k, out in zip(names, outs):
            done.append(out[3])
            res[owners[k][0]] = tuple(back(k, a) for a in out)

    def finish(group, after):
        names, send, recv, sums, lands, _ = in_flight[group]
        wmv = [tuple(there(k, a) for a in owners[k][1:]) for k in names]
        own = jnp.reshape(me, (1,)).astype(jnp.int32)
        sums, lands = _split_wait(_direct_copies, send, recv, sums, lands, after, name="rs_wait_" + group)
        record(names, _adamw_rs(wmv, sums, lands, own, name="adamw_" + group))

    _, grad_x, _, vec = _local_step(
        x[0], loss_target[0], mod9, norms, sinks, rel_bias, q_norm, kv_norm, fetch, on_grads=on_grads)

    names, send, recv, sums, lands, step1_started = in_flight["ffn1"]
    vec = vec.reshape(1, 1, N_VEC)
    allvec = lax.dynamic_update_slice(lax.empty((N_DEV, 1, N_VEC), F32), vec, (me, 0, 0))
    vsend, vrecv, _, (allvec,), vec_started = _split_start(_vec_copies, [], [allvec], N_DEV - 1, [step1_started],
                                                           name="vec_start")
    finish("ffn2", [vec_started])
    n = len(names)
    sums, lands = _split_wait(_rs_step1_copies, send, recv, sums, lands, [done[-1]], name="rs_ici_wait_ffn1")
    direct, relay = lands[:n], lands[n:]
    qxy = jnp.stack([2 * (1 - mx) + my, 2 * mx + (1 - my)]).astype(jnp.int32)
    relayed = _relay_sum(sums, relay, qxy, name="relay_sum_ffn1")
    second = [lax.empty(a.shape, a.dtype) for a in relayed]
    send, recv, relayed, second, step2_started = _split_start(_rs_step2_copies, relayed, second, 2 * n, [],
                                                              name="rs_ici_start2_ffn1")

    _, (allvec,) = _split_wait(_vec_copies, vsend, vrecv, [], [allvec], [step2_started], name="vec_wait")
    g_wmod, gvec = _mod_bwd(allvec, ca, jnp.reshape(me, (1,)).astype(jnp.int32))
    loss = gvec[0, N_MODVEC + LOSS_SLOT]
    small_in = {"norm_ffn1": (norm_ffn1, m_norm_ffn1, v_norm_ffn1), "norm_mix": (norm_mix, m_norm_mix, v_norm_mix),
                "norm_ffn2": (norm_ffn2, m_norm_ffn2, v_norm_ffn2), "norm_final": (norm_final, m_norm_final, v_norm_final),
                "q_norm": (q_norm, m_q_norm, v_q_norm), "kv_norm": (kv_norm, m_kv_norm, v_kv_norm),
                "sinks": (sinks, m_sinks, v_sinks), "rel_bias": (rel_bias, m_rel_bias, v_rel_bias),
                "b_mod": (b_mod, m_b_mod, v_b_mod)}
    as_row = lambda k, a: a.T if k == "rel_bias" else a.reshape(1, -1)
    from_row = lambda k, a: a.T if k == "rel_bias" else a.reshape(small_in[k][0].shape)
    small_out = _adamw_small(gvec, [as_row(k, a) for k in SMALL_PARAMS for a in small_in[k]])
    for k in SMALL_PARAMS:
        res[k] = tuple(from_row(k, a) for a in small_out[k])

    out = _adamw(w_mod[0], g_wmod, m_w_mod[0], v_w_mod[0], name="adamw_w_mod")
    res["w_mod"] = tuple(a[None] for a in (g_wmod,) + tuple(out))

    wmv = [tuple(there(k, a) for a in owners[k][1:]) for k in names]
    after = done + [out[2]] + [a for k in SMALL_PARAMS for a in res[k]]
    outs = []
    for i, k in enumerate(names):
        _, (sec,) = _split_wait(functools.partial(_rs_step2_copies, k0=i), send, recv, [relayed[i]], [second[i]],
                                after, name="rs_ici_wait2_" + k)
        outs.append(_adamw_rs2([wmv[i]], [sums[i]], [direct[i]], [sec], qidx, name="adamw_" + owners[k][0])[0])
        after = [outs[-1][3]]
    record(names, outs)

    order = ("w_mod", "b_mod", "norm_ffn1", "ffn1_gate", "ffn1_up", "ffn1_down", "norm_mix", "w_in", "q_norm",
             "kv_norm", "w_uq", "w_ukv", "sinks", "w_o", "norm_ffn2", "ffn2_gate", "ffn2_up", "ffn2_down",
             "rel_bias", "norm_final")
    return (loss, grad_x[None]) + tuple(res[nm][kind] for kind in range(4) for nm in order)
```

```python
import functools
import math

import numpy as np
import jax
import jax.numpy as jnp
from jax import lax
from jax.experimental import pallas as pl
from jax.experimental.pallas import tpu as pltpu

F32 = jnp.float32
BF16 = jnp.bfloat16
MESH = pl.DeviceIdType.MESH

N_DEV = 8
D = 1024
D_FF = 2816
EPS = 1e-6
N_MOD = 9
SWA_HEADS = 8
SWA_DH = 64
WINDOW = 128
MLA_HEADS = 4
MLA_NOPE = 128
MLA_ROPE = 64
MLA_V = 128
MLA_QR = 256
MLA_KVR = 128
ROPE_THETA = 10000.0
NUM_BUCKETS = 32
D_IN = 1216
D_IN_PAD = 1280
SWA_SCALE = SWA_DH ** -0.5
MLA_SCALE = (MLA_NOPE + MLA_ROPE) ** -0.5

ADAM_LR = 0.001
ADAM_B1 = 0.9
ADAM_B2 = 0.999
ADAM_EPS = 1e-08
ADAM_WD = 0.01
ADAM_STEP = 10

V7X_VMEM_LIMIT = 56 * 1024 * 1024
ROW_TILE = 512

NT_DIMS = (((1,), (1,)), ((), ()))
TN_DIMS = (((0,), (0,)), ((), ()))


def _dot(a, b):
    return jnp.dot(a, b, preferred_element_type=F32)


def _dot_nt(a, b):
    return lax.dot_general(a, b, NT_DIMS, preferred_element_type=F32)


def _dot_tn(a, b):
    return lax.dot_general(a, b, TN_DIMS, preferred_element_type=F32)


def _params(sem=None):
    return pltpu.CompilerParams(dimension_semantics=sem, vmem_limit_bytes=V7X_VMEM_LIMIT)


def _rstd(x):
    return lax.rsqrt(jnp.mean(x * x, axis=-1, keepdims=True) + EPS)


def _rms_bwd(dy, xhat, r):
    return r * (dy - xhat * jnp.mean(dy * xhat, axis=-1, keepdims=True))


def _sigmoid(a):
    return 1.0 / (1.0 + jnp.exp(-a))


def _ffn_fwd(x, vecs, wgT, wuT, wd, *, name, tm=256, tf=D_FF):
    S, F = x.shape[0], wd.shape[0]
    tm = min(tm, S)
    ni, nj = S // tm, F // tf

    def body(x_ref, vec_ref, wg_ref, wu_ref, wd_ref, xo_ref, h_ref, a_ref, b_ref, f_ref, acc_ref):
        j = pl.program_id(1)

        @pl.when(j == 0)
        def _():
            xv = x_ref[...]
            hn = xv * _rstd(xv) * vec_ref[0:1, :]
            h_ref[...] = (hn * (1.0 + vec_ref[2:3, :]) + vec_ref[1:2, :]).astype(BF16)

        h = h_ref[...]
        a = _dot_nt(h, wg_ref[...])
        b = _dot_nt(h, wu_ref[...])
        a_ref[...] = a.astype(BF16)
        b_ref[...] = b.astype(BF16)
        part = _dot((a * _sigmoid(a) * b).astype(BF16), wd_ref[...])

        def finish(f):
            f_ref[...] = f
            xo_ref[...] = x_ref[...] + (0.5 * vec_ref[3:4, :]) * f

        if nj == 1:
            finish(part)
        else:
            @pl.when(j == 0)
            def _():
                acc_ref[...] = part

            @pl.when((j > 0) & (j < nj - 1))
            def _():
                acc_ref[...] += part

            @pl.when(j == nj - 1)
            def _():
                finish(acc_ref[...] + part)

    row = pl.BlockSpec((tm, D), lambda i, j: (i, 0))
    wspec = pl.BlockSpec((tf, D), lambda i, j: (j, 0), pipeline_mode=pl.Buffered(1) if nj == 1 else None)
    act = pl.BlockSpec((tm, tf), lambda i, j: (i, j))
    return pl.pallas_call(
        body, name=name, grid=(ni, nj),
        in_specs=[row, pl.BlockSpec((8, D), lambda i, j: (0, 0)), wspec, wspec, wspec],
        out_specs=[row, row, act, act, row],
        out_shape=[jax.ShapeDtypeStruct((S, D), F32), jax.ShapeDtypeStruct((S, D), BF16),
                   jax.ShapeDtypeStruct((S, F), BF16), jax.ShapeDtypeStruct((S, F), BF16),
                   jax.ShapeDtypeStruct((S, D), F32)],
        scratch_shapes=[pltpu.VMEM((tm, D) if nj > 1 else (8, 128), F32)],
        compiler_params=_params(("parallel", "arbitrary")),
    )(x, vecs, wgT, wuT, wd)


def _ffn_bwd_main(h, df, a, b, wgT, wuT, wd, *, name, after=(), tm=2048, tf=256):
    S = h.shape[0]
    tm = min(tm, S)
    ni, nj = S // tm, D_FF // tf

    def body(h_hbm, df_hbm, a_ref, b_ref, wg_ref, wu_ref, wd_ref, *rest):
        gg_ref, gu_ref, gd_ref, dh_hbm, h_v, df_v, dh_v, gg_acc, gu_acc, gd_acc, sem = rest[len(after):]
        j = pl.program_id(0)
        i = pl.program_id(1)

        @pl.when((j == 0) & (i == 0))
        def _():
            c1 = pltpu.make_async_copy(h_hbm, h_v, sem.at[0])
            c2 = pltpu.make_async_copy(df_hbm, df_v, sem.at[1])
            c1.start()
            c2.start()
            c1.wait()
            c2.wait()

        @pl.when(i == 0)
        def _():
            gg_acc[...] = jnp.zeros_like(gg_acc)
            gu_acc[...] = jnp.zeros_like(gu_acc)
            gd_acc[...] = jnp.zeros_like(gd_acc)

        rows = pl.ds(pl.multiple_of(i * tm, tm), tm)
        hi = h_v[rows, :]
        dfi = df_v[rows, :]
        av = a_ref[...].astype(F32)
        bv = b_ref[...].astype(F32)
        sg = _sigmoid(av)
        sa = av * sg
        hsw = (sa * bv).astype(BF16)
        dhsw = _dot_nt(dfi, wd_ref[...])
        da = (dhsw * bv * (sg * (1.0 + av * (1.0 - sg)))).astype(BF16)
        db = (dhsw * sa).astype(BF16)
        gd_acc[...] += _dot_tn(hsw, dfi)
        gg_acc[...] += _dot_tn(da, hi)
        gu_acc[...] += _dot_tn(db, hi)
        dh = _dot(da, wg_ref[...]) + _dot(db, wu_ref[...])

        @pl.when(j == 0)
        def _():
            dh_v[rows, :] = dh

        @pl.when(j > 0)
        def _():
            dh_v[rows, :] += dh

        @pl.when(i == ni - 1)
        def _():
            gg_ref[...] = gg_acc[...].astype(BF16)
            gu_ref[...] = gu_acc[...].astype(BF16)
            gd_ref[...] = gd_acc[...].astype(BF16)

        @pl.when((j == nj - 1) & (i == ni - 1))
        def _():
            c3 = pltpu.make_async_copy(dh_v, dh_hbm, sem.at[2])
            c3.start()
            c3.wait()

    anyspec = pl.BlockSpec(memory_space=pl.ANY)
    wspec = pl.BlockSpec((tf, D), lambda j, i: (j, 0))
    act = pl.BlockSpec((tm, tf), lambda j, i: (i, j))
    return pl.pallas_call(
        body, name=name, grid=(nj, ni),
        in_specs=[anyspec, anyspec, act, act, wspec, wspec, wspec] + [anyspec] * len(after),
        out_specs=[wspec, wspec, wspec, anyspec],
        out_shape=[jax.ShapeDtypeStruct((D_FF, D), BF16)] * 3 + [jax.ShapeDtypeStruct((S, D), F32)],
        scratch_shapes=[pltpu.VMEM((S, D), BF16), pltpu.VMEM((S, D), BF16), pltpu.VMEM((S, D), F32),
                        pltpu.VMEM((tf, D), F32), pltpu.VMEM((tf, D), F32), pltpu.VMEM((tf, D), F32),
                        pltpu.SemaphoreType.DMA((3,))],
        compiler_params=_params(("arbitrary", "arbitrary")),
    )(h, df, a, b, wgT, wuT, wd, *after)


def _ffn_out_bwd(dx, f, gate, df_ref, part_ref):
    df_ref[...] = ((0.5 * gate) * dx).astype(BF16)
    part_ref[3:4, :] += 0.5 * jnp.sum(dx * f, axis=0, keepdims=True)


def _norm_bwd(dh, x, dxo, vecs, *, name, below=None, tm=ROW_TILE):
    S = x.shape[0]
    tm = min(tm, S)

    def body(dh_ref, x_ref, dxo_ref, vec_ref, *rest):
        dx_ref, part_ref = rest[-2 if below is None else -3], rest[-1 if below is None else -2]

        @pl.when(pl.program_id(0) == 0)
        def _():
            part_ref[...] = jnp.zeros_like(part_ref)

        dh = dh_ref[...]
        xv = x_ref[...]
        r = _rstd(xv)
        xhat = xv * r
        w = vec_ref[0:1, :]
        xn = xhat * w
        dxn = dh * (1.0 + vec_ref[2:3, :])
        part_ref[0:1, :] += jnp.sum(dxn * xhat, axis=0, keepdims=True)
        part_ref[1:2, :] += jnp.sum(dh, axis=0, keepdims=True)
        part_ref[2:3, :] += jnp.sum(dh * xn, axis=0, keepdims=True)
        dx = dxo_ref[...] + _rms_bwd(dxn * w, xhat, r)
        dx_ref[...] = dx
        if below is not None:
            _ffn_out_bwd(dx, rest[0][...], rest[1][3:4, :], rest[-1], part_ref)

    row = pl.BlockSpec((tm, D), lambda i: (i, 0))
    vec = pl.BlockSpec((8, D), lambda i: (0, 0))
    extra = [] if below is None else [row, vec]
    return pl.pallas_call(
        body, name=name, grid=(S // tm,), in_specs=[row, row, row, vec] + extra,
        out_specs=[row, vec] + ([] if below is None else [row]),
        out_shape=[jax.ShapeDtypeStruct((S, D), F32), jax.ShapeDtypeStruct((8, D), F32)]
        + ([] if below is None else [jax.ShapeDtypeStruct((S, D), BF16)]),
        compiler_params=_params(("arbitrary",)),
    )(dh, x, dxo, vecs, *([] if below is None else below))


def _head(x, tgt, nf, f, vecs, *, tm=ROW_TILE):
    S = x.shape[0]
    tm = min(tm, S)

    def body(x_ref, t_ref, nf_ref, f_ref, vec_ref, dx_ref, part_ref, df_ref):
        @pl.when(pl.program_id(0) == 0)
        def _():
            part_ref[...] = jnp.zeros_like(part_ref)

        xv = x_ref[...]
        r = _rstd(xv)
        xhat = xv * r
        w = nf_ref[...]
        e = xhat * w - t_ref[...]
        dy = e * (1.0 / D)
        part_ref[0:1, :] += jnp.sum(dy * xhat, axis=0, keepdims=True)
        part_ref[1:2, :] += jnp.sum(e * e) * (0.5 / D)
        dx = _rms_bwd(dy * w, xhat, r)
        dx_ref[...] = dx
        _ffn_out_bwd(dx, f_ref[...], vec_ref[3:4, :], df_ref, part_ref)

    row = pl.BlockSpec((tm, D), lambda i: (i, 0))
    vec = pl.BlockSpec((8, D), lambda i: (0, 0))
    return pl.pallas_call(
        body, name="head", grid=(S // tm,),
        in_specs=[row, row, pl.BlockSpec((1, D), lambda i: (0, 0)), row, vec],
        out_specs=[row, vec, row],
        out_shape=[jax.ShapeDtypeStruct((S, D), F32), jax.ShapeDtypeStruct((8, D), F32),
                   jax.ShapeDtypeStruct((S, D), BF16)],
        compiler_params=_params(("arbitrary",)),
    )(x, tgt, nf, f, vecs)


def _mix_in_fwd(x, vecs, w_inT, *, tm=ROW_TILE):
    S = x.shape[0]
    tm = min(tm, S)

    def body(x_ref, vec_ref, w_ref, h_ref, p_ref, wb_ref):
        @pl.when(pl.program_id(0) == 0)
        def _():
            wb_ref[0:D_IN, :] = w_ref[...].astype(BF16)
            wb_ref[D_IN:D_IN_PAD, :] = jnp.zeros((D_IN_PAD - D_IN, D), BF16)

        xv = x_ref[...]
        hn = xv * _rstd(xv) * vec_ref[0:1, :]
        h = (hn * (1.0 + vec_ref[2:3, :]) + vec_ref[1:2, :]).astype(BF16)
        h_ref[...] = h
        p_ref[...] = _dot_nt(h, wb_ref[...])

    row = pl.BlockSpec((tm, D), lambda i: (i, 0))
    return pl.pallas_call(
        body, name="mix_in_fwd", grid=(S // tm,),
        in_specs=[row, pl.BlockSpec((8, D), lambda i: (0, 0)),
                  pl.BlockSpec((D_IN, D), lambda i: (0, 0), pipeline_mode=pl.Buffered(1))],
        out_specs=[row, pl.BlockSpec((tm, D_IN_PAD), lambda i: (i, 0)), pl.BlockSpec((D_IN_PAD, D), lambda i: (0, 0))],
        out_shape=[jax.ShapeDtypeStruct((S, D), BF16), jax.ShapeDtypeStruct((S, D_IN_PAD), F32),
                   jax.ShapeDtypeStruct((D_IN_PAD, D), BF16)],
        compiler_params=_params(("arbitrary",)),
    )(x, vecs, w_inT)


def _bucket_table():
    qi = np.arange(WINDOW)[:, None]
    kj = np.arange(2 * WINDOW)[None, :]
    dist = qi + WINDOW - kj
    max_exact = NUM_BUCKETS // 2
    n = np.maximum(dist, 0)
    nf = np.maximum(n, 1).astype(np.float32)
    large = max_exact + (np.log(nf / np.float32(max_exact)) / np.float32(math.log(WINDOW / max_exact))
                         * np.float32(NUM_BUCKETS - max_exact)).astype(np.int32)
    large = np.minimum(large, NUM_BUCKETS - 1)
    return np.where(n < max_exact, n, large).astype(np.int32)


SWA_GROUP = 4
GROUP_ROWS = SWA_GROUP * WINDOW


SWA_SUB = 2


def _swa_valid(has_prev):
    row = lax.broadcasted_iota(jnp.int32, (GROUP_ROWS, 2 * WINDOW), 0) % WINDOW
    col = lax.broadcasted_iota(jnp.int32, (GROUP_ROWS, 2 * WINDOW), 1)
    dist = row + WINDOW - col
    return (dist >= 0) & (dist < WINDOW) & ((col >= WINDOW) | has_prev)


def _swa_keys(prev_ref, cur_ref, u):
    cur = cur_ref[...]
    before = prev_ref[...] if u == 0 else cur[WINDOW * (u - 1):WINDOW * u]
    return jnp.concatenate([before, cur[WINDOW * u:WINDOW * (u + 1)]], axis=0).astype(BF16)


def _stack_heads(x, g):
    return jnp.concatenate([x[:, 64 * h:64 * h + 64] for h in range(SWA_GROUP * g, SWA_GROUP * (g + 1))], axis=0)


def _unstack_heads(x4):
    return jnp.concatenate([x4[WINDOW * a:WINDOW * (a + 1)] for a in range(SWA_GROUP)], axis=1)


def _group_sinks(sink_ref, g):
    head = lax.broadcasted_iota(jnp.int32, (GROUP_ROWS, 1), 0) // WINDOW
    out = jnp.full((GROUP_ROWS, 1), sink_ref[0, SWA_GROUP * g], F32)
    for a in range(1, SWA_GROUP):
        out = jnp.where(head == a, sink_ref[0, SWA_GROUP * g + a], out)
    return out


def _swa_probs(qh, kk, bias_h, sink, valid):
    s = _dot_nt(qh, kk) * SWA_SCALE + bias_h
    s = jnp.where(valid, s, -jnp.inf)
    m = jnp.maximum(jnp.max(s, axis=-1, keepdims=True), sink)
    p = jnp.exp(s - m)
    ps = jnp.exp(sink - m)
    inv = 1.0 / (jnp.sum(p, axis=-1, keepdims=True) + ps)
    return p * inv, ps * inv


SWA_ROWS = SWA_SUB * WINDOW


def _swa_specs():
    prev = lambda n: jnp.maximum(SWA_SUB * n - 1, 0)
    return [pl.BlockSpec((SWA_ROWS, 512), lambda n: (n, 0)),
            pl.BlockSpec((SWA_ROWS, 128), lambda n: (n, 4)),
            pl.BlockSpec((WINDOW, 128), lambda n: (prev(n), 4)),
            pl.BlockSpec((SWA_ROWS, 128), lambda n: (n, 5)),
            pl.BlockSpec((WINDOW, 128), lambda n: (prev(n), 5)),
            pl.BlockSpec((SWA_HEADS, WINDOW, 2 * WINDOW), lambda n: (0, 0, 0)),
            pl.BlockSpec(memory_space=pltpu.SMEM)]


def _swa_fwd(proj, rel_bias, bucket, sinks):
    S = proj.shape[0]

    def body(q_ref, kc_ref, kp_ref, vc_ref, vp_ref, rb_ref, sink_ref, bk_ref, o_ref, bias_ref):
        n = pl.program_id(0)

        @pl.when(n == 0)
        def _():
            bk = bk_ref[...]
            for h in range(SWA_HEADS):
                acc = jnp.zeros((WINDOW, 2 * WINDOW), F32)
                for b in range(NUM_BUCKETS):
                    acc = jnp.where(bk == b, rb_ref[b, h], acc)
                bias_ref[h] = acc

        for u in range(SWA_SUB):
            rows = slice(WINDOW * u, WINDOW * (u + 1))
            valid = _swa_valid(n > 0 if u == 0 else True)
            q = q_ref[rows, :].astype(BF16)
            kfull = _swa_keys(kp_ref, kc_ref, u)
            vfull = _swa_keys(vp_ref, vc_ref, u)
            for g in range(SWA_HEADS // SWA_GROUP):
                kk = kfull[:, 64 * g:64 * g + 64]
                vv = vfull[:, 64 * g:64 * g + 64]
                bias4 = bias_ref[SWA_GROUP * g:SWA_GROUP * (g + 1)].reshape(GROUP_ROWS, 2 * WINDOW)
                pk, _ = _swa_probs(_stack_heads(q, g), kk, bias4, _group_sinks(sink_ref, g), valid)
                o_ref[rows, 256 * g:256 * (g + 1)] = _unstack_heads(_dot(pk.astype(BF16), vv))

    specs = _swa_specs()
    whole = pl.BlockSpec((SWA_HEADS, WINDOW, 2 * WINDOW), lambda n: (0, 0, 0))
    return pl.pallas_call(
        body, name="swa_fwd", grid=(S // SWA_ROWS,),
        in_specs=specs[:5] + [pl.BlockSpec(memory_space=pltpu.SMEM), specs[6],
                              pl.BlockSpec((WINDOW, 2 * WINDOW), lambda n: (0, 0))],
        out_specs=[pl.BlockSpec((SWA_ROWS, 512), lambda n: (n, 0)), whole],
        out_shape=[jax.ShapeDtypeStruct((S, 512), F32), jax.ShapeDtypeStruct((SWA_HEADS, WINDOW, 2 * WINDOW), F32)],
        compiler_params=_params(("arbitrary",)),
    )(proj, proj, proj, proj, proj, rel_bias, sinks, bucket)


def _swa_bwd(proj, bias, sinks, o, do, bucket):
    S = proj.shape[0]
    nb = S // SWA_ROWS

    def body(q_ref, kc_ref, kp_ref, vc_ref, vp_ref, bias_ref, sink_ref, o_ref, do_ref, bk_ref,
             dq_ref, dk_ref, dv_ref, drb_ref, dsk_ref, dbias_acc):
        n = pl.program_id(0)

        @pl.when(n == 0)
        def _():
            dk_ref[...] = jnp.zeros_like(dk_ref)
            dv_ref[...] = jnp.zeros_like(dv_ref)
            dsk_ref[...] = jnp.zeros_like(dsk_ref)
            dbias_acc[...] = jnp.zeros_like(dbias_acc)
            drb_ref[...] = jnp.zeros_like(drb_ref)

        for u in range(SWA_SUB):
            rows = slice(WINDOW * u, WINDOW * (u + 1))
            blk = SWA_SUB * n + u
            valid = _swa_valid(n > 0 if u == 0 else True)
            q = q_ref[rows, :].astype(BF16)
            dov = do_ref[rows, :]
            ov = o_ref[rows, :]
            kfull = _swa_keys(kp_ref, kc_ref, u)
            vfull = _swa_keys(vp_ref, vc_ref, u)
            prow = pl.ds(pl.multiple_of(jnp.maximum(blk - 1, 0) * WINDOW, WINDOW), WINDOW)
            crow = pl.ds(pl.multiple_of(blk * WINDOW, WINDOW), WINDOW)
            for g in range(SWA_HEADS // SWA_GROUP):
                heads = slice(SWA_GROUP * g, SWA_GROUP * (g + 1))
                kk = kfull[:, 64 * g:64 * g + 64]
                vv = vfull[:, 64 * g:64 * g + 64]
                q4 = _stack_heads(q, g)
                pk, psink = _swa_probs(q4, kk, bias_ref[heads].reshape(GROUP_ROWS, 2 * WINDOW),
                                       _group_sinks(sink_ref, g), valid)
                pkb = pk.astype(BF16)
                do4 = _stack_heads(dov, g)
                dob = do4.astype(BF16)
                dp = _dot_nt(dob, vv)
                delta = jnp.sum(do4 * _stack_heads(ov, g), axis=-1, keepdims=True)
                ds = pk * (dp - delta)
                dsink = -psink * delta
                for a in range(SWA_GROUP):
                    h = SWA_GROUP * g + a
                    part = jnp.sum(dsink[WINDOW * a:WINDOW * (a + 1)], keepdims=True)
                    dsk_ref[h:h + 1, :] += jnp.broadcast_to(part, (1, 128))
                dbias_acc[heads] += ds.reshape(SWA_GROUP, WINDOW, 2 * WINDOW)
                dsb = (ds * SWA_SCALE).astype(BF16)
                dq_ref[rows, 256 * g:256 * (g + 1)] = _unstack_heads(_dot(dsb, kk))
                dkk = _dot_tn(dsb, q4)
                dvv = _dot_tn(pkb, dob)
                dk_ref[prow, 64 * g:64 * g + 64] += dkk[:WINDOW]
                dk_ref[crow, 64 * g:64 * g + 64] += dkk[WINDOW:]
                dv_ref[prow, 64 * g:64 * g + 64] += dvv[:WINDOW]
                dv_ref[crow, 64 * g:64 * g + 64] += dvv[WINDOW:]

        @pl.when(n == nb - 1)
        def _():
            bk = bk_ref[...]
            for h in range(SWA_HEADS):
                dbh = dbias_acc[h]
                for b in range(NUM_BUCKETS):
                    val = jnp.sum(jnp.where(bk == b, dbh, 0.0), keepdims=True)
                    row = h * NUM_BUCKETS + b
                    drb_ref[row:row + 1, :] = jnp.broadcast_to(val, (1, 128))

    full = lambda shape: pl.BlockSpec(shape, lambda n: tuple(0 for _ in shape))
    return pl.pallas_call(
        body, name="swa_bwd", grid=(nb,),
        in_specs=_swa_specs() + [pl.BlockSpec((SWA_ROWS, 512), lambda n: (n, 0)),
                                 pl.BlockSpec((SWA_ROWS, 512), lambda n: (n, 0)), full((WINDOW, 2 * WINDOW))],
        out_specs=[pl.BlockSpec((SWA_ROWS, 512), lambda n: (n, 0)), full((S, 128)), full((S, 128)),
                   full((NUM_BUCKETS * 8, 128)), full((8, 128))],
        out_shape=[jax.ShapeDtypeStruct((S, 512), F32), jax.ShapeDtypeStruct((S, 128), F32),
                   jax.ShapeDtypeStruct((S, 128), F32), jax.ShapeDtypeStruct((NUM_BUCKETS * 8, 128), F32),
                   jax.ShapeDtypeStruct((8, 128), F32)],
        scratch_shapes=[pltpu.VMEM((SWA_HEADS, WINDOW, 2 * WINDOW), F32)],
        compiler_params=_params(("arbitrary",)),
    )(proj, proj, proj, proj, proj, bias, sinks, o, do, bucket)


def _rope_tables(S):
    inv = np.float32(ROPE_THETA) ** (-np.arange(0, MLA_ROPE, 2, dtype=np.float32) / np.float32(MLA_ROPE))
    ang = np.arange(S, dtype=np.float32)[:, None] * inv[None, :]
    cos, sin = np.cos(ang), np.sin(ang)
    return (jnp.asarray(np.tile(np.concatenate([cos, cos], axis=1), (1, 2))),
            jnp.asarray(np.tile(np.concatenate([-sin, sin], axis=1), (1, 2))))


def _rope_wide(ref):
    t = ref[...]
    return jnp.concatenate([t, t], axis=1)


def _swap_halves(x):
    w = x.shape[-1]
    lane = lax.broadcasted_iota(jnp.int32, x.shape, x.ndim - 1)
    return jnp.where((lane % 64) < 32, pltpu.roll(x, w - 32, x.ndim - 1), pltpu.roll(x, 32, x.ndim - 1))


def _mla_pre_fwd(proj, qn_w, kvn_w, wuqT, wukv, cos, sin, *, tm=ROW_TILE):
    S = proj.shape[0]
    tm = min(tm, S)

    def body(ql_ref, kl_ref, kr_ref, qw_ref, kw_ref, wuq_ref, wukv_ref, cos_ref, sin_ref,
             qc_ref, kc_ref, vv_ref):
        ql = ql_ref[...]
        qn = (ql * _rstd(ql) * qw_ref[...]).astype(BF16)
        q = _dot_nt(qn, wuq_ref[...])
        cs, sn = _rope_wide(cos_ref), _rope_wide(sin_ref)
        qr = q[:, 512:768]
        qr = qr * cs + _swap_halves(qr) * sn
        half = lax.broadcasted_iota(jnp.int32, (tm, 128), 1) // 64
        kl = kl_ref[...]
        kvn = (kl * _rstd(kl) * kw_ref[...]).astype(BF16)
        kr = kr_ref[...]
        kr = kr * cs[:, :128] + _swap_halves(kr) * sn[:, :128]
        kr2 = (kr + pltpu.roll(kr, 64, 1)).astype(BF16)
        for h in range(MLA_HEADS):
            qc_ref[h, :, 0:128] = q[:, 128 * h:128 * h + 128].astype(BF16)
            chunk = qr[:, 128 * (h // 2):128 * (h // 2) + 128]
            qc_ref[h, :, 128:256] = jnp.where(half == (h % 2), chunk, 0.0).astype(BF16)
            kc_ref[h, :, 0:128] = _dot(kvn, wukv_ref[2 * h]).astype(BF16)
            kc_ref[h, :, 128:256] = kr2
            vv_ref[h] = _dot(kvn, wukv_ref[2 * h + 1]).astype(BF16)

    const = lambda shape: pl.BlockSpec(shape, lambda i: tuple(0 for _ in shape))
    return pl.pallas_call(
        body, name="mla_pre_fwd", grid=(S // tm,),
        in_specs=[pl.BlockSpec((tm, 256), lambda i: (i, 3)), pl.BlockSpec((tm, 128), lambda i: (i, 8)),
                  pl.BlockSpec((tm, 128), lambda i: (i, 9)), const((1, 256)), const((1, 128)),
                  const((768, 256)), const((8, 128, 128)),
                  pl.BlockSpec((tm, 128), lambda i: (i, 0)), pl.BlockSpec((tm, 128), lambda i: (i, 0))],
        out_specs=[pl.BlockSpec((MLA_HEADS, tm, 256), lambda i: (0, i, 0)),
                   pl.BlockSpec((MLA_HEADS, tm, 256), lambda i: (0, i, 0)),
                   pl.BlockSpec((MLA_HEADS, tm, 128), lambda i: (0, i, 0))],
        out_shape=[jax.ShapeDtypeStruct((MLA_HEADS, S, 256), BF16), jax.ShapeDtypeStruct((MLA_HEADS, S, 256), BF16),
                   jax.ShapeDtypeStruct((MLA_HEADS, S, 128), BF16)],
        compiler_params=_params(("parallel",)),
    )(proj, proj, proj, qn_w, kvn_w, wuqT, wukv, cos, sin)


def _causal(i, j, t):
    row = i * t + lax.broadcasted_iota(jnp.int32, (t, t), 0)
    col = j * t + lax.broadcasted_iota(jnp.int32, (t, t), 1)
    return col <= row


def _mla_attn_fwd(qc, kc, vv, *, t=512):
    S = qc.shape[1]
    t = min(t, S)

    def body(q_ref, k_ref, v_ref, o_ref, l_ref):
        i = pl.program_id(0)
        diag = _causal(0, 0, t)

        def step(j, carry, masked):
            rows = pl.ds(pl.multiple_of(j * t, t), t)
            out = []
            for h in range(MLA_HEADS):
                m, l, acc = carry[h]
                s = _dot_nt(q_ref[h], k_ref[h, rows, :]) * MLA_SCALE
                if masked:
                    s = jnp.where(diag, s, -jnp.inf)
                m_new = jnp.maximum(m, jnp.max(s, axis=-1, keepdims=True))
                alpha = jnp.exp(m - m_new)
                p = jnp.exp(s - m_new)
                l = alpha * l + jnp.sum(p, axis=-1, keepdims=True)
                acc = alpha * acc + _dot(p.astype(BF16), v_ref[h, rows, :])
                out.append((m_new, l, acc))
            return tuple(out)

        init = tuple((jnp.full((t, 1), -jnp.inf, F32), jnp.zeros((t, 1), F32), jnp.zeros((t, MLA_V), F32))
                     for _ in range(MLA_HEADS))
        carry = lax.fori_loop(0, i, lambda j, c: step(j, c, False), init)
        carry = step(i, carry, True)
        for h in range(MLA_HEADS):
            m, l, acc = carry[h]
            o_ref[:, 128 * h:128 * h + 128] = acc / l
            l_ref[h] = jnp.broadcast_to(m + jnp.log(l), (t, 128))

    return pl.pallas_call(
        body, name="mla_attn_fwd", grid=(S // t,),
        in_specs=[pl.BlockSpec((MLA_HEADS, t, 256), lambda i: (0, i, 0)),
                  pl.BlockSpec((MLA_HEADS, S, 256), lambda i: (0, 0, 0)),
                  pl.BlockSpec((MLA_HEADS, S, 128), lambda i: (0, 0, 0))],
        out_specs=[pl.BlockSpec((t, 512), lambda i: (i, 0)),
                   pl.BlockSpec((MLA_HEADS, t, 128), lambda i: (0, i, 0))],
        out_shape=[jax.ShapeDtypeStruct((S, 512), F32), jax.ShapeDtypeStruct((MLA_HEADS, S, 128), F32)],
        compiler_params=_params(("parallel",)),
    )(qc, kc, vv)


def _mla_attn_bwd(qc, kc, vv, o, lse, do, *, t=512, tq=1024):
    S = qc.shape[1]
    t = min(t, S)
    tq = min(tq, S)
    nblk = S // t
    hp = MLA_HEADS
    once = pl.Buffered(1)

    def body(q_ref, k_ref, v_ref, o_ref, l_ref, do_ref, dq_ref, dk_ref, dv_ref):
        j = pl.program_id(1)

        @pl.when(j == 0)
        def _():
            dq_ref[...] = jnp.zeros_like(dq_ref)

        first = (j * t) // tq

        def step(i, carry, masked):
            rows = pl.ds(pl.multiple_of(i * tq, tq), tq)
            if masked:
                row = i * tq + lax.broadcasted_iota(jnp.int32, (tq, t), 0)
                col = j * t + lax.broadcasted_iota(jnp.int32, (tq, t), 1)
                visible = col <= row
            out = []
            for h in range(hp):
                dk, dv = carry[h]
                k = k_ref[h]
                q = q_ref[h, rows, :]
                dov = do_ref[rows, 128 * h:128 * h + 128]
                lrow = l_ref[h, rows, :][:, 0:1]
                p = jnp.exp(_dot_nt(q, k) * MLA_SCALE - lrow)
                if masked:
                    p = jnp.where(visible, p, 0.0)
                dob = dov.astype(BF16)
                dv = dv + _dot_tn(p.astype(BF16), dob)
                dp = _dot_nt(dob, v_ref[h])
                delta = jnp.sum(dov * o_ref[rows, 128 * h:128 * h + 128], axis=-1, keepdims=True)
                ds = (p * (dp - delta) * MLA_SCALE).astype(BF16)
                dk = dk + _dot_tn(ds, q)
                dq_ref[h, rows, :] += _dot(ds, k)
                out.append((dk, dv))
            return tuple(out)

        init = tuple((jnp.zeros((t, 256), F32), jnp.zeros((t, MLA_V), F32)) for _ in range(hp))
        carry = step(first, init, True)
        carry = lax.fori_loop(first + 1, S // tq, lambda i, c: step(i, c, False), carry)
        for h in range(hp):
            dk_ref[h] = carry[h][0]
            dv_ref[h] = carry[h][1]

    return pl.pallas_call(
        body, name="mla_attn_bwd", grid=(MLA_HEADS // hp, nblk),
        in_specs=[pl.BlockSpec((hp, S, 256), lambda g, j: (g, 0, 0), pipeline_mode=once),
                  pl.BlockSpec((hp, t, 256), lambda g, j: (g, j, 0)),
                  pl.BlockSpec((hp, t, 128), lambda g, j: (g, j, 0)),
                  pl.BlockSpec((S, 128 * hp), lambda g, j: (0, g), pipeline_mode=once),
                  pl.BlockSpec((hp, S, 128), lambda g, j: (g, 0, 0), pipeline_mode=once),
                  pl.BlockSpec((S, 128 * hp), lambda g, j: (0, g), pipeline_mode=once)],
        out_specs=[pl.BlockSpec((hp, S, 256), lambda g, j: (g, 0, 0)),
                   pl.BlockSpec((hp, t, 256), lambda g, j: (g, j, 0)),
                   pl.BlockSpec((hp, t, 128), lambda g, j: (g, j, 0))],
        out_shape=[jax.ShapeDtypeStruct((MLA_HEADS, S, 256), F32), jax.ShapeDtypeStruct((MLA_HEADS, S, 256), F32),
                   jax.ShapeDtypeStruct((MLA_HEADS, S, 128), F32)],
        compiler_params=_params(("parallel", "arbitrary")),
    )(qc, kc, vv, o, lse, do)


def _mla_pre_bwd(proj, qn_w, kvn_w, wuqT, wukv, cos, sin, dqc, dkc, dvv, *, tm=ROW_TILE):
    S = proj.shape[0]
    tm = min(tm, S)

    def body(ql_ref, kl_ref, qw_ref, kw_ref, wuq_ref, wukv_ref, cos_ref, sin_ref, dqc_ref, dkc_ref, dvv_ref,
             dql_ref, dkl_ref, dkr_ref, gq_ref, gkv_ref, part_ref, gq_acc, gkv_acc):
        @pl.when(pl.program_id(0) == 0)
        def _():
            gq_acc[...] = jnp.zeros_like(gq_acc)
            gkv_acc[...] = jnp.zeros_like(gkv_acc)
            part_ref[...] = jnp.zeros_like(part_ref)

        cs, sn = _rope_wide(cos_ref), _rope_wide(sin_ref)
        half = lax.broadcasted_iota(jnp.int32, (tm, 128), 1) // 64
        ql = ql_ref[...]
        rq = _rstd(ql)
        qhat = ql * rq
        qw = qw_ref[...]
        qn = (qhat * qw).astype(BF16)
        chunks = []
        for pair in range(2):
            chunks.append(jnp.where(half == 0, dqc_ref[2 * pair, :, 128:256], dqc_ref[2 * pair + 1, :, 128:256]))
        dqr = jnp.concatenate(chunks, axis=1)
        dqr = dqr * cs + _swap_halves(dqr * sn)
        dq = jnp.concatenate([dqc_ref[h, :, 0:128] for h in range(MLA_HEADS)] + [dqr], axis=1).astype(BF16)
        gq_acc[...] += _dot_tn(dq, qn)
        dqn = _dot(dq, wuq_ref[...])
        part_ref[0:1, :] += jnp.sum(dqn * qhat, axis=0, keepdims=True)
        dql_ref[...] = _rms_bwd(dqn * qw, qhat, rq)
        kl = kl_ref[...]
        rk = _rstd(kl)
        khat = kl * rk
        kw = kw_ref[...]
        kvn = (khat * kw).astype(BF16)
        dkvn = jnp.zeros((tm, MLA_KVR), F32)
        dkr2 = jnp.zeros((tm, 128), F32)
        for h in range(MLA_HEADS):
            dkn = dkc_ref[h, :, 0:128].astype(BF16)
            dvh = dvv_ref[h].astype(BF16)
            gkv_acc[2 * h] += _dot_tn(kvn, dkn)
            gkv_acc[2 * h + 1] += _dot_tn(kvn, dvh)
            dkvn += _dot_nt(dkn, wukv_ref[2 * h]) + _dot_nt(dvh, wukv_ref[2 * h + 1])
            dkr2 += dkc_ref[h, :, 128:256]
        part_ref[1:2, 0:128] += jnp.sum(dkvn * khat, axis=0, keepdims=True)
        dkl_ref[...] = _rms_bwd(dkvn * kw, khat, rk)
        dkr = jnp.where(half == 0, dkr2 + pltpu.roll(dkr2, 64, 1), 0.0)
        dkr_ref[...] = dkr * cs[:, :128] + _swap_halves(dkr * sn[:, :128])

        @pl.when(pl.program_id(0) == S // tm - 1)
        def _():
            gkv_ref[...] = gkv_acc[...].astype(BF16)
            per = MLA_NOPE + MLA_ROPE
            for h in range(MLA_HEADS):
                gq_ref[per * h:per * h + MLA_NOPE, :] = gq_acc[MLA_NOPE * h:MLA_NOPE * (h + 1), :].astype(BF16)
                gq_ref[per * h + MLA_NOPE:per * (h + 1), :] = gq_acc[512 + MLA_ROPE * h:512 + MLA_ROPE * (h + 1), :].astype(BF16)

    const = lambda shape: pl.BlockSpec(shape, lambda i: tuple(0 for _ in shape))
    heads = lambda w: pl.BlockSpec((MLA_HEADS, tm, w), lambda i: (0, i, 0))
    return pl.pallas_call(
        body, name="mla_pre_bwd", grid=(S // tm,),
        in_specs=[pl.BlockSpec((tm, 256), lambda i: (i, 3)), pl.BlockSpec((tm, 128), lambda i: (i, 8)),
                  const((1, 256)), const((1, 128)), const((768, 256)), const((8, 128, 128)),
                  pl.BlockSpec((tm, 128), lambda i: (i, 0)), pl.BlockSpec((tm, 128), lambda i: (i, 0)),
                  heads(256), heads(256), heads(128)],
        out_specs=[pl.BlockSpec((tm, 256), lambda i: (i, 0)), pl.BlockSpec((tm, 128), lambda i: (i, 0)),
                   pl.BlockSpec((tm, 128), lambda i: (i, 0)), const((768, 256)), const((8, 128, 128)), const((8, 256))],
        out_shape=[jax.ShapeDtypeStruct((S, 256), F32), jax.ShapeDtypeStruct((S, 128), F32),
                   jax.ShapeDtypeStruct((S, 128), F32), jax.ShapeDtypeStruct((768, 256), BF16),
                   jax.ShapeDtypeStruct((8, 128, 128), BF16), jax.ShapeDtypeStruct((8, 256), F32)],
        scratch_shapes=[pltpu.VMEM((768, 256), F32), pltpu.VMEM((8, 128, 128), F32)],
        compiler_params=_params(("arbitrary",)),
    )(proj, proj, qn_w, kvn_w, wuqT, wukv, cos, sin, dqc, dkc, dvv)


def _mix_out_fwd(x, oa, ob, w_o, vecs, *, tm=ROW_TILE):
    S = x.shape[0]
    tm = min(tm, S)

    def body(x_ref, oa_ref, ob_ref, w_ref, vec_ref, xo_ref, mo_ref):
        mo = _dot(oa_ref[...].astype(BF16), w_ref[0:512, :]) + _dot(ob_ref[...].astype(BF16), w_ref[512:1024, :])
        mo_ref[...] = mo
        xo_ref[...] = x_ref[...] + vec_ref[3:4, :] * mo

    row = pl.BlockSpec((tm, D), lambda i: (i, 0))
    half = pl.BlockSpec((tm, 512), lambda i: (i, 0))
    return pl.pallas_call(
        body, name="mix_out_fwd", grid=(S // tm,),
        in_specs=[row, half, half, pl.BlockSpec((D, D), lambda i: (0, 0)), pl.BlockSpec((8, D), lambda i: (0, 0))],
        out_specs=[row, row],
        out_shape=[jax.ShapeDtypeStruct((S, D), F32), jax.ShapeDtypeStruct((S, D), F32)],
        compiler_params=_params(("parallel",)),
    )(x, oa, ob, w_o, vecs)


def _mix_out_bwd(dxo, mo, oa, ob, w_o, vecs, *, tm=ROW_TILE):
    S = dxo.shape[0]
    tm = min(tm, S)

    def body(dx_ref, mo_ref, oa_ref, ob_ref, w_ref, vec_ref, doa_ref, dob_ref, gw_ref, part_ref, gw_acc):
        @pl.when(pl.program_id(0) == 0)
        def _():
            gw_acc[...] = jnp.zeros_like(gw_acc)
            part_ref[...] = jnp.zeros_like(part_ref)

        dx = dx_ref[...]
        part_ref[0:1, :] += jnp.sum(dx * mo_ref[...], axis=0, keepdims=True)
        dmo = (vec_ref[3:4, :] * dx).astype(BF16)
        doa_ref[...] = _dot_nt(dmo, w_ref[0:512, :])
        dob_ref[...] = _dot_nt(dmo, w_ref[512:1024, :])
        gw_acc[0:512, :] += _dot_tn(oa_ref[...].astype(BF16), dmo)
        gw_acc[512:1024, :] += _dot_tn(ob_ref[...].astype(BF16), dmo)

        @pl.when(pl.program_id(0) == S // tm - 1)
        def _():
            gw_ref[...] = gw_acc[...].astype(BF16)

    row = pl.BlockSpec((tm, D), lambda i: (i, 0))
    half = pl.BlockSpec((tm, 512), lambda i: (i, 0))
    return pl.pallas_call(
        body, name="mix_out_bwd", grid=(S // tm,),
        in_specs=[row, row, half, half, pl.BlockSpec((D, D), lambda i: (0, 0)), pl.BlockSpec((8, D), lambda i: (0, 0))],
        out_specs=[half, half, pl.BlockSpec((D, D), lambda i: (0, 0)), pl.BlockSpec((8, D), lambda i: (0, 0))],
        out_shape=[jax.ShapeDtypeStruct((S, 512), F32), jax.ShapeDtypeStruct((S, 512), F32),
                   jax.ShapeDtypeStruct((D, D), BF16), jax.ShapeDtypeStruct((8, D), F32)],
        scratch_shapes=[pltpu.VMEM((D, D), F32)],
        compiler_params=_params(("arbitrary",)),
    )(dxo, mo, oa, ob, w_o, vecs)


def _mix_in_bwd(h, w_inT, dq, dk, dv, dql, dkl, dkr, *, tm=ROW_TILE):
    S = h.shape[0]
    tm = min(tm, S)
    offs = (0, 512, 640, 768, 1024, 1152)
    wid = (512, 128, 128, 256, 128, 128)

    def body(h_ref, w_ref, dq_ref, dk_ref, dv_ref, dql_ref, dkl_ref, dkr_ref, dh_ref, gw_ref):
        @pl.when(pl.program_id(0) == 0)
        def _():
            gw_ref[...] = jnp.zeros_like(gw_ref)

        hv = h_ref[...]
        dh = jnp.zeros((tm, D), F32)
        for ref, o, w in zip((dq_ref, dk_ref, dv_ref, dql_ref, dkl_ref, dkr_ref), offs, wid):
            w = min(w, D_IN - o)
            dpart = ref[...][:, :w].astype(BF16)
            dh += _dot(dpart, w_ref[o:o + w, :])
            gw_ref[o:o + w, :] += _dot_tn(dpart, hv)
        dh_ref[...] = dh

    row = pl.BlockSpec((tm, D), lambda i: (i, 0))
    part = lambda w: pl.BlockSpec((tm, w), lambda i: (i, 0))
    return pl.pallas_call(
        body, name="mix_in_bwd", grid=(S // tm,),
        in_specs=[row, pl.BlockSpec((D_IN_PAD, D), lambda i: (0, 0))] + [part(w) for w in wid],
        out_specs=[row, pl.BlockSpec((D_IN, D), lambda i: (0, 0))],
        out_shape=[jax.ShapeDtypeStruct((S, D), F32), jax.ShapeDtypeStruct((D_IN, D), F32)],
        compiler_params=_params(("arbitrary",)),
    )(h, w_inT, dq, dk, dv, dql, dkl, dkr)


def _vecs(norm_w, mod9, k):
    return jnp.concatenate([norm_w.reshape(1, D), mod9[3 * k:3 * k + 3], jnp.zeros((4, D), F32)], axis=0)


def _uq_group_rows(wuqT):
    per = MLA_NOPE + MLA_ROPE
    nope = [wuqT[per * h:per * h + MLA_NOPE] for h in range(MLA_HEADS)]
    rope = [wuqT[per * h + MLA_NOPE:per * (h + 1)] for h in range(MLA_HEADS)]
    return jnp.concatenate(nope + rope, axis=0)


def _local_step(x, tgt, mod9, norms, sinks, rel_bias, q_norm, kv_norm, W, on_grads=None):
    if on_grads is None:
        on_grads = lambda group, grads, after, vecs: vecs
    S = x.shape[0]
    v1 = _vecs(norms["ffn1"], mod9, 0)
    v2 = _vecs(norms["mix"], mod9, 1)
    v3 = _vecs(norms["ffn2"], mod9, 2)
    bucket = jnp.asarray(_bucket_table())
    cos, sin = _rope_tables(S)
    if isinstance(W, dict):
        full, W = W, (lambda group, after, vecs: (full, vecs))

    W1, v1 = W("ffn1", [], v1)
    x1, h1, a1, b1, f1 = _ffn_fwd(x, v1, W1["g1T"], W1["u1T"], W1["d1"], name="ffn1_fwd")
    W2, v2 = W("mixer", [x1], v2)
    wuqT = _uq_group_rows(W2["w_uqT"])
    h2, proj, w_inT = _mix_in_fwd(x1, v2, W2["w_inT"])
    oa, bias = _swa_fwd(proj, rel_bias, bucket, sinks)
    qc, kc, vv = _mla_pre_fwd(proj, q_norm, kv_norm, wuqT, W2["w_ukv"], cos, sin)
    ob, lse = _mla_attn_fwd(qc, kc, vv)
    _, v2o = W("ffn2_on_its_way", [ob], v2)
    x2, mo = _mix_out_fwd(x1, oa, ob, W2["w_o"], v2o)
    W3, v3 = W("ffn2", [x2], v3)
    x3, h3, a3, b3, f3 = _ffn_fwd(x2, v3, W3["g3T"], W3["u3T"], W3["d3"], name="ffn2_fwd")
    dx3, head_part, df3 = _head(x3, tgt, norms["final"], f3, v3)

    gg3, gu3, gd3, dh3 = _ffn_bwd_main(h3, df3, a3, b3, W3["g3T"], W3["u3T"], W3["d3"], name="ffn2_bwd")
    ffn2 = {"g3T": gg3, "u3T": gu3, "d3": gd3}
    v3 = on_grads("ffn2", ffn2, [], v3)
    dx2, n3_part = _norm_bwd(dh3, x2, dx3, v3, name="ffn2_norm_bwd")
    v2 = on_grads("ffn2", None, [dx2], v2)
    doa, dob, g_wo, g2_part = _mix_out_bwd(dx2, mo, oa, ob, W2["w_o"], v2)
    dq, dk, dv, drb, dsk = _swa_bwd(proj, bias, sinks, oa, doa, bucket)
    dqc, dkc, dvv = _mla_attn_bwd(qc, kc, vv, ob, lse, dob)
    dql, dkl, dkr, g_uq, g_ukv, mla_part = _mla_pre_bwd(proj, q_norm, kv_norm, wuqT, W2["w_ukv"], cos, sin, dqc, dkc, dvv)
    dh2, g_win = _mix_in_bwd(h2, w_inT, dq, dk, dv, dql, dkl, dkr)
    mixer = {"w_inT": g_win, "w_uqT": g_uq, "w_ukv": g_ukv, "w_o": g_wo}
    v2 = on_grads("mixer", mixer, [], v2)
    dx1, n2_part, df1 = _norm_bwd(dh2, x1, dx2, v2, name="mix_norm_bwd", below=(f1, v1))
    started = on_grads("mixer", None, [dx1], jnp.zeros((1, 1), F32))
    gg1, gu1, gd1, dh1 = _ffn_bwd_main(h1, df1, a1, b1, W1["g1T"], W1["u1T"], W1["d1"], name="ffn1_bwd",
                                       after=[started])
    ffn1 = {"g1T": gg1, "u1T": gu1, "d1": gd1}
    v1 = on_grads("ffn1", ffn1, [], v1)
    dx0, n1_part = _norm_bwd(dh1, x, dx1, v1, name="ffn1_norm_bwd")

    grads = {**ffn1, **ffn2, **mixer}
    return head_part[1, 0], dx0, grads, _pack_vec(n1_part, n2_part, n3_part, head_part, g2_part, mla_part, dsk, drb)


SMALL_LAYOUT = (("norm_ffn1", 1024), ("norm_mix", 1024), ("norm_ffn2", 1024), ("norm_final", 1024),
                ("q_norm", 256), ("kv_norm", 128), ("sinks", 128), ("rel_bias", 256))
N_SMALL = sum(n for _, n in SMALL_LAYOUT)
LOSS_SLOT = 4 * 1024 + 256 + 128 + SWA_HEADS
N_MODVEC = N_MOD * D
N_VEC = N_MODVEC + N_SMALL


def _pack_vec(n1, n2, n3, head, g2, mla, dsk, drb):
    def body(n1_ref, n2_ref, n3_ref, head_ref, g2_ref, mla_ref, dsk_ref, drb_ref, out_ref):
        rows = [n1_ref[1:2, :], n1_ref[2:3, :], n2_ref[3:4, :], n2_ref[1:2, :], n2_ref[2:3, :], g2_ref[0:1, :],
                n3_ref[1:2, :], n3_ref[2:3, :], head_ref[3:4, :],
                n1_ref[0:1, :], n2_ref[0:1, :], n3_ref[0:1, :], head_ref[0:1, :]]
        for i, row in enumerate(rows):
            out_ref[:, D * i:D * (i + 1)] = row
        off = D * len(rows)
        out_ref[:, off:off + 256] = mla_ref[0:1, :]
        out_ref[:, off + 256:off + 384] = mla_ref[1:2, 0:128]

        def diagonal(block):
            r = lax.broadcasted_iota(jnp.int32, block.shape, 0)
            lane = lax.broadcasted_iota(jnp.int32, block.shape, 1)
            return jnp.sum(jnp.where(r == lane, block, 0.0), axis=0, keepdims=True)

        lane = lax.broadcasted_iota(jnp.int32, (1, 128), 1)
        out_ref[:, off + 384:off + 512] = jnp.where(lane == SWA_HEADS, head_ref[1:2, 0:128], diagonal(dsk_ref[...]))
        out_ref[:, off + 512:off + 640] = diagonal(drb_ref[0:128, :])
        out_ref[:, off + 640:off + 768] = diagonal(drb_ref[128:256, :])

    vm = pl.BlockSpec(memory_space=pltpu.VMEM)
    return pl.pallas_call(body, name="pack_vec", in_specs=[vm] * 8, out_specs=vm,
                          out_shape=jax.ShapeDtypeStruct((1, N_VEC), F32))(n1, n2, n3, head, g2, mla, dsk, drb)


def _coords():
    return lax.axis_index("x"), lax.axis_index("y"), lax.axis_index("c")


def _flip(v, bit):
    return 1 - v if bit else v


def _peer(r):
    x, y, c = _coords()
    return (_flip(x, r & 4), _flip(y, r & 2), _flip(c, r & 1))


def _mod_fwd(c_tile, w_mod, b_mod3):
    W = w_mod.shape[1]

    def body(c_ref, w_ref, b_ref, mod_ref, ca_ref, call_ref, part_ref, send_sems, recv_sems):
        x, y, c = _coords()
        me = 4 * x + 2 * y + c
        call_ref[me] = c_ref[...]
        sends = []
        for r in range(1, N_DEV):
            cp = pltpu.make_async_remote_copy(c_ref, call_ref.at[me], send_sems.at[0, r], recv_sems.at[0, r],
                                              device_id=_peer(r), device_id_type=MESH)
            cp.start()
            sends.append(cp)
        for r in range(1, N_DEV):
            pltpu.make_async_remote_copy(c_ref, call_ref.at[me], send_sems.at[0, r], recv_sems.at[0, r],
                                         device_id=_peer(r), device_id_type=MESH).wait_recv()
        cv = call_ref[...].reshape(8 * N_DEV, D)
        ca = (cv * _sigmoid(cv)).astype(BF16)
        ca_ref[...] = ca
        part_ref[...] = _dot(ca, w_ref[...].astype(BF16)).reshape(N_DEV, 8, W)
        mod_ref[me] = part_ref[me] + b_ref[me]
        for r in range(1, N_DEV):
            cp = pltpu.make_async_remote_copy(part_ref.at[me ^ r], mod_ref.at[me], send_sems.at[1, r],
                                              recv_sems.at[1, r], device_id=_peer(r), device_id_type=MESH)
            cp.start()
            sends.append(cp)
        for r in range(1, N_DEV):
            pltpu.make_async_remote_copy(part_ref.at[me ^ r], mod_ref.at[me], send_sems.at[1, r],
                                         recv_sems.at[1, r], device_id=_peer(r), device_id_type=MESH).wait_recv()
            mod_ref[me ^ r] = mod_ref[me ^ r] + b_ref[me ^ r]
        for cp in sends:
            cp.wait_send()

    vm = pl.BlockSpec(memory_space=pltpu.VMEM)
    return pl.pallas_call(
        body, name="mod_fwd", in_specs=[vm, vm, vm], out_specs=[vm, vm],
        out_shape=[jax.ShapeDtypeStruct((N_DEV, 8, W), F32), jax.ShapeDtypeStruct((8 * N_DEV, D), BF16)],
        scratch_shapes=[pltpu.VMEM((N_DEV, 8, D), F32), pltpu.VMEM((N_DEV, 8, W), F32),
                        pltpu.SemaphoreType.DMA((2, N_DEV)), pltpu.SemaphoreType.DMA((2, N_DEV))],
        compiler_params=_params(),
    )(c_tile, w_mod, b_mod3)


def _mod_bwd(allvec, ca, me_idx):
    W = N_MODVEC // N_DEV

    def body(me_ref, all_ref, cols_ref, ca_ref, gw_ref, sum_ref):
        in_first_row = lax.broadcasted_iota(jnp.int32, (N_DEV, 8, W), 1) == 0
        dm = jnp.where(in_first_row, cols_ref[...], 0.0).reshape(8 * N_DEV, W)
        gw_ref[...] = _dot_tn(ca_ref[...], dm.astype(BF16))
        total = all_ref[0]
        for k in range(1, N_DEV):
            total = total + all_ref[k]
        sum_ref[...] = total

    return pl.pallas_call(
        body, name="mod_bwd",
        grid_spec=pltpu.PrefetchScalarGridSpec(
            num_scalar_prefetch=1, grid=(1,),
            in_specs=[pl.BlockSpec((N_DEV, 1, N_VEC), lambda i, me: (0, 0, 0)),
                      pl.BlockSpec((N_DEV, 1, W), lambda i, me: (0, 0, me[0])),
                      pl.BlockSpec((8 * N_DEV, D), lambda i, me: (0, 0))],
            out_specs=[pl.BlockSpec((D, W), lambda i, me: (0, 0)), pl.BlockSpec((1, N_VEC), lambda i, me: (0, 0))]),
        out_shape=[jax.ShapeDtypeStruct((D, W), F32), jax.ShapeDtypeStruct((1, N_VEC), F32)],
        compiler_params=_params(("arbitrary",)),
    )(me_idx, allvec, allvec, ca)


def _wgather(shards):
    n = len(shards)
    rows = [s.shape[0] for s in shards]

    def body(*refs):
        ins, outs, token = refs[:n], refs[n:2 * n], refs[2 * n]
        send_sems, recv_sems, local_sems = refs[2 * n + 1:]
        token[...] = jnp.zeros_like(token)
        x, y, c = _coords()
        me = 4 * x + 2 * y + c
        sib, xn, yn = (x, y, 1 - c), (1 - x, y, c), (x, 1 - y, c)
        block = lambda px, py, pc: 4 * px + 2 * py + pc

        def part(k, blk, half):
            if half is None:
                return outs[k].at[blk]
            return outs[k].at[blk, pl.ds(half * (rows[k] // 2), rows[k] // 2)]

        def copy(k, slot, blk, to, half=None, src=None):
            ref = part(k, blk, half)
            return pltpu.make_async_remote_copy(
                src_ref=ref if src is None else src, dst_ref=ref, send_sem=send_sems.at[k, slot],
                recv_sem=recv_sems.at[k, slot], device_id=to, device_id_type=MESH)

        local = [pltpu.make_async_copy(ins[k], outs[k].at[me], local_sems.at[k]) for k in range(n)]
        for cp in local:
            cp.start()
        sent = [copy(k, slot, me, to, src=ins[k]) for k in range(n) for slot, to in ((0, sib), (1, xn), (2, yn))]
        for cp in sent:
            cp.start()
        bx, by, bd = block(1 - x, y, c), block(x, 1 - y, c), block(1 - x, 1 - y, c)
        for k in range(n):
            copy(k, 1, bx, sib).wait_recv()
            sent += [copy(k, 4, bx, yn, half=1), copy(k, 5, bx, sib)]
            sent[-2].start()
            sent[-1].start()
        for k in range(n):
            copy(k, 2, by, sib).wait_recv()
            sent += [copy(k, 3, by, xn, half=0), copy(k, 6, by, sib)]
            sent[-2].start()
            sent[-1].start()
        for k in range(n):
            copy(k, 3, bd, sib, half=0).wait_recv()
            copy(k, 4, bd, sib, half=1).wait_recv()
            sent.append(copy(k, 7, bd, sib))
            sent[-1].start()
        for k in range(n):
            copy(k, 0, block(x, y, 1 - c), sib).wait_recv()
            for slot, blk in ((5, block(1 - x, y, 1 - c)), (6, block(x, 1 - y, 1 - c)), (7, block(1 - x, 1 - y, 1 - c))):
                copy(k, slot, blk, sib).wait_recv()
        for cp in sent:
            cp.wait_send()
        for cp in local:
            cp.wait()

    anyspec = pl.BlockSpec(memory_space=pl.ANY)
    return pl.pallas_call(
        body, name="wgather", in_specs=[anyspec] * n,
        out_specs=[anyspec] * n + [pl.BlockSpec(memory_space=pltpu.VMEM)],
        out_shape=[jax.ShapeDtypeStruct((N_DEV,) + s.shape, s.dtype) for s in shards]
        + [jax.ShapeDtypeStruct((8, 128), F32)],
        scratch_shapes=[pltpu.SemaphoreType.DMA((n, 8)), pltpu.SemaphoreType.DMA((n, 8)),
                        pltpu.SemaphoreType.DMA((n,))],
    )(*shards)


class _GatherCopies:
    def __init__(self, lands, send_sems, recv_sems, k0=0, batches=None):
        x, y, c = _coords()
        me = 4 * x + 2 * y + c
        sib = (x, y, 1 - c)
        chips = [(1 - x, y), (x, 1 - y), (1 - x, 1 - y)]

        def copy(k, slot, block, to):
            return pltpu.make_async_remote_copy(
                src_ref=lands[k].at[block], dst_ref=lands[k].at[block],
                send_sem=send_sems.at[7 * (k0 + k) + slot], recv_sem=recv_sems.at[7 * (k0 + k) + slot],
                device_id=to, device_id_type=MESH)

        n = len(lands)
        self.first = [copy(k, 0, me, sib) for k in range(n)]
        for batch in batches or [range(n)]:
            self.first += [copy(k, 1 + j, me, (cx, cy, c)) for j, (cx, cy) in enumerate(chips) for k in batch]
        self.landed = [copy(k, 1 + j, 4 * cx + 2 * cy + c, sib) for j, (cx, cy) in enumerate(chips) for k in range(n)]
        self.passed = [copy(k, 4 + j, 4 * cx + 2 * cy + c, sib) for j, (cx, cy) in enumerate(chips) for k in range(n)]
        self.from_sib = [copy(k, 0, 4 * x + 2 * y + (1 - c), sib) for k in range(n)]
        self.from_sib += [copy(k, 4 + j, 4 * cx + 2 * cy + (1 - c), sib) for j, (cx, cy) in enumerate(chips)
                          for k in range(n)]


def _gather_start(lands, *, name, batches=None):
    n = len(lands)

    def body(*refs):
        for cp in _GatherCopies(refs[:n], refs[n], refs[n + 1], batches=batches).first:
            cp.start()
        refs[-1][...] = jnp.zeros_like(refs[-1])

    out = pl.pallas_call(
        body, name=name,
        out_shape=(pltpu.SemaphoreType.DMA((7 * n,)), pltpu.SemaphoreType.DMA((7 * n,)),
                   *[pltpu.HBM(l.shape, l.dtype) for l in lands], jax.ShapeDtypeStruct((8, 128), F32)),
        in_specs=[HBM_SPEC] * n,
        out_specs=(SEM_SPEC, SEM_SPEC, *[HBM_SPEC] * n, pl.BlockSpec(memory_space=pltpu.VMEM)),
        input_output_aliases={i: 2 + i for i in range(n)},
        compiler_params=pltpu.CompilerParams(has_side_effects=DATAFLOW),
    )(*[_in_hbm(l) for l in lands])
    return out[0], out[1], list(out[2:2 + n]), out[-1]


def _gather_pass(send_sems, recv_sems, lands, after, *, name, stage, k0=0):
    n = len(lands)

    def body(*refs):
        cps = _GatherCopies(refs[:n], refs[n], refs[n + 1], k0)
        if stage == "landed":
            for cp in cps.landed:
                cp.wait_recv()
        else:
            for cp in cps.passed:
                cp.start()
        refs[-1][...] = jnp.zeros_like(refs[-1])

    out = pl.pallas_call(
        body, name=name,
        out_shape=(*[pltpu.HBM(l.shape, l.dtype) for l in lands], jax.ShapeDtypeStruct((8, 128), F32)),
        in_specs=[HBM_SPEC] * n + [SEM_SPEC, SEM_SPEC] + [pl.BlockSpec(memory_space=pl.ANY)] * len(after),
        out_specs=(*[HBM_SPEC] * n, pl.BlockSpec(memory_space=pltpu.VMEM)),
        input_output_aliases={i: i for i in range(n)},
        compiler_params=pltpu.CompilerParams(has_side_effects=DATAFLOW),
    )(*lands, send_sems, recv_sems, *after)
    return list(out[:n]), out[-1]


def _gather_end(send_sems, recv_sems, lands, after, *, name, k0=0):
    n = len(lands)

    def body(*refs):
        cps = _GatherCopies(refs[:n], refs[n], refs[n + 1], k0)
        for cp in cps.from_sib:
            cp.wait_recv()
        for cp in cps.first + cps.passed:
            cp.wait_send()

    out = pl.pallas_call(
        body, name=name,
        out_shape=[pltpu.HBM(l.shape, l.dtype) for l in lands],
        in_specs=[HBM_SPEC] * n + [SEM_SPEC, SEM_SPEC] + [pl.BlockSpec(memory_space=pl.ANY)] * len(after),
        out_specs=[HBM_SPEC] * n,
        input_output_aliases={i: i for i in range(n)},
        compiler_params=pltpu.CompilerParams(has_side_effects=DATAFLOW),
    )(*lands, send_sems, recv_sems, *after)
    return list(out)


def _d2d_copies(grads, lands, send_sems, recv_sems):
    x, y, c = _coords()
    return [pltpu.make_async_remote_copy(
        src_ref=grads[k].at[2 * q + (1 - c)], dst_ref=lands[k].at[q],
        send_sem=send_sems.at[4 * k + q], recv_sem=recv_sems.at[4 * k + q],
        device_id=(x, y, 1 - c), device_id_type=MESH) for k in range(len(grads)) for q in range(4)]


def _direct_copies(grads, lands, send_sems, recv_sems):
    x, y, c = _coords()
    me = 4 * x + 2 * y + c
    return [pltpu.make_async_remote_copy(
        src_ref=grads[k].at[me ^ r], dst_ref=lands[k].at[r - 1],
        send_sem=send_sems.at[7 * k + r - 1], recv_sem=recv_sems.at[7 * k + r - 1],
        device_id=_peer(r), device_id_type=MESH) for k in range(len(grads)) for r in range(1, N_DEV)]


def _vec_copies(srcs, lands, send_sems, recv_sems):
    x, y, c = _coords()
    me = 4 * x + 2 * y + c
    return [pltpu.make_async_remote_copy(
        src_ref=lands[0].at[me], dst_ref=lands[0].at[me], send_sem=send_sems.at[r - 1], recv_sem=recv_sems.at[r - 1],
        device_id=_peer(r), device_id_type=MESH) for r in range(1, N_DEV)]


def _chipsum(gs, sibs, cidx, *, name):
    n = len(gs)

    def body(c_ref, *refs):
        for k in range(n):
            refs[2 * n + k][...] = (refs[k][...].astype(F32) + refs[n + k][...].astype(F32)).astype(refs[2 * n + k].dtype)

    mine = [pl.BlockSpec((1,) + g.shape[1:], lambda q, c_ref: (2 * q + c_ref[0], 0, 0)) for g in gs]
    other = [pl.BlockSpec((1,) + g.shape[1:], lambda q, c_ref: (q, 0, 0)) for g in gs]
    return pl.pallas_call(
        body, name=name,
        grid_spec=pltpu.PrefetchScalarGridSpec(num_scalar_prefetch=1, grid=(4,), in_specs=mine + other, out_specs=other),
        out_shape=[jax.ShapeDtypeStruct((4,) + g.shape[1:], g.dtype) for g in gs],
        compiler_params=_params(("arbitrary",)),
    )(cidx, *gs, *sibs)


HBM_SPEC = pl.BlockSpec(memory_space=pltpu.HBM)
SEM_SPEC = pl.BlockSpec(memory_space=pltpu.SEMAPHORE)
DATAFLOW = pltpu.SideEffectType.DATAFLOW_SIDE_EFFECTING


def _in_hbm(a):
    return pltpu.with_memory_space_constraint(a, pltpu.HBM)


def _rs_step1_copies(sums, lands, send_sems, recv_sems):
    n = len(sums)
    direct, relay = lands[:n], lands[n:]
    x, y, c = _coords()
    xn, yn = (1 - x, y, c), (x, 1 - y, c)
    qx, qy, qd = 2 * (1 - x) + y, 2 * x + (1 - y), 2 * (1 - x) + (1 - y)
    cps = []
    for k in range(n):
        h = sums[k].shape[1] // 2
        a, b = pl.ds(0, h), pl.ds(h, h)
        moves = ((sums[k].at[qx, a], direct[k].at[0], xn), (sums[k].at[qy, b], direct[k].at[1], yn),
                 (sums[k].at[qd, a], relay[k].at[0], xn), (sums[k].at[qd, b], relay[k].at[1], yn))
        for s, (src, dst, to) in enumerate(moves):
            cps.append(pltpu.make_async_remote_copy(
                src_ref=src, dst_ref=dst, send_sem=send_sems.at[4 * k + s], recv_sem=recv_sems.at[4 * k + s],
                device_id=to, device_id_type=MESH))
    return cps


def _rs_step2_copies(relayed, lands, send_sems, recv_sems, k0=0):
    x, y, c = _coords()
    cps = []
    for k in range(len(relayed)):
        for s, to in enumerate(((1 - x, y, c), (x, 1 - y, c))):
            cps.append(pltpu.make_async_remote_copy(
                src_ref=relayed[k].at[s], dst_ref=lands[k].at[s], send_sem=send_sems.at[2 * (k0 + k) + s],
                recv_sem=recv_sems.at[2 * (k0 + k) + s], device_id=to, device_id_type=MESH))
    return cps


def _relay_sum(sums, relay, qxy, *, name):
    n = len(sums)

    def body(q_ref, *refs):
        for k in range(n):
            refs[2 * n + k][...] = (refs[k][...].astype(F32) + refs[n + k][...].astype(F32)).astype(refs[2 * n + k].dtype)

    half = lambda s: (1, s.shape[1] // 2) + s.shape[2:]
    return pl.pallas_call(
        body, name=name,
        grid_spec=pltpu.PrefetchScalarGridSpec(
            num_scalar_prefetch=1, grid=(2,),
            in_specs=[pl.BlockSpec(half(s), lambda t, q_ref: (q_ref[t], 1 - t, 0)) for s in sums]
            + [pl.BlockSpec(half(s), lambda t, q_ref: (1 - t, 0, 0)) for s in sums],
            out_specs=[pl.BlockSpec(half(s), lambda t, q_ref: (t, 0, 0)) for s in sums]),
        out_shape=[jax.ShapeDtypeStruct((2,) + half(s)[1:], s.dtype) for s in sums],
        compiler_params=_params(("arbitrary",)),
    )(qxy, *sums, *relay)


def _split_start(copies, srcs, lands, n_sems, after, *, name):
    ns, nl = len(srcs), len(lands)

    def body(*refs):
        for cp in copies(refs[:ns], refs[ns:ns + nl], refs[ns + nl + len(after)], refs[ns + nl + len(after) + 1]):
            cp.start()
        refs[-1][...] = jnp.zeros_like(refs[-1])

    bufs = [_in_hbm(a) for a in list(srcs) + list(lands)]
    out = pl.pallas_call(
        body, name=name,
        out_shape=(pltpu.SemaphoreType.DMA((n_sems,)), pltpu.SemaphoreType.DMA((n_sems,)),
                   *[pltpu.HBM(a.shape, a.dtype) for a in bufs], jax.ShapeDtypeStruct((8, 128), F32)),
        in_specs=[HBM_SPEC] * len(bufs) + [pl.BlockSpec(memory_space=pl.ANY)] * len(after),
        out_specs=(SEM_SPEC, SEM_SPEC, *[HBM_SPEC] * len(bufs), pl.BlockSpec(memory_space=pltpu.VMEM)),
        input_output_aliases={i: 2 + i for i in range(len(bufs))},
        compiler_params=pltpu.CompilerParams(has_side_effects=DATAFLOW),
    )(*bufs, *after)
    return out[0], out[1], list(out[2:2 + ns]), list(out[2 + ns:2 + ns + nl]), out[-1]


def _split_wait(copies, send_sems, recv_sems, srcs, lands, after, *, name):
    ns, nl = len(srcs), len(lands)

    def body(*refs):
        for cp in copies(refs[:ns], refs[ns:ns + nl], refs[ns + nl], refs[ns + nl + 1]):
            cp.wait_send()
            cp.wait_recv()

    out = pl.pallas_call(
        body, name=name,
        out_shape=[pltpu.HBM(a.shape, a.dtype) for a in list(srcs) + list(lands)],
        in_specs=[HBM_SPEC] * (ns + nl) + [SEM_SPEC, SEM_SPEC] + [pl.BlockSpec(memory_space=pl.ANY)] * len(after),
        out_specs=[HBM_SPEC] * (ns + nl),
        input_output_aliases={i: i for i in range(ns + nl)},
        compiler_params=pltpu.CompilerParams(has_side_effects=DATAFLOW),
    )(*srcs, *lands, send_sems, recv_sems, *after)
    return list(out[:ns]), list(out[ns:])


ADAM_C1 = 1.0 / (1.0 - ADAM_B1 ** ADAM_STEP)
ADAM_C2 = 1.0 / (1.0 - ADAM_B2 ** ADAM_STEP)


def _adam_math(w, g, m, v):
    m2 = ADAM_B1 * m + (1.0 - ADAM_B1) * g
    v2 = ADAM_B2 * v + (1.0 - ADAM_B2) * (g * g)
    return -ADAM_LR * ((m2 * ADAM_C1) / (jnp.sqrt(v2 * ADAM_C2) + ADAM_EPS) + ADAM_WD * w), m2, v2


def _adamw(w, g, m, v, *, name, after=()):
    R, C = w.shape
    tr = R if R <= 512 else 256

    def body(w_ref, g_ref, m_ref, v_ref, *rest):
        d_ref, nm_ref, nv_ref = rest[len(after):]
        d_ref[...], nm_ref[...], nv_ref[...] = _adam_math(w_ref[...], g_ref[...], m_ref[...], v_ref[...])

    blk = pl.BlockSpec((tr, C), lambda i: (i, 0))
    return pl.pallas_call(
        body, name=name, grid=(R // tr,), in_specs=[blk] * 4 + [pl.BlockSpec(memory_space=pl.ANY)] * len(after),
        out_specs=[blk] * 3, out_shape=[jax.ShapeDtypeStruct((R, C), F32)] * 3,
        compiler_params=_params(("parallel",)),
    )(w, g, m, v, *after)


def _adamw_rs2(wmv, cs, direct, second, qidx, *, name):
    n = len(wmv)
    r, cc = wmv[0][0].shape
    h = r // 2

    def body(q_ref, *refs):
        ins, outs = refs[:6 * n], refs[6 * n:]
        for k in range(n):
            w_ref, m_ref, v_ref, c_ref, d1_ref, d2_ref = ins[6 * k:6 * k + 6]
            g_ref, d_ref, nm_ref, nv_ref = outs[4 * k:4 * k + 4]
            g = (c_ref[0].astype(F32) + d1_ref[0].astype(F32)) + d2_ref[0].astype(F32)
            g_ref[...] = g
            d_ref[...], nm_ref[...], nv_ref[...] = _adam_math(w_ref[...], g, m_ref[...], v_ref[...])

    blk = pl.BlockSpec((h, cc), lambda i, q_ref: (i, 0))
    one = [blk, blk, blk, pl.BlockSpec((1, h, cc), lambda i, q_ref: (q_ref[0], i, 0)),
           pl.BlockSpec((1, h, cc), lambda i, q_ref: (i, 0, 0)),
           pl.BlockSpec((1, h, cc), lambda i, q_ref: (1 - i, 0, 0))]
    out = pl.pallas_call(
        body, name=name,
        grid_spec=pltpu.PrefetchScalarGridSpec(num_scalar_prefetch=1, grid=(2,), in_specs=one * n,
                                               out_specs=[blk] * (4 * n)),
        out_shape=[jax.ShapeDtypeStruct((r, cc), F32)] * (4 * n),
        compiler_params=_params(("arbitrary",)),
    )(qidx, *[a for (w, m, v), c, d1, d2 in zip(wmv, cs, direct, second) for a in (w, m, v, c, d1, d2)])
    return [tuple(out[4 * k:4 * k + 4]) for k in range(n)]


def _adamw_rs(wmv, cs, rcv, qidx, *, name):
    n = len(wmv)
    shapes = [w.shape for w, _, _ in wmv]
    n_rcv = rcv[0].shape[0]
    halved = len(set(shapes)) == 1 and shapes[0][0] % 32 == 0 and shapes[0][0] > 128
    tiles = 2 if halved else 1

    def body(q_ref, *refs):
        ins, outs = refs[:5 * n], refs[5 * n:]
        for k in range(n):
            w_ref, m_ref, v_ref, c_ref, r_ref = ins[5 * k:5 * k + 5]
            g_ref, d_ref, nm_ref, nv_ref = outs[4 * k:4 * k + 4]
            g = c_ref[0].astype(F32)
            for j in range(n_rcv):
                g = g + r_ref[j].astype(F32)
            g_ref[...] = g
            d_ref[...], nm_ref[...], nv_ref[...] = _adam_math(w_ref[...], g, m_ref[...], v_ref[...])

    in_specs, out_specs = [], []
    for r, cc in shapes:
        blk = pl.BlockSpec((r // tiles, cc), lambda i, q_ref: (i, 0))
        in_specs += [blk, blk, blk, pl.BlockSpec((1, r // tiles, cc), lambda i, q_ref: (q_ref[0], i, 0)),
                     pl.BlockSpec((n_rcv, r // tiles, cc), lambda i, q_ref: (0, i, 0))]
        out_specs += [blk] * 4
    out = pl.pallas_call(
        body, name=name,
        grid_spec=pltpu.PrefetchScalarGridSpec(num_scalar_prefetch=1, grid=(tiles,), in_specs=in_specs,
                                               out_specs=out_specs),
        out_shape=[jax.ShapeDtypeStruct(s, F32) for s in shapes for _ in range(4)],
        compiler_params=_params(("arbitrary",)),
    )(qidx, *[a for (w, m, v), c, rc in zip(wmv, cs, rcv) for a in (w, m, v, c, rc)])
    return [tuple(out[4 * k:4 * k + 4]) for k in range(n)]


SMALL_PARAMS = ("norm_ffn1", "norm_mix", "norm_ffn2", "norm_final", "q_norm", "kv_norm", "sinks", "rel_bias", "b_mod")


def _adamw_small(gvec, wmv):
    shapes = [wmv[3 * i].shape for i in range(len(SMALL_PARAMS))]

    def body(*refs):
        g_all = refs[0]
        ins = refs[1:1 + 3 * len(SMALL_PARAMS)]
        outs = refs[1 + 3 * len(SMALL_PARAMS):]
        off = N_MODVEC
        for i, name in enumerate(SMALL_PARAMS):
            g_ref, d_ref, nm_ref, nv_ref = outs[4 * i:4 * i + 4]
            w_ref, m_ref, v_ref = ins[3 * i:3 * i + 3]
            start = 0 if name == "b_mod" else off
            rows, width = shapes[i]
            g = jnp.concatenate([g_all[:, start + width * r:start + width * (r + 1)] for r in range(rows)], axis=0)
            g_ref[...] = g
            d_ref[...], nm_ref[...], nv_ref[...] = _adam_math(w_ref[...], g, m_ref[...], v_ref[...])
            if name != "b_mod":
                off += dict(SMALL_LAYOUT)[name]

    vm = pl.BlockSpec(memory_space=pltpu.VMEM)
    n_out = 4 * len(SMALL_PARAMS)
    out = pl.pallas_call(
        body, name="adamw_small", in_specs=[vm] * (1 + len(wmv)), out_specs=[vm] * n_out,
        out_shape=[jax.ShapeDtypeStruct(shapes[i // 4], F32) for i in range(n_out)],
        compiler_params=_params(),
    )(gvec, *wmv)
    return {name: out[4 * i:4 * i + 4] for i, name in enumerate(SMALL_PARAMS)}


TRANSPOSED = ("g1T", "u1T", "g3T", "u3T", "w_inT", "w_uqT")


def kernel(x, c, w_mod, b_mod, norm_ffn1, ffn1_gate, ffn1_up, ffn1_down, norm_mix, w_in, q_norm, kv_norm, w_uq, w_ukv, sinks, w_o, norm_ffn2, ffn2_gate, ffn2_up, ffn2_down, rel_bias, norm_final, loss_target, m_w_mod, m_b_mod, m_norm_ffn1, m_ffn1_gate, m_ffn1_up, m_ffn1_down, m_norm_mix, m_w_in, m_q_norm, m_kv_norm, m_w_uq, m_w_ukv, m_sinks, m_w_o, m_norm_ffn2, m_ffn2_gate, m_ffn2_up, m_ffn2_down, m_rel_bias, m_norm_final, v_w_mod, v_b_mod, v_norm_ffn1, v_ffn1_gate, v_ffn1_up, v_ffn1_down, v_norm_mix, v_w_in, v_q_norm, v_kv_norm, v_w_uq, v_w_ukv, v_sinks, v_w_o, v_norm_ffn2, v_ffn2_gate, v_ffn2_up, v_ffn2_down, v_rel_bias, v_norm_final):
    mx, my, mc = _coords()
    cidx = jnp.reshape(mc, (1,)).astype(jnp.int32)
    qidx = jnp.reshape(2 * mx + my, (1,)).astype(jnp.int32)
    WM = w_mod.shape[2]

    c_tile = jnp.pad(c, ((0, 7), (0, 0)))
    b_mod3 = jnp.pad(b_mod.reshape(N_DEV, 1, WM), ((0, 0), (0, 7), (0, 0)))
    mod3, ca = _mod_fwd(c_tile, w_mod[0], b_mod3)
    mod9 = mod3[:, 0, :].reshape(N_MOD, D)

    shards = {"g1T": ffn1_gate[0].T.astype(BF16), "u1T": ffn1_up[0].T.astype(BF16), "d1": ffn1_down[0].astype(BF16),
              "g3T": ffn2_gate[0].T.astype(BF16), "u3T": ffn2_up[0].T.astype(BF16), "d3": ffn2_down[0].astype(BF16),
              "w_inT": w_in[0].T, "w_uqT": w_uq[0].T.astype(BF16), "w_ukv": w_ukv[0].astype(BF16),
              "w_o": w_o[0].astype(BF16)}
    me = 4 * mx + 2 * my + mc
    groups = {"ffn1": ("g1T", "u1T", "d1"), "mixer": ("w_inT", "w_uqT", "w_ukv", "w_o"), "ffn2": ("g3T", "u3T", "d3")}
    arriving = {}

    def as_weights(group, gathered):
        return {k: g if k == "w_ukv" else g.reshape(N_DEV * g.shape[1], g.shape[2])
                for k, g in zip(groups[group], gathered)}

    later = groups["mixer"] + groups["ffn2"]
    place = {"mixer": 0, "ffn2": len(groups["mixer"])}

    def start_gather(token):
        lands = []
        for k in later:
            sh = shards[k] + token[0, 0].astype(shards[k].dtype)
            lands.append(lax.dynamic_update_slice(lax.empty((N_DEV,) + sh.shape, sh.dtype), sh[None], (me, 0, 0)))
        batches = [range(k0, k0 + len(groups[group])) for group, k0 in place.items()]
        send, recv, lands, started = _gather_start(lands, name="gather_start", batches=batches)
        for group, k0 in place.items():
            arriving[group] = (send, recv, lands[k0:k0 + len(groups[group])])
        return started

    def fetch(group, after, vecs):
        if group == "ffn1":
            *gathered, token = _wgather([shards[k] + ca[1, 0].astype(shards[k].dtype) for k in groups["ffn1"]])
            return as_weights("ffn1", gathered), vecs + start_gather(token)[0:1, 0:1]

        def pass_on(group, after):
            send, recv, lands = arriving[group]
            lands, token = _gather_pass(send, recv, lands, after, name="gather_landed_" + group, stage="landed",
                                        k0=place[group])
            lands, token = _gather_pass(send, recv, lands, [token], name="gather_onward_" + group, stage="onward",
                                        k0=place[group])
            arriving[group] = (send, recv, lands)
            return token

        if group == "ffn2_on_its_way":
            return None, vecs + pass_on("ffn2", after)[0:1, 0:1]
        if group == "mixer":
            after = [pass_on("mixer", after)]
        send, recv, lands = arriving[group]
        return as_weights(group, _gather_end(send, recv, lands, after, name="gather_end_" + group,
                                             k0=place[group])), vecs

    norms ={"ffn1": norm_ffn1, "mix": norm_mix, "ffn2": norm_ffn2, "final": norm_final.reshape(1, D)}
    in_flight = {}

    def on_grads(group, g, after, vecs):
        if group != "ffn1":
            if g is None:
                return vecs
            names = list(g)
            by_dest = [g[k] if k == "w_ukv" else g[k].reshape((N_DEV, g[k].shape[0] // N_DEV) + g[k].shape[1:])
                       for k in names]
            lands = [lax.empty((N_DEV - 1,) + a.shape[1:], a.dtype) for a in by_dest]
            send, recv, by_dest, lands, token = _split_start(_direct_copies, by_dest, lands, 7 * len(names), after,
                                                             name="rs_start_" + group)
            in_flight[group] = (names, send, recv, by_dest, lands, token)
            return vecs + token[0:1, 0:1]
        names = list(g)
        by_dest = [g[k].reshape((N_DEV, g[k].shape[0] // N_DEV) + g[k].shape[1:]) for k in names]
        lands = [lax.empty((4,) + a.shape[1:], a.dtype) for a in by_dest]
        send, recv, by_dest, lands, token = _split_start(_d2d_copies, by_dest, lands, 4 * len(names), after,
                                                         name="rs_d2d_start_" + group)
        finish("mixer", [token])
        by_dest, from_sib = _split_wait(_d2d_copies, send, recv, by_dest, lands, [done[-1]], name="rs_d2d_wait_" + group)
        sums = _chipsum(by_dest, from_sib, cidx, name="chipsum_" + group)
        halves = lambda: [lax.empty((2, s.shape[1] // 2) + s.shape[2:], s.dtype) for s in sums]
        send, recv, sums, lands, token = _split_start(_rs_step1_copies, sums, halves() + halves(), 4 * len(names), [],
                                                      name="rs_ici_start_" + group)
        in_flight[group] = (names, send, recv, sums, lands, token)
        return vecs + token[0:1, 0:1]

    owners = {"g1T": ("ffn1_gate", ffn1_gate, m_ffn1_gate, v_ffn1_gate), "u1T": ("ffn1_up", ffn1_up, m_ffn1_up, v_ffn1_up),
              "d1": ("ffn1_down", ffn1_down, m_ffn1_down, v_ffn1_down),
              "g3T": ("ffn2_gate", ffn2_gate, m_ffn2_gate, v_ffn2_gate), "u3T": ("ffn2_up", ffn2_up, m_ffn2_up, v_ffn2_up),
              "d3": ("ffn2_down", ffn2_down, m_ffn2_down, v_ffn2_down),
              "w_inT": ("w_in", w_in, m_w_in, v_w_in), "w_uqT": ("w_uq", w_uq, m_w_uq, v_w_uq),
              "w_ukv": ("w_ukv", w_ukv, m_w_ukv, v_w_ukv), "w_o": ("w_o", w_o, m_w_o, v_w_o)}
    res, done = {}, []

    there = lambda k, a: a[0].T if k in TRANSPOSED else a[0]
    back = lambda k, a: a.T[None] if k in TRANSPOSED else a[None]

    def record(names, outs):
        for k, out in zip(names, outs):
            done.append(out[3])
            res[owners[k][0]] = tuple(back(k, a) for a in out)

    def finish(group, after):
        names, send, recv, sums, lands, _ = in_flight[group]
        wmv = [tuple(there(k, a) for a in owners[k][1:]) for k in names]
        own = jnp.reshape(me, (1,)).astype(jnp.int32)
        sums, lands = _split_wait(_direct_copies, send, recv, sums, lands, after, name="rs_wait_" + group)
        record(names, _adamw_rs(wmv, sums, lands, own, name="adamw_" + group))

    _, grad_x, _, vec = _local_step(
        x[0], loss_target[0], mod9, norms, sinks, rel_bias, q_norm, kv_norm, fetch, on_grads=on_grads)

    names, send, recv, sums, lands, step1_started = in_flight["ffn1"]
    vec = vec.reshape(1, 1, N_VEC)
    allvec = lax.dynamic_update_slice(lax.empty((N_DEV, 1, N_VEC), F32), vec, (me, 0, 0))
    vsend, vrecv, _, (allvec,), vec_started = _split_start(_vec_copies, [], [allvec], N_DEV - 1, [step1_started],
                                                           name="vec_start")
    finish("ffn2", [vec_started])
    n = len(names)
    sums, lands = _split_wait(_rs_step1_copies, send, recv, sums, lands, [done[-1]], name="rs_ici_wait_ffn1")
    direct, relay = lands[:n], lands[n:]
    qxy = jnp.stack([2 * (1 - mx) + my, 2 * mx + (1 - my)]).astype(jnp.int32)
    relayed = _relay_sum(sums, relay, qxy, name="relay_sum_ffn1")
    second = [lax.empty(a.shape, a.dtype) for a in relayed]
    send, recv, relayed, second, step2_started = _split_start(_rs_step2_copies, relayed, second, 2 * n, [],
                                                              name="rs_ici_start2_ffn1")

    _, (allvec,) = _split_wait(_vec_copies, vsend, vrecv, [], [allvec], [step2_started], name="vec_wait")
    g_wmod, gvec = _mod_bwd(allvec, ca, jnp.reshape(me, (1,)).astype(jnp.int32))
    loss = gvec[0, N_MODVEC + LOSS_SLOT]
    small_in = {"norm_ffn1": (norm_ffn1, m_norm_ffn1, v_norm_ffn1), "norm_mix": (norm_mix, m_norm_mix, v_norm_mix),
                "norm_ffn2": (norm_ffn2, m_norm_ffn2, v_norm_ffn2), "norm_final": (norm_final, m_norm_final, v_norm_final),
                "q_norm": (q_norm, m_q_norm, v_q_norm), "kv_norm": (kv_norm, m_kv_norm, v_kv_norm),
                "sinks": (sinks, m_sinks, v_sinks), "rel_bias": (rel_bias, m_rel_bias, v_rel_bias),
                "b_mod": (b_mod, m_b_mod, v_b_mod)}
    as_row = lambda k, a: a.T if k == "rel_bias" else a.reshape(1, -1)
    from_row = lambda k, a: a.T if k == "rel_bias" else a.reshape(small_in[k][0].shape)
    small_out = _adamw_small(gvec, [as_row(k, a) for k in SMALL_PARAMS for a in small_in[k]])
    for k in SMALL_PARAMS:
        res[k] = tuple(from_row(k, a) for a in small_out[k])

    out = _adamw(w_mod[0], g_wmod, m_w_mod[0], v_w_mod[0], name="adamw_w_mod")
    res["w_mod"] = tuple(a[None] for a in (g_wmod,) + tuple(out))

    wmv = [tuple(there(k, a) for a in owners[k][1:]) for k in names]
    after = done + [out[2]] + [a for k in SMALL_PARAMS for a in res[k]]
    outs = []
    for i, k in enumerate(names):
        _, (sec,) = _split_wait(functools.partial(_rs_step2_copies, k0=i), send, recv, [relayed[i]], [second[i]],
                                after, name="rs_ici_wait2_" + k)
        outs.append(_adamw_rs2([wmv[i]], [sums[i]], [direct[i]], [sec], qidx, name="adamw_" + owners[k][0])[0])
        after = [outs[-1][3]]
    record(names, outs)

    order = ("w_mod", "b_mod", "norm_ffn1", "ffn1_gate", "ffn1_up", "ffn1_down", "norm_mix", "w_in", "q_norm",
             "kv_norm", "w_uq", "w_ukv", "sinks", "w_o", "norm_ffn2", "ffn2_gate", "ffn2_up", "ffn2_down",
             "rel_bias", "norm_final")
    return (loss, grad_x[None]) + tuple(res[nm][kind] for kind in range(4) for nm in order)
```

```python
import functools
import math

import numpy as np
import jax
import jax.numpy as jnp
from jax import lax
from jax.experimental import pallas as pl
from jax.experimental.pallas import tpu as pltpu

F32 = jnp.float32
BF16 = jnp.bfloat16
MESH = pl.DeviceIdType.MESH

N_DEV = 8
D = 1024
D_FF = 2816
EPS = 1e-6
N_MOD = 9
SWA_HEADS = 8
SWA_DH = 64
WINDOW = 128
MLA_HEADS = 4
MLA_NOPE = 128
MLA_ROPE = 64
MLA_V = 128
MLA_QR = 256
MLA_KVR = 128
ROPE_THETA = 10000.0
NUM_BUCKETS = 32
D_IN = 1216
D_IN_PAD = 1280
SWA_SCALE = SWA_DH ** -0.5
MLA_SCALE = (MLA_NOPE + MLA_ROPE) ** -0.5

ADAM_LR = 0.001
ADAM_B1 = 0.9
ADAM_B2 = 0.999
ADAM_EPS = 1e-08
ADAM_WD = 0.01
ADAM_STEP = 10

V7X_VMEM_LIMIT = 56 * 1024 * 1024
ROW_TILE = 512

NT_DIMS = (((1,), (1,)), ((), ()))
TN_DIMS = (((0,), (0,)), ((), ()))


def _dot(a, b):
    return jnp.dot(a, b, preferred_element_type=F32)


def _dot_nt(a, b):
    return lax.dot_general(a, b, NT_DIMS, preferred_element_type=F32)


def _dot_tn(a, b):
    return lax.dot_general(a, b, TN_DIMS, preferred_element_type=F32)


def _params(sem=None):
    return pltpu.CompilerParams(dimension_semantics=sem, vmem_limit_bytes=V7X_VMEM_LIMIT)


def _rstd(x):
    return lax.rsqrt(jnp.mean(x * x, axis=-1, keepdims=True) + EPS)


def _rms_bwd(dy, xhat, r):
    return r * (dy - xhat * jnp.mean(dy * xhat, axis=-1, keepdims=True))


def _sigmoid(a):
    return 1.0 / (1.0 + jnp.exp(-a))


def _ffn_fwd(x, vecs, wgT, wuT, wd, *, name, tm=256, tf=D_FF):
    S, F = x.shape[0], wd.shape[0]
    tm = min(tm, S)
    ni, nj = S // tm, F // tf

    def body(x_ref, vec_ref, wg_ref, wu_ref, wd_ref, xo_ref, h_ref, a_ref, b_ref, f_ref, acc_ref):
        j = pl.program_id(1)

        @pl.when(j == 0)
        def _():
            xv = x_ref[...]
            hn = xv * _rstd(xv) * vec_ref[0:1, :]
            h_ref[...] = (hn * (1.0 + vec_ref[2:3, :]) + vec_ref[1:2, :]).astype(BF16)

        h = h_ref[...]
        a = _dot_nt(h, wg_ref[...])
        b = _dot_nt(h, wu_ref[...])
        a_ref[...] = a.astype(BF16)
        b_ref[...] = b.astype(BF16)
        part = _dot((a * _sigmoid(a) * b).astype(BF16), wd_ref[...])

        def finish(f):
            f_ref[...] = f
            xo_ref[...] = x_ref[...] + (0.5 * vec_ref[3:4, :]) * f

        if nj == 1:
            finish(part)
        else:
            @pl.when(j == 0)
            def _():
                acc_ref[...] = part

            @pl.when((j > 0) & (j < nj - 1))
            def _():
                acc_ref[...] += part

            @pl.when(j == nj - 1)
            def _():
                finish(acc_ref[...] + part)

    row = pl.BlockSpec((tm, D), lambda i, j: (i, 0))
    wspec = pl.BlockSpec((tf, D), lambda i, j: (j, 0), pipeline_mode=pl.Buffered(1) if nj == 1 else None)
    act = pl.BlockSpec((tm, tf), lambda i, j: (i, j))
    return pl.pallas_call(
        body, name=name, grid=(ni, nj),
        in_specs=[row, pl.BlockSpec((8, D), lambda i, j: (0, 0)), wspec, wspec, wspec],
        out_specs=[row, row, act, act, row],
        out_shape=[jax.ShapeDtypeStruct((S, D), F32), jax.ShapeDtypeStruct((S, D), BF16),
                   jax.ShapeDtypeStruct((S, F), BF16), jax.ShapeDtypeStruct((S, F), BF16),
                   jax.ShapeDtypeStruct((S, D), F32)],
        scratch_shapes=[pltpu.VMEM((tm, D) if nj > 1 else (8, 128), F32)],
        compiler_params=_params(("parallel", "arbitrary")),
    )(x, vecs, wgT, wuT, wd)


def _ffn_bwd_main(h, df, a, b, wgT, wuT, wd, *, name, after=(), tm=2048, tf=256):
    S = h.shape[0]
    tm = min(tm, S)
    ni, nj = S // tm, D_FF // tf

    def body(h_hbm, df_hbm, a_ref, b_ref, wg_ref, wu_ref, wd_ref, *rest):
        gg_ref, gu_ref, gd_ref, dh_hbm, h_v, df_v, dh_v, gg_acc, gu_acc, gd_acc, sem = rest[len(after):]
        j = pl.program_id(0)
        i = pl.program_id(1)

        @pl.when((j == 0) & (i == 0))
        def _():
            c1 = pltpu.make_async_copy(h_hbm, h_v, sem.at[0])
            c2 = pltpu.make_async_copy(df_hbm, df_v, sem.at[1])
            c1.start()
            c2.start()
            c1.wait()
            c2.wait()

        @pl.when(i == 0)
        def _():
            gg_acc[...] = jnp.zeros_like(gg_acc)
            gu_acc[...] = jnp.zeros_like(gu_acc)
            gd_acc[...] = jnp.zeros_like(gd_acc)

        rows = pl.ds(pl.multiple_of(i * tm, tm), tm)
        hi = h_v[rows, :]
        dfi = df_v[rows, :]
        av = a_ref[...].astype(F32)
        bv = b_ref[...].astype(F32)
        sg = _sigmoid(av)
        sa = av * sg
        hsw = (sa * bv).astype(BF16)
        dhsw = _dot_nt(dfi, wd_ref[...])
        da = (dhsw * bv * (sg * (1.0 + av * (1.0 - sg)))).astype(BF16)
        db = (dhsw * sa).astype(BF16)
        gd_acc[...] += _dot_tn(hsw, dfi)
        gg_acc[...] += _dot_tn(da, hi)
        gu_acc[...] += _dot_tn(db, hi)
        dh = _dot(da, wg_ref[...]) + _dot(db, wu_ref[...])

        @pl.when(j == 0)
        def _():
            dh_v[rows, :] = dh

        @pl.when(j > 0)
        def _():
            dh_v[rows, :] += dh

        @pl.when(i == ni - 1)
        def _():
            gg_ref[...] = gg_acc[...].astype(BF16)
            gu_ref[...] = gu_acc[...].astype(BF16)
            gd_ref[...] = gd_acc[...].astype(BF16)

        @pl.when((j == nj - 1) & (i == ni - 1))
        def _():
            c3 = pltpu.make_async_copy(dh_v, dh_hbm, sem.at[2])
            c3.start()
            c3.wait()

    anyspec = pl.BlockSpec(memory_space=pl.ANY)
    wspec = pl.BlockSpec((tf, D), lambda j, i: (j, 0))
    act = pl.BlockSpec((tm, tf), lambda j, i: (i, j))
    return pl.pallas_call(
        body, name=name, grid=(nj, ni),
        in_specs=[anyspec, anyspec, act, act, wspec, wspec, wspec] + [anyspec] * len(after),
        out_specs=[wspec, wspec, wspec, anyspec],
        out_shape=[jax.ShapeDtypeStruct((D_FF, D), BF16)] * 3 + [jax.ShapeDtypeStruct((S, D), F32)],
        scratch_shapes=[pltpu.VMEM((S, D), BF16), pltpu.VMEM((S, D), BF16), pltpu.VMEM((S, D), F32),
                        pltpu.VMEM((tf, D), F32), pltpu.VMEM((tf, D), F32), pltpu.VMEM((tf, D), F32),
                        pltpu.SemaphoreType.DMA((3,))],
        compiler_params=_params(("arbitrary", "arbitrary")),
    )(h, df, a, b, wgT, wuT, wd, *after)


def _ffn_out_bwd(dx, f, gate, df_ref, part_ref):
    df_ref[...] = ((0.5 * gate) * dx).astype(BF16)
    part_ref[3:4, :] += 0.5 * jnp.sum(dx * f, axis=0, keepdims=True)


def _norm_bwd(dh, x, dxo, vecs, *, name, below=None, tm=ROW_TILE):
    S = x.shape[0]
    tm = min(tm, S)

    def body(dh_ref, x_ref, dxo_ref, vec_ref, *rest):
        dx_ref, part_ref = rest[-2 if below is None else -3], rest[-1 if below is None else -2]

        @pl.when(pl.program_id(0) == 0)
        def _():
            part_ref[...] = jnp.zeros_like(part_ref)

        dh = dh_ref[...]
        xv = x_ref[...]
        r = _rstd(xv)
        xhat = xv * r
        w = vec_ref[0:1, :]
        xn = xhat * w
        dxn = dh * (1.0 + vec_ref[2:3, :])
        part_ref[0:1, :] += jnp.sum(dxn * xhat, axis=0, keepdims=True)
        part_ref[1:2, :] += jnp.sum(dh, axis=0, keepdims=True)
        part_ref[2:3, :] += jnp.sum(dh * xn, axis=0, keepdims=True)
        dx = dxo_ref[...] + _rms_bwd(dxn * w, xhat, r)
        dx_ref[...] = dx
        if below is not None:
            _ffn_out_bwd(dx, rest[0][...], rest[1][3:4, :], rest[-1], part_ref)

    row = pl.BlockSpec((tm, D), lambda i: (i, 0))
    vec = pl.BlockSpec((8, D), lambda i: (0, 0))
    extra = [] if below is None else [row, vec]
    return pl.pallas_call(
        body, name=name, grid=(S // tm,), in_specs=[row, row, row, vec] + extra,
        out_specs=[row, vec] + ([] if below is None else [row]),
        out_shape=[jax.ShapeDtypeStruct((S, D), F32), jax.ShapeDtypeStruct((8, D), F32)]
        + ([] if below is None else [jax.ShapeDtypeStruct((S, D), BF16)]),
        compiler_params=_params(("arbitrary",)),
    )(dh, x, dxo, vecs, *([] if below is None else below))


def _head(x, tgt, nf, f, vecs, *, tm=ROW_TILE):
    S = x.shape[0]
    tm = min(tm, S)

    def body(x_ref, t_ref, nf_ref, f_ref, vec_ref, dx_ref, part_ref, df_ref):
        @pl.when(pl.program_id(0) == 0)
        def _():
            part_ref[...] = jnp.zeros_like(part_ref)

        xv = x_ref[...]
        r = _rstd(xv)
        xhat = xv * r
        w = nf_ref[...]
        e = xhat * w - t_ref[...]
        dy = e * (1.0 / D)
        part_ref[0:1, :] += jnp.sum(dy * xhat, axis=0, keepdims=True)
        part_ref[1:2, :] += jnp.sum(e * e) * (0.5 / D)
        dx = _rms_bwd(dy * w, xhat, r)
        dx_ref[...] = dx
        _ffn_out_bwd(dx, f_ref[...], vec_ref[3:4, :], df_ref, part_ref)

    row = pl.BlockSpec((tm, D), lambda i: (i, 0))
    vec = pl.BlockSpec((8, D), lambda i: (0, 0))
    return pl.pallas_call(
        body, name="head", grid=(S // tm,),
        in_specs=[row, row, pl.BlockSpec((1, D), lambda i: (0, 0)), row, vec],
        out_specs=[row, vec, row],
        out_shape=[jax.ShapeDtypeStruct((S, D), F32), jax.ShapeDtypeStruct((8, D), F32),
                   jax.ShapeDtypeStruct((S, D), BF16)],
        compiler_params=_params(("arbitrary",)),
    )(x, tgt, nf, f, vecs)


def _mix_in_fwd(x, vecs, w_inT, *, tm=ROW_TILE):
    S = x.shape[0]
    tm = min(tm, S)

    def body(x_ref, vec_ref, w_ref, h_ref, p_ref, wb_ref):
        @pl.when(pl.program_id(0) == 0)
        def _():
            wb_ref[0:D_IN, :] = w_ref[...].astype(BF16)
            wb_ref[D_IN:D_IN_PAD, :] = jnp.zeros((D_IN_PAD - D_IN, D), BF16)

        xv = x_ref[...]
        hn = xv * _rstd(xv) * vec_ref[0:1, :]
        h = (hn * (1.0 + vec_ref[2:3, :]) + vec_ref[1:2, :]).astype(BF16)
        h_ref[...] = h
        p_ref[...] = _dot_nt(h, wb_ref[...])

    row = pl.BlockSpec((tm, D), lambda i: (i, 0))
    return pl.pallas_call(
        body, name="mix_in_fwd", grid=(S // tm,),
        in_specs=[row, pl.BlockSpec((8, D), lambda i: (0, 0)),
                  pl.BlockSpec((D_IN, D), lambda i: (0, 0), pipeline_mode=pl.Buffered(1))],
        out_specs=[row, pl.BlockSpec((tm, D_IN_PAD), lambda i: (i, 0)), pl.BlockSpec((D_IN_PAD, D), lambda i: (0, 0))],
        out_shape=[jax.ShapeDtypeStruct((S, D), BF16), jax.ShapeDtypeStruct((S, D_IN_PAD), F32),
                   jax.ShapeDtypeStruct((D_IN_PAD, D), BF16)],
        compiler_params=_params(("arbitrary",)),
    )(x, vecs, w_inT)


def _bucket_table():
    qi = np.arange(WINDOW)[:, None]
    kj = np.arange(2 * WINDOW)[None, :]
    dist = qi + WINDOW - kj
    max_exact = NUM_BUCKETS // 2
    n = np.maximum(dist, 0)
    nf = np.maximum(n, 1).astype(np.float32)
    large = max_exact + (np.log(nf / np.float32(max_exact)) / np.float32(math.log(WINDOW / max_exact))
                         * np.float32(NUM_BUCKETS - max_exact)).astype(np.int32)
    large = np.minimum(large, NUM_BUCKETS - 1)
    return np.where(n < max_exact, n, large).astype(np.int32)


SWA_GROUP = 4
GROUP_ROWS = SWA_GROUP * WINDOW


SWA_SUB = 2


def _swa_valid(has_prev):
    row = lax.broadcasted_iota(jnp.int32, (GROUP_ROWS, 2 * WINDOW), 0) % WINDOW
    col = lax.broadcasted_iota(jnp.int32, (GROUP_ROWS, 2 * WINDOW), 1)
    dist = row + WINDOW - col
    return (dist >= 0) & (dist < WINDOW) & ((col >= WINDOW) | has_prev)


def _swa_keys(prev_ref, cur_ref, u):
    cur = cur_ref[...]
    before = prev_ref[...] if u == 0 else cur[WINDOW * (u - 1):WINDOW * u]
    return jnp.concatenate([before, cur[WINDOW * u:WINDOW * (u + 1)]], axis=0).astype(BF16)


def _stack_heads(x, g):
    return jnp.concatenate([x[:, 64 * h:64 * h + 64] for h in range(SWA_GROUP * g, SWA_GROUP * (g + 1))], axis=0)


def _unstack_heads(x4):
    return jnp.concatenate([x4[WINDOW * a:WINDOW * (a + 1)] for a in range(SWA_GROUP)], axis=1)


def _group_sinks(sink_ref, g):
    head = lax.broadcasted_iota(jnp.int32, (GROUP_ROWS, 1), 0) // WINDOW
    out = jnp.full((GROUP_ROWS, 1), sink_ref[0, SWA_GROUP * g], F32)
    for a in range(1, SWA_GROUP):
        out = jnp.where(head == a, sink_ref[0, SWA_GROUP * g + a], out)
    return out


def _swa_probs(qh, kk, bias_h, sink, valid):
    s = _dot_nt(qh, kk) * SWA_SCALE + bias_h
    s = jnp.where(valid, s, -jnp.inf)
    m = jnp.maximum(jnp.max(s, axis=-1, keepdims=True), sink)
    p = jnp.exp(s - m)
    ps = jnp.exp(sink - m)
    inv = 1.0 / (jnp.sum(p, axis=-1, keepdims=True) + ps)
    return p * inv, ps * inv


SWA_ROWS = SWA_SUB * WINDOW


def _swa_specs():
    prev = lambda n: jnp.maximum(SWA_SUB * n - 1, 0)
    return [pl.BlockSpec((SWA_ROWS, 512), lambda n: (n, 0)),
            pl.BlockSpec((SWA_ROWS, 128), lambda n: (n, 4)),
            pl.BlockSpec((WINDOW, 128), lambda n: (prev(n), 4)),
            pl.BlockSpec((SWA_ROWS, 128), lambda n: (n, 5)),
            pl.BlockSpec((WINDOW, 128), lambda n: (prev(n), 5)),
            pl.BlockSpec((SWA_HEADS, WINDOW, 2 * WINDOW), lambda n: (0, 0, 0)),
            pl.BlockSpec(memory_space=pltpu.SMEM)]


def _swa_fwd(proj, rel_bias, bucket, sinks):
    S = proj.shape[0]

    def body(q_ref, kc_ref, kp_ref, vc_ref, vp_ref, rb_ref, sink_ref, bk_ref, o_ref, bias_ref):
        n = pl.program_id(0)

        @pl.when(n == 0)
        def _():
            bk = bk_ref[...]
            for h in range(SWA_HEADS):
                acc = jnp.zeros((WINDOW, 2 * WINDOW), F32)
                for b in range(NUM_BUCKETS):
                    acc = jnp.where(bk == b, rb_ref[b, h], acc)
                bias_ref[h] = acc

        for u in range(SWA_SUB):
            rows = slice(WINDOW * u, WINDOW * (u + 1))
            valid = _swa_valid(n > 0 if u == 0 else True)
            q = q_ref[rows, :].astype(BF16)
            kfull = _swa_keys(kp_ref, kc_ref, u)
            vfull = _swa_keys(vp_ref, vc_ref, u)
            for g in range(SWA_HEADS // SWA_GROUP):
                kk = kfull[:, 64 * g:64 * g + 64]
                vv = vfull[:, 64 * g:64 * g + 64]
                bias4 = bias_ref[SWA_GROUP * g:SWA_GROUP * (g + 1)].reshape(GROUP_ROWS, 2 * WINDOW)
                pk, _ = _swa_probs(_stack_heads(q, g), kk, bias4, _group_sinks(sink_ref, g), valid)
                o_ref[rows, 256 * g:256 * (g + 1)] = _unstack_heads(_dot(pk.astype(BF16), vv))

    specs = _swa_specs()
    whole = pl.BlockSpec((SWA_HEADS, WINDOW, 2 * WINDOW), lambda n: (0, 0, 0))
    return pl.pallas_call(
        body, name="swa_fwd", grid=(S // SWA_ROWS,),
        in_specs=specs[:5] + [pl.BlockSpec(memory_space=pltpu.SMEM), specs[6],
                              pl.BlockSpec((WINDOW, 2 * WINDOW), lambda n: (0, 0))],
        out_specs=[pl.BlockSpec((SWA_ROWS, 512), lambda n: (n, 0)), whole],
        out_shape=[jax.ShapeDtypeStruct((S, 512), F32), jax.ShapeDtypeStruct((SWA_HEADS, WINDOW, 2 * WINDOW), F32)],
        compiler_params=_params(("arbitrary",)),
    )(proj, proj, proj, proj, proj, rel_bias, sinks, bucket)


def _swa_bwd(proj, bias, sinks, o, do, bucket):
    S = proj.shape[0]
    nb = S // SWA_ROWS

    def body(q_ref, kc_ref, kp_ref, vc_ref, vp_ref, bias_ref, sink_ref, o_ref, do_ref, bk_ref,
             dq_ref, dk_ref, dv_ref, drb_ref, dsk_ref, dbias_acc):
        n = pl.program_id(0)

        @pl.when(n == 0)
        def _():
            dk_ref[...] = jnp.zeros_like(dk_ref)
            dv_ref[...] = jnp.zeros_like(dv_ref)
            dsk_ref[...] = jnp.zeros_like(dsk_ref)
            dbias_acc[...] = jnp.zeros_like(dbias_acc)
            drb_ref[...] = jnp.zeros_like(drb_ref)

        for u in range(SWA_SUB):
            rows = slice(WINDOW * u, WINDOW * (u + 1))
            blk = SWA_SUB * n + u
            valid = _swa_valid(n > 0 if u == 0 else True)
            q = q_ref[rows, :].astype(BF16)
            dov = do_ref[rows, :]
            ov = o_ref[rows, :]
            kfull = _swa_keys(kp_ref, kc_ref, u)
            vfull = _swa_keys(vp_ref, vc_ref, u)
            prow = pl.ds(pl.multiple_of(jnp.maximum(blk - 1, 0) * WINDOW, WINDOW), WINDOW)
            crow = pl.ds(pl.multiple_of(blk * WINDOW, WINDOW), WINDOW)
            for g in range(SWA_HEADS // SWA_GROUP):
                heads = slice(SWA_GROUP * g, SWA_GROUP * (g + 1))
                kk = kfull[:, 64 * g:64 * g + 64]
                vv = vfull[:, 64 * g:64 * g + 64]
                q4 = _stack_heads(q, g)
                pk, psink = _swa_probs(q4, kk, bias_ref[heads].reshape(GROUP_ROWS, 2 * WINDOW),
                                       _group_sinks(sink_ref, g), valid)
                pkb = pk.astype(BF16)
                do4 = _stack_heads(dov, g)
                dob = do4.astype(BF16)
                dp = _dot_nt(dob, vv)
                delta = jnp.sum(do4 * _stack_heads(ov, g), axis=-1, keepdims=True)
                ds = pk * (dp - delta)
                dsink = -psink * delta
                for a in range(SWA_GROUP):
                    h = SWA_GROUP * g + a
                    part = jnp.sum(dsink[WINDOW * a:WINDOW * (a + 1)], keepdims=True)
                    dsk_ref[h:h + 1, :] += jnp.broadcast_to(part, (1, 128))
                dbias_acc[heads] += ds.reshape(SWA_GROUP, WINDOW, 2 * WINDOW)
                dsb = (ds * SWA_SCALE).astype(BF16)
                dq_ref[rows, 256 * g:256 * (g + 1)] = _unstack_heads(_dot(dsb, kk))
                dkk = _dot_tn(dsb, q4)
                dvv = _dot_tn(pkb, dob)
                dk_ref[prow, 64 * g:64 * g + 64] += dkk[:WINDOW]
                dk_ref[crow, 64 * g:64 * g + 64] += dkk[WINDOW:]
                dv_ref[prow, 64 * g:64 * g + 64] += dvv[:WINDOW]
                dv_ref[crow, 64 * g:64 * g + 64] += dvv[WINDOW:]

        @pl.when(n == nb - 1)
        def _():
            bk = bk_ref[...]
            for h in range(SWA_HEADS):
                dbh = dbias_acc[h]
                for b in range(NUM_BUCKETS):
                    val = jnp.sum(jnp.where(bk == b, dbh, 0.0), keepdims=True)
                    row = h * NUM_BUCKETS + b
                    drb_ref[row:row + 1, :] = jnp.broadcast_to(val, (1, 128))

    full = lambda shape: pl.BlockSpec(shape, lambda n: tuple(0 for _ in shape))
    return pl.pallas_call(
        body, name="swa_bwd", grid=(nb,),
        in_specs=_swa_specs() + [pl.BlockSpec((SWA_ROWS, 512), lambda n: (n, 0)),
                                 pl.BlockSpec((SWA_ROWS, 512), lambda n: (n, 0)), full((WINDOW, 2 * WINDOW))],
        out_specs=[pl.BlockSpec((SWA_ROWS, 512), lambda n: (n, 0)), full((S, 128)), full((S, 128)),
                   full((NUM_BUCKETS * 8, 128)), full((8, 128))],
        out_shape=[jax.ShapeDtypeStruct((S, 512), F32), jax.ShapeDtypeStruct((S, 128), F32),
                   jax.ShapeDtypeStruct((S, 128), F32), jax.ShapeDtypeStruct((NUM_BUCKETS * 8, 128), F32),
                   jax.ShapeDtypeStruct((8, 128), F32)],
        scratch_shapes=[pltpu.VMEM((SWA_HEADS, WINDOW, 2 * WINDOW), F32)],
        compiler_params=_params(("arbitrary",)),
    )(proj, proj, proj, proj, proj, bias, sinks, o, do, bucket)


def _rope_tables(S):
    inv = np.float32(ROPE_THETA) ** (-np.arange(0, MLA_ROPE, 2, dtype=np.float32) / np.float32(MLA_ROPE))
    ang = np.arange(S, dtype=np.float32)[:, None] * inv[None, :]
    cos, sin = np.cos(ang), np.sin(ang)
    return (jnp.asarray(np.tile(np.concatenate([cos, cos], axis=1), (1, 2))),
            jnp.asarray(np.tile(np.concatenate([-sin, sin], axis=1), (1, 2))))


def _rope_wide(ref):
    t = ref[...]
    return jnp.concatenate([t, t], axis=1)


def _swap_halves(x):
    w = x.shape[-1]
    lane = lax.broadcasted_iota(jnp.int32, x.shape, x.ndim - 1)
    return jnp.where((lane % 64) < 32, pltpu.roll(x, w - 32, x.ndim - 1), pltpu.roll(x, 32, x.ndim - 1))


def _mla_pre_fwd(proj, qn_w, kvn_w, wuqT, wukv, cos, sin, *, tm=ROW_TILE):
    S = proj.shape[0]
    tm = min(tm, S)

    def body(ql_ref, kl_ref, kr_ref, qw_ref, kw_ref, wuq_ref, wukv_ref, cos_ref, sin_ref,
             qc_ref, kc_ref, vv_ref):
        ql = ql_ref[...]
        qn = (ql * _rstd(ql) * qw_ref[...]).astype(BF16)
        q = _dot_nt(qn, wuq_ref[...])
        cs, sn = _rope_wide(cos_ref), _rope_wide(sin_ref)
        qr = q[:, 512:768]
        qr = qr * cs + _swap_halves(qr) * sn
        half = lax.broadcasted_iota(jnp.int32, (tm, 128), 1) // 64
        kl = kl_ref[...]
        kvn = (kl * _rstd(kl) * kw_ref[...]).astype(BF16)
        kr = kr_ref[...]
        kr = kr * cs[:, :128] + _swap_halves(kr) * sn[:, :128]
        kr2 = (kr + pltpu.roll(kr, 64, 1)).astype(BF16)
        kv = _dot(kvn, jnp.concatenate([wukv_ref[j] for j in range(2 * MLA_HEADS)], axis=1)).astype(BF16)
        for h in range(MLA_HEADS):
            qc_ref[h, :, 0:128] = q[:, 128 * h:128 * h + 128].astype(BF16)
            chunk = qr[:, 128 * (h // 2):128 * (h // 2) + 128]
            qc_ref[h, :, 128:256] = jnp.where(half == (h % 2), chunk, 0.0).astype(BF16)
            kc_ref[h, :, 0:128] = kv[:, 256 * h:256 * h + 128]
            kc_ref[h, :, 128:256] = kr2
            vv_ref[h] = kv[:, 256 * h + 128:256 * h + 256]

    const = lambda shape: pl.BlockSpec(shape, lambda i: tuple(0 for _ in shape))
    return pl.pallas_call(
        body, name="mla_pre_fwd", grid=(S // tm,),
        in_specs=[pl.BlockSpec((tm, 256), lambda i: (i, 3)), pl.BlockSpec((tm, 128), lambda i: (i, 8)),
                  pl.BlockSpec((tm, 128), lambda i: (i, 9)), const((1, 256)), const((1, 128)),
                  const((768, 256)), const((8, 128, 128)),
                  pl.BlockSpec((tm, 128), lambda i: (i, 0)), pl.BlockSpec((tm, 128), lambda i: (i, 0))],
        out_specs=[pl.BlockSpec((MLA_HEADS, tm, 256), lambda i: (0, i, 0)),
                   pl.BlockSpec((MLA_HEADS, tm, 256), lambda i: (0, i, 0)),
                   pl.BlockSpec((MLA_HEADS, tm, 128), lambda i: (0, i, 0))],
        out_shape=[jax.ShapeDtypeStruct((MLA_HEADS, S, 256), BF16), jax.ShapeDtypeStruct((MLA_HEADS, S, 256), BF16),
                   jax.ShapeDtypeStruct((MLA_HEADS, S, 128), BF16)],
        compiler_params=_params(("parallel",)),
    )(proj, proj, proj, qn_w, kvn_w, wuqT, wukv, cos, sin)


def _causal(i, j, t):
    row = i * t + lax.broadcasted_iota(jnp.int32, (t, t), 0)
    col = j * t + lax.broadcasted_iota(jnp.int32, (t, t), 1)
    return col <= row


def _mla_attn_fwd(qc, kc, vv, *, t=512):
    S = qc.shape[1]
    t = min(t, S)

    def body(q_ref, k_ref, v_ref, o_ref, l_ref):
        i = pl.program_id(0)
        diag = _causal(0, 0, t)

        def step(j, carry, masked):
            rows = pl.ds(pl.multiple_of(j * t, t), t)
            out = []
            for h in range(MLA_HEADS):
                m, l, acc = carry[h]
                s = _dot_nt(q_ref[h], k_ref[h, rows, :]) * MLA_SCALE
                if masked:
                    s = jnp.where(diag, s, -jnp.inf)
                m_new = jnp.maximum(m, jnp.max(s, axis=-1, keepdims=True))
                alpha = jnp.exp(m - m_new)
                p = jnp.exp(s - m_new)
                l = alpha * l + jnp.sum(p, axis=-1, keepdims=True)
                acc = alpha * acc + _dot(p.astype(BF16), v_ref[h, rows, :])
                out.append((m_new, l, acc))
            return tuple(out)

        init = tuple((jnp.full((t, 1), -jnp.inf, F32), jnp.zeros((t, 1), F32), jnp.zeros((t, MLA_V), F32))
                     for _ in range(MLA_HEADS))
        carry = lax.fori_loop(0, i, lambda j, c: step(j, c, False), init)
        carry = step(i, carry, True)
        for h in range(MLA_HEADS):
            m, l, acc = carry[h]
            o_ref[:, 128 * h:128 * h + 128] = acc / l
            l_ref[h] = jnp.broadcast_to(m + jnp.log(l), (t, 128))

    return pl.pallas_call(
        body, name="mla_attn_fwd", grid=(S // t,),
        in_specs=[pl.BlockSpec((MLA_HEADS, t, 256), lambda i: (0, i, 0)),
                  pl.BlockSpec((MLA_HEADS, S, 256), lambda i: (0, 0, 0)),
                  pl.BlockSpec((MLA_HEADS, S, 128), lambda i: (0, 0, 0))],
        out_specs=[pl.BlockSpec((t, 512), lambda i: (i, 0)),
                   pl.BlockSpec((MLA_HEADS, t, 128), lambda i: (0, i, 0))],
        out_shape=[jax.ShapeDtypeStruct((S, 512), F32), jax.ShapeDtypeStruct((MLA_HEADS, S, 128), F32)],
        compiler_params=_params(("parallel",)),
    )(qc, kc, vv)


def _mla_attn_bwd(qc, kc, vv, o, lse, do, *, t=512, tq=1024):
    S = qc.shape[1]
    t = min(t, S)
    tq = min(tq, S)
    nblk = S // t
    hp = MLA_HEADS
    once = pl.Buffered(1)

    def body(q_ref, k_ref, v_ref, o_ref, l_ref, do_ref, dq_ref, dk_ref, dv_ref):
        j = pl.program_id(1)

        @pl.when(j == 0)
        def _():
            dq_ref[...] = jnp.zeros_like(dq_ref)

        first = (j * t) // tq

        def step(i, carry, masked):
            rows = pl.ds(pl.multiple_of(i * tq, tq), tq)
            if masked:
                row = i * tq + lax.broadcasted_iota(jnp.int32, (tq, t), 0)
                col = j * t + lax.broadcasted_iota(jnp.int32, (tq, t), 1)
                visible = col <= row
            out = []
            for h in range(hp):
                dk, dv = carry[h]
                k = k_ref[h]
                q = q_ref[h, rows, :]
                dov = do_ref[rows, 128 * h:128 * h + 128]
                lrow = l_ref[h, rows, :][:, 0:1]
                p = jnp.exp(_dot_nt(q, k) * MLA_SCALE - lrow)
                if masked:
                    p = jnp.where(visible, p, 0.0)
                dob = dov.astype(BF16)
                dv = dv + _dot_tn(p.astype(BF16), dob)
                dp = _dot_nt(dob, v_ref[h])
                delta = jnp.sum(dov * o_ref[rows, 128 * h:128 * h + 128], axis=-1, keepdims=True)
                ds = (p * (dp - delta) * MLA_SCALE).astype(BF16)
                dk = dk + _dot_tn(ds, q)
                dq_ref[h, rows, :] += _dot(ds, k)
                out.append((dk, dv))
            return tuple(out)

        init = tuple((jnp.zeros((t, 256), F32), jnp.zeros((t, MLA_V), F32)) for _ in range(hp))
        carry = step(first, init, True)
        carry = lax.fori_loop(first + 1, S // tq, lambda i, c: step(i, c, False), carry)
        for h in range(hp):
            dk_ref[h] = carry[h][0]
            dv_ref[h] = carry[h][1]

    return pl.pallas_call(
        body, name="mla_attn_bwd", grid=(MLA_HEADS // hp, nblk),
        in_specs=[pl.BlockSpec((hp, S, 256), lambda g, j: (g, 0, 0), pipeline_mode=once),
                  pl.BlockSpec((hp, t, 256), lambda g, j: (g, j, 0)),
                  pl.BlockSpec((hp, t, 128), lambda g, j: (g, j, 0)),
                  pl.BlockSpec((S, 128 * hp), lambda g, j: (0, g), pipeline_mode=once),
                  pl.BlockSpec((hp, S, 128), lambda g, j: (g, 0, 0), pipeline_mode=once),
                  pl.BlockSpec((S, 128 * hp), lambda g, j: (0, g), pipeline_mode=once)],
        out_specs=[pl.BlockSpec((hp, S, 256), lambda g, j: (g, 0, 0)),
                   pl.BlockSpec((hp, t, 256), lambda g, j: (g, j, 0)),
                   pl.BlockSpec((hp, t, 128), lambda g, j: (g, j, 0))],
        out_shape=[jax.ShapeDtypeStruct((MLA_HEADS, S, 256), F32), jax.ShapeDtypeStruct((MLA_HEADS, S, 256), F32),
                   jax.ShapeDtypeStruct((MLA_HEADS, S, 128), F32)],
        compiler_params=_params(("parallel", "arbitrary")),
    )(qc, kc, vv, o, lse, do)


def _mla_pre_bwd(proj, qn_w, kvn_w, wuqT, wukv, cos, sin, dqc, dkc, dvv, *, tm=ROW_TILE):
    S = proj.shape[0]
    tm = min(tm, S)

    def body(ql_ref, kl_ref, qw_ref, kw_ref, wuq_ref, wukv_ref, cos_ref, sin_ref, dqc_ref, dkc_ref, dvv_ref,
             dql_ref, dkl_ref, dkr_ref, gq_ref, gkv_ref, part_ref, gq_acc, gkv_acc):
        @pl.when(pl.program_id(0) == 0)
        def _():
            gq_acc[...] = jnp.zeros_like(gq_acc)
            gkv_acc[...] = jnp.zeros_like(gkv_acc)
            part_ref[...] = jnp.zeros_like(part_ref)

        cs, sn = _rope_wide(cos_ref), _rope_wide(sin_ref)
        half = lax.broadcasted_iota(jnp.int32, (tm, 128), 1) // 64
        ql = ql_ref[...]
        rq = _rstd(ql)
        qhat = ql * rq
        qw = qw_ref[...]
        qn = (qhat * qw).astype(BF16)
        chunks = []
        for pair in range(2):
            chunks.append(jnp.where(half == 0, dqc_ref[2 * pair, :, 128:256], dqc_ref[2 * pair + 1, :, 128:256]))
        dqr = jnp.concatenate(chunks, axis=1)
        dqr = dqr * cs + _swap_halves(dqr * sn)
        dq = jnp.concatenate([dqc_ref[h, :, 0:128] for h in range(MLA_HEADS)] + [dqr], axis=1).astype(BF16)
        gq_acc[...] += _dot_tn(dq, qn)
        dqn = _dot(dq, wuq_ref[...])
        part_ref[0:1, :] += jnp.sum(dqn * qhat, axis=0, keepdims=True)
        dql_ref[...] = _rms_bwd(dqn * qw, qhat, rq)
        kl = kl_ref[...]
        rk = _rstd(kl)
        khat = kl * rk
        kw = kw_ref[...]
        kvn = (khat * kw).astype(BF16)
        dkr2 = jnp.zeros((tm, 128), F32)
        dkv = []
        for h in range(MLA_HEADS):
            dkv += [dkc_ref[h, :, 0:128].astype(BF16), dvv_ref[h].astype(BF16)]
            dkr2 += dkc_ref[h, :, 128:256]
        dkv = jnp.concatenate(dkv, axis=1)
        gkv = _dot_tn(kvn, dkv)
        for j in range(2 * MLA_HEADS):
            gkv_acc[j] += gkv[:, 128 * j:128 * (j + 1)]
        dkvn = _dot_nt(dkv, jnp.concatenate([wukv_ref[j] for j in range(2 * MLA_HEADS)], axis=1))
        part_ref[1:2, 0:128] += jnp.sum(dkvn * khat, axis=0, keepdims=True)
        dkl_ref[...] = _rms_bwd(dkvn * kw, khat, rk)
        dkr = jnp.where(half == 0, dkr2 + pltpu.roll(dkr2, 64, 1), 0.0)
        dkr_ref[...] = dkr * cs[:, :128] + _swap_halves(dkr * sn[:, :128])

        @pl.when(pl.program_id(0) == S // tm - 1)
        def _():
            gkv_ref[...] = gkv_acc[...].astype(BF16)
            per = MLA_NOPE + MLA_ROPE
            for h in range(MLA_HEADS):
                gq_ref[per * h:per * h + MLA_NOPE, :] = gq_acc[MLA_NOPE * h:MLA_NOPE * (h + 1), :].astype(BF16)
                gq_ref[per * h + MLA_NOPE:per * (h + 1), :] = gq_acc[512 + MLA_ROPE * h:512 + MLA_ROPE * (h + 1), :].astype(BF16)

    const = lambda shape: pl.BlockSpec(shape, lambda i: tuple(0 for _ in shape))
    heads = lambda w: pl.BlockSpec((MLA_HEADS, tm, w), lambda i: (0, i, 0))
    return pl.pallas_call(
        body, name="mla_pre_bwd", grid=(S // tm,),
        in_specs=[pl.BlockSpec((tm, 256), lambda i: (i, 3)), pl.BlockSpec((tm, 128), lambda i: (i, 8)),
                  const((1, 256)), const((1, 128)), const((768, 256)), const((8, 128, 128)),
                  pl.BlockSpec((tm, 128), lambda i: (i, 0)), pl.BlockSpec((tm, 128), lambda i: (i, 0)),
                  heads(256), heads(256), heads(128)],
        out_specs=[pl.BlockSpec((tm, 256), lambda i: (i, 0)), pl.BlockSpec((tm, 128), lambda i: (i, 0)),
                   pl.BlockSpec((tm, 128), lambda i: (i, 0)), const((768, 256)), const((8, 128, 128)), const((8, 256))],
        out_shape=[jax.ShapeDtypeStruct((S, 256), F32), jax.ShapeDtypeStruct((S, 128), F32),
                   jax.ShapeDtypeStruct((S, 128), F32), jax.ShapeDtypeStruct((768, 256), BF16),
                   jax.ShapeDtypeStruct((8, 128, 128), BF16), jax.ShapeDtypeStruct((8, 256), F32)],
        scratch_shapes=[pltpu.VMEM((768, 256), F32), pltpu.VMEM((8, 128, 128), F32)],
        compiler_params=_params(("arbitrary",)),
    )(proj, proj, qn_w, kvn_w, wuqT, wukv, cos, sin, dqc, dkc, dvv)


def _mix_out_fwd(x, oa, ob, w_o, vecs, *, tm=ROW_TILE):
    S = x.shape[0]
    tm = min(tm, S)

    def body(x_ref, oa_ref, ob_ref, w_ref, vec_ref, xo_ref, mo_ref):
        mo = _dot(oa_ref[...].astype(BF16), w_ref[0:512, :]) + _dot(ob_ref[...].astype(BF16), w_ref[512:1024, :])
        mo_ref[...] = mo
        xo_ref[...] = x_ref[...] + vec_ref[3:4, :] * mo

    row = pl.BlockSpec((tm, D), lambda i: (i, 0))
    half = pl.BlockSpec((tm, 512), lambda i: (i, 0))
    return pl.pallas_call(
        body, name="mix_out_fwd", grid=(S // tm,),
        in_specs=[row, half, half, pl.BlockSpec((D, D), lambda i: (0, 0)), pl.BlockSpec((8, D), lambda i: (0, 0))],
        out_specs=[row, row],
        out_shape=[jax.ShapeDtypeStruct((S, D), F32), jax.ShapeDtypeStruct((S, D), F32)],
        compiler_params=_params(("parallel",)),
    )(x, oa, ob, w_o, vecs)


def _mix_out_bwd(dxo, mo, oa, ob, w_o, vecs, *, tm=ROW_TILE):
    S = dxo.shape[0]
    tm = min(tm, S)

    def body(dx_ref, mo_ref, oa_ref, ob_ref, w_ref, vec_ref, doa_ref, dob_ref, gw_ref, part_ref, gw_acc):
        @pl.when(pl.program_id(0) == 0)
        def _():
            gw_acc[...] = jnp.zeros_like(gw_acc)
            part_ref[...] = jnp.zeros_like(part_ref)

        dx = dx_ref[...]
        part_ref[0:1, :] += jnp.sum(dx * mo_ref[...], axis=0, keepdims=True)
        dmo = (vec_ref[3:4, :] * dx).astype(BF16)
        doa_ref[...] = _dot_nt(dmo, w_ref[0:512, :])
        dob_ref[...] = _dot_nt(dmo, w_ref[512:1024, :])
        gw_acc[0:512, :] += _dot_tn(oa_ref[...].astype(BF16), dmo)
        gw_acc[512:1024, :] += _dot_tn(ob_ref[...].astype(BF16), dmo)

        @pl.when(pl.program_id(0) == S // tm - 1)
        def _():
            gw_ref[...] = gw_acc[...].astype(BF16)

    row = pl.BlockSpec((tm, D), lambda i: (i, 0))
    half = pl.BlockSpec((tm, 512), lambda i: (i, 0))
    return pl.pallas_call(
        body, name="mix_out_bwd", grid=(S // tm,),
        in_specs=[row, row, half, half, pl.BlockSpec((D, D), lambda i: (0, 0)), pl.BlockSpec((8, D), lambda i: (0, 0))],
        out_specs=[half, half, pl.BlockSpec((D, D), lambda i: (0, 0)), pl.BlockSpec((8, D), lambda i: (0, 0))],
        out_shape=[jax.ShapeDtypeStruct((S, 512), F32), jax.ShapeDtypeStruct((S, 512), F32),
                   jax.ShapeDtypeStruct((D, D), BF16), jax.ShapeDtypeStruct((8, D), F32)],
        scratch_shapes=[pltpu.VMEM((D, D), F32)],
        compiler_params=_params(("arbitrary",)),
    )(dxo, mo, oa, ob, w_o, vecs)


def _mix_in_bwd(h, w_inT, dq, dk, dv, dql, dkl, dkr, *, tm=ROW_TILE):
    S = h.shape[0]
    tm = min(tm, S)
    wid = (512, 128, 128, 256, 128, 128)

    def body(h_ref, w_ref, dq_ref, dk_ref, dv_ref, dql_ref, dkl_ref, dkr_ref, dh_ref, gw_ref):
        @pl.when(pl.program_id(0) == 0)
        def _():
            gw_ref[...] = jnp.zeros_like(gw_ref)

        parts = (dq_ref, dk_ref, dv_ref, dql_ref, dkl_ref, dkr_ref)
        dproj = jnp.concatenate([ref[...].astype(BF16) for ref in parts], axis=1)
        dh_ref[...] = _dot(dproj, w_ref[...])
        gw_ref[...] += _dot_tn(dproj, h_ref[...])[0:D_IN, :]

    row = pl.BlockSpec((tm, D), lambda i: (i, 0))
    part = lambda w: pl.BlockSpec((tm, w), lambda i: (i, 0))
    return pl.pallas_call(
        body, name="mix_in_bwd", grid=(S // tm,),
        in_specs=[row, pl.BlockSpec((D_IN_PAD, D), lambda i: (0, 0))] + [part(w) for w in wid],
        out_specs=[row, pl.BlockSpec((D_IN, D), lambda i: (0, 0))],
        out_shape=[jax.ShapeDtypeStruct((S, D), F32), jax.ShapeDtypeStruct((D_IN, D), F32)],
        compiler_params=_params(("arbitrary",)),
    )(h, w_inT, dq, dk, dv, dql, dkl, dkr)


def _vecs(norm_w, mod9, k):
    return jnp.concatenate([norm_w.reshape(1, D), mod9[3 * k:3 * k + 3], jnp.zeros((4, D), F32)], axis=0)


def _uq_group_rows(wuqT):
    per = MLA_NOPE + MLA_ROPE
    nope = [wuqT[per * h:per * h + MLA_NOPE] for h in range(MLA_HEADS)]
    rope = [wuqT[per * h + MLA_NOPE:per * (h + 1)] for h in range(MLA_HEADS)]
    return jnp.concatenate(nope + rope, axis=0)


def _local_step(x, tgt, mod9, norms, sinks, rel_bias, q_norm, kv_norm, W, on_grads=None):
    if on_grads is None:
        on_grads = lambda group, grads, after, vecs: vecs
    S = x.shape[0]
    v1 = _vecs(norms["ffn1"], mod9, 0)
    v2 = _vecs(norms["mix"], mod9, 1)
    v3 = _vecs(norms["ffn2"], mod9, 2)
    bucket = jnp.asarray(_bucket_table())
    cos, sin = _rope_tables(S)
    if isinstance(W, dict):
        full, W = W, (lambda group, after, vecs: (full, vecs))

    W1, v1 = W("ffn1", [], v1)
    x1, h1, a1, b1, f1 = _ffn_fwd(x, v1, W1["g1T"], W1["u1T"], W1["d1"], name="ffn1_fwd")
    W2, v2 = W("mixer", [x1], v2)
    wuqT = _uq_group_rows(W2["w_uqT"])
    h2, proj, w_inT = _mix_in_fwd(x1, v2, W2["w_inT"])
    oa, bias = _swa_fwd(proj, rel_bias, bucket, sinks)
    qc, kc, vv = _mla_pre_fwd(proj, q_norm, kv_norm, wuqT, W2["w_ukv"], cos, sin)
    ob, lse = _mla_attn_fwd(qc, kc, vv)
    _, v2o = W("ffn2_on_its_way", [ob], v2)
    x2, mo = _mix_out_fwd(x1, oa, ob, W2["w_o"], v2o)
    W3, v3 = W("ffn2", [x2], v3)
    x3, h3, a3, b3, f3 = _ffn_fwd(x2, v3, W3["g3T"], W3["u3T"], W3["d3"], name="ffn2_fwd")
    dx3, head_part, df3 = _head(x3, tgt, norms["final"], f3, v3)

    gg3, gu3, gd3, dh3 = _ffn_bwd_main(h3, df3, a3, b3, W3["g3T"], W3["u3T"], W3["d3"], name="ffn2_bwd")
    ffn2 = {"g3T": gg3, "u3T": gu3, "d3": gd3}
    v3 = on_grads("ffn2", ffn2, [], v3)
    dx2, n3_part = _norm_bwd(dh3, x2, dx3, v3, name="ffn2_norm_bwd")
    v2 = on_grads("ffn2", None, [dx2], v2)
    doa, dob, g_wo, g2_part = _mix_out_bwd(dx2, mo, oa, ob, W2["w_o"], v2)
    dq, dk, dv, drb, dsk = _swa_bwd(proj, bias, sinks, oa, doa, bucket)
    dqc, dkc, dvv = _mla_attn_bwd(qc, kc, vv, ob, lse, dob)
    dql, dkl, dkr, g_uq, g_ukv, mla_part = _mla_pre_bwd(proj, q_norm, kv_norm, wuqT, W2["w_ukv"], cos, sin, dqc, dkc, dvv)
    dh2, g_win = _mix_in_bwd(h2, w_inT, dq, dk, dv, dql, dkl, dkr)
    mixer = {"w_inT": g_win, "w_uqT": g_uq, "w_ukv": g_ukv, "w_o": g_wo}
    v2 = on_grads("mixer", mixer, [], v2)
    dx1, n2_part, df1 = _norm_bwd(dh2, x1, dx2, v2, name="mix_norm_bwd", below=(f1, v1))
    started = on_grads("mixer", None, [dx1], jnp.zeros((1, 1), F32))
    gg1, gu1, gd1, dh1 = _ffn_bwd_main(h1, df1, a1, b1, W1["g1T"], W1["u1T"], W1["d1"], name="ffn1_bwd",
                                       after=[started])
    ffn1 = {"g1T": gg1, "u1T": gu1, "d1": gd1}
    v1 = on_grads("ffn1", ffn1, [], v1)
    dx0, n1_part = _norm_bwd(dh1, x, dx1, v1, name="ffn1_norm_bwd")

    grads = {**ffn1, **ffn2, **mixer}
    return head_part[1, 0], dx0, grads, _pack_vec(n1_part, n2_part, n3_part, head_part, g2_part, mla_part, dsk, drb)


SMALL_LAYOUT = (("norm_ffn1", 1024), ("norm_mix", 1024), ("norm_ffn2", 1024), ("norm_final", 1024),
                ("q_norm", 256), ("kv_norm", 128), ("sinks", 128), ("rel_bias", 256))
N_SMALL = sum(n for _, n in SMALL_LAYOUT)
LOSS_SLOT = 4 * 1024 + 256 + 128 + SWA_HEADS
N_MODVEC = N_MOD * D
N_VEC = N_MODVEC + N_SMALL


def _pack_vec(n1, n2, n3, head, g2, mla, dsk, drb):
    def body(n1_ref, n2_ref, n3_ref, head_ref, g2_ref, mla_ref, dsk_ref, drb_ref, out_ref):
        rows = [n1_ref[1:2, :], n1_ref[2:3, :], n2_ref[3:4, :], n2_ref[1:2, :], n2_ref[2:3, :], g2_ref[0:1, :],
                n3_ref[1:2, :], n3_ref[2:3, :], head_ref[3:4, :],
                n1_ref[0:1, :], n2_ref[0:1, :], n3_ref[0:1, :], head_ref[0:1, :]]
        for i, row in enumerate(rows):
            out_ref[:, D * i:D * (i + 1)] = row
        off = D * len(rows)
        out_ref[:, off:off + 256] = mla_ref[0:1, :]
        out_ref[:, off + 256:off + 384] = mla_ref[1:2, 0:128]

        def diagonal(block):
            r = lax.broadcasted_iota(jnp.int32, block.shape, 0)
            lane = lax.broadcasted_iota(jnp.int32, block.shape, 1)
            return jnp.sum(jnp.where(r == lane, block, 0.0), axis=0, keepdims=True)

        lane = lax.broadcasted_iota(jnp.int32, (1, 128), 1)
        out_ref[:, off + 384:off + 512] = jnp.where(lane == SWA_HEADS, head_ref[1:2, 0:128], diagonal(dsk_ref[...]))
        out_ref[:, off + 512:off + 640] = diagonal(drb_ref[0:128, :])
        out_ref[:, off + 640:off + 768] = diagonal(drb_ref[128:256, :])

    vm = pl.BlockSpec(memory_space=pltpu.VMEM)
    return pl.pallas_call(body, name="pack_vec", in_specs=[vm] * 8, out_specs=vm,
                          out_shape=jax.ShapeDtypeStruct((1, N_VEC), F32))(n1, n2, n3, head, g2, mla, dsk, drb)


def _coords():
    return lax.axis_index("x"), lax.axis_index("y"), lax.axis_index("c")


def _flip(v, bit):
    return 1 - v if bit else v


def _peer(r):
    x, y, c = _coords()
    return (_flip(x, r & 4), _flip(y, r & 2), _flip(c, r & 1))


def _mod_fwd(c_tile, w_mod, b_mod3):
    W = w_mod.shape[1]

    def body(c_ref, w_ref, b_ref, mod_ref, ca_ref, call_ref, part_ref, send_sems, recv_sems):
        x, y, c = _coords()
        me = 4 * x + 2 * y + c
        call_ref[me] = c_ref[...]
        sends = []
        for r in range(1, N_DEV):
            cp = pltpu.make_async_remote_copy(c_ref, call_ref.at[me], send_sems.at[0, r], recv_sems.at[0, r],
                                              device_id=_peer(r), device_id_type=MESH)
            cp.start()
            sends.append(cp)
        for r in range(1, N_DEV):
            pltpu.make_async_remote_copy(c_ref, call_ref.at[me], send_sems.at[0, r], recv_sems.at[0, r],
                                         device_id=_peer(r), device_id_type=MESH).wait_recv()
        cv = call_ref[...].reshape(8 * N_DEV, D)
        ca = (cv * _sigmoid(cv)).astype(BF16)
        ca_ref[...] = ca
        part_ref[...] = _dot(ca, w_ref[...].astype(BF16)).reshape(N_DEV, 8, W)
        mod_ref[me] = part_ref[me] + b_ref[me]
        for r in range(1, N_DEV):
            cp = pltpu.make_async_remote_copy(part_ref.at[me ^ r], mod_ref.at[me], send_sems.at[1, r],
                                              recv_sems.at[1, r], device_id=_peer(r), device_id_type=MESH)
            cp.start()
            sends.append(cp)
        for r in range(1, N_DEV):
            pltpu.make_async_remote_copy(part_ref.at[me ^ r], mod_ref.at[me], send_sems.at[1, r],
                                         recv_sems.at[1, r], device_id=_peer(r), device_id_type=MESH).wait_recv()
            mod_ref[me ^ r] = mod_ref[me ^ r] + b_ref[me ^ r]
        for cp in sends:
            cp.wait_send()

    vm = pl.BlockSpec(memory_space=pltpu.VMEM)
    return pl.pallas_call(
        body, name="mod_fwd", in_specs=[vm, vm, vm], out_specs=[vm, vm],
        out_shape=[jax.ShapeDtypeStruct((N_DEV, 8, W), F32), jax.ShapeDtypeStruct((8 * N_DEV, D), BF16)],
        scratch_shapes=[pltpu.VMEM((N_DEV, 8, D), F32), pltpu.VMEM((N_DEV, 8, W), F32),
                        pltpu.SemaphoreType.DMA((2, N_DEV)), pltpu.SemaphoreType.DMA((2, N_DEV))],
        compiler_params=_params(),
    )(c_tile, w_mod, b_mod3)


def _mod_bwd(allvec, ca, me_idx):
    W = N_MODVEC // N_DEV

    def body(me_ref, all_ref, cols_ref, ca_ref, gw_ref, sum_ref):
        in_first_row = lax.broadcasted_iota(jnp.int32, (N_DEV, 8, W), 1) == 0
        dm = jnp.where(in_first_row, cols_ref[...], 0.0).reshape(8 * N_DEV, W)
        gw_ref[...] = _dot_tn(ca_ref[...], dm.astype(BF16))
        total = all_ref[0]
        for k in range(1, N_DEV):
            total = total + all_ref[k]
        sum_ref[...] = total

    return pl.pallas_call(
        body, name="mod_bwd",
        grid_spec=pltpu.PrefetchScalarGridSpec(
            num_scalar_prefetch=1, grid=(1,),
            in_specs=[pl.BlockSpec((N_DEV, 1, N_VEC), lambda i, me: (0, 0, 0)),
                      pl.BlockSpec((N_DEV, 1, W), lambda i, me: (0, 0, me[0])),
                      pl.BlockSpec((8 * N_DEV, D), lambda i, me: (0, 0))],
            out_specs=[pl.BlockSpec((D, W), lambda i, me: (0, 0)), pl.BlockSpec((1, N_VEC), lambda i, me: (0, 0))]),
        out_shape=[jax.ShapeDtypeStruct((D, W), F32), jax.ShapeDtypeStruct((1, N_VEC), F32)],
        compiler_params=_params(("arbitrary",)),
    )(me_idx, allvec, allvec, ca)


def _wgather(shards):
    n = len(shards)
    rows = [s.shape[0] for s in shards]

    def body(*refs):
        ins, outs, token = refs[:n], refs[n:2 * n], refs[2 * n]
        send_sems, recv_sems, local_sems = refs[2 * n + 1:]
        token[...] = jnp.zeros_like(token)
        x, y, c = _coords()
        me = 4 * x + 2 * y + c
        sib, xn, yn = (x, y, 1 - c), (1 - x, y, c), (x, 1 - y, c)
        block = lambda px, py, pc: 4 * px + 2 * py + pc

        def part(k, blk, half):
            if half is None:
                return outs[k].at[blk]
            return outs[k].at[blk, pl.ds(half * (rows[k] // 2), rows[k] // 2)]

        def copy(k, slot, blk, to, half=None, src=None):
            ref = part(k, blk, half)
            return pltpu.make_async_remote_copy(
                src_ref=ref if src is None else src, dst_ref=ref, send_sem=send_sems.at[k, slot],
                recv_sem=recv_sems.at[k, slot], device_id=to, device_id_type=MESH)

        local = [pltpu.make_async_copy(ins[k], outs[k].at[me], local_sems.at[k]) for k in range(n)]
        for cp in local:
            cp.start()
        sent = [copy(k, slot, me, to, src=ins[k]) for k in range(n) for slot, to in ((0, sib), (1, xn), (2, yn))]
        for cp in sent:
            cp.start()
        bx, by, bd = block(1 - x, y, c), block(x, 1 - y, c), block(1 - x, 1 - y, c)
        for k in range(n):
            copy(k, 1, bx, sib).wait_recv()
            sent += [copy(k, 4, bx, yn, half=1), copy(k, 5, bx, sib)]
            sent[-2].start()
            sent[-1].start()
        for k in range(n):
            copy(k, 2, by, sib).wait_recv()
            sent += [copy(k, 3, by, xn, half=0), copy(k, 6, by, sib)]
            sent[-2].start()
            sent[-1].start()
        for k in range(n):
            copy(k, 3, bd, sib, half=0).wait_recv()
            copy(k, 4, bd, sib, half=1).wait_recv()
            sent.append(copy(k, 7, bd, sib))
            sent[-1].start()
        for k in range(n):
            copy(k, 0, block(x, y, 1 - c), sib).wait_recv()
            for slot, blk in ((5, block(1 - x, y, 1 - c)), (6, block(x, 1 - y, 1 - c)), (7, block(1 - x, 1 - y, 1 - c))):
                copy(k, slot, blk, sib).wait_recv()
        for cp in sent:
            cp.wait_send()
        for cp in local:
            cp.wait()

    anyspec = pl.BlockSpec(memory_space=pl.ANY)
    return pl.pallas_call(
        body, name="wgather", in_specs=[anyspec] * n,
        out_specs=[anyspec] * n + [pl.BlockSpec(memory_space=pltpu.VMEM)],
        out_shape=[jax.ShapeDtypeStruct((N_DEV,) + s.shape, s.dtype) for s in shards]
        + [jax.ShapeDtypeStruct((8, 128), F32)],
        scratch_shapes=[pltpu.SemaphoreType.DMA((n, 8)), pltpu.SemaphoreType.DMA((n, 8)),
                        pltpu.SemaphoreType.DMA((n,))],
    )(*shards)


class _GatherCopies:
    def __init__(self, lands, send_sems, recv_sems, k0=0, batches=None):
        x, y, c = _coords()
        me = 4 * x + 2 * y + c
        sib = (x, y, 1 - c)
        chips = [(1 - x, y), (x, 1 - y), (1 - x, 1 - y)]

        def copy(k, slot, block, to):
            return pltpu.make_async_remote_copy(
                src_ref=lands[k].at[block], dst_ref=lands[k].at[block],
                send_sem=send_sems.at[7 * (k0 + k) + slot], recv_sem=recv_sems.at[7 * (k0 + k) + slot],
                device_id=to, device_id_type=MESH)

        n = len(lands)
        self.first = [copy(k, 0, me, sib) for k in range(n)]
        for batch in batches or [range(n)]:
            self.first += [copy(k, 1 + j, me, (cx, cy, c)) for j, (cx, cy) in enumerate(chips) for k in batch]
        self.landed = [copy(k, 1 + j, 4 * cx + 2 * cy + c, sib) for j, (cx, cy) in enumerate(chips) for k in range(n)]
        self.passed = [copy(k, 4 + j, 4 * cx + 2 * cy + c, sib) for j, (cx, cy) in enumerate(chips) for k in range(n)]
        self.from_sib = [copy(k, 0, 4 * x + 2 * y + (1 - c), sib) for k in range(n)]
        self.from_sib += [copy(k, 4 + j, 4 * cx + 2 * cy + (1 - c), sib) for j, (cx, cy) in enumerate(chips)
                          for k in range(n)]


def _gather_start(lands, *, name, batches=None):
    n = len(lands)

    def body(*refs):
        for cp in _GatherCopies(refs[:n], refs[n], refs[n + 1], batches=batches).first:
            cp.start()
        refs[-1][...] = jnp.zeros_like(refs[-1])

    out = pl.pallas_call(
        body, name=name,
        out_shape=(pltpu.SemaphoreType.DMA((7 * n,)), pltpu.SemaphoreType.DMA((7 * n,)),
                   *[pltpu.HBM(l.shape, l.dtype) for l in lands], jax.ShapeDtypeStruct((8, 128), F32)),
        in_specs=[HBM_SPEC] * n,
        out_specs=(SEM_SPEC, SEM_SPEC, *[HBM_SPEC] * n, pl.BlockSpec(memory_space=pltpu.VMEM)),
        input_output_aliases={i: 2 + i for i in range(n)},
        compiler_params=pltpu.CompilerParams(has_side_effects=DATAFLOW),
    )(*[_in_hbm(l) for l in lands])
    return out[0], out[1], list(out[2:2 + n]), out[-1]


def _gather_pass(send_sems, recv_sems, lands, after, *, name, stage, k0=0):
    n = len(lands)

    def body(*refs):
        cps = _GatherCopies(refs[:n], refs[n], refs[n + 1], k0)
        if stage == "landed":
            for cp in cps.landed:
                cp.wait_recv()
        else:
            for cp in cps.passed:
                cp.start()
        refs[-1][...] = jnp.zeros_like(refs[-1])

    out = pl.pallas_call(
        body, name=name,
        out_shape=(*[pltpu.HBM(l.shape, l.dtype) for l in lands], jax.ShapeDtypeStruct((8, 128), F32)),
        in_specs=[HBM_SPEC] * n + [SEM_SPEC, SEM_SPEC] + [pl.BlockSpec(memory_space=pl.ANY)] * len(after),
        out_specs=(*[HBM_SPEC] * n, pl.BlockSpec(memory_space=pltpu.VMEM)),
        input_output_aliases={i: i for i in range(n)},
        compiler_params=pltpu.CompilerParams(has_side_effects=DATAFLOW),
    )(*lands, send_sems, recv_sems, *after)
    return list(out[:n]), out[-1]


def _gather_end(send_sems, recv_sems, lands, after, *, name, k0=0):
    n = len(lands)

    def body(*refs):
        cps = _GatherCopies(refs[:n], refs[n], refs[n + 1], k0)
        for cp in cps.from_sib:
            cp.wait_recv()
        for cp in cps.first + cps.passed:
            cp.wait_send()

    out = pl.pallas_call(
        body, name=name,
        out_shape=[pltpu.HBM(l.shape, l.dtype) for l in lands],
        in_specs=[HBM_SPEC] * n + [SEM_SPEC, SEM_SPEC] + [pl.BlockSpec(memory_space=pl.ANY)] * len(after),
        out_specs=[HBM_SPEC] * n,
        input_output_aliases={i: i for i in range(n)},
        compiler_params=pltpu.CompilerParams(has_side_effects=DATAFLOW),
    )(*lands, send_sems, recv_sems, *after)
    return list(out)


def _d2d_copies(grads, lands, send_sems, recv_sems):
    x, y, c = _coords()
    return [pltpu.make_async_remote_copy(
        src_ref=grads[k].at[2 * q + (1 - c)], dst_ref=lands[k].at[q],
        send_sem=send_sems.at[4 * k + q], recv_sem=recv_sems.at[4 * k + q],
        device_id=(x, y, 1 - c), device_id_type=MESH) for k in range(len(grads)) for q in range(4)]


def _direct_copies(grads, lands, send_sems, recv_sems):
    x, y, c = _coords()
    me = 4 * x + 2 * y + c
    return [pltpu.make_async_remote_copy(
        src_ref=grads[k].at[me ^ r], dst_ref=lands[k].at[r - 1],
        send_sem=send_sems.at[7 * k + r - 1], recv_sem=recv_sems.at[7 * k + r - 1],
        device_id=_peer(r), device_id_type=MESH) for k in range(len(grads)) for r in range(1, N_DEV)]


def _vec_copies(srcs, lands, send_sems, recv_sems):
    x, y, c = _coords()
    me = 4 * x + 2 * y + c
    return [pltpu.make_async_remote_copy(
        src_ref=lands[0].at[me], dst_ref=lands[0].at[me], send_sem=send_sems.at[r - 1], recv_sem=recv_sems.at[r - 1],
        device_id=_peer(r), device_id_type=MESH) for r in range(1, N_DEV)]


def _chipsum(gs, sibs, cidx, *, name):
    n = len(gs)

    def body(c_ref, *refs):
        for k in range(n):
            refs[2 * n + k][...] = (refs[k][...].astype(F32) + refs[n + k][...].astype(F32)).astype(refs[2 * n + k].dtype)

    mine = [pl.BlockSpec((1,) + g.shape[1:], lambda q, c_ref: (2 * q + c_ref[0], 0, 0)) for g in gs]
    other = [pl.BlockSpec((1,) + g.shape[1:], lambda q, c_ref: (q, 0, 0)) for g in gs]
    return pl.pallas_call(
        body, name=name,
        grid_spec=pltpu.PrefetchScalarGridSpec(num_scalar_prefetch=1, grid=(4,), in_specs=mine + other, out_specs=other),
        out_shape=[jax.ShapeDtypeStruct((4,) + g.shape[1:], g.dtype) for g in gs],
        compiler_params=_params(("arbitrary",)),
    )(cidx, *gs, *sibs)


HBM_SPEC = pl.BlockSpec(memory_space=pltpu.HBM)
SEM_SPEC = pl.BlockSpec(memory_space=pltpu.SEMAPHORE)
DATAFLOW = pltpu.SideEffectType.DATAFLOW_SIDE_EFFECTING


def _in_hbm(a):
    return pltpu.with_memory_space_constraint(a, pltpu.HBM)


def _rs_step1_copies(sums, lands, send_sems, recv_sems):
    n = len(sums)
    direct, relay = lands[:n], lands[n:]
    x, y, c = _coords()
    xn, yn = (1 - x, y, c), (x, 1 - y, c)
    qx, qy, qd = 2 * (1 - x) + y, 2 * x + (1 - y), 2 * (1 - x) + (1 - y)
    cps = []
    for k in range(n):
        h = sums[k].shape[1] // 2
        a, b = pl.ds(0, h), pl.ds(h, h)
        moves = ((sums[k].at[qx, a], direct[k].at[0], xn), (sums[k].at[qy, b], direct[k].at[1], yn),
                 (sums[k].at[qd, a], relay[k].at[0], xn), (sums[k].at[qd, b], relay[k].at[1], yn))
        for s, (src, dst, to) in enumerate(moves):
            cps.append(pltpu.make_async_remote_copy(
                src_ref=src, dst_ref=dst, send_sem=send_sems.at[4 * k + s], recv_sem=recv_sems.at[4 * k + s],
                device_id=to, device_id_type=MESH))
    return cps


def _rs_step2_copies(relayed, lands, send_sems, recv_sems, k0=0):
    x, y, c = _coords()
    cps = []
    for k in range(len(relayed)):
        for s, to in enumerate(((1 - x, y, c), (x, 1 - y, c))):
            cps.append(pltpu.make_async_remote_copy(
                src_ref=relayed[k].at[s], dst_ref=lands[k].at[s], send_sem=send_sems.at[2 * (k0 + k) + s],
                recv_sem=recv_sems.at[2 * (k0 + k) + s], device_id=to, device_id_type=MESH))
    return cps


def _relay_sum(sums, relay, qxy, *, name):
    n = len(sums)

    def body(q_ref, *refs):
        for k in range(n):
            refs[2 * n + k][...] = (refs[k][...].astype(F32) + refs[n + k][...].astype(F32)).astype(refs[2 * n + k].dtype)

    half = lambda s: (1, s.shape[1] // 2) + s.shape[2:]
    return pl.pallas_call(
        body, name=name,
        grid_spec=pltpu.PrefetchScalarGridSpec(
            num_scalar_prefetch=1, grid=(2,),
            in_specs=[pl.BlockSpec(half(s), lambda t, q_ref: (q_ref[t], 1 - t, 0)) for s in sums]
            + [pl.BlockSpec(half(s), lambda t, q_ref: (1 - t, 0, 0)) for s in sums],
            out_specs=[pl.BlockSpec(half(s), lambda t, q_ref: (t, 0, 0)) for s in sums]),
        out_shape=[jax.ShapeDtypeStruct((2,) + half(s)[1:], s.dtype) for s in sums],
        compiler_params=_params(("arbitrary",)),
    )(qxy, *sums, *relay)


def _split_start(copies, srcs, lands, n_sems, after, *, name):
    ns, nl = len(srcs), len(lands)

    def body(*refs):
        for cp in copies(refs[:ns], refs[ns:ns + nl], refs[ns + nl + len(after)], refs[ns + nl + len(after) + 1]):
            cp.start()
        refs[-1][...] = jnp.zeros_like(refs[-1])

    bufs = [_in_hbm(a) for a in list(srcs) + list(lands)]
    out = pl.pallas_call(
        body, name=name,
        out_shape=(pltpu.SemaphoreType.DMA((n_sems,)), pltpu.SemaphoreType.DMA((n_sems,)),
                   *[pltpu.HBM(a.shape, a.dtype) for a in bufs], jax.ShapeDtypeStruct((8, 128), F32)),
        in_specs=[HBM_SPEC] * len(bufs) + [pl.BlockSpec(memory_space=pl.ANY)] * len(after),
        out_specs=(SEM_SPEC, SEM_SPEC, *[HBM_SPEC] * len(bufs), pl.BlockSpec(memory_space=pltpu.VMEM)),
        input_output_aliases={i: 2 + i for i in range(len(bufs))},
        compiler_params=pltpu.CompilerParams(has_side_effects=DATAFLOW),
    )(*bufs, *after)
    return out[0], out[1], list(out[2:2 + ns]), list(out[2 + ns:2 + ns + nl]), out[-1]


def _split_wait(copies, send_sems, recv_sems, srcs, lands, after, *, name):
    ns, nl = len(srcs), len(lands)

    def body(*refs):
        for cp in copies(refs[:ns], refs[ns:ns + nl], refs[ns + nl], refs[ns + nl + 1]):
            cp.wait_send()
            cp.wait_recv()

    out = pl.pallas_call(
        body, name=name,
        out_shape=[pltpu.HBM(a.shape, a.dtype) for a in list(srcs) + list(lands)],
        in_specs=[HBM_SPEC] * (ns + nl) + [SEM_SPEC, SEM_SPEC] + [pl.BlockSpec(memory_space=pl.ANY)] * len(after),
        out_specs=[HBM_SPEC] * (ns + nl),
        input_output_aliases={i: i for i in range(ns + nl)},
        compiler_params=pltpu.CompilerParams(has_side_effects=DATAFLOW),
    )(*srcs, *lands, send_sems, recv_sems, *after)
    return list(out[:ns]), list(out[ns:])


ADAM_C1 = 1.0 / (1.0 - ADAM_B1 ** ADAM_STEP)
ADAM_C2 = 1.0 / (1.0 - ADAM_B2 ** ADAM_STEP)


def _adam_math(w, g, m, v):
    m2 = ADAM_B1 * m + (1.0 - ADAM_B1) * g
    v2 = ADAM_B2 * v + (1.0 - ADAM_B2) * (g * g)
    return -ADAM_LR * ((m2 * ADAM_C1) / (jnp.sqrt(v2 * ADAM_C2) + ADAM_EPS) + ADAM_WD * w), m2, v2


def _adamw(w, g, m, v, *, name, after=()):
    R, C = w.shape
    tr = R if R <= 512 else 256

    def body(w_ref, g_ref, m_ref, v_ref, *rest):
        d_ref, nm_ref, nv_ref = rest[len(after):]
        d_ref[...], nm_ref[...], nv_ref[...] = _adam_math(w_ref[...], g_ref[...], m_ref[...], v_ref[...])

    blk = pl.BlockSpec((tr, C), lambda i: (i, 0))
    return pl.pallas_call(
        body, name=name, grid=(R // tr,), in_specs=[blk] * 4 + [pl.BlockSpec(memory_space=pl.ANY)] * len(after),
        out_specs=[blk] * 3, out_shape=[jax.ShapeDtypeStruct((R, C), F32)] * 3,
        compiler_params=_params(("parallel",)),
    )(w, g, m, v, *after)


def _adamw_rs2(wmv, cs, direct, second, qidx, *, name):
    n = len(wmv)
    r, cc = wmv[0][0].shape
    h = r // 2

    def body(q_ref, *refs):
        ins, outs = refs[:6 * n], refs[6 * n:]
        for k in range(n):
            w_ref, m_ref, v_ref, c_ref, d1_ref, d2_ref = ins[6 * k:6 * k + 6]
            g_ref, d_ref, nm_ref, nv_ref = outs[4 * k:4 * k + 4]
            g = (c_ref[0].astype(F32) + d1_ref[0].astype(F32)) + d2_ref[0].astype(F32)
            g_ref[...] = g
            d_ref[...], nm_ref[...], nv_ref[...] = _adam_math(w_ref[...], g, m_ref[...], v_ref[...])

    blk = pl.BlockSpec((h, cc), lambda i, q_ref: (i, 0))
    one = [blk, blk, blk, pl.BlockSpec((1, h, cc), lambda i, q_ref: (q_ref[0], i, 0)),
           pl.BlockSpec((1, h, cc), lambda i, q_ref: (i, 0, 0)),
           pl.BlockSpec((1, h, cc), lambda i, q_ref: (1 - i, 0, 0))]
    out = pl.pallas_call(
        body, name=name,
        grid_spec=pltpu.PrefetchScalarGridSpec(num_scalar_prefetch=1, grid=(2,), in_specs=one * n,
                                               out_specs=[blk] * (4 * n)),
        out_shape=[jax.ShapeDtypeStruct((r, cc), F32)] * (4 * n),
        compiler_params=_params(("arbitrary",)),
    )(qidx, *[a for (w, m, v), c, d1, d2 in zip(wmv, cs, direct, second) for a in (w, m, v, c, d1, d2)])
    return [tuple(out[4 * k:4 * k + 4]) for k in range(n)]


def _adamw_rs(wmv, cs, rcv, qidx, *, name):
    n = len(wmv)
    shapes = [w.shape for w, _, _ in wmv]
    n_rcv = rcv[0].shape[0]
    halved = len(set(shapes)) == 1 and shapes[0][0] % 32 == 0 and shapes[0][0] > 128
    tiles = 2 if halved else 1

    def body(q_ref, *refs):
        ins, outs = refs[:5 * n], refs[5 * n:]
        for k in range(n):
            w_ref, m_ref, v_ref, c_ref, r_ref = ins[5 * k:5 * k + 5]
            g_ref, d_ref, nm_ref, nv_ref = outs[4 * k:4 * k + 4]
            g = c_ref[0].astype(F32)
            for j in range(n_rcv):
                g = g + r_ref[j].astype(F32)
            g_ref[...] = g
            d_ref[...], nm_ref[...], nv_ref[...] = _adam_math(w_ref[...], g, m_ref[...], v_ref[...])

    in_specs, out_specs = [], []
    for r, cc in shapes:
        blk = pl.BlockSpec((r // tiles, cc), lambda i, q_ref: (i, 0))
        in_specs += [blk, blk, blk, pl.BlockSpec((1, r // tiles, cc), lambda i, q_ref: (q_ref[0], i, 0)),
                     pl.BlockSpec((n_rcv, r // tiles, cc), lambda i, q_ref: (0, i, 0))]
        out_specs += [blk] * 4
    out = pl.pallas_call(
        body, name=name,
        grid_spec=pltpu.PrefetchScalarGridSpec(num_scalar_prefetch=1, grid=(tiles,), in_specs=in_specs,
                                               out_specs=out_specs),
        out_shape=[jax.ShapeDtypeStruct(s, F32) for s in shapes for _ in range(4)],
        compiler_params=_params(("arbitrary",)),
    )(qidx, *[a for (w, m, v), c, rc in zip(wmv, cs, rcv) for a in (w, m, v, c, rc)])
    return [tuple(out[4 * k:4 * k + 4]) for k in range(n)]


SMALL_PARAMS = ("norm_ffn1", "norm_mix", "norm_ffn2", "norm_final", "q_norm", "kv_norm", "sinks", "rel_bias", "b_mod")


def _adamw_small(gvec, wmv):
    shapes = [wmv[3 * i].shape for i in range(len(SMALL_PARAMS))]

    def body(*refs):
        g_all = refs[0]
        ins = refs[1:1 + 3 * len(SMALL_PARAMS)]
        outs = refs[1 + 3 * len(SMALL_PARAMS):]
        off = N_MODVEC
        for i, name in enumerate(SMALL_PARAMS):
            g_ref, d_ref, nm_ref, nv_ref = outs[4 * i:4 * i + 4]
            w_ref, m_ref, v_ref = ins[3 * i:3 * i + 3]
            start = 0 if name == "b_mod" else off
            rows, width = shapes[i]
            g = jnp.concatenate([g_all[:, start + width * r:start + width * (r + 1)] for r in range(rows)], axis=0)
            g_ref[...] = g
            d_ref[...], nm_ref[...], nv_ref[...] = _adam_math(w_ref[...], g, m_ref[...], v_ref[...])
            if name != "b_mod":
                off += dict(SMALL_LAYOUT)[name]

    vm = pl.BlockSpec(memory_space=pltpu.VMEM)
    n_out = 4 * len(SMALL_PARAMS)
    out = pl.pallas_call(
        body, name="adamw_small", in_specs=[vm] * (1 + len(wmv)), out_specs=[vm] * n_out,
        out_shape=[jax.ShapeDtypeStruct(shapes[i // 4], F32) for i in range(n_out)],
        compiler_params=_params(),
    )(gvec, *wmv)
    return {name: out[4 * i:4 * i + 4] for i, name in enumerate(SMALL_PARAMS)}


TRANSPOSED = ("g1T", "u1T", "g3T", "u3T", "w_inT", "w_uqT")


def kernel(x, c, w_mod, b_mod, norm_ffn1, ffn1_gate, ffn1_up, ffn1_down, norm_mix, w_in, q_norm, kv_norm, w_uq, w_ukv, sinks, w_o, norm_ffn2, ffn2_gate, ffn2_up, ffn2_down, rel_bias, norm_final, loss_target, m_w_mod, m_b_mod, m_norm_ffn1, m_ffn1_gate, m_ffn1_up, m_ffn1_down, m_norm_mix, m_w_in, m_q_norm, m_kv_norm, m_w_uq, m_w_ukv, m_sinks, m_w_o, m_norm_ffn2, m_ffn2_gate, m_ffn2_up, m_ffn2_down, m_rel_bias, m_norm_final, v_w_mod, v_b_mod, v_norm_ffn1, v_ffn1_gate, v_ffn1_up, v_ffn1_down, v_norm_mix, v_w_in, v_q_norm, v_kv_norm, v_w_uq, v_w_ukv, v_sinks, v_w_o, v_norm_ffn2, v_ffn2_gate, v_ffn2_up, v_ffn2_down, v_rel_bias, v_norm_final):
    mx, my, mc = _coords()
    cidx = jnp.reshape(mc, (1,)).astype(jnp.int32)
    qidx = jnp.reshape(2 * mx + my, (1,)).astype(jnp.int32)
    WM = w_mod.shape[2]

    c_tile = jnp.pad(c, ((0, 7), (0, 0)))
    b_mod3 = jnp.pad(b_mod.reshape(N_DEV, 1, WM), ((0, 0), (0, 7), (0, 0)))
    mod3, ca = _mod_fwd(c_tile, w_mod[0], b_mod3)
    mod9 = mod3[:, 0, :].reshape(N_MOD, D)

    shards = {"g1T": ffn1_gate[0].T.astype(BF16), "u1T": ffn1_up[0].T.astype(BF16), "d1": ffn1_down[0].astype(BF16),
              "g3T": ffn2_gate[0].T.astype(BF16), "u3T": ffn2_up[0].T.astype(BF16), "d3": ffn2_down[0].astype(BF16),
              "w_inT": w_in[0].T, "w_uqT": w_uq[0].T.astype(BF16), "w_ukv": w_ukv[0].astype(BF16),
              "w_o": w_o[0].astype(BF16)}
    me = 4 * mx + 2 * my + mc
    groups = {"ffn1": ("g1T", "u1T", "d1"), "mixer": ("w_inT", "w_uqT", "w_ukv", "w_o"), "ffn2": ("g3T", "u3T", "d3")}
    arriving = {}

    def as_weights(group, gathered):
        return {k: g if k == "w_ukv" else g.reshape(N_DEV * g.shape[1], g.shape[2])
                for k, g in zip(groups[group], gathered)}

    later = groups["mixer"] + groups["ffn2"]
    place = {"mixer": 0, "ffn2": len(groups["mixer"])}

    def start_gather(token):
        lands = []
        for k in later:
            sh = shards[k] + token[0, 0].astype(shards[k].dtype)
            lands.append(lax.dynamic_update_slice(lax.empty((N_DEV,) + sh.shape, sh.dtype), sh[None], (me, 0, 0)))
        batches = [range(k0, k0 + len(groups[group])) for group, k0 in place.items()]
        send, recv, lands, started = _gather_start(lands, name="gather_start", batches=batches)
        for group, k0 in place.items():
            arriving[group] = (send, recv, lands[k0:k0 + len(groups[group])])
        return started

    def fetch(group, after, vecs):
        if group == "ffn1":
            *gathered, token = _wgather([shards[k] + ca[1, 0].astype(shards[k].dtype) for k in groups["ffn1"]])
            return as_weights("ffn1", gathered), vecs + start_gather(token)[0:1, 0:1]

        def pass_on(group, after):
            send, recv, lands = arriving[group]
            lands, token = _gather_pass(send, recv, lands, after, name="gather_landed_" + group, stage="landed",
                                        k0=place[group])
            lands, token = _gather_pass(send, recv, lands, [token], name="gather_onward_" + group, stage="onward",
                                        k0=place[group])
            arriving[group] = (send, recv, lands)
            return token

        if group == "ffn2_on_its_way":
            return None, vecs + pass_on("ffn2", after)[0:1, 0:1]
        if group == "mixer":
            after = [pass_on("mixer", after)]
        send, recv, lands = arriving[group]
        return as_weights(group, _gather_end(send, recv, lands, after, name="gather_end_" + group,
                                             k0=place[group])), vecs

    norms ={"ffn1": norm_ffn1, "mix": norm_mix, "ffn2": norm_ffn2, "final": norm_final.reshape(1, D)}
    in_flight = {}

    def on_grads(group, g, after, vecs):
        if group != "ffn1":
            if g is None:
                return vecs
            names = list(g)
            by_dest = [g[k] if k == "w_ukv" else g[k].reshape((N_DEV, g[k].shape[0] // N_DEV) + g[k].shape[1:])
                       for k in names]
            lands = [lax.empty((N_DEV - 1,) + a.shape[1:], a.dtype) for a in by_dest]
            send, recv, by_dest, lands, token = _split_start(_direct_copies, by_dest, lands, 7 * len(names), after,
                                                             name="rs_start_" + group)
            in_flight[group] = (names, send, recv, by_dest, lands, token)
            return vecs + token[0:1, 0:1]
        names = list(g)
        by_dest = [g[k].reshape((N_DEV, g[k].shape[0] // N_DEV) + g[k].shape[1:]) for k in names]
        lands = [lax.empty((4,) + a.shape[1:], a.dtype) for a in by_dest]
        send, recv, by_dest, lands, token = _split_start(_d2d_copies, by_dest, lands, 4 * len(names), after,
                                                         name="rs_d2d_start_" + group)
        finish("mixer", [token])
        by_dest, from_sib = _split_wait(_d2d_copies, send, recv, by_dest, lands, [done[-1]], name="rs_d2d_wait_" + group)
        sums = _chipsum(by_dest, from_sib, cidx, name="chipsum_" + group)
        halves = lambda: [lax.empty((2, s.shape[1] // 2) + s.shape[2:], s.dtype) for s in sums]
        send, recv, sums, lands, token = _split_start(_rs_step1_copies, sums, halves() + halves(), 4 * len(names), [],
                                                      name="rs_ici_start_" + group)
        in_flight[group] = (names, send, recv, sums, lands, token)
        return vecs + token[0:1, 0:1]

    owners = {"g1T": ("ffn1_gate", ffn1_gate, m_ffn1_gate, v_ffn1_gate), "u1T": ("ffn1_up", ffn1_up, m_ffn1_up, v_ffn1_up),
              "d1": ("ffn1_down", ffn1_down, m_ffn1_down, v_ffn1_down),
              "g3T": ("ffn2_gate", ffn2_gate, m_ffn2_gate, v_ffn2_gate), "u3T": ("ffn2_up", ffn2_up, m_ffn2_up, v_ffn2_up),
              "d3": ("ffn2_down", ffn2_down, m_ffn2_down, v_ffn2_down),
              "w_inT": ("w_in", w_in, m_w_in, v_w_in), "w_uqT": ("w_uq", w_uq, m_w_uq, v_w_uq),
              "w_ukv": ("w_ukv", w_ukv, m_w_ukv, v_w_ukv), "w_o": ("w_o", w_o, m_w_o, v_w_o)}
    res, done = {}, []

    there = lambda k, a: a[0].T if k in TRANSPOSED else a[0]
    back = lambda k, a: a.T[None] if k in TRANSPOSED else a[None]

    def record(names, outs):
        for k, out in zip(names, outs):
            done.append(out[3])
            res[owners[k][0]] = tuple(back(k, a) for a in out)

    def finish(group, after):
        names, send, recv, sums, lands, _ = in_flight[group]
        wmv = [tuple(there(k, a) for a in owners[k][1:]) for k in names]
        own = jnp.reshape(me, (1,)).astype(jnp.int32)
        sums, lands = _split_wait(_direct_copies, send, recv, sums, lands, after, name="rs_wait_" + group)
        record(names, _adamw_rs(wmv, sums, lands, own, name="adamw_" + group))

    _, grad_x, _, vec = _local_step(
        x[0], loss_target[0], mod9, norms, sinks, rel_bias, q_norm, kv_norm, fetch, on_grads=on_grads)

    names, send, recv, sums, lands, step1_started = in_flight["ffn1"]
    vec = vec.reshape(1, 1, N_VEC)
    allvec = lax.dynamic_update_slice(lax.empty((N_DEV, 1, N_VEC), F32), vec, (me, 0, 0))
    vsend, vrecv, _, (allvec,), vec_started = _split_start(_vec_copies, [], [allvec], N_DEV - 1, [step1_started],
                                                           name="vec_start")
    finish("ffn2", [vec_started])
    n = len(names)
    sums, lands = _split_wait(_rs_step1_copies, send, recv, sums, lands, [done[-1]], name="rs_ici_wait_ffn1")
    direct, relay = lands[:n], lands[n:]
    qxy = jnp.stack([2 * (1 - mx) + my, 2 * mx + (1 - my)]).astype(jnp.int32)
    relayed = _relay_sum(sums, relay, qxy, name="relay_sum_ffn1")
    second = [lax.empty(a.shape, a.dtype) for a in relayed]
    send, recv, relayed, second, step2_started = _split_start(_rs_step2_copies, relayed, second, 2 * n, [],
                                                              name="rs_ici_start2_ffn1")

    _, (allvec,) = _split_wait(_vec_copies, vsend, vrecv, [], [allvec], [step2_started], name="vec_wait")
    g_wmod, gvec = _mod_bwd(allvec, ca, jnp.reshape(me, (1,)).astype(jnp.int32))
    loss = gvec[0, N_MODVEC + LOSS_SLOT]
    small_in = {"norm_ffn1": (norm_ffn1, m_norm_ffn1, v_norm_ffn1), "norm_mix": (norm_mix, m_norm_mix, v_norm_mix),
                "norm_ffn2": (norm_ffn2, m_norm_ffn2, v_norm_ffn2), "norm_final": (norm_final, m_norm_final, v_norm_final),
                "q_norm": (q_norm, m_q_norm, v_q_norm), "kv_norm": (kv_norm, m_kv_norm, v_kv_norm),
                "sinks": (sinks, m_sinks, v_sinks), "rel_bias": (rel_bias, m_rel_bias, v_rel_bias),
                "b_mod": (b_mod, m_b_mod, v_b_mod)}
    as_row = lambda k, a: a.T if k == "rel_bias" else a.reshape(1, -1)
    from_row = lambda k, a: a.T if k == "rel_bias" else a.reshape(small_in[k][0].shape)
    small_out = _adamw_small(gvec, [as_row(k, a) for k in SMALL_PARAMS for a in small_in[k]])
    for k in SMALL_PARAMS:
        res[k] = tuple(from_row(k, a) for a in small_out[k])

    out = _adamw(w_mod[0], g_wmod, m_w_mod[0], v_w_mod[0], name="adamw_w_mod")
    res["w_mod"] = tuple(a[None] for a in (g_wmod,) + tuple(out))

    wmv = [tuple(there(k, a) for a in owners[k][1:]) for k in names]
    after = done + [out[2]] + [a for k in SMALL_PARAMS for a in res[k]]
    outs = []
    for i, k in enumerate(names):
        _, (sec,) = _split_wait(functools.partial(_rs_step2_copies, k0=i), send, recv, [relayed[i]], [second[i]],
                                after, name="rs_ici_wait2_" + k)
        outs.append(_adamw_rs2([wmv[i]], [sums[i]], [direct[i]], [sec], qidx, name="adamw_" + owners[k][0])[0])
        after = [outs[-1][3]]
    record(names, outs)

    order = ("w_mod", "b_mod", "norm_ffn1", "ffn1_gate", "ffn1_up", "ffn1_down", "norm_mix", "w_in", "q_norm",
             "kv_norm", "w_uq", "w_ukv", "sinks", "w_o", "norm_ffn2", "ffn2_gate", "ffn2_up", "ffn2_down",
             "rel_bias", "norm_final")
    return (loss, grad_x[None]) + tuple(res[nm][kind] for kind in range(4) for nm in order)
```

```python
import functools
import math

import numpy as np
import jax
import jax.numpy as jnp
from jax import lax
from jax.experimental import pallas as pl
from jax.experimental.pallas import tpu as pltpu

F32 = jnp.float32
BF16 = jnp.bfloat16
MESH = pl.DeviceIdType.MESH

N_DEV = 8
D = 1024
D_FF = 2816
EPS = 1e-6
N_MOD = 9
SWA_HEADS = 8
SWA_DH = 64
WINDOW = 128
MLA_HEADS = 4
MLA_NOPE = 128
MLA_ROPE = 64
MLA_V = 128
MLA_QR = 256
MLA_KVR = 128
ROPE_THETA = 10000.0
NUM_BUCKETS = 32
D_IN = 1216
D_IN_PAD = 1280
SWA_SCALE = SWA_DH ** -0.5
MLA_SCALE = (MLA_NOPE + MLA_ROPE) ** -0.5

ADAM_LR = 0.001
ADAM_B1 = 0.9
ADAM_B2 = 0.999
ADAM_EPS = 1e-08
ADAM_WD = 0.01
ADAM_STEP = 10

V7X_VMEM_LIMIT = 56 * 1024 * 1024
ROW_TILE = 512

NT_DIMS = (((1,), (1,)), ((), ()))
TN_DIMS = (((0,), (0,)), ((), ()))


def _dot(a, b):
    return jnp.dot(a, b, preferred_element_type=F32)


def _dot_nt(a, b):
    return lax.dot_general(a, b, NT_DIMS, preferred_element_type=F32)


def _dot_tn(a, b):
    return lax.dot_general(a, b, TN_DIMS, preferred_element_type=F32)


def _params(sem=None):
    return pltpu.CompilerParams(dimension_semantics=sem, vmem_limit_bytes=V7X_VMEM_LIMIT)


def _rstd(x):
    return lax.rsqrt(jnp.mean(x * x, axis=-1, keepdims=True) + EPS)


def _rms_bwd(dy, xhat, r):
    return r * (dy - xhat * jnp.mean(dy * xhat, axis=-1, keepdims=True))


def _sigmoid(a):
    return 1.0 / (1.0 + jnp.exp(-a))


def _ffn_fwd(x, vecs, wgT, wuT, wd, *, name, tm=256, tf=D_FF):
    S, F = x.shape[0], wd.shape[0]
    tm = min(tm, S)
    ni, nj = S // tm, F // tf

    def body(x_ref, vec_ref, wg_ref, wu_ref, wd_ref, xo_ref, h_ref, a_ref, b_ref, f_ref, acc_ref):
        j = pl.program_id(1)

        @pl.when(j == 0)
        def _():
            xv = x_ref[...]
            hn = xv * _rstd(xv) * vec_ref[0:1, :]
            h_ref[...] = (hn * (1.0 + vec_ref[2:3, :]) + vec_ref[1:2, :]).astype(BF16)

        h = h_ref[...]
        a = _dot_nt(h, wg_ref[...])
        b = _dot_nt(h, wu_ref[...])
        a_ref[...] = a.astype(BF16)
        b_ref[...] = b.astype(BF16)
        part = _dot((a * _sigmoid(a) * b).astype(BF16), wd_ref[...])

        def finish(f):
            f_ref[...] = f
            xo_ref[...] = x_ref[...] + (0.5 * vec_ref[3:4, :]) * f

        if nj == 1:
            finish(part)
        else:
            @pl.when(j == 0)
            def _():
                acc_ref[...] = part

            @pl.when((j > 0) & (j < nj - 1))
            def _():
                acc_ref[...] += part

            @pl.when(j == nj - 1)
            def _():
                finish(acc_ref[...] + part)

    row = pl.BlockSpec((tm, D), lambda i, j: (i, 0))
    wspec = pl.BlockSpec((tf, D), lambda i, j: (j, 0), pipeline_mode=pl.Buffered(1) if nj == 1 else None)
    act = pl.BlockSpec((tm, tf), lambda i, j: (i, j))
    return pl.pallas_call(
        body, name=name, grid=(ni, nj),
        in_specs=[row, pl.BlockSpec((8, D), lambda i, j: (0, 0)), wspec, wspec, wspec],
        out_specs=[row, row, act, act, row],
        out_shape=[jax.ShapeDtypeStruct((S, D), F32), jax.ShapeDtypeStruct((S, D), BF16),
                   jax.ShapeDtypeStruct((S, F), BF16), jax.ShapeDtypeStruct((S, F), BF16),
                   jax.ShapeDtypeStruct((S, D), F32)],
        scratch_shapes=[pltpu.VMEM((tm, D) if nj > 1 else (8, 128), F32)],
        compiler_params=_params(("parallel", "arbitrary")),
    )(x, vecs, wgT, wuT, wd)


def _ffn_bwd_main(h, df, a, b, wgT, wuT, wd, *, name, after=(), tm=2048, tf=256):
    S = h.shape[0]
    tm = min(tm, S)
    ni, nj = S // tm, D_FF // tf

    def body(h_hbm, df_hbm, a_ref, b_ref, wg_ref, wu_ref, wd_ref, *rest):
        gg_ref, gu_ref, gd_ref, dh_hbm, h_v, df_v, dh_v, gg_acc, gu_acc, gd_acc, sem = rest[len(after):]
        j = pl.program_id(0)
        i = pl.program_id(1)

        @pl.when((j == 0) & (i == 0))
        def _():
            c1 = pltpu.make_async_copy(h_hbm, h_v, sem.at[0])
            c2 = pltpu.make_async_copy(df_hbm, df_v, sem.at[1])
            c1.start()
            c2.start()
            c1.wait()
            c2.wait()

        @pl.when(i == 0)
        def _():
            gg_acc[...] = jnp.zeros_like(gg_acc)
            gu_acc[...] = jnp.zeros_like(gu_acc)
            gd_acc[...] = jnp.zeros_like(gd_acc)

        rows = pl.ds(pl.multiple_of(i * tm, tm), tm)
        hi = h_v[rows, :]
        dfi = df_v[rows, :]
        av = a_ref[...].astype(F32)
        bv = b_ref[...].astype(F32)
        sg = _sigmoid(av)
        sa = av * sg
        hsw = (sa * bv).astype(BF16)
        dhsw = _dot_nt(dfi, wd_ref[...])
        da = (dhsw * bv * (sg * (1.0 + av * (1.0 - sg)))).astype(BF16)
        db = (dhsw * sa).astype(BF16)
        gd_acc[...] += _dot_tn(hsw, dfi)
        gg_acc[...] += _dot_tn(da, hi)
        gu_acc[...] += _dot_tn(db, hi)
        dh = _dot(da, wg_ref[...]) + _dot(db, wu_ref[...])

        @pl.when(j == 0)
        def _():
            dh_v[rows, :] = dh

        @pl.when(j > 0)
        def _():
            dh_v[rows, :] += dh

        @pl.when(i == ni - 1)
        def _():
            gg_ref[...] = gg_acc[...].astype(BF16)
            gu_ref[...] = gu_acc[...].astype(BF16)
            gd_ref[...] = gd_acc[...].astype(BF16)

        @pl.when((j == nj - 1) & (i == ni - 1))
        def _():
            c3 = pltpu.make_async_copy(dh_v, dh_hbm, sem.at[2])
            c3.start()
            c3.wait()

    anyspec = pl.BlockSpec(memory_space=pl.ANY)
    wspec = pl.BlockSpec((tf, D), lambda j, i: (j, 0))
    act = pl.BlockSpec((tm, tf), lambda j, i: (i, j))
    return pl.pallas_call(
        body, name=name, grid=(nj, ni),
        in_specs=[anyspec, anyspec, act, act, wspec, wspec, wspec] + [anyspec] * len(after),
        out_specs=[wspec, wspec, wspec, anyspec],
        out_shape=[jax.ShapeDtypeStruct((D_FF, D), BF16)] * 3 + [jax.ShapeDtypeStruct((S, D), F32)],
        scratch_shapes=[pltpu.VMEM((S, D), BF16), pltpu.VMEM((S, D), BF16), pltpu.VMEM((S, D), F32),
                        pltpu.VMEM((tf, D), F32), pltpu.VMEM((tf, D), F32), pltpu.VMEM((tf, D), F32),
                        pltpu.SemaphoreType.DMA((3,))],
        compiler_params=_params(("arbitrary", "arbitrary")),
    )(h, df, a, b, wgT, wuT, wd, *after)


def _ffn_out_bwd(dx, f, gate, df_ref, part_ref):
    df_ref[...] = ((0.5 * gate) * dx).astype(BF16)
    part_ref[3:4, :] += 0.5 * jnp.sum(dx * f, axis=0, keepdims=True)


def _norm_bwd(dh, x, dxo, vecs, *, name, below=None, tm=ROW_TILE):
    S = x.shape[0]
    tm = min(tm, S)

    def body(dh_ref, x_ref, dxo_ref, vec_ref, *rest):
        dx_ref, part_ref = rest[-2 if below is None else -3], rest[-1 if below is None else -2]

        @pl.when(pl.program_id(0) == 0)
        def _():
            part_ref[...] = jnp.zeros_like(part_ref)

        dh = dh_ref[...]
        xv = x_ref[...]
        r = _rstd(xv)
        xhat = xv * r
        w = vec_ref[0:1, :]
        xn = xhat * w
        dxn = dh * (1.0 + vec_ref[2:3, :])
        part_ref[0:1, :] += jnp.sum(dxn * xhat, axis=0, keepdims=True)
        part_ref[1:2, :] += jnp.sum(dh, axis=0, keepdims=True)
        part_ref[2:3, :] += jnp.sum(dh * xn, axis=0, keepdims=True)
        dx = dxo_ref[...] + _rms_bwd(dxn * w, xhat, r)
        dx_ref[...] = dx
        if below is not None:
            _ffn_out_bwd(dx, rest[0][...], rest[1][3:4, :], rest[-1], part_ref)

    row = pl.BlockSpec((tm, D), lambda i: (i, 0))
    vec = pl.BlockSpec((8, D), lambda i: (0, 0))
    extra = [] if below is None else [row, vec]
    return pl.pallas_call(
        body, name=name, grid=(S // tm,), in_specs=[row, row, row, vec] + extra,
        out_specs=[row, vec] + ([] if below is None else [row]),
        out_shape=[jax.ShapeDtypeStruct((S, D), F32), jax.ShapeDtypeStruct((8, D), F32)]
        + ([] if below is None else [jax.ShapeDtypeStruct((S, D), BF16)]),
        compiler_params=_params(("arbitrary",)),
    )(dh, x, dxo, vecs, *([] if below is None else below))


def _head(x, tgt, nf, f, vecs, *, tm=ROW_TILE):
    S = x.shape[0]
    tm = min(tm, S)

    def body(x_ref, t_ref, nf_ref, f_ref, vec_ref, dx_ref, part_ref, df_ref):
        @pl.when(pl.program_id(0) == 0)
        def _():
            part_ref[...] = jnp.zeros_like(part_ref)

        xv = x_ref[...]
        r = _rstd(xv)
        xhat = xv * r
        w = nf_ref[...]
        e = xhat * w - t_ref[...]
        dy = e * (1.0 / D)
        part_ref[0:1, :] += jnp.sum(dy * xhat, axis=0, keepdims=True)
        part_ref[1:2, :] += jnp.sum(e * e) * (0.5 / D)
        dx = _rms_bwd(dy * w, xhat, r)
        dx_ref[...] = dx
        _ffn_out_bwd(dx, f_ref[...], vec_ref[3:4, :], df_ref, part_ref)

    row = pl.BlockSpec((tm, D), lambda i: (i, 0))
    vec = pl.BlockSpec((8, D), lambda i: (0, 0))
    return pl.pallas_call(
        body, name="head", grid=(S // tm,),
        in_specs=[row, row, pl.BlockSpec((1, D), lambda i: (0, 0)), row, vec],
        out_specs=[row, vec, row],
        out_shape=[jax.ShapeDtypeStruct((S, D), F32), jax.ShapeDtypeStruct((8, D), F32),
                   jax.ShapeDtypeStruct((S, D), BF16)],
        compiler_params=_params(("arbitrary",)),
    )(x, tgt, nf, f, vecs)


def _mix_in_fwd(x, vecs, w_inT, *, tm=ROW_TILE):
    S = x.shape[0]
    tm = min(tm, S)

    def body(x_ref, vec_ref, w_ref, h_ref, p_ref, wb_ref):
        @pl.when(pl.program_id(0) == 0)
        def _():
            wb_ref[0:D_IN, :] = w_ref[...].astype(BF16)
            wb_ref[D_IN:D_IN_PAD, :] = jnp.zeros((D_IN_PAD - D_IN, D), BF16)

        xv = x_ref[...]
        hn = xv * _rstd(xv) * vec_ref[0:1, :]
        h = (hn * (1.0 + vec_ref[2:3, :]) + vec_ref[1:2, :]).astype(BF16)
        h_ref[...] = h
        p_ref[...] = _dot_nt(h, wb_ref[...])

    row = pl.BlockSpec((tm, D), lambda i: (i, 0))
    return pl.pallas_call(
        body, name="mix_in_fwd", grid=(S // tm,),
        in_specs=[row, pl.BlockSpec((8, D), lambda i: (0, 0)),
                  pl.BlockSpec((D_IN, D), lambda i: (0, 0), pipeline_mode=pl.Buffered(1))],
        out_specs=[row, pl.BlockSpec((tm, D_IN_PAD), lambda i: (i, 0)), pl.BlockSpec((D_IN_PAD, D), lambda i: (0, 0))],
        out_shape=[jax.ShapeDtypeStruct((S, D), BF16), jax.ShapeDtypeStruct((S, D_IN_PAD), F32),
                   jax.ShapeDtypeStruct((D_IN_PAD, D), BF16)],
        compiler_params=_params(("arbitrary",)),
    )(x, vecs, w_inT)


def _bucket_table():
    qi = np.arange(WINDOW)[:, None]
    kj = np.arange(2 * WINDOW)[None, :]
    dist = qi + WINDOW - kj
    max_exact = NUM_BUCKETS // 2
    n = np.maximum(dist, 0)
    nf = np.maximum(n, 1).astype(np.float32)
    large = max_exact + (np.log(nf / np.float32(max_exact)) / np.float32(math.log(WINDOW / max_exact))
                         * np.float32(NUM_BUCKETS - max_exact)).astype(np.int32)
    large = np.minimum(large, NUM_BUCKETS - 1)
    return np.where(n < max_exact, n, large).astype(np.int32)


SWA_GROUP = 4
GROUP_ROWS = SWA_GROUP * WINDOW


SWA_SUB = 2


def _swa_valid(has_prev):
    row = lax.broadcasted_iota(jnp.int32, (GROUP_ROWS, 2 * WINDOW), 0) % WINDOW
    col = lax.broadcasted_iota(jnp.int32, (GROUP_ROWS, 2 * WINDOW), 1)
    dist = row + WINDOW - col
    return (dist >= 0) & (dist < WINDOW) & ((col >= WINDOW) | has_prev)


def _swa_keys(prev_ref, cur_ref, u):
    cur = cur_ref[...]
    before = prev_ref[...] if u == 0 else cur[WINDOW * (u - 1):WINDOW * u]
    return jnp.concatenate([before, cur[WINDOW * u:WINDOW * (u + 1)]], axis=0).astype(BF16)


def _stack_heads(x, g):
    return jnp.concatenate([x[:, 64 * h:64 * h + 64] for h in range(SWA_GROUP * g, SWA_GROUP * (g + 1))], axis=0)


def _unstack_heads(x4):
    return jnp.concatenate([x4[WINDOW * a:WINDOW * (a + 1)] for a in range(SWA_GROUP)], axis=1)


def _group_sinks(sink_ref, g):
    head = lax.broadcasted_iota(jnp.int32, (GROUP_ROWS, 1), 0) // WINDOW
    out = jnp.full((GROUP_ROWS, 1), sink_ref[0, SWA_GROUP * g], F32)
    for a in range(1, SWA_GROUP):
        out = jnp.where(head == a, sink_ref[0, SWA_GROUP * g + a], out)
    return out


def _swa_probs(qh, kk, bias_h, sink, valid):
    s = _dot_nt(qh, kk) * SWA_SCALE + bias_h
    s = jnp.where(valid, s, -jnp.inf)
    m = jnp.maximum(jnp.max(s, axis=-1, keepdims=True), sink)
    p = jnp.exp(s - m)
    ps = jnp.exp(sink - m)
    inv = 1.0 / (jnp.sum(p, axis=-1, keepdims=True) + ps)
    return p * inv, ps * inv


SWA_ROWS = SWA_SUB * WINDOW


def _swa_specs():
    prev = lambda n: jnp.maximum(SWA_SUB * n - 1, 0)
    return [pl.BlockSpec((SWA_ROWS, 512), lambda n: (n, 0)),
            pl.BlockSpec((SWA_ROWS, 128), lambda n: (n, 4)),
            pl.BlockSpec((WINDOW, 128), lambda n: (prev(n), 4)),
            pl.BlockSpec((SWA_ROWS, 128), lambda n: (n, 5)),
            pl.BlockSpec((WINDOW, 128), lambda n: (prev(n), 5)),
            pl.BlockSpec((SWA_HEADS, WINDOW, 2 * WINDOW), lambda n: (0, 0, 0)),
            pl.BlockSpec(memory_space=pltpu.SMEM)]


def _swa_fwd(proj, rel_bias, bucket, sinks):
    S = proj.shape[0]

    def body(q_ref, kc_ref, kp_ref, vc_ref, vp_ref, rb_ref, sink_ref, bk_ref, o_ref, bias_ref):
        n = pl.program_id(0)

        @pl.when(n == 0)
        def _():
            bk = bk_ref[...]
            for h in range(SWA_HEADS):
                acc = jnp.zeros((WINDOW, 2 * WINDOW), F32)
                for b in range(NUM_BUCKETS):
                    acc = jnp.where(bk == b, rb_ref[b, h], acc)
                bias_ref[h] = acc

        for u in range(SWA_SUB):
            rows = slice(WINDOW * u, WINDOW * (u + 1))
            valid = _swa_valid(n > 0 if u == 0 else True)
            q = q_ref[rows, :].astype(BF16)
            kfull = _swa_keys(kp_ref, kc_ref, u)
            vfull = _swa_keys(vp_ref, vc_ref, u)
            for g in range(SWA_HEADS // SWA_GROUP):
                kk = kfull[:, 64 * g:64 * g + 64]
                vv = vfull[:, 64 * g:64 * g + 64]
                bias4 = bias_ref[SWA_GROUP * g:SWA_GROUP * (g + 1)].reshape(GROUP_ROWS, 2 * WINDOW)
                pk, _ = _swa_probs(_stack_heads(q, g), kk, bias4, _group_sinks(sink_ref, g), valid)
                o_ref[rows, 256 * g:256 * (g + 1)] = _unstack_heads(_dot(pk.astype(BF16), vv))

    specs = _swa_specs()
    whole = pl.BlockSpec((SWA_HEADS, WINDOW, 2 * WINDOW), lambda n: (0, 0, 0))
    return pl.pallas_call(
        body, name="swa_fwd", grid=(S // SWA_ROWS,),
        in_specs=specs[:5] + [pl.BlockSpec(memory_space=pltpu.SMEM), specs[6],
                              pl.BlockSpec((WINDOW, 2 * WINDOW), lambda n: (0, 0))],
        out_specs=[pl.BlockSpec((SWA_ROWS, 512), lambda n: (n, 0)), whole],
        out_shape=[jax.ShapeDtypeStruct((S, 512), F32), jax.ShapeDtypeStruct((SWA_HEADS, WINDOW, 2 * WINDOW), F32)],
        compiler_params=_params(("arbitrary",)),
    )(proj, proj, proj, proj, proj, rel_bias, sinks, bucket)


def _swa_bwd(proj, bias, sinks, o, do, bucket):
    S = proj.shape[0]
    nb = S // SWA_ROWS

    def body(q_ref, kc_ref, kp_ref, vc_ref, vp_ref, bias_ref, sink_ref, o_ref, do_ref, bk_ref,
             dq_ref, dk_ref, dv_ref, drb_ref, dsk_ref, dbias_acc):
        n = pl.program_id(0)

        @pl.when(n == 0)
        def _():
            dk_ref[...] = jnp.zeros_like(dk_ref)
            dv_ref[...] = jnp.zeros_like(dv_ref)
            dsk_ref[...] = jnp.zeros_like(dsk_ref)
            dbias_acc[...] = jnp.zeros_like(dbias_acc)
            drb_ref[...] = jnp.zeros_like(drb_ref)

        for u in range(SWA_SUB):
            rows = slice(WINDOW * u, WINDOW * (u + 1))
            blk = SWA_SUB * n + u
            valid = _swa_valid(n > 0 if u == 0 else True)
            q = q_ref[rows, :].astype(BF16)
            dov = do_ref[rows, :]
            ov = o_ref[rows, :]
            kfull = _swa_keys(kp_ref, kc_ref, u)
            vfull = _swa_keys(vp_ref, vc_ref, u)
            prow = pl.ds(pl.multiple_of(jnp.maximum(blk - 1, 0) * WINDOW, WINDOW), WINDOW)
            crow = pl.ds(pl.multiple_of(blk * WINDOW, WINDOW), WINDOW)
            for g in range(SWA_HEADS // SWA_GROUP):
                heads = slice(SWA_GROUP * g, SWA_GROUP * (g + 1))
                kk = kfull[:, 64 * g:64 * g + 64]
                vv = vfull[:, 64 * g:64 * g + 64]
                q4 = _stack_heads(q, g)
                pk, psink = _swa_probs(q4, kk, bias_ref[heads].reshape(GROUP_ROWS, 2 * WINDOW),
                                       _group_sinks(sink_ref, g), valid)
                pkb = pk.astype(BF16)
                do4 = _stack_heads(dov, g)
                dob = do4.astype(BF16)
                dp = _dot_nt(dob, vv)
                delta = jnp.sum(do4 * _stack_heads(ov, g), axis=-1, keepdims=True)
                ds = pk * (dp - delta)
                dsink = -psink * delta
                for a in range(SWA_GROUP):
                    h = SWA_GROUP * g + a
                    part = jnp.sum(dsink[WINDOW * a:WINDOW * (a + 1)], keepdims=True)
                    dsk_ref[h:h + 1, :] += jnp.broadcast_to(part, (1, 128))
                dbias_acc[heads] += ds.reshape(SWA_GROUP, WINDOW, 2 * WINDOW)
                dsb = (ds * SWA_SCALE).astype(BF16)
                dq_ref[rows, 256 * g:256 * (g + 1)] = _unstack_heads(_dot(dsb, kk))
                dkk = _dot_tn(dsb, q4)
                dvv = _dot_tn(pkb, dob)
                dk_ref[prow, 64 * g:64 * g + 64] += dkk[:WINDOW]
                dk_ref[crow, 64 * g:64 * g + 64] += dkk[WINDOW:]
                dv_ref[prow, 64 * g:64 * g + 64] += dvv[:WINDOW]
                dv_ref[crow, 64 * g:64 * g + 64] += dvv[WINDOW:]

        @pl.when(n == nb - 1)
        def _():
            bk = bk_ref[...]
            for h in range(SWA_HEADS):
                dbh = dbias_acc[h]
                for b in range(NUM_BUCKETS):
                    val = jnp.sum(jnp.where(bk == b, dbh, 0.0), keepdims=True)
                    row = h * NUM_BUCKETS + b
                    drb_ref[row:row + 1, :] = jnp.broadcast_to(val, (1, 128))

    full = lambda shape: pl.BlockSpec(shape, lambda n: tuple(0 for _ in shape))
    return pl.pallas_call(
        body, name="swa_bwd", grid=(nb,),
        in_specs=_swa_specs() + [pl.BlockSpec((SWA_ROWS, 512), lambda n: (n, 0)),
                                 pl.BlockSpec((SWA_ROWS, 512), lambda n: (n, 0)), full((WINDOW, 2 * WINDOW))],
        out_specs=[pl.BlockSpec((SWA_ROWS, 512), lambda n: (n, 0)), full((S, 128)), full((S, 128)),
                   full((NUM_BUCKETS * 8, 128)), full((8, 128))],
        out_shape=[jax.ShapeDtypeStruct((S, 512), F32), jax.ShapeDtypeStruct((S, 128), F32),
                   jax.ShapeDtypeStruct((S, 128), F32), jax.ShapeDtypeStruct((NUM_BUCKETS * 8, 128), F32),
                   jax.ShapeDtypeStruct((8, 128), F32)],
        scratch_shapes=[pltpu.VMEM((SWA_HEADS, WINDOW, 2 * WINDOW), F32)],
        compiler_params=_params(("arbitrary",)),
    )(proj, proj, proj, proj, proj, bias, sinks, o, do, bucket)


def _rope_tables(S):
    inv = np.float32(ROPE_THETA) ** (-np.arange(0, MLA_ROPE, 2, dtype=np.float32) / np.float32(MLA_ROPE))
    ang = np.arange(S, dtype=np.float32)[:, None] * inv[None, :]
    cos, sin = np.cos(ang), np.sin(ang)
    return (jnp.asarray(np.tile(np.concatenate([cos, cos], axis=1), (1, 2))),
            jnp.asarray(np.tile(np.concatenate([-sin, sin], axis=1), (1, 2))))


def _rope_wide(ref):
    t = ref[...]
    return jnp.concatenate([t, t], axis=1)


def _swap_halves(x):
    w = x.shape[-1]
    lane = lax.broadcasted_iota(jnp.int32, x.shape, x.ndim - 1)
    return jnp.where((lane % 64) < 32, pltpu.roll(x, w - 32, x.ndim - 1), pltpu.roll(x, 32, x.ndim - 1))


def _mla_pre_fwd(proj, qn_w, kvn_w, wuqT, wukv, cos, sin, *, tm=ROW_TILE):
    S = proj.shape[0]
    tm = min(tm, S)

    def body(ql_ref, kl_ref, kr_ref, qw_ref, kw_ref, wuq_ref, wukv_ref, cos_ref, sin_ref,
             qc_ref, kc_ref, vv_ref):
        ql = ql_ref[...]
        qn = (ql * _rstd(ql) * qw_ref[...]).astype(BF16)
        q = _dot_nt(qn, wuq_ref[...])
        cs, sn = _rope_wide(cos_ref), _rope_wide(sin_ref)
        qr = q[:, 512:768]
        qr = qr * cs + _swap_halves(qr) * sn
        half = lax.broadcasted_iota(jnp.int32, (tm, 128), 1) // 64
        kl = kl_ref[...]
        kvn = (kl * _rstd(kl) * kw_ref[...]).astype(BF16)
        kr = kr_ref[...]
        kr = kr * cs[:, :128] + _swap_halves(kr) * sn[:, :128]
        kr2 = (kr + pltpu.roll(kr, 64, 1)).astype(BF16)
        kv = _dot(kvn, jnp.concatenate([wukv_ref[j] for j in range(2 * MLA_HEADS)], axis=1)).astype(BF16)
        for h in range(MLA_HEADS):
            qc_ref[h, :, 0:128] = q[:, 128 * h:128 * h + 128].astype(BF16)
            chunk = qr[:, 128 * (h // 2):128 * (h // 2) + 128]
            qc_ref[h, :, 128:256] = jnp.where(half == (h % 2), chunk, 0.0).astype(BF16)
            kc_ref[h, :, 0:128] = kv[:, 256 * h:256 * h + 128]
            kc_ref[h, :, 128:256] = kr2
            vv_ref[h] = kv[:, 256 * h + 128:256 * h + 256]

    const = lambda shape: pl.BlockSpec(shape, lambda i: tuple(0 for _ in shape))
    return pl.pallas_call(
        body, name="mla_pre_fwd", grid=(S // tm,),
        in_specs=[pl.BlockSpec((tm, 256), lambda i: (i, 3)), pl.BlockSpec((tm, 128), lambda i: (i, 8)),
                  pl.BlockSpec((tm, 128), lambda i: (i, 9)), const((1, 256)), const((1, 128)),
                  const((768, 256)), const((8, 128, 128)),
                  pl.BlockSpec((tm, 128), lambda i: (i, 0)), pl.BlockSpec((tm, 128), lambda i: (i, 0))],
        out_specs=[pl.BlockSpec((MLA_HEADS, tm, 256), lambda i: (0, i, 0)),
                   pl.BlockSpec((MLA_HEADS, tm, 256), lambda i: (0, i, 0)),
                   pl.BlockSpec((MLA_HEADS, tm, 128), lambda i: (0, i, 0))],
        out_shape=[jax.ShapeDtypeStruct((MLA_HEADS, S, 256), BF16), jax.ShapeDtypeStruct((MLA_HEADS, S, 256), BF16),
                   jax.ShapeDtypeStruct((MLA_HEADS, S, 128), BF16)],
        compiler_params=_params(("parallel",)),
    )(proj, proj, proj, qn_w, kvn_w, wuqT, wukv, cos, sin)


def _causal(i, j, t):
    row = i * t + lax.broadcasted_iota(jnp.int32, (t, t), 0)
    col = j * t + lax.broadcasted_iota(jnp.int32, (t, t), 1)
    return col <= row


def _mla_attn_fwd(qc, kc, vv, *, t=512):
    S = qc.shape[1]
    t = min(t, S)

    def body(q_ref, k_ref, v_ref, o_ref, l_ref):
        i = pl.program_id(0)
        diag = _causal(0, 0, t)

        def step(j, carry, masked):
            rows = pl.ds(pl.multiple_of(j * t, t), t)
            out = []
            for h in range(MLA_HEADS):
                m, l, acc = carry[h]
                s = _dot_nt(q_ref[h], k_ref[h, rows, :]) * MLA_SCALE
                if masked:
                    s = jnp.where(diag, s, -jnp.inf)
                m_new = jnp.maximum(m, jnp.max(s, axis=-1, keepdims=True))
                alpha = jnp.exp(m - m_new)
                p = jnp.exp(s - m_new)
                l = alpha * l + jnp.sum(p, axis=-1, keepdims=True)
                acc = alpha * acc + _dot(p.astype(BF16), v_ref[h, rows, :])
                out.append((m_new, l, acc))
            return tuple(out)

        init = tuple((jnp.full((t, 1), -jnp.inf, F32), jnp.zeros((t, 1), F32), jnp.zeros((t, MLA_V), F32))
                     for _ in range(MLA_HEADS))
        carry = lax.fori_loop(0, i, lambda j, c: step(j, c, False), init)
        carry = step(i, carry, True)
        for h in range(MLA_HEADS):
            m, l, acc = carry[h]
            o_ref[:, 128 * h:128 * h + 128] = acc / l
            l_ref[h] = jnp.broadcast_to(m + jnp.log(l), (t, 128))

    return pl.pallas_call(
        body, name="mla_attn_fwd", grid=(S // t,),
        in_specs=[pl.BlockSpec((MLA_HEADS, t, 256), lambda i: (0, i, 0)),
                  pl.BlockSpec((MLA_HEADS, S, 256), lambda i: (0, 0, 0)),
                  pl.BlockSpec((MLA_HEADS, S, 128), lambda i: (0, 0, 0))],
        out_specs=[pl.BlockSpec((t, 512), lambda i: (i, 0)),
                   pl.BlockSpec((MLA_HEADS, t, 128), lambda i: (0, i, 0))],
        out_shape=[jax.ShapeDtypeStruct((S, 512), F32), jax.ShapeDtypeStruct((MLA_HEADS, S, 128), F32)],
        compiler_params=_params(("parallel",)),
    )(qc, kc, vv)


def _mla_attn_bwd(qc, kc, vv, o, lse, do, *, t=512, tq=1024):
    S = qc.shape[1]
    t = min(t, S)
    tq = min(tq, S)
    nblk = S // t
    hp = MLA_HEADS
    once = pl.Buffered(1)

    def body(q_ref, k_ref, v_ref, o_ref, l_ref, do_ref, dq_ref, dk_ref, dv_ref):
        j = pl.program_id(1)

        @pl.when(j == 0)
        def _():
            dq_ref[...] = jnp.zeros_like(dq_ref)

        first = (j * t) // tq

        def step(i, carry, masked):
            rows = pl.ds(pl.multiple_of(i * tq, tq), tq)
            if masked:
                row = i * tq + lax.broadcasted_iota(jnp.int32, (tq, t), 0)
                col = j * t + lax.broadcasted_iota(jnp.int32, (tq, t), 1)
                visible = col <= row
            out = []
            for h in range(hp):
                dk, dv = carry[h]
                k = k_ref[h]
                q = q_ref[h, rows, :]
                dov = do_ref[rows, 128 * h:128 * h + 128]
                lrow = l_ref[h, rows, :][:, 0:1]
                p = jnp.exp(_dot_nt(q, k) * MLA_SCALE - lrow)
                if masked:
                    p = jnp.where(visible, p, 0.0)
                dob = dov.astype(BF16)
                dv = dv + _dot_tn(p.astype(BF16), dob)
                dp = _dot_nt(dob, v_ref[h])
                delta = jnp.sum(dov * o_ref[rows, 128 * h:128 * h + 128], axis=-1, keepdims=True)
                ds = (p * (dp - delta) * MLA_SCALE).astype(BF16)
                dk = dk + _dot_tn(ds, q)
                dq_ref[h, rows, :] += _dot(ds, k)
                out.append((dk, dv))
            return tuple(out)

        init = tuple((jnp.zeros((t, 256), F32), jnp.zeros((t, MLA_V), F32)) for _ in range(hp))
        carry = step(first, init, True)
        carry = lax.fori_loop(first + 1, S // tq, lambda i, c: step(i, c, False), carry)
        for h in range(hp):
            dk_ref[h] = carry[h][0]
            dv_ref[h] = carry[h][1]

    return pl.pallas_call(
        body, name="mla_attn_bwd", grid=(MLA_HEADS // hp, nblk),
        in_specs=[pl.BlockSpec((hp, S, 256), lambda g, j: (g, 0, 0), pipeline_mode=once),
                  pl.BlockSpec((hp, t, 256), lambda g, j: (g, j, 0)),
                  pl.BlockSpec((hp, t, 128), lambda g, j: (g, j, 0)),
                  pl.BlockSpec((S, 128 * hp), lambda g, j: (0, g), pipeline_mode=once),
                  pl.BlockSpec((hp, S, 128), lambda g, j: (g, 0, 0), pipeline_mode=once),
                  pl.BlockSpec((S, 128 * hp), lambda g, j: (0, g), pipeline_mode=once)],
        out_specs=[pl.BlockSpec((hp, S, 256), lambda g, j: (g, 0, 0)),
                   pl.BlockSpec((hp, t, 256), lambda g, j: (g, j, 0)),
                   pl.BlockSpec((hp, t, 128), lambda g, j: (g, j, 0))],
        out_shape=[jax.ShapeDtypeStruct((MLA_HEADS, S, 256), F32), jax.ShapeDtypeStruct((MLA_HEADS, S, 256), F32),
                   jax.ShapeDtypeStruct((MLA_HEADS, S, 128), F32)],
        compiler_params=_params(("parallel", "arbitrary")),
    )(qc, kc, vv, o, lse, do)


def _mla_pre_bwd(proj, qn_w, kvn_w, wuqT, wukv, cos, sin, dqc, dkc, dvv, *, tm=ROW_TILE):
    S = proj.shape[0]
    tm = min(tm, S)

    def body(ql_ref, kl_ref, qw_ref, kw_ref, wuq_ref, wukv_ref, cos_ref, sin_ref, dqc_ref, dkc_ref, dvv_ref,
             dql_ref, dkl_ref, dkr_ref, gq_ref, gkv_ref, part_ref, gq_acc, gkv_acc):
        @pl.when(pl.program_id(0) == 0)
        def _():
            gq_acc[...] = jnp.zeros_like(gq_acc)
            gkv_acc[...] = jnp.zeros_like(gkv_acc)
            part_ref[...] = jnp.zeros_like(part_ref)

        cs, sn = _rope_wide(cos_ref), _rope_wide(sin_ref)
        half = lax.broadcasted_iota(jnp.int32, (tm, 128), 1) // 64
        ql = ql_ref[...]
        rq = _rstd(ql)
        qhat = ql * rq
        qw = qw_ref[...]
        qn = (qhat * qw).astype(BF16)
        chunks = []
        for pair in range(2):
            chunks.append(jnp.where(half == 0, dqc_ref[2 * pair, :, 128:256], dqc_ref[2 * pair + 1, :, 128:256]))
        dqr = jnp.concatenate(chunks, axis=1)
        dqr = dqr * cs + _swap_halves(dqr * sn)
        dq = jnp.concatenate([dqc_ref[h, :, 0:128] for h in range(MLA_HEADS)] + [dqr], axis=1).astype(BF16)
        gq_acc[...] += _dot_tn(dq, qn)
        dqn = _dot(dq, wuq_ref[...])
        part_ref[0:1, :] += jnp.sum(dqn * qhat, axis=0, keepdims=True)
        dql_ref[...] = _rms_bwd(dqn * qw, qhat, rq)
        kl = kl_ref[...]
        rk = _rstd(kl)
        khat = kl * rk
        kw = kw_ref[...]
        kvn = (khat * kw).astype(BF16)
        dkvn = jnp.zeros((tm, MLA_KVR), F32)
        dkr2 = jnp.zeros((tm, 128), F32)
        for h in range(MLA_HEADS):
            dkn = dkc_ref[h, :, 0:128].astype(BF16)
            dvh = dvv_ref[h].astype(BF16)
            gkv_acc[2 * h] += _dot_tn(kvn, dkn)
            gkv_acc[2 * h + 1] += _dot_tn(kvn, dvh)
            dkvn += _dot_nt(dkn, wukv_ref[2 * h]) + _dot_nt(dvh, wukv_ref[2 * h + 1])
            dkr2 += dkc_ref[h, :, 128:256]
        part_ref[1:2, 0:128] += jnp.sum(dkvn * khat, axis=0, keepdims=True)
        dkl_ref[...] = _rms_bwd(dkvn * kw, khat, rk)
        dkr = jnp.where(half == 0, dkr2 + pltpu.roll(dkr2, 64, 1), 0.0)
        dkr_ref[...] = dkr * cs[:, :128] + _swap_halves(dkr * sn[:, :128])

        @pl.when(pl.program_id(0) == S // tm - 1)
        def _():
            gkv_ref[...] = gkv_acc[...].astype(BF16)
            per = MLA_NOPE + MLA_ROPE
            for h in range(MLA_HEADS):
                gq_ref[per * h:per * h + MLA_NOPE, :] = gq_acc[MLA_NOPE * h:MLA_NOPE * (h + 1), :].astype(BF16)
                gq_ref[per * h + MLA_NOPE:per * (h + 1), :] = gq_acc[512 + MLA_ROPE * h:512 + MLA_ROPE * (h + 1), :].astype(BF16)

    const = lambda shape: pl.BlockSpec(shape, lambda i: tuple(0 for _ in shape))
    heads = lambda w: pl.BlockSpec((MLA_HEADS, tm, w), lambda i: (0, i, 0))
    return pl.pallas_call(
        body, name="mla_pre_bwd", grid=(S // tm,),
        in_specs=[pl.BlockSpec((tm, 256), lambda i: (i, 3)), pl.BlockSpec((tm, 128), lambda i: (i, 8)),
                  const((1, 256)), const((1, 128)), const((768, 256)), const((8, 128, 128)),
                  pl.BlockSpec((tm, 128), lambda i: (i, 0)), pl.BlockSpec((tm, 128), lambda i: (i, 0)),
                  heads(256), heads(256), heads(128)],
        out_specs=[pl.BlockSpec((tm, 256), lambda i: (i, 0)), pl.BlockSpec((tm, 128), lambda i: (i, 0)),
                   pl.BlockSpec((tm, 128), lambda i: (i, 0)), const((768, 256)), const((8, 128, 128)), const((8, 256))],
        out_shape=[jax.ShapeDtypeStruct((S, 256), F32), jax.ShapeDtypeStruct((S, 128), F32),
                   jax.ShapeDtypeStruct((S, 128), F32), jax.ShapeDtypeStruct((768, 256), BF16),
                   jax.ShapeDtypeStruct((8, 128, 128), BF16), jax.ShapeDtypeStruct((8, 256), F32)],
        scratch_shapes=[pltpu.VMEM((768, 256), F32), pltpu.VMEM((8, 128, 128), F32)],
        compiler_params=_params(("arbitrary",)),
    )(proj, proj, qn_w, kvn_w, wuqT, wukv, cos, sin, dqc, dkc, dvv)


def _mix_out_fwd(x, oa, ob, w_o, vecs, *, tm=ROW_TILE):
    S = x.shape[0]
    tm = min(tm, S)

    def body(x_ref, oa_ref, ob_ref, w_ref, vec_ref, xo_ref, mo_ref):
        mo = _dot(oa_ref[...].astype(BF16), w_ref[0:512, :]) + _dot(ob_ref[...].astype(BF16), w_ref[512:1024, :])
        mo_ref[...] = mo
        xo_ref[...] = x_ref[...] + vec_ref[3:4, :] * mo

    row = pl.BlockSpec((tm, D), lambda i: (i, 0))
    half = pl.BlockSpec((tm, 512), lambda i: (i, 0))
    return pl.pallas_call(
        body, name="mix_out_fwd", grid=(S // tm,),
        in_specs=[row, half, half, pl.BlockSpec((D, D), lambda i: (0, 0)), pl.BlockSpec((8, D), lambda i: (0, 0))],
        out_specs=[row, row],
        out_shape=[jax.ShapeDtypeStruct((S, D), F32), jax.ShapeDtypeStruct((S, D), F32)],
        compiler_params=_params(("parallel",)),
    )(x, oa, ob, w_o, vecs)


def _mix_out_bwd(dxo, mo, oa, ob, w_o, vecs, *, tm=ROW_TILE):
    S = dxo.shape[0]
    tm = min(tm, S)

    def body(dx_ref, mo_ref, oa_ref, ob_ref, w_ref, vec_ref, doa_ref, dob_ref, gw_ref, part_ref, gw_acc):
        @pl.when(pl.program_id(0) == 0)
        def _():
            gw_acc[...] = jnp.zeros_like(gw_acc)
            part_ref[...] = jnp.zeros_like(part_ref)

        dx = dx_ref[...]
        part_ref[0:1, :] += jnp.sum(dx * mo_ref[...], axis=0, keepdims=True)
        dmo = (vec_ref[3:4, :] * dx).astype(BF16)
        doa_ref[...] = _dot_nt(dmo, w_ref[0:512, :])
        dob_ref[...] = _dot_nt(dmo, w_ref[512:1024, :])
        gw_acc[0:512, :] += _dot_tn(oa_ref[...].astype(BF16), dmo)
        gw_acc[512:1024, :] += _dot_tn(ob_ref[...].astype(BF16), dmo)

        @pl.when(pl.program_id(0) == S // tm - 1)
        def _():
            gw_ref[...] = gw_acc[...].astype(BF16)

    row = pl.BlockSpec((tm, D), lambda i: (i, 0))
    half = pl.BlockSpec((tm, 512), lambda i: (i, 0))
    return pl.pallas_call(
        body, name="mix_out_bwd", grid=(S // tm,),
        in_specs=[row, row, half, half, pl.BlockSpec((D, D), lambda i: (0, 0)), pl.BlockSpec((8, D), lambda i: (0, 0))],
        out_specs=[half, half, pl.BlockSpec((D, D), lambda i: (0, 0)), pl.BlockSpec((8, D), lambda i: (0, 0))],
        out_shape=[jax.ShapeDtypeStruct((S, 512), F32), jax.ShapeDtypeStruct((S, 512), F32),
                   jax.ShapeDtypeStruct((D, D), BF16), jax.ShapeDtypeStruct((8, D), F32)],
        scratch_shapes=[pltpu.VMEM((D, D), F32)],
        compiler_params=_params(("arbitrary",)),
    )(dxo, mo, oa, ob, w_o, vecs)


def _mix_in_bwd(h, w_inT, dq, dk, dv, dql, dkl, dkr, *, tm=ROW_TILE):
    S = h.shape[0]
    tm = min(tm, S)
    wid = (512, 128, 128, 256, 128, 128)

    def body(h_ref, w_ref, dq_ref, dk_ref, dv_ref, dql_ref, dkl_ref, dkr_ref, dh_ref, gw_ref):
        @pl.when(pl.program_id(0) == 0)
        def _():
            gw_ref[...] = jnp.zeros_like(gw_ref)

        parts = (dq_ref, dk_ref, dv_ref, dql_ref, dkl_ref, dkr_ref)
        dproj = jnp.concatenate([ref[...].astype(BF16) for ref in parts], axis=1)
        dh_ref[...] = _dot(dproj, w_ref[...])
        gw_ref[...] += _dot_tn(dproj, h_ref[...])[0:D_IN, :]

    row = pl.BlockSpec((tm, D), lambda i: (i, 0))
    part = lambda w: pl.BlockSpec((tm, w), lambda i: (i, 0))
    return pl.pallas_call(
        body, name="mix_in_bwd", grid=(S // tm,),
        in_specs=[row, pl.BlockSpec((D_IN_PAD, D), lambda i: (0, 0))] + [part(w) for w in wid],
        out_specs=[row, pl.BlockSpec((D_IN, D), lambda i: (0, 0))],
        out_shape=[jax.ShapeDtypeStruct((S, D), F32), jax.ShapeDtypeStruct((D_IN, D), F32)],
        compiler_params=_params(("arbitrary",)),
    )(h, w_inT, dq, dk, dv, dql, dkl, dkr)


def _vecs(norm_w, mod9, k):
    return jnp.concatenate([norm_w.reshape(1, D), mod9[3 * k:3 * k + 3], jnp.zeros((4, D), F32)], axis=0)


def _uq_group_rows(wuqT):
    per = MLA_NOPE + MLA_ROPE
    nope = [wuqT[per * h:per * h + MLA_NOPE] for h in range(MLA_HEADS)]
    rope = [wuqT[per * h + MLA_NOPE:per * (h + 1)] for h in range(MLA_HEADS)]
    return jnp.concatenate(nope + rope, axis=0)


def _local_step(x, tgt, mod9, norms, sinks, rel_bias, q_norm, kv_norm, W, on_grads=None):
    if on_grads is None:
        on_grads = lambda group, grads, after, vecs: vecs
    S = x.shape[0]
    v1 = _vecs(norms["ffn1"], mod9, 0)
    v2 = _vecs(norms["mix"], mod9, 1)
    v3 = _vecs(norms["ffn2"], mod9, 2)
    bucket = jnp.asarray(_bucket_table())
    cos, sin = _rope_tables(S)
    if isinstance(W, dict):
        full, W = W, (lambda group, after, vecs: (full, vecs))

    W1, v1 = W("ffn1", [], v1)
    x1, h1, a1, b1, f1 = _ffn_fwd(x, v1, W1["g1T"], W1["u1T"], W1["d1"], name="ffn1_fwd")
    W2, v2 = W("mixer", [x1], v2)
    wuqT = _uq_group_rows(W2["w_uqT"])
    h2, proj, w_inT = _mix_in_fwd(x1, v2, W2["w_inT"])
    oa, bias = _swa_fwd(proj, rel_bias, bucket, sinks)
    qc, kc, vv = _mla_pre_fwd(proj, q_norm, kv_norm, wuqT, W2["w_ukv"], cos, sin)
    ob, lse = _mla_attn_fwd(qc, kc, vv)
    _, v2o = W("ffn2_on_its_way", [ob], v2)
    x2, mo = _mix_out_fwd(x1, oa, ob, W2["w_o"], v2o)
    W3, v3 = W("ffn2", [x2], v3)
    x3, h3, a3, b3, f3 = _ffn_fwd(x2, v3, W3["g3T"], W3["u3T"], W3["d3"], name="ffn2_fwd")
    dx3, head_part, df3 = _head(x3, tgt, norms["final"], f3, v3)

    gg3, gu3, gd3, dh3 = _ffn_bwd_main(h3, df3, a3, b3, W3["g3T"], W3["u3T"], W3["d3"], name="ffn2_bwd")
    ffn2 = {"g3T": gg3, "u3T": gu3, "d3": gd3}
    v3 = on_grads("ffn2", ffn2, [], v3)
    dx2, n3_part = _norm_bwd(dh3, x2, dx3, v3, name="ffn2_norm_bwd")
    v2 = on_grads("ffn2", None, [dx2], v2)
    doa, dob, g_wo, g2_part = _mix_out_bwd(dx2, mo, oa, ob, W2["w_o"], v2)
    dq, dk, dv, drb, dsk = _swa_bwd(proj, bias, sinks, oa, doa, bucket)
    dqc, dkc, dvv = _mla_attn_bwd(qc, kc, vv, ob, lse, dob)
    dql, dkl, dkr, g_uq, g_ukv, mla_part = _mla_pre_bwd(proj, q_norm, kv_norm, wuqT, W2["w_ukv"], cos, sin, dqc, dkc, dvv)
    dh2, g_win = _mix_in_bwd(h2, w_inT, dq, dk, dv, dql, dkl, dkr)
    mixer = {"w_inT": g_win, "w_uqT": g_uq, "w_ukv": g_ukv, "w_o": g_wo}
    v2 = on_grads("mixer", mixer, [], v2)
    dx1, n2_part, df1 = _norm_bwd(dh2, x1, dx2, v2, name="mix_norm_bwd", below=(f1, v1))
    started = on_grads("mixer", None, [dx1], jnp.zeros((1, 1), F32))
    gg1, gu1, gd1, dh1 = _ffn_bwd_main(h1, df1, a1, b1, W1["g1T"], W1["u1T"], W1["d1"], name="ffn1_bwd",
                                       after=[started])
    ffn1 = {"g1T": gg1, "u1T": gu1, "d1": gd1}
    v1 = on_grads("ffn1", ffn1, [], v1)
    dx0, n1_part = _norm_bwd(dh1, x, dx1, v1, name="ffn1_norm_bwd")

    grads = {**ffn1, **ffn2, **mixer}
    return head_part[1, 0], dx0, grads, _pack_vec(n1_part, n2_part, n3_part, head_part, g2_part, mla_part, dsk, drb)


SMALL_LAYOUT = (("norm_ffn1", 1024), ("norm_mix", 1024), ("norm_ffn2", 1024), ("norm_final", 1024),
                ("q_norm", 256), ("kv_norm", 128), ("sinks", 128), ("rel_bias", 256))
N_SMALL = sum(n for _, n in SMALL_LAYOUT)
LOSS_SLOT = 4 * 1024 + 256 + 128 + SWA_HEADS
N_MODVEC = N_MOD * D
N_VEC = N_MODVEC + N_SMALL


def _pack_vec(n1, n2, n3, head, g2, mla, dsk, drb):
    def body(n1_ref, n2_ref, n3_ref, head_ref, g2_ref, mla_ref, dsk_ref, drb_ref, out_ref):
        rows = [n1_ref[1:2, :], n1_ref[2:3, :], n2_ref[3:4, :], n2_ref[1:2, :], n2_ref[2:3, :], g2_ref[0:1, :],
                n3_ref[1:2, :], n3_ref[2:3, :], head_ref[3:4, :],
                n1_ref[0:1, :], n2_ref[0:1, :], n3_ref[0:1, :], head_ref[0:1, :]]
        for i, row in enumerate(rows):
            out_ref[:, D * i:D * (i + 1)] = row
        off = D * len(rows)
        out_ref[:, off:off + 256] = mla_ref[0:1, :]
        out_ref[:, off + 256:off + 384] = mla_ref[1:2, 0:128]

        def diagonal(block):
            r = lax.broadcasted_iota(jnp.int32, block.shape, 0)
            lane = lax.broadcasted_iota(jnp.int32, block.shape, 1)
            return jnp.sum(jnp.where(r == lane, block, 0.0), axis=0, keepdims=True)

        lane = lax.broadcasted_iota(jnp.int32, (1, 128), 1)
        out_ref[:, off + 384:off + 512] = jnp.where(lane == SWA_HEADS, head_ref[1:2, 0:128], diagonal(dsk_ref[...]))
        out_ref[:, off + 512:off + 640] = diagonal(drb_ref[0:128, :])
        out_ref[:, off + 640:off + 768] = diagonal(drb_ref[128:256, :])

    vm = pl.BlockSpec(memory_space=pltpu.VMEM)
    return pl.pallas_call(body, name="pack_vec", in_specs=[vm] * 8, out_specs=vm,
                          out_shape=jax.ShapeDtypeStruct((1, N_VEC), F32))(n1, n2, n3, head, g2, mla, dsk, drb)


def _coords():
    return lax.axis_index("x"), lax.axis_index("y"), lax.axis_index("c")


def _flip(v, bit):
    return 1 - v if bit else v


def _peer(r):
    x, y, c = _coords()
    return (_flip(x, r & 4), _flip(y, r & 2), _flip(c, r & 1))


class _ModExchange:
    def __init__(self, c_ref, w_ref, b_ref, mod_ref, ca_ref, call_ref, part_ref, send_sems, recv_sems):
        self.refs = (c_ref, w_ref, b_ref, mod_ref, ca_ref, call_ref, part_ref)
        self.sems = (send_sems, recv_sems)
        x, y, c = _coords()
        self.me = 4 * x + 2 * y + c
        self.sends = []

    def _copy(self, phase, r):
        c_ref, _, _, mod_ref, _, call_ref, part_ref = self.refs
        src, dst = (c_ref, call_ref.at[self.me]) if phase == 0 else (part_ref.at[self.me ^ r], mod_ref.at[self.me])
        return pltpu.make_async_remote_copy(src, dst, self.sems[0].at[phase, r], self.sems[1].at[phase, r],
                                            device_id=_peer(r), device_id_type=MESH)

    def _start(self, phase):
        for r in range(1, N_DEV):
            self.sends.append(self._copy(phase, r))
            self.sends[-1].start()

    def begin(self):
        c_ref, _, _, _, _, call_ref, _ = self.refs
        call_ref[self.me] = c_ref[...]
        self._start(0)

    def middle(self):
        _, w_ref, b_ref, mod_ref, ca_ref, call_ref, part_ref = self.refs
        for r in range(1, N_DEV):
            self._copy(0, r).wait_recv()
        cv = call_ref[...].reshape(8 * N_DEV, D)
        ca = (cv * _sigmoid(cv)).astype(BF16)
        ca_ref[...] = ca
        part_ref[...] = _dot(ca, w_ref[...].astype(BF16)).reshape(part_ref.shape)
        mod_ref[self.me] = part_ref[self.me] + b_ref[self.me]
        self._start(1)

    def end(self):
        _, _, b_ref, mod_ref, _, _, _ = self.refs
        for r in range(1, N_DEV):
            self._copy(1, r).wait_recv()
            mod_ref[self.me ^ r] = mod_ref[self.me ^ r] + b_ref[self.me ^ r]
        for cp in self.sends:
            cp.wait_send()


def _mod_bwd(allvec, ca, me_idx):
    W = N_MODVEC // N_DEV

    def body(me_ref, all_ref, cols_ref, ca_ref, gw_ref, sum_ref):
        in_first_row = lax.broadcasted_iota(jnp.int32, (N_DEV, 8, W), 1) == 0
        dm = jnp.where(in_first_row, cols_ref[...], 0.0).reshape(8 * N_DEV, W)
        gw_ref[...] = _dot_tn(ca_ref[...], dm.astype(BF16))
        total = all_ref[0]
        for k in range(1, N_DEV):
            total = total + all_ref[k]
        sum_ref[...] = total

    return pl.pallas_call(
        body, name="mod_bwd",
        grid_spec=pltpu.PrefetchScalarGridSpec(
            num_scalar_prefetch=1, grid=(1,),
            in_specs=[pl.BlockSpec((N_DEV, 1, N_VEC), lambda i, me: (0, 0, 0)),
                      pl.BlockSpec((N_DEV, 1, W), lambda i, me: (0, 0, me[0])),
                      pl.BlockSpec((8 * N_DEV, D), lambda i, me: (0, 0))],
            out_specs=[pl.BlockSpec((D, W), lambda i, me: (0, 0)), pl.BlockSpec((1, N_VEC), lambda i, me: (0, 0))]),
        out_shape=[jax.ShapeDtypeStruct((D, W), F32), jax.ShapeDtypeStruct((1, N_VEC), F32)],
        compiler_params=_params(("arbitrary",)),
    )(me_idx, allvec, allvec, ca)


def _wgather(shards, c_tile, w_mod, b_mod3):
    n = len(shards)
    rows = [s.shape[0] for s in shards]

    W = w_mod.shape[1]

    def body(*refs):
        ins, (c_ref, w_ref, b_ref), outs = refs[:n], refs[n:n + 3], refs[n + 3:2 * n + 3]
        token, mod_ref, ca_ref, send_sems, recv_sems, local_sems = refs[2 * n + 3:2 * n + 9]
        mod = _ModExchange(c_ref, w_ref, b_ref, mod_ref, ca_ref, *refs[2 * n + 9:])
        mod.begin()
        token[...] = jnp.zeros_like(token)
        x, y, c = _coords()
        me = 4 * x + 2 * y + c
        sib, xn, yn = (x, y, 1 - c), (1 - x, y, c), (x, 1 - y, c)
        block = lambda px, py, pc: 4 * px + 2 * py + pc

        def part(k, blk, half):
            if half is None:
                return outs[k].at[blk]
            return outs[k].at[blk, pl.ds(half * (rows[k] // 2), rows[k] // 2)]

        def copy(k, slot, blk, to, half=None, src=None):
            ref = part(k, blk, half)
            return pltpu.make_async_remote_copy(
                src_ref=ref if src is None else src, dst_ref=ref, send_sem=send_sems.at[k, slot],
                recv_sem=recv_sems.at[k, slot], device_id=to, device_id_type=MESH)

        local = [pltpu.make_async_copy(ins[k], outs[k].at[me], local_sems.at[k]) for k in range(n)]
        for cp in local:
            cp.start()
        sent = [copy(k, slot, me, to, src=ins[k]) for k in range(n) for slot, to in ((0, sib), (1, xn), (2, yn))]
        for cp in sent:
            cp.start()
        mod.middle()
        bx, by, bd = block(1 - x, y, c), block(x, 1 - y, c), block(1 - x, 1 - y, c)
        for k in range(n):
            copy(k, 1, bx, sib).wait_recv()
            sent += [copy(k, 4, bx, yn, half=1), copy(k, 5, bx, sib)]
            sent[-2].start()
            sent[-1].start()
        for k in range(n):
            copy(k, 2, by, sib).wait_recv()
            sent += [copy(k, 3, by, xn, half=0), copy(k, 6, by, sib)]
            sent[-2].start()
            sent[-1].start()
        for k in range(n):
            copy(k, 3, bd, sib, half=0).wait_recv()
            copy(k, 4, bd, sib, half=1).wait_recv()
            sent.append(copy(k, 7, bd, sib))
            sent[-1].start()
        for k in range(n):
            copy(k, 0, block(x, y, 1 - c), sib).wait_recv()
            for slot, blk in ((5, block(1 - x, y, 1 - c)), (6, block(x, 1 - y, 1 - c)), (7, block(1 - x, 1 - y, 1 - c))):
                copy(k, slot, blk, sib).wait_recv()
        for cp in sent:
            cp.wait_send()
        for cp in local:
            cp.wait()
        mod.end()

    anyspec, vm = pl.BlockSpec(memory_space=pl.ANY), pl.BlockSpec(memory_space=pltpu.VMEM)
    return pl.pallas_call(
        body, name="wgather", in_specs=[anyspec] * n + [vm] * 3,
        out_specs=[anyspec] * n + [vm] * 3,
        out_shape=[jax.ShapeDtypeStruct((N_DEV,) + s.shape, s.dtype) for s in shards]
        + [jax.ShapeDtypeStruct((8, 128), F32), jax.ShapeDtypeStruct((N_DEV, 8, W), F32),
           jax.ShapeDtypeStruct((8 * N_DEV, D), BF16)],
        scratch_shapes=[pltpu.SemaphoreType.DMA((n, 8)), pltpu.SemaphoreType.DMA((n, 8)),
                        pltpu.SemaphoreType.DMA((n,)),
                        pltpu.VMEM((N_DEV, 8, D), F32), pltpu.VMEM((N_DEV, 8, W), F32),
                        pltpu.SemaphoreType.DMA((2, N_DEV)), pltpu.SemaphoreType.DMA((2, N_DEV))],
        compiler_params=_params(),
    )(*shards, c_tile, w_mod, b_mod3)


class _GatherCopies:
    def __init__(self, lands, send_sems, recv_sems, k0=0, batches=None):
        x, y, c = _coords()
        me = 4 * x + 2 * y + c
        sib = (x, y, 1 - c)
        chips = [(1 - x, y), (x, 1 - y), (1 - x, 1 - y)]

        def copy(k, slot, block, to):
            return pltpu.make_async_remote_copy(
                src_ref=lands[k].at[block], dst_ref=lands[k].at[block],
                send_sem=send_sems.at[7 * (k0 + k) + slot], recv_sem=recv_sems.at[7 * (k0 + k) + slot],
                device_id=to, device_id_type=MESH)

        n = len(lands)
        self.first = [copy(k, 0, me, sib) for k in range(n)]
        for batch in batches or [range(n)]:
            self.first += [copy(k, 1 + j, me, (cx, cy, c)) for j, (cx, cy) in enumerate(chips) for k in batch]
        self.landed = [copy(k, 1 + j, 4 * cx + 2 * cy + c, sib) for j, (cx, cy) in enumerate(chips) for k in range(n)]
        self.passed = [copy(k, 4 + j, 4 * cx + 2 * cy + c, sib) for j, (cx, cy) in enumerate(chips) for k in range(n)]
        self.from_sib = [copy(k, 0, 4 * x + 2 * y + (1 - c), sib) for k in range(n)]
        self.from_sib += [copy(k, 4 + j, 4 * cx + 2 * cy + (1 - c), sib) for j, (cx, cy) in enumerate(chips)
                          for k in range(n)]


def _gather_start(lands, *, name, batches=None):
    n = len(lands)

    def body(*refs):
        for cp in _GatherCopies(refs[:n], refs[n], refs[n + 1], batches=batches).first:
            cp.start()
        refs[-1][...] = jnp.zeros_like(refs[-1])

    out = pl.pallas_call(
        body, name=name,
        out_shape=(pltpu.SemaphoreType.DMA((7 * n,)), pltpu.SemaphoreType.DMA((7 * n,)),
                   *[pltpu.HBM(l.shape, l.dtype) for l in lands], jax.ShapeDtypeStruct((8, 128), F32)),
        in_specs=[HBM_SPEC] * n,
        out_specs=(SEM_SPEC, SEM_SPEC, *[HBM_SPEC] * n, pl.BlockSpec(memory_space=pltpu.VMEM)),
        input_output_aliases={i: 2 + i for i in range(n)},
        compiler_params=pltpu.CompilerParams(has_side_effects=DATAFLOW),
    )(*[_in_hbm(l) for l in lands])
    return out[0], out[1], list(out[2:2 + n]), out[-1]


def _gather_pass(send_sems, recv_sems, lands, after, *, name, stage, k0=0):
    n = len(lands)

    def body(*refs):
        cps = _GatherCopies(refs[:n], refs[n], refs[n + 1], k0)
        if stage == "landed":
            for cp in cps.landed:
                cp.wait_recv()
        else:
            for cp in cps.passed:
                cp.start()
        refs[-1][...] = jnp.zeros_like(refs[-1])

    out = pl.pallas_call(
        body, name=name,
        out_shape=(*[pltpu.HBM(l.shape, l.dtype) for l in lands], jax.ShapeDtypeStruct((8, 128), F32)),
        in_specs=[HBM_SPEC] * n + [SEM_SPEC, SEM_SPEC] + [pl.BlockSpec(memory_space=pl.ANY)] * len(after),
        out_specs=(*[HBM_SPEC] * n, pl.BlockSpec(memory_space=pltpu.VMEM)),
        input_output_aliases={i: i for i in range(n)},
        compiler_params=pltpu.CompilerParams(has_side_effects=DATAFLOW),
    )(*lands, send_sems, recv_sems, *after)
    return list(out[:n]), out[-1]


def _gather_end(send_sems, recv_sems, lands, after, *, name, k0=0):
    n = len(lands)

    def body(*refs):
        cps = _GatherCopies(refs[:n], refs[n], refs[n + 1], k0)
        for cp in cps.from_sib:
            cp.wait_recv()
        for cp in cps.first + cps.passed:
            cp.wait_send()

    out = pl.pallas_call(
        body, name=name,
        out_shape=[pltpu.HBM(l.shape, l.dtype) for l in lands],
        in_specs=[HBM_SPEC] * n + [SEM_SPEC, SEM_SPEC] + [pl.BlockSpec(memory_space=pl.ANY)] * len(after),
        out_specs=[HBM_SPEC] * n,
        input_output_aliases={i: i for i in range(n)},
        compiler_params=pltpu.CompilerParams(has_side_effects=DATAFLOW),
    )(*lands, send_sems, recv_sems, *after)
    return list(out)


def _d2d_copies(grads, lands, send_sems, recv_sems):
    x, y, c = _coords()
    return [pltpu.make_async_remote_copy(
        src_ref=grads[k].at[2 * q + (1 - c)], dst_ref=lands[k].at[q],
        send_sem=send_sems.at[4 * k + q], recv_sem=recv_sems.at[4 * k + q],
        device_id=(x, y, 1 - c), device_id_type=MESH) for k in range(len(grads)) for q in range(4)]


def _direct_copies(grads, lands, send_sems, recv_sems):
    x, y, c = _coords()
    me = 4 * x + 2 * y + c
    return [pltpu.make_async_remote_copy(
        src_ref=grads[k].at[me ^ r], dst_ref=lands[k].at[r - 1],
        send_sem=send_sems.at[7 * k + r - 1], recv_sem=recv_sems.at[7 * k + r - 1],
        device_id=_peer(r), device_id_type=MESH) for k in range(len(grads)) for r in range(1, N_DEV)]


def _vec_copies(srcs, lands, send_sems, recv_sems):
    x, y, c = _coords()
    me = 4 * x + 2 * y + c
    return [pltpu.make_async_remote_copy(
        src_ref=lands[0].at[me], dst_ref=lands[0].at[me], send_sem=send_sems.at[r - 1], recv_sem=recv_sems.at[r - 1],
        device_id=_peer(r), device_id_type=MESH) for r in range(1, N_DEV)]


def _chipsum(gs, sibs, cidx, *, name):
    n = len(gs)

    def body(c_ref, *refs):
        for k in range(n):
            refs[2 * n + k][...] = (refs[k][...].astype(F32) + refs[n + k][...].astype(F32)).astype(refs[2 * n + k].dtype)

    mine = [pl.BlockSpec((1,) + g.shape[1:], lambda q, c_ref: (2 * q + c_ref[0], 0, 0)) for g in gs]
    other = [pl.BlockSpec((1,) + g.shape[1:], lambda q, c_ref: (q, 0, 0)) for g in gs]
    return pl.pallas_call(
        body, name=name,
        grid_spec=pltpu.PrefetchScalarGridSpec(num_scalar_prefetch=1, grid=(4,), in_specs=mine + other, out_specs=other),
        out_shape=[jax.ShapeDtypeStruct((4,) + g.shape[1:], g.dtype) for g in gs],
        compiler_params=_params(("arbitrary",)),
    )(cidx, *gs, *sibs)


HBM_SPEC = pl.BlockSpec(memory_space=pltpu.HBM)
SEM_SPEC = pl.BlockSpec(memory_space=pltpu.SEMAPHORE)
DATAFLOW = pltpu.SideEffectType.DATAFLOW_SIDE_EFFECTING


def _in_hbm(a):
    return pltpu.with_memory_space_constraint(a, pltpu.HBM)


def _rs_step1_copies(sums, lands, send_sems, recv_sems):
    n = len(sums)
    direct, relay = lands[:n], lands[n:]
    x, y, c = _coords()
    xn, yn = (1 - x, y, c), (x, 1 - y, c)
    qx, qy, qd = 2 * (1 - x) + y, 2 * x + (1 - y), 2 * (1 - x) + (1 - y)
    cps = []
    for k in range(n):
        h = sums[k].shape[1] // 2
        a, b = pl.ds(0, h), pl.ds(h, h)
        moves = ((sums[k].at[qx, a], direct[k].at[0], xn), (sums[k].at[qy, b], direct[k].at[1], yn),
                 (sums[k].at[qd, a], relay[k].at[0], xn), (sums[k].at[qd, b], relay[k].at[1], yn))
        for s, (src, dst, to) in enumerate(moves):
            cps.append(pltpu.make_async_remote_copy(
                src_ref=src, dst_ref=dst, send_sem=send_sems.at[4 * k + s], recv_sem=recv_sems.at[4 * k + s],
                device_id=to, device_id_type=MESH))
    return cps


def _rs_step2_copies(relayed, lands, send_sems, recv_sems, k0=0):
    x, y, c = _coords()
    cps = []
    for k in range(len(relayed)):
        for s, to in enumerate(((1 - x, y, c), (x, 1 - y, c))):
            cps.append(pltpu.make_async_remote_copy(
                src_ref=relayed[k].at[s], dst_ref=lands[k].at[s], send_sem=send_sems.at[2 * (k0 + k) + s],
                recv_sem=recv_sems.at[2 * (k0 + k) + s], device_id=to, device_id_type=MESH))
    return cps


def _relay_sum(sums, relay, qxy, *, name):
    n = len(sums)

    def body(q_ref, *refs):
        for k in range(n):
            refs[2 * n + k][...] = (refs[k][...].astype(F32) + refs[n + k][...].astype(F32)).astype(refs[2 * n + k].dtype)

    half = lambda s: (1, s.shape[1] // 2) + s.shape[2:]
    return pl.pallas_call(
        body, name=name,
        grid_spec=pltpu.PrefetchScalarGridSpec(
            num_scalar_prefetch=1, grid=(2,),
            in_specs=[pl.BlockSpec(half(s), lambda t, q_ref: (q_ref[t], 1 - t, 0)) for s in sums]
            + [pl.BlockSpec(half(s), lambda t, q_ref: (1 - t, 0, 0)) for s in sums],
            out_specs=[pl.BlockSpec(half(s), lambda t, q_ref: (t, 0, 0)) for s in sums]),
        out_shape=[jax.ShapeDtypeStruct((2,) + half(s)[1:], s.dtype) for s in sums],
        compiler_params=_params(("arbitrary",)),
    )(qxy, *sums, *relay)


def _split_start(copies, srcs, lands, n_sems, after, *, name):
    ns, nl = len(srcs), len(lands)

    def body(*refs):
        for cp in copies(refs[:ns], refs[ns:ns + nl], refs[ns + nl + len(after)], refs[ns + nl + len(after) + 1]):
            cp.start()
        refs[-1][...] = jnp.zeros_like(refs[-1])

    bufs = [_in_hbm(a) for a in list(srcs) + list(lands)]
    out = pl.pallas_call(
        body, name=name,
        out_shape=(pltpu.SemaphoreType.DMA((n_sems,)), pltpu.SemaphoreType.DMA((n_sems,)),
                   *[pltpu.HBM(a.shape, a.dtype) for a in bufs], jax.ShapeDtypeStruct((8, 128), F32)),
        in_specs=[HBM_SPEC] * len(bufs) + [pl.BlockSpec(memory_space=pl.ANY)] * len(after),
        out_specs=(SEM_SPEC, SEM_SPEC, *[HBM_SPEC] * len(bufs), pl.BlockSpec(memory_space=pltpu.VMEM)),
        input_output_aliases={i: 2 + i for i in range(len(bufs))},
        compiler_params=pltpu.CompilerParams(has_side_effects=DATAFLOW),
    )(*bufs, *after)
    return out[0], out[1], list(out[2:2 + ns]), list(out[2 + ns:2 + ns + nl]), out[-1]


def _split_wait(copies, send_sems, recv_sems, srcs, lands, after, *, name):
    ns, nl = len(srcs), len(lands)

    def body(*refs):
        for cp in copies(refs[:ns], refs[ns:ns + nl], refs[ns + nl], refs[ns + nl + 1]):
            cp.wait_send()
            cp.wait_recv()

    out = pl.pallas_call(
        body, name=name,
        out_shape=[pltpu.HBM(a.shape, a.dtype) for a in list(srcs) + list(lands)],
        in_specs=[HBM_SPEC] * (ns + nl) + [SEM_SPEC, SEM_SPEC] + [pl.BlockSpec(memory_space=pl.ANY)] * len(after),
        out_specs=[HBM_SPEC] * (ns + nl),
        input_output_aliases={i: i for i in range(ns + nl)},
        compiler_params=pltpu.CompilerParams(has_side_effects=DATAFLOW),
    )(*srcs, *lands, send_sems, recv_sems, *after)
    return list(out[:ns]), list(out[ns:])


ADAM_C1 = 1.0 / (1.0 - ADAM_B1 ** ADAM_STEP)
ADAM_C2 = 1.0 / (1.0 - ADAM_B2 ** ADAM_STEP)


def _adam_math(w, g, m, v):
    m2 = ADAM_B1 * m + (1.0 - ADAM_B1) * g
    v2 = ADAM_B2 * v + (1.0 - ADAM_B2) * (g * g)
    return -ADAM_LR * ((m2 * ADAM_C1) / (jnp.sqrt(v2 * ADAM_C2) + ADAM_EPS) + ADAM_WD * w), m2, v2


def _adamw(w, g, m, v, *, name, after=()):
    R, C = w.shape
    tr = R if R <= 512 else 256

    def body(w_ref, g_ref, m_ref, v_ref, *rest):
        d_ref, nm_ref, nv_ref = rest[len(after):]
        d_ref[...], nm_ref[...], nv_ref[...] = _adam_math(w_ref[...], g_ref[...], m_ref[...], v_ref[...])

    blk = pl.BlockSpec((tr, C), lambda i: (i, 0))
    return pl.pallas_call(
        body, name=name, grid=(R // tr,), in_specs=[blk] * 4 + [pl.BlockSpec(memory_space=pl.ANY)] * len(after),
        out_specs=[blk] * 3, out_shape=[jax.ShapeDtypeStruct((R, C), F32)] * 3,
        compiler_params=_params(("parallel",)),
    )(w, g, m, v, *after)


def _adamw_rs2(wmv, cs, direct, second, qidx, *, name):
    n = len(wmv)
    r, cc = wmv[0][0].shape
    h = r // 2

    def body(q_ref, *refs):
        ins, outs = refs[:6 * n], refs[6 * n:]
        for k in range(n):
            w_ref, m_ref, v_ref, c_ref, d1_ref, d2_ref = ins[6 * k:6 * k + 6]
            g_ref, d_ref, nm_ref, nv_ref = outs[4 * k:4 * k + 4]
            g = (c_ref[0].astype(F32) + d1_ref[0].astype(F32)) + d2_ref[0].astype(F32)
            g_ref[...] = g
            d_ref[...], nm_ref[...], nv_ref[...] = _adam_math(w_ref[...], g, m_ref[...], v_ref[...])

    blk = pl.BlockSpec((h, cc), lambda i, q_ref: (i, 0))
    one = [blk, blk, blk, pl.BlockSpec((1, h, cc), lambda i, q_ref: (q_ref[0], i, 0)),
           pl.BlockSpec((1, h, cc), lambda i, q_ref: (i, 0, 0)),
           pl.BlockSpec((1, h, cc), lambda i, q_ref: (1 - i, 0, 0))]
    out = pl.pallas_call(
        body, name=name,
        grid_spec=pltpu.PrefetchScalarGridSpec(num_scalar_prefetch=1, grid=(2,), in_specs=one * n,
                                               out_specs=[blk] * (4 * n)),
        out_shape=[jax.ShapeDtypeStruct((r, cc), F32)] * (4 * n),
        compiler_params=_params(("arbitrary",)),
    )(qidx, *[a for (w, m, v), c, d1, d2 in zip(wmv, cs, direct, second) for a in (w, m, v, c, d1, d2)])
    return [tuple(out[4 * k:4 * k + 4]) for k in range(n)]


def _adamw_rs(wmv, cs, rcv, qidx, *, name):
    n = len(wmv)
    shapes = [w.shape for w, _, _ in wmv]
    n_rcv = rcv[0].shape[0]
    halved = len(set(shapes)) == 1 and shapes[0][0] % 32 == 0 and shapes[0][0] > 128
    tiles = 2 if halved else 1

    def body(q_ref, *refs):
        ins, outs = refs[:5 * n], refs[5 * n:]
        for k in range(n):
            w_ref, m_ref, v_ref, c_ref, r_ref = ins[5 * k:5 * k + 5]
            g_ref, d_ref, nm_ref, nv_ref = outs[4 * k:4 * k + 4]
            g = c_ref[0].astype(F32)
            for j in range(n_rcv):
                g = g + r_ref[j].astype(F32)
            g_ref[...] = g
            d_ref[...], nm_ref[...], nv_ref[...] = _adam_math(w_ref[...], g, m_ref[...], v_ref[...])

    in_specs, out_specs = [], []
    for r, cc in shapes:
        blk = pl.BlockSpec((r // tiles, cc), lambda i, q_ref: (i, 0))
        in_specs += [blk, blk, blk, pl.BlockSpec((1, r // tiles, cc), lambda i, q_ref: (q_ref[0], i, 0)),
                     pl.BlockSpec((n_rcv, r // tiles, cc), lambda i, q_ref: (0, i, 0))]
        out_specs += [blk] * 4
    out = pl.pallas_call(
        body, name=name,
        grid_spec=pltpu.PrefetchScalarGridSpec(num_scalar_prefetch=1, grid=(tiles,), in_specs=in_specs,
                                               out_specs=out_specs),
        out_shape=[jax.ShapeDtypeStruct(s, F32) for s in shapes for _ in range(4)],
        compiler_params=_params(("arbitrary",)),
    )(qidx, *[a for (w, m, v), c, rc in zip(wmv, cs, rcv) for a in (w, m, v, c, rc)])
    return [tuple(out[4 * k:4 * k + 4]) for k in range(n)]


SMALL_PARAMS = ("norm_ffn1", "norm_mix", "norm_ffn2", "norm_final", "q_norm", "kv_norm", "sinks", "rel_bias", "b_mod")


def _adamw_small(gvec, wmv):
    shapes = [wmv[3 * i].shape for i in range(len(SMALL_PARAMS))]

    def body(*refs):
        g_all = refs[0]
        ins = refs[1:1 + 3 * len(SMALL_PARAMS)]
        outs = refs[1 + 3 * len(SMALL_PARAMS):]
        off = N_MODVEC
        for i, name in enumerate(SMALL_PARAMS):
            g_ref, d_ref, nm_ref, nv_ref = outs[4 * i:4 * i + 4]
            w_ref, m_ref, v_ref = ins[3 * i:3 * i + 3]
            start = 0 if name == "b_mod" else off
            rows, width = shapes[i]
            g = jnp.concatenate([g_all[:, start + width * r:start + width * (r + 1)] for r in range(rows)], axis=0)
            g_ref[...] = g
            d_ref[...], nm_ref[...], nv_ref[...] = _adam_math(w_ref[...], g, m_ref[...], v_ref[...])
            if name != "b_mod":
                off += dict(SMALL_LAYOUT)[name]

    vm = pl.BlockSpec(memory_space=pltpu.VMEM)
    n_out = 4 * len(SMALL_PARAMS)
    out = pl.pallas_call(
        body, name="adamw_small", in_specs=[vm] * (1 + len(wmv)), out_specs=[vm] * n_out,
        out_shape=[jax.ShapeDtypeStruct(shapes[i // 4], F32) for i in range(n_out)],
        compiler_params=_params(),
    )(gvec, *wmv)
    return {name: out[4 * i:4 * i + 4] for i, name in enumerate(SMALL_PARAMS)}


TRANSPOSED = ("g1T", "u1T", "g3T", "u3T", "w_inT", "w_uqT")


def kernel(x, c, w_mod, b_mod, norm_ffn1, ffn1_gate, ffn1_up, ffn1_down, norm_mix, w_in, q_norm, kv_norm, w_uq, w_ukv, sinks, w_o, norm_ffn2, ffn2_gate, ffn2_up, ffn2_down, rel_bias, norm_final, loss_target, m_w_mod, m_b_mod, m_norm_ffn1, m_ffn1_gate, m_ffn1_up, m_ffn1_down, m_norm_mix, m_w_in, m_q_norm, m_kv_norm, m_w_uq, m_w_ukv, m_sinks, m_w_o, m_norm_ffn2, m_ffn2_gate, m_ffn2_up, m_ffn2_down, m_rel_bias, m_norm_final, v_w_mod, v_b_mod, v_norm_ffn1, v_ffn1_gate, v_ffn1_up, v_ffn1_down, v_norm_mix, v_w_in, v_q_norm, v_kv_norm, v_w_uq, v_w_ukv, v_sinks, v_w_o, v_norm_ffn2, v_ffn2_gate, v_ffn2_up, v_ffn2_down, v_rel_bias, v_norm_final):
    mx, my, mc = _coords()
    cidx = jnp.reshape(mc, (1,)).astype(jnp.int32)
    qidx = jnp.reshape(2 * mx + my, (1,)).astype(jnp.int32)
    WM = w_mod.shape[2]

    shards = {"g1T": ffn1_gate[0].T.astype(BF16), "u1T": ffn1_up[0].T.astype(BF16), "d1": ffn1_down[0].astype(BF16),
              "g3T": ffn2_gate[0].T.astype(BF16), "u3T": ffn2_up[0].T.astype(BF16), "d3": ffn2_down[0].astype(BF16),
              "w_inT": w_in[0].T, "w_uqT": w_uq[0].T.astype(BF16), "w_ukv": w_ukv[0].astype(BF16),
              "w_o": w_o[0].astype(BF16)}
    me = 4 * mx + 2 * my + mc
    groups = {"ffn1": ("g1T", "u1T", "d1"), "mixer": ("w_inT", "w_uqT", "w_ukv", "w_o"), "ffn2": ("g3T", "u3T", "d3")}
    arriving = {}

    c_tile = jnp.pad(c, ((0, 7), (0, 0)))
    b_mod3 = jnp.pad(b_mod.reshape(N_DEV, 1, WM), ((0, 0), (0, 7), (0, 0)))
    *gathered_ffn1, gathered_token, mod3, ca = _wgather([shards[k] for k in groups["ffn1"]], c_tile, w_mod[0], b_mod3)
    mod9 = mod3[:, 0, :].reshape(N_MOD, D)

    def as_weights(group, gathered):
        return {k: g if k == "w_ukv" else g.reshape(N_DEV * g.shape[1], g.shape[2])
                for k, g in zip(groups[group], gathered)}

    later = groups["mixer"] + groups["ffn2"]
    place = {"mixer": 0, "ffn2": len(groups["mixer"])}

    def start_gather(token):
        lands = []
        for k in later:
            sh = shards[k] + token[0, 0].astype(shards[k].dtype)
            lands.append(lax.dynamic_update_slice(lax.empty((N_DEV,) + sh.shape, sh.dtype), sh[None], (me, 0, 0)))
        batches = [range(k0, k0 + len(groups[group])) for group, k0 in place.items()]
        send, recv, lands, started = _gather_start(lands, name="gather_start", batches=batches)
        for group, k0 in place.items():
            arriving[group] = (send, recv, lands[k0:k0 + len(groups[group])])
        return started

    def fetch(group, after, vecs):
        if group == "ffn1":
            return as_weights("ffn1", gathered_ffn1), vecs + start_gather(gathered_token)[0:1, 0:1]

        def pass_on(group, after):
            send, recv, lands = arriving[group]
            lands, token = _gather_pass(send, recv, lands, after, name="gather_landed_" + group, stage="landed",
                                        k0=place[group])
            lands, token = _gather_pass(send, recv, lands, [token], name="gather_onward_" + group, stage="onward",
                                        k0=place[group])
            arriving[group] = (send, recv, lands)
            return token

        if group == "ffn2_on_its_way":
            return None, vecs + pass_on("ffn2", after)[0:1, 0:1]
        if group == "mixer":
            after = [pass_on("mixer", after)]
        send, recv, lands = arriving[group]
        return as_weights(group, _gather_end(send, recv, lands, after, name="gather_end_" + group,
                                             k0=place[group])), vecs

    norms ={"ffn1": norm_ffn1, "mix": norm_mix, "ffn2": norm_ffn2, "final": norm_final.reshape(1, D)}
    in_flight = {}

    def on_grads(group, g, after, vecs):
        if group != "ffn1":
            if g is None:
                return vecs
            names = list(g)
            by_dest = [g[k] if k == "w_ukv" else g[k].reshape((N_DEV, g[k].shape[0] // N_DEV) + g[k].shape[1:])
                       for k in names]
            lands = [lax.empty((N_DEV - 1,) + a.shape[1:], a.dtype) for a in by_dest]
            send, recv, by_dest, lands, token = _split_start(_direct_copies, by_dest, lands, 7 * len(names), after,
                                                             name="rs_start_" + group)
            in_flight[group] = (names, send, recv, by_dest, lands, token)
            return vecs + token[0:1, 0:1]
        names = list(g)
        by_dest = [g[k].reshape((N_DEV, g[k].shape[0] // N_DEV) + g[k].shape[1:]) for k in names]
        lands = [lax.empty((4,) + a.shape[1:], a.dtype) for a in by_dest]
        send, recv, by_dest, lands, token = _split_start(_d2d_copies, by_dest, lands, 4 * len(names), after,
                                                         name="rs_d2d_start_" + group)
        finish("mixer", [token])
        by_dest, from_sib = _split_wait(_d2d_copies, send, recv, by_dest, lands, [done[-1]], name="rs_d2d_wait_" + group)
        sums = _chipsum(by_dest, from_sib, cidx, name="chipsum_" + group)
        halves = lambda: [lax.empty((2, s.shape[1] // 2) + s.shape[2:], s.dtype) for s in sums]
        send, recv, sums, lands, token = _split_start(_rs_step1_copies, sums, halves() + halves(), 4 * len(names), [],
                                                      name="rs_ici_start_" + group)
        in_flight[group] = (names, send, recv, sums, lands, token)
        return vecs + token[0:1, 0:1]

    owners = {"g1T": ("ffn1_gate", ffn1_gate, m_ffn1_gate, v_ffn1_gate), "u1T": ("ffn1_up", ffn1_up, m_ffn1_up, v_ffn1_up),
              "d1": ("ffn1_down", ffn1_down, m_ffn1_down, v_ffn1_down),
              "g3T": ("ffn2_gate", ffn2_gate, m_ffn2_gate, v_ffn2_gate), "u3T": ("ffn2_up", ffn2_up, m_ffn2_up, v_ffn2_up),
              "d3": ("ffn2_down", ffn2_down, m_ffn2_down, v_ffn2_down),
              "w_inT": ("w_in", w_in, m_w_in, v_w_in), "w_uqT": ("w_uq", w_uq, m_w_uq, v_w_uq),
              "w_ukv": ("w_ukv", w_ukv, m_w_ukv, v_w_ukv), "w_o": ("w_o", w_o, m_w_o, v_w_o)}
    res, done = {}, []

    there = lambda k, a: a[0].T if k in TRANSPOSED else a[0]
    back = lambda k, a: a.T[None] if k in TRANSPOSED else a[None]

    def record(names, outs):
        for k, out in zip(names, outs):
            done.append(out[3])
            res[owners[k][0]] = tuple(back(k, a) for a in out)

    def finish(group, after):
        names, send, recv, sums, lands, _ = in_flight[group]
        wmv = [tuple(there(k, a) for a in owners[k][1:]) for k in names]
        own = jnp.reshape(me, (1,)).astype(jnp.int32)
        sums, lands = _split_wait(_direct_copies, send, recv, sums, lands, after, name="rs_wait_" + group)
        record(names, _adamw_rs(wmv, sums, lands, own, name="adamw_" + group))

    _, grad_x, _, vec = _local_step(
        x[0], loss_target[0], mod9, norms, sinks, rel_bias, q_norm, kv_norm, fetch, on_grads=on_grads)

    names, send, recv, sums, lands, step1_started = in_flight["ffn1"]
    vec = vec.reshape(1, 1, N_VEC)
    allvec = lax.dynamic_update_slice(lax.empty((N_DEV, 1, N_VEC), F32), vec, (me, 0, 0))
    vsend, vrecv, _, (allvec,), vec_started = _split_start(_vec_copies, [], [allvec], N_DEV - 1, [step1_started],
                                                           name="vec_start")
    finish("ffn2", [vec_started])
    n = len(names)
    sums, lands = _split_wait(_rs_step1_copies, send, recv, sums, lands, [done[-1]], name="rs_ici_wait_ffn1")
    direct, relay = lands[:n], lands[n:]
    qxy = jnp.stack([2 * (1 - mx) + my, 2 * mx + (1 - my)]).astype(jnp.int32)
    relayed = _relay_sum(sums, relay, qxy, name="relay_sum_ffn1")
    second = [lax.empty(a.shape, a.dtype) for a in relayed]
    send, recv, relayed, second, step2_started = _split_start(_rs_step2_copies, relayed, second, 2 * n, [],
                                                              name="rs_ici_start2_ffn1")

    _, (allvec,) = _split_wait(_vec_copies, vsend, vrecv, [], [allvec], [step2_started], name="vec_wait")
    g_wmod, gvec = _mod_bwd(allvec, ca, jnp.reshape(me, (1,)).astype(jnp.int32))
    loss = gvec[0, N_MODVEC + LOSS_SLOT]
    small_in = {"norm_ffn1": (norm_ffn1, m_norm_ffn1, v_norm_ffn1), "norm_mix": (norm_mix, m_norm_mix, v_norm_mix),
                "norm_ffn2": (norm_ffn2, m_norm_ffn2, v_norm_ffn2), "norm_final": (norm_final, m_norm_final, v_norm_final),
                "q_norm": (q_norm, m_q_norm, v_q_norm), "kv_norm": (kv_norm, m_kv_norm, v_kv_norm),
                "sinks": (sinks, m_sinks, v_sinks), "rel_bias": (rel_bias, m_rel_bias, v_rel_bias),
                "b_mod": (b_mod, m_b_mod, v_b_mod)}
    as_row = lambda k, a: a.T if k == "rel_bias" else a.reshape(1, -1)
    from_row = lambda k, a: a.T if k == "rel_bias" else a.reshape(small_in[k][0].shape)
    small_out = _adamw_small(gvec, [as_row(k, a) for k in SMALL_PARAMS for a in small_in[k]])
    for k in SMALL_PARAMS:
        res[k] = tuple(from_row(k, a) for a in small_out[k])

    out = _adamw(w_mod[0], g_wmod, m_w_mod[0], v_w_mod[0], name="adamw_w_mod")
    res["w_mod"] = tuple(a[None] for a in (g_wmod,) + tuple(out))

    wmv = [tuple(there(k, a) for a in owners[k][1:]) for k in names]
    after = done + [out[2]] + [a for k in SMALL_PARAMS for a in res[k]]
    outs = []
    for i, k in enumerate(names):
        _, (sec,) = _split_wait(functools.partial(_rs_step2_copies, k0=i), send, recv, [relayed[i]], [second[i]],
                                after, name="rs_ici_wait2_" + k)
        outs.append(_adamw_rs2([wmv[i]], [sums[i]], [direct[i]], [sec], qidx, name="adamw_" + owners[k][0])[0])
        after = [outs[-1][3]]
    record(names, outs)

    order = ("w_mod", "b_mod", "norm_ffn1", "ffn1_gate", "ffn1_up", "ffn1_down", "norm_mix", "w_in", "q_norm",
             "kv_norm", "w_uq", "w_ukv", "sinks", "w_o", "norm_ffn2", "ffn2_gate", "ffn2_up", "ffn2_down",
             "rel_bias", "norm_final")
    return (loss, grad_x[None]) + tuple(res[nm][kind] for kind in range(4) for nm in order)
```

```python
import functools
import math

import numpy as np
import jax
import jax.numpy as jnp
from jax import lax
from jax.experimental import pallas as pl
from jax.experimental.pallas import tpu as pltpu

F32 = jnp.float32
BF16 = jnp.bfloat16
MESH = pl.DeviceIdType.MESH

N_DEV = 8
D = 1024
D_FF = 2816
EPS = 1e-6
N_MOD = 9
SWA_HEADS = 8
SWA_DH = 64
WINDOW = 128
MLA_HEADS = 4
MLA_NOPE = 128
MLA_ROPE = 64
MLA_V = 128
MLA_QR = 256
MLA_KVR = 128
ROPE_THETA = 10000.0
NUM_BUCKETS = 32
D_IN = 1216
D_IN_PAD = 1280
SWA_SCALE = SWA_DH ** -0.5
MLA_SCALE = (MLA_NOPE + MLA_ROPE) ** -0.5

ADAM_LR = 0.001
ADAM_B1 = 0.9
ADAM_B2 = 0.999
ADAM_EPS = 1e-08
ADAM_WD = 0.01
ADAM_STEP = 10

V7X_VMEM_LIMIT = 56 * 1024 * 1024
ROW_TILE = 512

NT_DIMS = (((1,), (1,)), ((), ()))
TN_DIMS = (((0,), (0,)), ((), ()))


def _dot(a, b):
    return jnp.dot(a, b, preferred_element_type=F32)


def _dot_nt(a, b):
    return lax.dot_general(a, b, NT_DIMS, preferred_element_type=F32)


def _dot_tn(a, b):
    return lax.dot_general(a, b, TN_DIMS, preferred_element_type=F32)


def _params(sem=None):
    return pltpu.CompilerParams(dimension_semantics=sem, vmem_limit_bytes=V7X_VMEM_LIMIT)


def _rstd(x):
    return lax.rsqrt(jnp.mean(x * x, axis=-1, keepdims=True) + EPS)


def _rms_bwd(dy, xhat, r):
    return r * (dy - xhat * jnp.mean(dy * xhat, axis=-1, keepdims=True))


def _sigmoid(a):
    return 1.0 / (1.0 + jnp.exp(-a))


def _ffn_fwd(x, vecs, wgT, wuT, wd, *, name, tm=256, tf=D_FF):
    S, F = x.shape[0], wd.shape[0]
    tm = min(tm, S)
    ni, nj = S // tm, F // tf

    def body(x_ref, vec_ref, wg_ref, wu_ref, wd_ref, xo_ref, h_ref, a_ref, b_ref, f_ref, acc_ref):
        j = pl.program_id(1)

        @pl.when(j == 0)
        def _():
            xv = x_ref[...]
            hn = xv * _rstd(xv) * vec_ref[0:1, :]
            h_ref[...] = (hn * (1.0 + vec_ref[2:3, :]) + vec_ref[1:2, :]).astype(BF16)

        h = h_ref[...]
        a = _dot_nt(h, wg_ref[...])
        b = _dot_nt(h, wu_ref[...])
        a_ref[...] = a.astype(BF16)
        b_ref[...] = b.astype(BF16)
        part = _dot((a * _sigmoid(a) * b).astype(BF16), wd_ref[...])

        def finish(f):
            f_ref[...] = f
            xo_ref[...] = x_ref[...] + (0.5 * vec_ref[3:4, :]) * f

        if nj == 1:
            finish(part)
        else:
            @pl.when(j == 0)
            def _():
                acc_ref[...] = part

            @pl.when((j > 0) & (j < nj - 1))
            def _():
                acc_ref[...] += part

            @pl.when(j == nj - 1)
            def _():
                finish(acc_ref[...] + part)

    row = pl.BlockSpec((tm, D), lambda i, j: (i, 0))
    wspec = pl.BlockSpec((tf, D), lambda i, j: (j, 0), pipeline_mode=pl.Buffered(1) if nj == 1 else None)
    act = pl.BlockSpec((tm, tf), lambda i, j: (i, j))
    return pl.pallas_call(
        body, name=name, grid=(ni, nj),
        in_specs=[row, pl.BlockSpec((8, D), lambda i, j: (0, 0)), wspec, wspec, wspec],
        out_specs=[row, row, act, act, row],
        out_shape=[jax.ShapeDtypeStruct((S, D), F32), jax.ShapeDtypeStruct((S, D), BF16),
                   jax.ShapeDtypeStruct((S, F), BF16), jax.ShapeDtypeStruct((S, F), BF16),
                   jax.ShapeDtypeStruct((S, D), F32)],
        scratch_shapes=[pltpu.VMEM((tm, D) if nj > 1 else (8, 128), F32)],
        compiler_params=_params(("parallel", "arbitrary")),
    )(x, vecs, wgT, wuT, wd)


def _ffn_bwd_main(h, df, a, b, wgT, wuT, wd, *, name, after=(), tm=2048, tf=256):
    S = h.shape[0]
    tm = min(tm, S)
    ni, nj = S // tm, D_FF // tf

    def body(h_hbm, df_hbm, a_ref, b_ref, wg_ref, wu_ref, wd_ref, *rest):
        gg_ref, gu_ref, gd_ref, dh_hbm, h_v, df_v, dh_v, gg_acc, gu_acc, gd_acc, sem = rest[len(after):]
        j = pl.program_id(0)
        i = pl.program_id(1)

        @pl.when((j == 0) & (i == 0))
        def _():
            c1 = pltpu.make_async_copy(h_hbm, h_v, sem.at[0])
            c2 = pltpu.make_async_copy(df_hbm, df_v, sem.at[1])
            c1.start()
            c2.start()
            c1.wait()
            c2.wait()

        @pl.when(i == 0)
        def _():
            gg_acc[...] = jnp.zeros_like(gg_acc)
            gu_acc[...] = jnp.zeros_like(gu_acc)
            gd_acc[...] = jnp.zeros_like(gd_acc)

        rows = pl.ds(pl.multiple_of(i * tm, tm), tm)
        hi = h_v[rows, :]
        dfi = df_v[rows, :]
        av = a_ref[...].astype(F32)
        bv = b_ref[...].astype(F32)
        sg = _sigmoid(av)
        sa = av * sg
        hsw = (sa * bv).astype(BF16)
        dhsw = _dot_nt(dfi, wd_ref[...])
        da = (dhsw * bv * (sg * (1.0 + av * (1.0 - sg)))).astype(BF16)
        db = (dhsw * sa).astype(BF16)
        gd_acc[...] += _dot_tn(hsw, dfi)
        gg_acc[...] += _dot_tn(da, hi)
        gu_acc[...] += _dot_tn(db, hi)
        dh = _dot(da, wg_ref[...]) + _dot(db, wu_ref[...])

        @pl.when(j == 0)
        def _():
            dh_v[rows, :] = dh

        @pl.when(j > 0)
        def _():
            dh_v[rows, :] += dh

        @pl.when(i == ni - 1)
        def _():
            gg_ref[...] = gg_acc[...].astype(BF16)
            gu_ref[...] = gu_acc[...].astype(BF16)
            gd_ref[...] = gd_acc[...].astype(BF16)

        @pl.when((j == nj - 1) & (i == ni - 1))
        def _():
            c3 = pltpu.make_async_copy(dh_v, dh_hbm, sem.at[2])
            c3.start()
            c3.wait()

    anyspec = pl.BlockSpec(memory_space=pl.ANY)
    wspec = pl.BlockSpec((tf, D), lambda j, i: (j, 0))
    act = pl.BlockSpec((tm, tf), lambda j, i: (i, j))
    return pl.pallas_call(
        body, name=name, grid=(nj, ni),
        in_specs=[anyspec, anyspec, act, act, wspec, wspec, wspec] + [anyspec] * len(after),
        out_specs=[wspec, wspec, wspec, anyspec],
        out_shape=[jax.ShapeDtypeStruct((D_FF, D), BF16)] * 3 + [jax.ShapeDtypeStruct((S, D), F32)],
        scratch_shapes=[pltpu.VMEM((S, D), BF16), pltpu.VMEM((S, D), BF16), pltpu.VMEM((S, D), F32),
                        pltpu.VMEM((tf, D), F32), pltpu.VMEM((tf, D), F32), pltpu.VMEM((tf, D), F32),
                        pltpu.SemaphoreType.DMA((3,))],
        compiler_params=_params(("arbitrary", "arbitrary")),
    )(h, df, a, b, wgT, wuT, wd, *after)


def _ffn_out_bwd(dx, f, gate, df_ref, part_ref):
    df_ref[...] = ((0.5 * gate) * dx).astype(BF16)
    part_ref[3:4, :] += 0.5 * jnp.sum(dx * f, axis=0, keepdims=True)


def _norm_bwd(dh, x, dxo, vecs, *, name, below=None, tm=ROW_TILE):
    S = x.shape[0]
    tm = min(tm, S)

    def body(dh_ref, x_ref, dxo_ref, vec_ref, *rest):
        dx_ref, part_ref = rest[-2 if below is None else -3], rest[-1 if below is None else -2]

        @pl.when(pl.program_id(0) == 0)
        def _():
            part_ref[...] = jnp.zeros_like(part_ref)

        dh = dh_ref[...]
        xv = x_ref[...]
        r = _rstd(xv)
        xhat = xv * r
        w = vec_ref[0:1, :]
        xn = xhat * w
        dxn = dh * (1.0 + vec_ref[2:3, :])
        part_ref[0:1, :] += jnp.sum(dxn * xhat, axis=0, keepdims=True)
        part_ref[1:2, :] += jnp.sum(dh, axis=0, keepdims=True)
        part_ref[2:3, :] += jnp.sum(dh * xn, axis=0, keepdims=True)
        dx = dxo_ref[...] + _rms_bwd(dxn * w, xhat, r)
        dx_ref[...] = dx
        if below is not None:
            _ffn_out_bwd(dx, rest[0][...], rest[1][3:4, :], rest[-1], part_ref)

    row = pl.BlockSpec((tm, D), lambda i: (i, 0))
    vec = pl.BlockSpec((8, D), lambda i: (0, 0))
    extra = [] if below is None else [row, vec]
    return pl.pallas_call(
        body, name=name, grid=(S // tm,), in_specs=[row, row, row, vec] + extra,
        out_specs=[row, vec] + ([] if below is None else [row]),
        out_shape=[jax.ShapeDtypeStruct((S, D), F32), jax.ShapeDtypeStruct((8, D), F32)]
        + ([] if below is None else [jax.ShapeDtypeStruct((S, D), BF16)]),
        compiler_params=_params(("arbitrary",)),
    )(dh, x, dxo, vecs, *([] if below is None else below))


def _head(x, tgt, nf, f, vecs, *, tm=ROW_TILE):
    S = x.shape[0]
    tm = min(tm, S)

    def body(x_ref, t_ref, nf_ref, f_ref, vec_ref, dx_ref, part_ref, df_ref):
        @pl.when(pl.program_id(0) == 0)
        def _():
            part_ref[...] = jnp.zeros_like(part_ref)

        xv = x_ref[...]
        r = _rstd(xv)
        xhat = xv * r
        w = nf_ref[...]
        e = xhat * w - t_ref[...]
        dy = e * (1.0 / D)
        part_ref[0:1, :] += jnp.sum(dy * xhat, axis=0, keepdims=True)
        part_ref[1:2, :] += jnp.sum(e * e) * (0.5 / D)
        dx = _rms_bwd(dy * w, xhat, r)
        dx_ref[...] = dx
        _ffn_out_bwd(dx, f_ref[...], vec_ref[3:4, :], df_ref, part_ref)

    row = pl.BlockSpec((tm, D), lambda i: (i, 0))
    vec = pl.BlockSpec((8, D), lambda i: (0, 0))
    return pl.pallas_call(
        body, name="head", grid=(S // tm,),
        in_specs=[row, row, pl.BlockSpec((1, D), lambda i: (0, 0)), row, vec],
        out_specs=[row, vec, row],
        out_shape=[jax.ShapeDtypeStruct((S, D), F32), jax.ShapeDtypeStruct((8, D), F32),
                   jax.ShapeDtypeStruct((S, D), BF16)],
        compiler_params=_params(("arbitrary",)),
    )(x, tgt, nf, f, vecs)


def _mix_in_fwd(x, vecs, w_inT, *, tm=ROW_TILE):
    S = x.shape[0]
    tm = min(tm, S)

    def body(x_ref, vec_ref, w_ref, h_ref, p_ref, wb_ref):
        @pl.when(pl.program_id(0) == 0)
        def _():
            wb_ref[0:D_IN, :] = w_ref[...].astype(BF16)
            wb_ref[D_IN:D_IN_PAD, :] = jnp.zeros((D_IN_PAD - D_IN, D), BF16)

        xv = x_ref[...]
        hn = xv * _rstd(xv) * vec_ref[0:1, :]
        h = (hn * (1.0 + vec_ref[2:3, :]) + vec_ref[1:2, :]).astype(BF16)
        h_ref[...] = h
        p_ref[...] = _dot_nt(h, wb_ref[...])

    row = pl.BlockSpec((tm, D), lambda i: (i, 0))
    return pl.pallas_call(
        body, name="mix_in_fwd", grid=(S // tm,),
        in_specs=[row, pl.BlockSpec((8, D), lambda i: (0, 0)),
                  pl.BlockSpec((D_IN, D), lambda i: (0, 0), pipeline_mode=pl.Buffered(1))],
        out_specs=[row, pl.BlockSpec((tm, D_IN_PAD), lambda i: (i, 0)), pl.BlockSpec((D_IN_PAD, D), lambda i: (0, 0))],
        out_shape=[jax.ShapeDtypeStruct((S, D), BF16), jax.ShapeDtypeStruct((S, D_IN_PAD), F32),
                   jax.ShapeDtypeStruct((D_IN_PAD, D), BF16)],
        compiler_params=_params(("arbitrary",)),
    )(x, vecs, w_inT)


def _bucket_table():
    qi = np.arange(WINDOW)[:, None]
    kj = np.arange(2 * WINDOW)[None, :]
    dist = qi + WINDOW - kj
    max_exact = NUM_BUCKETS // 2
    n = np.maximum(dist, 0)
    nf = np.maximum(n, 1).astype(np.float32)
    large = max_exact + (np.log(nf / np.float32(max_exact)) / np.float32(math.log(WINDOW / max_exact))
                         * np.float32(NUM_BUCKETS - max_exact)).astype(np.int32)
    large = np.minimum(large, NUM_BUCKETS - 1)
    return np.where(n < max_exact, n, large).astype(np.int32)


SWA_GROUP = 4
GROUP_ROWS = SWA_GROUP * WINDOW


SWA_SUB = 2


def _swa_valid(has_prev):
    row = lax.broadcasted_iota(jnp.int32, (GROUP_ROWS, 2 * WINDOW), 0) % WINDOW
    col = lax.broadcasted_iota(jnp.int32, (GROUP_ROWS, 2 * WINDOW), 1)
    dist = row + WINDOW - col
    return (dist >= 0) & (dist < WINDOW) & ((col >= WINDOW) | has_prev)


def _swa_keys(prev_ref, cur_ref, u):
    cur = cur_ref[...]
    before = prev_ref[...] if u == 0 else cur[WINDOW * (u - 1):WINDOW * u]
    return jnp.concatenate([before, cur[WINDOW * u:WINDOW * (u + 1)]], axis=0).astype(BF16)


def _stack_heads(x, g):
    return jnp.concatenate([x[:, 64 * h:64 * h + 64] for h in range(SWA_GROUP * g, SWA_GROUP * (g + 1))], axis=0)


def _unstack_heads(x4):
    return jnp.concatenate([x4[WINDOW * a:WINDOW * (a + 1)] for a in range(SWA_GROUP)], axis=1)


def _group_sinks(sink_ref, g):
    head = lax.broadcasted_iota(jnp.int32, (GROUP_ROWS, 1), 0) // WINDOW
    out = jnp.full((GROUP_ROWS, 1), sink_ref[0, SWA_GROUP * g], F32)
    for a in range(1, SWA_GROUP):
        out = jnp.where(head == a, sink_ref[0, SWA_GROUP * g + a], out)
    return out


def _swa_probs(qh, kk, bias_h, sink, valid):
    s = _dot_nt(qh, kk) * SWA_SCALE + bias_h
    s = jnp.where(valid, s, -jnp.inf)
    m = jnp.maximum(jnp.max(s, axis=-1, keepdims=True), sink)
    p = jnp.exp(s - m)
    ps = jnp.exp(sink - m)
    inv = 1.0 / (jnp.sum(p, axis=-1, keepdims=True) + ps)
    return p * inv, ps * inv


SWA_ROWS = SWA_SUB * WINDOW


def _swa_specs():
    prev = lambda n: jnp.maximum(SWA_SUB * n - 1, 0)
    return [pl.BlockSpec((SWA_ROWS, 512), lambda n: (n, 0)),
            pl.BlockSpec((SWA_ROWS, 128), lambda n: (n, 4)),
            pl.BlockSpec((WINDOW, 128), lambda n: (prev(n), 4)),
            pl.BlockSpec((SWA_ROWS, 128), lambda n: (n, 5)),
            pl.BlockSpec((WINDOW, 128), lambda n: (prev(n), 5)),
            pl.BlockSpec((SWA_HEADS, WINDOW, 2 * WINDOW), lambda n: (0, 0, 0)),
            pl.BlockSpec(memory_space=pltpu.SMEM)]


def _swa_fwd(proj, rel_bias, bucket, sinks):
    S = proj.shape[0]

    def body(q_ref, kc_ref, kp_ref, vc_ref, vp_ref, rb_ref, sink_ref, bk_ref, o_ref, bias_ref):
        n = pl.program_id(0)

        @pl.when(n == 0)
        def _():
            bk = bk_ref[...]
            for h in range(SWA_HEADS):
                acc = jnp.zeros((WINDOW, 2 * WINDOW), F32)
                for b in range(NUM_BUCKETS):
                    acc = jnp.where(bk == b, rb_ref[b, h], acc)
                bias_ref[h] = acc

        for u in range(SWA_SUB):
            rows = slice(WINDOW * u, WINDOW * (u + 1))
            valid = _swa_valid(n > 0 if u == 0 else True)
            q = q_ref[rows, :].astype(BF16)
            kfull = _swa_keys(kp_ref, kc_ref, u)
            vfull = _swa_keys(vp_ref, vc_ref, u)
            for g in range(SWA_HEADS // SWA_GROUP):
                kk = kfull[:, 64 * g:64 * g + 64]
                vv = vfull[:, 64 * g:64 * g + 64]
                bias4 = bias_ref[SWA_GROUP * g:SWA_GROUP * (g + 1)].reshape(GROUP_ROWS, 2 * WINDOW)
                pk, _ = _swa_probs(_stack_heads(q, g), kk, bias4, _group_sinks(sink_ref, g), valid)
                o_ref[rows, 256 * g:256 * (g + 1)] = _unstack_heads(_dot(pk.astype(BF16), vv))

    specs = _swa_specs()
    whole = pl.BlockSpec((SWA_HEADS, WINDOW, 2 * WINDOW), lambda n: (0, 0, 0))
    return pl.pallas_call(
        body, name="swa_fwd", grid=(S // SWA_ROWS,),
        in_specs=specs[:5] + [pl.BlockSpec(memory_space=pltpu.SMEM), specs[6],
                              pl.BlockSpec((WINDOW, 2 * WINDOW), lambda n: (0, 0))],
        out_specs=[pl.BlockSpec((SWA_ROWS, 512), lambda n: (n, 0)), whole],
        out_shape=[jax.ShapeDtypeStruct((S, 512), F32), jax.ShapeDtypeStruct((SWA_HEADS, WINDOW, 2 * WINDOW), F32)],
        compiler_params=_params(("arbitrary",)),
    )(proj, proj, proj, proj, proj, rel_bias, sinks, bucket)


def _swa_bwd(proj, bias, sinks, o, do, bucket):
    S = proj.shape[0]
    nb = S // SWA_ROWS

    def body(q_ref, kc_ref, kp_ref, vc_ref, vp_ref, bias_ref, sink_ref, o_ref, do_ref, bk_ref,
             dq_ref, dk_ref, dv_ref, drb_ref, dsk_ref, dbias_acc):
        n = pl.program_id(0)

        @pl.when(n == 0)
        def _():
            dk_ref[...] = jnp.zeros_like(dk_ref)
            dv_ref[...] = jnp.zeros_like(dv_ref)
            dsk_ref[...] = jnp.zeros_like(dsk_ref)
            dbias_acc[...] = jnp.zeros_like(dbias_acc)
            drb_ref[...] = jnp.zeros_like(drb_ref)

        for u in range(SWA_SUB):
            rows = slice(WINDOW * u, WINDOW * (u + 1))
            blk = SWA_SUB * n + u
            valid = _swa_valid(n > 0 if u == 0 else True)
            q = q_ref[rows, :].astype(BF16)
            dov = do_ref[rows, :]
            ov = o_ref[rows, :]
            kfull = _swa_keys(kp_ref, kc_ref, u)
            vfull = _swa_keys(vp_ref, vc_ref, u)
            prow = pl.ds(pl.multiple_of(jnp.maximum(blk - 1, 0) * WINDOW, WINDOW), WINDOW)
            crow = pl.ds(pl.multiple_of(blk * WINDOW, WINDOW), WINDOW)
            for g in range(SWA_HEADS // SWA_GROUP):
                heads = slice(SWA_GROUP * g, SWA_GROUP * (g + 1))
                kk = kfull[:, 64 * g:64 * g + 64]
                vv = vfull[:, 64 * g:64 * g + 64]
                q4 = _stack_heads(q, g)
                pk, psink = _swa_probs(q4, kk, bias_ref[heads].reshape(GROUP_ROWS, 2 * WINDOW),
                                       _group_sinks(sink_ref, g), valid)
                pkb = pk.astype(BF16)
                do4 = _stack_heads(dov, g)
                dob = do4.astype(BF16)
                dp = _dot_nt(dob, vv)
                delta = jnp.sum(do4 * _stack_heads(ov, g), axis=-1, keepdims=True)
                ds = pk * (dp - delta)
                dsink = -psink * delta
                for a in range(SWA_GROUP):
                    h = SWA_GROUP * g + a
                    part = jnp.sum(dsink[WINDOW * a:WINDOW * (a + 1)], keepdims=True)
                    dsk_ref[h:h + 1, :] += jnp.broadcast_to(part, (1, 128))
                dbias_acc[heads] += ds.reshape(SWA_GROUP, WINDOW, 2 * WINDOW)
                dsb = (ds * SWA_SCALE).astype(BF16)
                dq_ref[rows, 256 * g:256 * (g + 1)] = _unstack_heads(_dot(dsb, kk))
                dkk = _dot_tn(dsb, q4)
                dvv = _dot_tn(pkb, dob)
                dk_ref[prow, 64 * g:64 * g + 64] += dkk[:WINDOW]
                dk_ref[crow, 64 * g:64 * g + 64] += dkk[WINDOW:]
                dv_ref[prow, 64 * g:64 * g + 64] += dvv[:WINDOW]
                dv_ref[crow, 64 * g:64 * g + 64] += dvv[WINDOW:]

        @pl.when(n == nb - 1)
        def _():
            bk = bk_ref[...]
            for h in range(SWA_HEADS):
                dbh = dbias_acc[h]
                for b in range(NUM_BUCKETS):
                    val = jnp.sum(jnp.where(bk == b, dbh, 0.0), keepdims=True)
                    row = h * NUM_BUCKETS + b
                    drb_ref[row:row + 1, :] = jnp.broadcast_to(val, (1, 128))

    full = lambda shape: pl.BlockSpec(shape, lambda n: tuple(0 for _ in shape))
    return pl.pallas_call(
        body, name="swa_bwd", grid=(nb,),
        in_specs=_swa_specs() + [pl.BlockSpec((SWA_ROWS, 512), lambda n: (n, 0)),
                                 pl.BlockSpec((SWA_ROWS, 512), lambda n: (n, 0)), full((WINDOW, 2 * WINDOW))],
        out_specs=[pl.BlockSpec((SWA_ROWS, 512), lambda n: (n, 0)), full((S, 128)), full((S, 128)),
                   full((NUM_BUCKETS * 8, 128)), full((8, 128))],
        out_shape=[jax.ShapeDtypeStruct((S, 512), F32), jax.ShapeDtypeStruct((S, 128), F32),
                   jax.ShapeDtypeStruct((S, 128), F32), jax.ShapeDtypeStruct((NUM_BUCKETS * 8, 128), F32),
                   jax.ShapeDtypeStruct((8, 128), F32)],
        scratch_shapes=[pltpu.VMEM((SWA_HEADS, WINDOW, 2 * WINDOW), F32)],
        compiler_params=_params(("arbitrary",)),
    )(proj, proj, proj, proj, proj, bias, sinks, o, do, bucket)


def _rope_tables(S):
    inv = np.float32(ROPE_THETA) ** (-np.arange(0, MLA_ROPE, 2, dtype=np.float32) / np.float32(MLA_ROPE))
    ang = np.arange(S, dtype=np.float32)[:, None] * inv[None, :]
    cos, sin = np.cos(ang), np.sin(ang)
    return (jnp.asarray(np.tile(np.concatenate([cos, cos], axis=1), (1, 2))),
            jnp.asarray(np.tile(np.concatenate([-sin, sin], axis=1), (1, 2))))


def _rope_wide(ref):
    t = ref[...]
    return jnp.concatenate([t, t], axis=1)


def _swap_halves(x):
    w = x.shape[-1]
    lane = lax.broadcasted_iota(jnp.int32, x.shape, x.ndim - 1)
    return jnp.where((lane % 64) < 32, pltpu.roll(x, w - 32, x.ndim - 1), pltpu.roll(x, 32, x.ndim - 1))


def _mla_pre_fwd(proj, qn_w, kvn_w, wuqT, wukv, cos, sin, *, tm=ROW_TILE):
    S = proj.shape[0]
    tm = min(tm, S)

    def body(ql_ref, kl_ref, kr_ref, qw_ref, kw_ref, wuq_ref, wukv_ref, cos_ref, sin_ref,
             qc_ref, kc_ref, vv_ref):
        ql = ql_ref[...]
        qn = (ql * _rstd(ql) * qw_ref[...]).astype(BF16)
        q = _dot_nt(qn, wuq_ref[...])
        cs, sn = _rope_wide(cos_ref), _rope_wide(sin_ref)
        qr = q[:, 512:768]
        qr = qr * cs + _swap_halves(qr) * sn
        half = lax.broadcasted_iota(jnp.int32, (tm, 128), 1) // 64
        kl = kl_ref[...]
        kvn = (kl * _rstd(kl) * kw_ref[...]).astype(BF16)
        kr = kr_ref[...]
        kr = kr * cs[:, :128] + _swap_halves(kr) * sn[:, :128]
        kr2 = (kr + pltpu.roll(kr, 64, 1)).astype(BF16)
        kv = _dot(kvn, jnp.concatenate([wukv_ref[j] for j in range(2 * MLA_HEADS)], axis=1)).astype(BF16)
        for h in range(MLA_HEADS):
            qc_ref[h, :, 0:128] = q[:, 128 * h:128 * h + 128].astype(BF16)
            chunk = qr[:, 128 * (h // 2):128 * (h // 2) + 128]
            qc_ref[h, :, 128:256] = jnp.where(half == (h % 2), chunk, 0.0).astype(BF16)
            kc_ref[h, :, 0:128] = kv[:, 256 * h:256 * h + 128]
            kc_ref[h, :, 128:256] = kr2
            vv_ref[h] = kv[:, 256 * h + 128:256 * h + 256]

    const = lambda shape: pl.BlockSpec(shape, lambda i: tuple(0 for _ in shape))
    return pl.pallas_call(
        body, name="mla_pre_fwd", grid=(S // tm,),
        in_specs=[pl.BlockSpec((tm, 256), lambda i: (i, 3)), pl.BlockSpec((tm, 128), lambda i: (i, 8)),
                  pl.BlockSpec((tm, 128), lambda i: (i, 9)), const((1, 256)), const((1, 128)),
                  const((768, 256)), const((8, 128, 128)),
                  pl.BlockSpec((tm, 128), lambda i: (i, 0)), pl.BlockSpec((tm, 128), lambda i: (i, 0))],
        out_specs=[pl.BlockSpec((MLA_HEADS, tm, 256), lambda i: (0, i, 0)),
                   pl.BlockSpec((MLA_HEADS, tm, 256), lambda i: (0, i, 0)),
                   pl.BlockSpec((MLA_HEADS, tm, 128), lambda i: (0, i, 0))],
        out_shape=[jax.ShapeDtypeStruct((MLA_HEADS, S, 256), BF16), jax.ShapeDtypeStruct((MLA_HEADS, S, 256), BF16),
                   jax.ShapeDtypeStruct((MLA_HEADS, S, 128), BF16)],
        compiler_params=_params(("parallel",)),
    )(proj, proj, proj, qn_w, kvn_w, wuqT, wukv, cos, sin)


def _causal(i, j, t):
    row = i * t + lax.broadcasted_iota(jnp.int32, (t, t), 0)
    col = j * t + lax.broadcasted_iota(jnp.int32, (t, t), 1)
    return col <= row


def _mla_attn_fwd(qc, kc, vv, *, t=512):
    S = qc.shape[1]
    t = min(t, S)

    def body(q_ref, k_ref, v_ref, o_ref, l_ref):
        i = pl.program_id(0)
        diag = _causal(0, 0, t)

        def step(j, carry, masked):
            rows = pl.ds(pl.multiple_of(j * t, t), t)
            out = []
            for h in range(MLA_HEADS):
                m, l, acc = carry[h]
                s = _dot_nt(q_ref[h], k_ref[h, rows, :]) * MLA_SCALE
                if masked:
                    s = jnp.where(diag, s, -jnp.inf)
                m_new = jnp.maximum(m, jnp.max(s, axis=-1, keepdims=True))
                alpha = jnp.exp(m - m_new)
                p = jnp.exp(s - m_new)
                l = alpha * l + jnp.sum(p, axis=-1, keepdims=True)
                acc = alpha * acc + _dot(p.astype(BF16), v_ref[h, rows, :])
                out.append((m_new, l, acc))
            return tuple(out)

        init = tuple((jnp.full((t, 1), -jnp.inf, F32), jnp.zeros((t, 1), F32), jnp.zeros((t, MLA_V), F32))
                     for _ in range(MLA_HEADS))
        carry = lax.fori_loop(0, i, lambda j, c: step(j, c, False), init)
        carry = step(i, carry, True)
        for h in range(MLA_HEADS):
            m, l, acc = carry[h]
            o_ref[:, 128 * h:128 * h + 128] = acc / l
            l_ref[h] = jnp.broadcast_to(m + jnp.log(l), (t, 128))

    return pl.pallas_call(
        body, name="mla_attn_fwd", grid=(S // t,),
        in_specs=[pl.BlockSpec((MLA_HEADS, t, 256), lambda i: (0, i, 0)),
                  pl.BlockSpec((MLA_HEADS, S, 256), lambda i: (0, 0, 0)),
                  pl.BlockSpec((MLA_HEADS, S, 128), lambda i: (0, 0, 0))],
        out_specs=[pl.BlockSpec((t, 512), lambda i: (i, 0)),
                   pl.BlockSpec((MLA_HEADS, t, 128), lambda i: (0, i, 0))],
        out_shape=[jax.ShapeDtypeStruct((S, 512), F32), jax.ShapeDtypeStruct((MLA_HEADS, S, 128), F32)],
        compiler_params=_params(("parallel",)),
    )(qc, kc, vv)


def _mla_attn_bwd(qc, kc, vv, o, lse, do, *, t=512, tq=1024):
    S = qc.shape[1]
    t = min(t, S)
    tq = min(tq, S)
    nblk = S // t
    hp = MLA_HEADS
    once = pl.Buffered(1)

    def body(q_ref, k_ref, v_ref, o_ref, l_ref, do_ref, dq_ref, dk_ref, dv_ref):
        j = pl.program_id(1)

        @pl.when(j == 0)
        def _():
            dq_ref[...] = jnp.zeros_like(dq_ref)

        first = (j * t) // tq

        def step(i, carry, masked):
            rows = pl.ds(pl.multiple_of(i * tq, tq), tq)
            if masked:
                row = i * tq + lax.broadcasted_iota(jnp.int32, (tq, t), 0)
                col = j * t + lax.broadcasted_iota(jnp.int32, (tq, t), 1)
                visible = col <= row
            out = []
            for h in range(hp):
                dk, dv = carry[h]
                k = k_ref[h]
                q = q_ref[h, rows, :]
                dov = do_ref[rows, 128 * h:128 * h + 128]
                lrow = l_ref[h, rows, :][:, 0:1]
                p = jnp.exp(_dot_nt(q, k) * MLA_SCALE - lrow)
                if masked:
                    p = jnp.where(visible, p, 0.0)
                dob = dov.astype(BF16)
                dv = dv + _dot_tn(p.astype(BF16), dob)
                dp = _dot_nt(dob, v_ref[h])
                delta = jnp.sum(dov * o_ref[rows, 128 * h:128 * h + 128], axis=-1, keepdims=True)
                ds = (p * (dp - delta) * MLA_SCALE).astype(BF16)
                dk = dk + _dot_tn(ds, q)
                dq_ref[h, rows, :] += _dot(ds, k)
                out.append((dk, dv))
            return tuple(out)

        init = tuple((jnp.zeros((t, 256), F32), jnp.zeros((t, MLA_V), F32)) for _ in range(hp))
        carry = step(first, init, True)
        carry = lax.fori_loop(first + 1, S // tq, lambda i, c: step(i, c, False), carry)
        for h in range(hp):
            dk_ref[h] = carry[h][0]
            dv_ref[h] = carry[h][1]

    return pl.pallas_call(
        body, name="mla_attn_bwd", grid=(MLA_HEADS // hp, nblk),
        in_specs=[pl.BlockSpec((hp, S, 256), lambda g, j: (g, 0, 0), pipeline_mode=once),
                  pl.BlockSpec((hp, t, 256), lambda g, j: (g, j, 0)),
                  pl.BlockSpec((hp, t, 128), lambda g, j: (g, j, 0)),
                  pl.BlockSpec((S, 128 * hp), lambda g, j: (0, g), pipeline_mode=once),
                  pl.BlockSpec((hp, S, 128), lambda g, j: (g, 0, 0), pipeline_mode=once),
                  pl.BlockSpec((S, 128 * hp), lambda g, j: (0, g), pipeline_mode=once)],
        out_specs=[pl.BlockSpec((hp, S, 256), lambda g, j: (g, 0, 0)),
                   pl.BlockSpec((hp, t, 256), lambda g, j: (g, j, 0)),
                   pl.BlockSpec((hp, t, 128), lambda g, j: (g, j, 0))],
        out_shape=[jax.ShapeDtypeStruct((MLA_HEADS, S, 256), F32), jax.ShapeDtypeStruct((MLA_HEADS, S, 256), F32),
                   jax.ShapeDtypeStruct((MLA_HEADS, S, 128), F32)],
        compiler_params=_params(("parallel", "arbitrary")),
    )(qc, kc, vv, o, lse, do)


def _mla_pre_bwd(proj, qn_w, kvn_w, wuqT, wukv, cos, sin, dqc, dkc, dvv, *, tm=ROW_TILE):
    S = proj.shape[0]
    tm = min(tm, S)

    def body(ql_ref, kl_ref, qw_ref, kw_ref, wuq_ref, wukv_ref, cos_ref, sin_ref, dqc_ref, dkc_ref, dvv_ref,
             dql_ref, dkl_ref, dkr_ref, gq_ref, gkv_ref, part_ref, gq_acc, gkv_acc):
        @pl.when(pl.program_id(0) == 0)
        def _():
            gq_acc[...] = jnp.zeros_like(gq_acc)
            gkv_acc[...] = jnp.zeros_like(gkv_acc)
            part_ref[...] = jnp.zeros_like(part_ref)

        cs, sn = _rope_wide(cos_ref), _rope_wide(sin_ref)
        half = lax.broadcasted_iota(jnp.int32, (tm, 128), 1) // 64
        ql = ql_ref[...]
        rq = _rstd(ql)
        qhat = ql * rq
        qw = qw_ref[...]
        qn = (qhat * qw).astype(BF16)
        chunks = []
        for pair in range(2):
            chunks.append(jnp.where(half == 0, dqc_ref[2 * pair, :, 128:256], dqc_ref[2 * pair + 1, :, 128:256]))
        dqr = jnp.concatenate(chunks, axis=1)
        dqr = dqr * cs + _swap_halves(dqr * sn)
        dq = jnp.concatenate([dqc_ref[h, :, 0:128] for h in range(MLA_HEADS)] + [dqr], axis=1).astype(BF16)
        gq_acc[...] += _dot_tn(dq, qn)
        dqn = _dot(dq, wuq_ref[...])
        part_ref[0:1, :] += jnp.sum(dqn * qhat, axis=0, keepdims=True)
        dql_ref[...] = _rms_bwd(dqn * qw, qhat, rq)
        kl = kl_ref[...]
        rk = _rstd(kl)
        khat = kl * rk
        kw = kw_ref[...]
        kvn = (khat * kw).astype(BF16)
        dkvn = jnp.zeros((tm, MLA_KVR), F32)
        dkr2 = jnp.zeros((tm, 128), F32)
        for h in range(MLA_HEADS):
            dkn = dkc_ref[h, :, 0:128].astype(BF16)
            dvh = dvv_ref[h].astype(BF16)
            gkv_acc[2 * h] += _dot_tn(kvn, dkn)
            gkv_acc[2 * h + 1] += _dot_tn(kvn, dvh)
            dkvn += _dot_nt(dkn, wukv_ref[2 * h]) + _dot_nt(dvh, wukv_ref[2 * h + 1])
            dkr2 += dkc_ref[h, :, 128:256]
        part_ref[1:2, 0:128] += jnp.sum(dkvn * khat, axis=0, keepdims=True)
        dkl_ref[...] = _rms_bwd(dkvn * kw, khat, rk)
        dkr = jnp.where(half == 0, dkr2 + pltpu.roll(dkr2, 64, 1), 0.0)
        dkr_ref[...] = dkr * cs[:, :128] + _swap_halves(dkr * sn[:, :128])

        @pl.when(pl.program_id(0) == S // tm - 1)
        def _():
            gkv_ref[...] = gkv_acc[...].astype(BF16)
            per = MLA_NOPE + MLA_ROPE
            for h in range(MLA_HEADS):
                gq_ref[per * h:per * h + MLA_NOPE, :] = gq_acc[MLA_NOPE * h:MLA_NOPE * (h + 1), :].astype(BF16)
                gq_ref[per * h + MLA_NOPE:per * (h + 1), :] = gq_acc[512 + MLA_ROPE * h:512 + MLA_ROPE * (h + 1), :].astype(BF16)

    const = lambda shape: pl.BlockSpec(shape, lambda i: tuple(0 for _ in shape))
    heads = lambda w: pl.BlockSpec((MLA_HEADS, tm, w), lambda i: (0, i, 0))
    return pl.pallas_call(
        body, name="mla_pre_bwd", grid=(S // tm,),
        in_specs=[pl.BlockSpec((tm, 256), lambda i: (i, 3)), pl.BlockSpec((tm, 128), lambda i: (i, 8)),
                  const((1, 256)), const((1, 128)), const((768, 256)), const((8, 128, 128)),
                  pl.BlockSpec((tm, 128), lambda i: (i, 0)), pl.BlockSpec((tm, 128), lambda i: (i, 0)),
                  heads(256), heads(256), heads(128)],
        out_specs=[pl.BlockSpec((tm, 256), lambda i: (i, 0)), pl.BlockSpec((tm, 128), lambda i: (i, 0)),
                   pl.BlockSpec((tm, 128), lambda i: (i, 0)), const((768, 256)), const((8, 128, 128)), const((8, 256))],
        out_shape=[jax.ShapeDtypeStruct((S, 256), F32), jax.ShapeDtypeStruct((S, 128), F32),
                   jax.ShapeDtypeStruct((S, 128), F32), jax.ShapeDtypeStruct((768, 256), BF16),
                   jax.ShapeDtypeStruct((8, 128, 128), BF16), jax.ShapeDtypeStruct((8, 256), F32)],
        scratch_shapes=[pltpu.VMEM((768, 256), F32), pltpu.VMEM((8, 128, 128), F32)],
        compiler_params=_params(("arbitrary",)),
    )(proj, proj, qn_w, kvn_w, wuqT, wukv, cos, sin, dqc, dkc, dvv)


def _mix_out_fwd(x, oa, ob, w_o, vecs, *, tm=ROW_TILE):
    S = x.shape[0]
    tm = min(tm, S)

    def body(x_ref, oa_ref, ob_ref, w_ref, vec_ref, xo_ref, mo_ref):
        mo = _dot(oa_ref[...].astype(BF16), w_ref[0:512, :]) + _dot(ob_ref[...].astype(BF16), w_ref[512:1024, :])
        mo_ref[...] = mo
        xo_ref[...] = x_ref[...] + vec_ref[3:4, :] * mo

    row = pl.BlockSpec((tm, D), lambda i: (i, 0))
    half = pl.BlockSpec((tm, 512), lambda i: (i, 0))
    return pl.pallas_call(
        body, name="mix_out_fwd", grid=(S // tm,),
        in_specs=[row, half, half, pl.BlockSpec((D, D), lambda i: (0, 0)), pl.BlockSpec((8, D), lambda i: (0, 0))],
        out_specs=[row, row],
        out_shape=[jax.ShapeDtypeStruct((S, D), F32), jax.ShapeDtypeStruct((S, D), F32)],
        compiler_params=_params(("parallel",)),
    )(x, oa, ob, w_o, vecs)


def _mix_out_bwd(dxo, mo, oa, ob, w_o, vecs, *, tm=ROW_TILE):
    S = dxo.shape[0]
    tm = min(tm, S)

    def body(dx_ref, mo_ref, oa_ref, ob_ref, w_ref, vec_ref, doa_ref, dob_ref, gw_ref, part_ref, gw_acc):
        @pl.when(pl.program_id(0) == 0)
        def _():
            gw_acc[...] = jnp.zeros_like(gw_acc)
            part_ref[...] = jnp.zeros_like(part_ref)

        dx = dx_ref[...]
        part_ref[0:1, :] += jnp.sum(dx * mo_ref[...], axis=0, keepdims=True)
        dmo = (vec_ref[3:4, :] * dx).astype(BF16)
        doa_ref[...] = _dot_nt(dmo, w_ref[0:512, :])
        dob_ref[...] = _dot_nt(dmo, w_ref[512:1024, :])
        gw_acc[0:512, :] += _dot_tn(oa_ref[...].astype(BF16), dmo)
        gw_acc[512:1024, :] += _dot_tn(ob_ref[...].astype(BF16), dmo)

        @pl.when(pl.program_id(0) == S // tm - 1)
        def _():
            gw_ref[...] = gw_acc[...].astype(BF16)

    row = pl.BlockSpec((tm, D), lambda i: (i, 0))
    half = pl.BlockSpec((tm, 512), lambda i: (i, 0))
    return pl.pallas_call(
        body, name="mix_out_bwd", grid=(S // tm,),
        in_specs=[row, row, half, half, pl.BlockSpec((D, D), lambda i: (0, 0)), pl.BlockSpec((8, D), lambda i: (0, 0))],
        out_specs=[half, half, pl.BlockSpec((D, D), lambda i: (0, 0)), pl.BlockSpec((8, D), lambda i: (0, 0))],
        out_shape=[jax.ShapeDtypeStruct((S, 512), F32), jax.ShapeDtypeStruct((S, 512), F32),
                   jax.ShapeDtypeStruct((D, D), BF16), jax.ShapeDtypeStruct((8, D), F32)],
        scratch_shapes=[pltpu.VMEM((D, D), F32)],
        compiler_params=_params(("arbitrary",)),
    )(dxo, mo, oa, ob, w_o, vecs)


def _mix_in_bwd(h, w_inT, dq, dk, dv, dql, dkl, dkr, *, tm=ROW_TILE):
    S = h.shape[0]
    tm = min(tm, S)
    wid = (512, 128, 128, 256, 128, 128)

    def body(h_ref, w_ref, dq_ref, dk_ref, dv_ref, dql_ref, dkl_ref, dkr_ref, dh_ref, gw_ref):
        @pl.when(pl.program_id(0) == 0)
        def _():
            gw_ref[...] = jnp.zeros_like(gw_ref)

        parts = (dq_ref, dk_ref, dv_ref, dql_ref, dkl_ref, dkr_ref)
        dproj = jnp.concatenate([ref[...].astype(BF16) for ref in parts], axis=1)
        dh_ref[...] = _dot(dproj, w_ref[...])
        gw_ref[...] += _dot_tn(dproj, h_ref[...])[0:D_IN, :]

    row = pl.BlockSpec((tm, D), lambda i: (i, 0))
    part = lambda w: pl.BlockSpec((tm, w), lambda i: (i, 0))
    return pl.pallas_call(
        body, name="mix_in_bwd", grid=(S // tm,),
        in_specs=[row, pl.BlockSpec((D_IN_PAD, D), lambda i: (0, 0))] + [part(w) for w in wid],
        out_specs=[row, pl.BlockSpec((D_IN, D), lambda i: (0, 0))],
        out_shape=[jax.ShapeDtypeStruct((S, D), F32), jax.ShapeDtypeStruct((D_IN, D), F32)],
        compiler_params=_params(("arbitrary",)),
    )(h, w_inT, dq, dk, dv, dql, dkl, dkr)


def _vecs(norm_w, mod9, k):
    return jnp.concatenate([norm_w.reshape(1, D), mod9[3 * k:3 * k + 3], jnp.zeros((4, D), F32)], axis=0)


def _uq_group_rows(wuqT):
    per = MLA_NOPE + MLA_ROPE
    nope = [wuqT[per * h:per * h + MLA_NOPE] for h in range(MLA_HEADS)]
    rope = [wuqT[per * h + MLA_NOPE:per * (h + 1)] for h in range(MLA_HEADS)]
    return jnp.concatenate(nope + rope, axis=0)


def _local_step(x, tgt, mod9, norms, sinks, rel_bias, q_norm, kv_norm, W, on_grads=None):
    if on_grads is None:
        on_grads = lambda group, grads, after, vecs: vecs
    S = x.shape[0]
    v1 = _vecs(norms["ffn1"], mod9, 0)
    v2 = _vecs(norms["mix"], mod9, 1)
    v3 = _vecs(norms["ffn2"], mod9, 2)
    bucket = jnp.asarray(_bucket_table())
    cos, sin = _rope_tables(S)
    if isinstance(W, dict):
        full, W = W, (lambda group, after, vecs: (full, vecs))

    W1, v1 = W("ffn1", [], v1)
    x1, h1, a1, b1, f1 = _ffn_fwd(x, v1, W1["g1T"], W1["u1T"], W1["d1"], name="ffn1_fwd")
    W2, v2 = W("mixer", [x1], v2)
    wuqT = _uq_group_rows(W2["w_uqT"])
    h2, proj, w_inT = _mix_in_fwd(x1, v2, W2["w_inT"])
    oa, bias = _swa_fwd(proj, rel_bias, bucket, sinks)
    qc, kc, vv = _mla_pre_fwd(proj, q_norm, kv_norm, wuqT, W2["w_ukv"], cos, sin)
    ob, lse = _mla_attn_fwd(qc, kc, vv)
    _, v2o = W("ffn2_on_its_way", [ob], v2)
    x2, mo = _mix_out_fwd(x1, oa, ob, W2["w_o"], v2o)
    W3, v3 = W("ffn2", [x2], v3)
    x3, h3, a3, b3, f3 = _ffn_fwd(x2, v3, W3["g3T"], W3["u3T"], W3["d3"], name="ffn2_fwd")
    dx3, head_part, df3 = _head(x3, tgt, norms["final"], f3, v3)

    gg3, gu3, gd3, dh3 = _ffn_bwd_main(h3, df3, a3, b3, W3["g3T"], W3["u3T"], W3["d3"], name="ffn2_bwd")
    ffn2 = {"g3T": gg3, "u3T": gu3, "d3": gd3}
    v3 = on_grads("ffn2", ffn2, [], v3)
    dx2, n3_part = _norm_bwd(dh3, x2, dx3, v3, name="ffn2_norm_bwd")
    v2 = on_grads("ffn2", None, [dx2], v2)
    doa, dob, g_wo, g2_part = _mix_out_bwd(dx2, mo, oa, ob, W2["w_o"], v2)
    dq, dk, dv, drb, dsk = _swa_bwd(proj, bias, sinks, oa, doa, bucket)
    dqc, dkc, dvv = _mla_attn_bwd(qc, kc, vv, ob, lse, dob)
    dql, dkl, dkr, g_uq, g_ukv, mla_part = _mla_pre_bwd(proj, q_norm, kv_norm, wuqT, W2["w_ukv"], cos, sin, dqc, dkc, dvv)
    dh2, g_win = _mix_in_bwd(h2, w_inT, dq, dk, dv, dql, dkl, dkr)
    mixer = {"w_inT": g_win, "w_uqT": g_uq, "w_ukv": g_ukv, "w_o": g_wo}
    v2 = on_grads("mixer", mixer, [], v2)
    dx1, n2_part, df1 = _norm_bwd(dh2, x1, dx2, v2, name="mix_norm_bwd", below=(f1, v1))
    started = on_grads("mixer", None, [dx1], jnp.zeros((1, 1), F32))
    gg1, gu1, gd1, dh1 = _ffn_bwd_main(h1, df1, a1, b1, W1["g1T"], W1["u1T"], W1["d1"], name="ffn1_bwd",
                                       after=[started])
    ffn1 = {"g1T": gg1, "u1T": gu1, "d1": gd1}
    v1 = on_grads("ffn1", ffn1, [], v1)
    dx0, n1_part = _norm_bwd(dh1, x, dx1, v1, name="ffn1_norm_bwd")

    grads = {**ffn1, **ffn2, **mixer}
    return head_part[1, 0], dx0, grads, _pack_vec(n1_part, n2_part, n3_part, head_part, g2_part, mla_part, dsk, drb)


SMALL_LAYOUT = (("norm_ffn1", 1024), ("norm_mix", 1024), ("norm_ffn2", 1024), ("norm_final", 1024),
                ("q_norm", 256), ("kv_norm", 128), ("sinks", 128), ("rel_bias", 256))
N_SMALL = sum(n for _, n in SMALL_LAYOUT)
LOSS_SLOT = 4 * 1024 + 256 + 128 + SWA_HEADS
N_MODVEC = N_MOD * D
N_VEC = N_MODVEC + N_SMALL


def _pack_vec(n1, n2, n3, head, g2, mla, dsk, drb):
    def body(n1_ref, n2_ref, n3_ref, head_ref, g2_ref, mla_ref, dsk_ref, drb_ref, out_ref):
        rows = [n1_ref[1:2, :], n1_ref[2:3, :], n2_ref[3:4, :], n2_ref[1:2, :], n2_ref[2:3, :], g2_ref[0:1, :],
                n3_ref[1:2, :], n3_ref[2:3, :], head_ref[3:4, :],
                n1_ref[0:1, :], n2_ref[0:1, :], n3_ref[0:1, :], head_ref[0:1, :]]
        for i, row in enumerate(rows):
            out_ref[:, D * i:D * (i + 1)] = row
        off = D * len(rows)
        out_ref[:, off:off + 256] = mla_ref[0:1, :]
        out_ref[:, off + 256:off + 384] = mla_ref[1:2, 0:128]

        def diagonal(block):
            r = lax.broadcasted_iota(jnp.int32, block.shape, 0)
            lane = lax.broadcasted_iota(jnp.int32, block.shape, 1)
            return jnp.sum(jnp.where(r == lane, block, 0.0), axis=0, keepdims=True)

        lane = lax.broadcasted_iota(jnp.int32, (1, 128), 1)
        out_ref[:, off + 384:off + 512] = jnp.where(lane == SWA_HEADS, head_ref[1:2, 0:128], diagonal(dsk_ref[...]))
        out_ref[:, off + 512:off + 640] = diagonal(drb_ref[0:128, :])
        out_ref[:, off + 640:off + 768] = diagonal(drb_ref[128:256, :])

    vm = pl.BlockSpec(memory_space=pltpu.VMEM)
    return pl.pallas_call(body, name="pack_vec", in_specs=[vm] * 8, out_specs=vm,
                          out_shape=jax.ShapeDtypeStruct((1, N_VEC), F32))(n1, n2, n3, head, g2, mla, dsk, drb)


def _coords():
    return lax.axis_index("x"), lax.axis_index("y"), lax.axis_index("c")


def _flip(v, bit):
    return 1 - v if bit else v


def _peer(r):
    x, y, c = _coords()
    return (_flip(x, r & 4), _flip(y, r & 2), _flip(c, r & 1))


class _ModExchange:
    def __init__(self, c_ref, w_ref, b_ref, mod_ref, ca_ref, call_ref, part_ref, send_sems, recv_sems):
        self.refs = (c_ref, w_ref, b_ref, mod_ref, ca_ref, call_ref, part_ref)
        self.sems = (send_sems, recv_sems)
        x, y, c = _coords()
        self.me = 4 * x + 2 * y + c
        self.sends = []

    def _copy(self, phase, r):
        c_ref, _, _, mod_ref, _, call_ref, part_ref = self.refs
        src, dst = (c_ref, call_ref.at[self.me]) if phase == 0 else (part_ref.at[self.me ^ r], mod_ref.at[self.me])
        return pltpu.make_async_remote_copy(src, dst, self.sems[0].at[phase, r], self.sems[1].at[phase, r],
                                            device_id=_peer(r), device_id_type=MESH)

    def _start(self, phase):
        for r in range(1, N_DEV):
            self.sends.append(self._copy(phase, r))
            self.sends[-1].start()

    def begin(self):
        c_ref, _, _, _, _, call_ref, _ = self.refs
        call_ref[self.me] = c_ref[...]
        self._start(0)

    def middle(self):
        _, w_ref, b_ref, mod_ref, ca_ref, call_ref, part_ref = self.refs
        for r in range(1, N_DEV):
            self._copy(0, r).wait_recv()
        cv = call_ref[...].reshape(8 * N_DEV, D)
        ca = (cv * _sigmoid(cv)).astype(BF16)
        ca_ref[...] = ca
        part_ref[...] = _dot(ca, w_ref[...].astype(BF16)).reshape(part_ref.shape)
        mod_ref[self.me] = part_ref[self.me] + b_ref[self.me]
        self._start(1)

    def end(self):
        _, _, b_ref, mod_ref, _, _, _ = self.refs
        for r in range(1, N_DEV):
            self._copy(1, r).wait_recv()
            mod_ref[self.me ^ r] = mod_ref[self.me ^ r] + b_ref[self.me ^ r]
        for cp in self.sends:
            cp.wait_send()


def _mod_bwd(allvec, ca, me_idx):
    W = N_MODVEC // N_DEV

    def body(me_ref, all_ref, cols_ref, ca_ref, gw_ref, sum_ref):
        in_first_row = lax.broadcasted_iota(jnp.int32, (N_DEV, 8, W), 1) == 0
        dm = jnp.where(in_first_row, cols_ref[...], 0.0).reshape(8 * N_DEV, W)
        gw_ref[...] = _dot_tn(ca_ref[...], dm.astype(BF16))
        total = all_ref[0]
        for k in range(1, N_DEV):
            total = total + all_ref[k]
        sum_ref[...] = total

    return pl.pallas_call(
        body, name="mod_bwd",
        grid_spec=pltpu.PrefetchScalarGridSpec(
            num_scalar_prefetch=1, grid=(1,),
            in_specs=[pl.BlockSpec((N_DEV, 1, N_VEC), lambda i, me: (0, 0, 0)),
                      pl.BlockSpec((N_DEV, 1, W), lambda i, me: (0, 0, me[0])),
                      pl.BlockSpec((8 * N_DEV, D), lambda i, me: (0, 0))],
            out_specs=[pl.BlockSpec((D, W), lambda i, me: (0, 0)), pl.BlockSpec((1, N_VEC), lambda i, me: (0, 0))]),
        out_shape=[jax.ShapeDtypeStruct((D, W), F32), jax.ShapeDtypeStruct((1, N_VEC), F32)],
        compiler_params=_params(("arbitrary",)),
    )(me_idx, allvec, allvec, ca)


def _wgather(shards, c_tile, w_mod, b_mod3):
    n = len(shards)
    rows = [s.shape[0] for s in shards]

    W = w_mod.shape[1]

    def body(*refs):
        ins, (c_ref, w_ref, b_ref), outs = refs[:n], refs[n:n + 3], refs[n + 3:2 * n + 3]
        token, mod_ref, ca_ref, send_sems, recv_sems, local_sems = refs[2 * n + 3:2 * n + 9]
        mod = _ModExchange(c_ref, w_ref, b_ref, mod_ref, ca_ref, *refs[2 * n + 9:])
        mod.begin()
        token[...] = jnp.zeros_like(token)
        x, y, c = _coords()
        me = 4 * x + 2 * y + c
        sib, xn, yn = (x, y, 1 - c), (1 - x, y, c), (x, 1 - y, c)
        block = lambda px, py, pc: 4 * px + 2 * py + pc

        def part(k, blk, half):
            if half is None:
                return outs[k].at[blk]
            return outs[k].at[blk, pl.ds(half * (rows[k] // 2), rows[k] // 2)]

        def copy(k, slot, blk, to, half=None, src=None):
            ref = part(k, blk, half)
            return pltpu.make_async_remote_copy(
                src_ref=ref if src is None else src, dst_ref=ref, send_sem=send_sems.at[k, slot],
                recv_sem=recv_sems.at[k, slot], device_id=to, device_id_type=MESH)

        local = [pltpu.make_async_copy(ins[k], outs[k].at[me], local_sems.at[k]) for k in range(n)]
        for cp in local:
            cp.start()
        sent = [copy(k, slot, me, to, src=ins[k]) for k in range(n) for slot, to in ((0, sib), (1, xn), (2, yn))]
        for cp in sent:
            cp.start()
        mod.middle()
        bx, by, bd = block(1 - x, y, c), block(x, 1 - y, c), block(1 - x, 1 - y, c)
        for k in range(n):
            copy(k, 1, bx, sib).wait_recv()
            sent += [copy(k, 4, bx, yn, half=1), copy(k, 5, bx, sib)]
            sent[-2].start()
            sent[-1].start()
        for k in range(n):
            copy(k, 2, by, sib).wait_recv()
            sent += [copy(k, 3, by, xn, half=0), copy(k, 6, by, sib)]
            sent[-2].start()
            sent[-1].start()
        for k in range(n):
            copy(k, 3, bd, sib, half=0).wait_recv()
            copy(k, 4, bd, sib, half=1).wait_recv()
            sent.append(copy(k, 7, bd, sib))
            sent[-1].start()
        for k in range(n):
            copy(k, 0, block(x, y, 1 - c), sib).wait_recv()
            for slot, blk in ((5, block(1 - x, y, 1 - c)), (6, block(x, 1 - y, 1 - c)), (7, block(1 - x, 1 - y, 1 - c))):
                copy(k, slot, blk, sib).wait_recv()
        for cp in sent:
            cp.wait_send()
        for cp in local:
            cp.wait()
        mod.end()

    anyspec, vm = pl.BlockSpec(memory_space=pl.ANY), pl.BlockSpec(memory_space=pltpu.VMEM)
    return pl.pallas_call(
        body, name="wgather", in_specs=[anyspec] * n + [vm] * 3,
        out_specs=[anyspec] * n + [vm] * 3,
        out_shape=[jax.ShapeDtypeStruct((N_DEV,) + s.shape, s.dtype) for s in shards]
        + [jax.ShapeDtypeStruct((8, 128), F32), jax.ShapeDtypeStruct((N_DEV, 8, W), F32),
           jax.ShapeDtypeStruct((8 * N_DEV, D), BF16)],
        scratch_shapes=[pltpu.SemaphoreType.DMA((n, 8)), pltpu.SemaphoreType.DMA((n, 8)),
                        pltpu.SemaphoreType.DMA((n,)),
                        pltpu.VMEM((N_DEV, 8, D), F32), pltpu.VMEM((N_DEV, 8, W), F32),
                        pltpu.SemaphoreType.DMA((2, N_DEV)), pltpu.SemaphoreType.DMA((2, N_DEV))],
        compiler_params=_params(),
    )(*shards, c_tile, w_mod, b_mod3)


class _GatherCopies:
    def __init__(self, lands, send_sems, recv_sems, k0=0, batches=None):
        x, y, c = _coords()
        me = 4 * x + 2 * y + c
        sib = (x, y, 1 - c)
        chips = [(1 - x, y), (x, 1 - y), (1 - x, 1 - y)]

        def copy(k, slot, block, to):
            return pltpu.make_async_remote_copy(
                src_ref=lands[k].at[block], dst_ref=lands[k].at[block],
                send_sem=send_sems.at[7 * (k0 + k) + slot], recv_sem=recv_sems.at[7 * (k0 + k) + slot],
                device_id=to, device_id_type=MESH)

        n = len(lands)
        self.first = [copy(k, 0, me, sib) for k in range(n)]
        for batch in batches or [range(n)]:
            self.first += [copy(k, 1 + j, me, (cx, cy, c)) for j, (cx, cy) in enumerate(chips) for k in batch]
        self.landed = [copy(k, 1 + j, 4 * cx + 2 * cy + c, sib) for j, (cx, cy) in enumerate(chips) for k in range(n)]
        self.passed = [copy(k, 4 + j, 4 * cx + 2 * cy + c, sib) for j, (cx, cy) in enumerate(chips) for k in range(n)]
        self.from_sib = [copy(k, 0, 4 * x + 2 * y + (1 - c), sib) for k in range(n)]
        self.from_sib += [copy(k, 4 + j, 4 * cx + 2 * cy + (1 - c), sib) for j, (cx, cy) in enumerate(chips)
                          for k in range(n)]


def _token(carry):
    return [] if carry is None else [carry], jax.ShapeDtypeStruct((8, 128), F32) if carry is None else carry


def _gather_start(lands, *, name, batches=None, carry=None):
    n = len(lands)
    carried, token = _token(carry)

    def body(*refs):
        n_in = n + len(carried)
        for cp in _GatherCopies(refs[:n], refs[n_in], refs[n_in + 1], batches=batches).first:
            cp.start()
        refs[-1][...] = refs[n][...] if carried else jnp.zeros_like(refs[-1])

    vm = pl.BlockSpec(memory_space=pltpu.VMEM)
    out = pl.pallas_call(
        body, name=name,
        out_shape=(pltpu.SemaphoreType.DMA((7 * n,)), pltpu.SemaphoreType.DMA((7 * n,)),
                   *[pltpu.HBM(l.shape, l.dtype) for l in lands], jax.ShapeDtypeStruct(token.shape, token.dtype)),
        in_specs=[HBM_SPEC] * n + [vm] * len(carried),
        out_specs=(SEM_SPEC, SEM_SPEC, *[HBM_SPEC] * n, vm),
        input_output_aliases={i: 2 + i for i in range(n)},
        compiler_params=pltpu.CompilerParams(has_side_effects=DATAFLOW),
    )(*[_in_hbm(l) for l in lands], *carried)
    return out[0], out[1], list(out[2:2 + n]), out[-1]


def _gather_pass(send_sems, recv_sems, lands, after, *, name, stage, k0=0, carry=None):
    n = len(lands)
    carried, token = _token(carry)

    def body(*refs):
        cps = _GatherCopies(refs[:n], refs[n], refs[n + 1], k0)
        if stage == "landed":
            for cp in cps.landed:
                cp.wait_recv()
        else:
            for cp in cps.passed:
                cp.start()
        refs[-1][...] = refs[n + 2 + len(after)][...] if carried else jnp.zeros_like(refs[-1])

    vm = pl.BlockSpec(memory_space=pltpu.VMEM)
    out = pl.pallas_call(
        body, name=name,
        out_shape=(*[pltpu.HBM(l.shape, l.dtype) for l in lands], jax.ShapeDtypeStruct(token.shape, token.dtype)),
        in_specs=[HBM_SPEC] * n + [SEM_SPEC, SEM_SPEC] + [pl.BlockSpec(memory_space=pl.ANY)] * len(after)
        + [vm] * len(carried),
        out_specs=(*[HBM_SPEC] * n, vm),
        input_output_aliases={i: i for i in range(n)},
        compiler_params=pltpu.CompilerParams(has_side_effects=DATAFLOW),
    )(*lands, send_sems, recv_sems, *after, *carried)
    return list(out[:n]), out[-1]


def _gather_end(send_sems, recv_sems, lands, after, *, name, k0=0):
    n = len(lands)

    def body(*refs):
        cps = _GatherCopies(refs[:n], refs[n], refs[n + 1], k0)
        for cp in cps.from_sib:
            cp.wait_recv()
        for cp in cps.first + cps.passed:
            cp.wait_send()

    out = pl.pallas_call(
        body, name=name,
        out_shape=[pltpu.HBM(l.shape, l.dtype) for l in lands],
        in_specs=[HBM_SPEC] * n + [SEM_SPEC, SEM_SPEC] + [pl.BlockSpec(memory_space=pl.ANY)] * len(after),
        out_specs=[HBM_SPEC] * n,
        input_output_aliases={i: i for i in range(n)},
        compiler_params=pltpu.CompilerParams(has_side_effects=DATAFLOW),
    )(*lands, send_sems, recv_sems, *after)
    return list(out)


def _d2d_copies(grads, lands, send_sems, recv_sems):
    x, y, c = _coords()
    return [pltpu.make_async_remote_copy(
        src_ref=grads[k].at[2 * q + (1 - c)], dst_ref=lands[k].at[q],
        send_sem=send_sems.at[4 * k + q], recv_sem=recv_sems.at[4 * k + q],
        device_id=(x, y, 1 - c), device_id_type=MESH) for k in range(len(grads)) for q in range(4)]


def _direct_copies(grads, lands, send_sems, recv_sems):
    x, y, c = _coords()
    me = 4 * x + 2 * y + c
    return [pltpu.make_async_remote_copy(
        src_ref=grads[k].at[me ^ r], dst_ref=lands[k].at[r - 1],
        send_sem=send_sems.at[7 * k + r - 1], recv_sem=recv_sems.at[7 * k + r - 1],
        device_id=_peer(r), device_id_type=MESH) for k in range(len(grads)) for r in range(1, N_DEV)]


def _vec_copies(srcs, lands, send_sems, recv_sems):
    x, y, c = _coords()
    me = 4 * x + 2 * y + c
    return [pltpu.make_async_remote_copy(
        src_ref=lands[0].at[me], dst_ref=lands[0].at[me], send_sem=send_sems.at[r - 1], recv_sem=recv_sems.at[r - 1],
        device_id=_peer(r), device_id_type=MESH) for r in range(1, N_DEV)]


def _chipsum(gs, sibs, cidx, *, name):
    n = len(gs)

    def body(c_ref, *refs):
        for k in range(n):
            refs[2 * n + k][...] = (refs[k][...].astype(F32) + refs[n + k][...].astype(F32)).astype(refs[2 * n + k].dtype)

    mine = [pl.BlockSpec((1,) + g.shape[1:], lambda q, c_ref: (2 * q + c_ref[0], 0, 0)) for g in gs]
    other = [pl.BlockSpec((1,) + g.shape[1:], lambda q, c_ref: (q, 0, 0)) for g in gs]
    return pl.pallas_call(
        body, name=name,
        grid_spec=pltpu.PrefetchScalarGridSpec(num_scalar_prefetch=1, grid=(4,), in_specs=mine + other, out_specs=other),
        out_shape=[jax.ShapeDtypeStruct((4,) + g.shape[1:], g.dtype) for g in gs],
        compiler_params=_params(("arbitrary",)),
    )(cidx, *gs, *sibs)


HBM_SPEC = pl.BlockSpec(memory_space=pltpu.HBM)
SEM_SPEC = pl.BlockSpec(memory_space=pltpu.SEMAPHORE)
DATAFLOW = pltpu.SideEffectType.DATAFLOW_SIDE_EFFECTING


def _in_hbm(a):
    return pltpu.with_memory_space_constraint(a, pltpu.HBM)


def _rs_step1_copies(sums, lands, send_sems, recv_sems):
    n = len(sums)
    direct, relay = lands[:n], lands[n:]
    x, y, c = _coords()
    xn, yn = (1 - x, y, c), (x, 1 - y, c)
    qx, qy, qd = 2 * (1 - x) + y, 2 * x + (1 - y), 2 * (1 - x) + (1 - y)
    cps = []
    for k in range(n):
        h = sums[k].shape[1] // 2
        a, b = pl.ds(0, h), pl.ds(h, h)
        moves = ((sums[k].at[qx, a], direct[k].at[0], xn), (sums[k].at[qy, b], direct[k].at[1], yn),
                 (sums[k].at[qd, a], relay[k].at[0], xn), (sums[k].at[qd, b], relay[k].at[1], yn))
        for s, (src, dst, to) in enumerate(moves):
            cps.append(pltpu.make_async_remote_copy(
                src_ref=src, dst_ref=dst, send_sem=send_sems.at[4 * k + s], recv_sem=recv_sems.at[4 * k + s],
                device_id=to, device_id_type=MESH))
    return cps


def _rs_step2_copies(relayed, lands, send_sems, recv_sems, k0=0):
    x, y, c = _coords()
    cps = []
    for k in range(len(relayed)):
        for s, to in enumerate(((1 - x, y, c), (x, 1 - y, c))):
            cps.append(pltpu.make_async_remote_copy(
                src_ref=relayed[k].at[s], dst_ref=lands[k].at[s], send_sem=send_sems.at[2 * (k0 + k) + s],
                recv_sem=recv_sems.at[2 * (k0 + k) + s], device_id=to, device_id_type=MESH))
    return cps


def _relay_sum(sums, relay, qxy, *, name):
    n = len(sums)

    def body(q_ref, *refs):
        for k in range(n):
            refs[2 * n + k][...] = (refs[k][...].astype(F32) + refs[n + k][...].astype(F32)).astype(refs[2 * n + k].dtype)

    half = lambda s: (1, s.shape[1] // 2) + s.shape[2:]
    return pl.pallas_call(
        body, name=name,
        grid_spec=pltpu.PrefetchScalarGridSpec(
            num_scalar_prefetch=1, grid=(2,),
            in_specs=[pl.BlockSpec(half(s), lambda t, q_ref: (q_ref[t], 1 - t, 0)) for s in sums]
            + [pl.BlockSpec(half(s), lambda t, q_ref: (1 - t, 0, 0)) for s in sums],
            out_specs=[pl.BlockSpec(half(s), lambda t, q_ref: (t, 0, 0)) for s in sums]),
        out_shape=[jax.ShapeDtypeStruct((2,) + half(s)[1:], s.dtype) for s in sums],
        compiler_params=_params(("arbitrary",)),
    )(qxy, *sums, *relay)


def _split_start(copies, srcs, lands, n_sems, after, *, name, carry=None):
    ns, nl = len(srcs), len(lands)
    carried, token = _token(carry)

    def body(*refs):
        n_in = ns + nl + len(after) + len(carried)
        for cp in copies(refs[:ns], refs[ns:ns + nl], refs[n_in], refs[n_in + 1]):
            cp.start()
        refs[-1][...] = refs[n_in - 1][...] if carried else jnp.zeros_like(refs[-1])

    bufs = [_in_hbm(a) for a in list(srcs) + list(lands)]
    vm = pl.BlockSpec(memory_space=pltpu.VMEM)
    out = pl.pallas_call(
        body, name=name,
        out_shape=(pltpu.SemaphoreType.DMA((n_sems,)), pltpu.SemaphoreType.DMA((n_sems,)),
                   *[pltpu.HBM(a.shape, a.dtype) for a in bufs], jax.ShapeDtypeStruct(token.shape, token.dtype)),
        in_specs=[HBM_SPEC] * len(bufs) + [pl.BlockSpec(memory_space=pl.ANY)] * len(after) + [vm] * len(carried),
        out_specs=(SEM_SPEC, SEM_SPEC, *[HBM_SPEC] * len(bufs), vm),
        input_output_aliases={i: 2 + i for i in range(len(bufs))},
        compiler_params=pltpu.CompilerParams(has_side_effects=DATAFLOW),
    )(*bufs, *after, *carried)
    return out[0], out[1], list(out[2:2 + ns]), list(out[2 + ns:2 + ns + nl]), out[-1]


def _split_wait(copies, send_sems, recv_sems, srcs, lands, after, *, name):
    ns, nl = len(srcs), len(lands)

    def body(*refs):
        for cp in copies(refs[:ns], refs[ns:ns + nl], refs[ns + nl], refs[ns + nl + 1]):
            cp.wait_send()
            cp.wait_recv()

    out = pl.pallas_call(
        body, name=name,
        out_shape=[pltpu.HBM(a.shape, a.dtype) for a in list(srcs) + list(lands)],
        in_specs=[HBM_SPEC] * (ns + nl) + [SEM_SPEC, SEM_SPEC] + [pl.BlockSpec(memory_space=pl.ANY)] * len(after),
        out_specs=[HBM_SPEC] * (ns + nl),
        input_output_aliases={i: i for i in range(ns + nl)},
        compiler_params=pltpu.CompilerParams(has_side_effects=DATAFLOW),
    )(*srcs, *lands, send_sems, recv_sems, *after)
    return list(out[:ns]), list(out[ns:])


ADAM_C1 = 1.0 / (1.0 - ADAM_B1 ** ADAM_STEP)
ADAM_C2 = 1.0 / (1.0 - ADAM_B2 ** ADAM_STEP)


def _adam_math(w, g, m, v):
    m2 = ADAM_B1 * m + (1.0 - ADAM_B1) * g
    v2 = ADAM_B2 * v + (1.0 - ADAM_B2) * (g * g)
    return -ADAM_LR * ((m2 * ADAM_C1) / (jnp.sqrt(v2 * ADAM_C2) + ADAM_EPS) + ADAM_WD * w), m2, v2


def _adamw(w, g, m, v, *, name, after=()):
    R, C = w.shape
    tr = R if R <= 512 else 256

    def body(w_ref, g_ref, m_ref, v_ref, *rest):
        d_ref, nm_ref, nv_ref = rest[len(after):]
        d_ref[...], nm_ref[...], nv_ref[...] = _adam_math(w_ref[...], g_ref[...], m_ref[...], v_ref[...])

    blk = pl.BlockSpec((tr, C), lambda i: (i, 0))
    return pl.pallas_call(
        body, name=name, grid=(R // tr,), in_specs=[blk] * 4 + [pl.BlockSpec(memory_space=pl.ANY)] * len(after),
        out_specs=[blk] * 3, out_shape=[jax.ShapeDtypeStruct((R, C), F32)] * 3,
        compiler_params=_params(("parallel",)),
    )(w, g, m, v, *after)


def _adamw_rs2(wmv, cs, direct, second, qidx, *, name):
    n = len(wmv)
    r, cc = wmv[0][0].shape
    h = r // 2

    def body(q_ref, *refs):
        ins, outs = refs[:6 * n], refs[6 * n:]
        for k in range(n):
            w_ref, m_ref, v_ref, c_ref, d1_ref, d2_ref = ins[6 * k:6 * k + 6]
            g_ref, d_ref, nm_ref, nv_ref = outs[4 * k:4 * k + 4]
            g = (c_ref[0].astype(F32) + d1_ref[0].astype(F32)) + d2_ref[0].astype(F32)
            g_ref[...] = g
            d_ref[...], nm_ref[...], nv_ref[...] = _adam_math(w_ref[...], g, m_ref[...], v_ref[...])

    blk = pl.BlockSpec((h, cc), lambda i, q_ref: (i, 0))
    one = [blk, blk, blk, pl.BlockSpec((1, h, cc), lambda i, q_ref: (q_ref[0], i, 0)),
           pl.BlockSpec((1, h, cc), lambda i, q_ref: (i, 0, 0)),
           pl.BlockSpec((1, h, cc), lambda i, q_ref: (1 - i, 0, 0))]
    out = pl.pallas_call(
        body, name=name,
        grid_spec=pltpu.PrefetchScalarGridSpec(num_scalar_prefetch=1, grid=(2,), in_specs=one * n,
                                               out_specs=[blk] * (4 * n)),
        out_shape=[jax.ShapeDtypeStruct((r, cc), F32)] * (4 * n),
        compiler_params=_params(("arbitrary",)),
    )(qidx, *[a for (w, m, v), c, d1, d2 in zip(wmv, cs, direct, second) for a in (w, m, v, c, d1, d2)])
    return [tuple(out[4 * k:4 * k + 4]) for k in range(n)]


def _adamw_rs(wmv, cs, rcv, qidx, *, name):
    n = len(wmv)
    shapes = [w.shape for w, _, _ in wmv]
    n_rcv = rcv[0].shape[0]
    halved = len(set(shapes)) == 1 and shapes[0][0] % 32 == 0 and shapes[0][0] > 128
    tiles = 2 if halved else 1

    def body(q_ref, *refs):
        ins, outs = refs[:5 * n], refs[5 * n:]
        for k in range(n):
            w_ref, m_ref, v_ref, c_ref, r_ref = ins[5 * k:5 * k + 5]
            g_ref, d_ref, nm_ref, nv_ref = outs[4 * k:4 * k + 4]
            g = c_ref[0].astype(F32)
            for j in range(n_rcv):
                g = g + r_ref[j].astype(F32)
            g_ref[...] = g
            d_ref[...], nm_ref[...], nv_ref[...] = _adam_math(w_ref[...], g, m_ref[...], v_ref[...])

    in_specs, out_specs = [], []
    for r, cc in shapes:
        blk = pl.BlockSpec((r // tiles, cc), lambda i, q_ref: (i, 0))
        in_specs += [blk, blk, blk, pl.BlockSpec((1, r // tiles, cc), lambda i, q_ref: (q_ref[0], i, 0)),
                     pl.BlockSpec((n_rcv, r // tiles, cc), lambda i, q_ref: (0, i, 0))]
        out_specs += [blk] * 4
    out = pl.pallas_call(
        body, name=name,
        grid_spec=pltpu.PrefetchScalarGridSpec(num_scalar_prefetch=1, grid=(tiles,), in_specs=in_specs,
                                               out_specs=out_specs),
        out_shape=[jax.ShapeDtypeStruct(s, F32) for s in shapes for _ in range(4)],
        compiler_params=_params(("arbitrary",)),
    )(qidx, *[a for (w, m, v), c, rc in zip(wmv, cs, rcv) for a in (w, m, v, c, rc)])
    return [tuple(out[4 * k:4 * k + 4]) for k in range(n)]


SMALL_PARAMS = ("norm_ffn1", "norm_mix", "norm_ffn2", "norm_final", "q_norm", "kv_norm", "sinks", "rel_bias", "b_mod")


def _adamw_small(gvec, wmv):
    shapes = [wmv[3 * i].shape for i in range(len(SMALL_PARAMS))]

    def body(*refs):
        g_all = refs[0]
        ins = refs[1:1 + 3 * len(SMALL_PARAMS)]
        outs = refs[1 + 3 * len(SMALL_PARAMS):]
        off = N_MODVEC
        for i, name in enumerate(SMALL_PARAMS):
            g_ref, d_ref, nm_ref, nv_ref = outs[4 * i:4 * i + 4]
            w_ref, m_ref, v_ref = ins[3 * i:3 * i + 3]
            start = 0 if name == "b_mod" else off
            rows, width = shapes[i]
            g = jnp.concatenate([g_all[:, start + width * r:start + width * (r + 1)] for r in range(rows)], axis=0)
            g_ref[...] = g
            d_ref[...], nm_ref[...], nv_ref[...] = _adam_math(w_ref[...], g, m_ref[...], v_ref[...])
            if name != "b_mod":
                off += dict(SMALL_LAYOUT)[name]

    vm = pl.BlockSpec(memory_space=pltpu.VMEM)
    n_out = 4 * len(SMALL_PARAMS)
    out = pl.pallas_call(
        body, name="adamw_small", in_specs=[vm] * (1 + len(wmv)), out_specs=[vm] * n_out,
        out_shape=[jax.ShapeDtypeStruct(shapes[i // 4], F32) for i in range(n_out)],
        compiler_params=_params(),
    )(gvec, *wmv)
    return {name: out[4 * i:4 * i + 4] for i, name in enumerate(SMALL_PARAMS)}


TRANSPOSED = ("g1T", "u1T", "g3T", "u3T", "w_inT", "w_uqT")


def kernel(x, c, w_mod, b_mod, norm_ffn1, ffn1_gate, ffn1_up, ffn1_down, norm_mix, w_in, q_norm, kv_norm, w_uq, w_ukv, sinks, w_o, norm_ffn2, ffn2_gate, ffn2_up, ffn2_down, rel_bias, norm_final, loss_target, m_w_mod, m_b_mod, m_norm_ffn1, m_ffn1_gate, m_ffn1_up, m_ffn1_down, m_norm_mix, m_w_in, m_q_norm, m_kv_norm, m_w_uq, m_w_ukv, m_sinks, m_w_o, m_norm_ffn2, m_ffn2_gate, m_ffn2_up, m_ffn2_down, m_rel_bias, m_norm_final, v_w_mod, v_b_mod, v_norm_ffn1, v_ffn1_gate, v_ffn1_up, v_ffn1_down, v_norm_mix, v_w_in, v_q_norm, v_kv_norm, v_w_uq, v_w_ukv, v_sinks, v_w_o, v_norm_ffn2, v_ffn2_gate, v_ffn2_up, v_ffn2_down, v_rel_bias, v_norm_final):
    mx, my, mc = _coords()
    cidx = jnp.reshape(mc, (1,)).astype(jnp.int32)
    qidx = jnp.reshape(2 * mx + my, (1,)).astype(jnp.int32)
    WM = w_mod.shape[2]

    shards = {"g1T": ffn1_gate[0].T.astype(BF16), "u1T": ffn1_up[0].T.astype(BF16), "d1": ffn1_down[0].astype(BF16),
              "g3T": ffn2_gate[0].T.astype(BF16), "u3T": ffn2_up[0].T.astype(BF16), "d3": ffn2_down[0].astype(BF16),
              "w_inT": w_in[0].T, "w_uqT": w_uq[0].T.astype(BF16), "w_ukv": w_ukv[0].astype(BF16),
              "w_o": w_o[0].astype(BF16)}
    me = 4 * mx + 2 * my + mc
    groups = {"ffn1": ("g1T", "u1T", "d1"), "mixer": ("w_inT", "w_uqT", "w_ukv", "w_o"), "ffn2": ("g3T", "u3T", "d3")}
    arriving = {}

    c_tile = jnp.pad(c, ((0, 7), (0, 0)))
    b_mod3 = jnp.pad(b_mod.reshape(N_DEV, 1, WM), ((0, 0), (0, 7), (0, 0)))
    *gathered_ffn1, gathered_token, mod3, ca = _wgather([shards[k] for k in groups["ffn1"]], c_tile, w_mod[0], b_mod3)
    mod9 = mod3[:, 0, :].reshape(N_MOD, D)

    def as_weights(group, gathered):
        return {k: g if k == "w_ukv" else g.reshape(N_DEV * g.shape[1], g.shape[2])
                for k, g in zip(groups[group], gathered)}

    later = groups["mixer"] + groups["ffn2"]
    place = {"mixer": 0, "ffn2": len(groups["mixer"])}

    def start_gather(token, carry):
        lands = []
        for k in later:
            sh = shards[k] + token[0, 0].astype(shards[k].dtype)
            lands.append(lax.dynamic_update_slice(lax.empty((N_DEV,) + sh.shape, sh.dtype), sh[None], (me, 0, 0)))
        batches = [range(k0, k0 + len(groups[group])) for group, k0 in place.items()]
        send, recv, lands, started = _gather_start(lands, name="gather_start", batches=batches, carry=carry)
        for group, k0 in place.items():
            arriving[group] = (send, recv, lands[k0:k0 + len(groups[group])])
        return started

    def fetch(group, after, vecs):
        if group == "ffn1":
            return as_weights("ffn1", gathered_ffn1), start_gather(gathered_token, vecs)

        def pass_on(group, after, carry=None):
            send, recv, lands = arriving[group]
            lands, token = _gather_pass(send, recv, lands, after, name="gather_landed_" + group, stage="landed",
                                        k0=place[group])
            lands, token = _gather_pass(send, recv, lands, [token], name="gather_onward_" + group, stage="onward",
                                        k0=place[group], carry=carry)
            arriving[group] = (send, recv, lands)
            return token

        if group == "ffn2_on_its_way":
            return None, pass_on("ffn2", after, vecs)
        if group == "mixer":
            after = [pass_on("mixer", after)]
        send, recv, lands = arriving[group]
        return as_weights(group, _gather_end(send, recv, lands, after, name="gather_end_" + group,
                                             k0=place[group])), vecs

    norms ={"ffn1": norm_ffn1, "mix": norm_mix, "ffn2": norm_ffn2, "final": norm_final.reshape(1, D)}
    in_flight = {}

    def on_grads(group, g, after, vecs):
        if group != "ffn1":
            if g is None:
                return vecs
            names = list(g)
            by_dest = [g[k] if k == "w_ukv" else g[k].reshape((N_DEV, g[k].shape[0] // N_DEV) + g[k].shape[1:])
                       for k in names]
            lands = [lax.empty((N_DEV - 1,) + a.shape[1:], a.dtype) for a in by_dest]
            send, recv, by_dest, lands, token = _split_start(_direct_copies, by_dest, lands, 7 * len(names), after,
                                                             name="rs_start_" + group, carry=vecs)
            in_flight[group] = (names, send, recv, by_dest, lands, token)
            return token
        names = list(g)
        by_dest = [g[k].reshape((N_DEV, g[k].shape[0] // N_DEV) + g[k].shape[1:]) for k in names]
        lands = [lax.empty((4,) + a.shape[1:], a.dtype) for a in by_dest]
        send, recv, by_dest, lands, token = _split_start(_d2d_copies, by_dest, lands, 4 * len(names), after,
                                                         name="rs_d2d_start_" + group)
        finish("mixer", [token])
        by_dest, from_sib = _split_wait(_d2d_copies, send, recv, by_dest, lands, [done[-1]], name="rs_d2d_wait_" + group)
        sums = _chipsum(by_dest, from_sib, cidx, name="chipsum_" + group)
        halves = lambda: [lax.empty((2, s.shape[1] // 2) + s.shape[2:], s.dtype) for s in sums]
        send, recv, sums, lands, token = _split_start(_rs_step1_copies, sums, halves() + halves(), 4 * len(names), [],
                                                      name="rs_ici_start_" + group, carry=vecs)
        in_flight[group] = (names, send, recv, sums, lands, token)
        return token

    owners = {"g1T": ("ffn1_gate", ffn1_gate, m_ffn1_gate, v_ffn1_gate), "u1T": ("ffn1_up", ffn1_up, m_ffn1_up, v_ffn1_up),
              "d1": ("ffn1_down", ffn1_down, m_ffn1_down, v_ffn1_down),
              "g3T": ("ffn2_gate", ffn2_gate, m_ffn2_gate, v_ffn2_gate), "u3T": ("ffn2_up", ffn2_up, m_ffn2_up, v_ffn2_up),
              "d3": ("ffn2_down", ffn2_down, m_ffn2_down, v_ffn2_down),
              "w_inT": ("w_in", w_in, m_w_in, v_w_in), "w_uqT": ("w_uq", w_uq, m_w_uq, v_w_uq),
              "w_ukv": ("w_ukv", w_ukv, m_w_ukv, v_w_ukv), "w_o": ("w_o", w_o, m_w_o, v_w_o)}
    res, done = {}, []

    there = lambda k, a: a[0].T if k in TRANSPOSED else a[0]
    back = lambda k, a: a.T[None] if k in TRANSPOSED else a[None]

    def record(names, outs):
        for k, out in zip(names, outs):
            done.append(out[3])
            res[owners[k][0]] = tuple(back(k, a) for a in out)

    def finish(group, after):
        names, send, recv, sums, lands, _ = in_flight[group]
        wmv = [tuple(there(k, a) for a in owners[k][1:]) for k in names]
        own = jnp.reshape(me, (1,)).astype(jnp.int32)
        sums, lands = _split_wait(_direct_copies, send, recv, sums, lands, after, name="rs_wait_" + group)
        record(names, _adamw_rs(wmv, sums, lands, own, name="adamw_" + group))

    _, grad_x, _, vec = _local_step(
        x[0], loss_target[0], mod9, norms, sinks, rel_bias, q_norm, kv_norm, fetch, on_grads=on_grads)

    names, send, recv, sums, lands, step1_started = in_flight["ffn1"]
    vec = vec.reshape(1, 1, N_VEC)
    allvec = lax.dynamic_update_slice(lax.empty((N_DEV, 1, N_VEC), F32), vec, (me, 0, 0))
    vsend, vrecv, _, (allvec,), vec_started = _split_start(_vec_copies, [], [allvec], N_DEV - 1, [step1_started],
                                                           name="vec_start")
    finish("ffn2", [vec_started])
    n = len(names)
    sums, lands = _split_wait(_rs_step1_copies, send, recv, sums, lands, [done[-1]], name="rs_ici_wait_ffn1")
    direct, relay = lands[:n], lands[n:]
    qxy = jnp.stack([2 * (1 - mx) + my, 2 * mx + (1 - my)]).astype(jnp.int32)
    relayed = _relay_sum(sums, relay, qxy, name="relay_sum_ffn1")
    second = [lax.empty(a.shape, a.dtype) for a in relayed]
    send, recv, relayed, second, step2_started = _split_start(_rs_step2_copies, relayed, second, 2 * n, [],
                                                              name="rs_ici_start2_ffn1")

    _, (allvec,) = _split_wait(_vec_copies, vsend, vrecv, [], [allvec], [step2_started], name="vec_wait")
    g_wmod, gvec = _mod_bwd(allvec, ca, jnp.reshape(me, (1,)).astype(jnp.int32))
    loss = gvec[0, N_MODVEC + LOSS_SLOT]
    small_in = {"norm_ffn1": (norm_ffn1, m_norm_ffn1, v_norm_ffn1), "norm_mix": (norm_mix, m_norm_mix, v_norm_mix),
                "norm_ffn2": (norm_ffn2, m_norm_ffn2, v_norm_ffn2), "norm_final": (norm_final, m_norm_final, v_norm_final),
                "q_norm": (q_norm, m_q_norm, v_q_norm), "kv_norm": (kv_norm, m_kv_norm, v_kv_norm),
                "sinks": (sinks, m_sinks, v_sinks), "rel_bias": (rel_bias, m_rel_bias, v_rel_bias),
                "b_mod": (b_mod, m_b_mod, v_b_mod)}
    as_row = lambda k, a: a.T if k == "rel_bias" else a.reshape(1, -1)
    from_row = lambda k, a: a.T if k == "rel_bias" else a.reshape(small_in[k][0].shape)
    small_out = _adamw_small(gvec, [as_row(k, a) for k in SMALL_PARAMS for a in small_in[k]])
    for k in SMALL_PARAMS:
        res[k] = tuple(from_row(k, a) for a in small_out[k])

    out = _adamw(w_mod[0], g_wmod, m_w_mod[0], v_w_mod[0], name="adamw_w_mod")
    res["w_mod"] = tuple(a[None] for a in (g_wmod,) + tuple(out))

    wmv = [tuple(there(k, a) for a in owners[k][1:]) for k in names]
    after = done + [out[2]] + [a for k in SMALL_PARAMS for a in res[k]]
    outs = []
    for i, k in enumerate(names):
        _, (sec,) = _split_wait(functools.partial(_rs_step2_copies, k0=i), send, recv, [relayed[i]], [second[i]],
                                after, name="rs_ici_wait2_" + k)
        outs.append(_adamw_rs2([wmv[i]], [sums[i]], [direct[i]], [sec], qidx, name="adamw_" + owners[k][0])[0])
        after = [outs[-1][3]]
    record(names, outs)

    order = ("w_mod", "b_mod", "norm_ffn1", "ffn1_gate", "ffn1_up", "ffn1_down", "norm_mix", "w_in", "q_norm",
             "kv_norm", "w_uq", "w_ukv", "sinks", "w_o", "norm_ffn2", "ffn2_gate", "ffn2_up", "ffn2_down",
             "rel_bias", "norm_final")
    return (loss, grad_x[None]) + tuple(res[nm][kind] for kind in range(4) for nm in order)
```

```python
import functools
import math

import numpy as np
import jax
import jax.numpy as jnp
from jax import lax
from jax.experimental import pallas as pl
from jax.experimental.pallas import tpu as pltpu

F32 = jnp.float32
BF16 = jnp.bfloat16
MESH = pl.DeviceIdType.MESH

N_DEV = 8
D = 1024
D_FF = 2816
EPS = 1e-6
N_MOD = 9
SWA_HEADS = 8
SWA_DH = 64
WINDOW = 128
MLA_HEADS = 4
MLA_NOPE = 128
MLA_ROPE = 64
MLA_V = 128
MLA_QR = 256
MLA_KVR = 128
ROPE_THETA = 10000.0
NUM_BUCKETS = 32
D_IN = 1216
D_IN_PAD = 1280
SWA_SCALE = SWA_DH ** -0.5
MLA_SCALE = (MLA_NOPE + MLA_ROPE) ** -0.5

ADAM_LR = 0.001
ADAM_B1 = 0.9
ADAM_B2 = 0.999
ADAM_EPS = 1e-08
ADAM_WD = 0.01
ADAM_STEP = 10

V7X_VMEM_LIMIT = 56 * 1024 * 1024
ROW_TILE = 512

NT_DIMS = (((1,), (1,)), ((), ()))
TN_DIMS = (((0,), (0,)), ((), ()))


def _dot(a, b):
    return jnp.dot(a, b, preferred_element_type=F32)


def _dot_nt(a, b):
    return lax.dot_general(a, b, NT_DIMS, preferred_element_type=F32)


def _dot_tn(a, b):
    return lax.dot_general(a, b, TN_DIMS, preferred_element_type=F32)


def _params(sem=None):
    return pltpu.CompilerParams(dimension_semantics=sem, vmem_limit_bytes=V7X_VMEM_LIMIT)


def _rstd(x):
    return lax.rsqrt(jnp.mean(x * x, axis=-1, keepdims=True) + EPS)


def _rms_bwd(dy, xhat, r):
    return r * (dy - xhat * jnp.mean(dy * xhat, axis=-1, keepdims=True))


def _sigmoid(a):
    return 1.0 / (1.0 + jnp.exp(-a))


def _ffn_fwd(x, vecs, wgT, wuT, wd, *, name, tm=256, tf=D_FF):
    S, F = x.shape[0], wd.shape[0]
    tm = min(tm, S)
    ni, nj = S // tm, F // tf

    def body(x_ref, vec_ref, wg_ref, wu_ref, wd_ref, xo_ref, h_ref, a_ref, b_ref, f_ref, acc_ref):
        j = pl.program_id(1)

        @pl.when(j == 0)
        def _():
            xv = x_ref[...]
            hn = xv * _rstd(xv) * vec_ref[0:1, :]
            h_ref[...] = (hn * (1.0 + vec_ref[2:3, :]) + vec_ref[1:2, :]).astype(BF16)

        h = h_ref[...]
        a = _dot_nt(h, wg_ref[...])
        b = _dot_nt(h, wu_ref[...])
        a_ref[...] = a.astype(BF16)
        b_ref[...] = b.astype(BF16)
        part = _dot((a * _sigmoid(a) * b).astype(BF16), wd_ref[...])

        def finish(f):
            f_ref[...] = f
            xo_ref[...] = x_ref[...] + (0.5 * vec_ref[3:4, :]) * f

        if nj == 1:
            finish(part)
        else:
            @pl.when(j == 0)
            def _():
                acc_ref[...] = part

            @pl.when((j > 0) & (j < nj - 1))
            def _():
                acc_ref[...] += part

            @pl.when(j == nj - 1)
            def _():
                finish(acc_ref[...] + part)

    row = pl.BlockSpec((tm, D), lambda i, j: (i, 0))
    wspec = pl.BlockSpec((tf, D), lambda i, j: (j, 0), pipeline_mode=pl.Buffered(1) if nj == 1 else None)
    act = pl.BlockSpec((tm, tf), lambda i, j: (i, j))
    return pl.pallas_call(
        body, name=name, grid=(ni, nj),
        in_specs=[row, pl.BlockSpec((8, D), lambda i, j: (0, 0)), wspec, wspec, wspec],
        out_specs=[row, row, act, act, row],
        out_shape=[jax.ShapeDtypeStruct((S, D), F32), jax.ShapeDtypeStruct((S, D), BF16),
                   jax.ShapeDtypeStruct((S, F), BF16), jax.ShapeDtypeStruct((S, F), BF16),
                   jax.ShapeDtypeStruct((S, D), F32)],
        scratch_shapes=[pltpu.VMEM((tm, D) if nj > 1 else (8, 128), F32)],
        compiler_params=_params(("parallel", "arbitrary")),
    )(x, vecs, wgT, wuT, wd)


def _ffn_bwd_main(h, df, a, b, wgT, wuT, wd, *, name, after=(), tm=2048, tf=256):
    S = h.shape[0]
    tm = min(tm, S)
    ni, nj = S // tm, D_FF // tf

    def body(h_hbm, df_hbm, a_ref, b_ref, wg_ref, wu_ref, wd_ref, *rest):
        gg_ref, gu_ref, gd_ref, dh_hbm, h_v, df_v, dh_v, gg_acc, gu_acc, gd_acc, sem = rest[len(after):]
        j = pl.program_id(0)
        i = pl.program_id(1)

        @pl.when((j == 0) & (i == 0))
        def _():
            c1 = pltpu.make_async_copy(h_hbm, h_v, sem.at[0])
            c2 = pltpu.make_async_copy(df_hbm, df_v, sem.at[1])
            c1.start()
            c2.start()
            c1.wait()
            c2.wait()

        @pl.when(i == 0)
        def _():
            gg_acc[...] = jnp.zeros_like(gg_acc)
            gu_acc[...] = jnp.zeros_like(gu_acc)
            gd_acc[...] = jnp.zeros_like(gd_acc)

        rows = pl.ds(pl.multiple_of(i * tm, tm), tm)
        hi = h_v[rows, :]
        dfi = df_v[rows, :]
        av = a_ref[...].astype(F32)
        bv = b_ref[...].astype(F32)
        sg = _sigmoid(av)
        sa = av * sg
        hsw = (sa * bv).astype(BF16)
        dhsw = _dot_nt(dfi, wd_ref[...])
        da = (dhsw * bv * (sg * (1.0 + av * (1.0 - sg)))).astype(BF16)
        db = (dhsw * sa).astype(BF16)
        gd_acc[...] += _dot_tn(hsw, dfi)
        gg_acc[...] += _dot_tn(da, hi)
        gu_acc[...] += _dot_tn(db, hi)
        dh = _dot(da, wg_ref[...]) + _dot(db, wu_ref[...])

        @pl.when(j == 0)
        def _():
            dh_v[rows, :] = dh

        @pl.when(j > 0)
        def _():
            dh_v[rows, :] += dh

        @pl.when(i == ni - 1)
        def _():
            gg_ref[...] = gg_acc[...].astype(BF16)
            gu_ref[...] = gu_acc[...].astype(BF16)
            gd_ref[...] = gd_acc[...].astype(BF16)

        @pl.when((j == nj - 1) & (i == ni - 1))
        def _():
            c3 = pltpu.make_async_copy(dh_v, dh_hbm, sem.at[2])
            c3.start()
            c3.wait()

    anyspec = pl.BlockSpec(memory_space=pl.ANY)
    wspec = pl.BlockSpec((tf, D), lambda j, i: (j, 0))
    act = pl.BlockSpec((tm, tf), lambda j, i: (i, j))
    return pl.pallas_call(
        body, name=name, grid=(nj, ni),
        in_specs=[anyspec, anyspec, act, act, wspec, wspec, wspec] + [anyspec] * len(after),
        out_specs=[wspec, wspec, wspec, anyspec],
        out_shape=[jax.ShapeDtypeStruct((D_FF, D), BF16)] * 3 + [jax.ShapeDtypeStruct((S, D), F32)],
        scratch_shapes=[pltpu.VMEM((S, D), BF16), pltpu.VMEM((S, D), BF16), pltpu.VMEM((S, D), F32),
                        pltpu.VMEM((tf, D), F32), pltpu.VMEM((tf, D), F32), pltpu.VMEM((tf, D), F32),
                        pltpu.SemaphoreType.DMA((3,))],
        compiler_params=_params(("arbitrary", "arbitrary")),
    )(h, df, a, b, wgT, wuT, wd, *after)


def _ffn_out_bwd(dx, f, gate, df_ref, part_ref):
    df_ref[...] = ((0.5 * gate) * dx).astype(BF16)
    part_ref[3:4, :] += 0.5 * jnp.sum(dx * f, axis=0, keepdims=True)


def _norm_bwd(dh, x, dxo, vecs, *, name, below=None, tm=ROW_TILE):
    S = x.shape[0]
    tm = min(tm, S)

    def body(dh_ref, x_ref, dxo_ref, vec_ref, *rest):
        dx_ref, part_ref = rest[-2 if below is None else -3], rest[-1 if below is None else -2]

        @pl.when(pl.program_id(0) == 0)
        def _():
            part_ref[...] = jnp.zeros_like(part_ref)

        dh = dh_ref[...]
        xv = x_ref[...]
        r = _rstd(xv)
        xhat = xv * r
        w = vec_ref[0:1, :]
        xn = xhat * w
        dxn = dh * (1.0 + vec_ref[2:3, :])
        part_ref[0:1, :] += jnp.sum(dxn * xhat, axis=0, keepdims=True)
        part_ref[1:2, :] += jnp.sum(dh, axis=0, keepdims=True)
        part_ref[2:3, :] += jnp.sum(dh * xn, axis=0, keepdims=True)
        dx = dxo_ref[...] + _rms_bwd(dxn * w, xhat, r)
        dx_ref[...] = dx
        if below is not None:
            _ffn_out_bwd(dx, rest[0][...], rest[1][3:4, :], rest[-1], part_ref)

    row = pl.BlockSpec((tm, D), lambda i: (i, 0))
    vec = pl.BlockSpec((8, D), lambda i: (0, 0))
    extra = [] if below is None else [row, vec]
    return pl.pallas_call(
        body, name=name, grid=(S // tm,), in_specs=[row, row, row, vec] + extra,
        out_specs=[row, vec] + ([] if below is None else [row]),
        out_shape=[jax.ShapeDtypeStruct((S, D), F32), jax.ShapeDtypeStruct((8, D), F32)]
        + ([] if below is None else [jax.ShapeDtypeStruct((S, D), BF16)]),
        compiler_params=_params(("arbitrary",)),
    )(dh, x, dxo, vecs, *([] if below is None else below))


def _head(x, tgt, nf, f, vecs, *, tm=ROW_TILE):
    S = x.shape[0]
    tm = min(tm, S)

    def body(x_ref, t_ref, nf_ref, f_ref, vec_ref, dx_ref, part_ref, df_ref):
        @pl.when(pl.program_id(0) == 0)
        def _():
            part_ref[...] = jnp.zeros_like(part_ref)

        xv = x_ref[...]
        r = _rstd(xv)
        xhat = xv * r
        w = nf_ref[...]
        e = xhat * w - t_ref[...]
        dy = e * (1.0 / D)
        part_ref[0:1, :] += jnp.sum(dy * xhat, axis=0, keepdims=True)
        part_ref[1:2, :] += jnp.sum(e * e) * (0.5 / D)
        dx = _rms_bwd(dy * w, xhat, r)
        dx_ref[...] = dx
        _ffn_out_bwd(dx, f_ref[...], vec_ref[3:4, :], df_ref, part_ref)

    row = pl.BlockSpec((tm, D), lambda i: (i, 0))
    vec = pl.BlockSpec((8, D), lambda i: (0, 0))
    return pl.pallas_call(
        body, name="head", grid=(S // tm,),
        in_specs=[row, row, pl.BlockSpec((1, D), lambda i: (0, 0)), row, vec],
        out_specs=[row, vec, row],
        out_shape=[jax.ShapeDtypeStruct((S, D), F32), jax.ShapeDtypeStruct((8, D), F32),
                   jax.ShapeDtypeStruct((S, D), BF16)],
        compiler_params=_params(("arbitrary",)),
    )(x, tgt, nf, f, vecs)


def _mix_in_fwd(x, vecs, w_inT, *, tm=ROW_TILE):
    S = x.shape[0]
    tm = min(tm, S)

    def body(x_ref, vec_ref, w_ref, h_ref, p_ref, wb_ref):
        @pl.when(pl.program_id(0) == 0)
        def _():
            wb_ref[0:D_IN, :] = w_ref[...].astype(BF16)
            wb_ref[D_IN:D_IN_PAD, :] = jnp.zeros((D_IN_PAD - D_IN, D), BF16)

        xv = x_ref[...]
        hn = xv * _rstd(xv) * vec_ref[0:1, :]
        h = (hn * (1.0 + vec_ref[2:3, :]) + vec_ref[1:2, :]).astype(BF16)
        h_ref[...] = h
        p_ref[...] = _dot_nt(h, wb_ref[...])

    row = pl.BlockSpec((tm, D), lambda i: (i, 0))
    return pl.pallas_call(
        body, name="mix_in_fwd", grid=(S // tm,),
        in_specs=[row, pl.BlockSpec((8, D), lambda i: (0, 0)),
                  pl.BlockSpec((D_IN, D), lambda i: (0, 0), pipeline_mode=pl.Buffered(1))],
        out_specs=[row, pl.BlockSpec((tm, D_IN_PAD), lambda i: (i, 0)), pl.BlockSpec((D_IN_PAD, D), lambda i: (0, 0))],
        out_shape=[jax.ShapeDtypeStruct((S, D), BF16), jax.ShapeDtypeStruct((S, D_IN_PAD), F32),
                   jax.ShapeDtypeStruct((D_IN_PAD, D), BF16)],
        compiler_params=_params(("arbitrary",)),
    )(x, vecs, w_inT)


def _bucket_table():
    qi = np.arange(WINDOW)[:, None]
    kj = np.arange(2 * WINDOW)[None, :]
    dist = qi + WINDOW - kj
    max_exact = NUM_BUCKETS // 2
    n = np.maximum(dist, 0)
    nf = np.maximum(n, 1).astype(np.float32)
    large = max_exact + (np.log(nf / np.float32(max_exact)) / np.float32(math.log(WINDOW / max_exact))
                         * np.float32(NUM_BUCKETS - max_exact)).astype(np.int32)
    large = np.minimum(large, NUM_BUCKETS - 1)
    return np.where(n < max_exact, n, large).astype(np.int32)


SWA_GROUP = 4
GROUP_ROWS = SWA_GROUP * WINDOW


SWA_SUB = 2


def _swa_valid(has_prev):
    row = lax.broadcasted_iota(jnp.int32, (GROUP_ROWS, 2 * WINDOW), 0) % WINDOW
    col = lax.broadcasted_iota(jnp.int32, (GROUP_ROWS, 2 * WINDOW), 1)
    dist = row + WINDOW - col
    return (dist >= 0) & (dist < WINDOW) & ((col >= WINDOW) | has_prev)


def _swa_keys(prev_ref, cur_ref, u):
    cur = cur_ref[...]
    before = prev_ref[...] if u == 0 else cur[WINDOW * (u - 1):WINDOW * u]
    return jnp.concatenate([before, cur[WINDOW * u:WINDOW * (u + 1)]], axis=0).astype(BF16)


def _stack_heads(x, g):
    return jnp.concatenate([x[:, 64 * h:64 * h + 64] for h in range(SWA_GROUP * g, SWA_GROUP * (g + 1))], axis=0)


def _unstack_heads(x4):
    return jnp.concatenate([x4[WINDOW * a:WINDOW * (a + 1)] for a in range(SWA_GROUP)], axis=1)


def _group_sinks(sink_ref, g):
    head = lax.broadcasted_iota(jnp.int32, (GROUP_ROWS, 1), 0) // WINDOW
    out = jnp.full((GROUP_ROWS, 1), sink_ref[0, SWA_GROUP * g], F32)
    for a in range(1, SWA_GROUP):
        out = jnp.where(head == a, sink_ref[0, SWA_GROUP * g + a], out)
    return out


def _swa_probs(qh, kk, bias_h, sink, valid):
    s = _dot_nt(qh, kk) * SWA_SCALE + bias_h
    s = jnp.where(valid, s, -jnp.inf)
    m = jnp.maximum(jnp.max(s, axis=-1, keepdims=True), sink)
    p = jnp.exp(s - m)
    ps = jnp.exp(sink - m)
    inv = 1.0 / (jnp.sum(p, axis=-1, keepdims=True) + ps)
    return p * inv, ps * inv


SWA_ROWS = SWA_SUB * WINDOW


def _swa_specs():
    prev = lambda n: jnp.maximum(SWA_SUB * n - 1, 0)
    return [pl.BlockSpec((SWA_ROWS, 512), lambda n: (n, 0)),
            pl.BlockSpec((SWA_ROWS, 128), lambda n: (n, 4)),
            pl.BlockSpec((WINDOW, 128), lambda n: (prev(n), 4)),
            pl.BlockSpec((SWA_ROWS, 128), lambda n: (n, 5)),
            pl.BlockSpec((WINDOW, 128), lambda n: (prev(n), 5)),
            pl.BlockSpec((SWA_HEADS, WINDOW, 2 * WINDOW), lambda n: (0, 0, 0)),
            pl.BlockSpec(memory_space=pltpu.SMEM)]


def _swa_fwd(proj, rel_bias, bucket, sinks):
    S = proj.shape[0]

    def body(q_ref, kc_ref, kp_ref, vc_ref, vp_ref, rb_ref, sink_ref, bk_ref, o_ref, bias_ref):
        n = pl.program_id(0)

        @pl.when(n == 0)
        def _():
            bk = bk_ref[...]
            for h in range(SWA_HEADS):
                acc = jnp.zeros((WINDOW, 2 * WINDOW), F32)
                for b in range(NUM_BUCKETS):
                    acc = jnp.where(bk == b, rb_ref[b, h], acc)
                bias_ref[h] = acc

        for u in range(SWA_SUB):
            rows = slice(WINDOW * u, WINDOW * (u + 1))
            valid = _swa_valid(n > 0 if u == 0 else True)
            q = q_ref[rows, :].astype(BF16)
            kfull = _swa_keys(kp_ref, kc_ref, u)
            vfull = _swa_keys(vp_ref, vc_ref, u)
            for g in range(SWA_HEADS // SWA_GROUP):
                kk = kfull[:, 64 * g:64 * g + 64]
                vv = vfull[:, 64 * g:64 * g + 64]
                bias4 = bias_ref[SWA_GROUP * g:SWA_GROUP * (g + 1)].reshape(GROUP_ROWS, 2 * WINDOW)
                pk, _ = _swa_probs(_stack_heads(q, g), kk, bias4, _group_sinks(sink_ref, g), valid)
                o_ref[rows, 256 * g:256 * (g + 1)] = _unstack_heads(_dot(pk.astype(BF16), vv))

    specs = _swa_specs()
    whole = pl.BlockSpec((SWA_HEADS, WINDOW, 2 * WINDOW), lambda n: (0, 0, 0))
    return pl.pallas_call(
        body, name="swa_fwd", grid=(S // SWA_ROWS,),
        in_specs=specs[:5] + [pl.BlockSpec(memory_space=pltpu.SMEM), specs[6],
                              pl.BlockSpec((WINDOW, 2 * WINDOW), lambda n: (0, 0))],
        out_specs=[pl.BlockSpec((SWA_ROWS, 512), lambda n: (n, 0)), whole],
        out_shape=[jax.ShapeDtypeStruct((S, 512), F32), jax.ShapeDtypeStruct((SWA_HEADS, WINDOW, 2 * WINDOW), F32)],
        compiler_params=_params(("arbitrary",)),
    )(proj, proj, proj, proj, proj, rel_bias, sinks, bucket)


def _swa_bwd(proj, bias, sinks, o, do, bucket):
    S = proj.shape[0]
    nb = S // SWA_ROWS

    def body(q_ref, kc_ref, kp_ref, vc_ref, vp_ref, bias_ref, sink_ref, o_ref, do_ref, bk_ref,
             dq_ref, dk_ref, dv_ref, drb_ref, dsk_ref, dbias_acc):
        n = pl.program_id(0)

        @pl.when(n == 0)
        def _():
            dk_ref[...] = jnp.zeros_like(dk_ref)
            dv_ref[...] = jnp.zeros_like(dv_ref)
            dsk_ref[...] = jnp.zeros_like(dsk_ref)
            dbias_acc[...] = jnp.zeros_like(dbias_acc)
            drb_ref[...] = jnp.zeros_like(drb_ref)

        for u in range(SWA_SUB):
            rows = slice(WINDOW * u, WINDOW * (u + 1))
            blk = SWA_SUB * n + u
            valid = _swa_valid(n > 0 if u == 0 else True)
            q = q_ref[rows, :].astype(BF16)
            dov = do_ref[rows, :]
            ov = o_ref[rows, :]
            kfull = _swa_keys(kp_ref, kc_ref, u)
            vfull = _swa_keys(vp_ref, vc_ref, u)
            prow = pl.ds(pl.multiple_of(jnp.maximum(blk - 1, 0) * WINDOW, WINDOW), WINDOW)
            crow = pl.ds(pl.multiple_of(blk * WINDOW, WINDOW), WINDOW)
            for g in range(SWA_HEADS // SWA_GROUP):
                heads = slice(SWA_GROUP * g, SWA_GROUP * (g + 1))
                kk = kfull[:, 64 * g:64 * g + 64]
                vv = vfull[:, 64 * g:64 * g + 64]
                q4 = _stack_heads(q, g)
                pk, psink = _swa_probs(q4, kk, bias_ref[heads].reshape(GROUP_ROWS, 2 * WINDOW),
                                       _group_sinks(sink_ref, g), valid)
                pkb = pk.astype(BF16)
                do4 = _stack_heads(dov, g)
                dob = do4.astype(BF16)
                dp = _dot_nt(dob, vv)
                delta = jnp.sum(do4 * _stack_heads(ov, g), axis=-1, keepdims=True)
                ds = pk * (dp - delta)
                dsink = -psink * delta
                for a in range(SWA_GROUP):
                    h = SWA_GROUP * g + a
                    part = jnp.sum(dsink[WINDOW * a:WINDOW * (a + 1)], keepdims=True)
                    dsk_ref[h:h + 1, :] += jnp.broadcast_to(part, (1, 128))
                dbias_acc[heads] += ds.reshape(SWA_GROUP, WINDOW, 2 * WINDOW)
                dsb = (ds * SWA_SCALE).astype(BF16)
                dq_ref[rows, 256 * g:256 * (g + 1)] = _unstack_heads(_dot(dsb, kk))
                dkk = _dot_tn(dsb, q4)
                dvv = _dot_tn(pkb, dob)
                dk_ref[prow, 64 * g:64 * g + 64] += dkk[:WINDOW]
                dk_ref[crow, 64 * g:64 * g + 64] += dkk[WINDOW:]
                dv_ref[prow, 64 * g:64 * g + 64] += dvv[:WINDOW]
                dv_ref[crow, 64 * g:64 * g + 64] += dvv[WINDOW:]

        @pl.when(n == nb - 1)
        def _():
            bk = bk_ref[...]
            for h in range(SWA_HEADS):
                dbh = dbias_acc[h]
                for b in range(NUM_BUCKETS):
                    val = jnp.sum(jnp.where(bk == b, dbh, 0.0), keepdims=True)
                    row = h * NUM_BUCKETS + b
                    drb_ref[row:row + 1, :] = jnp.broadcast_to(val, (1, 128))

    full = lambda shape: pl.BlockSpec(shape, lambda n: tuple(0 for _ in shape))
    return pl.pallas_call(
        body, name="swa_bwd", grid=(nb,),
        in_specs=_swa_specs() + [pl.BlockSpec((SWA_ROWS, 512), lambda n: (n, 0)),
                                 pl.BlockSpec((SWA_ROWS, 512), lambda n: (n, 0)), full((WINDOW, 2 * WINDOW))],
        out_specs=[pl.BlockSpec((SWA_ROWS, 512), lambda n: (n, 0)), full((S, 128)), full((S, 128)),
                   full((NUM_BUCKETS * 8, 128)), full((8, 128))],
        out_shape=[jax.ShapeDtypeStruct((S, 512), F32), jax.ShapeDtypeStruct((S, 128), F32),
                   jax.ShapeDtypeStruct((S, 128), F32), jax.ShapeDtypeStruct((NUM_BUCKETS * 8, 128), F32),
                   jax.ShapeDtypeStruct((8, 128), F32)],
        scratch_shapes=[pltpu.VMEM((SWA_HEADS, WINDOW, 2 * WINDOW), F32)],
        compiler_params=_params(("arbitrary",)),
    )(proj, proj, proj, proj, proj, bias, sinks, o, do, bucket)


def _rope_tables(S):
    inv = np.float32(ROPE_THETA) ** (-np.arange(0, MLA_ROPE, 2, dtype=np.float32) / np.float32(MLA_ROPE))
    ang = np.arange(S, dtype=np.float32)[:, None] * inv[None, :]
    cos, sin = np.cos(ang), np.sin(ang)
    return (jnp.asarray(np.tile(np.concatenate([cos, cos], axis=1), (1, 2))),
            jnp.asarray(np.tile(np.concatenate([-sin, sin], axis=1), (1, 2))))


def _rope_wide(ref):
    t = ref[...]
    return jnp.concatenate([t, t], axis=1)


def _swap_halves(x):
    w = x.shape[-1]
    lane = lax.broadcasted_iota(jnp.int32, x.shape, x.ndim - 1)
    return jnp.where((lane % 64) < 32, pltpu.roll(x, w - 32, x.ndim - 1), pltpu.roll(x, 32, x.ndim - 1))


def _mla_pre_fwd(proj, qn_w, kvn_w, wuqT, wukv, cos, sin, *, tm=ROW_TILE):
    S = proj.shape[0]
    tm = min(tm, S)

    def body(ql_ref, kl_ref, kr_ref, qw_ref, kw_ref, wuq_ref, wukv_ref, cos_ref, sin_ref,
             qc_ref, kc_ref, vv_ref):
        ql = ql_ref[...]
        qn = (ql * _rstd(ql) * qw_ref[...]).astype(BF16)
        q = _dot_nt(qn, wuq_ref[...])
        cs, sn = _rope_wide(cos_ref), _rope_wide(sin_ref)
        qr = q[:, 512:768]
        qr = qr * cs + _swap_halves(qr) * sn
        half = lax.broadcasted_iota(jnp.int32, (tm, 128), 1) // 64
        kl = kl_ref[...]
        kvn = (kl * _rstd(kl) * kw_ref[...]).astype(BF16)
        kr = kr_ref[...]
        kr = kr * cs[:, :128] + _swap_halves(kr) * sn[:, :128]
        kr2 = (kr + pltpu.roll(kr, 64, 1)).astype(BF16)
        kv = _dot(kvn, jnp.concatenate([wukv_ref[j] for j in range(2 * MLA_HEADS)], axis=1)).astype(BF16)
        for h in range(MLA_HEADS):
            qc_ref[h, :, 0:128] = q[:, 128 * h:128 * h + 128].astype(BF16)
            chunk = qr[:, 128 * (h // 2):128 * (h // 2) + 128]
            qc_ref[h, :, 128:256] = jnp.where(half == (h % 2), chunk, 0.0).astype(BF16)
            kc_ref[h, :, 0:128] = kv[:, 256 * h:256 * h + 128]
            kc_ref[h, :, 128:256] = kr2
            vv_ref[h] = kv[:, 256 * h + 128:256 * h + 256]

    const = lambda shape: pl.BlockSpec(shape, lambda i: tuple(0 for _ in shape))
    return pl.pallas_call(
        body, name="mla_pre_fwd", grid=(S // tm,),
        in_specs=[pl.BlockSpec((tm, 256), lambda i: (i, 3)), pl.BlockSpec((tm, 128), lambda i: (i, 8)),
                  pl.BlockSpec((tm, 128), lambda i: (i, 9)), const((1, 256)), const((1, 128)),
                  const((768, 256)), const((8, 128, 128)),
                  pl.BlockSpec((tm, 128), lambda i: (i, 0)), pl.BlockSpec((tm, 128), lambda i: (i, 0))],
        out_specs=[pl.BlockSpec((MLA_HEADS, tm, 256), lambda i: (0, i, 0)),
                   pl.BlockSpec((MLA_HEADS, tm, 256), lambda i: (0, i, 0)),
                   pl.BlockSpec((MLA_HEADS, tm, 128), lambda i: (0, i, 0))],
        out_shape=[jax.ShapeDtypeStruct((MLA_HEADS, S, 256), BF16), jax.ShapeDtypeStruct((MLA_HEADS, S, 256), BF16),
                   jax.ShapeDtypeStruct((MLA_HEADS, S, 128), BF16)],
        compiler_params=_params(("parallel",)),
    )(proj, proj, proj, qn_w, kvn_w, wuqT, wukv, cos, sin)


def _causal(i, j, t):
    row = i * t + lax.broadcasted_iota(jnp.int32, (t, t), 0)
    col = j * t + lax.broadcasted_iota(jnp.int32, (t, t), 1)
    return col <= row


def _mla_attn_fwd(qc, kc, vv, *, t=512):
    S = qc.shape[1]
    t = min(t, S)

    def body(q_ref, k_ref, v_ref, o_ref, l_ref):
        i = pl.program_id(0)
        diag = _causal(0, 0, t)

        def step(j, carry, masked):
            rows = pl.ds(pl.multiple_of(j * t, t), t)
            out = []
            for h in range(MLA_HEADS):
                m, l, acc = carry[h]
                s = _dot_nt(q_ref[h], k_ref[h, rows, :]) * MLA_SCALE
                if masked:
                    s = jnp.where(diag, s, -jnp.inf)
                m_new = jnp.maximum(m, jnp.max(s, axis=-1, keepdims=True))
                alpha = jnp.exp(m - m_new)
                p = jnp.exp(s - m_new)
                l = alpha * l + jnp.sum(p, axis=-1, keepdims=True)
                acc = alpha * acc + _dot(p.astype(BF16), v_ref[h, rows, :])
                out.append((m_new, l, acc))
            return tuple(out)

        init = tuple((jnp.full((t, 1), -jnp.inf, F32), jnp.zeros((t, 1), F32), jnp.zeros((t, MLA_V), F32))
                     for _ in range(MLA_HEADS))
        carry = lax.fori_loop(0, i, lambda j, c: step(j, c, False), init)
        carry = step(i, carry, True)
        for h in range(MLA_HEADS):
            m, l, acc = carry[h]
            o_ref[:, 128 * h:128 * h + 128] = acc / l
            l_ref[h] = jnp.broadcast_to(m + jnp.log(l), (t, 128))

    return pl.pallas_call(
        body, name="mla_attn_fwd", grid=(S // t,),
        in_specs=[pl.BlockSpec((MLA_HEADS, t, 256), lambda i: (0, i, 0)),
                  pl.BlockSpec((MLA_HEADS, S, 256), lambda i: (0, 0, 0)),
                  pl.BlockSpec((MLA_HEADS, S, 128), lambda i: (0, 0, 0))],
        out_specs=[pl.BlockSpec((t, 512), lambda i: (i, 0)),
                   pl.BlockSpec((MLA_HEADS, t, 128), lambda i: (0, i, 0))],
        out_shape=[jax.ShapeDtypeStruct((S, 512), F32), jax.ShapeDtypeStruct((MLA_HEADS, S, 128), F32)],
        compiler_params=_params(("parallel",)),
    )(qc, kc, vv)


def _mla_attn_bwd(qc, kc, vv, o, lse, do, *, t=512, tq=1024):
    S = qc.shape[1]
    t = min(t, S)
    tq = min(tq, S)
    nblk = S // t
    hp = MLA_HEADS
    once = pl.Buffered(1)

    def body(q_ref, k_ref, v_ref, o_ref, l_ref, do_ref, dq_ref, dk_ref, dv_ref):
        j = pl.program_id(1)

        @pl.when(j == 0)
        def _():
            dq_ref[...] = jnp.zeros_like(dq_ref)

        first = (j * t) // tq

        def step(i, carry, masked):
            rows = pl.ds(pl.multiple_of(i * tq, tq), tq)
            if masked:
                row = i * tq + lax.broadcasted_iota(jnp.int32, (tq, t), 0)
                col = j * t + lax.broadcasted_iota(jnp.int32, (tq, t), 1)
                visible = col <= row
            out = []
            for h in range(hp):
                dk, dv = carry[h]
                k = k_ref[h]
                q = q_ref[h, rows, :]
                dov = do_ref[rows, 128 * h:128 * h + 128]
                lrow = l_ref[h, rows, :][:, 0:1]
                p = jnp.exp(_dot_nt(q, k) * MLA_SCALE - lrow)
                if masked:
                    p = jnp.where(visible, p, 0.0)
                dob = dov.astype(BF16)
                dv = dv + _dot_tn(p.astype(BF16), dob)
                dp = _dot_nt(dob, v_ref[h])
                delta = jnp.sum(dov * o_ref[rows, 128 * h:128 * h + 128], axis=-1, keepdims=True)
                ds = (p * (dp - delta) * MLA_SCALE).astype(BF16)
                dk = dk + _dot_tn(ds, q)
                dq_ref[h, rows, :] += _dot(ds, k)
                out.append((dk, dv))
            return tuple(out)

        init = tuple((jnp.zeros((t, 256), F32), jnp.zeros((t, MLA_V), F32)) for _ in range(hp))
        carry = step(first, init, True)
        carry = lax.fori_loop(first + 1, S // tq, lambda i, c: step(i, c, False), carry)
        for h in range(hp):
            dk_ref[h] = carry[h][0]
            dv_ref[h] = carry[h][1]

    return pl.pallas_call(
        body, name="mla_attn_bwd", grid=(MLA_HEADS // hp, nblk),
        in_specs=[pl.BlockSpec((hp, S, 256), lambda g, j: (g, 0, 0), pipeline_mode=once),
                  pl.BlockSpec((hp, t, 256), lambda g, j: (g, j, 0)),
                  pl.BlockSpec((hp, t, 128), lambda g, j: (g, j, 0)),
                  pl.BlockSpec((S, 128 * hp), lambda g, j: (0, g), pipeline_mode=once),
                  pl.BlockSpec((hp, S, 128), lambda g, j: (g, 0, 0), pipeline_mode=once),
                  pl.BlockSpec((S, 128 * hp), lambda g, j: (0, g), pipeline_mode=once)],
        out_specs=[pl.BlockSpec((hp, S, 256), lambda g, j: (g, 0, 0)),
                   pl.BlockSpec((hp, t, 256), lambda g, j: (g, j, 0)),
                   pl.BlockSpec((hp, t, 128), lambda g, j: (g, j, 0))],
        out_shape=[jax.ShapeDtypeStruct((MLA_HEADS, S, 256), F32), jax.ShapeDtypeStruct((MLA_HEADS, S, 256), F32),
                   jax.ShapeDtypeStruct((MLA_HEADS, S, 128), F32)],
        compiler_params=_params(("parallel", "arbitrary")),
    )(qc, kc, vv, o, lse, do)


def _mla_pre_bwd(proj, qn_w, kvn_w, wuqT, wukv, cos, sin, dqc, dkc, dvv, *, tm=ROW_TILE):
    S = proj.shape[0]
    tm = min(tm, S)

    def body(ql_ref, kl_ref, qw_ref, kw_ref, wuq_ref, wukv_ref, cos_ref, sin_ref, dqc_ref, dkc_ref, dvv_ref,
             dql_ref, dkl_ref, dkr_ref, gq_ref, gkv_ref, part_ref, gq_acc, gkv_acc):
        @pl.when(pl.program_id(0) == 0)
        def _():
            gq_acc[...] = jnp.zeros_like(gq_acc)
            gkv_acc[...] = jnp.zeros_like(gkv_acc)
            part_ref[...] = jnp.zeros_like(part_ref)

        cs, sn = _rope_wide(cos_ref), _rope_wide(sin_ref)
        half = lax.broadcasted_iota(jnp.int32, (tm, 128), 1) // 64
        ql = ql_ref[...]
        rq = _rstd(ql)
        qhat = ql * rq
        qw = qw_ref[...]
        qn = (qhat * qw).astype(BF16)
        chunks = []
        for pair in range(2):
            chunks.append(jnp.where(half == 0, dqc_ref[2 * pair, :, 128:256], dqc_ref[2 * pair + 1, :, 128:256]))
        dqr = jnp.concatenate(chunks, axis=1)
        dqr = dqr * cs + _swap_halves(dqr * sn)
        dq = jnp.concatenate([dqc_ref[h, :, 0:128] for h in range(MLA_HEADS)] + [dqr], axis=1).astype(BF16)
        gq_acc[...] += _dot_tn(dq, qn)
        dqn = _dot(dq, wuq_ref[...])
        part_ref[0:1, :] += jnp.sum(dqn * qhat, axis=0, keepdims=True)
        dql_ref[...] = _rms_bwd(dqn * qw, qhat, rq)
        kl = kl_ref[...]
        rk = _rstd(kl)
        khat = kl * rk
        kw = kw_ref[...]
        kvn = (khat * kw).astype(BF16)
        dkvn = jnp.zeros((tm, MLA_KVR), F32)
        dkr2 = jnp.zeros((tm, 128), F32)
        for h in range(MLA_HEADS):
            dkn = dkc_ref[h, :, 0:128].astype(BF16)
            dvh = dvv_ref[h].astype(BF16)
            gkv_acc[2 * h] += _dot_tn(kvn, dkn)
            gkv_acc[2 * h + 1] += _dot_tn(kvn, dvh)
            dkvn += _dot_nt(dkn, wukv_ref[2 * h]) + _dot_nt(dvh, wukv_ref[2 * h + 1])
            dkr2 += dkc_ref[h, :, 128:256]
        part_ref[1:2, 0:128] += jnp.sum(dkvn * khat, axis=0, keepdims=True)
        dkl_ref[...] = _rms_bwd(dkvn * kw, khat, rk)
        dkr = jnp.where(half == 0, dkr2 + pltpu.roll(dkr2, 64, 1), 0.0)
        dkr_ref[...] = dkr * cs[:, :128] + _swap_halves(dkr * sn[:, :128])

        @pl.when(pl.program_id(0) == S // tm - 1)
        def _():
            gkv_ref[...] = gkv_acc[...].astype(BF16)
            per = MLA_NOPE + MLA_ROPE
            for h in range(MLA_HEADS):
                gq_ref[per * h:per * h + MLA_NOPE, :] = gq_acc[MLA_NOPE * h:MLA_NOPE * (h + 1), :].astype(BF16)
                gq_ref[per * h + MLA_NOPE:per * (h + 1), :] = gq_acc[512 + MLA_ROPE * h:512 + MLA_ROPE * (h + 1), :].astype(BF16)

    const = lambda shape: pl.BlockSpec(shape, lambda i: tuple(0 for _ in shape))
    heads = lambda w: pl.BlockSpec((MLA_HEADS, tm, w), lambda i: (0, i, 0))
    return pl.pallas_call(
        body, name="mla_pre_bwd", grid=(S // tm,),
        in_specs=[pl.BlockSpec((tm, 256), lambda i: (i, 3)), pl.BlockSpec((tm, 128), lambda i: (i, 8)),
                  const((1, 256)), const((1, 128)), const((768, 256)), const((8, 128, 128)),
                  pl.BlockSpec((tm, 128), lambda i: (i, 0)), pl.BlockSpec((tm, 128), lambda i: (i, 0)),
                  heads(256), heads(256), heads(128)],
        out_specs=[pl.BlockSpec((tm, 256), lambda i: (i, 0)), pl.BlockSpec((tm, 128), lambda i: (i, 0)),
                   pl.BlockSpec((tm, 128), lambda i: (i, 0)), const((768, 256)), const((8, 128, 128)), const((8, 256))],
        out_shape=[jax.ShapeDtypeStruct((S, 256), F32), jax.ShapeDtypeStruct((S, 128), F32),
                   jax.ShapeDtypeStruct((S, 128), F32), jax.ShapeDtypeStruct((768, 256), BF16),
                   jax.ShapeDtypeStruct((8, 128, 128), BF16), jax.ShapeDtypeStruct((8, 256), F32)],
        scratch_shapes=[pltpu.VMEM((768, 256), F32), pltpu.VMEM((8, 128, 128), F32)],
        compiler_params=_params(("arbitrary",)),
    )(proj, proj, qn_w, kvn_w, wuqT, wukv, cos, sin, dqc, dkc, dvv)


def _mix_out_fwd(x, oa, ob, w_o, vecs, *, tm=ROW_TILE):
    S = x.shape[0]
    tm = min(tm, S)

    def body(x_ref, oa_ref, ob_ref, w_ref, vec_ref, xo_ref, mo_ref):
        mo = _dot(oa_ref[...].astype(BF16), w_ref[0:512, :]) + _dot(ob_ref[...].astype(BF16), w_ref[512:1024, :])
        mo_ref[...] = mo
        xo_ref[...] = x_ref[...] + vec_ref[3:4, :] * mo

    row = pl.BlockSpec((tm, D), lambda i: (i, 0))
    half = pl.BlockSpec((tm, 512), lambda i: (i, 0))
    return pl.pallas_call(
        body, name="mix_out_fwd", grid=(S // tm,),
        in_specs=[row, half, half, pl.BlockSpec((D, D), lambda i: (0, 0)), pl.BlockSpec((8, D), lambda i: (0, 0))],
        out_specs=[row, row],
        out_shape=[jax.ShapeDtypeStruct((S, D), F32), jax.ShapeDtypeStruct((S, D), F32)],
        compiler_params=_params(("parallel",)),
    )(x, oa, ob, w_o, vecs)


def _mix_out_bwd(dxo, mo, oa, ob, w_o, vecs, *, tm=ROW_TILE):
    S = dxo.shape[0]
    tm = min(tm, S)

    def body(dx_ref, mo_ref, oa_ref, ob_ref, w_ref, vec_ref, doa_ref, dob_ref, gw_ref, part_ref, gw_acc):
        @pl.when(pl.program_id(0) == 0)
        def _():
            gw_acc[...] = jnp.zeros_like(gw_acc)
            part_ref[...] = jnp.zeros_like(part_ref)

        dx = dx_ref[...]
        part_ref[0:1, :] += jnp.sum(dx * mo_ref[...], axis=0, keepdims=True)
        dmo = (vec_ref[3:4, :] * dx).astype(BF16)
        doa_ref[...] = _dot_nt(dmo, w_ref[0:512, :])
        dob_ref[...] = _dot_nt(dmo, w_ref[512:1024, :])
        gw_acc[0:512, :] += _dot_tn(oa_ref[...].astype(BF16), dmo)
        gw_acc[512:1024, :] += _dot_tn(ob_ref[...].astype(BF16), dmo)

        @pl.when(pl.program_id(0) == S // tm - 1)
        def _():
            gw_ref[...] = gw_acc[...].astype(BF16)

    row = pl.BlockSpec((tm, D), lambda i: (i, 0))
    half = pl.BlockSpec((tm, 512), lambda i: (i, 0))
    return pl.pallas_call(
        body, name="mix_out_bwd", grid=(S // tm,),
        in_specs=[row, row, half, half, pl.BlockSpec((D, D), lambda i: (0, 0)), pl.BlockSpec((8, D), lambda i: (0, 0))],
        out_specs=[half, half, pl.BlockSpec((D, D), lambda i: (0, 0)), pl.BlockSpec((8, D), lambda i: (0, 0))],
        out_shape=[jax.ShapeDtypeStruct((S, 512), F32), jax.ShapeDtypeStruct((S, 512), F32),
                   jax.ShapeDtypeStruct((D, D), BF16), jax.ShapeDtypeStruct((8, D), F32)],
        scratch_shapes=[pltpu.VMEM((D, D), F32)],
        compiler_params=_params(("arbitrary",)),
    )(dxo, mo, oa, ob, w_o, vecs)


def _mix_in_bwd(h, w_inT, dq, dk, dv, dql, dkl, dkr, *, tm=ROW_TILE):
    S = h.shape[0]
    tm = min(tm, S)
    wid = (512, 128, 128, 256, 128, 128)

    def body(h_ref, w_ref, dq_ref, dk_ref, dv_ref, dql_ref, dkl_ref, dkr_ref, dh_ref, gw_ref):
        @pl.when(pl.program_id(0) == 0)
        def _():
            gw_ref[...] = jnp.zeros_like(gw_ref)

        parts = (dq_ref, dk_ref, dv_ref, dql_ref, dkl_ref, dkr_ref)
        dproj = jnp.concatenate([ref[...].astype(BF16) for ref in parts], axis=1)
        dh_ref[...] = _dot(dproj, w_ref[...])
        gw_ref[...] += _dot_tn(dproj, h_ref[...])[0:D_IN, :]

    row = pl.BlockSpec((tm, D), lambda i: (i, 0))
    part = lambda w: pl.BlockSpec((tm, w), lambda i: (i, 0))
    return pl.pallas_call(
        body, name="mix_in_bwd", grid=(S // tm,),
        in_specs=[row, pl.BlockSpec((D_IN_PAD, D), lambda i: (0, 0))] + [part(w) for w in wid],
        out_specs=[row, pl.BlockSpec((D_IN, D), lambda i: (0, 0))],
        out_shape=[jax.ShapeDtypeStruct((S, D), F32), jax.ShapeDtypeStruct((D_IN, D), F32)],
        compiler_params=_params(("arbitrary",)),
    )(h, w_inT, dq, dk, dv, dql, dkl, dkr)


def _vecs(norm_w, mod9, k):
    return jnp.concatenate([norm_w.reshape(1, D), mod9[3 * k:3 * k + 3], jnp.zeros((4, D), F32)], axis=0)


def _uq_group_rows(wuqT):
    per = MLA_NOPE + MLA_ROPE
    nope = [wuqT[per * h:per * h + MLA_NOPE] for h in range(MLA_HEADS)]
    rope = [wuqT[per * h + MLA_NOPE:per * (h + 1)] for h in range(MLA_HEADS)]
    return jnp.concatenate(nope + rope, axis=0)


def _local_step(x, tgt, mod9, norms, sinks, rel_bias, q_norm, kv_norm, W, on_grads=None):
    if on_grads is None:
        on_grads = lambda group, grads, after, vecs: vecs
    S = x.shape[0]
    v1 = _vecs(norms["ffn1"], mod9, 0)
    v2 = _vecs(norms["mix"], mod9, 1)
    v3 = _vecs(norms["ffn2"], mod9, 2)
    bucket = jnp.asarray(_bucket_table())
    cos, sin = _rope_tables(S)
    if isinstance(W, dict):
        full, W = W, (lambda group, after, vecs: (full, vecs))

    W1, v1 = W("ffn1", [], v1)
    x1, h1, a1, b1, f1 = _ffn_fwd(x, v1, W1["g1T"], W1["u1T"], W1["d1"], name="ffn1_fwd")
    W2, v2 = W("mixer", [x1], v2)
    wuqT = _uq_group_rows(W2["w_uqT"])
    h2, proj, w_inT = _mix_in_fwd(x1, v2, W2["w_inT"])
    oa, bias = _swa_fwd(proj, rel_bias, bucket, sinks)
    qc, kc, vv = _mla_pre_fwd(proj, q_norm, kv_norm, wuqT, W2["w_ukv"], cos, sin)
    ob, lse = _mla_attn_fwd(qc, kc, vv)
    _, v2o = W("ffn2_on_its_way", [ob], v2)
    x2, mo = _mix_out_fwd(x1, oa, ob, W2["w_o"], v2o)
    W3, v3 = W("ffn2", [x2], v3)
    x3, h3, a3, b3, f3 = _ffn_fwd(x2, v3, W3["g3T"], W3["u3T"], W3["d3"], name="ffn2_fwd")
    dx3, head_part, df3 = _head(x3, tgt, norms["final"], f3, v3)

    gg3, gu3, gd3, dh3 = _ffn_bwd_main(h3, df3, a3, b3, W3["g3T"], W3["u3T"], W3["d3"], name="ffn2_bwd")
    ffn2 = {"g3T": gg3, "u3T": gu3, "d3": gd3}
    v3 = on_grads("ffn2", ffn2, [], v3)
    dx2, n3_part = _norm_bwd(dh3, x2, dx3, v3, name="ffn2_norm_bwd")
    v2 = on_grads("ffn2", None, [dx2], v2)
    doa, dob, g_wo, g2_part = _mix_out_bwd(dx2, mo, oa, ob, W2["w_o"], v2)
    dq, dk, dv, drb, dsk = _swa_bwd(proj, bias, sinks, oa, doa, bucket)
    dqc, dkc, dvv = _mla_attn_bwd(qc, kc, vv, ob, lse, dob)
    dql, dkl, dkr, g_uq, g_ukv, mla_part = _mla_pre_bwd(proj, q_norm, kv_norm, wuqT, W2["w_ukv"], cos, sin, dqc, dkc, dvv)
    dh2, g_win = _mix_in_bwd(h2, w_inT, dq, dk, dv, dql, dkl, dkr)
    mixer = {"w_inT": g_win, "w_uqT": g_uq, "w_ukv": g_ukv, "w_o": g_wo}
    v2 = on_grads("mixer", mixer, [], v2)
    dx1, n2_part, df1 = _norm_bwd(dh2, x1, dx2, v2, name="mix_norm_bwd", below=(f1, v1))
    started = on_grads("mixer", None, [dx1], jnp.zeros((1, 1), F32))
    gg1, gu1, gd1, dh1 = _ffn_bwd_main(h1, df1, a1, b1, W1["g1T"], W1["u1T"], W1["d1"], name="ffn1_bwd",
                                       after=[started])
    ffn1 = {"g1T": gg1, "u1T": gu1, "d1": gd1}
    v1 = on_grads("ffn1", ffn1, [], v1)
    dx0, n1_part = _norm_bwd(dh1, x, dx1, v1, name="ffn1_norm_bwd")

    grads = {**ffn1, **ffn2, **mixer}
    return head_part[1, 0], dx0, grads, _pack_vec(n1_part, n2_part, n3_part, head_part, g2_part, mla_part, dsk, drb)


SMALL_LAYOUT = (("norm_ffn1", 1024), ("norm_mix", 1024), ("norm_ffn2", 1024), ("norm_final", 1024),
                ("q_norm", 256), ("kv_norm", 128), ("sinks", 128), ("rel_bias", 256))
N_SMALL = sum(n for _, n in SMALL_LAYOUT)
LOSS_SLOT = 4 * 1024 + 256 + 128 + SWA_HEADS
N_MODVEC = N_MOD * D
N_VEC = N_MODVEC + N_SMALL


def _pack_vec(n1, n2, n3, head, g2, mla, dsk, drb):
    def body(n1_ref, n2_ref, n3_ref, head_ref, g2_ref, mla_ref, dsk_ref, drb_ref, out_ref):
        rows = [n1_ref[1:2, :], n1_ref[2:3, :], n2_ref[3:4, :], n2_ref[1:2, :], n2_ref[2:3, :], g2_ref[0:1, :],
                n3_ref[1:2, :], n3_ref[2:3, :], head_ref[3:4, :],
                n1_ref[0:1, :], n2_ref[0:1, :], n3_ref[0:1, :], head_ref[0:1, :]]
        for i, row in enumerate(rows):
            out_ref[:, D * i:D * (i + 1)] = row
        off = D * len(rows)
        out_ref[:, off:off + 256] = mla_ref[0:1, :]
        out_ref[:, off + 256:off + 384] = mla_ref[1:2, 0:128]

        def diagonal(block):
            r = lax.broadcasted_iota(jnp.int32, block.shape, 0)
            lane = lax.broadcasted_iota(jnp.int32, block.shape, 1)
            return jnp.sum(jnp.where(r == lane, block, 0.0), axis=0, keepdims=True)

        lane = lax.broadcasted_iota(jnp.int32, (1, 128), 1)
        out_ref[:, off + 384:off + 512] = jnp.where(lane == SWA_HEADS, head_ref[1:2, 0:128], diagonal(dsk_ref[...]))
        out_ref[:, off + 512:off + 640] = diagonal(drb_ref[0:128, :])
        out_ref[:, off + 640:off + 768] = diagonal(drb_ref[128:256, :])

    vm = pl.BlockSpec(memory_space=pltpu.VMEM)
    return pl.pallas_call(body, name="pack_vec", in_specs=[vm] * 8, out_specs=vm,
                          out_shape=jax.ShapeDtypeStruct((1, N_VEC), F32))(n1, n2, n3, head, g2, mla, dsk, drb)


def _coords():
    return lax.axis_index("x"), lax.axis_index("y"), lax.axis_index("c")


def _flip(v, bit):
    return 1 - v if bit else v


def _peer(r):
    x, y, c = _coords()
    return (_flip(x, r & 4), _flip(y, r & 2), _flip(c, r & 1))


class _ModExchange:
    def __init__(self, c_ref, w_ref, b_ref, mod_ref, ca_ref, call_ref, part_ref, send_sems, recv_sems):
        self.refs = (c_ref, w_ref, b_ref, mod_ref, ca_ref, call_ref, part_ref)
        self.sems = (send_sems, recv_sems)
        x, y, c = _coords()
        self.me = 4 * x + 2 * y + c
        self.sends = []

    def _copy(self, phase, r):
        c_ref, _, _, mod_ref, _, call_ref, part_ref = self.refs
        src, dst = (c_ref, call_ref.at[self.me]) if phase == 0 else (part_ref.at[self.me ^ r], mod_ref.at[self.me])
        return pltpu.make_async_remote_copy(src, dst, self.sems[0].at[phase, r], self.sems[1].at[phase, r],
                                            device_id=_peer(r), device_id_type=MESH)

    def _start(self, phase):
        for r in range(1, N_DEV):
            self.sends.append(self._copy(phase, r))
            self.sends[-1].start()

    def begin(self):
        c_ref, _, _, _, _, call_ref, _ = self.refs
        call_ref[self.me] = c_ref[...]
        self._start(0)

    def middle(self):
        _, w_ref, b_ref, mod_ref, ca_ref, call_ref, part_ref = self.refs
        for r in range(1, N_DEV):
            self._copy(0, r).wait_recv()
        cv = call_ref[...].reshape(8 * N_DEV, D)
        ca = (cv * _sigmoid(cv)).astype(BF16)
        ca_ref[...] = ca
        part_ref[...] = _dot(ca, w_ref[...].astype(BF16)).reshape(part_ref.shape)
        mod_ref[self.me] = part_ref[self.me] + b_ref[self.me]
        self._start(1)

    def end(self):
        _, _, b_ref, mod_ref, _, _, _ = self.refs
        for r in range(1, N_DEV):
            self._copy(1, r).wait_recv()
            mod_ref[self.me ^ r] = mod_ref[self.me ^ r] + b_ref[self.me ^ r]
        for cp in self.sends:
            cp.wait_send()


def _mod_bwd(allvec, ca, me_idx):
    W = N_MODVEC // N_DEV

    def body(me_ref, all_ref, cols_ref, ca_ref, gw_ref, sum_ref):
        in_first_row = lax.broadcasted_iota(jnp.int32, (N_DEV, 8, W), 1) == 0
        dm = jnp.where(in_first_row, cols_ref[...], 0.0).reshape(8 * N_DEV, W)
        gw_ref[...] = _dot_tn(ca_ref[...], dm.astype(BF16))
        total = all_ref[0]
        for k in range(1, N_DEV):
            total = total + all_ref[k]
        sum_ref[...] = total

    return pl.pallas_call(
        body, name="mod_bwd",
        grid_spec=pltpu.PrefetchScalarGridSpec(
            num_scalar_prefetch=1, grid=(1,),
            in_specs=[pl.BlockSpec((N_DEV, 1, N_VEC), lambda i, me: (0, 0, 0)),
                      pl.BlockSpec((N_DEV, 1, W), lambda i, me: (0, 0, me[0])),
                      pl.BlockSpec((8 * N_DEV, D), lambda i, me: (0, 0))],
            out_specs=[pl.BlockSpec((D, W), lambda i, me: (0, 0)), pl.BlockSpec((1, N_VEC), lambda i, me: (0, 0))]),
        out_shape=[jax.ShapeDtypeStruct((D, W), F32), jax.ShapeDtypeStruct((1, N_VEC), F32)],
        compiler_params=_params(("arbitrary",)),
    )(me_idx, allvec, allvec, ca)


def _wgather(shards, later, c_tile, w_mod, b_mod3):
    n, nl = len(shards), len(later)
    rows = [s.shape[0] for s in shards]
    W = w_mod.shape[1]
    cast = [(s.shape, True) for s in shards] + [(s.shape, s.dtype != dt) for s, dt in later]
    staging = [pltpu.VMEM(shape, dt) for shape, is_cast in cast if is_cast for dt in (F32, BF16)]

    def body(*refs):
        ins, (c_ref, w_ref, b_ref), raw = refs[:n], refs[n:n + 3], refs[n + 3:n + 3 + nl]
        o = n + 3 + 2 * nl
        outs, (mod_ref, ca_ref), lands = refs[o:o + n], refs[o + n:o + n + 2], refs[o + n + 2:o + n + 2 + nl]
        o += n + 2 + nl
        send_sems, recv_sems, local_sems, load_sems, store_sems = refs[o:o + 5]
        mod = _ModExchange(c_ref, w_ref, b_ref, mod_ref, ca_ref, *refs[o + 5:o + 9])
        stage = iter(refs[o + 9:])
        staged = [(next(stage), next(stage)) if is_cast else None for _, is_cast in cast]
        mod.begin()
        x, y, c = _coords()
        me = 4 * x + 2 * y + c
        sib, xn, yn = (x, y, 1 - c), (1 - x, y, c), (x, 1 - y, c)
        block = lambda px, py, pc: 4 * px + 2 * py + pc

        def part(k, blk, half):
            if half is None:
                return outs[k].at[blk]
            return outs[k].at[blk, pl.ds(half * (rows[k] // 2), rows[k] // 2)]

        def copy(k, slot, blk, to, half=None, src=None):
            ref = part(k, blk, half)
            return pltpu.make_async_remote_copy(
                src_ref=ref if src is None else src, dst_ref=ref, send_sem=send_sems.at[k, slot],
                recv_sem=recv_sems.at[k, slot], device_id=to, device_id_type=MESH)

        def in_bf16(sources, first):
            loads = [pltpu.make_async_copy(src, staged[first + i][0], load_sems.at[first + i])
                     if staged[first + i] else None for i, src in enumerate(sources)]
            for cp in loads:
                if cp is not None:
                    cp.start()
            for i, src in enumerate(sources):
                if loads[i] is None:
                    yield i, src
                    continue
                loads[i].wait()
                wide, narrow = staged[first + i]
                narrow[...] = wide[...].astype(BF16)
                yield i, narrow

        mine = [ref for _, ref in in_bf16(ins, 0)]
        local = [pltpu.make_async_copy(mine[k], outs[k].at[me], local_sems.at[k]) for k in range(n)]
        for cp in local:
            cp.start()
        sent = [copy(k, slot, me, to, src=mine[k]) for k in range(n) for slot, to in ((0, sib), (1, xn), (2, yn))]
        for cp in sent:
            cp.start()
        mod.middle()
        for k, ref in in_bf16(raw, n):
            local.append(pltpu.make_async_copy(ref, lands[k].at[me], store_sems.at[k]))
            local[-1].start()
        bx, by, bd = block(1 - x, y, c), block(x, 1 - y, c), block(1 - x, 1 - y, c)
        for k in range(n):
            copy(k, 1, bx, sib).wait_recv()
            sent += [copy(k, 4, bx, yn, half=1), copy(k, 5, bx, sib)]
            sent[-2].start()
            sent[-1].start()
        for k in range(n):
            copy(k, 2, by, sib).wait_recv()
            sent += [copy(k, 3, by, xn, half=0), copy(k, 6, by, sib)]
            sent[-2].start()
            sent[-1].start()
        for k in range(n):
            copy(k, 3, bd, sib, half=0).wait_recv()
            copy(k, 4, bd, sib, half=1).wait_recv()
            sent.append(copy(k, 7, bd, sib))
            sent[-1].start()
        for k in range(n):
            copy(k, 0, block(x, y, 1 - c), sib).wait_recv()
            for slot, blk in ((5, block(1 - x, y, 1 - c)), (6, block(x, 1 - y, 1 - c)), (7, block(1 - x, 1 - y, 1 - c))):
                copy(k, slot, blk, sib).wait_recv()
        for cp in sent:
            cp.wait_send()
        for cp in local:
            cp.wait()
        mod.end()

    anyspec, vm = pl.BlockSpec(memory_space=pl.ANY), pl.BlockSpec(memory_space=pltpu.VMEM)
    zones = [lax.empty((N_DEV,) + s.shape, dt) for s, dt in later]
    out = pl.pallas_call(
        body, name="wgather", in_specs=[anyspec] * n + [vm] * 3 + [anyspec] * (2 * nl),
        out_specs=[anyspec] * n + [vm] * 2 + [anyspec] * nl,
        out_shape=[jax.ShapeDtypeStruct((N_DEV,) + s.shape, BF16) for s in shards]
        + [jax.ShapeDtypeStruct((N_DEV, 8, W), F32), jax.ShapeDtypeStruct((8 * N_DEV, D), BF16)]
        + [jax.ShapeDtypeStruct(z.shape, z.dtype) for z in zones],
        input_output_aliases={n + 3 + nl + i: n + 2 + i for i in range(nl)},
        scratch_shapes=[pltpu.SemaphoreType.DMA((n, 8)), pltpu.SemaphoreType.DMA((n, 8)),
                        pltpu.SemaphoreType.DMA((n,)), pltpu.SemaphoreType.DMA((n + nl,)),
                        pltpu.SemaphoreType.DMA((nl,)),
                        pltpu.VMEM((N_DEV, 8, D), F32), pltpu.VMEM((N_DEV, 8, W), F32),
                        pltpu.SemaphoreType.DMA((2, N_DEV)), pltpu.SemaphoreType.DMA((2, N_DEV))] + staging,
        compiler_params=_params(),
    )(*shards, c_tile, w_mod, b_mod3, *[s for s, _ in later], *zones)
    return out[:n], out[n], out[n + 1], list(out[n + 2:])


class _GatherCopies:
    def __init__(self, lands, send_sems, recv_sems, k0=0, batches=None):
        x, y, c = _coords()
        me = 4 * x + 2 * y + c
        sib = (x, y, 1 - c)
        chips = [(1 - x, y), (x, 1 - y), (1 - x, 1 - y)]

        def copy(k, slot, block, to):
            return pltpu.make_async_remote_copy(
                src_ref=lands[k].at[block], dst_ref=lands[k].at[block],
                send_sem=send_sems.at[7 * (k0 + k) + slot], recv_sem=recv_sems.at[7 * (k0 + k) + slot],
                device_id=to, device_id_type=MESH)

        n = len(lands)
        self.first = [copy(k, 0, me, sib) for k in range(n)]
        for batch in batches or [range(n)]:
            self.first += [copy(k, 1 + j, me, (cx, cy, c)) for j, (cx, cy) in enumerate(chips) for k in batch]
        self.landed = [copy(k, 1 + j, 4 * cx + 2 * cy + c, sib) for j, (cx, cy) in enumerate(chips) for k in range(n)]
        self.passed = [copy(k, 4 + j, 4 * cx + 2 * cy + c, sib) for j, (cx, cy) in enumerate(chips) for k in range(n)]
        self.from_sib = [copy(k, 0, 4 * x + 2 * y + (1 - c), sib) for k in range(n)]
        self.from_sib += [copy(k, 4 + j, 4 * cx + 2 * cy + (1 - c), sib) for j, (cx, cy) in enumerate(chips)
                          for k in range(n)]


def _token(carry):
    return [] if carry is None else [carry], jax.ShapeDtypeStruct((8, 128), F32) if carry is None else carry


def _gather_start(lands, *, name, batches=None, carry=None):
    n = len(lands)
    carried, token = _token(carry)

    def body(*refs):
        n_in = n + len(carried)
        for cp in _GatherCopies(refs[:n], refs[n_in], refs[n_in + 1], batches=batches).first:
            cp.start()
        refs[-1][...] = refs[n][...] if carried else jnp.zeros_like(refs[-1])

    vm = pl.BlockSpec(memory_space=pltpu.VMEM)
    out = pl.pallas_call(
        body, name=name,
        out_shape=(pltpu.SemaphoreType.DMA((7 * n,)), pltpu.SemaphoreType.DMA((7 * n,)),
                   *[pltpu.HBM(l.shape, l.dtype) for l in lands], jax.ShapeDtypeStruct(token.shape, token.dtype)),
        in_specs=[HBM_SPEC] * n + [vm] * len(carried),
        out_specs=(SEM_SPEC, SEM_SPEC, *[HBM_SPEC] * n, vm),
        input_output_aliases={i: 2 + i for i in range(n)},
        compiler_params=pltpu.CompilerParams(has_side_effects=DATAFLOW),
    )(*[_in_hbm(l) for l in lands], *carried)
    return out[0], out[1], list(out[2:2 + n]), out[-1]


def _gather_pass(send_sems, recv_sems, lands, after, *, name, stage, k0=0, carry=None):
    n = len(lands)
    carried, token = _token(carry)

    def body(*refs):
        cps = _GatherCopies(refs[:n], refs[n], refs[n + 1], k0)
        if stage == "landed":
            for cp in cps.landed:
                cp.wait_recv()
        else:
            for cp in cps.passed:
                cp.start()
        refs[-1][...] = refs[n + 2 + len(after)][...] if carried else jnp.zeros_like(refs[-1])

    vm = pl.BlockSpec(memory_space=pltpu.VMEM)
    out = pl.pallas_call(
        body, name=name,
        out_shape=(*[pltpu.HBM(l.shape, l.dtype) for l in lands], jax.ShapeDtypeStruct(token.shape, token.dtype)),
        in_specs=[HBM_SPEC] * n + [SEM_SPEC, SEM_SPEC] + [pl.BlockSpec(memory_space=pl.ANY)] * len(after)
        + [vm] * len(carried),
        out_specs=(*[HBM_SPEC] * n, vm),
        input_output_aliases={i: i for i in range(n)},
        compiler_params=pltpu.CompilerParams(has_side_effects=DATAFLOW),
    )(*lands, send_sems, recv_sems, *after, *carried)
    return list(out[:n]), out[-1]


def _gather_end(send_sems, recv_sems, lands, after, *, name, k0=0):
    n = len(lands)

    def body(*refs):
        cps = _GatherCopies(refs[:n], refs[n], refs[n + 1], k0)
        for cp in cps.from_sib:
            cp.wait_recv()
        for cp in cps.first + cps.passed:
            cp.wait_send()

    out = pl.pallas_call(
        body, name=name,
        out_shape=[pltpu.HBM(l.shape, l.dtype) for l in lands],
        in_specs=[HBM_SPEC] * n + [SEM_SPEC, SEM_SPEC] + [pl.BlockSpec(memory_space=pl.ANY)] * len(after),
        out_specs=[HBM_SPEC] * n,
        input_output_aliases={i: i for i in range(n)},
        compiler_params=pltpu.CompilerParams(has_side_effects=DATAFLOW),
    )(*lands, send_sems, recv_sems, *after)
    return list(out)


def _d2d_copies(grads, lands, send_sems, recv_sems):
    x, y, c = _coords()
    return [pltpu.make_async_remote_copy(
        src_ref=grads[k].at[2 * q + (1 - c)], dst_ref=lands[k].at[q],
        send_sem=send_sems.at[4 * k + q], recv_sem=recv_sems.at[4 * k + q],
        device_id=(x, y, 1 - c), device_id_type=MESH) for k in range(len(grads)) for q in range(4)]


def _direct_copies(grads, lands, send_sems, recv_sems):
    x, y, c = _coords()
    me = 4 * x + 2 * y + c
    return [pltpu.make_async_remote_copy(
        src_ref=grads[k].at[me ^ r], dst_ref=lands[k].at[r - 1],
        send_sem=send_sems.at[7 * k + r - 1], recv_sem=recv_sems.at[7 * k + r - 1],
        device_id=_peer(r), device_id_type=MESH) for k in range(len(grads)) for r in range(1, N_DEV)]


def _vec_copies(srcs, lands, send_sems, recv_sems):
    x, y, c = _coords()
    me = 4 * x + 2 * y + c
    return [pltpu.make_async_remote_copy(
        src_ref=lands[0].at[me], dst_ref=lands[0].at[me], send_sem=send_sems.at[r - 1], recv_sem=recv_sems.at[r - 1],
        device_id=_peer(r), device_id_type=MESH) for r in range(1, N_DEV)]


def _chipsum(gs, sibs, cidx, *, name):
    n = len(gs)

    def body(c_ref, *refs):
        for k in range(n):
            refs[2 * n + k][...] = (refs[k][...].astype(F32) + refs[n + k][...].astype(F32)).astype(refs[2 * n + k].dtype)

    mine = [pl.BlockSpec((1,) + g.shape[1:], lambda q, c_ref: (2 * q + c_ref[0], 0, 0)) for g in gs]
    other = [pl.BlockSpec((1,) + g.shape[1:], lambda q, c_ref: (q, 0, 0)) for g in gs]
    return pl.pallas_call(
        body, name=name,
        grid_spec=pltpu.PrefetchScalarGridSpec(num_scalar_prefetch=1, grid=(4,), in_specs=mine + other, out_specs=other),
        out_shape=[jax.ShapeDtypeStruct((4,) + g.shape[1:], g.dtype) for g in gs],
        compiler_params=_params(("arbitrary",)),
    )(cidx, *gs, *sibs)


HBM_SPEC = pl.BlockSpec(memory_space=pltpu.HBM)
SEM_SPEC = pl.BlockSpec(memory_space=pltpu.SEMAPHORE)
DATAFLOW = pltpu.SideEffectType.DATAFLOW_SIDE_EFFECTING


def _in_hbm(a):
    return pltpu.with_memory_space_constraint(a, pltpu.HBM)


def _rs_step1_copies(sums, lands, send_sems, recv_sems):
    n = len(sums)
    direct, relay = lands[:n], lands[n:]
    x, y, c = _coords()
    xn, yn = (1 - x, y, c), (x, 1 - y, c)
    qx, qy, qd = 2 * (1 - x) + y, 2 * x + (1 - y), 2 * (1 - x) + (1 - y)
    cps = []
    for k in range(n):
        h = sums[k].shape[1] // 2
        a, b = pl.ds(0, h), pl.ds(h, h)
        moves = ((sums[k].at[qx, a], direct[k].at[0], xn), (sums[k].at[qy, b], direct[k].at[1], yn),
                 (sums[k].at[qd, a], relay[k].at[0], xn), (sums[k].at[qd, b], relay[k].at[1], yn))
        for s, (src, dst, to) in enumerate(moves):
            cps.append(pltpu.make_async_remote_copy(
                src_ref=src, dst_ref=dst, send_sem=send_sems.at[4 * k + s], recv_sem=recv_sems.at[4 * k + s],
                device_id=to, device_id_type=MESH))
    return cps


def _rs_step2_copies(relayed, lands, send_sems, recv_sems, k0=0):
    x, y, c = _coords()
    cps = []
    for k in range(len(relayed)):
        for s, to in enumerate(((1 - x, y, c), (x, 1 - y, c))):
            cps.append(pltpu.make_async_remote_copy(
                src_ref=relayed[k].at[s], dst_ref=lands[k].at[s], send_sem=send_sems.at[2 * (k0 + k) + s],
                recv_sem=recv_sems.at[2 * (k0 + k) + s], device_id=to, device_id_type=MESH))
    return cps


def _relay_sum(sums, relay, qxy, *, name):
    n = len(sums)

    def body(q_ref, *refs):
        for k in range(n):
            refs[2 * n + k][...] = (refs[k][...].astype(F32) + refs[n + k][...].astype(F32)).astype(refs[2 * n + k].dtype)

    half = lambda s: (1, s.shape[1] // 2) + s.shape[2:]
    return pl.pallas_call(
        body, name=name,
        grid_spec=pltpu.PrefetchScalarGridSpec(
            num_scalar_prefetch=1, grid=(2,),
            in_specs=[pl.BlockSpec(half(s), lambda t, q_ref: (q_ref[t], 1 - t, 0)) for s in sums]
            + [pl.BlockSpec(half(s), lambda t, q_ref: (1 - t, 0, 0)) for s in sums],
            out_specs=[pl.BlockSpec(half(s), lambda t, q_ref: (t, 0, 0)) for s in sums]),
        out_shape=[jax.ShapeDtypeStruct((2,) + half(s)[1:], s.dtype) for s in sums],
        compiler_params=_params(("arbitrary",)),
    )(qxy, *sums, *relay)


def _split_start(copies, srcs, lands, n_sems, after, *, name, carry=None):
    ns, nl = len(srcs), len(lands)
    carried, token = _token(carry)

    def body(*refs):
        n_in = ns + nl + len(after) + len(carried)
        for cp in copies(refs[:ns], refs[ns:ns + nl], refs[n_in], refs[n_in + 1]):
            cp.start()
        refs[-1][...] = refs[n_in - 1][...] if carried else jnp.zeros_like(refs[-1])

    bufs = [_in_hbm(a) for a in list(srcs) + list(lands)]
    vm = pl.BlockSpec(memory_space=pltpu.VMEM)
    out = pl.pallas_call(
        body, name=name,
        out_shape=(pltpu.SemaphoreType.DMA((n_sems,)), pltpu.SemaphoreType.DMA((n_sems,)),
                   *[pltpu.HBM(a.shape, a.dtype) for a in bufs], jax.ShapeDtypeStruct(token.shape, token.dtype)),
        in_specs=[HBM_SPEC] * len(bufs) + [pl.BlockSpec(memory_space=pl.ANY)] * len(after) + [vm] * len(carried),
        out_specs=(SEM_SPEC, SEM_SPEC, *[HBM_SPEC] * len(bufs), vm),
        input_output_aliases={i: 2 + i for i in range(len(bufs))},
        compiler_params=pltpu.CompilerParams(has_side_effects=DATAFLOW),
    )(*bufs, *after, *carried)
    return out[0], out[1], list(out[2:2 + ns]), list(out[2 + ns:2 + ns + nl]), out[-1]


def _split_wait(copies, send_sems, recv_sems, srcs, lands, after, *, name):
    ns, nl = len(srcs), len(lands)

    def body(*refs):
        for cp in copies(refs[:ns], refs[ns:ns + nl], refs[ns + nl], refs[ns + nl + 1]):
            cp.wait_send()
            cp.wait_recv()

    out = pl.pallas_call(
        body, name=name,
        out_shape=[pltpu.HBM(a.shape, a.dtype) for a in list(srcs) + list(lands)],
        in_specs=[HBM_SPEC] * (ns + nl) + [SEM_SPEC, SEM_SPEC] + [pl.BlockSpec(memory_space=pl.ANY)] * len(after),
        out_specs=[HBM_SPEC] * (ns + nl),
        input_output_aliases={i: i for i in range(ns + nl)},
        compiler_params=pltpu.CompilerParams(has_side_effects=DATAFLOW),
    )(*srcs, *lands, send_sems, recv_sems, *after)
    return list(out[:ns]), list(out[ns:])


ADAM_C1 = 1.0 / (1.0 - ADAM_B1 ** ADAM_STEP)
ADAM_C2 = 1.0 / (1.0 - ADAM_B2 ** ADAM_STEP)


def _adam_math(w, g, m, v):
    m2 = ADAM_B1 * m + (1.0 - ADAM_B1) * g
    v2 = ADAM_B2 * v + (1.0 - ADAM_B2) * (g * g)
    return -ADAM_LR * ((m2 * ADAM_C1) / (jnp.sqrt(v2 * ADAM_C2) + ADAM_EPS) + ADAM_WD * w), m2, v2


def _adamw(w, g, m, v, *, name, after=()):
    R, C = w.shape
    tr = R if R <= 512 else 256

    def body(w_ref, g_ref, m_ref, v_ref, *rest):
        d_ref, nm_ref, nv_ref = rest[len(after):]
        d_ref[...], nm_ref[...], nv_ref[...] = _adam_math(w_ref[...], g_ref[...], m_ref[...], v_ref[...])

    blk = pl.BlockSpec((tr, C), lambda i: (i, 0))
    return pl.pallas_call(
        body, name=name, grid=(R // tr,), in_specs=[blk] * 4 + [pl.BlockSpec(memory_space=pl.ANY)] * len(after),
        out_specs=[blk] * 3, out_shape=[jax.ShapeDtypeStruct((R, C), F32)] * 3,
        compiler_params=_params(("parallel",)),
    )(w, g, m, v, *after)


def _adamw_rs2(wmv, cs, direct, second, qidx, *, name):
    n = len(wmv)
    r, cc = wmv[0][0].shape
    h = r // 2

    def body(q_ref, *refs):
        ins, outs = refs[:6 * n], refs[6 * n:]
        for k in range(n):
            w_ref, m_ref, v_ref, c_ref, d1_ref, d2_ref = ins[6 * k:6 * k + 6]
            g_ref, d_ref, nm_ref, nv_ref = outs[4 * k:4 * k + 4]
            g = (c_ref[0].astype(F32) + d1_ref[0].astype(F32)) + d2_ref[0].astype(F32)
            g_ref[...] = g
            d_ref[...], nm_ref[...], nv_ref[...] = _adam_math(w_ref[...], g, m_ref[...], v_ref[...])

    blk = pl.BlockSpec((h, cc), lambda i, q_ref: (i, 0))
    one = [blk, blk, blk, pl.BlockSpec((1, h, cc), lambda i, q_ref: (q_ref[0], i, 0)),
           pl.BlockSpec((1, h, cc), lambda i, q_ref: (i, 0, 0)),
           pl.BlockSpec((1, h, cc), lambda i, q_ref: (1 - i, 0, 0))]
    out = pl.pallas_call(
        body, name=name,
        grid_spec=pltpu.PrefetchScalarGridSpec(num_scalar_prefetch=1, grid=(2,), in_specs=one * n,
                                               out_specs=[blk] * (4 * n)),
        out_shape=[jax.ShapeDtypeStruct((r, cc), F32)] * (4 * n),
        compiler_params=_params(("arbitrary",)),
    )(qidx, *[a for (w, m, v), c, d1, d2 in zip(wmv, cs, direct, second) for a in (w, m, v, c, d1, d2)])
    return [tuple(out[4 * k:4 * k + 4]) for k in range(n)]


def _adamw_rs(wmv, cs, rcv, qidx, *, name):
    n = len(wmv)
    shapes = [w.shape for w, _, _ in wmv]
    n_rcv = rcv[0].shape[0]
    halved = len(set(shapes)) == 1 and shapes[0][0] % 32 == 0 and shapes[0][0] > 128
    tiles = 2 if halved else 1

    def body(q_ref, *refs):
        ins, outs = refs[:5 * n], refs[5 * n:]
        for k in range(n):
            w_ref, m_ref, v_ref, c_ref, r_ref = ins[5 * k:5 * k + 5]
            g_ref, d_ref, nm_ref, nv_ref = outs[4 * k:4 * k + 4]
            g = c_ref[0].astype(F32)
            for j in range(n_rcv):
                g = g + r_ref[j].astype(F32)
            g_ref[...] = g
            d_ref[...], nm_ref[...], nv_ref[...] = _adam_math(w_ref[...], g, m_ref[...], v_ref[...])

    in_specs, out_specs = [], []
    for r, cc in shapes:
        blk = pl.BlockSpec((r // tiles, cc), lambda i, q_ref: (i, 0))
        in_specs += [blk, blk, blk, pl.BlockSpec((1, r // tiles, cc), lambda i, q_ref: (q_ref[0], i, 0)),
                     pl.BlockSpec((n_rcv, r // tiles, cc), lambda i, q_ref: (0, i, 0))]
        out_specs += [blk] * 4
    out = pl.pallas_call(
        body, name=name,
        grid_spec=pltpu.PrefetchScalarGridSpec(num_scalar_prefetch=1, grid=(tiles,), in_specs=in_specs,
                                               out_specs=out_specs),
        out_shape=[jax.ShapeDtypeStruct(s, F32) for s in shapes for _ in range(4)],
        compiler_params=_params(("arbitrary",)),
    )(qidx, *[a for (w, m, v), c, rc in zip(wmv, cs, rcv) for a in (w, m, v, c, rc)])
    return [tuple(out[4 * k:4 * k + 4]) for k in range(n)]


SMALL_PARAMS = ("norm_ffn1", "norm_mix", "norm_ffn2", "norm_final", "q_norm", "kv_norm", "sinks", "rel_bias", "b_mod")


def _adamw_small(gvec, wmv):
    shapes = [wmv[3 * i].shape for i in range(len(SMALL_PARAMS))]

    def body(*refs):
        g_all = refs[0]
        ins = refs[1:1 + 3 * len(SMALL_PARAMS)]
        outs = refs[1 + 3 * len(SMALL_PARAMS):]
        off = N_MODVEC
        for i, name in enumerate(SMALL_PARAMS):
            g_ref, d_ref, nm_ref, nv_ref = outs[4 * i:4 * i + 4]
            w_ref, m_ref, v_ref = ins[3 * i:3 * i + 3]
            start = 0 if name == "b_mod" else off
            rows, width = shapes[i]
            g = jnp.concatenate([g_all[:, start + width * r:start + width * (r + 1)] for r in range(rows)], axis=0)
            g_ref[...] = g
            d_ref[...], nm_ref[...], nv_ref[...] = _adam_math(w_ref[...], g, m_ref[...], v_ref[...])
            if name != "b_mod":
                off += dict(SMALL_LAYOUT)[name]

    vm = pl.BlockSpec(memory_space=pltpu.VMEM)
    n_out = 4 * len(SMALL_PARAMS)
    out = pl.pallas_call(
        body, name="adamw_small", in_specs=[vm] * (1 + len(wmv)), out_specs=[vm] * n_out,
        out_shape=[jax.ShapeDtypeStruct(shapes[i // 4], F32) for i in range(n_out)],
        compiler_params=_params(),
    )(gvec, *wmv)
    return {name: out[4 * i:4 * i + 4] for i, name in enumerate(SMALL_PARAMS)}


TRANSPOSED = ("g1T", "u1T", "g3T", "u3T", "w_inT", "w_uqT")


def kernel(x, c, w_mod, b_mod, norm_ffn1, ffn1_gate, ffn1_up, ffn1_down, norm_mix, w_in, q_norm, kv_norm, w_uq, w_ukv, sinks, w_o, norm_ffn2, ffn2_gate, ffn2_up, ffn2_down, rel_bias, norm_final, loss_target, m_w_mod, m_b_mod, m_norm_ffn1, m_ffn1_gate, m_ffn1_up, m_ffn1_down, m_norm_mix, m_w_in, m_q_norm, m_kv_norm, m_w_uq, m_w_ukv, m_sinks, m_w_o, m_norm_ffn2, m_ffn2_gate, m_ffn2_up, m_ffn2_down, m_rel_bias, m_norm_final, v_w_mod, v_b_mod, v_norm_ffn1, v_ffn1_gate, v_ffn1_up, v_ffn1_down, v_norm_mix, v_w_in, v_q_norm, v_kv_norm, v_w_uq, v_w_ukv, v_sinks, v_w_o, v_norm_ffn2, v_ffn2_gate, v_ffn2_up, v_ffn2_down, v_rel_bias, v_norm_final):
    mx, my, mc = _coords()
    cidx = jnp.reshape(mc, (1,)).astype(jnp.int32)
    qidx = jnp.reshape(2 * mx + my, (1,)).astype(jnp.int32)
    WM = w_mod.shape[2]

    shards = {"g1T": ffn1_gate[0].T, "u1T": ffn1_up[0].T, "d1": ffn1_down[0],
              "g3T": ffn2_gate[0].T, "u3T": ffn2_up[0].T, "d3": ffn2_down[0],
              "w_inT": w_in[0].T, "w_uqT": w_uq[0].T, "w_ukv": w_ukv[0], "w_o": w_o[0]}
    travels = lambda k: F32 if k == "w_inT" else BF16
    me = 4 * mx + 2 * my + mc
    groups = {"ffn1": ("g1T", "u1T", "d1"), "mixer": ("w_inT", "w_uqT", "w_ukv", "w_o"), "ffn2": ("g3T", "u3T", "d3")}
    arriving = {}
    later = groups["mixer"] + groups["ffn2"]
    place = {"mixer": 0, "ffn2": len(groups["mixer"])}

    c_tile = jnp.pad(c, ((0, 7), (0, 0)))
    b_mod3 = jnp.pad(b_mod.reshape(N_DEV, 1, WM), ((0, 0), (0, 7), (0, 0)))
    gathered_ffn1, mod3, ca, zones = _wgather([shards[k] for k in groups["ffn1"]],
                                              [(shards[k], travels(k)) for k in later], c_tile, w_mod[0], b_mod3)
    mod9 = mod3[:, 0, :].reshape(N_MOD, D)

    def as_weights(group, gathered):
        return {k: g if k == "w_ukv" else g.reshape(N_DEV * g.shape[1], g.shape[2])
                for k, g in zip(groups[group], gathered)}

    def start_gather(carry):
        batches = [range(k0, k0 + len(groups[group])) for group, k0 in place.items()]
        send, recv, lands, started = _gather_start(zones, name="gather_start", batches=batches, carry=carry)
        for group, k0 in place.items():
            arriving[group] = (send, recv, lands[k0:k0 + len(groups[group])])
        return started

    def fetch(group, after, vecs):
        if group == "ffn1":
            return as_weights("ffn1", gathered_ffn1), start_gather(vecs)

        def pass_on(group, after, carry=None):
            send, recv, lands = arriving[group]
            lands, token = _gather_pass(send, recv, lands, after, name="gather_landed_" + group, stage="landed",
                                        k0=place[group])
            lands, token = _gather_pass(send, recv, lands, [token], name="gather_onward_" + group, stage="onward",
                                        k0=place[group], carry=carry)
            arriving[group] = (send, recv, lands)
            return token

        if group == "ffn2_on_its_way":
            return None, pass_on("ffn2", after, vecs)
        if group == "mixer":
            after = [pass_on("mixer", after)]
        send, recv, lands = arriving[group]
        return as_weights(group, _gather_end(send, recv, lands, after, name="gather_end_" + group,
                                             k0=place[group])), vecs

    norms ={"ffn1": norm_ffn1, "mix": norm_mix, "ffn2": norm_ffn2, "final": norm_final.reshape(1, D)}
    in_flight = {}

    def on_grads(group, g, after, vecs):
        if group != "ffn1":
            if g is None:
                return vecs
            names = list(g)
            by_dest = [g[k] if k == "w_ukv" else g[k].reshape((N_DEV, g[k].shape[0] // N_DEV) + g[k].shape[1:])
                       for k in names]
            lands = [lax.empty((N_DEV - 1,) + a.shape[1:], a.dtype) for a in by_dest]
            send, recv, by_dest, lands, token = _split_start(_direct_copies, by_dest, lands, 7 * len(names), after,
                                                             name="rs_start_" + group, carry=vecs)
            in_flight[group] = (names, send, recv, by_dest, lands, token)
            return token
        names = list(g)
        by_dest = [g[k].reshape((N_DEV, g[k].shape[0] // N_DEV) + g[k].shape[1:]) for k in names]
        lands = [lax.empty((4,) + a.shape[1:], a.dtype) for a in by_dest]
        send, recv, by_dest, lands, token = _split_start(_d2d_copies, by_dest, lands, 4 * len(names), after,
                                                         name="rs_d2d_start_" + group)
        finish("mixer", [token])
        by_dest, from_sib = _split_wait(_d2d_copies, send, recv, by_dest, lands, [done[-1]], name="rs_d2d_wait_" + group)
        sums = _chipsum(by_dest, from_sib, cidx, name="chipsum_" + group)
        halves = lambda: [lax.empty((2, s.shape[1] // 2) + s.shape[2:], s.dtype) for s in sums]
        send, recv, sums, lands, token = _split_start(_rs_step1_copies, sums, halves() + halves(), 4 * len(names), [],
                                                      name="rs_ici_start_" + group, carry=vecs)
        in_flight[group] = (names, send, recv, sums, lands, token)
        return token

    owners = {"g1T": ("ffn1_gate", ffn1_gate, m_ffn1_gate, v_ffn1_gate), "u1T": ("ffn1_up", ffn1_up, m_ffn1_up, v_ffn1_up),
              "d1": ("ffn1_down", ffn1_down, m_ffn1_down, v_ffn1_down),
              "g3T": ("ffn2_gate", ffn2_gate, m_ffn2_gate, v_ffn2_gate), "u3T": ("ffn2_up", ffn2_up, m_ffn2_up, v_ffn2_up),
              "d3": ("ffn2_down", ffn2_down, m_ffn2_down, v_ffn2_down),
              "w_inT": ("w_in", w_in, m_w_in, v_w_in), "w_uqT": ("w_uq", w_uq, m_w_uq, v_w_uq),
              "w_ukv": ("w_ukv", w_ukv, m_w_ukv, v_w_ukv), "w_o": ("w_o", w_o, m_w_o, v_w_o)}
    res, done = {}, []

    there = lambda k, a: a[0].T if k in TRANSPOSED else a[0]
    back = lambda k, a: a.T[None] if k in TRANSPOSED else a[None]

    def record(names, outs):
        for k, out in zip(names, outs):
            done.append(out[3])
            res[owners[k][0]] = tuple(back(k, a) for a in out)

    def finish(group, after):
        names, send, recv, sums, lands, _ = in_flight[group]
        wmv = [tuple(there(k, a) for a in owners[k][1:]) for k in names]
        own = jnp.reshape(me, (1,)).astype(jnp.int32)
        sums, lands = _split_wait(_direct_copies, send, recv, sums, lands, after, name="rs_wait_" + group)
        record(names, _adamw_rs(wmv, sums, lands, own, name="adamw_" + group))

    _, grad_x, _, vec = _local_step(
        x[0], loss_target[0], mod9, norms, sinks, rel_bias, q_norm, kv_norm, fetch, on_grads=on_grads)

    names, send, recv, sums, lands, step1_started = in_flight["ffn1"]
    vec = vec.reshape(1, 1, N_VEC)
    allvec = lax.dynamic_update_slice(lax.empty((N_DEV, 1, N_VEC), F32), vec, (me, 0, 0))
    vsend, vrecv, _, (allvec,), vec_started = _split_start(_vec_copies, [], [allvec], N_DEV - 1, [step1_started],
                                                           name="vec_start")
    finish("ffn2", [vec_started])
    n = len(names)
    sums, lands = _split_wait(_rs_step1_copies, send, recv, sums, lands, [done[-1]], name="rs_ici_wait_ffn1")
    direct, relay = lands[:n], lands[n:]
    qxy = jnp.stack([2 * (1 - mx) + my, 2 * mx + (1 - my)]).astype(jnp.int32)
    relayed = _relay_sum(sums, relay, qxy, name="relay_sum_ffn1")
    second = [lax.empty(a.shape, a.dtype) for a in relayed]
    send, recv, relayed, second, step2_started = _split_start(_rs_step2_copies, relayed, second, 2 * n, [],
                                                              name="rs_ici_start2_ffn1")

    _, (allvec,) = _split_wait(_vec_copies, vsend, vrecv, [], [allvec], [step2_started], name="vec_wait")
    g_wmod, gvec = _mod_bwd(allvec, ca, jnp.reshape(me, (1,)).astype(jnp.int32))
    loss = gvec[0, N_MODVEC + LOSS_SLOT]
    small_in = {"norm_ffn1": (norm_ffn1, m_norm_ffn1, v_norm_ffn1), "norm_mix": (norm_mix, m_norm_mix, v_norm_mix),
                "norm_ffn2": (norm_ffn2, m_norm_ffn2, v_norm_ffn2), "norm_final": (norm_final, m_norm_final, v_norm_final),
                "q_norm": (q_norm, m_q_norm, v_q_norm), "kv_norm": (kv_norm, m_kv_norm, v_kv_norm),
                "sinks": (sinks, m_sinks, v_sinks), "rel_bias": (rel_bias, m_rel_bias, v_rel_bias),
                "b_mod": (b_mod, m_b_mod, v_b_mod)}
    as_row = lambda k, a: a.T if k == "rel_bias" else a.reshape(1, -1)
    from_row = lambda k, a: a.T if k == "rel_bias" else a.reshape(small_in[k][0].shape)
    small_out = _adamw_small(gvec, [as_row(k, a) for k in SMALL_PARAMS for a in small_in[k]])
    for k in SMALL_PARAMS:
        res[k] = tuple(from_row(k, a) for a in small_out[k])

    out = _adamw(w_mod[0], g_wmod, m_w_mod[0], v_w_mod[0], name="adamw_w_mod")
    res["w_mod"] = tuple(a[None] for a in (g_wmod,) + tuple(out))

    wmv = [tuple(there(k, a) for a in owners[k][1:]) for k in names]
    after = done + [out[2]] + [a for k in SMALL_PARAMS for a in res[k]]
    outs = []
    for i, k in enumerate(names):
        _, (sec,) = _split_wait(functools.partial(_rs_step2_copies, k0=i), send, recv, [relayed[i]], [second[i]],
                                after, name="rs_ici_wait2_" + k)
        outs.append(_adamw_rs2([wmv[i]], [sums[i]], [direct[i]], [sec], qidx, name="adamw_" + owners[k][0])[0])
        after = [outs[-1][3]]
    record(names, outs)

    order = ("w_mod", "b_mod", "norm_ffn1", "ffn1_gate", "ffn1_up", "ffn1_down", "norm_mix", "w_in", "q_norm",
             "kv_norm", "w_uq", "w_ukv", "sinks", "w_o", "norm_ffn2", "ffn2_gate", "ffn2_up", "ffn2_down",
             "rel_bias", "norm_final")
    return (loss, grad_x[None]) + tuple(res[nm][kind] for kind in range(4) for nm in order)
```

```python
import functools
import math

import numpy as np
import jax
import jax.numpy as jnp
from jax import lax
from jax.experimental import pallas as pl
from jax.experimental.pallas import tpu as pltpu

F32 = jnp.float32
BF16 = jnp.bfloat16
MESH = pl.DeviceIdType.MESH

N_DEV = 8
D = 1024
D_FF = 2816
EPS = 1e-6
N_MOD = 9
SWA_HEADS = 8
SWA_DH = 64
WINDOW = 128
MLA_HEADS = 4
MLA_NOPE = 128
MLA_ROPE = 64
MLA_V = 128
MLA_QR = 256
MLA_KVR = 128
ROPE_THETA = 10000.0
NUM_BUCKETS = 32
D_IN = 1216
D_IN_PAD = 1280
SWA_SCALE = SWA_DH ** -0.5
MLA_SCALE = (MLA_NOPE + MLA_ROPE) ** -0.5

ADAM_LR = 0.001
ADAM_B1 = 0.9
ADAM_B2 = 0.999
ADAM_EPS = 1e-08
ADAM_WD = 0.01
ADAM_STEP = 10

V7X_VMEM_LIMIT = 56 * 1024 * 1024
ROW_TILE = 512

NT_DIMS = (((1,), (1,)), ((), ()))
TN_DIMS = (((0,), (0,)), ((), ()))


def _dot(a, b):
    return jnp.dot(a, b, preferred_element_type=F32)


def _dot_nt(a, b):
    return lax.dot_general(a, b, NT_DIMS, preferred_element_type=F32)


def _dot_tn(a, b):
    return lax.dot_general(a, b, TN_DIMS, preferred_element_type=F32)


def _params(sem=None):
    return pltpu.CompilerParams(dimension_semantics=sem, vmem_limit_bytes=V7X_VMEM_LIMIT)


def _rstd(x):
    return lax.rsqrt(jnp.mean(x * x, axis=-1, keepdims=True) + EPS)


def _rms_bwd(dy, xhat, r):
    return r * (dy - xhat * jnp.mean(dy * xhat, axis=-1, keepdims=True))


def _sigmoid(a):
    return 1.0 / (1.0 + jnp.exp(-a))


def _ffn_fwd(x, vecs, wgT, wuT, wd, *, name, tm=256, tf=D_FF):
    S, F = x.shape[0], wd.shape[0]
    tm = min(tm, S)
    ni, nj = S // tm, F // tf

    def body(x_ref, vec_ref, wg_ref, wu_ref, wd_ref, xo_ref, h_ref, a_ref, b_ref, f_ref, acc_ref):
        j = pl.program_id(1)

        @pl.when(j == 0)
        def _():
            xv = x_ref[...]
            hn = xv * _rstd(xv) * vec_ref[0:1, :]
            h_ref[...] = (hn * (1.0 + vec_ref[2:3, :]) + vec_ref[1:2, :]).astype(BF16)

        h = h_ref[...]
        a = _dot_nt(h, wg_ref[...])
        b = _dot_nt(h, wu_ref[...])
        a_ref[...] = a.astype(BF16)
        b_ref[...] = b.astype(BF16)
        part = _dot((a * _sigmoid(a) * b).astype(BF16), wd_ref[...])

        def finish(f):
            f_ref[...] = f
            xo_ref[...] = x_ref[...] + (0.5 * vec_ref[3:4, :]) * f

        if nj == 1:
            finish(part)
        else:
            @pl.when(j == 0)
            def _():
                acc_ref[...] = part

            @pl.when((j > 0) & (j < nj - 1))
            def _():
                acc_ref[...] += part

            @pl.when(j == nj - 1)
            def _():
                finish(acc_ref[...] + part)

    row = pl.BlockSpec((tm, D), lambda i, j: (i, 0))
    wspec = pl.BlockSpec((tf, D), lambda i, j: (j, 0), pipeline_mode=pl.Buffered(1) if nj == 1 else None)
    act = pl.BlockSpec((tm, tf), lambda i, j: (i, j))
    return pl.pallas_call(
        body, name=name, grid=(ni, nj),
        in_specs=[row, pl.BlockSpec((8, D), lambda i, j: (0, 0)), wspec, wspec, wspec],
        out_specs=[row, row, act, act, row],
        out_shape=[jax.ShapeDtypeStruct((S, D), F32), jax.ShapeDtypeStruct((S, D), BF16),
                   jax.ShapeDtypeStruct((S, F), BF16), jax.ShapeDtypeStruct((S, F), BF16),
                   jax.ShapeDtypeStruct((S, D), F32)],
        scratch_shapes=[pltpu.VMEM((tm, D) if nj > 1 else (8, 128), F32)],
        compiler_params=_params(("parallel", "arbitrary")),
    )(x, vecs, wgT, wuT, wd)


def _ffn_bwd_main(h, df, a, b, wgT, wuT, wd, *, name, after=(), tm=2048, tf=256):
    S = h.shape[0]
    tm = min(tm, S)
    ni, nj = S // tm, D_FF // tf

    def body(h_hbm, df_hbm, a_ref, b_ref, wg_ref, wu_ref, wd_ref, *rest):
        gg_ref, gu_ref, gd_ref, dh_hbm, h_v, df_v, dh_v, gg_acc, gu_acc, gd_acc, sem = rest[len(after):]
        j = pl.program_id(0)
        i = pl.program_id(1)

        @pl.when((j == 0) & (i == 0))
        def _():
            c1 = pltpu.make_async_copy(h_hbm, h_v, sem.at[0])
            c2 = pltpu.make_async_copy(df_hbm, df_v, sem.at[1])
            c1.start()
            c2.start()
            c1.wait()
            c2.wait()

        @pl.when(i == 0)
        def _():
            gg_acc[...] = jnp.zeros_like(gg_acc)
            gu_acc[...] = jnp.zeros_like(gu_acc)
            gd_acc[...] = jnp.zeros_like(gd_acc)

        rows = pl.ds(pl.multiple_of(i * tm, tm), tm)
        hi = h_v[rows, :]
        dfi = df_v[rows, :]
        av = a_ref[...].astype(F32)
        bv = b_ref[...].astype(F32)
        sg = _sigmoid(av)
        sa = av * sg
        hsw = (sa * bv).astype(BF16)
        dhsw = _dot_nt(dfi, wd_ref[...])
        da = (dhsw * bv * (sg * (1.0 + av * (1.0 - sg)))).astype(BF16)
        db = (dhsw * sa).astype(BF16)
        gd_acc[...] += _dot_tn(hsw, dfi)
        gg_acc[...] += _dot_tn(da, hi)
        gu_acc[...] += _dot_tn(db, hi)
        dh = _dot(da, wg_ref[...]) + _dot(db, wu_ref[...])

        @pl.when(j == 0)
        def _():
            dh_v[rows, :] = dh

        @pl.when(j > 0)
        def _():
            dh_v[rows, :] += dh

        @pl.when(i == ni - 1)
        def _():
            gg_ref[...] = gg_acc[...].astype(BF16)
            gu_ref[...] = gu_acc[...].astype(BF16)
            gd_ref[...] = gd_acc[...].astype(BF16)

        @pl.when((j == nj - 1) & (i == ni - 1))
        def _():
            c3 = pltpu.make_async_copy(dh_v, dh_hbm, sem.at[2])
            c3.start()
            c3.wait()

    anyspec = pl.BlockSpec(memory_space=pl.ANY)
    wspec = pl.BlockSpec((tf, D), lambda j, i: (j, 0))
    act = pl.BlockSpec((tm, tf), lambda j, i: (i, j))
    return pl.pallas_call(
        body, name=name, grid=(nj, ni),
        in_specs=[anyspec, anyspec, act, act, wspec, wspec, wspec] + [anyspec] * len(after),
        out_specs=[wspec, wspec, wspec, anyspec],
        out_shape=[jax.ShapeDtypeStruct((D_FF, D), BF16)] * 3 + [jax.ShapeDtypeStruct((S, D), F32)],
        scratch_shapes=[pltpu.VMEM((S, D), BF16), pltpu.VMEM((S, D), BF16), pltpu.VMEM((S, D), F32),
                        pltpu.VMEM((tf, D), F32), pltpu.VMEM((tf, D), F32), pltpu.VMEM((tf, D), F32),
                        pltpu.SemaphoreType.DMA((3,))],
        compiler_params=_params(("arbitrary", "arbitrary")),
    )(h, df, a, b, wgT, wuT, wd, *after)


def _ffn_out_bwd(dx, f, gate, df_ref, part_ref):
    df_ref[...] = ((0.5 * gate) * dx).astype(BF16)
    part_ref[3:4, :] += 0.5 * jnp.sum(dx * f, axis=0, keepdims=True)


def _norm_bwd(dh, x, dxo, vecs, *, name, below=None, tm=ROW_TILE):
    S = x.shape[0]
    tm = min(tm, S)

    def body(dh_ref, x_ref, dxo_ref, vec_ref, *rest):
        dx_ref, part_ref = rest[-2 if below is None else -3], rest[-1 if below is None else -2]

        @pl.when(pl.program_id(0) == 0)
        def _():
            part_ref[...] = jnp.zeros_like(part_ref)

        dh = dh_ref[...]
        xv = x_ref[...]
        r = _rstd(xv)
        xhat = xv * r
        w = vec_ref[0:1, :]
        xn = xhat * w
        dxn = dh * (1.0 + vec_ref[2:3, :])
        part_ref[0:1, :] += jnp.sum(dxn * xhat, axis=0, keepdims=True)
        part_ref[1:2, :] += jnp.sum(dh, axis=0, keepdims=True)
        part_ref[2:3, :] += jnp.sum(dh * xn, axis=0, keepdims=True)
        dx = dxo_ref[...] + _rms_bwd(dxn * w, xhat, r)
        dx_ref[...] = dx
        if below is not None:
            _ffn_out_bwd(dx, rest[0][...], rest[1][3:4, :], rest[-1], part_ref)

    row = pl.BlockSpec((tm, D), lambda i: (i, 0))
    vec = pl.BlockSpec((8, D), lambda i: (0, 0))
    extra = [] if below is None else [row, vec]
    return pl.pallas_call(
        body, name=name, grid=(S // tm,), in_specs=[row, row, row, vec] + extra,
        out_specs=[row, vec] + ([] if below is None else [row]),
        out_shape=[jax.ShapeDtypeStruct((S, D), F32), jax.ShapeDtypeStruct((8, D), F32)]
        + ([] if below is None else [jax.ShapeDtypeStruct((S, D), BF16)]),
        compiler_params=_params(("arbitrary",)),
    )(dh, x, dxo, vecs, *([] if below is None else below))


def _head(x, tgt, nf, f, vecs, *, tm=ROW_TILE):
    S = x.shape[0]
    tm = min(tm, S)

    def body(x_ref, t_ref, nf_ref, f_ref, vec_ref, dx_ref, part_ref, df_ref):
        @pl.when(pl.program_id(0) == 0)
        def _():
            part_ref[...] = jnp.zeros_like(part_ref)

        xv = x_ref[...]
        r = _rstd(xv)
        xhat = xv * r
        w = nf_ref[...]
        e = xhat * w - t_ref[...]
        dy = e * (1.0 / D)
        part_ref[0:1, :] += jnp.sum(dy * xhat, axis=0, keepdims=True)
        part_ref[1:2, :] += jnp.sum(e * e) * (0.5 / D)
        dx = _rms_bwd(dy * w, xhat, r)
        dx_ref[...] = dx
        _ffn_out_bwd(dx, f_ref[...], vec_ref[3:4, :], df_ref, part_ref)

    row = pl.BlockSpec((tm, D), lambda i: (i, 0))
    vec = pl.BlockSpec((8, D), lambda i: (0, 0))
    return pl.pallas_call(
        body, name="head", grid=(S // tm,),
        in_specs=[row, row, pl.BlockSpec((1, D), lambda i: (0, 0)), row, vec],
        out_specs=[row, vec, row],
        out_shape=[jax.ShapeDtypeStruct((S, D), F32), jax.ShapeDtypeStruct((8, D), F32),
                   jax.ShapeDtypeStruct((S, D), BF16)],
        compiler_params=_params(("arbitrary",)),
    )(x, tgt, nf, f, vecs)


def _mix_in_fwd(x, vecs, w_inT, *, tm=ROW_TILE):
    S = x.shape[0]
    tm = min(tm, S)

    def body(x_ref, vec_ref, w_ref, h_ref, p_ref, wb_ref):
        @pl.when(pl.program_id(0) == 0)
        def _():
            wb_ref[0:D_IN, :] = w_ref[...].astype(BF16)
            wb_ref[D_IN:D_IN_PAD, :] = jnp.zeros((D_IN_PAD - D_IN, D), BF16)

        xv = x_ref[...]
        hn = xv * _rstd(xv) * vec_ref[0:1, :]
        h = (hn * (1.0 + vec_ref[2:3, :]) + vec_ref[1:2, :]).astype(BF16)
        h_ref[...] = h
        p_ref[...] = _dot_nt(h, wb_ref[...])

    row = pl.BlockSpec((tm, D), lambda i: (i, 0))
    return pl.pallas_call(
        body, name="mix_in_fwd", grid=(S // tm,),
        in_specs=[row, pl.BlockSpec((8, D), lambda i: (0, 0)),
                  pl.BlockSpec((D_IN, D), lambda i: (0, 0), pipeline_mode=pl.Buffered(1))],
        out_specs=[row, pl.BlockSpec((tm, D_IN_PAD), lambda i: (i, 0)), pl.BlockSpec((D_IN_PAD, D), lambda i: (0, 0))],
        out_shape=[jax.ShapeDtypeStruct((S, D), BF16), jax.ShapeDtypeStruct((S, D_IN_PAD), F32),
                   jax.ShapeDtypeStruct((D_IN_PAD, D), BF16)],
        compiler_params=_params(("arbitrary",)),
    )(x, vecs, w_inT)


def _bucket_table():
    qi = np.arange(WINDOW)[:, None]
    kj = np.arange(2 * WINDOW)[None, :]
    dist = qi + WINDOW - kj
    max_exact = NUM_BUCKETS // 2
    n = np.maximum(dist, 0)
    nf = np.maximum(n, 1).astype(np.float32)
    large = max_exact + (np.log(nf / np.float32(max_exact)) / np.float32(math.log(WINDOW / max_exact))
                         * np.float32(NUM_BUCKETS - max_exact)).astype(np.int32)
    large = np.minimum(large, NUM_BUCKETS - 1)
    return np.where(n < max_exact, n, large).astype(np.int32)


SWA_GROUP = 4
GROUP_ROWS = SWA_GROUP * WINDOW


SWA_SUB = 2


def _swa_valid(has_prev):
    row = lax.broadcasted_iota(jnp.int32, (GROUP_ROWS, 2 * WINDOW), 0) % WINDOW
    col = lax.broadcasted_iota(jnp.int32, (GROUP_ROWS, 2 * WINDOW), 1)
    dist = row + WINDOW - col
    return (dist >= 0) & (dist < WINDOW) & ((col >= WINDOW) | has_prev)


def _swa_keys(prev_ref, cur_ref, u):
    cur = cur_ref[...]
    before = prev_ref[...] if u == 0 else cur[WINDOW * (u - 1):WINDOW * u]
    return jnp.concatenate([before, cur[WINDOW * u:WINDOW * (u + 1)]], axis=0).astype(BF16)


def _stack_heads(x, g):
    return jnp.concatenate([x[:, 64 * h:64 * h + 64] for h in range(SWA_GROUP * g, SWA_GROUP * (g + 1))], axis=0)


def _unstack_heads(x4):
    return jnp.concatenate([x4[WINDOW * a:WINDOW * (a + 1)] for a in range(SWA_GROUP)], axis=1)


def _group_sinks(sink_ref, g):
    head = lax.broadcasted_iota(jnp.int32, (GROUP_ROWS, 1), 0) // WINDOW
    out = jnp.full((GROUP_ROWS, 1), sink_ref[0, SWA_GROUP * g], F32)
    for a in range(1, SWA_GROUP):
        out = jnp.where(head == a, sink_ref[0, SWA_GROUP * g + a], out)
    return out


def _swa_probs(qh, kk, bias_h, sink, valid):
    s = _dot_nt(qh, kk) * SWA_SCALE + bias_h
    s = jnp.where(valid, s, -jnp.inf)
    m = jnp.maximum(jnp.max(s, axis=-1, keepdims=True), sink)
    p = jnp.exp(s - m)
    ps = jnp.exp(sink - m)
    inv = 1.0 / (jnp.sum(p, axis=-1, keepdims=True) + ps)
    return p * inv, ps * inv


SWA_ROWS = SWA_SUB * WINDOW


def _swa_specs():
    prev = lambda n: jnp.maximum(SWA_SUB * n - 1, 0)
    return [pl.BlockSpec((SWA_ROWS, 512), lambda n: (n, 0)),
            pl.BlockSpec((SWA_ROWS, 128), lambda n: (n, 4)),
            pl.BlockSpec((WINDOW, 128), lambda n: (prev(n), 4)),
            pl.BlockSpec((SWA_ROWS, 128), lambda n: (n, 5)),
            pl.BlockSpec((WINDOW, 128), lambda n: (prev(n), 5)),
            pl.BlockSpec((SWA_HEADS, WINDOW, 2 * WINDOW), lambda n: (0, 0, 0)),
            pl.BlockSpec(memory_space=pltpu.SMEM)]


def _swa_fwd(proj, rel_bias, bucket, sinks):
    S = proj.shape[0]

    def body(q_ref, kc_ref, kp_ref, vc_ref, vp_ref, rb_ref, sink_ref, bk_ref, o_ref, bias_ref):
        n = pl.program_id(0)

        @pl.when(n == 0)
        def _():
            bk = bk_ref[...]
            for h in range(SWA_HEADS):
                acc = jnp.zeros((WINDOW, 2 * WINDOW), F32)
                for b in range(NUM_BUCKETS):
                    acc = jnp.where(bk == b, rb_ref[b, h], acc)
                bias_ref[h] = acc

        for u in range(SWA_SUB):
            rows = slice(WINDOW * u, WINDOW * (u + 1))
            valid = _swa_valid(n > 0 if u == 0 else True)
            q = q_ref[rows, :].astype(BF16)
            kfull = _swa_keys(kp_ref, kc_ref, u)
            vfull = _swa_keys(vp_ref, vc_ref, u)
            for g in range(SWA_HEADS // SWA_GROUP):
                kk = kfull[:, 64 * g:64 * g + 64]
                vv = vfull[:, 64 * g:64 * g + 64]
                bias4 = bias_ref[SWA_GROUP * g:SWA_GROUP * (g + 1)].reshape(GROUP_ROWS, 2 * WINDOW)
                pk, _ = _swa_probs(_stack_heads(q, g), kk, bias4, _group_sinks(sink_ref, g), valid)
                o_ref[rows, 256 * g:256 * (g + 1)] = _unstack_heads(_dot(pk.astype(BF16), vv))

    specs = _swa_specs()
    whole = pl.BlockSpec((SWA_HEADS, WINDOW, 2 * WINDOW), lambda n: (0, 0, 0))
    return pl.pallas_call(
        body, name="swa_fwd", grid=(S // SWA_ROWS,),
        in_specs=specs[:5] + [pl.BlockSpec(memory_space=pltpu.SMEM), specs[6],
                              pl.BlockSpec((WINDOW, 2 * WINDOW), lambda n: (0, 0))],
        out_specs=[pl.BlockSpec((SWA_ROWS, 512), lambda n: (n, 0)), whole],
        out_shape=[jax.ShapeDtypeStruct((S, 512), F32), jax.ShapeDtypeStruct((SWA_HEADS, WINDOW, 2 * WINDOW), F32)],
        compiler_params=_params(("arbitrary",)),
    )(proj, proj, proj, proj, proj, rel_bias, sinks, bucket)


def _swa_bwd(proj, bias, sinks, o, do, bucket):
    S = proj.shape[0]
    nb = S // SWA_ROWS

    def body(q_ref, kc_ref, kp_ref, vc_ref, vp_ref, bias_ref, sink_ref, o_ref, do_ref, bk_ref,
             dq_ref, dk_ref, dv_ref, drb_ref, dsk_ref, dbias_acc):
        n = pl.program_id(0)

        @pl.when(n == 0)
        def _():
            dk_ref[...] = jnp.zeros_like(dk_ref)
            dv_ref[...] = jnp.zeros_like(dv_ref)
            dsk_ref[...] = jnp.zeros_like(dsk_ref)
            dbias_acc[...] = jnp.zeros_like(dbias_acc)
            drb_ref[...] = jnp.zeros_like(drb_ref)

        for u in range(SWA_SUB):
            rows = slice(WINDOW * u, WINDOW * (u + 1))
            blk = SWA_SUB * n + u
            valid = _swa_valid(n > 0 if u == 0 else True)
            q = q_ref[rows, :].astype(BF16)
            dov = do_ref[rows, :]
            ov = o_ref[rows, :]
            kfull = _swa_keys(kp_ref, kc_ref, u)
            vfull = _swa_keys(vp_ref, vc_ref, u)
            prow = pl.ds(pl.multiple_of(jnp.maximum(blk - 1, 0) * WINDOW, WINDOW), WINDOW)
            crow = pl.ds(pl.multiple_of(blk * WINDOW, WINDOW), WINDOW)
            for g in range(SWA_HEADS // SWA_GROUP):
                heads = slice(SWA_GROUP * g, SWA_GROUP * (g + 1))
                kk = kfull[:, 64 * g:64 * g + 64]
                vv = vfull[:, 64 * g:64 * g + 64]
                q4 = _stack_heads(q, g)
                pk, psink = _swa_probs(q4, kk, bias_ref[heads].reshape(GROUP_ROWS, 2 * WINDOW),
                                       _group_sinks(sink_ref, g), valid)
                pkb = pk.astype(BF16)
                do4 = _stack_heads(dov, g)
                dob = do4.astype(BF16)
                dp = _dot_nt(dob, vv)
                delta = jnp.sum(do4 * _stack_heads(ov, g), axis=-1, keepdims=True)
                ds = pk * (dp - delta)
                dsink = -psink * delta
                for a in range(SWA_GROUP):
                    h = SWA_GROUP * g + a
                    part = jnp.sum(dsink[WINDOW * a:WINDOW * (a + 1)], keepdims=True)
                    dsk_ref[h:h + 1, :] += jnp.broadcast_to(part, (1, 128))
                dbias_acc[heads] += ds.reshape(SWA_GROUP, WINDOW, 2 * WINDOW)
                dsb = (ds * SWA_SCALE).astype(BF16)
                dq_ref[rows, 256 * g:256 * (g + 1)] = _unstack_heads(_dot(dsb, kk))
                dkk = _dot_tn(dsb, q4)
                dvv = _dot_tn(pkb, dob)
                dk_ref[prow, 64 * g:64 * g + 64] += dkk[:WINDOW]
                dk_ref[crow, 64 * g:64 * g + 64] += dkk[WINDOW:]
                dv_ref[prow, 64 * g:64 * g + 64] += dvv[:WINDOW]
                dv_ref[crow, 64 * g:64 * g + 64] += dvv[WINDOW:]

        @pl.when(n == nb - 1)
        def _():
            bk = bk_ref[...]
            for h in range(SWA_HEADS):
                dbh = dbias_acc[h]
                for b in range(NUM_BUCKETS):
                    val = jnp.sum(jnp.where(bk == b, dbh, 0.0), keepdims=True)
                    row = h * NUM_BUCKETS + b
                    drb_ref[row:row + 1, :] = jnp.broadcast_to(val, (1, 128))

    full = lambda shape: pl.BlockSpec(shape, lambda n: tuple(0 for _ in shape))
    return pl.pallas_call(
        body, name="swa_bwd", grid=(nb,),
        in_specs=_swa_specs() + [pl.BlockSpec((SWA_ROWS, 512), lambda n: (n, 0)),
                                 pl.BlockSpec((SWA_ROWS, 512), lambda n: (n, 0)), full((WINDOW, 2 * WINDOW))],
        out_specs=[pl.BlockSpec((SWA_ROWS, 512), lambda n: (n, 0)), full((S, 128)), full((S, 128)),
                   full((NUM_BUCKETS * 8, 128)), full((8, 128))],
        out_shape=[jax.ShapeDtypeStruct((S, 512), F32), jax.ShapeDtypeStruct((S, 128), F32),
                   jax.ShapeDtypeStruct((S, 128), F32), jax.ShapeDtypeStruct((NUM_BUCKETS * 8, 128), F32),
                   jax.ShapeDtypeStruct((8, 128), F32)],
        scratch_shapes=[pltpu.VMEM((SWA_HEADS, WINDOW, 2 * WINDOW), F32)],
        compiler_params=_params(("arbitrary",)),
    )(proj, proj, proj, proj, proj, bias, sinks, o, do, bucket)


def _rope_tables(S):
    inv = np.float32(ROPE_THETA) ** (-np.arange(0, MLA_ROPE, 2, dtype=np.float32) / np.float32(MLA_ROPE))
    ang = np.arange(S, dtype=np.float32)[:, None] * inv[None, :]
    cos, sin = np.cos(ang), np.sin(ang)
    return (jnp.asarray(np.tile(np.concatenate([cos, cos], axis=1), (1, 2))),
            jnp.asarray(np.tile(np.concatenate([-sin, sin], axis=1), (1, 2))))


def _rope_wide(ref):
    t = ref[...]
    return jnp.concatenate([t, t], axis=1)


def _swap_halves(x):
    w = x.shape[-1]
    lane = lax.broadcasted_iota(jnp.int32, x.shape, x.ndim - 1)
    return jnp.where((lane % 64) < 32, pltpu.roll(x, w - 32, x.ndim - 1), pltpu.roll(x, 32, x.ndim - 1))


def _uq_group_rows(wuq_ref):
    per = MLA_NOPE + MLA_ROPE
    nope = [wuq_ref[per * h:per * h + MLA_NOPE, :] for h in range(MLA_HEADS)]
    rope = [wuq_ref[per * h + MLA_NOPE:per * (h + 1), :] for h in range(MLA_HEADS)]
    return jnp.concatenate(nope + rope, axis=0)


def _mla_pre_fwd(proj, qn_w, kvn_w, wuqT, wukv, cos, sin, *, tm=ROW_TILE):
    S = proj.shape[0]
    tm = min(tm, S)

    def body(ql_ref, kl_ref, kr_ref, qw_ref, kw_ref, wuq_ref, wukv_ref, cos_ref, sin_ref,
             qc_ref, kc_ref, vv_ref):
        ql = ql_ref[...]
        qn = (ql * _rstd(ql) * qw_ref[...]).astype(BF16)
        q = _dot_nt(qn, _uq_group_rows(wuq_ref))
        cs, sn = _rope_wide(cos_ref), _rope_wide(sin_ref)
        qr = q[:, 512:768]
        qr = qr * cs + _swap_halves(qr) * sn
        half = lax.broadcasted_iota(jnp.int32, (tm, 128), 1) // 64
        kl = kl_ref[...]
        kvn = (kl * _rstd(kl) * kw_ref[...]).astype(BF16)
        kr = kr_ref[...]
        kr = kr * cs[:, :128] + _swap_halves(kr) * sn[:, :128]
        kr2 = (kr + pltpu.roll(kr, 64, 1)).astype(BF16)
        kv = _dot(kvn, jnp.concatenate([wukv_ref[j] for j in range(2 * MLA_HEADS)], axis=1)).astype(BF16)
        for h in range(MLA_HEADS):
            qc_ref[h, :, 0:128] = q[:, 128 * h:128 * h + 128].astype(BF16)
            chunk = qr[:, 128 * (h // 2):128 * (h // 2) + 128]
            qc_ref[h, :, 128:256] = jnp.where(half == (h % 2), chunk, 0.0).astype(BF16)
            kc_ref[h, :, 0:128] = kv[:, 256 * h:256 * h + 128]
            kc_ref[h, :, 128:256] = kr2
            vv_ref[h] = kv[:, 256 * h + 128:256 * h + 256]

    const = lambda shape: pl.BlockSpec(shape, lambda i: tuple(0 for _ in shape))
    return pl.pallas_call(
        body, name="mla_pre_fwd", grid=(S // tm,),
        in_specs=[pl.BlockSpec((tm, 256), lambda i: (i, 3)), pl.BlockSpec((tm, 128), lambda i: (i, 8)),
                  pl.BlockSpec((tm, 128), lambda i: (i, 9)), const((1, 256)), const((1, 128)),
                  const((768, 256)), const((8, 128, 128)),
                  pl.BlockSpec((tm, 128), lambda i: (i, 0)), pl.BlockSpec((tm, 128), lambda i: (i, 0))],
        out_specs=[pl.BlockSpec((MLA_HEADS, tm, 256), lambda i: (0, i, 0)),
                   pl.BlockSpec((MLA_HEADS, tm, 256), lambda i: (0, i, 0)),
                   pl.BlockSpec((MLA_HEADS, tm, 128), lambda i: (0, i, 0))],
        out_shape=[jax.ShapeDtypeStruct((MLA_HEADS, S, 256), BF16), jax.ShapeDtypeStruct((MLA_HEADS, S, 256), BF16),
                   jax.ShapeDtypeStruct((MLA_HEADS, S, 128), BF16)],
        compiler_params=_params(("parallel",)),
    )(proj, proj, proj, qn_w, kvn_w, wuqT, wukv, cos, sin)


def _causal(i, j, t):
    row = i * t + lax.broadcasted_iota(jnp.int32, (t, t), 0)
    col = j * t + lax.broadcasted_iota(jnp.int32, (t, t), 1)
    return col <= row


def _mla_attn_fwd(qc, kc, vv, *, t=512):
    S = qc.shape[1]
    t = min(t, S)

    def body(q_ref, k_ref, v_ref, o_ref, l_ref):
        i = pl.program_id(0)
        diag = _causal(0, 0, t)

        def step(j, carry, masked):
            rows = pl.ds(pl.multiple_of(j * t, t), t)
            out = []
            for h in range(MLA_HEADS):
                m, l, acc = carry[h]
                s = _dot_nt(q_ref[h], k_ref[h, rows, :]) * MLA_SCALE
                if masked:
                    s = jnp.where(diag, s, -jnp.inf)
                m_new = jnp.maximum(m, jnp.max(s, axis=-1, keepdims=True))
                alpha = jnp.exp(m - m_new)
                p = jnp.exp(s - m_new)
                l = alpha * l + jnp.sum(p, axis=-1, keepdims=True)
                acc = alpha * acc + _dot(p.astype(BF16), v_ref[h, rows, :])
                out.append((m_new, l, acc))
            return tuple(out)

        init = tuple((jnp.full((t, 1), -jnp.inf, F32), jnp.zeros((t, 1), F32), jnp.zeros((t, MLA_V), F32))
                     for _ in range(MLA_HEADS))
        carry = lax.fori_loop(0, i, lambda j, c: step(j, c, False), init)
        carry = step(i, carry, True)
        for h in range(MLA_HEADS):
            m, l, acc = carry[h]
            o_ref[:, 128 * h:128 * h + 128] = acc / l
            l_ref[h] = jnp.broadcast_to(m + jnp.log(l), (t, 128))

    return pl.pallas_call(
        body, name="mla_attn_fwd", grid=(S // t,),
        in_specs=[pl.BlockSpec((MLA_HEADS, t, 256), lambda i: (0, i, 0)),
                  pl.BlockSpec((MLA_HEADS, S, 256), lambda i: (0, 0, 0)),
                  pl.BlockSpec((MLA_HEADS, S, 128), lambda i: (0, 0, 0))],
        out_specs=[pl.BlockSpec((t, 512), lambda i: (i, 0)),
                   pl.BlockSpec((MLA_HEADS, t, 128), lambda i: (0, i, 0))],
        out_shape=[jax.ShapeDtypeStruct((S, 512), F32), jax.ShapeDtypeStruct((MLA_HEADS, S, 128), F32)],
        compiler_params=_params(("parallel",)),
    )(qc, kc, vv)


def _mla_attn_bwd(qc, kc, vv, o, lse, do, *, t=512, tq=1024):
    S = qc.shape[1]
    t = min(t, S)
    tq = min(tq, S)
    nblk = S // t
    hp = MLA_HEADS
    once = pl.Buffered(1)

    def body(q_ref, k_ref, v_ref, o_ref, l_ref, do_ref, dq_ref, dk_ref, dv_ref):
        j = pl.program_id(1)

        @pl.when(j == 0)
        def _():
            dq_ref[...] = jnp.zeros_like(dq_ref)

        first = (j * t) // tq

        def step(i, carry, masked):
            rows = pl.ds(pl.multiple_of(i * tq, tq), tq)
            if masked:
                row = i * tq + lax.broadcasted_iota(jnp.int32, (tq, t), 0)
                col = j * t + lax.broadcasted_iota(jnp.int32, (tq, t), 1)
                visible = col <= row
            out = []
            for h in range(hp):
                dk, dv = carry[h]
                k = k_ref[h]
                q = q_ref[h, rows, :]
                dov = do_ref[rows, 128 * h:128 * h + 128]
                lrow = l_ref[h, rows, :][:, 0:1]
                p = jnp.exp(_dot_nt(q, k) * MLA_SCALE - lrow)
                if masked:
                    p = jnp.where(visible, p, 0.0)
                dob = dov.astype(BF16)
                dv = dv + _dot_tn(p.astype(BF16), dob)
                dp = _dot_nt(dob, v_ref[h])
                delta = jnp.sum(dov * o_ref[rows, 128 * h:128 * h + 128], axis=-1, keepdims=True)
                ds = (p * (dp - delta) * MLA_SCALE).astype(BF16)
                dk = dk + _dot_tn(ds, q)
                dq_ref[h, rows, :] += _dot(ds, k)
                out.append((dk, dv))
            return tuple(out)

        init = tuple((jnp.zeros((t, 256), F32), jnp.zeros((t, MLA_V), F32)) for _ in range(hp))
        carry = step(first, init, True)
        carry = lax.fori_loop(first + 1, S // tq, lambda i, c: step(i, c, False), carry)
        for h in range(hp):
            dk_ref[h] = carry[h][0]
            dv_ref[h] = carry[h][1]

    return pl.pallas_call(
        body, name="mla_attn_bwd", grid=(MLA_HEADS // hp, nblk),
        in_specs=[pl.BlockSpec((hp, S, 256), lambda g, j: (g, 0, 0), pipeline_mode=once),
                  pl.BlockSpec((hp, t, 256), lambda g, j: (g, j, 0)),
                  pl.BlockSpec((hp, t, 128), lambda g, j: (g, j, 0)),
                  pl.BlockSpec((S, 128 * hp), lambda g, j: (0, g), pipeline_mode=once),
                  pl.BlockSpec((hp, S, 128), lambda g, j: (g, 0, 0), pipeline_mode=once),
                  pl.BlockSpec((S, 128 * hp), lambda g, j: (0, g), pipeline_mode=once)],
        out_specs=[pl.BlockSpec((hp, S, 256), lambda g, j: (g, 0, 0)),
                   pl.BlockSpec((hp, t, 256), lambda g, j: (g, j, 0)),
                   pl.BlockSpec((hp, t, 128), lambda g, j: (g, j, 0))],
        out_shape=[jax.ShapeDtypeStruct((MLA_HEADS, S, 256), F32), jax.ShapeDtypeStruct((MLA_HEADS, S, 256), F32),
                   jax.ShapeDtypeStruct((MLA_HEADS, S, 128), F32)],
        compiler_params=_params(("parallel", "arbitrary")),
    )(qc, kc, vv, o, lse, do)


def _mla_pre_bwd(proj, qn_w, kvn_w, wuqT, wukv, cos, sin, dqc, dkc, dvv, *, tm=ROW_TILE):
    S = proj.shape[0]
    tm = min(tm, S)

    def body(ql_ref, kl_ref, qw_ref, kw_ref, wuq_ref, wukv_ref, cos_ref, sin_ref, dqc_ref, dkc_ref, dvv_ref,
             dql_ref, dkl_ref, dkr_ref, gq_ref, gkv_ref, part_ref, gq_acc, gkv_acc):
        @pl.when(pl.program_id(0) == 0)
        def _():
            gq_acc[...] = jnp.zeros_like(gq_acc)
            gkv_acc[...] = jnp.zeros_like(gkv_acc)
            part_ref[...] = jnp.zeros_like(part_ref)

        cs, sn = _rope_wide(cos_ref), _rope_wide(sin_ref)
        half = lax.broadcasted_iota(jnp.int32, (tm, 128), 1) // 64
        ql = ql_ref[...]
        rq = _rstd(ql)
        qhat = ql * rq
        qw = qw_ref[...]
        qn = (qhat * qw).astype(BF16)
        chunks = []
        for pair in range(2):
            chunks.append(jnp.where(half == 0, dqc_ref[2 * pair, :, 128:256], dqc_ref[2 * pair + 1, :, 128:256]))
        dqr = jnp.concatenate(chunks, axis=1)
        dqr = dqr * cs + _swap_halves(dqr * sn)
        dq = jnp.concatenate([dqc_ref[h, :, 0:128] for h in range(MLA_HEADS)] + [dqr], axis=1).astype(BF16)
        gq_acc[...] += _dot_tn(dq, qn)
        dqn = _dot(dq, _uq_group_rows(wuq_ref))
        part_ref[0:1, :] += jnp.sum(dqn * qhat, axis=0, keepdims=True)
        dql_ref[...] = _rms_bwd(dqn * qw, qhat, rq)
        kl = kl_ref[...]
        rk = _rstd(kl)
        khat = kl * rk
        kw = kw_ref[...]
        kvn = (khat * kw).astype(BF16)
        dkvn = jnp.zeros((tm, MLA_KVR), F32)
        dkr2 = jnp.zeros((tm, 128), F32)
        for h in range(MLA_HEADS):
            dkn = dkc_ref[h, :, 0:128].astype(BF16)
            dvh = dvv_ref[h].astype(BF16)
            gkv_acc[2 * h] += _dot_tn(kvn, dkn)
            gkv_acc[2 * h + 1] += _dot_tn(kvn, dvh)
            dkvn += _dot_nt(dkn, wukv_ref[2 * h]) + _dot_nt(dvh, wukv_ref[2 * h + 1])
            dkr2 += dkc_ref[h, :, 128:256]
        part_ref[1:2, 0:128] += jnp.sum(dkvn * khat, axis=0, keepdims=True)
        dkl_ref[...] = _rms_bwd(dkvn * kw, khat, rk)
        dkr = jnp.where(half == 0, dkr2 + pltpu.roll(dkr2, 64, 1), 0.0)
        dkr_ref[...] = dkr * cs[:, :128] + _swap_halves(dkr * sn[:, :128])

        @pl.when(pl.program_id(0) == S // tm - 1)
        def _():
            gkv_ref[...] = gkv_acc[...].astype(BF16)
            per = MLA_NOPE + MLA_ROPE
            for h in range(MLA_HEADS):
                gq_ref[per * h:per * h + MLA_NOPE, :] = gq_acc[MLA_NOPE * h:MLA_NOPE * (h + 1), :].astype(BF16)
                gq_ref[per * h + MLA_NOPE:per * (h + 1), :] = gq_acc[512 + MLA_ROPE * h:512 + MLA_ROPE * (h + 1), :].astype(BF16)

    const = lambda shape: pl.BlockSpec(shape, lambda i: tuple(0 for _ in shape))
    heads = lambda w: pl.BlockSpec((MLA_HEADS, tm, w), lambda i: (0, i, 0))
    return pl.pallas_call(
        body, name="mla_pre_bwd", grid=(S // tm,),
        in_specs=[pl.BlockSpec((tm, 256), lambda i: (i, 3)), pl.BlockSpec((tm, 128), lambda i: (i, 8)),
                  const((1, 256)), const((1, 128)), const((768, 256)), const((8, 128, 128)),
                  pl.BlockSpec((tm, 128), lambda i: (i, 0)), pl.BlockSpec((tm, 128), lambda i: (i, 0)),
                  heads(256), heads(256), heads(128)],
        out_specs=[pl.BlockSpec((tm, 256), lambda i: (i, 0)), pl.BlockSpec((tm, 128), lambda i: (i, 0)),
                   pl.BlockSpec((tm, 128), lambda i: (i, 0)), const((768, 256)), const((8, 128, 128)), const((8, 256))],
        out_shape=[jax.ShapeDtypeStruct((S, 256), F32), jax.ShapeDtypeStruct((S, 128), F32),
                   jax.ShapeDtypeStruct((S, 128), F32), jax.ShapeDtypeStruct((768, 256), BF16),
                   jax.ShapeDtypeStruct((8, 128, 128), BF16), jax.ShapeDtypeStruct((8, 256), F32)],
        scratch_shapes=[pltpu.VMEM((768, 256), F32), pltpu.VMEM((8, 128, 128), F32)],
        compiler_params=_params(("arbitrary",)),
    )(proj, proj, qn_w, kvn_w, wuqT, wukv, cos, sin, dqc, dkc, dvv)


def _mix_out_fwd(x, oa, ob, w_o, vecs, *, tm=ROW_TILE):
    S = x.shape[0]
    tm = min(tm, S)

    def body(x_ref, oa_ref, ob_ref, w_ref, vec_ref, xo_ref, mo_ref):
        mo = _dot(oa_ref[...].astype(BF16), w_ref[0:512, :]) + _dot(ob_ref[...].astype(BF16), w_ref[512:1024, :])
        mo_ref[...] = mo
        xo_ref[...] = x_ref[...] + vec_ref[3:4, :] * mo

    row = pl.BlockSpec((tm, D), lambda i: (i, 0))
    half = pl.BlockSpec((tm, 512), lambda i: (i, 0))
    return pl.pallas_call(
        body, name="mix_out_fwd", grid=(S // tm,),
        in_specs=[row, half, half, pl.BlockSpec((D, D), lambda i: (0, 0)), pl.BlockSpec((8, D), lambda i: (0, 0))],
        out_specs=[row, row],
        out_shape=[jax.ShapeDtypeStruct((S, D), F32), jax.ShapeDtypeStruct((S, D), F32)],
        compiler_params=_params(("parallel",)),
    )(x, oa, ob, w_o, vecs)


def _mix_out_bwd(dxo, mo, oa, ob, w_o, vecs, *, tm=ROW_TILE):
    S = dxo.shape[0]
    tm = min(tm, S)

    def body(dx_ref, mo_ref, oa_ref, ob_ref, w_ref, vec_ref, doa_ref, dob_ref, gw_ref, part_ref, gw_acc):
        @pl.when(pl.program_id(0) == 0)
        def _():
            gw_acc[...] = jnp.zeros_like(gw_acc)
            part_ref[...] = jnp.zeros_like(part_ref)

        dx = dx_ref[...]
        part_ref[0:1, :] += jnp.sum(dx * mo_ref[...], axis=0, keepdims=True)
        dmo = (vec_ref[3:4, :] * dx).astype(BF16)
        doa_ref[...] = _dot_nt(dmo, w_ref[0:512, :])
        dob_ref[...] = _dot_nt(dmo, w_ref[512:1024, :])
        gw_acc[0:512, :] += _dot_tn(oa_ref[...].astype(BF16), dmo)
        gw_acc[512:1024, :] += _dot_tn(ob_ref[...].astype(BF16), dmo)

        @pl.when(pl.program_id(0) == S // tm - 1)
        def _():
            gw_ref[...] = gw_acc[...].astype(BF16)

    row = pl.BlockSpec((tm, D), lambda i: (i, 0))
    half = pl.BlockSpec((tm, 512), lambda i: (i, 0))
    return pl.pallas_call(
        body, name="mix_out_bwd", grid=(S // tm,),
        in_specs=[row, row, half, half, pl.BlockSpec((D, D), lambda i: (0, 0)), pl.BlockSpec((8, D), lambda i: (0, 0))],
        out_specs=[half, half, pl.BlockSpec((D, D), lambda i: (0, 0)), pl.BlockSpec((8, D), lambda i: (0, 0))],
        out_shape=[jax.ShapeDtypeStruct((S, 512), F32), jax.ShapeDtypeStruct((S, 512), F32),
                   jax.ShapeDtypeStruct((D, D), BF16), jax.ShapeDtypeStruct((8, D), F32)],
        scratch_shapes=[pltpu.VMEM((D, D), F32)],
        compiler_params=_params(("arbitrary",)),
    )(dxo, mo, oa, ob, w_o, vecs)


def _mix_in_bwd(h, w_inT, dq, dk, dv, dql, dkl, dkr, *, tm=ROW_TILE):
    S = h.shape[0]
    tm = min(tm, S)
    wid = (512, 128, 128, 256, 128, 128)

    def body(h_ref, w_ref, dq_ref, dk_ref, dv_ref, dql_ref, dkl_ref, dkr_ref, dh_ref, gw_ref):
        @pl.when(pl.program_id(0) == 0)
        def _():
            gw_ref[...] = jnp.zeros_like(gw_ref)

        parts = (dq_ref, dk_ref, dv_ref, dql_ref, dkl_ref, dkr_ref)
        dproj = jnp.concatenate([ref[...].astype(BF16) for ref in parts], axis=1)
        dh_ref[...] = _dot(dproj, w_ref[...])
        gw_ref[...] += _dot_tn(dproj, h_ref[...])[0:D_IN, :]

    row = pl.BlockSpec((tm, D), lambda i: (i, 0))
    part = lambda w: pl.BlockSpec((tm, w), lambda i: (i, 0))
    return pl.pallas_call(
        body, name="mix_in_bwd", grid=(S // tm,),
        in_specs=[row, pl.BlockSpec((D_IN_PAD, D), lambda i: (0, 0))] + [part(w) for w in wid],
        out_specs=[row, pl.BlockSpec((D_IN, D), lambda i: (0, 0))],
        out_shape=[jax.ShapeDtypeStruct((S, D), F32), jax.ShapeDtypeStruct((D_IN, D), F32)],
        compiler_params=_params(("arbitrary",)),
    )(h, w_inT, dq, dk, dv, dql, dkl, dkr)


def _vecs(norm_w, mod9, k):
    return jnp.concatenate([norm_w.reshape(1, D), mod9[3 * k:3 * k + 3], jnp.zeros((4, D), F32)], axis=0)


def _local_step(x, tgt, mod9, norms, sinks, rel_bias, q_norm, kv_norm, W, on_grads=None):
    if on_grads is None:
        on_grads = lambda group, grads, after, vecs: vecs
    S = x.shape[0]
    v1 = _vecs(norms["ffn1"], mod9, 0)
    v2 = _vecs(norms["mix"], mod9, 1)
    v3 = _vecs(norms["ffn2"], mod9, 2)
    bucket = jnp.asarray(_bucket_table())
    cos, sin = _rope_tables(S)
    if isinstance(W, dict):
        full, W = W, (lambda group, after, vecs: (full, vecs))

    W1, v1 = W("ffn1", [], v1)
    x1, h1, a1, b1, f1 = _ffn_fwd(x, v1, W1["g1T"], W1["u1T"], W1["d1"], name="ffn1_fwd")
    W2, v2 = W("mixer", [x1], v2)
    wuqT = W2["w_uqT"]
    h2, proj, w_inT = _mix_in_fwd(x1, v2, W2["w_inT"])
    oa, bias = _swa_fwd(proj, rel_bias, bucket, sinks)
    qc, kc, vv = _mla_pre_fwd(proj, q_norm, kv_norm, wuqT, W2["w_ukv"], cos, sin)
    ob, lse = _mla_attn_fwd(qc, kc, vv)
    _, v2o = W("ffn2_on_its_way", [oa, ob], v2)
    x2, mo = _mix_out_fwd(x1, oa, ob, W2["w_o"], v2o)
    W3, v3 = W("ffn2", [x2], v3)
    x3, h3, a3, b3, f3 = _ffn_fwd(x2, v3, W3["g3T"], W3["u3T"], W3["d3"], name="ffn2_fwd")
    dx3, head_part, df3 = _head(x3, tgt, norms["final"], f3, v3)

    gg3, gu3, gd3, dh3 = _ffn_bwd_main(h3, df3, a3, b3, W3["g3T"], W3["u3T"], W3["d3"], name="ffn2_bwd")
    ffn2 = {"g3T": gg3, "u3T": gu3, "d3": gd3}
    v3 = on_grads("ffn2", ffn2, [], v3)
    dx2, n3_part = _norm_bwd(dh3, x2, dx3, v3, name="ffn2_norm_bwd")
    v2 = on_grads("ffn2", None, [dx2], v2)
    doa, dob, g_wo, g2_part = _mix_out_bwd(dx2, mo, oa, ob, W2["w_o"], v2)
    dq, dk, dv, drb, dsk = _swa_bwd(proj, bias, sinks, oa, doa, bucket)
    dqc, dkc, dvv = _mla_attn_bwd(qc, kc, vv, ob, lse, dob)
    dql, dkl, dkr, g_uq, g_ukv, mla_part = _mla_pre_bwd(proj, q_norm, kv_norm, wuqT, W2["w_ukv"], cos, sin, dqc, dkc, dvv)
    dh2, g_win = _mix_in_bwd(h2, w_inT, dq, dk, dv, dql, dkl, dkr)
    mixer = {"w_inT": g_win, "w_uqT": g_uq, "w_ukv": g_ukv, "w_o": g_wo}
    v2 = on_grads("mixer", mixer, [], v2)
    dx1, n2_part, df1 = _norm_bwd(dh2, x1, dx2, v2, name="mix_norm_bwd", below=(f1, v1))
    started = on_grads("mixer", None, [dx1], jnp.zeros((1, 1), F32))
    gg1, gu1, gd1, dh1 = _ffn_bwd_main(h1, df1, a1, b1, W1["g1T"], W1["u1T"], W1["d1"], name="ffn1_bwd",
                                       after=[started])
    ffn1 = {"g1T": gg1, "u1T": gu1, "d1": gd1}
    v1 = on_grads("ffn1", ffn1, [], v1)
    dx0, n1_part = _norm_bwd(dh1, x, dx1, v1, name="ffn1_norm_bwd")

    grads = {**ffn1, **ffn2, **mixer}
    return head_part[1, 0], dx0, grads, _pack_vec(n1_part, n2_part, n3_part, head_part, g2_part, mla_part, dsk, drb)


SMALL_LAYOUT = (("norm_ffn1", 1024), ("norm_mix", 1024), ("norm_ffn2", 1024), ("norm_final", 1024),
                ("q_norm", 256), ("kv_norm", 128), ("sinks", 128), ("rel_bias", 256))
N_SMALL = sum(n for _, n in SMALL_LAYOUT)
LOSS_SLOT = 4 * 1024 + 256 + 128 + SWA_HEADS
N_MODVEC = N_MOD * D
N_VEC = N_MODVEC + N_SMALL


def _pack_vec(n1, n2, n3, head, g2, mla, dsk, drb):
    def body(n1_ref, n2_ref, n3_ref, head_ref, g2_ref, mla_ref, dsk_ref, drb_ref, out_ref):
        rows = [n1_ref[1:2, :], n1_ref[2:3, :], n2_ref[3:4, :], n2_ref[1:2, :], n2_ref[2:3, :], g2_ref[0:1, :],
                n3_ref[1:2, :], n3_ref[2:3, :], head_ref[3:4, :],
                n1_ref[0:1, :], n2_ref[0:1, :], n3_ref[0:1, :], head_ref[0:1, :]]
        for i, row in enumerate(rows):
            out_ref[:, D * i:D * (i + 1)] = row
        off = D * len(rows)
        out_ref[:, off:off + 256] = mla_ref[0:1, :]
        out_ref[:, off + 256:off + 384] = mla_ref[1:2, 0:128]

        def diagonal(block):
            r = lax.broadcasted_iota(jnp.int32, block.shape, 0)
            lane = lax.broadcasted_iota(jnp.int32, block.shape, 1)
            return jnp.sum(jnp.where(r == lane, block, 0.0), axis=0, keepdims=True)

        lane = lax.broadcasted_iota(jnp.int32, (1, 128), 1)
        out_ref[:, off + 384:off + 512] = jnp.where(lane == SWA_HEADS, head_ref[1:2, 0:128], diagonal(dsk_ref[...]))
        out_ref[:, off + 512:off + 640] = diagonal(drb_ref[0:128, :])
        out_ref[:, off + 640:off + 768] = diagonal(drb_ref[128:256, :])

    vm = pl.BlockSpec(memory_space=pltpu.VMEM)
    return pl.pallas_call(body, name="pack_vec", in_specs=[vm] * 8, out_specs=vm,
                          out_shape=jax.ShapeDtypeStruct((1, N_VEC), F32))(n1, n2, n3, head, g2, mla, dsk, drb)


def _coords():
    return lax.axis_index("x"), lax.axis_index("y"), lax.axis_index("c")


def _flip(v, bit):
    return 1 - v if bit else v


def _peer(r):
    x, y, c = _coords()
    return (_flip(x, r & 4), _flip(y, r & 2), _flip(c, r & 1))


class _ModExchange:
    def __init__(self, c_ref, w_ref, b_ref, mod_ref, ca_ref, call_ref, part_ref, send_sems, recv_sems):
        self.refs = (c_ref, w_ref, b_ref, mod_ref, ca_ref, call_ref, part_ref)
        self.sems = (send_sems, recv_sems)
        x, y, c = _coords()
        self.me = 4 * x + 2 * y + c
        self.sends = []

    def _copy(self, phase, r):
        c_ref, _, _, mod_ref, _, call_ref, part_ref = self.refs
        src, dst = (c_ref, call_ref.at[self.me]) if phase == 0 else (part_ref.at[self.me ^ r], mod_ref.at[self.me])
        return pltpu.make_async_remote_copy(src, dst, self.sems[0].at[phase, r], self.sems[1].at[phase, r],
                                            device_id=_peer(r), device_id_type=MESH)

    def _start(self, phase):
        for r in range(1, N_DEV):
            self.sends.append(self._copy(phase, r))
            self.sends[-1].start()

    def begin(self):
        c_ref, _, _, _, _, call_ref, _ = self.refs
        call_ref[self.me] = c_ref[...]
        self._start(0)

    def middle(self):
        _, w_ref, b_ref, mod_ref, ca_ref, call_ref, part_ref = self.refs
        for r in range(1, N_DEV):
            self._copy(0, r).wait_recv()
        cv = call_ref[...].reshape(8 * N_DEV, D)
        ca = (cv * _sigmoid(cv)).astype(BF16)
        ca_ref[...] = ca
        part_ref[...] = _dot(ca, w_ref[...].astype(BF16)).reshape(part_ref.shape)
        mod_ref[self.me] = part_ref[self.me] + b_ref[self.me]
        self._start(1)

    def end(self):
        _, _, b_ref, mod_ref, _, _, _ = self.refs
        for r in range(1, N_DEV):
            self._copy(1, r).wait_recv()
            mod_ref[self.me ^ r] = mod_ref[self.me ^ r] + b_ref[self.me ^ r]
        for cp in self.sends:
            cp.wait_send()


def _mod_bwd(allvec, ca, me_idx):
    W = N_MODVEC // N_DEV

    def body(me_ref, all_ref, cols_ref, ca_ref, gw_ref, sum_ref):
        in_first_row = lax.broadcasted_iota(jnp.int32, (N_DEV, 8, W), 1) == 0
        dm = jnp.where(in_first_row, cols_ref[...], 0.0).reshape(8 * N_DEV, W)
        gw_ref[...] = _dot_tn(ca_ref[...], dm.astype(BF16))
        total = all_ref[0]
        for k in range(1, N_DEV):
            total = total + all_ref[k]
        sum_ref[...] = total

    return pl.pallas_call(
        body, name="mod_bwd",
        grid_spec=pltpu.PrefetchScalarGridSpec(
            num_scalar_prefetch=1, grid=(1,),
            in_specs=[pl.BlockSpec((N_DEV, 1, N_VEC), lambda i, me: (0, 0, 0)),
                      pl.BlockSpec((N_DEV, 1, W), lambda i, me: (0, 0, me[0])),
                      pl.BlockSpec((8 * N_DEV, D), lambda i, me: (0, 0))],
            out_specs=[pl.BlockSpec((D, W), lambda i, me: (0, 0)), pl.BlockSpec((1, N_VEC), lambda i, me: (0, 0))]),
        out_shape=[jax.ShapeDtypeStruct((D, W), F32), jax.ShapeDtypeStruct((1, N_VEC), F32)],
        compiler_params=_params(("arbitrary",)),
    )(me_idx, allvec, allvec, ca)


def _wgather(shards, later, c_tile, w_mod, b_mod3):
    n, nl = len(shards), len(later)
    rows = [s.shape[0] for s in shards]
    W = w_mod.shape[1]
    cast = [(s.shape, True) for s in shards] + [(s.shape, s.dtype != dt) for s, dt in later]
    staging = [pltpu.VMEM(shape, dt) for shape, is_cast in cast if is_cast for dt in (F32, BF16)]

    def body(*refs):
        ins, (c_ref, w_ref, b_ref), raw = refs[:n], refs[n:n + 3], refs[n + 3:n + 3 + nl]
        o = n + 3 + 2 * nl
        outs, (mod_ref, ca_ref), lands = refs[o:o + n], refs[o + n:o + n + 2], refs[o + n + 2:o + n + 2 + nl]
        o += n + 2 + nl
        send_sems, recv_sems, local_sems, load_sems, store_sems = refs[o:o + 5]
        mod = _ModExchange(c_ref, w_ref, b_ref, mod_ref, ca_ref, *refs[o + 5:o + 9])
        stage = iter(refs[o + 9:])
        staged = [(next(stage), next(stage)) if is_cast else None for _, is_cast in cast]
        mod.begin()
        x, y, c = _coords()
        me = 4 * x + 2 * y + c
        sib, xn, yn = (x, y, 1 - c), (1 - x, y, c), (x, 1 - y, c)
        block = lambda px, py, pc: 4 * px + 2 * py + pc

        def part(k, blk, half):
            if half is None:
                return outs[k].at[blk]
            return outs[k].at[blk, pl.ds(half * (rows[k] // 2), rows[k] // 2)]

        def copy(k, slot, blk, to, half=None, src=None):
            ref = part(k, blk, half)
            return pltpu.make_async_remote_copy(
                src_ref=ref if src is None else src, dst_ref=ref, send_sem=send_sems.at[k, slot],
                recv_sem=recv_sems.at[k, slot], device_id=to, device_id_type=MESH)

        def in_bf16(sources, first):
            loads = [pltpu.make_async_copy(src, staged[first + i][0], load_sems.at[first + i])
                     if staged[first + i] else None for i, src in enumerate(sources)]
            for cp in loads:
                if cp is not None:
                    cp.start()
            for i, src in enumerate(sources):
                if loads[i] is None:
                    yield i, src
                    continue
                loads[i].wait()
                wide, narrow = staged[first + i]
                narrow[...] = wide[...].astype(BF16)
                yield i, narrow

        mine = [ref for _, ref in in_bf16(ins, 0)]
        local = [pltpu.make_async_copy(mine[k], outs[k].at[me], local_sems.at[k]) for k in range(n)]
        for cp in local:
            cp.start()
        sent = [copy(k, slot, me, to, src=mine[k]) for k in range(n) for slot, to in ((0, sib), (1, xn), (2, yn))]
        for cp in sent:
            cp.start()
        mod.middle()
        for k, ref in in_bf16(raw, n):
            local.append(pltpu.make_async_copy(ref, lands[k].at[me], store_sems.at[k]))
            local[-1].start()
        bx, by, bd = block(1 - x, y, c), block(x, 1 - y, c), block(1 - x, 1 - y, c)
        for k in range(n):
            copy(k, 1, bx, sib).wait_recv()
            sent += [copy(k, 4, bx, yn, half=1), copy(k, 5, bx, sib)]
            sent[-2].start()
            sent[-1].start()
        for k in range(n):
            copy(k, 2, by, sib).wait_recv()
            sent += [copy(k, 3, by, xn, half=0), copy(k, 6, by, sib)]
            sent[-2].start()
            sent[-1].start()
        for k in range(n):
            copy(k, 3, bd, sib, half=0).wait_recv()
            copy(k, 4, bd, sib, half=1).wait_recv()
            sent.append(copy(k, 7, bd, sib))
            sent[-1].start()
        for k in range(n):
            copy(k, 0, block(x, y, 1 - c), sib).wait_recv()
            for slot, blk in ((5, block(1 - x, y, 1 - c)), (6, block(x, 1 - y, 1 - c)), (7, block(1 - x, 1 - y, 1 - c))):
                copy(k, slot, blk, sib).wait_recv()
        for cp in sent:
            cp.wait_send()
        for cp in local:
            cp.wait()
        mod.end()

    anyspec, vm = pl.BlockSpec(memory_space=pl.ANY), pl.BlockSpec(memory_space=pltpu.VMEM)
    zones = [lax.empty((N_DEV,) + s.shape, dt) for s, dt in later]
    out = pl.pallas_call(
        body, name="wgather", in_specs=[anyspec] * n + [vm] * 3 + [anyspec] * (2 * nl),
        out_specs=[anyspec] * n + [vm] * 2 + [anyspec] * nl,
        out_shape=[jax.ShapeDtypeStruct((N_DEV,) + s.shape, BF16) for s in shards]
        + [jax.ShapeDtypeStruct((N_DEV, 8, W), F32), jax.ShapeDtypeStruct((8 * N_DEV, D), BF16)]
        + [jax.ShapeDtypeStruct(z.shape, z.dtype) for z in zones],
        input_output_aliases={n + 3 + nl + i: n + 2 + i for i in range(nl)},
        scratch_shapes=[pltpu.SemaphoreType.DMA((n, 8)), pltpu.SemaphoreType.DMA((n, 8)),
                        pltpu.SemaphoreType.DMA((n,)), pltpu.SemaphoreType.DMA((n + nl,)),
                        pltpu.SemaphoreType.DMA((nl,)),
                        pltpu.VMEM((N_DEV, 8, D), F32), pltpu.VMEM((N_DEV, 8, W), F32),
                        pltpu.SemaphoreType.DMA((2, N_DEV)), pltpu.SemaphoreType.DMA((2, N_DEV))] + staging,
        compiler_params=_params(),
    )(*shards, c_tile, w_mod, b_mod3, *[s for s, _ in later], *zones)
    return out[:n], out[n], out[n + 1], list(out[n + 2:])


class _GatherCopies:
    def __init__(self, lands, send_sems, recv_sems, k0=0, batches=None):
        x, y, c = _coords()
        me = 4 * x + 2 * y + c
        sib = (x, y, 1 - c)
        chips = [(1 - x, y), (x, 1 - y), (1 - x, 1 - y)]

        def copy(k, slot, block, to):
            return pltpu.make_async_remote_copy(
                src_ref=lands[k].at[block], dst_ref=lands[k].at[block],
                send_sem=send_sems.at[7 * (k0 + k) + slot], recv_sem=recv_sems.at[7 * (k0 + k) + slot],
                device_id=to, device_id_type=MESH)

        n = len(lands)
        self.first = [copy(k, 0, me, sib) for k in range(n)]
        for batch in batches or [range(n)]:
            self.first += [copy(k, 1 + j, me, (cx, cy, c)) for j, (cx, cy) in enumerate(chips) for k in batch]
        self.landed = [copy(k, 1 + j, 4 * cx + 2 * cy + c, sib) for j, (cx, cy) in enumerate(chips) for k in range(n)]
        self.passed = [copy(k, 4 + j, 4 * cx + 2 * cy + c, sib) for j, (cx, cy) in enumerate(chips) for k in range(n)]
        self.from_sib = [copy(k, 0, 4 * x + 2 * y + (1 - c), sib) for k in range(n)]
        self.from_sib += [copy(k, 4 + j, 4 * cx + 2 * cy + (1 - c), sib) for j, (cx, cy) in enumerate(chips)
                          for k in range(n)]


def _token(carry):
    return [] if carry is None else [carry], jax.ShapeDtypeStruct((8, 128), F32) if carry is None else carry


def _gather_start(lands, *, name, batches=None, carry=None):
    n = len(lands)
    carried, token = _token(carry)

    def body(*refs):
        n_in = n + len(carried)
        for cp in _GatherCopies(refs[:n], refs[n_in], refs[n_in + 1], batches=batches).first:
            cp.start()
        refs[-1][...] = refs[n][...] if carried else jnp.zeros_like(refs[-1])

    vm = pl.BlockSpec(memory_space=pltpu.VMEM)
    out = pl.pallas_call(
        body, name=name,
        out_shape=(pltpu.SemaphoreType.DMA((7 * n,)), pltpu.SemaphoreType.DMA((7 * n,)),
                   *[pltpu.HBM(l.shape, l.dtype) for l in lands], jax.ShapeDtypeStruct(token.shape, token.dtype)),
        in_specs=[HBM_SPEC] * n + [vm] * len(carried),
        out_specs=(SEM_SPEC, SEM_SPEC, *[HBM_SPEC] * n, vm),
        input_output_aliases={i: 2 + i for i in range(n)},
        compiler_params=pltpu.CompilerParams(has_side_effects=DATAFLOW),
    )(*[_in_hbm(l) for l in lands], *carried)
    return out[0], out[1], list(out[2:2 + n]), out[-1]


def _gather_pass(send_sems, recv_sems, lands, after, *, name, stage, k0=0, carry=None):
    n = len(lands)
    carried, token = _token(carry)

    def body(*refs):
        cps = _GatherCopies(refs[:n], refs[n], refs[n + 1], k0)
        if stage == "landed":
            for cp in cps.landed:
                cp.wait_recv()
        else:
            for cp in cps.passed:
                cp.start()
        refs[-1][...] = refs[n + 2 + len(after)][...] if carried else jnp.zeros_like(refs[-1])

    vm = pl.BlockSpec(memory_space=pltpu.VMEM)
    out = pl.pallas_call(
        body, name=name,
        out_shape=(*[pltpu.HBM(l.shape, l.dtype) for l in lands], jax.ShapeDtypeStruct(token.shape, token.dtype)),
        in_specs=[HBM_SPEC] * n + [SEM_SPEC, SEM_SPEC] + [pl.BlockSpec(memory_space=pl.ANY)] * len(after)
        + [vm] * len(carried),
        out_specs=(*[HBM_SPEC] * n, vm),
        input_output_aliases={i: i for i in range(n)},
        compiler_params=pltpu.CompilerParams(has_side_effects=DATAFLOW),
    )(*lands, send_sems, recv_sems, *after, *carried)
    return list(out[:n]), out[-1]


def _gather_end(send_sems, recv_sems, lands, after, *, name, k0=0):
    n = len(lands)

    def body(*refs):
        cps = _GatherCopies(refs[:n], refs[n], refs[n + 1], k0)
        for cp in cps.from_sib:
            cp.wait_recv()
        for cp in cps.first + cps.passed:
            cp.wait_send()

    out = pl.pallas_call(
        body, name=name,
        out_shape=[pltpu.HBM(l.shape, l.dtype) for l in lands],
        in_specs=[HBM_SPEC] * n + [SEM_SPEC, SEM_SPEC] + [pl.BlockSpec(memory_space=pl.ANY)] * len(after),
        out_specs=[HBM_SPEC] * n,
        input_output_aliases={i: i for i in range(n)},
        compiler_params=pltpu.CompilerParams(has_side_effects=DATAFLOW),
    )(*lands, send_sems, recv_sems, *after)
    return list(out)


def _d2d_copies(grads, lands, send_sems, recv_sems):
    x, y, c = _coords()
    return [pltpu.make_async_remote_copy(
        src_ref=grads[k].at[2 * q + (1 - c)], dst_ref=lands[k].at[q],
        send_sem=send_sems.at[4 * k + q], recv_sem=recv_sems.at[4 * k + q],
        device_id=(x, y, 1 - c), device_id_type=MESH) for k in range(len(grads)) for q in range(4)]


def _direct_copies(grads, lands, send_sems, recv_sems):
    x, y, c = _coords()
    me = 4 * x + 2 * y + c
    return [pltpu.make_async_remote_copy(
        src_ref=grads[k].at[me ^ r], dst_ref=lands[k].at[r - 1],
        send_sem=send_sems.at[7 * k + r - 1], recv_sem=recv_sems.at[7 * k + r - 1],
        device_id=_peer(r), device_id_type=MESH) for k in range(len(grads)) for r in range(1, N_DEV)]


def _vec_copies(srcs, lands, send_sems, recv_sems):
    x, y, c = _coords()
    me = 4 * x + 2 * y + c
    return [pltpu.make_async_remote_copy(
        src_ref=lands[0].at[me], dst_ref=lands[0].at[me], send_sem=send_sems.at[r - 1], recv_sem=recv_sems.at[r - 1],
        device_id=_peer(r), device_id_type=MESH) for r in range(1, N_DEV)]


def _chipsum(gs, sibs, cidx, *, name):
    n = len(gs)

    def body(c_ref, *refs):
        for k in range(n):
            refs[2 * n + k][...] = (refs[k][...].astype(F32) + refs[n + k][...].astype(F32)).astype(refs[2 * n + k].dtype)

    mine = [pl.BlockSpec((1,) + g.shape[1:], lambda q, c_ref: (2 * q + c_ref[0], 0, 0)) for g in gs]
    other = [pl.BlockSpec((1,) + g.shape[1:], lambda q, c_ref: (q, 0, 0)) for g in gs]
    return pl.pallas_call(
        body, name=name,
        grid_spec=pltpu.PrefetchScalarGridSpec(num_scalar_prefetch=1, grid=(4,), in_specs=mine + other, out_specs=other),
        out_shape=[jax.ShapeDtypeStruct((4,) + g.shape[1:], g.dtype) for g in gs],
        compiler_params=_params(("arbitrary",)),
    )(cidx, *gs, *sibs)


HBM_SPEC = pl.BlockSpec(memory_space=pltpu.HBM)
SEM_SPEC = pl.BlockSpec(memory_space=pltpu.SEMAPHORE)
DATAFLOW = pltpu.SideEffectType.DATAFLOW_SIDE_EFFECTING


def _in_hbm(a):
    return pltpu.with_memory_space_constraint(a, pltpu.HBM)


def _rs_step1_copies(sums, lands, send_sems, recv_sems):
    n = len(sums)
    direct, relay = lands[:n], lands[n:]
    x, y, c = _coords()
    xn, yn = (1 - x, y, c), (x, 1 - y, c)
    qx, qy, qd = 2 * (1 - x) + y, 2 * x + (1 - y), 2 * (1 - x) + (1 - y)
    cps = []
    for k in range(n):
        h = sums[k].shape[1] // 2
        a, b = pl.ds(0, h), pl.ds(h, h)
        moves = ((sums[k].at[qx, a], direct[k].at[0], xn), (sums[k].at[qy, b], direct[k].at[1], yn),
                 (sums[k].at[qd, a], relay[k].at[0], xn), (sums[k].at[qd, b], relay[k].at[1], yn))
        for s, (src, dst, to) in enumerate(moves):
            cps.append(pltpu.make_async_remote_copy(
                src_ref=src, dst_ref=dst, send_sem=send_sems.at[4 * k + s], recv_sem=recv_sems.at[4 * k + s],
                device_id=to, device_id_type=MESH))
    return cps


def _rs_step2_copies(relayed, lands, send_sems, recv_sems, k0=0):
    x, y, c = _coords()
    cps = []
    for k in range(len(relayed)):
        for s, to in enumerate(((1 - x, y, c), (x, 1 - y, c))):
            cps.append(pltpu.make_async_remote_copy(
                src_ref=relayed[k].at[s], dst_ref=lands[k].at[s], send_sem=send_sems.at[2 * (k0 + k) + s],
                recv_sem=recv_sems.at[2 * (k0 + k) + s], device_id=to, device_id_type=MESH))
    return cps


def _relay_sum(sums, relay, qxy, *, name):
    n = len(sums)

    def body(q_ref, *refs):
        for k in range(n):
            refs[2 * n + k][...] = (refs[k][...].astype(F32) + refs[n + k][...].astype(F32)).astype(refs[2 * n + k].dtype)

    half = lambda s: (1, s.shape[1] // 2) + s.shape[2:]
    return pl.pallas_call(
        body, name=name,
        grid_spec=pltpu.PrefetchScalarGridSpec(
            num_scalar_prefetch=1, grid=(2,),
            in_specs=[pl.BlockSpec(half(s), lambda t, q_ref: (q_ref[t], 1 - t, 0)) for s in sums]
            + [pl.BlockSpec(half(s), lambda t, q_ref: (1 - t, 0, 0)) for s in sums],
            out_specs=[pl.BlockSpec(half(s), lambda t, q_ref: (t, 0, 0)) for s in sums]),
        out_shape=[jax.ShapeDtypeStruct((2,) + half(s)[1:], s.dtype) for s in sums],
        compiler_params=_params(("arbitrary",)),
    )(qxy, *sums, *relay)


def _split_start(copies, srcs, lands, n_sems, after, *, name, carry=None):
    ns, nl = len(srcs), len(lands)
    carried, token = _token(carry)

    def body(*refs):
        n_in = ns + nl + len(after) + len(carried)
        for cp in copies(refs[:ns], refs[ns:ns + nl], refs[n_in], refs[n_in + 1]):
            cp.start()
        refs[-1][...] = refs[n_in - 1][...] if carried else jnp.zeros_like(refs[-1])

    bufs = [_in_hbm(a) for a in list(srcs) + list(lands)]
    vm = pl.BlockSpec(memory_space=pltpu.VMEM)
    out = pl.pallas_call(
        body, name=name,
        out_shape=(pltpu.SemaphoreType.DMA((n_sems,)), pltpu.SemaphoreType.DMA((n_sems,)),
                   *[pltpu.HBM(a.shape, a.dtype) for a in bufs], jax.ShapeDtypeStruct(token.shape, token.dtype)),
        in_specs=[HBM_SPEC] * len(bufs) + [pl.BlockSpec(memory_space=pl.ANY)] * len(after) + [vm] * len(carried),
        out_specs=(SEM_SPEC, SEM_SPEC, *[HBM_SPEC] * len(bufs), vm),
        input_output_aliases={i: 2 + i for i in range(len(bufs))},
        compiler_params=pltpu.CompilerParams(has_side_effects=DATAFLOW),
    )(*bufs, *after, *carried)
    return out[0], out[1], list(out[2:2 + ns]), list(out[2 + ns:2 + ns + nl]), out[-1]


def _split_wait(copies, send_sems, recv_sems, srcs, lands, after, *, name):
    ns, nl = len(srcs), len(lands)

    def body(*refs):
        for cp in copies(refs[:ns], refs[ns:ns + nl], refs[ns + nl], refs[ns + nl + 1]):
            cp.wait_send()
            cp.wait_recv()

    out = pl.pallas_call(
        body, name=name,
        out_shape=[pltpu.HBM(a.shape, a.dtype) for a in list(srcs) + list(lands)],
        in_specs=[HBM_SPEC] * (ns + nl) + [SEM_SPEC, SEM_SPEC] + [pl.BlockSpec(memory_space=pl.ANY)] * len(after),
        out_specs=[HBM_SPEC] * (ns + nl),
        input_output_aliases={i: i for i in range(ns + nl)},
        compiler_params=pltpu.CompilerParams(has_side_effects=DATAFLOW),
    )(*srcs, *lands, send_sems, recv_sems, *after)
    return list(out[:ns]), list(out[ns:])


ADAM_C1 = 1.0 / (1.0 - ADAM_B1 ** ADAM_STEP)
ADAM_C2 = 1.0 / (1.0 - ADAM_B2 ** ADAM_STEP)


def _adam_math(w, g, m, v):
    m2 = ADAM_B1 * m + (1.0 - ADAM_B1) * g
    v2 = ADAM_B2 * v + (1.0 - ADAM_B2) * (g * g)
    return -ADAM_LR * ((m2 * ADAM_C1) / (jnp.sqrt(v2 * ADAM_C2) + ADAM_EPS) + ADAM_WD * w), m2, v2


def _adamw(w, g, m, v, *, name, after=()):
    R, C = w.shape
    tr = R if R <= 512 else 256

    def body(w_ref, g_ref, m_ref, v_ref, *rest):
        d_ref, nm_ref, nv_ref = rest[len(after):]
        d_ref[...], nm_ref[...], nv_ref[...] = _adam_math(w_ref[...], g_ref[...], m_ref[...], v_ref[...])

    blk = pl.BlockSpec((tr, C), lambda i: (i, 0))
    return pl.pallas_call(
        body, name=name, grid=(R // tr,), in_specs=[blk] * 4 + [pl.BlockSpec(memory_space=pl.ANY)] * len(after),
        out_specs=[blk] * 3, out_shape=[jax.ShapeDtypeStruct((R, C), F32)] * 3,
        compiler_params=_params(("parallel",)),
    )(w, g, m, v, *after)


def _adamw_rs2(wmv, cs, direct, second, qidx, *, name):
    n = len(wmv)
    r, cc = wmv[0][0].shape
    h = r // 2

    def body(q_ref, *refs):
        ins, outs = refs[:6 * n], refs[6 * n:]
        for k in range(n):
            w_ref, m_ref, v_ref, c_ref, d1_ref, d2_ref = ins[6 * k:6 * k + 6]
            g_ref, d_ref, nm_ref, nv_ref = outs[4 * k:4 * k + 4]
            g = (c_ref[0].astype(F32) + d1_ref[0].astype(F32)) + d2_ref[0].astype(F32)
            g_ref[...] = g
            d_ref[...], nm_ref[...], nv_ref[...] = _adam_math(w_ref[...], g, m_ref[...], v_ref[...])

    blk = pl.BlockSpec((h, cc), lambda i, q_ref: (i, 0))
    one = [blk, blk, blk, pl.BlockSpec((1, h, cc), lambda i, q_ref: (q_ref[0], i, 0)),
           pl.BlockSpec((1, h, cc), lambda i, q_ref: (i, 0, 0)),
           pl.BlockSpec((1, h, cc), lambda i, q_ref: (1 - i, 0, 0))]
    out = pl.pallas_call(
        body, name=name,
        grid_spec=pltpu.PrefetchScalarGridSpec(num_scalar_prefetch=1, grid=(2,), in_specs=one * n,
                                               out_specs=[blk] * (4 * n)),
        out_shape=[jax.ShapeDtypeStruct((r, cc), F32)] * (4 * n),
        compiler_params=_params(("arbitrary",)),
    )(qidx, *[a for (w, m, v), c, d1, d2 in zip(wmv, cs, direct, second) for a in (w, m, v, c, d1, d2)])
    return [tuple(out[4 * k:4 * k + 4]) for k in range(n)]


def _adamw_rs(wmv, cs, rcv, qidx, *, name):
    n = len(wmv)
    shapes = [w.shape for w, _, _ in wmv]
    n_rcv = rcv[0].shape[0]
    halved = len(set(shapes)) == 1 and shapes[0][0] % 32 == 0 and shapes[0][0] > 128
    tiles = 2 if halved else 1

    def body(q_ref, *refs):
        ins, outs = refs[:5 * n], refs[5 * n:]
        for k in range(n):
            w_ref, m_ref, v_ref, c_ref, r_ref = ins[5 * k:5 * k + 5]
            g_ref, d_ref, nm_ref, nv_ref = outs[4 * k:4 * k + 4]
            g = c_ref[0].astype(F32)
            for j in range(n_rcv):
                g = g + r_ref[j].astype(F32)
            g_ref[...] = g
            d_ref[...], nm_ref[...], nv_ref[...] = _adam_math(w_ref[...], g, m_ref[...], v_ref[...])

    in_specs, out_specs = [], []
    for r, cc in shapes:
        blk = pl.BlockSpec((r // tiles, cc), lambda i, q_ref: (i, 0))
        in_specs += [blk, blk, blk, pl.BlockSpec((1, r // tiles, cc), lambda i, q_ref: (q_ref[0], i, 0)),
                     pl.BlockSpec((n_rcv, r // tiles, cc), lambda i, q_ref: (0, i, 0))]
        out_specs += [blk] * 4
    out = pl.pallas_call(
        body, name=name,
        grid_spec=pltpu.PrefetchScalarGridSpec(num_scalar_prefetch=1, grid=(tiles,), in_specs=in_specs,
                                               out_specs=out_specs),
        out_shape=[jax.ShapeDtypeStruct(s, F32) for s in shapes for _ in range(4)],
        compiler_params=_params(("arbitrary",)),
    )(qidx, *[a for (w, m, v), c, rc in zip(wmv, cs, rcv) for a in (w, m, v, c, rc)])
    return [tuple(out[4 * k:4 * k + 4]) for k in range(n)]


SMALL_PARAMS = ("norm_ffn1", "norm_mix", "norm_ffn2", "norm_final", "q_norm", "kv_norm", "sinks", "rel_bias", "b_mod")


def _adamw_small(gvec, wmv):
    shapes = [wmv[3 * i].shape for i in range(len(SMALL_PARAMS))]

    def body(*refs):
        g_all = refs[0]
        ins = refs[1:1 + 3 * len(SMALL_PARAMS)]
        outs = refs[1 + 3 * len(SMALL_PARAMS):]
        off = N_MODVEC
        for i, name in enumerate(SMALL_PARAMS):
            g_ref, d_ref, nm_ref, nv_ref = outs[4 * i:4 * i + 4]
            w_ref, m_ref, v_ref = ins[3 * i:3 * i + 3]
            start = 0 if name == "b_mod" else off
            rows, width = shapes[i]
            g = jnp.concatenate([g_all[:, start + width * r:start + width * (r + 1)] for r in range(rows)], axis=0)
            g_ref[...] = g
            d_ref[...], nm_ref[...], nv_ref[...] = _adam_math(w_ref[...], g, m_ref[...], v_ref[...])
            if name != "b_mod":
                off += dict(SMALL_LAYOUT)[name]

    vm = pl.BlockSpec(memory_space=pltpu.VMEM)
    n_out = 4 * len(SMALL_PARAMS)
    out = pl.pallas_call(
        body, name="adamw_small", in_specs=[vm] * (1 + len(wmv)), out_specs=[vm] * n_out,
        out_shape=[jax.ShapeDtypeStruct(shapes[i // 4], F32) for i in range(n_out)],
        compiler_params=_params(),
    )(gvec, *wmv)
    return {name: out[4 * i:4 * i + 4] for i, name in enumerate(SMALL_PARAMS)}


TRANSPOSED = ("g1T", "u1T", "g3T", "u3T", "w_inT", "w_uqT")


def kernel(x, c, w_mod, b_mod, norm_ffn1, ffn1_gate, ffn1_up, ffn1_down, norm_mix, w_in, q_norm, kv_norm, w_uq, w_ukv, sinks, w_o, norm_ffn2, ffn2_gate, ffn2_up, ffn2_down, rel_bias, norm_final, loss_target, m_w_mod, m_b_mod, m_norm_ffn1, m_ffn1_gate, m_ffn1_up, m_ffn1_down, m_norm_mix, m_w_in, m_q_norm, m_kv_norm, m_w_uq, m_w_ukv, m_sinks, m_w_o, m_norm_ffn2, m_ffn2_gate, m_ffn2_up, m_ffn2_down, m_rel_bias, m_norm_final, v_w_mod, v_b_mod, v_norm_ffn1, v_ffn1_gate, v_ffn1_up, v_ffn1_down, v_norm_mix, v_w_in, v_q_norm, v_kv_norm, v_w_uq, v_w_ukv, v_sinks, v_w_o, v_norm_ffn2, v_ffn2_gate, v_ffn2_up, v_ffn2_down, v_rel_bias, v_norm_final):
    mx, my, mc = _coords()
    cidx = jnp.reshape(mc, (1,)).astype(jnp.int32)
    qidx = jnp.reshape(2 * mx + my, (1,)).astype(jnp.int32)
    WM = w_mod.shape[2]

    shards = {"g1T": ffn1_gate[0].T, "u1T": ffn1_up[0].T, "d1": ffn1_down[0],
              "g3T": ffn2_gate[0].T, "u3T": ffn2_up[0].T, "d3": ffn2_down[0],
              "w_inT": w_in[0].T, "w_uqT": w_uq[0].T, "w_ukv": w_ukv[0], "w_o": w_o[0]}
    travels = lambda k: F32 if k == "w_inT" else BF16
    me = 4 * mx + 2 * my + mc
    groups = {"ffn1": ("g1T", "u1T", "d1"), "mixer": ("w_inT", "w_uqT", "w_ukv", "w_o"), "ffn2": ("g3T", "u3T", "d3")}
    arriving = {}
    later = groups["mixer"] + groups["ffn2"]
    place = {"mixer": 0, "ffn2": len(groups["mixer"])}

    c_tile = jnp.pad(c, ((0, 7), (0, 0)))
    b_mod3 = jnp.pad(b_mod.reshape(N_DEV, 1, WM), ((0, 0), (0, 7), (0, 0)))
    gathered_ffn1, mod3, ca, zones = _wgather([shards[k] for k in groups["ffn1"]],
                                              [(shards[k], travels(k)) for k in later], c_tile, w_mod[0], b_mod3)
    mod9 = mod3[:, 0, :].reshape(N_MOD, D)

    def as_weights(group, gathered):
        return {k: g if k == "w_ukv" else g.reshape(N_DEV * g.shape[1], g.shape[2])
                for k, g in zip(groups[group], gathered)}

    def start_gather(carry):
        batches = [range(k0, k0 + len(groups[group])) for group, k0 in place.items()]
        send, recv, lands, started = _gather_start(zones, name="gather_start", batches=batches, carry=carry)
        for group, k0 in place.items():
            arriving[group] = (send, recv, lands[k0:k0 + len(groups[group])])
        return started

    def fetch(group, after, vecs):
        if group == "ffn1":
            return as_weights("ffn1", gathered_ffn1), start_gather(vecs)

        def pass_on(group, after, carry=None):
            send, recv, lands = arriving[group]
            lands, token = _gather_pass(send, recv, lands, after, name="gather_landed_" + group, stage="landed",
                                        k0=place[group])
            lands, token = _gather_pass(send, recv, lands, [token], name="gather_onward_" + group, stage="onward",
                                        k0=place[group], carry=carry)
            arriving[group] = (send, recv, lands)
            return token

        if group == "ffn2_on_its_way":
            return None, pass_on("ffn2", after, vecs)
        if group == "mixer":
            after = [pass_on("mixer", after)]
        send, recv, lands = arriving[group]
        return as_weights(group, _gather_end(send, recv, lands, after, name="gather_end_" + group,
                                             k0=place[group])), vecs

    norms ={"ffn1": norm_ffn1, "mix": norm_mix, "ffn2": norm_ffn2, "final": norm_final.reshape(1, D)}
    in_flight = {}

    def on_grads(group, g, after, vecs):
        if group != "ffn1":
            if g is None:
                return vecs
            names = list(g)
            by_dest = [g[k] if k == "w_ukv" else g[k].reshape((N_DEV, g[k].shape[0] // N_DEV) + g[k].shape[1:])
                       for k in names]
            lands = [lax.empty((N_DEV - 1,) + a.shape[1:], a.dtype) for a in by_dest]
            send, recv, by_dest, lands, token = _split_start(_direct_copies, by_dest, lands, 7 * len(names), after,
                                                             name="rs_start_" + group, carry=vecs)
            in_flight[group] = (names, send, recv, by_dest, lands, token)
            return token
        names = list(g)
        by_dest = [g[k].reshape((N_DEV, g[k].shape[0] // N_DEV) + g[k].shape[1:]) for k in names]
        lands = [lax.empty((4,) + a.shape[1:], a.dtype) for a in by_dest]
        send, recv, by_dest, lands, token = _split_start(_d2d_copies, by_dest, lands, 4 * len(names), after,
                                                         name="rs_d2d_start_" + group)
        finish("mixer", [token])
        by_dest, from_sib = _split_wait(_d2d_copies, send, recv, by_dest, lands, [done[-1]], name="rs_d2d_wait_" + group)
        sums = _chipsum(by_dest, from_sib, cidx, name="chipsum_" + group)
        halves = lambda: [lax.empty((2, s.shape[1] // 2) + s.shape[2:], s.dtype) for s in sums]
        send, recv, sums, lands, token = _split_start(_rs_step1_copies, sums, halves() + halves(), 4 * len(names), [],
                                                      name="rs_ici_start_" + group, carry=vecs)
        in_flight[group] = (names, send, recv, sums, lands, token)
        return token

    owners = {"g1T": ("ffn1_gate", ffn1_gate, m_ffn1_gate, v_ffn1_gate), "u1T": ("ffn1_up", ffn1_up, m_ffn1_up, v_ffn1_up),
              "d1": ("ffn1_down", ffn1_down, m_ffn1_down, v_ffn1_down),
              "g3T": ("ffn2_gate", ffn2_gate, m_ffn2_gate, v_ffn2_gate), "u3T": ("ffn2_up", ffn2_up, m_ffn2_up, v_ffn2_up),
              "d3": ("ffn2_down", ffn2_down, m_ffn2_down, v_ffn2_down),
              "w_inT": ("w_in", w_in, m_w_in, v_w_in), "w_uqT": ("w_uq", w_uq, m_w_uq, v_w_uq),
              "w_ukv": ("w_ukv", w_ukv, m_w_ukv, v_w_ukv), "w_o": ("w_o", w_o, m_w_o, v_w_o)}
    res, done = {}, []

    there = lambda k, a: a[0].T if k in TRANSPOSED else a[0]
    back = lambda k, a: a.T[None] if k in TRANSPOSED else a[None]

    def record(names, outs):
        for k, out in zip(names, outs):
            done.append(out[3])
            res[owners[k][0]] = tuple(back(k, a) for a in out)

    def finish(group, after):
        names, send, recv, sums, lands, _ = in_flight[group]
        wmv = [tuple(there(k, a) for a in owners[k][1:]) for k in names]
        own = jnp.reshape(me, (1,)).astype(jnp.int32)
        sums, lands = _split_wait(_direct_copies, send, recv, sums, lands, after, name="rs_wait_" + group)
        record(names, _adamw_rs(wmv, sums, lands, own, name="adamw_" + group))

    _, grad_x, _, vec = _local_step(
        x[0], loss_target[0], mod9, norms, sinks, rel_bias, q_norm, kv_norm, fetch, on_grads=on_grads)

    names, send, recv, sums, lands, step1_started = in_flight["ffn1"]
    vec = vec.reshape(1, 1, N_VEC)
    allvec = lax.dynamic_update_slice(lax.empty((N_DEV, 1, N_VEC), F32), vec, (me, 0, 0))
    vsend, vrecv, _, (allvec,), vec_started = _split_start(_vec_copies, [], [allvec], N_DEV - 1, [step1_started],
                                                           name="vec_start")
    finish("ffn2", [vec_started])
    n = len(names)
    sums, lands = _split_wait(_rs_step1_copies, send, recv, sums, lands, [done[-1]], name="rs_ici_wait_ffn1")
    direct, relay = lands[:n], lands[n:]
    qxy = jnp.stack([2 * (1 - mx) + my, 2 * mx + (1 - my)]).astype(jnp.int32)
    relayed = _relay_sum(sums, relay, qxy, name="relay_sum_ffn1")
    second = [lax.empty(a.shape, a.dtype) for a in relayed]
    send, recv, relayed, second, step2_started = _split_start(_rs_step2_copies, relayed, second, 2 * n, [],
                                                              name="rs_ici_start2_ffn1")

    _, (allvec,) = _split_wait(_vec_copies, vsend, vrecv, [], [allvec], [step2_started], name="vec_wait")
    g_wmod, gvec = _mod_bwd(allvec, ca, jnp.reshape(me, (1,)).astype(jnp.int32))
    loss = gvec[0, N_MODVEC + LOSS_SLOT]
    small_in = {"norm_ffn1": (norm_ffn1, m_norm_ffn1, v_norm_ffn1), "norm_mix": (norm_mix, m_norm_mix, v_norm_mix),
                "norm_ffn2": (norm_ffn2, m_norm_ffn2, v_norm_ffn2), "norm_final": (norm_final, m_norm_final, v_norm_final),
                "q_norm": (q_norm, m_q_norm, v_q_norm), "kv_norm": (kv_norm, m_kv_norm, v_kv_norm),
                "sinks": (sinks, m_sinks, v_sinks), "rel_bias": (rel_bias, m_rel_bias, v_rel_bias),
                "b_mod": (b_mod, m_b_mod, v_b_mod)}
    as_row = lambda k, a: a.T if k == "rel_bias" else a.reshape(1, -1)
    from_row = lambda k, a: a.T if k == "rel_bias" else a.reshape(small_in[k][0].shape)
    small_out = _adamw_small(gvec, [as_row(k, a) for k in SMALL_PARAMS for a in small_in[k]])
    for k in SMALL_PARAMS:
        res[k] = tuple(from_row(k, a) for a in small_out[k])

    out = _adamw(w_mod[0], g_wmod, m_w_mod[0], v_w_mod[0], name="adamw_w_mod")
    res["w_mod"] = tuple(a[None] for a in (g_wmod,) + tuple(out))

    wmv = [tuple(there(k, a) for a in owners[k][1:]) for k in names]
    after = done + [out[2]] + [a for k in SMALL_PARAMS for a in res[k]]
    outs = []
    for i, k in enumerate(names):
        _, (sec,) = _split_wait(functools.partial(_rs_step2_copies, k0=i), send, recv, [relayed[i]], [second[i]],
                                after, name="rs_ici_wait2_" + k)
        outs.append(_adamw_rs2([wmv[i]], [sums[i]], [direct[i]], [sec], qidx, name="adamw_" + owners[k][0])[0])
        after = [outs[-1][3]]
    record(names, outs)

    order = ("w_mod", "b_mod", "norm_ffn1", "ffn1_gate", "ffn1_up", "ffn1_down", "norm_mix", "w_in", "q_norm",
             "kv_norm", "w_uq", "w_ukv", "sinks", "w_o", "norm_ffn2", "ffn2_gate", "ffn2_up", "ffn2_down",
             "rel_bias", "norm_final")
    return (loss, grad_x[None]) + tuple(res[nm][kind] for kind in range(4) for nm in order)
```

```python
import functools
import math

import numpy as np
import jax
import jax.numpy as jnp
from jax import lax
from jax.experimental import pallas as pl
from jax.experimental.pallas import tpu as pltpu

F32 = jnp.float32
BF16 = jnp.bfloat16
MESH = pl.DeviceIdType.MESH

N_DEV = 8
D = 1024
D_FF = 2816
EPS = 1e-6
N_MOD = 9
SWA_HEADS = 8
SWA_DH = 64
WINDOW = 128
MLA_HEADS = 4
MLA_NOPE = 128
MLA_ROPE = 64
MLA_V = 128
MLA_QR = 256
MLA_KVR = 128
ROPE_THETA = 10000.0
NUM_BUCKETS = 32
D_IN = 1216
D_IN_PAD = 1280
SWA_SCALE = SWA_DH ** -0.5
MLA_SCALE = (MLA_NOPE + MLA_ROPE) ** -0.5

ADAM_LR = 0.001
ADAM_B1 = 0.9
ADAM_B2 = 0.999
ADAM_EPS = 1e-08
ADAM_WD = 0.01
ADAM_STEP = 10

V7X_VMEM_LIMIT = 56 * 1024 * 1024
ROW_TILE = 512

NT_DIMS = (((1,), (1,)), ((), ()))
TN_DIMS = (((0,), (0,)), ((), ()))


def _dot(a, b):
    return jnp.dot(a, b, preferred_element_type=F32)


def _dot_nt(a, b):
    return lax.dot_general(a, b, NT_DIMS, preferred_element_type=F32)


def _dot_tn(a, b):
    return lax.dot_general(a, b, TN_DIMS, preferred_element_type=F32)


def _params(sem=None):
    return pltpu.CompilerParams(dimension_semantics=sem, vmem_limit_bytes=V7X_VMEM_LIMIT)


def _rstd(x):
    return lax.rsqrt(jnp.mean(x * x, axis=-1, keepdims=True) + EPS)


def _rms_bwd(dy, xhat, r):
    return r * (dy - xhat * jnp.mean(dy * xhat, axis=-1, keepdims=True))


def _sigmoid(a):
    return 1.0 / (1.0 + jnp.exp(-a))


def _ffn_fwd(x, vecs, wgT, wuT, wd, *, name, tm=256, tf=D_FF):
    S, F = x.shape[0], wd.shape[0]
    tm = min(tm, S)
    ni, nj = S // tm, F // tf

    def body(x_ref, vec_ref, wg_ref, wu_ref, wd_ref, xo_ref, h_ref, a_ref, b_ref, f_ref, acc_ref):
        j = pl.program_id(1)

        @pl.when(j == 0)
        def _():
            xv = x_ref[...]
            hn = xv * _rstd(xv) * vec_ref[0:1, :]
            h_ref[...] = (hn * (1.0 + vec_ref[2:3, :]) + vec_ref[1:2, :]).astype(BF16)

        h = h_ref[...]
        a = _dot_nt(h, wg_ref[...])
        b = _dot_nt(h, wu_ref[...])
        a_ref[...] = a.astype(BF16)
        b_ref[...] = b.astype(BF16)
        part = _dot((a * _sigmoid(a) * b).astype(BF16), wd_ref[...])

        def finish(f):
            f_ref[...] = f
            xo_ref[...] = x_ref[...] + (0.5 * vec_ref[3:4, :]) * f

        if nj == 1:
            finish(part)
        else:
            @pl.when(j == 0)
            def _():
                acc_ref[...] = part

            @pl.when((j > 0) & (j < nj - 1))
            def _():
                acc_ref[...] += part

            @pl.when(j == nj - 1)
            def _():
                finish(acc_ref[...] + part)

    row = pl.BlockSpec((tm, D), lambda i, j: (i, 0))
    wspec = pl.BlockSpec((tf, D), lambda i, j: (j, 0), pipeline_mode=pl.Buffered(1) if nj == 1 else None)
    act = pl.BlockSpec((tm, tf), lambda i, j: (i, j))
    return pl.pallas_call(
        body, name=name, grid=(ni, nj),
        in_specs=[row, pl.BlockSpec((8, D), lambda i, j: (0, 0)), wspec, wspec, wspec],
        out_specs=[row, row, act, act, row],
        out_shape=[jax.ShapeDtypeStruct((S, D), F32), jax.ShapeDtypeStruct((S, D), BF16),
                   jax.ShapeDtypeStruct((S, F), BF16), jax.ShapeDtypeStruct((S, F), BF16),
                   jax.ShapeDtypeStruct((S, D), F32)],
        scratch_shapes=[pltpu.VMEM((tm, D) if nj > 1 else (8, 128), F32)],
        compiler_params=_params(("parallel", "arbitrary")),
    )(x, vecs, wgT, wuT, wd)


def _ffn_bwd_main(h, df, a, b, wgT, wuT, wd, *, name, after=(), tm=2048, tf=256):
    S = h.shape[0]
    tm = min(tm, S)
    ni, nj = S // tm, D_FF // tf

    def body(h_hbm, df_hbm, a_ref, b_ref, wg_ref, wu_ref, wd_ref, *rest):
        gg_ref, gu_ref, gd_ref, dh_hbm, h_v, df_v, dh_v, gg_acc, gu_acc, gd_acc, sem = rest[len(after):]
        j = pl.program_id(0)
        i = pl.program_id(1)

        @pl.when((j == 0) & (i == 0))
        def _():
            c1 = pltpu.make_async_copy(h_hbm, h_v, sem.at[0])
            c2 = pltpu.make_async_copy(df_hbm, df_v, sem.at[1])
            c1.start()
            c2.start()
            c1.wait()
            c2.wait()

        @pl.when(i == 0)
        def _():
            gg_acc[...] = jnp.zeros_like(gg_acc)
            gu_acc[...] = jnp.zeros_like(gu_acc)
            gd_acc[...] = jnp.zeros_like(gd_acc)

        rows = pl.ds(pl.multiple_of(i * tm, tm), tm)
        hi = h_v[rows, :]
        dfi = df_v[rows, :]
        av = a_ref[...].astype(F32)
        bv = b_ref[...].astype(F32)
        sg = _sigmoid(av)
        sa = av * sg
        hsw = (sa * bv).astype(BF16)
        dhsw = _dot_nt(dfi, wd_ref[...])
        da = (dhsw * bv * (sg * (1.0 + av * (1.0 - sg)))).astype(BF16)
        db = (dhsw * sa).astype(BF16)
        gd_acc[...] += _dot_tn(hsw, dfi)
        gg_acc[...] += _dot_tn(da, hi)
        gu_acc[...] += _dot_tn(db, hi)
        dh = _dot(da, wg_ref[...]) + _dot(db, wu_ref[...])

        @pl.when(j == 0)
        def _():
            dh_v[rows, :] = dh

        @pl.when(j > 0)
        def _():
            dh_v[rows, :] += dh

        @pl.when(i == ni - 1)
        def _():
            gg_ref[...] = gg_acc[...].astype(BF16)
            gu_ref[...] = gu_acc[...].astype(BF16)
            gd_ref[...] = gd_acc[...].astype(BF16)

        @pl.when((j == nj - 1) & (i == ni - 1))
        def _():
            c3 = pltpu.make_async_copy(dh_v, dh_hbm, sem.at[2])
            c3.start()
            c3.wait()

    anyspec = pl.BlockSpec(memory_space=pl.ANY)
    wspec = pl.BlockSpec((tf, D), lambda j, i: (j, 0))
    act = pl.BlockSpec((tm, tf), lambda j, i: (i, j))
    return pl.pallas_call(
        body, name=name, grid=(nj, ni),
        in_specs=[anyspec, anyspec, act, act, wspec, wspec, wspec] + [anyspec] * len(after),
        out_specs=[wspec, wspec, wspec, anyspec],
        out_shape=[jax.ShapeDtypeStruct((D_FF, D), BF16)] * 3 + [jax.ShapeDtypeStruct((S, D), F32)],
        scratch_shapes=[pltpu.VMEM((S, D), BF16), pltpu.VMEM((S, D), BF16), pltpu.VMEM((S, D), F32),
                        pltpu.VMEM((tf, D), F32), pltpu.VMEM((tf, D), F32), pltpu.VMEM((tf, D), F32),
                        pltpu.SemaphoreType.DMA((3,))],
        compiler_params=_params(("arbitrary", "arbitrary")),
    )(h, df, a, b, wgT, wuT, wd, *after)


def _ffn_out_bwd(dx, f, gate, df_ref, part_ref):
    df_ref[...] = ((0.5 * gate) * dx).astype(BF16)
    part_ref[3:4, :] += 0.5 * jnp.sum(dx * f, axis=0, keepdims=True)


def _norm_bwd(dh, x, dxo, vecs, *, name, below=None, tm=ROW_TILE):
    S = x.shape[0]
    tm = min(tm, S)

    def body(dh_ref, x_ref, dxo_ref, vec_ref, *rest):
        dx_ref, part_ref = rest[-2 if below is None else -3], rest[-1 if below is None else -2]

        @pl.when(pl.program_id(0) == 0)
        def _():
            part_ref[...] = jnp.zeros_like(part_ref)

        dh = dh_ref[...]
        xv = x_ref[...]
        r = _rstd(xv)
        xhat = xv * r
        w = vec_ref[0:1, :]
        xn = xhat * w
        dxn = dh * (1.0 + vec_ref[2:3, :])
        part_ref[0:1, :] += jnp.sum(dxn * xhat, axis=0, keepdims=True)
        part_ref[1:2, :] += jnp.sum(dh, axis=0, keepdims=True)
        part_ref[2:3, :] += jnp.sum(dh * xn, axis=0, keepdims=True)
        dx = dxo_ref[...] + _rms_bwd(dxn * w, xhat, r)
        dx_ref[...] = dx
        if below is not None:
            _ffn_out_bwd(dx, rest[0][...], rest[1][3:4, :], rest[-1], part_ref)

    row = pl.BlockSpec((tm, D), lambda i: (i, 0))
    vec = pl.BlockSpec((8, D), lambda i: (0, 0))
    extra = [] if below is None else [row, vec]
    return pl.pallas_call(
        body, name=name, grid=(S // tm,), in_specs=[row, row, row, vec] + extra,
        out_specs=[row, vec] + ([] if below is None else [row]),
        out_shape=[jax.ShapeDtypeStruct((S, D), F32), jax.ShapeDtypeStruct((8, D), F32)]
        + ([] if below is None else [jax.ShapeDtypeStruct((S, D), BF16)]),
        compiler_params=_params(("arbitrary",)),
    )(dh, x, dxo, vecs, *([] if below is None else below))


def _head(x, tgt, nf, f, vecs, *, tm=ROW_TILE):
    S = x.shape[0]
    tm = min(tm, S)

    def body(x_ref, t_ref, nf_ref, f_ref, vec_ref, dx_ref, part_ref, df_ref):
        @pl.when(pl.program_id(0) == 0)
        def _():
            part_ref[...] = jnp.zeros_like(part_ref)

        xv = x_ref[...]
        r = _rstd(xv)
        xhat = xv * r
        w = nf_ref[...]
        e = xhat * w - t_ref[...]
        dy = e * (1.0 / D)
        part_ref[0:1, :] += jnp.sum(dy * xhat, axis=0, keepdims=True)
        part_ref[1:2, :] += jnp.sum(e * e) * (0.5 / D)
        dx = _rms_bwd(dy * w, xhat, r)
        dx_ref[...] = dx
        _ffn_out_bwd(dx, f_ref[...], vec_ref[3:4, :], df_ref, part_ref)

    row = pl.BlockSpec((tm, D), lambda i: (i, 0))
    vec = pl.BlockSpec((8, D), lambda i: (0, 0))
    return pl.pallas_call(
        body, name="head", grid=(S // tm,),
        in_specs=[row, row, pl.BlockSpec((1, D), lambda i: (0, 0)), row, vec],
        out_specs=[row, vec, row],
        out_shape=[jax.ShapeDtypeStruct((S, D), F32), jax.ShapeDtypeStruct((8, D), F32),
                   jax.ShapeDtypeStruct((S, D), BF16)],
        compiler_params=_params(("arbitrary",)),
    )(x, tgt, nf, f, vecs)


def _mix_in_fwd(x, vecs, w_inT, *, tm=ROW_TILE):
    S = x.shape[0]
    tm = min(tm, S)

    def body(x_ref, vec_ref, w_ref, h_ref, p_ref, wb_ref):
        @pl.when(pl.program_id(0) == 0)
        def _():
            wb_ref[0:D_IN, :] = w_ref[...].astype(BF16)
            wb_ref[D_IN:D_IN_PAD, :] = jnp.zeros((D_IN_PAD - D_IN, D), BF16)

        xv = x_ref[...]
        hn = xv * _rstd(xv) * vec_ref[0:1, :]
        h = (hn * (1.0 + vec_ref[2:3, :]) + vec_ref[1:2, :]).astype(BF16)
        h_ref[...] = h
        p_ref[...] = _dot_nt(h, wb_ref[...])

    row = pl.BlockSpec((tm, D), lambda i: (i, 0))
    return pl.pallas_call(
        body, name="mix_in_fwd", grid=(S // tm,),
        in_specs=[row, pl.BlockSpec((8, D), lambda i: (0, 0)),
                  pl.BlockSpec((D_IN, D), lambda i: (0, 0), pipeline_mode=pl.Buffered(1))],
        out_specs=[row, pl.BlockSpec((tm, D_IN_PAD), lambda i: (i, 0)), pl.BlockSpec((D_IN_PAD, D), lambda i: (0, 0))],
        out_shape=[jax.ShapeDtypeStruct((S, D), BF16), jax.ShapeDtypeStruct((S, D_IN_PAD), F32),
                   jax.ShapeDtypeStruct((D_IN_PAD, D), BF16)],
        compiler_params=_params(("arbitrary",)),
    )(x, vecs, w_inT)


def _bucket_table():
    qi = np.arange(WINDOW)[:, None]
    kj = np.arange(2 * WINDOW)[None, :]
    dist = qi + WINDOW - kj
    max_exact = NUM_BUCKETS // 2
    n = np.maximum(dist, 0)
    nf = np.maximum(n, 1).astype(np.float32)
    large = max_exact + (np.log(nf / np.float32(max_exact)) / np.float32(math.log(WINDOW / max_exact))
                         * np.float32(NUM_BUCKETS - max_exact)).astype(np.int32)
    large = np.minimum(large, NUM_BUCKETS - 1)
    return np.where(n < max_exact, n, large).astype(np.int32)


SWA_GROUP = 4
GROUP_ROWS = SWA_GROUP * WINDOW


SWA_SUB = 2


def _swa_valid(has_prev):
    row = lax.broadcasted_iota(jnp.int32, (GROUP_ROWS, 2 * WINDOW), 0) % WINDOW
    col = lax.broadcasted_iota(jnp.int32, (GROUP_ROWS, 2 * WINDOW), 1)
    dist = row + WINDOW - col
    return (dist >= 0) & (dist < WINDOW) & ((col >= WINDOW) | has_prev)


def _swa_keys(prev_ref, cur_ref, u):
    cur = cur_ref[...]
    before = prev_ref[...] if u == 0 else cur[WINDOW * (u - 1):WINDOW * u]
    return jnp.concatenate([before, cur[WINDOW * u:WINDOW * (u + 1)]], axis=0).astype(BF16)


def _stack_heads(x, g):
    return jnp.concatenate([x[:, 64 * h:64 * h + 64] for h in range(SWA_GROUP * g, SWA_GROUP * (g + 1))], axis=0)


def _unstack_heads(x4):
    return jnp.concatenate([x4[WINDOW * a:WINDOW * (a + 1)] for a in range(SWA_GROUP)], axis=1)


def _group_sinks(sink_ref, g):
    head = lax.broadcasted_iota(jnp.int32, (GROUP_ROWS, 1), 0) // WINDOW
    out = jnp.full((GROUP_ROWS, 1), sink_ref[0, SWA_GROUP * g], F32)
    for a in range(1, SWA_GROUP):
        out = jnp.where(head == a, sink_ref[0, SWA_GROUP * g + a], out)
    return out


def _swa_probs(qh, kk, bias_h, sink, valid):
    s = _dot_nt(qh, kk) * SWA_SCALE + bias_h
    s = jnp.where(valid, s, -jnp.inf)
    m = jnp.maximum(jnp.max(s, axis=-1, keepdims=True), sink)
    p = jnp.exp(s - m)
    ps = jnp.exp(sink - m)
    inv = 1.0 / (jnp.sum(p, axis=-1, keepdims=True) + ps)
    return p * inv, ps * inv


SWA_ROWS = SWA_SUB * WINDOW


def _swa_specs():
    prev = lambda n: jnp.maximum(SWA_SUB * n - 1, 0)
    return [pl.BlockSpec((SWA_ROWS, 512), lambda n: (n, 0)),
            pl.BlockSpec((SWA_ROWS, 128), lambda n: (n, 4)),
            pl.BlockSpec((WINDOW, 128), lambda n: (prev(n), 4)),
            pl.BlockSpec((SWA_ROWS, 128), lambda n: (n, 5)),
            pl.BlockSpec((WINDOW, 128), lambda n: (prev(n), 5)),
            pl.BlockSpec((SWA_HEADS, WINDOW, 2 * WINDOW), lambda n: (0, 0, 0)),
            pl.BlockSpec(memory_space=pltpu.SMEM)]


def _swa_fwd(proj, rel_bias, bucket, sinks):
    S = proj.shape[0]

    def body(q_ref, kc_ref, kp_ref, vc_ref, vp_ref, rb_ref, sink_ref, bk_ref, o_ref, bias_ref):
        n = pl.program_id(0)

        @pl.when(n == 0)
        def _():
            bk = bk_ref[...]
            for h in range(SWA_HEADS):
                acc = jnp.zeros((WINDOW, 2 * WINDOW), F32)
                for b in range(NUM_BUCKETS):
                    acc = jnp.where(bk == b, rb_ref[b, h], acc)
                bias_ref[h] = acc

        for u in range(SWA_SUB):
            rows = slice(WINDOW * u, WINDOW * (u + 1))
            valid = _swa_valid(n > 0 if u == 0 else True)
            q = q_ref[rows, :].astype(BF16)
            kfull = _swa_keys(kp_ref, kc_ref, u)
            vfull = _swa_keys(vp_ref, vc_ref, u)
            for g in range(SWA_HEADS // SWA_GROUP):
                kk = kfull[:, 64 * g:64 * g + 64]
                vv = vfull[:, 64 * g:64 * g + 64]
                bias4 = bias_ref[SWA_GROUP * g:SWA_GROUP * (g + 1)].reshape(GROUP_ROWS, 2 * WINDOW)
                pk, _ = _swa_probs(_stack_heads(q, g), kk, bias4, _group_sinks(sink_ref, g), valid)
                o_ref[rows, 256 * g:256 * (g + 1)] = _unstack_heads(_dot(pk.astype(BF16), vv))

    specs = _swa_specs()
    whole = pl.BlockSpec((SWA_HEADS, WINDOW, 2 * WINDOW), lambda n: (0, 0, 0))
    return pl.pallas_call(
        body, name="swa_fwd", grid=(S // SWA_ROWS,),
        in_specs=specs[:5] + [pl.BlockSpec(memory_space=pltpu.SMEM), specs[6],
                              pl.BlockSpec((WINDOW, 2 * WINDOW), lambda n: (0, 0))],
        out_specs=[pl.BlockSpec((SWA_ROWS, 512), lambda n: (n, 0)), whole],
        out_shape=[jax.ShapeDtypeStruct((S, 512), F32), jax.ShapeDtypeStruct((SWA_HEADS, WINDOW, 2 * WINDOW), F32)],
        compiler_params=_params(("arbitrary",)),
    )(proj, proj, proj, proj, proj, rel_bias, sinks, bucket)


def _swa_bwd(proj, bias, sinks, o, do, bucket):
    S = proj.shape[0]
    nb = S // SWA_ROWS

    def body(q_ref, kc_ref, kp_ref, vc_ref, vp_ref, bias_ref, sink_ref, o_ref, do_ref, bk_ref,
             dq_ref, dk_ref, dv_ref, drb_ref, dsk_ref, dbias_acc):
        n = pl.program_id(0)

        @pl.when(n == 0)
        def _():
            dk_ref[...] = jnp.zeros_like(dk_ref)
            dv_ref[...] = jnp.zeros_like(dv_ref)
            dsk_ref[...] = jnp.zeros_like(dsk_ref)
            dbias_acc[...] = jnp.zeros_like(dbias_acc)
            drb_ref[...] = jnp.zeros_like(drb_ref)

        for u in range(SWA_SUB):
            rows = slice(WINDOW * u, WINDOW * (u + 1))
            blk = SWA_SUB * n + u
            valid = _swa_valid(n > 0 if u == 0 else True)
            q = q_ref[rows, :].astype(BF16)
            dov = do_ref[rows, :]
            ov = o_ref[rows, :]
            kfull = _swa_keys(kp_ref, kc_ref, u)
            vfull = _swa_keys(vp_ref, vc_ref, u)
            prow = pl.ds(pl.multiple_of(jnp.maximum(blk - 1, 0) * WINDOW, WINDOW), WINDOW)
            crow = pl.ds(pl.multiple_of(blk * WINDOW, WINDOW), WINDOW)
            for g in range(SWA_HEADS // SWA_GROUP):
                heads = slice(SWA_GROUP * g, SWA_GROUP * (g + 1))
                kk = kfull[:, 64 * g:64 * g + 64]
                vv = vfull[:, 64 * g:64 * g + 64]
                q4 = _stack_heads(q, g)
                pk, psink = _swa_probs(q4, kk, bias_ref[heads].reshape(GROUP_ROWS, 2 * WINDOW),
                                       _group_sinks(sink_ref, g), valid)
                pkb = pk.astype(BF16)
                do4 = _stack_heads(dov, g)
                dob = do4.astype(BF16)
                dp = _dot_nt(dob, vv)
                delta = jnp.sum(do4 * _stack_heads(ov, g), axis=-1, keepdims=True)
                ds = pk * (dp - delta)
                dsink = -psink * delta
                for a in range(SWA_GROUP):
                    h = SWA_GROUP * g + a
                    part = jnp.sum(dsink[WINDOW * a:WINDOW * (a + 1)], keepdims=True)
                    dsk_ref[h:h + 1, :] += jnp.broadcast_to(part, (1, 128))
                dbias_acc[heads] += ds.reshape(SWA_GROUP, WINDOW, 2 * WINDOW)
                dsb = (ds * SWA_SCALE).astype(BF16)
                dq_ref[rows, 256 * g:256 * (g + 1)] = _unstack_heads(_dot(dsb, kk))
                dkk = _dot_tn(dsb, q4)
                dvv = _dot_tn(pkb, dob)
                dk_ref[prow, 64 * g:64 * g + 64] += dkk[:WINDOW]
                dk_ref[crow, 64 * g:64 * g + 64] += dkk[WINDOW:]
                dv_ref[prow, 64 * g:64 * g + 64] += dvv[:WINDOW]
                dv_ref[crow, 64 * g:64 * g + 64] += dvv[WINDOW:]

        @pl.when(n == nb - 1)
        def _():
            bk = bk_ref[...]
            for h in range(SWA_HEADS):
                dbh = dbias_acc[h]
                for b in range(NUM_BUCKETS):
                    val = jnp.sum(jnp.where(bk == b, dbh, 0.0), keepdims=True)
                    row = h * NUM_BUCKETS + b
                    drb_ref[row:row + 1, :] = jnp.broadcast_to(val, (1, 128))

    full = lambda shape: pl.BlockSpec(shape, lambda n: tuple(0 for _ in shape))
    return pl.pallas_call(
        body, name="swa_bwd", grid=(nb,),
        in_specs=_swa_specs() + [pl.BlockSpec((SWA_ROWS, 512), lambda n: (n, 0)),
                                 pl.BlockSpec((SWA_ROWS, 512), lambda n: (n, 0)), full((WINDOW, 2 * WINDOW))],
        out_specs=[pl.BlockSpec((SWA_ROWS, 512), lambda n: (n, 0)), full((S, 128)), full((S, 128)),
                   full((NUM_BUCKETS * 8, 128)), full((8, 128))],
        out_shape=[jax.ShapeDtypeStruct((S, 512), F32), jax.ShapeDtypeStruct((S, 128), F32),
                   jax.ShapeDtypeStruct((S, 128), F32), jax.ShapeDtypeStruct((NUM_BUCKETS * 8, 128), F32),
                   jax.ShapeDtypeStruct((8, 128), F32)],
        scratch_shapes=[pltpu.VMEM((SWA_HEADS, WINDOW, 2 * WINDOW), F32)],
        compiler_params=_params(("arbitrary",)),
    )(proj, proj, proj, proj, proj, bias, sinks, o, do, bucket)


def _rope_tables(S):
    inv = np.float32(ROPE_THETA) ** (-np.arange(0, MLA_ROPE, 2, dtype=np.float32) / np.float32(MLA_ROPE))
    ang = np.arange(S, dtype=np.float32)[:, None] * inv[None, :]
    cos, sin = np.cos(ang), np.sin(ang)
    return (jnp.asarray(np.tile(np.concatenate([cos, cos], axis=1), (1, 2))),
            jnp.asarray(np.tile(np.concatenate([-sin, sin], axis=1), (1, 2))))


def _rope_wide(ref):
    t = ref[...]
    return jnp.concatenate([t, t], axis=1)


def _swap_halves(x):
    w = x.shape[-1]
    lane = lax.broadcasted_iota(jnp.int32, x.shape, x.ndim - 1)
    return jnp.where((lane % 64) < 32, pltpu.roll(x, w - 32, x.ndim - 1), pltpu.roll(x, 32, x.ndim - 1))


def _uq_group_rows(wuq_ref):
    per = MLA_NOPE + MLA_ROPE
    nope = [wuq_ref[per * h:per * h + MLA_NOPE, :] for h in range(MLA_HEADS)]
    rope = [wuq_ref[per * h + MLA_NOPE:per * (h + 1), :] for h in range(MLA_HEADS)]
    return jnp.concatenate(nope + rope, axis=0)


def _mla_pre_fwd(proj, qn_w, kvn_w, wuqT, wukv, cos, sin, *, tm=ROW_TILE):
    S = proj.shape[0]
    tm = min(tm, S)

    def body(ql_ref, kl_ref, kr_ref, qw_ref, kw_ref, wuq_ref, wukv_ref, cos_ref, sin_ref,
             qc_ref, kc_ref, vv_ref):
        ql = ql_ref[...]
        qn = (ql * _rstd(ql) * qw_ref[...]).astype(BF16)
        q = _dot_nt(qn, _uq_group_rows(wuq_ref))
        cs, sn = _rope_wide(cos_ref), _rope_wide(sin_ref)
        qr = q[:, 512:768]
        qr = qr * cs + _swap_halves(qr) * sn
        half = lax.broadcasted_iota(jnp.int32, (tm, 128), 1) // 64
        kl = kl_ref[...]
        kvn = (kl * _rstd(kl) * kw_ref[...]).astype(BF16)
        kr = kr_ref[...]
        kr = kr * cs[:, :128] + _swap_halves(kr) * sn[:, :128]
        kr2 = (kr + pltpu.roll(kr, 64, 1)).astype(BF16)
        kv = _dot(kvn, jnp.concatenate([wukv_ref[j] for j in range(2 * MLA_HEADS)], axis=1)).astype(BF16)
        for h in range(MLA_HEADS):
            qc_ref[h, :, 0:128] = q[:, 128 * h:128 * h + 128].astype(BF16)
            chunk = qr[:, 128 * (h // 2):128 * (h // 2) + 128]
            qc_ref[h, :, 128:256] = jnp.where(half == (h % 2), chunk, 0.0).astype(BF16)
            kc_ref[h, :, 0:128] = kv[:, 256 * h:256 * h + 128]
            kc_ref[h, :, 128:256] = kr2
            vv_ref[h] = kv[:, 256 * h + 128:256 * h + 256]

    const = lambda shape: pl.BlockSpec(shape, lambda i: tuple(0 for _ in shape))
    return pl.pallas_call(
        body, name="mla_pre_fwd", grid=(S // tm,),
        in_specs=[pl.BlockSpec((tm, 256), lambda i: (i, 3)), pl.BlockSpec((tm, 128), lambda i: (i, 8)),
                  pl.BlockSpec((tm, 128), lambda i: (i, 9)), const((1, 256)), const((1, 128)),
                  const((768, 256)), const((8, 128, 128)),
                  pl.BlockSpec((tm, 128), lambda i: (i, 0)), pl.BlockSpec((tm, 128), lambda i: (i, 0))],
        out_specs=[pl.BlockSpec((MLA_HEADS, tm, 256), lambda i: (0, i, 0)),
                   pl.BlockSpec((MLA_HEADS, tm, 256), lambda i: (0, i, 0)),
                   pl.BlockSpec((MLA_HEADS, tm, 128), lambda i: (0, i, 0))],
        out_shape=[jax.ShapeDtypeStruct((MLA_HEADS, S, 256), BF16), jax.ShapeDtypeStruct((MLA_HEADS, S, 256), BF16),
                   jax.ShapeDtypeStruct((MLA_HEADS, S, 128), BF16)],
        compiler_params=_params(("parallel",)),
    )(proj, proj, proj, qn_w, kvn_w, wuqT, wukv, cos, sin)


def _causal(i, j, t):
    row = i * t + lax.broadcasted_iota(jnp.int32, (t, t), 0)
    col = j * t + lax.broadcasted_iota(jnp.int32, (t, t), 1)
    return col <= row


def _mla_attn_fwd(qc, kc, vv, *, t=512):
    S = qc.shape[1]
    t = min(t, S)

    def body(q_ref, k_ref, v_ref, o_ref, l_ref):
        i = pl.program_id(0)
        diag = _causal(0, 0, t)

        def step(j, carry, masked):
            rows = pl.ds(pl.multiple_of(j * t, t), t)
            out = []
            for h in range(MLA_HEADS):
                m, l, acc = carry[h]
                s = _dot_nt(q_ref[h], k_ref[h, rows, :]) * MLA_SCALE
                if masked:
                    s = jnp.where(diag, s, -jnp.inf)
                m_new = jnp.maximum(m, jnp.max(s, axis=-1, keepdims=True))
                alpha = jnp.exp(m - m_new)
                p = jnp.exp(s - m_new)
                l = alpha * l + jnp.sum(p, axis=-1, keepdims=True)
                acc = alpha * acc + _dot(p.astype(BF16), v_ref[h, rows, :])
                out.append((m_new, l, acc))
            return tuple(out)

        init = tuple((jnp.full((t, 1), -jnp.inf, F32), jnp.zeros((t, 1), F32), jnp.zeros((t, MLA_V), F32))
                     for _ in range(MLA_HEADS))
        carry = lax.fori_loop(0, i, lambda j, c: step(j, c, False), init)
        carry = step(i, carry, True)
        for h in range(MLA_HEADS):
            m, l, acc = carry[h]
            o_ref[:, 128 * h:128 * h + 128] = acc / l
            l_ref[h] = jnp.broadcast_to(m + jnp.log(l), (t, 128))

    return pl.pallas_call(
        body, name="mla_attn_fwd", grid=(S // t,),
        in_specs=[pl.BlockSpec((MLA_HEADS, t, 256), lambda i: (0, i, 0)),
                  pl.BlockSpec((MLA_HEADS, S, 256), lambda i: (0, 0, 0)),
                  pl.BlockSpec((MLA_HEADS, S, 128), lambda i: (0, 0, 0))],
        out_specs=[pl.BlockSpec((t, 512), lambda i: (i, 0)),
                   pl.BlockSpec((MLA_HEADS, t, 128), lambda i: (0, i, 0))],
        out_shape=[jax.ShapeDtypeStruct((S, 512), F32), jax.ShapeDtypeStruct((MLA_HEADS, S, 128), F32)],
        compiler_params=_params(("parallel",)),
    )(qc, kc, vv)


def _mla_attn_bwd(qc, kc, vv, o, lse, do, *, t=512, tq=1024):
    S = qc.shape[1]
    t = min(t, S)
    tq = min(tq, S)
    nblk = S // t
    hp = MLA_HEADS
    once = pl.Buffered(1)

    def body(q_ref, k_ref, v_ref, o_ref, l_ref, do_ref, dq_ref, dk_ref, dv_ref):
        j = pl.program_id(1)

        @pl.when(j == 0)
        def _():
            dq_ref[...] = jnp.zeros_like(dq_ref)

        first = (j * t) // tq

        def step(i, carry, masked):
            rows = pl.ds(pl.multiple_of(i * tq, tq), tq)
            if masked:
                row = i * tq + lax.broadcasted_iota(jnp.int32, (tq, t), 0)
                col = j * t + lax.broadcasted_iota(jnp.int32, (tq, t), 1)
                visible = col <= row
            out = []
            for h in range(hp):
                dk, dv = carry[h]
                k = k_ref[h]
                q = q_ref[h, rows, :]
                dov = do_ref[rows, 128 * h:128 * h + 128]
                lrow = l_ref[h, rows, :][:, 0:1]
                p = jnp.exp(_dot_nt(q, k) * MLA_SCALE - lrow)
                if masked:
                    p = jnp.where(visible, p, 0.0)
                dob = dov.astype(BF16)
                dv = dv + _dot_tn(p.astype(BF16), dob)
                dp = _dot_nt(dob, v_ref[h])
                delta = jnp.sum(dov * o_ref[rows, 128 * h:128 * h + 128], axis=-1, keepdims=True)
                ds = (p * (dp - delta) * MLA_SCALE).astype(BF16)
                dk = dk + _dot_tn(ds, q)
                dq_ref[h, rows, :] += _dot(ds, k)
                out.append((dk, dv))
            return tuple(out)

        init = tuple((jnp.zeros((t, 256), F32), jnp.zeros((t, MLA_V), F32)) for _ in range(hp))
        carry = step(first, init, True)
        carry = lax.fori_loop(first + 1, S // tq, lambda i, c: step(i, c, False), carry)
        for h in range(hp):
            dk_ref[h] = carry[h][0]
            dv_ref[h] = carry[h][1]

    return pl.pallas_call(
        body, name="mla_attn_bwd", grid=(MLA_HEADS // hp, nblk),
        in_specs=[pl.BlockSpec((hp, S, 256), lambda g, j: (g, 0, 0), pipeline_mode=once),
                  pl.BlockSpec((hp, t, 256), lambda g, j: (g, j, 0)),
                  pl.BlockSpec((hp, t, 128), lambda g, j: (g, j, 0)),
                  pl.BlockSpec((S, 128 * hp), lambda g, j: (0, g), pipeline_mode=once),
                  pl.BlockSpec((hp, S, 128), lambda g, j: (g, 0, 0), pipeline_mode=once),
                  pl.BlockSpec((S, 128 * hp), lambda g, j: (0, g), pipeline_mode=once)],
        out_specs=[pl.BlockSpec((hp, S, 256), lambda g, j: (g, 0, 0)),
                   pl.BlockSpec((hp, t, 256), lambda g, j: (g, j, 0)),
                   pl.BlockSpec((hp, t, 128), lambda g, j: (g, j, 0))],
        out_shape=[jax.ShapeDtypeStruct((MLA_HEADS, S, 256), F32), jax.ShapeDtypeStruct((MLA_HEADS, S, 256), F32),
                   jax.ShapeDtypeStruct((MLA_HEADS, S, 128), F32)],
        compiler_params=_params(("parallel", "arbitrary")),
    )(qc, kc, vv, o, lse, do)


def _mla_pre_bwd(proj, qn_w, kvn_w, wuqT, wukv, cos, sin, dqc, dkc, dvv, *, tm=ROW_TILE):
    S = proj.shape[0]
    tm = min(tm, S)

    def body(ql_ref, kl_ref, qw_ref, kw_ref, wuq_ref, wukv_ref, cos_ref, sin_ref, dqc_ref, dkc_ref, dvv_ref,
             dql_ref, dkl_ref, dkr_ref, gq_ref, gkv_ref, part_ref, gq_acc, gkv_acc):
        @pl.when(pl.program_id(0) == 0)
        def _():
            gq_acc[...] = jnp.zeros_like(gq_acc)
            gkv_acc[...] = jnp.zeros_like(gkv_acc)
            part_ref[...] = jnp.zeros_like(part_ref)

        cs, sn = _rope_wide(cos_ref), _rope_wide(sin_ref)
        half = lax.broadcasted_iota(jnp.int32, (tm, 128), 1) // 64
        ql = ql_ref[...]
        rq = _rstd(ql)
        qhat = ql * rq
        qw = qw_ref[...]
        qn = (qhat * qw).astype(BF16)
        chunks = []
        for pair in range(2):
            chunks.append(jnp.where(half == 0, dqc_ref[2 * pair, :, 128:256], dqc_ref[2 * pair + 1, :, 128:256]))
        dqr = jnp.concatenate(chunks, axis=1)
        dqr = dqr * cs + _swap_halves(dqr * sn)
        dq = jnp.concatenate([dqc_ref[h, :, 0:128] for h in range(MLA_HEADS)] + [dqr], axis=1).astype(BF16)
        gq_acc[...] += _dot_tn(dq, qn)
        dqn = _dot(dq, _uq_group_rows(wuq_ref))
        part_ref[0:1, :] += jnp.sum(dqn * qhat, axis=0, keepdims=True)
        dql_ref[...] = _rms_bwd(dqn * qw, qhat, rq)
        kl = kl_ref[...]
        rk = _rstd(kl)
        khat = kl * rk
        kw = kw_ref[...]
        kvn = (khat * kw).astype(BF16)
        dkvn = jnp.zeros((tm, MLA_KVR), F32)
        dkr2 = jnp.zeros((tm, 128), F32)
        for h in range(MLA_HEADS):
            dkn = dkc_ref[h, :, 0:128].astype(BF16)
            dvh = dvv_ref[h].astype(BF16)
            gkv_acc[2 * h] += _dot_tn(kvn, dkn)
            gkv_acc[2 * h + 1] += _dot_tn(kvn, dvh)
            dkvn += _dot_nt(dkn, wukv_ref[2 * h]) + _dot_nt(dvh, wukv_ref[2 * h + 1])
            dkr2 += dkc_ref[h, :, 128:256]
        part_ref[1:2, 0:128] += jnp.sum(dkvn * khat, axis=0, keepdims=True)
        dkl_ref[...] = _rms_bwd(dkvn * kw, khat, rk)
        dkr = jnp.where(half == 0, dkr2 + pltpu.roll(dkr2, 64, 1), 0.0)
        dkr_ref[...] = dkr * cs[:, :128] + _swap_halves(dkr * sn[:, :128])

        @pl.when(pl.program_id(0) == S // tm - 1)
        def _():
            gkv_ref[...] = gkv_acc[...].astype(BF16)
            per = MLA_NOPE + MLA_ROPE
            for h in range(MLA_HEADS):
                gq_ref[per * h:per * h + MLA_NOPE, :] = gq_acc[MLA_NOPE * h:MLA_NOPE * (h + 1), :].astype(BF16)
                gq_ref[per * h + MLA_NOPE:per * (h + 1), :] = gq_acc[512 + MLA_ROPE * h:512 + MLA_ROPE * (h + 1), :].astype(BF16)

    const = lambda shape: pl.BlockSpec(shape, lambda i: tuple(0 for _ in shape))
    heads = lambda w: pl.BlockSpec((MLA_HEADS, tm, w), lambda i: (0, i, 0))
    return pl.pallas_call(
        body, name="mla_pre_bwd", grid=(S // tm,),
        in_specs=[pl.BlockSpec((tm, 256), lambda i: (i, 3)), pl.BlockSpec((tm, 128), lambda i: (i, 8)),
                  const((1, 256)), const((1, 128)), const((768, 256)), const((8, 128, 128)),
                  pl.BlockSpec((tm, 128), lambda i: (i, 0)), pl.BlockSpec((tm, 128), lambda i: (i, 0)),
                  heads(256), heads(256), heads(128)],
        out_specs=[pl.BlockSpec((tm, 256), lambda i: (i, 0)), pl.BlockSpec((tm, 128), lambda i: (i, 0)),
                   pl.BlockSpec((tm, 128), lambda i: (i, 0)), const((768, 256)), const((8, 128, 128)), const((8, 256))],
        out_shape=[jax.ShapeDtypeStruct((S, 256), F32), jax.ShapeDtypeStruct((S, 128), F32),
                   jax.ShapeDtypeStruct((S, 128), F32), jax.ShapeDtypeStruct((768, 256), BF16),
                   jax.ShapeDtypeStruct((8, 128, 128), BF16), jax.ShapeDtypeStruct((8, 256), F32)],
        scratch_shapes=[pltpu.VMEM((768, 256), F32), pltpu.VMEM((8, 128, 128), F32)],
        compiler_params=_params(("arbitrary",)),
    )(proj, proj, qn_w, kvn_w, wuqT, wukv, cos, sin, dqc, dkc, dvv)


def _mix_out_fwd(x, oa, ob, w_o, vecs, *, tm=ROW_TILE):
    S = x.shape[0]
    tm = min(tm, S)

    def body(x_ref, oa_ref, ob_ref, w_ref, vec_ref, xo_ref, mo_ref):
        mo = _dot(oa_ref[...].astype(BF16), w_ref[0:512, :]) + _dot(ob_ref[...].astype(BF16), w_ref[512:1024, :])
        mo_ref[...] = mo
        xo_ref[...] = x_ref[...] + vec_ref[3:4, :] * mo

    row = pl.BlockSpec((tm, D), lambda i: (i, 0))
    half = pl.BlockSpec((tm, 512), lambda i: (i, 0))
    return pl.pallas_call(
        body, name="mix_out_fwd", grid=(S // tm,),
        in_specs=[row, half, half, pl.BlockSpec((D, D), lambda i: (0, 0)), pl.BlockSpec((8, D), lambda i: (0, 0))],
        out_specs=[row, row],
        out_shape=[jax.ShapeDtypeStruct((S, D), F32), jax.ShapeDtypeStruct((S, D), F32)],
        compiler_params=_params(("parallel",)),
    )(x, oa, ob, w_o, vecs)


def _mix_out_bwd(dxo, mo, oa, ob, w_o, vecs, *, tm=ROW_TILE):
    S = dxo.shape[0]
    tm = min(tm, S)

    def body(dx_ref, mo_ref, oa_ref, ob_ref, w_ref, vec_ref, doa_ref, dob_ref, gw_ref, part_ref, gw_acc):
        @pl.when(pl.program_id(0) == 0)
        def _():
            gw_acc[...] = jnp.zeros_like(gw_acc)
            part_ref[...] = jnp.zeros_like(part_ref)

        dx = dx_ref[...]
        part_ref[0:1, :] += jnp.sum(dx * mo_ref[...], axis=0, keepdims=True)
        dmo = (vec_ref[3:4, :] * dx).astype(BF16)
        doa_ref[...] = _dot_nt(dmo, w_ref[0:512, :])
        dob_ref[...] = _dot_nt(dmo, w_ref[512:1024, :])
        gw_acc[0:512, :] += _dot_tn(oa_ref[...].astype(BF16), dmo)
        gw_acc[512:1024, :] += _dot_tn(ob_ref[...].astype(BF16), dmo)

        @pl.when(pl.program_id(0) == S // tm - 1)
        def _():
            gw_ref[...] = gw_acc[...].astype(BF16)

    row = pl.BlockSpec((tm, D), lambda i: (i, 0))
    half = pl.BlockSpec((tm, 512), lambda i: (i, 0))
    return pl.pallas_call(
        body, name="mix_out_bwd", grid=(S // tm,),
        in_specs=[row, row, half, half, pl.BlockSpec((D, D), lambda i: (0, 0)), pl.BlockSpec((8, D), lambda i: (0, 0))],
        out_specs=[half, half, pl.BlockSpec((D, D), lambda i: (0, 0)), pl.BlockSpec((8, D), lambda i: (0, 0))],
        out_shape=[jax.ShapeDtypeStruct((S, 512), F32), jax.ShapeDtypeStruct((S, 512), F32),
                   jax.ShapeDtypeStruct((D, D), BF16), jax.ShapeDtypeStruct((8, D), F32)],
        scratch_shapes=[pltpu.VMEM((D, D), F32)],
        compiler_params=_params(("arbitrary",)),
    )(dxo, mo, oa, ob, w_o, vecs)


def _mix_in_bwd(h, w_inT, dq, dk, dv, dql, dkl, dkr, *, tm=ROW_TILE):
    S = h.shape[0]
    tm = min(tm, S)
    wid = (512, 128, 128, 256, 128, 128)

    def body(h_ref, w_ref, dq_ref, dk_ref, dv_ref, dql_ref, dkl_ref, dkr_ref, dh_ref, gw_ref):
        @pl.when(pl.program_id(0) == 0)
        def _():
            gw_ref[...] = jnp.zeros_like(gw_ref)

        parts = (dq_ref, dk_ref, dv_ref, dql_ref, dkl_ref, dkr_ref)
        dproj = jnp.concatenate([ref[...].astype(BF16) for ref in parts], axis=1)
        dh_ref[...] = _dot(dproj, w_ref[...])
        gw_ref[...] += _dot_tn(dproj, h_ref[...])[0:D_IN, :]

    row = pl.BlockSpec((tm, D), lambda i: (i, 0))
    part = lambda w: pl.BlockSpec((tm, w), lambda i: (i, 0))
    return pl.pallas_call(
        body, name="mix_in_bwd", grid=(S // tm,),
        in_specs=[row, pl.BlockSpec((D_IN_PAD, D), lambda i: (0, 0))] + [part(w) for w in wid],
        out_specs=[row, pl.BlockSpec((D_IN, D), lambda i: (0, 0))],
        out_shape=[jax.ShapeDtypeStruct((S, D), F32), jax.ShapeDtypeStruct((D_IN, D), F32)],
        compiler_params=_params(("arbitrary",)),
    )(h, w_inT, dq, dk, dv, dql, dkl, dkr)


def _vecs(norm_w, mod9, k):
    return jnp.concatenate([norm_w.reshape(1, D), mod9[3 * k:3 * k + 3], jnp.zeros((4, D), F32)], axis=0)


def _local_step(x, tgt, mod9, norms, sinks, rel_bias, q_norm, kv_norm, W, on_grads=None):
    if on_grads is None:
        on_grads = lambda group, grads, after, vecs: vecs
    S = x.shape[0]
    v1 = _vecs(norms["ffn1"], mod9, 0)
    v2 = _vecs(norms["mix"], mod9, 1)
    v3 = _vecs(norms["ffn2"], mod9, 2)
    bucket = jnp.asarray(_bucket_table())
    cos, sin = _rope_tables(S)
    if isinstance(W, dict):
        full, W = W, (lambda group, after, vecs: (full, vecs))

    W1, v1 = W("ffn1", [], v1)
    x1, h1, a1, b1, f1 = _ffn_fwd(x, v1, W1["g1T"], W1["u1T"], W1["d1"], name="ffn1_fwd")
    W2, v2 = W("mixer", [x1], v2)
    wuqT = W2["w_uqT"]
    h2, proj, w_inT = _mix_in_fwd(x1, v2, W2["w_inT"])
    oa, bias = _swa_fwd(proj, rel_bias, bucket, sinks)
    qc, kc, vv = _mla_pre_fwd(proj, q_norm, kv_norm, wuqT, W2["w_ukv"], cos, sin)
    ob, lse = _mla_attn_fwd(qc, kc, vv)
    _, v2o = W("ffn2_on_its_way", [oa, ob], v2)
    x2, mo = _mix_out_fwd(x1, oa, ob, W2["w_o"], v2o)
    W3, v3 = W("ffn2", [x2], v3)
    x3, h3, a3, b3, f3 = _ffn_fwd(x2, v3, W3["g3T"], W3["u3T"], W3["d3"], name="ffn2_fwd")
    dx3, head_part, df3 = _head(x3, tgt, norms["final"], f3, v3)

    gg3, gu3, gd3, dh3 = _ffn_bwd_main(h3, df3, a3, b3, W3["g3T"], W3["u3T"], W3["d3"], name="ffn2_bwd")
    ffn2 = {"g3T": gg3, "u3T": gu3, "d3": gd3}
    v3 = on_grads("ffn2", ffn2, [], v3)
    dx2, n3_part = _norm_bwd(dh3, x2, dx3, v3, name="ffn2_norm_bwd")
    v2 = on_grads("ffn2", None, [dx2], v2)
    doa, dob, g_wo, g2_part = _mix_out_bwd(dx2, mo, oa, ob, W2["w_o"], v2)
    dq, dk, dv, drb, dsk = _swa_bwd(proj, bias, sinks, oa, doa, bucket)
    dqc, dkc, dvv = _mla_attn_bwd(qc, kc, vv, ob, lse, dob)
    dql, dkl, dkr, g_uq, g_ukv, mla_part = _mla_pre_bwd(proj, q_norm, kv_norm, wuqT, W2["w_ukv"], cos, sin, dqc, dkc, dvv)
    dh2, g_win = _mix_in_bwd(h2, w_inT, dq, dk, dv, dql, dkl, dkr)
    mixer = {"w_inT": g_win, "w_uqT": g_uq, "w_ukv": g_ukv, "w_o": g_wo}
    v2 = on_grads("mixer", mixer, [], v2)
    dx1, n2_part, df1 = _norm_bwd(dh2, x1, dx2, v2, name="mix_norm_bwd", below=(f1, v1))
    started = on_grads("mixer", None, [dx1], jnp.zeros((1, 1), F32))
    gg1, gu1, gd1, dh1 = _ffn_bwd_main(h1, df1, a1, b1, W1["g1T"], W1["u1T"], W1["d1"], name="ffn1_bwd",
                                       after=[started])
    ffn1 = {"g1T": gg1, "u1T": gu1, "d1": gd1}
    v1 = on_grads("ffn1", ffn1, [], v1)
    dx0, n1_part = _norm_bwd(dh1, x, dx1, v1, name="ffn1_norm_bwd")

    grads = {**ffn1, **ffn2, **mixer}
    return head_part[1, 0], dx0, grads, _pack_vec(n1_part, n2_part, n3_part, head_part, g2_part, mla_part, dsk, drb)


SMALL_LAYOUT = (("norm_ffn1", 1024), ("norm_mix", 1024), ("norm_ffn2", 1024), ("norm_final", 1024),
                ("q_norm", 256), ("kv_norm", 128), ("sinks", 128), ("rel_bias", 256))
N_SMALL = sum(n for _, n in SMALL_LAYOUT)
LOSS_SLOT = 4 * 1024 + 256 + 128 + SWA_HEADS
N_MODVEC = N_MOD * D
N_VEC = N_MODVEC + N_SMALL


def _pack_vec(n1, n2, n3, head, g2, mla, dsk, drb):
    def body(n1_ref, n2_ref, n3_ref, head_ref, g2_ref, mla_ref, dsk_ref, drb_ref, out_ref):
        rows = [n1_ref[1:2, :], n1_ref[2:3, :], n2_ref[3:4, :], n2_ref[1:2, :], n2_ref[2:3, :], g2_ref[0:1, :],
                n3_ref[1:2, :], n3_ref[2:3, :], head_ref[3:4, :],
                n1_ref[0:1, :], n2_ref[0:1, :], n3_ref[0:1, :], head_ref[0:1, :]]
        for i, row in enumerate(rows):
            out_ref[:, D * i:D * (i + 1)] = row
        off = D * len(rows)
        out_ref[:, off:off + 256] = mla_ref[0:1, :]
        out_ref[:, off + 256:off + 384] = mla_ref[1:2, 0:128]

        def diagonal(block):
            r = lax.broadcasted_iota(jnp.int32, block.shape, 0)
            lane = lax.broadcasted_iota(jnp.int32, block.shape, 1)
            return jnp.sum(jnp.where(r == lane, block, 0.0), axis=0, keepdims=True)

        lane = lax.broadcasted_iota(jnp.int32, (1, 128), 1)
        out_ref[:, off + 384:off + 512] = jnp.where(lane == SWA_HEADS, head_ref[1:2, 0:128], diagonal(dsk_ref[...]))
        out_ref[:, off + 512:off + 640] = diagonal(drb_ref[0:128, :])
        out_ref[:, off + 640:off + 768] = diagonal(drb_ref[128:256, :])

    vm = pl.BlockSpec(memory_space=pltpu.VMEM)
    return pl.pallas_call(body, name="pack_vec", in_specs=[vm] * 8, out_specs=vm,
                          out_shape=jax.ShapeDtypeStruct((1, N_VEC), F32))(n1, n2, n3, head, g2, mla, dsk, drb)


def _coords():
    return lax.axis_index("x"), lax.axis_index("y"), lax.axis_index("c")


def _flip(v, bit):
    return 1 - v if bit else v


def _peer(r):
    x, y, c = _coords()
    return (_flip(x, r & 4), _flip(y, r & 2), _flip(c, r & 1))


class _ModExchange:
    def __init__(self, c_ref, w_ref, b_ref, mod_ref, ca_ref, call_ref, part_ref, send_sems, recv_sems):
        self.refs = (c_ref, w_ref, b_ref, mod_ref, ca_ref, call_ref, part_ref)
        self.sems = (send_sems, recv_sems)
        x, y, c = _coords()
        self.me = 4 * x + 2 * y + c
        self.sends = []

    def _copy(self, phase, r):
        c_ref, _, _, mod_ref, _, call_ref, part_ref = self.refs
        src, dst = ((call_ref.at[self.me], call_ref.at[self.me]) if phase == 0 else
                    (part_ref.at[self.me ^ r], mod_ref.at[self.me]))
        return pltpu.make_async_remote_copy(src, dst, self.sems[0].at[phase, r], self.sems[1].at[phase, r],
                                            device_id=_peer(r), device_id_type=MESH)

    def _start(self, phase):
        for r in range(1, N_DEV):
            self.sends.append(self._copy(phase, r))
            self.sends[-1].start()

    def begin(self):
        c_ref, _, _, _, _, call_ref, _ = self.refs
        call_ref[self.me] = jnp.concatenate([c_ref[...], jnp.zeros((7, D), F32)], axis=0)
        self._start(0)

    def middle(self):
        _, w_ref, b_ref, mod_ref, ca_ref, call_ref, part_ref = self.refs
        for r in range(1, N_DEV):
            self._copy(0, r).wait_recv()
        cv = call_ref[...].reshape(8 * N_DEV, D)
        ca = (cv * _sigmoid(cv)).astype(BF16)
        ca_ref[...] = ca
        part_ref[...] = _dot(ca, w_ref[...].astype(BF16)).reshape(part_ref.shape)
        mod_ref[self.me] = part_ref[self.me] + b_ref[self.me]
        self._start(1)

    def end(self):
        _, _, b_ref, mod_ref, _, _, _ = self.refs
        for r in range(1, N_DEV):
            self._copy(1, r).wait_recv()
            mod_ref[self.me ^ r] = mod_ref[self.me ^ r] + b_ref[self.me ^ r]
        for cp in self.sends:
            cp.wait_send()


def _mod_bwd(allvec, ca, me_idx):
    W = N_MODVEC // N_DEV

    def body(me_ref, all_ref, cols_ref, ca_ref, gw_ref, sum_ref):
        in_first_row = lax.broadcasted_iota(jnp.int32, (N_DEV, 8, W), 1) == 0
        dm = jnp.where(in_first_row, cols_ref[...], 0.0).reshape(8 * N_DEV, W)
        gw_ref[...] = _dot_tn(ca_ref[...], dm.astype(BF16))
        total = all_ref[0]
        for k in range(1, N_DEV):
            total = total + all_ref[k]
        sum_ref[...] = total

    return pl.pallas_call(
        body, name="mod_bwd",
        grid_spec=pltpu.PrefetchScalarGridSpec(
            num_scalar_prefetch=1, grid=(1,),
            in_specs=[pl.BlockSpec((N_DEV, 1, N_VEC), lambda i, me: (0, 0, 0)),
                      pl.BlockSpec((N_DEV, 1, W), lambda i, me: (0, 0, me[0])),
                      pl.BlockSpec((8 * N_DEV, D), lambda i, me: (0, 0))],
            out_specs=[pl.BlockSpec((D, W), lambda i, me: (0, 0)), pl.BlockSpec((1, N_VEC), lambda i, me: (0, 0))]),
        out_shape=[jax.ShapeDtypeStruct((D, W), F32), jax.ShapeDtypeStruct((1, N_VEC), F32)],
        compiler_params=_params(("arbitrary",)),
    )(me_idx, allvec, allvec, ca)


def _wgather(shards, later, c_row, w_mod, b_shards):
    n, nl = len(shards), len(later)
    rows = [s.shape[0] for s in shards]
    W = w_mod.shape[1]
    cast = [(s.shape, True) for s in shards] + [(s.shape, s.dtype != dt) for s, dt in later]
    staging = [pltpu.VMEM(shape, dt) for shape, is_cast in cast if is_cast for dt in (F32, BF16)]

    def body(*refs):
        ins, (c_ref, w_ref, b_ref), raw = refs[:n], refs[n:n + 3], refs[n + 3:n + 3 + nl]
        o = n + 3 + 2 * nl
        outs, (mod_ref, ca_ref), lands = refs[o:o + n], refs[o + n:o + n + 2], refs[o + n + 2:o + n + 2 + nl]
        o += n + 2 + nl
        send_sems, recv_sems, local_sems, load_sems, store_sems = refs[o:o + 5]
        mod = _ModExchange(c_ref, w_ref, b_ref, mod_ref, ca_ref, *refs[o + 5:o + 9])
        stage = iter(refs[o + 9:])
        staged = [(next(stage), next(stage)) if is_cast else None for _, is_cast in cast]
        mod.begin()
        x, y, c = _coords()
        me = 4 * x + 2 * y + c
        sib, xn, yn = (x, y, 1 - c), (1 - x, y, c), (x, 1 - y, c)
        block = lambda px, py, pc: 4 * px + 2 * py + pc

        def part(k, blk, half):
            if half is None:
                return outs[k].at[blk]
            return outs[k].at[blk, pl.ds(half * (rows[k] // 2), rows[k] // 2)]

        def copy(k, slot, blk, to, half=None, src=None):
            ref = part(k, blk, half)
            return pltpu.make_async_remote_copy(
                src_ref=ref if src is None else src, dst_ref=ref, send_sem=send_sems.at[k, slot],
                recv_sem=recv_sems.at[k, slot], device_id=to, device_id_type=MESH)

        def in_bf16(sources, first):
            loads = [pltpu.make_async_copy(src, staged[first + i][0], load_sems.at[first + i])
                     if staged[first + i] else None for i, src in enumerate(sources)]
            for cp in loads:
                if cp is not None:
                    cp.start()
            for i, src in enumerate(sources):
                if loads[i] is None:
                    yield i, src
                    continue
                loads[i].wait()
                wide, narrow = staged[first + i]
                narrow[...] = wide[...].astype(BF16)
                yield i, narrow

        mine = [ref for _, ref in in_bf16(ins, 0)]
        local = [pltpu.make_async_copy(mine[k], outs[k].at[me], local_sems.at[k]) for k in range(n)]
        for cp in local:
            cp.start()
        sent = [copy(k, slot, me, to, src=mine[k]) for k in range(n) for slot, to in ((0, sib), (1, xn), (2, yn))]
        for cp in sent:
            cp.start()
        mod.middle()
        for k, ref in in_bf16(raw, n):
            local.append(pltpu.make_async_copy(ref, lands[k].at[me], store_sems.at[k]))
            local[-1].start()
        bx, by, bd = block(1 - x, y, c), block(x, 1 - y, c), block(1 - x, 1 - y, c)
        for k in range(n):
            copy(k, 1, bx, sib).wait_recv()
            sent += [copy(k, 4, bx, yn, half=1), copy(k, 5, bx, sib)]
            sent[-2].start()
            sent[-1].start()
        for k in range(n):
            copy(k, 2, by, sib).wait_recv()
            sent += [copy(k, 3, by, xn, half=0), copy(k, 6, by, sib)]
            sent[-2].start()
            sent[-1].start()
        for k in range(n):
            copy(k, 3, bd, sib, half=0).wait_recv()
            copy(k, 4, bd, sib, half=1).wait_recv()
            sent.append(copy(k, 7, bd, sib))
            sent[-1].start()
        for k in range(n):
            copy(k, 0, block(x, y, 1 - c), sib).wait_recv()
            for slot, blk in ((5, block(1 - x, y, 1 - c)), (6, block(x, 1 - y, 1 - c)), (7, block(1 - x, 1 - y, 1 - c))):
                copy(k, slot, blk, sib).wait_recv()
        for cp in sent:
            cp.wait_send()
        for cp in local:
            cp.wait()
        mod.end()

    anyspec, vm = pl.BlockSpec(memory_space=pl.ANY), pl.BlockSpec(memory_space=pltpu.VMEM)
    zones = [lax.empty((N_DEV,) + s.shape, dt) for s, dt in later]
    out = pl.pallas_call(
        body, name="wgather", in_specs=[anyspec] * n + [vm] * 3 + [anyspec] * (2 * nl),
        out_specs=[anyspec] * n + [vm] * 2 + [anyspec] * nl,
        out_shape=[jax.ShapeDtypeStruct((N_DEV,) + s.shape, BF16) for s in shards]
        + [jax.ShapeDtypeStruct((N_DEV, 8, W), F32), jax.ShapeDtypeStruct((8 * N_DEV, D), BF16)]
        + [jax.ShapeDtypeStruct(z.shape, z.dtype) for z in zones],
        input_output_aliases={n + 3 + nl + i: n + 2 + i for i in range(nl)},
        scratch_shapes=[pltpu.SemaphoreType.DMA((n, 8)), pltpu.SemaphoreType.DMA((n, 8)),
                        pltpu.SemaphoreType.DMA((n,)), pltpu.SemaphoreType.DMA((n + nl,)),
                        pltpu.SemaphoreType.DMA((nl,)),
                        pltpu.VMEM((N_DEV, 8, D), F32), pltpu.VMEM((N_DEV, 8, W), F32),
                        pltpu.SemaphoreType.DMA((2, N_DEV)), pltpu.SemaphoreType.DMA((2, N_DEV))] + staging,
        compiler_params=_params(),
    )(*shards, c_row, w_mod, b_shards, *[s for s, _ in later], *zones)
    return out[:n], out[n], out[n + 1], list(out[n + 2:])


class _GatherCopies:
    def __init__(self, lands, send_sems, recv_sems, k0=0, batches=None):
        x, y, c = _coords()
        me = 4 * x + 2 * y + c
        sib = (x, y, 1 - c)
        chips = [(1 - x, y), (x, 1 - y), (1 - x, 1 - y)]

        def copy(k, slot, block, to):
            return pltpu.make_async_remote_copy(
                src_ref=lands[k].at[block], dst_ref=lands[k].at[block],
                send_sem=send_sems.at[7 * (k0 + k) + slot], recv_sem=recv_sems.at[7 * (k0 + k) + slot],
                device_id=to, device_id_type=MESH)

        n = len(lands)
        self.first = [copy(k, 0, me, sib) for k in range(n)]
        for batch in batches or [range(n)]:
            self.first += [copy(k, 1 + j, me, (cx, cy, c)) for j, (cx, cy) in enumerate(chips) for k in batch]
        self.landed = [copy(k, 1 + j, 4 * cx + 2 * cy + c, sib) for j, (cx, cy) in enumerate(chips) for k in range(n)]
        self.passed = [copy(k, 4 + j, 4 * cx + 2 * cy + c, sib) for j, (cx, cy) in enumerate(chips) for k in range(n)]
        self.from_sib = [copy(k, 0, 4 * x + 2 * y + (1 - c), sib) for k in range(n)]
        self.from_sib += [copy(k, 4 + j, 4 * cx + 2 * cy + (1 - c), sib) for j, (cx, cy) in enumerate(chips)
                          for k in range(n)]


def _token(carry):
    return [] if carry is None else [carry], jax.ShapeDtypeStruct((8, 128), F32) if carry is None else carry


def _gather_start(lands, *, name, batches=None, carry=None):
    n = len(lands)
    carried, token = _token(carry)

    def body(*refs):
        n_in = n + len(carried)
        for cp in _GatherCopies(refs[:n], refs[n_in], refs[n_in + 1], batches=batches).first:
            cp.start()
        refs[-1][...] = refs[n][...] if carried else jnp.zeros_like(refs[-1])

    vm = pl.BlockSpec(memory_space=pltpu.VMEM)
    out = pl.pallas_call(
        body, name=name,
        out_shape=(pltpu.SemaphoreType.DMA((7 * n,)), pltpu.SemaphoreType.DMA((7 * n,)),
                   *[pltpu.HBM(l.shape, l.dtype) for l in lands], jax.ShapeDtypeStruct(token.shape, token.dtype)),
        in_specs=[HBM_SPEC] * n + [vm] * len(carried),
        out_specs=(SEM_SPEC, SEM_SPEC, *[HBM_SPEC] * n, vm),
        input_output_aliases={i: 2 + i for i in range(n)},
        compiler_params=pltpu.CompilerParams(has_side_effects=DATAFLOW),
    )(*[_in_hbm(l) for l in lands], *carried)
    return out[0], out[1], list(out[2:2 + n]), out[-1]


def _gather_pass(send_sems, recv_sems, lands, after, *, name, stage, k0=0, carry=None):
    n = len(lands)
    carried, token = _token(carry)

    def body(*refs):
        cps = _GatherCopies(refs[:n], refs[n], refs[n + 1], k0)
        if stage == "landed":
            for cp in cps.landed:
                cp.wait_recv()
        else:
            for cp in cps.passed:
                cp.start()
        refs[-1][...] = refs[n + 2 + len(after)][...] if carried else jnp.zeros_like(refs[-1])

    vm = pl.BlockSpec(memory_space=pltpu.VMEM)
    out = pl.pallas_call(
        body, name=name,
        out_shape=(*[pltpu.HBM(l.shape, l.dtype) for l in lands], jax.ShapeDtypeStruct(token.shape, token.dtype)),
        in_specs=[HBM_SPEC] * n + [SEM_SPEC, SEM_SPEC] + [pl.BlockSpec(memory_space=pl.ANY)] * len(after)
        + [vm] * len(carried),
        out_specs=(*[HBM_SPEC] * n, vm),
        input_output_aliases={i: i for i in range(n)},
        compiler_params=pltpu.CompilerParams(has_side_effects=DATAFLOW),
    )(*lands, send_sems, recv_sems, *after, *carried)
    return list(out[:n]), out[-1]


def _gather_end(send_sems, recv_sems, lands, after, *, name, k0=0):
    n = len(lands)

    def body(*refs):
        cps = _GatherCopies(refs[:n], refs[n], refs[n + 1], k0)
        for cp in cps.from_sib:
            cp.wait_recv()
        for cp in cps.first + cps.passed:
            cp.wait_send()

    out = pl.pallas_call(
        body, name=name,
        out_shape=[pltpu.HBM(l.shape, l.dtype) for l in lands],
        in_specs=[HBM_SPEC] * n + [SEM_SPEC, SEM_SPEC] + [pl.BlockSpec(memory_space=pl.ANY)] * len(after),
        out_specs=[HBM_SPEC] * n,
        input_output_aliases={i: i for i in range(n)},
        compiler_params=pltpu.CompilerParams(has_side_effects=DATAFLOW),
    )(*lands, send_sems, recv_sems, *after)
    return list(out)


def _d2d_copies(grads, lands, send_sems, recv_sems):
    x, y, c = _coords()
    return [pltpu.make_async_remote_copy(
        src_ref=grads[k].at[2 * q + (1 - c)], dst_ref=lands[k].at[q],
        send_sem=send_sems.at[4 * k + q], recv_sem=recv_sems.at[4 * k + q],
        device_id=(x, y, 1 - c), device_id_type=MESH) for k in range(len(grads)) for q in range(4)]


def _direct_copies(grads, lands, send_sems, recv_sems):
    x, y, c = _coords()
    me = 4 * x + 2 * y + c
    return [pltpu.make_async_remote_copy(
        src_ref=grads[k].at[me ^ r], dst_ref=lands[k].at[r - 1],
        send_sem=send_sems.at[7 * k + r - 1], recv_sem=recv_sems.at[7 * k + r - 1],
        device_id=_peer(r), device_id_type=MESH) for k in range(len(grads)) for r in range(1, N_DEV)]


def _vec_copies(srcs, lands, send_sems, recv_sems):
    x, y, c = _coords()
    me = 4 * x + 2 * y + c
    return [pltpu.make_async_remote_copy(
        src_ref=lands[0].at[me], dst_ref=lands[0].at[me], send_sem=send_sems.at[r - 1], recv_sem=recv_sems.at[r - 1],
        device_id=_peer(r), device_id_type=MESH) for r in range(1, N_DEV)]


def _chipsum(gs, sibs, cidx, *, name):
    n = len(gs)

    def body(c_ref, *refs):
        for k in range(n):
            refs[2 * n + k][...] = (refs[k][...].astype(F32) + refs[n + k][...].astype(F32)).astype(refs[2 * n + k].dtype)

    mine = [pl.BlockSpec((1,) + g.shape[1:], lambda q, c_ref: (2 * q + c_ref[0], 0, 0)) for g in gs]
    other = [pl.BlockSpec((1,) + g.shape[1:], lambda q, c_ref: (q, 0, 0)) for g in gs]
    return pl.pallas_call(
        body, name=name,
        grid_spec=pltpu.PrefetchScalarGridSpec(num_scalar_prefetch=1, grid=(4,), in_specs=mine + other, out_specs=other),
        out_shape=[jax.ShapeDtypeStruct((4,) + g.shape[1:], g.dtype) for g in gs],
        compiler_params=_params(("arbitrary",)),
    )(cidx, *gs, *sibs)


HBM_SPEC = pl.BlockSpec(memory_space=pltpu.HBM)
SEM_SPEC = pl.BlockSpec(memory_space=pltpu.SEMAPHORE)
DATAFLOW = pltpu.SideEffectType.DATAFLOW_SIDE_EFFECTING


def _in_hbm(a):
    return pltpu.with_memory_space_constraint(a, pltpu.HBM)


def _rs_step1_copies(sums, lands, send_sems, recv_sems):
    n = len(sums)
    direct, relay = lands[:n], lands[n:]
    x, y, c = _coords()
    xn, yn = (1 - x, y, c), (x, 1 - y, c)
    qx, qy, qd = 2 * (1 - x) + y, 2 * x + (1 - y), 2 * (1 - x) + (1 - y)
    cps = []
    for k in range(n):
        h = sums[k].shape[1] // 2
        a, b = pl.ds(0, h), pl.ds(h, h)
        moves = ((sums[k].at[qx, a], direct[k].at[0], xn), (sums[k].at[qy, b], direct[k].at[1], yn),
                 (sums[k].at[qd, a], relay[k].at[0], xn), (sums[k].at[qd, b], relay[k].at[1], yn))
        for s, (src, dst, to) in enumerate(moves):
            cps.append(pltpu.make_async_remote_copy(
                src_ref=src, dst_ref=dst, send_sem=send_sems.at[4 * k + s], recv_sem=recv_sems.at[4 * k + s],
                device_id=to, device_id_type=MESH))
    return cps


def _rs_step2_copies(relayed, lands, send_sems, recv_sems, k0=0):
    x, y, c = _coords()
    cps = []
    for k in range(len(relayed)):
        for s, to in enumerate(((1 - x, y, c), (x, 1 - y, c))):
            cps.append(pltpu.make_async_remote_copy(
                src_ref=relayed[k].at[s], dst_ref=lands[k].at[s], send_sem=send_sems.at[2 * (k0 + k) + s],
                recv_sem=recv_sems.at[2 * (k0 + k) + s], device_id=to, device_id_type=MESH))
    return cps


def _relay_sum(sums, relay, qxy, *, name):
    n = len(sums)

    def body(q_ref, *refs):
        for k in range(n):
            refs[2 * n + k][...] = (refs[k][...].astype(F32) + refs[n + k][...].astype(F32)).astype(refs[2 * n + k].dtype)

    half = lambda s: (1, s.shape[1] // 2) + s.shape[2:]
    return pl.pallas_call(
        body, name=name,
        grid_spec=pltpu.PrefetchScalarGridSpec(
            num_scalar_prefetch=1, grid=(2,),
            in_specs=[pl.BlockSpec(half(s), lambda t, q_ref: (q_ref[t], 1 - t, 0)) for s in sums]
            + [pl.BlockSpec(half(s), lambda t, q_ref: (1 - t, 0, 0)) for s in sums],
            out_specs=[pl.BlockSpec(half(s), lambda t, q_ref: (t, 0, 0)) for s in sums]),
        out_shape=[jax.ShapeDtypeStruct((2,) + half(s)[1:], s.dtype) for s in sums],
        compiler_params=_params(("arbitrary",)),
    )(qxy, *sums, *relay)


def _split_start(copies, srcs, lands, n_sems, after, *, name, carry=None):
    ns, nl = len(srcs), len(lands)
    carried, token = _token(carry)

    def body(*refs):
        n_in = ns + nl + len(after) + len(carried)
        for cp in copies(refs[:ns], refs[ns:ns + nl], refs[n_in], refs[n_in + 1]):
            cp.start()
        refs[-1][...] = refs[n_in - 1][...] if carried else jnp.zeros_like(refs[-1])

    bufs = [_in_hbm(a) for a in list(srcs) + list(lands)]
    vm = pl.BlockSpec(memory_space=pltpu.VMEM)
    out = pl.pallas_call(
        body, name=name,
        out_shape=(pltpu.SemaphoreType.DMA((n_sems,)), pltpu.SemaphoreType.DMA((n_sems,)),
                   *[pltpu.HBM(a.shape, a.dtype) for a in bufs], jax.ShapeDtypeStruct(token.shape, token.dtype)),
        in_specs=[HBM_SPEC] * len(bufs) + [pl.BlockSpec(memory_space=pl.ANY)] * len(after) + [vm] * len(carried),
        out_specs=(SEM_SPEC, SEM_SPEC, *[HBM_SPEC] * len(bufs), vm),
        input_output_aliases={i: 2 + i for i in range(len(bufs))},
        compiler_params=pltpu.CompilerParams(has_side_effects=DATAFLOW),
    )(*bufs, *after, *carried)
    return out[0], out[1], list(out[2:2 + ns]), list(out[2 + ns:2 + ns + nl]), out[-1]


def _split_wait(copies, send_sems, recv_sems, srcs, lands, after, *, name):
    ns, nl = len(srcs), len(lands)

    def body(*refs):
        for cp in copies(refs[:ns], refs[ns:ns + nl], refs[ns + nl], refs[ns + nl + 1]):
            cp.wait_send()
            cp.wait_recv()

    out = pl.pallas_call(
        body, name=name,
        out_shape=[pltpu.HBM(a.shape, a.dtype) for a in list(srcs) + list(lands)],
        in_specs=[HBM_SPEC] * (ns + nl) + [SEM_SPEC, SEM_SPEC] + [pl.BlockSpec(memory_space=pl.ANY)] * len(after),
        out_specs=[HBM_SPEC] * (ns + nl),
        input_output_aliases={i: i for i in range(ns + nl)},
        compiler_params=pltpu.CompilerParams(has_side_effects=DATAFLOW),
    )(*srcs, *lands, send_sems, recv_sems, *after)
    return list(out[:ns]), list(out[ns:])


ADAM_C1 = 1.0 / (1.0 - ADAM_B1 ** ADAM_STEP)
ADAM_C2 = 1.0 / (1.0 - ADAM_B2 ** ADAM_STEP)


def _adam_math(w, g, m, v):
    m2 = ADAM_B1 * m + (1.0 - ADAM_B1) * g
    v2 = ADAM_B2 * v + (1.0 - ADAM_B2) * (g * g)
    return -ADAM_LR * ((m2 * ADAM_C1) / (jnp.sqrt(v2 * ADAM_C2) + ADAM_EPS) + ADAM_WD * w), m2, v2


def _adamw(w, g, m, v, *, name, after=()):
    R, C = w.shape
    tr = R if R <= 512 else 256

    def body(w_ref, g_ref, m_ref, v_ref, *rest):
        d_ref, nm_ref, nv_ref = rest[len(after):]
        d_ref[...], nm_ref[...], nv_ref[...] = _adam_math(w_ref[...], g_ref[...], m_ref[...], v_ref[...])

    blk = pl.BlockSpec((tr, C), lambda i: (i, 0))
    return pl.pallas_call(
        body, name=name, grid=(R // tr,), in_specs=[blk] * 4 + [pl.BlockSpec(memory_space=pl.ANY)] * len(after),
        out_specs=[blk] * 3, out_shape=[jax.ShapeDtypeStruct((R, C), F32)] * 3,
        compiler_params=_params(("parallel",)),
    )(w, g, m, v, *after)


def _adamw_rs2(wmv, cs, direct, second, qidx, *, name):
    n = len(wmv)
    r, cc = wmv[0][0].shape
    h = r // 2

    def body(q_ref, *refs):
        ins, outs = refs[:6 * n], refs[6 * n:]
        for k in range(n):
            w_ref, m_ref, v_ref, c_ref, d1_ref, d2_ref = ins[6 * k:6 * k + 6]
            g_ref, d_ref, nm_ref, nv_ref = outs[4 * k:4 * k + 4]
            g = (c_ref[0].astype(F32) + d1_ref[0].astype(F32)) + d2_ref[0].astype(F32)
            g_ref[...] = g
            d_ref[...], nm_ref[...], nv_ref[...] = _adam_math(w_ref[...], g, m_ref[...], v_ref[...])

    blk = pl.BlockSpec((h, cc), lambda i, q_ref: (i, 0))
    one = [blk, blk, blk, pl.BlockSpec((1, h, cc), lambda i, q_ref: (q_ref[0], i, 0)),
           pl.BlockSpec((1, h, cc), lambda i, q_ref: (i, 0, 0)),
           pl.BlockSpec((1, h, cc), lambda i, q_ref: (1 - i, 0, 0))]
    out = pl.pallas_call(
        body, name=name,
        grid_spec=pltpu.PrefetchScalarGridSpec(num_scalar_prefetch=1, grid=(2,), in_specs=one * n,
                                               out_specs=[blk] * (4 * n)),
        out_shape=[jax.ShapeDtypeStruct((r, cc), F32)] * (4 * n),
        compiler_params=_params(("arbitrary",)),
    )(qidx, *[a for (w, m, v), c, d1, d2 in zip(wmv, cs, direct, second) for a in (w, m, v, c, d1, d2)])
    return [tuple(out[4 * k:4 * k + 4]) for k in range(n)]


def _adamw_rs(wmv, cs, rcv, qidx, *, name):
    n = len(wmv)
    shapes = [w.shape for w, _, _ in wmv]
    n_rcv = rcv[0].shape[0]
    halved = len(set(shapes)) == 1 and shapes[0][0] % 32 == 0 and shapes[0][0] > 128
    tiles = 2 if halved else 1

    def body(q_ref, *refs):
        ins, outs = refs[:5 * n], refs[5 * n:]
        for k in range(n):
            w_ref, m_ref, v_ref, c_ref, r_ref = ins[5 * k:5 * k + 5]
            g_ref, d_ref, nm_ref, nv_ref = outs[4 * k:4 * k + 4]
            g = c_ref[0].astype(F32)
            for j in range(n_rcv):
                g = g + r_ref[j].astype(F32)
            g_ref[...] = g
            d_ref[...], nm_ref[...], nv_ref[...] = _adam_math(w_ref[...], g, m_ref[...], v_ref[...])

    in_specs, out_specs = [], []
    for r, cc in shapes:
        blk = pl.BlockSpec((r // tiles, cc), lambda i, q_ref: (i, 0))
        in_specs += [blk, blk, blk, pl.BlockSpec((1, r // tiles, cc), lambda i, q_ref: (q_ref[0], i, 0)),
                     pl.BlockSpec((n_rcv, r // tiles, cc), lambda i, q_ref: (0, i, 0))]
        out_specs += [blk] * 4
    out = pl.pallas_call(
        body, name=name,
        grid_spec=pltpu.PrefetchScalarGridSpec(num_scalar_prefetch=1, grid=(tiles,), in_specs=in_specs,
                                               out_specs=out_specs),
        out_shape=[jax.ShapeDtypeStruct(s, F32) for s in shapes for _ in range(4)],
        compiler_params=_params(("arbitrary",)),
    )(qidx, *[a for (w, m, v), c, rc in zip(wmv, cs, rcv) for a in (w, m, v, c, rc)])
    return [tuple(out[4 * k:4 * k + 4]) for k in range(n)]


SMALL_PARAMS = ("norm_ffn1", "norm_mix", "norm_ffn2", "norm_final", "q_norm", "kv_norm", "sinks", "rel_bias", "b_mod")


def _adamw_small(gvec, wmv):
    shapes = [wmv[3 * i].shape for i in range(len(SMALL_PARAMS))]

    def body(*refs):
        g_all = refs[0]
        ins = refs[1:1 + 3 * len(SMALL_PARAMS)]
        outs = refs[1 + 3 * len(SMALL_PARAMS):]
        off = N_MODVEC
        for i, name in enumerate(SMALL_PARAMS):
            g_ref, d_ref, nm_ref, nv_ref = outs[4 * i:4 * i + 4]
            w_ref, m_ref, v_ref = ins[3 * i:3 * i + 3]
            start = 0 if name == "b_mod" else off
            rows, width = shapes[i]
            if name == "rel_bias":
                sub = lax.broadcasted_iota(jnp.int32, (rows, rows), 0)
                lane = lax.broadcasted_iota(jnp.int32, (rows, rows), 1)
                col = lax.broadcasted_iota(jnp.int32, (rows, width), 1)
                g = jnp.zeros((rows, width), F32)
                for h in range(width):
                    along = jnp.broadcast_to(g_all[:, start + rows * h:start + rows * (h + 1)], (rows, rows))
                    g = jnp.where(col == h, jnp.sum(jnp.where(sub == lane, along, 0.0), axis=1, keepdims=True), g)
            else:
                g = jnp.concatenate([g_all[:, start + width * r:start + width * (r + 1)] for r in range(rows)], axis=0)
            g_ref[...] = g
            d_ref[...], nm_ref[...], nv_ref[...] = _adam_math(w_ref[...], g, m_ref[...], v_ref[...])
            if name != "b_mod":
                off += dict(SMALL_LAYOUT)[name]

    vm = pl.BlockSpec(memory_space=pltpu.VMEM)
    n_out = 4 * len(SMALL_PARAMS)
    out = pl.pallas_call(
        body, name="adamw_small", in_specs=[vm] * (1 + len(wmv)), out_specs=[vm] * n_out,
        out_shape=[jax.ShapeDtypeStruct(shapes[i // 4], F32) for i in range(n_out)],
        compiler_params=_params(),
    )(gvec, *wmv)
    return {name: out[4 * i:4 * i + 4] for i, name in enumerate(SMALL_PARAMS)}


TRANSPOSED = ("g1T", "u1T", "g3T", "u3T", "w_inT", "w_uqT")


def kernel(x, c, w_mod, b_mod, norm_ffn1, ffn1_gate, ffn1_up, ffn1_down, norm_mix, w_in, q_norm, kv_norm, w_uq, w_ukv, sinks, w_o, norm_ffn2, ffn2_gate, ffn2_up, ffn2_down, rel_bias, norm_final, loss_target, m_w_mod, m_b_mod, m_norm_ffn1, m_ffn1_gate, m_ffn1_up, m_ffn1_down, m_norm_mix, m_w_in, m_q_norm, m_kv_norm, m_w_uq, m_w_ukv, m_sinks, m_w_o, m_norm_ffn2, m_ffn2_gate, m_ffn2_up, m_ffn2_down, m_rel_bias, m_norm_final, v_w_mod, v_b_mod, v_norm_ffn1, v_ffn1_gate, v_ffn1_up, v_ffn1_down, v_norm_mix, v_w_in, v_q_norm, v_kv_norm, v_w_uq, v_w_ukv, v_sinks, v_w_o, v_norm_ffn2, v_ffn2_gate, v_ffn2_up, v_ffn2_down, v_rel_bias, v_norm_final):
    mx, my, mc = _coords()
    cidx = jnp.reshape(mc, (1,)).astype(jnp.int32)
    qidx = jnp.reshape(2 * mx + my, (1,)).astype(jnp.int32)
    WM = w_mod.shape[2]

    shards = {"g1T": ffn1_gate[0].T, "u1T": ffn1_up[0].T, "d1": ffn1_down[0],
              "g3T": ffn2_gate[0].T, "u3T": ffn2_up[0].T, "d3": ffn2_down[0],
              "w_inT": w_in[0].T, "w_uqT": w_uq[0].T, "w_ukv": w_ukv[0], "w_o": w_o[0]}
    travels = lambda k: F32 if k == "w_inT" else BF16
    me = 4 * mx + 2 * my + mc
    groups = {"ffn1": ("g1T", "u1T", "d1"), "mixer": ("w_inT", "w_uqT", "w_ukv", "w_o"), "ffn2": ("g3T", "u3T", "d3")}
    arriving = {}
    later = groups["mixer"] + groups["ffn2"]
    place = {"mixer": 0, "ffn2": len(groups["mixer"])}

    gathered_ffn1, mod3, ca, zones = _wgather([shards[k] for k in groups["ffn1"]],
                                              [(shards[k], travels(k)) for k in later], c, w_mod[0],
                                              b_mod.reshape(N_DEV, 1, WM))
    mod9 = mod3[:, 0, :].reshape(N_MOD, D)

    def as_weights(group, gathered):
        return {k: g if k == "w_ukv" else g.reshape(N_DEV * g.shape[1], g.shape[2])
                for k, g in zip(groups[group], gathered)}

    def start_gather(carry):
        batches = [range(k0, k0 + len(groups[group])) for group, k0 in place.items()]
        send, recv, lands, started = _gather_start(zones, name="gather_start", batches=batches, carry=carry)
        for group, k0 in place.items():
            arriving[group] = (send, recv, lands[k0:k0 + len(groups[group])])
        return started

    def fetch(group, after, vecs):
        if group == "ffn1":
            return as_weights("ffn1", gathered_ffn1), start_gather(vecs)

        def pass_on(group, after, carry=None):
            send, recv, lands = arriving[group]
            lands, token = _gather_pass(send, recv, lands, after, name="gather_landed_" + group, stage="landed",
                                        k0=place[group])
            lands, token = _gather_pass(send, recv, lands, [token], name="gather_onward_" + group, stage="onward",
                                        k0=place[group], carry=carry)
            arriving[group] = (send, recv, lands)
            return token

        if group == "ffn2_on_its_way":
            return None, pass_on("ffn2", after, vecs)
        if group == "mixer":
            after = [pass_on("mixer", after)]
        send, recv, lands = arriving[group]
        return as_weights(group, _gather_end(send, recv, lands, after, name="gather_end_" + group,
                                             k0=place[group])), vecs

    norms ={"ffn1": norm_ffn1, "mix": norm_mix, "ffn2": norm_ffn2, "final": norm_final.reshape(1, D)}
    in_flight = {}

    def on_grads(group, g, after, vecs):
        if group != "ffn1":
            if g is None:
                return vecs
            names = list(g)
            by_dest = [g[k] if k == "w_ukv" else g[k].reshape((N_DEV, g[k].shape[0] // N_DEV) + g[k].shape[1:])
                       for k in names]
            lands = [lax.empty((N_DEV - 1,) + a.shape[1:], a.dtype) for a in by_dest]
            send, recv, by_dest, lands, token = _split_start(_direct_copies, by_dest, lands, 7 * len(names), after,
                                                             name="rs_start_" + group, carry=vecs)
            in_flight[group] = (names, send, recv, by_dest, lands, token)
            return token
        names = list(g)
        by_dest = [g[k].reshape((N_DEV, g[k].shape[0] // N_DEV) + g[k].shape[1:]) for k in names]
        lands = [lax.empty((4,) + a.shape[1:], a.dtype) for a in by_dest]
        send, recv, by_dest, lands, token = _split_start(_d2d_copies, by_dest, lands, 4 * len(names), after,
                                                         name="rs_d2d_start_" + group)
        finish("mixer", [token])
        by_dest, from_sib = _split_wait(_d2d_copies, send, recv, by_dest, lands, [done[-1]], name="rs_d2d_wait_" + group)
        sums = _chipsum(by_dest, from_sib, cidx, name="chipsum_" + group)
        halves = lambda: [lax.empty((2, s.shape[1] // 2) + s.shape[2:], s.dtype) for s in sums]
        send, recv, sums, lands, token = _split_start(_rs_step1_copies, sums, halves() + halves(), 4 * len(names), [],
                                                      name="rs_ici_start_" + group, carry=vecs)
        in_flight[group] = (names, send, recv, sums, lands, token)
        return token

    owners = {"g1T": ("ffn1_gate", ffn1_gate, m_ffn1_gate, v_ffn1_gate), "u1T": ("ffn1_up", ffn1_up, m_ffn1_up, v_ffn1_up),
              "d1": ("ffn1_down", ffn1_down, m_ffn1_down, v_ffn1_down),
              "g3T": ("ffn2_gate", ffn2_gate, m_ffn2_gate, v_ffn2_gate), "u3T": ("ffn2_up", ffn2_up, m_ffn2_up, v_ffn2_up),
              "d3": ("ffn2_down", ffn2_down, m_ffn2_down, v_ffn2_down),
              "w_inT": ("w_in", w_in, m_w_in, v_w_in), "w_uqT": ("w_uq", w_uq, m_w_uq, v_w_uq),
              "w_ukv": ("w_ukv", w_ukv, m_w_ukv, v_w_ukv), "w_o": ("w_o", w_o, m_w_o, v_w_o)}
    res, done = {}, []

    there = lambda k, a: a[0].T if k in TRANSPOSED else a[0]
    back = lambda k, a: a.T[None] if k in TRANSPOSED else a[None]

    def record(names, outs):
        for k, out in zip(names, outs):
            done.append(out[3])
            res[owners[k][0]] = tuple(back(k, a) for a in out)

    def finish(group, after):
        names, send, recv, sums, lands, _ = in_flight[group]
        wmv = [tuple(there(k, a) for a in owners[k][1:]) for k in names]
        own = jnp.reshape(me, (1,)).astype(jnp.int32)
        sums, lands = _split_wait(_direct_copies, send, recv, sums, lands, after, name="rs_wait_" + group)
        record(names, _adamw_rs(wmv, sums, lands, own, name="adamw_" + group))

    _, grad_x, _, vec = _local_step(
        x[0], loss_target[0], mod9, norms, sinks, rel_bias, q_norm, kv_norm, fetch, on_grads=on_grads)

    names, send, recv, sums, lands, step1_started = in_flight["ffn1"]
    vec = vec.reshape(1, 1, N_VEC)
    allvec = lax.dynamic_update_slice(lax.empty((N_DEV, 1, N_VEC), F32), vec, (me, 0, 0))
    vsend, vrecv, _, (allvec,), vec_started = _split_start(_vec_copies, [], [allvec], N_DEV - 1, [step1_started],
                                                           name="vec_start")
    finish("ffn2", [vec_started])
    n = len(names)
    sums, lands = _split_wait(_rs_step1_copies, send, recv, sums, lands, [done[-1]], name="rs_ici_wait_ffn1")
    direct, relay = lands[:n], lands[n:]
    qxy = jnp.stack([2 * (1 - mx) + my, 2 * mx + (1 - my)]).astype(jnp.int32)
    relayed = _relay_sum(sums, relay, qxy, name="relay_sum_ffn1")
    second = [lax.empty(a.shape, a.dtype) for a in relayed]
    send, recv, relayed, second, step2_started = _split_start(_rs_step2_copies, relayed, second, 2 * n, [],
                                                              name="rs_ici_start2_ffn1")

    _, (allvec,) = _split_wait(_vec_copies, vsend, vrecv, [], [allvec], [step2_started], name="vec_wait")
    g_wmod, gvec = _mod_bwd(allvec, ca, jnp.reshape(me, (1,)).astype(jnp.int32))
    loss = gvec[0, N_MODVEC + LOSS_SLOT]
    small_in = {"norm_ffn1": (norm_ffn1, m_norm_ffn1, v_norm_ffn1), "norm_mix": (norm_mix, m_norm_mix, v_norm_mix),
                "norm_ffn2": (norm_ffn2, m_norm_ffn2, v_norm_ffn2), "norm_final": (norm_final, m_norm_final, v_norm_final),
                "q_norm": (q_norm, m_q_norm, v_q_norm), "kv_norm": (kv_norm, m_kv_norm, v_kv_norm),
                "sinks": (sinks, m_sinks, v_sinks), "rel_bias": (rel_bias, m_rel_bias, v_rel_bias),
                "b_mod": (b_mod, m_b_mod, v_b_mod)}
    as_row = lambda k, a: a if k == "rel_bias" else a.reshape(1, -1)
    from_row = lambda k, a: a if k == "rel_bias" else a.reshape(small_in[k][0].shape)
    small_out = _adamw_small(gvec, [as_row(k, a) for k in SMALL_PARAMS for a in small_in[k]])
    for k in SMALL_PARAMS:
        res[k] = tuple(from_row(k, a) for a in small_out[k])

    out = _adamw(w_mod[0], g_wmod, m_w_mod[0], v_w_mod[0], name="adamw_w_mod")
    res["w_mod"] = tuple(a[None] for a in (g_wmod,) + tuple(out))

    wmv = [tuple(there(k, a) for a in owners[k][1:]) for k in names]
    after = done + [out[2]] + [a for k in SMALL_PARAMS for a in res[k]]
    outs = []
    for i, k in enumerate(names):
        _, (sec,) = _split_wait(functools.partial(_rs_step2_copies, k0=i), send, recv, [relayed[i]], [second[i]],
                                after, name="rs_ici_wait2_" + k)
        outs.append(_adamw_rs2([wmv[i]], [sums[i]], [direct[i]], [sec], qidx, name="adamw_" + owners[k][0])[0])
        after = [outs[-1][3]]
    record(names, outs)

    order = ("w_mod", "b_mod", "norm_ffn1", "ffn1_gate", "ffn1_up", "ffn1_down", "norm_mix", "w_in", "q_norm",
             "kv_norm", "w_uq", "w_ukv", "sinks", "w_o", "norm_ffn2", "ffn2_gate", "ffn2_up", "ffn2_down",
             "rel_bias", "norm_final")
    return (loss, grad_x[None]) + tuple(res[nm][kind] for kind in range(4) for nm in order)
```

```python
import functools
import math

import numpy as np
import jax
import jax.numpy as jnp
from jax import lax
from jax.experimental import pallas as pl
from jax.experimental.pallas import tpu as pltpu

F32 = jnp.float32
BF16 = jnp.bfloat16
MESH = pl.DeviceIdType.MESH

N_DEV = 8
D = 1024
D_FF = 2816
EPS = 1e-6
N_MOD = 9
SWA_HEADS = 8
SWA_DH = 64
WINDOW = 128
MLA_HEADS = 4
MLA_NOPE = 128
MLA_ROPE = 64
MLA_V = 128
MLA_QR = 256
MLA_KVR = 128
ROPE_THETA = 10000.0
NUM_BUCKETS = 32
D_IN = 1216
D_IN_PAD = 1280
SWA_SCALE = SWA_DH ** -0.5
MLA_SCALE = (MLA_NOPE + MLA_ROPE) ** -0.5

ADAM_LR = 0.001
ADAM_B1 = 0.9
ADAM_B2 = 0.999
ADAM_EPS = 1e-08
ADAM_WD = 0.01
ADAM_STEP = 10

V7X_VMEM_LIMIT = 56 * 1024 * 1024
ROW_TILE = 512

NT_DIMS = (((1,), (1,)), ((), ()))
TN_DIMS = (((0,), (0,)), ((), ()))


def _dot(a, b):
    return jnp.dot(a, b, preferred_element_type=F32)


def _dot_nt(a, b):
    return lax.dot_general(a, b, NT_DIMS, preferred_element_type=F32)


def _dot_tn(a, b):
    return lax.dot_general(a, b, TN_DIMS, preferred_element_type=F32)


def _params(sem=None):
    return pltpu.CompilerParams(dimension_semantics=sem, vmem_limit_bytes=V7X_VMEM_LIMIT)


def _rstd(x):
    return lax.rsqrt(jnp.mean(x * x, axis=-1, keepdims=True) + EPS)


def _rms_bwd(dy, xhat, r):
    return r * (dy - xhat * jnp.mean(dy * xhat, axis=-1, keepdims=True))


def _sigmoid(a):
    return 1.0 / (1.0 + jnp.exp(-a))


def _ffn_fwd(x, vecs, wgT, wuT, wd, *, name, tm=256, tf=D_FF):
    S, F = x.shape[0], wd.shape[0]
    tm = min(tm, S)
    ni, nj = S // tm, F // tf
    resident = nj == 1

    def body(x_ref, vec_ref, wg_ref, wu_in, wd_in, xo_ref, h_ref, a_ref, b_ref, f_ref, acc_ref, *own):
        j = pl.program_id(1)
        first = pl.program_id(0) == 0
        wu_ref, wd_ref = own[:2] if resident else (wu_in, wd_in)
        loads = [pltpu.make_async_copy(src, dst, own[2].at[k])
                 for k, (src, dst) in enumerate(((wu_in, wu_ref), (wd_in, wd_ref)))] if resident else []

        def arrived(k):
            if resident:
                pl.when(first)(loads[k].wait)

        if resident:
            @pl.when(first)
            def _():
                for cp in loads:
                    cp.start()

        @pl.when(j == 0)
        def _():
            xv = x_ref[...]
            hn = xv * _rstd(xv) * vec_ref[0:1, :]
            h_ref[...] = (hn * (1.0 + vec_ref[2:3, :]) + vec_ref[1:2, :]).astype(BF16)

        h = h_ref[...]
        a = _dot_nt(h, wg_ref[...])
        arrived(0)
        b = _dot_nt(h, wu_ref[...])
        a_ref[...] = a.astype(BF16)
        b_ref[...] = b.astype(BF16)
        arrived(1)
        part = _dot((a * _sigmoid(a) * b).astype(BF16), wd_ref[...])

        def finish(f):
            f_ref[...] = f
            xo_ref[...] = x_ref[...] + (0.5 * vec_ref[3:4, :]) * f

        if nj == 1:
            finish(part)
        else:
            @pl.when(j == 0)
            def _():
                acc_ref[...] = part

            @pl.when((j > 0) & (j < nj - 1))
            def _():
                acc_ref[...] += part

            @pl.when(j == nj - 1)
            def _():
                finish(acc_ref[...] + part)

    row = pl.BlockSpec((tm, D), lambda i, j: (i, 0))
    wspec = pl.BlockSpec((tf, D), lambda i, j: (j, 0), pipeline_mode=pl.Buffered(1) if nj == 1 else None)
    act = pl.BlockSpec((tm, tf), lambda i, j: (i, j))
    later = pl.BlockSpec(memory_space=pl.ANY) if resident else wspec
    own = [pltpu.VMEM((F, D), BF16), pltpu.VMEM((F, D), BF16), pltpu.SemaphoreType.DMA((2,))] if resident else []
    return pl.pallas_call(
        body, name=name, grid=(ni, nj),
        in_specs=[row, pl.BlockSpec((8, D), lambda i, j: (0, 0)), wspec, later, later],
        out_specs=[row, row, act, act, row],
        out_shape=[jax.ShapeDtypeStruct((S, D), F32), jax.ShapeDtypeStruct((S, D), BF16),
                   jax.ShapeDtypeStruct((S, F), BF16), jax.ShapeDtypeStruct((S, F), BF16),
                   jax.ShapeDtypeStruct((S, D), F32)],
        scratch_shapes=[pltpu.VMEM((tm, D) if nj > 1 else (8, 128), F32)] + own,
        compiler_params=_params(("arbitrary", "arbitrary")),
    )(x, vecs, wgT, wuT, wd)


def _ffn_bwd_main(h, df, a, b, wgT, wuT, wd, *, name, after=(), tm=2048, tf=256):
    S = h.shape[0]
    tm = min(tm, S)
    ni, nj = S // tm, D_FF // tf

    def body(h_hbm, df_hbm, a_ref, b_ref, wg_ref, wu_ref, wd_ref, *rest):
        gg_ref, gu_ref, gd_ref, dh_hbm, h_v, df_v, dh_v, gg_acc, gu_acc, gd_acc, sem = rest[len(after):]
        j = pl.program_id(0)
        i = pl.program_id(1)

        @pl.when((j == 0) & (i == 0))
        def _():
            c1 = pltpu.make_async_copy(h_hbm, h_v, sem.at[0])
            c2 = pltpu.make_async_copy(df_hbm, df_v, sem.at[1])
            c1.start()
            c2.start()
            c1.wait()
            c2.wait()

        @pl.when(i == 0)
        def _():
            gg_acc[...] = jnp.zeros_like(gg_acc)
            gu_acc[...] = jnp.zeros_like(gu_acc)
            gd_acc[...] = jnp.zeros_like(gd_acc)

        rows = pl.ds(pl.multiple_of(i * tm, tm), tm)
        hi = h_v[rows, :]
        dfi = df_v[rows, :]
        av = a_ref[...].astype(F32)
        bv = b_ref[...].astype(F32)
        sg = _sigmoid(av)
        sa = av * sg
        hsw = (sa * bv).astype(BF16)
        dhsw = _dot_nt(dfi, wd_ref[...])
        da = (dhsw * bv * (sg * (1.0 + av * (1.0 - sg)))).astype(BF16)
        db = (dhsw * sa).astype(BF16)
        gd_acc[...] += _dot_tn(hsw, dfi)
        gg_acc[...] += _dot_tn(da, hi)
        gu_acc[...] += _dot_tn(db, hi)
        dh = _dot(da, wg_ref[...]) + _dot(db, wu_ref[...])

        @pl.when(j == 0)
        def _():
            dh_v[rows, :] = dh

        @pl.when(j > 0)
        def _():
            dh_v[rows, :] += dh

        @pl.when(i == ni - 1)
        def _():
            gg_ref[...] = gg_acc[...].astype(BF16)
            gu_ref[...] = gu_acc[...].astype(BF16)
            gd_ref[...] = gd_acc[...].astype(BF16)

        @pl.when((j == nj - 1) & (i == ni - 1))
        def _():
            c3 = pltpu.make_async_copy(dh_v, dh_hbm, sem.at[2])
            c3.start()
            c3.wait()

    anyspec = pl.BlockSpec(memory_space=pl.ANY)
    wspec = pl.BlockSpec((tf, D), lambda j, i: (j, 0))
    act = pl.BlockSpec((tm, tf), lambda j, i: (i, j))
    return pl.pallas_call(
        body, name=name, grid=(nj, ni),
        in_specs=[anyspec, anyspec, act, act, wspec, wspec, wspec] + [anyspec] * len(after),
        out_specs=[wspec, wspec, wspec, anyspec],
        out_shape=[jax.ShapeDtypeStruct((D_FF, D), BF16)] * 3 + [jax.ShapeDtypeStruct((S, D), F32)],
        scratch_shapes=[pltpu.VMEM((S, D), BF16), pltpu.VMEM((S, D), BF16), pltpu.VMEM((S, D), F32),
                        pltpu.VMEM((tf, D), F32), pltpu.VMEM((tf, D), F32), pltpu.VMEM((tf, D), F32),
                        pltpu.SemaphoreType.DMA((3,))],
        compiler_params=_params(("arbitrary", "arbitrary")),
    )(h, df, a, b, wgT, wuT, wd, *after)


def _ffn_out_bwd(dx, f, gate, df_ref, part_ref):
    df_ref[...] = ((0.5 * gate) * dx).astype(BF16)
    part_ref[3:4, :] += 0.5 * jnp.sum(dx * f, axis=0, keepdims=True)


def _norm_bwd(dh, x, dxo, vecs, *, name, below=None, tm=ROW_TILE):
    S = x.shape[0]
    tm = min(tm, S)

    def body(dh_ref, x_ref, dxo_ref, vec_ref, *rest):
        dx_ref, part_ref = rest[-2 if below is None else -3], rest[-1 if below is None else -2]

        @pl.when(pl.program_id(0) == 0)
        def _():
            part_ref[...] = jnp.zeros_like(part_ref)

        dh = dh_ref[...]
        xv = x_ref[...]
        r = _rstd(xv)
        xhat = xv * r
        w = vec_ref[0:1, :]
        xn = xhat * w
        dxn = dh * (1.0 + vec_ref[2:3, :])
        part_ref[0:1, :] += jnp.sum(dxn * xhat, axis=0, keepdims=True)
        part_ref[1:2, :] += jnp.sum(dh, axis=0, keepdims=True)
        part_ref[2:3, :] += jnp.sum(dh * xn, axis=0, keepdims=True)
        dx = dxo_ref[...] + _rms_bwd(dxn * w, xhat, r)
        dx_ref[...] = dx
        if below is not None:
            _ffn_out_bwd(dx, rest[0][...], rest[1][3:4, :], rest[-1], part_ref)

    row = pl.BlockSpec((tm, D), lambda i: (i, 0))
    vec = pl.BlockSpec((8, D), lambda i: (0, 0))
    extra = [] if below is None else [row, vec]
    return pl.pallas_call(
        body, name=name, grid=(S // tm,), in_specs=[row, row, row, vec] + extra,
        out_specs=[row, vec] + ([] if below is None else [row]),
        out_shape=[jax.ShapeDtypeStruct((S, D), F32), jax.ShapeDtypeStruct((8, D), F32)]
        + ([] if below is None else [jax.ShapeDtypeStruct((S, D), BF16)]),
        compiler_params=_params(("arbitrary",)),
    )(dh, x, dxo, vecs, *([] if below is None else below))


def _head(x, tgt, nf, f, vecs, *, tm=ROW_TILE):
    S = x.shape[0]
    tm = min(tm, S)

    def body(x_ref, t_ref, nf_ref, f_ref, vec_ref, dx_ref, part_ref, df_ref):
        @pl.when(pl.program_id(0) == 0)
        def _():
            part_ref[...] = jnp.zeros_like(part_ref)

        xv = x_ref[...]
        r = _rstd(xv)
        xhat = xv * r
        w = nf_ref[...]
        e = xhat * w - t_ref[...]
        dy = e * (1.0 / D)
        part_ref[0:1, :] += jnp.sum(dy * xhat, axis=0, keepdims=True)
        part_ref[1:2, :] += jnp.sum(e * e) * (0.5 / D)
        dx = _rms_bwd(dy * w, xhat, r)
        dx_ref[...] = dx
        _ffn_out_bwd(dx, f_ref[...], vec_ref[3:4, :], df_ref, part_ref)

    row = pl.BlockSpec((tm, D), lambda i: (i, 0))
    vec = pl.BlockSpec((8, D), lambda i: (0, 0))
    return pl.pallas_call(
        body, name="head", grid=(S // tm,),
        in_specs=[row, row, pl.BlockSpec((1, D), lambda i: (0, 0)), row, vec],
        out_specs=[row, vec, row],
        out_shape=[jax.ShapeDtypeStruct((S, D), F32), jax.ShapeDtypeStruct((8, D), F32),
                   jax.ShapeDtypeStruct((S, D), BF16)],
        compiler_params=_params(("arbitrary",)),
    )(x, tgt, nf, f, vecs)


def _mix_in_fwd(x, vecs, w_inT, *, tm=ROW_TILE):
    S = x.shape[0]
    tm = min(tm, S)

    def body(x_ref, vec_ref, w_ref, h_ref, p_ref, wb_ref):
        @pl.when(pl.program_id(0) == 0)
        def _():
            wb_ref[0:D_IN, :] = w_ref[...].astype(BF16)
            wb_ref[D_IN:D_IN_PAD, :] = jnp.zeros((D_IN_PAD - D_IN, D), BF16)

        xv = x_ref[...]
        hn = xv * _rstd(xv) * vec_ref[0:1, :]
        h = (hn * (1.0 + vec_ref[2:3, :]) + vec_ref[1:2, :]).astype(BF16)
        h_ref[...] = h
        p_ref[...] = _dot_nt(h, wb_ref[...])

    row = pl.BlockSpec((tm, D), lambda i: (i, 0))
    return pl.pallas_call(
        body, name="mix_in_fwd", grid=(S // tm,),
        in_specs=[row, pl.BlockSpec((8, D), lambda i: (0, 0)),
                  pl.BlockSpec((D_IN, D), lambda i: (0, 0), pipeline_mode=pl.Buffered(1))],
        out_specs=[row, pl.BlockSpec((tm, D_IN_PAD), lambda i: (i, 0)), pl.BlockSpec((D_IN_PAD, D), lambda i: (0, 0))],
        out_shape=[jax.ShapeDtypeStruct((S, D), BF16), jax.ShapeDtypeStruct((S, D_IN_PAD), F32),
                   jax.ShapeDtypeStruct((D_IN_PAD, D), BF16)],
        compiler_params=_params(("arbitrary",)),
    )(x, vecs, w_inT)


def _bucket_table():
    qi = np.arange(WINDOW)[:, None]
    kj = np.arange(2 * WINDOW)[None, :]
    dist = qi + WINDOW - kj
    max_exact = NUM_BUCKETS // 2
    n = np.maximum(dist, 0)
    nf = np.maximum(n, 1).astype(np.float32)
    large = max_exact + (np.log(nf / np.float32(max_exact)) / np.float32(math.log(WINDOW / max_exact))
                         * np.float32(NUM_BUCKETS - max_exact)).astype(np.int32)
    large = np.minimum(large, NUM_BUCKETS - 1)
    return np.where(n < max_exact, n, large).astype(np.int32)


SWA_GROUP = 4
GROUP_ROWS = SWA_GROUP * WINDOW


SWA_SUB = 2


def _swa_valid(has_prev):
    row = lax.broadcasted_iota(jnp.int32, (GROUP_ROWS, 2 * WINDOW), 0) % WINDOW
    col = lax.broadcasted_iota(jnp.int32, (GROUP_ROWS, 2 * WINDOW), 1)
    dist = row + WINDOW - col
    return (dist >= 0) & (dist < WINDOW) & ((col >= WINDOW) | has_prev)


def _swa_keys(prev_ref, cur_ref, u):
    cur = cur_ref[...]
    before = prev_ref[...] if u == 0 else cur[WINDOW * (u - 1):WINDOW * u]
    return jnp.concatenate([before, cur[WINDOW * u:WINDOW * (u + 1)]], axis=0).astype(BF16)


def _stack_heads(x, g):
    return jnp.concatenate([x[:, 64 * h:64 * h + 64] for h in range(SWA_GROUP * g, SWA_GROUP * (g + 1))], axis=0)


def _unstack_heads(x4):
    return jnp.concatenate([x4[WINDOW * a:WINDOW * (a + 1)] for a in range(SWA_GROUP)], axis=1)


def _group_sinks(sink_ref, g):
    head = lax.broadcasted_iota(jnp.int32, (GROUP_ROWS, 1), 0) // WINDOW
    out = jnp.full((GROUP_ROWS, 1), sink_ref[0, SWA_GROUP * g], F32)
    for a in range(1, SWA_GROUP):
        out = jnp.where(head == a, sink_ref[0, SWA_GROUP * g + a], out)
    return out


def _swa_probs(qh, kk, bias_h, sink, valid):
    s = _dot_nt(qh, kk) * SWA_SCALE + bias_h
    s = jnp.where(valid, s, -jnp.inf)
    m = jnp.maximum(jnp.max(s, axis=-1, keepdims=True), sink)
    p = jnp.exp(s - m)
    ps = jnp.exp(sink - m)
    inv = 1.0 / (jnp.sum(p, axis=-1, keepdims=True) + ps)
    return p * inv, ps * inv


SWA_ROWS = SWA_SUB * WINDOW


def _swa_specs():
    prev = lambda n: jnp.maximum(SWA_SUB * n - 1, 0)
    return [pl.BlockSpec((SWA_ROWS, 512), lambda n: (n, 0)),
            pl.BlockSpec((SWA_ROWS, 128), lambda n: (n, 4)),
            pl.BlockSpec((WINDOW, 128), lambda n: (prev(n), 4)),
            pl.BlockSpec((SWA_ROWS, 128), lambda n: (n, 5)),
            pl.BlockSpec((WINDOW, 128), lambda n: (prev(n), 5)),
            pl.BlockSpec((SWA_HEADS, WINDOW, 2 * WINDOW), lambda n: (0, 0, 0)),
            pl.BlockSpec(memory_space=pltpu.SMEM)]


def _swa_fwd(proj, rel_bias, bucket, sinks):
    S = proj.shape[0]

    def body(q_ref, kc_ref, kp_ref, vc_ref, vp_ref, rb_ref, sink_ref, bk_ref, o_ref, bias_ref):
        n = pl.program_id(0)

        @pl.when(n == 0)
        def _():
            bk = bk_ref[...]
            for h in range(SWA_HEADS):
                acc = jnp.zeros((WINDOW, 2 * WINDOW), F32)
                for b in range(NUM_BUCKETS):
                    acc = jnp.where(bk == b, rb_ref[b, h], acc)
                bias_ref[h] = acc

        for u in range(SWA_SUB):
            rows = slice(WINDOW * u, WINDOW * (u + 1))
            valid = _swa_valid(n > 0 if u == 0 else True)
            q = q_ref[rows, :].astype(BF16)
            kfull = _swa_keys(kp_ref, kc_ref, u)
            vfull = _swa_keys(vp_ref, vc_ref, u)
            for g in range(SWA_HEADS // SWA_GROUP):
                kk = kfull[:, 64 * g:64 * g + 64]
                vv = vfull[:, 64 * g:64 * g + 64]
                bias4 = bias_ref[SWA_GROUP * g:SWA_GROUP * (g + 1)].reshape(GROUP_ROWS, 2 * WINDOW)
                pk, _ = _swa_probs(_stack_heads(q, g), kk, bias4, _group_sinks(sink_ref, g), valid)
                o_ref[rows, 256 * g:256 * (g + 1)] = _unstack_heads(_dot(pk.astype(BF16), vv))

    specs = _swa_specs()
    whole = pl.BlockSpec((SWA_HEADS, WINDOW, 2 * WINDOW), lambda n: (0, 0, 0))
    return pl.pallas_call(
        body, name="swa_fwd", grid=(S // SWA_ROWS,),
        in_specs=specs[:5] + [pl.BlockSpec(memory_space=pltpu.SMEM), specs[6],
                              pl.BlockSpec((WINDOW, 2 * WINDOW), lambda n: (0, 0))],
        out_specs=[pl.BlockSpec((SWA_ROWS, 512), lambda n: (n, 0)), whole],
        out_shape=[jax.ShapeDtypeStruct((S, 512), F32), jax.ShapeDtypeStruct((SWA_HEADS, WINDOW, 2 * WINDOW), F32)],
        compiler_params=_params(("arbitrary",)),
    )(proj, proj, proj, proj, proj, rel_bias, sinks, bucket)


def _swa_bwd(proj, bias, sinks, o, do, bucket):
    S = proj.shape[0]
    nb = S // SWA_ROWS

    def body(q_ref, kc_ref, kp_ref, vc_ref, vp_ref, bias_ref, sink_ref, o_ref, do_ref, bk_ref,
             dq_ref, dk_ref, dv_ref, drb_ref, dsk_ref, dbias_acc):
        n = pl.program_id(0)

        @pl.when(n == 0)
        def _():
            dk_ref[...] = jnp.zeros_like(dk_ref)
            dv_ref[...] = jnp.zeros_like(dv_ref)
            dsk_ref[...] = jnp.zeros_like(dsk_ref)
            dbias_acc[...] = jnp.zeros_like(dbias_acc)
            drb_ref[...] = jnp.zeros_like(drb_ref)

        for u in range(SWA_SUB):
            rows = slice(WINDOW * u, WINDOW * (u + 1))
            blk = SWA_SUB * n + u
            valid = _swa_valid(n > 0 if u == 0 else True)
            q = q_ref[rows, :].astype(BF16)
            dov = do_ref[rows, :]
            ov = o_ref[rows, :]
            kfull = _swa_keys(kp_ref, kc_ref, u)
            vfull = _swa_keys(vp_ref, vc_ref, u)
            prow = pl.ds(pl.multiple_of(jnp.maximum(blk - 1, 0) * WINDOW, WINDOW), WINDOW)
            crow = pl.ds(pl.multiple_of(blk * WINDOW, WINDOW), WINDOW)
            for g in range(SWA_HEADS // SWA_GROUP):
                heads = slice(SWA_GROUP * g, SWA_GROUP * (g + 1))
                kk = kfull[:, 64 * g:64 * g + 64]
                vv = vfull[:, 64 * g:64 * g + 64]
                q4 = _stack_heads(q, g)
                pk, psink = _swa_probs(q4, kk, bias_ref[heads].reshape(GROUP_ROWS, 2 * WINDOW),
                                       _group_sinks(sink_ref, g), valid)
                pkb = pk.astype(BF16)
                do4 = _stack_heads(dov, g)
                dob = do4.astype(BF16)
                dp = _dot_nt(dob, vv)
                delta = jnp.sum(do4 * _stack_heads(ov, g), axis=-1, keepdims=True)
                ds = pk * (dp - delta)
                dsink = -psink * delta
                for a in range(SWA_GROUP):
                    h = SWA_GROUP * g + a
                    part = jnp.sum(dsink[WINDOW * a:WINDOW * (a + 1)], keepdims=True)
                    dsk_ref[h:h + 1, :] += jnp.broadcast_to(part, (1, 128))
                dbias_acc[heads] += ds.reshape(SWA_GROUP, WINDOW, 2 * WINDOW)
                dsb = (ds * SWA_SCALE).astype(BF16)
                dq_ref[rows, 256 * g:256 * (g + 1)] = _unstack_heads(_dot(dsb, kk))
                dkk = _dot_tn(dsb, q4)
                dvv = _dot_tn(pkb, dob)
                dk_ref[prow, 64 * g:64 * g + 64] += dkk[:WINDOW]
                dk_ref[crow, 64 * g:64 * g + 64] += dkk[WINDOW:]
                dv_ref[prow, 64 * g:64 * g + 64] += dvv[:WINDOW]
                dv_ref[crow, 64 * g:64 * g + 64] += dvv[WINDOW:]

        @pl.when(n == nb - 1)
        def _():
            bk = bk_ref[...]
            for h in range(SWA_HEADS):
                dbh = dbias_acc[h]
                for b in range(NUM_BUCKETS):
                    val = jnp.sum(jnp.where(bk == b, dbh, 0.0), keepdims=True)
                    row = h * NUM_BUCKETS + b
                    drb_ref[row:row + 1, :] = jnp.broadcast_to(val, (1, 128))

    full = lambda shape: pl.BlockSpec(shape, lambda n: tuple(0 for _ in shape))
    return pl.pallas_call(
        body, name="swa_bwd", grid=(nb,),
        in_specs=_swa_specs() + [pl.BlockSpec((SWA_ROWS, 512), lambda n: (n, 0)),
                                 pl.BlockSpec((SWA_ROWS, 512), lambda n: (n, 0)), full((WINDOW, 2 * WINDOW))],
        out_specs=[pl.BlockSpec((SWA_ROWS, 512), lambda n: (n, 0)), full((S, 128)), full((S, 128)),
                   full((NUM_BUCKETS * 8, 128)), full((8, 128))],
        out_shape=[jax.ShapeDtypeStruct((S, 512), F32), jax.ShapeDtypeStruct((S, 128), F32),
                   jax.ShapeDtypeStruct((S, 128), F32), jax.ShapeDtypeStruct((NUM_BUCKETS * 8, 128), F32),
                   jax.ShapeDtypeStruct((8, 128), F32)],
        scratch_shapes=[pltpu.VMEM((SWA_HEADS, WINDOW, 2 * WINDOW), F32)],
        compiler_params=_params(("arbitrary",)),
    )(proj, proj, proj, proj, proj, bias, sinks, o, do, bucket)


def _rope_tables(S):
    inv = np.float32(ROPE_THETA) ** (-np.arange(0, MLA_ROPE, 2, dtype=np.float32) / np.float32(MLA_ROPE))
    ang = np.arange(S, dtype=np.float32)[:, None] * inv[None, :]
    cos, sin = np.cos(ang), np.sin(ang)
    return (jnp.asarray(np.tile(np.concatenate([cos, cos], axis=1), (1, 2))),
            jnp.asarray(np.tile(np.concatenate([-sin, sin], axis=1), (1, 2))))


def _rope_wide(ref):
    t = ref[...]
    return jnp.concatenate([t, t], axis=1)


def _swap_halves(x):
    w = x.shape[-1]
    lane = lax.broadcasted_iota(jnp.int32, x.shape, x.ndim - 1)
    return jnp.where((lane % 64) < 32, pltpu.roll(x, w - 32, x.ndim - 1), pltpu.roll(x, 32, x.ndim - 1))


def _uq_group_rows(wuq_ref):
    per = MLA_NOPE + MLA_ROPE
    nope = [wuq_ref[per * h:per * h + MLA_NOPE, :] for h in range(MLA_HEADS)]
    rope = [wuq_ref[per * h + MLA_NOPE:per * (h + 1), :] for h in range(MLA_HEADS)]
    return jnp.concatenate(nope + rope, axis=0)


def _mla_pre_fwd(proj, qn_w, kvn_w, wuqT, wukv, cos, sin, *, tm=ROW_TILE):
    S = proj.shape[0]
    tm = min(tm, S)

    def body(ql_ref, kl_ref, kr_ref, qw_ref, kw_ref, wuq_ref, wukv_ref, cos_ref, sin_ref,
             qc_ref, kc_ref, vv_ref):
        ql = ql_ref[...]
        qn = (ql * _rstd(ql) * qw_ref[...]).astype(BF16)
        q = _dot_nt(qn, _uq_group_rows(wuq_ref))
        cs, sn = _rope_wide(cos_ref), _rope_wide(sin_ref)
        qr = q[:, 512:768]
        qr = qr * cs + _swap_halves(qr) * sn
        half = lax.broadcasted_iota(jnp.int32, (tm, 128), 1) // 64
        kl = kl_ref[...]
        kvn = (kl * _rstd(kl) * kw_ref[...]).astype(BF16)
        kr = kr_ref[...]
        kr = kr * cs[:, :128] + _swap_halves(kr) * sn[:, :128]
        kr2 = (kr + pltpu.roll(kr, 64, 1)).astype(BF16)
        kv = _dot(kvn, jnp.concatenate([wukv_ref[j] for j in range(2 * MLA_HEADS)], axis=1)).astype(BF16)
        for h in range(MLA_HEADS):
            qc_ref[h, :, 0:128] = q[:, 128 * h:128 * h + 128].astype(BF16)
            chunk = qr[:, 128 * (h // 2):128 * (h // 2) + 128]
            qc_ref[h, :, 128:256] = jnp.where(half == (h % 2), chunk, 0.0).astype(BF16)
            kc_ref[h, :, 0:128] = kv[:, 256 * h:256 * h + 128]
            kc_ref[h, :, 128:256] = kr2
            vv_ref[h] = kv[:, 256 * h + 128:256 * h + 256]

    const = lambda shape: pl.BlockSpec(shape, lambda i: tuple(0 for _ in shape))
    return pl.pallas_call(
        body, name="mla_pre_fwd", grid=(S // tm,),
        in_specs=[pl.BlockSpec((tm, 256), lambda i: (i, 3)), pl.BlockSpec((tm, 128), lambda i: (i, 8)),
                  pl.BlockSpec((tm, 128), lambda i: (i, 9)), const((1, 256)), const((1, 128)),
                  const((768, 256)), const((8, 128, 128)),
                  pl.BlockSpec((tm, 128), lambda i: (i, 0)), pl.BlockSpec((tm, 128), lambda i: (i, 0))],
        out_specs=[pl.BlockSpec((MLA_HEADS, tm, 256), lambda i: (0, i, 0)),
                   pl.BlockSpec((MLA_HEADS, tm, 256), lambda i: (0, i, 0)),
                   pl.BlockSpec((MLA_HEADS, tm, 128), lambda i: (0, i, 0))],
        out_shape=[jax.ShapeDtypeStruct((MLA_HEADS, S, 256), BF16), jax.ShapeDtypeStruct((MLA_HEADS, S, 256), BF16),
                   jax.ShapeDtypeStruct((MLA_HEADS, S, 128), BF16)],
        compiler_params=_params(("parallel",)),
    )(proj, proj, proj, qn_w, kvn_w, wuqT, wukv, cos, sin)


def _causal(i, j, t):
    row = i * t + lax.broadcasted_iota(jnp.int32, (t, t), 0)
    col = j * t + lax.broadcasted_iota(jnp.int32, (t, t), 1)
    return col <= row


def _mla_attn_fwd(qc, kc, vv, *, t=512):
    S = qc.shape[1]
    t = min(t, S)

    def body(q_ref, k_ref, v_ref, o_ref, l_ref):
        i = pl.program_id(0)
        diag = _causal(0, 0, t)

        def step(j, carry, masked):
            rows = pl.ds(pl.multiple_of(j * t, t), t)
            out = []
            for h in range(MLA_HEADS):
                m, l, acc = carry[h]
                s = _dot_nt(q_ref[h], k_ref[h, rows, :]) * MLA_SCALE
                if masked:
                    s = jnp.where(diag, s, -jnp.inf)
                m_new = jnp.maximum(m, jnp.max(s, axis=-1, keepdims=True))
                alpha = jnp.exp(m - m_new)
                p = jnp.exp(s - m_new)
                l = alpha * l + jnp.sum(p, axis=-1, keepdims=True)
                acc = alpha * acc + _dot(p.astype(BF16), v_ref[h, rows, :])
                out.append((m_new, l, acc))
            return tuple(out)

        init = tuple((jnp.full((t, 1), -jnp.inf, F32), jnp.zeros((t, 1), F32), jnp.zeros((t, MLA_V), F32))
                     for _ in range(MLA_HEADS))
        carry = lax.fori_loop(0, i, lambda j, c: step(j, c, False), init)
        carry = step(i, carry, True)
        for h in range(MLA_HEADS):
            m, l, acc = carry[h]
            o_ref[:, 128 * h:128 * h + 128] = acc / l
            l_ref[h] = jnp.broadcast_to(m + jnp.log(l), (t, 128))

    return pl.pallas_call(
        body, name="mla_attn_fwd", grid=(S // t,),
        in_specs=[pl.BlockSpec((MLA_HEADS, t, 256), lambda i: (0, i, 0)),
                  pl.BlockSpec((MLA_HEADS, S, 256), lambda i: (0, 0, 0)),
                  pl.BlockSpec((MLA_HEADS, S, 128), lambda i: (0, 0, 0))],
        out_specs=[pl.BlockSpec((t, 512), lambda i: (i, 0)),
                   pl.BlockSpec((MLA_HEADS, t, 128), lambda i: (0, i, 0))],
        out_shape=[jax.ShapeDtypeStruct((S, 512), F32), jax.ShapeDtypeStruct((MLA_HEADS, S, 128), F32)],
        compiler_params=_params(("parallel",)),
    )(qc, kc, vv)


def _mla_attn_bwd(qc, kc, vv, o, lse, do, *, t=512, tq=1024):
    S = qc.shape[1]
    t = min(t, S)
    tq = min(tq, S)
    nblk = S // t
    hp = MLA_HEADS
    once = pl.Buffered(1)

    def body(q_ref, k_ref, v_ref, o_ref, l_ref, do_ref, dq_ref, dk_ref, dv_ref):
        j = pl.program_id(1)

        @pl.when(j == 0)
        def _():
            dq_ref[...] = jnp.zeros_like(dq_ref)

        first = (j * t) // tq

        def step(i, carry, masked):
            rows = pl.ds(pl.multiple_of(i * tq, tq), tq)
            if masked:
                row = i * tq + lax.broadcasted_iota(jnp.int32, (tq, t), 0)
                col = j * t + lax.broadcasted_iota(jnp.int32, (tq, t), 1)
                visible = col <= row
            out = []
            for h in range(hp):
                dk, dv = carry[h]
                k = k_ref[h]
                q = q_ref[h, rows, :]
                dov = do_ref[rows, 128 * h:128 * h + 128]
                lrow = l_ref[h, rows, :][:, 0:1]
                p = jnp.exp(_dot_nt(q, k) * MLA_SCALE - lrow)
                if masked:
                    p = jnp.where(visible, p, 0.0)
                dob = dov.astype(BF16)
                dv = dv + _dot_tn(p.astype(BF16), dob)
                dp = _dot_nt(dob, v_ref[h])
                delta = jnp.sum(dov * o_ref[rows, 128 * h:128 * h + 128], axis=-1, keepdims=True)
                ds = (p * (dp - delta) * MLA_SCALE).astype(BF16)
                dk = dk + _dot_tn(ds, q)
                dq_ref[h, rows, :] += _dot(ds, k)
                out.append((dk, dv))
            return tuple(out)

        init = tuple((jnp.zeros((t, 256), F32), jnp.zeros((t, MLA_V), F32)) for _ in range(hp))
        carry = step(first, init, True)
        carry = lax.fori_loop(first + 1, S // tq, lambda i, c: step(i, c, False), carry)
        for h in range(hp):
            dk_ref[h] = carry[h][0]
            dv_ref[h] = carry[h][1]

    return pl.pallas_call(
        body, name="mla_attn_bwd", grid=(MLA_HEADS // hp, nblk),
        in_specs=[pl.BlockSpec((hp, S, 256), lambda g, j: (g, 0, 0), pipeline_mode=once),
                  pl.BlockSpec((hp, t, 256), lambda g, j: (g, j, 0)),
                  pl.BlockSpec((hp, t, 128), lambda g, j: (g, j, 0)),
                  pl.BlockSpec((S, 128 * hp), lambda g, j: (0, g), pipeline_mode=once),
                  pl.BlockSpec((hp, S, 128), lambda g, j: (g, 0, 0), pipeline_mode=once),
                  pl.BlockSpec((S, 128 * hp), lambda g, j: (0, g), pipeline_mode=once)],
        out_specs=[pl.BlockSpec((hp, S, 256), lambda g, j: (g, 0, 0)),
                   pl.BlockSpec((hp, t, 256), lambda g, j: (g, j, 0)),
                   pl.BlockSpec((hp, t, 128), lambda g, j: (g, j, 0))],
        out_shape=[jax.ShapeDtypeStruct((MLA_HEADS, S, 256), F32), jax.ShapeDtypeStruct((MLA_HEADS, S, 256), F32),
                   jax.ShapeDtypeStruct((MLA_HEADS, S, 128), F32)],
        compiler_params=_params(("parallel", "arbitrary")),
    )(qc, kc, vv, o, lse, do)


def _mla_pre_bwd(proj, qn_w, kvn_w, wuqT, wukv, cos, sin, dqc, dkc, dvv, *, tm=ROW_TILE):
    S = proj.shape[0]
    tm = min(tm, S)

    def body(ql_ref, kl_ref, qw_ref, kw_ref, wuq_ref, wukv_ref, cos_ref, sin_ref, dqc_ref, dkc_ref, dvv_ref,
             dql_ref, dkl_ref, dkr_ref, gq_ref, gkv_ref, part_ref, gq_acc, gkv_acc):
        @pl.when(pl.program_id(0) == 0)
        def _():
            gq_acc[...] = jnp.zeros_like(gq_acc)
            gkv_acc[...] = jnp.zeros_like(gkv_acc)
            part_ref[...] = jnp.zeros_like(part_ref)

        cs, sn = _rope_wide(cos_ref), _rope_wide(sin_ref)
        half = lax.broadcasted_iota(jnp.int32, (tm, 128), 1) // 64
        ql = ql_ref[...]
        rq = _rstd(ql)
        qhat = ql * rq
        qw = qw_ref[...]
        qn = (qhat * qw).astype(BF16)
        chunks = []
        for pair in range(2):
            chunks.append(jnp.where(half == 0, dqc_ref[2 * pair, :, 128:256], dqc_ref[2 * pair + 1, :, 128:256]))
        dqr = jnp.concatenate(chunks, axis=1)
        dqr = dqr * cs + _swap_halves(dqr * sn)
        dq = jnp.concatenate([dqc_ref[h, :, 0:128] for h in range(MLA_HEADS)] + [dqr], axis=1).astype(BF16)
        gq_acc[...] += _dot_tn(dq, qn)
        dqn = _dot(dq, _uq_group_rows(wuq_ref))
        part_ref[0:1, :] += jnp.sum(dqn * qhat, axis=0, keepdims=True)
        dql_ref[...] = _rms_bwd(dqn * qw, qhat, rq)
        kl = kl_ref[...]
        rk = _rstd(kl)
        khat = kl * rk
        kw = kw_ref[...]
        kvn = (khat * kw).astype(BF16)
        dkvn = jnp.zeros((tm, MLA_KVR), F32)
        dkr2 = jnp.zeros((tm, 128), F32)
        for h in range(MLA_HEADS):
            dkn = dkc_ref[h, :, 0:128].astype(BF16)
            dvh = dvv_ref[h].astype(BF16)
            gkv_acc[2 * h] += _dot_tn(kvn, dkn)
            gkv_acc[2 * h + 1] += _dot_tn(kvn, dvh)
            dkvn += _dot_nt(dkn, wukv_ref[2 * h]) + _dot_nt(dvh, wukv_ref[2 * h + 1])
            dkr2 += dkc_ref[h, :, 128:256]
        part_ref[1:2, 0:128] += jnp.sum(dkvn * khat, axis=0, keepdims=True)
        dkl_ref[...] = _rms_bwd(dkvn * kw, khat, rk)
        dkr = jnp.where(half == 0, dkr2 + pltpu.roll(dkr2, 64, 1), 0.0)
        dkr_ref[...] = dkr * cs[:, :128] + _swap_halves(dkr * sn[:, :128])

        @pl.when(pl.program_id(0) == S // tm - 1)
        def _():
            gkv_ref[...] = gkv_acc[...].astype(BF16)
            per = MLA_NOPE + MLA_ROPE
            for h in range(MLA_HEADS):
                gq_ref[per * h:per * h + MLA_NOPE, :] = gq_acc[MLA_NOPE * h:MLA_NOPE * (h + 1), :].astype(BF16)
                gq_ref[per * h + MLA_NOPE:per * (h + 1), :] = gq_acc[512 + MLA_ROPE * h:512 + MLA_ROPE * (h + 1), :].astype(BF16)

    const = lambda shape: pl.BlockSpec(shape, lambda i: tuple(0 for _ in shape))
    heads = lambda w: pl.BlockSpec((MLA_HEADS, tm, w), lambda i: (0, i, 0))
    return pl.pallas_call(
        body, name="mla_pre_bwd", grid=(S // tm,),
        in_specs=[pl.BlockSpec((tm, 256), lambda i: (i, 3)), pl.BlockSpec((tm, 128), lambda i: (i, 8)),
                  const((1, 256)), const((1, 128)), const((768, 256)), const((8, 128, 128)),
                  pl.BlockSpec((tm, 128), lambda i: (i, 0)), pl.BlockSpec((tm, 128), lambda i: (i, 0)),
                  heads(256), heads(256), heads(128)],
        out_specs=[pl.BlockSpec((tm, 256), lambda i: (i, 0)), pl.BlockSpec((tm, 128), lambda i: (i, 0)),
                   pl.BlockSpec((tm, 128), lambda i: (i, 0)), const((768, 256)), const((8, 128, 128)), const((8, 256))],
        out_shape=[jax.ShapeDtypeStruct((S, 256), F32), jax.ShapeDtypeStruct((S, 128), F32),
                   jax.ShapeDtypeStruct((S, 128), F32), jax.ShapeDtypeStruct((768, 256), BF16),
                   jax.ShapeDtypeStruct((8, 128, 128), BF16), jax.ShapeDtypeStruct((8, 256), F32)],
        scratch_shapes=[pltpu.VMEM((768, 256), F32), pltpu.VMEM((8, 128, 128), F32)],
        compiler_params=_params(("arbitrary",)),
    )(proj, proj, qn_w, kvn_w, wuqT, wukv, cos, sin, dqc, dkc, dvv)


def _mix_out_fwd(x, oa, ob, w_o, vecs, *, tm=ROW_TILE):
    S = x.shape[0]
    tm = min(tm, S)

    def body(x_ref, oa_ref, ob_ref, w_ref, vec_ref, xo_ref, mo_ref):
        mo = _dot(oa_ref[...].astype(BF16), w_ref[0:512, :]) + _dot(ob_ref[...].astype(BF16), w_ref[512:1024, :])
        mo_ref[...] = mo
        xo_ref[...] = x_ref[...] + vec_ref[3:4, :] * mo

    row = pl.BlockSpec((tm, D), lambda i: (i, 0))
    half = pl.BlockSpec((tm, 512), lambda i: (i, 0))
    return pl.pallas_call(
        body, name="mix_out_fwd", grid=(S // tm,),
        in_specs=[row, half, half, pl.BlockSpec((D, D), lambda i: (0, 0)), pl.BlockSpec((8, D), lambda i: (0, 0))],
        out_specs=[row, row],
        out_shape=[jax.ShapeDtypeStruct((S, D), F32), jax.ShapeDtypeStruct((S, D), F32)],
        compiler_params=_params(("parallel",)),
    )(x, oa, ob, w_o, vecs)


def _mix_out_bwd(dxo, mo, oa, ob, w_o, vecs, *, tm=ROW_TILE):
    S = dxo.shape[0]
    tm = min(tm, S)

    def body(dx_ref, mo_ref, oa_ref, ob_ref, w_ref, vec_ref, doa_ref, dob_ref, gw_ref, part_ref, gw_acc):
        @pl.when(pl.program_id(0) == 0)
        def _():
            gw_acc[...] = jnp.zeros_like(gw_acc)
            part_ref[...] = jnp.zeros_like(part_ref)

        dx = dx_ref[...]
        part_ref[0:1, :] += jnp.sum(dx * mo_ref[...], axis=0, keepdims=True)
        dmo = (vec_ref[3:4, :] * dx).astype(BF16)
        doa_ref[...] = _dot_nt(dmo, w_ref[0:512, :])
        dob_ref[...] = _dot_nt(dmo, w_ref[512:1024, :])
        gw_acc[0:512, :] += _dot_tn(oa_ref[...].astype(BF16), dmo)
        gw_acc[512:1024, :] += _dot_tn(ob_ref[...].astype(BF16), dmo)

        @pl.when(pl.program_id(0) == S // tm - 1)
        def _():
            gw_ref[...] = gw_acc[...].astype(BF16)

    row = pl.BlockSpec((tm, D), lambda i: (i, 0))
    half = pl.BlockSpec((tm, 512), lambda i: (i, 0))
    return pl.pallas_call(
        body, name="mix_out_bwd", grid=(S // tm,),
        in_specs=[row, row, half, half, pl.BlockSpec((D, D), lambda i: (0, 0)), pl.BlockSpec((8, D), lambda i: (0, 0))],
        out_specs=[half, half, pl.BlockSpec((D, D), lambda i: (0, 0)), pl.BlockSpec((8, D), lambda i: (0, 0))],
        out_shape=[jax.ShapeDtypeStruct((S, 512), F32), jax.ShapeDtypeStruct((S, 512), F32),
                   jax.ShapeDtypeStruct((D, D), BF16), jax.ShapeDtypeStruct((8, D), F32)],
        scratch_shapes=[pltpu.VMEM((D, D), F32)],
        compiler_params=_params(("arbitrary",)),
    )(dxo, mo, oa, ob, w_o, vecs)


def _mix_in_bwd(h, w_inT, dq, dk, dv, dql, dkl, dkr, *, tm=ROW_TILE):
    S = h.shape[0]
    tm = min(tm, S)
    wid = (512, 128, 128, 256, 128, 128)

    def body(h_ref, w_ref, dq_ref, dk_ref, dv_ref, dql_ref, dkl_ref, dkr_ref, dh_ref, gw_ref):
        @pl.when(pl.program_id(0) == 0)
        def _():
            gw_ref[...] = jnp.zeros_like(gw_ref)

        parts = (dq_ref, dk_ref, dv_ref, dql_ref, dkl_ref, dkr_ref)
        dproj = jnp.concatenate([ref[...].astype(BF16) for ref in parts], axis=1)
        dh_ref[...] = _dot(dproj, w_ref[...])
        gw_ref[...] += _dot_tn(dproj, h_ref[...])[0:D_IN, :]

    row = pl.BlockSpec((tm, D), lambda i: (i, 0))
    part = lambda w: pl.BlockSpec((tm, w), lambda i: (i, 0))
    return pl.pallas_call(
        body, name="mix_in_bwd", grid=(S // tm,),
        in_specs=[row, pl.BlockSpec((D_IN_PAD, D), lambda i: (0, 0))] + [part(w) for w in wid],
        out_specs=[row, pl.BlockSpec((D_IN, D), lambda i: (0, 0))],
        out_shape=[jax.ShapeDtypeStruct((S, D), F32), jax.ShapeDtypeStruct((D_IN, D), F32)],
        compiler_params=_params(("arbitrary",)),
    )(h, w_inT, dq, dk, dv, dql, dkl, dkr)


def _vecs(norm_w, mod9, k):
    return jnp.concatenate([norm_w.reshape(1, D), mod9[3 * k:3 * k + 3], jnp.zeros((4, D), F32)], axis=0)


def _local_step(x, tgt, mod9, norms, sinks, rel_bias, q_norm, kv_norm, W, on_grads=None):
    if on_grads is None:
        on_grads = lambda group, grads, after, vecs: vecs
    S = x.shape[0]
    v1 = _vecs(norms["ffn1"], mod9, 0)
    v2 = _vecs(norms["mix"], mod9, 1)
    v3 = _vecs(norms["ffn2"], mod9, 2)
    bucket = jnp.asarray(_bucket_table())
    cos, sin = _rope_tables(S)
    if isinstance(W, dict):
        full, W = W, (lambda group, after, vecs: (full, vecs))

    W1, v1 = W("ffn1", [], v1)
    x1, h1, a1, b1, f1 = _ffn_fwd(x, v1, W1["g1T"], W1["u1T"], W1["d1"], name="ffn1_fwd")
    W2, v2 = W("mixer", [x1], v2)
    wuqT = W2["w_uqT"]
    h2, proj, w_inT = _mix_in_fwd(x1, v2, W2["w_inT"])
    oa, bias = _swa_fwd(proj, rel_bias, bucket, sinks)
    qc, kc, vv = _mla_pre_fwd(proj, q_norm, kv_norm, wuqT, W2["w_ukv"], cos, sin)
    ob, lse = _mla_attn_fwd(qc, kc, vv)
    _, v2o = W("ffn2_on_its_way", [oa, ob], v2)
    x2, mo = _mix_out_fwd(x1, oa, ob, W2["w_o"], v2o)
    W3, v3 = W("ffn2", [x2], v3)
    x3, h3, a3, b3, f3 = _ffn_fwd(x2, v3, W3["g3T"], W3["u3T"], W3["d3"], name="ffn2_fwd")
    dx3, head_part, df3 = _head(x3, tgt, norms["final"], f3, v3)

    gg3, gu3, gd3, dh3 = _ffn_bwd_main(h3, df3, a3, b3, W3["g3T"], W3["u3T"], W3["d3"], name="ffn2_bwd")
    ffn2 = {"g3T": gg3, "u3T": gu3, "d3": gd3}
    v3 = on_grads("ffn2", ffn2, [], v3)
    dx2, n3_part = _norm_bwd(dh3, x2, dx3, v3, name="ffn2_norm_bwd")
    v2 = on_grads("ffn2", None, [dx2], v2)
    doa, dob, g_wo, g2_part = _mix_out_bwd(dx2, mo, oa, ob, W2["w_o"], v2)
    dq, dk, dv, drb, dsk = _swa_bwd(proj, bias, sinks, oa, doa, bucket)
    dqc, dkc, dvv = _mla_attn_bwd(qc, kc, vv, ob, lse, dob)
    dql, dkl, dkr, g_uq, g_ukv, mla_part = _mla_pre_bwd(proj, q_norm, kv_norm, wuqT, W2["w_ukv"], cos, sin, dqc, dkc, dvv)
    dh2, g_win = _mix_in_bwd(h2, w_inT, dq, dk, dv, dql, dkl, dkr)
    mixer = {"w_inT": g_win, "w_uqT": g_uq, "w_ukv": g_ukv, "w_o": g_wo}
    v2 = on_grads("mixer", mixer, [], v2)
    dx1, n2_part, df1 = _norm_bwd(dh2, x1, dx2, v2, name="mix_norm_bwd", below=(f1, v1))
    started = on_grads("mixer", None, [dx1], jnp.zeros((1, 1), F32))
    gg1, gu1, gd1, dh1 = _ffn_bwd_main(h1, df1, a1, b1, W1["g1T"], W1["u1T"], W1["d1"], name="ffn1_bwd",
                                       after=[started])
    ffn1 = {"g1T": gg1, "u1T": gu1, "d1": gd1}
    v1 = on_grads("ffn1", ffn1, [], v1)
    dx0, n1_part = _norm_bwd(dh1, x, dx1, v1, name="ffn1_norm_bwd")

    grads = {**ffn1, **ffn2, **mixer}
    return head_part[1, 0], dx0, grads, _pack_vec(n1_part, n2_part, n3_part, head_part, g2_part, mla_part, dsk, drb)


SMALL_LAYOUT = (("norm_ffn1", 1024), ("norm_mix", 1024), ("norm_ffn2", 1024), ("norm_final", 1024),
                ("q_norm", 256), ("kv_norm", 128), ("sinks", 128), ("rel_bias", 256))
N_SMALL = sum(n for _, n in SMALL_LAYOUT)
LOSS_SLOT = 4 * 1024 + 256 + 128 + SWA_HEADS
N_MODVEC = N_MOD * D
N_VEC = N_MODVEC + N_SMALL


def _pack_vec(n1, n2, n3, head, g2, mla, dsk, drb):
    def body(n1_ref, n2_ref, n3_ref, head_ref, g2_ref, mla_ref, dsk_ref, drb_ref, out_ref):
        rows = [n1_ref[1:2, :], n1_ref[2:3, :], n2_ref[3:4, :], n2_ref[1:2, :], n2_ref[2:3, :], g2_ref[0:1, :],
                n3_ref[1:2, :], n3_ref[2:3, :], head_ref[3:4, :],
                n1_ref[0:1, :], n2_ref[0:1, :], n3_ref[0:1, :], head_ref[0:1, :]]
        for i, row in enumerate(rows):
            out_ref[:, D * i:D * (i + 1)] = row
        off = D * len(rows)
        out_ref[:, off:off + 256] = mla_ref[0:1, :]
        out_ref[:, off + 256:off + 384] = mla_ref[1:2, 0:128]

        def diagonal(block):
            r = lax.broadcasted_iota(jnp.int32, block.shape, 0)
            lane = lax.broadcasted_iota(jnp.int32, block.shape, 1)
            return jnp.sum(jnp.where(r == lane, block, 0.0), axis=0, keepdims=True)

        lane = lax.broadcasted_iota(jnp.int32, (1, 128), 1)
        out_ref[:, off + 384:off + 512] = jnp.where(lane == SWA_HEADS, head_ref[1:2, 0:128], diagonal(dsk_ref[...]))
        out_ref[:, off + 512:off + 640] = diagonal(drb_ref[0:128, :])
        out_ref[:, off + 640:off + 768] = diagonal(drb_ref[128:256, :])

    vm = pl.BlockSpec(memory_space=pltpu.VMEM)
    return pl.pallas_call(body, name="pack_vec", in_specs=[vm] * 8, out_specs=vm,
                          out_shape=jax.ShapeDtypeStruct((1, N_VEC), F32))(n1, n2, n3, head, g2, mla, dsk, drb)


def _coords():
    return lax.axis_index("x"), lax.axis_index("y"), lax.axis_index("c")


def _flip(v, bit):
    return 1 - v if bit else v


def _peer(r):
    x, y, c = _coords()
    return (_flip(x, r & 4), _flip(y, r & 2), _flip(c, r & 1))


class _ModExchange:
    def __init__(self, c_ref, w_ref, b_ref, mod_ref, ca_ref, call_ref, part_ref, send_sems, recv_sems):
        self.refs = (c_ref, w_ref, b_ref, mod_ref, ca_ref, call_ref, part_ref)
        self.sems = (send_sems, recv_sems)
        x, y, c = _coords()
        self.me = 4 * x + 2 * y + c
        self.sends = []

    def _copy(self, phase, r):
        c_ref, _, _, mod_ref, _, call_ref, part_ref = self.refs
        src, dst = ((call_ref.at[self.me], call_ref.at[self.me]) if phase == 0 else
                    (part_ref.at[self.me ^ r], mod_ref.at[self.me]))
        return pltpu.make_async_remote_copy(src, dst, self.sems[0].at[phase, r], self.sems[1].at[phase, r],
                                            device_id=_peer(r), device_id_type=MESH)

    def _start(self, phase):
        for r in range(1, N_DEV):
            self.sends.append(self._copy(phase, r))
            self.sends[-1].start()

    def begin(self):
        c_ref, _, _, _, _, call_ref, _ = self.refs
        call_ref[self.me] = jnp.concatenate([c_ref[...], jnp.zeros((7, D), F32)], axis=0)
        self._start(0)

    def middle(self):
        _, w_ref, b_ref, mod_ref, ca_ref, call_ref, part_ref = self.refs
        for r in range(1, N_DEV):
            self._copy(0, r).wait_recv()
        cv = call_ref[...].reshape(8 * N_DEV, D)
        ca = (cv * _sigmoid(cv)).astype(BF16)
        ca_ref[...] = ca
        part_ref[...] = _dot(ca, w_ref[...].astype(BF16)).reshape(part_ref.shape)
        mod_ref[self.me] = part_ref[self.me] + b_ref[self.me]
        self._start(1)

    def end(self):
        _, _, b_ref, mod_ref, _, _, _ = self.refs
        for r in range(1, N_DEV):
            self._copy(1, r).wait_recv()
            mod_ref[self.me ^ r] = mod_ref[self.me ^ r] + b_ref[self.me ^ r]
        for cp in self.sends:
            cp.wait_send()


def _mod_bwd(allvec, ca, me_idx):
    W = N_MODVEC // N_DEV

    def body(me_ref, all_ref, cols_ref, ca_ref, gw_ref, sum_ref):
        in_first_row = lax.broadcasted_iota(jnp.int32, (N_DEV, 8, W), 1) == 0
        dm = jnp.where(in_first_row, cols_ref[...], 0.0).reshape(8 * N_DEV, W)
        gw_ref[...] = _dot_tn(ca_ref[...], dm.astype(BF16))
        total = all_ref[0]
        for k in range(1, N_DEV):
            total = total + all_ref[k]
        sum_ref[...] = total

    return pl.pallas_call(
        body, name="mod_bwd",
        grid_spec=pltpu.PrefetchScalarGridSpec(
            num_scalar_prefetch=1, grid=(1,),
            in_specs=[pl.BlockSpec((N_DEV, 1, N_VEC), lambda i, me: (0, 0, 0)),
                      pl.BlockSpec((N_DEV, 1, W), lambda i, me: (0, 0, me[0])),
                      pl.BlockSpec((8 * N_DEV, D), lambda i, me: (0, 0))],
            out_specs=[pl.BlockSpec((D, W), lambda i, me: (0, 0)), pl.BlockSpec((1, N_VEC), lambda i, me: (0, 0))]),
        out_shape=[jax.ShapeDtypeStruct((D, W), F32), jax.ShapeDtypeStruct((1, N_VEC), F32)],
        compiler_params=_params(("arbitrary",)),
    )(me_idx, allvec, allvec, ca)


def _wgather(shards, later, c_row, w_mod, b_shards):
    n, nl = len(shards), len(later)
    rows = [s.shape[0] for s in shards]
    W = w_mod.shape[1]
    cast = [(s.shape, True) for s in shards] + [(s.shape, s.dtype != dt) for s, dt in later]
    staging = [pltpu.VMEM(shape, dt) for shape, is_cast in cast if is_cast for dt in (F32, BF16)]

    def body(*refs):
        ins, (c_ref, w_ref, b_ref), raw = refs[:n], refs[n:n + 3], refs[n + 3:n + 3 + nl]
        o = n + 3 + 2 * nl
        outs, (mod_ref, ca_ref), lands = refs[o:o + n], refs[o + n:o + n + 2], refs[o + n + 2:o + n + 2 + nl]
        o += n + 2 + nl
        send_sems, recv_sems, local_sems, load_sems, store_sems = refs[o:o + 5]
        mod = _ModExchange(c_ref, w_ref, b_ref, mod_ref, ca_ref, *refs[o + 5:o + 9])
        stage = iter(refs[o + 9:])
        staged = [(next(stage), next(stage)) if is_cast else None for _, is_cast in cast]
        mod.begin()
        x, y, c = _coords()
        me = 4 * x + 2 * y + c
        sib, xn, yn = (x, y, 1 - c), (1 - x, y, c), (x, 1 - y, c)
        block = lambda px, py, pc: 4 * px + 2 * py + pc

        def part(k, blk, half):
            if half is None:
                return outs[k].at[blk]
            return outs[k].at[blk, pl.ds(half * (rows[k] // 2), rows[k] // 2)]

        def copy(k, slot, blk, to, half=None, src=None):
            ref = part(k, blk, half)
            return pltpu.make_async_remote_copy(
                src_ref=ref if src is None else src, dst_ref=ref, send_sem=send_sems.at[k, slot],
                recv_sem=recv_sems.at[k, slot], device_id=to, device_id_type=MESH)

        def in_bf16(sources, first):
            loads = [pltpu.make_async_copy(src, staged[first + i][0], load_sems.at[first + i])
                     if staged[first + i] else None for i, src in enumerate(sources)]
            for cp in loads:
                if cp is not None:
                    cp.start()
            for i, src in enumerate(sources):
                if loads[i] is None:
                    yield i, src
                    continue
                loads[i].wait()
                wide, narrow = staged[first + i]
                narrow[...] = wide[...].astype(BF16)
                yield i, narrow

        mine = [ref for _, ref in in_bf16(ins, 0)]
        local = [pltpu.make_async_copy(mine[k], outs[k].at[me], local_sems.at[k]) for k in range(n)]
        for cp in local:
            cp.start()
        sent = [copy(k, slot, me, to, src=mine[k]) for k in range(n) for slot, to in ((0, sib), (1, xn), (2, yn))]
        for cp in sent:
            cp.start()
        mod.middle()
        for k, ref in in_bf16(raw, n):
            local.append(pltpu.make_async_copy(ref, lands[k].at[me], store_sems.at[k]))
            local[-1].start()
        bx, by, bd = block(1 - x, y, c), block(x, 1 - y, c), block(1 - x, 1 - y, c)
        for k in range(n):
            copy(k, 1, bx, sib).wait_recv()
            sent += [copy(k, 4, bx, yn, half=1), copy(k, 5, bx, sib)]
            sent[-2].start()
            sent[-1].start()
        for k in range(n):
            copy(k, 2, by, sib).wait_recv()
            sent += [copy(k, 3, by, xn, half=0), copy(k, 6, by, sib)]
            sent[-2].start()
            sent[-1].start()
        for k in range(n):
            copy(k, 3, bd, sib, half=0).wait_recv()
            copy(k, 4, bd, sib, half=1).wait_recv()
            sent.append(copy(k, 7, bd, sib))
            sent[-1].start()
        for k in range(n):
            copy(k, 0, block(x, y, 1 - c), sib).wait_recv()
            for slot, blk in ((5, block(1 - x, y, 1 - c)), (6, block(x, 1 - y, 1 - c)), (7, block(1 - x, 1 - y, 1 - c))):
                copy(k, slot, blk, sib).wait_recv()
        for cp in sent:
            cp.wait_send()
        for cp in local:
            cp.wait()
        mod.end()

    anyspec, vm = pl.BlockSpec(memory_space=pl.ANY), pl.BlockSpec(memory_space=pltpu.VMEM)
    zones = [lax.empty((N_DEV,) + s.shape, dt) for s, dt in later]
    out = pl.pallas_call(
        body, name="wgather", in_specs=[anyspec] * n + [vm] * 3 + [anyspec] * (2 * nl),
        out_specs=[anyspec] * n + [vm] * 2 + [anyspec] * nl,
        out_shape=[jax.ShapeDtypeStruct((N_DEV,) + s.shape, BF16) for s in shards]
        + [jax.ShapeDtypeStruct((N_DEV, 8, W), F32), jax.ShapeDtypeStruct((8 * N_DEV, D), BF16)]
        + [jax.ShapeDtypeStruct(z.shape, z.dtype) for z in zones],
        input_output_aliases={n + 3 + nl + i: n + 2 + i for i in range(nl)},
        scratch_shapes=[pltpu.SemaphoreType.DMA((n, 8)), pltpu.SemaphoreType.DMA((n, 8)),
                        pltpu.SemaphoreType.DMA((n,)), pltpu.SemaphoreType.DMA((n + nl,)),
                        pltpu.SemaphoreType.DMA((nl,)),
                        pltpu.VMEM((N_DEV, 8, D), F32), pltpu.VMEM((N_DEV, 8, W), F32),
                        pltpu.SemaphoreType.DMA((2, N_DEV)), pltpu.SemaphoreType.DMA((2, N_DEV))] + staging,
        compiler_params=_params(),
    )(*shards, c_row, w_mod, b_shards, *[s for s, _ in later], *zones)
    return out[:n], out[n], out[n + 1], list(out[n + 2:])


class _GatherCopies:
    def __init__(self, lands, send_sems, recv_sems, k0=0, batches=None):
        x, y, c = _coords()
        me = 4 * x + 2 * y + c
        sib = (x, y, 1 - c)
        chips = [(1 - x, y), (x, 1 - y), (1 - x, 1 - y)]

        def copy(k, slot, block, to):
            return pltpu.make_async_remote_copy(
                src_ref=lands[k].at[block], dst_ref=lands[k].at[block],
                send_sem=send_sems.at[7 * (k0 + k) + slot], recv_sem=recv_sems.at[7 * (k0 + k) + slot],
                device_id=to, device_id_type=MESH)

        n = len(lands)
        self.first = [copy(k, 0, me, sib) for k in range(n)]
        for batch in batches or [range(n)]:
            self.first += [copy(k, 1 + j, me, (cx, cy, c)) for j, (cx, cy) in enumerate(chips) for k in batch]
        self.landed = [copy(k, 1 + j, 4 * cx + 2 * cy + c, sib) for j, (cx, cy) in enumerate(chips) for k in range(n)]
        self.passed = [copy(k, 4 + j, 4 * cx + 2 * cy + c, sib) for j, (cx, cy) in enumerate(chips) for k in range(n)]
        self.from_sib = [copy(k, 0, 4 * x + 2 * y + (1 - c), sib) for k in range(n)]
        self.from_sib += [copy(k, 4 + j, 4 * cx + 2 * cy + (1 - c), sib) for j, (cx, cy) in enumerate(chips)
                          for k in range(n)]


def _token(carry):
    return [] if carry is None else [carry], jax.ShapeDtypeStruct((8, 128), F32) if carry is None else carry


def _gather_start(lands, *, name, batches=None, carry=None):
    n = len(lands)
    carried, token = _token(carry)

    def body(*refs):
        n_in = n + len(carried)
        for cp in _GatherCopies(refs[:n], refs[n_in], refs[n_in + 1], batches=batches).first:
            cp.start()
        refs[-1][...] = refs[n][...] if carried else jnp.zeros_like(refs[-1])

    vm = pl.BlockSpec(memory_space=pltpu.VMEM)
    out = pl.pallas_call(
        body, name=name,
        out_shape=(pltpu.SemaphoreType.DMA((7 * n,)), pltpu.SemaphoreType.DMA((7 * n,)),
                   *[pltpu.HBM(l.shape, l.dtype) for l in lands], jax.ShapeDtypeStruct(token.shape, token.dtype)),
        in_specs=[HBM_SPEC] * n + [vm] * len(carried),
        out_specs=(SEM_SPEC, SEM_SPEC, *[HBM_SPEC] * n, vm),
        input_output_aliases={i: 2 + i for i in range(n)},
        compiler_params=pltpu.CompilerParams(has_side_effects=DATAFLOW),
    )(*[_in_hbm(l) for l in lands], *carried)
    return out[0], out[1], list(out[2:2 + n]), out[-1]


def _gather_pass(send_sems, recv_sems, lands, after, *, name, stage, k0=0, carry=None):
    n = len(lands)
    carried, token = _token(carry)

    def body(*refs):
        cps = _GatherCopies(refs[:n], refs[n], refs[n + 1], k0)
        if stage == "landed":
            for cp in cps.landed:
                cp.wait_recv()
        else:
            for cp in cps.passed:
                cp.start()
        refs[-1][...] = refs[n + 2 + len(after)][...] if carried else jnp.zeros_like(refs[-1])

    vm = pl.BlockSpec(memory_space=pltpu.VMEM)
    out = pl.pallas_call(
        body, name=name,
        out_shape=(*[pltpu.HBM(l.shape, l.dtype) for l in lands], jax.ShapeDtypeStruct(token.shape, token.dtype)),
        in_specs=[HBM_SPEC] * n + [SEM_SPEC, SEM_SPEC] + [pl.BlockSpec(memory_space=pl.ANY)] * len(after)
        + [vm] * len(carried),
        out_specs=(*[HBM_SPEC] * n, vm),
        input_output_aliases={i: i for i in range(n)},
        compiler_params=pltpu.CompilerParams(has_side_effects=DATAFLOW),
    )(*lands, send_sems, recv_sems, *after, *carried)
    return list(out[:n]), out[-1]


def _gather_end(send_sems, recv_sems, lands, after, *, name, k0=0):
    n = len(lands)

    def body(*refs):
        cps = _GatherCopies(refs[:n], refs[n], refs[n + 1], k0)
        for cp in cps.from_sib:
            cp.wait_recv()
        for cp in cps.first + cps.passed:
            cp.wait_send()

    out = pl.pallas_call(
        body, name=name,
        out_shape=[pltpu.HBM(l.shape, l.dtype) for l in lands],
        in_specs=[HBM_SPEC] * n + [SEM_SPEC, SEM_SPEC] + [pl.BlockSpec(memory_space=pl.ANY)] * len(after),
        out_specs=[HBM_SPEC] * n,
        input_output_aliases={i: i for i in range(n)},
        compiler_params=pltpu.CompilerParams(has_side_effects=DATAFLOW),
    )(*lands, send_sems, recv_sems, *after)
    return list(out)


def _d2d_copies(grads, lands, send_sems, recv_sems):
    x, y, c = _coords()
    return [pltpu.make_async_remote_copy(
        src_ref=grads[k].at[2 * q + (1 - c)], dst_ref=lands[k].at[q],
        send_sem=send_sems.at[4 * k + q], recv_sem=recv_sems.at[4 * k + q],
        device_id=(x, y, 1 - c), device_id_type=MESH) for k in range(len(grads)) for q in range(4)]


def _direct_copies(grads, lands, send_sems, recv_sems):
    x, y, c = _coords()
    me = 4 * x + 2 * y + c
    return [pltpu.make_async_remote_copy(
        src_ref=grads[k].at[me ^ r], dst_ref=lands[k].at[r - 1],
        send_sem=send_sems.at[7 * k + r - 1], recv_sem=recv_sems.at[7 * k + r - 1],
        device_id=_peer(r), device_id_type=MESH) for k in range(len(grads)) for r in range(1, N_DEV)]


def _vec_copies(srcs, lands, send_sems, recv_sems):
    x, y, c = _coords()
    me = 4 * x + 2 * y + c
    return [pltpu.make_async_remote_copy(
        src_ref=lands[0].at[me], dst_ref=lands[0].at[me], send_sem=send_sems.at[r - 1], recv_sem=recv_sems.at[r - 1],
        device_id=_peer(r), device_id_type=MESH) for r in range(1, N_DEV)]


def _chipsum(gs, sibs, cidx, *, name):
    n = len(gs)

    def body(c_ref, *refs):
        for k in range(n):
            refs[2 * n + k][...] = (refs[k][...].astype(F32) + refs[n + k][...].astype(F32)).astype(refs[2 * n + k].dtype)

    mine = [pl.BlockSpec((1,) + g.shape[1:], lambda q, c_ref: (2 * q + c_ref[0], 0, 0)) for g in gs]
    other = [pl.BlockSpec((1,) + g.shape[1:], lambda q, c_ref: (q, 0, 0)) for g in gs]
    return pl.pallas_call(
        body, name=name,
        grid_spec=pltpu.PrefetchScalarGridSpec(num_scalar_prefetch=1, grid=(4,), in_specs=mine + other, out_specs=other),
        out_shape=[jax.ShapeDtypeStruct((4,) + g.shape[1:], g.dtype) for g in gs],
        compiler_params=_params(("arbitrary",)),
    )(cidx, *gs, *sibs)


HBM_SPEC = pl.BlockSpec(memory_space=pltpu.HBM)
SEM_SPEC = pl.BlockSpec(memory_space=pltpu.SEMAPHORE)
DATAFLOW = pltpu.SideEffectType.DATAFLOW_SIDE_EFFECTING


def _in_hbm(a):
    return pltpu.with_memory_space_constraint(a, pltpu.HBM)


def _rs_step1_copies(sums, lands, send_sems, recv_sems):
    n = len(sums)
    direct, relay = lands[:n], lands[n:]
    x, y, c = _coords()
    xn, yn = (1 - x, y, c), (x, 1 - y, c)
    qx, qy, qd = 2 * (1 - x) + y, 2 * x + (1 - y), 2 * (1 - x) + (1 - y)
    cps = []
    for k in range(n):
        h = sums[k].shape[1] // 2
        a, b = pl.ds(0, h), pl.ds(h, h)
        moves = ((sums[k].at[qx, a], direct[k].at[0], xn), (sums[k].at[qy, b], direct[k].at[1], yn),
                 (sums[k].at[qd, a], relay[k].at[0], xn), (sums[k].at[qd, b], relay[k].at[1], yn))
        for s, (src, dst, to) in enumerate(moves):
            cps.append(pltpu.make_async_remote_copy(
                src_ref=src, dst_ref=dst, send_sem=send_sems.at[4 * k + s], recv_sem=recv_sems.at[4 * k + s],
                device_id=to, device_id_type=MESH))
    return cps


def _rs_step2_copies(relayed, lands, send_sems, recv_sems, k0=0):
    x, y, c = _coords()
    cps = []
    for k in range(len(relayed)):
        for s, to in enumerate(((1 - x, y, c), (x, 1 - y, c))):
            cps.append(pltpu.make_async_remote_copy(
                src_ref=relayed[k].at[s], dst_ref=lands[k].at[s], send_sem=send_sems.at[2 * (k0 + k) + s],
                recv_sem=recv_sems.at[2 * (k0 + k) + s], device_id=to, device_id_type=MESH))
    return cps


def _relay_sum(sums, relay, qxy, *, name):
    n = len(sums)

    def body(q_ref, *refs):
        for k in range(n):
            refs[2 * n + k][...] = (refs[k][...].astype(F32) + refs[n + k][...].astype(F32)).astype(refs[2 * n + k].dtype)

    half = lambda s: (1, s.shape[1] // 2) + s.shape[2:]
    return pl.pallas_call(
        body, name=name,
        grid_spec=pltpu.PrefetchScalarGridSpec(
            num_scalar_prefetch=1, grid=(2,),
            in_specs=[pl.BlockSpec(half(s), lambda t, q_ref: (q_ref[t], 1 - t, 0)) for s in sums]
            + [pl.BlockSpec(half(s), lambda t, q_ref: (1 - t, 0, 0)) for s in sums],
            out_specs=[pl.BlockSpec(half(s), lambda t, q_ref: (t, 0, 0)) for s in sums]),
        out_shape=[jax.ShapeDtypeStruct((2,) + half(s)[1:], s.dtype) for s in sums],
        compiler_params=_params(("arbitrary",)),
    )(qxy, *sums, *relay)


def _split_start(copies, srcs, lands, n_sems, after, *, name, carry=None):
    ns, nl = len(srcs), len(lands)
    carried, token = _token(carry)

    def body(*refs):
        n_in = ns + nl + len(after) + len(carried)
        for cp in copies(refs[:ns], refs[ns:ns + nl], refs[n_in], refs[n_in + 1]):
            cp.start()
        refs[-1][...] = refs[n_in - 1][...] if carried else jnp.zeros_like(refs[-1])

    bufs = [_in_hbm(a) for a in list(srcs) + list(lands)]
    vm = pl.BlockSpec(memory_space=pltpu.VMEM)
    out = pl.pallas_call(
        body, name=name,
        out_shape=(pltpu.SemaphoreType.DMA((n_sems,)), pltpu.SemaphoreType.DMA((n_sems,)),
                   *[pltpu.HBM(a.shape, a.dtype) for a in bufs], jax.ShapeDtypeStruct(token.shape, token.dtype)),
        in_specs=[HBM_SPEC] * len(bufs) + [pl.BlockSpec(memory_space=pl.ANY)] * len(after) + [vm] * len(carried),
        out_specs=(SEM_SPEC, SEM_SPEC, *[HBM_SPEC] * len(bufs), vm),
        input_output_aliases={i: 2 + i for i in range(len(bufs))},
        compiler_params=pltpu.CompilerParams(has_side_effects=DATAFLOW),
    )(*bufs, *after, *carried)
    return out[0], out[1], list(out[2:2 + ns]), list(out[2 + ns:2 + ns + nl]), out[-1]


def _split_wait(copies, send_sems, recv_sems, srcs, lands, after, *, name):
    ns, nl = len(srcs), len(lands)

    def body(*refs):
        for cp in copies(refs[:ns], refs[ns:ns + nl], refs[ns + nl], refs[ns + nl + 1]):
            cp.wait_send()
            cp.wait_recv()

    out = pl.pallas_call(
        body, name=name,
        out_shape=[pltpu.HBM(a.shape, a.dtype) for a in list(srcs) + list(lands)],
        in_specs=[HBM_SPEC] * (ns + nl) + [SEM_SPEC, SEM_SPEC] + [pl.BlockSpec(memory_space=pl.ANY)] * len(after),
        out_specs=[HBM_SPEC] * (ns + nl),
        input_output_aliases={i: i for i in range(ns + nl)},
        compiler_params=pltpu.CompilerParams(has_side_effects=DATAFLOW),
    )(*srcs, *lands, send_sems, recv_sems, *after)
    return list(out[:ns]), list(out[ns:])


ADAM_C1 = 1.0 / (1.0 - ADAM_B1 ** ADAM_STEP)
ADAM_C2 = 1.0 / (1.0 - ADAM_B2 ** ADAM_STEP)


def _adam_math(w, g, m, v):
    m2 = ADAM_B1 * m + (1.0 - ADAM_B1) * g
    v2 = ADAM_B2 * v + (1.0 - ADAM_B2) * (g * g)
    return -ADAM_LR * ((m2 * ADAM_C1) / (jnp.sqrt(v2 * ADAM_C2) + ADAM_EPS) + ADAM_WD * w), m2, v2


def _adamw(w, g, m, v, *, name, after=()):
    R, C = w.shape
    tr = R if R <= 512 else 256

    def body(w_ref, g_ref, m_ref, v_ref, *rest):
        d_ref, nm_ref, nv_ref = rest[len(after):]
        d_ref[...], nm_ref[...], nv_ref[...] = _adam_math(w_ref[...], g_ref[...], m_ref[...], v_ref[...])

    blk = pl.BlockSpec((tr, C), lambda i: (i, 0))
    return pl.pallas_call(
        body, name=name, grid=(R // tr,), in_specs=[blk] * 4 + [pl.BlockSpec(memory_space=pl.ANY)] * len(after),
        out_specs=[blk] * 3, out_shape=[jax.ShapeDtypeStruct((R, C), F32)] * 3,
        compiler_params=_params(("parallel",)),
    )(w, g, m, v, *after)


def _adamw_rs2(wmv, cs, direct, second, qidx, *, name):
    n = len(wmv)
    r, cc = wmv[0][0].shape
    h = r // 2

    def body(q_ref, *refs):
        ins, outs = refs[:6 * n], refs[6 * n:]
        for k in range(n):
            w_ref, m_ref, v_ref, c_ref, d1_ref, d2_ref = ins[6 * k:6 * k + 6]
            g_ref, d_ref, nm_ref, nv_ref = outs[4 * k:4 * k + 4]
            g = (c_ref[0].astype(F32) + d1_ref[0].astype(F32)) + d2_ref[0].astype(F32)
            g_ref[...] = g
            d_ref[...], nm_ref[...], nv_ref[...] = _adam_math(w_ref[...], g, m_ref[...], v_ref[...])

    blk = pl.BlockSpec((h, cc), lambda i, q_ref: (i, 0))
    one = [blk, blk, blk, pl.BlockSpec((1, h, cc), lambda i, q_ref: (q_ref[0], i, 0)),
           pl.BlockSpec((1, h, cc), lambda i, q_ref: (i, 0, 0)),
           pl.BlockSpec((1, h, cc), lambda i, q_ref: (1 - i, 0, 0))]
    out = pl.pallas_call(
        body, name=name,
        grid_spec=pltpu.PrefetchScalarGridSpec(num_scalar_prefetch=1, grid=(2,), in_specs=one * n,
                                               out_specs=[blk] * (4 * n)),
        out_shape=[jax.ShapeDtypeStruct((r, cc), F32)] * (4 * n),
        compiler_params=_params(("arbitrary",)),
    )(qidx, *[a for (w, m, v), c, d1, d2 in zip(wmv, cs, direct, second) for a in (w, m, v, c, d1, d2)])
    return [tuple(out[4 * k:4 * k + 4]) for k in range(n)]


def _adamw_rs(wmv, cs, rcv, qidx, *, name):
    n = len(wmv)
    shapes = [w.shape for w, _, _ in wmv]
    n_rcv = rcv[0].shape[0]
    halved = len(set(shapes)) == 1 and shapes[0][0] % 32 == 0 and shapes[0][0] > 128
    tiles = 2 if halved else 1

    def body(q_ref, *refs):
        ins, outs = refs[:5 * n], refs[5 * n:]
        for k in range(n):
            w_ref, m_ref, v_ref, c_ref, r_ref = ins[5 * k:5 * k + 5]
            g_ref, d_ref, nm_ref, nv_ref = outs[4 * k:4 * k + 4]
            g = c_ref[0].astype(F32)
            for j in range(n_rcv):
                g = g + r_ref[j].astype(F32)
            g_ref[...] = g
            d_ref[...], nm_ref[...], nv_ref[...] = _adam_math(w_ref[...], g, m_ref[...], v_ref[...])

    in_specs, out_specs = [], []
    for r, cc in shapes:
        blk = pl.BlockSpec((r // tiles, cc), lambda i, q_ref: (i, 0))
        in_specs += [blk, blk, blk, pl.BlockSpec((1, r // tiles, cc), lambda i, q_ref: (q_ref[0], i, 0)),
                     pl.BlockSpec((n_rcv, r // tiles, cc), lambda i, q_ref: (0, i, 0))]
        out_specs += [blk] * 4
    out = pl.pallas_call(
        body, name=name,
        grid_spec=pltpu.PrefetchScalarGridSpec(num_scalar_prefetch=1, grid=(tiles,), in_specs=in_specs,
                                               out_specs=out_specs),
        out_shape=[jax.ShapeDtypeStruct(s, F32) for s in shapes for _ in range(4)],
        compiler_params=_params(("arbitrary",)),
    )(qidx, *[a for (w, m, v), c, rc in zip(wmv, cs, rcv) for a in (w, m, v, c, rc)])
    return [tuple(out[4 * k:4 * k + 4]) for k in range(n)]


SMALL_PARAMS = ("norm_ffn1", "norm_mix", "norm_ffn2", "norm_final", "q_norm", "kv_norm", "sinks", "rel_bias", "b_mod")


def _adamw_small(gvec, wmv):
    shapes = [wmv[3 * i].shape for i in range(len(SMALL_PARAMS))]

    def body(*refs):
        g_all = refs[0]
        ins = refs[1:1 + 3 * len(SMALL_PARAMS)]
        outs = refs[1 + 3 * len(SMALL_PARAMS):]
        off = N_MODVEC
        for i, name in enumerate(SMALL_PARAMS):
            g_ref, d_ref, nm_ref, nv_ref = outs[4 * i:4 * i + 4]
            w_ref, m_ref, v_ref = ins[3 * i:3 * i + 3]
            start = 0 if name == "b_mod" else off
            rows, width = shapes[i]
            if name == "rel_bias":
                sub = lax.broadcasted_iota(jnp.int32, (rows, rows), 0)
                lane = lax.broadcasted_iota(jnp.int32, (rows, rows), 1)
                col = lax.broadcasted_iota(jnp.int32, (rows, width), 1)
                g = jnp.zeros((rows, width), F32)
                for h in range(width):
                    along = jnp.broadcast_to(g_all[:, start + rows * h:start + rows * (h + 1)], (rows, rows))
                    g = jnp.where(col == h, jnp.sum(jnp.where(sub == lane, along, 0.0), axis=1, keepdims=True), g)
            else:
                g = jnp.concatenate([g_all[:, start + width * r:start + width * (r + 1)] for r in range(rows)], axis=0)
            g_ref[...] = g
            d_ref[...], nm_ref[...], nv_ref[...] = _adam_math(w_ref[...], g, m_ref[...], v_ref[...])
            if name != "b_mod":
                off += dict(SMALL_LAYOUT)[name]

    vm = pl.BlockSpec(memory_space=pltpu.VMEM)
    n_out = 4 * len(SMALL_PARAMS)
    out = pl.pallas_call(
        body, name="adamw_small", in_specs=[vm] * (1 + len(wmv)), out_specs=[vm] * n_out,
        out_shape=[jax.ShapeDtypeStruct(shapes[i // 4], F32) for i in range(n_out)],
        compiler_params=_params(),
    )(gvec, *wmv)
    return {name: out[4 * i:4 * i + 4] for i, name in enumerate(SMALL_PARAMS)}


TRANSPOSED = ("g1T", "u1T", "g3T", "u3T", "w_inT", "w_uqT")


def kernel(x, c, w_mod, b_mod, norm_ffn1, ffn1_gate, ffn1_up, ffn1_down, norm_mix, w_in, q_norm, kv_norm, w_uq, w_ukv, sinks, w_o, norm_ffn2, ffn2_gate, ffn2_up, ffn2_down, rel_bias, norm_final, loss_target, m_w_mod, m_b_mod, m_norm_ffn1, m_ffn1_gate, m_ffn1_up, m_ffn1_down, m_norm_mix, m_w_in, m_q_norm, m_kv_norm, m_w_uq, m_w_ukv, m_sinks, m_w_o, m_norm_ffn2, m_ffn2_gate, m_ffn2_up, m_ffn2_down, m_rel_bias, m_norm_final, v_w_mod, v_b_mod, v_norm_ffn1, v_ffn1_gate, v_ffn1_up, v_ffn1_down, v_norm_mix, v_w_in, v_q_norm, v_kv_norm, v_w_uq, v_w_ukv, v_sinks, v_w_o, v_norm_ffn2, v_ffn2_gate, v_ffn2_up, v_ffn2_down, v_rel_bias, v_norm_final):
    mx, my, mc = _coords()
    cidx = jnp.reshape(mc, (1,)).astype(jnp.int32)
    qidx = jnp.reshape(2 * mx + my, (1,)).astype(jnp.int32)
    WM = w_mod.shape[2]

    shards = {"g1T": ffn1_gate[0].T, "u1T": ffn1_up[0].T, "d1": ffn1_down[0],
              "g3T": ffn2_gate[0].T, "u3T": ffn2_up[0].T, "d3": ffn2_down[0],
              "w_inT": w_in[0].T, "w_uqT": w_uq[0].T, "w_ukv": w_ukv[0], "w_o": w_o[0]}
    travels = lambda k: F32 if k == "w_inT" else BF16
    me = 4 * mx + 2 * my + mc
    groups = {"ffn1": ("g1T", "u1T", "d1"), "mixer": ("w_inT", "w_uqT", "w_ukv", "w_o"), "ffn2": ("g3T", "u3T", "d3")}
    arriving = {}
    later = groups["mixer"] + groups["ffn2"]
    place = {"mixer": 0, "ffn2": len(groups["mixer"])}

    gathered_ffn1, mod3, ca, zones = _wgather([shards[k] for k in groups["ffn1"]],
                                              [(shards[k], travels(k)) for k in later], c, w_mod[0],
                                              b_mod.reshape(N_DEV, 1, WM))
    mod9 = mod3[:, 0, :].reshape(N_MOD, D)

    def as_weights(group, gathered):
        return {k: g if k == "w_ukv" else g.reshape(N_DEV * g.shape[1], g.shape[2])
                for k, g in zip(groups[group], gathered)}

    def start_gather(carry):
        batches = [range(k0, k0 + len(groups[group])) for group, k0 in place.items()]
        send, recv, lands, started = _gather_start(zones, name="gather_start", batches=batches, carry=carry)
        for group, k0 in place.items():
            arriving[group] = (send, recv, lands[k0:k0 + len(groups[group])])
        return started

    def fetch(group, after, vecs):
        if group == "ffn1":
            return as_weights("ffn1", gathered_ffn1), start_gather(vecs)

        def pass_on(group, after, carry=None):
            send, recv, lands = arriving[group]
            lands, token = _gather_pass(send, recv, lands, after, name="gather_landed_" + group, stage="landed",
                                        k0=place[group])
            lands, token = _gather_pass(send, recv, lands, [token], name="gather_onward_" + group, stage="onward",
                                        k0=place[group], carry=carry)
            arriving[group] = (send, recv, lands)
            return token

        if group == "ffn2_on_its_way":
            return None, pass_on("ffn2", after, vecs)
        if group == "mixer":
            after = [pass_on("mixer", after)]
        send, recv, lands = arriving[group]
        return as_weights(group, _gather_end(send, recv, lands, after, name="gather_end_" + group,
                                             k0=place[group])), vecs

    norms ={"ffn1": norm_ffn1, "mix": norm_mix, "ffn2": norm_ffn2, "final": norm_final.reshape(1, D)}
    in_flight = {}

    def on_grads(group, g, after, vecs):
        if group != "ffn1":
            if g is None:
                return vecs
            names = list(g)
            by_dest = [g[k] if k == "w_ukv" else g[k].reshape((N_DEV, g[k].shape[0] // N_DEV) + g[k].shape[1:])
                       for k in names]
            lands = [lax.empty((N_DEV - 1,) + a.shape[1:], a.dtype) for a in by_dest]
            send, recv, by_dest, lands, token = _split_start(_direct_copies, by_dest, lands, 7 * len(names), after,
                                                             name="rs_start_" + group, carry=vecs)
            in_flight[group] = (names, send, recv, by_dest, lands, token)
            return token
        names = list(g)
        by_dest = [g[k].reshape((N_DEV, g[k].shape[0] // N_DEV) + g[k].shape[1:]) for k in names]
        lands = [lax.empty((4,) + a.shape[1:], a.dtype) for a in by_dest]
        send, recv, by_dest, lands, token = _split_start(_d2d_copies, by_dest, lands, 4 * len(names), after,
                                                         name="rs_d2d_start_" + group)
        finish("mixer", [token])
        by_dest, from_sib = _split_wait(_d2d_copies, send, recv, by_dest, lands, [done[-1]], name="rs_d2d_wait_" + group)
        sums = _chipsum(by_dest, from_sib, cidx, name="chipsum_" + group)
        halves = lambda: [lax.empty((2, s.shape[1] // 2) + s.shape[2:], s.dtype) for s in sums]
        send, recv, sums, lands, token = _split_start(_rs_step1_copies, sums, halves() + halves(), 4 * len(names), [],
                                                      name="rs_ici_start_" + group, carry=vecs)
        in_flight[group] = (names, send, recv, sums, lands, token)
        return token

    owners = {"g1T": ("ffn1_gate", ffn1_gate, m_ffn1_gate, v_ffn1_gate), "u1T": ("ffn1_up", ffn1_up, m_ffn1_up, v_ffn1_up),
              "d1": ("ffn1_down", ffn1_down, m_ffn1_down, v_ffn1_down),
              "g3T": ("ffn2_gate", ffn2_gate, m_ffn2_gate, v_ffn2_gate), "u3T": ("ffn2_up", ffn2_up, m_ffn2_up, v_ffn2_up),
              "d3": ("ffn2_down", ffn2_down, m_ffn2_down, v_ffn2_down),
              "w_inT": ("w_in", w_in, m_w_in, v_w_in), "w_uqT": ("w_uq", w_uq, m_w_uq, v_w_uq),
              "w_ukv": ("w_ukv", w_ukv, m_w_ukv, v_w_ukv), "w_o": ("w_o", w_o, m_w_o, v_w_o)}
    res, done = {}, []

    there = lambda k, a: a[0].T if k in TRANSPOSED else a[0]
    back = lambda k, a: a.T[None] if k in TRANSPOSED else a[None]

    def record(names, outs):
        for k, out in zip(names, outs):
            done.append(out[3])
            res[owners[k][0]] = tuple(back(k, a) for a in out)

    def finish(group, after):
        names, send, recv, sums, lands, _ = in_flight[group]
        wmv = [tuple(there(k, a) for a in owners[k][1:]) for k in names]
        own = jnp.reshape(me, (1,)).astype(jnp.int32)
        sums, lands = _split_wait(_direct_copies, send, recv, sums, lands, after, name="rs_wait_" + group)
        record(names, _adamw_rs(wmv, sums, lands, own, name="adamw_" + group))

    _, grad_x, _, vec = _local_step(
        x[0], loss_target[0], mod9, norms, sinks, rel_bias, q_norm, kv_norm, fetch, on_grads=on_grads)

    names, send, recv, sums, lands, step1_started = in_flight["ffn1"]
    vec = vec.reshape(1, 1, N_VEC)
    allvec = lax.dynamic_update_slice(lax.empty((N_DEV, 1, N_VEC), F32), vec, (me, 0, 0))
    vsend, vrecv, _, (allvec,), vec_started = _split_start(_vec_copies, [], [allvec], N_DEV - 1, [step1_started],
                                                           name="vec_start")
    finish("ffn2", [vec_started])
    n = len(names)
    sums, lands = _split_wait(_rs_step1_copies, send, recv, sums, lands, [done[-1]], name="rs_ici_wait_ffn1")
    direct, relay = lands[:n], lands[n:]
    qxy = jnp.stack([2 * (1 - mx) + my, 2 * mx + (1 - my)]).astype(jnp.int32)
    relayed = _relay_sum(sums, relay, qxy, name="relay_sum_ffn1")
    second = [lax.empty(a.shape, a.dtype) for a in relayed]
    send, recv, relayed, second, step2_started = _split_start(_rs_step2_copies, relayed, second, 2 * n, [],
                                                              name="rs_ici_start2_ffn1")

    _, (allvec,) = _split_wait(_vec_copies, vsend, vrecv, [], [allvec], [step2_started], name="vec_wait")
    g_wmod, gvec = _mod_bwd(allvec, ca, jnp.reshape(me, (1,)).astype(jnp.int32))
    loss = gvec[0, N_MODVEC + LOSS_SLOT]
    small_in = {"norm_ffn1": (norm_ffn1, m_norm_ffn1, v_norm_ffn1), "norm_mix": (norm_mix, m_norm_mix, v_norm_mix),
                "norm_ffn2": (norm_ffn2, m_norm_ffn2, v_norm_ffn2), "norm_final": (norm_final, m_norm_final, v_norm_final),
                "q_norm": (q_norm, m_q_norm, v_q_norm), "kv_norm": (kv_norm, m_kv_norm, v_kv_norm),
                "sinks": (sinks, m_sinks, v_sinks), "rel_bias": (rel_bias, m_rel_bias, v_rel_bias),
                "b_mod": (b_mod, m_b_mod, v_b_mod)}
    as_row = lambda k, a: a if k == "rel_bias" else a.reshape(1, -1)
    from_row = lambda k, a: a if k == "rel_bias" else a.reshape(small_in[k][0].shape)
    small_out = _adamw_small(gvec, [as_row(k, a) for k in SMALL_PARAMS for a in small_in[k]])
    for k in SMALL_PARAMS:
        res[k] = tuple(from_row(k, a) for a in small_out[k])

    out = _adamw(w_mod[0], g_wmod, m_w_mod[0], v_w_mod[0], name="adamw_w_mod")
    res["w_mod"] = tuple(a[None] for a in (g_wmod,) + tuple(out))

    wmv = [tuple(there(k, a) for a in owners[k][1:]) for k in names]
    after = done + [out[2]] + [a for k in SMALL_PARAMS for a in res[k]]
    outs = []
    for i, k in enumerate(names):
        _, (sec,) = _split_wait(functools.partial(_rs_step2_copies, k0=i), send, recv, [relayed[i]], [second[i]],
                                after, name="rs_ici_wait2_" + k)
        outs.append(_adamw_rs2([wmv[i]], [sums[i]], [direct[i]], [sec], qidx, name="adamw_" + owners[k][0])[0])
        after = [outs[-1][3]]
    record(names, outs)

    order = ("w_mod", "b_mod", "norm_ffn1", "ffn1_gate", "ffn1_up", "ffn1_down", "norm_mix", "w_in", "q_norm",
             "kv_norm", "w_uq", "w_ukv", "sinks", "w_o", "norm_ffn2", "ffn2_gate", "ffn2_up", "ffn2_down",
             "rel_bias", "norm_final")
    return (loss, grad_x[None]) + tuple(res[nm][kind] for kind in range(4) for nm in order)
```

```python
import functools
import math

import numpy as np
import jax
import jax.numpy as jnp
from jax import lax
from jax.experimental import pallas as pl
from jax.experimental.pallas import tpu as pltpu

F32 = jnp.float32
BF16 = jnp.bfloat16
MESH = pl.DeviceIdType.MESH

N_DEV = 8
D = 1024
D_FF = 2816
EPS = 1e-6
N_MOD = 9
SWA_HEADS = 8
SWA_DH = 64
WINDOW = 128
MLA_HEADS = 4
MLA_NOPE = 128
MLA_ROPE = 64
MLA_V = 128
MLA_QR = 256
MLA_KVR = 128
ROPE_THETA = 10000.0
NUM_BUCKETS = 32
D_IN = 1216
D_IN_PAD = 1280
SWA_SCALE = SWA_DH ** -0.5
MLA_SCALE = (MLA_NOPE + MLA_ROPE) ** -0.5

ADAM_LR = 0.001
ADAM_B1 = 0.9
ADAM_B2 = 0.999
ADAM_EPS = 1e-08
ADAM_WD = 0.01
ADAM_STEP = 10

V7X_VMEM_LIMIT = 56 * 1024 * 1024
ROW_TILE = 512

NT_DIMS = (((1,), (1,)), ((), ()))
TN_DIMS = (((0,), (0,)), ((), ()))


def _dot(a, b):
    return jnp.dot(a, b, preferred_element_type=F32)


def _dot_nt(a, b):
    return lax.dot_general(a, b, NT_DIMS, preferred_element_type=F32)


def _dot_tn(a, b):
    return lax.dot_general(a, b, TN_DIMS, preferred_element_type=F32)


def _params(sem=None):
    return pltpu.CompilerParams(dimension_semantics=sem, vmem_limit_bytes=V7X_VMEM_LIMIT)


def _rstd(x):
    return lax.rsqrt(jnp.mean(x * x, axis=-1, keepdims=True) + EPS)


def _rms_bwd(dy, xhat, r):
    return r * (dy - xhat * jnp.mean(dy * xhat, axis=-1, keepdims=True))


def _sigmoid(a):
    return 1.0 / (1.0 + jnp.exp(-a))


def _ffn_fwd(x, vecs, wgT, wuT, wd, *, name, tm=256, tf=D_FF):
    S, F = x.shape[0], wd.shape[0]
    tm = min(tm, S)
    ni, nj = S // tm, F // tf

    def body(x_ref, vec_ref, wg_ref, wu_ref, wd_ref, xo_ref, h_ref, a_ref, b_ref, f_ref, acc_ref):
        j = pl.program_id(1)

        @pl.when(j == 0)
        def _():
            xv = x_ref[...]
            hn = xv * _rstd(xv) * vec_ref[0:1, :]
            h_ref[...] = (hn * (1.0 + vec_ref[2:3, :]) + vec_ref[1:2, :]).astype(BF16)

        h = h_ref[...]
        a = _dot_nt(h, wg_ref[...])
        b = _dot_nt(h, wu_ref[...])
        a_ref[...] = a.astype(BF16)
        b_ref[...] = b.astype(BF16)
        part = _dot((a * _sigmoid(a) * b).astype(BF16), wd_ref[...])

        def finish(f):
            f_ref[...] = f
            xo_ref[...] = x_ref[...] + (0.5 * vec_ref[3:4, :]) * f

        if nj == 1:
            finish(part)
        else:
            @pl.when(j == 0)
            def _():
                acc_ref[...] = part

            @pl.when((j > 0) & (j < nj - 1))
            def _():
                acc_ref[...] += part

            @pl.when(j == nj - 1)
            def _():
                finish(acc_ref[...] + part)

    row = pl.BlockSpec((tm, D), lambda i, j: (i, 0))
    wspec = pl.BlockSpec((tf, D), lambda i, j: (j, 0), pipeline_mode=pl.Buffered(1) if nj == 1 else None)
    act = pl.BlockSpec((tm, tf), lambda i, j: (i, j))
    return pl.pallas_call(
        body, name=name, grid=(ni, nj),
        in_specs=[row, pl.BlockSpec((8, D), lambda i, j: (0, 0)), wspec, wspec, wspec],
        out_specs=[row, row, act, act, row],
        out_shape=[jax.ShapeDtypeStruct((S, D), F32), jax.ShapeDtypeStruct((S, D), BF16),
                   jax.ShapeDtypeStruct((S, F), BF16), jax.ShapeDtypeStruct((S, F), BF16),
                   jax.ShapeDtypeStruct((S, D), F32)],
        scratch_shapes=[pltpu.VMEM((tm, D) if nj > 1 else (8, 128), F32)],
        compiler_params=_params(("parallel", "arbitrary")),
    )(x, vecs, wgT, wuT, wd)


def _ffn_bwd_main(h, df, a, b, wgT, wuT, wd, *, name, after=(), tm=2048, tf=256):
    S = h.shape[0]
    tm = min(tm, S)
    ni, nj = S // tm, D_FF // tf

    def body(h_hbm, df_hbm, a_ref, b_ref, wg_ref, wu_ref, wd_ref, *rest):
        gg_ref, gu_ref, gd_ref, dh_hbm, h_v, df_v, dh_v, gg_acc, gu_acc, gd_acc, sem = rest[len(after):]
        j = pl.program_id(0)
        i = pl.program_id(1)

        @pl.when((j == 0) & (i == 0))
        def _():
            c1 = pltpu.make_async_copy(h_hbm, h_v, sem.at[0])
            c2 = pltpu.make_async_copy(df_hbm, df_v, sem.at[1])
            c1.start()
            c2.start()
            c1.wait()
            c2.wait()

        @pl.when(i == 0)
        def _():
            gg_acc[...] = jnp.zeros_like(gg_acc)
            gu_acc[...] = jnp.zeros_like(gu_acc)
            gd_acc[...] = jnp.zeros_like(gd_acc)

        rows = pl.ds(pl.multiple_of(i * tm, tm), tm)
        hi = h_v[rows, :]
        dfi = df_v[rows, :]
        av = a_ref[...].astype(F32)
        bv = b_ref[...].astype(F32)
        sg = _sigmoid(av)
        sa = av * sg
        hsw = (sa * bv).astype(BF16)
        dhsw = _dot_nt(dfi, wd_ref[...])
        da = (dhsw * bv * (sg * (1.0 + av * (1.0 - sg)))).astype(BF16)
        db = (dhsw * sa).astype(BF16)
        gd_acc[...] += _dot_tn(hsw, dfi)
        gg_acc[...] += _dot_tn(da, hi)
        gu_acc[...] += _dot_tn(db, hi)
        dh = _dot(da, wg_ref[...]) + _dot(db, wu_ref[...])

        @pl.when(j == 0)
        def _():
            dh_v[rows, :] = dh

        @pl.when(j > 0)
        def _():
            dh_v[rows, :] += dh

        @pl.when(i == ni - 1)
        def _():
            gg_ref[...] = gg_acc[...].astype(BF16)
            gu_ref[...] = gu_acc[...].astype(BF16)
            gd_ref[...] = gd_acc[...].astype(BF16)

        @pl.when((j == nj - 1) & (i == ni - 1))
        def _():
            c3 = pltpu.make_async_copy(dh_v, dh_hbm, sem.at[2])
            c3.start()
            c3.wait()

    anyspec = pl.BlockSpec(memory_space=pl.ANY)
    wspec = pl.BlockSpec((tf, D), lambda j, i: (j, 0))
    act = pl.BlockSpec((tm, tf), lambda j, i: (i, j))
    return pl.pallas_call(
        body, name=name, grid=(nj, ni),
        in_specs=[anyspec, anyspec, act, act, wspec, wspec, wspec] + [anyspec] * len(after),
        out_specs=[wspec, wspec, wspec, anyspec],
        out_shape=[jax.ShapeDtypeStruct((D_FF, D), BF16)] * 3 + [jax.ShapeDtypeStruct((S, D), F32)],
        scratch_shapes=[pltpu.VMEM((S, D), BF16), pltpu.VMEM((S, D), BF16), pltpu.VMEM((S, D), F32),
                        pltpu.VMEM((tf, D), F32), pltpu.VMEM((tf, D), F32), pltpu.VMEM((tf, D), F32),
                        pltpu.SemaphoreType.DMA((3,))],
        compiler_params=_params(("arbitrary", "arbitrary")),
    )(h, df, a, b, wgT, wuT, wd, *after)


def _ffn_out_bwd(dx, f, gate, df_ref, part_ref):
    df_ref[...] = ((0.5 * gate) * dx).astype(BF16)
    part_ref[3:4, :] += 0.5 * jnp.sum(dx * f, axis=0, keepdims=True)


def _norm_bwd(dh, x, dxo, vecs, *, name, below=None, tm=ROW_TILE):
    S = x.shape[0]
    tm = min(tm, S)

    def body(dh_ref, x_ref, dxo_ref, vec_ref, *rest):
        dx_ref, part_ref = rest[-2 if below is None else -3], rest[-1 if below is None else -2]

        @pl.when(pl.program_id(0) == 0)
        def _():
            part_ref[...] = jnp.zeros_like(part_ref)

        dh = dh_ref[...]
        xv = x_ref[...]
        r = _rstd(xv)
        xhat = xv * r
        w = vec_ref[0:1, :]
        xn = xhat * w
        dxn = dh * (1.0 + vec_ref[2:3, :])
        part_ref[0:1, :] += jnp.sum(dxn * xhat, axis=0, keepdims=True)
        part_ref[1:2, :] += jnp.sum(dh, axis=0, keepdims=True)
        part_ref[2:3, :] += jnp.sum(dh * xn, axis=0, keepdims=True)
        dx = dxo_ref[...] + _rms_bwd(dxn * w, xhat, r)
        dx_ref[...] = dx
        if below is not None:
            _ffn_out_bwd(dx, rest[0][...], rest[1][3:4, :], rest[-1], part_ref)

    row = pl.BlockSpec((tm, D), lambda i: (i, 0))
    vec = pl.BlockSpec((8, D), lambda i: (0, 0))
    extra = [] if below is None else [row, vec]
    return pl.pallas_call(
        body, name=name, grid=(S // tm,), in_specs=[row, row, row, vec] + extra,
        out_specs=[row, vec] + ([] if below is None else [row]),
        out_shape=[jax.ShapeDtypeStruct((S, D), F32), jax.ShapeDtypeStruct((8, D), F32)]
        + ([] if below is None else [jax.ShapeDtypeStruct((S, D), BF16)]),
        compiler_params=_params(("arbitrary",)),
    )(dh, x, dxo, vecs, *([] if below is None else below))


def _head(x, tgt, nf, f, vecs, *, tm=ROW_TILE):
    S = x.shape[0]
    tm = min(tm, S)

    def body(x_ref, t_ref, nf_ref, f_ref, vec_ref, dx_ref, part_ref, df_ref):
        @pl.when(pl.program_id(0) == 0)
        def _():
            part_ref[...] = jnp.zeros_like(part_ref)

        xv = x_ref[...]
        r = _rstd(xv)
        xhat = xv * r
        w = nf_ref[...]
        e = xhat * w - t_ref[...]
        dy = e * (1.0 / D)
        part_ref[0:1, :] += jnp.sum(dy * xhat, axis=0, keepdims=True)
        part_ref[1:2, :] += jnp.sum(e * e) * (0.5 / D)
        dx = _rms_bwd(dy * w, xhat, r)
        dx_ref[...] = dx
        _ffn_out_bwd(dx, f_ref[...], vec_ref[3:4, :], df_ref, part_ref)

    row = pl.BlockSpec((tm, D), lambda i: (i, 0))
    vec = pl.BlockSpec((8, D), lambda i: (0, 0))
    return pl.pallas_call(
        body, name="head", grid=(S // tm,),
        in_specs=[row, row, pl.BlockSpec((1, D), lambda i: (0, 0)), row, vec],
        out_specs=[row, vec, row],
        out_shape=[jax.ShapeDtypeStruct((S, D), F32), jax.ShapeDtypeStruct((8, D), F32),
                   jax.ShapeDtypeStruct((S, D), BF16)],
        compiler_params=_params(("arbitrary",)),
    )(x, tgt, nf, f, vecs)


def _mix_in_fwd(x, vecs, w_inT, *, tm=ROW_TILE):
    S = x.shape[0]
    tm = min(tm, S)

    def body(x_ref, vec_ref, w_ref, h_ref, p_ref, wb_ref):
        @pl.when(pl.program_id(0) == 0)
        def _():
            wb_ref[0:D_IN, :] = w_ref[...].astype(BF16)
            wb_ref[D_IN:D_IN_PAD, :] = jnp.zeros((D_IN_PAD - D_IN, D), BF16)

        xv = x_ref[...]
        hn = xv * _rstd(xv) * vec_ref[0:1, :]
        h = (hn * (1.0 + vec_ref[2:3, :]) + vec_ref[1:2, :]).astype(BF16)
        h_ref[...] = h
        p_ref[...] = _dot_nt(h, wb_ref[...])

    row = pl.BlockSpec((tm, D), lambda i: (i, 0))
    return pl.pallas_call(
        body, name="mix_in_fwd", grid=(S // tm,),
        in_specs=[row, pl.BlockSpec((8, D), lambda i: (0, 0)),
                  pl.BlockSpec((D_IN, D), lambda i: (0, 0), pipeline_mode=pl.Buffered(1))],
        out_specs=[row, pl.BlockSpec((tm, D_IN_PAD), lambda i: (i, 0)), pl.BlockSpec((D_IN_PAD, D), lambda i: (0, 0))],
        out_shape=[jax.ShapeDtypeStruct((S, D), BF16), jax.ShapeDtypeStruct((S, D_IN_PAD), F32),
                   jax.ShapeDtypeStruct((D_IN_PAD, D), BF16)],
        compiler_params=_params(("arbitrary",)),
    )(x, vecs, w_inT)


def _bucket_table():
    qi = np.arange(WINDOW)[:, None]
    kj = np.arange(2 * WINDOW)[None, :]
    dist = qi + WINDOW - kj
    max_exact = NUM_BUCKETS // 2
    n = np.maximum(dist, 0)
    nf = np.maximum(n, 1).astype(np.float32)
    large = max_exact + (np.log(nf / np.float32(max_exact)) / np.float32(math.log(WINDOW / max_exact))
                         * np.float32(NUM_BUCKETS - max_exact)).astype(np.int32)
    large = np.minimum(large, NUM_BUCKETS - 1)
    return np.where(n < max_exact, n, large).astype(np.int32)


SWA_GROUP = 4
GROUP_ROWS = SWA_GROUP * WINDOW


SWA_SUB = 2


def _swa_valid(has_prev):
    row = lax.broadcasted_iota(jnp.int32, (GROUP_ROWS, 2 * WINDOW), 0) % WINDOW
    col = lax.broadcasted_iota(jnp.int32, (GROUP_ROWS, 2 * WINDOW), 1)
    dist = row + WINDOW - col
    return (dist >= 0) & (dist < WINDOW) & ((col >= WINDOW) | has_prev)


def _swa_keys(prev_ref, cur_ref, u):
    cur = cur_ref[...]
    before = prev_ref[...] if u == 0 else cur[WINDOW * (u - 1):WINDOW * u]
    return jnp.concatenate([before, cur[WINDOW * u:WINDOW * (u + 1)]], axis=0).astype(BF16)


def _stack_heads(x, g):
    return jnp.concatenate([x[:, 64 * h:64 * h + 64] for h in range(SWA_GROUP * g, SWA_GROUP * (g + 1))], axis=0)


def _unstack_heads(x4):
    return jnp.concatenate([x4[WINDOW * a:WINDOW * (a + 1)] for a in range(SWA_GROUP)], axis=1)


def _group_sinks(sink_ref, g):
    head = lax.broadcasted_iota(jnp.int32, (GROUP_ROWS, 1), 0) // WINDOW
    out = jnp.full((GROUP_ROWS, 1), sink_ref[0, SWA_GROUP * g], F32)
    for a in range(1, SWA_GROUP):
        out = jnp.where(head == a, sink_ref[0, SWA_GROUP * g + a], out)
    return out


def _swa_probs(qh, kk, bias_h, sink, valid):
    s = _dot_nt(qh, kk) * SWA_SCALE + bias_h
    s = jnp.where(valid, s, -jnp.inf)
    m = jnp.maximum(jnp.max(s, axis=-1, keepdims=True), sink)
    p = jnp.exp(s - m)
    ps = jnp.exp(sink - m)
    inv = 1.0 / (jnp.sum(p, axis=-1, keepdims=True) + ps)
    return p * inv, ps * inv


SWA_ROWS = SWA_SUB * WINDOW


def _swa_specs():
    prev = lambda n: jnp.maximum(SWA_SUB * n - 1, 0)
    return [pl.BlockSpec((SWA_ROWS, 512), lambda n: (n, 0)),
            pl.BlockSpec((SWA_ROWS, 128), lambda n: (n, 4)),
            pl.BlockSpec((WINDOW, 128), lambda n: (prev(n), 4)),
            pl.BlockSpec((SWA_ROWS, 128), lambda n: (n, 5)),
            pl.BlockSpec((WINDOW, 128), lambda n: (prev(n), 5)),
            pl.BlockSpec((SWA_HEADS, WINDOW, 2 * WINDOW), lambda n: (0, 0, 0)),
            pl.BlockSpec(memory_space=pltpu.SMEM)]


def _swa_fwd(proj, rel_bias, bucket, sinks):
    S = proj.shape[0]

    def body(q_ref, kc_ref, kp_ref, vc_ref, vp_ref, rb_ref, sink_ref, bk_ref, o_ref, bias_ref):
        n = pl.program_id(0)

        @pl.when(n == 0)
        def _():
            bk = bk_ref[...]
            for h in range(SWA_HEADS):
                acc = jnp.zeros((WINDOW, 2 * WINDOW), F32)
                for b in range(NUM_BUCKETS):
                    acc = jnp.where(bk == b, rb_ref[b, h], acc)
                bias_ref[h] = acc

        for u in range(SWA_SUB):
            rows = slice(WINDOW * u, WINDOW * (u + 1))
            valid = _swa_valid(n > 0 if u == 0 else True)
            q = q_ref[rows, :].astype(BF16)
            kfull = _swa_keys(kp_ref, kc_ref, u)
            vfull = _swa_keys(vp_ref, vc_ref, u)
            for g in range(SWA_HEADS // SWA_GROUP):
                kk = kfull[:, 64 * g:64 * g + 64]
                vv = vfull[:, 64 * g:64 * g + 64]
                bias4 = bias_ref[SWA_GROUP * g:SWA_GROUP * (g + 1)].reshape(GROUP_ROWS, 2 * WINDOW)
                pk, _ = _swa_probs(_stack_heads(q, g), kk, bias4, _group_sinks(sink_ref, g), valid)
                o_ref[rows, 256 * g:256 * (g + 1)] = _unstack_heads(_dot(pk.astype(BF16), vv))

    specs = _swa_specs()
    whole = pl.BlockSpec((SWA_HEADS, WINDOW, 2 * WINDOW), lambda n: (0, 0, 0))
    return pl.pallas_call(
        body, name="swa_fwd", grid=(S // SWA_ROWS,),
        in_specs=specs[:5] + [pl.BlockSpec(memory_space=pltpu.SMEM), specs[6],
                              pl.BlockSpec((WINDOW, 2 * WINDOW), lambda n: (0, 0))],
        out_specs=[pl.BlockSpec((SWA_ROWS, 512), lambda n: (n, 0)), whole],
        out_shape=[jax.ShapeDtypeStruct((S, 512), F32), jax.ShapeDtypeStruct((SWA_HEADS, WINDOW, 2 * WINDOW), F32)],
        compiler_params=_params(("arbitrary",)),
    )(proj, proj, proj, proj, proj, rel_bias, sinks, bucket)


def _swa_bwd(proj, bias, sinks, o, do, bucket):
    S = proj.shape[0]
    nb = S // SWA_ROWS

    def body(q_ref, kc_ref, kp_ref, vc_ref, vp_ref, bias_ref, sink_ref, o_ref, do_ref, bk_ref,
             dq_ref, dk_ref, dv_ref, drb_ref, dsk_ref, dbias_acc):
        n = pl.program_id(0)

        @pl.when(n == 0)
        def _():
            dk_ref[...] = jnp.zeros_like(dk_ref)
            dv_ref[...] = jnp.zeros_like(dv_ref)
            dsk_ref[...] = jnp.zeros_like(dsk_ref)
            dbias_acc[...] = jnp.zeros_like(dbias_acc)
            drb_ref[...] = jnp.zeros_like(drb_ref)

        for u in range(SWA_SUB):
            rows = slice(WINDOW * u, WINDOW * (u + 1))
            blk = SWA_SUB * n + u
            valid = _swa_valid(n > 0 if u == 0 else True)
            q = q_ref[rows, :].astype(BF16)
            dov = do_ref[rows, :]
            ov = o_ref[rows, :]
            kfull = _swa_keys(kp_ref, kc_ref, u)
            vfull = _swa_keys(vp_ref, vc_ref, u)
            prow = pl.ds(pl.multiple_of(jnp.maximum(blk - 1, 0) * WINDOW, WINDOW), WINDOW)
            crow = pl.ds(pl.multiple_of(blk * WINDOW, WINDOW), WINDOW)
            for g in range(SWA_HEADS // SWA_GROUP):
                heads = slice(SWA_GROUP * g, SWA_GROUP * (g + 1))
                kk = kfull[:, 64 * g:64 * g + 64]
                vv = vfull[:, 64 * g:64 * g + 64]
                q4 = _stack_heads(q, g)
                pk, psink = _swa_probs(q4, kk, bias_ref[heads].reshape(GROUP_ROWS, 2 * WINDOW),
                                       _group_sinks(sink_ref, g), valid)
                pkb = pk.astype(BF16)
                do4 = _stack_heads(dov, g)
                dob = do4.astype(BF16)
                dp = _dot_nt(dob, vv)
                delta = jnp.sum(do4 * _stack_heads(ov, g), axis=-1, keepdims=True)
                ds = pk * (dp - delta)
                dsink = -psink * delta
                for a in range(SWA_GROUP):
                    h = SWA_GROUP * g + a
                    part = jnp.sum(dsink[WINDOW * a:WINDOW * (a + 1)], keepdims=True)
                    dsk_ref[h:h + 1, :] += jnp.broadcast_to(part, (1, 128))
                dbias_acc[heads] += ds.reshape(SWA_GROUP, WINDOW, 2 * WINDOW)
                dsb = (ds * SWA_SCALE).astype(BF16)
                dq_ref[rows, 256 * g:256 * (g + 1)] = _unstack_heads(_dot(dsb, kk))
                dkk = _dot_tn(dsb, q4)
                dvv = _dot_tn(pkb, dob)
                dk_ref[prow, 64 * g:64 * g + 64] += dkk[:WINDOW]
                dk_ref[crow, 64 * g:64 * g + 64] += dkk[WINDOW:]
                dv_ref[prow, 64 * g:64 * g + 64] += dvv[:WINDOW]
                dv_ref[crow, 64 * g:64 * g + 64] += dvv[WINDOW:]

        @pl.when(n == nb - 1)
        def _():
            bk = bk_ref[...]
            for h in range(SWA_HEADS):
                dbh = dbias_acc[h]
                for b in range(NUM_BUCKETS):
                    val = jnp.sum(jnp.where(bk == b, dbh, 0.0), keepdims=True)
                    row = h * NUM_BUCKETS + b
                    drb_ref[row:row + 1, :] = jnp.broadcast_to(val, (1, 128))

    full = lambda shape: pl.BlockSpec(shape, lambda n: tuple(0 for _ in shape))
    return pl.pallas_call(
        body, name="swa_bwd", grid=(nb,),
        in_specs=_swa_specs() + [pl.BlockSpec((SWA_ROWS, 512), lambda n: (n, 0)),
                                 pl.BlockSpec((SWA_ROWS, 512), lambda n: (n, 0)), full((WINDOW, 2 * WINDOW))],
        out_specs=[pl.BlockSpec((SWA_ROWS, 512), lambda n: (n, 0)), full((S, 128)), full((S, 128)),
                   full((NUM_BUCKETS * 8, 128)), full((8, 128))],
        out_shape=[jax.ShapeDtypeStruct((S, 512), F32), jax.ShapeDtypeStruct((S, 128), F32),
                   jax.ShapeDtypeStruct((S, 128), F32), jax.ShapeDtypeStruct((NUM_BUCKETS * 8, 128), F32),
                   jax.ShapeDtypeStruct((8, 128), F32)],
        scratch_shapes=[pltpu.VMEM((SWA_HEADS, WINDOW, 2 * WINDOW), F32)],
        compiler_params=_params(("arbitrary",)),
    )(proj, proj, proj, proj, proj, bias, sinks, o, do, bucket)


def _rope_tables(S):
    inv = np.float32(ROPE_THETA) ** (-np.arange(0, MLA_ROPE, 2, dtype=np.float32) / np.float32(MLA_ROPE))
    ang = np.arange(S, dtype=np.float32)[:, None] * inv[None, :]
    cos, sin = np.cos(ang), np.sin(ang)
    return (jnp.asarray(np.tile(np.concatenate([cos, cos], axis=1), (1, 2))),
            jnp.asarray(np.tile(np.concatenate([-sin, sin], axis=1), (1, 2))))


def _rope_wide(ref):
    t = ref[...]
    return jnp.concatenate([t, t], axis=1)


def _swap_halves(x):
    w = x.shape[-1]
    lane = lax.broadcasted_iota(jnp.int32, x.shape, x.ndim - 1)
    return jnp.where((lane % 64) < 32, pltpu.roll(x, w - 32, x.ndim - 1), pltpu.roll(x, 32, x.ndim - 1))


def _uq_group_rows(wuq_ref):
    per = MLA_NOPE + MLA_ROPE
    nope = [wuq_ref[per * h:per * h + MLA_NOPE, :] for h in range(MLA_HEADS)]
    rope = [wuq_ref[per * h + MLA_NOPE:per * (h + 1), :] for h in range(MLA_HEADS)]
    return jnp.concatenate(nope + rope, axis=0)


def _mla_pre_fwd(proj, qn_w, kvn_w, wuqT, wukv, cos, sin, *, tm=ROW_TILE):
    S = proj.shape[0]
    tm = min(tm, S)

    def body(ql_ref, kl_ref, kr_ref, qw_ref, kw_ref, wuq_ref, wukv_ref, cos_ref, sin_ref,
             qc_ref, kc_ref, vv_ref):
        ql = ql_ref[...]
        qn = (ql * _rstd(ql) * qw_ref[...]).astype(BF16)
        q = _dot_nt(qn, _uq_group_rows(wuq_ref))
        cs, sn = _rope_wide(cos_ref), _rope_wide(sin_ref)
        qr = q[:, 512:768]
        qr = qr * cs + _swap_halves(qr) * sn
        half = lax.broadcasted_iota(jnp.int32, (tm, 128), 1) // 64
        kl = kl_ref[...]
        kvn = (kl * _rstd(kl) * kw_ref[...]).astype(BF16)
        kr = kr_ref[...]
        kr = kr * cs[:, :128] + _swap_halves(kr) * sn[:, :128]
        kr2 = (kr + pltpu.roll(kr, 64, 1)).astype(BF16)
        kv = _dot(kvn, jnp.concatenate([wukv_ref[j] for j in range(2 * MLA_HEADS)], axis=1)).astype(BF16)
        for h in range(MLA_HEADS):
            qc_ref[h, :, 0:128] = q[:, 128 * h:128 * h + 128].astype(BF16)
            chunk = qr[:, 128 * (h // 2):128 * (h // 2) + 128]
            qc_ref[h, :, 128:256] = jnp.where(half == (h % 2), chunk, 0.0).astype(BF16)
            kc_ref[h, :, 0:128] = kv[:, 256 * h:256 * h + 128]
            kc_ref[h, :, 128:256] = kr2
            vv_ref[h] = kv[:, 256 * h + 128:256 * h + 256]

    const = lambda shape: pl.BlockSpec(shape, lambda i: tuple(0 for _ in shape))
    return pl.pallas_call(
        body, name="mla_pre_fwd", grid=(S // tm,),
        in_specs=[pl.BlockSpec((tm, 256), lambda i: (i, 3)), pl.BlockSpec((tm, 128), lambda i: (i, 8)),
                  pl.BlockSpec((tm, 128), lambda i: (i, 9)), const((1, 256)), const((1, 128)),
                  const((768, 256)), const((8, 128, 128)),
                  pl.BlockSpec((tm, 128), lambda i: (i, 0)), pl.BlockSpec((tm, 128), lambda i: (i, 0))],
        out_specs=[pl.BlockSpec((MLA_HEADS, tm, 256), lambda i: (0, i, 0)),
                   pl.BlockSpec((MLA_HEADS, tm, 256), lambda i: (0, i, 0)),
                   pl.BlockSpec((MLA_HEADS, tm, 128), lambda i: (0, i, 0))],
        out_shape=[jax.ShapeDtypeStruct((MLA_HEADS, S, 256), BF16), jax.ShapeDtypeStruct((MLA_HEADS, S, 256), BF16),
                   jax.ShapeDtypeStruct((MLA_HEADS, S, 128), BF16)],
        compiler_params=_params(("parallel",)),
    )(proj, proj, proj, qn_w, kvn_w, wuqT, wukv, cos, sin)


def _causal(i, j, t):
    row = i * t + lax.broadcasted_iota(jnp.int32, (t, t), 0)
    col = j * t + lax.broadcasted_iota(jnp.int32, (t, t), 1)
    return col <= row


def _mla_attn_fwd(qc, kc, vv, *, t=512):
    S = qc.shape[1]
    t = min(t, S)

    def body(q_ref, k_ref, v_ref, o_ref, l_ref):
        i = pl.program_id(0)
        diag = _causal(0, 0, t)

        def step(j, carry, masked):
            rows = pl.ds(pl.multiple_of(j * t, t), t)
            out = []
            for h in range(MLA_HEADS):
                m, l, acc = carry[h]
                s = _dot_nt(q_ref[h], k_ref[h, rows, :]) * MLA_SCALE
                if masked:
                    s = jnp.where(diag, s, -jnp.inf)
                m_new = jnp.maximum(m, jnp.max(s, axis=-1, keepdims=True))
                alpha = jnp.exp(m - m_new)
                p = jnp.exp(s - m_new)
                l = alpha * l + jnp.sum(p, axis=-1, keepdims=True)
                acc = alpha * acc + _dot(p.astype(BF16), v_ref[h, rows, :])
                out.append((m_new, l, acc))
            return tuple(out)

        init = tuple((jnp.full((t, 1), -jnp.inf, F32), jnp.zeros((t, 1), F32), jnp.zeros((t, MLA_V), F32))
                     for _ in range(MLA_HEADS))
        carry = lax.fori_loop(0, i, lambda j, c: step(j, c, False), init)
        carry = step(i, carry, True)
        for h in range(MLA_HEADS):
            m, l, acc = carry[h]
            o_ref[:, 128 * h:128 * h + 128] = acc / l
            l_ref[h] = jnp.broadcast_to(m + jnp.log(l), (t, 128))

    return pl.pallas_call(
        body, name="mla_attn_fwd", grid=(S // t,),
        in_specs=[pl.BlockSpec((MLA_HEADS, t, 256), lambda i: (0, i, 0)),
                  pl.BlockSpec((MLA_HEADS, S, 256), lambda i: (0, 0, 0)),
                  pl.BlockSpec((MLA_HEADS, S, 128), lambda i: (0, 0, 0))],
        out_specs=[pl.BlockSpec((t, 512), lambda i: (i, 0)),
                   pl.BlockSpec((MLA_HEADS, t, 128), lambda i: (0, i, 0))],
        out_shape=[jax.ShapeDtypeStruct((S, 512), F32), jax.ShapeDtypeStruct((MLA_HEADS, S, 128), F32)],
        compiler_params=_params(("parallel",)),
    )(qc, kc, vv)


def _mla_attn_bwd(qc, kc, vv, o, lse, do, *, t=512, tq=1024):
    S = qc.shape[1]
    t = min(t, S)
    tq = min(tq, S)
    nblk = S // t
    hp = MLA_HEADS
    once = pl.Buffered(1)

    def body(q_ref, k_ref, v_ref, o_ref, l_ref, do_ref, dq_ref, dk_ref, dv_ref):
        j = pl.program_id(1)

        @pl.when(j == 0)
        def _():
            dq_ref[...] = jnp.zeros_like(dq_ref)

        first = (j * t) // tq

        def step(i, carry, masked):
            rows = pl.ds(pl.multiple_of(i * tq, tq), tq)
            if masked:
                row = i * tq + lax.broadcasted_iota(jnp.int32, (tq, t), 0)
                col = j * t + lax.broadcasted_iota(jnp.int32, (tq, t), 1)
                visible = col <= row
            out = []
            for h in range(hp):
                dk, dv = carry[h]
                k = k_ref[h]
                q = q_ref[h, rows, :]
                dov = do_ref[rows, 128 * h:128 * h + 128]
                lrow = l_ref[h, rows, :][:, 0:1]
                p = jnp.exp(_dot_nt(q, k) * MLA_SCALE - lrow)
                if masked:
                    p = jnp.where(visible, p, 0.0)
                dob = dov.astype(BF16)
                dv = dv + _dot_tn(p.astype(BF16), dob)
                dp = _dot_nt(dob, v_ref[h])
                delta = jnp.sum(dov * o_ref[rows, 128 * h:128 * h + 128], axis=-1, keepdims=True)
                ds = (p * (dp - delta) * MLA_SCALE).astype(BF16)
                dk = dk + _dot_tn(ds, q)
                dq_ref[h, rows, :] += _dot(ds, k)
                out.append((dk, dv))
            return tuple(out)

        init = tuple((jnp.zeros((t, 256), F32), jnp.zeros((t, MLA_V), F32)) for _ in range(hp))
        carry = step(first, init, True)
        carry = lax.fori_loop(first + 1, S // tq, lambda i, c: step(i, c, False), carry)
        for h in range(hp):
            dk_ref[h] = carry[h][0]
            dv_ref[h] = carry[h][1]

    return pl.pallas_call(
        body, name="mla_attn_bwd", grid=(MLA_HEADS // hp, nblk),
        in_specs=[pl.BlockSpec((hp, S, 256), lambda g, j: (g, 0, 0), pipeline_mode=once),
                  pl.BlockSpec((hp, t, 256), lambda g, j: (g, j, 0)),
                  pl.BlockSpec((hp, t, 128), lambda g, j: (g, j, 0)),
                  pl.BlockSpec((S, 128 * hp), lambda g, j: (0, g), pipeline_mode=once),
                  pl.BlockSpec((hp, S, 128), lambda g, j: (g, 0, 0), pipeline_mode=once),
                  pl.BlockSpec((S, 128 * hp), lambda g, j: (0, g), pipeline_mode=once)],
        out_specs=[pl.BlockSpec((hp, S, 256), lambda g, j: (g, 0, 0)),
                   pl.BlockSpec((hp, t, 256), lambda g, j: (g, j, 0)),
                   pl.BlockSpec((hp, t, 128), lambda g, j: (g, j, 0))],
        out_shape=[jax.ShapeDtypeStruct((MLA_HEADS, S, 256), F32), jax.ShapeDtypeStruct((MLA_HEADS, S, 256), F32),
                   jax.ShapeDtypeStruct((MLA_HEADS, S, 128), F32)],
        compiler_params=_params(("parallel", "arbitrary")),
    )(qc, kc, vv, o, lse, do)


def _mla_pre_bwd(proj, qn_w, kvn_w, wuqT, wukv, cos, sin, dqc, dkc, dvv, *, tm=ROW_TILE):
    S = proj.shape[0]
    tm = min(tm, S)

    def body(ql_ref, kl_ref, qw_ref, kw_ref, wuq_ref, wukv_ref, cos_ref, sin_ref, dqc_ref, dkc_ref, dvv_ref,
             dql_ref, dkl_ref, dkr_ref, gq_ref, gkv_ref, part_ref, gq_acc, gkv_acc):
        @pl.when(pl.program_id(0) == 0)
        def _():
            gq_acc[...] = jnp.zeros_like(gq_acc)
            gkv_acc[...] = jnp.zeros_like(gkv_acc)
            part_ref[...] = jnp.zeros_like(part_ref)

        cs, sn = _rope_wide(cos_ref), _rope_wide(sin_ref)
        half = lax.broadcasted_iota(jnp.int32, (tm, 128), 1) // 64
        ql = ql_ref[...]
        rq = _rstd(ql)
        qhat = ql * rq
        qw = qw_ref[...]
        qn = (qhat * qw).astype(BF16)
        chunks = []
        for pair in range(2):
            chunks.append(jnp.where(half == 0, dqc_ref[2 * pair, :, 128:256], dqc_ref[2 * pair + 1, :, 128:256]))
        dqr = jnp.concatenate(chunks, axis=1)
        dqr = dqr * cs + _swap_halves(dqr * sn)
        dq = jnp.concatenate([dqc_ref[h, :, 0:128] for h in range(MLA_HEADS)] + [dqr], axis=1).astype(BF16)
        gq_acc[...] += _dot_tn(dq, qn)
        dqn = _dot(dq, _uq_group_rows(wuq_ref))
        part_ref[0:1, :] += jnp.sum(dqn * qhat, axis=0, keepdims=True)
        dql_ref[...] = _rms_bwd(dqn * qw, qhat, rq)
        kl = kl_ref[...]
        rk = _rstd(kl)
        khat = kl * rk
        kw = kw_ref[...]
        kvn = (khat * kw).astype(BF16)
        dkvn = jnp.zeros((tm, MLA_KVR), F32)
        dkr2 = jnp.zeros((tm, 128), F32)
        for h in range(MLA_HEADS):
            dkn = dkc_ref[h, :, 0:128].astype(BF16)
            dvh = dvv_ref[h].astype(BF16)
            gkv_acc[2 * h] += _dot_tn(kvn, dkn)
            gkv_acc[2 * h + 1] += _dot_tn(kvn, dvh)
            dkvn += _dot_nt(dkn, wukv_ref[2 * h]) + _dot_nt(dvh, wukv_ref[2 * h + 1])
            dkr2 += dkc_ref[h, :, 128:256]
        part_ref[1:2, 0:128] += jnp.sum(dkvn * khat, axis=0, keepdims=True)
        dkl_ref[...] = _rms_bwd(dkvn * kw, khat, rk)
        dkr = jnp.where(half == 0, dkr2 + pltpu.roll(dkr2, 64, 1), 0.0)
        dkr_ref[...] = dkr * cs[:, :128] + _swap_halves(dkr * sn[:, :128])

        @pl.when(pl.program_id(0) == S // tm - 1)
        def _():
            gkv_ref[...] = gkv_acc[...].astype(BF16)
            per = MLA_NOPE + MLA_ROPE
            for h in range(MLA_HEADS):
                gq_ref[per * h:per * h + MLA_NOPE, :] = gq_acc[MLA_NOPE * h:MLA_NOPE * (h + 1), :].astype(BF16)
                gq_ref[per * h + MLA_NOPE:per * (h + 1), :] = gq_acc[512 + MLA_ROPE * h:512 + MLA_ROPE * (h + 1), :].astype(BF16)

    const = lambda shape: pl.BlockSpec(shape, lambda i: tuple(0 for _ in shape))
    heads = lambda w: pl.BlockSpec((MLA_HEADS, tm, w), lambda i: (0, i, 0))
    return pl.pallas_call(
        body, name="mla_pre_bwd", grid=(S // tm,),
        in_specs=[pl.BlockSpec((tm, 256), lambda i: (i, 3)), pl.BlockSpec((tm, 128), lambda i: (i, 8)),
                  const((1, 256)), const((1, 128)), const((768, 256)), const((8, 128, 128)),
                  pl.BlockSpec((tm, 128), lambda i: (i, 0)), pl.BlockSpec((tm, 128), lambda i: (i, 0)),
                  heads(256), heads(256), heads(128)],
        out_specs=[pl.BlockSpec((tm, 256), lambda i: (i, 0)), pl.BlockSpec((tm, 128), lambda i: (i, 0)),
                   pl.BlockSpec((tm, 128), lambda i: (i, 0)), const((768, 256)), const((8, 128, 128)), const((8, 256))],
        out_shape=[jax.ShapeDtypeStruct((S, 256), F32), jax.ShapeDtypeStruct((S, 128), F32),
                   jax.ShapeDtypeStruct((S, 128), F32), jax.ShapeDtypeStruct((768, 256), BF16),
                   jax.ShapeDtypeStruct((8, 128, 128), BF16), jax.ShapeDtypeStruct((8, 256), F32)],
        scratch_shapes=[pltpu.VMEM((768, 256), F32), pltpu.VMEM((8, 128, 128), F32)],
        compiler_params=_params(("arbitrary",)),
    )(proj, proj, qn_w, kvn_w, wuqT, wukv, cos, sin, dqc, dkc, dvv)


def _mix_out_fwd(x, oa, ob, w_o, vecs, *, tm=ROW_TILE):
    S = x.shape[0]
    tm = min(tm, S)

    def body(x_ref, oa_ref, ob_ref, w_ref, vec_ref, xo_ref, mo_ref):
        mo = _dot(oa_ref[...].astype(BF16), w_ref[0:512, :]) + _dot(ob_ref[...].astype(BF16), w_ref[512:1024, :])
        mo_ref[...] = mo
        xo_ref[...] = x_ref[...] + vec_ref[3:4, :] * mo

    row = pl.BlockSpec((tm, D), lambda i: (i, 0))
    half = pl.BlockSpec((tm, 512), lambda i: (i, 0))
    return pl.pallas_call(
        body, name="mix_out_fwd", grid=(S // tm,),
        in_specs=[row, half, half, pl.BlockSpec((D, D), lambda i: (0, 0)), pl.BlockSpec((8, D), lambda i: (0, 0))],
        out_specs=[row, row],
        out_shape=[jax.ShapeDtypeStruct((S, D), F32), jax.ShapeDtypeStruct((S, D), F32)],
        compiler_params=_params(("parallel",)),
    )(x, oa, ob, w_o, vecs)


def _mix_out_bwd(dxo, mo, oa, ob, w_o, vecs, *, tm=ROW_TILE):
    S = dxo.shape[0]
    tm = min(tm, S)

    def body(dx_ref, mo_ref, oa_ref, ob_ref, w_ref, vec_ref, doa_ref, dob_ref, gw_ref, part_ref, gw_acc):
        @pl.when(pl.program_id(0) == 0)
        def _():
            gw_acc[...] = jnp.zeros_like(gw_acc)
            part_ref[...] = jnp.zeros_like(part_ref)

        dx = dx_ref[...]
        part_ref[0:1, :] += jnp.sum(dx * mo_ref[...], axis=0, keepdims=True)
        dmo = (vec_ref[3:4, :] * dx).astype(BF16)
        doa_ref[...] = _dot_nt(dmo, w_ref[0:512, :])
        dob_ref[...] = _dot_nt(dmo, w_ref[512:1024, :])
        gw_acc[0:512, :] += _dot_tn(oa_ref[...].astype(BF16), dmo)
        gw_acc[512:1024, :] += _dot_tn(ob_ref[...].astype(BF16), dmo)

        @pl.when(pl.program_id(0) == S // tm - 1)
        def _():
            gw_ref[...] = gw_acc[...].astype(BF16)

    row = pl.BlockSpec((tm, D), lambda i: (i, 0))
    half = pl.BlockSpec((tm, 512), lambda i: (i, 0))
    return pl.pallas_call(
        body, name="mix_out_bwd", grid=(S // tm,),
        in_specs=[row, row, half, half, pl.BlockSpec((D, D), lambda i: (0, 0)), pl.BlockSpec((8, D), lambda i: (0, 0))],
        out_specs=[half, half, pl.BlockSpec((D, D), lambda i: (0, 0)), pl.BlockSpec((8, D), lambda i: (0, 0))],
        out_shape=[jax.ShapeDtypeStruct((S, 512), F32), jax.ShapeDtypeStruct((S, 512), F32),
                   jax.ShapeDtypeStruct((D, D), BF16), jax.ShapeDtypeStruct((8, D), F32)],
        scratch_shapes=[pltpu.VMEM((D, D), F32)],
        compiler_params=_params(("arbitrary",)),
    )(dxo, mo, oa, ob, w_o, vecs)


def _mix_in_bwd(h, w_inT, dq, dk, dv, dql, dkl, dkr, *, tm=ROW_TILE):
    S = h.shape[0]
    tm = min(tm, S)
    wid = (512, 128, 128, 256, 128, 128)

    def body(h_ref, w_ref, dq_ref, dk_ref, dv_ref, dql_ref, dkl_ref, dkr_ref, dh_ref, gw_ref):
        @pl.when(pl.program_id(0) == 0)
        def _():
            gw_ref[...] = jnp.zeros_like(gw_ref)

        parts = (dq_ref, dk_ref, dv_ref, dql_ref, dkl_ref, dkr_ref)
        dproj = jnp.concatenate([ref[...].astype(BF16) for ref in parts], axis=1)
        dh_ref[...] = _dot(dproj, w_ref[...])
        gw_ref[...] += _dot_tn(dproj, h_ref[...])[0:D_IN, :]

    row = pl.BlockSpec((tm, D), lambda i: (i, 0))
    part = lambda w: pl.BlockSpec((tm, w), lambda i: (i, 0))
    return pl.pallas_call(
        body, name="mix_in_bwd", grid=(S // tm,),
        in_specs=[row, pl.BlockSpec((D_IN_PAD, D), lambda i: (0, 0))] + [part(w) for w in wid],
        out_specs=[row, pl.BlockSpec((D_IN, D), lambda i: (0, 0))],
        out_shape=[jax.ShapeDtypeStruct((S, D), F32), jax.ShapeDtypeStruct((D_IN, D), F32)],
        compiler_params=_params(("arbitrary",)),
    )(h, w_inT, dq, dk, dv, dql, dkl, dkr)


def _vecs(norm_w, mod9, k):
    return jnp.concatenate([norm_w.reshape(1, D), mod9[3 * k:3 * k + 3], jnp.zeros((4, D), F32)], axis=0)


def _local_step(x, tgt, mod9, norms, sinks, rel_bias, q_norm, kv_norm, W, on_grads=None):
    if on_grads is None:
        on_grads = lambda group, grads, after, vecs: vecs
    S = x.shape[0]
    v1 = _vecs(norms["ffn1"], mod9, 0)
    v2 = _vecs(norms["mix"], mod9, 1)
    v3 = _vecs(norms["ffn2"], mod9, 2)
    bucket = jnp.asarray(_bucket_table())
    cos, sin = _rope_tables(S)
    if isinstance(W, dict):
        full, W = W, (lambda group, after, vecs: (full, vecs))

    W1, v1 = W("ffn1", [], v1)
    x1, h1, a1, b1, f1 = _ffn_fwd(x, v1, W1["g1T"], W1["u1T"], W1["d1"], name="ffn1_fwd")
    W2, v2 = W("mixer", [x1], v2)
    wuqT = W2["w_uqT"]
    h2, proj, w_inT = _mix_in_fwd(x1, v2, W2["w_inT"])
    oa, bias = _swa_fwd(proj, rel_bias, bucket, sinks)
    qc, kc, vv = _mla_pre_fwd(proj, q_norm, kv_norm, wuqT, W2["w_ukv"], cos, sin)
    ob, lse = _mla_attn_fwd(qc, kc, vv)
    _, v2o = W("ffn2_on_its_way", [oa, ob], v2)
    x2, mo = _mix_out_fwd(x1, oa, ob, W2["w_o"], v2o)
    W3, v3 = W("ffn2", [x2], v3)
    x3, h3, a3, b3, f3 = _ffn_fwd(x2, v3, W3["g3T"], W3["u3T"], W3["d3"], name="ffn2_fwd")
    dx3, head_part, df3 = _head(x3, tgt, norms["final"], f3, v3)

    gg3, gu3, gd3, dh3 = _ffn_bwd_main(h3, df3, a3, b3, W3["g3T"], W3["u3T"], W3["d3"], name="ffn2_bwd")
    ffn2 = {"g3T": gg3, "u3T": gu3, "d3": gd3}
    v3 = on_grads("ffn2", ffn2, [], v3)
    dx2, n3_part = _norm_bwd(dh3, x2, dx3, v3, name="ffn2_norm_bwd")
    v2 = on_grads("ffn2", None, [dx2], v2)
    doa, dob, g_wo, g2_part = _mix_out_bwd(dx2, mo, oa, ob, W2["w_o"], v2)
    dq, dk, dv, drb, dsk = _swa_bwd(proj, bias, sinks, oa, doa, bucket)
    dqc, dkc, dvv = _mla_attn_bwd(qc, kc, vv, ob, lse, dob)
    dql, dkl, dkr, g_uq, g_ukv, mla_part = _mla_pre_bwd(proj, q_norm, kv_norm, wuqT, W2["w_ukv"], cos, sin, dqc, dkc, dvv)
    dh2, g_win = _mix_in_bwd(h2, w_inT, dq, dk, dv, dql, dkl, dkr)
    mixer = {"w_inT": g_win, "w_uqT": g_uq, "w_ukv": g_ukv, "w_o": g_wo}
    v2 = on_grads("mixer", mixer, [], v2)
    dx1, n2_part, df1 = _norm_bwd(dh2, x1, dx2, v2, name="mix_norm_bwd", below=(f1, v1))
    started = on_grads("mixer", None, [dx1], jnp.zeros((1, 1), F32))
    gg1, gu1, gd1, dh1 = _ffn_bwd_main(h1, df1, a1, b1, W1["g1T"], W1["u1T"], W1["d1"], name="ffn1_bwd",
                                       after=[started])
    ffn1 = {"g1T": gg1, "u1T": gu1, "d1": gd1}
    v1 = on_grads("ffn1", ffn1, [], v1)
    dx0, n1_part = _norm_bwd(dh1, x, dx1, v1, name="ffn1_norm_bwd")

    grads = {**ffn1, **ffn2, **mixer}
    return head_part[1, 0], dx0, grads, _pack_vec(n1_part, n2_part, n3_part, head_part, g2_part, mla_part, dsk, drb)


SMALL_LAYOUT = (("norm_ffn1", 1024), ("norm_mix", 1024), ("norm_ffn2", 1024), ("norm_final", 1024),
                ("q_norm", 256), ("kv_norm", 128), ("sinks", 128), ("rel_bias", 256))
N_SMALL = sum(n for _, n in SMALL_LAYOUT)
LOSS_SLOT = 4 * 1024 + 256 + 128 + SWA_HEADS
N_MODVEC = N_MOD * D
N_VEC = N_MODVEC + N_SMALL


def _pack_vec(n1, n2, n3, head, g2, mla, dsk, drb):
    def body(n1_ref, n2_ref, n3_ref, head_ref, g2_ref, mla_ref, dsk_ref, drb_ref, out_ref):
        rows = [n1_ref[1:2, :], n1_ref[2:3, :], n2_ref[3:4, :], n2_ref[1:2, :], n2_ref[2:3, :], g2_ref[0:1, :],
                n3_ref[1:2, :], n3_ref[2:3, :], head_ref[3:4, :],
                n1_ref[0:1, :], n2_ref[0:1, :], n3_ref[0:1, :], head_ref[0:1, :]]
        for i, row in enumerate(rows):
            out_ref[:, D * i:D * (i + 1)] = row
        off = D * len(rows)
        out_ref[:, off:off + 256] = mla_ref[0:1, :]
        out_ref[:, off + 256:off + 384] = mla_ref[1:2, 0:128]

        def diagonal(block):
            r = lax.broadcasted_iota(jnp.int32, block.shape, 0)
            lane = lax.broadcasted_iota(jnp.int32, block.shape, 1)
            return jnp.sum(jnp.where(r == lane, block, 0.0), axis=0, keepdims=True)

        lane = lax.broadcasted_iota(jnp.int32, (1, 128), 1)
        out_ref[:, off + 384:off + 512] = jnp.where(lane == SWA_HEADS, head_ref[1:2, 0:128], diagonal(dsk_ref[...]))
        out_ref[:, off + 512:off + 640] = diagonal(drb_ref[0:128, :])
        out_ref[:, off + 640:off + 768] = diagonal(drb_ref[128:256, :])

    vm = pl.BlockSpec(memory_space=pltpu.VMEM)
    return pl.pallas_call(body, name="pack_vec", in_specs=[vm] * 8, out_specs=vm,
                          out_shape=jax.ShapeDtypeStruct((1, N_VEC), F32))(n1, n2, n3, head, g2, mla, dsk, drb)


def _coords():
    return lax.axis_index("x"), lax.axis_index("y"), lax.axis_index("c")


def _flip(v, bit):
    return 1 - v if bit else v


def _peer(r):
    x, y, c = _coords()
    return (_flip(x, r & 4), _flip(y, r & 2), _flip(c, r & 1))


class _ModExchange:
    def __init__(self, c_ref, w_ref, b_ref, mod_ref, ca_ref, call_ref, part_ref, send_sems, recv_sems):
        self.refs = (c_ref, w_ref, b_ref, mod_ref, ca_ref, call_ref, part_ref)
        self.sems = (send_sems, recv_sems)
        x, y, c = _coords()
        self.me = 4 * x + 2 * y + c
        self.sends = []

    def _copy(self, phase, r):
        c_ref, _, _, mod_ref, _, call_ref, part_ref = self.refs
        src, dst = ((call_ref.at[self.me], call_ref.at[self.me]) if phase == 0 else
                    (part_ref.at[self.me ^ r], mod_ref.at[self.me]))
        return pltpu.make_async_remote_copy(src, dst, self.sems[0].at[phase, r], self.sems[1].at[phase, r],
                                            device_id=_peer(r), device_id_type=MESH)

    def _start(self, phase):
        for r in range(1, N_DEV):
            self.sends.append(self._copy(phase, r))
            self.sends[-1].start()

    def begin(self):
        c_ref, _, _, _, _, call_ref, _ = self.refs
        call_ref[self.me] = jnp.concatenate([c_ref[...], jnp.zeros((7, D), F32)], axis=0)
        self._start(0)

    def middle(self):
        _, w_ref, b_ref, mod_ref, ca_ref, call_ref, part_ref = self.refs
        for r in range(1, N_DEV):
            self._copy(0, r).wait_recv()
        cv = call_ref[...].reshape(8 * N_DEV, D)
        ca = (cv * _sigmoid(cv)).astype(BF16)
        ca_ref[...] = ca
        part_ref[...] = _dot(ca, w_ref[...].astype(BF16)).reshape(part_ref.shape)
        mod_ref[self.me] = part_ref[self.me] + b_ref[self.me]
        self._start(1)

    def end(self):
        _, _, b_ref, mod_ref, _, _, _ = self.refs
        for r in range(1, N_DEV):
            self._copy(1, r).wait_recv()
            mod_ref[self.me ^ r] = mod_ref[self.me ^ r] + b_ref[self.me ^ r]
        for cp in self.sends:
            cp.wait_send()


def _mod_bwd(allvec, ca, me_idx):
    W = N_MODVEC // N_DEV

    def body(me_ref, all_ref, cols_ref, ca_ref, gw_ref, sum_ref):
        in_first_row = lax.broadcasted_iota(jnp.int32, (N_DEV, 8, W), 1) == 0
        dm = jnp.where(in_first_row, cols_ref[...], 0.0).reshape(8 * N_DEV, W)
        gw_ref[...] = _dot_tn(ca_ref[...], dm.astype(BF16))
        total = all_ref[0]
        for k in range(1, N_DEV):
            total = total + all_ref[k]
        sum_ref[...] = total

    return pl.pallas_call(
        body, name="mod_bwd",
        grid_spec=pltpu.PrefetchScalarGridSpec(
            num_scalar_prefetch=1, grid=(1,),
            in_specs=[pl.BlockSpec((N_DEV, 1, N_VEC), lambda i, me: (0, 0, 0)),
                      pl.BlockSpec((N_DEV, 1, W), lambda i, me: (0, 0, me[0])),
                      pl.BlockSpec((8 * N_DEV, D), lambda i, me: (0, 0))],
            out_specs=[pl.BlockSpec((D, W), lambda i, me: (0, 0)), pl.BlockSpec((1, N_VEC), lambda i, me: (0, 0))]),
        out_shape=[jax.ShapeDtypeStruct((D, W), F32), jax.ShapeDtypeStruct((1, N_VEC), F32)],
        compiler_params=_params(("arbitrary",)),
    )(me_idx, allvec, allvec, ca)


def _wgather(shards, later, c_row, w_mod, b_shards):
    n, nl = len(shards), len(later)
    rows = [s.shape[0] for s in shards]
    W = w_mod.shape[1]
    cast = [(s.shape, True) for s in shards] + [(s.shape, s.dtype != dt) for s, dt in later]
    staging = [pltpu.VMEM(shape, dt) for shape, is_cast in cast if is_cast for dt in (F32, BF16)]

    def body(*refs):
        ins, (c_ref, w_ref, b_ref), raw = refs[:n], refs[n:n + 3], refs[n + 3:n + 3 + nl]
        o = n + 3 + 2 * nl
        outs, (mod_ref, ca_ref), lands = refs[o:o + n], refs[o + n:o + n + 2], refs[o + n + 2:o + n + 2 + nl]
        o += n + 2 + nl
        send_sems, recv_sems, local_sems, load_sems, store_sems = refs[o:o + 5]
        mod = _ModExchange(c_ref, w_ref, b_ref, mod_ref, ca_ref, *refs[o + 5:o + 9])
        stage = iter(refs[o + 9:])
        staged = [(next(stage), next(stage)) if is_cast else None for _, is_cast in cast]
        mod.begin()
        x, y, c = _coords()
        me = 4 * x + 2 * y + c
        sib, xn, yn = (x, y, 1 - c), (1 - x, y, c), (x, 1 - y, c)
        block = lambda px, py, pc: 4 * px + 2 * py + pc

        def part(k, blk, half):
            if half is None:
                return outs[k].at[blk]
            return outs[k].at[blk, pl.ds(half * (rows[k] // 2), rows[k] // 2)]

        def copy(k, slot, blk, to, half=None, src=None):
            ref = part(k, blk, half)
            return pltpu.make_async_remote_copy(
                src_ref=ref if src is None else src, dst_ref=ref, send_sem=send_sems.at[k, slot],
                recv_sem=recv_sems.at[k, slot], device_id=to, device_id_type=MESH)

        def in_bf16(sources, first):
            loads = [pltpu.make_async_copy(src, staged[first + i][0], load_sems.at[first + i])
                     if staged[first + i] else None for i, src in enumerate(sources)]
            for cp in loads:
                if cp is not None:
                    cp.start()
            for i, src in enumerate(sources):
                if loads[i] is None:
                    yield i, src
                    continue
                loads[i].wait()
                wide, narrow = staged[first + i]
                narrow[...] = wide[...].astype(BF16)
                yield i, narrow

        local, sent = [], []
        for k, ref in in_bf16(ins, 0):
            local.append(pltpu.make_async_copy(ref, outs[k].at[me], local_sems.at[k]))
            sent += [copy(k, slot, me, to, src=ref) for slot, to in ((0, sib), (1, xn), (2, yn))]
            for cp in [local[-1]] + sent[-3:]:
                cp.start()
        mod.middle()
        for k, ref in in_bf16(raw, n):
            local.append(pltpu.make_async_copy(ref, lands[k].at[me], store_sems.at[k]))
            local[-1].start()
        bx, by, bd = block(1 - x, y, c), block(x, 1 - y, c), block(1 - x, 1 - y, c)
        for k in range(n):
            copy(k, 1, bx, sib).wait_recv()
            sent += [copy(k, 4, bx, yn, half=1), copy(k, 5, bx, sib)]
            sent[-2].start()
            sent[-1].start()
        for k in range(n):
            copy(k, 2, by, sib).wait_recv()
            sent += [copy(k, 3, by, xn, half=0), copy(k, 6, by, sib)]
            sent[-2].start()
            sent[-1].start()
        for k in range(n):
            copy(k, 3, bd, sib, half=0).wait_recv()
            copy(k, 4, bd, sib, half=1).wait_recv()
            sent.append(copy(k, 7, bd, sib))
            sent[-1].start()
        for k in range(n):
            copy(k, 0, block(x, y, 1 - c), sib).wait_recv()
            for slot, blk in ((5, block(1 - x, y, 1 - c)), (6, block(x, 1 - y, 1 - c)), (7, block(1 - x, 1 - y, 1 - c))):
                copy(k, slot, blk, sib).wait_recv()
        for cp in sent:
            cp.wait_send()
        for cp in local:
            cp.wait()
        mod.end()

    anyspec, vm = pl.BlockSpec(memory_space=pl.ANY), pl.BlockSpec(memory_space=pltpu.VMEM)
    zones = [lax.empty((N_DEV,) + s.shape, dt) for s, dt in later]
    out = pl.pallas_call(
        body, name="wgather", in_specs=[anyspec] * n + [vm] * 3 + [anyspec] * (2 * nl),
        out_specs=[anyspec] * n + [vm] * 2 + [anyspec] * nl,
        out_shape=[jax.ShapeDtypeStruct((N_DEV,) + s.shape, BF16) for s in shards]
        + [jax.ShapeDtypeStruct((N_DEV, 8, W), F32), jax.ShapeDtypeStruct((8 * N_DEV, D), BF16)]
        + [jax.ShapeDtypeStruct(z.shape, z.dtype) for z in zones],
        input_output_aliases={n + 3 + nl + i: n + 2 + i for i in range(nl)},
        scratch_shapes=[pltpu.SemaphoreType.DMA((n, 8)), pltpu.SemaphoreType.DMA((n, 8)),
                        pltpu.SemaphoreType.DMA((n,)), pltpu.SemaphoreType.DMA((n + nl,)),
                        pltpu.SemaphoreType.DMA((nl,)),
                        pltpu.VMEM((N_DEV, 8, D), F32), pltpu.VMEM((N_DEV, 8, W), F32),
                        pltpu.SemaphoreType.DMA((2, N_DEV)), pltpu.SemaphoreType.DMA((2, N_DEV))] + staging,
        compiler_params=_params(),
    )(*shards, c_row, w_mod, b_shards, *[s for s, _ in later], *zones)
    return out[:n], out[n], out[n + 1], list(out[n + 2:])


class _GatherCopies:
    def __init__(self, lands, send_sems, recv_sems, k0=0, batches=None):
        x, y, c = _coords()
        me = 4 * x + 2 * y + c
        sib = (x, y, 1 - c)
        chips = [(1 - x, y), (x, 1 - y), (1 - x, 1 - y)]

        def copy(k, slot, block, to):
            return pltpu.make_async_remote_copy(
                src_ref=lands[k].at[block], dst_ref=lands[k].at[block],
                send_sem=send_sems.at[7 * (k0 + k) + slot], recv_sem=recv_sems.at[7 * (k0 + k) + slot],
                device_id=to, device_id_type=MESH)

        n = len(lands)
        self.first = [copy(k, 0, me, sib) for k in range(n)]
        for batch in batches or [range(n)]:
            self.first += [copy(k, 1 + j, me, (cx, cy, c)) for j, (cx, cy) in enumerate(chips) for k in batch]
        self.landed = [copy(k, 1 + j, 4 * cx + 2 * cy + c, sib) for j, (cx, cy) in enumerate(chips) for k in range(n)]
        self.passed = [copy(k, 4 + j, 4 * cx + 2 * cy + c, sib) for j, (cx, cy) in enumerate(chips) for k in range(n)]
        self.from_sib = [copy(k, 0, 4 * x + 2 * y + (1 - c), sib) for k in range(n)]
        self.from_sib += [copy(k, 4 + j, 4 * cx + 2 * cy + (1 - c), sib) for j, (cx, cy) in enumerate(chips)
                          for k in range(n)]


def _token(carry):
    return [] if carry is None else [carry], jax.ShapeDtypeStruct((8, 128), F32) if carry is None else carry


def _gather_start(lands, *, name, batches=None, carry=None):
    n = len(lands)
    carried, token = _token(carry)

    def body(*refs):
        n_in = n + len(carried)
        for cp in _GatherCopies(refs[:n], refs[n_in], refs[n_in + 1], batches=batches).first:
            cp.start()
        refs[-1][...] = refs[n][...] if carried else jnp.zeros_like(refs[-1])

    vm = pl.BlockSpec(memory_space=pltpu.VMEM)
    out = pl.pallas_call(
        body, name=name,
        out_shape=(pltpu.SemaphoreType.DMA((7 * n,)), pltpu.SemaphoreType.DMA((7 * n,)),
                   *[pltpu.HBM(l.shape, l.dtype) for l in lands], jax.ShapeDtypeStruct(token.shape, token.dtype)),
        in_specs=[HBM_SPEC] * n + [vm] * len(carried),
        out_specs=(SEM_SPEC, SEM_SPEC, *[HBM_SPEC] * n, vm),
        input_output_aliases={i: 2 + i for i in range(n)},
        compiler_params=pltpu.CompilerParams(has_side_effects=DATAFLOW),
    )(*[_in_hbm(l) for l in lands], *carried)
    return out[0], out[1], list(out[2:2 + n]), out[-1]


def _gather_pass(send_sems, recv_sems, lands, after, *, name, stage, k0=0, carry=None):
    n = len(lands)
    carried, token = _token(carry)

    def body(*refs):
        cps = _GatherCopies(refs[:n], refs[n], refs[n + 1], k0)
        if stage == "landed":
            for cp in cps.landed:
                cp.wait_recv()
        else:
            for cp in cps.passed:
                cp.start()
        refs[-1][...] = refs[n + 2 + len(after)][...] if carried else jnp.zeros_like(refs[-1])

    vm = pl.BlockSpec(memory_space=pltpu.VMEM)
    out = pl.pallas_call(
        body, name=name,
        out_shape=(*[pltpu.HBM(l.shape, l.dtype) for l in lands], jax.ShapeDtypeStruct(token.shape, token.dtype)),
        in_specs=[HBM_SPEC] * n + [SEM_SPEC, SEM_SPEC] + [pl.BlockSpec(memory_space=pl.ANY)] * len(after)
        + [vm] * len(carried),
        out_specs=(*[HBM_SPEC] * n, vm),
        input_output_aliases={i: i for i in range(n)},
        compiler_params=pltpu.CompilerParams(has_side_effects=DATAFLOW),
    )(*lands, send_sems, recv_sems, *after, *carried)
    return list(out[:n]), out[-1]


def _gather_end(send_sems, recv_sems, lands, after, *, name, k0=0):
    n = len(lands)

    def body(*refs):
        cps = _GatherCopies(refs[:n], refs[n], refs[n + 1], k0)
        for cp in cps.from_sib:
            cp.wait_recv()
        for cp in cps.first + cps.passed:
            cp.wait_send()

    out = pl.pallas_call(
        body, name=name,
        out_shape=[pltpu.HBM(l.shape, l.dtype) for l in lands],
        in_specs=[HBM_SPEC] * n + [SEM_SPEC, SEM_SPEC] + [pl.BlockSpec(memory_space=pl.ANY)] * len(after),
        out_specs=[HBM_SPEC] * n,
        input_output_aliases={i: i for i in range(n)},
        compiler_params=pltpu.CompilerParams(has_side_effects=DATAFLOW),
    )(*lands, send_sems, recv_sems, *after)
    return list(out)


def _d2d_copies(grads, lands, send_sems, recv_sems):
    x, y, c = _coords()
    return [pltpu.make_async_remote_copy(
        src_ref=grads[k].at[2 * q + (1 - c)], dst_ref=lands[k].at[q],
        send_sem=send_sems.at[4 * k + q], recv_sem=recv_sems.at[4 * k + q],
        device_id=(x, y, 1 - c), device_id_type=MESH) for k in range(len(grads)) for q in range(4)]


def _direct_copies(grads, lands, send_sems, recv_sems):
    x, y, c = _coords()
    me = 4 * x + 2 * y + c
    return [pltpu.make_async_remote_copy(
        src_ref=grads[k].at[me ^ r], dst_ref=lands[k].at[r - 1],
        send_sem=send_sems.at[7 * k + r - 1], recv_sem=recv_sems.at[7 * k + r - 1],
        device_id=_peer(r), device_id_type=MESH) for k in range(len(grads)) for r in range(1, N_DEV)]


def _vec_copies(srcs, lands, send_sems, recv_sems):
    x, y, c = _coords()
    me = 4 * x + 2 * y + c
    return [pltpu.make_async_remote_copy(
        src_ref=lands[0].at[me], dst_ref=lands[0].at[me], send_sem=send_sems.at[r - 1], recv_sem=recv_sems.at[r - 1],
        device_id=_peer(r), device_id_type=MESH) for r in range(1, N_DEV)]


def _chipsum(gs, sibs, cidx, *, name):
    n = len(gs)

    def body(c_ref, *refs):
        for k in range(n):
            refs[2 * n + k][...] = (refs[k][...].astype(F32) + refs[n + k][...].astype(F32)).astype(refs[2 * n + k].dtype)

    mine = [pl.BlockSpec((1,) + g.shape[1:], lambda q, c_ref: (2 * q + c_ref[0], 0, 0)) for g in gs]
    other = [pl.BlockSpec((1,) + g.shape[1:], lambda q, c_ref: (q, 0, 0)) for g in gs]
    return pl.pallas_call(
        body, name=name,
        grid_spec=pltpu.PrefetchScalarGridSpec(num_scalar_prefetch=1, grid=(4,), in_specs=mine + other, out_specs=other),
        out_shape=[jax.ShapeDtypeStruct((4,) + g.shape[1:], g.dtype) for g in gs],
        compiler_params=_params(("arbitrary",)),
    )(cidx, *gs, *sibs)


HBM_SPEC = pl.BlockSpec(memory_space=pltpu.HBM)
SEM_SPEC = pl.BlockSpec(memory_space=pltpu.SEMAPHORE)
DATAFLOW = pltpu.SideEffectType.DATAFLOW_SIDE_EFFECTING


def _in_hbm(a):
    return pltpu.with_memory_space_constraint(a, pltpu.HBM)


def _rs_step1_copies(sums, lands, send_sems, recv_sems):
    n = len(sums)
    direct, relay = lands[:n], lands[n:]
    x, y, c = _coords()
    xn, yn = (1 - x, y, c), (x, 1 - y, c)
    qx, qy, qd = 2 * (1 - x) + y, 2 * x + (1 - y), 2 * (1 - x) + (1 - y)
    cps = []
    for k in range(n):
        h = sums[k].shape[1] // 2
        a, b = pl.ds(0, h), pl.ds(h, h)
        moves = ((sums[k].at[qx, a], direct[k].at[0], xn), (sums[k].at[qy, b], direct[k].at[1], yn),
                 (sums[k].at[qd, a], relay[k].at[0], xn), (sums[k].at[qd, b], relay[k].at[1], yn))
        for s, (src, dst, to) in enumerate(moves):
            cps.append(pltpu.make_async_remote_copy(
                src_ref=src, dst_ref=dst, send_sem=send_sems.at[4 * k + s], recv_sem=recv_sems.at[4 * k + s],
                device_id=to, device_id_type=MESH))
    return cps


def _rs_step2_copies(relayed, lands, send_sems, recv_sems, k0=0):
    x, y, c = _coords()
    cps = []
    for k in range(len(relayed)):
        for s, to in enumerate(((1 - x, y, c), (x, 1 - y, c))):
            cps.append(pltpu.make_async_remote_copy(
                src_ref=relayed[k].at[s], dst_ref=lands[k].at[s], send_sem=send_sems.at[2 * (k0 + k) + s],
                recv_sem=recv_sems.at[2 * (k0 + k) + s], device_id=to, device_id_type=MESH))
    return cps


def _relay_sum(sums, relay, qxy, *, name):
    n = len(sums)

    def body(q_ref, *refs):
        for k in range(n):
            refs[2 * n + k][...] = (refs[k][...].astype(F32) + refs[n + k][...].astype(F32)).astype(refs[2 * n + k].dtype)

    half = lambda s: (1, s.shape[1] // 2) + s.shape[2:]
    return pl.pallas_call(
        body, name=name,
        grid_spec=pltpu.PrefetchScalarGridSpec(
            num_scalar_prefetch=1, grid=(2,),
            in_specs=[pl.BlockSpec(half(s), lambda t, q_ref: (q_ref[t], 1 - t, 0)) for s in sums]
            + [pl.BlockSpec(half(s), lambda t, q_ref: (1 - t, 0, 0)) for s in sums],
            out_specs=[pl.BlockSpec(half(s), lambda t, q_ref: (t, 0, 0)) for s in sums]),
        out_shape=[jax.ShapeDtypeStruct((2,) + half(s)[1:], s.dtype) for s in sums],
        compiler_params=_params(("arbitrary",)),
    )(qxy, *sums, *relay)


def _split_start(copies, srcs, lands, n_sems, after, *, name, carry=None):
    ns, nl = len(srcs), len(lands)
    carried, token = _token(carry)

    def body(*refs):
        n_in = ns + nl + len(after) + len(carried)
        for cp in copies(refs[:ns], refs[ns:ns + nl], refs[n_in], refs[n_in + 1]):
            cp.start()
        refs[-1][...] = refs[n_in - 1][...] if carried else jnp.zeros_like(refs[-1])

    bufs = [_in_hbm(a) for a in list(srcs) + list(lands)]
    vm = pl.BlockSpec(memory_space=pltpu.VMEM)
    out = pl.pallas_call(
        body, name=name,
        out_shape=(pltpu.SemaphoreType.DMA((n_sems,)), pltpu.SemaphoreType.DMA((n_sems,)),
                   *[pltpu.HBM(a.shape, a.dtype) for a in bufs], jax.ShapeDtypeStruct(token.shape, token.dtype)),
        in_specs=[HBM_SPEC] * len(bufs) + [pl.BlockSpec(memory_space=pl.ANY)] * len(after) + [vm] * len(carried),
        out_specs=(SEM_SPEC, SEM_SPEC, *[HBM_SPEC] * len(bufs), vm),
        input_output_aliases={i: 2 + i for i in range(len(bufs))},
        compiler_params=pltpu.CompilerParams(has_side_effects=DATAFLOW),
    )(*bufs, *after, *carried)
    return out[0], out[1], list(out[2:2 + ns]), list(out[2 + ns:2 + ns + nl]), out[-1]


def _split_wait(copies, send_sems, recv_sems, srcs, lands, after, *, name):
    ns, nl = len(srcs), len(lands)

    def body(*refs):
        for cp in copies(refs[:ns], refs[ns:ns + nl], refs[ns + nl], refs[ns + nl + 1]):
            cp.wait_send()
            cp.wait_recv()

    out = pl.pallas_call(
        body, name=name,
        out_shape=[pltpu.HBM(a.shape, a.dtype) for a in list(srcs) + list(lands)],
        in_specs=[HBM_SPEC] * (ns + nl) + [SEM_SPEC, SEM_SPEC] + [pl.BlockSpec(memory_space=pl.ANY)] * len(after),
        out_specs=[HBM_SPEC] * (ns + nl),
        input_output_aliases={i: i for i in range(ns + nl)},
        compiler_params=pltpu.CompilerParams(has_side_effects=DATAFLOW),
    )(*srcs, *lands, send_sems, recv_sems, *after)
    return list(out[:ns]), list(out[ns:])


ADAM_C1 = 1.0 / (1.0 - ADAM_B1 ** ADAM_STEP)
ADAM_C2 = 1.0 / (1.0 - ADAM_B2 ** ADAM_STEP)


def _adam_math(w, g, m, v):
    m2 = ADAM_B1 * m + (1.0 - ADAM_B1) * g
    v2 = ADAM_B2 * v + (1.0 - ADAM_B2) * (g * g)
    return -ADAM_LR * ((m2 * ADAM_C1) / (jnp.sqrt(v2 * ADAM_C2) + ADAM_EPS) + ADAM_WD * w), m2, v2


def _adamw(w, g, m, v, *, name, after=()):
    R, C = w.shape
    tr = R if R <= 512 else 256

    def body(w_ref, g_ref, m_ref, v_ref, *rest):
        d_ref, nm_ref, nv_ref = rest[len(after):]
        d_ref[...], nm_ref[...], nv_ref[...] = _adam_math(w_ref[...], g_ref[...], m_ref[...], v_ref[...])

    blk = pl.BlockSpec((tr, C), lambda i: (i, 0))
    return pl.pallas_call(
        body, name=name, grid=(R // tr,), in_specs=[blk] * 4 + [pl.BlockSpec(memory_space=pl.ANY)] * len(after),
        out_specs=[blk] * 3, out_shape=[jax.ShapeDtypeStruct((R, C), F32)] * 3,
        compiler_params=_params(("parallel",)),
    )(w, g, m, v, *after)


def _adamw_rs2(wmv, cs, direct, second, qidx, *, name):
    n = len(wmv)
    r, cc = wmv[0][0].shape
    h = r // 2

    def body(q_ref, *refs):
        ins, outs = refs[:6 * n], refs[6 * n:]
        for k in range(n):
            w_ref, m_ref, v_ref, c_ref, d1_ref, d2_ref = ins[6 * k:6 * k + 6]
            g_ref, d_ref, nm_ref, nv_ref = outs[4 * k:4 * k + 4]
            g = (c_ref[0].astype(F32) + d1_ref[0].astype(F32)) + d2_ref[0].astype(F32)
            g_ref[...] = g
            d_ref[...], nm_ref[...], nv_ref[...] = _adam_math(w_ref[...], g, m_ref[...], v_ref[...])

    blk = pl.BlockSpec((h, cc), lambda i, q_ref: (i, 0))
    one = [blk, blk, blk, pl.BlockSpec((1, h, cc), lambda i, q_ref: (q_ref[0], i, 0)),
           pl.BlockSpec((1, h, cc), lambda i, q_ref: (i, 0, 0)),
           pl.BlockSpec((1, h, cc), lambda i, q_ref: (1 - i, 0, 0))]
    out = pl.pallas_call(
        body, name=name,
        grid_spec=pltpu.PrefetchScalarGridSpec(num_scalar_prefetch=1, grid=(2,), in_specs=one * n,
                                               out_specs=[blk] * (4 * n)),
        out_shape=[jax.ShapeDtypeStruct((r, cc), F32)] * (4 * n),
        compiler_params=_params(("arbitrary",)),
    )(qidx, *[a for (w, m, v), c, d1, d2 in zip(wmv, cs, direct, second) for a in (w, m, v, c, d1, d2)])
    return [tuple(out[4 * k:4 * k + 4]) for k in range(n)]


def _adamw_rs(wmv, cs, rcv, qidx, *, name):
    n = len(wmv)
    shapes = [w.shape for w, _, _ in wmv]
    n_rcv = rcv[0].shape[0]
    halved = len(set(shapes)) == 1 and shapes[0][0] % 32 == 0 and shapes[0][0] > 128
    tiles = 2 if halved else 1

    def body(q_ref, *refs):
        ins, outs = refs[:5 * n], refs[5 * n:]
        for k in range(n):
            w_ref, m_ref, v_ref, c_ref, r_ref = ins[5 * k:5 * k + 5]
            g_ref, d_ref, nm_ref, nv_ref = outs[4 * k:4 * k + 4]
            g = c_ref[0].astype(F32)
            for j in range(n_rcv):
                g = g + r_ref[j].astype(F32)
            g_ref[...] = g
            d_ref[...], nm_ref[...], nv_ref[...] = _adam_math(w_ref[...], g, m_ref[...], v_ref[...])

    in_specs, out_specs = [], []
    for r, cc in shapes:
        blk = pl.BlockSpec((r // tiles, cc), lambda i, q_ref: (i, 0))
        in_specs += [blk, blk, blk, pl.BlockSpec((1, r // tiles, cc), lambda i, q_ref: (q_ref[0], i, 0)),
                     pl.BlockSpec((n_rcv, r // tiles, cc), lambda i, q_ref: (0, i, 0))]
        out_specs += [blk] * 4
    out = pl.pallas_call(
        body, name=name,
        grid_spec=pltpu.PrefetchScalarGridSpec(num_scalar_prefetch=1, grid=(tiles,), in_specs=in_specs,
                                               out_specs=out_specs),
        out_shape=[jax.ShapeDtypeStruct(s, F32) for s in shapes for _ in range(4)],
        compiler_params=_params(("arbitrary",)),
    )(qidx, *[a for (w, m, v), c, rc in zip(wmv, cs, rcv) for a in (w, m, v, c, rc)])
    return [tuple(out[4 * k:4 * k + 4]) for k in range(n)]


SMALL_PARAMS = ("norm_ffn1", "norm_mix", "norm_ffn2", "norm_final", "q_norm", "kv_norm", "sinks", "rel_bias", "b_mod")


def _adamw_small(gvec, wmv):
    shapes = [wmv[3 * i].shape for i in range(len(SMALL_PARAMS))]

    def body(*refs):
        g_all = refs[0]
        ins = refs[1:1 + 3 * len(SMALL_PARAMS)]
        outs = refs[1 + 3 * len(SMALL_PARAMS):]
        off = N_MODVEC
        for i, name in enumerate(SMALL_PARAMS):
            g_ref, d_ref, nm_ref, nv_ref = outs[4 * i:4 * i + 4]
            w_ref, m_ref, v_ref = ins[3 * i:3 * i + 3]
            start = 0 if name == "b_mod" else off
            rows, width = shapes[i]
            if name == "rel_bias":
                sub = lax.broadcasted_iota(jnp.int32, (rows, rows), 0)
                lane = lax.broadcasted_iota(jnp.int32, (rows, rows), 1)
                col = lax.broadcasted_iota(jnp.int32, (rows, width), 1)
                g = jnp.zeros((rows, width), F32)
                for h in range(width):
                    along = jnp.broadcast_to(g_all[:, start + rows * h:start + rows * (h + 1)], (rows, rows))
                    g = jnp.where(col == h, jnp.sum(jnp.where(sub == lane, along, 0.0), axis=1, keepdims=True), g)
            else:
                g = jnp.concatenate([g_all[:, start + width * r:start + width * (r + 1)] for r in range(rows)], axis=0)
            g_ref[...] = g
            d_ref[...], nm_ref[...], nv_ref[...] = _adam_math(w_ref[...], g, m_ref[...], v_ref[...])
            if name != "b_mod":
                off += dict(SMALL_LAYOUT)[name]

    vm = pl.BlockSpec(memory_space=pltpu.VMEM)
    n_out = 4 * len(SMALL_PARAMS)
    out = pl.pallas_call(
        body, name="adamw_small", in_specs=[vm] * (1 + len(wmv)), out_specs=[vm] * n_out,
        out_shape=[jax.ShapeDtypeStruct(shapes[i // 4], F32) for i in range(n_out)],
        compiler_params=_params(),
    )(gvec, *wmv)
    return {name: out[4 * i:4 * i + 4] for i, name in enumerate(SMALL_PARAMS)}


TRANSPOSED = ("g1T", "u1T", "g3T", "u3T", "w_inT", "w_uqT")


def kernel(x, c, w_mod, b_mod, norm_ffn1, ffn1_gate, ffn1_up, ffn1_down, norm_mix, w_in, q_norm, kv_norm, w_uq, w_ukv, sinks, w_o, norm_ffn2, ffn2_gate, ffn2_up, ffn2_down, rel_bias, norm_final, loss_target, m_w_mod, m_b_mod, m_norm_ffn1, m_ffn1_gate, m_ffn1_up, m_ffn1_down, m_norm_mix, m_w_in, m_q_norm, m_kv_norm, m_w_uq, m_w_ukv, m_sinks, m_w_o, m_norm_ffn2, m_ffn2_gate, m_ffn2_up, m_ffn2_down, m_rel_bias, m_norm_final, v_w_mod, v_b_mod, v_norm_ffn1, v_ffn1_gate, v_ffn1_up, v_ffn1_down, v_norm_mix, v_w_in, v_q_norm, v_kv_norm, v_w_uq, v_w_ukv, v_sinks, v_w_o, v_norm_ffn2, v_ffn2_gate, v_ffn2_up, v_ffn2_down, v_rel_bias, v_norm_final):
    mx, my, mc = _coords()
    cidx = jnp.reshape(mc, (1,)).astype(jnp.int32)
    qidx = jnp.reshape(2 * mx + my, (1,)).astype(jnp.int32)
    WM = w_mod.shape[2]

    shards = {"g1T": ffn1_gate[0].T, "u1T": ffn1_up[0].T, "d1": ffn1_down[0],
              "g3T": ffn2_gate[0].T, "u3T": ffn2_up[0].T, "d3": ffn2_down[0],
              "w_inT": w_in[0].T, "w_uqT": w_uq[0].T, "w_ukv": w_ukv[0], "w_o": w_o[0]}
    travels = lambda k: F32 if k == "w_inT" else BF16
    me = 4 * mx + 2 * my + mc
    groups = {"ffn1": ("g1T", "u1T", "d1"), "mixer": ("w_inT", "w_uqT", "w_ukv", "w_o"), "ffn2": ("g3T", "u3T", "d3")}
    arriving = {}
    later = groups["mixer"] + groups["ffn2"]
    place = {"mixer": 0, "ffn2": len(groups["mixer"])}

    gathered_ffn1, mod3, ca, zones = _wgather([shards[k] for k in groups["ffn1"]],
                                              [(shards[k], travels(k)) for k in later], c, w_mod[0],
                                              b_mod.reshape(N_DEV, 1, WM))
    mod9 = mod3[:, 0, :].reshape(N_MOD, D)

    def as_weights(group, gathered):
        return {k: g if k == "w_ukv" else g.reshape(N_DEV * g.shape[1], g.shape[2])
                for k, g in zip(groups[group], gathered)}

    def start_gather(carry):
        batches = [range(k0, k0 + len(groups[group])) for group, k0 in place.items()]
        send, recv, lands, started = _gather_start(zones, name="gather_start", batches=batches, carry=carry)
        for group, k0 in place.items():
            arriving[group] = (send, recv, lands[k0:k0 + len(groups[group])])
        return started

    def fetch(group, after, vecs):
        if group == "ffn1":
            return as_weights("ffn1", gathered_ffn1), start_gather(vecs)

        def pass_on(group, after, carry=None):
            send, recv, lands = arriving[group]
            lands, token = _gather_pass(send, recv, lands, after, name="gather_landed_" + group, stage="landed",
                                        k0=place[group])
            lands, token = _gather_pass(send, recv, lands, [token], name="gather_onward_" + group, stage="onward",
                                        k0=place[group], carry=carry)
            arriving[group] = (send, recv, lands)
            return token

        if group == "ffn2_on_its_way":
            return None, pass_on("ffn2", after, vecs)
        if group == "mixer":
            after = [pass_on("mixer", after)]
        send, recv, lands = arriving[group]
        return as_weights(group, _gather_end(send, recv, lands, after, name="gather_end_" + group,
                                             k0=place[group])), vecs

    norms ={"ffn1": norm_ffn1, "mix": norm_mix, "ffn2": norm_ffn2, "final": norm_final.reshape(1, D)}
    in_flight = {}

    def on_grads(group, g, after, vecs):
        if group != "ffn1":
            if g is None:
                return vecs
            names = list(g)
            by_dest = [g[k] if k == "w_ukv" else g[k].reshape((N_DEV, g[k].shape[0] // N_DEV) + g[k].shape[1:])
                       for k in names]
            lands = [lax.empty((N_DEV - 1,) + a.shape[1:], a.dtype) for a in by_dest]
            send, recv, by_dest, lands, token = _split_start(_direct_copies, by_dest, lands, 7 * len(names), after,
                                                             name="rs_start_" + group, carry=vecs)
            in_flight[group] = (names, send, recv, by_dest, lands, token)
            return token
        names = list(g)
        by_dest = [g[k].reshape((N_DEV, g[k].shape[0] // N_DEV) + g[k].shape[1:]) for k in names]
        lands = [lax.empty((4,) + a.shape[1:], a.dtype) for a in by_dest]
        send, recv, by_dest, lands, token = _split_start(_d2d_copies, by_dest, lands, 4 * len(names), after,
                                                         name="rs_d2d_start_" + group)
        finish("mixer", [token])
        by_dest, from_sib = _split_wait(_d2d_copies, send, recv, by_dest, lands, [done[-1]], name="rs_d2d_wait_" + group)
        sums = _chipsum(by_dest, from_sib, cidx, name="chipsum_" + group)
        halves = lambda: [lax.empty((2, s.shape[1] // 2) + s.shape[2:], s.dtype) for s in sums]
        send, recv, sums, lands, token = _split_start(_rs_step1_copies, sums, halves() + halves(), 4 * len(names), [],
                                                      name="rs_ici_start_" + group, carry=vecs)
        in_flight[group] = (names, send, recv, sums, lands, token)
        return token

    owners = {"g1T": ("ffn1_gate", ffn1_gate, m_ffn1_gate, v_ffn1_gate), "u1T": ("ffn1_up", ffn1_up, m_ffn1_up, v_ffn1_up),
              "d1": ("ffn1_down", ffn1_down, m_ffn1_down, v_ffn1_down),
              "g3T": ("ffn2_gate", ffn2_gate, m_ffn2_gate, v_ffn2_gate), "u3T": ("ffn2_up", ffn2_up, m_ffn2_up, v_ffn2_up),
              "d3": ("ffn2_down", ffn2_down, m_ffn2_down, v_ffn2_down),
              "w_inT": ("w_in", w_in, m_w_in, v_w_in), "w_uqT": ("w_uq", w_uq, m_w_uq, v_w_uq),
              "w_ukv": ("w_ukv", w_ukv, m_w_ukv, v_w_ukv), "w_o": ("w_o", w_o, m_w_o, v_w_o)}
    res, done = {}, []

    there = lambda k, a: a[0].T if k in TRANSPOSED else a[0]
    back = lambda k, a: a.T[None] if k in TRANSPOSED else a[None]

    def record(names, outs):
        for k, out in zip(names, outs):
            done.append(out[3])
            res[owners[k][0]] = tuple(back(k, a) for a in out)

    def finish(group, after):
        names, send, recv, sums, lands, _ = in_flight[group]
        wmv = [tuple(there(k, a) for a in owners[k][1:]) for k in names]
        own = jnp.reshape(me, (1,)).astype(jnp.int32)
        sums, lands = _split_wait(_direct_copies, send, recv, sums, lands, after, name="rs_wait_" + group)
        record(names, _adamw_rs(wmv, sums, lands, own, name="adamw_" + group))

    _, grad_x, _, vec = _local_step(
        x[0], loss_target[0], mod9, norms, sinks, rel_bias, q_norm, kv_norm, fetch, on_grads=on_grads)

    names, send, recv, sums, lands, step1_started = in_flight["ffn1"]
    vec = vec.reshape(1, 1, N_VEC)
    allvec = lax.dynamic_update_slice(lax.empty((N_DEV, 1, N_VEC), F32), vec, (me, 0, 0))
    vsend, vrecv, _, (allvec,), vec_started = _split_start(_vec_copies, [], [allvec], N_DEV - 1, [step1_started],
                                                           name="vec_start")
    finish("ffn2", [vec_started])
    n = len(names)
    sums, lands = _split_wait(_rs_step1_copies, send, recv, sums, lands, [done[-1]], name="rs_ici_wait_ffn1")
    direct, relay = lands[:n], lands[n:]
    qxy = jnp.stack([2 * (1 - mx) + my, 2 * mx + (1 - my)]).astype(jnp.int32)
    relayed = _relay_sum(sums, relay, qxy, name="relay_sum_ffn1")
    second = [lax.empty(a.shape, a.dtype) for a in relayed]
    send, recv, relayed, second, step2_started = _split_start(_rs_step2_copies, relayed, second, 2 * n, [],
                                                              name="rs_ici_start2_ffn1")

    _, (allvec,) = _split_wait(_vec_copies, vsend, vrecv, [], [allvec], [step2_started], name="vec_wait")
    g_wmod, gvec = _mod_bwd(allvec, ca, jnp.reshape(me, (1,)).astype(jnp.int32))
    loss = gvec[0, N_MODVEC + LOSS_SLOT]
    small_in = {"norm_ffn1": (norm_ffn1, m_norm_ffn1, v_norm_ffn1), "norm_mix": (norm_mix, m_norm_mix, v_norm_mix),
                "norm_ffn2": (norm_ffn2, m_norm_ffn2, v_norm_ffn2), "norm_final": (norm_final, m_norm_final, v_norm_final),
                "q_norm": (q_norm, m_q_norm, v_q_norm), "kv_norm": (kv_norm, m_kv_norm, v_kv_norm),
                "sinks": (sinks, m_sinks, v_sinks), "rel_bias": (rel_bias, m_rel_bias, v_rel_bias),
                "b_mod": (b_mod, m_b_mod, v_b_mod)}
    as_row = lambda k, a: a if k == "rel_bias" else a.reshape(1, -1)
    from_row = lambda k, a: a if k == "rel_bias" else a.reshape(small_in[k][0].shape)
    small_out = _adamw_small(gvec, [as_row(k, a) for k in SMALL_PARAMS for a in small_in[k]])
    for k in SMALL_PARAMS:
        res[k] = tuple(from_row(k, a) for a in small_out[k])

    out = _adamw(w_mod[0], g_wmod, m_w_mod[0], v_w_mod[0], name="adamw_w_mod")
    res["w_mod"] = tuple(a[None] for a in (g_wmod,) + tuple(out))

    wmv = [tuple(there(k, a) for a in owners[k][1:]) for k in names]
    after = done + [out[2]] + [a for k in SMALL_PARAMS for a in res[k]]
    outs = []
    for i, k in enumerate(names):
        _, (sec,) = _split_wait(functools.partial(_rs_step2_copies, k0=i), send, recv, [relayed[i]], [second[i]],
                                after, name="rs_ici_wait2_" + k)
        outs.append(_adamw_rs2([wmv[i]], [sums[i]], [direct[i]], [sec], qidx, name="adamw_" + owners[k][0])[0])
        after = [outs[-1][3]]
    record(names, outs)

    order = ("w_mod", "b_mod", "norm_ffn1", "ffn1_gate", "ffn1_up", "ffn1_down", "norm_mix", "w_in", "q_norm",
             "kv_norm", "w_uq", "w_ukv", "sinks", "w_o", "norm_ffn2", "ffn2_gate", "ffn2_up", "ffn2_down",
             "rel_bias", "norm_final")
    return (loss, grad_x[None]) + tuple(res[nm][kind] for kind in range(4) for nm in order)
```
